```python
import jax, jax.numpy as jnp
from jax import lax
import numpy as np

D_MODEL = 2048
BATCH = 8
SEQ = 8192
DEPTH = 2

RET_HEADS = 4
RET_HEAD_DIM = D_MODEL // 2 // RET_HEADS
RET_WIDTH = RET_HEADS * RET_HEAD_DIM
CONV_WIDTH = D_MODEL - RET_WIDTH
CONV_K = 3
RET_CHUNK = 128
EVEN_IN = 4 * RET_WIDTH + 4 * CONV_WIDTH
ATTN_HEAD_DIM = 64
ATTN_Q_HEADS = D_MODEL // ATTN_HEAD_DIM
ATTN_KV_HEADS = ATTN_Q_HEADS // 8
ATTN_WIDTH = ATTN_Q_HEADS * ATTN_HEAD_DIM
KV_WIDTH = ATTN_KV_HEADS * ATTN_HEAD_DIM
ODD_IN = 2 * ATTN_WIDTH + 2 * KV_WIDTH
WINDOW = 128
BLOCK = 128
ROPE_THETA = 10000.0
EPS = 1e-6
N_EVEN = (DEPTH + 1) // 2
N_ODD = DEPTH // 2

kernel_name = "hybrid_retention_shortconv_swa_sinks"


def rms_norm(x, w):
    xf = x.astype(jnp.float32)
    y = xf * lax.rsqrt(jnp.mean(xf * xf, axis=-1, keepdims=True) + EPS)
    return (y * w.astype(jnp.float32)).astype(x.dtype)


def rms_norm_nogain(x):
    xf = x.astype(jnp.float32)
    return (xf * lax.rsqrt(jnp.mean(xf * xf, axis=-1, keepdims=True) + EPS)).astype(x.dtype)


def rope(x, pos):
    d = x.shape[-1]
    inv = 1.0 / (ROPE_THETA ** (jnp.arange(0, d, 2, dtype=jnp.float32) / d))
    ang = pos.astype(jnp.float32)[:, None] * inv[None, :]
    cos = jnp.cos(ang)[None, :, None, :]
    sin = jnp.sin(ang)[None, :, None, :]
    xf = x.astype(jnp.float32)
    x1, x2 = xf[..., : d // 2], xf[..., d // 2:]
    return jnp.concatenate([x1 * cos - x2 * sin, x2 * cos + x1 * sin], axis=-1).astype(x.dtype)


def retention_chunkwise(q, k, v):
    b, s, h, d = q.shape
    c = RET_CHUNK
    n = s // c
    dt = q.dtype
    log_g = jnp.log(1.0 - 2.0 ** (-5.0 - jnp.arange(h, dtype=jnp.float32)))
    idx = jnp.arange(c, dtype=jnp.float32)
    diff = idx[:, None] - idx[None, :]
    intra = jnp.where(diff >= 0, jnp.exp(log_g[:, None, None] * jnp.maximum(diff, 0.0)), 0.0).astype(dt)
    q_dec = jnp.exp(log_g[:, None] * (idx[None, :] + 1.0)).astype(dt)
    k_dec = jnp.exp(log_g[:, None] * (c - 1.0 - idx[None, :])).astype(dt)
    chunk_dec = jnp.exp(log_g * c).astype(dt)
    qc = q.reshape(b, n, c, h, d)
    kc = (k * (d ** -0.5)).reshape(b, n, c, h, d)
    vc = v.reshape(b, n, c, h, d)
    scores = jnp.einsum('bnihd,bnjhd->bnhij', qc, kc) * intra[None, None]
    inner = jnp.einsum('bnhij,bnjhe->bnihe', scores, vc)
    kv = jnp.einsum('bnjhd,bnjhe,hj->bnhde', kc, vc, k_dec)

    def step(state, kv_n):
        return chunk_dec[None, :, None, None] * state + kv_n, state

    _, prev = lax.scan(step, jnp.zeros((b, h, d, d), dt), jnp.moveaxis(kv, 1, 0))
    prev = jnp.moveaxis(prev, 0, 1)
    cross = jnp.einsum('bnihd,bnhde,hi->bnihe', qc, prev, q_dec)
    return (inner + cross).reshape(b, s, h, d)


def even_mixer(h, w_in, conv_w, w_out, pos):
    b, s, _ = h.shape
    proj = h @ w_in
    q, k, v, g_ret, gate_b, gate_c, u, g_conv = jnp.split(proj, 8, axis=-1)
    shp = (b, s, RET_HEADS, RET_HEAD_DIM)
    q = rope(q.reshape(shp), pos)
    k = rope(k.reshape(shp), pos)
    o = retention_chunkwise(q, k, v.reshape(shp))
    ret_out = rms_norm_nogain(o).reshape(b, s, RET_WIDTH) * jax.nn.silu(g_ret)
    conv = lax.conv_general_dilated(
        gate_c * u, conv_w[:, None, :].astype(u.dtype), window_strides=(1,),
        padding=[(CONV_K - 1, 0)], dimension_numbers=('NWC', 'WIO', 'NWC'),
        feature_group_count=CONV_WIDTH)
    conv_out = gate_b * conv * jax.nn.silu(g_conv)
    return jnp.concatenate([ret_out, conv_out], axis=-1) @ w_out


def swa_sinks(q, k, v, sinks):
    b, s, hq, d = q.shape
    hk = k.shape[2]
    g = hq // hk
    n = s // BLOCK
    qb = q.reshape(b, n, BLOCK, hk, g, d)
    kb = k.reshape(b, n, BLOCK, hk, d)
    vb = v.reshape(b, n, BLOCK, hk, d)
    pad = ((0, 0), (1, 0), (0, 0), (0, 0), (0, 0))
    kk = jnp.concatenate([jnp.pad(kb, pad)[:, :-1], kb], axis=2)
    vv = jnp.concatenate([jnp.pad(vb, pad)[:, :-1], vb], axis=2)
    scores = jnp.einsum('bnihgd,bnjhd->bnhgij', qb, kk).astype(jnp.float32) * (d ** -0.5)
    qi = jnp.arange(BLOCK)[:, None] + BLOCK
    kj = jnp.arange(2 * BLOCK)[None, :]
    band = (kj <= qi) & (qi - kj < WINDOW)
    valid = band[None] & ((jnp.arange(n)[:, None, None] > 0) | (kj >= BLOCK)[None])
    scores = jnp.where(valid[None, :, None, None], scores, -1e30)
    sink = sinks.astype(jnp.float32).reshape(hk, g)[None, None, :, :, None, None]
    m = jnp.maximum(jnp.max(scores, axis=-1, keepdims=True), sink)
    p = jnp.exp(scores - m)
    p = p / (jnp.sum(p, axis=-1, keepdims=True) + jnp.exp(sink - m))
    o = jnp.einsum('bnhgij,bnjhd->bnihgd', p.astype(v.dtype), vv)
    return o.reshape(b, s, hq, d)


def odd_mixer(h, w_in, q_norm_w, k_norm_w, sinks, w_out, pos):
    b, s, _ = h.shape
    proj = h @ w_in
    q, k, v, gate = jnp.split(proj, [ATTN_WIDTH, ATTN_WIDTH + KV_WIDTH, ATTN_WIDTH + 2 * KV_WIDTH], axis=-1)
    q = rope(rms_norm(q.reshape(b, s, ATTN_Q_HEADS, ATTN_HEAD_DIM), q_norm_w), pos)
    k = rope(rms_norm(k.reshape(b, s, ATTN_KV_HEADS, ATTN_HEAD_DIM), k_norm_w), pos)
    v = v.reshape(b, s, ATTN_KV_HEADS, ATTN_HEAD_DIM)
    o = swa_sinks(q, k, v, sinks).reshape(b, s, ATTN_WIDTH)
    return (o * jax.nn.silu(gate)) @ w_out


def _fwd_setup_inputs(seed: int = 0) -> dict:
    key = jax.random.key(seed)
    ks = jax.random.split(key, 11)
    f32 = jnp.float32
    x = jax.random.normal(ks[0], (BATCH, SEQ, D_MODEL), f32)
    ev_norm_w = 1.0 + 0.02 * jax.random.normal(ks[1], (N_EVEN, D_MODEL), f32)
    ev_w_in = jax.random.normal(ks[2], (N_EVEN, D_MODEL, EVEN_IN), f32) * D_MODEL ** -0.5
    ev_conv_w = jax.random.normal(ks[3], (N_EVEN, CONV_K, CONV_WIDTH), f32) * CONV_K ** -0.5
    ev_w_out = jax.random.normal(ks[4], (N_EVEN, D_MODEL, D_MODEL), f32) * D_MODEL ** -0.5
    od_norm_w = 1.0 + 0.02 * jax.random.normal(ks[5], (N_ODD, D_MODEL), f32)
    od_w_in = jax.random.normal(ks[6], (N_ODD, D_MODEL, ODD_IN), f32) * D_MODEL ** -0.5
    od_q_norm_w = 1.0 + 0.02 * jax.random.normal(ks[7], (N_ODD, ATTN_HEAD_DIM), f32)
    od_k_norm_w = 1.0 + 0.02 * jax.random.normal(ks[8], (N_ODD, ATTN_HEAD_DIM), f32)
    od_sinks = 0.5 * jax.random.normal(ks[9], (N_ODD, ATTN_Q_HEADS), f32)
    od_w_out = jax.random.normal(ks[10], (N_ODD, ATTN_WIDTH, D_MODEL), f32) * ATTN_WIDTH ** -0.5
    return {"x": x, "ev_norm_w": ev_norm_w, "ev_w_in": ev_w_in, "ev_conv_w": ev_conv_w,
            "ev_w_out": ev_w_out, "od_norm_w": od_norm_w, "od_w_in": od_w_in,
            "od_q_norm_w": od_q_norm_w, "od_k_norm_w": od_k_norm_w, "od_sinks": od_sinks,
            "od_w_out": od_w_out}


def _fwd_reference(x, ev_norm_w, ev_w_in, ev_conv_w, ev_w_out, od_norm_w, od_w_in,
              od_q_norm_w, od_k_norm_w, od_sinks, od_w_out):
    pos = jnp.arange(x.shape[1])
    for layer in range(DEPTH):
        i = layer // 2
        if layer % 2 == 0:
            h = rms_norm(x, ev_norm_w[i])
            x = x + even_mixer(h, ev_w_in[i], ev_conv_w[i], ev_w_out[i], pos)
        else:
            h = rms_norm(x, od_norm_w[i])
            x = x + odd_mixer(h, od_w_in[i], od_q_norm_w[i], od_k_norm_w[i], od_sinks[i], od_w_out[i], pos)
    return x


import jax as _jax
import jax.numpy as _jnp

TWIN_FORMAT = 'train_step'
FWD_PARAMS = ['x', 'ev_norm_w', 'ev_w_in', 'ev_conv_w', 'ev_w_out', 'od_norm_w', 'od_w_in', 'od_q_norm_w', 'od_k_norm_w', 'od_sinks', 'od_w_out']
TWIN_WEIGHTS = ['ev_norm_w', 'ev_w_in', 'ev_conv_w', 'ev_w_out', 'od_norm_w', 'od_w_in', 'od_q_norm_w', 'od_k_norm_w', 'od_sinks', 'od_w_out']
TWIN_DIFF_INPUT = 'x'
TWIN_INPUTS = ['x', 'ev_norm_w', 'ev_w_in', 'ev_conv_w', 'ev_w_out', 'od_norm_w', 'od_w_in', 'od_q_norm_w', 'od_k_norm_w', 'od_sinks', 'od_w_out', 'loss_target', 'm_ev_norm_w', 'm_ev_w_in', 'm_ev_conv_w', 'm_ev_w_out', 'm_od_norm_w', 'm_od_w_in', 'm_od_q_norm_w', 'm_od_k_norm_w', 'm_od_sinks', 'm_od_w_out', 'v_ev_norm_w', 'v_ev_w_in', 'v_ev_conv_w', 'v_ev_w_out', 'v_od_norm_w', 'v_od_w_in', 'v_od_q_norm_w', 'v_od_k_norm_w', 'v_od_sinks', 'v_od_w_out']
TWIN_OUTPUTS = ['loss', 'grad_x', 'grad_ev_norm_w', 'grad_ev_w_in', 'grad_ev_conv_w', 'grad_ev_w_out', 'grad_od_norm_w', 'grad_od_w_in', 'grad_od_q_norm_w', 'grad_od_k_norm_w', 'grad_od_sinks', 'grad_od_w_out', 'delta_ev_norm_w', 'delta_ev_w_in', 'delta_ev_conv_w', 'delta_ev_w_out', 'delta_od_norm_w', 'delta_od_w_in', 'delta_od_q_norm_w', 'delta_od_k_norm_w', 'delta_od_sinks', 'delta_od_w_out', 'new_m_ev_norm_w', 'new_m_ev_w_in', 'new_m_ev_conv_w', 'new_m_ev_w_out', 'new_m_od_norm_w', 'new_m_od_w_in', 'new_m_od_q_norm_w', 'new_m_od_k_norm_w', 'new_m_od_sinks', 'new_m_od_w_out', 'new_v_ev_norm_w', 'new_v_ev_w_in', 'new_v_ev_conv_w', 'new_v_ev_w_out', 'new_v_od_norm_w', 'new_v_od_w_in', 'new_v_od_q_norm_w', 'new_v_od_k_norm_w', 'new_v_od_sinks', 'new_v_od_w_out']
TWIN_LEAF_KINDS = {'loss': 'loss', 'grad_x': 'grad_x', 'grad_ev_norm_w': 'grad_w', 'grad_ev_w_in': 'grad_w', 'grad_ev_conv_w': 'grad_w', 'grad_ev_w_out': 'grad_w', 'grad_od_norm_w': 'grad_w', 'grad_od_w_in': 'grad_w', 'grad_od_q_norm_w': 'grad_w', 'grad_od_k_norm_w': 'grad_w', 'grad_od_sinks': 'grad_w', 'grad_od_w_out': 'grad_w', 'delta_ev_norm_w': 'delta_w', 'delta_ev_w_in': 'delta_w', 'delta_ev_conv_w': 'delta_w', 'delta_ev_w_out': 'delta_w', 'delta_od_norm_w': 'delta_w', 'delta_od_w_in': 'delta_w', 'delta_od_q_norm_w': 'delta_w', 'delta_od_k_norm_w': 'delta_w', 'delta_od_sinks': 'delta_w', 'delta_od_w_out': 'delta_w', 'new_m_ev_norm_w': 'new_m', 'new_m_ev_w_in': 'new_m', 'new_m_ev_conv_w': 'new_m', 'new_m_ev_w_out': 'new_m', 'new_m_od_norm_w': 'new_m', 'new_m_od_w_in': 'new_m', 'new_m_od_q_norm_w': 'new_m', 'new_m_od_k_norm_w': 'new_m', 'new_m_od_sinks': 'new_m', 'new_m_od_w_out': 'new_m', 'new_v_ev_norm_w': 'new_v', 'new_v_ev_w_in': 'new_v', 'new_v_ev_conv_w': 'new_v', 'new_v_ev_w_out': 'new_v', 'new_v_od_norm_w': 'new_v', 'new_v_od_w_in': 'new_v', 'new_v_od_q_norm_w': 'new_v', 'new_v_od_k_norm_w': 'new_v', 'new_v_od_sinks': 'new_v', 'new_v_od_w_out': 'new_v'}


def _forward(args):
    return _fwd_reference(*[args[k] for k in FWD_PARAMS])


def _output_shape():
    def fwd():
        inp = _fwd_setup_inputs(0)
        return _fwd_reference(*[inp[k] for k in FWD_PARAMS])
    out = _jax.eval_shape(fwd)
    return out.shape, out.dtype

N_MICROBATCH = 1
ADAM_LR = 0.001
ADAM_B1 = 0.9
ADAM_B2 = 0.999
ADAM_EPS = 1e-08
ADAM_WD = 0.01
ADAM_STEP = 10
PER_EXAMPLE_BATCH_AXIS = {'x': 0, 'loss_target': 0}
SHARED_INPUTS = []
_WEIGHT_DTYPES = {'ev_norm_w': _jnp.float32, 'ev_w_in': _jnp.float32, 'ev_conv_w': _jnp.float32, 'ev_w_out': _jnp.float32, 'od_norm_w': _jnp.float32, 'od_w_in': _jnp.float32, 'od_q_norm_w': _jnp.float32, 'od_k_norm_w': _jnp.float32, 'od_sinks': _jnp.float32, 'od_w_out': _jnp.float32}
MOMENT_SCALE = {'ev_norm_w': 2.951959e+01, 'ev_w_in': 3.674478e-01, 'ev_conv_w': 6.447927e+00, 'ev_w_out': 2.059988e-01, 'od_norm_w': 2.881453e-01, 'od_w_in': 3.482126e-02, 'od_q_norm_w': 4.016787e+00, 'od_k_norm_w': 4.011159e+00, 'od_sinks': 2.660085e-01, 'od_w_out': 2.334745e-02}


def _to_microbatches(a, axis):
    t = _jnp.moveaxis(a, axis, 0)
    t = t.reshape((N_MICROBATCH, t.shape[0] // N_MICROBATCH) + t.shape[1:])
    return _jnp.moveaxis(t, 1, axis + 1)


def setup_inputs(seed: int = 0) -> dict:
    inp = _fwd_setup_inputs(seed)
    key = _jax.random.fold_in(_jax.random.key(seed), 7919)
    shape, _ = _output_shape()
    out = dict(inp)
    out["loss_target"] = _jax.random.normal(_jax.random.fold_in(key, 0), shape, _jnp.float32)
    for i, name in enumerate(TWIN_WEIGHTS):
        w = inp[name].astype(_jnp.float32)
        if MOMENT_SCALE is None:
            s = _jnp.sqrt(_jnp.mean(_jnp.square(w)) + 1e-30)
        else:
            s = MOMENT_SCALE[name]
        km, kv = _jax.random.split(_jax.random.fold_in(key, i + 1))
        out[name] = w
        out["m_" + name] = s * _jax.random.normal(km, w.shape, _jnp.float32)
        out["v_" + name] = (s * s) * _jax.random.uniform(kv, w.shape, _jnp.float32, 0.5, 1.5)
    if N_MICROBATCH > 1:
        for name, axis in PER_EXAMPLE_BATCH_AXIS.items():
            out[name] = _to_microbatches(out[name], axis)
    return {'x': out['x'], 'ev_norm_w': out['ev_norm_w'], 'ev_w_in': out['ev_w_in'], 'ev_conv_w': out['ev_conv_w'], 'ev_w_out': out['ev_w_out'], 'od_norm_w': out['od_norm_w'], 'od_w_in': out['od_w_in'], 'od_q_norm_w': out['od_q_norm_w'], 'od_k_norm_w': out['od_k_norm_w'], 'od_sinks': out['od_sinks'], 'od_w_out': out['od_w_out'], 'loss_target': out['loss_target'], 'm_ev_norm_w': out['m_ev_norm_w'], 'm_ev_w_in': out['m_ev_w_in'], 'm_ev_conv_w': out['m_ev_conv_w'], 'm_ev_w_out': out['m_ev_w_out'], 'm_od_norm_w': out['m_od_norm_w'], 'm_od_w_in': out['m_od_w_in'], 'm_od_q_norm_w': out['m_od_q_norm_w'], 'm_od_k_norm_w': out['m_od_k_norm_w'], 'm_od_sinks': out['m_od_sinks'], 'm_od_w_out': out['m_od_w_out'], 'v_ev_norm_w': out['v_ev_norm_w'], 'v_ev_w_in': out['v_ev_w_in'], 'v_ev_conv_w': out['v_ev_conv_w'], 'v_ev_w_out': out['v_ev_w_out'], 'v_od_norm_w': out['v_od_norm_w'], 'v_od_w_in': out['v_od_w_in'], 'v_od_q_norm_w': out['v_od_q_norm_w'], 'v_od_k_norm_w': out['v_od_k_norm_w'], 'v_od_sinks': out['v_od_sinks'], 'v_od_w_out': out['v_od_w_out']}


def _loss(weights, diff, rest, loss_target):
    with _jax.named_scope("forward"):
        args = {**rest, TWIN_DIFF_INPUT: diff, **{k: w.astype(_WEIGHT_DTYPES[k]) for k, w in weights.items()}}
        y = _forward(args)
    with _jax.named_scope("loss_head"):
        err = _jnp.square(y.astype(_jnp.float32) - loss_target)
        return 0.5 * _jnp.sum(_jnp.mean(err, axis=-1)) if err.ndim else 0.5 * err


def _adamw(w, g, m, v):
    m = ADAM_B1 * m + (1.0 - ADAM_B1) * g
    v = ADAM_B2 * v + (1.0 - ADAM_B2) * _jnp.square(g)
    m_hat = m / (1.0 - ADAM_B1 ** ADAM_STEP)
    v_hat = v / (1.0 - ADAM_B2 ** ADAM_STEP)
    delta = -ADAM_LR * (m_hat / (_jnp.sqrt(v_hat) + ADAM_EPS) + ADAM_WD * w)
    return delta, m, v


def reference(x, ev_norm_w, ev_w_in, ev_conv_w, ev_w_out, od_norm_w, od_w_in, od_q_norm_w, od_k_norm_w, od_sinks, od_w_out, loss_target, m_ev_norm_w, m_ev_w_in, m_ev_conv_w, m_ev_w_out, m_od_norm_w, m_od_w_in, m_od_q_norm_w, m_od_k_norm_w, m_od_sinks, m_od_w_out, v_ev_norm_w, v_ev_w_in, v_ev_conv_w, v_ev_w_out, v_od_norm_w, v_od_w_in, v_od_q_norm_w, v_od_k_norm_w, v_od_sinks, v_od_w_out):
    given = dict(x=x, ev_norm_w=ev_norm_w, ev_w_in=ev_w_in, ev_conv_w=ev_conv_w, ev_w_out=ev_w_out, od_norm_w=od_norm_w, od_w_in=od_w_in, od_q_norm_w=od_q_norm_w, od_k_norm_w=od_k_norm_w, od_sinks=od_sinks, od_w_out=od_w_out, loss_target=loss_target, m_ev_norm_w=m_ev_norm_w, m_ev_w_in=m_ev_w_in, m_ev_conv_w=m_ev_conv_w, m_ev_w_out=m_ev_w_out, m_od_norm_w=m_od_norm_w, m_od_w_in=m_od_w_in, m_od_q_norm_w=m_od_q_norm_w, m_od_k_norm_w=m_od_k_norm_w, m_od_sinks=m_od_sinks, m_od_w_out=m_od_w_out, v_ev_norm_w=v_ev_norm_w, v_ev_w_in=v_ev_w_in, v_ev_conv_w=v_ev_conv_w, v_ev_w_out=v_ev_w_out, v_od_norm_w=v_od_norm_w, v_od_w_in=v_od_w_in, v_od_q_norm_w=v_od_q_norm_w, v_od_k_norm_w=v_od_k_norm_w, v_od_sinks=v_od_sinks, v_od_w_out=v_od_w_out)
    weights = {n: given[n] for n in TWIN_WEIGHTS}
    shared = {n: given[n] for n in SHARED_INPUTS}
    per_example = {n: given[n] for n in ['x']}
    grad_fn = _jax.value_and_grad(_loss, argnums=(0, 1))

    def one_microbatch(ex, loss_target):
        ex = dict(ex)
        diff = ex.pop(TWIN_DIFF_INPUT)
        return grad_fn(weights, diff, {**shared, **ex}, loss_target)

    if N_MICROBATCH == 1:
        loss, (grad_w, grad_x) = one_microbatch(per_example, given["loss_target"])
    else:
        def body(carry, xs):
            loss_sum, grad_sum = carry
            l_k, (gw_k, gx_k) = one_microbatch(xs[0], xs[1])
            with _jax.named_scope("update"):
                return (loss_sum + l_k, _jax.tree.map(_jnp.add, grad_sum, gw_k)), gx_k

        init = (_jnp.zeros((), _jnp.float32), _jax.tree.map(_jnp.zeros_like, weights))
        (loss, grad_w), grad_x = _jax.lax.scan(body, init, (per_example, given["loss_target"]))
    with _jax.named_scope("update"):
        delta_w, new_m, new_v = {}, {}, {}
        for n in TWIN_WEIGHTS:
            delta_w[n], new_m[n], new_v[n] = _adamw(weights[n], grad_w[n], given["m_" + n], given["v_" + n])
    return (loss, grad_x, *[grad_w[n] for n in TWIN_WEIGHTS], *[delta_w[n] for n in TWIN_WEIGHTS],
            *[new_m[n] for n in TWIN_WEIGHTS], *[new_v[n] for n in TWIN_WEIGHTS])
```

```python
import functools
import math

import jax
import jax.numpy as jnp
from jax import lax
from jax.experimental import pallas as pl
from jax.experimental.pallas import tpu as pltpu

F32 = jnp.float32
BF16 = jnp.bfloat16

D_MODEL = 2048
RET_HEADS = 4
RET_DIM = 256
RET_WIDTH = 1024
CONV_WIDTH = 1024
EVEN_IN = 8192
Q_HEADS = 32
HEAD_DIM = 64
KV_HEADS = 4
KV_WIDTH = 256
ATTN_WIDTH = 2048
ODD_IN = 4608
BLK = 128
ROPE_THETA = 10000.0
EPS = 1e-6
ADAM_LR = 0.001
ADAM_B1 = 0.9
ADAM_B2 = 0.999
ADAM_EPS = 1e-08
ADAM_WD = 0.01
ADAM_STEP = 10
N_CHIPS = 4
N_DEV = 8
VMEM_LIMIT_BYTES = 48 * 1024 * 1024
MESH = pl.DeviceIdType.MESH
ANY = pl.BlockSpec(memory_space=pl.ANY)


def _params(*sem):
    return pltpu.CompilerParams(dimension_semantics=sem, vmem_limit_bytes=VMEM_LIMIT_BYTES)


def _dot(a, b):
    return jnp.dot(a, b, preferred_element_type=F32)


def _dot_nt(a, b):
    return lax.dot_general(a, b, (((1,), (1,)), ((), ())), preferred_element_type=F32)


def _dot_tn(a, b):
    return lax.dot_general(a, b, (((0,), (0,)), ((), ())), preferred_element_type=F32)


def _sigmoid(x):
    return 1.0 / (1.0 + jnp.exp(-x))


def _mm(a, b, *, mode, tm, tn, tk, out_dtype, name, add=None):
    if mode == "nn":
        (m, k), n = a.shape, b.shape[1]
    elif mode == "nt":
        (m, k), n = a.shape, b.shape[0]
    else:
        (k, m), n = a.shape, b.shape[1]
    tm, tn, tk = min(tm, m), min(tn, n), min(tk, k)
    assert m % tm == 0 and n % tn == 0 and k % tk == 0, (name, m, n, k)
    nk = k // tk
    dot = {"nn": _dot, "nt": _dot_nt, "tn": _dot_tn}[mode]
    a_spec = (pl.BlockSpec((tk, tm), lambda i, j, kk: (kk, i)) if mode == "tn"
              else pl.BlockSpec((tm, tk), lambda i, j, kk: (i, kk)))
    b_spec = (pl.BlockSpec((tn, tk), lambda i, j, kk: (j, kk)) if mode == "nt"
              else pl.BlockSpec((tk, tn), lambda i, j, kk: (kk, j)))
    o_spec = pl.BlockSpec((tm, tn), lambda i, j, kk: (i, j))
    has_add = add is not None

    def body(*refs):
        a_ref, b_ref = refs[0], refs[1]
        add_ref = refs[2] if has_add else None
        o_ref = refs[3] if has_add else refs[2]
        acc_ref = refs[-1]
        p = dot(a_ref[...], b_ref[...])

        def finish(total):
            if has_add:
                total = total + add_ref[...].astype(F32)
            o_ref[...] = total.astype(out_dtype)

        if nk == 1:
            finish(p)
        else:
            kk = pl.program_id(2)

            @pl.when(kk == 0)
            def _():
                acc_ref[...] = p

            @pl.when(jnp.logical_and(kk > 0, kk < nk - 1))
            def _():
                acc_ref[...] += p

            @pl.when(kk == nk - 1)
            def _():
                finish(acc_ref[...] + p)

    in_specs = [a_spec, b_spec] + ([o_spec] if has_add else [])
    args = (a, b) + ((add,) if has_add else ())
    return pl.pallas_call(
        body, name=name, grid=(m // tm, n // tn, nk), in_specs=in_specs, out_specs=o_spec,
        out_shape=jax.ShapeDtypeStruct((m, n), out_dtype),
        scratch_shapes=[pltpu.VMEM((tm, tn) if nk > 1 else (8, 128), F32)],
        compiler_params=_params("parallel", "parallel", "arbitrary"),
    )(*args)


def _cast_bf16(w, name):
    r, c = w.shape
    tr = min(r, 512)

    def body(w_ref, o_ref):
        o_ref[...] = w_ref[...].astype(BF16)

    spec = pl.BlockSpec((tr, c), lambda i: (i, 0))
    return pl.pallas_call(body, name=name, grid=(r // tr,), in_specs=[spec], out_specs=spec,
                          out_shape=jax.ShapeDtypeStruct((r, c), BF16), compiler_params=_params("parallel"))(w)


def _rmsnorm(x, w, name):
    s, d = x.shape
    tr = 256

    def body(x_ref, w_ref, h_ref):
        xv = x_ref[...]
        rstd = lax.rsqrt(jnp.mean(xv * xv, axis=-1, keepdims=True) + EPS)
        h_ref[...] = (xv * rstd * w_ref[...]).astype(BF16)

    return pl.pallas_call(
        body, name=name, grid=(s // tr,),
        in_specs=[pl.BlockSpec((tr, d), lambda i: (i, 0)), pl.BlockSpec((1, d), lambda i: (0, 0))],
        out_specs=pl.BlockSpec((tr, d), lambda i: (i, 0)),
        out_shape=jax.ShapeDtypeStruct((s, d), BF16), compiler_params=_params("parallel"))(x, w)


def _rmsnorm_bwd(x, w, dh, dres, name, with_bf16):
    s, d = x.shape
    tr = 256

    def body(x_ref, w_ref, dh_ref, dres_ref, *outs):
        dx_ref, dw_ref = outs[0], outs[-1]
        xv = x_ref[...]
        rstd = lax.rsqrt(jnp.mean(xv * xv, axis=-1, keepdims=True) + EPS)
        nrm = xv * rstd
        dhv = dh_ref[...].astype(F32)
        dn = dhv * w_ref[...]
        dx = dres_ref[...] + rstd * (dn - nrm * jnp.mean(dn * nrm, axis=-1, keepdims=True))
        dx_ref[...] = dx
        if with_bf16:
            outs[1][...] = dx.astype(BF16)

        @pl.when(pl.program_id(0) == 0)
        def _():
            dw_ref[...] = jnp.zeros_like(dw_ref)

        dw_ref[...] += jnp.sum(dhv * nrm, axis=0, keepdims=True)

    row = pl.BlockSpec((tr, d), lambda i: (i, 0))
    out_shape = [jax.ShapeDtypeStruct((s, d), F32)] + ([jax.ShapeDtypeStruct((s, d), BF16)] if with_bf16 else [])
    out_shape.append(jax.ShapeDtypeStruct((8, d), F32))
    out_specs = [row] * (2 if with_bf16 else 1) + [pl.BlockSpec((8, d), lambda i: (0, 0))]
    return pl.pallas_call(
        body, name=name, grid=(s // tr,),
        in_specs=[row, pl.BlockSpec((1, d), lambda i: (0, 0)), row, row],
        out_specs=out_specs, out_shape=out_shape, compiler_params=_params("arbitrary"))(x, w, dh, dres)


def _loss_grad(y, target):
    s, d = y.shape
    tr = 256

    def body(y_ref, t_ref, dy_ref, dyb_ref, l_ref):
        diff = y_ref[...] - t_ref[...]
        dy = diff * (1.0 / d)
        dy_ref[...] = dy
        dyb_ref[...] = dy.astype(BF16)

        @pl.when(pl.program_id(0) == 0)
        def _():
            l_ref[...] = jnp.zeros_like(l_ref)

        l_ref[...] += jnp.sum(jnp.sum(diff * diff, axis=1, keepdims=True), axis=0, keepdims=True)

    row = pl.BlockSpec((tr, d), lambda i: (i, 0))
    return pl.pallas_call(
        body, name="loss_grad", grid=(s // tr,), in_specs=[row, row],
        out_specs=[row, row, pl.BlockSpec((8, 128), lambda i: (0, 0))],
        out_shape=[jax.ShapeDtypeStruct((s, d), F32), jax.ShapeDtypeStruct((s, d), BF16),
                   jax.ShapeDtypeStruct((8, 128), F32)],
        compiler_params=_params("arbitrary"))(y, target)


def _adamw(w, g, m, v, name):
    r, c = w.shape
    tr = min(r, 256)
    assert r % tr == 0

    def body(w_ref, g_ref, m_ref, v_ref, d_ref, nm_ref, nv_ref):
        gv = g_ref[...]
        nm = ADAM_B1 * m_ref[...] + (1.0 - ADAM_B1) * gv
        nv = ADAM_B2 * v_ref[...] + (1.0 - ADAM_B2) * (gv * gv)
        m_hat = nm / (1.0 - ADAM_B1 ** ADAM_STEP)
        v_hat = nv / (1.0 - ADAM_B2 ** ADAM_STEP)
        d_ref[...] = -ADAM_LR * (m_hat / (jnp.sqrt(v_hat) + ADAM_EPS) + ADAM_WD * w_ref[...])
        nm_ref[...] = nm
        nv_ref[...] = nv

    spec = pl.BlockSpec((tr, c), lambda i: (i, 0))
    shp = jax.ShapeDtypeStruct((r, c), F32)
    return pl.pallas_call(body, name=name, grid=(r // tr,), in_specs=[spec] * 4, out_specs=[spec] * 3,
                          out_shape=[shp] * 3, compiler_params=_params("parallel"))(w, g, m, v)


def _rope_tables(s, dim):
    inv = 1.0 / (ROPE_THETA ** (jnp.arange(0, dim, 2, dtype=F32) / dim))
    ang = jnp.arange(s).astype(F32)[:, None] * inv[None, :]
    return jnp.cos(ang), jnp.sin(ang)


def _rope_half(x, cos, sin):
    h = x.shape[1] // 2
    x1, x2 = x[:, :h], x[:, h:]
    return jnp.concatenate([x1 * cos - x2 * sin, x2 * cos + x1 * sin], axis=1)


def _unrope_half(dy, cos, sin):
    h = dy.shape[1] // 2
    d1, d2 = dy[:, :h], dy[:, h:]
    return jnp.concatenate([d1 * cos + d2 * sin, d2 * cos - d1 * sin], axis=1)


def _lane(shape):
    return lax.broadcasted_iota(jnp.int32, shape, 1)


def _partner64(x):
    w = x.shape[1]
    first = (_lane(x.shape) % HEAD_DIM) < (HEAD_DIM // 2)
    return jnp.where(first, pltpu.roll(x, w - HEAD_DIM // 2, 1), pltpu.roll(x, HEAD_DIM // 2, 1))


def _tile_lanes(t, reps):
    return t if reps == 1 else jnp.concatenate([t] * reps, axis=1)


def _group_mean(x, ones_bd):
    hi = x.astype(BF16)
    lo = (x - hi.astype(F32)).astype(BF16)
    return _dot(hi, ones_bd) + _dot(lo, ones_bd)


def _block_diag_mean(width):
    idx = jnp.arange(width) // HEAD_DIM
    return jnp.where(idx[:, None] == idx[None, :], 1.0 / HEAD_DIM, 0.0).astype(BF16)


def _retention_tables():
    h = RET_HEADS
    log_g = jnp.log(1.0 - 2.0 ** (-5.0 - jnp.arange(h, dtype=F32)))
    idx = jnp.arange(BLK, dtype=F32)
    diff = idx[:, None] - idx[None, :]
    intra = jnp.where(diff >= 0, jnp.exp(log_g[:, None, None] * jnp.maximum(diff, 0.0)), 0.0).astype(F32)
    q_dec = jnp.exp(log_g[:, None] * (idx[None, :] + 1.0)).astype(F32)[:, :, None]
    k_dec = jnp.exp(log_g[:, None] * (BLK - 1.0 - idx[None, :])).astype(F32)[:, :, None]
    chunk_dec = jnp.exp(log_g * BLK).astype(F32)[:, None, None]
    return intra, q_dec, k_dec, chunk_dec


def _retention_fwd(proj, cos, sin, tables):
    s = proj.shape[0]
    nc = s // BLK
    intra, q_dec, k_dec, chunk_dec = tables

    def body(p_ref, cos_ref, sin_ref, in_ref, qd_ref, kd_ref, cd_ref, o_ref, cat_ref, st_ref, state):
        @pl.when(pl.program_id(0) == 0)
        def _():
            state[...] = jnp.zeros_like(state)

        cosv, sinv = cos_ref[...], sin_ref[...]
        for h in range(RET_HEADS):
            c0 = h * RET_DIM
            q = p_ref[:, c0:c0 + RET_DIM].astype(F32)
            k = p_ref[:, RET_WIDTH + c0:RET_WIDTH + c0 + RET_DIM].astype(F32)
            v = p_ref[:, 2 * RET_WIDTH + c0:2 * RET_WIDTH + c0 + RET_DIM]
            g = p_ref[:, 3 * RET_WIDTH + c0:3 * RET_WIDTH + c0 + RET_DIM].astype(F32)
            qb = _rope_half(q, cosv, sinv).astype(BF16)
            kr = _rope_half(k, cosv, sinv) * (RET_DIM ** -0.5)
            kb = kr.astype(BF16)
            scores = _dot_nt(qb, kb) * in_ref[h]
            inner = _dot(scores.astype(BF16), v)
            prev = state[h]
            prev_b = prev.astype(BF16)
            st_ref[h, 0] = prev_b
            o = inner + _dot(qb, prev_b) * qd_ref[h]
            o_ref[:, c0:c0 + RET_DIM] = o
            rstd = lax.rsqrt(jnp.mean(o * o, axis=-1, keepdims=True) + EPS)
            cat_ref[:, c0:c0 + RET_DIM] = (o * rstd * (g * _sigmoid(g))).astype(BF16)
            state[h] = cd_ref[h] * prev + _dot_tn((kr * kd_ref[h]).astype(BF16), v)

    full = lambda shape: pl.BlockSpec(shape, lambda n: (0,) * len(shape))
    return pl.pallas_call(
        body, name="retention_fwd", grid=(nc,),
        in_specs=[pl.BlockSpec((BLK, 4 * RET_WIDTH), lambda n: (n, 0)),
                  pl.BlockSpec((BLK, RET_DIM // 2), lambda n: (n, 0)), pl.BlockSpec((BLK, RET_DIM // 2), lambda n: (n, 0)),
                  full((RET_HEADS, BLK, BLK)), full((RET_HEADS, BLK, 1)), full((RET_HEADS, BLK, 1)), full((RET_HEADS, 1, 1))],
        out_specs=[pl.BlockSpec((BLK, RET_WIDTH), lambda n: (n, 0)), pl.BlockSpec((BLK, RET_WIDTH), lambda n: (n, 0)),
                   pl.BlockSpec((RET_HEADS, 1, RET_DIM, RET_DIM), lambda n: (0, n, 0, 0))],
        out_shape=[jax.ShapeDtypeStruct((s, RET_WIDTH), F32), jax.ShapeDtypeStruct((s, D_MODEL), BF16),
                   jax.ShapeDtypeStruct((RET_HEADS, nc, RET_DIM, RET_DIM), BF16)],
        scratch_shapes=[pltpu.VMEM((RET_HEADS, RET_DIM, RET_DIM), F32)],
        compiler_params=_params("arbitrary"))(proj, cos, sin, intra, q_dec, k_dec, chunk_dec)


def _retention_bwd(proj, o, states, dcat, cos, sin, tables):
    s = proj.shape[0]
    nc = s // BLK
    intra, q_dec, k_dec, chunk_dec = tables

    def body(p_ref, o_ref, st_ref, dc_ref, cos_ref, sin_ref, in_ref, qd_ref, kd_ref, cd_ref, dp_ref, dstate):
        @pl.when(pl.program_id(0) == 0)
        def _():
            dstate[...] = jnp.zeros_like(dstate)

        cosv, sinv = cos_ref[...], sin_ref[...]
        for h in range(RET_HEADS):
            c0 = h * RET_DIM
            q = p_ref[:, c0:c0 + RET_DIM].astype(F32)
            k = p_ref[:, RET_WIDTH + c0:RET_WIDTH + c0 + RET_DIM].astype(F32)
            v = p_ref[:, 2 * RET_WIDTH + c0:2 * RET_WIDTH + c0 + RET_DIM]
            g = p_ref[:, 3 * RET_WIDTH + c0:3 * RET_WIDTH + c0 + RET_DIM].astype(F32)
            o = o_ref[:, c0:c0 + RET_DIM]
            dc = dc_ref[:, c0:c0 + RET_DIM].astype(F32)
            rstd = lax.rsqrt(jnp.mean(o * o, axis=-1, keepdims=True) + EPS)
            nrm = o * rstd
            sg = _sigmoid(g)
            dg = dc * nrm * (sg * (1.0 + g * (1.0 - sg)))
            dn = dc * (g * sg)
            do = rstd * (dn - nrm * jnp.mean(dn * nrm, axis=-1, keepdims=True))
            qb = _rope_half(q, cosv, sinv).astype(BF16)
            kr = _rope_half(k, cosv, sinv) * (RET_DIM ** -0.5)
            kb = kr.astype(BF16)
            mask = in_ref[h]
            qd, kd = qd_ref[h], kd_ref[h]
            prev_b = st_ref[h, 0]
            dnext = dstate[h]
            dnext_b = dnext.astype(BF16)
            att = (_dot_nt(qb, kb) * mask).astype(BF16)
            do_b = do.astype(BF16)
            doq = (do * qd).astype(BF16)
            dv = _dot_tn(att, do_b) + _dot((kr * kd).astype(BF16), dnext_b)
            ds = (_dot_nt(do_b, v) * mask).astype(BF16)
            dqr = _dot(ds, kb) + _dot_nt(doq, prev_b)
            dkr = _dot_tn(ds, qb) + _dot_nt(v, dnext_b) * kd
            dstate[h] = cd_ref[h] * dnext + _dot_tn(qb, doq)
            dq = _unrope_half(dqr, cosv, sinv)
            dk = _unrope_half(dkr * (RET_DIM ** -0.5), cosv, sinv)
            dp_ref[:, c0:c0 + RET_DIM] = dq.astype(BF16)
            dp_ref[:, RET_WIDTH + c0:RET_WIDTH + c0 + RET_DIM] = dk.astype(BF16)
            dp_ref[:, 2 * RET_WIDTH + c0:2 * RET_WIDTH + c0 + RET_DIM] = dv.astype(BF16)
            dp_ref[:, 3 * RET_WIDTH + c0:3 * RET_WIDTH + c0 + RET_DIM] = dg.astype(BF16)

    rev = lambda n: nc - 1 - n
    full = lambda shape: pl.BlockSpec(shape, lambda n: (0,) * len(shape))
    return pl.pallas_call(
        body, name="retention_bwd", grid=(nc,),
        in_specs=[pl.BlockSpec((BLK, 4 * RET_WIDTH), lambda n: (rev(n), 0)),
                  pl.BlockSpec((BLK, RET_WIDTH), lambda n: (rev(n), 0)),
                  pl.BlockSpec((RET_HEADS, 1, RET_DIM, RET_DIM), lambda n: (0, rev(n), 0, 0)),
                  pl.BlockSpec((BLK, RET_WIDTH), lambda n: (rev(n), 0)),
                  pl.BlockSpec((BLK, RET_DIM // 2), lambda n: (rev(n), 0)), pl.BlockSpec((BLK, RET_DIM // 2), lambda n: (rev(n), 0)),
                  full((RET_HEADS, BLK, BLK)), full((RET_HEADS, BLK, 1)), full((RET_HEADS, BLK, 1)), full((RET_HEADS, 1, 1))],
        out_specs=pl.BlockSpec((BLK, 4 * RET_WIDTH), lambda n: (rev(n), 0)),
        out_shape=jax.ShapeDtypeStruct((s, EVEN_IN), BF16),
        scratch_shapes=[pltpu.VMEM((RET_HEADS, RET_DIM, RET_DIM), F32)],
        compiler_params=_params("arbitrary"))(proj, o, states, dcat, cos, sin, intra, q_dec, k_dec, chunk_dec)


CONV_ROWS = 256
HALO = 16


def _conv_pieces(p, halo, conv_w, first):
    rows = p.shape[0]
    gb, gc, u, gv = (p[:, i * CONV_WIDTH:(i + 1) * CONV_WIDTH] for i in range(4))
    cu = gc * u
    hcu = halo[:, CONV_WIDTH:2 * CONV_WIDTH] * halo[:, 2 * CONV_WIDTH:3 * CONV_WIDTH]
    hcu = jnp.where(first, 0.0, hcu)
    r1, r2 = hcu[HALO - 1:HALO], hcu[HALO - 2:HALO - 1]
    row = lax.broadcasted_iota(jnp.int32, cu.shape, 0)
    m1 = jnp.where(row == 0, r1, pltpu.roll(cu, 1, 0))
    m2 = jnp.where(row == 0, r2, jnp.where(row == 1, r1, pltpu.roll(cu, 2, 0)))
    del rows, conv_w
    return gb, gc, u, gv, cu, m1, m2


def _conv_fwd(proj, conv_w, cat):
    s = proj.shape[0]
    per = CONV_ROWS // HALO

    def body(p_ref, halo_ref, w_ref, cat_in, cat_ref):
        del cat_in
        first = pl.program_id(0) == 0
        gb, _, _, gv, cu, m1, m2 = _conv_pieces(p_ref[...].astype(F32), halo_ref[...].astype(F32), None, first)
        conv = w_ref[0:1, :] * m2 + w_ref[1:2, :] * m1 + w_ref[2:3, :] * cu
        cat_ref[...] = (gb * conv * (gv * _sigmoid(gv))).astype(BF16)

    return pl.pallas_call(
        body, name="conv_fwd", grid=(s // CONV_ROWS,),
        in_specs=[pl.BlockSpec((CONV_ROWS, 4 * CONV_WIDTH), lambda i: (i, 1)),
                  pl.BlockSpec((HALO, 4 * CONV_WIDTH), lambda i: (jnp.maximum(i * per - 1, 0), 1)),
                  pl.BlockSpec((3, CONV_WIDTH), lambda i: (0, 0)), ANY],
        out_specs=pl.BlockSpec((CONV_ROWS, CONV_WIDTH), lambda i: (i, 1)),
        out_shape=jax.ShapeDtypeStruct(cat.shape, cat.dtype), input_output_aliases={3: 0},
        compiler_params=_params("parallel"))(proj, proj, conv_w, cat)


def _conv_bwd(proj, dcat, conv_w, dproj):
    s = proj.shape[0]
    per = CONV_ROWS // HALO
    last_halo = s // HALO - 1
    nsteps = s // CONV_ROWS

    def body(p_ref, halo_ref, nxt_ref, dc_ref, dnxt_ref, w_ref, dp_in, dp_ref, dw_ref):
        del dp_in
        i = pl.program_id(0)
        gb, gc, u, gv, cu, m1, m2 = _conv_pieces(p_ref[...].astype(F32), halo_ref[...].astype(F32), None, i == 0)
        w0, w1, w2 = w_ref[0:1, :], w_ref[1:2, :], w_ref[2:3, :]
        conv = w0 * m2 + w1 * m1 + w2 * cu
        dco = dc_ref[...].astype(F32)
        sg = _sigmoid(gv)
        silu = gv * sg
        dgb = dco * conv * silu
        dgv = dco * gb * conv * (sg * (1.0 + gv * (1.0 - sg)))
        dconv = dco * gb * silu
        nxt = nxt_ref[...].astype(F32)
        ngv = nxt[:, 3 * CONV_WIDTH:]
        dnext = dnxt_ref[...].astype(F32) * nxt[:, :CONV_WIDTH] * (ngv * _sigmoid(ngv))
        dnext = jnp.where(i == nsteps - 1, 0.0, dnext)
        n1, n2 = dnext[0:1], dnext[1:2]
        row = lax.broadcasted_iota(jnp.int32, dconv.shape, 0)
        p1 = jnp.where(row == CONV_ROWS - 1, n1, pltpu.roll(dconv, CONV_ROWS - 1, 0))
        p2 = jnp.where(row == CONV_ROWS - 1, n2, jnp.where(row == CONV_ROWS - 2, n1, pltpu.roll(dconv, CONV_ROWS - 2, 0)))
        dcu = w2 * dconv + w1 * p1 + w0 * p2
        dp_ref[...] = jnp.concatenate([dgb, dcu * u, dcu * gc, dgv], axis=1).astype(BF16)

        @pl.when(i == 0)
        def _():
            dw_ref[...] = jnp.zeros_like(dw_ref)

        taps = [jnp.sum(dconv * m, axis=0, keepdims=True) for m in (m2, m1, cu)]
        r8 = lax.broadcasted_iota(jnp.int32, dw_ref.shape, 0)
        dw_ref[...] += jnp.where(r8 == 0, taps[0], jnp.where(r8 == 1, taps[1], jnp.where(r8 == 2, taps[2], 0.0)))

    return pl.pallas_call(
        body, name="conv_bwd", grid=(nsteps,),
        in_specs=[pl.BlockSpec((CONV_ROWS, 4 * CONV_WIDTH), lambda i: (i, 1)),
                  pl.BlockSpec((HALO, 4 * CONV_WIDTH), lambda i: (jnp.maximum(i * per - 1, 0), 1)),
                  pl.BlockSpec((HALO, 4 * CONV_WIDTH), lambda i: (jnp.minimum((i + 1) * per, last_halo), 1)),
                  pl.BlockSpec((CONV_ROWS, CONV_WIDTH), lambda i: (i, 1)),
                  pl.BlockSpec((HALO, CONV_WIDTH), lambda i: (jnp.minimum((i + 1) * per, last_halo), 1)),
                  pl.BlockSpec((3, CONV_WIDTH), lambda i: (0, 0)), ANY],
        out_specs=[pl.BlockSpec((CONV_ROWS, 4 * CONV_WIDTH), lambda i: (i, 1)), pl.BlockSpec((8, CONV_WIDTH), lambda i: (0, 0))],
        out_shape=[jax.ShapeDtypeStruct(dproj.shape, dproj.dtype), jax.ShapeDtypeStruct((8, CONV_WIDTH), F32)],
        input_output_aliases={6: 0},
        compiler_params=_params("arbitrary"))(proj, proj, proj, dcat, dcat, conv_w, dproj)


GROUP_WIDTH = 8 * HEAD_DIM
SLAB = 128
Q_COL = 0
KV_COL = 4
GATE_COL = 5
K_SLAB0 = ATTN_WIDTH // SLAB
V_SLAB0 = (ATTN_WIDTH + KV_WIDTH) // SLAB


def _half_mask(shape, which):
    return (_lane(shape) // HEAD_DIM) == which


def _dup_head(slab, which):
    kept = jnp.where(_half_mask(slab.shape, which), slab, 0.0)
    return kept + pltpu.roll(kept, HEAD_DIM, 1)


def _q_prep(q, qw, cosf, sins, ones_bd):
    rstd = lax.rsqrt(_group_mean(q * q, ones_bd) + EPS)
    nrm = q * rstd
    y = nrm * qw
    return nrm, rstd, y * cosf + _partner64(y) * sins


def _attn_probs(qm, kcat, sink, valid):
    sc = _dot_nt(qm, kcat) * (HEAD_DIM ** -0.5)
    sc = jnp.where(valid, sc, -1e30)
    m = jnp.maximum(jnp.max(sc, axis=-1, keepdims=True), sink)
    p = jnp.exp(sc - m)
    psink = jnp.exp(sink - m)
    inv = 1.0 / (jnp.sum(p, axis=-1, keepdims=True) + psink)
    return p * inv, psink * inv


def _valid_mask(n):
    qi = lax.broadcasted_iota(jnp.int32, (BLK, 2 * BLK), 0) + BLK
    kj = lax.broadcasted_iota(jnp.int32, (BLK, 2 * BLK), 1)
    band = jnp.logical_and(kj <= qi, qi - kj < BLK)
    return jnp.logical_and(band, jnp.logical_or(n > 0, kj >= BLK))


def _keys_values(kc_ref, kp_ref, vc_ref, vp_ref, kw, ones_k, cosc, sinc, cosp, sinp, which):
    def key(ref, cosf, sins):
        k = ref[...].astype(F32)
        y = k * lax.rsqrt(_group_mean(k * k, ones_k) + EPS) * kw
        return _dup_head(y * cosf + _partner64(y) * sins, which)

    kcat = jnp.concatenate([key(kp_ref, cosp, sinp), key(kc_ref, cosc, sinc)], axis=0).astype(BF16)
    vcat = jnp.concatenate([_dup_head(vp_ref[...].astype(F32), which), _dup_head(vc_ref[...].astype(F32), which)],
                           axis=0).astype(BF16)
    return kcat, vcat


def _swa_specs(nb):
    prev = lambda n: jnp.maximum(n - 1, 0)
    slab = lambda col0, row: pl.BlockSpec((BLK, SLAB), lambda g, n: (row(n), col0 + g // 2))
    cur = lambda n: n
    tab = lambda row: pl.BlockSpec((BLK, SLAB), lambda g, n: (row(n), 0))
    full = lambda shape: pl.BlockSpec(shape, lambda g, n: (0,) * len(shape))
    del nb
    return dict(
        sinks=pl.BlockSpec(memory_space=pltpu.SMEM),
        q=pl.BlockSpec((BLK, GROUP_WIDTH), lambda g, n: (n, Q_COL + g)),
        gate=pl.BlockSpec((BLK, GROUP_WIDTH), lambda g, n: (n, GATE_COL + g)),
        kc=slab(K_SLAB0, cur), kp=slab(K_SLAB0, prev), vc=slab(V_SLAB0, cur), vp=slab(V_SLAB0, prev),
        tab_c=tab(cur), tab_p=tab(prev),
        qw=full((1, GROUP_WIDTH)), kw=full((1, SLAB)), ones_q=full((GROUP_WIDTH, GROUP_WIDTH)), ones_k=full((SLAB, SLAB)),
        group=pl.BlockSpec((BLK, GROUP_WIDTH), lambda g, n: (n, g)))


def _swa_fwd(proj, sinks, qw, kw, cos, sins, ones_q, ones_k):
    s = proj.shape[0]
    nb = s // BLK
    sp = _swa_specs(nb)

    def body(sink_ref, q_ref, kc_ref, kp_ref, vc_ref, vp_ref, gate_ref, cosc_ref, sinc_ref, cosp_ref, sinp_ref,
             qw_ref, kw_ref, oq_ref, ok_ref, ag_ref, o_ref):
        g, n = pl.program_id(0), pl.program_id(1)
        cosc, sinc = cosc_ref[...], sinc_ref[...]
        kcat, vcat = _keys_values(kc_ref, kp_ref, vc_ref, vp_ref, kw_ref[...], ok_ref[...], cosc, sinc,
                                  cosp_ref[...], sinp_ref[...], g % 2)
        _, _, qr = _q_prep(q_ref[...].astype(F32), qw_ref[...], _tile_lanes(cosc, 4), _tile_lanes(sinc, 4), oq_ref[...])
        valid = _valid_mask(n)
        slabs = []
        for sl in range(GROUP_WIDTH // SLAB):
            qs = qr[:, sl * SLAB:(sl + 1) * SLAB]
            out = None
            for e in range(2):
                qm = jnp.where(_half_mask(qs.shape, e), qs, 0.0).astype(BF16)
                p, _ = _attn_probs(qm, kcat, sink_ref[g * 8 + sl * 2 + e], valid)
                oe = _dot(p.astype(BF16), vcat)
                out = oe if e == 0 else jnp.where(_half_mask(oe.shape, 0), out, oe)
            slabs.append(out)
        o = jnp.concatenate(slabs, axis=1)
        gate = gate_ref[...].astype(F32)
        o_ref[...] = o.astype(BF16)
        ag_ref[...] = (o * (gate * _sigmoid(gate))).astype(BF16)

    shp = jax.ShapeDtypeStruct((s, ATTN_WIDTH), BF16)
    return pl.pallas_call(
        body, name="swa_fwd", grid=(KV_HEADS, nb),
        in_specs=[sp["sinks"], sp["q"], sp["kc"], sp["kp"], sp["vc"], sp["vp"], sp["gate"], sp["tab_c"], sp["tab_c"],
                  sp["tab_p"], sp["tab_p"], sp["qw"], sp["kw"], sp["ones_q"], sp["ones_k"]],
        out_specs=[sp["group"], sp["group"]], out_shape=[shp, shp],
        compiler_params=_params("parallel", "arbitrary"),
    )(sinks, proj, proj, proj, proj, proj, proj, cos, sins, cos, sins, qw, kw, ones_q, ones_k)


def _swa_bwd(proj, o, dag, sinks, qw, kw, cos, sins, ones_q, ones_k):
    s = proj.shape[0]
    nb = s // BLK
    sp = _swa_specs(nb)

    def body(sink_ref, q_ref, kc_ref, kp_ref, vc_ref, vp_ref, gate_ref, o_ref, dag_ref, cosc_ref, sinc_ref, cosp_ref, sinp_ref,
             qw_ref, kw_ref, oq_ref, ok_ref, dq_ref, dgate_ref, dkc_ref, dkp_ref, dvc_ref, dvp_ref, dqw_ref, dsink_ref):
        g, n = pl.program_id(0), pl.program_id(1)
        cosc, sinc = cosc_ref[...], sinc_ref[...]
        kcat, vcat = _keys_values(kc_ref, kp_ref, vc_ref, vp_ref, kw_ref[...], ok_ref[...], cosc, sinc,
                                  cosp_ref[...], sinp_ref[...], g % 2)
        cosq, sinq = _tile_lanes(cosc, 4), _tile_lanes(sinc, 4)
        qwv = qw_ref[...]
        nrm, rstd, qr = _q_prep(q_ref[...].astype(F32), qwv, cosq, sinq, oq_ref[...])
        gate = gate_ref[...].astype(F32)
        sg = _sigmoid(gate)
        dagv = dag_ref[...].astype(F32)
        dgate_ref[...] = (dagv * o_ref[...].astype(F32) * (sg * (1.0 + gate * (1.0 - sg)))).astype(BF16)
        do = dagv * (gate * sg)
        valid = _valid_mask(n)
        dk_acc = jnp.zeros((2 * BLK, SLAB), F32)
        dv_acc = jnp.zeros((2 * BLK, SLAB), F32)
        dq_slabs, dsinks = [], []
        for sl in range(GROUP_WIDTH // SLAB):
            qs = qr[:, sl * SLAB:(sl + 1) * SLAB]
            dos = do[:, sl * SLAB:(sl + 1) * SLAB]
            dq_s = None
            for e in range(2):
                keep = _half_mask(qs.shape, e)
                qm = jnp.where(keep, qs, 0.0).astype(BF16)
                dom = jnp.where(keep, dos, 0.0).astype(BF16)
                p, psink = _attn_probs(qm, kcat, sink_ref[g * 8 + sl * 2 + e], valid)
                dp = _dot_nt(dom, vcat)
                delta = jnp.sum(p * dp, axis=-1, keepdims=True)
                ds = (p * (dp - delta) * (HEAD_DIM ** -0.5)).astype(BF16)
                dsinks.append(-jnp.sum(psink * delta, axis=0, keepdims=True))
                dqe = _dot(ds, kcat)
                dq_s = dqe if e == 0 else jnp.where(_half_mask(dqe.shape, 0), dq_s, dqe)
                dk_acc = dk_acc + _dot_tn(ds, qm)
                dv_acc = dv_acc + _dot_tn(p.astype(BF16), dom)
            dq_slabs.append(dq_s)
        dk_acc = dk_acc + pltpu.roll(dk_acc, HEAD_DIM, 1)
        dv_acc = dv_acc + pltpu.roll(dv_acc, HEAD_DIM, 1)
        dkp_ref[0], dkc_ref[0] = dk_acc[:BLK], dk_acc[BLK:]
        dvp_ref[0], dvc_ref[0] = dv_acc[:BLK], dv_acc[BLK:]
        dqr = jnp.concatenate(dq_slabs, axis=1)
        dy = dqr * cosq + _partner64(dqr * sinq)
        dn = dy * qwv
        dq_ref[...] = (rstd * (dn - nrm * _group_mean(dn * nrm, oq_ref[...]))).astype(BF16)

        @pl.when(n == 0)
        def _():
            dqw_ref[...] = jnp.zeros_like(dqw_ref)
            dsink_ref[...] = jnp.zeros_like(dsink_ref)

        dqw_ref[0] += jnp.sum(dy * nrm, axis=0, keepdims=True)
        r8 = lax.broadcasted_iota(jnp.int32, (8, SLAB), 0)
        upd = jnp.zeros((8, SLAB), F32)
        for j, dsk in enumerate(dsinks):
            upd = jnp.where(r8 == j, dsk, upd)
        dsink_ref[0] += upd

    kv_out = pl.BlockSpec((1, BLK, SLAB), lambda g, n: (g, n, 0))
    kv_shape = jax.ShapeDtypeStruct((KV_HEADS, s, SLAB), F32)
    return pl.pallas_call(
        body, name="swa_bwd", grid=(KV_HEADS, nb),
        in_specs=[sp["sinks"], sp["q"], sp["kc"], sp["kp"], sp["vc"], sp["vp"], sp["gate"], sp["group"], sp["group"],
                  sp["tab_c"], sp["tab_c"], sp["tab_p"], sp["tab_p"], sp["qw"], sp["kw"], sp["ones_q"], sp["ones_k"]],
        out_specs=[sp["q"], sp["group"], kv_out, kv_out, kv_out, kv_out,
                   pl.BlockSpec((1, 8, GROUP_WIDTH), lambda g, n: (g, 0, 0)), pl.BlockSpec((1, 8, SLAB), lambda g, n: (g, 0, 0))],
        out_shape=[jax.ShapeDtypeStruct((s, ODD_IN), BF16), jax.ShapeDtypeStruct((s, ATTN_WIDTH), BF16),
                   kv_shape, kv_shape, kv_shape, kv_shape,
                   jax.ShapeDtypeStruct((KV_HEADS, 8, GROUP_WIDTH), F32), jax.ShapeDtypeStruct((KV_HEADS, 8, SLAB), F32)],
        compiler_params=_params("parallel", "arbitrary"),
    )(sinks, proj, proj, proj, proj, proj, proj, o, dag, cos, sins, cos, sins, qw, kw, ones_q, ones_k)


def _swa_bwd_kv(proj, dproj, dgate, dkc, dkp, dvc, dvp, kw, cos, sins, ones_k):
    s = proj.shape[0]
    nb = s // BLK

    def body(kv_ref, dgate_ref, dkc_ref, dkp_ref, dvc_ref, dvp_ref, kw_ref, cos_ref, sin_ref, ok_ref, dp_in, dp_ref, dkw_ref):
        del dp_in
        n, j = pl.program_id(0), pl.program_id(1)

        @pl.when(jnp.logical_and(n == 0, j == 0))
        def _():
            dkw_ref[...] = jnp.zeros_like(dkw_ref)

        @pl.when(j == 0)
        def _():
            has_next = n < nb - 1

            def assemble(cur_ref, nxt_ref):
                tot = [cur_ref[h] + jnp.where(has_next, nxt_ref[h], 0.0) for h in range(KV_HEADS)]
                first = _half_mask(tot[0].shape, 0)
                return jnp.concatenate([jnp.where(first, tot[0], tot[1]), jnp.where(first, tot[2], tot[3])], axis=1)

            dkr = assemble(dkc_ref, dkp_ref)
            dv = assemble(dvc_ref, dvp_ref)
            cosf, sinf = _tile_lanes(cos_ref[...], 2), _tile_lanes(sin_ref[...], 2)
            k = kv_ref[:, :KV_WIDTH].astype(F32)
            rstd = lax.rsqrt(_group_mean(k * k, ok_ref[...]) + EPS)
            nrm = k * rstd
            dy = dkr * cosf + _partner64(dkr * sinf)
            dn = dy * kw_ref[...]
            dk = rstd * (dn - nrm * _group_mean(dn * nrm, ok_ref[...]))
            dp_ref[...] = jnp.concatenate([dk, dv], axis=1).astype(BF16)
            dkw_ref[...] += jnp.sum(dy * nrm, axis=0, keepdims=True)

        @pl.when(j > 0)
        def _():
            dp_ref[...] = dgate_ref[...]


    nxt = lambda n: jnp.minimum(n + 1, nb - 1)
    acc = lambda row: pl.BlockSpec((KV_HEADS, BLK, SLAB), lambda n, j: (0, row(n), 0))
    return pl.pallas_call(
        body, name="swa_bwd_kv", grid=(nb, 5),
        in_specs=[pl.BlockSpec((BLK, GROUP_WIDTH), lambda n, j: (n, KV_COL)),
                  pl.BlockSpec((BLK, GROUP_WIDTH), lambda n, j: (n, jnp.maximum(j - 1, 0))),
                  acc(lambda n: n), acc(nxt), acc(lambda n: n), acc(nxt),
                  pl.BlockSpec((1, KV_WIDTH), lambda n, j: (0, 0)),
                  pl.BlockSpec((BLK, SLAB), lambda n, j: (n, 0)), pl.BlockSpec((BLK, SLAB), lambda n, j: (n, 0)),
                  pl.BlockSpec((KV_WIDTH, KV_WIDTH), lambda n, j: (0, 0)), ANY],
        out_specs=[pl.BlockSpec((BLK, GROUP_WIDTH), lambda n, j: (n, KV_COL + j)), pl.BlockSpec((8, KV_WIDTH), lambda n, j: (0, 0))],
        out_shape=[jax.ShapeDtypeStruct(dproj.shape, dproj.dtype), jax.ShapeDtypeStruct((8, KV_WIDTH), F32)],
        input_output_aliases={10: 0},
        compiler_params=_params("arbitrary", "arbitrary"),
    )(proj, dgate, dkc, dkp, dvc, dvp, kw, cos, sins, ones_k, dproj)


def _place():
    x, y, c = lax.axis_index("x"), lax.axis_index("y"), lax.axis_index("c")
    return x, y, c


OTHER_CHIPS = ((1, 0), (0, 1), (1, 1))


def _half_rows(ref, half, rows):
    return ref.at[pl.ds(pl.multiple_of(half * (rows // 2), 8), rows // 2)]


def _gather_weights(w_in0, w_out0, w_in1, w_out1, small):
    pieces = (
        (w_in0, "col"), (w_out0, "row"), (w_in1, "col"), (w_out1, "row"))
    n_big = len(pieces)

    def body(a0, a1, a2, a3, sm, f0, f1, f2, f3, fsm, ici_send, ici_recv, fwd_send, fwd_recv, small_send, small_recv, local_sem):
        x, y, c = _place()
        my_chip = 2 * x + y
        shards, fulls = (a0, a1, a2, a3), (f0, f1, f2, f3)

        def region(i, chip, half=None):
            rows, width = shards[i].shape
            if pieces[i][1] == "col":
                cols = fulls[i].at[:, pl.ds(pl.multiple_of(chip * width, 128), width)]
                return cols if half is None else _half_rows(cols, half, rows)
            whole = fulls[i].at[pl.ds(pl.multiple_of(chip * rows, 8), rows)]
            return whole if half is None else _half_rows(whole, half, rows)

        local = [pltpu.make_async_copy(shards[i], region(i, my_chip), local_sem.at[i]) for i in range(n_big)]
        local.append(pltpu.make_async_copy(sm, fsm.at[my_chip], local_sem.at[n_big]))
        for cp in local:
            cp.start()
        sends = []
        for r, (fx, fy) in enumerate(OTHER_CHIPS):
            peer = (x ^ fx, y ^ fy, c)
            for i in range(n_big):
                cp = pltpu.make_async_remote_copy(
                    src_ref=_half_rows(shards[i], c, shards[i].shape[0]), dst_ref=region(i, my_chip, c),
                    send_sem=ici_send.at[r * n_big + i], recv_sem=ici_recv.at[r * n_big + i], device_id=peer, device_id_type=MESH)
                cp.start()
                sends.append(cp)
            cp = pltpu.make_async_remote_copy(src_ref=sm, dst_ref=fsm.at[my_chip], send_sem=small_send.at[r],
                                              recv_sem=small_recv.at[r], device_id=peer, device_id_type=MESH)
            cp.start()
            sends.append(cp)
        sibling = (x, y, 1 - c)
        for r, (fx, fy) in enumerate(OTHER_CHIPS):
            chip = 2 * (x ^ fx) + (y ^ fy)
            for i in range(n_big):
                landed = region(i, chip, c)
                pltpu.make_async_remote_copy(src_ref=landed, dst_ref=landed, send_sem=ici_send.at[r * n_big + i],
                                             recv_sem=ici_recv.at[r * n_big + i], device_id=sibling, device_id_type=MESH).wait_recv()
                cp = pltpu.make_async_remote_copy(src_ref=landed, dst_ref=landed, send_sem=fwd_send.at[r * n_big + i],
                                                  recv_sem=fwd_recv.at[r * n_big + i], device_id=sibling, device_id_type=MESH)
                cp.start()
                sends.append(cp)
            pltpu.make_async_remote_copy(src_ref=sm, dst_ref=fsm.at[chip], send_sem=small_send.at[r],
                                         recv_sem=small_recv.at[r], device_id=sibling, device_id_type=MESH).wait_recv()
        for r, (fx, fy) in enumerate(OTHER_CHIPS):
            chip = 2 * (x ^ fx) + (y ^ fy)
            for i in range(n_big):
                theirs = region(i, chip, 1 - c)
                pltpu.make_async_remote_copy(src_ref=theirs, dst_ref=theirs, send_sem=fwd_send.at[r * n_big + i],
                                             recv_sem=fwd_recv.at[r * n_big + i], device_id=sibling, device_id_type=MESH).wait_recv()
        for cp in sends:
            cp.wait_send()
        for cp in local:
            cp.wait()

    def full_shape(w, kind):
        rows, width = w.shape
        return jax.ShapeDtypeStruct((rows, N_CHIPS * width) if kind == "col" else (N_CHIPS * rows, width), w.dtype)

    out_shape = [full_shape(w, kind) for w, kind in pieces] + [jax.ShapeDtypeStruct((N_CHIPS,) + small.shape, small.dtype)]
    dma = pltpu.SemaphoreType.DMA
    return pl.pallas_call(
        body, name="gather_weights", in_specs=[ANY] * 5, out_specs=[ANY] * 5, out_shape=out_shape,
        scratch_shapes=[dma((3 * n_big,)), dma((3 * n_big,)), dma((3 * n_big,)), dma((3 * n_big,)), dma((3,)), dma((3,)),
                        dma((n_big + 1,))],
        compiler_params=pltpu.CompilerParams(has_side_effects=True),
    )(w_in0, w_out0, w_in1, w_out1, small)


def _allreduce_small(v):
    def body(v_ref, out_ref, buf, send_sems, recv_sems):
        x, y, c = _place()
        me = 4 * x + 2 * y + c
        buf[me] = v_ref[...]
        copies = []
        for r in range(1, N_DEV):
            peer = (x ^ (r >> 2), y ^ ((r >> 1) & 1), c ^ (r & 1))
            cp = pltpu.make_async_remote_copy(src_ref=v_ref, dst_ref=buf.at[me], send_sem=send_sems.at[r - 1],
                                              recv_sem=recv_sems.at[r - 1], device_id=peer, device_id_type=MESH)
            cp.start()
            copies.append(cp)
        for cp in copies:
            cp.wait_recv()
        for cp in copies:
            cp.wait_send()
        total = buf[0]
        for d in range(1, N_DEV):
            total = total + buf[d]
        out_ref[...] = total

    vm = pl.BlockSpec(memory_space=pltpu.VMEM)
    return pl.pallas_call(
        body, name="allreduce_small", in_specs=[vm], out_specs=vm, out_shape=jax.ShapeDtypeStruct(v.shape, v.dtype),
        scratch_shapes=[pltpu.VMEM((N_DEV,) + v.shape, v.dtype), pltpu.SemaphoreType.DMA((N_DEV - 1,)),
                        pltpu.SemaphoreType.DMA((N_DEV - 1,))],
        compiler_params=pltpu.CompilerParams(has_side_effects=True),
    )(v)


def _exchange_halves(grads):
    n = len(grads)

    def body(*refs):
        g, mine, theirs = refs[:n], refs[n:2 * n], refs[2 * n:3 * n]
        send_sem, recv_sem, local_sem = refs[3 * n:]
        x, y, c = _place()
        sibling = (x, y, 1 - c)

        def half(ref, which):
            if len(ref.shape) == 2:
                return _half_rows(ref, which, ref.shape[0])
            return ref.at[:, pl.ds(pl.multiple_of(which * (ref.shape[1] // 2), 8), ref.shape[1] // 2)]

        copies = []
        for i in range(n):
            lc = pltpu.make_async_copy(half(g[i], c), mine[i], local_sem.at[i])
            lc.start()
            rc = pltpu.make_async_remote_copy(src_ref=half(g[i], 1 - c), dst_ref=theirs[i], send_sem=send_sem.at[i],
                                              recv_sem=recv_sem.at[i], device_id=sibling, device_id_type=MESH)
            rc.start()
            copies.append((lc, rc))
        for lc, rc in copies:
            rc.wait_recv()
            rc.wait_send()
            lc.wait()

    def half_shape(a):
        shp = (a.shape[0] // 2, a.shape[1]) if a.ndim == 2 else (a.shape[0], a.shape[1] // 2, a.shape[2])
        return jax.ShapeDtypeStruct(shp, a.dtype)

    dma = pltpu.SemaphoreType.DMA
    outs = pl.pallas_call(
        body, name="exchange_halves", in_specs=[ANY] * n, out_specs=[ANY] * (2 * n),
        out_shape=[half_shape(a) for a in grads] * 2, scratch_shapes=[dma((n,)), dma((n,)), dma((n,))],
        compiler_params=pltpu.CompilerParams(has_side_effects=True),
    )(*grads)
    return outs[:n], outs[n:]


def _add_bf16(a, b, name):
    r, c = a.shape
    tr = 256

    def body(a_ref, b_ref, o_ref):
        o_ref[...] = (a_ref[...].astype(F32) + b_ref[...].astype(F32)).astype(BF16)

    spec = pl.BlockSpec((tr, c), lambda i: (i, 0))
    return pl.pallas_call(body, name=name, grid=(r // tr,), in_specs=[spec, spec], out_specs=spec,
                          out_shape=jax.ShapeDtypeStruct((r, c), BF16), compiler_params=_params("parallel"))(a, b)


def _scatter_to_owners(parts):
    n = len(parts)

    def body(*refs):
        src, dst = refs[:n], refs[n:2 * n]
        send_sem, recv_sem, local_sem = refs[2 * n:]
        x, y, c = _place()
        my_chip = 2 * x + y

        def piece(i, chip):
            if len(src[i].shape) == 3:
                return src[i].at[chip]
            width = src[i].shape[1] // N_CHIPS
            return src[i].at[:, pl.ds(pl.multiple_of(chip * width, 128), width)]

        copies = []
        for i in range(n):
            lc = pltpu.make_async_copy(piece(i, my_chip), dst[i].at[my_chip], local_sem.at[i])
            lc.start()
            copies.append(lc)
        remote = []
        for r, (fx, fy) in enumerate(OTHER_CHIPS):
            chip = 2 * (x ^ fx) + (y ^ fy)
            for i in range(n):
                rc = pltpu.make_async_remote_copy(src_ref=piece(i, chip), dst_ref=dst[i].at[my_chip], send_sem=send_sem.at[r * n + i],
                                                  recv_sem=recv_sem.at[r * n + i], device_id=(x ^ fx, y ^ fy, c), device_id_type=MESH)
                rc.start()
                remote.append(rc)
        for rc in remote:
            rc.wait_recv()
        for rc in remote:
            rc.wait_send()
        for lc in copies:
            lc.wait()

    def out_shape(a):
        if a.ndim == 3:
            return jax.ShapeDtypeStruct(a.shape, a.dtype)
        return jax.ShapeDtypeStruct((N_CHIPS, a.shape[0], a.shape[1] // N_CHIPS), a.dtype)

    dma = pltpu.SemaphoreType.DMA
    return pl.pallas_call(
        body, name="scatter_to_owners", in_specs=[ANY] * n, out_specs=[ANY] * n, out_shape=[out_shape(a) for a in parts],
        scratch_shapes=[dma((3 * n,)), dma((3 * n,)), dma((n,))],
        compiler_params=pltpu.CompilerParams(has_side_effects=True),
    )(*parts)


def _sum_chips(stack, name):
    _, r, c = stack.shape
    tr = 256

    def body(s_ref, o_ref):
        total = s_ref[0].astype(F32)
        for k in range(1, N_CHIPS):
            total = total + s_ref[k].astype(F32)
        o_ref[...] = total

    return pl.pallas_call(
        body, name=name, grid=(r // tr,), in_specs=[pl.BlockSpec((N_CHIPS, tr, c), lambda i: (0, i, 0))],
        out_specs=pl.BlockSpec((tr, c), lambda i: (i, 0)), out_shape=jax.ShapeDtypeStruct((r, c), F32),
        compiler_params=_params("parallel"))(stack)


def _share_halves(halves):
    n = len(halves)

    def body(*refs):
        src, dst = refs[:n], refs[n:2 * n]
        send_sem, recv_sem, local_sem = refs[2 * n:]
        x, y, c = _place()
        copies = []
        for i in range(n):
            rows = dst[i].shape[0]
            lc = pltpu.make_async_copy(src[i], _half_rows(dst[i], c, rows), local_sem.at[i])
            lc.start()
            rc = pltpu.make_async_remote_copy(src_ref=src[i], dst_ref=_half_rows(dst[i], c, rows), send_sem=send_sem.at[i],
                                              recv_sem=recv_sem.at[i], device_id=(x, y, 1 - c), device_id_type=MESH)
            rc.start()
            copies.append((lc, rc))
        for lc, rc in copies:
            rc.wait_recv()
            rc.wait_send()
            lc.wait()

    dma = pltpu.SemaphoreType.DMA
    return pl.pallas_call(
        body, name="share_halves", in_specs=[ANY] * n, out_specs=[ANY] * n,
        out_shape=[jax.ShapeDtypeStruct((2 * a.shape[0], a.shape[1]), a.dtype) for a in halves],
        scratch_shapes=[dma((n,)), dma((n,)), dma((n,))],
        compiler_params=pltpu.CompilerParams(has_side_effects=True),
    )(*halves)


MM = dict(tm=1024, tn=1024, tk=2048)


def _local_step(x, target, ev_norm_w, w_in0, conv_w, w_out0, od_norm_w, w_in1, q_norm_w, k_norm_w, sinks, w_out1):
    s = x.shape[0]
    cos_r, sin_r = _rope_tables(s, RET_DIM)
    cos_a, sin_a = _rope_tables(s, HEAD_DIM)
    cos_a = jnp.tile(cos_a, (1, 4))
    sins_a = jnp.tile(jnp.concatenate([-sin_a, sin_a], axis=1), (1, 2))
    tables = _retention_tables()
    ones_q, ones_k, ones_kv = _block_diag_mean(GROUP_WIDTH), _block_diag_mean(SLAB), _block_diag_mean(KV_WIDTH)
    qw_g = jnp.tile(q_norm_w, (1, GROUP_WIDTH // HEAD_DIM))
    kw_s = jnp.tile(k_norm_w, (1, SLAB // HEAD_DIM))
    kw_kv = jnp.tile(k_norm_w, (1, KV_WIDTH // HEAD_DIM))
    sinks1 = sinks.reshape(Q_HEADS)

    h0 = _rmsnorm(x, ev_norm_w, "norm0")
    proj0 = _mm(h0, w_in0, mode="nn", out_dtype=BF16, name="proj0", **MM)
    o_ret, cat, states = _retention_fwd(proj0, cos_r, sin_r, tables)
    cat = _conv_fwd(proj0, conv_w, cat)
    x1 = _mm(cat, w_out0, mode="nn", out_dtype=F32, name="out0", add=x, **MM)
    h1 = _rmsnorm(x1, od_norm_w, "norm1")
    proj1 = _mm(h1, w_in1, mode="nn", out_dtype=BF16, name="proj1", tm=1024, tn=1152, tk=2048)
    ag, o_att = _swa_fwd(proj1, sinks1, qw_g, kw_s, cos_a, sins_a, ones_q, ones_k)
    x2 = _mm(ag, w_out1, mode="nn", out_dtype=F32, name="out1", add=x1, **MM)
    dy, dy_b, sq = _loss_grad(x2, target)

    g_w_out1 = _mm(ag, dy_b, mode="tn", out_dtype=BF16, name="g_w_out1", tm=1024, tn=1024, tk=1024)
    dag = _mm(dy_b, w_out1, mode="nt", out_dtype=BF16, name="d_ag", **MM)
    dproj1, dgate, dkc, dkp, dvc, dvp, dqw, dsink = _swa_bwd(proj1, o_att, dag, sinks1, qw_g, kw_s, cos_a, sins_a, ones_q, ones_k)
    dproj1, dkw = _swa_bwd_kv(proj1, dproj1, dgate, dkc, dkp, dvc, dvp, kw_kv, cos_a, sins_a, ones_kv)
    g_w_in1 = _mm(h1, dproj1, mode="tn", out_dtype=BF16, name="g_w_in1", tm=1024, tn=1152, tk=1024)
    dh1 = _mm(dproj1, w_in1, mode="nt", out_dtype=BF16, name="d_h1", tm=1024, tn=1024, tk=1152)
    dx1, dx1_b, g_norm1 = _rmsnorm_bwd(x1, od_norm_w, dh1, dy, "norm1_bwd", True)

    g_w_out0 = _mm(cat, dx1_b, mode="tn", out_dtype=BF16, name="g_w_out0", tm=1024, tn=1024, tk=1024)
    dcat = _mm(dx1_b, w_out0, mode="nt", out_dtype=BF16, name="d_cat", **MM)
    dproj0 = _retention_bwd(proj0, o_ret, states, dcat, cos_r, sin_r, tables)
    dproj0, g_conv = _conv_bwd(proj0, dcat, conv_w, dproj0)
    g_w_in0 = _mm(h0, dproj0, mode="tn", out_dtype=BF16, name="g_w_in0", tm=1024, tn=1024, tk=1024)
    dh0 = _mm(dproj0, w_in0, mode="nt", out_dtype=BF16, name="d_h0", **MM)
    grad_x, g_norm0 = _rmsnorm_bwd(x, ev_norm_w, dh0, dx1, "norm0_bwd", False)

    g_qw = dqw[:, 0, :].reshape(Q_HEADS, HEAD_DIM).sum(axis=0)
    g_kw = dkw[0].reshape(KV_HEADS, HEAD_DIM).sum(axis=0)
    g_sinks = dsink[:, :, 0].reshape(Q_HEADS)
    small = dict(ev_norm=g_norm0[0], od_norm=g_norm1[0], conv=g_conv[:3], qw=g_qw, kw=g_kw, sinks=g_sinks)
    return sq[0, 0], grad_x, (g_w_in0, g_w_out0, g_w_in1, g_w_out1), small


def _pack_small_grads(small):
    pad = lambda v: jnp.pad(v, (0, D_MODEL - v.shape[0]))
    tail = pad(jnp.concatenate([small["qw"], small["kw"], small["sinks"]]))
    rows = [small["ev_norm"], small["od_norm"]] + [pad(small["conv"][t]) for t in range(3)] + [tail]
    rows += [jnp.zeros((D_MODEL,), F32)] * (8 - len(rows))
    return jnp.stack(rows)


def _reduce_scatter(grads):
    g_w_in0, g_w_out0, g_w_in1, g_w_out1 = grads
    as_rows = lambda g: g.reshape(N_CHIPS, g.shape[0] // N_CHIPS, g.shape[1])
    mine, theirs = _exchange_halves([g_w_in0, as_rows(g_w_out0), g_w_in1, as_rows(g_w_out1)])
    names = ("w_in0", "w_out0", "w_in1", "w_out1")
    parts = []
    for a, b, nm in zip(mine, theirs, names):
        flat = (lambda t: t.reshape(-1, t.shape[-1])) if a.ndim == 3 else (lambda t: t)
        parts.append(_add_bf16(flat(a), flat(b), "pair_sum_" + nm).reshape(a.shape))
    stacks = _scatter_to_owners(parts)
    halves = [_sum_chips(st, "chip_sum_" + nm) for st, nm in zip(stacks, names)]
    return _share_halves(halves)


def kernel(x, ev_norm_w, ev_w_in, ev_conv_w, ev_w_out, od_norm_w, od_w_in, od_q_norm_w, od_k_norm_w, od_sinks, od_w_out, loss_target, m_ev_norm_w, m_ev_w_in, m_ev_conv_w, m_ev_w_out, m_od_norm_w, m_od_w_in, m_od_q_norm_w, m_od_k_norm_w, m_od_sinks, m_od_w_out, v_ev_norm_w, v_ev_w_in, v_ev_conv_w, v_ev_w_out, v_od_norm_w, v_od_w_in, v_od_q_norm_w, v_od_k_norm_w, v_od_sinks, v_od_w_out):
    my_chip = 2 * lax.axis_index("x") + lax.axis_index("y")
    shard_w = D_MODEL // N_CHIPS
    conv_shard = CONV_WIDTH // N_CHIPS

    small_in = jnp.zeros((8, shard_w), F32)
    small_in = small_in.at[0].set(od_norm_w[0]).at[1:4, :conv_shard].set(ev_conv_w[0])
    w_in0, w_out0, w_in1, w_out1, small_all = _gather_weights(
        _cast_bf16(ev_w_in[0], "cast_w_in0"), _cast_bf16(ev_w_out[0], "cast_w_out0"),
        _cast_bf16(od_w_in[0], "cast_w_in1"), _cast_bf16(od_w_out[0], "cast_w_out1"), small_in)
    od_norm_full = small_all[:, 0, :].reshape(1, D_MODEL)
    conv_full = jnp.transpose(small_all[:, 1:4, :conv_shard], (1, 0, 2)).reshape(3, CONV_WIDTH)

    sq, grad_x, big, small = _local_step(x[0], loss_target[0], ev_norm_w, w_in0, conv_full, w_out0, od_norm_full, w_in1,
                                         od_q_norm_w, od_k_norm_w, od_sinks, w_out1)
    loss = lax.psum(0.5 * sq / D_MODEL, ("x", "y", "c"))

    g_ev_w_in, g_ev_w_out, g_od_w_in, g_od_w_out = _reduce_scatter(big)
    tot = _allreduce_small(_pack_small_grads(small))
    g_ev_norm = tot[0:1]
    g_od_norm = lax.dynamic_slice(tot, (1, my_chip * shard_w), (1, shard_w))
    g_conv = lax.dynamic_slice(tot, (2, my_chip * conv_shard), (3, conv_shard))
    g_qw, g_kw, g_sinks = tot[5:6, 0:HEAD_DIM], tot[5:6, HEAD_DIM:2 * HEAD_DIM], tot[5:6, 2 * HEAD_DIM:2 * HEAD_DIM + Q_HEADS]

    upd = {}
    upd["ev_w_in"] = _adamw(ev_w_in[0], g_ev_w_in, m_ev_w_in[0], v_ev_w_in[0], "adamw_ev_w_in")
    upd["ev_w_out"] = _adamw(ev_w_out[0], g_ev_w_out, m_ev_w_out[0], v_ev_w_out[0], "adamw_ev_w_out")
    upd["od_w_in"] = _adamw(od_w_in[0], g_od_w_in, m_od_w_in[0], v_od_w_in[0], "adamw_od_w_in")
    upd["od_w_out"] = _adamw(od_w_out[0], g_od_w_out, m_od_w_out[0], v_od_w_out[0], "adamw_od_w_out")
    smalls = (("ev_norm_w", ev_norm_w, g_ev_norm, m_ev_norm_w, v_ev_norm_w),
              ("ev_conv_w", ev_conv_w, g_conv, m_ev_conv_w, v_ev_conv_w),
              ("od_norm_w", od_norm_w, g_od_norm, m_od_norm_w, v_od_norm_w),
              ("od_q_norm_w", od_q_norm_w, g_qw, m_od_q_norm_w, v_od_q_norm_w),
              ("od_k_norm_w", od_k_norm_w, g_kw, m_od_k_norm_w, v_od_k_norm_w),
              ("od_sinks", od_sinks, g_sinks, m_od_sinks, v_od_sinks))
    sizes = [w.size for _, w, _, _, _ in smalls]
    padded = 8 * 128 * math.ceil(sum(sizes) / (8 * 128))
    pack = lambda arrs, fill: jnp.concatenate(
        [a.reshape(-1) for a in arrs] + [jnp.full((padded - sum(sizes),), fill, F32)]).reshape(8, padded // 8)
    packed = _adamw(pack([w for _, w, _, _, _ in smalls], 0.0), pack([g for _, _, g, _, _ in smalls], 0.0),
                    pack([m for _, _, _, m, _ in smalls], 0.0), pack([v for _, _, _, _, v in smalls], 1.0), "adamw_small")
    offs = [sum(sizes[:i]) for i in range(len(sizes))]
    grads = {"ev_w_in": g_ev_w_in[None], "ev_w_out": g_ev_w_out[None], "od_w_in": g_od_w_in[None], "od_w_out": g_od_w_out[None]}
    for (nm, w, g, _, _), off, size in zip(smalls, offs, sizes):
        upd[nm] = tuple(p.reshape(-1)[off:off + size].reshape(w.shape) for p in packed)
        grads[nm] = g.reshape(w.shape)
    for nm in ("ev_w_in", "ev_w_out", "od_w_in", "od_w_out"):
        upd[nm] = tuple(u[None] for u in upd[nm])
    order = ("ev_norm_w", "ev_w_in", "ev_conv_w", "ev_w_out", "od_norm_w", "od_w_in", "od_q_norm_w", "od_k_norm_w", "od_sinks", "od_w_out")
    return (loss, grad_x[None], *[grads[nm] for nm in order], *[upd[nm][0] for nm in order],
            *[upd[nm][1] for nm in order], *[upd[nm][2] for nm in order])
```

```python
import functools
import math

import jax
import jax.numpy as jnp
from jax import lax
from jax.experimental import pallas as pl
from jax.experimental.pallas import tpu as pltpu

F32 = jnp.float32
BF16 = jnp.bfloat16

D_MODEL = 2048
RET_HEADS = 4
RET_DIM = 256
RET_WIDTH = 1024
CONV_WIDTH = 1024
EVEN_IN = 8192
Q_HEADS = 32
HEAD_DIM = 64
KV_HEADS = 4
KV_WIDTH = 256
ATTN_WIDTH = 2048
ODD_IN = 4608
BLK = 128
ROPE_THETA = 10000.0
EPS = 1e-6
ADAM_LR = 0.001
ADAM_B1 = 0.9
ADAM_B2 = 0.999
ADAM_EPS = 1e-08
ADAM_WD = 0.01
ADAM_STEP = 10
N_CHIPS = 4
N_DEV = 8
VMEM_LIMIT_BYTES = 48 * 1024 * 1024
MESH = pl.DeviceIdType.MESH
ANY = pl.BlockSpec(memory_space=pl.ANY)


def _params(*sem):
    return pltpu.CompilerParams(dimension_semantics=sem, vmem_limit_bytes=VMEM_LIMIT_BYTES)


def _dot(a, b):
    return jnp.dot(a, b, preferred_element_type=F32)


def _dot_nt(a, b):
    return lax.dot_general(a, b, (((1,), (1,)), ((), ())), preferred_element_type=F32)


def _dot_tn(a, b):
    return lax.dot_general(a, b, (((0,), (0,)), ((), ())), preferred_element_type=F32)


def _sigmoid(x):
    return 1.0 / (1.0 + jnp.exp(-x))


def _mm(a, b, *, mode, tm, tn, tk, out_dtype, name, add=None):
    if mode == "nn":
        (m, k), n = a.shape, b.shape[1]
    elif mode == "nt":
        (m, k), n = a.shape, b.shape[0]
    else:
        (k, m), n = a.shape, b.shape[1]
    tm, tn, tk = min(tm, m), min(tn, n), min(tk, k)
    assert m % tm == 0 and n % tn == 0 and k % tk == 0, (name, m, n, k)
    nk = k // tk
    dot = {"nn": _dot, "nt": _dot_nt, "tn": _dot_tn}[mode]
    a_spec = (pl.BlockSpec((tk, tm), lambda i, j, kk: (kk, i)) if mode == "tn"
              else pl.BlockSpec((tm, tk), lambda i, j, kk: (i, kk)))
    b_spec = (pl.BlockSpec((tn, tk), lambda i, j, kk: (j, kk)) if mode == "nt"
              else pl.BlockSpec((tk, tn), lambda i, j, kk: (kk, j)))
    o_spec = pl.BlockSpec((tm, tn), lambda i, j, kk: (i, j))
    has_add = add is not None

    def body(*refs):
        a_ref, b_ref = refs[0], refs[1]
        add_ref = refs[2] if has_add else None
        o_ref = refs[3] if has_add else refs[2]
        acc_ref = refs[-1]
        p = dot(a_ref[...], b_ref[...])

        def finish(total):
            if has_add:
                total = total + add_ref[...].astype(F32)
            o_ref[...] = total.astype(out_dtype)

        if nk == 1:
            finish(p)
        else:
            kk = pl.program_id(2)

            @pl.when(kk == 0)
            def _():
                acc_ref[...] = p

            @pl.when(jnp.logical_and(kk > 0, kk < nk - 1))
            def _():
                acc_ref[...] += p

            @pl.when(kk == nk - 1)
            def _():
                finish(acc_ref[...] + p)

    in_specs = [a_spec, b_spec] + ([o_spec] if has_add else [])
    args = (a, b) + ((add,) if has_add else ())
    return pl.pallas_call(
        body, name=name, grid=(m // tm, n // tn, nk), in_specs=in_specs, out_specs=o_spec,
        out_shape=jax.ShapeDtypeStruct((m, n), out_dtype),
        scratch_shapes=[pltpu.VMEM((tm, tn) if nk > 1 else (8, 128), F32)],
        compiler_params=_params("parallel", "parallel", "arbitrary"),
    )(*args)


def _cast_bf16(w, name):
    r, c = w.shape
    tr = min(r, 512)

    def body(w_ref, o_ref):
        o_ref[...] = w_ref[...].astype(BF16)

    spec = pl.BlockSpec((tr, c), lambda i: (i, 0))
    return pl.pallas_call(body, name=name, grid=(r // tr,), in_specs=[spec], out_specs=spec,
                          out_shape=jax.ShapeDtypeStruct((r, c), BF16), compiler_params=_params("parallel"))(w)


def _rmsnorm(x, w, name):
    s, d = x.shape
    tr = 256

    def body(x_ref, w_ref, h_ref):
        xv = x_ref[...]
        rstd = lax.rsqrt(jnp.mean(xv * xv, axis=-1, keepdims=True) + EPS)
        h_ref[...] = (xv * rstd * w_ref[...]).astype(BF16)

    return pl.pallas_call(
        body, name=name, grid=(s // tr,),
        in_specs=[pl.BlockSpec((tr, d), lambda i: (i, 0)), pl.BlockSpec((1, d), lambda i: (0, 0))],
        out_specs=pl.BlockSpec((tr, d), lambda i: (i, 0)),
        out_shape=jax.ShapeDtypeStruct((s, d), BF16), compiler_params=_params("parallel"))(x, w)


def _rmsnorm_bwd(x, w, dh, dres, name, with_bf16):
    s, d = x.shape
    tr = 256

    def body(x_ref, w_ref, dh_ref, dres_ref, *outs):
        dx_ref, dw_ref = outs[0], outs[-1]
        xv = x_ref[...]
        rstd = lax.rsqrt(jnp.mean(xv * xv, axis=-1, keepdims=True) + EPS)
        nrm = xv * rstd
        dhv = dh_ref[...].astype(F32)
        dn = dhv * w_ref[...]
        dx = dres_ref[...] + rstd * (dn - nrm * jnp.mean(dn * nrm, axis=-1, keepdims=True))
        dx_ref[...] = dx
        if with_bf16:
            outs[1][...] = dx.astype(BF16)

        @pl.when(pl.program_id(0) == 0)
        def _():
            dw_ref[...] = jnp.zeros_like(dw_ref)

        dw_ref[...] += jnp.sum(dhv * nrm, axis=0, keepdims=True)

    row = pl.BlockSpec((tr, d), lambda i: (i, 0))
    out_shape = [jax.ShapeDtypeStruct((s, d), F32)] + ([jax.ShapeDtypeStruct((s, d), BF16)] if with_bf16 else [])
    out_shape.append(jax.ShapeDtypeStruct((8, d), F32))
    out_specs = [row] * (2 if with_bf16 else 1) + [pl.BlockSpec((8, d), lambda i: (0, 0))]
    return pl.pallas_call(
        body, name=name, grid=(s // tr,),
        in_specs=[row, pl.BlockSpec((1, d), lambda i: (0, 0)), row, row],
        out_specs=out_specs, out_shape=out_shape, compiler_params=_params("arbitrary"))(x, w, dh, dres)


def _loss_grad(y, target):
    s, d = y.shape
    tr = 256

    def body(y_ref, t_ref, dy_ref, dyb_ref, l_ref):
        diff = y_ref[...] - t_ref[...]
        dy = diff * (1.0 / d)
        dy_ref[...] = dy
        dyb_ref[...] = dy.astype(BF16)

        @pl.when(pl.program_id(0) == 0)
        def _():
            l_ref[...] = jnp.zeros_like(l_ref)

        l_ref[...] += jnp.sum(jnp.sum(diff * diff, axis=1, keepdims=True), axis=0, keepdims=True)

    row = pl.BlockSpec((tr, d), lambda i: (i, 0))
    return pl.pallas_call(
        body, name="loss_grad", grid=(s // tr,), in_specs=[row, row],
        out_specs=[row, row, pl.BlockSpec((8, 128), lambda i: (0, 0))],
        out_shape=[jax.ShapeDtypeStruct((s, d), F32), jax.ShapeDtypeStruct((s, d), BF16),
                   jax.ShapeDtypeStruct((8, 128), F32)],
        compiler_params=_params("arbitrary"))(y, target)


def _adamw(w, g, m, v, name):
    r, c = w.shape
    tr = min(r, 256)
    assert r % tr == 0

    def body(w_ref, g_ref, m_ref, v_ref, d_ref, nm_ref, nv_ref):
        gv = g_ref[...]
        nm = ADAM_B1 * m_ref[...] + (1.0 - ADAM_B1) * gv
        nv = ADAM_B2 * v_ref[...] + (1.0 - ADAM_B2) * (gv * gv)
        m_hat = nm / (1.0 - ADAM_B1 ** ADAM_STEP)
        v_hat = nv / (1.0 - ADAM_B2 ** ADAM_STEP)
        d_ref[...] = -ADAM_LR * (m_hat / (jnp.sqrt(v_hat) + ADAM_EPS) + ADAM_WD * w_ref[...])
        nm_ref[...] = nm
        nv_ref[...] = nv

    spec = pl.BlockSpec((tr, c), lambda i: (i, 0))
    shp = jax.ShapeDtypeStruct((r, c), F32)
    return pl.pallas_call(body, name=name, grid=(r // tr,), in_specs=[spec] * 4, out_specs=[spec] * 3,
                          out_shape=[shp] * 3, compiler_params=_params("parallel"))(w, g, m, v)


def _rope_tables(s, dim):
    inv = 1.0 / (ROPE_THETA ** (jnp.arange(0, dim, 2, dtype=F32) / dim))
    ang = jnp.arange(s).astype(F32)[:, None] * inv[None, :]
    return jnp.cos(ang), jnp.sin(ang)


def _rope_half(x, cos, sin):
    h = x.shape[1] // 2
    x1, x2 = x[:, :h], x[:, h:]
    return jnp.concatenate([x1 * cos - x2 * sin, x2 * cos + x1 * sin], axis=1)


def _unrope_half(dy, cos, sin):
    h = dy.shape[1] // 2
    d1, d2 = dy[:, :h], dy[:, h:]
    return jnp.concatenate([d1 * cos + d2 * sin, d2 * cos - d1 * sin], axis=1)


def _lane(shape):
    return lax.broadcasted_iota(jnp.int32, shape, 1)


def _partner64(x):
    w = x.shape[1]
    first = (_lane(x.shape) % HEAD_DIM) < (HEAD_DIM // 2)
    return jnp.where(first, pltpu.roll(x, w - HEAD_DIM // 2, 1), pltpu.roll(x, HEAD_DIM // 2, 1))


def _tile_lanes(t, reps):
    return t if reps == 1 else jnp.concatenate([t] * reps, axis=1)


def _group_mean(x, ones_bd):
    hi = x.astype(BF16)
    lo = (x - hi.astype(F32)).astype(BF16)
    return _dot(hi, ones_bd) + _dot(lo, ones_bd)


def _block_diag_mean(width):
    idx = jnp.arange(width) // HEAD_DIM
    return jnp.where(idx[:, None] == idx[None, :], 1.0 / HEAD_DIM, 0.0).astype(BF16)


def _retention_tables():
    h = RET_HEADS
    log_g = jnp.log(1.0 - 2.0 ** (-5.0 - jnp.arange(h, dtype=F32)))
    idx = jnp.arange(BLK, dtype=F32)
    diff = idx[:, None] - idx[None, :]
    intra = jnp.where(diff >= 0, jnp.exp(log_g[:, None, None] * jnp.maximum(diff, 0.0)), 0.0).astype(F32)
    q_dec = jnp.exp(log_g[:, None] * (idx[None, :] + 1.0)).astype(F32)[:, :, None]
    k_dec = jnp.exp(log_g[:, None] * (BLK - 1.0 - idx[None, :])).astype(F32)[:, :, None]
    chunk_dec = jnp.exp(log_g * BLK).astype(F32)[:, None, None]
    return intra, q_dec, k_dec, chunk_dec


def _retention_fwd(proj, cos, sin, tables):
    s = proj.shape[0]
    nc = s // BLK
    intra, q_dec, k_dec, chunk_dec = tables

    def body(p_ref, cos_ref, sin_ref, in_ref, qd_ref, kd_ref, cd_ref, o_ref, cat_ref, st_ref, state):
        @pl.when(pl.program_id(0) == 0)
        def _():
            state[...] = jnp.zeros_like(state)

        cosv, sinv = cos_ref[...], sin_ref[...]
        for h in range(RET_HEADS):
            c0 = h * RET_DIM
            q = p_ref[:, c0:c0 + RET_DIM].astype(F32)
            k = p_ref[:, RET_WIDTH + c0:RET_WIDTH + c0 + RET_DIM].astype(F32)
            v = p_ref[:, 2 * RET_WIDTH + c0:2 * RET_WIDTH + c0 + RET_DIM]
            g = p_ref[:, 3 * RET_WIDTH + c0:3 * RET_WIDTH + c0 + RET_DIM].astype(F32)
            qb = _rope_half(q, cosv, sinv).astype(BF16)
            kr = _rope_half(k, cosv, sinv) * (RET_DIM ** -0.5)
            kb = kr.astype(BF16)
            scores = _dot_nt(qb, kb) * in_ref[h]
            inner = _dot(scores.astype(BF16), v)
            prev = state[h]
            prev_b = prev.astype(BF16)
            st_ref[h, 0] = prev_b
            o = inner + _dot(qb, prev_b) * qd_ref[h]
            o_ref[:, c0:c0 + RET_DIM] = o
            rstd = lax.rsqrt(jnp.mean(o * o, axis=-1, keepdims=True) + EPS)
            cat_ref[:, c0:c0 + RET_DIM] = (o * rstd * (g * _sigmoid(g))).astype(BF16)
            state[h] = cd_ref[h] * prev + _dot_tn((kr * kd_ref[h]).astype(BF16), v)

    full = lambda shape: pl.BlockSpec(shape, lambda n: (0,) * len(shape))
    return pl.pallas_call(
        body, name="retention_fwd", grid=(nc,),
        in_specs=[pl.BlockSpec((BLK, 4 * RET_WIDTH), lambda n: (n, 0)),
                  pl.BlockSpec((BLK, RET_DIM // 2), lambda n: (n, 0)), pl.BlockSpec((BLK, RET_DIM // 2), lambda n: (n, 0)),
                  full((RET_HEADS, BLK, BLK)), full((RET_HEADS, BLK, 1)), full((RET_HEADS, BLK, 1)), full((RET_HEADS, 1, 1))],
        out_specs=[pl.BlockSpec((BLK, RET_WIDTH), lambda n: (n, 0)), pl.BlockSpec((BLK, RET_WIDTH), lambda n: (n, 0)),
                   pl.BlockSpec((RET_HEADS, 1, RET_DIM, RET_DIM), lambda n: (0, n, 0, 0))],
        out_shape=[jax.ShapeDtypeStruct((s, RET_WIDTH), F32), jax.ShapeDtypeStruct((s, D_MODEL), BF16),
                   jax.ShapeDtypeStruct((RET_HEADS, nc, RET_DIM, RET_DIM), BF16)],
        scratch_shapes=[pltpu.VMEM((RET_HEADS, RET_DIM, RET_DIM), F32)],
        compiler_params=_params("arbitrary"))(proj, cos, sin, intra, q_dec, k_dec, chunk_dec)


def _retention_bwd(proj, o, states, dcat, cos, sin, tables):
    s = proj.shape[0]
    nc = s // BLK
    intra, q_dec, k_dec, chunk_dec = tables

    def body(p_ref, o_ref, st_ref, dc_ref, cos_ref, sin_ref, in_ref, qd_ref, kd_ref, cd_ref, dp_ref, dstate):
        @pl.when(pl.program_id(0) == 0)
        def _():
            dstate[...] = jnp.zeros_like(dstate)

        cosv, sinv = cos_ref[...], sin_ref[...]
        for h in range(RET_HEADS):
            c0 = h * RET_DIM
            q = p_ref[:, c0:c0 + RET_DIM].astype(F32)
            k = p_ref[:, RET_WIDTH + c0:RET_WIDTH + c0 + RET_DIM].astype(F32)
            v = p_ref[:, 2 * RET_WIDTH + c0:2 * RET_WIDTH + c0 + RET_DIM]
            g = p_ref[:, 3 * RET_WIDTH + c0:3 * RET_WIDTH + c0 + RET_DIM].astype(F32)
            o = o_ref[:, c0:c0 + RET_DIM]
            dc = dc_ref[:, c0:c0 + RET_DIM].astype(F32)
            rstd = lax.rsqrt(jnp.mean(o * o, axis=-1, keepdims=True) + EPS)
            nrm = o * rstd
            sg = _sigmoid(g)
            dg = dc * nrm * (sg * (1.0 + g * (1.0 - sg)))
            dn = dc * (g * sg)
            do = rstd * (dn - nrm * jnp.mean(dn * nrm, axis=-1, keepdims=True))
            qb = _rope_half(q, cosv, sinv).astype(BF16)
            kr = _rope_half(k, cosv, sinv) * (RET_DIM ** -0.5)
            kb = kr.astype(BF16)
            mask = in_ref[h]
            qd, kd = qd_ref[h], kd_ref[h]
            prev_b = st_ref[h, 0]
            dnext = dstate[h]
            dnext_b = dnext.astype(BF16)
            att = (_dot_nt(qb, kb) * mask).astype(BF16)
            do_b = do.astype(BF16)
            doq = (do * qd).astype(BF16)
            dv = _dot_tn(att, do_b) + _dot((kr * kd).astype(BF16), dnext_b)
            ds = (_dot_nt(do_b, v) * mask).astype(BF16)
            dqr = _dot(ds, kb) + _dot_nt(doq, prev_b)
            dkr = _dot_tn(ds, qb) + _dot_nt(v, dnext_b) * kd
            dstate[h] = cd_ref[h] * dnext + _dot_tn(qb, doq)
            dq = _unrope_half(dqr, cosv, sinv)
            dk = _unrope_half(dkr * (RET_DIM ** -0.5), cosv, sinv)
            dp_ref[:, c0:c0 + RET_DIM] = dq.astype(BF16)
            dp_ref[:, RET_WIDTH + c0:RET_WIDTH + c0 + RET_DIM] = dk.astype(BF16)
            dp_ref[:, 2 * RET_WIDTH + c0:2 * RET_WIDTH + c0 + RET_DIM] = dv.astype(BF16)
            dp_ref[:, 3 * RET_WIDTH + c0:3 * RET_WIDTH + c0 + RET_DIM] = dg.astype(BF16)

    rev = lambda n: nc - 1 - n
    full = lambda shape: pl.BlockSpec(shape, lambda n: (0,) * len(shape))
    return pl.pallas_call(
        body, name="retention_bwd", grid=(nc,),
        in_specs=[pl.BlockSpec((BLK, 4 * RET_WIDTH), lambda n: (rev(n), 0)),
                  pl.BlockSpec((BLK, RET_WIDTH), lambda n: (rev(n), 0)),
                  pl.BlockSpec((RET_HEADS, 1, RET_DIM, RET_DIM), lambda n: (0, rev(n), 0, 0)),
                  pl.BlockSpec((BLK, RET_WIDTH), lambda n: (rev(n), 0)),
                  pl.BlockSpec((BLK, RET_DIM // 2), lambda n: (rev(n), 0)), pl.BlockSpec((BLK, RET_DIM // 2), lambda n: (rev(n), 0)),
                  full((RET_HEADS, BLK, BLK)), full((RET_HEADS, BLK, 1)), full((RET_HEADS, BLK, 1)), full((RET_HEADS, 1, 1))],
        out_specs=pl.BlockSpec((BLK, 4 * RET_WIDTH), lambda n: (rev(n), 0)),
        out_shape=jax.ShapeDtypeStruct((s, EVEN_IN), BF16),
        scratch_shapes=[pltpu.VMEM((RET_HEADS, RET_DIM, RET_DIM), F32)],
        compiler_params=_params("arbitrary"))(proj, o, states, dcat, cos, sin, intra, q_dec, k_dec, chunk_dec)


CONV_ROWS = 256
HALO = 16


def _conv_pieces(p, halo, conv_w, first):
    rows = p.shape[0]
    gb, gc, u, gv = (p[:, i * CONV_WIDTH:(i + 1) * CONV_WIDTH] for i in range(4))
    cu = gc * u
    hcu = halo[:, CONV_WIDTH:2 * CONV_WIDTH] * halo[:, 2 * CONV_WIDTH:3 * CONV_WIDTH]
    hcu = jnp.where(first, 0.0, hcu)
    r1, r2 = hcu[HALO - 1:HALO], hcu[HALO - 2:HALO - 1]
    row = lax.broadcasted_iota(jnp.int32, cu.shape, 0)
    m1 = jnp.where(row == 0, r1, pltpu.roll(cu, 1, 0))
    m2 = jnp.where(row == 0, r2, jnp.where(row == 1, r1, pltpu.roll(cu, 2, 0)))
    del rows, conv_w
    return gb, gc, u, gv, cu, m1, m2


def _conv_fwd(proj, conv_w, cat):
    s = proj.shape[0]
    per = CONV_ROWS // HALO

    def body(p_ref, halo_ref, w_ref, cat_in, cat_ref):
        del cat_in
        first = pl.program_id(0) == 0
        gb, _, _, gv, cu, m1, m2 = _conv_pieces(p_ref[...].astype(F32), halo_ref[...].astype(F32), None, first)
        conv = w_ref[0:1, :] * m2 + w_ref[1:2, :] * m1 + w_ref[2:3, :] * cu
        cat_ref[...] = (gb * conv * (gv * _sigmoid(gv))).astype(BF16)

    return pl.pallas_call(
        body, name="conv_fwd", grid=(s // CONV_ROWS,),
        in_specs=[pl.BlockSpec((CONV_ROWS, 4 * CONV_WIDTH), lambda i: (i, 1)),
                  pl.BlockSpec((HALO, 4 * CONV_WIDTH), lambda i: (jnp.maximum(i * per - 1, 0), 1)),
                  pl.BlockSpec((3, CONV_WIDTH), lambda i: (0, 0)), ANY],
        out_specs=pl.BlockSpec((CONV_ROWS, CONV_WIDTH), lambda i: (i, 1)),
        out_shape=jax.ShapeDtypeStruct(cat.shape, cat.dtype), input_output_aliases={3: 0},
        compiler_params=_params("parallel"))(proj, proj, conv_w, cat)


def _conv_bwd(proj, dcat, conv_w, dproj):
    s = proj.shape[0]
    per = CONV_ROWS // HALO
    last_halo = s // HALO - 1
    nsteps = s // CONV_ROWS

    def body(p_ref, halo_ref, nxt_ref, dc_ref, dnxt_ref, w_ref, dp_in, dp_ref, dw_ref):
        del dp_in
        i = pl.program_id(0)
        gb, gc, u, gv, cu, m1, m2 = _conv_pieces(p_ref[...].astype(F32), halo_ref[...].astype(F32), None, i == 0)
        w0, w1, w2 = w_ref[0:1, :], w_ref[1:2, :], w_ref[2:3, :]
        conv = w0 * m2 + w1 * m1 + w2 * cu
        dco = dc_ref[...].astype(F32)
        sg = _sigmoid(gv)
        silu = gv * sg
        dgb = dco * conv * silu
        dgv = dco * gb * conv * (sg * (1.0 + gv * (1.0 - sg)))
        dconv = dco * gb * silu
        nxt = nxt_ref[...].astype(F32)
        ngv = nxt[:, 3 * CONV_WIDTH:]
        dnext = dnxt_ref[...].astype(F32) * nxt[:, :CONV_WIDTH] * (ngv * _sigmoid(ngv))
        dnext = jnp.where(i == nsteps - 1, 0.0, dnext)
        n1, n2 = dnext[0:1], dnext[1:2]
        row = lax.broadcasted_iota(jnp.int32, dconv.shape, 0)
        p1 = jnp.where(row == CONV_ROWS - 1, n1, pltpu.roll(dconv, CONV_ROWS - 1, 0))
        p2 = jnp.where(row == CONV_ROWS - 1, n2, jnp.where(row == CONV_ROWS - 2, n1, pltpu.roll(dconv, CONV_ROWS - 2, 0)))
        dcu = w2 * dconv + w1 * p1 + w0 * p2
        dp_ref[...] = jnp.concatenate([dgb, dcu * u, dcu * gc, dgv], axis=1).astype(BF16)

        @pl.when(i == 0)
        def _():
            dw_ref[...] = jnp.zeros_like(dw_ref)

        taps = [jnp.sum(dconv * m, axis=0, keepdims=True) for m in (m2, m1, cu)]
        r8 = lax.broadcasted_iota(jnp.int32, dw_ref.shape, 0)
        dw_ref[...] += jnp.where(r8 == 0, taps[0], jnp.where(r8 == 1, taps[1], jnp.where(r8 == 2, taps[2], 0.0)))

    return pl.pallas_call(
        body, name="conv_bwd", grid=(nsteps,),
        in_specs=[pl.BlockSpec((CONV_ROWS, 4 * CONV_WIDTH), lambda i: (i, 1)),
                  pl.BlockSpec((HALO, 4 * CONV_WIDTH), lambda i: (jnp.maximum(i * per - 1, 0), 1)),
                  pl.BlockSpec((HALO, 4 * CONV_WIDTH), lambda i: (jnp.minimum((i + 1) * per, last_halo), 1)),
                  pl.BlockSpec((CONV_ROWS, CONV_WIDTH), lambda i: (i, 1)),
                  pl.BlockSpec((HALO, CONV_WIDTH), lambda i: (jnp.minimum((i + 1) * per, last_halo), 1)),
                  pl.BlockSpec((3, CONV_WIDTH), lambda i: (0, 0)), ANY],
        out_specs=[pl.BlockSpec((CONV_ROWS, 4 * CONV_WIDTH), lambda i: (i, 1)), pl.BlockSpec((8, CONV_WIDTH), lambda i: (0, 0))],
        out_shape=[jax.ShapeDtypeStruct(dproj.shape, dproj.dtype), jax.ShapeDtypeStruct((8, CONV_WIDTH), F32)],
        input_output_aliases={6: 0},
        compiler_params=_params("arbitrary"))(proj, proj, proj, dcat, dcat, conv_w, dproj)


GROUP_WIDTH = 8 * HEAD_DIM
GROUP_HEADS = 8
SLAB = 128
Q_COL = 0
KV_COL = 4
GATE_COL = 5
K_SLAB0 = ATTN_WIDTH // SLAB
V_SLAB0 = (ATTN_WIDTH + KV_WIDTH) // SLAB
KV_ROWS = 512


def _half_mask(shape, which):
    return (_lane(shape) // HEAD_DIM) == which


def _dup_head(slab, which):
    kept = jnp.where(_half_mask(slab.shape, which), slab, 0.0)
    return kept + pltpu.roll(kept, HEAD_DIM, 1)


def _stack_heads(x):
    parts = []
    for sl in range(GROUP_WIDTH // SLAB):
        slab = x[:, sl * SLAB:(sl + 1) * SLAB]
        parts += [jnp.where(_half_mask(slab.shape, e), slab, 0.0) for e in range(2)]
    return jnp.concatenate(parts, axis=0)


def _unstack_heads(y):
    slabs = []
    for sl in range(GROUP_WIDTH // SLAB):
        a, b = y[(2 * sl) * BLK:(2 * sl + 1) * BLK], y[(2 * sl + 1) * BLK:(2 * sl + 2) * BLK]
        slabs.append(jnp.where(_half_mask(a.shape, 0), a, b))
    return jnp.concatenate(slabs, axis=1)


def _q_prep(q, qw, cosf, sins, ones_bd):
    rstd = lax.rsqrt(_group_mean(q * q, ones_bd) + EPS)
    nrm = q * rstd
    y = nrm * qw
    return nrm, rstd, y * cosf + _partner64(y) * sins


def _attn_probs(q_stack, kcat, sink_col, n):
    rows = q_stack.shape[0]
    sc = _dot_nt(q_stack, kcat) * (HEAD_DIM ** -0.5)
    qi = (lax.broadcasted_iota(jnp.int32, (rows, 2 * BLK), 0) % BLK) + BLK
    kj = lax.broadcasted_iota(jnp.int32, (rows, 2 * BLK), 1)
    valid = jnp.logical_and(jnp.logical_and(kj <= qi, qi - kj < BLK), jnp.logical_or(n > 0, kj >= BLK))
    sc = jnp.where(valid, sc, -1e30)
    m = jnp.maximum(jnp.max(sc, axis=-1, keepdims=True), sink_col)
    p = jnp.exp(sc - m)
    psink = jnp.exp(sink_col - m)
    inv = 1.0 / (jnp.sum(p, axis=-1, keepdims=True) + psink)
    return p * inv, psink * inv


def _sink_column(sink_ref, g):
    return jnp.concatenate([jnp.full((BLK, 1), sink_ref[g * GROUP_HEADS + j], F32) for j in range(GROUP_HEADS)], axis=0)


def _keys_values(kc_ref, kp_ref, vc_ref, vp_ref, kw, ones_k, cosc, sinc, cosp, sinp, which):
    def key(ref, cosf, sins):
        k = ref[...].astype(F32)
        y = k * lax.rsqrt(_group_mean(k * k, ones_k) + EPS) * kw
        return _dup_head(y * cosf + _partner64(y) * sins, which)

    kcat = jnp.concatenate([key(kp_ref, cosp, sinp), key(kc_ref, cosc, sinc)], axis=0).astype(BF16)
    vcat = jnp.concatenate([_dup_head(vp_ref[...].astype(F32), which), _dup_head(vc_ref[...].astype(F32), which)],
                           axis=0).astype(BF16)
    return kcat, vcat


def _swa_specs():
    prev = lambda n: jnp.maximum(n - 1, 0)
    slab = lambda col0, row: pl.BlockSpec((BLK, SLAB), lambda g, n: (row(n), col0 + g // 2))
    cur = lambda n: n
    tab = lambda row: pl.BlockSpec((BLK, SLAB), lambda g, n: (row(n), 0))
    full = lambda shape: pl.BlockSpec(shape, lambda g, n: (0,) * len(shape))
    return dict(
        sinks=pl.BlockSpec(memory_space=pltpu.SMEM),
        q=pl.BlockSpec((BLK, GROUP_WIDTH), lambda g, n: (n, Q_COL + g)),
        gate=pl.BlockSpec((BLK, GROUP_WIDTH), lambda g, n: (n, GATE_COL + g)),
        kc=slab(K_SLAB0, cur), kp=slab(K_SLAB0, prev), vc=slab(V_SLAB0, cur), vp=slab(V_SLAB0, prev),
        tab_c=tab(cur), tab_p=tab(prev),
        qw=full((1, GROUP_WIDTH)), kw=full((1, SLAB)), ones_q=full((GROUP_WIDTH, GROUP_WIDTH)), ones_k=full((SLAB, SLAB)),
        group=pl.BlockSpec((BLK, GROUP_WIDTH), lambda g, n: (n, g)))


def _swa_fwd(proj, sinks, qw, kw, cos, sins, ones_q, ones_k):
    s = proj.shape[0]
    nb = s // BLK
    sp = _swa_specs()

    def body(sink_ref, q_ref, kc_ref, kp_ref, vc_ref, vp_ref, gate_ref, cosc_ref, sinc_ref, cosp_ref, sinp_ref,
             qw_ref, kw_ref, oq_ref, ok_ref, ag_ref, o_ref):
        g, n = pl.program_id(0), pl.program_id(1)
        cosc, sinc = cosc_ref[...], sinc_ref[...]
        kcat, vcat = _keys_values(kc_ref, kp_ref, vc_ref, vp_ref, kw_ref[...], ok_ref[...], cosc, sinc,
                                  cosp_ref[...], sinp_ref[...], g % 2)
        _, _, qr = _q_prep(q_ref[...].astype(F32), qw_ref[...], _tile_lanes(cosc, 4), _tile_lanes(sinc, 4), oq_ref[...])
        p, _ = _attn_probs(_stack_heads(qr).astype(BF16), kcat, _sink_column(sink_ref, g), n)
        o = _unstack_heads(_dot(p.astype(BF16), vcat))
        gate = gate_ref[...].astype(F32)
        o_ref[...] = o.astype(BF16)
        ag_ref[...] = (o * (gate * _sigmoid(gate))).astype(BF16)

    shp = jax.ShapeDtypeStruct((s, ATTN_WIDTH), BF16)
    return pl.pallas_call(
        body, name="swa_fwd", grid=(KV_HEADS, nb),
        in_specs=[sp["sinks"], sp["q"], sp["kc"], sp["kp"], sp["vc"], sp["vp"], sp["gate"], sp["tab_c"], sp["tab_c"],
                  sp["tab_p"], sp["tab_p"], sp["qw"], sp["kw"], sp["ones_q"], sp["ones_k"]],
        out_specs=[sp["group"], sp["group"]], out_shape=[shp, shp],
        compiler_params=_params("parallel", "arbitrary"),
    )(sinks, proj, proj, proj, proj, proj, proj, cos, sins, cos, sins, qw, kw, ones_q, ones_k)


def _swa_bwd(proj, dag, sinks, qw, kw, cos, sins, ones_q, ones_k):
    s = proj.shape[0]
    nb = s // BLK
    sp = _swa_specs()

    def body(sink_ref, q_ref, kc_ref, kp_ref, vc_ref, vp_ref, gate_ref, dag_ref, cosc_ref, sinc_ref, cosp_ref, sinp_ref,
             qw_ref, kw_ref, oq_ref, ok_ref, dq_ref, dkc_ref, dkp_ref, dvc_ref, dvp_ref, dqw_ref, dsink_ref):
        g, n = pl.program_id(0), pl.program_id(1)
        cosc, sinc = cosc_ref[...], sinc_ref[...]
        kcat, vcat = _keys_values(kc_ref, kp_ref, vc_ref, vp_ref, kw_ref[...], ok_ref[...], cosc, sinc,
                                  cosp_ref[...], sinp_ref[...], g % 2)
        cosq, sinq = _tile_lanes(cosc, 4), _tile_lanes(sinc, 4)
        qwv = qw_ref[...]
        nrm, rstd, qr = _q_prep(q_ref[...].astype(F32), qwv, cosq, sinq, oq_ref[...])
        gate = gate_ref[...].astype(F32)
        do = dag_ref[...].astype(F32) * (gate * _sigmoid(gate))
        q_stack = _stack_heads(qr).astype(BF16)
        do_stack = _stack_heads(do).astype(BF16)
        p, psink = _attn_probs(q_stack, kcat, _sink_column(sink_ref, g), n)
        dp = _dot_nt(do_stack, vcat)
        delta = jnp.sum(p * dp, axis=-1, keepdims=True)
        ds = (p * (dp - delta) * (HEAD_DIM ** -0.5)).astype(BF16)
        dk = _dot_tn(ds, q_stack)
        dv = _dot_tn(p.astype(BF16), do_stack)
        dk = dk + pltpu.roll(dk, HEAD_DIM, 1)
        dv = dv + pltpu.roll(dv, HEAD_DIM, 1)
        dkp_ref[0], dkc_ref[0] = dk[:BLK], dk[BLK:]
        dvp_ref[0], dvc_ref[0] = dv[:BLK], dv[BLK:]
        dqr = _unstack_heads(_dot(ds, kcat))
        dy = dqr * cosq + _partner64(dqr * sinq)
        dn = dy * qwv
        dq_ref[...] = (rstd * (dn - nrm * _group_mean(dn * nrm, oq_ref[...]))).astype(BF16)

        @pl.when(n == 0)
        def _():
            dqw_ref[...] = jnp.zeros_like(dqw_ref)
            dsink_ref[...] = jnp.zeros_like(dsink_ref)

        dqw_ref[0] += jnp.sum(dy * nrm, axis=0, keepdims=True)
        sink_term = psink * delta
        r8 = lax.broadcasted_iota(jnp.int32, (8, SLAB), 0)
        upd = jnp.zeros((8, SLAB), F32)
        for j in range(GROUP_HEADS):
            upd = jnp.where(r8 == j, -jnp.sum(sink_term[j * BLK:(j + 1) * BLK], axis=0, keepdims=True), upd)
        dsink_ref[0] += upd

    cur_out = pl.BlockSpec((1, BLK, SLAB), lambda g, n: (g, n, 0))
    prev_out = pl.BlockSpec((1, BLK, SLAB), lambda g, n: (g, (n + nb - 1) % nb, 0))
    kv_shape = jax.ShapeDtypeStruct((KV_HEADS, s, SLAB), F32)
    return pl.pallas_call(
        body, name="swa_bwd", grid=(KV_HEADS, nb),
        in_specs=[sp["sinks"], sp["q"], sp["kc"], sp["kp"], sp["vc"], sp["vp"], sp["gate"], sp["group"],
                  sp["tab_c"], sp["tab_c"], sp["tab_p"], sp["tab_p"], sp["qw"], sp["kw"], sp["ones_q"], sp["ones_k"]],
        out_specs=[sp["q"], cur_out, prev_out, cur_out, prev_out,
                   pl.BlockSpec((1, 8, GROUP_WIDTH), lambda g, n: (g, 0, 0)), pl.BlockSpec((1, 8, SLAB), lambda g, n: (g, 0, 0))],
        out_shape=[jax.ShapeDtypeStruct((s, ODD_IN), BF16), kv_shape, kv_shape, kv_shape, kv_shape,
                   jax.ShapeDtypeStruct((KV_HEADS, 8, GROUP_WIDTH), F32), jax.ShapeDtypeStruct((KV_HEADS, 8, SLAB), F32)],
        compiler_params=_params("parallel", "arbitrary"),
    )(sinks, proj, proj, proj, proj, proj, proj, dag, cos, sins, cos, sins, qw, kw, ones_q, ones_k)


def _swa_bwd_kv(proj, dproj, o, dag, dkc, dkp, dvc, dvp, kw, cos, sins, ones_k):
    s = proj.shape[0]
    rows = min(KV_ROWS, s)

    def body(kv_ref, gate_ref, o_ref, dag_ref, dkc_ref, dkp_ref, dvc_ref, dvp_ref, kw_ref, cos_ref, sin_ref, ok_ref,
             dp_in, dp_ref, dkw_ref):
        del dp_in
        i, j = pl.program_id(0), pl.program_id(1)

        @pl.when(jnp.logical_and(i == 0, j == 0))
        def _():
            dkw_ref[...] = jnp.zeros_like(dkw_ref)

        @pl.when(j == 0)
        def _():
            def assemble(cur_ref, prv_ref):
                tot = [cur_ref[h] + prv_ref[h] for h in range(KV_HEADS)]
                first = _half_mask(tot[0].shape, 0)
                return jnp.concatenate([jnp.where(first, tot[0], tot[1]), jnp.where(first, tot[2], tot[3])], axis=1)

            dkr = assemble(dkc_ref, dkp_ref)
            dv = assemble(dvc_ref, dvp_ref)
            cosf, sinf = _tile_lanes(cos_ref[...], 2), _tile_lanes(sin_ref[...], 2)
            k = kv_ref[:, :KV_WIDTH].astype(F32)
            rstd = lax.rsqrt(_group_mean(k * k, ok_ref[...]) + EPS)
            nrm = k * rstd
            dy = dkr * cosf + _partner64(dkr * sinf)
            dn = dy * kw_ref[...]
            dk = rstd * (dn - nrm * _group_mean(dn * nrm, ok_ref[...]))
            dp_ref[...] = jnp.concatenate([dk, dv], axis=1).astype(BF16)
            dkw_ref[...] += jnp.sum(dy * nrm, axis=0, keepdims=True)

        @pl.when(j > 0)
        def _():
            gate = gate_ref[...].astype(F32)
            sg = _sigmoid(gate)
            dp_ref[...] = (dag_ref[...].astype(F32) * o_ref[...].astype(F32) * (sg * (1.0 + gate * (1.0 - sg)))).astype(BF16)

    acc = pl.BlockSpec((KV_HEADS, rows, SLAB), lambda i, j: (0, i, 0))
    group = pl.BlockSpec((rows, GROUP_WIDTH), lambda i, j: (i, jnp.maximum(j - 1, 0)))
    tab = pl.BlockSpec((rows, SLAB), lambda i, j: (i, 0))
    return pl.pallas_call(
        body, name="swa_bwd_kv", grid=(s // rows, 5),
        in_specs=[pl.BlockSpec((rows, GROUP_WIDTH), lambda i, j: (i, KV_COL)),
                  pl.BlockSpec((rows, GROUP_WIDTH), lambda i, j: (i, GATE_COL + jnp.maximum(j - 1, 0))),
                  group, group, acc, acc, acc, acc,
                  pl.BlockSpec((1, KV_WIDTH), lambda i, j: (0, 0)), tab, tab,
                  pl.BlockSpec((KV_WIDTH, KV_WIDTH), lambda i, j: (0, 0)), ANY],
        out_specs=[pl.BlockSpec((rows, GROUP_WIDTH), lambda i, j: (i, KV_COL + j)), pl.BlockSpec((8, KV_WIDTH), lambda i, j: (0, 0))],
        out_shape=[jax.ShapeDtypeStruct(dproj.shape, dproj.dtype), jax.ShapeDtypeStruct((8, KV_WIDTH), F32)],
        input_output_aliases={12: 0},
        compiler_params=_params("arbitrary", "arbitrary"),
    )(proj, proj, o, dag, dkc, dkp, dvc, dvp, kw, cos, sins, ones_k, dproj)


def _place():
    x, y, c = lax.axis_index("x"), lax.axis_index("y"), lax.axis_index("c")
    return x, y, c


OTHER_CHIPS = ((1, 0), (0, 1), (1, 1))


def _half_rows(ref, half, rows):
    return ref.at[pl.ds(pl.multiple_of(half * (rows // 2), 8), rows // 2)]


DMA_CHUNK_BYTES = 1 << 20
BF16_TILE_ROWS = 16


def _n_chunks(ref):
    rows = ref.shape[-2]
    nbytes = math.prod(ref.shape) * jnp.dtype(ref.dtype).itemsize
    n = 1
    while 2 * n * DMA_CHUNK_BYTES <= nbytes and rows % (2 * n * BF16_TILE_ROWS) == 0:
        n *= 2
    return n


def _row_chunk(ref, k, n):
    rows = ref.shape[-2] // n
    return ref.at[pl.ds(k * rows, rows)] if len(ref.shape) == 2 else ref.at[:, pl.ds(k * rows, rows)]


def _push(src, dst, send_sem, recv_sem, device_id):
    n = _n_chunks(src)
    for k in range(n):
        pltpu.make_async_remote_copy(src_ref=_row_chunk(src, k, n), dst_ref=_row_chunk(dst, k, n), send_sem=send_sem,
                                     recv_sem=recv_sem, device_id=device_id, device_id_type=MESH).start()
    return pltpu.make_async_remote_copy(src_ref=src, dst_ref=dst, send_sem=send_sem, recv_sem=recv_sem,
                                        device_id=device_id, device_id_type=MESH)


def _copy(src, dst, sem):
    n = _n_chunks(src)
    for k in range(n):
        pltpu.make_async_copy(_row_chunk(src, k, n), _row_chunk(dst, k, n), sem).start()
    return pltpu.make_async_copy(src, dst, sem)


def _gather_weights(w_in0, w_out0, w_in1, w_out1, small):
    pieces = (
        (w_in0, "col"), (w_out0, "row"), (w_in1, "col"), (w_out1, "row"))
    n_big = len(pieces)

    def body(a0, a1, a2, a3, sm, f0, f1, f2, f3, fsm, ici_send, ici_recv, fwd_send, fwd_recv, small_send, small_recv, local_sem):
        x, y, c = _place()
        my_chip = 2 * x + y
        shards, fulls = (a0, a1, a2, a3), (f0, f1, f2, f3)

        def region(i, chip, half=None):
            rows, width = shards[i].shape
            if pieces[i][1] == "col":
                cols = fulls[i].at[:, pl.ds(pl.multiple_of(chip * width, 128), width)]
                return cols if half is None else _half_rows(cols, half, rows)
            whole = fulls[i].at[pl.ds(pl.multiple_of(chip * rows, 8), rows)]
            return whole if half is None else _half_rows(whole, half, rows)

        sends = []
        for r, (fx, fy) in enumerate(OTHER_CHIPS):
            peer = (x ^ fx, y ^ fy, c)
            for i in range(n_big):
                sends.append(_push(_half_rows(shards[i], c, shards[i].shape[0]), region(i, my_chip, c),
                                   ici_send.at[r * n_big + i], ici_recv.at[r * n_big + i], peer))
            sends.append(_push(sm, fsm.at[my_chip], small_send.at[r], small_recv.at[r], peer))
        local = [_copy(shards[i], region(i, my_chip), local_sem.at[i]) for i in range(n_big)]
        local.append(_copy(sm, fsm.at[my_chip], local_sem.at[n_big]))
        sibling = (x, y, 1 - c)
        for r, (fx, fy) in enumerate(OTHER_CHIPS):
            chip = 2 * (x ^ fx) + (y ^ fy)
            for i in range(n_big):
                landed = region(i, chip, c)
                pltpu.make_async_remote_copy(src_ref=landed, dst_ref=landed, send_sem=ici_send.at[r * n_big + i],
                                             recv_sem=ici_recv.at[r * n_big + i], device_id=sibling, device_id_type=MESH).wait_recv()
                sends.append(_push(landed, landed, fwd_send.at[r * n_big + i], fwd_recv.at[r * n_big + i], sibling))
            pltpu.make_async_remote_copy(src_ref=sm, dst_ref=fsm.at[chip], send_sem=small_send.at[r],
                                         recv_sem=small_recv.at[r], device_id=sibling, device_id_type=MESH).wait_recv()
        for r, (fx, fy) in enumerate(OTHER_CHIPS):
            chip = 2 * (x ^ fx) + (y ^ fy)
            for i in range(n_big):
                theirs = region(i, chip, 1 - c)
                pltpu.make_async_remote_copy(src_ref=theirs, dst_ref=theirs, send_sem=fwd_send.at[r * n_big + i],
                                             recv_sem=fwd_recv.at[r * n_big + i], device_id=sibling, device_id_type=MESH).wait_recv()
        for cp in sends:
            cp.wait_send()
        for cp in local:
            cp.wait()

    def full_shape(w, kind):
        rows, width = w.shape
        return jax.ShapeDtypeStruct((rows, N_CHIPS * width) if kind == "col" else (N_CHIPS * rows, width), w.dtype)

    out_shape = [full_shape(w, kind) for w, kind in pieces] + [jax.ShapeDtypeStruct((N_CHIPS,) + small.shape, small.dtype)]
    dma = pltpu.SemaphoreType.DMA
    return pl.pallas_call(
        body, name="gather_weights", in_specs=[ANY] * 5, out_specs=[ANY] * 5, out_shape=out_shape,
        scratch_shapes=[dma((3 * n_big,)), dma((3 * n_big,)), dma((3 * n_big,)), dma((3 * n_big,)), dma((3,)), dma((3,)),
                        dma((n_big + 1,))],
        compiler_params=pltpu.CompilerParams(has_side_effects=True),
    )(w_in0, w_out0, w_in1, w_out1, small)


def _allreduce_small(v):
    def body(v_ref, out_ref, buf, send_sems, recv_sems):
        x, y, c = _place()
        me = 4 * x + 2 * y + c
        buf[me] = v_ref[...]
        copies = []
        for r in range(1, N_DEV):
            peer = (x ^ (r >> 2), y ^ ((r >> 1) & 1), c ^ (r & 1))
            cp = pltpu.make_async_remote_copy(src_ref=v_ref, dst_ref=buf.at[me], send_sem=send_sems.at[r - 1],
                                              recv_sem=recv_sems.at[r - 1], device_id=peer, device_id_type=MESH)
            cp.start()
            copies.append(cp)
        for cp in copies:
            cp.wait_recv()
        for cp in copies:
            cp.wait_send()
        total = buf[0]
        for d in range(1, N_DEV):
            total = total + buf[d]
        out_ref[...] = total

    vm = pl.BlockSpec(memory_space=pltpu.VMEM)
    return pl.pallas_call(
        body, name="allreduce_small", in_specs=[vm], out_specs=vm, out_shape=jax.ShapeDtypeStruct(v.shape, v.dtype),
        scratch_shapes=[pltpu.VMEM((N_DEV,) + v.shape, v.dtype), pltpu.SemaphoreType.DMA((N_DEV - 1,)),
                        pltpu.SemaphoreType.DMA((N_DEV - 1,))],
        compiler_params=pltpu.CompilerParams(has_side_effects=True),
    )(v)


def _exchange_halves(grads):
    n = len(grads)

    def body(*refs):
        g, theirs = refs[:n], refs[n:2 * n]
        send_sem, recv_sem = refs[2 * n:]
        x, y, c = _place()
        copies = []
        for i in range(n):
            half = g[i].shape[1] // 2
            src = g[i].at[:, pl.ds(pl.multiple_of((1 - c) * half, BF16_TILE_ROWS), half)]
            copies.append(_push(src, theirs[i], send_sem.at[i], recv_sem.at[i], (x, y, 1 - c)))
        for cp in copies:
            cp.wait_recv()
            cp.wait_send()

    dma = pltpu.SemaphoreType.DMA
    return pl.pallas_call(
        body, name="exchange_halves", in_specs=[ANY] * n, out_specs=[ANY] * n,
        out_shape=[jax.ShapeDtypeStruct((a.shape[0], a.shape[1] // 2, a.shape[2]), a.dtype) for a in grads],
        scratch_shapes=[dma((n,)), dma((n,))],
        compiler_params=pltpu.CompilerParams(has_side_effects=True),
    )(*grads)


def _pair_sum(g, theirs, core, name):
    pieces, half, cols = theirs.shape
    tr = min(half, 256)
    per = half // tr

    def body(core_ref, g_ref, t_ref, o_ref):
        del core_ref
        o_ref[...] = (g_ref[...].astype(F32) + t_ref[...].astype(F32)).astype(BF16)

    spec = pl.BlockSpec((1, tr, cols), lambda p, i, core_ref: (p, i, 0))
    return pl.pallas_call(
        body, name=name, out_shape=jax.ShapeDtypeStruct(theirs.shape, BF16),
        grid_spec=pltpu.PrefetchScalarGridSpec(
            num_scalar_prefetch=1, grid=(pieces, per),
            in_specs=[pl.BlockSpec((1, tr, cols), lambda p, i, core_ref: (p, core_ref[0] * per + i, 0)), spec],
            out_specs=spec),
        compiler_params=_params("parallel", "parallel"))(core, g, theirs)


def _scatter_to_owners(parts):
    n = len(parts)

    def body(*refs):
        src, dst = refs[:n], refs[n:2 * n]
        send_sem, recv_sem, local_sem = refs[2 * n:]
        x, y, c = _place()
        my_chip = 2 * x + y

        def piece(i, chip):
            if src[i].shape[0] == N_CHIPS:
                return src[i].at[chip]
            width = src[i].shape[2] // N_CHIPS
            return src[i].at[0].at[:, pl.ds(pl.multiple_of(chip * width, 128), width)]

        remote = []
        for r, (fx, fy) in enumerate(OTHER_CHIPS):
            chip = 2 * (x ^ fx) + (y ^ fy)
            for i in range(n):
                remote.append(_push(piece(i, chip), dst[i].at[my_chip], send_sem.at[r * n + i], recv_sem.at[r * n + i],
                                    (x ^ fx, y ^ fy, c)))
        copies = [_copy(piece(i, my_chip), dst[i].at[my_chip], local_sem.at[i]) for i in range(n)]
        for rc in remote:
            rc.wait_recv()
        for rc in remote:
            rc.wait_send()
        for lc in copies:
            lc.wait()

    def out_shape(a):
        if a.shape[0] == N_CHIPS:
            return jax.ShapeDtypeStruct(a.shape, a.dtype)
        return jax.ShapeDtypeStruct((N_CHIPS, a.shape[1], a.shape[2] // N_CHIPS), a.dtype)

    dma = pltpu.SemaphoreType.DMA
    return pl.pallas_call(
        body, name="scatter_to_owners", in_specs=[ANY] * n, out_specs=[ANY] * n, out_shape=[out_shape(a) for a in parts],
        scratch_shapes=[dma((3 * n,)), dma((3 * n,)), dma((n,))],
        compiler_params=pltpu.CompilerParams(has_side_effects=True),
    )(*parts)


def _sum_chips(stack, core, name):
    _, r, c = stack.shape
    tr = 256
    per = r // tr

    def body(core_ref, s_ref, o_ref):
        del core_ref
        total = s_ref[0].astype(F32)
        for k in range(1, N_CHIPS):
            total = total + s_ref[k].astype(F32)
        o_ref[...] = total

    return pl.pallas_call(
        body, name=name, out_shape=jax.ShapeDtypeStruct((2 * r, c), F32),
        grid_spec=pltpu.PrefetchScalarGridSpec(
            num_scalar_prefetch=1, grid=(per,),
            in_specs=[pl.BlockSpec((N_CHIPS, tr, c), lambda i, core_ref: (0, i, 0))],
            out_specs=pl.BlockSpec((tr, c), lambda i, core_ref: (core_ref[0] * per + i, 0))),
        compiler_params=_params("parallel"))(core, stack)


def _share_halves(fulls):
    n = len(fulls)

    def body(*refs):
        dst = refs[n:2 * n]
        send_sem, recv_sem = refs[2 * n:]
        x, y, c = _place()
        copies = []
        for i in range(n):
            mine = _half_rows(dst[i], c, dst[i].shape[0])
            copies.append(_push(mine, mine, send_sem.at[i], recv_sem.at[i], (x, y, 1 - c)))
        for cp in copies:
            cp.wait_recv()
            cp.wait_send()

    dma = pltpu.SemaphoreType.DMA
    return pl.pallas_call(
        body, name="share_halves", in_specs=[ANY] * n, out_specs=[ANY] * n,
        out_shape=[jax.ShapeDtypeStruct(a.shape, a.dtype) for a in fulls],
        input_output_aliases={i: i for i in range(n)}, scratch_shapes=[dma((n,)), dma((n,))],
        compiler_params=pltpu.CompilerParams(has_side_effects=True),
    )(*fulls)


MM = dict(tm=1024, tn=1024, tk=2048)


def _local_step(x, target, ev_norm_w, w_in0, conv_w, w_out0, od_norm_w, w_in1, q_norm_w, k_norm_w, sinks, w_out1):
    s = x.shape[0]
    cos_r, sin_r = _rope_tables(s, RET_DIM)
    cos_a, sin_a = _rope_tables(s, HEAD_DIM)
    cos_a = jnp.tile(cos_a, (1, 4))
    sins_a = jnp.tile(jnp.concatenate([-sin_a, sin_a], axis=1), (1, 2))
    tables = _retention_tables()
    ones_q, ones_k, ones_kv = _block_diag_mean(GROUP_WIDTH), _block_diag_mean(SLAB), _block_diag_mean(KV_WIDTH)
    qw_g = jnp.tile(q_norm_w, (1, GROUP_WIDTH // HEAD_DIM))
    kw_s = jnp.tile(k_norm_w, (1, SLAB // HEAD_DIM))
    kw_kv = jnp.tile(k_norm_w, (1, KV_WIDTH // HEAD_DIM))
    sinks1 = sinks.reshape(Q_HEADS)

    h0 = _rmsnorm(x, ev_norm_w, "norm0")
    proj0 = _mm(h0, w_in0, mode="nn", out_dtype=BF16, name="proj0", **MM)
    o_ret, cat, states = _retention_fwd(proj0, cos_r, sin_r, tables)
    cat = _conv_fwd(proj0, conv_w, cat)
    x1 = _mm(cat, w_out0, mode="nn", out_dtype=F32, name="out0", add=x, **MM)
    h1 = _rmsnorm(x1, od_norm_w, "norm1")
    proj1 = _mm(h1, w_in1, mode="nn", out_dtype=BF16, name="proj1", tm=1024, tn=1152, tk=2048)
    ag, o_att = _swa_fwd(proj1, sinks1, qw_g, kw_s, cos_a, sins_a, ones_q, ones_k)
    x2 = _mm(ag, w_out1, mode="nn", out_dtype=F32, name="out1", add=x1, **MM)
    dy, dy_b, sq = _loss_grad(x2, target)

    g_w_out1 = _mm(ag, dy_b, mode="tn", out_dtype=BF16, name="g_w_out1", tm=1024, tn=1024, tk=1024)
    dag = _mm(dy_b, w_out1, mode="nt", out_dtype=BF16, name="d_ag", **MM)
    dproj1, dkc, dkp, dvc, dvp, dqw, dsink = _swa_bwd(proj1, dag, sinks1, qw_g, kw_s, cos_a, sins_a, ones_q, ones_k)
    dproj1, dkw = _swa_bwd_kv(proj1, dproj1, o_att, dag, dkc, dkp, dvc, dvp, kw_kv, cos_a, sins_a, ones_kv)
    g_w_in1 = _mm(h1, dproj1, mode="tn", out_dtype=BF16, name="g_w_in1", tm=1024, tn=1152, tk=1024)
    dh1 = _mm(dproj1, w_in1, mode="nt", out_dtype=BF16, name="d_h1", tm=1024, tn=1024, tk=1152)
    dx1, dx1_b, g_norm1 = _rmsnorm_bwd(x1, od_norm_w, dh1, dy, "norm1_bwd", True)

    g_w_out0 = _mm(cat, dx1_b, mode="tn", out_dtype=BF16, name="g_w_out0", tm=1024, tn=1024, tk=1024)
    dcat = _mm(dx1_b, w_out0, mode="nt", out_dtype=BF16, name="d_cat", **MM)
    dproj0 = _retention_bwd(proj0, o_ret, states, dcat, cos_r, sin_r, tables)
    dproj0, g_conv = _conv_bwd(proj0, dcat, conv_w, dproj0)
    g_w_in0 = _mm(h0, dproj0, mode="tn", out_dtype=BF16, name="g_w_in0", tm=1024, tn=1024, tk=1024)
    dh0 = _mm(dproj0, w_in0, mode="nt", out_dtype=BF16, name="d_h0", **MM)
    grad_x, g_norm0 = _rmsnorm_bwd(x, ev_norm_w, dh0, dx1, "norm0_bwd", False)

    g_qw = dqw[:, 0, :].reshape(Q_HEADS, HEAD_DIM).sum(axis=0)
    g_kw = dkw[0].reshape(KV_HEADS, HEAD_DIM).sum(axis=0)
    g_sinks = dsink[:, :, 0].reshape(Q_HEADS)
    small = dict(ev_norm=g_norm0[0], od_norm=g_norm1[0], conv=g_conv[:3], qw=g_qw, kw=g_kw, sinks=g_sinks)
    return sq[0, 0], grad_x, (g_w_in0, g_w_out0, g_w_in1, g_w_out1), small


def _pack_small_grads(small):
    pad = lambda v: jnp.pad(v, (0, D_MODEL - v.shape[0]))
    tail = pad(jnp.concatenate([small["qw"], small["kw"], small["sinks"]]))
    rows = [small["ev_norm"], small["od_norm"]] + [pad(small["conv"][t]) for t in range(3)] + [tail]
    rows += [jnp.zeros((D_MODEL,), F32)] * (8 - len(rows))
    return jnp.stack(rows)


def _reduce_scatter(grads):
    g_w_in0, g_w_out0, g_w_in1, g_w_out1 = grads
    core = lax.axis_index("c").astype(jnp.int32).reshape(1)
    by_rows = lambda g: g.reshape(N_CHIPS, g.shape[0] // N_CHIPS, g.shape[1])
    pieces = [g_w_in0[None], by_rows(g_w_out0), g_w_in1[None], by_rows(g_w_out1)]
    names = ("w_in0", "w_out0", "w_in1", "w_out1")
    theirs = _exchange_halves(pieces)
    parts = [_pair_sum(g, t, core, "pair_sum_" + nm) for g, t, nm in zip(pieces, theirs, names)]
    stacks = _scatter_to_owners(parts)
    return _share_halves([_sum_chips(st, core, "chip_sum_" + nm) for st, nm in zip(stacks, names)])


def kernel(x, ev_norm_w, ev_w_in, ev_conv_w, ev_w_out, od_norm_w, od_w_in, od_q_norm_w, od_k_norm_w, od_sinks, od_w_out, loss_target, m_ev_norm_w, m_ev_w_in, m_ev_conv_w, m_ev_w_out, m_od_norm_w, m_od_w_in, m_od_q_norm_w, m_od_k_norm_w, m_od_sinks, m_od_w_out, v_ev_norm_w, v_ev_w_in, v_ev_conv_w, v_ev_w_out, v_od_norm_w, v_od_w_in, v_od_q_norm_w, v_od_k_norm_w, v_od_sinks, v_od_w_out):
    my_chip = 2 * lax.axis_index("x") + lax.axis_index("y")
    shard_w = D_MODEL // N_CHIPS
    conv_shard = CONV_WIDTH // N_CHIPS

    small_in = jnp.zeros((8, shard_w), F32)
    small_in = small_in.at[0].set(od_norm_w[0]).at[1:4, :conv_shard].set(ev_conv_w[0])
    w_in0, w_out0, w_in1, w_out1, small_all = _gather_weights(
        _cast_bf16(ev_w_in[0], "cast_w_in0"), _cast_bf16(ev_w_out[0], "cast_w_out0"),
        _cast_bf16(od_w_in[0], "cast_w_in1"), _cast_bf16(od_w_out[0], "cast_w_out1"), small_in)
    od_norm_full = small_all[:, 0, :].reshape(1, D_MODEL)
    conv_full = jnp.transpose(small_all[:, 1:4, :conv_shard], (1, 0, 2)).reshape(3, CONV_WIDTH)

    sq, grad_x, big, small = _local_step(x[0], loss_target[0], ev_norm_w, w_in0, conv_full, w_out0, od_norm_full, w_in1,
                                         od_q_norm_w, od_k_norm_w, od_sinks, w_out1)
    loss = lax.psum(0.5 * sq / D_MODEL, ("x", "y", "c"))

    g_ev_w_in, g_ev_w_out, g_od_w_in, g_od_w_out = _reduce_scatter(big)
    tot = _allreduce_small(_pack_small_grads(small))
    g_ev_norm = tot[0:1]
    g_od_norm = lax.dynamic_slice(tot, (1, my_chip * shard_w), (1, shard_w))
    g_conv = lax.dynamic_slice(tot, (2, my_chip * conv_shard), (3, conv_shard))
    g_qw, g_kw, g_sinks = tot[5:6, 0:HEAD_DIM], tot[5:6, HEAD_DIM:2 * HEAD_DIM], tot[5:6, 2 * HEAD_DIM:2 * HEAD_DIM + Q_HEADS]

    upd = {}
    upd["ev_w_in"] = _adamw(ev_w_in[0], g_ev_w_in, m_ev_w_in[0], v_ev_w_in[0], "adamw_ev_w_in")
    upd["ev_w_out"] = _adamw(ev_w_out[0], g_ev_w_out, m_ev_w_out[0], v_ev_w_out[0], "adamw_ev_w_out")
    upd["od_w_in"] = _adamw(od_w_in[0], g_od_w_in, m_od_w_in[0], v_od_w_in[0], "adamw_od_w_in")
    upd["od_w_out"] = _adamw(od_w_out[0], g_od_w_out, m_od_w_out[0], v_od_w_out[0], "adamw_od_w_out")
    smalls = (("ev_norm_w", ev_norm_w, g_ev_norm, m_ev_norm_w, v_ev_norm_w),
              ("ev_conv_w", ev_conv_w, g_conv, m_ev_conv_w, v_ev_conv_w),
              ("od_norm_w", od_norm_w, g_od_norm, m_od_norm_w, v_od_norm_w),
              ("od_q_norm_w", od_q_norm_w, g_qw, m_od_q_norm_w, v_od_q_norm_w),
              ("od_k_norm_w", od_k_norm_w, g_kw, m_od_k_norm_w, v_od_k_norm_w),
              ("od_sinks", od_sinks, g_sinks, m_od_sinks, v_od_sinks))
    sizes = [w.size for _, w, _, _, _ in smalls]
    padded = 8 * 128 * math.ceil(sum(sizes) / (8 * 128))
    pack = lambda arrs, fill: jnp.concatenate(
        [a.reshape(-1) for a in arrs] + [jnp.full((padded - sum(sizes),), fill, F32)]).reshape(8, padded // 8)
    packed = _adamw(pack([w for _, w, _, _, _ in smalls], 0.0), pack([g for _, _, g, _, _ in smalls], 0.0),
                    pack([m for _, _, _, m, _ in smalls], 0.0), pack([v for _, _, _, _, v in smalls], 1.0), "adamw_small")
    offs = [sum(sizes[:i]) for i in range(len(sizes))]
    grads = {"ev_w_in": g_ev_w_in[None], "ev_w_out": g_ev_w_out[None], "od_w_in": g_od_w_in[None], "od_w_out": g_od_w_out[None]}
    for (nm, w, g, _, _), off, size in zip(smalls, offs, sizes):
        upd[nm] = tuple(p.reshape(-1)[off:off + size].reshape(w.shape) for p in packed)
        grads[nm] = g.reshape(w.shape)
    for nm in ("ev_w_in", "ev_w_out", "od_w_in", "od_w_out"):
        upd[nm] = tuple(u[None] for u in upd[nm])
    order = ("ev_norm_w", "ev_w_in", "ev_conv_w", "ev_w_out", "od_norm_w", "od_w_in", "od_q_norm_w", "od_k_norm_w", "od_sinks", "od_w_out")
    return (loss, grad_x[None], *[grads[nm] for nm in order], *[upd[nm][0] for nm in order],
            *[upd[nm][1] for nm in order], *[upd[nm][2] for nm in order])
```

```python
import functools
import math

import jax
import jax.numpy as jnp
from jax import lax
from jax.experimental import pallas as pl
from jax.experimental.pallas import tpu as pltpu

F32 = jnp.float32
BF16 = jnp.bfloat16

D_MODEL = 2048
RET_HEADS = 4
RET_DIM = 256
RET_WIDTH = 1024
CONV_WIDTH = 1024
EVEN_IN = 8192
Q_HEADS = 32
HEAD_DIM = 64
KV_HEADS = 4
KV_WIDTH = 256
ATTN_WIDTH = 2048
ODD_IN = 4608
BLK = 128
ROPE_THETA = 10000.0
EPS = 1e-6
ADAM_LR = 0.001
ADAM_B1 = 0.9
ADAM_B2 = 0.999
ADAM_EPS = 1e-08
ADAM_WD = 0.01
ADAM_STEP = 10
N_CHIPS = 4
N_DEV = 8
VMEM_LIMIT_BYTES = 48 * 1024 * 1024
MESH = pl.DeviceIdType.MESH
ANY = pl.BlockSpec(memory_space=pl.ANY)


def _params(*sem):
    return pltpu.CompilerParams(dimension_semantics=sem, vmem_limit_bytes=VMEM_LIMIT_BYTES)


def _dot(a, b):
    return jnp.dot(a, b, preferred_element_type=F32)


def _dot_nt(a, b):
    return lax.dot_general(a, b, (((1,), (1,)), ((), ())), preferred_element_type=F32)


def _dot_tn(a, b):
    return lax.dot_general(a, b, (((0,), (0,)), ((), ())), preferred_element_type=F32)


def _sigmoid(x):
    return 1.0 / (1.0 + jnp.exp(-x))


def _mm(a, b, *, mode, tm, tn, tk, out_dtype, name, add=None):
    if mode == "nn":
        (m, k), n = a.shape, b.shape[1]
    elif mode == "nt":
        (m, k), n = a.shape, b.shape[0]
    else:
        (k, m), n = a.shape, b.shape[1]
    tm, tn, tk = min(tm, m), min(tn, n), min(tk, k)
    assert m % tm == 0 and n % tn == 0 and k % tk == 0, (name, m, n, k)
    nk = k // tk
    dot = {"nn": _dot, "nt": _dot_nt, "tn": _dot_tn}[mode]
    a_spec = (pl.BlockSpec((tk, tm), lambda i, j, kk: (kk, i)) if mode == "tn"
              else pl.BlockSpec((tm, tk), lambda i, j, kk: (i, kk)))
    b_spec = (pl.BlockSpec((tn, tk), lambda i, j, kk: (j, kk)) if mode == "nt"
              else pl.BlockSpec((tk, tn), lambda i, j, kk: (kk, j)))
    o_spec = pl.BlockSpec((tm, tn), lambda i, j, kk: (i, j))
    has_add = add is not None

    def body(*refs):
        a_ref, b_ref = refs[0], refs[1]
        add_ref = refs[2] if has_add else None
        o_ref = refs[3] if has_add else refs[2]
        acc_ref = refs[-1]
        p = dot(a_ref[...], b_ref[...])

        def finish(total):
            if has_add:
                total = total + add_ref[...].astype(F32)
            o_ref[...] = total.astype(out_dtype)

        if nk == 1:
            finish(p)
        else:
            kk = pl.program_id(2)

            @pl.when(kk == 0)
            def _():
                acc_ref[...] = p

            @pl.when(jnp.logical_and(kk > 0, kk < nk - 1))
            def _():
                acc_ref[...] += p

            @pl.when(kk == nk - 1)
            def _():
                finish(acc_ref[...] + p)

    in_specs = [a_spec, b_spec] + ([o_spec] if has_add else [])
    args = (a, b) + ((add,) if has_add else ())
    return pl.pallas_call(
        body, name=name, grid=(m // tm, n // tn, nk), in_specs=in_specs, out_specs=o_spec,
        out_shape=jax.ShapeDtypeStruct((m, n), out_dtype),
        scratch_shapes=[pltpu.VMEM((tm, tn) if nk > 1 else (8, 128), F32)],
        compiler_params=_params("parallel", "parallel", "arbitrary"),
    )(*args)


def _cast_bf16(w, name):
    r, c = w.shape
    tr = min(r, 512)

    def body(w_ref, o_ref):
        o_ref[...] = w_ref[...].astype(BF16)

    spec = pl.BlockSpec((tr, c), lambda i: (i, 0))
    return pl.pallas_call(body, name=name, grid=(r // tr,), in_specs=[spec], out_specs=spec,
                          out_shape=jax.ShapeDtypeStruct((r, c), BF16), compiler_params=_params("parallel"))(w)


def _rmsnorm(x, w, name):
    s, d = x.shape
    tr = 256

    def body(x_ref, w_ref, h_ref):
        xv = x_ref[...]
        rstd = lax.rsqrt(jnp.mean(xv * xv, axis=-1, keepdims=True) + EPS)
        h_ref[...] = (xv * rstd * w_ref[...]).astype(BF16)

    return pl.pallas_call(
        body, name=name, grid=(s // tr,),
        in_specs=[pl.BlockSpec((tr, d), lambda i: (i, 0)), pl.BlockSpec((1, d), lambda i: (0, 0))],
        out_specs=pl.BlockSpec((tr, d), lambda i: (i, 0)),
        out_shape=jax.ShapeDtypeStruct((s, d), BF16), compiler_params=_params("parallel"))(x, w)


def _rmsnorm_bwd(x, w, dh, dres, name, with_bf16):
    s, d = x.shape
    tr = 256

    def body(x_ref, w_ref, dh_ref, dres_ref, *outs):
        dx_ref, dw_ref = outs[0], outs[-1]
        xv = x_ref[...]
        rstd = lax.rsqrt(jnp.mean(xv * xv, axis=-1, keepdims=True) + EPS)
        nrm = xv * rstd
        dhv = dh_ref[...].astype(F32)
        dn = dhv * w_ref[...]
        dx = dres_ref[...] + rstd * (dn - nrm * jnp.mean(dn * nrm, axis=-1, keepdims=True))
        dx_ref[...] = dx
        if with_bf16:
            outs[1][...] = dx.astype(BF16)

        @pl.when(pl.program_id(0) == 0)
        def _():
            dw_ref[...] = jnp.zeros_like(dw_ref)

        dw_ref[...] += jnp.sum(dhv * nrm, axis=0, keepdims=True)

    row = pl.BlockSpec((tr, d), lambda i: (i, 0))
    out_shape = [jax.ShapeDtypeStruct((s, d), F32)] + ([jax.ShapeDtypeStruct((s, d), BF16)] if with_bf16 else [])
    out_shape.append(jax.ShapeDtypeStruct((8, d), F32))
    out_specs = [row] * (2 if with_bf16 else 1) + [pl.BlockSpec((8, d), lambda i: (0, 0))]
    return pl.pallas_call(
        body, name=name, grid=(s // tr,),
        in_specs=[row, pl.BlockSpec((1, d), lambda i: (0, 0)), row, row],
        out_specs=out_specs, out_shape=out_shape, compiler_params=_params("arbitrary"))(x, w, dh, dres)


def _loss_grad(y, target):
    s, d = y.shape
    tr = 256

    def body(y_ref, t_ref, dy_ref, dyb_ref, l_ref):
        diff = y_ref[...] - t_ref[...]
        dy = diff * (1.0 / d)
        dy_ref[...] = dy
        dyb_ref[...] = dy.astype(BF16)

        @pl.when(pl.program_id(0) == 0)
        def _():
            l_ref[...] = jnp.zeros_like(l_ref)

        l_ref[...] += jnp.sum(jnp.sum(diff * diff, axis=1, keepdims=True), axis=0, keepdims=True)

    row = pl.BlockSpec((tr, d), lambda i: (i, 0))
    return pl.pallas_call(
        body, name="loss_grad", grid=(s // tr,), in_specs=[row, row],
        out_specs=[row, row, pl.BlockSpec((8, 128), lambda i: (0, 0))],
        out_shape=[jax.ShapeDtypeStruct((s, d), F32), jax.ShapeDtypeStruct((s, d), BF16),
                   jax.ShapeDtypeStruct((8, 128), F32)],
        compiler_params=_params("arbitrary"))(y, target)


def _adamw(w, g, m, v, name):
    r, c = w.shape
    tr = min(r, 256)
    assert r % tr == 0

    def body(w_ref, g_ref, m_ref, v_ref, d_ref, nm_ref, nv_ref):
        gv = g_ref[...]
        nm = ADAM_B1 * m_ref[...] + (1.0 - ADAM_B1) * gv
        nv = ADAM_B2 * v_ref[...] + (1.0 - ADAM_B2) * (gv * gv)
        m_hat = nm / (1.0 - ADAM_B1 ** ADAM_STEP)
        v_hat = nv / (1.0 - ADAM_B2 ** ADAM_STEP)
        d_ref[...] = -ADAM_LR * (m_hat / (jnp.sqrt(v_hat) + ADAM_EPS) + ADAM_WD * w_ref[...])
        nm_ref[...] = nm
        nv_ref[...] = nv

    spec = pl.BlockSpec((tr, c), lambda i: (i, 0))
    shp = jax.ShapeDtypeStruct((r, c), F32)
    return pl.pallas_call(body, name=name, grid=(r // tr,), in_specs=[spec] * 4, out_specs=[spec] * 3,
                          out_shape=[shp] * 3, compiler_params=_params("parallel"))(w, g, m, v)


def _rope_tables(s, dim):
    inv = 1.0 / (ROPE_THETA ** (jnp.arange(0, dim, 2, dtype=F32) / dim))
    ang = jnp.arange(s).astype(F32)[:, None] * inv[None, :]
    return jnp.cos(ang), jnp.sin(ang)


def _rope_half(x, cos, sin):
    h = x.shape[1] // 2
    x1, x2 = x[:, :h], x[:, h:]
    return jnp.concatenate([x1 * cos - x2 * sin, x2 * cos + x1 * sin], axis=1)


def _unrope_half(dy, cos, sin):
    h = dy.shape[1] // 2
    d1, d2 = dy[:, :h], dy[:, h:]
    return jnp.concatenate([d1 * cos + d2 * sin, d2 * cos - d1 * sin], axis=1)


def _lane(shape):
    return lax.broadcasted_iota(jnp.int32, shape, 1)


def _partner64(x):
    w = x.shape[1]
    first = (_lane(x.shape) % HEAD_DIM) < (HEAD_DIM // 2)
    return jnp.where(first, pltpu.roll(x, w - HEAD_DIM // 2, 1), pltpu.roll(x, HEAD_DIM // 2, 1))


def _tile_lanes(t, reps):
    return t if reps == 1 else jnp.concatenate([t] * reps, axis=1)


def _group_mean(x, ones_bd):
    hi = x.astype(BF16)
    lo = (x - hi.astype(F32)).astype(BF16)
    return _dot(hi, ones_bd) + _dot(lo, ones_bd)


def _block_diag_mean(width):
    idx = jnp.arange(width) // HEAD_DIM
    return jnp.where(idx[:, None] == idx[None, :], 1.0 / HEAD_DIM, 0.0).astype(BF16)


def _retention_tables():
    h = RET_HEADS
    log_g = jnp.log(1.0 - 2.0 ** (-5.0 - jnp.arange(h, dtype=F32)))
    idx = jnp.arange(BLK, dtype=F32)
    diff = idx[:, None] - idx[None, :]
    intra = jnp.where(diff >= 0, jnp.exp(log_g[:, None, None] * jnp.maximum(diff, 0.0)), 0.0).astype(F32)
    q_dec = jnp.exp(log_g[:, None] * (idx[None, :] + 1.0)).astype(F32)[:, :, None]
    k_dec = jnp.exp(log_g[:, None] * (BLK - 1.0 - idx[None, :])).astype(F32)[:, :, None]
    chunk_dec = jnp.exp(log_g * BLK).astype(F32)[:, None, None]
    return intra, q_dec, k_dec, chunk_dec


def _retention_fwd(proj, cos, sin, tables):
    s = proj.shape[0]
    nc = s // BLK
    intra, q_dec, k_dec, chunk_dec = tables

    def body(p_ref, cos_ref, sin_ref, in_ref, qd_ref, kd_ref, cd_ref, o_ref, cat_ref, st_ref, state):
        @pl.when(pl.program_id(0) == 0)
        def _():
            state[...] = jnp.zeros_like(state)

        cosv, sinv = cos_ref[...], sin_ref[...]
        for h in range(RET_HEADS):
            c0 = h * RET_DIM
            q = p_ref[:, c0:c0 + RET_DIM].astype(F32)
            k = p_ref[:, RET_WIDTH + c0:RET_WIDTH + c0 + RET_DIM].astype(F32)
            v = p_ref[:, 2 * RET_WIDTH + c0:2 * RET_WIDTH + c0 + RET_DIM]
            g = p_ref[:, 3 * RET_WIDTH + c0:3 * RET_WIDTH + c0 + RET_DIM].astype(F32)
            qb = _rope_half(q, cosv, sinv).astype(BF16)
            kr = _rope_half(k, cosv, sinv) * (RET_DIM ** -0.5)
            kb = kr.astype(BF16)
            scores = _dot_nt(qb, kb) * in_ref[h]
            inner = _dot(scores.astype(BF16), v)
            prev = state[h]
            prev_b = prev.astype(BF16)
            st_ref[h, 0] = prev_b
            o = inner + _dot(qb, prev_b) * qd_ref[h]
            o_ref[:, c0:c0 + RET_DIM] = o
            rstd = lax.rsqrt(jnp.mean(o * o, axis=-1, keepdims=True) + EPS)
            cat_ref[:, c0:c0 + RET_DIM] = (o * rstd * (g * _sigmoid(g))).astype(BF16)
            state[h] = cd_ref[h] * prev + _dot_tn((kr * kd_ref[h]).astype(BF16), v)

    full = lambda shape: pl.BlockSpec(shape, lambda n: (0,) * len(shape))
    return pl.pallas_call(
        body, name="retention_fwd", grid=(nc,),
        in_specs=[pl.BlockSpec((BLK, 4 * RET_WIDTH), lambda n: (n, 0)),
                  pl.BlockSpec((BLK, RET_DIM // 2), lambda n: (n, 0)), pl.BlockSpec((BLK, RET_DIM // 2), lambda n: (n, 0)),
                  full((RET_HEADS, BLK, BLK)), full((RET_HEADS, BLK, 1)), full((RET_HEADS, BLK, 1)), full((RET_HEADS, 1, 1))],
        out_specs=[pl.BlockSpec((BLK, RET_WIDTH), lambda n: (n, 0)), pl.BlockSpec((BLK, RET_WIDTH), lambda n: (n, 0)),
                   pl.BlockSpec((RET_HEADS, 1, RET_DIM, RET_DIM), lambda n: (0, n, 0, 0))],
        out_shape=[jax.ShapeDtypeStruct((s, RET_WIDTH), F32), jax.ShapeDtypeStruct((s, D_MODEL), BF16),
                   jax.ShapeDtypeStruct((RET_HEADS, nc, RET_DIM, RET_DIM), BF16)],
        scratch_shapes=[pltpu.VMEM((RET_HEADS, RET_DIM, RET_DIM), F32)],
        compiler_params=_params("arbitrary"))(proj, cos, sin, intra, q_dec, k_dec, chunk_dec)


def _retention_bwd(proj, o, states, dcat, cos, sin, tables):
    s = proj.shape[0]
    nc = s // BLK
    intra, q_dec, k_dec, chunk_dec = tables

    def body(p_ref, o_ref, st_ref, dc_ref, cos_ref, sin_ref, in_ref, qd_ref, kd_ref, cd_ref, dp_ref, dstate):
        @pl.when(pl.program_id(0) == 0)
        def _():
            dstate[...] = jnp.zeros_like(dstate)

        cosv, sinv = cos_ref[...], sin_ref[...]
        for h in range(RET_HEADS):
            c0 = h * RET_DIM
            q = p_ref[:, c0:c0 + RET_DIM].astype(F32)
            k = p_ref[:, RET_WIDTH + c0:RET_WIDTH + c0 + RET_DIM].astype(F32)
            v = p_ref[:, 2 * RET_WIDTH + c0:2 * RET_WIDTH + c0 + RET_DIM]
            g = p_ref[:, 3 * RET_WIDTH + c0:3 * RET_WIDTH + c0 + RET_DIM].astype(F32)
            o = o_ref[:, c0:c0 + RET_DIM]
            dc = dc_ref[:, c0:c0 + RET_DIM].astype(F32)
            rstd = lax.rsqrt(jnp.mean(o * o, axis=-1, keepdims=True) + EPS)
            nrm = o * rstd
            sg = _sigmoid(g)
            dg = dc * nrm * (sg * (1.0 + g * (1.0 - sg)))
            dn = dc * (g * sg)
            do = rstd * (dn - nrm * jnp.mean(dn * nrm, axis=-1, keepdims=True))
            qb = _rope_half(q, cosv, sinv).astype(BF16)
            kr = _rope_half(k, cosv, sinv) * (RET_DIM ** -0.5)
            kb = kr.astype(BF16)
            mask = in_ref[h]
            qd, kd = qd_ref[h], kd_ref[h]
            prev_b = st_ref[h, 0]
            dnext = dstate[h]
            dnext_b = dnext.astype(BF16)
            att = (_dot_nt(qb, kb) * mask).astype(BF16)
            do_b = do.astype(BF16)
            doq = (do * qd).astype(BF16)
            dv = _dot_tn(att, do_b) + _dot((kr * kd).astype(BF16), dnext_b)
            ds = (_dot_nt(do_b, v) * mask).astype(BF16)
            dqr = _dot(ds, kb) + _dot_nt(doq, prev_b)
            dkr = _dot_tn(ds, qb) + _dot_nt(v, dnext_b) * kd
            dstate[h] = cd_ref[h] * dnext + _dot_tn(qb, doq)
            dq = _unrope_half(dqr, cosv, sinv)
            dk = _unrope_half(dkr * (RET_DIM ** -0.5), cosv, sinv)
            dp_ref[:, c0:c0 + RET_DIM] = dq.astype(BF16)
            dp_ref[:, RET_WIDTH + c0:RET_WIDTH + c0 + RET_DIM] = dk.astype(BF16)
            dp_ref[:, 2 * RET_WIDTH + c0:2 * RET_WIDTH + c0 + RET_DIM] = dv.astype(BF16)
            dp_ref[:, 3 * RET_WIDTH + c0:3 * RET_WIDTH + c0 + RET_DIM] = dg.astype(BF16)

    rev = lambda n: nc - 1 - n
    full = lambda shape: pl.BlockSpec(shape, lambda n: (0,) * len(shape))
    return pl.pallas_call(
        body, name="retention_bwd", grid=(nc,),
        in_specs=[pl.BlockSpec((BLK, 4 * RET_WIDTH), lambda n: (rev(n), 0)),
                  pl.BlockSpec((BLK, RET_WIDTH), lambda n: (rev(n), 0)),
                  pl.BlockSpec((RET_HEADS, 1, RET_DIM, RET_DIM), lambda n: (0, rev(n), 0, 0)),
                  pl.BlockSpec((BLK, RET_WIDTH), lambda n: (rev(n), 0)),
                  pl.BlockSpec((BLK, RET_DIM // 2), lambda n: (rev(n), 0)), pl.BlockSpec((BLK, RET_DIM // 2), lambda n: (rev(n), 0)),
                  full((RET_HEADS, BLK, BLK)), full((RET_HEADS, BLK, 1)), full((RET_HEADS, BLK, 1)), full((RET_HEADS, 1, 1))],
        out_specs=pl.BlockSpec((BLK, 4 * RET_WIDTH), lambda n: (rev(n), 0)),
        out_shape=jax.ShapeDtypeStruct((s, EVEN_IN), BF16),
        scratch_shapes=[pltpu.VMEM((RET_HEADS, RET_DIM, RET_DIM), F32)],
        compiler_params=_params("arbitrary"))(proj, o, states, dcat, cos, sin, intra, q_dec, k_dec, chunk_dec)


CONV_ROWS = 256
HALO = 16


def _conv_pieces(p, halo, conv_w, first):
    rows = p.shape[0]
    gb, gc, u, gv = (p[:, i * CONV_WIDTH:(i + 1) * CONV_WIDTH] for i in range(4))
    cu = gc * u
    hcu = halo[:, CONV_WIDTH:2 * CONV_WIDTH] * halo[:, 2 * CONV_WIDTH:3 * CONV_WIDTH]
    hcu = jnp.where(first, 0.0, hcu)
    r1, r2 = hcu[HALO - 1:HALO], hcu[HALO - 2:HALO - 1]
    row = lax.broadcasted_iota(jnp.int32, cu.shape, 0)
    m1 = jnp.where(row == 0, r1, pltpu.roll(cu, 1, 0))
    m2 = jnp.where(row == 0, r2, jnp.where(row == 1, r1, pltpu.roll(cu, 2, 0)))
    del rows, conv_w
    return gb, gc, u, gv, cu, m1, m2


def _conv_fwd(proj, conv_w, cat):
    s = proj.shape[0]
    per = CONV_ROWS // HALO

    def body(p_ref, halo_ref, w_ref, cat_in, cat_ref):
        del cat_in
        first = pl.program_id(0) == 0
        gb, _, _, gv, cu, m1, m2 = _conv_pieces(p_ref[...].astype(F32), halo_ref[...].astype(F32), None, first)
        conv = w_ref[0:1, :] * m2 + w_ref[1:2, :] * m1 + w_ref[2:3, :] * cu
        cat_ref[...] = (gb * conv * (gv * _sigmoid(gv))).astype(BF16)

    return pl.pallas_call(
        body, name="conv_fwd", grid=(s // CONV_ROWS,),
        in_specs=[pl.BlockSpec((CONV_ROWS, 4 * CONV_WIDTH), lambda i: (i, 1)),
                  pl.BlockSpec((HALO, 4 * CONV_WIDTH), lambda i: (jnp.maximum(i * per - 1, 0), 1)),
                  pl.BlockSpec((3, CONV_WIDTH), lambda i: (0, 0)), ANY],
        out_specs=pl.BlockSpec((CONV_ROWS, CONV_WIDTH), lambda i: (i, 1)),
        out_shape=jax.ShapeDtypeStruct(cat.shape, cat.dtype), input_output_aliases={3: 0},
        compiler_params=_params("parallel"))(proj, proj, conv_w, cat)


def _conv_bwd(proj, dcat, conv_w, dproj):
    s = proj.shape[0]
    per = CONV_ROWS // HALO
    last_halo = s // HALO - 1
    nsteps = s // CONV_ROWS

    def body(p_ref, halo_ref, nxt_ref, dc_ref, dnxt_ref, w_ref, dp_in, dp_ref, dw_ref):
        del dp_in
        i = pl.program_id(0)
        gb, gc, u, gv, cu, m1, m2 = _conv_pieces(p_ref[...].astype(F32), halo_ref[...].astype(F32), None, i == 0)
        w0, w1, w2 = w_ref[0:1, :], w_ref[1:2, :], w_ref[2:3, :]
        conv = w0 * m2 + w1 * m1 + w2 * cu
        dco = dc_ref[...].astype(F32)
        sg = _sigmoid(gv)
        silu = gv * sg
        dgb = dco * conv * silu
        dgv = dco * gb * conv * (sg * (1.0 + gv * (1.0 - sg)))
        dconv = dco * gb * silu
        nxt = nxt_ref[...].astype(F32)
        ngv = nxt[:, 3 * CONV_WIDTH:]
        dnext = dnxt_ref[...].astype(F32) * nxt[:, :CONV_WIDTH] * (ngv * _sigmoid(ngv))
        dnext = jnp.where(i == nsteps - 1, 0.0, dnext)
        n1, n2 = dnext[0:1], dnext[1:2]
        row = lax.broadcasted_iota(jnp.int32, dconv.shape, 0)
        p1 = jnp.where(row == CONV_ROWS - 1, n1, pltpu.roll(dconv, CONV_ROWS - 1, 0))
        p2 = jnp.where(row == CONV_ROWS - 1, n2, jnp.where(row == CONV_ROWS - 2, n1, pltpu.roll(dconv, CONV_ROWS - 2, 0)))
        dcu = w2 * dconv + w1 * p1 + w0 * p2
        dp_ref[...] = jnp.concatenate([dgb, dcu * u, dcu * gc, dgv], axis=1).astype(BF16)

        @pl.when(i == 0)
        def _():
            dw_ref[...] = jnp.zeros_like(dw_ref)

        taps = [jnp.sum(dconv * m, axis=0, keepdims=True) for m in (m2, m1, cu)]
        r8 = lax.broadcasted_iota(jnp.int32, dw_ref.shape, 0)
        dw_ref[...] += jnp.where(r8 == 0, taps[0], jnp.where(r8 == 1, taps[1], jnp.where(r8 == 2, taps[2], 0.0)))

    return pl.pallas_call(
        body, name="conv_bwd", grid=(nsteps,),
        in_specs=[pl.BlockSpec((CONV_ROWS, 4 * CONV_WIDTH), lambda i: (i, 1)),
                  pl.BlockSpec((HALO, 4 * CONV_WIDTH), lambda i: (jnp.maximum(i * per - 1, 0), 1)),
                  pl.BlockSpec((HALO, 4 * CONV_WIDTH), lambda i: (jnp.minimum((i + 1) * per, last_halo), 1)),
                  pl.BlockSpec((CONV_ROWS, CONV_WIDTH), lambda i: (i, 1)),
                  pl.BlockSpec((HALO, CONV_WIDTH), lambda i: (jnp.minimum((i + 1) * per, last_halo), 1)),
                  pl.BlockSpec((3, CONV_WIDTH), lambda i: (0, 0)), ANY],
        out_specs=[pl.BlockSpec((CONV_ROWS, 4 * CONV_WIDTH), lambda i: (i, 1)), pl.BlockSpec((8, CONV_WIDTH), lambda i: (0, 0))],
        out_shape=[jax.ShapeDtypeStruct(dproj.shape, dproj.dtype), jax.ShapeDtypeStruct((8, CONV_WIDTH), F32)],
        input_output_aliases={6: 0},
        compiler_params=_params("arbitrary"))(proj, proj, proj, dcat, dcat, conv_w, dproj)


GROUP_WIDTH = 8 * HEAD_DIM
GROUP_HEADS = 8
SLAB = 128
Q_COL = 0
KV_COL = 4
GATE_COL = 5
K_SLAB0 = ATTN_WIDTH // SLAB
V_SLAB0 = (ATTN_WIDTH + KV_WIDTH) // SLAB
KV_ROWS = 512


def _half_mask(shape, which):
    return (_lane(shape) // HEAD_DIM) == which


def _dup_head(slab, which):
    kept = jnp.where(_half_mask(slab.shape, which), slab, 0.0)
    return kept + pltpu.roll(kept, HEAD_DIM, 1)


def _stack_heads(x):
    parts = []
    for sl in range(GROUP_WIDTH // SLAB):
        slab = x[:, sl * SLAB:(sl + 1) * SLAB]
        parts += [jnp.where(_half_mask(slab.shape, e), slab, 0.0) for e in range(2)]
    return jnp.concatenate(parts, axis=0)


def _unstack_heads(y):
    slabs = []
    for sl in range(GROUP_WIDTH // SLAB):
        a, b = y[(2 * sl) * BLK:(2 * sl + 1) * BLK], y[(2 * sl + 1) * BLK:(2 * sl + 2) * BLK]
        slabs.append(jnp.where(_half_mask(a.shape, 0), a, b))
    return jnp.concatenate(slabs, axis=1)


def _q_prep(q, qw, cosf, sins, ones_bd):
    rstd = lax.rsqrt(_group_mean(q * q, ones_bd) + EPS)
    nrm = q * rstd
    y = nrm * qw
    return nrm, rstd, y * cosf + _partner64(y) * sins


def _valid_keys(n):
    qi = lax.broadcasted_iota(jnp.int32, (BLK, 2 * BLK), 0) + BLK
    kj = lax.broadcasted_iota(jnp.int32, (BLK, 2 * BLK), 1)
    return jnp.logical_and(jnp.logical_and(kj <= qi, qi - kj < BLK), jnp.logical_or(n > 0, kj >= BLK))


def _head_probs(raw_scores, sink, valid):
    sc = jnp.where(valid, raw_scores * (HEAD_DIM ** -0.5), -1e30)
    m = jnp.maximum(jnp.max(sc, axis=-1, keepdims=True), sink)
    p = jnp.exp(sc - m)
    psink = jnp.exp(sink - m)
    inv = 1.0 / (jnp.sum(p, axis=-1, keepdims=True) + psink)
    return p * inv, psink * inv


def _keys_values(kc_ref, kp_ref, vc_ref, vp_ref, kw, ones_k, cosc, sinc, cosp, sinp, which):
    def key(ref, cosf, sins):
        k = ref[...].astype(F32)
        y = k * lax.rsqrt(_group_mean(k * k, ones_k) + EPS) * kw
        return _dup_head(y * cosf + _partner64(y) * sins, which)

    kcat = jnp.concatenate([key(kp_ref, cosp, sinp), key(kc_ref, cosc, sinc)], axis=0).astype(BF16)
    vcat = jnp.concatenate([_dup_head(vp_ref[...].astype(F32), which), _dup_head(vc_ref[...].astype(F32), which)],
                           axis=0).astype(BF16)
    return kcat, vcat


def _swa_specs():
    prev = lambda n: jnp.maximum(n - 1, 0)
    slab = lambda col0, row: pl.BlockSpec((BLK, SLAB), lambda g, n: (row(n), col0 + g // 2))
    cur = lambda n: n
    tab = lambda row: pl.BlockSpec((BLK, SLAB), lambda g, n: (row(n), 0))
    full = lambda shape: pl.BlockSpec(shape, lambda g, n: (0,) * len(shape))
    return dict(
        sinks=pl.BlockSpec(memory_space=pltpu.SMEM),
        q=pl.BlockSpec((BLK, GROUP_WIDTH), lambda g, n: (n, Q_COL + g)),
        gate=pl.BlockSpec((BLK, GROUP_WIDTH), lambda g, n: (n, GATE_COL + g)),
        kc=slab(K_SLAB0, cur), kp=slab(K_SLAB0, prev), vc=slab(V_SLAB0, cur), vp=slab(V_SLAB0, prev),
        tab_c=tab(cur), tab_p=tab(prev),
        qw=full((1, GROUP_WIDTH)), kw=full((1, SLAB)), ones_q=full((GROUP_WIDTH, GROUP_WIDTH)), ones_k=full((SLAB, SLAB)),
        group=pl.BlockSpec((BLK, GROUP_WIDTH), lambda g, n: (n, g)))


def _swa_fwd(proj, sinks, qw, kw, cos, sins, ones_q, ones_k):
    s = proj.shape[0]
    nb = s // BLK
    sp = _swa_specs()

    def body(sink_ref, q_ref, kc_ref, kp_ref, vc_ref, vp_ref, gate_ref, cosc_ref, sinc_ref, cosp_ref, sinp_ref,
             qw_ref, kw_ref, oq_ref, ok_ref, ag_ref, o_ref):
        g, n = pl.program_id(0), pl.program_id(1)
        cosc, sinc = cosc_ref[...], sinc_ref[...]
        kcat, vcat = _keys_values(kc_ref, kp_ref, vc_ref, vp_ref, kw_ref[...], ok_ref[...], cosc, sinc,
                                  cosp_ref[...], sinp_ref[...], g % 2)
        _, _, qr = _q_prep(q_ref[...].astype(F32), qw_ref[...], _tile_lanes(cosc, 4), _tile_lanes(sinc, 4), oq_ref[...])
        scores = _dot_nt(_stack_heads(qr).astype(BF16), kcat)
        valid = _valid_keys(n)
        probs = [_head_probs(scores[j * BLK:(j + 1) * BLK], sink_ref[g * GROUP_HEADS + j], valid)[0].astype(BF16)
                 for j in range(GROUP_HEADS)]
        o = _unstack_heads(_dot(jnp.concatenate(probs, axis=0), vcat))
        gate = gate_ref[...].astype(F32)
        o_ref[...] = o.astype(BF16)
        ag_ref[...] = (o * (gate * _sigmoid(gate))).astype(BF16)

    shp = jax.ShapeDtypeStruct((s, ATTN_WIDTH), BF16)
    return pl.pallas_call(
        body, name="swa_fwd", grid=(KV_HEADS, nb),
        in_specs=[sp["sinks"], sp["q"], sp["kc"], sp["kp"], sp["vc"], sp["vp"], sp["gate"], sp["tab_c"], sp["tab_c"],
                  sp["tab_p"], sp["tab_p"], sp["qw"], sp["kw"], sp["ones_q"], sp["ones_k"]],
        out_specs=[sp["group"], sp["group"]], out_shape=[shp, shp],
        compiler_params=_params("parallel", "arbitrary"),
    )(sinks, proj, proj, proj, proj, proj, proj, cos, sins, cos, sins, qw, kw, ones_q, ones_k)


def _swa_bwd(proj, dag, sinks, qw, kw, cos, sins, ones_q, ones_k):
    s = proj.shape[0]
    nb = s // BLK
    sp = _swa_specs()

    def body(sink_ref, q_ref, kc_ref, kp_ref, vc_ref, vp_ref, gate_ref, dag_ref, cosc_ref, sinc_ref, cosp_ref, sinp_ref,
             qw_ref, kw_ref, oq_ref, ok_ref, dq_ref, dkc_ref, dkp_ref, dvc_ref, dvp_ref, dqw_ref, dsink_ref):
        g, n = pl.program_id(0), pl.program_id(1)
        cosc, sinc = cosc_ref[...], sinc_ref[...]
        kcat, vcat = _keys_values(kc_ref, kp_ref, vc_ref, vp_ref, kw_ref[...], ok_ref[...], cosc, sinc,
                                  cosp_ref[...], sinp_ref[...], g % 2)
        cosq, sinq = _tile_lanes(cosc, 4), _tile_lanes(sinc, 4)
        qwv = qw_ref[...]
        nrm, rstd, qr = _q_prep(q_ref[...].astype(F32), qwv, cosq, sinq, oq_ref[...])
        gate = gate_ref[...].astype(F32)
        do = dag_ref[...].astype(F32) * (gate * _sigmoid(gate))
        q_stack = _stack_heads(qr).astype(BF16)
        do_stack = _stack_heads(do).astype(BF16)
        scores = _dot_nt(q_stack, kcat)
        dprobs = _dot_nt(do_stack, vcat)
        valid = _valid_keys(n)
        probs, dscores, dsinks = [], [], []
        for j in range(GROUP_HEADS):
            rows = slice(j * BLK, (j + 1) * BLK)
            p, psink = _head_probs(scores[rows], sink_ref[g * GROUP_HEADS + j], valid)
            dp = dprobs[rows]
            delta = jnp.sum(p * dp, axis=-1, keepdims=True)
            probs.append(p.astype(BF16))
            dscores.append((p * (dp - delta) * (HEAD_DIM ** -0.5)).astype(BF16))
            dsinks.append(-jnp.sum(psink * delta, axis=0, keepdims=True))
        ds = jnp.concatenate(dscores, axis=0)
        dk = _dot_tn(ds, q_stack)
        dv = _dot_tn(jnp.concatenate(probs, axis=0), do_stack)
        dk = dk + pltpu.roll(dk, HEAD_DIM, 1)
        dv = dv + pltpu.roll(dv, HEAD_DIM, 1)
        dkp_ref[0], dkc_ref[0] = dk[:BLK], dk[BLK:]
        dvp_ref[0], dvc_ref[0] = dv[:BLK], dv[BLK:]
        dqr = _unstack_heads(_dot(ds, kcat))
        dy = dqr * cosq + _partner64(dqr * sinq)
        dn = dy * qwv
        dq_ref[...] = (rstd * (dn - nrm * _group_mean(dn * nrm, oq_ref[...]))).astype(BF16)

        @pl.when(n == 0)
        def _():
            dqw_ref[...] = jnp.zeros_like(dqw_ref)
            dsink_ref[...] = jnp.zeros_like(dsink_ref)

        dqw_ref[0] += jnp.sum(dy * nrm, axis=0, keepdims=True)
        r8 = lax.broadcasted_iota(jnp.int32, (8, SLAB), 0)
        upd = jnp.zeros((8, SLAB), F32)
        for j in range(GROUP_HEADS):
            upd = jnp.where(r8 == j, dsinks[j], upd)
        dsink_ref[0] += upd

    cur_out = pl.BlockSpec((1, BLK, SLAB), lambda g, n: (g, n, 0))
    prev_out = pl.BlockSpec((1, BLK, SLAB), lambda g, n: (g, (n + nb - 1) % nb, 0))
    kv_shape = jax.ShapeDtypeStruct((KV_HEADS, s, SLAB), F32)
    return pl.pallas_call(
        body, name="swa_bwd", grid=(KV_HEADS, nb),
        in_specs=[sp["sinks"], sp["q"], sp["kc"], sp["kp"], sp["vc"], sp["vp"], sp["gate"], sp["group"],
                  sp["tab_c"], sp["tab_c"], sp["tab_p"], sp["tab_p"], sp["qw"], sp["kw"], sp["ones_q"], sp["ones_k"]],
        out_specs=[sp["q"], cur_out, prev_out, cur_out, prev_out,
                   pl.BlockSpec((1, 8, GROUP_WIDTH), lambda g, n: (g, 0, 0)), pl.BlockSpec((1, 8, SLAB), lambda g, n: (g, 0, 0))],
        out_shape=[jax.ShapeDtypeStruct((s, ODD_IN), BF16), kv_shape, kv_shape, kv_shape, kv_shape,
                   jax.ShapeDtypeStruct((KV_HEADS, 8, GROUP_WIDTH), F32), jax.ShapeDtypeStruct((KV_HEADS, 8, SLAB), F32)],
        compiler_params=_params("parallel", "arbitrary"),
    )(sinks, proj, proj, proj, proj, proj, proj, dag, cos, sins, cos, sins, qw, kw, ones_q, ones_k)


def _swa_bwd_kv(proj, dproj, o, dag, dkc, dkp, dvc, dvp, kw, cos, sins, ones_k):
    s = proj.shape[0]
    rows = min(KV_ROWS, s)

    def body(kv_ref, gate_ref, o_ref, dag_ref, dkc_ref, dkp_ref, dvc_ref, dvp_ref, kw_ref, cos_ref, sin_ref, ok_ref,
             dp_in, dp_ref, dkw_ref):
        del dp_in
        i, j = pl.program_id(0), pl.program_id(1)

        @pl.when(jnp.logical_and(i == 0, j == 0))
        def _():
            dkw_ref[...] = jnp.zeros_like(dkw_ref)

        @pl.when(j == 0)
        def _():
            def assemble(cur_ref, prv_ref):
                tot = [cur_ref[h] + prv_ref[h] for h in range(KV_HEADS)]
                first = _half_mask(tot[0].shape, 0)
                return jnp.concatenate([jnp.where(first, tot[0], tot[1]), jnp.where(first, tot[2], tot[3])], axis=1)

            dkr = assemble(dkc_ref, dkp_ref)
            dv = assemble(dvc_ref, dvp_ref)
            cosf, sinf = _tile_lanes(cos_ref[...], 2), _tile_lanes(sin_ref[...], 2)
            k = kv_ref[:, :KV_WIDTH].astype(F32)
            rstd = lax.rsqrt(_group_mean(k * k, ok_ref[...]) + EPS)
            nrm = k * rstd
            dy = dkr * cosf + _partner64(dkr * sinf)
            dn = dy * kw_ref[...]
            dk = rstd * (dn - nrm * _group_mean(dn * nrm, ok_ref[...]))
            dp_ref[...] = jnp.concatenate([dk, dv], axis=1).astype(BF16)
            dkw_ref[...] += jnp.sum(dy * nrm, axis=0, keepdims=True)

        @pl.when(j > 0)
        def _():
            gate = gate_ref[...].astype(F32)
            sg = _sigmoid(gate)
            dp_ref[...] = (dag_ref[...].astype(F32) * o_ref[...].astype(F32) * (sg * (1.0 + gate * (1.0 - sg)))).astype(BF16)

    acc = pl.BlockSpec((KV_HEADS, rows, SLAB), lambda i, j: (0, i, 0))
    group = pl.BlockSpec((rows, GROUP_WIDTH), lambda i, j: (i, jnp.maximum(j - 1, 0)))
    tab = pl.BlockSpec((rows, SLAB), lambda i, j: (i, 0))
    return pl.pallas_call(
        body, name="swa_bwd_kv", grid=(s // rows, 5),
        in_specs=[pl.BlockSpec((rows, GROUP_WIDTH), lambda i, j: (i, KV_COL)),
                  pl.BlockSpec((rows, GROUP_WIDTH), lambda i, j: (i, GATE_COL + jnp.maximum(j - 1, 0))),
                  group, group, acc, acc, acc, acc,
                  pl.BlockSpec((1, KV_WIDTH), lambda i, j: (0, 0)), tab, tab,
                  pl.BlockSpec((KV_WIDTH, KV_WIDTH), lambda i, j: (0, 0)), ANY],
        out_specs=[pl.BlockSpec((rows, GROUP_WIDTH), lambda i, j: (i, KV_COL + j)), pl.BlockSpec((8, KV_WIDTH), lambda i, j: (0, 0))],
        out_shape=[jax.ShapeDtypeStruct(dproj.shape, dproj.dtype), jax.ShapeDtypeStruct((8, KV_WIDTH), F32)],
        input_output_aliases={12: 0},
        compiler_params=_params("arbitrary", "arbitrary"),
    )(proj, proj, o, dag, dkc, dkp, dvc, dvp, kw, cos, sins, ones_k, dproj)


def _place():
    x, y, c = lax.axis_index("x"), lax.axis_index("y"), lax.axis_index("c")
    return x, y, c


OTHER_CHIPS = ((1, 0), (0, 1), (1, 1))


def _half_rows(ref, half, rows):
    return ref.at[pl.ds(pl.multiple_of(half * (rows // 2), 8), rows // 2)]


DMA_CHUNK_BYTES = 1 << 20
BF16_TILE_ROWS = 16


def _n_chunks(ref):
    rows = ref.shape[-2]
    nbytes = math.prod(ref.shape) * jnp.dtype(ref.dtype).itemsize
    n = 1
    while 2 * n * DMA_CHUNK_BYTES <= nbytes and rows % (2 * n * BF16_TILE_ROWS) == 0:
        n *= 2
    return n


def _row_chunk(ref, k, n):
    rows = ref.shape[-2] // n
    return ref.at[pl.ds(k * rows, rows)] if len(ref.shape) == 2 else ref.at[:, pl.ds(k * rows, rows)]


def _push(src, dst, send_sem, recv_sem, device_id):
    n = _n_chunks(src)
    for k in range(n):
        pltpu.make_async_remote_copy(src_ref=_row_chunk(src, k, n), dst_ref=_row_chunk(dst, k, n), send_sem=send_sem,
                                     recv_sem=recv_sem, device_id=device_id, device_id_type=MESH).start()
    return pltpu.make_async_remote_copy(src_ref=src, dst_ref=dst, send_sem=send_sem, recv_sem=recv_sem,
                                        device_id=device_id, device_id_type=MESH)


def _copy(src, dst, sem):
    n = _n_chunks(src)
    for k in range(n):
        pltpu.make_async_copy(_row_chunk(src, k, n), _row_chunk(dst, k, n), sem).start()
    return pltpu.make_async_copy(src, dst, sem)


def _gather_weights(w_in0, w_out0, w_in1, w_out1, small):
    pieces = (
        (w_in0, "col"), (w_out0, "row"), (w_in1, "col"), (w_out1, "row"))
    n_big = len(pieces)

    def body(a0, a1, a2, a3, sm, f0, f1, f2, f3, fsm, ici_send, ici_recv, fwd_send, fwd_recv, small_send, small_recv, local_sem):
        x, y, c = _place()
        my_chip = 2 * x + y
        shards, fulls = (a0, a1, a2, a3), (f0, f1, f2, f3)

        def region(i, chip, half=None):
            rows, width = shards[i].shape
            if pieces[i][1] == "col":
                cols = fulls[i].at[:, pl.ds(pl.multiple_of(chip * width, 128), width)]
                return cols if half is None else _half_rows(cols, half, rows)
            whole = fulls[i].at[pl.ds(pl.multiple_of(chip * rows, 8), rows)]
            return whole if half is None else _half_rows(whole, half, rows)

        sends = []
        for r, (fx, fy) in enumerate(OTHER_CHIPS):
            peer = (x ^ fx, y ^ fy, c)
            for i in range(n_big):
                sends.append(_push(_half_rows(shards[i], c, shards[i].shape[0]), region(i, my_chip, c),
                                   ici_send.at[r * n_big + i], ici_recv.at[r * n_big + i], peer))
            sends.append(_push(sm, fsm.at[my_chip], small_send.at[r], small_recv.at[r], peer))
        local = [_copy(shards[i], region(i, my_chip), local_sem.at[i]) for i in range(n_big)]
        local.append(_copy(sm, fsm.at[my_chip], local_sem.at[n_big]))
        sibling = (x, y, 1 - c)
        for r, (fx, fy) in enumerate(OTHER_CHIPS):
            chip = 2 * (x ^ fx) + (y ^ fy)
            for i in range(n_big):
                landed = region(i, chip, c)
                pltpu.make_async_remote_copy(src_ref=landed, dst_ref=landed, send_sem=ici_send.at[r * n_big + i],
                                             recv_sem=ici_recv.at[r * n_big + i], device_id=sibling, device_id_type=MESH).wait_recv()
                sends.append(_push(landed, landed, fwd_send.at[r * n_big + i], fwd_recv.at[r * n_big + i], sibling))
            pltpu.make_async_remote_copy(src_ref=sm, dst_ref=fsm.at[chip], send_sem=small_send.at[r],
                                         recv_sem=small_recv.at[r], device_id=sibling, device_id_type=MESH).wait_recv()
        for r, (fx, fy) in enumerate(OTHER_CHIPS):
            chip = 2 * (x ^ fx) + (y ^ fy)
            for i in range(n_big):
                theirs = region(i, chip, 1 - c)
                pltpu.make_async_remote_copy(src_ref=theirs, dst_ref=theirs, send_sem=fwd_send.at[r * n_big + i],
                                             recv_sem=fwd_recv.at[r * n_big + i], device_id=sibling, device_id_type=MESH).wait_recv()
        for cp in sends:
            cp.wait_send()
        for cp in local:
            cp.wait()

    def full_shape(w, kind):
        rows, width = w.shape
        return jax.ShapeDtypeStruct((rows, N_CHIPS * width) if kind == "col" else (N_CHIPS * rows, width), w.dtype)

    out_shape = [full_shape(w, kind) for w, kind in pieces] + [jax.ShapeDtypeStruct((N_CHIPS,) + small.shape, small.dtype)]
    dma = pltpu.SemaphoreType.DMA
    return pl.pallas_call(
        body, name="gather_weights", in_specs=[ANY] * 5, out_specs=[ANY] * 5, out_shape=out_shape,
        scratch_shapes=[dma((3 * n_big,)), dma((3 * n_big,)), dma((3 * n_big,)), dma((3 * n_big,)), dma((3,)), dma((3,)),
                        dma((n_big + 1,))],
        compiler_params=pltpu.CompilerParams(has_side_effects=True),
    )(w_in0, w_out0, w_in1, w_out1, small)


def _allreduce_small(v):
    def body(v_ref, out_ref, buf, send_sems, recv_sems):
        x, y, c = _place()
        me = 4 * x + 2 * y + c
        buf[me] = v_ref[...]
        copies = []
        for r in range(1, N_DEV):
            peer = (x ^ (r >> 2), y ^ ((r >> 1) & 1), c ^ (r & 1))
            cp = pltpu.make_async_remote_copy(src_ref=v_ref, dst_ref=buf.at[me], send_sem=send_sems.at[r - 1],
                                              recv_sem=recv_sems.at[r - 1], device_id=peer, device_id_type=MESH)
            cp.start()
            copies.append(cp)
        for cp in copies:
            cp.wait_recv()
        for cp in copies:
            cp.wait_send()
        total = buf[0]
        for d in range(1, N_DEV):
            total = total + buf[d]
        out_ref[...] = total

    vm = pl.BlockSpec(memory_space=pltpu.VMEM)
    return pl.pallas_call(
        body, name="allreduce_small", in_specs=[vm], out_specs=vm, out_shape=jax.ShapeDtypeStruct(v.shape, v.dtype),
        scratch_shapes=[pltpu.VMEM((N_DEV,) + v.shape, v.dtype), pltpu.SemaphoreType.DMA((N_DEV - 1,)),
                        pltpu.SemaphoreType.DMA((N_DEV - 1,))],
        compiler_params=pltpu.CompilerParams(has_side_effects=True),
    )(v)


def _exchange_halves(grads):
    n = len(grads)

    def body(*refs):
        g, theirs = refs[:n], refs[n:2 * n]
        send_sem, recv_sem = refs[2 * n:]
        x, y, c = _place()
        copies = []
        for i in range(n):
            half = g[i].shape[1] // 2
            src = g[i].at[:, pl.ds(pl.multiple_of((1 - c) * half, BF16_TILE_ROWS), half)]
            copies.append(_push(src, theirs[i], send_sem.at[i], recv_sem.at[i], (x, y, 1 - c)))
        for cp in copies:
            cp.wait_recv()
            cp.wait_send()

    dma = pltpu.SemaphoreType.DMA
    return pl.pallas_call(
        body, name="exchange_halves", in_specs=[ANY] * n, out_specs=[ANY] * n,
        out_shape=[jax.ShapeDtypeStruct((a.shape[0], a.shape[1] // 2, a.shape[2]), a.dtype) for a in grads],
        scratch_shapes=[dma((n,)), dma((n,))],
        compiler_params=pltpu.CompilerParams(has_side_effects=True),
    )(*grads)


def _pair_sum(g, theirs, core, name):
    pieces, half, cols = theirs.shape
    tr = min(half, 256)
    per = half // tr

    def body(core_ref, g_ref, t_ref, o_ref):
        del core_ref
        o_ref[...] = (g_ref[...].astype(F32) + t_ref[...].astype(F32)).astype(BF16)

    spec = pl.BlockSpec((1, tr, cols), lambda p, i, core_ref: (p, i, 0))
    return pl.pallas_call(
        body, name=name, out_shape=jax.ShapeDtypeStruct(theirs.shape, BF16),
        grid_spec=pltpu.PrefetchScalarGridSpec(
            num_scalar_prefetch=1, grid=(pieces, per),
            in_specs=[pl.BlockSpec((1, tr, cols), lambda p, i, core_ref: (p, core_ref[0] * per + i, 0)), spec],
            out_specs=spec),
        compiler_params=_params("parallel", "parallel"))(core, g, theirs)


def _scatter_to_owners(parts):
    n = len(parts)

    def body(*refs):
        src, dst = refs[:n], refs[n:2 * n]
        send_sem, recv_sem, local_sem = refs[2 * n:]
        x, y, c = _place()
        my_chip = 2 * x + y

        def piece(i, chip):
            if src[i].shape[0] == N_CHIPS:
                return src[i].at[chip]
            width = src[i].shape[2] // N_CHIPS
            return src[i].at[0].at[:, pl.ds(pl.multiple_of(chip * width, 128), width)]

        remote = []
        for r, (fx, fy) in enumerate(OTHER_CHIPS):
            chip = 2 * (x ^ fx) + (y ^ fy)
            for i in range(n):
                remote.append(_push(piece(i, chip), dst[i].at[my_chip], send_sem.at[r * n + i], recv_sem.at[r * n + i],
                                    (x ^ fx, y ^ fy, c)))
        copies = [_copy(piece(i, my_chip), dst[i].at[my_chip], local_sem.at[i]) for i in range(n)]
        for rc in remote:
            rc.wait_recv()
        for rc in remote:
            rc.wait_send()
        for lc in copies:
            lc.wait()

    def out_shape(a):
        if a.shape[0] == N_CHIPS:
            return jax.ShapeDtypeStruct(a.shape, a.dtype)
        return jax.ShapeDtypeStruct((N_CHIPS, a.shape[1], a.shape[2] // N_CHIPS), a.dtype)

    dma = pltpu.SemaphoreType.DMA
    return pl.pallas_call(
        body, name="scatter_to_owners", in_specs=[ANY] * n, out_specs=[ANY] * n, out_shape=[out_shape(a) for a in parts],
        scratch_shapes=[dma((3 * n,)), dma((3 * n,)), dma((n,))],
        compiler_params=pltpu.CompilerParams(has_side_effects=True),
    )(*parts)


def _sum_chips(stack, core, name):
    _, r, c = stack.shape
    tr = 256
    per = r // tr

    def body(core_ref, s_ref, o_ref):
        del core_ref
        total = s_ref[0].astype(F32)
        for k in range(1, N_CHIPS):
            total = total + s_ref[k].astype(F32)
        o_ref[...] = total

    return pl.pallas_call(
        body, name=name, out_shape=jax.ShapeDtypeStruct((2 * r, c), F32),
        grid_spec=pltpu.PrefetchScalarGridSpec(
            num_scalar_prefetch=1, grid=(per,),
            in_specs=[pl.BlockSpec((N_CHIPS, tr, c), lambda i, core_ref: (0, i, 0))],
            out_specs=pl.BlockSpec((tr, c), lambda i, core_ref: (core_ref[0] * per + i, 0))),
        compiler_params=_params("parallel"))(core, stack)


def _share_halves(fulls):
    n = len(fulls)

    def body(*refs):
        dst = refs[n:2 * n]
        send_sem, recv_sem = refs[2 * n:]
        x, y, c = _place()
        copies = []
        for i in range(n):
            mine = _half_rows(dst[i], c, dst[i].shape[0])
            copies.append(_push(mine, mine, send_sem.at[i], recv_sem.at[i], (x, y, 1 - c)))
        for cp in copies:
            cp.wait_recv()
            cp.wait_send()

    dma = pltpu.SemaphoreType.DMA
    return pl.pallas_call(
        body, name="share_halves", in_specs=[ANY] * n, out_specs=[ANY] * n,
        out_shape=[jax.ShapeDtypeStruct(a.shape, a.dtype) for a in fulls],
        input_output_aliases={i: i for i in range(n)}, scratch_shapes=[dma((n,)), dma((n,))],
        compiler_params=pltpu.CompilerParams(has_side_effects=True),
    )(*fulls)


MM = dict(tm=1024, tn=1024, tk=2048)


def _local_step(x, target, ev_norm_w, w_in0, conv_w, w_out0, od_norm_w, w_in1, q_norm_w, k_norm_w, sinks, w_out1):
    s = x.shape[0]
    cos_r, sin_r = _rope_tables(s, RET_DIM)
    cos_a, sin_a = _rope_tables(s, HEAD_DIM)
    cos_a = jnp.tile(cos_a, (1, 4))
    sins_a = jnp.tile(jnp.concatenate([-sin_a, sin_a], axis=1), (1, 2))
    tables = _retention_tables()
    ones_q, ones_k, ones_kv = _block_diag_mean(GROUP_WIDTH), _block_diag_mean(SLAB), _block_diag_mean(KV_WIDTH)
    qw_g = jnp.tile(q_norm_w, (1, GROUP_WIDTH // HEAD_DIM))
    kw_s = jnp.tile(k_norm_w, (1, SLAB // HEAD_DIM))
    kw_kv = jnp.tile(k_norm_w, (1, KV_WIDTH // HEAD_DIM))
    sinks1 = sinks.reshape(Q_HEADS)

    h0 = _rmsnorm(x, ev_norm_w, "norm0")
    proj0 = _mm(h0, w_in0, mode="nn", out_dtype=BF16, name="proj0", **MM)
    o_ret, cat, states = _retention_fwd(proj0, cos_r, sin_r, tables)
    cat = _conv_fwd(proj0, conv_w, cat)
    x1 = _mm(cat, w_out0, mode="nn", out_dtype=F32, name="out0", add=x, **MM)
    h1 = _rmsnorm(x1, od_norm_w, "norm1")
    proj1 = _mm(h1, w_in1, mode="nn", out_dtype=BF16, name="proj1", tm=1024, tn=1152, tk=2048)
    ag, o_att = _swa_fwd(proj1, sinks1, qw_g, kw_s, cos_a, sins_a, ones_q, ones_k)
    x2 = _mm(ag, w_out1, mode="nn", out_dtype=F32, name="out1", add=x1, **MM)
    dy, dy_b, sq = _loss_grad(x2, target)

    g_w_out1 = _mm(ag, dy_b, mode="tn", out_dtype=BF16, name="g_w_out1", **MM)
    dag = _mm(dy_b, w_out1, mode="nt", out_dtype=BF16, name="d_ag", **MM)
    dproj1, dkc, dkp, dvc, dvp, dqw, dsink = _swa_bwd(proj1, dag, sinks1, qw_g, kw_s, cos_a, sins_a, ones_q, ones_k)
    dproj1, dkw = _swa_bwd_kv(proj1, dproj1, o_att, dag, dkc, dkp, dvc, dvp, kw_kv, cos_a, sins_a, ones_kv)
    g_w_in1 = _mm(h1, dproj1, mode="tn", out_dtype=BF16, name="g_w_in1", tm=1024, tn=1152, tk=2048)
    dh1 = _mm(dproj1, w_in1, mode="nt", out_dtype=BF16, name="d_h1", tm=1024, tn=1024, tk=2304)
    dx1, dx1_b, g_norm1 = _rmsnorm_bwd(x1, od_norm_w, dh1, dy, "norm1_bwd", True)

    g_w_out0 = _mm(cat, dx1_b, mode="tn", out_dtype=BF16, name="g_w_out0", **MM)
    dcat = _mm(dx1_b, w_out0, mode="nt", out_dtype=BF16, name="d_cat", **MM)
    dproj0 = _retention_bwd(proj0, o_ret, states, dcat, cos_r, sin_r, tables)
    dproj0, g_conv = _conv_bwd(proj0, dcat, conv_w, dproj0)
    g_w_in0 = _mm(h0, dproj0, mode="tn", out_dtype=BF16, name="g_w_in0", **MM)
    dh0 = _mm(dproj0, w_in0, mode="nt", out_dtype=BF16, name="d_h0", **MM)
    grad_x, g_norm0 = _rmsnorm_bwd(x, ev_norm_w, dh0, dx1, "norm0_bwd", False)

    g_qw = dqw[:, 0, :].reshape(Q_HEADS, HEAD_DIM).sum(axis=0)
    g_kw = dkw[0].reshape(KV_HEADS, HEAD_DIM).sum(axis=0)
    g_sinks = dsink[:, :, 0].reshape(Q_HEADS)
    small = dict(ev_norm=g_norm0[0], od_norm=g_norm1[0], conv=g_conv[:3], qw=g_qw, kw=g_kw, sinks=g_sinks)
    return sq[0, 0], grad_x, (g_w_in0, g_w_out0, g_w_in1, g_w_out1), small


def _pack_small_grads(small):
    pad = lambda v: jnp.pad(v, (0, D_MODEL - v.shape[0]))
    tail = pad(jnp.concatenate([small["qw"], small["kw"], small["sinks"]]))
    rows = [small["ev_norm"], small["od_norm"]] + [pad(small["conv"][t]) for t in range(3)] + [tail]
    rows += [jnp.zeros((D_MODEL,), F32)] * (8 - len(rows))
    return jnp.stack(rows)


def _reduce_scatter(grads):
    g_w_in0, g_w_out0, g_w_in1, g_w_out1 = grads
    core = lax.axis_index("c").astype(jnp.int32).reshape(1)
    by_rows = lambda g: g.reshape(N_CHIPS, g.shape[0] // N_CHIPS, g.shape[1])
    pieces = [g_w_in0[None], by_rows(g_w_out0), g_w_in1[None], by_rows(g_w_out1)]
    names = ("w_in0", "w_out0", "w_in1", "w_out1")
    theirs = _exchange_halves(pieces)
    parts = [_pair_sum(g, t, core, "pair_sum_" + nm) for g, t, nm in zip(pieces, theirs, names)]
    stacks = _scatter_to_owners(parts)
    return _share_halves([_sum_chips(st, core, "chip_sum_" + nm) for st, nm in zip(stacks, names)])


def kernel(x, ev_norm_w, ev_w_in, ev_conv_w, ev_w_out, od_norm_w, od_w_in, od_q_norm_w, od_k_norm_w, od_sinks, od_w_out, loss_target, m_ev_norm_w, m_ev_w_in, m_ev_conv_w, m_ev_w_out, m_od_norm_w, m_od_w_in, m_od_q_norm_w, m_od_k_norm_w, m_od_sinks, m_od_w_out, v_ev_norm_w, v_ev_w_in, v_ev_conv_w, v_ev_w_out, v_od_norm_w, v_od_w_in, v_od_q_norm_w, v_od_k_norm_w, v_od_sinks, v_od_w_out):
    my_chip = 2 * lax.axis_index("x") + lax.axis_index("y")
    shard_w = D_MODEL // N_CHIPS
    conv_shard = CONV_WIDTH // N_CHIPS

    small_in = jnp.zeros((8, shard_w), F32)
    small_in = small_in.at[0].set(od_norm_w[0]).at[1:4, :conv_shard].set(ev_conv_w[0])
    w_in0, w_out0, w_in1, w_out1, small_all = _gather_weights(
        _cast_bf16(ev_w_in[0], "cast_w_in0"), _cast_bf16(ev_w_out[0], "cast_w_out0"),
        _cast_bf16(od_w_in[0], "cast_w_in1"), _cast_bf16(od_w_out[0], "cast_w_out1"), small_in)
    od_norm_full = small_all[:, 0, :].reshape(1, D_MODEL)
    conv_full = jnp.transpose(small_all[:, 1:4, :conv_shard], (1, 0, 2)).reshape(3, CONV_WIDTH)

    sq, grad_x, big, small = _local_step(x[0], loss_target[0], ev_norm_w, w_in0, conv_full, w_out0, od_norm_full, w_in1,
                                         od_q_norm_w, od_k_norm_w, od_sinks, w_out1)
    loss = lax.psum(0.5 * sq / D_MODEL, ("x", "y", "c"))

    g_ev_w_in, g_ev_w_out, g_od_w_in, g_od_w_out = _reduce_scatter(big)
    tot = _allreduce_small(_pack_small_grads(small))
    g_ev_norm = tot[0:1]
    g_od_norm = lax.dynamic_slice(tot, (1, my_chip * shard_w), (1, shard_w))
    g_conv = lax.dynamic_slice(tot, (2, my_chip * conv_shard), (3, conv_shard))
    g_qw, g_kw, g_sinks = tot[5:6, 0:HEAD_DIM], tot[5:6, HEAD_DIM:2 * HEAD_DIM], tot[5:6, 2 * HEAD_DIM:2 * HEAD_DIM + Q_HEADS]

    upd = {}
    upd["ev_w_in"] = _adamw(ev_w_in[0], g_ev_w_in, m_ev_w_in[0], v_ev_w_in[0], "adamw_ev_w_in")
    upd["ev_w_out"] = _adamw(ev_w_out[0], g_ev_w_out, m_ev_w_out[0], v_ev_w_out[0], "adamw_ev_w_out")
    upd["od_w_in"] = _adamw(od_w_in[0], g_od_w_in, m_od_w_in[0], v_od_w_in[0], "adamw_od_w_in")
    upd["od_w_out"] = _adamw(od_w_out[0], g_od_w_out, m_od_w_out[0], v_od_w_out[0], "adamw_od_w_out")
    smalls = (("ev_norm_w", ev_norm_w, g_ev_norm, m_ev_norm_w, v_ev_norm_w),
              ("ev_conv_w", ev_conv_w, g_conv, m_ev_conv_w, v_ev_conv_w),
              ("od_norm_w", od_norm_w, g_od_norm, m_od_norm_w, v_od_norm_w),
              ("od_q_norm_w", od_q_norm_w, g_qw, m_od_q_norm_w, v_od_q_norm_w),
              ("od_k_norm_w", od_k_norm_w, g_kw, m_od_k_norm_w, v_od_k_norm_w),
              ("od_sinks", od_sinks, g_sinks, m_od_sinks, v_od_sinks))
    sizes = [w.size for _, w, _, _, _ in smalls]
    padded = 8 * 128 * math.ceil(sum(sizes) / (8 * 128))
    pack = lambda arrs, fill: jnp.concatenate(
        [a.reshape(-1) for a in arrs] + [jnp.full((padded - sum(sizes),), fill, F32)]).reshape(8, padded // 8)
    packed = _adamw(pack([w for _, w, _, _, _ in smalls], 0.0), pack([g for _, _, g, _, _ in smalls], 0.0),
                    pack([m for _, _, _, m, _ in smalls], 0.0), pack([v for _, _, _, _, v in smalls], 1.0), "adamw_small")
    offs = [sum(sizes[:i]) for i in range(len(sizes))]
    grads = {"ev_w_in": g_ev_w_in[None], "ev_w_out": g_ev_w_out[None], "od_w_in": g_od_w_in[None], "od_w_out": g_od_w_out[None]}
    for (nm, w, g, _, _), off, size in zip(smalls, offs, sizes):
        upd[nm] = tuple(p.reshape(-1)[off:off + size].reshape(w.shape) for p in packed)
        grads[nm] = g.reshape(w.shape)
    for nm in ("ev_w_in", "ev_w_out", "od_w_in", "od_w_out"):
        upd[nm] = tuple(u[None] for u in upd[nm])
    order = ("ev_norm_w", "ev_w_in", "ev_conv_w", "ev_w_out", "od_norm_w", "od_w_in", "od_q_norm_w", "od_k_norm_w", "od_sinks", "od_w_out")
    return (loss, grad_x[None], *[grads[nm] for nm in order], *[upd[nm][0] for nm in order],
            *[upd[nm][1] for nm in order], *[upd[nm][2] for nm in order])
```

```python
import functools
import math

import jax
import jax.numpy as jnp
from jax import lax
from jax.experimental import pallas as pl
from jax.experimental.pallas import tpu as pltpu

F32 = jnp.float32
BF16 = jnp.bfloat16

D_MODEL = 2048
RET_HEADS = 4
RET_DIM = 256
RET_WIDTH = 1024
CONV_WIDTH = 1024
EVEN_IN = 8192
Q_HEADS = 32
HEAD_DIM = 64
KV_HEADS = 4
KV_WIDTH = 256
ATTN_WIDTH = 2048
ODD_IN = 4608
BLK = 128
ROPE_THETA = 10000.0
EPS = 1e-6
ADAM_LR = 0.001
ADAM_B1 = 0.9
ADAM_B2 = 0.999
ADAM_EPS = 1e-08
ADAM_WD = 0.01
ADAM_STEP = 10
N_CHIPS = 4
N_DEV = 8
VMEM_LIMIT_BYTES = 48 * 1024 * 1024
MESH = pl.DeviceIdType.MESH
ANY = pl.BlockSpec(memory_space=pl.ANY)


def _params(*sem):
    return pltpu.CompilerParams(dimension_semantics=sem, vmem_limit_bytes=VMEM_LIMIT_BYTES)


def _dot(a, b):
    return jnp.dot(a, b, preferred_element_type=F32)


def _dot_nt(a, b):
    return lax.dot_general(a, b, (((1,), (1,)), ((), ())), preferred_element_type=F32)


def _dot_tn(a, b):
    return lax.dot_general(a, b, (((0,), (0,)), ((), ())), preferred_element_type=F32)


def _sigmoid(x):
    return 1.0 / (1.0 + jnp.exp(-x))


def _mm(a, b, *, mode, tm, tn, tk, out_dtype, name, add=None, after=None):
    if mode == "nn":
        (m, k), n = a.shape, b.shape[1]
    elif mode == "nt":
        (m, k), n = a.shape, b.shape[0]
    else:
        (k, m), n = a.shape, b.shape[1]
    tm, tn, tk = min(tm, m), min(tn, n), min(tk, k)
    assert m % tm == 0 and n % tn == 0 and k % tk == 0, (name, m, n, k)
    nk = k // tk
    dot = {"nn": _dot, "nt": _dot_nt, "tn": _dot_tn}[mode]
    a_spec = (pl.BlockSpec((tk, tm), lambda i, j, kk: (kk, i)) if mode == "tn"
              else pl.BlockSpec((tm, tk), lambda i, j, kk: (i, kk)))
    b_spec = (pl.BlockSpec((tn, tk), lambda i, j, kk: (j, kk)) if mode == "nt"
              else pl.BlockSpec((tk, tn), lambda i, j, kk: (kk, j)))
    o_spec = pl.BlockSpec((tm, tn), lambda i, j, kk: (i, j))
    has_add = add is not None

    def body(*refs):
        a_ref, b_ref = refs[0], refs[1]
        add_ref = refs[2] if has_add else None
        o_ref, acc_ref = refs[-2], refs[-1]
        p = dot(a_ref[...], b_ref[...])

        def finish(total):
            if has_add:
                total = total + add_ref[...].astype(F32)
            o_ref[...] = total.astype(out_dtype)

        if nk == 1:
            finish(p)
        else:
            kk = pl.program_id(2)

            @pl.when(kk == 0)
            def _():
                acc_ref[...] = p

            @pl.when(jnp.logical_and(kk > 0, kk < nk - 1))
            def _():
                acc_ref[...] += p

            @pl.when(kk == nk - 1)
            def _():
                finish(acc_ref[...] + p)

    in_specs = [a_spec, b_spec] + ([o_spec] if has_add else []) + ([ANY] if after is not None else [])
    args = (a, b) + ((add,) if has_add else ()) + ((after,) if after is not None else ())
    return pl.pallas_call(
        body, name=name, grid=(m // tm, n // tn, nk), in_specs=in_specs, out_specs=o_spec,
        out_shape=jax.ShapeDtypeStruct((m, n), out_dtype),
        scratch_shapes=[pltpu.VMEM((tm, tn) if nk > 1 else (8, 128), F32)],
        compiler_params=_params("parallel", "parallel", "arbitrary"),
    )(*args)


def _cast_bf16(w, name):
    r, c = w.shape
    tr = min(r, 512)

    def body(w_ref, o_ref):
        o_ref[...] = w_ref[...].astype(BF16)

    spec = pl.BlockSpec((tr, c), lambda i: (i, 0))
    return pl.pallas_call(body, name=name, grid=(r // tr,), in_specs=[spec], out_specs=spec,
                          out_shape=jax.ShapeDtypeStruct((r, c), BF16), compiler_params=_params("parallel"))(w)


def _rmsnorm(x, w, name, after=None):
    s, d = x.shape
    tr = 256

    def body(x_ref, w_ref, *rest):
        xv = x_ref[...]
        rstd = lax.rsqrt(jnp.mean(xv * xv, axis=-1, keepdims=True) + EPS)
        rest[-1][...] = (xv * rstd * w_ref[...]).astype(BF16)

    return pl.pallas_call(
        body, name=name, grid=(s // tr,),
        in_specs=[pl.BlockSpec((tr, d), lambda i: (i, 0)), pl.BlockSpec((1, d), lambda i: (0, 0))]
        + ([ANY] if after is not None else []),
        out_specs=pl.BlockSpec((tr, d), lambda i: (i, 0)),
        out_shape=jax.ShapeDtypeStruct((s, d), BF16), compiler_params=_params("parallel"),
    )(x, w, *((after,) if after is not None else ()))


def _rmsnorm_bwd(x, w, dh, dres, name, with_bf16):
    s, d = x.shape
    tr = 256

    def body(x_ref, w_ref, dh_ref, dres_ref, *outs):
        dx_ref, dw_ref = outs[0], outs[-1]
        xv = x_ref[...]
        rstd = lax.rsqrt(jnp.mean(xv * xv, axis=-1, keepdims=True) + EPS)
        nrm = xv * rstd
        dhv = dh_ref[...].astype(F32)
        dn = dhv * w_ref[...]
        dx = dres_ref[...] + rstd * (dn - nrm * jnp.mean(dn * nrm, axis=-1, keepdims=True))
        dx_ref[...] = dx
        if with_bf16:
            outs[1][...] = dx.astype(BF16)

        @pl.when(pl.program_id(0) == 0)
        def _():
            dw_ref[...] = jnp.zeros_like(dw_ref)

        dw_ref[...] += jnp.sum(dhv * nrm, axis=0, keepdims=True)

    row = pl.BlockSpec((tr, d), lambda i: (i, 0))
    out_shape = [jax.ShapeDtypeStruct((s, d), F32)] + ([jax.ShapeDtypeStruct((s, d), BF16)] if with_bf16 else [])
    out_shape.append(jax.ShapeDtypeStruct((8, d), F32))
    out_specs = [row] * (2 if with_bf16 else 1) + [pl.BlockSpec((8, d), lambda i: (0, 0))]
    return pl.pallas_call(
        body, name=name, grid=(s // tr,),
        in_specs=[row, pl.BlockSpec((1, d), lambda i: (0, 0)), row, row],
        out_specs=out_specs, out_shape=out_shape, compiler_params=_params("arbitrary"))(x, w, dh, dres)


def _loss_grad(y, target):
    s, d = y.shape
    tr = 256

    def body(y_ref, t_ref, dy_ref, dyb_ref, l_ref):
        diff = y_ref[...] - t_ref[...]
        dy = diff * (1.0 / d)
        dy_ref[...] = dy
        dyb_ref[...] = dy.astype(BF16)

        @pl.when(pl.program_id(0) == 0)
        def _():
            l_ref[...] = jnp.zeros_like(l_ref)

        l_ref[...] += jnp.sum(jnp.sum(diff * diff, axis=1, keepdims=True), axis=0, keepdims=True)

    row = pl.BlockSpec((tr, d), lambda i: (i, 0))
    return pl.pallas_call(
        body, name="loss_grad", grid=(s // tr,), in_specs=[row, row],
        out_specs=[row, row, pl.BlockSpec((8, 128), lambda i: (0, 0))],
        out_shape=[jax.ShapeDtypeStruct((s, d), F32), jax.ShapeDtypeStruct((s, d), BF16),
                   jax.ShapeDtypeStruct((8, 128), F32)],
        compiler_params=_params("arbitrary"))(y, target)


def _adamw(w, g, m, v, name):
    r, c = w.shape
    tr = min(r, 256)
    assert r % tr == 0

    def body(w_ref, g_ref, m_ref, v_ref, d_ref, nm_ref, nv_ref):
        gv = g_ref[...]
        nm = ADAM_B1 * m_ref[...] + (1.0 - ADAM_B1) * gv
        nv = ADAM_B2 * v_ref[...] + (1.0 - ADAM_B2) * (gv * gv)
        m_hat = nm / (1.0 - ADAM_B1 ** ADAM_STEP)
        v_hat = nv / (1.0 - ADAM_B2 ** ADAM_STEP)
        d_ref[...] = -ADAM_LR * (m_hat / (jnp.sqrt(v_hat) + ADAM_EPS) + ADAM_WD * w_ref[...])
        nm_ref[...] = nm
        nv_ref[...] = nv

    spec = pl.BlockSpec((tr, c), lambda i: (i, 0))
    shp = jax.ShapeDtypeStruct((r, c), F32)
    return pl.pallas_call(body, name=name, grid=(r // tr,), in_specs=[spec] * 4, out_specs=[spec] * 3,
                          out_shape=[shp] * 3, compiler_params=_params("parallel"))(w, g, m, v)


def _rope_tables(s, dim):
    inv = 1.0 / (ROPE_THETA ** (jnp.arange(0, dim, 2, dtype=F32) / dim))
    ang = jnp.arange(s).astype(F32)[:, None] * inv[None, :]
    return jnp.cos(ang), jnp.sin(ang)


def _rope_half(x, cos, sin):
    h = x.shape[1] // 2
    x1, x2 = x[:, :h], x[:, h:]
    return jnp.concatenate([x1 * cos - x2 * sin, x2 * cos + x1 * sin], axis=1)


def _unrope_half(dy, cos, sin):
    h = dy.shape[1] // 2
    d1, d2 = dy[:, :h], dy[:, h:]
    return jnp.concatenate([d1 * cos + d2 * sin, d2 * cos - d1 * sin], axis=1)


def _lane(shape):
    return lax.broadcasted_iota(jnp.int32, shape, 1)


def _partner64(x):
    w = x.shape[1]
    first = (_lane(x.shape) % HEAD_DIM) < (HEAD_DIM // 2)
    return jnp.where(first, pltpu.roll(x, w - HEAD_DIM // 2, 1), pltpu.roll(x, HEAD_DIM // 2, 1))


def _tile_lanes(t, reps):
    return t if reps == 1 else jnp.concatenate([t] * reps, axis=1)


def _group_mean(x, ones_bd):
    hi = x.astype(BF16)
    lo = (x - hi.astype(F32)).astype(BF16)
    return _dot(hi, ones_bd) + _dot(lo, ones_bd)


def _block_diag_mean(width):
    idx = jnp.arange(width) // HEAD_DIM
    return jnp.where(idx[:, None] == idx[None, :], 1.0 / HEAD_DIM, 0.0).astype(BF16)


def _retention_tables():
    h = RET_HEADS
    log_g = jnp.log(1.0 - 2.0 ** (-5.0 - jnp.arange(h, dtype=F32)))
    idx = jnp.arange(BLK, dtype=F32)
    diff = idx[:, None] - idx[None, :]
    intra = jnp.where(diff >= 0, jnp.exp(log_g[:, None, None] * jnp.maximum(diff, 0.0)), 0.0).astype(F32)
    q_dec = jnp.exp(log_g[:, None] * (idx[None, :] + 1.0)).astype(F32)[:, :, None]
    k_dec = jnp.exp(log_g[:, None] * (BLK - 1.0 - idx[None, :])).astype(F32)[:, :, None]
    chunk_dec = jnp.exp(log_g * BLK).astype(F32)[:, None, None]
    return intra, q_dec, k_dec, chunk_dec


def _retention_fwd(proj, cos, sin, tables):
    s = proj.shape[0]
    nc = s // BLK
    intra, q_dec, k_dec, chunk_dec = tables

    def body(p_ref, cos_ref, sin_ref, in_ref, qd_ref, kd_ref, cd_ref, o_ref, cat_ref, st_ref, state):
        @pl.when(pl.program_id(0) == 0)
        def _():
            state[...] = jnp.zeros_like(state)

        cosv, sinv = cos_ref[...], sin_ref[...]
        for h in range(RET_HEADS):
            c0 = h * RET_DIM
            q = p_ref[:, c0:c0 + RET_DIM].astype(F32)
            k = p_ref[:, RET_WIDTH + c0:RET_WIDTH + c0 + RET_DIM].astype(F32)
            v = p_ref[:, 2 * RET_WIDTH + c0:2 * RET_WIDTH + c0 + RET_DIM]
            g = p_ref[:, 3 * RET_WIDTH + c0:3 * RET_WIDTH + c0 + RET_DIM].astype(F32)
            qb = _rope_half(q, cosv, sinv).astype(BF16)
            kr = _rope_half(k, cosv, sinv) * (RET_DIM ** -0.5)
            kb = kr.astype(BF16)
            scores = _dot_nt(qb, kb) * in_ref[h]
            inner = _dot(scores.astype(BF16), v)
            prev = state[h]
            prev_b = prev.astype(BF16)
            st_ref[h, 0] = prev_b
            o = inner + _dot(qb, prev_b) * qd_ref[h]
            o_ref[:, c0:c0 + RET_DIM] = o
            rstd = lax.rsqrt(jnp.mean(o * o, axis=-1, keepdims=True) + EPS)
            cat_ref[:, c0:c0 + RET_DIM] = (o * rstd * (g * _sigmoid(g))).astype(BF16)
            state[h] = cd_ref[h] * prev + _dot_tn((kr * kd_ref[h]).astype(BF16), v)

    full = lambda shape: pl.BlockSpec(shape, lambda n: (0,) * len(shape))
    return pl.pallas_call(
        body, name="retention_fwd", grid=(nc,),
        in_specs=[pl.BlockSpec((BLK, 4 * RET_WIDTH), lambda n: (n, 0)),
                  pl.BlockSpec((BLK, RET_DIM // 2), lambda n: (n, 0)), pl.BlockSpec((BLK, RET_DIM // 2), lambda n: (n, 0)),
                  full((RET_HEADS, BLK, BLK)), full((RET_HEADS, BLK, 1)), full((RET_HEADS, BLK, 1)), full((RET_HEADS, 1, 1))],
        out_specs=[pl.BlockSpec((BLK, RET_WIDTH), lambda n: (n, 0)), pl.BlockSpec((BLK, RET_WIDTH), lambda n: (n, 0)),
                   pl.BlockSpec((RET_HEADS, 1, RET_DIM, RET_DIM), lambda n: (0, n, 0, 0))],
        out_shape=[jax.ShapeDtypeStruct((s, RET_WIDTH), F32), jax.ShapeDtypeStruct((s, D_MODEL), BF16),
                   jax.ShapeDtypeStruct((RET_HEADS, nc, RET_DIM, RET_DIM), BF16)],
        scratch_shapes=[pltpu.VMEM((RET_HEADS, RET_DIM, RET_DIM), F32)],
        compiler_params=_params("arbitrary"))(proj, cos, sin, intra, q_dec, k_dec, chunk_dec)


def _retention_bwd(proj, o, states, dcat, cos, sin, tables):
    s = proj.shape[0]
    nc = s // BLK
    intra, q_dec, k_dec, chunk_dec = tables

    def body(p_ref, o_ref, st_ref, dc_ref, cos_ref, sin_ref, in_ref, qd_ref, kd_ref, cd_ref, dp_ref, dstate):
        @pl.when(pl.program_id(0) == 0)
        def _():
            dstate[...] = jnp.zeros_like(dstate)

        cosv, sinv = cos_ref[...], sin_ref[...]
        for h in range(RET_HEADS):
            c0 = h * RET_DIM
            q = p_ref[:, c0:c0 + RET_DIM].astype(F32)
            k = p_ref[:, RET_WIDTH + c0:RET_WIDTH + c0 + RET_DIM].astype(F32)
            v = p_ref[:, 2 * RET_WIDTH + c0:2 * RET_WIDTH + c0 + RET_DIM]
            g = p_ref[:, 3 * RET_WIDTH + c0:3 * RET_WIDTH + c0 + RET_DIM].astype(F32)
            o = o_ref[:, c0:c0 + RET_DIM]
            dc = dc_ref[:, c0:c0 + RET_DIM].astype(F32)
            rstd = lax.rsqrt(jnp.mean(o * o, axis=-1, keepdims=True) + EPS)
            nrm = o * rstd
            sg = _sigmoid(g)
            dg = dc * nrm * (sg * (1.0 + g * (1.0 - sg)))
            dn = dc * (g * sg)
            do = rstd * (dn - nrm * jnp.mean(dn * nrm, axis=-1, keepdims=True))
            qb = _rope_half(q, cosv, sinv).astype(BF16)
            kr = _rope_half(k, cosv, sinv) * (RET_DIM ** -0.5)
            kb = kr.astype(BF16)
            mask = in_ref[h]
            qd, kd = qd_ref[h], kd_ref[h]
            prev_b = st_ref[h, 0]
            dnext = dstate[h]
            dnext_b = dnext.astype(BF16)
            att = (_dot_nt(qb, kb) * mask).astype(BF16)
            do_b = do.astype(BF16)
            doq = (do * qd).astype(BF16)
            dv = _dot_tn(att, do_b) + _dot((kr * kd).astype(BF16), dnext_b)
            ds = (_dot_nt(do_b, v) * mask).astype(BF16)
            dqr = _dot(ds, kb) + _dot_nt(doq, prev_b)
            dkr = _dot_tn(ds, qb) + _dot_nt(v, dnext_b) * kd
            dstate[h] = cd_ref[h] * dnext + _dot_tn(qb, doq)
            dq = _unrope_half(dqr, cosv, sinv)
            dk = _unrope_half(dkr * (RET_DIM ** -0.5), cosv, sinv)
            dp_ref[:, c0:c0 + RET_DIM] = dq.astype(BF16)
            dp_ref[:, RET_WIDTH + c0:RET_WIDTH + c0 + RET_DIM] = dk.astype(BF16)
            dp_ref[:, 2 * RET_WIDTH + c0:2 * RET_WIDTH + c0 + RET_DIM] = dv.astype(BF16)
            dp_ref[:, 3 * RET_WIDTH + c0:3 * RET_WIDTH + c0 + RET_DIM] = dg.astype(BF16)

    rev = lambda n: nc - 1 - n
    full = lambda shape: pl.BlockSpec(shape, lambda n: (0,) * len(shape))
    return pl.pallas_call(
        body, name="retention_bwd", grid=(nc,),
        in_specs=[pl.BlockSpec((BLK, 4 * RET_WIDTH), lambda n: (rev(n), 0)),
                  pl.BlockSpec((BLK, RET_WIDTH), lambda n: (rev(n), 0)),
                  pl.BlockSpec((RET_HEADS, 1, RET_DIM, RET_DIM), lambda n: (0, rev(n), 0, 0)),
                  pl.BlockSpec((BLK, RET_WIDTH), lambda n: (rev(n), 0)),
                  pl.BlockSpec((BLK, RET_DIM // 2), lambda n: (rev(n), 0)), pl.BlockSpec((BLK, RET_DIM // 2), lambda n: (rev(n), 0)),
                  full((RET_HEADS, BLK, BLK)), full((RET_HEADS, BLK, 1)), full((RET_HEADS, BLK, 1)), full((RET_HEADS, 1, 1))],
        out_specs=pl.BlockSpec((BLK, 4 * RET_WIDTH), lambda n: (rev(n), 0)),
        out_shape=jax.ShapeDtypeStruct((s, EVEN_IN), BF16),
        scratch_shapes=[pltpu.VMEM((RET_HEADS, RET_DIM, RET_DIM), F32)],
        compiler_params=_params("arbitrary"))(proj, o, states, dcat, cos, sin, intra, q_dec, k_dec, chunk_dec)


CONV_ROWS = 256
HALO = 16


def _conv_pieces(p, halo, conv_w, first):
    rows = p.shape[0]
    gb, gc, u, gv = (p[:, i * CONV_WIDTH:(i + 1) * CONV_WIDTH] for i in range(4))
    cu = gc * u
    hcu = halo[:, CONV_WIDTH:2 * CONV_WIDTH] * halo[:, 2 * CONV_WIDTH:3 * CONV_WIDTH]
    hcu = jnp.where(first, 0.0, hcu)
    r1, r2 = hcu[HALO - 1:HALO], hcu[HALO - 2:HALO - 1]
    row = lax.broadcasted_iota(jnp.int32, cu.shape, 0)
    m1 = jnp.where(row == 0, r1, pltpu.roll(cu, 1, 0))
    m2 = jnp.where(row == 0, r2, jnp.where(row == 1, r1, pltpu.roll(cu, 2, 0)))
    del rows, conv_w
    return gb, gc, u, gv, cu, m1, m2


def _conv_fwd(proj, conv_w, cat):
    s = proj.shape[0]
    per = CONV_ROWS // HALO

    def body(p_ref, halo_ref, w_ref, cat_in, cat_ref):
        del cat_in
        first = pl.program_id(0) == 0
        gb, _, _, gv, cu, m1, m2 = _conv_pieces(p_ref[...].astype(F32), halo_ref[...].astype(F32), None, first)
        conv = w_ref[0:1, :] * m2 + w_ref[1:2, :] * m1 + w_ref[2:3, :] * cu
        cat_ref[...] = (gb * conv * (gv * _sigmoid(gv))).astype(BF16)

    return pl.pallas_call(
        body, name="conv_fwd", grid=(s // CONV_ROWS,),
        in_specs=[pl.BlockSpec((CONV_ROWS, 4 * CONV_WIDTH), lambda i: (i, 1)),
                  pl.BlockSpec((HALO, 4 * CONV_WIDTH), lambda i: (jnp.maximum(i * per - 1, 0), 1)),
                  pl.BlockSpec((3, CONV_WIDTH), lambda i: (0, 0)), ANY],
        out_specs=pl.BlockSpec((CONV_ROWS, CONV_WIDTH), lambda i: (i, 1)),
        out_shape=jax.ShapeDtypeStruct(cat.shape, cat.dtype), input_output_aliases={3: 0},
        compiler_params=_params("parallel"))(proj, proj, conv_w, cat)


def _conv_bwd(proj, dcat, conv_w, dproj):
    s = proj.shape[0]
    per = CONV_ROWS // HALO
    last_halo = s // HALO - 1
    nsteps = s // CONV_ROWS

    def body(p_ref, halo_ref, nxt_ref, dc_ref, dnxt_ref, w_ref, dp_in, dp_ref, dw_ref):
        del dp_in
        i = pl.program_id(0)
        gb, gc, u, gv, cu, m1, m2 = _conv_pieces(p_ref[...].astype(F32), halo_ref[...].astype(F32), None, i == 0)
        w0, w1, w2 = w_ref[0:1, :], w_ref[1:2, :], w_ref[2:3, :]
        conv = w0 * m2 + w1 * m1 + w2 * cu
        dco = dc_ref[...].astype(F32)
        sg = _sigmoid(gv)
        silu = gv * sg
        dgb = dco * conv * silu
        dgv = dco * gb * conv * (sg * (1.0 + gv * (1.0 - sg)))
        dconv = dco * gb * silu
        nxt = nxt_ref[...].astype(F32)
        ngv = nxt[:, 3 * CONV_WIDTH:]
        dnext = dnxt_ref[...].astype(F32) * nxt[:, :CONV_WIDTH] * (ngv * _sigmoid(ngv))
        dnext = jnp.where(i == nsteps - 1, 0.0, dnext)
        n1, n2 = dnext[0:1], dnext[1:2]
        row = lax.broadcasted_iota(jnp.int32, dconv.shape, 0)
        p1 = jnp.where(row == CONV_ROWS - 1, n1, pltpu.roll(dconv, CONV_ROWS - 1, 0))
        p2 = jnp.where(row == CONV_ROWS - 1, n2, jnp.where(row == CONV_ROWS - 2, n1, pltpu.roll(dconv, CONV_ROWS - 2, 0)))
        dcu = w2 * dconv + w1 * p1 + w0 * p2
        dp_ref[...] = jnp.concatenate([dgb, dcu * u, dcu * gc, dgv], axis=1).astype(BF16)

        @pl.when(i == 0)
        def _():
            dw_ref[...] = jnp.zeros_like(dw_ref)

        taps = [jnp.sum(dconv * m, axis=0, keepdims=True) for m in (m2, m1, cu)]
        r8 = lax.broadcasted_iota(jnp.int32, dw_ref.shape, 0)
        dw_ref[...] += jnp.where(r8 == 0, taps[0], jnp.where(r8 == 1, taps[1], jnp.where(r8 == 2, taps[2], 0.0)))

    return pl.pallas_call(
        body, name="conv_bwd", grid=(nsteps,),
        in_specs=[pl.BlockSpec((CONV_ROWS, 4 * CONV_WIDTH), lambda i: (i, 1)),
                  pl.BlockSpec((HALO, 4 * CONV_WIDTH), lambda i: (jnp.maximum(i * per - 1, 0), 1)),
                  pl.BlockSpec((HALO, 4 * CONV_WIDTH), lambda i: (jnp.minimum((i + 1) * per, last_halo), 1)),
                  pl.BlockSpec((CONV_ROWS, CONV_WIDTH), lambda i: (i, 1)),
                  pl.BlockSpec((HALO, CONV_WIDTH), lambda i: (jnp.minimum((i + 1) * per, last_halo), 1)),
                  pl.BlockSpec((3, CONV_WIDTH), lambda i: (0, 0)), ANY],
        out_specs=[pl.BlockSpec((CONV_ROWS, 4 * CONV_WIDTH), lambda i: (i, 1)), pl.BlockSpec((8, CONV_WIDTH), lambda i: (0, 0))],
        out_shape=[jax.ShapeDtypeStruct(dproj.shape, dproj.dtype), jax.ShapeDtypeStruct((8, CONV_WIDTH), F32)],
        input_output_aliases={6: 0},
        compiler_params=_params("arbitrary"))(proj, proj, proj, dcat, dcat, conv_w, dproj)


GROUP_WIDTH = 8 * HEAD_DIM
GROUP_HEADS = 8
SLAB = 128
Q_COL = 0
KV_COL = 4
GATE_COL = 5
K_SLAB0 = ATTN_WIDTH // SLAB
V_SLAB0 = (ATTN_WIDTH + KV_WIDTH) // SLAB
KV_ROWS = 512


def _half_mask(shape, which):
    return (_lane(shape) // HEAD_DIM) == which


def _dup_head(slab, which):
    kept = jnp.where(_half_mask(slab.shape, which), slab, 0.0)
    return kept + pltpu.roll(kept, HEAD_DIM, 1)


def _stack_heads(x):
    parts = []
    for sl in range(GROUP_WIDTH // SLAB):
        slab = x[:, sl * SLAB:(sl + 1) * SLAB]
        parts += [jnp.where(_half_mask(slab.shape, e), slab, 0.0) for e in range(2)]
    return jnp.concatenate(parts, axis=0)


def _unstack_heads(y):
    slabs = []
    for sl in range(GROUP_WIDTH // SLAB):
        a, b = y[(2 * sl) * BLK:(2 * sl + 1) * BLK], y[(2 * sl + 1) * BLK:(2 * sl + 2) * BLK]
        slabs.append(jnp.where(_half_mask(a.shape, 0), a, b))
    return jnp.concatenate(slabs, axis=1)


def _q_prep(q, qw, cosf, sins, ones_bd):
    rstd = lax.rsqrt(_group_mean(q * q, ones_bd) + EPS)
    nrm = q * rstd
    y = nrm * qw
    return nrm, rstd, y * cosf + _partner64(y) * sins


def _valid_keys(n):
    qi = lax.broadcasted_iota(jnp.int32, (BLK, 2 * BLK), 0) + BLK
    kj = lax.broadcasted_iota(jnp.int32, (BLK, 2 * BLK), 1)
    return jnp.logical_and(jnp.logical_and(kj <= qi, qi - kj < BLK), jnp.logical_or(n > 0, kj >= BLK))


def _head_probs(raw_scores, sink, valid):
    sc = jnp.where(valid, raw_scores * (HEAD_DIM ** -0.5), -1e30)
    m = jnp.maximum(jnp.max(sc, axis=-1, keepdims=True), sink)
    p = jnp.exp(sc - m)
    psink = jnp.exp(sink - m)
    inv = 1.0 / (jnp.sum(p, axis=-1, keepdims=True) + psink)
    return p * inv, psink * inv


def _keys_values(kc_ref, kp_ref, vc_ref, vp_ref, kw, ones_k, cosc, sinc, cosp, sinp, which):
    def key(ref, cosf, sins):
        k = ref[...].astype(F32)
        y = k * lax.rsqrt(_group_mean(k * k, ones_k) + EPS) * kw
        return _dup_head(y * cosf + _partner64(y) * sins, which)

    kcat = jnp.concatenate([key(kp_ref, cosp, sinp), key(kc_ref, cosc, sinc)], axis=0).astype(BF16)
    vcat = jnp.concatenate([_dup_head(vp_ref[...].astype(F32), which), _dup_head(vc_ref[...].astype(F32), which)],
                           axis=0).astype(BF16)
    return kcat, vcat


def _swa_specs():
    prev = lambda n: jnp.maximum(n - 1, 0)
    slab = lambda col0, row: pl.BlockSpec((BLK, SLAB), lambda g, n: (row(n), col0 + g // 2))
    cur = lambda n: n
    tab = lambda row: pl.BlockSpec((BLK, SLAB), lambda g, n: (row(n), 0))
    full = lambda shape: pl.BlockSpec(shape, lambda g, n: (0,) * len(shape))
    return dict(
        sinks=pl.BlockSpec(memory_space=pltpu.SMEM),
        q=pl.BlockSpec((BLK, GROUP_WIDTH), lambda g, n: (n, Q_COL + g)),
        gate=pl.BlockSpec((BLK, GROUP_WIDTH), lambda g, n: (n, GATE_COL + g)),
        kc=slab(K_SLAB0, cur), kp=slab(K_SLAB0, prev), vc=slab(V_SLAB0, cur), vp=slab(V_SLAB0, prev),
        tab_c=tab(cur), tab_p=tab(prev),
        qw=full((1, GROUP_WIDTH)), kw=full((1, SLAB)), ones_q=full((GROUP_WIDTH, GROUP_WIDTH)), ones_k=full((SLAB, SLAB)),
        group=pl.BlockSpec((BLK, GROUP_WIDTH), lambda g, n: (n, g)))


def _swa_fwd(proj, sinks, qw, kw, cos, sins, ones_q, ones_k):
    s = proj.shape[0]
    nb = s // BLK
    sp = _swa_specs()

    def body(sink_ref, q_ref, kc_ref, kp_ref, vc_ref, vp_ref, gate_ref, cosc_ref, sinc_ref, cosp_ref, sinp_ref,
             qw_ref, kw_ref, oq_ref, ok_ref, ag_ref, o_ref):
        g, n = pl.program_id(0), pl.program_id(1)
        cosc, sinc = cosc_ref[...], sinc_ref[...]
        kcat, vcat = _keys_values(kc_ref, kp_ref, vc_ref, vp_ref, kw_ref[...], ok_ref[...], cosc, sinc,
                                  cosp_ref[...], sinp_ref[...], g % 2)
        _, _, qr = _q_prep(q_ref[...].astype(F32), qw_ref[...], _tile_lanes(cosc, 4), _tile_lanes(sinc, 4), oq_ref[...])
        scores = _dot_nt(_stack_heads(qr).astype(BF16), kcat)
        valid = _valid_keys(n)
        probs = [_head_probs(scores[j * BLK:(j + 1) * BLK], sink_ref[g * GROUP_HEADS + j], valid)[0].astype(BF16)
                 for j in range(GROUP_HEADS)]
        o = _unstack_heads(_dot(jnp.concatenate(probs, axis=0), vcat))
        gate = gate_ref[...].astype(F32)
        o_ref[...] = o.astype(BF16)
        ag_ref[...] = (o * (gate * _sigmoid(gate))).astype(BF16)

    shp = jax.ShapeDtypeStruct((s, ATTN_WIDTH), BF16)
    return pl.pallas_call(
        body, name="swa_fwd", grid=(KV_HEADS, nb),
        in_specs=[sp["sinks"], sp["q"], sp["kc"], sp["kp"], sp["vc"], sp["vp"], sp["gate"], sp["tab_c"], sp["tab_c"],
                  sp["tab_p"], sp["tab_p"], sp["qw"], sp["kw"], sp["ones_q"], sp["ones_k"]],
        out_specs=[sp["group"], sp["group"]], out_shape=[shp, shp],
        compiler_params=_params("parallel", "arbitrary"),
    )(sinks, proj, proj, proj, proj, proj, proj, cos, sins, cos, sins, qw, kw, ones_q, ones_k)


def _swa_bwd(proj, dag, sinks, qw, kw, cos, sins, ones_q, ones_k):
    s = proj.shape[0]
    nb = s // BLK
    sp = _swa_specs()

    def body(sink_ref, q_ref, kc_ref, kp_ref, vc_ref, vp_ref, gate_ref, dag_ref, cosc_ref, sinc_ref, cosp_ref, sinp_ref,
             qw_ref, kw_ref, oq_ref, ok_ref, dq_ref, dkc_ref, dkp_ref, dvc_ref, dvp_ref, dqw_ref, dsink_ref):
        g, n = pl.program_id(0), pl.program_id(1)
        cosc, sinc = cosc_ref[...], sinc_ref[...]
        kcat, vcat = _keys_values(kc_ref, kp_ref, vc_ref, vp_ref, kw_ref[...], ok_ref[...], cosc, sinc,
                                  cosp_ref[...], sinp_ref[...], g % 2)
        cosq, sinq = _tile_lanes(cosc, 4), _tile_lanes(sinc, 4)
        qwv = qw_ref[...]
        nrm, rstd, qr = _q_prep(q_ref[...].astype(F32), qwv, cosq, sinq, oq_ref[...])
        gate = gate_ref[...].astype(F32)
        do = dag_ref[...].astype(F32) * (gate * _sigmoid(gate))
        q_stack = _stack_heads(qr).astype(BF16)
        do_stack = _stack_heads(do).astype(BF16)
        scores = _dot_nt(q_stack, kcat)
        dprobs = _dot_nt(do_stack, vcat)
        valid = _valid_keys(n)
        probs, dscores, dsinks = [], [], []
        for j in range(GROUP_HEADS):
            rows = slice(j * BLK, (j + 1) * BLK)
            p, psink = _head_probs(scores[rows], sink_ref[g * GROUP_HEADS + j], valid)
            dp = dprobs[rows]
            delta = jnp.sum(p * dp, axis=-1, keepdims=True)
            probs.append(p.astype(BF16))
            dscores.append((p * (dp - delta) * (HEAD_DIM ** -0.5)).astype(BF16))
            dsinks.append(-jnp.sum(psink * delta, axis=0, keepdims=True))
        ds = jnp.concatenate(dscores, axis=0)
        dk = _dot_tn(ds, q_stack)
        dv = _dot_tn(jnp.concatenate(probs, axis=0), do_stack)
        dk = dk + pltpu.roll(dk, HEAD_DIM, 1)
        dv = dv + pltpu.roll(dv, HEAD_DIM, 1)
        dkp_ref[0], dkc_ref[0] = dk[:BLK], dk[BLK:]
        dvp_ref[0], dvc_ref[0] = dv[:BLK], dv[BLK:]
        dqr = _unstack_heads(_dot(ds, kcat))
        dy = dqr * cosq + _partner64(dqr * sinq)
        dn = dy * qwv
        dq_ref[...] = (rstd * (dn - nrm * _group_mean(dn * nrm, oq_ref[...]))).astype(BF16)

        @pl.when(n == 0)
        def _():
            dqw_ref[...] = jnp.zeros_like(dqw_ref)
            dsink_ref[...] = jnp.zeros_like(dsink_ref)

        dqw_ref[0] += jnp.sum(dy * nrm, axis=0, keepdims=True)
        r8 = lax.broadcasted_iota(jnp.int32, (8, SLAB), 0)
        upd = jnp.zeros((8, SLAB), F32)
        for j in range(GROUP_HEADS):
            upd = jnp.where(r8 == j, dsinks[j], upd)
        dsink_ref[0] += upd

    cur_out = pl.BlockSpec((1, BLK, SLAB), lambda g, n: (g, n, 0))
    prev_out = pl.BlockSpec((1, BLK, SLAB), lambda g, n: (g, (n + nb - 1) % nb, 0))
    kv_shape = jax.ShapeDtypeStruct((KV_HEADS, s, SLAB), F32)
    return pl.pallas_call(
        body, name="swa_bwd", grid=(KV_HEADS, nb),
        in_specs=[sp["sinks"], sp["q"], sp["kc"], sp["kp"], sp["vc"], sp["vp"], sp["gate"], sp["group"],
                  sp["tab_c"], sp["tab_c"], sp["tab_p"], sp["tab_p"], sp["qw"], sp["kw"], sp["ones_q"], sp["ones_k"]],
        out_specs=[sp["q"], cur_out, prev_out, cur_out, prev_out,
                   pl.BlockSpec((1, 8, GROUP_WIDTH), lambda g, n: (g, 0, 0)), pl.BlockSpec((1, 8, SLAB), lambda g, n: (g, 0, 0))],
        out_shape=[jax.ShapeDtypeStruct((s, ODD_IN), BF16), kv_shape, kv_shape, kv_shape, kv_shape,
                   jax.ShapeDtypeStruct((KV_HEADS, 8, GROUP_WIDTH), F32), jax.ShapeDtypeStruct((KV_HEADS, 8, SLAB), F32)],
        compiler_params=_params("parallel", "arbitrary"),
    )(sinks, proj, proj, proj, proj, proj, proj, dag, cos, sins, cos, sins, qw, kw, ones_q, ones_k)


def _swa_bwd_kv(proj, dproj, o, dag, dkc, dkp, dvc, dvp, kw, cos, sins, ones_k):
    s = proj.shape[0]
    rows = min(KV_ROWS, s)

    def body(kv_ref, gate_ref, o_ref, dag_ref, dkc_ref, dkp_ref, dvc_ref, dvp_ref, kw_ref, cos_ref, sin_ref, ok_ref,
             dp_in, dp_ref, dkw_ref):
        del dp_in
        i, j = pl.program_id(0), pl.program_id(1)

        @pl.when(jnp.logical_and(i == 0, j == 0))
        def _():
            dkw_ref[...] = jnp.zeros_like(dkw_ref)

        @pl.when(j == 0)
        def _():
            def assemble(cur_ref, prv_ref):
                tot = [cur_ref[h] + prv_ref[h] for h in range(KV_HEADS)]
                first = _half_mask(tot[0].shape, 0)
                return jnp.concatenate([jnp.where(first, tot[0], tot[1]), jnp.where(first, tot[2], tot[3])], axis=1)

            dkr = assemble(dkc_ref, dkp_ref)
            dv = assemble(dvc_ref, dvp_ref)
            cosf, sinf = _tile_lanes(cos_ref[...], 2), _tile_lanes(sin_ref[...], 2)
            k = kv_ref[:, :KV_WIDTH].astype(F32)
            rstd = lax.rsqrt(_group_mean(k * k, ok_ref[...]) + EPS)
            nrm = k * rstd
            dy = dkr * cosf + _partner64(dkr * sinf)
            dn = dy * kw_ref[...]
            dk = rstd * (dn - nrm * _group_mean(dn * nrm, ok_ref[...]))
            dp_ref[...] = jnp.concatenate([dk, dv], axis=1).astype(BF16)
            dkw_ref[...] += jnp.sum(dy * nrm, axis=0, keepdims=True)

        @pl.when(j > 0)
        def _():
            gate = gate_ref[...].astype(F32)
            sg = _sigmoid(gate)
            dp_ref[...] = (dag_ref[...].astype(F32) * o_ref[...].astype(F32) * (sg * (1.0 + gate * (1.0 - sg)))).astype(BF16)

    acc = pl.BlockSpec((KV_HEADS, rows, SLAB), lambda i, j: (0, i, 0))
    group = pl.BlockSpec((rows, GROUP_WIDTH), lambda i, j: (i, jnp.maximum(j - 1, 0)))
    tab = pl.BlockSpec((rows, SLAB), lambda i, j: (i, 0))
    return pl.pallas_call(
        body, name="swa_bwd_kv", grid=(s // rows, 5),
        in_specs=[pl.BlockSpec((rows, GROUP_WIDTH), lambda i, j: (i, KV_COL)),
                  pl.BlockSpec((rows, GROUP_WIDTH), lambda i, j: (i, GATE_COL + jnp.maximum(j - 1, 0))),
                  group, group, acc, acc, acc, acc,
                  pl.BlockSpec((1, KV_WIDTH), lambda i, j: (0, 0)), tab, tab,
                  pl.BlockSpec((KV_WIDTH, KV_WIDTH), lambda i, j: (0, 0)), ANY],
        out_specs=[pl.BlockSpec((rows, GROUP_WIDTH), lambda i, j: (i, KV_COL + j)), pl.BlockSpec((8, KV_WIDTH), lambda i, j: (0, 0))],
        out_shape=[jax.ShapeDtypeStruct(dproj.shape, dproj.dtype), jax.ShapeDtypeStruct((8, KV_WIDTH), F32)],
        input_output_aliases={12: 0},
        compiler_params=_params("arbitrary", "arbitrary"),
    )(proj, proj, o, dag, dkc, dkp, dvc, dvp, kw, cos, sins, ones_k, dproj)


def _place():
    x, y, c = lax.axis_index("x"), lax.axis_index("y"), lax.axis_index("c")
    return x, y, c


OTHER_CHIPS = ((1, 0), (0, 1), (1, 1))


def _half_rows(ref, half, rows):
    return ref.at[pl.ds(pl.multiple_of(half * (rows // 2), 8), rows // 2)]


DMA_CHUNK_BYTES = 1 << 20
BF16_TILE_ROWS = 16


def _n_chunks(ref):
    rows = ref.shape[-2]
    nbytes = math.prod(ref.shape) * jnp.dtype(ref.dtype).itemsize
    n = 1
    while 2 * n * DMA_CHUNK_BYTES <= nbytes and rows % (2 * n * BF16_TILE_ROWS) == 0:
        n *= 2
    return n


def _row_chunk(ref, k, n):
    rows = ref.shape[-2] // n
    return ref.at[pl.ds(k * rows, rows)] if len(ref.shape) == 2 else ref.at[:, pl.ds(k * rows, rows)]


def _push(src, dst, send_sem, recv_sem, device_id):
    n = _n_chunks(src)
    for k in range(n):
        pltpu.make_async_remote_copy(src_ref=_row_chunk(src, k, n), dst_ref=_row_chunk(dst, k, n), send_sem=send_sem,
                                     recv_sem=recv_sem, device_id=device_id, device_id_type=MESH).start()
    return pltpu.make_async_remote_copy(src_ref=src, dst_ref=dst, send_sem=send_sem, recv_sem=recv_sem,
                                        device_id=device_id, device_id_type=MESH)


def _copy(src, dst, sem):
    n = _n_chunks(src)
    for k in range(n):
        pltpu.make_async_copy(_row_chunk(src, k, n), _row_chunk(dst, k, n), sem).start()
    return pltpu.make_async_copy(src, dst, sem)


HBM = pl.BlockSpec(memory_space=pltpu.HBM)
SEM = pl.BlockSpec(memory_space=pltpu.SEMAPHORE)
SPLIT_COPY_EFFECT = pltpu.SideEffectType.DATAFLOW_SIDE_EFFECTING


def _in_hbm(a):
    return pltpu.with_memory_space_constraint(a, pltpu.HBM)


def _start_copies(name, arrays, plan, n_copies, after=None):
    n = len(arrays)

    def body(*refs):
        send_sem, recv_sem = refs[-n - 3], refs[-n - 2]
        for k, (src, dst, peer) in enumerate(plan(refs[:n])):
            _push(src, dst, send_sem.at[k], recv_sem.at[k], peer)
        refs[-1][...] = jnp.zeros_like(refs[-1])

    dma = pltpu.SemaphoreType.DMA((n_copies,))
    outs = pl.pallas_call(
        body, name=name,
        out_shape=(dma, dma, *[pltpu.HBM(a.shape, a.dtype) for a in arrays], jax.ShapeDtypeStruct((8, 128), F32)),
        in_specs=[HBM] * n + ([ANY] if after is not None else []),
        out_specs=(SEM, SEM, *[HBM] * n, pl.BlockSpec(memory_space=pltpu.VMEM)),
        input_output_aliases={i: i + 2 for i in range(n)},
        compiler_params=pltpu.CompilerParams(has_side_effects=SPLIT_COPY_EFFECT),
    )(*[_in_hbm(a) for a in arrays], *((after,) if after is not None else ()))
    return outs[0], outs[1], list(outs[2:2 + n]), outs[-1]


def _wait_copies(name, send_sem, recv_sem, arrays, plan, after):
    n = len(arrays)

    def body(*refs):
        send_ref, recv_ref = refs[n], refs[n + 1]
        for k, (src, dst, peer) in enumerate(plan(refs[:n])):
            cp = pltpu.make_async_remote_copy(src_ref=src, dst_ref=dst, send_sem=send_ref.at[k], recv_sem=recv_ref.at[k],
                                              device_id=peer, device_id_type=MESH)
            cp.wait_send()
            cp.wait_recv()

    return list(pl.pallas_call(
        body, name=name, out_shape=tuple(pltpu.HBM(a.shape, a.dtype) for a in arrays),
        in_specs=[HBM] * n + [SEM, SEM, ANY], out_specs=tuple([HBM] * n),
        input_output_aliases={i: i for i in range(n)},
        compiler_params=pltpu.CompilerParams(has_side_effects=SPLIT_COPY_EFFECT),
    )(*arrays, send_sem, recv_sem, after))


def _gather_region(shard, full, kind, chip, half=None):
    if kind == "whole":
        return full.at[chip]
    rows, width = shard.shape
    if kind == "col":
        piece = full.at[:, pl.ds(pl.multiple_of(chip * width, 128), width)]
    else:
        piece = full.at[pl.ds(pl.multiple_of(chip * rows, BF16_TILE_ROWS), rows)]
    return piece if half is None else _half_rows(piece, half, rows)


def _gather_plan(kinds):
    n = len(kinds)

    def plan(refs):
        shards, fulls = refs[:n], refs[n:]
        x, y, c = _place()
        copies = []
        for fx, fy in OTHER_CHIPS:
            for shard, full, kind in zip(shards, fulls, kinds):
                src = shard if kind == "whole" else _half_rows(shard, c, shard.shape[0])
                copies.append((src, _gather_region(shard, full, kind, 2 * x + y, c), (x ^ fx, y ^ fy, c)))
        return copies

    return plan


def _gathered_shape(w, kind):
    if kind == "whole":
        return (N_CHIPS,) + w.shape
    return (w.shape[0], N_CHIPS * w.shape[1]) if kind == "col" else (N_CHIPS * w.shape[0], w.shape[1])


def _gather_start(name, shards, kinds, after=None):
    fulls = [lax.empty(_gathered_shape(w, kind), w.dtype) for w, kind in zip(shards, kinds)]
    return _start_copies(name, list(shards) + fulls, _gather_plan(kinds), 3 * len(kinds), after)


def _gather_finish(name, started, kinds, after):
    send_sem, recv_sem, arrays, _ = started
    n = len(kinds)
    arrays = _wait_copies(name + "_wait", send_sem, recv_sem, arrays, _gather_plan(kinds), after)
    split = [i for i, kind in enumerate(kinds) if kind != "whole"]

    def body(*refs):
        shards, fulls = refs[:n], refs[2 * n:3 * n]
        send, recv, local_sem = refs[3 * n:]
        x, y, c = _place()
        my_chip = 2 * x + y
        pushes = []
        for r, (fx, fy) in enumerate(OTHER_CHIPS):
            chip = 2 * (x ^ fx) + (y ^ fy)
            for j, i in enumerate(split):
                landed = _gather_region(shards[i], fulls[i], kinds[i], chip, c)
                pushes.append(_push(landed, landed, send.at[r * len(split) + j], recv.at[r * len(split) + j], (x, y, 1 - c)))
        local = [_copy(shards[i], _gather_region(shards[i], fulls[i], kinds[i], my_chip), local_sem.at[i]) for i in range(n)]
        for cp in pushes:
            cp.wait_recv()
        for cp in pushes:
            cp.wait_send()
        for cp in local:
            cp.wait()

    dma = pltpu.SemaphoreType.DMA
    fulls = arrays[n:]
    return pl.pallas_call(
        body, name=name + "_share", in_specs=[ANY] * (2 * n), out_specs=[ANY] * n,
        out_shape=[jax.ShapeDtypeStruct(a.shape, a.dtype) for a in fulls],
        input_output_aliases={n + i: i for i in range(n)},
        scratch_shapes=[dma((3 * len(split),)), dma((3 * len(split),)), dma((n,))],
        compiler_params=pltpu.CompilerParams(has_side_effects=True),
    )(*arrays)


def _allreduce_small(v):
    def body(v_ref, out_ref, buf, send_sems, recv_sems):
        x, y, c = _place()
        me = 4 * x + 2 * y + c
        buf[me] = v_ref[...]
        copies = []
        for r in range(1, N_DEV):
            peer = (x ^ (r >> 2), y ^ ((r >> 1) & 1), c ^ (r & 1))
            cp = pltpu.make_async_remote_copy(src_ref=v_ref, dst_ref=buf.at[me], send_sem=send_sems.at[r - 1],
                                              recv_sem=recv_sems.at[r - 1], device_id=peer, device_id_type=MESH)
            cp.start()
            copies.append(cp)
        for cp in copies:
            cp.wait_recv()
        for cp in copies:
            cp.wait_send()
        total = buf[0]
        for d in range(1, N_DEV):
            total = total + buf[d]
        out_ref[...] = total

    vm = pl.BlockSpec(memory_space=pltpu.VMEM)
    return pl.pallas_call(
        body, name="allreduce_small", in_specs=[vm], out_specs=vm, out_shape=jax.ShapeDtypeStruct(v.shape, v.dtype),
        scratch_shapes=[pltpu.VMEM((N_DEV,) + v.shape, v.dtype), pltpu.SemaphoreType.DMA((N_DEV - 1,)),
                        pltpu.SemaphoreType.DMA((N_DEV - 1,))],
        compiler_params=pltpu.CompilerParams(has_side_effects=True),
    )(v)


def _exchange_halves(grads, name):
    n = len(grads)

    def body(*refs):
        g, theirs = refs[:n], refs[n:2 * n]
        send_sem, recv_sem = refs[2 * n:]
        x, y, c = _place()
        copies = []
        for i in range(n):
            half = g[i].shape[1] // 2
            src = g[i].at[:, pl.ds(pl.multiple_of((1 - c) * half, BF16_TILE_ROWS), half)]
            copies.append(_push(src, theirs[i], send_sem.at[i], recv_sem.at[i], (x, y, 1 - c)))
        for cp in copies:
            cp.wait_recv()
            cp.wait_send()

    dma = pltpu.SemaphoreType.DMA
    return pl.pallas_call(
        body, name=name, in_specs=[ANY] * n, out_specs=[ANY] * n,
        out_shape=[jax.ShapeDtypeStruct((a.shape[0], a.shape[1] // 2, a.shape[2]), a.dtype) for a in grads],
        scratch_shapes=[dma((n,)), dma((n,))],
        compiler_params=pltpu.CompilerParams(has_side_effects=True),
    )(*grads)


def _pair_sum(g, theirs, core, name):
    pieces, half, cols = theirs.shape
    tr = min(half, 256)
    per = half // tr

    def body(core_ref, g_ref, t_ref, o_ref):
        del core_ref
        o_ref[...] = (g_ref[...].astype(F32) + t_ref[...].astype(F32)).astype(BF16)

    spec = pl.BlockSpec((1, tr, cols), lambda p, i, core_ref: (p, i, 0))
    return pl.pallas_call(
        body, name=name, out_shape=jax.ShapeDtypeStruct(theirs.shape, BF16),
        grid_spec=pltpu.PrefetchScalarGridSpec(
            num_scalar_prefetch=1, grid=(pieces, per),
            in_specs=[pl.BlockSpec((1, tr, cols), lambda p, i, core_ref: (p, core_ref[0] * per + i, 0)), spec],
            out_specs=spec),
        compiler_params=_params("parallel", "parallel"))(core, g, theirs)


def _scatter_plan(n):
    def plan(refs):
        parts, stacks = refs[:n], refs[n:]
        x, y, c = _place()
        copies = []
        for fx, fy in OTHER_CHIPS:
            chip = 2 * (x ^ fx) + (y ^ fy)
            for part, stack in zip(parts, stacks):
                if part.shape[0] == N_CHIPS:
                    piece = part.at[chip]
                else:
                    width = part.shape[2] // N_CHIPS
                    piece = part.at[0].at[:, pl.ds(pl.multiple_of(chip * width, 128), width)]
                copies.append((piece, stack.at[2 * x + y], (x ^ fx, y ^ fy, c)))
        return copies

    return plan


def _scatter_start(name, parts, after=None):
    def landing(a):
        return (N_CHIPS, a.shape[1], a.shape[2] if a.shape[0] == N_CHIPS else a.shape[2] // N_CHIPS)

    stacks = [lax.empty(landing(a), a.dtype) for a in parts]
    return _start_copies(name, list(parts) + stacks, _scatter_plan(len(parts)), 3 * len(parts), after)


def _scatter_finish(name, started, after):
    send_sem, recv_sem, arrays, _ = started
    n = len(arrays) // 2
    arrays = _wait_copies(name + "_wait", send_sem, recv_sem, arrays, _scatter_plan(n), after)
    return arrays[:n], arrays[n:]


def _sum_chips(part, stack, place, name):
    _, r, c = stack.shape
    tr = 256
    per = r // tr

    def body(place_ref, own_ref, a_ref, b_ref, c_ref, o_ref):
        del place_ref
        total = own_ref[0].astype(F32)
        for ref in (a_ref, b_ref, c_ref):
            total = total + ref[0].astype(F32)
        o_ref[...] = total

    if part.shape[0] == N_CHIPS:
        own = pl.BlockSpec((1, tr, c), lambda i, pr: (pr[1], i, 0))
    else:
        own = pl.BlockSpec((1, tr, c), lambda i, pr: (0, i, pr[1]))
    other = lambda flip: pl.BlockSpec((1, tr, c), lambda i, pr: (pr[1] ^ flip, i, 0))
    return pl.pallas_call(
        body, name=name, out_shape=jax.ShapeDtypeStruct((2 * r, c), F32),
        grid_spec=pltpu.PrefetchScalarGridSpec(
            num_scalar_prefetch=1, grid=(per,), in_specs=[own, other(2), other(1), other(3)],
            out_specs=pl.BlockSpec((tr, c), lambda i, pr: (pr[0] * per + i, 0))),
        compiler_params=_params("parallel"))(place, part, stack, stack, stack)


def _share_halves(fulls):
    n = len(fulls)

    def body(*refs):
        dst = refs[n:2 * n]
        send_sem, recv_sem = refs[2 * n:]
        x, y, c = _place()
        copies = []
        for i in range(n):
            mine = _half_rows(dst[i], c, dst[i].shape[0])
            copies.append(_push(mine, mine, send_sem.at[i], recv_sem.at[i], (x, y, 1 - c)))
        for cp in copies:
            cp.wait_recv()
            cp.wait_send()

    dma = pltpu.SemaphoreType.DMA
    return pl.pallas_call(
        body, name="share_halves", in_specs=[ANY] * n, out_specs=[ANY] * n,
        out_shape=[jax.ShapeDtypeStruct(a.shape, a.dtype) for a in fulls],
        input_output_aliases={i: i for i in range(n)}, scratch_shapes=[dma((n,)), dma((n,))],
        compiler_params=pltpu.CompilerParams(has_side_effects=True),
    )(*fulls)


MM = dict(tm=1024, tn=1024, tk=2048)


def _local_step(x, target, ev_norm_w, q_norm_w, k_norm_w, sinks, start_token, weights_first, weights_late, emit):
    s = x.shape[0]
    cos_r, sin_r = _rope_tables(s, RET_DIM)
    cos_a, sin_a = _rope_tables(s, HEAD_DIM)
    cos_a = jnp.tile(cos_a, (1, 4))
    sins_a = jnp.tile(jnp.concatenate([-sin_a, sin_a], axis=1), (1, 2))
    tables = _retention_tables()
    ones_q, ones_k, ones_kv = _block_diag_mean(GROUP_WIDTH), _block_diag_mean(SLAB), _block_diag_mean(KV_WIDTH)
    qw_g = jnp.tile(q_norm_w, (1, GROUP_WIDTH // HEAD_DIM))
    kw_s = jnp.tile(k_norm_w, (1, SLAB // HEAD_DIM))
    kw_kv = jnp.tile(k_norm_w, (1, KV_WIDTH // HEAD_DIM))
    sinks1 = sinks.reshape(Q_HEADS)

    h0 = _rmsnorm(x, ev_norm_w, "norm0", after=start_token)
    w_in0, conv_w, od_norm_w, token = weights_first(h0)
    proj0 = _mm(h0, w_in0, mode="nn", out_dtype=BF16, name="proj0", after=token, **MM)
    o_ret, cat, states = _retention_fwd(proj0, cos_r, sin_r, tables)
    cat = _conv_fwd(proj0, conv_w, cat)
    w_out0, w_in1, w_out1 = weights_late(cat)
    x1 = _mm(cat, w_out0, mode="nn", out_dtype=F32, name="out0", add=x, **MM)
    h1 = _rmsnorm(x1, od_norm_w, "norm1")
    proj1 = _mm(h1, w_in1, mode="nn", out_dtype=BF16, name="proj1", tm=1024, tn=1152, tk=2048)
    ag, o_att = _swa_fwd(proj1, sinks1, qw_g, kw_s, cos_a, sins_a, ones_q, ones_k)
    x2 = _mm(ag, w_out1, mode="nn", out_dtype=F32, name="out1", add=x1, **MM)
    dy, dy_b, sq = _loss_grad(x2, target)

    g_w_out1 = _mm(ag, dy_b, mode="tn", out_dtype=BF16, name="g_w_out1", **MM)
    dag = _mm(dy_b, w_out1, mode="nt", out_dtype=BF16, name="d_ag", **MM)
    dproj1, dkc, dkp, dvc, dvp, dqw, dsink = _swa_bwd(proj1, dag, sinks1, qw_g, kw_s, cos_a, sins_a, ones_q, ones_k)
    dproj1, dkw = _swa_bwd_kv(proj1, dproj1, o_att, dag, dkc, dkp, dvc, dvp, kw_kv, cos_a, sins_a, ones_kv)
    g_w_in1 = _mm(h1, dproj1, mode="tn", out_dtype=BF16, name="g_w_in1", tm=1024, tn=1152, tk=2048)
    token = emit("layer1", (("od_w_in", g_w_in1, "col"), ("od_w_out", g_w_out1, "row")))
    dh1 = _mm(dproj1, w_in1, mode="nt", out_dtype=BF16, name="d_h1", tm=1024, tn=1024, tk=2304, after=token)
    dx1, dx1_b, g_norm1 = _rmsnorm_bwd(x1, od_norm_w, dh1, dy, "norm1_bwd", True)

    g_w_out0 = _mm(cat, dx1_b, mode="tn", out_dtype=BF16, name="g_w_out0", **MM)
    token = emit("out0", (("ev_w_out", g_w_out0, "row"),))
    dcat = _mm(dx1_b, w_out0, mode="nt", out_dtype=BF16, name="d_cat", after=token, **MM)
    dproj0 = _retention_bwd(proj0, o_ret, states, dcat, cos_r, sin_r, tables)
    dproj0, g_conv = _conv_bwd(proj0, dcat, conv_w, dproj0)
    g_w_in0 = _mm(h0, dproj0, mode="tn", out_dtype=BF16, name="g_w_in0", **MM)
    token = emit("in0", (("ev_w_in", g_w_in0, "col"),))
    dh0 = _mm(dproj0, w_in0, mode="nt", out_dtype=BF16, name="d_h0", after=token, **MM)
    grad_x, g_norm0 = _rmsnorm_bwd(x, ev_norm_w, dh0, dx1, "norm0_bwd", False)

    g_qw = dqw[:, 0, :].reshape(Q_HEADS, HEAD_DIM).sum(axis=0)
    g_kw = dkw[0].reshape(KV_HEADS, HEAD_DIM).sum(axis=0)
    g_sinks = dsink[:, :, 0].reshape(Q_HEADS)
    small = dict(ev_norm=g_norm0[0], od_norm=g_norm1[0], conv=g_conv[:3], qw=g_qw, kw=g_kw, sinks=g_sinks)
    return sq[0, 0], grad_x, small


def _pack_small_grads(small):
    pad = lambda v: jnp.pad(v, (0, D_MODEL - v.shape[0]))
    tail = pad(jnp.concatenate([small["qw"], small["kw"], small["sinks"]]))
    rows = [small["ev_norm"], small["od_norm"]] + [pad(small["conv"][t]) for t in range(3)] + [tail]
    rows += [jnp.zeros((D_MODEL,), F32)] * (8 - len(rows))
    return jnp.stack(rows)


class _ReduceScatter:
    def __init__(self, place):
        self.place = place
        self.started = []

    def send(self, tag, grads):
        pieces = [g[None] if kind == "col" else g.reshape(N_CHIPS, g.shape[0] // N_CHIPS, g.shape[1]) for _, g, kind in grads]
        theirs = _exchange_halves(pieces, "exchange_halves_" + tag)
        parts = [_pair_sum(g, t, self.place[:1], "pair_sum_" + nm) for g, t, (nm, _, _) in zip(pieces, theirs, grads)]
        started = _scatter_start("scatter_" + tag, parts)
        self.started.append((tag, [nm for nm, _, _ in grads], started))
        return started[3]

    def finish(self, after):
        names, sums = [], []
        for tag, group, started in self.started:
            parts, stacks = _scatter_finish("scatter_" + tag, started, after)
            sums += [_sum_chips(p, s, self.place, "chip_sum_" + nm) for p, s, nm in zip(parts, stacks, group)]
            names += group
        return dict(zip(names, _share_halves(sums)))


def kernel(x, ev_norm_w, ev_w_in, ev_conv_w, ev_w_out, od_norm_w, od_w_in, od_q_norm_w, od_k_norm_w, od_sinks, od_w_out, loss_target, m_ev_norm_w, m_ev_w_in, m_ev_conv_w, m_ev_w_out, m_od_norm_w, m_od_w_in, m_od_q_norm_w, m_od_k_norm_w, m_od_sinks, m_od_w_out, v_ev_norm_w, v_ev_w_in, v_ev_conv_w, v_ev_w_out, v_od_norm_w, v_od_w_in, v_od_q_norm_w, v_od_k_norm_w, v_od_sinks, v_od_w_out):
    my_chip = 2 * lax.axis_index("x") + lax.axis_index("y")
    place = jnp.stack([lax.axis_index("c"), my_chip]).astype(jnp.int32)
    shard_w = D_MODEL // N_CHIPS
    conv_shard = CONV_WIDTH // N_CHIPS

    small_in = jnp.zeros((8, shard_w), F32)
    small_in = small_in.at[0].set(od_norm_w[0]).at[1:4, :conv_shard].set(ev_conv_w[0])
    first_kinds, late_kinds = ("col", "whole"), ("row", "col", "row")
    first = _gather_start("gather_first", [_cast_bf16(ev_w_in[0], "cast_w_in0"), small_in], first_kinds)
    late_shards = [_cast_bf16(ev_w_out[0], "cast_w_out0"), _cast_bf16(od_w_in[0], "cast_w_in1"),
                   _cast_bf16(od_w_out[0], "cast_w_out1")]
    late = []

    def weights_first(after):
        w_in0, small_all = _gather_finish("gather_first", first, first_kinds, after)
        late.append(_gather_start("gather_late", late_shards, late_kinds, after=w_in0))
        od_norm_full = small_all[:, 0, :].reshape(1, D_MODEL)
        conv_full = jnp.transpose(small_all[:, 1:4, :conv_shard], (1, 0, 2)).reshape(3, CONV_WIDTH)
        return w_in0, conv_full, od_norm_full, late[0][3]

    def weights_late(after):
        return _gather_finish("gather_late", late[0], late_kinds, after)

    reduce_scatter = _ReduceScatter(place)
    sq, grad_x, small = _local_step(x[0], loss_target[0], ev_norm_w, od_q_norm_w, od_k_norm_w, od_sinks, first[3],
                                    weights_first, weights_late, reduce_scatter.send)
    loss = lax.psum(0.5 * sq / D_MODEL, ("x", "y", "c"))

    big = reduce_scatter.finish(grad_x)
    g_ev_w_in, g_ev_w_out, g_od_w_in, g_od_w_out = big["ev_w_in"], big["ev_w_out"], big["od_w_in"], big["od_w_out"]
    tot = _allreduce_small(_pack_small_grads(small))
    g_ev_norm = tot[0:1]
    g_od_norm = lax.dynamic_slice(tot, (1, my_chip * shard_w), (1, shard_w))
    g_conv = lax.dynamic_slice(tot, (2, my_chip * conv_shard), (3, conv_shard))
    g_qw, g_kw, g_sinks = tot[5:6, 0:HEAD_DIM], tot[5:6, HEAD_DIM:2 * HEAD_DIM], tot[5:6, 2 * HEAD_DIM:2 * HEAD_DIM + Q_HEADS]

    upd = {}
    upd["ev_w_in"] = _adamw(ev_w_in[0], g_ev_w_in, m_ev_w_in[0], v_ev_w_in[0], "adamw_ev_w_in")
    upd["ev_w_out"] = _adamw(ev_w_out[0], g_ev_w_out, m_ev_w_out[0], v_ev_w_out[0], "adamw_ev_w_out")
    upd["od_w_in"] = _adamw(od_w_in[0], g_od_w_in, m_od_w_in[0], v_od_w_in[0], "adamw_od_w_in")
    upd["od_w_out"] = _adamw(od_w_out[0], g_od_w_out, m_od_w_out[0], v_od_w_out[0], "adamw_od_w_out")
    smalls = (("ev_norm_w", ev_norm_w, g_ev_norm, m_ev_norm_w, v_ev_norm_w),
              ("ev_conv_w", ev_conv_w, g_conv, m_ev_conv_w, v_ev_conv_w),
              ("od_norm_w", od_norm_w, g_od_norm, m_od_norm_w, v_od_norm_w),
              ("od_q_norm_w", od_q_norm_w, g_qw, m_od_q_norm_w, v_od_q_norm_w),
              ("od_k_norm_w", od_k_norm_w, g_kw, m_od_k_norm_w, v_od_k_norm_w),
              ("od_sinks", od_sinks, g_sinks, m_od_sinks, v_od_sinks))
    sizes = [w.size for _, w, _, _, _ in smalls]
    padded = 8 * 128 * math.ceil(sum(sizes) / (8 * 128))
    pack = lambda arrs, fill: jnp.concatenate(
        [a.reshape(-1) for a in arrs] + [jnp.full((padded - sum(sizes),), fill, F32)]).reshape(8, padded // 8)
    packed = _adamw(pack([w for _, w, _, _, _ in smalls], 0.0), pack([g for _, _, g, _, _ in smalls], 0.0),
                    pack([m for _, _, _, m, _ in smalls], 0.0), pack([v for _, _, _, _, v in smalls], 1.0), "adamw_small")
    offs = [sum(sizes[:i]) for i in range(len(sizes))]
    grads = {"ev_w_in": g_ev_w_in[None], "ev_w_out": g_ev_w_out[None], "od_w_in": g_od_w_in[None], "od_w_out": g_od_w_out[None]}
    for (nm, w, g, _, _), off, size in zip(smalls, offs, sizes):
        upd[nm] = tuple(p.reshape(-1)[off:off + size].reshape(w.shape) for p in packed)
        grads[nm] = g.reshape(w.shape)
    for nm in ("ev_w_in", "ev_w_out", "od_w_in", "od_w_out"):
        upd[nm] = tuple(u[None] for u in upd[nm])
    order = ("ev_norm_w", "ev_w_in", "ev_conv_w", "ev_w_out", "od_norm_w", "od_w_in", "od_q_norm_w", "od_k_norm_w", "od_sinks", "od_w_out")
    return (loss, grad_x[None], *[grads[nm] for nm in order], *[upd[nm][0] for nm in order],
            *[upd[nm][1] for nm in order], *[upd[nm][2] for nm in order])
```

```python
import functools
import math

import jax
import jax.numpy as jnp
from jax import lax
from jax.experimental import pallas as pl
from jax.experimental.pallas import tpu as pltpu

F32 = jnp.float32
BF16 = jnp.bfloat16

D_MODEL = 2048
RET_HEADS = 4
RET_DIM = 256
RET_WIDTH = 1024
CONV_WIDTH = 1024
EVEN_IN = 8192
Q_HEADS = 32
HEAD_DIM = 64
KV_HEADS = 4
KV_WIDTH = 256
ATTN_WIDTH = 2048
ODD_IN = 4608
BLK = 128
ROPE_THETA = 10000.0
EPS = 1e-6
ADAM_LR = 0.001
ADAM_B1 = 0.9
ADAM_B2 = 0.999
ADAM_EPS = 1e-08
ADAM_WD = 0.01
ADAM_STEP = 10
N_CHIPS = 4
N_DEV = 8
VMEM_LIMIT_BYTES = 48 * 1024 * 1024
MESH = pl.DeviceIdType.MESH
ANY = pl.BlockSpec(memory_space=pl.ANY)


def _params(*sem):
    return pltpu.CompilerParams(dimension_semantics=sem, vmem_limit_bytes=VMEM_LIMIT_BYTES)


def _dot(a, b):
    return jnp.dot(a, b, preferred_element_type=F32)


def _dot_nt(a, b):
    return lax.dot_general(a, b, (((1,), (1,)), ((), ())), preferred_element_type=F32)


def _dot_tn(a, b):
    return lax.dot_general(a, b, (((0,), (0,)), ((), ())), preferred_element_type=F32)


def _sigmoid(x):
    return 1.0 / (1.0 + jnp.exp(-x))


def _mm(a, b, *, mode, tm, tn, tk, out_dtype, name, add=None, after=None):
    if mode == "nn":
        (m, k), n = a.shape, b.shape[1]
    elif mode == "nt":
        (m, k), n = a.shape, b.shape[0]
    else:
        (k, m), n = a.shape, b.shape[1]
    tm, tn, tk = min(tm, m), min(tn, n), min(tk, k)
    assert m % tm == 0 and n % tn == 0 and k % tk == 0, (name, m, n, k)
    nk = k // tk
    dot = {"nn": _dot, "nt": _dot_nt, "tn": _dot_tn}[mode]
    a_spec = (pl.BlockSpec((tk, tm), lambda i, j, kk: (kk, i)) if mode == "tn"
              else pl.BlockSpec((tm, tk), lambda i, j, kk: (i, kk)))
    b_spec = (pl.BlockSpec((tn, tk), lambda i, j, kk: (j, kk)) if mode == "nt"
              else pl.BlockSpec((tk, tn), lambda i, j, kk: (kk, j)))
    o_spec = pl.BlockSpec((tm, tn), lambda i, j, kk: (i, j))
    has_add = add is not None

    def body(*refs):
        a_ref, b_ref = refs[0], refs[1]
        add_ref = refs[2] if has_add else None
        o_ref, acc_ref = refs[-2], refs[-1]
        p = dot(a_ref[...], b_ref[...])

        def finish(total):
            if has_add:
                total = total + add_ref[...].astype(F32)
            o_ref[...] = total.astype(out_dtype)

        if nk == 1:
            finish(p)
        else:
            kk = pl.program_id(2)

            @pl.when(kk == 0)
            def _():
                acc_ref[...] = p

            @pl.when(jnp.logical_and(kk > 0, kk < nk - 1))
            def _():
                acc_ref[...] += p

            @pl.when(kk == nk - 1)
            def _():
                finish(acc_ref[...] + p)

    in_specs = [a_spec, b_spec] + ([o_spec] if has_add else []) + ([ANY] if after is not None else [])
    args = (a, b) + ((add,) if has_add else ()) + ((after,) if after is not None else ())
    return pl.pallas_call(
        body, name=name, grid=(m // tm, n // tn, nk), in_specs=in_specs, out_specs=o_spec,
        out_shape=jax.ShapeDtypeStruct((m, n), out_dtype),
        scratch_shapes=[pltpu.VMEM((tm, tn) if nk > 1 else (8, 128), F32)],
        compiler_params=_params("parallel", "parallel", "arbitrary"),
    )(*args)


def _cast_into_gathered(w, kind, chip, name):
    r, c = w.shape
    tr = min(r, 512)
    per = r // tr

    def body(chip_ref, w_ref, o_ref):
        del chip_ref
        o_ref[...] = w_ref[...].astype(BF16)

    if kind == "col":
        shape, out_map = (r, N_CHIPS * c), (lambda i, chip_ref: (i, chip_ref[0]))
    else:
        shape, out_map = (N_CHIPS * r, c), (lambda i, chip_ref: (chip_ref[0] * per + i, 0))
    return pl.pallas_call(
        body, name=name, out_shape=jax.ShapeDtypeStruct(shape, BF16),
        grid_spec=pltpu.PrefetchScalarGridSpec(
            num_scalar_prefetch=1, grid=(per,), in_specs=[pl.BlockSpec((tr, c), lambda i, chip_ref: (i, 0))],
            out_specs=pl.BlockSpec((tr, c), out_map)),
        compiler_params=_params("parallel"))(chip, w)


def _rmsnorm(x, w, name, after=None):
    s, d = x.shape
    tr = 256

    def body(x_ref, w_ref, *rest):
        xv = x_ref[...]
        rstd = lax.rsqrt(jnp.mean(xv * xv, axis=-1, keepdims=True) + EPS)
        rest[-1][...] = (xv * rstd * w_ref[...]).astype(BF16)

    return pl.pallas_call(
        body, name=name, grid=(s // tr,),
        in_specs=[pl.BlockSpec((tr, d), lambda i: (i, 0)), pl.BlockSpec((1, d), lambda i: (0, 0))]
        + ([ANY] if after is not None else []),
        out_specs=pl.BlockSpec((tr, d), lambda i: (i, 0)),
        out_shape=jax.ShapeDtypeStruct((s, d), BF16), compiler_params=_params("parallel"),
    )(x, w, *((after,) if after is not None else ()))


def _rmsnorm_bwd(x, w, dh, dres, name, with_bf16):
    s, d = x.shape
    tr = 256

    def body(x_ref, w_ref, dh_ref, dres_ref, *outs):
        dx_ref, dw_ref = outs[0], outs[-1]
        xv = x_ref[...]
        rstd = lax.rsqrt(jnp.mean(xv * xv, axis=-1, keepdims=True) + EPS)
        nrm = xv * rstd
        dhv = dh_ref[...].astype(F32)
        dn = dhv * w_ref[...]
        dx = dres_ref[...] + rstd * (dn - nrm * jnp.mean(dn * nrm, axis=-1, keepdims=True))
        dx_ref[...] = dx
        if with_bf16:
            outs[1][...] = dx.astype(BF16)

        @pl.when(pl.program_id(0) == 0)
        def _():
            dw_ref[...] = jnp.zeros_like(dw_ref)

        dw_ref[...] += jnp.sum(dhv * nrm, axis=0, keepdims=True)

    row = pl.BlockSpec((tr, d), lambda i: (i, 0))
    out_shape = [jax.ShapeDtypeStruct((s, d), F32)] + ([jax.ShapeDtypeStruct((s, d), BF16)] if with_bf16 else [])
    out_shape.append(jax.ShapeDtypeStruct((8, d), F32))
    out_specs = [row] * (2 if with_bf16 else 1) + [pl.BlockSpec((8, d), lambda i: (0, 0))]
    return pl.pallas_call(
        body, name=name, grid=(s // tr,),
        in_specs=[row, pl.BlockSpec((1, d), lambda i: (0, 0)), row, row],
        out_specs=out_specs, out_shape=out_shape, compiler_params=_params("arbitrary"))(x, w, dh, dres)


def _loss_grad(y, target):
    s, d = y.shape
    tr = 256

    def body(y_ref, t_ref, dy_ref, dyb_ref, l_ref):
        diff = y_ref[...] - t_ref[...]
        dy = diff * (1.0 / d)
        dy_ref[...] = dy
        dyb_ref[...] = dy.astype(BF16)

        @pl.when(pl.program_id(0) == 0)
        def _():
            l_ref[...] = jnp.zeros_like(l_ref)

        l_ref[...] += jnp.sum(jnp.sum(diff * diff, axis=1, keepdims=True), axis=0, keepdims=True)

    row = pl.BlockSpec((tr, d), lambda i: (i, 0))
    return pl.pallas_call(
        body, name="loss_grad", grid=(s // tr,), in_specs=[row, row],
        out_specs=[row, row, pl.BlockSpec((8, 128), lambda i: (0, 0))],
        out_shape=[jax.ShapeDtypeStruct((s, d), F32), jax.ShapeDtypeStruct((s, d), BF16),
                   jax.ShapeDtypeStruct((8, 128), F32)],
        compiler_params=_params("arbitrary"))(y, target)


def _adamw(w, g, m, v, name):
    r, c = w.shape
    tr = min(r, 256)
    assert r % tr == 0

    def body(w_ref, g_ref, m_ref, v_ref, d_ref, nm_ref, nv_ref):
        gv = g_ref[...]
        nm = ADAM_B1 * m_ref[...] + (1.0 - ADAM_B1) * gv
        nv = ADAM_B2 * v_ref[...] + (1.0 - ADAM_B2) * (gv * gv)
        m_hat = nm / (1.0 - ADAM_B1 ** ADAM_STEP)
        v_hat = nv / (1.0 - ADAM_B2 ** ADAM_STEP)
        d_ref[...] = -ADAM_LR * (m_hat / (jnp.sqrt(v_hat) + ADAM_EPS) + ADAM_WD * w_ref[...])
        nm_ref[...] = nm
        nv_ref[...] = nv

    spec = pl.BlockSpec((tr, c), lambda i: (i, 0))
    shp = jax.ShapeDtypeStruct((r, c), F32)
    return pl.pallas_call(body, name=name, grid=(r // tr,), in_specs=[spec] * 4, out_specs=[spec] * 3,
                          out_shape=[shp] * 3, compiler_params=_params("parallel"))(w, g, m, v)


def _rope_tables(s, dim):
    inv = 1.0 / (ROPE_THETA ** (jnp.arange(0, dim, 2, dtype=F32) / dim))
    ang = jnp.arange(s).astype(F32)[:, None] * inv[None, :]
    return jnp.cos(ang), jnp.sin(ang)


def _rope_half(x, cos, sin):
    h = x.shape[1] // 2
    x1, x2 = x[:, :h], x[:, h:]
    return jnp.concatenate([x1 * cos - x2 * sin, x2 * cos + x1 * sin], axis=1)


def _unrope_half(dy, cos, sin):
    h = dy.shape[1] // 2
    d1, d2 = dy[:, :h], dy[:, h:]
    return jnp.concatenate([d1 * cos + d2 * sin, d2 * cos - d1 * sin], axis=1)


def _lane(shape):
    return lax.broadcasted_iota(jnp.int32, shape, 1)


def _partner64(x):
    w = x.shape[1]
    first = (_lane(x.shape) % HEAD_DIM) < (HEAD_DIM // 2)
    return jnp.where(first, pltpu.roll(x, w - HEAD_DIM // 2, 1), pltpu.roll(x, HEAD_DIM // 2, 1))


def _tile_lanes(t, reps):
    return t if reps == 1 else jnp.concatenate([t] * reps, axis=1)


def _group_mean(x, ones_bd):
    hi = x.astype(BF16)
    lo = (x - hi.astype(F32)).astype(BF16)
    return _dot(hi, ones_bd) + _dot(lo, ones_bd)


def _block_diag_mean(width):
    idx = jnp.arange(width) // HEAD_DIM
    return jnp.where(idx[:, None] == idx[None, :], 1.0 / HEAD_DIM, 0.0).astype(BF16)


def _retention_tables():
    h = RET_HEADS
    log_g = jnp.log(1.0 - 2.0 ** (-5.0 - jnp.arange(h, dtype=F32)))
    idx = jnp.arange(BLK, dtype=F32)
    diff = idx[:, None] - idx[None, :]
    intra = jnp.where(diff >= 0, jnp.exp(log_g[:, None, None] * jnp.maximum(diff, 0.0)), 0.0).astype(F32)
    q_dec = jnp.exp(log_g[:, None] * (idx[None, :] + 1.0)).astype(F32)[:, :, None]
    k_dec = jnp.exp(log_g[:, None] * (BLK - 1.0 - idx[None, :])).astype(F32)[:, :, None]
    chunk_dec = jnp.exp(log_g * BLK).astype(F32)[:, None, None]
    return intra, q_dec, k_dec, chunk_dec


def _retention_fwd(proj, cos, sin, tables):
    s = proj.shape[0]
    nc = s // BLK
    intra, q_dec, k_dec, chunk_dec = tables

    def body(p_ref, cos_ref, sin_ref, in_ref, qd_ref, kd_ref, cd_ref, o_ref, cat_ref, st_ref, state):
        @pl.when(pl.program_id(0) == 0)
        def _():
            state[...] = jnp.zeros_like(state)

        cosv, sinv = cos_ref[...], sin_ref[...]
        for h in range(RET_HEADS):
            c0 = h * RET_DIM
            q = p_ref[:, c0:c0 + RET_DIM].astype(F32)
            k = p_ref[:, RET_WIDTH + c0:RET_WIDTH + c0 + RET_DIM].astype(F32)
            v = p_ref[:, 2 * RET_WIDTH + c0:2 * RET_WIDTH + c0 + RET_DIM]
            g = p_ref[:, 3 * RET_WIDTH + c0:3 * RET_WIDTH + c0 + RET_DIM].astype(F32)
            qb = _rope_half(q, cosv, sinv).astype(BF16)
            kr = _rope_half(k, cosv, sinv) * (RET_DIM ** -0.5)
            kb = kr.astype(BF16)
            scores = _dot_nt(qb, kb) * in_ref[h]
            inner = _dot(scores.astype(BF16), v)
            prev = state[h]
            prev_b = prev.astype(BF16)
            st_ref[h, 0] = prev_b
            o = inner + _dot(qb, prev_b) * qd_ref[h]
            o_ref[:, c0:c0 + RET_DIM] = o
            rstd = lax.rsqrt(jnp.mean(o * o, axis=-1, keepdims=True) + EPS)
            cat_ref[:, c0:c0 + RET_DIM] = (o * rstd * (g * _sigmoid(g))).astype(BF16)
            state[h] = cd_ref[h] * prev + _dot_tn((kr * kd_ref[h]).astype(BF16), v)

    full = lambda shape: pl.BlockSpec(shape, lambda n: (0,) * len(shape))
    return pl.pallas_call(
        body, name="retention_fwd", grid=(nc,),
        in_specs=[pl.BlockSpec((BLK, 4 * RET_WIDTH), lambda n: (n, 0)),
                  pl.BlockSpec((BLK, RET_DIM // 2), lambda n: (n, 0)), pl.BlockSpec((BLK, RET_DIM // 2), lambda n: (n, 0)),
                  full((RET_HEADS, BLK, BLK)), full((RET_HEADS, BLK, 1)), full((RET_HEADS, BLK, 1)), full((RET_HEADS, 1, 1))],
        out_specs=[pl.BlockSpec((BLK, RET_WIDTH), lambda n: (n, 0)), pl.BlockSpec((BLK, RET_WIDTH), lambda n: (n, 0)),
                   pl.BlockSpec((RET_HEADS, 1, RET_DIM, RET_DIM), lambda n: (0, n, 0, 0))],
        out_shape=[jax.ShapeDtypeStruct((s, RET_WIDTH), F32), jax.ShapeDtypeStruct((s, D_MODEL), BF16),
                   jax.ShapeDtypeStruct((RET_HEADS, nc, RET_DIM, RET_DIM), BF16)],
        scratch_shapes=[pltpu.VMEM((RET_HEADS, RET_DIM, RET_DIM), F32)],
        compiler_params=_params("arbitrary"))(proj, cos, sin, intra, q_dec, k_dec, chunk_dec)


def _retention_bwd(proj, o, states, dcat, cos, sin, tables):
    s = proj.shape[0]
    nc = s // BLK
    intra, q_dec, k_dec, chunk_dec = tables

    def body(p_ref, o_ref, st_ref, dc_ref, cos_ref, sin_ref, in_ref, qd_ref, kd_ref, cd_ref, dp_ref, dstate):
        @pl.when(pl.program_id(0) == 0)
        def _():
            dstate[...] = jnp.zeros_like(dstate)

        cosv, sinv = cos_ref[...], sin_ref[...]
        for h in range(RET_HEADS):
            c0 = h * RET_DIM
            q = p_ref[:, c0:c0 + RET_DIM].astype(F32)
            k = p_ref[:, RET_WIDTH + c0:RET_WIDTH + c0 + RET_DIM].astype(F32)
            v = p_ref[:, 2 * RET_WIDTH + c0:2 * RET_WIDTH + c0 + RET_DIM]
            g = p_ref[:, 3 * RET_WIDTH + c0:3 * RET_WIDTH + c0 + RET_DIM].astype(F32)
            o = o_ref[:, c0:c0 + RET_DIM]
            dc = dc_ref[:, c0:c0 + RET_DIM].astype(F32)
            rstd = lax.rsqrt(jnp.mean(o * o, axis=-1, keepdims=True) + EPS)
            nrm = o * rstd
            sg = _sigmoid(g)
            dg = dc * nrm * (sg * (1.0 + g * (1.0 - sg)))
            dn = dc * (g * sg)
            do = rstd * (dn - nrm * jnp.mean(dn * nrm, axis=-1, keepdims=True))
            qb = _rope_half(q, cosv, sinv).astype(BF16)
            kr = _rope_half(k, cosv, sinv) * (RET_DIM ** -0.5)
            kb = kr.astype(BF16)
            mask = in_ref[h]
            qd, kd = qd_ref[h], kd_ref[h]
            prev_b = st_ref[h, 0]
            dnext = dstate[h]
            dnext_b = dnext.astype(BF16)
            att = (_dot_nt(qb, kb) * mask).astype(BF16)
            do_b = do.astype(BF16)
            doq = (do * qd).astype(BF16)
            dv = _dot_tn(att, do_b) + _dot((kr * kd).astype(BF16), dnext_b)
            ds = (_dot_nt(do_b, v) * mask).astype(BF16)
            dqr = _dot(ds, kb) + _dot_nt(doq, prev_b)
            dkr = _dot_tn(ds, qb) + _dot_nt(v, dnext_b) * kd
            dstate[h] = cd_ref[h] * dnext + _dot_tn(qb, doq)
            dq = _unrope_half(dqr, cosv, sinv)
            dk = _unrope_half(dkr * (RET_DIM ** -0.5), cosv, sinv)
            dp_ref[:, c0:c0 + RET_DIM] = dq.astype(BF16)
            dp_ref[:, RET_WIDTH + c0:RET_WIDTH + c0 + RET_DIM] = dk.astype(BF16)
            dp_ref[:, 2 * RET_WIDTH + c0:2 * RET_WIDTH + c0 + RET_DIM] = dv.astype(BF16)
            dp_ref[:, 3 * RET_WIDTH + c0:3 * RET_WIDTH + c0 + RET_DIM] = dg.astype(BF16)

    rev = lambda n: nc - 1 - n
    full = lambda shape: pl.BlockSpec(shape, lambda n: (0,) * len(shape))
    return pl.pallas_call(
        body, name="retention_bwd", grid=(nc,),
        in_specs=[pl.BlockSpec((BLK, 4 * RET_WIDTH), lambda n: (rev(n), 0)),
                  pl.BlockSpec((BLK, RET_WIDTH), lambda n: (rev(n), 0)),
                  pl.BlockSpec((RET_HEADS, 1, RET_DIM, RET_DIM), lambda n: (0, rev(n), 0, 0)),
                  pl.BlockSpec((BLK, RET_WIDTH), lambda n: (rev(n), 0)),
                  pl.BlockSpec((BLK, RET_DIM // 2), lambda n: (rev(n), 0)), pl.BlockSpec((BLK, RET_DIM // 2), lambda n: (rev(n), 0)),
                  full((RET_HEADS, BLK, BLK)), full((RET_HEADS, BLK, 1)), full((RET_HEADS, BLK, 1)), full((RET_HEADS, 1, 1))],
        out_specs=pl.BlockSpec((BLK, 4 * RET_WIDTH), lambda n: (rev(n), 0)),
        out_shape=jax.ShapeDtypeStruct((s, EVEN_IN), BF16),
        scratch_shapes=[pltpu.VMEM((RET_HEADS, RET_DIM, RET_DIM), F32)],
        compiler_params=_params("arbitrary"))(proj, o, states, dcat, cos, sin, intra, q_dec, k_dec, chunk_dec)


CONV_ROWS = 256
HALO = 16


def _conv_pieces(p, halo, conv_w, first):
    rows = p.shape[0]
    gb, gc, u, gv = (p[:, i * CONV_WIDTH:(i + 1) * CONV_WIDTH] for i in range(4))
    cu = gc * u
    hcu = halo[:, CONV_WIDTH:2 * CONV_WIDTH] * halo[:, 2 * CONV_WIDTH:3 * CONV_WIDTH]
    hcu = jnp.where(first, 0.0, hcu)
    r1, r2 = hcu[HALO - 1:HALO], hcu[HALO - 2:HALO - 1]
    row = lax.broadcasted_iota(jnp.int32, cu.shape, 0)
    m1 = jnp.where(row == 0, r1, pltpu.roll(cu, 1, 0))
    m2 = jnp.where(row == 0, r2, jnp.where(row == 1, r1, pltpu.roll(cu, 2, 0)))
    del rows, conv_w
    return gb, gc, u, gv, cu, m1, m2


def _conv_fwd(proj, conv_w, cat):
    s = proj.shape[0]
    per = CONV_ROWS // HALO

    def body(p_ref, halo_ref, w_ref, cat_in, cat_ref):
        del cat_in
        first = pl.program_id(0) == 0
        gb, _, _, gv, cu, m1, m2 = _conv_pieces(p_ref[...].astype(F32), halo_ref[...].astype(F32), None, first)
        conv = w_ref[0:1, :] * m2 + w_ref[1:2, :] * m1 + w_ref[2:3, :] * cu
        cat_ref[...] = (gb * conv * (gv * _sigmoid(gv))).astype(BF16)

    return pl.pallas_call(
        body, name="conv_fwd", grid=(s // CONV_ROWS,),
        in_specs=[pl.BlockSpec((CONV_ROWS, 4 * CONV_WIDTH), lambda i: (i, 1)),
                  pl.BlockSpec((HALO, 4 * CONV_WIDTH), lambda i: (jnp.maximum(i * per - 1, 0), 1)),
                  pl.BlockSpec((3, CONV_WIDTH), lambda i: (0, 0)), ANY],
        out_specs=pl.BlockSpec((CONV_ROWS, CONV_WIDTH), lambda i: (i, 1)),
        out_shape=jax.ShapeDtypeStruct(cat.shape, cat.dtype), input_output_aliases={3: 0},
        compiler_params=_params("parallel"))(proj, proj, conv_w, cat)


def _conv_bwd(proj, dcat, conv_w, dproj):
    s = proj.shape[0]
    per = CONV_ROWS // HALO
    last_halo = s // HALO - 1
    nsteps = s // CONV_ROWS

    def body(p_ref, halo_ref, nxt_ref, dc_ref, dnxt_ref, w_ref, dp_in, dp_ref, dw_ref):
        del dp_in
        i = pl.program_id(0)
        gb, gc, u, gv, cu, m1, m2 = _conv_pieces(p_ref[...].astype(F32), halo_ref[...].astype(F32), None, i == 0)
        w0, w1, w2 = w_ref[0:1, :], w_ref[1:2, :], w_ref[2:3, :]
        conv = w0 * m2 + w1 * m1 + w2 * cu
        dco = dc_ref[...].astype(F32)
        sg = _sigmoid(gv)
        silu = gv * sg
        dgb = dco * conv * silu
        dgv = dco * gb * conv * (sg * (1.0 + gv * (1.0 - sg)))
        dconv = dco * gb * silu
        nxt = nxt_ref[...].astype(F32)
        ngv = nxt[:, 3 * CONV_WIDTH:]
        dnext = dnxt_ref[...].astype(F32) * nxt[:, :CONV_WIDTH] * (ngv * _sigmoid(ngv))
        dnext = jnp.where(i == nsteps - 1, 0.0, dnext)
        n1, n2 = dnext[0:1], dnext[1:2]
        row = lax.broadcasted_iota(jnp.int32, dconv.shape, 0)
        p1 = jnp.where(row == CONV_ROWS - 1, n1, pltpu.roll(dconv, CONV_ROWS - 1, 0))
        p2 = jnp.where(row == CONV_ROWS - 1, n2, jnp.where(row == CONV_ROWS - 2, n1, pltpu.roll(dconv, CONV_ROWS - 2, 0)))
        dcu = w2 * dconv + w1 * p1 + w0 * p2
        dp_ref[...] = jnp.concatenate([dgb, dcu * u, dcu * gc, dgv], axis=1).astype(BF16)

        @pl.when(i == 0)
        def _():
            dw_ref[...] = jnp.zeros_like(dw_ref)

        taps = [jnp.sum(dconv * m, axis=0, keepdims=True) for m in (m2, m1, cu)]
        r8 = lax.broadcasted_iota(jnp.int32, dw_ref.shape, 0)
        dw_ref[...] += jnp.where(r8 == 0, taps[0], jnp.where(r8 == 1, taps[1], jnp.where(r8 == 2, taps[2], 0.0)))

    return pl.pallas_call(
        body, name="conv_bwd", grid=(nsteps,),
        in_specs=[pl.BlockSpec((CONV_ROWS, 4 * CONV_WIDTH), lambda i: (i, 1)),
                  pl.BlockSpec((HALO, 4 * CONV_WIDTH), lambda i: (jnp.maximum(i * per - 1, 0), 1)),
                  pl.BlockSpec((HALO, 4 * CONV_WIDTH), lambda i: (jnp.minimum((i + 1) * per, last_halo), 1)),
                  pl.BlockSpec((CONV_ROWS, CONV_WIDTH), lambda i: (i, 1)),
                  pl.BlockSpec((HALO, CONV_WIDTH), lambda i: (jnp.minimum((i + 1) * per, last_halo), 1)),
                  pl.BlockSpec((3, CONV_WIDTH), lambda i: (0, 0)), ANY],
        out_specs=[pl.BlockSpec((CONV_ROWS, 4 * CONV_WIDTH), lambda i: (i, 1)), pl.BlockSpec((8, CONV_WIDTH), lambda i: (0, 0))],
        out_shape=[jax.ShapeDtypeStruct(dproj.shape, dproj.dtype), jax.ShapeDtypeStruct((8, CONV_WIDTH), F32)],
        input_output_aliases={6: 0},
        compiler_params=_params("arbitrary"))(proj, proj, proj, dcat, dcat, conv_w, dproj)


GROUP_WIDTH = 8 * HEAD_DIM
GROUP_HEADS = 8
SLAB = 128
Q_COL = 0
KV_COL = 4
GATE_COL = 5
K_SLAB0 = ATTN_WIDTH // SLAB
V_SLAB0 = (ATTN_WIDTH + KV_WIDTH) // SLAB
KV_ROWS = 512


def _half_mask(shape, which):
    return (_lane(shape) // HEAD_DIM) == which


def _dup_head(slab, which):
    kept = jnp.where(_half_mask(slab.shape, which), slab, 0.0)
    return kept + pltpu.roll(kept, HEAD_DIM, 1)


def _stack_heads(x):
    parts = []
    for sl in range(GROUP_WIDTH // SLAB):
        slab = x[:, sl * SLAB:(sl + 1) * SLAB]
        parts += [jnp.where(_half_mask(slab.shape, e), slab, 0.0) for e in range(2)]
    return jnp.concatenate(parts, axis=0)


def _unstack_heads(y):
    slabs = []
    for sl in range(GROUP_WIDTH // SLAB):
        a, b = y[(2 * sl) * BLK:(2 * sl + 1) * BLK], y[(2 * sl + 1) * BLK:(2 * sl + 2) * BLK]
        slabs.append(jnp.where(_half_mask(a.shape, 0), a, b))
    return jnp.concatenate(slabs, axis=1)


def _q_prep(q, qw, cosf, sins, ones_bd):
    rstd = lax.rsqrt(_group_mean(q * q, ones_bd) + EPS)
    nrm = q * rstd
    y = nrm * qw
    return nrm, rstd, y * cosf + _partner64(y) * sins


def _valid_keys(n):
    qi = lax.broadcasted_iota(jnp.int32, (BLK, 2 * BLK), 0) + BLK
    kj = lax.broadcasted_iota(jnp.int32, (BLK, 2 * BLK), 1)
    return jnp.logical_and(jnp.logical_and(kj <= qi, qi - kj < BLK), jnp.logical_or(n > 0, kj >= BLK))


def _head_probs(raw_scores, sink, valid):
    sc = jnp.where(valid, raw_scores * (HEAD_DIM ** -0.5), -1e30)
    m = jnp.maximum(jnp.max(sc, axis=-1, keepdims=True), sink)
    p = jnp.exp(sc - m)
    psink = jnp.exp(sink - m)
    inv = 1.0 / (jnp.sum(p, axis=-1, keepdims=True) + psink)
    return p * inv, psink * inv


def _keys_values(kc_ref, kp_ref, vc_ref, vp_ref, kw, ones_k, cosc, sinc, cosp, sinp, which):
    def key(ref, cosf, sins):
        k = ref[...].astype(F32)
        y = k * lax.rsqrt(_group_mean(k * k, ones_k) + EPS) * kw
        return _dup_head(y * cosf + _partner64(y) * sins, which)

    kcat = jnp.concatenate([key(kp_ref, cosp, sinp), key(kc_ref, cosc, sinc)], axis=0).astype(BF16)
    vcat = jnp.concatenate([_dup_head(vp_ref[...].astype(F32), which), _dup_head(vc_ref[...].astype(F32), which)],
                           axis=0).astype(BF16)
    return kcat, vcat


def _swa_specs():
    prev = lambda n: jnp.maximum(n - 1, 0)
    slab = lambda col0, row: pl.BlockSpec((BLK, SLAB), lambda g, n: (row(n), col0 + g // 2))
    cur = lambda n: n
    tab = lambda row: pl.BlockSpec((BLK, SLAB), lambda g, n: (row(n), 0))
    full = lambda shape: pl.BlockSpec(shape, lambda g, n: (0,) * len(shape))
    return dict(
        sinks=pl.BlockSpec(memory_space=pltpu.SMEM),
        q=pl.BlockSpec((BLK, GROUP_WIDTH), lambda g, n: (n, Q_COL + g)),
        gate=pl.BlockSpec((BLK, GROUP_WIDTH), lambda g, n: (n, GATE_COL + g)),
        kc=slab(K_SLAB0, cur), kp=slab(K_SLAB0, prev), vc=slab(V_SLAB0, cur), vp=slab(V_SLAB0, prev),
        tab_c=tab(cur), tab_p=tab(prev),
        qw=full((1, GROUP_WIDTH)), kw=full((1, SLAB)), ones_q=full((GROUP_WIDTH, GROUP_WIDTH)), ones_k=full((SLAB, SLAB)),
        group=pl.BlockSpec((BLK, GROUP_WIDTH), lambda g, n: (n, g)))


def _swa_fwd(proj, sinks, qw, kw, cos, sins, ones_q, ones_k):
    s = proj.shape[0]
    nb = s // BLK
    sp = _swa_specs()

    def body(sink_ref, q_ref, kc_ref, kp_ref, vc_ref, vp_ref, gate_ref, cosc_ref, sinc_ref, cosp_ref, sinp_ref,
             qw_ref, kw_ref, oq_ref, ok_ref, ag_ref, o_ref):
        g, n = pl.program_id(0), pl.program_id(1)
        cosc, sinc = cosc_ref[...], sinc_ref[...]
        kcat, vcat = _keys_values(kc_ref, kp_ref, vc_ref, vp_ref, kw_ref[...], ok_ref[...], cosc, sinc,
                                  cosp_ref[...], sinp_ref[...], g % 2)
        _, _, qr = _q_prep(q_ref[...].astype(F32), qw_ref[...], _tile_lanes(cosc, 4), _tile_lanes(sinc, 4), oq_ref[...])
        scores = _dot_nt(_stack_heads(qr).astype(BF16), kcat)
        valid = _valid_keys(n)
        probs = [_head_probs(scores[j * BLK:(j + 1) * BLK], sink_ref[g * GROUP_HEADS + j], valid)[0].astype(BF16)
                 for j in range(GROUP_HEADS)]
        o = _unstack_heads(_dot(jnp.concatenate(probs, axis=0), vcat))
        gate = gate_ref[...].astype(F32)
        o_ref[...] = o.astype(BF16)
        ag_ref[...] = (o * (gate * _sigmoid(gate))).astype(BF16)

    shp = jax.ShapeDtypeStruct((s, ATTN_WIDTH), BF16)
    return pl.pallas_call(
        body, name="swa_fwd", grid=(KV_HEADS, nb),
        in_specs=[sp["sinks"], sp["q"], sp["kc"], sp["kp"], sp["vc"], sp["vp"], sp["gate"], sp["tab_c"], sp["tab_c"],
                  sp["tab_p"], sp["tab_p"], sp["qw"], sp["kw"], sp["ones_q"], sp["ones_k"]],
        out_specs=[sp["group"], sp["group"]], out_shape=[shp, shp],
        compiler_params=_params("parallel", "arbitrary"),
    )(sinks, proj, proj, proj, proj, proj, proj, cos, sins, cos, sins, qw, kw, ones_q, ones_k)


def _swa_bwd(proj, dag, sinks, qw, kw, cos, sins, ones_q, ones_k):
    s = proj.shape[0]
    nb = s // BLK
    sp = _swa_specs()

    def body(sink_ref, q_ref, kc_ref, kp_ref, vc_ref, vp_ref, gate_ref, dag_ref, cosc_ref, sinc_ref, cosp_ref, sinp_ref,
             qw_ref, kw_ref, oq_ref, ok_ref, dq_ref, dkc_ref, dkp_ref, dvc_ref, dvp_ref, dqw_ref, dsink_ref):
        g, n = pl.program_id(0), pl.program_id(1)
        cosc, sinc = cosc_ref[...], sinc_ref[...]
        kcat, vcat = _keys_values(kc_ref, kp_ref, vc_ref, vp_ref, kw_ref[...], ok_ref[...], cosc, sinc,
                                  cosp_ref[...], sinp_ref[...], g % 2)
        cosq, sinq = _tile_lanes(cosc, 4), _tile_lanes(sinc, 4)
        qwv = qw_ref[...]
        nrm, rstd, qr = _q_prep(q_ref[...].astype(F32), qwv, cosq, sinq, oq_ref[...])
        gate = gate_ref[...].astype(F32)
        do = dag_ref[...].astype(F32) * (gate * _sigmoid(gate))
        q_stack = _stack_heads(qr).astype(BF16)
        do_stack = _stack_heads(do).astype(BF16)
        scores = _dot_nt(q_stack, kcat)
        dprobs = _dot_nt(do_stack, vcat)
        valid = _valid_keys(n)
        probs, dscores, dsinks = [], [], []
        for j in range(GROUP_HEADS):
            rows = slice(j * BLK, (j + 1) * BLK)
            p, psink = _head_probs(scores[rows], sink_ref[g * GROUP_HEADS + j], valid)
            dp = dprobs[rows]
            delta = jnp.sum(p * dp, axis=-1, keepdims=True)
            probs.append(p.astype(BF16))
            dscores.append((p * (dp - delta) * (HEAD_DIM ** -0.5)).astype(BF16))
            dsinks.append(-jnp.sum(psink * delta, axis=0, keepdims=True))
        ds = jnp.concatenate(dscores, axis=0)
        dk = _dot_tn(ds, q_stack)
        dv = _dot_tn(jnp.concatenate(probs, axis=0), do_stack)
        dk = dk + pltpu.roll(dk, HEAD_DIM, 1)
        dv = dv + pltpu.roll(dv, HEAD_DIM, 1)
        dkp_ref[0], dkc_ref[0] = dk[:BLK], dk[BLK:]
        dvp_ref[0], dvc_ref[0] = dv[:BLK], dv[BLK:]
        dqr = _unstack_heads(_dot(ds, kcat))
        dy = dqr * cosq + _partner64(dqr * sinq)
        dn = dy * qwv
        dq_ref[...] = (rstd * (dn - nrm * _group_mean(dn * nrm, oq_ref[...]))).astype(BF16)

        @pl.when(n == 0)
        def _():
            dqw_ref[...] = jnp.zeros_like(dqw_ref)
            dsink_ref[...] = jnp.zeros_like(dsink_ref)

        dqw_ref[0] += jnp.sum(dy * nrm, axis=0, keepdims=True)
        r8 = lax.broadcasted_iota(jnp.int32, (8, SLAB), 0)
        upd = jnp.zeros((8, SLAB), F32)
        for j in range(GROUP_HEADS):
            upd = jnp.where(r8 == j, dsinks[j], upd)
        dsink_ref[0] += upd

    cur_out = pl.BlockSpec((1, BLK, SLAB), lambda g, n: (g, n, 0))
    prev_out = pl.BlockSpec((1, BLK, SLAB), lambda g, n: (g, (n + nb - 1) % nb, 0))
    kv_shape = jax.ShapeDtypeStruct((KV_HEADS, s, SLAB), F32)
    return pl.pallas_call(
        body, name="swa_bwd", grid=(KV_HEADS, nb),
        in_specs=[sp["sinks"], sp["q"], sp["kc"], sp["kp"], sp["vc"], sp["vp"], sp["gate"], sp["group"],
                  sp["tab_c"], sp["tab_c"], sp["tab_p"], sp["tab_p"], sp["qw"], sp["kw"], sp["ones_q"], sp["ones_k"]],
        out_specs=[sp["q"], cur_out, prev_out, cur_out, prev_out,
                   pl.BlockSpec((1, 8, GROUP_WIDTH), lambda g, n: (g, 0, 0)), pl.BlockSpec((1, 8, SLAB), lambda g, n: (g, 0, 0))],
        out_shape=[jax.ShapeDtypeStruct((s, ODD_IN), BF16), kv_shape, kv_shape, kv_shape, kv_shape,
                   jax.ShapeDtypeStruct((KV_HEADS, 8, GROUP_WIDTH), F32), jax.ShapeDtypeStruct((KV_HEADS, 8, SLAB), F32)],
        compiler_params=_params("parallel", "arbitrary"),
    )(sinks, proj, proj, proj, proj, proj, proj, dag, cos, sins, cos, sins, qw, kw, ones_q, ones_k)


def _swa_bwd_kv(proj, dproj, o, dag, dkc, dkp, dvc, dvp, kw, cos, sins, ones_k):
    s = proj.shape[0]
    rows = min(KV_ROWS, s)

    def body(kv_ref, gate_ref, o_ref, dag_ref, dkc_ref, dkp_ref, dvc_ref, dvp_ref, kw_ref, cos_ref, sin_ref, ok_ref,
             dp_in, dp_ref, dkw_ref):
        del dp_in
        i, j = pl.program_id(0), pl.program_id(1)

        @pl.when(jnp.logical_and(i == 0, j == 0))
        def _():
            dkw_ref[...] = jnp.zeros_like(dkw_ref)

        @pl.when(j == 0)
        def _():
            def assemble(cur_ref, prv_ref):
                tot = [cur_ref[h] + prv_ref[h] for h in range(KV_HEADS)]
                first = _half_mask(tot[0].shape, 0)
                return jnp.concatenate([jnp.where(first, tot[0], tot[1]), jnp.where(first, tot[2], tot[3])], axis=1)

            dkr = assemble(dkc_ref, dkp_ref)
            dv = assemble(dvc_ref, dvp_ref)
            cosf, sinf = _tile_lanes(cos_ref[...], 2), _tile_lanes(sin_ref[...], 2)
            k = kv_ref[:, :KV_WIDTH].astype(F32)
            rstd = lax.rsqrt(_group_mean(k * k, ok_ref[...]) + EPS)
            nrm = k * rstd
            dy = dkr * cosf + _partner64(dkr * sinf)
            dn = dy * kw_ref[...]
            dk = rstd * (dn - nrm * _group_mean(dn * nrm, ok_ref[...]))
            dp_ref[...] = jnp.concatenate([dk, dv], axis=1).astype(BF16)
            dkw_ref[...] += jnp.sum(dy * nrm, axis=0, keepdims=True)

        @pl.when(j > 0)
        def _():
            gate = gate_ref[...].astype(F32)
            sg = _sigmoid(gate)
            dp_ref[...] = (dag_ref[...].astype(F32) * o_ref[...].astype(F32) * (sg * (1.0 + gate * (1.0 - sg)))).astype(BF16)

    acc = pl.BlockSpec((KV_HEADS, rows, SLAB), lambda i, j: (0, i, 0))
    group = pl.BlockSpec((rows, GROUP_WIDTH), lambda i, j: (i, jnp.maximum(j - 1, 0)))
    tab = pl.BlockSpec((rows, SLAB), lambda i, j: (i, 0))
    return pl.pallas_call(
        body, name="swa_bwd_kv", grid=(s // rows, 5),
        in_specs=[pl.BlockSpec((rows, GROUP_WIDTH), lambda i, j: (i, KV_COL)),
                  pl.BlockSpec((rows, GROUP_WIDTH), lambda i, j: (i, GATE_COL + jnp.maximum(j - 1, 0))),
                  group, group, acc, acc, acc, acc,
                  pl.BlockSpec((1, KV_WIDTH), lambda i, j: (0, 0)), tab, tab,
                  pl.BlockSpec((KV_WIDTH, KV_WIDTH), lambda i, j: (0, 0)), ANY],
        out_specs=[pl.BlockSpec((rows, GROUP_WIDTH), lambda i, j: (i, KV_COL + j)), pl.BlockSpec((8, KV_WIDTH), lambda i, j: (0, 0))],
        out_shape=[jax.ShapeDtypeStruct(dproj.shape, dproj.dtype), jax.ShapeDtypeStruct((8, KV_WIDTH), F32)],
        input_output_aliases={12: 0},
        compiler_params=_params("arbitrary", "arbitrary"),
    )(proj, proj, o, dag, dkc, dkp, dvc, dvp, kw, cos, sins, ones_k, dproj)


def _place():
    x, y, c = lax.axis_index("x"), lax.axis_index("y"), lax.axis_index("c")
    return x, y, c


OTHER_CHIPS = ((1, 0), (0, 1), (1, 1))


def _half_rows(ref, half, rows):
    return ref.at[pl.ds(pl.multiple_of(half * (rows // 2), 8), rows // 2)]


DMA_CHUNK_BYTES = 1 << 20
BF16_TILE_ROWS = 16


def _n_chunks(ref):
    rows = ref.shape[-2]
    nbytes = math.prod(ref.shape) * jnp.dtype(ref.dtype).itemsize
    n = 1
    while 2 * n * DMA_CHUNK_BYTES <= nbytes and rows % (2 * n * BF16_TILE_ROWS) == 0:
        n *= 2
    return n


def _row_chunk(ref, k, n):
    rows = ref.shape[-2] // n
    return ref.at[pl.ds(k * rows, rows)] if len(ref.shape) == 2 else ref.at[:, pl.ds(k * rows, rows)]


def _push(src, dst, send_sem, recv_sem, device_id):
    n = _n_chunks(src)
    for k in range(n):
        pltpu.make_async_remote_copy(src_ref=_row_chunk(src, k, n), dst_ref=_row_chunk(dst, k, n), send_sem=send_sem,
                                     recv_sem=recv_sem, device_id=device_id, device_id_type=MESH).start()
    return pltpu.make_async_remote_copy(src_ref=src, dst_ref=dst, send_sem=send_sem, recv_sem=recv_sem,
                                        device_id=device_id, device_id_type=MESH)


def _copy(src, dst, sem):
    n = _n_chunks(src)
    for k in range(n):
        pltpu.make_async_copy(_row_chunk(src, k, n), _row_chunk(dst, k, n), sem).start()
    return pltpu.make_async_copy(src, dst, sem)


HBM = pl.BlockSpec(memory_space=pltpu.HBM)
SEM = pl.BlockSpec(memory_space=pltpu.SEMAPHORE)
SPLIT_COPY_EFFECT = pltpu.SideEffectType.DATAFLOW_SIDE_EFFECTING


def _in_hbm(a):
    return pltpu.with_memory_space_constraint(a, pltpu.HBM)


def _start_copies(name, arrays, plan, n_copies, after=None):
    n = len(arrays)

    def body(*refs):
        send_sem, recv_sem = refs[-n - 3], refs[-n - 2]
        for k, (src, dst, peer) in enumerate(plan(refs[:n])):
            _push(src, dst, send_sem.at[k], recv_sem.at[k], peer)
        refs[-1][...] = jnp.zeros_like(refs[-1])

    dma = pltpu.SemaphoreType.DMA((n_copies,))
    outs = pl.pallas_call(
        body, name=name,
        out_shape=(dma, dma, *[pltpu.HBM(a.shape, a.dtype) for a in arrays], jax.ShapeDtypeStruct((8, 128), F32)),
        in_specs=[HBM] * n + ([ANY] if after is not None else []),
        out_specs=(SEM, SEM, *[HBM] * n, pl.BlockSpec(memory_space=pltpu.VMEM)),
        input_output_aliases={i: i + 2 for i in range(n)},
        compiler_params=pltpu.CompilerParams(has_side_effects=SPLIT_COPY_EFFECT),
    )(*[_in_hbm(a) for a in arrays], *((after,) if after is not None else ()))
    return outs[0], outs[1], list(outs[2:2 + n]), outs[-1]


def _wait_copies(name, send_sem, recv_sem, arrays, plan, after):
    n = len(arrays)

    def body(*refs):
        send_ref, recv_ref = refs[n], refs[n + 1]
        for k, (src, dst, peer) in enumerate(plan(refs[:n])):
            cp = pltpu.make_async_remote_copy(src_ref=src, dst_ref=dst, send_sem=send_ref.at[k], recv_sem=recv_ref.at[k],
                                              device_id=peer, device_id_type=MESH)
            cp.wait_send()
            cp.wait_recv()

    return list(pl.pallas_call(
        body, name=name, out_shape=tuple(pltpu.HBM(a.shape, a.dtype) for a in arrays),
        in_specs=[HBM] * n + [SEM, SEM, ANY], out_specs=tuple([HBM] * n),
        input_output_aliases={i: i for i in range(n)},
        compiler_params=pltpu.CompilerParams(has_side_effects=SPLIT_COPY_EFFECT),
    )(*arrays, send_sem, recv_sem, after))


def _gather_region(full, kind, chip, half=None):
    if kind == "whole":
        return full.at[chip]
    if kind == "col":
        rows, width = full.shape[0], full.shape[1] // N_CHIPS
        piece = full.at[:, pl.ds(pl.multiple_of(chip * width, 128), width)]
    else:
        rows = full.shape[0] // N_CHIPS
        piece = full.at[pl.ds(pl.multiple_of(chip * rows, BF16_TILE_ROWS), rows)]
    return piece if half is None else _half_rows(piece, half, rows)


def _gather_plan(kinds):
    def plan(fulls):
        x, y, c = _place()
        copies = []
        for fx, fy in OTHER_CHIPS:
            for full, kind in zip(fulls, kinds):
                mine = _gather_region(full, kind, 2 * x + y, c)
                copies.append((mine, mine, (x ^ fx, y ^ fy, c)))
        return copies

    return plan


def _gather_start(name, fulls, kinds, after=None):
    return _start_copies(name, list(fulls), _gather_plan(kinds), 3 * len(kinds), after)


def _gather_finish(name, started, kinds, after):
    send_sem, recv_sem, fulls, _ = started
    n = len(kinds)
    fulls = _wait_copies(name + "_wait", send_sem, recv_sem, fulls, _gather_plan(kinds), after)
    split = [i for i, kind in enumerate(kinds) if kind != "whole"]

    def body(*refs):
        out = refs[n:2 * n]
        send, recv = refs[2 * n:]
        x, y, c = _place()
        pushes = []
        for r, (fx, fy) in enumerate(OTHER_CHIPS):
            chip = 2 * (x ^ fx) + (y ^ fy)
            for j, i in enumerate(split):
                landed = _gather_region(out[i], kinds[i], chip, c)
                pushes.append(_push(landed, landed, send.at[r * len(split) + j], recv.at[r * len(split) + j], (x, y, 1 - c)))
        for cp in pushes:
            cp.wait_recv()
        for cp in pushes:
            cp.wait_send()

    dma = pltpu.SemaphoreType.DMA
    return pl.pallas_call(
        body, name=name + "_share", in_specs=[ANY] * n, out_specs=[ANY] * n,
        out_shape=[jax.ShapeDtypeStruct(a.shape, a.dtype) for a in fulls],
        input_output_aliases={i: i for i in range(n)},
        scratch_shapes=[dma((3 * len(split),)), dma((3 * len(split),))],
        compiler_params=pltpu.CompilerParams(has_side_effects=True),
    )(*fulls)


def _allreduce_small(v):
    def body(v_ref, out_ref, buf, send_sems, recv_sems):
        x, y, c = _place()
        me = 4 * x + 2 * y + c
        buf[me] = v_ref[...]
        copies = []
        for r in range(1, N_DEV):
            peer = (x ^ (r >> 2), y ^ ((r >> 1) & 1), c ^ (r & 1))
            cp = pltpu.make_async_remote_copy(src_ref=v_ref, dst_ref=buf.at[me], send_sem=send_sems.at[r - 1],
                                              recv_sem=recv_sems.at[r - 1], device_id=peer, device_id_type=MESH)
            cp.start()
            copies.append(cp)
        for cp in copies:
            cp.wait_recv()
        for cp in copies:
            cp.wait_send()
        total = buf[0]
        for d in range(1, N_DEV):
            total = total + buf[d]
        out_ref[...] = total

    vm = pl.BlockSpec(memory_space=pltpu.VMEM)
    return pl.pallas_call(
        body, name="allreduce_small", in_specs=[vm], out_specs=vm, out_shape=jax.ShapeDtypeStruct(v.shape, v.dtype),
        scratch_shapes=[pltpu.VMEM((N_DEV,) + v.shape, v.dtype), pltpu.SemaphoreType.DMA((N_DEV - 1,)),
                        pltpu.SemaphoreType.DMA((N_DEV - 1,))],
        compiler_params=pltpu.CompilerParams(has_side_effects=True),
    )(v)


def _exchange_halves(grads, name):
    n = len(grads)

    def body(*refs):
        g, theirs = refs[:n], refs[n:2 * n]
        send_sem, recv_sem = refs[2 * n:]
        x, y, c = _place()
        copies = []
        for i in range(n):
            half = g[i].shape[1] // 2
            src = g[i].at[:, pl.ds(pl.multiple_of((1 - c) * half, BF16_TILE_ROWS), half)]
            copies.append(_push(src, theirs[i], send_sem.at[i], recv_sem.at[i], (x, y, 1 - c)))
        for cp in copies:
            cp.wait_recv()
            cp.wait_send()

    dma = pltpu.SemaphoreType.DMA
    return pl.pallas_call(
        body, name=name, in_specs=[ANY] * n, out_specs=[ANY] * n,
        out_shape=[jax.ShapeDtypeStruct((a.shape[0], a.shape[1] // 2, a.shape[2]), a.dtype) for a in grads],
        scratch_shapes=[dma((n,)), dma((n,))],
        compiler_params=pltpu.CompilerParams(has_side_effects=True),
    )(*grads)


def _pair_sum(g, theirs, core, name):
    pieces, half, cols = theirs.shape
    tr = min(half, 256)
    per = half // tr

    def body(core_ref, g_ref, t_ref, o_ref):
        del core_ref
        o_ref[...] = (g_ref[...].astype(F32) + t_ref[...].astype(F32)).astype(BF16)

    spec = pl.BlockSpec((1, tr, cols), lambda p, i, core_ref: (p, i, 0))
    return pl.pallas_call(
        body, name=name, out_shape=jax.ShapeDtypeStruct(theirs.shape, BF16),
        grid_spec=pltpu.PrefetchScalarGridSpec(
            num_scalar_prefetch=1, grid=(pieces, per),
            in_specs=[pl.BlockSpec((1, tr, cols), lambda p, i, core_ref: (p, core_ref[0] * per + i, 0)), spec],
            out_specs=spec),
        compiler_params=_params("parallel", "parallel"))(core, g, theirs)


def _scatter_plan(n):
    def plan(refs):
        parts, stacks = refs[:n], refs[n:]
        x, y, c = _place()
        copies = []
        for fx, fy in OTHER_CHIPS:
            chip = 2 * (x ^ fx) + (y ^ fy)
            for part, stack in zip(parts, stacks):
                if part.shape[0] == N_CHIPS:
                    piece = part.at[chip]
                else:
                    width = part.shape[2] // N_CHIPS
                    piece = part.at[0].at[:, pl.ds(pl.multiple_of(chip * width, 128), width)]
                copies.append((piece, stack.at[2 * x + y], (x ^ fx, y ^ fy, c)))
        return copies

    return plan


def _scatter_start(name, parts, after=None):
    def landing(a):
        return (N_CHIPS, a.shape[1], a.shape[2] if a.shape[0] == N_CHIPS else a.shape[2] // N_CHIPS)

    stacks = [lax.empty(landing(a), a.dtype) for a in parts]
    return _start_copies(name, list(parts) + stacks, _scatter_plan(len(parts)), 3 * len(parts), after)


def _scatter_finish(name, started, after):
    send_sem, recv_sem, arrays, _ = started
    n = len(arrays) // 2
    arrays = _wait_copies(name + "_wait", send_sem, recv_sem, arrays, _scatter_plan(n), after)
    return arrays[:n], arrays[n:]


def _sum_chips(part, stack, place, name):
    _, r, c = stack.shape
    tr = 256
    per = r // tr

    def body(place_ref, own_ref, a_ref, b_ref, c_ref, o_ref):
        del place_ref
        total = own_ref[0].astype(F32)
        for ref in (a_ref, b_ref, c_ref):
            total = total + ref[0].astype(F32)
        o_ref[...] = total

    if part.shape[0] == N_CHIPS:
        own = pl.BlockSpec((1, tr, c), lambda i, pr: (pr[1], i, 0))
    else:
        own = pl.BlockSpec((1, tr, c), lambda i, pr: (0, i, pr[1]))
    other = lambda flip: pl.BlockSpec((1, tr, c), lambda i, pr: (pr[1] ^ flip, i, 0))
    return pl.pallas_call(
        body, name=name, out_shape=jax.ShapeDtypeStruct((2 * r, c), F32),
        grid_spec=pltpu.PrefetchScalarGridSpec(
            num_scalar_prefetch=1, grid=(per,), in_specs=[own, other(2), other(1), other(3)],
            out_specs=pl.BlockSpec((tr, c), lambda i, pr: (pr[0] * per + i, 0))),
        compiler_params=_params("parallel"))(place, part, stack, stack, stack)


def _share_halves(fulls):
    n = len(fulls)

    def body(*refs):
        dst = refs[n:2 * n]
        send_sem, recv_sem = refs[2 * n:]
        x, y, c = _place()
        copies = []
        for i in range(n):
            mine = _half_rows(dst[i], c, dst[i].shape[0])
            copies.append(_push(mine, mine, send_sem.at[i], recv_sem.at[i], (x, y, 1 - c)))
        for cp in copies:
            cp.wait_recv()
            cp.wait_send()

    dma = pltpu.SemaphoreType.DMA
    return pl.pallas_call(
        body, name="share_halves", in_specs=[ANY] * n, out_specs=[ANY] * n,
        out_shape=[jax.ShapeDtypeStruct(a.shape, a.dtype) for a in fulls],
        input_output_aliases={i: i for i in range(n)}, scratch_shapes=[dma((n,)), dma((n,))],
        compiler_params=pltpu.CompilerParams(has_side_effects=True),
    )(*fulls)


MM = dict(tm=1024, tn=1024, tk=2048)


def _local_step(x, target, ev_norm_w, q_norm_w, k_norm_w, sinks, start_token, weights_first, weights_late, emit):
    s = x.shape[0]
    cos_r, sin_r = _rope_tables(s, RET_DIM)
    cos_a, sin_a = _rope_tables(s, HEAD_DIM)
    cos_a = jnp.tile(cos_a, (1, 4))
    sins_a = jnp.tile(jnp.concatenate([-sin_a, sin_a], axis=1), (1, 2))
    tables = _retention_tables()
    ones_q, ones_k, ones_kv = _block_diag_mean(GROUP_WIDTH), _block_diag_mean(SLAB), _block_diag_mean(KV_WIDTH)
    qw_g = jnp.tile(q_norm_w, (1, GROUP_WIDTH // HEAD_DIM))
    kw_s = jnp.tile(k_norm_w, (1, SLAB // HEAD_DIM))
    kw_kv = jnp.tile(k_norm_w, (1, KV_WIDTH // HEAD_DIM))
    sinks1 = sinks.reshape(Q_HEADS)

    h0 = _rmsnorm(x, ev_norm_w, "norm0", after=start_token)
    w_in0, conv_w, od_norm_w, token = weights_first(h0)
    proj0 = _mm(h0, w_in0, mode="nn", out_dtype=BF16, name="proj0", after=token, **MM)
    o_ret, cat, states = _retention_fwd(proj0, cos_r, sin_r, tables)
    cat = _conv_fwd(proj0, conv_w, cat)
    w_out0, w_in1, w_out1 = weights_late(cat)
    x1 = _mm(cat, w_out0, mode="nn", out_dtype=F32, name="out0", add=x, **MM)
    h1 = _rmsnorm(x1, od_norm_w, "norm1")
    proj1 = _mm(h1, w_in1, mode="nn", out_dtype=BF16, name="proj1", tm=1024, tn=1152, tk=2048)
    ag, o_att = _swa_fwd(proj1, sinks1, qw_g, kw_s, cos_a, sins_a, ones_q, ones_k)
    x2 = _mm(ag, w_out1, mode="nn", out_dtype=F32, name="out1", add=x1, **MM)
    dy, dy_b, sq = _loss_grad(x2, target)

    g_w_out1 = _mm(ag, dy_b, mode="tn", out_dtype=BF16, name="g_w_out1", **MM)
    dag = _mm(dy_b, w_out1, mode="nt", out_dtype=BF16, name="d_ag", **MM)
    dproj1, dkc, dkp, dvc, dvp, dqw, dsink = _swa_bwd(proj1, dag, sinks1, qw_g, kw_s, cos_a, sins_a, ones_q, ones_k)
    dproj1, dkw = _swa_bwd_kv(proj1, dproj1, o_att, dag, dkc, dkp, dvc, dvp, kw_kv, cos_a, sins_a, ones_kv)
    g_w_in1 = _mm(h1, dproj1, mode="tn", out_dtype=BF16, name="g_w_in1", tm=1024, tn=1152, tk=2048)
    token = emit("layer1", (("od_w_in", g_w_in1, "col"), ("od_w_out", g_w_out1, "row")))
    dh1 = _mm(dproj1, w_in1, mode="nt", out_dtype=BF16, name="d_h1", tm=1024, tn=1024, tk=2304, after=token)
    dx1, dx1_b, g_norm1 = _rmsnorm_bwd(x1, od_norm_w, dh1, dy, "norm1_bwd", True)

    g_w_out0 = _mm(cat, dx1_b, mode="tn", out_dtype=BF16, name="g_w_out0", **MM)
    token = emit("out0", (("ev_w_out", g_w_out0, "row"),))
    dcat = _mm(dx1_b, w_out0, mode="nt", out_dtype=BF16, name="d_cat", after=token, **MM)
    dproj0 = _retention_bwd(proj0, o_ret, states, dcat, cos_r, sin_r, tables)
    dproj0, g_conv = _conv_bwd(proj0, dcat, conv_w, dproj0)
    g_w_in0 = _mm(h0, dproj0, mode="tn", out_dtype=BF16, name="g_w_in0", **MM)
    token = emit("in0", (("ev_w_in", g_w_in0, "col"),))
    dh0 = _mm(dproj0, w_in0, mode="nt", out_dtype=BF16, name="d_h0", after=token, **MM)
    grad_x, g_norm0 = _rmsnorm_bwd(x, ev_norm_w, dh0, dx1, "norm0_bwd", False)

    g_qw = dqw[:, 0, :].reshape(Q_HEADS, HEAD_DIM).sum(axis=0)
    g_kw = dkw[0].reshape(KV_HEADS, HEAD_DIM).sum(axis=0)
    g_sinks = dsink[:, :, 0].reshape(Q_HEADS)
    small = dict(ev_norm=g_norm0[0], od_norm=g_norm1[0], conv=g_conv[:3], qw=g_qw, kw=g_kw, sinks=g_sinks)
    return sq[0, 0], grad_x, small


def _pack_small_grads(small):
    pad = lambda v: jnp.pad(v, (0, D_MODEL - v.shape[0]))
    tail = pad(jnp.concatenate([small["qw"], small["kw"], small["sinks"]]))
    rows = [small["ev_norm"], small["od_norm"]] + [pad(small["conv"][t]) for t in range(3)] + [tail]
    rows += [jnp.zeros((D_MODEL,), F32)] * (8 - len(rows))
    return jnp.stack(rows)


class _ReduceScatter:
    def __init__(self, place):
        self.place = place
        self.started = []

    def send(self, tag, grads):
        pieces = [g[None] if kind == "col" else g.reshape(N_CHIPS, g.shape[0] // N_CHIPS, g.shape[1]) for _, g, kind in grads]
        theirs = _exchange_halves(pieces, "exchange_halves_" + tag)
        parts = [_pair_sum(g, t, self.place[:1], "pair_sum_" + nm) for g, t, (nm, _, _) in zip(pieces, theirs, grads)]
        started = _scatter_start("scatter_" + tag, parts)
        self.started.append((tag, [nm for nm, _, _ in grads], started))
        return started[3]

    def finish(self, after):
        names, sums = [], []
        for tag, group, started in self.started:
            parts, stacks = _scatter_finish("scatter_" + tag, started, after)
            sums += [_sum_chips(p, s, self.place, "chip_sum_" + nm) for p, s, nm in zip(parts, stacks, group)]
            names += group
        return dict(zip(names, _share_halves(sums)))


def kernel(x, ev_norm_w, ev_w_in, ev_conv_w, ev_w_out, od_norm_w, od_w_in, od_q_norm_w, od_k_norm_w, od_sinks, od_w_out, loss_target, m_ev_norm_w, m_ev_w_in, m_ev_conv_w, m_ev_w_out, m_od_norm_w, m_od_w_in, m_od_q_norm_w, m_od_k_norm_w, m_od_sinks, m_od_w_out, v_ev_norm_w, v_ev_w_in, v_ev_conv_w, v_ev_w_out, v_od_norm_w, v_od_w_in, v_od_q_norm_w, v_od_k_norm_w, v_od_sinks, v_od_w_out):
    my_chip = 2 * lax.axis_index("x") + lax.axis_index("y")
    place = jnp.stack([lax.axis_index("c"), my_chip]).astype(jnp.int32)
    shard_w = D_MODEL // N_CHIPS
    conv_shard = CONV_WIDTH // N_CHIPS

    small_in = jnp.zeros((8, shard_w), F32)
    small_in = small_in.at[0].set(od_norm_w[0]).at[1:4, :conv_shard].set(ev_conv_w[0])
    small_in = lax.dynamic_update_slice(jnp.zeros((N_CHIPS, 8, shard_w), F32), small_in[None], (my_chip, 0, 0))
    chip = place[1:]
    first_kinds, late_kinds = ("col", "whole"), ("row", "col", "row")
    first = _gather_start("gather_first", [_cast_into_gathered(ev_w_in[0], "col", chip, "cast_w_in0"), small_in], first_kinds)
    late_own = [_cast_into_gathered(ev_w_out[0], "row", chip, "cast_w_out0"),
                _cast_into_gathered(od_w_in[0], "col", chip, "cast_w_in1"),
                _cast_into_gathered(od_w_out[0], "row", chip, "cast_w_out1")]
    late = []

    def weights_first(after):
        w_in0, small_all = _gather_finish("gather_first", first, first_kinds, after)
        late.append(_gather_start("gather_late", late_own, late_kinds, after=w_in0))
        od_norm_full = small_all[:, 0, :].reshape(1, D_MODEL)
        conv_full = jnp.transpose(small_all[:, 1:4, :conv_shard], (1, 0, 2)).reshape(3, CONV_WIDTH)
        return w_in0, conv_full, od_norm_full, late[0][3]

    def weights_late(after):
        return _gather_finish("gather_late", late[0], late_kinds, after)

    reduce_scatter = _ReduceScatter(place)
    sq, grad_x, small = _local_step(x[0], loss_target[0], ev_norm_w, od_q_norm_w, od_k_norm_w, od_sinks, first[3],
                                    weights_first, weights_late, reduce_scatter.send)
    loss = lax.psum(0.5 * sq / D_MODEL, ("x", "y", "c"))

    big = reduce_scatter.finish(grad_x)
    g_ev_w_in, g_ev_w_out, g_od_w_in, g_od_w_out = big["ev_w_in"], big["ev_w_out"], big["od_w_in"], big["od_w_out"]
    tot = _allreduce_small(_pack_small_grads(small))
    g_ev_norm = tot[0:1]
    g_od_norm = lax.dynamic_slice(tot, (1, my_chip * shard_w), (1, shard_w))
    g_conv = lax.dynamic_slice(tot, (2, my_chip * conv_shard), (3, conv_shard))
    g_qw, g_kw, g_sinks = tot[5:6, 0:HEAD_DIM], tot[5:6, HEAD_DIM:2 * HEAD_DIM], tot[5:6, 2 * HEAD_DIM:2 * HEAD_DIM + Q_HEADS]

    upd = {}
    upd["ev_w_in"] = _adamw(ev_w_in[0], g_ev_w_in, m_ev_w_in[0], v_ev_w_in[0], "adamw_ev_w_in")
    upd["ev_w_out"] = _adamw(ev_w_out[0], g_ev_w_out, m_ev_w_out[0], v_ev_w_out[0], "adamw_ev_w_out")
    upd["od_w_in"] = _adamw(od_w_in[0], g_od_w_in, m_od_w_in[0], v_od_w_in[0], "adamw_od_w_in")
    upd["od_w_out"] = _adamw(od_w_out[0], g_od_w_out, m_od_w_out[0], v_od_w_out[0], "adamw_od_w_out")
    smalls = (("ev_norm_w", ev_norm_w, g_ev_norm, m_ev_norm_w, v_ev_norm_w),
              ("ev_conv_w", ev_conv_w, g_conv, m_ev_conv_w, v_ev_conv_w),
              ("od_norm_w", od_norm_w, g_od_norm, m_od_norm_w, v_od_norm_w),
              ("od_q_norm_w", od_q_norm_w, g_qw, m_od_q_norm_w, v_od_q_norm_w),
              ("od_k_norm_w", od_k_norm_w, g_kw, m_od_k_norm_w, v_od_k_norm_w),
              ("od_sinks", od_sinks, g_sinks, m_od_sinks, v_od_sinks))
    sizes = [w.size for _, w, _, _, _ in smalls]
    padded = 8 * 128 * math.ceil(sum(sizes) / (8 * 128))
    pack = lambda arrs, fill: jnp.concatenate(
        [a.reshape(-1) for a in arrs] + [jnp.full((padded - sum(sizes),), fill, F32)]).reshape(8, padded // 8)
    packed = _adamw(pack([w for _, w, _, _, _ in smalls], 0.0), pack([g for _, _, g, _, _ in smalls], 0.0),
                    pack([m for _, _, _, m, _ in smalls], 0.0), pack([v for _, _, _, _, v in smalls], 1.0), "adamw_small")
    offs = [sum(sizes[:i]) for i in range(len(sizes))]
    grads = {"ev_w_in": g_ev_w_in[None], "ev_w_out": g_ev_w_out[None], "od_w_in": g_od_w_in[None], "od_w_out": g_od_w_out[None]}
    for (nm, w, g, _, _), off, size in zip(smalls, offs, sizes):
        upd[nm] = tuple(p.reshape(-1)[off:off + size].reshape(w.shape) for p in packed)
        grads[nm] = g.reshape(w.shape)
    for nm in ("ev_w_in", "ev_w_out", "od_w_in", "od_w_out"):
        upd[nm] = tuple(u[None] for u in upd[nm])
    order = ("ev_norm_w", "ev_w_in", "ev_conv_w", "ev_w_out", "od_norm_w", "od_w_in", "od_q_norm_w", "od_k_norm_w", "od_sinks", "od_w_out")
    return (loss, grad_x[None], *[grads[nm] for nm in order], *[upd[nm][0] for nm in order],
            *[upd[nm][1] for nm in order], *[upd[nm][2] for nm in order])
```

```python
import functools
import math

import jax
import jax.numpy as jnp
from jax import lax
from jax.experimental import pallas as pl
from jax.experimental.pallas import tpu as pltpu

F32 = jnp.float32
BF16 = jnp.bfloat16

D_MODEL = 2048
RET_HEADS = 4
RET_DIM = 256
RET_WIDTH = 1024
CONV_WIDTH = 1024
EVEN_IN = 8192
Q_HEADS = 32
HEAD_DIM = 64
KV_HEADS = 4
KV_WIDTH = 256
ATTN_WIDTH = 2048
ODD_IN = 4608
BLK = 128
ROPE_THETA = 10000.0
EPS = 1e-6
ADAM_LR = 0.001
ADAM_B1 = 0.9
ADAM_B2 = 0.999
ADAM_EPS = 1e-08
ADAM_WD = 0.01
ADAM_STEP = 10
N_CHIPS = 4
N_DEV = 8
VMEM_LIMIT_BYTES = 48 * 1024 * 1024
MESH = pl.DeviceIdType.MESH
ANY = pl.BlockSpec(memory_space=pl.ANY)


def _params(*sem):
    return pltpu.CompilerParams(dimension_semantics=sem, vmem_limit_bytes=VMEM_LIMIT_BYTES)


def _dot(a, b):
    return jnp.dot(a, b, preferred_element_type=F32)


def _dot_nt(a, b):
    return lax.dot_general(a, b, (((1,), (1,)), ((), ())), preferred_element_type=F32)


def _dot_tn(a, b):
    return lax.dot_general(a, b, (((0,), (0,)), ((), ())), preferred_element_type=F32)


def _sigmoid(x):
    return 1.0 / (1.0 + jnp.exp(-x))


def _mm(a, b, *, mode, tm, tn, tk, out_dtype, name, add=None, after=None):
    if mode == "nn":
        (m, k), n = a.shape, b.shape[1]
    elif mode == "nt":
        (m, k), n = a.shape, b.shape[0]
    else:
        (k, m), n = a.shape, b.shape[1]
    tm, tn, tk = min(tm, m), min(tn, n), min(tk, k)
    assert m % tm == 0 and n % tn == 0 and k % tk == 0, (name, m, n, k)
    nk = k // tk
    dot = {"nn": _dot, "nt": _dot_nt, "tn": _dot_tn}[mode]
    a_spec = (pl.BlockSpec((tk, tm), lambda i, j, kk: (kk, i)) if mode == "tn"
              else pl.BlockSpec((tm, tk), lambda i, j, kk: (i, kk)))
    b_spec = (pl.BlockSpec((tn, tk), lambda i, j, kk: (j, kk)) if mode == "nt"
              else pl.BlockSpec((tk, tn), lambda i, j, kk: (kk, j)))
    o_spec = pl.BlockSpec((tm, tn), lambda i, j, kk: (i, j))
    has_add = add is not None

    def body(*refs):
        a_ref, b_ref = refs[0], refs[1]
        add_ref = refs[2] if has_add else None
        o_ref, acc_ref = refs[-2], refs[-1]
        p = dot(a_ref[...], b_ref[...])

        def finish(total):
            if has_add:
                total = total + add_ref[...].astype(F32)
            o_ref[...] = total.astype(out_dtype)

        if nk == 1:
            finish(p)
        else:
            kk = pl.program_id(2)

            @pl.when(kk == 0)
            def _():
                acc_ref[...] = p

            @pl.when(jnp.logical_and(kk > 0, kk < nk - 1))
            def _():
                acc_ref[...] += p

            @pl.when(kk == nk - 1)
            def _():
                finish(acc_ref[...] + p)

    in_specs = [a_spec, b_spec] + ([o_spec] if has_add else []) + ([ANY] if after is not None else [])
    args = (a, b) + ((add,) if has_add else ()) + ((after,) if after is not None else ())
    return pl.pallas_call(
        body, name=name, grid=(m // tm, n // tn, nk), in_specs=in_specs, out_specs=o_spec,
        out_shape=jax.ShapeDtypeStruct((m, n), out_dtype),
        scratch_shapes=[pltpu.VMEM((tm, tn) if nk > 1 else (8, 128), F32)],
        compiler_params=_params("parallel", "parallel", "arbitrary"),
    )(*args)


def _mm_shifted(a, b, shift, *, b_shifted, first, count, total, tm, tn, out_dtype, name, into=None, after=None):
    m, k = a.shape
    tm = min(tm, m)
    assert m % tm == 0
    col = lambda j, shift_ref: (shift_ref[0] + first + j) % total
    extra = [arr for arr in (into, after) if arr is not None]

    def body(shift_ref, a_ref, b_ref, *rest):
        del shift_ref
        rest[-1][...] = _dot(a_ref[...], b_ref[...]).astype(out_dtype)

    return pl.pallas_call(
        body, name=name, out_shape=jax.ShapeDtypeStruct((m, total * tn), out_dtype),
        grid_spec=pltpu.PrefetchScalarGridSpec(
            num_scalar_prefetch=1, grid=(m // tm, count),
            in_specs=[pl.BlockSpec((tm, k), lambda i, j, s: (i, 0)),
                      pl.BlockSpec((k, tn), (lambda i, j, s: (0, col(j, s))) if b_shifted else (lambda i, j, s: (0, j)))]
            + [ANY] * len(extra),
            out_specs=pl.BlockSpec((tm, tn), lambda i, j, s: (i, col(j, s)))),
        input_output_aliases={3: 0} if into is not None else {},
        compiler_params=_params("parallel", "arbitrary"))(shift, a, b, *extra)


def _mm_rows(a, b, rows_in, vecs_in, out_shapes, epilogue, *, tm, name):
    m, k = a.shape
    n = b.shape[1]
    assert m % tm == 0
    row = pl.BlockSpec((tm, n), lambda i: (i, 0))

    def body(a_ref, b_ref, *rest):
        epilogue(_dot(a_ref[...], b_ref[...]), *rest)

    out_specs = [row if s.shape[0] == m else pl.BlockSpec(s.shape, lambda i: (0, 0)) for s in out_shapes]
    return pl.pallas_call(
        body, name=name, grid=(m // tm,),
        in_specs=[pl.BlockSpec((tm, k), lambda i: (i, 0)), pl.BlockSpec((k, n), lambda i: (0, 0))] + [row] * len(rows_in)
        + [pl.BlockSpec((1, n), lambda i: (0, 0))] * len(vecs_in),
        out_specs=out_specs, out_shape=out_shapes, compiler_params=_params("arbitrary"),
    )(a, b, *rows_in, *vecs_in)


def _cast_into_gathered(w, kind, chip, name, keep_shard=False):
    r, c = w.shape
    tr = min(r, 512)
    per = r // tr

    def body(chip_ref, w_ref, *outs):
        del chip_ref
        for o_ref in outs:
            o_ref[...] = w_ref[...].astype(BF16)

    if kind == "col":
        shape, out_map = (r, N_CHIPS * c), (lambda i, chip_ref: (i, chip_ref[0]))
    else:
        shape, out_map = (N_CHIPS * r, c), (lambda i, chip_ref: (chip_ref[0] * per + i, 0))
    plain = pl.BlockSpec((tr, c), lambda i, chip_ref: (i, 0))
    out = pl.pallas_call(
        body, name=name,
        out_shape=[jax.ShapeDtypeStruct(shape, BF16)] + ([jax.ShapeDtypeStruct((r, c), BF16)] if keep_shard else []),
        grid_spec=pltpu.PrefetchScalarGridSpec(
            num_scalar_prefetch=1, grid=(per,), in_specs=[plain],
            out_specs=[pl.BlockSpec((tr, c), out_map)] + ([plain] if keep_shard else [])),
        compiler_params=_params("parallel"))(chip, w)
    return out if keep_shard else out[0]


def _rmsnorm(x, w, name, after=None):
    s, d = x.shape
    tr = 256

    def body(x_ref, w_ref, *rest):
        xv = x_ref[...]
        rstd = lax.rsqrt(jnp.mean(xv * xv, axis=-1, keepdims=True) + EPS)
        rest[-1][...] = (xv * rstd * w_ref[...]).astype(BF16)

    return pl.pallas_call(
        body, name=name, grid=(s // tr,),
        in_specs=[pl.BlockSpec((tr, d), lambda i: (i, 0)), pl.BlockSpec((1, d), lambda i: (0, 0))]
        + ([ANY] if after is not None else []),
        out_specs=pl.BlockSpec((tr, d), lambda i: (i, 0)),
        out_shape=jax.ShapeDtypeStruct((s, d), BF16), compiler_params=_params("parallel"),
    )(x, w, *((after,) if after is not None else ()))


def _rmsnorm_bwd(x, w, dh, dres, name, with_bf16):
    s, d = x.shape
    tr = 256

    def body(x_ref, w_ref, dh_ref, dres_ref, *outs):
        dx_ref, dw_ref = outs[0], outs[-1]
        xv = x_ref[...]
        rstd = lax.rsqrt(jnp.mean(xv * xv, axis=-1, keepdims=True) + EPS)
        nrm = xv * rstd
        dhv = dh_ref[...].astype(F32)
        dn = dhv * w_ref[...]
        dx = dres_ref[...] + rstd * (dn - nrm * jnp.mean(dn * nrm, axis=-1, keepdims=True))
        dx_ref[...] = dx
        if with_bf16:
            outs[1][...] = dx.astype(BF16)

        @pl.when(pl.program_id(0) == 0)
        def _():
            dw_ref[...] = jnp.zeros_like(dw_ref)

        dw_ref[...] += jnp.sum(dhv * nrm, axis=0, keepdims=True)

    row = pl.BlockSpec((tr, d), lambda i: (i, 0))
    out_shape = [jax.ShapeDtypeStruct((s, d), F32)] + ([jax.ShapeDtypeStruct((s, d), BF16)] if with_bf16 else [])
    out_shape.append(jax.ShapeDtypeStruct((8, d), F32))
    out_specs = [row] * (2 if with_bf16 else 1) + [pl.BlockSpec((8, d), lambda i: (0, 0))]
    return pl.pallas_call(
        body, name=name, grid=(s // tr,),
        in_specs=[row, pl.BlockSpec((1, d), lambda i: (0, 0)), row, row],
        out_specs=out_specs, out_shape=out_shape, compiler_params=_params("arbitrary"))(x, w, dh, dres)


def _adamw(w, g, m, v, name):
    r, c = w.shape
    tr = min(r, 256)
    assert r % tr == 0

    def body(w_ref, g_ref, m_ref, v_ref, d_ref, nm_ref, nv_ref):
        gv = g_ref[...]
        nm = ADAM_B1 * m_ref[...] + (1.0 - ADAM_B1) * gv
        nv = ADAM_B2 * v_ref[...] + (1.0 - ADAM_B2) * (gv * gv)
        m_hat = nm / (1.0 - ADAM_B1 ** ADAM_STEP)
        v_hat = nv / (1.0 - ADAM_B2 ** ADAM_STEP)
        d_ref[...] = -ADAM_LR * (m_hat / (jnp.sqrt(v_hat) + ADAM_EPS) + ADAM_WD * w_ref[...])
        nm_ref[...] = nm
        nv_ref[...] = nv

    spec = pl.BlockSpec((tr, c), lambda i: (i, 0))
    shp = jax.ShapeDtypeStruct((r, c), F32)
    return pl.pallas_call(body, name=name, grid=(r // tr,), in_specs=[spec] * 4, out_specs=[spec] * 3,
                          out_shape=[shp] * 3, compiler_params=_params("parallel"))(w, g, m, v)


def _rope_tables(s, dim):
    inv = 1.0 / (ROPE_THETA ** (jnp.arange(0, dim, 2, dtype=F32) / dim))
    ang = jnp.arange(s).astype(F32)[:, None] * inv[None, :]
    return jnp.cos(ang), jnp.sin(ang)


def _rope_half(x, cos, sin):
    h = x.shape[1] // 2
    x1, x2 = x[:, :h], x[:, h:]
    return jnp.concatenate([x1 * cos - x2 * sin, x2 * cos + x1 * sin], axis=1)


def _unrope_half(dy, cos, sin):
    h = dy.shape[1] // 2
    d1, d2 = dy[:, :h], dy[:, h:]
    return jnp.concatenate([d1 * cos + d2 * sin, d2 * cos - d1 * sin], axis=1)


def _lane(shape):
    return lax.broadcasted_iota(jnp.int32, shape, 1)


def _partner64(x):
    w = x.shape[1]
    first = (_lane(x.shape) % HEAD_DIM) < (HEAD_DIM // 2)
    return jnp.where(first, pltpu.roll(x, w - HEAD_DIM // 2, 1), pltpu.roll(x, HEAD_DIM // 2, 1))


def _tile_lanes(t, reps):
    return t if reps == 1 else jnp.concatenate([t] * reps, axis=1)


def _group_mean(x, ones_bd):
    hi = x.astype(BF16)
    lo = (x - hi.astype(F32)).astype(BF16)
    return _dot(hi, ones_bd) + _dot(lo, ones_bd)


def _block_diag_mean(width):
    idx = jnp.arange(width) // HEAD_DIM
    return jnp.where(idx[:, None] == idx[None, :], 1.0 / HEAD_DIM, 0.0).astype(BF16)


def _retention_tables():
    h = RET_HEADS
    log_g = jnp.log(1.0 - 2.0 ** (-5.0 - jnp.arange(h, dtype=F32)))
    idx = jnp.arange(BLK, dtype=F32)
    diff = idx[:, None] - idx[None, :]
    intra = jnp.where(diff >= 0, jnp.exp(log_g[:, None, None] * jnp.maximum(diff, 0.0)), 0.0).astype(F32)
    q_dec = jnp.exp(log_g[:, None] * (idx[None, :] + 1.0)).astype(F32)[:, :, None]
    k_dec = jnp.exp(log_g[:, None] * (BLK - 1.0 - idx[None, :])).astype(F32)[:, :, None]
    chunk_dec = jnp.exp(log_g * BLK).astype(F32)[:, None, None]
    return intra, q_dec, k_dec, chunk_dec


def _retention_fwd(proj, cos, sin, tables):
    s = proj.shape[0]
    nc = s // BLK
    intra, q_dec, k_dec, chunk_dec = tables

    def body(p_ref, cos_ref, sin_ref, in_ref, qd_ref, kd_ref, cd_ref, o_ref, cat_ref, st_ref, state):
        @pl.when(pl.program_id(0) == 0)
        def _():
            state[...] = jnp.zeros_like(state)

        cosv, sinv = cos_ref[...], sin_ref[...]
        for h in range(RET_HEADS):
            c0 = h * RET_DIM
            q = p_ref[:, c0:c0 + RET_DIM].astype(F32)
            k = p_ref[:, RET_WIDTH + c0:RET_WIDTH + c0 + RET_DIM].astype(F32)
            v = p_ref[:, 2 * RET_WIDTH + c0:2 * RET_WIDTH + c0 + RET_DIM]
            g = p_ref[:, 3 * RET_WIDTH + c0:3 * RET_WIDTH + c0 + RET_DIM].astype(F32)
            qb = _rope_half(q, cosv, sinv).astype(BF16)
            kr = _rope_half(k, cosv, sinv) * (RET_DIM ** -0.5)
            kb = kr.astype(BF16)
            scores = _dot_nt(qb, kb) * in_ref[h]
            inner = _dot(scores.astype(BF16), v)
            prev = state[h]
            prev_b = prev.astype(BF16)
            st_ref[h, 0] = prev_b
            o = inner + _dot(qb, prev_b) * qd_ref[h]
            o_ref[:, c0:c0 + RET_DIM] = o
            rstd = lax.rsqrt(jnp.mean(o * o, axis=-1, keepdims=True) + EPS)
            cat_ref[:, c0:c0 + RET_DIM] = (o * rstd * (g * _sigmoid(g))).astype(BF16)
            state[h] = cd_ref[h] * prev + _dot_tn((kr * kd_ref[h]).astype(BF16), v)

    full = lambda shape: pl.BlockSpec(shape, lambda n: (0,) * len(shape))
    return pl.pallas_call(
        body, name="retention_fwd", grid=(nc,),
        in_specs=[pl.BlockSpec((BLK, 4 * RET_WIDTH), lambda n: (n, 0)),
                  pl.BlockSpec((BLK, RET_DIM // 2), lambda n: (n, 0)), pl.BlockSpec((BLK, RET_DIM // 2), lambda n: (n, 0)),
                  full((RET_HEADS, BLK, BLK)), full((RET_HEADS, BLK, 1)), full((RET_HEADS, BLK, 1)), full((RET_HEADS, 1, 1))],
        out_specs=[pl.BlockSpec((BLK, RET_WIDTH), lambda n: (n, 0)), pl.BlockSpec((BLK, RET_WIDTH), lambda n: (n, 0)),
                   pl.BlockSpec((RET_HEADS, 1, RET_DIM, RET_DIM), lambda n: (0, n, 0, 0))],
        out_shape=[jax.ShapeDtypeStruct((s, RET_WIDTH), F32), jax.ShapeDtypeStruct((s, D_MODEL), BF16),
                   jax.ShapeDtypeStruct((RET_HEADS, nc, RET_DIM, RET_DIM), BF16)],
        scratch_shapes=[pltpu.VMEM((RET_HEADS, RET_DIM, RET_DIM), F32)],
        compiler_params=_params("arbitrary"))(proj, cos, sin, intra, q_dec, k_dec, chunk_dec)


def _retention_bwd(proj, o, states, dcat, cos, sin, tables):
    s = proj.shape[0]
    nc = s // BLK
    intra, q_dec, k_dec, chunk_dec = tables

    def body(p_ref, o_ref, st_ref, dc_ref, cos_ref, sin_ref, in_ref, qd_ref, kd_ref, cd_ref, dp_ref, dstate):
        @pl.when(pl.program_id(0) == 0)
        def _():
            dstate[...] = jnp.zeros_like(dstate)

        cosv, sinv = cos_ref[...], sin_ref[...]
        for h in range(RET_HEADS):
            c0 = h * RET_DIM
            q = p_ref[:, c0:c0 + RET_DIM].astype(F32)
            k = p_ref[:, RET_WIDTH + c0:RET_WIDTH + c0 + RET_DIM].astype(F32)
            v = p_ref[:, 2 * RET_WIDTH + c0:2 * RET_WIDTH + c0 + RET_DIM]
            g = p_ref[:, 3 * RET_WIDTH + c0:3 * RET_WIDTH + c0 + RET_DIM].astype(F32)
            o = o_ref[:, c0:c0 + RET_DIM]
            dc = dc_ref[:, c0:c0 + RET_DIM].astype(F32)
            rstd = lax.rsqrt(jnp.mean(o * o, axis=-1, keepdims=True) + EPS)
            nrm = o * rstd
            sg = _sigmoid(g)
            dg = dc * nrm * (sg * (1.0 + g * (1.0 - sg)))
            dn = dc * (g * sg)
            do = rstd * (dn - nrm * jnp.mean(dn * nrm, axis=-1, keepdims=True))
            qb = _rope_half(q, cosv, sinv).astype(BF16)
            kr = _rope_half(k, cosv, sinv) * (RET_DIM ** -0.5)
            kb = kr.astype(BF16)
            mask = in_ref[h]
            qd, kd = qd_ref[h], kd_ref[h]
            prev_b = st_ref[h, 0]
            dnext = dstate[h]
            dnext_b = dnext.astype(BF16)
            att = (_dot_nt(qb, kb) * mask).astype(BF16)
            do_b = do.astype(BF16)
            doq = (do * qd).astype(BF16)
            dv = _dot_tn(att, do_b) + _dot((kr * kd).astype(BF16), dnext_b)
            ds = (_dot_nt(do_b, v) * mask).astype(BF16)
            dqr = _dot(ds, kb) + _dot_nt(doq, prev_b)
            dkr = _dot_tn(ds, qb) + _dot_nt(v, dnext_b) * kd
            dstate[h] = cd_ref[h] * dnext + _dot_tn(qb, doq)
            dq = _unrope_half(dqr, cosv, sinv)
            dk = _unrope_half(dkr * (RET_DIM ** -0.5), cosv, sinv)
            dp_ref[:, c0:c0 + RET_DIM] = dq.astype(BF16)
            dp_ref[:, RET_WIDTH + c0:RET_WIDTH + c0 + RET_DIM] = dk.astype(BF16)
            dp_ref[:, 2 * RET_WIDTH + c0:2 * RET_WIDTH + c0 + RET_DIM] = dv.astype(BF16)
            dp_ref[:, 3 * RET_WIDTH + c0:3 * RET_WIDTH + c0 + RET_DIM] = dg.astype(BF16)

    rev = lambda n: nc - 1 - n
    full = lambda shape: pl.BlockSpec(shape, lambda n: (0,) * len(shape))
    return pl.pallas_call(
        body, name="retention_bwd", grid=(nc,),
        in_specs=[pl.BlockSpec((BLK, 4 * RET_WIDTH), lambda n: (rev(n), 0)),
                  pl.BlockSpec((BLK, RET_WIDTH), lambda n: (rev(n), 0)),
                  pl.BlockSpec((RET_HEADS, 1, RET_DIM, RET_DIM), lambda n: (0, rev(n), 0, 0)),
                  pl.BlockSpec((BLK, RET_WIDTH), lambda n: (rev(n), 0)),
                  pl.BlockSpec((BLK, RET_DIM // 2), lambda n: (rev(n), 0)), pl.BlockSpec((BLK, RET_DIM // 2), lambda n: (rev(n), 0)),
                  full((RET_HEADS, BLK, BLK)), full((RET_HEADS, BLK, 1)), full((RET_HEADS, BLK, 1)), full((RET_HEADS, 1, 1))],
        out_specs=pl.BlockSpec((BLK, 4 * RET_WIDTH), lambda n: (rev(n), 0)),
        out_shape=jax.ShapeDtypeStruct((s, EVEN_IN), BF16),
        scratch_shapes=[pltpu.VMEM((RET_HEADS, RET_DIM, RET_DIM), F32)],
        compiler_params=_params("arbitrary"))(proj, o, states, dcat, cos, sin, intra, q_dec, k_dec, chunk_dec)


CONV_ROWS = 256
HALO = 16


def _conv_pieces(p, halo, conv_w, first):
    rows = p.shape[0]
    gb, gc, u, gv = (p[:, i * CONV_WIDTH:(i + 1) * CONV_WIDTH] for i in range(4))
    cu = gc * u
    hcu = halo[:, CONV_WIDTH:2 * CONV_WIDTH] * halo[:, 2 * CONV_WIDTH:3 * CONV_WIDTH]
    hcu = jnp.where(first, 0.0, hcu)
    r1, r2 = hcu[HALO - 1:HALO], hcu[HALO - 2:HALO - 1]
    row = lax.broadcasted_iota(jnp.int32, cu.shape, 0)
    m1 = jnp.where(row == 0, r1, pltpu.roll(cu, 1, 0))
    m2 = jnp.where(row == 0, r2, jnp.where(row == 1, r1, pltpu.roll(cu, 2, 0)))
    del rows, conv_w
    return gb, gc, u, gv, cu, m1, m2


def _conv_fwd(proj, conv_w, cat):
    s = proj.shape[0]
    per = CONV_ROWS // HALO

    def body(p_ref, halo_ref, w_ref, cat_in, cat_ref):
        del cat_in
        first = pl.program_id(0) == 0
        gb, _, _, gv, cu, m1, m2 = _conv_pieces(p_ref[...].astype(F32), halo_ref[...].astype(F32), None, first)
        conv = w_ref[0:1, :] * m2 + w_ref[1:2, :] * m1 + w_ref[2:3, :] * cu
        cat_ref[...] = (gb * conv * (gv * _sigmoid(gv))).astype(BF16)

    return pl.pallas_call(
        body, name="conv_fwd", grid=(s // CONV_ROWS,),
        in_specs=[pl.BlockSpec((CONV_ROWS, 4 * CONV_WIDTH), lambda i: (i, 1)),
                  pl.BlockSpec((HALO, 4 * CONV_WIDTH), lambda i: (jnp.maximum(i * per - 1, 0), 1)),
                  pl.BlockSpec((3, CONV_WIDTH), lambda i: (0, 0)), ANY],
        out_specs=pl.BlockSpec((CONV_ROWS, CONV_WIDTH), lambda i: (i, 1)),
        out_shape=jax.ShapeDtypeStruct(cat.shape, cat.dtype), input_output_aliases={3: 0},
        compiler_params=_params("parallel"))(proj, proj, conv_w, cat)


def _conv_bwd(proj, dcat, conv_w, dproj):
    s = proj.shape[0]
    per = CONV_ROWS // HALO
    last_halo = s // HALO - 1
    nsteps = s // CONV_ROWS

    def body(p_ref, halo_ref, nxt_ref, dc_ref, dnxt_ref, w_ref, dp_in, dp_ref, dw_ref):
        del dp_in
        i = pl.program_id(0)
        gb, gc, u, gv, cu, m1, m2 = _conv_pieces(p_ref[...].astype(F32), halo_ref[...].astype(F32), None, i == 0)
        w0, w1, w2 = w_ref[0:1, :], w_ref[1:2, :], w_ref[2:3, :]
        conv = w0 * m2 + w1 * m1 + w2 * cu
        dco = dc_ref[...].astype(F32)
        sg = _sigmoid(gv)
        silu = gv * sg
        dgb = dco * conv * silu
        dgv = dco * gb * conv * (sg * (1.0 + gv * (1.0 - sg)))
        dconv = dco * gb * silu
        nxt = nxt_ref[...].astype(F32)
        ngv = nxt[:, 3 * CONV_WIDTH:]
        dnext = dnxt_ref[...].astype(F32) * nxt[:, :CONV_WIDTH] * (ngv * _sigmoid(ngv))
        dnext = jnp.where(i == nsteps - 1, 0.0, dnext)
        n1, n2 = dnext[0:1], dnext[1:2]
        row = lax.broadcasted_iota(jnp.int32, dconv.shape, 0)
        p1 = jnp.where(row == CONV_ROWS - 1, n1, pltpu.roll(dconv, CONV_ROWS - 1, 0))
        p2 = jnp.where(row == CONV_ROWS - 1, n2, jnp.where(row == CONV_ROWS - 2, n1, pltpu.roll(dconv, CONV_ROWS - 2, 0)))
        dcu = w2 * dconv + w1 * p1 + w0 * p2
        dp_ref[...] = jnp.concatenate([dgb, dcu * u, dcu * gc, dgv], axis=1).astype(BF16)

        @pl.when(i == 0)
        def _():
            dw_ref[...] = jnp.zeros_like(dw_ref)

        taps = [jnp.sum(dconv * m, axis=0, keepdims=True) for m in (m2, m1, cu)]
        r8 = lax.broadcasted_iota(jnp.int32, dw_ref.shape, 0)
        dw_ref[...] += jnp.where(r8 == 0, taps[0], jnp.where(r8 == 1, taps[1], jnp.where(r8 == 2, taps[2], 0.0)))

    return pl.pallas_call(
        body, name="conv_bwd", grid=(nsteps,),
        in_specs=[pl.BlockSpec((CONV_ROWS, 4 * CONV_WIDTH), lambda i: (i, 1)),
                  pl.BlockSpec((HALO, 4 * CONV_WIDTH), lambda i: (jnp.maximum(i * per - 1, 0), 1)),
                  pl.BlockSpec((HALO, 4 * CONV_WIDTH), lambda i: (jnp.minimum((i + 1) * per, last_halo), 1)),
                  pl.BlockSpec((CONV_ROWS, CONV_WIDTH), lambda i: (i, 1)),
                  pl.BlockSpec((HALO, CONV_WIDTH), lambda i: (jnp.minimum((i + 1) * per, last_halo), 1)),
                  pl.BlockSpec((3, CONV_WIDTH), lambda i: (0, 0)), ANY],
        out_specs=[pl.BlockSpec((CONV_ROWS, 4 * CONV_WIDTH), lambda i: (i, 1)), pl.BlockSpec((8, CONV_WIDTH), lambda i: (0, 0))],
        out_shape=[jax.ShapeDtypeStruct(dproj.shape, dproj.dtype), jax.ShapeDtypeStruct((8, CONV_WIDTH), F32)],
        input_output_aliases={6: 0},
        compiler_params=_params("arbitrary"))(proj, proj, proj, dcat, dcat, conv_w, dproj)


GROUP_WIDTH = 8 * HEAD_DIM
GROUP_HEADS = 8
SLAB = 128
Q_COL = 0
KV_COL = 4
GATE_COL = 5
K_SLAB0 = ATTN_WIDTH // SLAB
V_SLAB0 = (ATTN_WIDTH + KV_WIDTH) // SLAB
KV_ROWS = 512


def _half_mask(shape, which):
    return (_lane(shape) // HEAD_DIM) == which


def _dup_head(slab, which):
    kept = jnp.where(_half_mask(slab.shape, which), slab, 0.0)
    return kept + pltpu.roll(kept, HEAD_DIM, 1)


def _stack_heads(x):
    parts = []
    for sl in range(GROUP_WIDTH // SLAB):
        slab = x[:, sl * SLAB:(sl + 1) * SLAB]
        parts += [jnp.where(_half_mask(slab.shape, e), slab, 0.0) for e in range(2)]
    return jnp.concatenate(parts, axis=0)


def _unstack_heads(y):
    slabs = []
    for sl in range(GROUP_WIDTH // SLAB):
        a, b = y[(2 * sl) * BLK:(2 * sl + 1) * BLK], y[(2 * sl + 1) * BLK:(2 * sl + 2) * BLK]
        slabs.append(jnp.where(_half_mask(a.shape, 0), a, b))
    return jnp.concatenate(slabs, axis=1)


def _q_prep(q, qw, cosf, sins, ones_bd):
    rstd = lax.rsqrt(_group_mean(q * q, ones_bd) + EPS)
    nrm = q * rstd
    y = nrm * qw
    return nrm, rstd, y * cosf + _partner64(y) * sins


def _band(tri_ref, n):
    own = tri_ref[...] > 0.5
    return own, jnp.where(jnp.logical_and(n == 0, jnp.logical_not(own)), -1e30, 0.0)


def _fold(pair, own):
    return jnp.where(own, pair[:, BLK:], pair[:, :BLK])


def _unfold(folded, own):
    return jnp.concatenate([jnp.where(own, 0.0, folded), jnp.where(own, folded, 0.0)], axis=1)


def _head_probs(raw_scores, sink, own, bias):
    sc = _fold(raw_scores, own) * (HEAD_DIM ** -0.5) + bias
    m = jnp.maximum(jnp.max(sc, axis=-1, keepdims=True), sink)
    p = jnp.exp(sc - m)
    psink = jnp.exp(sink - m)
    inv = 1.0 / (jnp.sum(p, axis=-1, keepdims=True) + psink)
    return p * inv, psink * inv


def _keys_values(kc_ref, kp_ref, vc_ref, vp_ref, kw, ones_k, cosc, sinc, cosp, sinp, which):
    def key(ref, cosf, sins):
        k = ref[...].astype(F32)
        y = k * lax.rsqrt(_group_mean(k * k, ones_k) + EPS) * kw
        return _dup_head(y * cosf + _partner64(y) * sins, which)

    kcat = jnp.concatenate([key(kp_ref, cosp, sinp), key(kc_ref, cosc, sinc)], axis=0).astype(BF16)
    vcat = jnp.concatenate([_dup_head(vp_ref[...].astype(F32), which), _dup_head(vc_ref[...].astype(F32), which)],
                           axis=0).astype(BF16)
    return kcat, vcat


def _swa_specs():
    prev = lambda n: jnp.maximum(n - 1, 0)
    slab = lambda col0, row: pl.BlockSpec((BLK, SLAB), lambda g, n: (row(n), col0 + g // 2))
    cur = lambda n: n
    tab = lambda row: pl.BlockSpec((BLK, SLAB), lambda g, n: (row(n), 0))
    full = lambda shape: pl.BlockSpec(shape, lambda g, n: (0,) * len(shape))
    return dict(
        sinks=pl.BlockSpec(memory_space=pltpu.SMEM),
        q=pl.BlockSpec((BLK, GROUP_WIDTH), lambda g, n: (n, Q_COL + g)),
        gate=pl.BlockSpec((BLK, GROUP_WIDTH), lambda g, n: (n, GATE_COL + g)),
        kc=slab(K_SLAB0, cur), kp=slab(K_SLAB0, prev), vc=slab(V_SLAB0, cur), vp=slab(V_SLAB0, prev),
        tab_c=tab(cur), tab_p=tab(prev),
        qw=full((1, GROUP_WIDTH)), kw=full((1, SLAB)), ones_q=full((GROUP_WIDTH, GROUP_WIDTH)), ones_k=full((SLAB, SLAB)),
        tri=full((BLK, BLK)), group=pl.BlockSpec((BLK, GROUP_WIDTH), lambda g, n: (n, g)))


def _lower_triangle():
    return jnp.tril(jnp.ones((BLK, BLK), F32))


def _swa_fwd(proj, sinks, qw, kw, cos, sins, ones_q, ones_k):
    s = proj.shape[0]
    nb = s // BLK
    sp = _swa_specs()

    def body(sink_ref, q_ref, kc_ref, kp_ref, vc_ref, vp_ref, gate_ref, cosc_ref, sinc_ref, cosp_ref, sinp_ref,
             qw_ref, kw_ref, oq_ref, ok_ref, tri_ref, ag_ref, o_ref):
        g, n = pl.program_id(0), pl.program_id(1)
        cosc, sinc = cosc_ref[...], sinc_ref[...]
        kcat, vcat = _keys_values(kc_ref, kp_ref, vc_ref, vp_ref, kw_ref[...], ok_ref[...], cosc, sinc,
                                  cosp_ref[...], sinp_ref[...], g % 2)
        _, _, qr = _q_prep(q_ref[...].astype(F32), qw_ref[...], _tile_lanes(cosc, 4), _tile_lanes(sinc, 4), oq_ref[...])
        scores = _dot_nt(_stack_heads(qr).astype(BF16), kcat)
        own, bias = _band(tri_ref, n)
        probs = []
        for j in range(GROUP_HEADS):
            p, _ = _head_probs(scores[j * BLK:(j + 1) * BLK], sink_ref[g * GROUP_HEADS + j], own, bias)
            probs.append(_unfold(p, own).astype(BF16))
        o = _unstack_heads(_dot(jnp.concatenate(probs, axis=0), vcat))
        gate = gate_ref[...].astype(F32)
        o_ref[...] = o.astype(BF16)
        ag_ref[...] = (o * (gate * _sigmoid(gate))).astype(BF16)

    shp = jax.ShapeDtypeStruct((s, ATTN_WIDTH), BF16)
    return pl.pallas_call(
        body, name="swa_fwd", grid=(KV_HEADS, nb),
        in_specs=[sp["sinks"], sp["q"], sp["kc"], sp["kp"], sp["vc"], sp["vp"], sp["gate"], sp["tab_c"], sp["tab_c"],
                  sp["tab_p"], sp["tab_p"], sp["qw"], sp["kw"], sp["ones_q"], sp["ones_k"], sp["tri"]],
        out_specs=[sp["group"], sp["group"]], out_shape=[shp, shp],
        compiler_params=_params("parallel", "arbitrary"),
    )(sinks, proj, proj, proj, proj, proj, proj, cos, sins, cos, sins, qw, kw, ones_q, ones_k, _lower_triangle())


def _swa_bwd(proj, dag, sinks, qw, kw, cos, sins, ones_q, ones_k):
    s = proj.shape[0]
    nb = s // BLK
    sp = _swa_specs()

    def body(sink_ref, q_ref, kc_ref, kp_ref, vc_ref, vp_ref, gate_ref, dag_ref, cosc_ref, sinc_ref, cosp_ref, sinp_ref,
             qw_ref, kw_ref, oq_ref, ok_ref, tri_ref, dq_ref, dkc_ref, dkp_ref, dvc_ref, dvp_ref, dqw_ref, dsink_ref):
        g, n = pl.program_id(0), pl.program_id(1)
        cosc, sinc = cosc_ref[...], sinc_ref[...]
        kcat, vcat = _keys_values(kc_ref, kp_ref, vc_ref, vp_ref, kw_ref[...], ok_ref[...], cosc, sinc,
                                  cosp_ref[...], sinp_ref[...], g % 2)
        cosq, sinq = _tile_lanes(cosc, 4), _tile_lanes(sinc, 4)
        qwv = qw_ref[...]
        nrm, rstd, qr = _q_prep(q_ref[...].astype(F32), qwv, cosq, sinq, oq_ref[...])
        gate = gate_ref[...].astype(F32)
        do = dag_ref[...].astype(F32) * (gate * _sigmoid(gate))
        q_stack = _stack_heads(qr).astype(BF16)
        do_stack = _stack_heads(do).astype(BF16)
        scores = _dot_nt(q_stack, kcat)
        dprobs = _dot_nt(do_stack, vcat)
        own, bias = _band(tri_ref, n)
        probs, dscores, dsinks = [], [], []
        for j in range(GROUP_HEADS):
            rows = slice(j * BLK, (j + 1) * BLK)
            p, psink = _head_probs(scores[rows], sink_ref[g * GROUP_HEADS + j], own, bias)
            dp = _fold(dprobs[rows], own)
            delta = jnp.sum(p * dp, axis=-1, keepdims=True)
            probs.append(_unfold(p, own).astype(BF16))
            dscores.append(_unfold(p * (dp - delta) * (HEAD_DIM ** -0.5), own).astype(BF16))
            dsinks.append(-jnp.sum(psink * delta, axis=0, keepdims=True))
        ds = jnp.concatenate(dscores, axis=0)
        dk = _dot_tn(ds, q_stack)
        dv = _dot_tn(jnp.concatenate(probs, axis=0), do_stack)
        dk = dk + pltpu.roll(dk, HEAD_DIM, 1)
        dv = dv + pltpu.roll(dv, HEAD_DIM, 1)
        dkp_ref[0], dkc_ref[0] = dk[:BLK], dk[BLK:]
        dvp_ref[0], dvc_ref[0] = dv[:BLK], dv[BLK:]
        dqr = _unstack_heads(_dot(ds, kcat))
        dy = dqr * cosq + _partner64(dqr * sinq)
        dn = dy * qwv
        dq_ref[...] = (rstd * (dn - nrm * _group_mean(dn * nrm, oq_ref[...]))).astype(BF16)

        @pl.when(n == 0)
        def _():
            dqw_ref[...] = jnp.zeros_like(dqw_ref)
            dsink_ref[...] = jnp.zeros_like(dsink_ref)

        dqw_ref[0] += jnp.sum(dy * nrm, axis=0, keepdims=True)
        r8 = lax.broadcasted_iota(jnp.int32, (8, SLAB), 0)
        upd = jnp.zeros((8, SLAB), F32)
        for j in range(GROUP_HEADS):
            upd = jnp.where(r8 == j, dsinks[j], upd)
        dsink_ref[0] += upd

    cur_out = pl.BlockSpec((1, BLK, SLAB), lambda g, n: (g, n, 0))
    prev_out = pl.BlockSpec((1, BLK, SLAB), lambda g, n: (g, (n + nb - 1) % nb, 0))
    kv_shape = jax.ShapeDtypeStruct((KV_HEADS, s, SLAB), F32)
    return pl.pallas_call(
        body, name="swa_bwd", grid=(KV_HEADS, nb),
        in_specs=[sp["sinks"], sp["q"], sp["kc"], sp["kp"], sp["vc"], sp["vp"], sp["gate"], sp["group"],
                  sp["tab_c"], sp["tab_c"], sp["tab_p"], sp["tab_p"], sp["qw"], sp["kw"], sp["ones_q"], sp["ones_k"], sp["tri"]],
        out_specs=[sp["q"], cur_out, prev_out, cur_out, prev_out,
                   pl.BlockSpec((1, 8, GROUP_WIDTH), lambda g, n: (g, 0, 0)), pl.BlockSpec((1, 8, SLAB), lambda g, n: (g, 0, 0))],
        out_shape=[jax.ShapeDtypeStruct((s, ODD_IN), BF16), kv_shape, kv_shape, kv_shape, kv_shape,
                   jax.ShapeDtypeStruct((KV_HEADS, 8, GROUP_WIDTH), F32), jax.ShapeDtypeStruct((KV_HEADS, 8, SLAB), F32)],
        compiler_params=_params("parallel", "arbitrary"),
    )(sinks, proj, proj, proj, proj, proj, proj, dag, cos, sins, cos, sins, qw, kw, ones_q, ones_k, _lower_triangle())


def _swa_bwd_kv(proj, dproj, o, dag, dkc, dkp, dvc, dvp, kw, cos, sins, ones_k):
    s = proj.shape[0]
    rows = min(KV_ROWS, s)

    def body(kv_ref, gate_ref, o_ref, dag_ref, dkc_ref, dkp_ref, dvc_ref, dvp_ref, kw_ref, cos_ref, sin_ref, ok_ref,
             dp_in, dp_ref, dkw_ref):
        del dp_in
        i, j = pl.program_id(0), pl.program_id(1)

        @pl.when(jnp.logical_and(i == 0, j == 0))
        def _():
            dkw_ref[...] = jnp.zeros_like(dkw_ref)

        @pl.when(j == 0)
        def _():
            def assemble(cur_ref, prv_ref):
                tot = [cur_ref[h] + prv_ref[h] for h in range(KV_HEADS)]
                first = _half_mask(tot[0].shape, 0)
                return jnp.concatenate([jnp.where(first, tot[0], tot[1]), jnp.where(first, tot[2], tot[3])], axis=1)

            dkr = assemble(dkc_ref, dkp_ref)
            dv = assemble(dvc_ref, dvp_ref)
            cosf, sinf = _tile_lanes(cos_ref[...], 2), _tile_lanes(sin_ref[...], 2)
            k = kv_ref[:, :KV_WIDTH].astype(F32)
            rstd = lax.rsqrt(_group_mean(k * k, ok_ref[...]) + EPS)
            nrm = k * rstd
            dy = dkr * cosf + _partner64(dkr * sinf)
            dn = dy * kw_ref[...]
            dk = rstd * (dn - nrm * _group_mean(dn * nrm, ok_ref[...]))
            dp_ref[...] = jnp.concatenate([dk, dv], axis=1).astype(BF16)
            dkw_ref[...] += jnp.sum(dy * nrm, axis=0, keepdims=True)

        @pl.when(j > 0)
        def _():
            gate = gate_ref[...].astype(F32)
            sg = _sigmoid(gate)
            dp_ref[...] = (dag_ref[...].astype(F32) * o_ref[...].astype(F32) * (sg * (1.0 + gate * (1.0 - sg)))).astype(BF16)

    acc = pl.BlockSpec((KV_HEADS, rows, SLAB), lambda i, j: (0, i, 0))
    group = pl.BlockSpec((rows, GROUP_WIDTH), lambda i, j: (i, jnp.maximum(j - 1, 0)))
    tab = pl.BlockSpec((rows, SLAB), lambda i, j: (i, 0))
    return pl.pallas_call(
        body, name="swa_bwd_kv", grid=(s // rows, 5),
        in_specs=[pl.BlockSpec((rows, GROUP_WIDTH), lambda i, j: (i, KV_COL)),
                  pl.BlockSpec((rows, GROUP_WIDTH), lambda i, j: (i, GATE_COL + jnp.maximum(j - 1, 0))),
                  group, group, acc, acc, acc, acc,
                  pl.BlockSpec((1, KV_WIDTH), lambda i, j: (0, 0)), tab, tab,
                  pl.BlockSpec((KV_WIDTH, KV_WIDTH), lambda i, j: (0, 0)), ANY],
        out_specs=[pl.BlockSpec((rows, GROUP_WIDTH), lambda i, j: (i, KV_COL + j)), pl.BlockSpec((8, KV_WIDTH), lambda i, j: (0, 0))],
        out_shape=[jax.ShapeDtypeStruct(dproj.shape, dproj.dtype), jax.ShapeDtypeStruct((8, KV_WIDTH), F32)],
        input_output_aliases={12: 0},
        compiler_params=_params("arbitrary", "arbitrary"),
    )(proj, proj, o, dag, dkc, dkp, dvc, dvp, kw, cos, sins, ones_k, dproj)


def _place():
    x, y, c = lax.axis_index("x"), lax.axis_index("y"), lax.axis_index("c")
    return x, y, c


OTHER_CHIPS = ((1, 0), (0, 1), (1, 1))


def _half_rows(ref, half, rows):
    return ref.at[pl.ds(pl.multiple_of(half * (rows // 2), 8), rows // 2)]


DMA_CHUNK_BYTES = 1 << 20
BF16_TILE_ROWS = 16


def _n_chunks(ref):
    rows = ref.shape[-2]
    nbytes = math.prod(ref.shape) * jnp.dtype(ref.dtype).itemsize
    n = 1
    while 2 * n * DMA_CHUNK_BYTES <= nbytes and rows % (2 * n * BF16_TILE_ROWS) == 0:
        n *= 2
    return n


def _row_chunk(ref, k, n):
    rows = ref.shape[-2] // n
    return ref.at[pl.ds(k * rows, rows)] if len(ref.shape) == 2 else ref.at[:, pl.ds(k * rows, rows)]


def _push(src, dst, send_sem, recv_sem, device_id):
    n = _n_chunks(src)
    for k in range(n):
        pltpu.make_async_remote_copy(src_ref=_row_chunk(src, k, n), dst_ref=_row_chunk(dst, k, n), send_sem=send_sem,
                                     recv_sem=recv_sem, device_id=device_id, device_id_type=MESH).start()
    return pltpu.make_async_remote_copy(src_ref=src, dst_ref=dst, send_sem=send_sem, recv_sem=recv_sem,
                                        device_id=device_id, device_id_type=MESH)


def _copy(src, dst, sem):
    n = _n_chunks(src)
    for k in range(n):
        pltpu.make_async_copy(_row_chunk(src, k, n), _row_chunk(dst, k, n), sem).start()
    return pltpu.make_async_copy(src, dst, sem)


HBM = pl.BlockSpec(memory_space=pltpu.HBM)
SEM = pl.BlockSpec(memory_space=pltpu.SEMAPHORE)
SPLIT_COPY_EFFECT = pltpu.SideEffectType.DATAFLOW_SIDE_EFFECTING


def _in_hbm(a):
    return pltpu.with_memory_space_constraint(a, pltpu.HBM)


def _start_copies(name, arrays, plan, n_copies, after=None):
    n = len(arrays)

    def body(*refs):
        send_sem, recv_sem = refs[-n - 3], refs[-n - 2]
        for k, (src, dst, peer) in enumerate(plan(refs[:n])):
            _push(src, dst, send_sem.at[k], recv_sem.at[k], peer)
        refs[-1][...] = jnp.zeros_like(refs[-1])

    dma = pltpu.SemaphoreType.DMA((n_copies,))
    outs = pl.pallas_call(
        body, name=name,
        out_shape=(dma, dma, *[pltpu.HBM(a.shape, a.dtype) for a in arrays], jax.ShapeDtypeStruct((8, 128), F32)),
        in_specs=[HBM] * n + ([ANY] if after is not None else []),
        out_specs=(SEM, SEM, *[HBM] * n, pl.BlockSpec(memory_space=pltpu.VMEM)),
        input_output_aliases={i: i + 2 for i in range(n)},
        compiler_params=pltpu.CompilerParams(has_side_effects=SPLIT_COPY_EFFECT),
    )(*[_in_hbm(a) for a in arrays], *((after,) if after is not None else ()))
    return outs[0], outs[1], list(outs[2:2 + n]), outs[-1]


def _wait_copies(name, send_sem, recv_sem, arrays, plan, after):
    n = len(arrays)

    def body(*refs):
        send_ref, recv_ref = refs[n], refs[n + 1]
        for k, (src, dst, peer) in enumerate(plan(refs[:n])):
            cp = pltpu.make_async_remote_copy(src_ref=src, dst_ref=dst, send_sem=send_ref.at[k], recv_sem=recv_ref.at[k],
                                              device_id=peer, device_id_type=MESH)
            cp.wait_send()
            cp.wait_recv()

    return list(pl.pallas_call(
        body, name=name, out_shape=tuple(pltpu.HBM(a.shape, a.dtype) for a in arrays),
        in_specs=[HBM] * n + [SEM, SEM, ANY], out_specs=tuple([HBM] * n),
        input_output_aliases={i: i for i in range(n)},
        compiler_params=pltpu.CompilerParams(has_side_effects=SPLIT_COPY_EFFECT),
    )(*arrays, send_sem, recv_sem, after))


def _gather_region(full, kind, chip, half=None):
    if kind == "whole":
        return full.at[chip]
    if kind == "col":
        rows, width = full.shape[0], full.shape[1] // N_CHIPS
        piece = full.at[:, pl.ds(pl.multiple_of(chip * width, 128), width)]
    else:
        rows = full.shape[0] // N_CHIPS
        piece = full.at[pl.ds(pl.multiple_of(chip * rows, BF16_TILE_ROWS), rows)]
    return piece if half is None else _half_rows(piece, half, rows)


def _gather_plan(kinds):
    def plan(fulls):
        x, y, c = _place()
        copies = []
        for fx, fy in OTHER_CHIPS:
            for full, kind in zip(fulls, kinds):
                mine = _gather_region(full, kind, 2 * x + y, c)
                copies.append((mine, mine, (x ^ fx, y ^ fy, c)))
        return copies

    return plan


def _gather_start(name, fulls, kinds, after=None):
    return _start_copies(name, list(fulls), _gather_plan(kinds), 3 * len(kinds), after)


def _gather_finish(name, started, kinds, after):
    send_sem, recv_sem, fulls, _ = started
    n = len(kinds)
    fulls = _wait_copies(name + "_wait", send_sem, recv_sem, fulls, _gather_plan(kinds), after)
    split = [i for i, kind in enumerate(kinds) if kind != "whole"]

    def body(*refs):
        out = refs[n:2 * n]
        send, recv = refs[2 * n:]
        x, y, c = _place()
        pushes = []
        for r, (fx, fy) in enumerate(OTHER_CHIPS):
            chip = 2 * (x ^ fx) + (y ^ fy)
            for j, i in enumerate(split):
                landed = _gather_region(out[i], kinds[i], chip, c)
                pushes.append(_push(landed, landed, send.at[r * len(split) + j], recv.at[r * len(split) + j], (x, y, 1 - c)))
        for cp in pushes:
            cp.wait_recv()
        for cp in pushes:
            cp.wait_send()

    dma = pltpu.SemaphoreType.DMA
    return pl.pallas_call(
        body, name=name + "_share", in_specs=[ANY] * n, out_specs=[ANY] * n,
        out_shape=[jax.ShapeDtypeStruct(a.shape, a.dtype) for a in fulls],
        input_output_aliases={i: i for i in range(n)},
        scratch_shapes=[dma((3 * len(split),)), dma((3 * len(split),))],
        compiler_params=pltpu.CompilerParams(has_side_effects=True),
    )(*fulls)


def _allreduce_small(v):
    def body(v_ref, out_ref, buf, send_sems, recv_sems):
        x, y, c = _place()
        me = 4 * x + 2 * y + c
        buf[me] = v_ref[...]
        copies = []
        for r in range(1, N_DEV):
            peer = (x ^ (r >> 2), y ^ ((r >> 1) & 1), c ^ (r & 1))
            cp = pltpu.make_async_remote_copy(src_ref=v_ref, dst_ref=buf.at[me], send_sem=send_sems.at[r - 1],
                                              recv_sem=recv_sems.at[r - 1], device_id=peer, device_id_type=MESH)
            cp.start()
            copies.append(cp)
        for cp in copies:
            cp.wait_recv()
        for cp in copies:
            cp.wait_send()
        total = buf[0]
        for d in range(1, N_DEV):
            total = total + buf[d]
        out_ref[...] = total

    vm = pl.BlockSpec(memory_space=pltpu.VMEM)
    return pl.pallas_call(
        body, name="allreduce_small", in_specs=[vm], out_specs=vm, out_shape=jax.ShapeDtypeStruct(v.shape, v.dtype),
        scratch_shapes=[pltpu.VMEM((N_DEV,) + v.shape, v.dtype), pltpu.SemaphoreType.DMA((N_DEV - 1,)),
                        pltpu.SemaphoreType.DMA((N_DEV - 1,))],
        compiler_params=pltpu.CompilerParams(has_side_effects=True),
    )(v)


def _exchange_halves(grads, name):
    n = len(grads)

    def body(*refs):
        g, theirs = refs[:n], refs[n:2 * n]
        send_sem, recv_sem = refs[2 * n:]
        x, y, c = _place()
        copies = []
        for i in range(n):
            half = g[i].shape[1] // 2
            src = g[i].at[:, pl.ds(pl.multiple_of((1 - c) * half, BF16_TILE_ROWS), half)]
            copies.append(_push(src, theirs[i], send_sem.at[i], recv_sem.at[i], (x, y, 1 - c)))
        for cp in copies:
            cp.wait_recv()
            cp.wait_send()

    dma = pltpu.SemaphoreType.DMA
    return pl.pallas_call(
        body, name=name, in_specs=[ANY] * n, out_specs=[ANY] * n,
        out_shape=[jax.ShapeDtypeStruct((a.shape[0], a.shape[1] // 2, a.shape[2]), a.dtype) for a in grads],
        scratch_shapes=[dma((n,)), dma((n,))],
        compiler_params=pltpu.CompilerParams(has_side_effects=True),
    )(*grads)


def _pair_sum(g, theirs, core, name):
    pieces, half, cols = theirs.shape
    tr = min(half, 256)
    per = half // tr

    def body(core_ref, g_ref, t_ref, o_ref):
        del core_ref
        o_ref[...] = (g_ref[...].astype(F32) + t_ref[...].astype(F32)).astype(BF16)

    spec = pl.BlockSpec((1, tr, cols), lambda p, i, core_ref: (p, i, 0))
    return pl.pallas_call(
        body, name=name, out_shape=jax.ShapeDtypeStruct(theirs.shape, BF16),
        grid_spec=pltpu.PrefetchScalarGridSpec(
            num_scalar_prefetch=1, grid=(pieces, per),
            in_specs=[pl.BlockSpec((1, tr, cols), lambda p, i, core_ref: (p, core_ref[0] * per + i, 0)), spec],
            out_specs=spec),
        compiler_params=_params("parallel", "parallel"))(core, g, theirs)


def _scatter_plan(n):
    def plan(refs):
        parts, stacks = refs[:n], refs[n:]
        x, y, c = _place()
        copies = []
        for fx, fy in OTHER_CHIPS:
            chip = 2 * (x ^ fx) + (y ^ fy)
            for part, stack in zip(parts, stacks):
                if part.shape[0] == N_CHIPS:
                    piece = part.at[chip]
                else:
                    width = part.shape[2] // N_CHIPS
                    piece = part.at[0].at[:, pl.ds(pl.multiple_of(chip * width, 128), width)]
                copies.append((piece, stack.at[2 * x + y], (x ^ fx, y ^ fy, c)))
        return copies

    return plan


def _scatter_start(name, parts, after=None):
    def landing(a):
        return (N_CHIPS, a.shape[1], a.shape[2] if a.shape[0] == N_CHIPS else a.shape[2] // N_CHIPS)

    stacks = [lax.empty(landing(a), a.dtype) for a in parts]
    return _start_copies(name, list(parts) + stacks, _scatter_plan(len(parts)), 3 * len(parts), after)


def _scatter_finish(name, started, after):
    send_sem, recv_sem, arrays, _ = started
    n = len(arrays) // 2
    arrays = _wait_copies(name + "_wait", send_sem, recv_sem, arrays, _scatter_plan(n), after)
    return arrays[:n], arrays[n:]


def _sum_chips(part, stack, place, name):
    _, r, c = stack.shape
    tr = 256
    per = r // tr

    def body(place_ref, own_ref, a_ref, b_ref, c_ref, o_ref):
        del place_ref
        total = own_ref[0].astype(F32)
        for ref in (a_ref, b_ref, c_ref):
            total = total + ref[0].astype(F32)
        o_ref[...] = total

    if part.shape[0] == N_CHIPS:
        own = pl.BlockSpec((1, tr, c), lambda i, pr: (pr[1], i, 0))
    else:
        own = pl.BlockSpec((1, tr, c), lambda i, pr: (0, i, pr[1]))
    other = lambda flip: pl.BlockSpec((1, tr, c), lambda i, pr: (pr[1] ^ flip, i, 0))
    return pl.pallas_call(
        body, name=name, out_shape=jax.ShapeDtypeStruct((2 * r, c), F32),
        grid_spec=pltpu.PrefetchScalarGridSpec(
            num_scalar_prefetch=1, grid=(per,), in_specs=[own, other(2), other(1), other(3)],
            out_specs=pl.BlockSpec((tr, c), lambda i, pr: (pr[0] * per + i, 0))),
        compiler_params=_params("parallel"))(place, part, stack, stack, stack)


def _share_halves(fulls):
    n = len(fulls)

    def body(*refs):
        dst = refs[n:2 * n]
        send_sem, recv_sem = refs[2 * n:]
        x, y, c = _place()
        copies = []
        for i in range(n):
            mine = _half_rows(dst[i], c, dst[i].shape[0])
            copies.append(_push(mine, mine, send_sem.at[i], recv_sem.at[i], (x, y, 1 - c)))
        for cp in copies:
            cp.wait_recv()
            cp.wait_send()

    dma = pltpu.SemaphoreType.DMA
    return pl.pallas_call(
        body, name="share_halves", in_specs=[ANY] * n, out_specs=[ANY] * n,
        out_shape=[jax.ShapeDtypeStruct(a.shape, a.dtype) for a in fulls],
        input_output_aliases={i: i for i in range(n)}, scratch_shapes=[dma((n,)), dma((n,))],
        compiler_params=pltpu.CompilerParams(has_side_effects=True),
    )(*fulls)


MM = dict(tm=1024, tn=1024, tk=2048)


def _local_step(x, target, ev_norm_w, q_norm_w, k_norm_w, sinks, own_first, weights_first, weights_late, emit):
    s = x.shape[0]
    cos_r, sin_r = _rope_tables(s, RET_DIM)
    cos_a, sin_a = _rope_tables(s, HEAD_DIM)
    cos_a = jnp.tile(cos_a, (1, 4))
    sins_a = jnp.tile(jnp.concatenate([-sin_a, sin_a], axis=1), (1, 2))
    tables = _retention_tables()
    ones_q, ones_k, ones_kv = _block_diag_mean(GROUP_WIDTH), _block_diag_mean(SLAB), _block_diag_mean(KV_WIDTH)
    qw_g = jnp.tile(q_norm_w, (1, GROUP_WIDTH // HEAD_DIM))
    kw_s = jnp.tile(k_norm_w, (1, SLAB // HEAD_DIM))
    kw_kv = jnp.tile(k_norm_w, (1, KV_WIDTH // HEAD_DIM))
    sinks1 = sinks.reshape(Q_HEADS)

    own_w_in0, own_block, start_token = own_first
    h0 = _rmsnorm(x, ev_norm_w, "norm0", after=start_token)
    shifted = dict(shift=own_block * (own_w_in0.shape[1] // MM["tn"]), total=EVEN_IN // MM["tn"], tm=MM["tm"], tn=MM["tn"],
                   out_dtype=BF16)
    own_blocks = own_w_in0.shape[1] // MM["tn"]
    proj0 = _mm_shifted(h0, own_w_in0, b_shifted=False, first=0, count=own_blocks, name="proj0_own", **shifted)
    w_in0, conv_w, od_norm_w, token = weights_first(proj0)
    proj0 = _mm_shifted(h0, w_in0, b_shifted=True, first=own_blocks, count=shifted["total"] - own_blocks, name="proj0_rest",
                        into=proj0, after=token, **shifted)
    o_ret, cat, states = _retention_fwd(proj0, cos_r, sin_r, tables)
    cat = _conv_fwd(proj0, conv_w, cat)
    w_out0, w_in1, w_out1 = weights_late(cat)

    def residual_and_norm(prod, x_ref, w_ref, x1_ref, h1_ref):
        x1v = x_ref[...] + prod
        x1_ref[...] = x1v
        rstd = lax.rsqrt(jnp.mean(x1v * x1v, axis=-1, keepdims=True) + EPS)
        h1_ref[...] = (x1v * rstd * w_ref[...]).astype(BF16)

    def residual_and_loss(prod, x1_ref, t_ref, dy_ref, dyb_ref, sq_ref):
        diff = (x1_ref[...] + prod) - t_ref[...]
        dyv = diff * (1.0 / D_MODEL)
        dy_ref[...] = dyv
        dyb_ref[...] = dyv.astype(BF16)

        @pl.when(pl.program_id(0) == 0)
        def _():
            sq_ref[...] = jnp.zeros_like(sq_ref)

        sq_ref[...] += jnp.sum(jnp.sum(diff * diff, axis=1, keepdims=True), axis=0, keepdims=True)

    act = lambda dt: jax.ShapeDtypeStruct((s, D_MODEL), dt)
    x1, h1 = _mm_rows(cat, w_out0, [x], [od_norm_w], [act(F32), act(BF16)], residual_and_norm, tm=min(s, 512), name="out0")
    proj1 = _mm(h1, w_in1, mode="nn", out_dtype=BF16, name="proj1", tm=1024, tn=1536, tk=2048)
    ag, o_att = _swa_fwd(proj1, sinks1, qw_g, kw_s, cos_a, sins_a, ones_q, ones_k)
    dy, dy_b, sq = _mm_rows(ag, w_out1, [x1, target], [], [act(F32), act(BF16), jax.ShapeDtypeStruct((8, 128), F32)],
                            residual_and_loss, tm=min(s, 256), name="out1")

    g_w_out1 = _mm(ag, dy_b, mode="tn", out_dtype=BF16, name="g_w_out1", **MM)
    dag = _mm(dy_b, w_out1, mode="nt", out_dtype=BF16, name="d_ag", **MM)
    dproj1, dkc, dkp, dvc, dvp, dqw, dsink = _swa_bwd(proj1, dag, sinks1, qw_g, kw_s, cos_a, sins_a, ones_q, ones_k)
    dproj1, dkw = _swa_bwd_kv(proj1, dproj1, o_att, dag, dkc, dkp, dvc, dvp, kw_kv, cos_a, sins_a, ones_kv)
    g_w_in1 = _mm(h1, dproj1, mode="tn", out_dtype=BF16, name="g_w_in1", tm=1024, tn=1536, tk=2048)
    token = emit("layer1", (("od_w_in", g_w_in1, "col"), ("od_w_out", g_w_out1, "row")))
    dh1 = _mm(dproj1, w_in1, mode="nt", out_dtype=BF16, name="d_h1", tm=1024, tn=1024, tk=2304, after=token)
    dx1, dx1_b, g_norm1 = _rmsnorm_bwd(x1, od_norm_w, dh1, dy, "norm1_bwd", True)

    g_w_out0 = _mm(cat, dx1_b, mode="tn", out_dtype=BF16, name="g_w_out0", **MM)
    token = emit("out0", (("ev_w_out", g_w_out0, "row"),))
    dcat = _mm(dx1_b, w_out0, mode="nt", out_dtype=BF16, name="d_cat", after=token, **MM)
    dproj0 = _retention_bwd(proj0, o_ret, states, dcat, cos_r, sin_r, tables)
    dproj0, g_conv = _conv_bwd(proj0, dcat, conv_w, dproj0)
    g_w_in0 = _mm(h0, dproj0, mode="tn", out_dtype=BF16, name="g_w_in0", **MM)
    token = emit("in0", (("ev_w_in", g_w_in0, "col"),))
    dh0 = _mm(dproj0, w_in0, mode="nt", out_dtype=BF16, name="d_h0", after=token, **MM)
    grad_x, g_norm0 = _rmsnorm_bwd(x, ev_norm_w, dh0, dx1, "norm0_bwd", False)

    g_qw = dqw[:, 0, :].reshape(Q_HEADS, HEAD_DIM).sum(axis=0)
    g_kw = dkw[0].reshape(KV_HEADS, HEAD_DIM).sum(axis=0)
    g_sinks = dsink[:, :, 0].reshape(Q_HEADS)
    small = dict(ev_norm=g_norm0[0], od_norm=g_norm1[0], conv=g_conv[:3], qw=g_qw, kw=g_kw, sinks=g_sinks)
    return sq[0, 0], grad_x, small


def _pack_small_grads(small):
    pad = lambda v: jnp.pad(v, (0, D_MODEL - v.shape[0]))
    tail = pad(jnp.concatenate([small["qw"], small["kw"], small["sinks"]]))
    rows = [small["ev_norm"], small["od_norm"]] + [pad(small["conv"][t]) for t in range(3)] + [tail]
    rows += [jnp.zeros((D_MODEL,), F32)] * (8 - len(rows))
    return jnp.stack(rows)


class _ReduceScatter:
    def __init__(self, place):
        self.place = place
        self.started = []

    def send(self, tag, grads):
        pieces = [g[None] if kind == "col" else g.reshape(N_CHIPS, g.shape[0] // N_CHIPS, g.shape[1]) for _, g, kind in grads]
        theirs = _exchange_halves(pieces, "exchange_halves_" + tag)
        parts = [_pair_sum(g, t, self.place[:1], "pair_sum_" + nm) for g, t, (nm, _, _) in zip(pieces, theirs, grads)]
        started = _scatter_start("scatter_" + tag, parts)
        self.started.append((tag, [nm for nm, _, _ in grads], started))
        return started[3]

    def finish(self, after):
        names, sums = [], []
        for tag, group, started in self.started:
            parts, stacks = _scatter_finish("scatter_" + tag, started, after)
            sums += [_sum_chips(p, s, self.place, "chip_sum_" + nm) for p, s, nm in zip(parts, stacks, group)]
            names += group
        return dict(zip(names, _share_halves(sums)))


def kernel(x, ev_norm_w, ev_w_in, ev_conv_w, ev_w_out, od_norm_w, od_w_in, od_q_norm_w, od_k_norm_w, od_sinks, od_w_out, loss_target, m_ev_norm_w, m_ev_w_in, m_ev_conv_w, m_ev_w_out, m_od_norm_w, m_od_w_in, m_od_q_norm_w, m_od_k_norm_w, m_od_sinks, m_od_w_out, v_ev_norm_w, v_ev_w_in, v_ev_conv_w, v_ev_w_out, v_od_norm_w, v_od_w_in, v_od_q_norm_w, v_od_k_norm_w, v_od_sinks, v_od_w_out):
    my_chip = 2 * lax.axis_index("x") + lax.axis_index("y")
    place = jnp.stack([lax.axis_index("c"), my_chip]).astype(jnp.int32)
    shard_w = D_MODEL // N_CHIPS
    conv_shard = CONV_WIDTH // N_CHIPS

    small_in = jnp.zeros((8, shard_w), F32)
    small_in = small_in.at[0].set(od_norm_w[0]).at[1:4, :conv_shard].set(ev_conv_w[0])
    small_in = lax.dynamic_update_slice(jnp.zeros((N_CHIPS, 8, shard_w), F32), small_in[None], (my_chip, 0, 0))
    chip = place[1:]
    first_kinds, late_kinds = ("col", "whole"), ("row", "col", "row")
    w_in0_own_place, w_in0_shard = _cast_into_gathered(ev_w_in[0], "col", chip, "cast_w_in0", keep_shard=True)
    first = _gather_start("gather_first", [w_in0_own_place, small_in], first_kinds)
    late_own = [_cast_into_gathered(ev_w_out[0], "row", chip, "cast_w_out0"),
                _cast_into_gathered(od_w_in[0], "col", chip, "cast_w_in1"),
                _cast_into_gathered(od_w_out[0], "row", chip, "cast_w_out1")]
    late = []

    def weights_first(after):
        w_in0, small_all = _gather_finish("gather_first", first, first_kinds, after)
        late.append(_gather_start("gather_late", late_own, late_kinds, after=w_in0))
        od_norm_full = small_all[:, 0, :].reshape(1, D_MODEL)
        conv_full = jnp.transpose(small_all[:, 1:4, :conv_shard], (1, 0, 2)).reshape(3, CONV_WIDTH)
        return w_in0, conv_full, od_norm_full, late[0][3]

    def weights_late(after):
        return _gather_finish("gather_late", late[0], late_kinds, after)

    reduce_scatter = _ReduceScatter(place)
    sq, grad_x, small = _local_step(x[0], loss_target[0], ev_norm_w, od_q_norm_w, od_k_norm_w, od_sinks,
                                    (w_in0_shard, chip, first[3]), weights_first, weights_late, reduce_scatter.send)
    loss = lax.psum(0.5 * sq / D_MODEL, ("x", "y", "c"))

    big = reduce_scatter.finish(grad_x)
    g_ev_w_in, g_ev_w_out, g_od_w_in, g_od_w_out = big["ev_w_in"], big["ev_w_out"], big["od_w_in"], big["od_w_out"]
    tot = _allreduce_small(_pack_small_grads(small))
    g_ev_norm = tot[0:1]
    g_od_norm = lax.dynamic_slice(tot, (1, my_chip * shard_w), (1, shard_w))
    g_conv = lax.dynamic_slice(tot, (2, my_chip * conv_shard), (3, conv_shard))
    g_qw, g_kw, g_sinks = tot[5:6, 0:HEAD_DIM], tot[5:6, HEAD_DIM:2 * HEAD_DIM], tot[5:6, 2 * HEAD_DIM:2 * HEAD_DIM + Q_HEADS]

    upd = {}
    upd["ev_w_in"] = _adamw(ev_w_in[0], g_ev_w_in, m_ev_w_in[0], v_ev_w_in[0], "adamw_ev_w_in")
    upd["ev_w_out"] = _adamw(ev_w_out[0], g_ev_w_out, m_ev_w_out[0], v_ev_w_out[0], "adamw_ev_w_out")
    upd["od_w_in"] = _adamw(od_w_in[0], g_od_w_in, m_od_w_in[0], v_od_w_in[0], "adamw_od_w_in")
    upd["od_w_out"] = _adamw(od_w_out[0], g_od_w_out, m_od_w_out[0], v_od_w_out[0], "adamw_od_w_out")
    smalls = (("ev_norm_w", ev_norm_w, g_ev_norm, m_ev_norm_w, v_ev_norm_w),
              ("ev_conv_w", ev_conv_w, g_conv, m_ev_conv_w, v_ev_conv_w),
              ("od_norm_w", od_norm_w, g_od_norm, m_od_norm_w, v_od_norm_w),
              ("od_q_norm_w", od_q_norm_w, g_qw, m_od_q_norm_w, v_od_q_norm_w),
              ("od_k_norm_w", od_k_norm_w, g_kw, m_od_k_norm_w, v_od_k_norm_w),
              ("od_sinks", od_sinks, g_sinks, m_od_sinks, v_od_sinks))
    sizes = [w.size for _, w, _, _, _ in smalls]
    padded = 8 * 128 * math.ceil(sum(sizes) / (8 * 128))
    pack = lambda arrs, fill: jnp.concatenate(
        [a.reshape(-1) for a in arrs] + [jnp.full((padded - sum(sizes),), fill, F32)]).reshape(8, padded // 8)
    packed = _adamw(pack([w for _, w, _, _, _ in smalls], 0.0), pack([g for _, _, g, _, _ in smalls], 0.0),
                    pack([m for _, _, _, m, _ in smalls], 0.0), pack([v for _, _, _, _, v in smalls], 1.0), "adamw_small")
    offs = [sum(sizes[:i]) for i in range(len(sizes))]
    grads = {"ev_w_in": g_ev_w_in[None], "ev_w_out": g_ev_w_out[None], "od_w_in": g_od_w_in[None], "od_w_out": g_od_w_out[None]}
    for (nm, w, g, _, _), off, size in zip(smalls, offs, sizes):
        upd[nm] = tuple(p.reshape(-1)[off:off + size].reshape(w.shape) for p in packed)
        grads[nm] = g.reshape(w.shape)
    for nm in ("ev_w_in", "ev_w_out", "od_w_in", "od_w_out"):
        upd[nm] = tuple(u[None] for u in upd[nm])
    order = ("ev_norm_w", "ev_w_in", "ev_conv_w", "ev_w_out", "od_norm_w", "od_w_in", "od_q_norm_w", "od_k_norm_w", "od_sinks", "od_w_out")
    return (loss, grad_x[None], *[grads[nm] for nm in order], *[upd[nm][0] for nm in order],
            *[upd[nm][1] for nm in order], *[upd[nm][2] for nm in order])
```

```python
import functools
import math

import jax
import jax.numpy as jnp
from jax import lax
from jax.experimental import pallas as pl
from jax.experimental.pallas import tpu as pltpu

F32 = jnp.float32
BF16 = jnp.bfloat16

D_MODEL = 2048
RET_HEADS = 4
RET_DIM = 256
RET_WIDTH = 1024
CONV_WIDTH = 1024
EVEN_IN = 8192
Q_HEADS = 32
HEAD_DIM = 64
KV_HEADS = 4
KV_WIDTH = 256
ATTN_WIDTH = 2048
ODD_IN = 4608
BLK = 128
ROPE_THETA = 10000.0
EPS = 1e-6
ADAM_LR = 0.001
ADAM_B1 = 0.9
ADAM_B2 = 0.999
ADAM_EPS = 1e-08
ADAM_WD = 0.01
ADAM_STEP = 10
N_CHIPS = 4
N_DEV = 8
VMEM_LIMIT_BYTES = 48 * 1024 * 1024
MESH = pl.DeviceIdType.MESH
ANY = pl.BlockSpec(memory_space=pl.ANY)


def _params(*sem):
    return pltpu.CompilerParams(dimension_semantics=sem, vmem_limit_bytes=VMEM_LIMIT_BYTES)


def _dot(a, b):
    return jnp.dot(a, b, preferred_element_type=F32)


def _dot_nt(a, b):
    return lax.dot_general(a, b, (((1,), (1,)), ((), ())), preferred_element_type=F32)


def _dot_tn(a, b):
    return lax.dot_general(a, b, (((0,), (0,)), ((), ())), preferred_element_type=F32)


def _sigmoid(x):
    return 1.0 / (1.0 + jnp.exp(-x))


def _mm(a, b, *, mode, tm, tn, tk, out_dtype, name, add=None, after=None):
    if mode == "nn":
        (m, k), n = a.shape, b.shape[1]
    elif mode == "nt":
        (m, k), n = a.shape, b.shape[0]
    else:
        (k, m), n = a.shape, b.shape[1]
    tm, tn, tk = min(tm, m), min(tn, n), min(tk, k)
    assert m % tm == 0 and n % tn == 0 and k % tk == 0, (name, m, n, k)
    nk = k // tk
    dot = {"nn": _dot, "nt": _dot_nt, "tn": _dot_tn}[mode]
    a_spec = (pl.BlockSpec((tk, tm), lambda i, j, kk: (kk, i)) if mode == "tn"
              else pl.BlockSpec((tm, tk), lambda i, j, kk: (i, kk)))
    b_spec = (pl.BlockSpec((tn, tk), lambda i, j, kk: (j, kk)) if mode == "nt"
              else pl.BlockSpec((tk, tn), lambda i, j, kk: (kk, j)))
    o_spec = pl.BlockSpec((tm, tn), lambda i, j, kk: (i, j))
    has_add = add is not None

    def body(*refs):
        a_ref, b_ref = refs[0], refs[1]
        add_ref = refs[2] if has_add else None
        o_ref, acc_ref = refs[-2], refs[-1]
        p = dot(a_ref[...], b_ref[...])

        def finish(total):
            if has_add:
                total = total + add_ref[...].astype(F32)
            o_ref[...] = total.astype(out_dtype)

        if nk == 1:
            finish(p)
        else:
            kk = pl.program_id(2)

            @pl.when(kk == 0)
            def _():
                acc_ref[...] = p

            @pl.when(jnp.logical_and(kk > 0, kk < nk - 1))
            def _():
                acc_ref[...] += p

            @pl.when(kk == nk - 1)
            def _():
                finish(acc_ref[...] + p)

    in_specs = [a_spec, b_spec] + ([o_spec] if has_add else []) + ([ANY] if after is not None else [])
    args = (a, b) + ((add,) if has_add else ()) + ((after,) if after is not None else ())
    return pl.pallas_call(
        body, name=name, grid=(m // tm, n // tn, nk), in_specs=in_specs, out_specs=o_spec,
        out_shape=jax.ShapeDtypeStruct((m, n), out_dtype),
        scratch_shapes=[pltpu.VMEM((tm, tn) if nk > 1 else (8, 128), F32)],
        compiler_params=_params("parallel", "parallel", "arbitrary"),
    )(*args)


def _mm_shifted(a, b, shift, *, b_shifted, first, count, total, tm, tn, out_dtype, name, into=None, after=None):
    m, k = a.shape
    tm = min(tm, m)
    assert m % tm == 0
    col = lambda j, shift_ref: (shift_ref[0] + first + j) % total
    extra = [arr for arr in (into, after) if arr is not None]

    def body(shift_ref, a_ref, b_ref, *rest):
        del shift_ref
        rest[-1][...] = _dot(a_ref[...], b_ref[...]).astype(out_dtype)

    return pl.pallas_call(
        body, name=name, out_shape=jax.ShapeDtypeStruct((m, total * tn), out_dtype),
        grid_spec=pltpu.PrefetchScalarGridSpec(
            num_scalar_prefetch=1, grid=(m // tm, count),
            in_specs=[pl.BlockSpec((tm, k), lambda i, j, s: (i, 0)),
                      pl.BlockSpec((k, tn), (lambda i, j, s: (0, col(j, s))) if b_shifted else (lambda i, j, s: (0, j)))]
            + [ANY] * len(extra),
            out_specs=pl.BlockSpec((tm, tn), lambda i, j, s: (i, col(j, s)))),
        input_output_aliases={3: 0} if into is not None else {},
        compiler_params=_params("parallel", "arbitrary"))(shift, a, b, *extra)


def _mm_rows(a, b, rows_in, vecs_in, out_shapes, epilogue, *, tm, name):
    m, k = a.shape
    n = b.shape[1]
    assert m % tm == 0
    row = pl.BlockSpec((tm, n), lambda i: (i, 0))

    def body(a_ref, b_ref, *rest):
        epilogue(_dot(a_ref[...], b_ref[...]), *rest)

    out_specs = [row if s.shape[0] == m else pl.BlockSpec(s.shape, lambda i: (0, 0)) for s in out_shapes]
    return pl.pallas_call(
        body, name=name, grid=(m // tm,),
        in_specs=[pl.BlockSpec((tm, k), lambda i: (i, 0)), pl.BlockSpec((k, n), lambda i: (0, 0))] + [row] * len(rows_in)
        + [pl.BlockSpec((1, n), lambda i: (0, 0))] * len(vecs_in),
        out_specs=out_specs, out_shape=out_shapes, compiler_params=_params("arbitrary"),
    )(a, b, *rows_in, *vecs_in)


def _cast_into_gathered(w, kind, chip, name, keep_shard=False):
    r, c = w.shape
    tr = min(r, 512)
    per = r // tr

    def body(chip_ref, w_ref, *outs):
        del chip_ref
        for o_ref in outs:
            o_ref[...] = w_ref[...].astype(BF16)

    if kind == "col":
        shape, out_map = (r, N_CHIPS * c), (lambda i, chip_ref: (i, chip_ref[0]))
    else:
        shape, out_map = (N_CHIPS * r, c), (lambda i, chip_ref: (chip_ref[0] * per + i, 0))
    plain = pl.BlockSpec((tr, c), lambda i, chip_ref: (i, 0))
    out = pl.pallas_call(
        body, name=name,
        out_shape=[jax.ShapeDtypeStruct(shape, BF16)] + ([jax.ShapeDtypeStruct((r, c), BF16)] if keep_shard else []),
        grid_spec=pltpu.PrefetchScalarGridSpec(
            num_scalar_prefetch=1, grid=(per,), in_specs=[plain],
            out_specs=[pl.BlockSpec((tr, c), out_map)] + ([plain] if keep_shard else [])),
        compiler_params=_params("parallel"))(chip, w)
    return out if keep_shard else out[0]


def _rmsnorm(x, w, name, after=None):
    s, d = x.shape
    tr = 256

    def body(x_ref, w_ref, *rest):
        xv = x_ref[...]
        rstd = lax.rsqrt(jnp.mean(xv * xv, axis=-1, keepdims=True) + EPS)
        rest[-1][...] = (xv * rstd * w_ref[...]).astype(BF16)

    return pl.pallas_call(
        body, name=name, grid=(s // tr,),
        in_specs=[pl.BlockSpec((tr, d), lambda i: (i, 0)), pl.BlockSpec((1, d), lambda i: (0, 0))]
        + ([ANY] if after is not None else []),
        out_specs=pl.BlockSpec((tr, d), lambda i: (i, 0)),
        out_shape=jax.ShapeDtypeStruct((s, d), BF16), compiler_params=_params("parallel"),
    )(x, w, *((after,) if after is not None else ()))


def _rmsnorm_bwd(x, w, dh, dres, name, with_bf16):
    s, d = x.shape
    tr = 256

    def body(x_ref, w_ref, dh_ref, dres_ref, *outs):
        dx_ref, dw_ref = outs[0], outs[-1]
        xv = x_ref[...]
        rstd = lax.rsqrt(jnp.mean(xv * xv, axis=-1, keepdims=True) + EPS)
        nrm = xv * rstd
        dhv = dh_ref[...].astype(F32)
        dn = dhv * w_ref[...]
        dx = dres_ref[...] + rstd * (dn - nrm * jnp.mean(dn * nrm, axis=-1, keepdims=True))
        dx_ref[...] = dx
        if with_bf16:
            outs[1][...] = dx.astype(BF16)

        @pl.when(pl.program_id(0) == 0)
        def _():
            dw_ref[...] = jnp.zeros_like(dw_ref)

        dw_ref[...] += jnp.sum(dhv * nrm, axis=0, keepdims=True)

    row = pl.BlockSpec((tr, d), lambda i: (i, 0))
    out_shape = [jax.ShapeDtypeStruct((s, d), F32)] + ([jax.ShapeDtypeStruct((s, d), BF16)] if with_bf16 else [])
    out_shape.append(jax.ShapeDtypeStruct((8, d), F32))
    out_specs = [row] * (2 if with_bf16 else 1) + [pl.BlockSpec((8, d), lambda i: (0, 0))]
    return pl.pallas_call(
        body, name=name, grid=(s // tr,),
        in_specs=[row, pl.BlockSpec((1, d), lambda i: (0, 0)), row, row],
        out_specs=out_specs, out_shape=out_shape, compiler_params=_params("arbitrary"))(x, w, dh, dres)


def _adamw(w, g, m, v, name):
    r, c = w.shape
    tr = min(r, 256)
    assert r % tr == 0

    def body(w_ref, g_ref, m_ref, v_ref, d_ref, nm_ref, nv_ref):
        gv = g_ref[...]
        nm = ADAM_B1 * m_ref[...] + (1.0 - ADAM_B1) * gv
        nv = ADAM_B2 * v_ref[...] + (1.0 - ADAM_B2) * (gv * gv)
        m_hat = nm / (1.0 - ADAM_B1 ** ADAM_STEP)
        v_hat = nv / (1.0 - ADAM_B2 ** ADAM_STEP)
        d_ref[...] = -ADAM_LR * (m_hat / (jnp.sqrt(v_hat) + ADAM_EPS) + ADAM_WD * w_ref[...])
        nm_ref[...] = nm
        nv_ref[...] = nv

    spec = pl.BlockSpec((tr, c), lambda i: (i, 0))
    shp = jax.ShapeDtypeStruct((r, c), F32)
    return pl.pallas_call(body, name=name, grid=(r // tr,), in_specs=[spec] * 4, out_specs=[spec] * 3,
                          out_shape=[shp] * 3, compiler_params=_params("parallel"))(w, g, m, v)


def _rope_tables(s, dim):
    inv = 1.0 / (ROPE_THETA ** (jnp.arange(0, dim, 2, dtype=F32) / dim))
    ang = jnp.arange(s).astype(F32)[:, None] * inv[None, :]
    return jnp.cos(ang), jnp.sin(ang)


def _rope_half(x, cos, sin):
    h = x.shape[1] // 2
    x1, x2 = x[:, :h], x[:, h:]
    return jnp.concatenate([x1 * cos - x2 * sin, x2 * cos + x1 * sin], axis=1)


def _unrope_half(dy, cos, sin):
    h = dy.shape[1] // 2
    d1, d2 = dy[:, :h], dy[:, h:]
    return jnp.concatenate([d1 * cos + d2 * sin, d2 * cos - d1 * sin], axis=1)


def _lane(shape):
    return lax.broadcasted_iota(jnp.int32, shape, 1)


def _partner64(x):
    w = x.shape[1]
    first = (_lane(x.shape) % HEAD_DIM) < (HEAD_DIM // 2)
    return jnp.where(first, pltpu.roll(x, w - HEAD_DIM // 2, 1), pltpu.roll(x, HEAD_DIM // 2, 1))


def _tile_lanes(t, reps):
    return t if reps == 1 else jnp.concatenate([t] * reps, axis=1)


def _group_mean(x, ones_bd):
    hi = x.astype(BF16)
    lo = (x - hi.astype(F32)).astype(BF16)
    return _dot(hi, ones_bd) + _dot(lo, ones_bd)


def _block_diag_mean(width):
    idx = jnp.arange(width) // HEAD_DIM
    return jnp.where(idx[:, None] == idx[None, :], 1.0 / HEAD_DIM, 0.0).astype(BF16)


def _retention_tables():
    h = RET_HEADS
    log_g = jnp.log(1.0 - 2.0 ** (-5.0 - jnp.arange(h, dtype=F32)))
    idx = jnp.arange(BLK, dtype=F32)
    diff = idx[:, None] - idx[None, :]
    intra = jnp.where(diff >= 0, jnp.exp(log_g[:, None, None] * jnp.maximum(diff, 0.0)), 0.0).astype(F32)
    q_dec = jnp.exp(log_g[:, None] * (idx[None, :] + 1.0)).astype(F32)[:, :, None]
    k_dec = jnp.exp(log_g[:, None] * (BLK - 1.0 - idx[None, :])).astype(F32)[:, :, None]
    chunk_dec = jnp.exp(log_g * BLK).astype(F32)[:, None, None]
    return intra, q_dec, k_dec, chunk_dec


def _retention_fwd(proj, cos, sin, tables):
    s = proj.shape[0]
    nc = s // BLK
    intra, q_dec, k_dec, chunk_dec = tables

    def body(p_ref, cos_ref, sin_ref, in_ref, qd_ref, kd_ref, cd_ref, o_ref, cat_ref, st_ref, state):
        @pl.when(pl.program_id(0) == 0)
        def _():
            state[...] = jnp.zeros_like(state)

        cosv, sinv = cos_ref[...], sin_ref[...]
        for h in range(RET_HEADS):
            c0 = h * RET_DIM
            q = p_ref[:, c0:c0 + RET_DIM].astype(F32)
            k = p_ref[:, RET_WIDTH + c0:RET_WIDTH + c0 + RET_DIM].astype(F32)
            v = p_ref[:, 2 * RET_WIDTH + c0:2 * RET_WIDTH + c0 + RET_DIM]
            g = p_ref[:, 3 * RET_WIDTH + c0:3 * RET_WIDTH + c0 + RET_DIM].astype(F32)
            qb = _rope_half(q, cosv, sinv).astype(BF16)
            kr = _rope_half(k, cosv, sinv) * (RET_DIM ** -0.5)
            kb = kr.astype(BF16)
            scores = _dot_nt(qb, kb) * in_ref[h]
            inner = _dot(scores.astype(BF16), v)
            prev = state[h]
            prev_b = prev.astype(BF16)
            st_ref[h, 0] = prev_b
            o = inner + _dot(qb, prev_b) * qd_ref[h]
            o_ref[:, c0:c0 + RET_DIM] = o
            rstd = lax.rsqrt(jnp.mean(o * o, axis=-1, keepdims=True) + EPS)
            cat_ref[:, c0:c0 + RET_DIM] = (o * rstd * (g * _sigmoid(g))).astype(BF16)
            state[h] = cd_ref[h] * prev + _dot_tn((kr * kd_ref[h]).astype(BF16), v)

    full = lambda shape: pl.BlockSpec(shape, lambda n: (0,) * len(shape))
    return pl.pallas_call(
        body, name="retention_fwd", grid=(nc,),
        in_specs=[pl.BlockSpec((BLK, 4 * RET_WIDTH), lambda n: (n, 0)),
                  pl.BlockSpec((BLK, RET_DIM // 2), lambda n: (n, 0)), pl.BlockSpec((BLK, RET_DIM // 2), lambda n: (n, 0)),
                  full((RET_HEADS, BLK, BLK)), full((RET_HEADS, BLK, 1)), full((RET_HEADS, BLK, 1)), full((RET_HEADS, 1, 1))],
        out_specs=[pl.BlockSpec((BLK, RET_WIDTH), lambda n: (n, 0)), pl.BlockSpec((BLK, RET_WIDTH), lambda n: (n, 0)),
                   pl.BlockSpec((RET_HEADS, 1, RET_DIM, RET_DIM), lambda n: (0, n, 0, 0))],
        out_shape=[jax.ShapeDtypeStruct((s, RET_WIDTH), F32), jax.ShapeDtypeStruct((s, D_MODEL), BF16),
                   jax.ShapeDtypeStruct((RET_HEADS, nc, RET_DIM, RET_DIM), BF16)],
        scratch_shapes=[pltpu.VMEM((RET_HEADS, RET_DIM, RET_DIM), F32)],
        compiler_params=_params("arbitrary"))(proj, cos, sin, intra, q_dec, k_dec, chunk_dec)


def _retention_bwd(proj, o, states, dcat, cos, sin, tables):
    s = proj.shape[0]
    nc = s // BLK
    intra, q_dec, k_dec, chunk_dec = tables

    def body(p_ref, o_ref, st_ref, dc_ref, cos_ref, sin_ref, in_ref, qd_ref, kd_ref, cd_ref, dp_ref, dstate):
        @pl.when(pl.program_id(0) == 0)
        def _():
            dstate[...] = jnp.zeros_like(dstate)

        cosv, sinv = cos_ref[...], sin_ref[...]
        for h in range(RET_HEADS):
            c0 = h * RET_DIM
            q = p_ref[:, c0:c0 + RET_DIM].astype(F32)
            k = p_ref[:, RET_WIDTH + c0:RET_WIDTH + c0 + RET_DIM].astype(F32)
            v = p_ref[:, 2 * RET_WIDTH + c0:2 * RET_WIDTH + c0 + RET_DIM]
            g = p_ref[:, 3 * RET_WIDTH + c0:3 * RET_WIDTH + c0 + RET_DIM].astype(F32)
            o = o_ref[:, c0:c0 + RET_DIM]
            dc = dc_ref[:, c0:c0 + RET_DIM].astype(F32)
            rstd = lax.rsqrt(jnp.mean(o * o, axis=-1, keepdims=True) + EPS)
            nrm = o * rstd
            sg = _sigmoid(g)
            dg = dc * nrm * (sg * (1.0 + g * (1.0 - sg)))
            dn = dc * (g * sg)
            do = rstd * (dn - nrm * jnp.mean(dn * nrm, axis=-1, keepdims=True))
            qb = _rope_half(q, cosv, sinv).astype(BF16)
            kr = _rope_half(k, cosv, sinv) * (RET_DIM ** -0.5)
            kb = kr.astype(BF16)
            mask = in_ref[h]
            qd, kd = qd_ref[h], kd_ref[h]
            prev_b = st_ref[h, 0]
            dnext = dstate[h]
            dnext_b = dnext.astype(BF16)
            att = (_dot_nt(qb, kb) * mask).astype(BF16)
            do_b = do.astype(BF16)
            doq = (do * qd).astype(BF16)
            dv = _dot_tn(att, do_b) + _dot((kr * kd).astype(BF16), dnext_b)
            ds = (_dot_nt(do_b, v) * mask).astype(BF16)
            dqr = _dot(ds, kb) + _dot_nt(doq, prev_b)
            dkr = _dot_tn(ds, qb) + _dot_nt(v, dnext_b) * kd
            dstate[h] = cd_ref[h] * dnext + _dot_tn(qb, doq)
            dq = _unrope_half(dqr, cosv, sinv)
            dk = _unrope_half(dkr * (RET_DIM ** -0.5), cosv, sinv)
            dp_ref[:, c0:c0 + RET_DIM] = dq.astype(BF16)
            dp_ref[:, RET_WIDTH + c0:RET_WIDTH + c0 + RET_DIM] = dk.astype(BF16)
            dp_ref[:, 2 * RET_WIDTH + c0:2 * RET_WIDTH + c0 + RET_DIM] = dv.astype(BF16)
            dp_ref[:, 3 * RET_WIDTH + c0:3 * RET_WIDTH + c0 + RET_DIM] = dg.astype(BF16)

    rev = lambda n: nc - 1 - n
    full = lambda shape: pl.BlockSpec(shape, lambda n: (0,) * len(shape))
    return pl.pallas_call(
        body, name="retention_bwd", grid=(nc,),
        in_specs=[pl.BlockSpec((BLK, 4 * RET_WIDTH), lambda n: (rev(n), 0)),
                  pl.BlockSpec((BLK, RET_WIDTH), lambda n: (rev(n), 0)),
                  pl.BlockSpec((RET_HEADS, 1, RET_DIM, RET_DIM), lambda n: (0, rev(n), 0, 0)),
                  pl.BlockSpec((BLK, RET_WIDTH), lambda n: (rev(n), 0)),
                  pl.BlockSpec((BLK, RET_DIM // 2), lambda n: (rev(n), 0)), pl.BlockSpec((BLK, RET_DIM // 2), lambda n: (rev(n), 0)),
                  full((RET_HEADS, BLK, BLK)), full((RET_HEADS, BLK, 1)), full((RET_HEADS, BLK, 1)), full((RET_HEADS, 1, 1))],
        out_specs=pl.BlockSpec((BLK, 4 * RET_WIDTH), lambda n: (rev(n), 0)),
        out_shape=jax.ShapeDtypeStruct((s, EVEN_IN), BF16),
        scratch_shapes=[pltpu.VMEM((RET_HEADS, RET_DIM, RET_DIM), F32)],
        compiler_params=_params("arbitrary"))(proj, o, states, dcat, cos, sin, intra, q_dec, k_dec, chunk_dec)


CONV_ROWS = 256
HALO = 16


def _conv_pieces(p, halo, conv_w, first):
    rows = p.shape[0]
    gb, gc, u, gv = (p[:, i * CONV_WIDTH:(i + 1) * CONV_WIDTH] for i in range(4))
    cu = gc * u
    hcu = halo[:, CONV_WIDTH:2 * CONV_WIDTH] * halo[:, 2 * CONV_WIDTH:3 * CONV_WIDTH]
    hcu = jnp.where(first, 0.0, hcu)
    r1, r2 = hcu[HALO - 1:HALO], hcu[HALO - 2:HALO - 1]
    row = lax.broadcasted_iota(jnp.int32, cu.shape, 0)
    m1 = jnp.where(row == 0, r1, pltpu.roll(cu, 1, 0))
    m2 = jnp.where(row == 0, r2, jnp.where(row == 1, r1, pltpu.roll(cu, 2, 0)))
    del rows, conv_w
    return gb, gc, u, gv, cu, m1, m2


def _conv_fwd(proj, conv_w, cat):
    s = proj.shape[0]
    per = CONV_ROWS // HALO

    def body(p_ref, halo_ref, w_ref, cat_in, cat_ref):
        del cat_in
        first = pl.program_id(0) == 0
        gb, _, _, gv, cu, m1, m2 = _conv_pieces(p_ref[...].astype(F32), halo_ref[...].astype(F32), None, first)
        conv = w_ref[0:1, :] * m2 + w_ref[1:2, :] * m1 + w_ref[2:3, :] * cu
        cat_ref[...] = (gb * conv * (gv * _sigmoid(gv))).astype(BF16)

    return pl.pallas_call(
        body, name="conv_fwd", grid=(s // CONV_ROWS,),
        in_specs=[pl.BlockSpec((CONV_ROWS, 4 * CONV_WIDTH), lambda i: (i, 1)),
                  pl.BlockSpec((HALO, 4 * CONV_WIDTH), lambda i: (jnp.maximum(i * per - 1, 0), 1)),
                  pl.BlockSpec((3, CONV_WIDTH), lambda i: (0, 0)), ANY],
        out_specs=pl.BlockSpec((CONV_ROWS, CONV_WIDTH), lambda i: (i, 1)),
        out_shape=jax.ShapeDtypeStruct(cat.shape, cat.dtype), input_output_aliases={3: 0},
        compiler_params=_params("parallel"))(proj, proj, conv_w, cat)


def _conv_bwd(proj, dcat, conv_w, dproj):
    s = proj.shape[0]
    per = CONV_ROWS // HALO
    last_halo = s // HALO - 1
    nsteps = s // CONV_ROWS

    def body(p_ref, halo_ref, nxt_ref, dc_ref, dnxt_ref, w_ref, dp_in, dp_ref, dw_ref):
        del dp_in
        i = pl.program_id(0)
        gb, gc, u, gv, cu, m1, m2 = _conv_pieces(p_ref[...].astype(F32), halo_ref[...].astype(F32), None, i == 0)
        w0, w1, w2 = w_ref[0:1, :], w_ref[1:2, :], w_ref[2:3, :]
        conv = w0 * m2 + w1 * m1 + w2 * cu
        dco = dc_ref[...].astype(F32)
        sg = _sigmoid(gv)
        silu = gv * sg
        dgb = dco * conv * silu
        dgv = dco * gb * conv * (sg * (1.0 + gv * (1.0 - sg)))
        dconv = dco * gb * silu
        nxt = nxt_ref[...].astype(F32)
        ngv = nxt[:, 3 * CONV_WIDTH:]
        dnext = dnxt_ref[...].astype(F32) * nxt[:, :CONV_WIDTH] * (ngv * _sigmoid(ngv))
        dnext = jnp.where(i == nsteps - 1, 0.0, dnext)
        n1, n2 = dnext[0:1], dnext[1:2]
        row = lax.broadcasted_iota(jnp.int32, dconv.shape, 0)
        p1 = jnp.where(row == CONV_ROWS - 1, n1, pltpu.roll(dconv, CONV_ROWS - 1, 0))
        p2 = jnp.where(row == CONV_ROWS - 1, n2, jnp.where(row == CONV_ROWS - 2, n1, pltpu.roll(dconv, CONV_ROWS - 2, 0)))
        dcu = w2 * dconv + w1 * p1 + w0 * p2
        dp_ref[...] = jnp.concatenate([dgb, dcu * u, dcu * gc, dgv], axis=1).astype(BF16)

        @pl.when(i == 0)
        def _():
            dw_ref[...] = jnp.zeros_like(dw_ref)

        taps = [jnp.sum(dconv * m, axis=0, keepdims=True) for m in (m2, m1, cu)]
        r8 = lax.broadcasted_iota(jnp.int32, dw_ref.shape, 0)
        dw_ref[...] += jnp.where(r8 == 0, taps[0], jnp.where(r8 == 1, taps[1], jnp.where(r8 == 2, taps[2], 0.0)))

    return pl.pallas_call(
        body, name="conv_bwd", grid=(nsteps,),
        in_specs=[pl.BlockSpec((CONV_ROWS, 4 * CONV_WIDTH), lambda i: (i, 1)),
                  pl.BlockSpec((HALO, 4 * CONV_WIDTH), lambda i: (jnp.maximum(i * per - 1, 0), 1)),
                  pl.BlockSpec((HALO, 4 * CONV_WIDTH), lambda i: (jnp.minimum((i + 1) * per, last_halo), 1)),
                  pl.BlockSpec((CONV_ROWS, CONV_WIDTH), lambda i: (i, 1)),
                  pl.BlockSpec((HALO, CONV_WIDTH), lambda i: (jnp.minimum((i + 1) * per, last_halo), 1)),
                  pl.BlockSpec((3, CONV_WIDTH), lambda i: (0, 0)), ANY],
        out_specs=[pl.BlockSpec((CONV_ROWS, 4 * CONV_WIDTH), lambda i: (i, 1)), pl.BlockSpec((8, CONV_WIDTH), lambda i: (0, 0))],
        out_shape=[jax.ShapeDtypeStruct(dproj.shape, dproj.dtype), jax.ShapeDtypeStruct((8, CONV_WIDTH), F32)],
        input_output_aliases={6: 0},
        compiler_params=_params("arbitrary"))(proj, proj, proj, dcat, dcat, conv_w, dproj)


GROUP_WIDTH = 8 * HEAD_DIM
GROUP_HEADS = 8
SLAB = 128
Q_COL = 0
KV_COL = 4
GATE_COL = 5
K_SLAB0 = ATTN_WIDTH // SLAB
V_SLAB0 = (ATTN_WIDTH + KV_WIDTH) // SLAB
KV_ROWS = 512


def _half_mask(shape, which):
    return (_lane(shape) // HEAD_DIM) == which


def _dup_head(slab, which):
    kept = jnp.where(_half_mask(slab.shape, which), slab, 0.0)
    return kept + pltpu.roll(kept, HEAD_DIM, 1)


def _stack_heads(x):
    parts = []
    for sl in range(GROUP_WIDTH // SLAB):
        slab = x[:, sl * SLAB:(sl + 1) * SLAB]
        parts += [jnp.where(_half_mask(slab.shape, e), slab, 0.0) for e in range(2)]
    return jnp.concatenate(parts, axis=0)


def _unstack_heads(y):
    slabs = []
    for sl in range(GROUP_WIDTH // SLAB):
        a, b = y[(2 * sl) * BLK:(2 * sl + 1) * BLK], y[(2 * sl + 1) * BLK:(2 * sl + 2) * BLK]
        slabs.append(jnp.where(_half_mask(a.shape, 0), a, b))
    return jnp.concatenate(slabs, axis=1)


def _q_prep(q, qw, cosf, sins, ones_bd):
    rstd = lax.rsqrt(_group_mean(q * q, ones_bd) + EPS)
    nrm = q * rstd
    y = nrm * qw
    return nrm, rstd, y * cosf + _partner64(y) * sins


def _band(tri_ref, n):
    own = tri_ref[...] > 0.5
    return own, jnp.where(jnp.logical_and(n == 0, jnp.logical_not(own)), -1e30, 0.0)


def _fold(pair, own):
    return jnp.where(own, pair[:, BLK:], pair[:, :BLK])


def _unfold(folded, own):
    return jnp.concatenate([jnp.where(own, 0.0, folded), jnp.where(own, folded, 0.0)], axis=1)


def _head_probs(raw_scores, sink, own, bias):
    sc = _fold(raw_scores, own) * (HEAD_DIM ** -0.5) + bias
    m = jnp.maximum(jnp.max(sc, axis=-1, keepdims=True), sink)
    p = jnp.exp(sc - m)
    psink = jnp.exp(sink - m)
    inv = 1.0 / (jnp.sum(p, axis=-1, keepdims=True) + psink)
    return p * inv, psink * inv


def _k_prep(k, kw, cosf, sins, ones_bd):
    rstd = lax.rsqrt(_group_mean(k * k, ones_bd) + EPS)
    nrm = k * rstd
    y = nrm * kw
    return nrm, rstd, y * cosf + _partner64(y) * sins


def _qk_prep(proj, qw, kw, cos, sins, ones_q, ones_kv):
    s = proj.shape[0]
    rows = min(KV_ROWS, s)

    def body(p_ref, qw_ref, kw_ref, cos_ref, sin_ref, oq_ref, ok_ref, o_ref):
        j = pl.program_id(1)

        @pl.when(j < KV_COL)
        def _():
            cosf, sinf = _tile_lanes(cos_ref[...], 4), _tile_lanes(sin_ref[...], 4)
            o_ref[...] = _q_prep(p_ref[...].astype(F32), qw_ref[...], cosf, sinf, oq_ref[...])[2].astype(BF16)

        @pl.when(j == KV_COL)
        def _():
            cosf, sinf = _tile_lanes(cos_ref[...], 2), _tile_lanes(sin_ref[...], 2)
            kr = _k_prep(p_ref[:, :KV_WIDTH].astype(F32), kw_ref[...], cosf, sinf, ok_ref[...])[2]
            o_ref[...] = jnp.concatenate([kr.astype(BF16), p_ref[:, KV_WIDTH:]], axis=1)

    full = lambda shape: pl.BlockSpec(shape, lambda i, j: (0,) * len(shape))
    tab = pl.BlockSpec((rows, SLAB), lambda i, j: (i, 0))
    blk = pl.BlockSpec((rows, GROUP_WIDTH), lambda i, j: (i, j))
    return pl.pallas_call(
        body, name="qk_prep", grid=(s // rows, KV_COL + 1),
        in_specs=[blk, full((1, GROUP_WIDTH)), full((1, KV_WIDTH)), tab, tab, full((GROUP_WIDTH, GROUP_WIDTH)),
                  full((KV_WIDTH, KV_WIDTH))],
        out_specs=blk, out_shape=jax.ShapeDtypeStruct((s, ATTN_WIDTH + 2 * KV_WIDTH), BF16),
        compiler_params=_params("parallel", "arbitrary"))(proj, qw, kw, cos, sins, ones_q, ones_kv)


def _keys_values(kc_ref, kp_ref, vc_ref, vp_ref, which):
    dup = lambda ref: _dup_head(ref[...].astype(F32), which)
    return (jnp.concatenate([dup(kp_ref), dup(kc_ref)], axis=0).astype(BF16),
            jnp.concatenate([dup(vp_ref), dup(vc_ref)], axis=0).astype(BF16))


def _swa_specs():
    prev = lambda n: jnp.maximum(n - 1, 0)
    slab = lambda col0, row: pl.BlockSpec((BLK, SLAB), lambda g, n: (row(n), col0 + g // 2))
    cur = lambda n: n
    full = lambda shape: pl.BlockSpec(shape, lambda g, n: (0,) * len(shape))
    return dict(
        sinks=pl.BlockSpec(memory_space=pltpu.SMEM),
        gate=pl.BlockSpec((BLK, GROUP_WIDTH), lambda g, n: (n, GATE_COL + g)),
        kc=slab(K_SLAB0, cur), kp=slab(K_SLAB0, prev), vc=slab(V_SLAB0, cur), vp=slab(V_SLAB0, prev),
        tri=full((BLK, BLK)), group=pl.BlockSpec((BLK, GROUP_WIDTH), lambda g, n: (n, g)))


def _lower_triangle():
    return jnp.tril(jnp.ones((BLK, BLK), F32))


def _swa_fwd(qk, proj, sinks):
    s = proj.shape[0]
    nb = s // BLK
    sp = _swa_specs()

    def body(sink_ref, q_ref, kc_ref, kp_ref, vc_ref, vp_ref, gate_ref, tri_ref, ag_ref, o_ref):
        g, n = pl.program_id(0), pl.program_id(1)
        kcat, vcat = _keys_values(kc_ref, kp_ref, vc_ref, vp_ref, g % 2)
        scores = _dot_nt(_stack_heads(q_ref[...]), kcat)
        own, bias = _band(tri_ref, n)
        probs = []
        for j in range(GROUP_HEADS):
            p, _ = _head_probs(scores[j * BLK:(j + 1) * BLK], sink_ref[g * GROUP_HEADS + j], own, bias)
            probs.append(_unfold(p, own).astype(BF16))
        o = _unstack_heads(_dot(jnp.concatenate(probs, axis=0), vcat))
        gate = gate_ref[...].astype(F32)
        o_ref[...] = o.astype(BF16)
        ag_ref[...] = (o * (gate * _sigmoid(gate))).astype(BF16)

    shp = jax.ShapeDtypeStruct((s, ATTN_WIDTH), BF16)
    return pl.pallas_call(
        body, name="swa_fwd", grid=(KV_HEADS, nb),
        in_specs=[sp["sinks"], sp["group"], sp["kc"], sp["kp"], sp["vc"], sp["vp"], sp["gate"], sp["tri"]],
        out_specs=[sp["group"], sp["group"]], out_shape=[shp, shp],
        compiler_params=_params("parallel", "arbitrary"),
    )(sinks, qk, qk, qk, qk, qk, proj, _lower_triangle())


def _swa_bwd(qk, proj, dag, sinks):
    s = proj.shape[0]
    nb = s // BLK
    sp = _swa_specs()

    def body(sink_ref, q_ref, kc_ref, kp_ref, vc_ref, vp_ref, gate_ref, dag_ref, tri_ref,
             dq_ref, dkc_ref, dkp_ref, dvc_ref, dvp_ref, dsink_ref):
        g, n = pl.program_id(0), pl.program_id(1)
        kcat, vcat = _keys_values(kc_ref, kp_ref, vc_ref, vp_ref, g % 2)
        gate = gate_ref[...].astype(F32)
        do = dag_ref[...].astype(F32) * (gate * _sigmoid(gate))
        q_stack = _stack_heads(q_ref[...])
        do_stack = _stack_heads(do).astype(BF16)
        scores = _dot_nt(q_stack, kcat)
        dprobs = _dot_nt(do_stack, vcat)
        own, bias = _band(tri_ref, n)
        probs, dscores, dsinks = [], [], []
        for j in range(GROUP_HEADS):
            rows = slice(j * BLK, (j + 1) * BLK)
            p, psink = _head_probs(scores[rows], sink_ref[g * GROUP_HEADS + j], own, bias)
            dp = _fold(dprobs[rows], own)
            delta = jnp.sum(p * dp, axis=-1, keepdims=True)
            probs.append(_unfold(p, own).astype(BF16))
            dscores.append(_unfold(p * (dp - delta) * (HEAD_DIM ** -0.5), own).astype(BF16))
            dsinks.append(-jnp.sum(psink * delta, axis=0, keepdims=True))
        ds = jnp.concatenate(dscores, axis=0)
        dk = _dot_tn(ds, q_stack)
        dv = _dot_tn(jnp.concatenate(probs, axis=0), do_stack)
        dk = dk + pltpu.roll(dk, HEAD_DIM, 1)
        dv = dv + pltpu.roll(dv, HEAD_DIM, 1)
        dkp_ref[0], dkc_ref[0] = dk[:BLK], dk[BLK:]
        dvp_ref[0], dvc_ref[0] = dv[:BLK], dv[BLK:]
        dq_ref[...] = _unstack_heads(_dot(ds, kcat)).astype(BF16)

        @pl.when(n == 0)
        def _():
            dsink_ref[...] = jnp.zeros_like(dsink_ref)

        r8 = lax.broadcasted_iota(jnp.int32, (8, SLAB), 0)
        upd = jnp.zeros((8, SLAB), F32)
        for j in range(GROUP_HEADS):
            upd = jnp.where(r8 == j, dsinks[j], upd)
        dsink_ref[0] += upd

    cur_out = pl.BlockSpec((1, BLK, SLAB), lambda g, n: (g, n, 0))
    prev_out = pl.BlockSpec((1, BLK, SLAB), lambda g, n: (g, (n + nb - 1) % nb, 0))
    kv_shape = jax.ShapeDtypeStruct((KV_HEADS, s, SLAB), F32)
    return pl.pallas_call(
        body, name="swa_bwd", grid=(KV_HEADS, nb),
        in_specs=[sp["sinks"], sp["group"], sp["kc"], sp["kp"], sp["vc"], sp["vp"], sp["gate"], sp["group"], sp["tri"]],
        out_specs=[sp["group"], cur_out, prev_out, cur_out, prev_out, pl.BlockSpec((1, 8, SLAB), lambda g, n: (g, 0, 0))],
        out_shape=[jax.ShapeDtypeStruct((s, ATTN_WIDTH), BF16), kv_shape, kv_shape, kv_shape, kv_shape,
                   jax.ShapeDtypeStruct((KV_HEADS, 8, SLAB), F32)],
        compiler_params=_params("parallel", "arbitrary"),
    )(sinks, qk, qk, qk, qk, qk, proj, dag, _lower_triangle())


def _swa_bwd_finish(proj, dqr, o, dag, dkc, dkp, dvc, dvp, qw, kw, cos, sins, ones_q, ones_kv):
    s = proj.shape[0]
    rows = min(KV_ROWS, s)
    n_q = KV_COL
    q_of = lambda j: jnp.clip(j - 1, 0, n_q - 1)
    gate_of = lambda j: jnp.clip(j - 1 - n_q, 0, n_q - 1)

    def body(kv_ref, p_ref, dqr_ref, o_ref, dag_ref, dkc_ref, dkp_ref, dvc_ref, dvp_ref, qw_ref, kw_ref, cos_ref, sin_ref,
             oq_ref, ok_ref, dp_ref, dqw_ref, dkw_ref):
        j, i = pl.program_id(0), pl.program_id(1)

        @pl.when(j == 0)
        def _():
            @pl.when(i == 0)
            def _():
                dkw_ref[...] = jnp.zeros_like(dkw_ref)

            def assemble(cur_ref, prv_ref):
                tot = [cur_ref[h] + prv_ref[h] for h in range(KV_HEADS)]
                first = _half_mask(tot[0].shape, 0)
                return jnp.concatenate([jnp.where(first, tot[0], tot[1]), jnp.where(first, tot[2], tot[3])], axis=1)

            dkr = assemble(dkc_ref, dkp_ref)
            dv = assemble(dvc_ref, dvp_ref)
            cosf, sinf = _tile_lanes(cos_ref[...], 2), _tile_lanes(sin_ref[...], 2)
            nrm, rstd, _ = _k_prep(kv_ref[:, :KV_WIDTH].astype(F32), kw_ref[...], cosf, sinf, ok_ref[...])
            dy = dkr * cosf + _partner64(dkr * sinf)
            dn = dy * kw_ref[...]
            dk = rstd * (dn - nrm * _group_mean(dn * nrm, ok_ref[...]))
            dp_ref[...] = jnp.concatenate([dk, dv], axis=1).astype(BF16)
            dkw_ref[...] += jnp.sum(dy * nrm, axis=0, keepdims=True)

        @pl.when(jnp.logical_and(j >= 1, j <= n_q))
        def _():
            @pl.when(i == 0)
            def _():
                dqw_ref[...] = jnp.zeros_like(dqw_ref)

            cosf, sinf = _tile_lanes(cos_ref[...], 4), _tile_lanes(sin_ref[...], 4)
            nrm, rstd, _ = _q_prep(p_ref[...].astype(F32), qw_ref[...], cosf, sinf, oq_ref[...])
            dq = dqr_ref[...].astype(F32)
            dy = dq * cosf + _partner64(dq * sinf)
            dn = dy * qw_ref[...]
            dp_ref[...] = (rstd * (dn - nrm * _group_mean(dn * nrm, oq_ref[...]))).astype(BF16)
            dqw_ref[0] += jnp.sum(dy * nrm, axis=0, keepdims=True)

        @pl.when(j > n_q)
        def _():
            gate = p_ref[...].astype(F32)
            sg = _sigmoid(gate)
            dp_ref[...] = (dag_ref[...].astype(F32) * o_ref[...].astype(F32) * (sg * (1.0 + gate * (1.0 - sg)))).astype(BF16)

    first_pass = lambda j, i: jnp.where(j == 0, i, 0)
    acc = pl.BlockSpec((KV_HEADS, rows, SLAB), lambda j, i: (0, first_pass(j, i), 0))
    full = lambda shape: pl.BlockSpec(shape, lambda j, i: (0,) * len(shape))
    tab = pl.BlockSpec((rows, SLAB), lambda j, i: (i, 0))
    out_col = lambda j: jnp.where(j == 0, KV_COL, jnp.where(j <= n_q, j - 1, j))
    return pl.pallas_call(
        body, name="swa_bwd_finish", grid=(2 * n_q + 1, s // rows),
        in_specs=[pl.BlockSpec((rows, GROUP_WIDTH), lambda j, i: (first_pass(j, i), KV_COL)),
                  pl.BlockSpec((rows, GROUP_WIDTH), lambda j, i: (jnp.where(j == 0, 0, i), jnp.where(j <= n_q, q_of(j), j))),
                  pl.BlockSpec((rows, GROUP_WIDTH), lambda j, i: (jnp.where(jnp.logical_and(j >= 1, j <= n_q), i, 0), q_of(j))),
                  pl.BlockSpec((rows, GROUP_WIDTH), lambda j, i: (jnp.where(j > n_q, i, 0), gate_of(j))),
                  pl.BlockSpec((rows, GROUP_WIDTH), lambda j, i: (jnp.where(j > n_q, i, 0), gate_of(j))),
                  acc, acc, acc, acc, full((1, GROUP_WIDTH)), full((1, KV_WIDTH)), tab, tab,
                  full((GROUP_WIDTH, GROUP_WIDTH)), full((KV_WIDTH, KV_WIDTH))],
        out_specs=[pl.BlockSpec((rows, GROUP_WIDTH), lambda j, i: (i, out_col(j))),
                   pl.BlockSpec((1, 8, GROUP_WIDTH), lambda j, i: (q_of(j), 0, 0)), pl.BlockSpec((8, KV_WIDTH), lambda j, i: (0, 0))],
        out_shape=[jax.ShapeDtypeStruct((s, ODD_IN), BF16), jax.ShapeDtypeStruct((n_q, 8, GROUP_WIDTH), F32),
                   jax.ShapeDtypeStruct((8, KV_WIDTH), F32)],
        compiler_params=_params("arbitrary", "arbitrary"),
    )(proj, proj, dqr, o, dag, dkc, dkp, dvc, dvp, qw, kw, cos, sins, ones_q, ones_kv)


def _place():
    x, y, c = lax.axis_index("x"), lax.axis_index("y"), lax.axis_index("c")
    return x, y, c


OTHER_CHIPS = ((1, 0), (0, 1), (1, 1))


def _half_rows(ref, half, rows):
    return ref.at[pl.ds(pl.multiple_of(half * (rows // 2), 8), rows // 2)]


DMA_CHUNK_BYTES = 1 << 20
BF16_TILE_ROWS = 16


def _n_chunks(ref):
    rows = ref.shape[-2]
    nbytes = math.prod(ref.shape) * jnp.dtype(ref.dtype).itemsize
    n = 1
    while 2 * n * DMA_CHUNK_BYTES <= nbytes and rows % (2 * n * BF16_TILE_ROWS) == 0:
        n *= 2
    return n


def _row_chunk(ref, k, n):
    rows = ref.shape[-2] // n
    return ref.at[pl.ds(k * rows, rows)] if len(ref.shape) == 2 else ref.at[:, pl.ds(k * rows, rows)]


def _push(src, dst, send_sem, recv_sem, device_id):
    n = _n_chunks(src)
    for k in range(n):
        pltpu.make_async_remote_copy(src_ref=_row_chunk(src, k, n), dst_ref=_row_chunk(dst, k, n), send_sem=send_sem,
                                     recv_sem=recv_sem, device_id=device_id, device_id_type=MESH).start()
    return pltpu.make_async_remote_copy(src_ref=src, dst_ref=dst, send_sem=send_sem, recv_sem=recv_sem,
                                        device_id=device_id, device_id_type=MESH)


def _copy(src, dst, sem):
    n = _n_chunks(src)
    for k in range(n):
        pltpu.make_async_copy(_row_chunk(src, k, n), _row_chunk(dst, k, n), sem).start()
    return pltpu.make_async_copy(src, dst, sem)


HBM = pl.BlockSpec(memory_space=pltpu.HBM)
SEM = pl.BlockSpec(memory_space=pltpu.SEMAPHORE)
SPLIT_COPY_EFFECT = pltpu.SideEffectType.DATAFLOW_SIDE_EFFECTING


def _in_hbm(a):
    return pltpu.with_memory_space_constraint(a, pltpu.HBM)


def _start_copies(name, arrays, plan, n_copies, after=None):
    n = len(arrays)

    def body(*refs):
        send_sem, recv_sem = refs[-n - 3], refs[-n - 2]
        for k, (src, dst, peer) in enumerate(plan(refs[:n])):
            _push(src, dst, send_sem.at[k], recv_sem.at[k], peer)
        refs[-1][...] = jnp.zeros_like(refs[-1])

    dma = pltpu.SemaphoreType.DMA((n_copies,))
    outs = pl.pallas_call(
        body, name=name,
        out_shape=(dma, dma, *[pltpu.HBM(a.shape, a.dtype) for a in arrays], jax.ShapeDtypeStruct((8, 128), F32)),
        in_specs=[HBM] * n + ([ANY] if after is not None else []),
        out_specs=(SEM, SEM, *[HBM] * n, pl.BlockSpec(memory_space=pltpu.VMEM)),
        input_output_aliases={i: i + 2 for i in range(n)},
        compiler_params=pltpu.CompilerParams(has_side_effects=SPLIT_COPY_EFFECT),
    )(*[_in_hbm(a) for a in arrays], *((after,) if after is not None else ()))
    return outs[0], outs[1], list(outs[2:2 + n]), outs[-1]


def _wait_copies(name, send_sem, recv_sem, arrays, plan, after):
    n = len(arrays)

    def body(*refs):
        send_ref, recv_ref = refs[n], refs[n + 1]
        for k, (src, dst, peer) in enumerate(plan(refs[:n])):
            cp = pltpu.make_async_remote_copy(src_ref=src, dst_ref=dst, send_sem=send_ref.at[k], recv_sem=recv_ref.at[k],
                                              device_id=peer, device_id_type=MESH)
            cp.wait_send()
            cp.wait_recv()

    return list(pl.pallas_call(
        body, name=name, out_shape=tuple(pltpu.HBM(a.shape, a.dtype) for a in arrays),
        in_specs=[HBM] * n + [SEM, SEM, ANY], out_specs=tuple([HBM] * n),
        input_output_aliases={i: i for i in range(n)},
        compiler_params=pltpu.CompilerParams(has_side_effects=SPLIT_COPY_EFFECT),
    )(*arrays, send_sem, recv_sem, after))


def _gather_region(full, kind, chip, half=None):
    if kind == "whole":
        return full.at[chip]
    if kind == "col":
        rows, width = full.shape[0], full.shape[1] // N_CHIPS
        piece = full.at[:, pl.ds(pl.multiple_of(chip * width, 128), width)]
    else:
        rows = full.shape[0] // N_CHIPS
        piece = full.at[pl.ds(pl.multiple_of(chip * rows, BF16_TILE_ROWS), rows)]
    return piece if half is None else _half_rows(piece, half, rows)


def _gather_plan(kinds):
    def plan(fulls):
        x, y, c = _place()
        copies = []
        for fx, fy in OTHER_CHIPS:
            for full, kind in zip(fulls, kinds):
                mine = _gather_region(full, kind, 2 * x + y, c)
                copies.append((mine, mine, (x ^ fx, y ^ fy, c)))
        return copies

    return plan


def _gather_start(name, fulls, kinds, after=None):
    return _start_copies(name, list(fulls), _gather_plan(kinds), 3 * len(kinds), after)


def _gather_finish(name, started, kinds, after):
    send_sem, recv_sem, fulls, _ = started
    n = len(kinds)
    fulls = _wait_copies(name + "_wait", send_sem, recv_sem, fulls, _gather_plan(kinds), after)
    split = [i for i, kind in enumerate(kinds) if kind != "whole"]

    def body(*refs):
        out = refs[n:2 * n]
        send, recv = refs[2 * n:]
        x, y, c = _place()
        pushes = []
        for r, (fx, fy) in enumerate(OTHER_CHIPS):
            chip = 2 * (x ^ fx) + (y ^ fy)
            for j, i in enumerate(split):
                landed = _gather_region(out[i], kinds[i], chip, c)
                pushes.append(_push(landed, landed, send.at[r * len(split) + j], recv.at[r * len(split) + j], (x, y, 1 - c)))
        for cp in pushes:
            cp.wait_recv()
        for cp in pushes:
            cp.wait_send()

    dma = pltpu.SemaphoreType.DMA
    return pl.pallas_call(
        body, name=name + "_share", in_specs=[ANY] * n, out_specs=[ANY] * n,
        out_shape=[jax.ShapeDtypeStruct(a.shape, a.dtype) for a in fulls],
        input_output_aliases={i: i for i in range(n)},
        scratch_shapes=[dma((3 * len(split),)), dma((3 * len(split),))],
        compiler_params=pltpu.CompilerParams(has_side_effects=True),
    )(*fulls)


def _allreduce_small(v):
    def body(v_ref, out_ref, buf, send_sems, recv_sems):
        x, y, c = _place()
        me = 4 * x + 2 * y + c
        buf[me] = v_ref[...]
        copies = []
        for r in range(1, N_DEV):
            peer = (x ^ (r >> 2), y ^ ((r >> 1) & 1), c ^ (r & 1))
            cp = pltpu.make_async_remote_copy(src_ref=v_ref, dst_ref=buf.at[me], send_sem=send_sems.at[r - 1],
                                              recv_sem=recv_sems.at[r - 1], device_id=peer, device_id_type=MESH)
            cp.start()
            copies.append(cp)
        for cp in copies:
            cp.wait_recv()
        for cp in copies:
            cp.wait_send()
        total = buf[0]
        for d in range(1, N_DEV):
            total = total + buf[d]
        out_ref[...] = total

    vm = pl.BlockSpec(memory_space=pltpu.VMEM)
    return pl.pallas_call(
        body, name="allreduce_small", in_specs=[vm], out_specs=vm, out_shape=jax.ShapeDtypeStruct(v.shape, v.dtype),
        scratch_shapes=[pltpu.VMEM((N_DEV,) + v.shape, v.dtype), pltpu.SemaphoreType.DMA((N_DEV - 1,)),
                        pltpu.SemaphoreType.DMA((N_DEV - 1,))],
        compiler_params=pltpu.CompilerParams(has_side_effects=True),
    )(v)


def _exchange_halves(grads, name):
    n = len(grads)

    def body(*refs):
        g, theirs = refs[:n], refs[n:2 * n]
        send_sem, recv_sem = refs[2 * n:]
        x, y, c = _place()
        copies = []
        for i in range(n):
            half = g[i].shape[1] // 2
            src = g[i].at[:, pl.ds(pl.multiple_of((1 - c) * half, BF16_TILE_ROWS), half)]
            copies.append(_push(src, theirs[i], send_sem.at[i], recv_sem.at[i], (x, y, 1 - c)))
        for cp in copies:
            cp.wait_recv()
            cp.wait_send()

    dma = pltpu.SemaphoreType.DMA
    return pl.pallas_call(
        body, name=name, in_specs=[ANY] * n, out_specs=[ANY] * n,
        out_shape=[jax.ShapeDtypeStruct((a.shape[0], a.shape[1] // 2, a.shape[2]), a.dtype) for a in grads],
        scratch_shapes=[dma((n,)), dma((n,))],
        compiler_params=pltpu.CompilerParams(has_side_effects=True),
    )(*grads)


def _pair_sum(g, theirs, core, name):
    pieces, half, cols = theirs.shape
    tr = min(half, 256)
    per = half // tr

    def body(core_ref, g_ref, t_ref, o_ref):
        del core_ref
        o_ref[...] = (g_ref[...].astype(F32) + t_ref[...].astype(F32)).astype(BF16)

    spec = pl.BlockSpec((1, tr, cols), lambda p, i, core_ref: (p, i, 0))
    return pl.pallas_call(
        body, name=name, out_shape=jax.ShapeDtypeStruct(theirs.shape, BF16),
        grid_spec=pltpu.PrefetchScalarGridSpec(
            num_scalar_prefetch=1, grid=(pieces, per),
            in_specs=[pl.BlockSpec((1, tr, cols), lambda p, i, core_ref: (p, core_ref[0] * per + i, 0)), spec],
            out_specs=spec),
        compiler_params=_params("parallel", "parallel"))(core, g, theirs)


def _scatter_plan(n):
    def plan(refs):
        parts, stacks = refs[:n], refs[n:]
        x, y, c = _place()
        copies = []
        for fx, fy in OTHER_CHIPS:
            chip = 2 * (x ^ fx) + (y ^ fy)
            for part, stack in zip(parts, stacks):
                if part.shape[0] == N_CHIPS:
                    piece = part.at[chip]
                else:
                    width = part.shape[2] // N_CHIPS
                    piece = part.at[0].at[:, pl.ds(pl.multiple_of(chip * width, 128), width)]
                copies.append((piece, stack.at[2 * x + y], (x ^ fx, y ^ fy, c)))
        return copies

    return plan


def _scatter_start(name, parts, after=None):
    def landing(a):
        return (N_CHIPS, a.shape[1], a.shape[2] if a.shape[0] == N_CHIPS else a.shape[2] // N_CHIPS)

    stacks = [lax.empty(landing(a), a.dtype) for a in parts]
    return _start_copies(name, list(parts) + stacks, _scatter_plan(len(parts)), 3 * len(parts), after)


def _scatter_finish(name, started, after):
    send_sem, recv_sem, arrays, _ = started
    n = len(arrays) // 2
    arrays = _wait_copies(name + "_wait", send_sem, recv_sem, arrays, _scatter_plan(n), after)
    return arrays[:n], arrays[n:]


def _sum_chips(part, stack, place, name):
    _, r, c = stack.shape
    tr = 256
    per = r // tr

    def body(place_ref, own_ref, a_ref, b_ref, c_ref, o_ref):
        del place_ref
        total = own_ref[0].astype(F32)
        for ref in (a_ref, b_ref, c_ref):
            total = total + ref[0].astype(F32)
        o_ref[...] = total

    if part.shape[0] == N_CHIPS:
        own = pl.BlockSpec((1, tr, c), lambda i, pr: (pr[1], i, 0))
    else:
        own = pl.BlockSpec((1, tr, c), lambda i, pr: (0, i, pr[1]))
    other = lambda flip: pl.BlockSpec((1, tr, c), lambda i, pr: (pr[1] ^ flip, i, 0))
    return pl.pallas_call(
        body, name=name, out_shape=jax.ShapeDtypeStruct((2 * r, c), F32),
        grid_spec=pltpu.PrefetchScalarGridSpec(
            num_scalar_prefetch=1, grid=(per,), in_specs=[own, other(2), other(1), other(3)],
            out_specs=pl.BlockSpec((tr, c), lambda i, pr: (pr[0] * per + i, 0))),
        compiler_params=_params("parallel"))(place, part, stack, stack, stack)


def _share_halves(fulls):
    n = len(fulls)

    def body(*refs):
        dst = refs[n:2 * n]
        send_sem, recv_sem = refs[2 * n:]
        x, y, c = _place()
        copies = []
        for i in range(n):
            mine = _half_rows(dst[i], c, dst[i].shape[0])
            copies.append(_push(mine, mine, send_sem.at[i], recv_sem.at[i], (x, y, 1 - c)))
        for cp in copies:
            cp.wait_recv()
            cp.wait_send()

    dma = pltpu.SemaphoreType.DMA
    return pl.pallas_call(
        body, name="share_halves", in_specs=[ANY] * n, out_specs=[ANY] * n,
        out_shape=[jax.ShapeDtypeStruct(a.shape, a.dtype) for a in fulls],
        input_output_aliases={i: i for i in range(n)}, scratch_shapes=[dma((n,)), dma((n,))],
        compiler_params=pltpu.CompilerParams(has_side_effects=True),
    )(*fulls)


MM = dict(tm=1024, tn=1024, tk=2048)


def _local_step(x, target, ev_norm_w, q_norm_w, k_norm_w, sinks, own_first, weights_first, weights_late, emit):
    s = x.shape[0]
    cos_r, sin_r = _rope_tables(s, RET_DIM)
    cos_a, sin_a = _rope_tables(s, HEAD_DIM)
    cos_a = jnp.tile(cos_a, (1, 4))
    sins_a = jnp.tile(jnp.concatenate([-sin_a, sin_a], axis=1), (1, 2))
    tables = _retention_tables()
    ones_q, ones_kv = _block_diag_mean(GROUP_WIDTH), _block_diag_mean(KV_WIDTH)
    qw_g = jnp.tile(q_norm_w, (1, GROUP_WIDTH // HEAD_DIM))
    kw_kv = jnp.tile(k_norm_w, (1, KV_WIDTH // HEAD_DIM))
    sinks1 = sinks.reshape(Q_HEADS)

    own_w_in0, own_block, start_token = own_first
    h0 = _rmsnorm(x, ev_norm_w, "norm0", after=start_token)
    shifted = dict(shift=own_block * (own_w_in0.shape[1] // MM["tn"]), total=EVEN_IN // MM["tn"], tm=MM["tm"], tn=MM["tn"],
                   out_dtype=BF16)
    own_blocks = own_w_in0.shape[1] // MM["tn"]
    proj0 = _mm_shifted(h0, own_w_in0, b_shifted=False, first=0, count=own_blocks, name="proj0_own", **shifted)
    w_in0, conv_w, od_norm_w, token = weights_first(proj0)
    proj0 = _mm_shifted(h0, w_in0, b_shifted=True, first=own_blocks, count=shifted["total"] - own_blocks, name="proj0_rest",
                        into=proj0, after=token, **shifted)
    o_ret, cat, states = _retention_fwd(proj0, cos_r, sin_r, tables)
    cat = _conv_fwd(proj0, conv_w, cat)
    w_out0, w_in1, w_out1 = weights_late(cat)

    def residual_and_norm(prod, x_ref, w_ref, x1_ref, h1_ref):
        x1v = x_ref[...] + prod
        x1_ref[...] = x1v
        rstd = lax.rsqrt(jnp.mean(x1v * x1v, axis=-1, keepdims=True) + EPS)
        h1_ref[...] = (x1v * rstd * w_ref[...]).astype(BF16)

    def residual_and_loss(prod, x1_ref, t_ref, dy_ref, dyb_ref, sq_ref):
        diff = (x1_ref[...] + prod) - t_ref[...]
        dyv = diff * (1.0 / D_MODEL)
        dy_ref[...] = dyv
        dyb_ref[...] = dyv.astype(BF16)

        @pl.when(pl.program_id(0) == 0)
        def _():
            sq_ref[...] = jnp.zeros_like(sq_ref)

        sq_ref[...] += jnp.sum(jnp.sum(diff * diff, axis=1, keepdims=True), axis=0, keepdims=True)

    act = lambda dt: jax.ShapeDtypeStruct((s, D_MODEL), dt)
    x1, h1 = _mm_rows(cat, w_out0, [x], [od_norm_w], [act(F32), act(BF16)], residual_and_norm, tm=min(s, 512), name="out0")
    proj1 = _mm(h1, w_in1, mode="nn", out_dtype=BF16, name="proj1", tm=1024, tn=1536, tk=2048)
    qk = _qk_prep(proj1, qw_g, kw_kv, cos_a, sins_a, ones_q, ones_kv)
    ag, o_att = _swa_fwd(qk, proj1, sinks1)
    dy, dy_b, sq = _mm_rows(ag, w_out1, [x1, target], [], [act(F32), act(BF16), jax.ShapeDtypeStruct((8, 128), F32)],
                            residual_and_loss, tm=min(s, 256), name="out1")

    g_w_out1 = _mm(ag, dy_b, mode="tn", out_dtype=BF16, name="g_w_out1", **MM)
    dag = _mm(dy_b, w_out1, mode="nt", out_dtype=BF16, name="d_ag", **MM)
    dqr, dkc, dkp, dvc, dvp, dsink = _swa_bwd(qk, proj1, dag, sinks1)
    dproj1, dqw, dkw = _swa_bwd_finish(proj1, dqr, o_att, dag, dkc, dkp, dvc, dvp, qw_g, kw_kv, cos_a, sins_a, ones_q, ones_kv)
    g_w_in1 = _mm(h1, dproj1, mode="tn", out_dtype=BF16, name="g_w_in1", tm=1024, tn=1536, tk=2048)
    token = emit("layer1", (("od_w_in", g_w_in1, "col"), ("od_w_out", g_w_out1, "row")))
    dh1 = _mm(dproj1, w_in1, mode="nt", out_dtype=BF16, name="d_h1", tm=1024, tn=1024, tk=2304, after=token)
    dx1, dx1_b, g_norm1 = _rmsnorm_bwd(x1, od_norm_w, dh1, dy, "norm1_bwd", True)

    g_w_out0 = _mm(cat, dx1_b, mode="tn", out_dtype=BF16, name="g_w_out0", **MM)
    token = emit("out0", (("ev_w_out", g_w_out0, "row"),))
    dcat = _mm(dx1_b, w_out0, mode="nt", out_dtype=BF16, name="d_cat", after=token, **MM)
    dproj0 = _retention_bwd(proj0, o_ret, states, dcat, cos_r, sin_r, tables)
    dproj0, g_conv = _conv_bwd(proj0, dcat, conv_w, dproj0)
    g_w_in0 = _mm(h0, dproj0, mode="tn", out_dtype=BF16, name="g_w_in0", **MM)
    token = emit("in0", (("ev_w_in", g_w_in0, "col"),))
    dh0 = _mm(dproj0, w_in0, mode="nt", out_dtype=BF16, name="d_h0", after=token, **MM)
    grad_x, g_norm0 = _rmsnorm_bwd(x, ev_norm_w, dh0, dx1, "norm0_bwd", False)

    g_qw = dqw[:, 0, :].reshape(Q_HEADS, HEAD_DIM).sum(axis=0)
    g_kw = dkw[0].reshape(KV_HEADS, HEAD_DIM).sum(axis=0)
    g_sinks = dsink[:, :, 0].reshape(Q_HEADS)
    small = dict(ev_norm=g_norm0[0], od_norm=g_norm1[0], conv=g_conv[:3], qw=g_qw, kw=g_kw, sinks=g_sinks)
    return sq[0, 0], grad_x, small


def _pack_small_grads(small):
    pad = lambda v: jnp.pad(v, (0, D_MODEL - v.shape[0]))
    tail = pad(jnp.concatenate([small["qw"], small["kw"], small["sinks"]]))
    rows = [small["ev_norm"], small["od_norm"]] + [pad(small["conv"][t]) for t in range(3)] + [tail]
    rows += [jnp.zeros((D_MODEL,), F32)] * (8 - len(rows))
    return jnp.stack(rows)


class _ReduceScatter:
    def __init__(self, place):
        self.place = place
        self.started = []

    def send(self, tag, grads):
        pieces = [g[None] if kind == "col" else g.reshape(N_CHIPS, g.shape[0] // N_CHIPS, g.shape[1]) for _, g, kind in grads]
        theirs = _exchange_halves(pieces, "exchange_halves_" + tag)
        parts = [_pair_sum(g, t, self.place[:1], "pair_sum_" + nm) for g, t, (nm, _, _) in zip(pieces, theirs, grads)]
        started = _scatter_start("scatter_" + tag, parts)
        self.started.append((tag, [nm for nm, _, _ in grads], started))
        return started[3]

    def finish(self, after):
        names, sums = [], []
        for tag, group, started in self.started:
            parts, stacks = _scatter_finish("scatter_" + tag, started, after)
            sums += [_sum_chips(p, s, self.place, "chip_sum_" + nm) for p, s, nm in zip(parts, stacks, group)]
            names += group
        return dict(zip(names, _share_halves(sums)))


def kernel(x, ev_norm_w, ev_w_in, ev_conv_w, ev_w_out, od_norm_w, od_w_in, od_q_norm_w, od_k_norm_w, od_sinks, od_w_out, loss_target, m_ev_norm_w, m_ev_w_in, m_ev_conv_w, m_ev_w_out, m_od_norm_w, m_od_w_in, m_od_q_norm_w, m_od_k_norm_w, m_od_sinks, m_od_w_out, v_ev_norm_w, v_ev_w_in, v_ev_conv_w, v_ev_w_out, v_od_norm_w, v_od_w_in, v_od_q_norm_w, v_od_k_norm_w, v_od_sinks, v_od_w_out):
    my_chip = 2 * lax.axis_index("x") + lax.axis_index("y")
    place = jnp.stack([lax.axis_index("c"), my_chip]).astype(jnp.int32)
    shard_w = D_MODEL // N_CHIPS
    conv_shard = CONV_WIDTH // N_CHIPS

    small_in = jnp.zeros((8, shard_w), F32)
    small_in = small_in.at[0].set(od_norm_w[0]).at[1:4, :conv_shard].set(ev_conv_w[0])
    small_in = lax.dynamic_update_slice(jnp.zeros((N_CHIPS, 8, shard_w), F32), small_in[None], (my_chip, 0, 0))
    chip = place[1:]
    first_kinds, late_kinds = ("col", "whole"), ("row", "col", "row")
    w_in0_own_place, w_in0_shard = _cast_into_gathered(ev_w_in[0], "col", chip, "cast_w_in0", keep_shard=True)
    first = _gather_start("gather_first", [w_in0_own_place, small_in], first_kinds)
    late_own = [_cast_into_gathered(ev_w_out[0], "row", chip, "cast_w_out0"),
                _cast_into_gathered(od_w_in[0], "col", chip, "cast_w_in1"),
                _cast_into_gathered(od_w_out[0], "row", chip, "cast_w_out1")]
    late = []

    def weights_first(after):
        w_in0, small_all = _gather_finish("gather_first", first, first_kinds, after)
        late.append(_gather_start("gather_late", late_own, late_kinds, after=w_in0))
        od_norm_full = small_all[:, 0, :].reshape(1, D_MODEL)
        conv_full = jnp.transpose(small_all[:, 1:4, :conv_shard], (1, 0, 2)).reshape(3, CONV_WIDTH)
        return w_in0, conv_full, od_norm_full, late[0][3]

    def weights_late(after):
        return _gather_finish("gather_late", late[0], late_kinds, after)

    reduce_scatter = _ReduceScatter(place)
    sq, grad_x, small = _local_step(x[0], loss_target[0], ev_norm_w, od_q_norm_w, od_k_norm_w, od_sinks,
                                    (w_in0_shard, chip, first[3]), weights_first, weights_late, reduce_scatter.send)
    loss = lax.psum(0.5 * sq / D_MODEL, ("x", "y", "c"))

    big = reduce_scatter.finish(grad_x)
    g_ev_w_in, g_ev_w_out, g_od_w_in, g_od_w_out = big["ev_w_in"], big["ev_w_out"], big["od_w_in"], big["od_w_out"]
    tot = _allreduce_small(_pack_small_grads(small))
    g_ev_norm = tot[0:1]
    g_od_norm = lax.dynamic_slice(tot, (1, my_chip * shard_w), (1, shard_w))
    g_conv = lax.dynamic_slice(tot, (2, my_chip * conv_shard), (3, conv_shard))
    g_qw, g_kw, g_sinks = tot[5:6, 0:HEAD_DIM], tot[5:6, HEAD_DIM:2 * HEAD_DIM], tot[5:6, 2 * HEAD_DIM:2 * HEAD_DIM + Q_HEADS]

    upd = {}
    upd["ev_w_in"] = _adamw(ev_w_in[0], g_ev_w_in, m_ev_w_in[0], v_ev_w_in[0], "adamw_ev_w_in")
    upd["ev_w_out"] = _adamw(ev_w_out[0], g_ev_w_out, m_ev_w_out[0], v_ev_w_out[0], "adamw_ev_w_out")
    upd["od_w_in"] = _adamw(od_w_in[0], g_od_w_in, m_od_w_in[0], v_od_w_in[0], "adamw_od_w_in")
    upd["od_w_out"] = _adamw(od_w_out[0], g_od_w_out, m_od_w_out[0], v_od_w_out[0], "adamw_od_w_out")
    smalls = (("ev_norm_w", ev_norm_w, g_ev_norm, m_ev_norm_w, v_ev_norm_w),
              ("ev_conv_w", ev_conv_w, g_conv, m_ev_conv_w, v_ev_conv_w),
              ("od_norm_w", od_norm_w, g_od_norm, m_od_norm_w, v_od_norm_w),
              ("od_q_norm_w", od_q_norm_w, g_qw, m_od_q_norm_w, v_od_q_norm_w),
              ("od_k_norm_w", od_k_norm_w, g_kw, m_od_k_norm_w, v_od_k_norm_w),
              ("od_sinks", od_sinks, g_sinks, m_od_sinks, v_od_sinks))
    sizes = [w.size for _, w, _, _, _ in smalls]
    padded = 8 * 128 * math.ceil(sum(sizes) / (8 * 128))
    pack = lambda arrs, fill: jnp.concatenate(
        [a.reshape(-1) for a in arrs] + [jnp.full((padded - sum(sizes),), fill, F32)]).reshape(8, padded // 8)
    packed = _adamw(pack([w for _, w, _, _, _ in smalls], 0.0), pack([g for _, _, g, _, _ in smalls], 0.0),
                    pack([m for _, _, _, m, _ in smalls], 0.0), pack([v for _, _, _, _, v in smalls], 1.0), "adamw_small")
    offs = [sum(sizes[:i]) for i in range(len(sizes))]
    grads = {"ev_w_in": g_ev_w_in[None], "ev_w_out": g_ev_w_out[None], "od_w_in": g_od_w_in[None], "od_w_out": g_od_w_out[None]}
    for (nm, w, g, _, _), off, size in zip(smalls, offs, sizes):
        upd[nm] = tuple(p.reshape(-1)[off:off + size].reshape(w.shape) for p in packed)
        grads[nm] = g.reshape(w.shape)
    for nm in ("ev_w_in", "ev_w_out", "od_w_in", "od_w_out"):
        upd[nm] = tuple(u[None] for u in upd[nm])
    order = ("ev_norm_w", "ev_w_in", "ev_conv_w", "ev_w_out", "od_norm_w", "od_w_in", "od_q_norm_w", "od_k_norm_w", "od_sinks", "od_w_out")
    return (loss, grad_x[None], *[grads[nm] for nm in order], *[upd[nm][0] for nm in order],
            *[upd[nm][1] for nm in order], *[upd[nm][2] for nm in order])
```

```python
import functools
import math

import jax
import jax.numpy as jnp
from jax import lax
from jax.experimental import pallas as pl
from jax.experimental.pallas import tpu as pltpu

F32 = jnp.float32
BF16 = jnp.bfloat16

D_MODEL = 2048
RET_HEADS = 4
RET_DIM = 256
RET_WIDTH = 1024
CONV_WIDTH = 1024
EVEN_IN = 8192
Q_HEADS = 32
HEAD_DIM = 64
KV_HEADS = 4
KV_WIDTH = 256
ATTN_WIDTH = 2048
ODD_IN = 4608
BLK = 128
ROPE_THETA = 10000.0
EPS = 1e-6
ADAM_LR = 0.001
ADAM_B1 = 0.9
ADAM_B2 = 0.999
ADAM_EPS = 1e-08
ADAM_WD = 0.01
ADAM_STEP = 10
N_CHIPS = 4
N_DEV = 8
VMEM_LIMIT_BYTES = 48 * 1024 * 1024
MESH = pl.DeviceIdType.MESH
ANY = pl.BlockSpec(memory_space=pl.ANY)


def _params(*sem):
    return pltpu.CompilerParams(dimension_semantics=sem, vmem_limit_bytes=VMEM_LIMIT_BYTES)


def _dot(a, b):
    return jnp.dot(a, b, preferred_element_type=F32)


def _dot_nt(a, b):
    return lax.dot_general(a, b, (((1,), (1,)), ((), ())), preferred_element_type=F32)


def _dot_tn(a, b):
    return lax.dot_general(a, b, (((0,), (0,)), ((), ())), preferred_element_type=F32)


def _sigmoid(x):
    return 1.0 / (1.0 + jnp.exp(-x))


def _mm(a, b, *, mode, tm, tn, tk, out_dtype, name, add=None, after=None):
    if mode == "nn":
        (m, k), n = a.shape, b.shape[1]
    elif mode == "nt":
        (m, k), n = a.shape, b.shape[0]
    else:
        (k, m), n = a.shape, b.shape[1]
    tm, tn, tk = min(tm, m), min(tn, n), min(tk, k)
    assert m % tm == 0 and n % tn == 0 and k % tk == 0, (name, m, n, k)
    nk = k // tk
    dot = {"nn": _dot, "nt": _dot_nt, "tn": _dot_tn}[mode]
    a_spec = (pl.BlockSpec((tk, tm), lambda i, j, kk: (kk, i)) if mode == "tn"
              else pl.BlockSpec((tm, tk), lambda i, j, kk: (i, kk)))
    b_spec = (pl.BlockSpec((tn, tk), lambda i, j, kk: (j, kk)) if mode == "nt"
              else pl.BlockSpec((tk, tn), lambda i, j, kk: (kk, j)))
    o_spec = pl.BlockSpec((tm, tn), lambda i, j, kk: (i, j))
    has_add = add is not None

    def body(*refs):
        a_ref, b_ref = refs[0], refs[1]
        add_ref = refs[2] if has_add else None
        o_ref, acc_ref = refs[-2], refs[-1]
        p = dot(a_ref[...], b_ref[...])

        def finish(total):
            if has_add:
                total = total + add_ref[...].astype(F32)
            o_ref[...] = total.astype(out_dtype)

        if nk == 1:
            finish(p)
        else:
            kk = pl.program_id(2)

            @pl.when(kk == 0)
            def _():
                acc_ref[...] = p

            @pl.when(jnp.logical_and(kk > 0, kk < nk - 1))
            def _():
                acc_ref[...] += p

            @pl.when(kk == nk - 1)
            def _():
                finish(acc_ref[...] + p)

    in_specs = [a_spec, b_spec] + ([o_spec] if has_add else []) + ([ANY] if after is not None else [])
    args = (a, b) + ((add,) if has_add else ()) + ((after,) if after is not None else ())
    return pl.pallas_call(
        body, name=name, grid=(m // tm, n // tn, nk), in_specs=in_specs, out_specs=o_spec,
        out_shape=jax.ShapeDtypeStruct((m, n), out_dtype),
        scratch_shapes=[pltpu.VMEM((tm, tn) if nk > 1 else (8, 128), F32)],
        compiler_params=_params("parallel", "parallel", "arbitrary"),
    )(*args)


def _mm_shifted(a, b, shift, *, b_shifted, first, count, total, tm, tn, out_dtype, name, into=None, after=None):
    m, k = a.shape
    tm = min(tm, m)
    assert m % tm == 0
    col = lambda j, shift_ref: (shift_ref[0] + first + j) % total
    extra = [arr for arr in (into, after) if arr is not None]

    def body(shift_ref, a_ref, b_ref, *rest):
        del shift_ref
        rest[-1][...] = _dot(a_ref[...], b_ref[...]).astype(out_dtype)

    return pl.pallas_call(
        body, name=name, out_shape=jax.ShapeDtypeStruct((m, total * tn), out_dtype),
        grid_spec=pltpu.PrefetchScalarGridSpec(
            num_scalar_prefetch=1, grid=(m // tm, count),
            in_specs=[pl.BlockSpec((tm, k), lambda i, j, s: (i, 0)),
                      pl.BlockSpec((k, tn), (lambda i, j, s: (0, col(j, s))) if b_shifted else (lambda i, j, s: (0, j)))]
            + [ANY] * len(extra),
            out_specs=pl.BlockSpec((tm, tn), lambda i, j, s: (i, col(j, s)))),
        input_output_aliases={3: 0} if into is not None else {},
        compiler_params=_params("parallel", "arbitrary"))(shift, a, b, *extra)


def _mm_rows(a, b, rows_in, vecs_in, out_shapes, epilogue, *, tm, name):
    m, k = a.shape
    n = b.shape[1]
    assert m % tm == 0
    row = pl.BlockSpec((tm, n), lambda i: (i, 0))

    def body(a_ref, b_ref, *rest):
        epilogue(_dot(a_ref[...], b_ref[...]), *rest)

    def out_spec(shape):
        if shape == (m, n):
            return row
        if shape == (n, m):
            return pl.BlockSpec((n, tm), lambda i: (0, i))
        return pl.BlockSpec(shape, lambda i: (0, 0))

    out_specs = [out_spec(tuple(s.shape)) for s in out_shapes]
    return pl.pallas_call(
        body, name=name, grid=(m // tm,),
        in_specs=[pl.BlockSpec((tm, k), lambda i: (i, 0)), pl.BlockSpec((k, n), lambda i: (0, 0))] + [row] * len(rows_in)
        + [pl.BlockSpec((1, n), lambda i: (0, 0))] * len(vecs_in),
        out_specs=out_specs, out_shape=out_shapes, compiler_params=_params("arbitrary"),
    )(a, b, *rows_in, *vecs_in)


def _cast_into_gathered(w, kind, chip, name, keep_shard=False):
    r, c = w.shape
    tr = min(r, 512)
    per = r // tr

    def body(chip_ref, w_ref, *outs):
        del chip_ref
        for o_ref in outs:
            o_ref[...] = w_ref[...].astype(BF16)

    if kind == "col":
        shape, out_map = (r, N_CHIPS * c), (lambda i, chip_ref: (i, chip_ref[0]))
    else:
        shape, out_map = (N_CHIPS * r, c), (lambda i, chip_ref: (chip_ref[0] * per + i, 0))
    plain = pl.BlockSpec((tr, c), lambda i, chip_ref: (i, 0))
    out = pl.pallas_call(
        body, name=name,
        out_shape=[jax.ShapeDtypeStruct(shape, BF16)] + ([jax.ShapeDtypeStruct((r, c), BF16)] if keep_shard else []),
        grid_spec=pltpu.PrefetchScalarGridSpec(
            num_scalar_prefetch=1, grid=(per,), in_specs=[plain],
            out_specs=[pl.BlockSpec((tr, c), out_map)] + ([plain] if keep_shard else [])),
        compiler_params=_params("parallel"))(chip, w)
    return out if keep_shard else out[0]


def _rmsnorm(x, w, name, after=None):
    s, d = x.shape
    tr = 256

    def body(x_ref, w_ref, *rest):
        xv = x_ref[...]
        rstd = lax.rsqrt(jnp.mean(xv * xv, axis=-1, keepdims=True) + EPS)
        h = (xv * rstd * w_ref[...]).astype(BF16)
        rest[-2][...] = h
        rest[-1][...] = h.T

    return pl.pallas_call(
        body, name=name, grid=(s // tr,),
        in_specs=[pl.BlockSpec((tr, d), lambda i: (i, 0)), pl.BlockSpec((1, d), lambda i: (0, 0))]
        + ([ANY] if after is not None else []),
        out_specs=[pl.BlockSpec((tr, d), lambda i: (i, 0)), pl.BlockSpec((d, tr), lambda i: (0, i))],
        out_shape=[jax.ShapeDtypeStruct((s, d), BF16), jax.ShapeDtypeStruct((d, s), BF16)],
        compiler_params=_params("parallel"),
    )(x, w, *((after,) if after is not None else ()))


def _rmsnorm_bwd(x, w, dh, dres, name, out_dtype):
    s, d = x.shape
    tr = 256

    def body(x_ref, w_ref, dh_ref, dres_ref, dx_ref, dw_ref):
        xv = x_ref[...]
        rstd = lax.rsqrt(jnp.mean(xv * xv, axis=-1, keepdims=True) + EPS)
        nrm = xv * rstd
        dhv = dh_ref[...].astype(F32)
        dn = dhv * w_ref[...]
        dx = dres_ref[...].astype(F32) + rstd * (dn - nrm * jnp.mean(dn * nrm, axis=-1, keepdims=True))
        dx_ref[...] = dx.astype(out_dtype)

        @pl.when(pl.program_id(0) == 0)
        def _():
            dw_ref[...] = jnp.zeros_like(dw_ref)

        dw_ref[...] += jnp.sum(dhv * nrm, axis=0, keepdims=True)

    row = pl.BlockSpec((tr, d), lambda i: (i, 0))
    return pl.pallas_call(
        body, name=name, grid=(s // tr,),
        in_specs=[row, pl.BlockSpec((1, d), lambda i: (0, 0)), row, row],
        out_specs=[row, pl.BlockSpec((8, d), lambda i: (0, 0))],
        out_shape=[jax.ShapeDtypeStruct((s, d), out_dtype), jax.ShapeDtypeStruct((8, d), F32)],
        compiler_params=_params("arbitrary"))(x, w, dh, dres)


def _adamw(w, g, m, v, name):
    r, c = w.shape
    tr = min(r, 256)
    assert r % tr == 0

    def body(w_ref, g_ref, m_ref, v_ref, g_out, d_ref, nm_ref, nv_ref):
        gv = g_ref[...]
        g_out[...] = gv
        nm = ADAM_B1 * m_ref[...] + (1.0 - ADAM_B1) * gv
        nv = ADAM_B2 * v_ref[...] + (1.0 - ADAM_B2) * (gv * gv)
        m_hat = nm / (1.0 - ADAM_B1 ** ADAM_STEP)
        v_hat = nv / (1.0 - ADAM_B2 ** ADAM_STEP)
        d_ref[...] = -ADAM_LR * (m_hat / (jnp.sqrt(v_hat) + ADAM_EPS) + ADAM_WD * w_ref[...])
        nm_ref[...] = nm
        nv_ref[...] = nv

    spec = pl.BlockSpec((tr, c), lambda i: (i, 0))
    shp = jax.ShapeDtypeStruct((r, c), F32)
    return pl.pallas_call(body, name=name, grid=(r // tr,), in_specs=[spec] * 4, out_specs=[spec] * 4,
                          out_shape=[shp] * 4, compiler_params=_params("parallel"))(w, g, m, v)


def _rope_tables(s, dim):
    inv = 1.0 / (ROPE_THETA ** (jnp.arange(0, dim, 2, dtype=F32) / dim))
    ang = jnp.arange(s).astype(F32)[:, None] * inv[None, :]
    return jnp.cos(ang), jnp.sin(ang)


def _rope_half(x, cos, sin):
    h = x.shape[1] // 2
    x1, x2 = x[:, :h], x[:, h:]
    return jnp.concatenate([x1 * cos - x2 * sin, x2 * cos + x1 * sin], axis=1)


def _unrope_half(dy, cos, sin):
    h = dy.shape[1] // 2
    d1, d2 = dy[:, :h], dy[:, h:]
    return jnp.concatenate([d1 * cos + d2 * sin, d2 * cos - d1 * sin], axis=1)


def _lane(shape):
    return lax.broadcasted_iota(jnp.int32, shape, 1)


def _partner64(x):
    w = x.shape[1]
    first = (_lane(x.shape) % HEAD_DIM) < (HEAD_DIM // 2)
    return jnp.where(first, pltpu.roll(x, w - HEAD_DIM // 2, 1), pltpu.roll(x, HEAD_DIM // 2, 1))


def _tile_lanes(t, reps):
    return t if reps == 1 else jnp.concatenate([t] * reps, axis=1)


def _group_mean(x, ones_bd):
    hi = x.astype(BF16)
    lo = (x - hi.astype(F32)).astype(BF16)
    return _dot(hi, ones_bd) + _dot(lo, ones_bd)


def _block_diag_mean(width):
    idx = jnp.arange(width) // HEAD_DIM
    return jnp.where(idx[:, None] == idx[None, :], 1.0 / HEAD_DIM, 0.0).astype(BF16)


def _retention_tables():
    h = RET_HEADS
    log_g = jnp.log(1.0 - 2.0 ** (-5.0 - jnp.arange(h, dtype=F32)))
    idx = jnp.arange(BLK, dtype=F32)
    diff = idx[:, None] - idx[None, :]
    intra = jnp.where(diff >= 0, jnp.exp(log_g[:, None, None] * jnp.maximum(diff, 0.0)), 0.0).astype(F32)
    q_dec = jnp.exp(log_g[:, None] * (idx[None, :] + 1.0)).astype(F32)[:, :, None]
    k_dec = jnp.exp(log_g[:, None] * (BLK - 1.0 - idx[None, :])).astype(F32)[:, :, None]
    chunk_dec = jnp.exp(log_g * BLK).astype(F32)[:, None, None]
    return intra, q_dec, k_dec, chunk_dec


def _retention_fwd(proj, cos, sin, tables):
    s = proj.shape[0]
    nc = s // BLK
    intra, q_dec, k_dec, chunk_dec = tables

    def body(p_ref, cos_ref, sin_ref, in_ref, qd_ref, kd_ref, cd_ref, o_ref, cat_ref, st_ref, state):
        @pl.when(pl.program_id(0) == 0)
        def _():
            state[...] = jnp.zeros_like(state)

        cosv, sinv = cos_ref[...], sin_ref[...]
        for h in range(RET_HEADS):
            c0 = h * RET_DIM
            q = p_ref[:, c0:c0 + RET_DIM].astype(F32)
            k = p_ref[:, RET_WIDTH + c0:RET_WIDTH + c0 + RET_DIM].astype(F32)
            v = p_ref[:, 2 * RET_WIDTH + c0:2 * RET_WIDTH + c0 + RET_DIM]
            g = p_ref[:, 3 * RET_WIDTH + c0:3 * RET_WIDTH + c0 + RET_DIM].astype(F32)
            qb = _rope_half(q, cosv, sinv).astype(BF16)
            kr = _rope_half(k, cosv, sinv) * (RET_DIM ** -0.5)
            kb = kr.astype(BF16)
            scores = _dot_nt(qb, kb) * in_ref[h]
            inner = _dot(scores.astype(BF16), v)
            prev = state[h]
            prev_b = prev.astype(BF16)
            st_ref[h, 0] = prev_b
            o = inner + _dot(qb, prev_b) * qd_ref[h]
            o_ref[:, c0:c0 + RET_DIM] = o
            rstd = lax.rsqrt(jnp.mean(o * o, axis=-1, keepdims=True) + EPS)
            cat_ref[:, c0:c0 + RET_DIM] = (o * rstd * (g * _sigmoid(g))).astype(BF16)
            state[h] = cd_ref[h] * prev + _dot_tn((kr * kd_ref[h]).astype(BF16), v)

    full = lambda shape: pl.BlockSpec(shape, lambda n: (0,) * len(shape))
    return pl.pallas_call(
        body, name="retention_fwd", grid=(nc,),
        in_specs=[pl.BlockSpec((BLK, 4 * RET_WIDTH), lambda n: (n, 0)),
                  pl.BlockSpec((BLK, RET_DIM // 2), lambda n: (n, 0)), pl.BlockSpec((BLK, RET_DIM // 2), lambda n: (n, 0)),
                  full((RET_HEADS, BLK, BLK)), full((RET_HEADS, BLK, 1)), full((RET_HEADS, BLK, 1)), full((RET_HEADS, 1, 1))],
        out_specs=[pl.BlockSpec((BLK, RET_WIDTH), lambda n: (n, 0)), pl.BlockSpec((BLK, RET_WIDTH), lambda n: (n, 0)),
                   pl.BlockSpec((RET_HEADS, 1, RET_DIM, RET_DIM), lambda n: (0, n, 0, 0))],
        out_shape=[jax.ShapeDtypeStruct((s, RET_WIDTH), F32), jax.ShapeDtypeStruct((s, D_MODEL), BF16),
                   jax.ShapeDtypeStruct((RET_HEADS, nc, RET_DIM, RET_DIM), BF16)],
        scratch_shapes=[pltpu.VMEM((RET_HEADS, RET_DIM, RET_DIM), F32)],
        compiler_params=_params("arbitrary"))(proj, cos, sin, intra, q_dec, k_dec, chunk_dec)


def _retention_bwd(proj, o, states, dcat, cos, sin, tables):
    s = proj.shape[0]
    nc = s // BLK
    intra, q_dec, k_dec, chunk_dec = tables

    def body(p_ref, o_ref, st_ref, dc_ref, cos_ref, sin_ref, in_ref, qd_ref, kd_ref, cd_ref, dp_ref, dstate):
        @pl.when(pl.program_id(0) == 0)
        def _():
            dstate[...] = jnp.zeros_like(dstate)

        cosv, sinv = cos_ref[...], sin_ref[...]
        for h in range(RET_HEADS):
            c0 = h * RET_DIM
            q = p_ref[:, c0:c0 + RET_DIM].astype(F32)
            k = p_ref[:, RET_WIDTH + c0:RET_WIDTH + c0 + RET_DIM].astype(F32)
            v = p_ref[:, 2 * RET_WIDTH + c0:2 * RET_WIDTH + c0 + RET_DIM]
            g = p_ref[:, 3 * RET_WIDTH + c0:3 * RET_WIDTH + c0 + RET_DIM].astype(F32)
            o = o_ref[:, c0:c0 + RET_DIM]
            dc = dc_ref[:, c0:c0 + RET_DIM].astype(F32)
            rstd = lax.rsqrt(jnp.mean(o * o, axis=-1, keepdims=True) + EPS)
            nrm = o * rstd
            sg = _sigmoid(g)
            dg = dc * nrm * (sg * (1.0 + g * (1.0 - sg)))
            dn = dc * (g * sg)
            do = rstd * (dn - nrm * jnp.mean(dn * nrm, axis=-1, keepdims=True))
            qb = _rope_half(q, cosv, sinv).astype(BF16)
            kr = _rope_half(k, cosv, sinv) * (RET_DIM ** -0.5)
            kb = kr.astype(BF16)
            mask = in_ref[h]
            qd, kd = qd_ref[h], kd_ref[h]
            prev_b = st_ref[h, 0]
            dnext = dstate[h]
            dnext_b = dnext.astype(BF16)
            att = (_dot_nt(qb, kb) * mask).astype(BF16)
            do_b = do.astype(BF16)
            doq = (do * qd).astype(BF16)
            dv = _dot_tn(att, do_b) + _dot((kr * kd).astype(BF16), dnext_b)
            ds = (_dot_nt(do_b, v) * mask).astype(BF16)
            dqr = _dot(ds, kb) + _dot_nt(doq, prev_b)
            dkr = _dot_tn(ds, qb) + _dot_nt(v, dnext_b) * kd
            dstate[h] = cd_ref[h] * dnext + _dot_tn(qb, doq)
            dq = _unrope_half(dqr, cosv, sinv)
            dk = _unrope_half(dkr * (RET_DIM ** -0.5), cosv, sinv)
            dp_ref[:, c0:c0 + RET_DIM] = dq.astype(BF16)
            dp_ref[:, RET_WIDTH + c0:RET_WIDTH + c0 + RET_DIM] = dk.astype(BF16)
            dp_ref[:, 2 * RET_WIDTH + c0:2 * RET_WIDTH + c0 + RET_DIM] = dv.astype(BF16)
            dp_ref[:, 3 * RET_WIDTH + c0:3 * RET_WIDTH + c0 + RET_DIM] = dg.astype(BF16)

    rev = lambda n: nc - 1 - n
    full = lambda shape: pl.BlockSpec(shape, lambda n: (0,) * len(shape))
    return pl.pallas_call(
        body, name="retention_bwd", grid=(nc,),
        in_specs=[pl.BlockSpec((BLK, 4 * RET_WIDTH), lambda n: (rev(n), 0)),
                  pl.BlockSpec((BLK, RET_WIDTH), lambda n: (rev(n), 0)),
                  pl.BlockSpec((RET_HEADS, 1, RET_DIM, RET_DIM), lambda n: (0, rev(n), 0, 0)),
                  pl.BlockSpec((BLK, RET_WIDTH), lambda n: (rev(n), 0)),
                  pl.BlockSpec((BLK, RET_DIM // 2), lambda n: (rev(n), 0)), pl.BlockSpec((BLK, RET_DIM // 2), lambda n: (rev(n), 0)),
                  full((RET_HEADS, BLK, BLK)), full((RET_HEADS, BLK, 1)), full((RET_HEADS, BLK, 1)), full((RET_HEADS, 1, 1))],
        out_specs=pl.BlockSpec((BLK, 4 * RET_WIDTH), lambda n: (rev(n), 0)),
        out_shape=jax.ShapeDtypeStruct((s, EVEN_IN), BF16),
        scratch_shapes=[pltpu.VMEM((RET_HEADS, RET_DIM, RET_DIM), F32)],
        compiler_params=_params("arbitrary"))(proj, o, states, dcat, cos, sin, intra, q_dec, k_dec, chunk_dec)


CONV_ROWS = 256
HALO = 16


def _conv_pieces(p, halo, conv_w, first):
    rows = p.shape[0]
    gb, gc, u, gv = (p[:, i * CONV_WIDTH:(i + 1) * CONV_WIDTH] for i in range(4))
    cu = gc * u
    hcu = halo[:, CONV_WIDTH:2 * CONV_WIDTH] * halo[:, 2 * CONV_WIDTH:3 * CONV_WIDTH]
    hcu = jnp.where(first, 0.0, hcu)
    r1, r2 = hcu[HALO - 1:HALO], hcu[HALO - 2:HALO - 1]
    row = lax.broadcasted_iota(jnp.int32, cu.shape, 0)
    m1 = jnp.where(row == 0, r1, pltpu.roll(cu, 1, 0))
    m2 = jnp.where(row == 0, r2, jnp.where(row == 1, r1, pltpu.roll(cu, 2, 0)))
    del rows, conv_w
    return gb, gc, u, gv, cu, m1, m2


def _conv_fwd(proj, conv_w, cat):
    s = proj.shape[0]
    per = CONV_ROWS // HALO

    def body(p_ref, halo_ref, w_ref, cat_in, cat_ref):
        del cat_in
        first = pl.program_id(0) == 0
        gb, _, _, gv, cu, m1, m2 = _conv_pieces(p_ref[...].astype(F32), halo_ref[...].astype(F32), None, first)
        conv = w_ref[0:1, :] * m2 + w_ref[1:2, :] * m1 + w_ref[2:3, :] * cu
        cat_ref[...] = (gb * conv * (gv * _sigmoid(gv))).astype(BF16)

    return pl.pallas_call(
        body, name="conv_fwd", grid=(s // CONV_ROWS,),
        in_specs=[pl.BlockSpec((CONV_ROWS, 4 * CONV_WIDTH), lambda i: (i, 1)),
                  pl.BlockSpec((HALO, 4 * CONV_WIDTH), lambda i: (jnp.maximum(i * per - 1, 0), 1)),
                  pl.BlockSpec((3, CONV_WIDTH), lambda i: (0, 0)), ANY],
        out_specs=pl.BlockSpec((CONV_ROWS, CONV_WIDTH), lambda i: (i, 1)),
        out_shape=jax.ShapeDtypeStruct(cat.shape, cat.dtype), input_output_aliases={3: 0},
        compiler_params=_params("parallel"))(proj, proj, conv_w, cat)


def _conv_bwd(proj, dcat, conv_w, dproj):
    s = proj.shape[0]
    per = CONV_ROWS // HALO
    last_halo = s // HALO - 1
    nsteps = s // CONV_ROWS

    def body(p_ref, halo_ref, nxt_ref, dc_ref, dnxt_ref, w_ref, dp_in, dp_ref, dw_ref):
        del dp_in
        i = pl.program_id(0)
        gb, gc, u, gv, cu, m1, m2 = _conv_pieces(p_ref[...].astype(F32), halo_ref[...].astype(F32), None, i == 0)
        w0, w1, w2 = w_ref[0:1, :], w_ref[1:2, :], w_ref[2:3, :]
        conv = w0 * m2 + w1 * m1 + w2 * cu
        dco = dc_ref[...].astype(F32)
        sg = _sigmoid(gv)
        silu = gv * sg
        dgb = dco * conv * silu
        dgv = dco * gb * conv * (sg * (1.0 + gv * (1.0 - sg)))
        dconv = dco * gb * silu
        nxt = nxt_ref[...].astype(F32)
        ngv = nxt[:, 3 * CONV_WIDTH:]
        dnext = dnxt_ref[...].astype(F32) * nxt[:, :CONV_WIDTH] * (ngv * _sigmoid(ngv))
        dnext = jnp.where(i == nsteps - 1, 0.0, dnext)
        n1, n2 = dnext[0:1], dnext[1:2]
        row = lax.broadcasted_iota(jnp.int32, dconv.shape, 0)
        p1 = jnp.where(row == CONV_ROWS - 1, n1, pltpu.roll(dconv, CONV_ROWS - 1, 0))
        p2 = jnp.where(row == CONV_ROWS - 1, n2, jnp.where(row == CONV_ROWS - 2, n1, pltpu.roll(dconv, CONV_ROWS - 2, 0)))
        dcu = w2 * dconv + w1 * p1 + w0 * p2
        dp_ref[...] = jnp.concatenate([dgb, dcu * u, dcu * gc, dgv], axis=1).astype(BF16)

        @pl.when(i == 0)
        def _():
            dw_ref[...] = jnp.zeros_like(dw_ref)

        taps = [jnp.sum(dconv * m, axis=0, keepdims=True) for m in (m2, m1, cu)]
        r8 = lax.broadcasted_iota(jnp.int32, dw_ref.shape, 0)
        dw_ref[...] += jnp.where(r8 == 0, taps[0], jnp.where(r8 == 1, taps[1], jnp.where(r8 == 2, taps[2], 0.0)))

    return pl.pallas_call(
        body, name="conv_bwd", grid=(nsteps,),
        in_specs=[pl.BlockSpec((CONV_ROWS, 4 * CONV_WIDTH), lambda i: (i, 1)),
                  pl.BlockSpec((HALO, 4 * CONV_WIDTH), lambda i: (jnp.maximum(i * per - 1, 0), 1)),
                  pl.BlockSpec((HALO, 4 * CONV_WIDTH), lambda i: (jnp.minimum((i + 1) * per, last_halo), 1)),
                  pl.BlockSpec((CONV_ROWS, CONV_WIDTH), lambda i: (i, 1)),
                  pl.BlockSpec((HALO, CONV_WIDTH), lambda i: (jnp.minimum((i + 1) * per, last_halo), 1)),
                  pl.BlockSpec((3, CONV_WIDTH), lambda i: (0, 0)), ANY],
        out_specs=[pl.BlockSpec((CONV_ROWS, 4 * CONV_WIDTH), lambda i: (i, 1)), pl.BlockSpec((8, CONV_WIDTH), lambda i: (0, 0))],
        out_shape=[jax.ShapeDtypeStruct(dproj.shape, dproj.dtype), jax.ShapeDtypeStruct((8, CONV_WIDTH), F32)],
        input_output_aliases={6: 0},
        compiler_params=_params("arbitrary"))(proj, proj, proj, dcat, dcat, conv_w, dproj)


GROUP_WIDTH = 8 * HEAD_DIM
GROUP_HEADS = 8
SLAB = 128
Q_COL = 0
KV_COL = 4
GATE_COL = 5
K_SLAB0 = ATTN_WIDTH // SLAB
V_SLAB0 = (ATTN_WIDTH + KV_WIDTH) // SLAB
KV_ROWS = 512


def _half_mask(shape, which):
    return (_lane(shape) // HEAD_DIM) == which


def _dup_head(slab, which):
    kept = jnp.where(_half_mask(slab.shape, which), slab, 0.0)
    return kept + pltpu.roll(kept, HEAD_DIM, 1)


def _stack_heads(x):
    parts = []
    for sl in range(GROUP_WIDTH // SLAB):
        slab = x[:, sl * SLAB:(sl + 1) * SLAB]
        parts += [jnp.where(_half_mask(slab.shape, e), slab, 0.0) for e in range(2)]
    return jnp.concatenate(parts, axis=0)


def _unstack_heads(y):
    slabs = []
    for sl in range(GROUP_WIDTH // SLAB):
        a, b = y[(2 * sl) * BLK:(2 * sl + 1) * BLK], y[(2 * sl + 1) * BLK:(2 * sl + 2) * BLK]
        slabs.append(jnp.where(_half_mask(a.shape, 0), a, b))
    return jnp.concatenate(slabs, axis=1)


def _q_prep(q, qw, cosf, sins, ones_bd):
    rstd = lax.rsqrt(_group_mean(q * q, ones_bd) + EPS)
    nrm = q * rstd
    y = nrm * qw
    return nrm, rstd, y * cosf + _partner64(y) * sins


def _band(tri_ref, n):
    own = tri_ref[...] > 0.5
    return own, jnp.where(jnp.logical_and(n == 0, jnp.logical_not(own)), -1e30, 0.0)


def _fold(pair, own):
    return jnp.where(own, pair[:, BLK:], pair[:, :BLK])


def _unfold(folded, own):
    return jnp.concatenate([jnp.where(own, 0.0, folded), jnp.where(own, folded, 0.0)], axis=1)


def _head_probs(raw_scores, sink, own, bias):
    sc = _fold(raw_scores, own) * (HEAD_DIM ** -0.5) + bias
    m = jnp.maximum(jnp.max(sc, axis=-1, keepdims=True), sink)
    p = jnp.exp(sc - m)
    psink = jnp.exp(sink - m)
    inv = 1.0 / (jnp.sum(p, axis=-1, keepdims=True) + psink)
    return p * inv, psink * inv


def _k_prep(k, kw, cosf, sins, ones_bd):
    rstd = lax.rsqrt(_group_mean(k * k, ones_bd) + EPS)
    nrm = k * rstd
    y = nrm * kw
    return nrm, rstd, y * cosf + _partner64(y) * sins


def _qk_prep(proj, qw, kw, cos, sins, ones_q, ones_kv):
    s = proj.shape[0]
    rows = min(KV_ROWS, s)

    def body(p_ref, qw_ref, kw_ref, cos_ref, sin_ref, oq_ref, ok_ref, o_ref):
        j = pl.program_id(1)

        @pl.when(j < KV_COL)
        def _():
            cosf, sinf = _tile_lanes(cos_ref[...], 4), _tile_lanes(sin_ref[...], 4)
            o_ref[...] = _q_prep(p_ref[...].astype(F32), qw_ref[...], cosf, sinf, oq_ref[...])[2].astype(BF16)

        @pl.when(j == KV_COL)
        def _():
            cosf, sinf = _tile_lanes(cos_ref[...], 2), _tile_lanes(sin_ref[...], 2)
            kr = _k_prep(p_ref[:, :KV_WIDTH].astype(F32), kw_ref[...], cosf, sinf, ok_ref[...])[2]
            o_ref[...] = jnp.concatenate([kr.astype(BF16), p_ref[:, KV_WIDTH:]], axis=1)

    full = lambda shape: pl.BlockSpec(shape, lambda i, j: (0,) * len(shape))
    tab = pl.BlockSpec((rows, SLAB), lambda i, j: (i, 0))
    blk = pl.BlockSpec((rows, GROUP_WIDTH), lambda i, j: (i, j))
    return pl.pallas_call(
        body, name="qk_prep", grid=(s // rows, KV_COL + 1),
        in_specs=[blk, full((1, GROUP_WIDTH)), full((1, KV_WIDTH)), tab, tab, full((GROUP_WIDTH, GROUP_WIDTH)),
                  full((KV_WIDTH, KV_WIDTH))],
        out_specs=blk, out_shape=jax.ShapeDtypeStruct((s, ATTN_WIDTH + 2 * KV_WIDTH), BF16),
        compiler_params=_params("parallel", "arbitrary"))(proj, qw, kw, cos, sins, ones_q, ones_kv)


def _keys_values(kc_ref, kp_ref, vc_ref, vp_ref, which):
    dup = lambda ref: _dup_head(ref[...].astype(F32), which)
    return (jnp.concatenate([dup(kp_ref), dup(kc_ref)], axis=0).astype(BF16),
            jnp.concatenate([dup(vp_ref), dup(vc_ref)], axis=0).astype(BF16))


def _swa_specs():
    prev = lambda n: jnp.maximum(n - 1, 0)
    slab = lambda col0, row: pl.BlockSpec((BLK, SLAB), lambda g, n: (row(n), col0 + g // 2))
    cur = lambda n: n
    full = lambda shape: pl.BlockSpec(shape, lambda g, n: (0,) * len(shape))
    return dict(
        sinks=pl.BlockSpec(memory_space=pltpu.SMEM),
        gate=pl.BlockSpec((BLK, GROUP_WIDTH), lambda g, n: (n, GATE_COL + g)),
        kc=slab(K_SLAB0, cur), kp=slab(K_SLAB0, prev), vc=slab(V_SLAB0, cur), vp=slab(V_SLAB0, prev),
        tri=full((BLK, BLK)), group=pl.BlockSpec((BLK, GROUP_WIDTH), lambda g, n: (n, g)))


def _lower_triangle():
    return jnp.tril(jnp.ones((BLK, BLK), F32))


def _swa_fwd(qk, proj, sinks):
    s = proj.shape[0]
    nb = s // BLK
    sp = _swa_specs()

    def body(sink_ref, q_ref, kc_ref, kp_ref, vc_ref, vp_ref, gate_ref, tri_ref, ag_ref, o_ref):
        g, n = pl.program_id(0), pl.program_id(1)
        kcat, vcat = _keys_values(kc_ref, kp_ref, vc_ref, vp_ref, g % 2)
        scores = _dot_nt(_stack_heads(q_ref[...]), kcat)
        own, bias = _band(tri_ref, n)
        probs = []
        for j in range(GROUP_HEADS):
            p, _ = _head_probs(scores[j * BLK:(j + 1) * BLK], sink_ref[g * GROUP_HEADS + j], own, bias)
            probs.append(_unfold(p, own).astype(BF16))
        o = _unstack_heads(_dot(jnp.concatenate(probs, axis=0), vcat))
        gate = gate_ref[...].astype(F32)
        o_ref[...] = o.astype(BF16)
        ag_ref[...] = (o * (gate * _sigmoid(gate))).astype(BF16)

    shp = jax.ShapeDtypeStruct((s, ATTN_WIDTH), BF16)
    return pl.pallas_call(
        body, name="swa_fwd", grid=(KV_HEADS, nb),
        in_specs=[sp["sinks"], sp["group"], sp["kc"], sp["kp"], sp["vc"], sp["vp"], sp["gate"], sp["tri"]],
        out_specs=[sp["group"], sp["group"]], out_shape=[shp, shp],
        compiler_params=_params("parallel", "arbitrary"),
    )(sinks, qk, qk, qk, qk, qk, proj, _lower_triangle())


def _swa_bwd(qk, proj, dag, sinks):
    s = proj.shape[0]
    nb = s // BLK
    sp = _swa_specs()

    def body(sink_ref, q_ref, kc_ref, kp_ref, vc_ref, vp_ref, gate_ref, dag_ref, tri_ref,
             dq_ref, dkc_ref, dkp_ref, dvc_ref, dvp_ref, dsink_ref):
        g, n = pl.program_id(0), pl.program_id(1)
        kcat, vcat = _keys_values(kc_ref, kp_ref, vc_ref, vp_ref, g % 2)
        gate = gate_ref[...].astype(F32)
        do = dag_ref[...].astype(F32) * (gate * _sigmoid(gate))
        q_stack = _stack_heads(q_ref[...])
        do_stack = _stack_heads(do).astype(BF16)
        scores = _dot_nt(q_stack, kcat)
        dprobs = _dot_nt(do_stack, vcat)
        own, bias = _band(tri_ref, n)
        probs, dscores, dsinks = [], [], []
        for j in range(GROUP_HEADS):
            rows = slice(j * BLK, (j + 1) * BLK)
            p, psink = _head_probs(scores[rows], sink_ref[g * GROUP_HEADS + j], own, bias)
            dp = _fold(dprobs[rows], own)
            delta = jnp.sum(p * dp, axis=-1, keepdims=True)
            probs.append(_unfold(p, own).astype(BF16))
            dscores.append(_unfold(p * (dp - delta) * (HEAD_DIM ** -0.5), own).astype(BF16))
            dsinks.append(-jnp.sum(psink * delta, axis=0, keepdims=True))
        ds = jnp.concatenate(dscores, axis=0)
        dk = _dot_tn(ds, q_stack)
        dv = _dot_tn(jnp.concatenate(probs, axis=0), do_stack)
        dk = dk + pltpu.roll(dk, HEAD_DIM, 1)
        dv = dv + pltpu.roll(dv, HEAD_DIM, 1)
        dkp_ref[0], dkc_ref[0] = dk[:BLK], dk[BLK:]
        dvp_ref[0], dvc_ref[0] = dv[:BLK], dv[BLK:]
        dq_ref[...] = _unstack_heads(_dot(ds, kcat)).astype(BF16)

        @pl.when(n == 0)
        def _():
            dsink_ref[...] = jnp.zeros_like(dsink_ref)

        r8 = lax.broadcasted_iota(jnp.int32, (8, SLAB), 0)
        upd = jnp.zeros((8, SLAB), F32)
        for j in range(GROUP_HEADS):
            upd = jnp.where(r8 == j, dsinks[j], upd)
        dsink_ref[0] += upd

    cur_out = pl.BlockSpec((1, BLK, SLAB), lambda g, n: (g, n, 0))
    prev_out = pl.BlockSpec((1, BLK, SLAB), lambda g, n: (g, (n + nb - 1) % nb, 0))
    kv_shape = jax.ShapeDtypeStruct((KV_HEADS, s, SLAB), F32)
    return pl.pallas_call(
        body, name="swa_bwd", grid=(KV_HEADS, nb),
        in_specs=[sp["sinks"], sp["group"], sp["kc"], sp["kp"], sp["vc"], sp["vp"], sp["gate"], sp["group"], sp["tri"]],
        out_specs=[sp["group"], cur_out, prev_out, cur_out, prev_out, pl.BlockSpec((1, 8, SLAB), lambda g, n: (g, 0, 0))],
        out_shape=[jax.ShapeDtypeStruct((s, ATTN_WIDTH), BF16), kv_shape, kv_shape, kv_shape, kv_shape,
                   jax.ShapeDtypeStruct((KV_HEADS, 8, SLAB), F32)],
        compiler_params=_params("parallel", "arbitrary"),
    )(sinks, qk, qk, qk, qk, qk, proj, dag, _lower_triangle())


def _swa_bwd_finish(proj, dqr, o, dag, dkc, dkp, dvc, dvp, qw, kw, cos, sins, ones_q, ones_kv):
    s = proj.shape[0]
    rows = min(KV_ROWS, s)
    n_q = KV_COL
    q_of = lambda j: jnp.clip(j - 1, 0, n_q - 1)
    gate_of = lambda j: jnp.clip(j - 1 - n_q, 0, n_q - 1)

    def body(kv_ref, p_ref, dqr_ref, o_ref, dag_ref, dkc_ref, dkp_ref, dvc_ref, dvp_ref, qw_ref, kw_ref, cos_ref, sin_ref,
             oq_ref, ok_ref, dp_ref, dqw_ref, dkw_ref):
        j, i = pl.program_id(0), pl.program_id(1)

        @pl.when(j == 0)
        def _():
            @pl.when(i == 0)
            def _():
                dkw_ref[...] = jnp.zeros_like(dkw_ref)

            def assemble(cur_ref, prv_ref):
                tot = [cur_ref[h] + prv_ref[h] for h in range(KV_HEADS)]
                first = _half_mask(tot[0].shape, 0)
                return jnp.concatenate([jnp.where(first, tot[0], tot[1]), jnp.where(first, tot[2], tot[3])], axis=1)

            dkr = assemble(dkc_ref, dkp_ref)
            dv = assemble(dvc_ref, dvp_ref)
            cosf, sinf = _tile_lanes(cos_ref[...], 2), _tile_lanes(sin_ref[...], 2)
            nrm, rstd, _ = _k_prep(kv_ref[:, :KV_WIDTH].astype(F32), kw_ref[...], cosf, sinf, ok_ref[...])
            dy = dkr * cosf + _partner64(dkr * sinf)
            dn = dy * kw_ref[...]
            dk = rstd * (dn - nrm * _group_mean(dn * nrm, ok_ref[...]))
            dp_ref[...] = jnp.concatenate([dk, dv], axis=1).astype(BF16)
            dkw_ref[...] += jnp.sum(dy * nrm, axis=0, keepdims=True)

        @pl.when(jnp.logical_and(j >= 1, j <= n_q))
        def _():
            @pl.when(i == 0)
            def _():
                dqw_ref[...] = jnp.zeros_like(dqw_ref)

            cosf, sinf = _tile_lanes(cos_ref[...], 4), _tile_lanes(sin_ref[...], 4)
            nrm, rstd, _ = _q_prep(p_ref[...].astype(F32), qw_ref[...], cosf, sinf, oq_ref[...])
            dq = dqr_ref[...].astype(F32)
            dy = dq * cosf + _partner64(dq * sinf)
            dn = dy * qw_ref[...]
            dp_ref[...] = (rstd * (dn - nrm * _group_mean(dn * nrm, oq_ref[...]))).astype(BF16)
            dqw_ref[0] += jnp.sum(dy * nrm, axis=0, keepdims=True)

        @pl.when(j > n_q)
        def _():
            gate = p_ref[...].astype(F32)
            sg = _sigmoid(gate)
            dp_ref[...] = (dag_ref[...].astype(F32) * o_ref[...].astype(F32) * (sg * (1.0 + gate * (1.0 - sg)))).astype(BF16)

    first_pass = lambda j, i: jnp.where(j == 0, i, 0)
    acc = pl.BlockSpec((KV_HEADS, rows, SLAB), lambda j, i: (0, first_pass(j, i), 0))
    full = lambda shape: pl.BlockSpec(shape, lambda j, i: (0,) * len(shape))
    tab = pl.BlockSpec((rows, SLAB), lambda j, i: (i, 0))
    out_col = lambda j: jnp.where(j == 0, KV_COL, jnp.where(j <= n_q, j - 1, j))
    return pl.pallas_call(
        body, name="swa_bwd_finish", grid=(2 * n_q + 1, s // rows),
        in_specs=[pl.BlockSpec((rows, GROUP_WIDTH), lambda j, i: (first_pass(j, i), KV_COL)),
                  pl.BlockSpec((rows, GROUP_WIDTH), lambda j, i: (jnp.where(j == 0, 0, i), jnp.where(j <= n_q, q_of(j), j))),
                  pl.BlockSpec((rows, GROUP_WIDTH), lambda j, i: (jnp.where(jnp.logical_and(j >= 1, j <= n_q), i, 0), q_of(j))),
                  pl.BlockSpec((rows, GROUP_WIDTH), lambda j, i: (jnp.where(j > n_q, i, 0), gate_of(j))),
                  pl.BlockSpec((rows, GROUP_WIDTH), lambda j, i: (jnp.where(j > n_q, i, 0), gate_of(j))),
                  acc, acc, acc, acc, full((1, GROUP_WIDTH)), full((1, KV_WIDTH)), tab, tab,
                  full((GROUP_WIDTH, GROUP_WIDTH)), full((KV_WIDTH, KV_WIDTH))],
        out_specs=[pl.BlockSpec((rows, GROUP_WIDTH), lambda j, i: (i, out_col(j))),
                   pl.BlockSpec((1, 8, GROUP_WIDTH), lambda j, i: (q_of(j), 0, 0)), pl.BlockSpec((8, KV_WIDTH), lambda j, i: (0, 0))],
        out_shape=[jax.ShapeDtypeStruct((s, ODD_IN), BF16), jax.ShapeDtypeStruct((n_q, 8, GROUP_WIDTH), F32),
                   jax.ShapeDtypeStruct((8, KV_WIDTH), F32)],
        compiler_params=_params("arbitrary", "arbitrary"),
    )(proj, proj, dqr, o, dag, dkc, dkp, dvc, dvp, qw, kw, cos, sins, ones_q, ones_kv)


def _place():
    x, y, c = lax.axis_index("x"), lax.axis_index("y"), lax.axis_index("c")
    return x, y, c


OTHER_CHIPS = ((1, 0), (0, 1), (1, 1))


def _half_rows(ref, half, rows):
    return ref.at[pl.ds(pl.multiple_of(half * (rows // 2), 8), rows // 2)]


DMA_CHUNK_BYTES = 1 << 20
BF16_TILE_ROWS = 16


def _n_chunks(ref):
    rows = ref.shape[-2]
    nbytes = math.prod(ref.shape) * jnp.dtype(ref.dtype).itemsize
    n = 1
    while 2 * n * DMA_CHUNK_BYTES <= nbytes and rows % (2 * n * BF16_TILE_ROWS) == 0:
        n *= 2
    return n


def _row_chunk(ref, k, n):
    rows = ref.shape[-2] // n
    return ref.at[pl.ds(k * rows, rows)] if len(ref.shape) == 2 else ref.at[:, pl.ds(k * rows, rows)]


def _push(src, dst, send_sem, recv_sem, device_id):
    n = _n_chunks(src)
    for k in range(n):
        pltpu.make_async_remote_copy(src_ref=_row_chunk(src, k, n), dst_ref=_row_chunk(dst, k, n), send_sem=send_sem,
                                     recv_sem=recv_sem, device_id=device_id, device_id_type=MESH).start()
    return pltpu.make_async_remote_copy(src_ref=src, dst_ref=dst, send_sem=send_sem, recv_sem=recv_sem,
                                        device_id=device_id, device_id_type=MESH)


def _copy(src, dst, sem):
    n = _n_chunks(src)
    for k in range(n):
        pltpu.make_async_copy(_row_chunk(src, k, n), _row_chunk(dst, k, n), sem).start()
    return pltpu.make_async_copy(src, dst, sem)


HBM = pl.BlockSpec(memory_space=pltpu.HBM)
SEM = pl.BlockSpec(memory_space=pltpu.SEMAPHORE)
SPLIT_COPY_EFFECT = pltpu.SideEffectType.DATAFLOW_SIDE_EFFECTING


def _in_hbm(a):
    return pltpu.with_memory_space_constraint(a, pltpu.HBM)


def _start_copies(name, arrays, plan, n_copies, after=None):
    n = len(arrays)

    def body(*refs):
        send_sem, recv_sem = refs[-n - 3], refs[-n - 2]
        for k, (src, dst, peer) in enumerate(plan(refs[:n])):
            _push(src, dst, send_sem.at[k], recv_sem.at[k], peer)
        refs[-1][...] = jnp.zeros_like(refs[-1])

    dma = pltpu.SemaphoreType.DMA((n_copies,))
    outs = pl.pallas_call(
        body, name=name,
        out_shape=(dma, dma, *[pltpu.HBM(a.shape, a.dtype) for a in arrays], jax.ShapeDtypeStruct((8, 128), F32)),
        in_specs=[HBM] * n + ([ANY] if after is not None else []),
        out_specs=(SEM, SEM, *[HBM] * n, pl.BlockSpec(memory_space=pltpu.VMEM)),
        input_output_aliases={i: i + 2 for i in range(n)},
        compiler_params=pltpu.CompilerParams(has_side_effects=SPLIT_COPY_EFFECT),
    )(*[_in_hbm(a) for a in arrays], *((after,) if after is not None else ()))
    return outs[0], outs[1], list(outs[2:2 + n]), outs[-1]


def _wait_copies(name, send_sem, recv_sem, arrays, plan, after):
    n = len(arrays)

    def body(*refs):
        send_ref, recv_ref = refs[n], refs[n + 1]
        for k, (src, dst, peer) in enumerate(plan(refs[:n])):
            cp = pltpu.make_async_remote_copy(src_ref=src, dst_ref=dst, send_sem=send_ref.at[k], recv_sem=recv_ref.at[k],
                                              device_id=peer, device_id_type=MESH)
            cp.wait_send()
            cp.wait_recv()

    return list(pl.pallas_call(
        body, name=name, out_shape=tuple(pltpu.HBM(a.shape, a.dtype) for a in arrays),
        in_specs=[HBM] * n + [SEM, SEM, ANY], out_specs=tuple([HBM] * n),
        input_output_aliases={i: i for i in range(n)},
        compiler_params=pltpu.CompilerParams(has_side_effects=SPLIT_COPY_EFFECT),
    )(*arrays, send_sem, recv_sem, after))


def _gather_region(full, kind, chip, half=None):
    if kind == "whole":
        return full.at[chip]
    if kind == "col":
        rows, width = full.shape[0], full.shape[1] // N_CHIPS
        piece = full.at[:, pl.ds(pl.multiple_of(chip * width, 128), width)]
    else:
        rows = full.shape[0] // N_CHIPS
        piece = full.at[pl.ds(pl.multiple_of(chip * rows, BF16_TILE_ROWS), rows)]
    return piece if half is None else _half_rows(piece, half, rows)


def _gather_plan(kinds):
    def plan(fulls):
        x, y, c = _place()
        copies = []
        for fx, fy in OTHER_CHIPS:
            for full, kind in zip(fulls, kinds):
                mine = _gather_region(full, kind, 2 * x + y, c)
                copies.append((mine, mine, (x ^ fx, y ^ fy, c)))
        return copies

    return plan


def _gather_start(name, fulls, kinds, after=None):
    return _start_copies(name, list(fulls), _gather_plan(kinds), 3 * len(kinds), after)


def _gather_finish(name, started, kinds, after):
    send_sem, recv_sem, fulls, _ = started
    n = len(kinds)
    fulls = _wait_copies(name + "_wait", send_sem, recv_sem, fulls, _gather_plan(kinds), after)
    split = [i for i, kind in enumerate(kinds) if kind != "whole"]

    def body(*refs):
        out = refs[n:2 * n]
        send, recv = refs[2 * n:]
        x, y, c = _place()
        pushes = []
        for r, (fx, fy) in enumerate(OTHER_CHIPS):
            chip = 2 * (x ^ fx) + (y ^ fy)
            for j, i in enumerate(split):
                landed = _gather_region(out[i], kinds[i], chip, c)
                pushes.append(_push(landed, landed, send.at[r * len(split) + j], recv.at[r * len(split) + j], (x, y, 1 - c)))
        for cp in pushes:
            cp.wait_recv()
        for cp in pushes:
            cp.wait_send()

    dma = pltpu.SemaphoreType.DMA
    return pl.pallas_call(
        body, name=name + "_share", in_specs=[ANY] * n, out_specs=[ANY] * n,
        out_shape=[jax.ShapeDtypeStruct(a.shape, a.dtype) for a in fulls],
        input_output_aliases={i: i for i in range(n)},
        scratch_shapes=[dma((3 * len(split),)), dma((3 * len(split),))],
        compiler_params=pltpu.CompilerParams(has_side_effects=True),
    )(*fulls)


def _allreduce_small(v):
    def body(v_ref, out_ref, buf, send_sems, recv_sems):
        x, y, c = _place()
        me = 4 * x + 2 * y + c
        buf[me] = v_ref[...]
        copies = []
        for r in range(1, N_DEV):
            peer = (x ^ (r >> 2), y ^ ((r >> 1) & 1), c ^ (r & 1))
            cp = pltpu.make_async_remote_copy(src_ref=v_ref, dst_ref=buf.at[me], send_sem=send_sems.at[r - 1],
                                              recv_sem=recv_sems.at[r - 1], device_id=peer, device_id_type=MESH)
            cp.start()
            copies.append(cp)
        for cp in copies:
            cp.wait_recv()
        for cp in copies:
            cp.wait_send()
        total = buf[0]
        for d in range(1, N_DEV):
            total = total + buf[d]
        out_ref[...] = total

    vm = pl.BlockSpec(memory_space=pltpu.VMEM)
    return pl.pallas_call(
        body, name="allreduce_small", in_specs=[vm], out_specs=vm, out_shape=jax.ShapeDtypeStruct(v.shape, v.dtype),
        scratch_shapes=[pltpu.VMEM((N_DEV,) + v.shape, v.dtype), pltpu.SemaphoreType.DMA((N_DEV - 1,)),
                        pltpu.SemaphoreType.DMA((N_DEV - 1,))],
        compiler_params=pltpu.CompilerParams(has_side_effects=True),
    )(v)


def _exchange_halves(grads, name):
    n = len(grads)

    def body(*refs):
        g, theirs = refs[:n], refs[n:2 * n]
        send_sem, recv_sem = refs[2 * n:]
        x, y, c = _place()
        copies = []
        for i in range(n):
            half = g[i].shape[1] // 2
            src = g[i].at[:, pl.ds(pl.multiple_of((1 - c) * half, BF16_TILE_ROWS), half)]
            copies.append(_push(src, theirs[i], send_sem.at[i], recv_sem.at[i], (x, y, 1 - c)))
        for cp in copies:
            cp.wait_recv()
            cp.wait_send()

    dma = pltpu.SemaphoreType.DMA
    return pl.pallas_call(
        body, name=name, in_specs=[ANY] * n, out_specs=[ANY] * n,
        out_shape=[jax.ShapeDtypeStruct((a.shape[0], a.shape[1] // 2, a.shape[2]), a.dtype) for a in grads],
        scratch_shapes=[dma((n,)), dma((n,))],
        compiler_params=pltpu.CompilerParams(has_side_effects=True),
    )(*grads)


def _pair_sum(g, theirs, core, name):
    pieces, half, cols = theirs.shape
    tr = min(half, 256)
    per = half // tr

    def body(core_ref, g_ref, t_ref, o_ref):
        del core_ref
        o_ref[...] = (g_ref[...].astype(F32) + t_ref[...].astype(F32)).astype(BF16)

    spec = pl.BlockSpec((1, tr, cols), lambda p, i, core_ref: (p, i, 0))
    return pl.pallas_call(
        body, name=name, out_shape=jax.ShapeDtypeStruct(theirs.shape, BF16),
        grid_spec=pltpu.PrefetchScalarGridSpec(
            num_scalar_prefetch=1, grid=(pieces, per),
            in_specs=[pl.BlockSpec((1, tr, cols), lambda p, i, core_ref: (p, core_ref[0] * per + i, 0)), spec],
            out_specs=spec),
        compiler_params=_params("parallel", "parallel"))(core, g, theirs)


def _scatter_plan(n):
    def plan(refs):
        parts, stacks = refs[:n], refs[n:]
        x, y, c = _place()
        copies = []
        for fx, fy in OTHER_CHIPS:
            chip = 2 * (x ^ fx) + (y ^ fy)
            for part, stack in zip(parts, stacks):
                if part.shape[0] == N_CHIPS:
                    piece = part.at[chip]
                else:
                    width = part.shape[2] // N_CHIPS
                    piece = part.at[0].at[:, pl.ds(pl.multiple_of(chip * width, 128), width)]
                copies.append((piece, stack.at[2 * x + y], (x ^ fx, y ^ fy, c)))
        return copies

    return plan


def _scatter_start(name, parts, after=None):
    def landing(a):
        return (N_CHIPS, a.shape[1], a.shape[2] if a.shape[0] == N_CHIPS else a.shape[2] // N_CHIPS)

    stacks = [lax.empty(landing(a), a.dtype) for a in parts]
    return _start_copies(name, list(parts) + stacks, _scatter_plan(len(parts)), 3 * len(parts), after)


def _scatter_finish(name, started, after):
    send_sem, recv_sem, arrays, _ = started
    n = len(arrays) // 2
    arrays = _wait_copies(name + "_wait", send_sem, recv_sem, arrays, _scatter_plan(n), after)
    return arrays[:n], arrays[n:]


def _sum_chips(part, stack, place, name):
    _, r, c = stack.shape
    tr = 256
    per = r // tr

    def body(place_ref, own_ref, a_ref, b_ref, c_ref, o_ref):
        del place_ref
        total = own_ref[0].astype(F32)
        for ref in (a_ref, b_ref, c_ref):
            total = total + ref[0].astype(F32)
        o_ref[...] = total

    if part.shape[0] == N_CHIPS:
        own = pl.BlockSpec((1, tr, c), lambda i, pr: (pr[1], i, 0))
    else:
        own = pl.BlockSpec((1, tr, c), lambda i, pr: (0, i, pr[1]))
    other = lambda flip: pl.BlockSpec((1, tr, c), lambda i, pr: (pr[1] ^ flip, i, 0))
    return pl.pallas_call(
        body, name=name, out_shape=jax.ShapeDtypeStruct((2 * r, c), F32),
        grid_spec=pltpu.PrefetchScalarGridSpec(
            num_scalar_prefetch=1, grid=(per,), in_specs=[own, other(2), other(1), other(3)],
            out_specs=pl.BlockSpec((tr, c), lambda i, pr: (pr[0] * per + i, 0))),
        compiler_params=_params("parallel"))(place, part, stack, stack, stack)


def _share_halves(fulls):
    n = len(fulls)

    def body(*refs):
        dst = refs[n:2 * n]
        send_sem, recv_sem = refs[2 * n:]
        x, y, c = _place()
        copies = []
        for i in range(n):
            mine = _half_rows(dst[i], c, dst[i].shape[0])
            copies.append(_push(mine, mine, send_sem.at[i], recv_sem.at[i], (x, y, 1 - c)))
        for cp in copies:
            cp.wait_recv()
            cp.wait_send()

    dma = pltpu.SemaphoreType.DMA
    return pl.pallas_call(
        body, name="share_halves", in_specs=[ANY] * n, out_specs=[ANY] * n,
        out_shape=[jax.ShapeDtypeStruct(a.shape, a.dtype) for a in fulls],
        input_output_aliases={i: i for i in range(n)}, scratch_shapes=[dma((n,)), dma((n,))],
        compiler_params=pltpu.CompilerParams(has_side_effects=True),
    )(*fulls)


MM = dict(tm=1024, tn=1024, tk=2048)


def _local_step(x, target, ev_norm_w, q_norm_w, k_norm_w, sinks, own_first, weights_first, weights_late, emit):
    s = x.shape[0]
    cos_r, sin_r = _rope_tables(s, RET_DIM)
    cos_a, sin_a = _rope_tables(s, HEAD_DIM)
    cos_a = jnp.tile(cos_a, (1, 4))
    sins_a = jnp.tile(jnp.concatenate([-sin_a, sin_a], axis=1), (1, 2))
    tables = _retention_tables()
    ones_q, ones_kv = _block_diag_mean(GROUP_WIDTH), _block_diag_mean(KV_WIDTH)
    qw_g = jnp.tile(q_norm_w, (1, GROUP_WIDTH // HEAD_DIM))
    kw_kv = jnp.tile(k_norm_w, (1, KV_WIDTH // HEAD_DIM))
    sinks1 = sinks.reshape(Q_HEADS)

    own_w_in0, own_block, start_token = own_first
    h0, h0_t = _rmsnorm(x, ev_norm_w, "norm0", after=start_token)
    shifted = dict(shift=own_block * (own_w_in0.shape[1] // MM["tn"]), total=EVEN_IN // MM["tn"], tm=MM["tm"], tn=MM["tn"],
                   out_dtype=BF16)
    own_blocks = own_w_in0.shape[1] // MM["tn"]
    proj0 = _mm_shifted(h0, own_w_in0, b_shifted=False, first=0, count=own_blocks, name="proj0_own", **shifted)
    w_in0, conv_w, od_norm_w, token = weights_first(proj0)
    proj0 = _mm_shifted(h0, w_in0, b_shifted=True, first=own_blocks, count=shifted["total"] - own_blocks, name="proj0_rest",
                        into=proj0, after=token, **shifted)
    o_ret, cat, states = _retention_fwd(proj0, cos_r, sin_r, tables)
    cat = _conv_fwd(proj0, conv_w, cat)
    w_out0, w_in1, w_out1 = weights_late(cat)

    def residual_and_norm(prod, x_ref, w_ref, x1_ref, h1_ref, h1t_ref):
        x1v = x_ref[...] + prod
        x1_ref[...] = x1v
        rstd = lax.rsqrt(jnp.mean(x1v * x1v, axis=-1, keepdims=True) + EPS)
        h1v = (x1v * rstd * w_ref[...]).astype(BF16)
        h1_ref[...] = h1v
        h1t_ref[...] = h1v.T

    def residual_and_loss(prod, x1_ref, t_ref, dyb_ref, sq_ref):
        diff = (x1_ref[...] + prod) - t_ref[...]
        dyb_ref[...] = (diff * (1.0 / D_MODEL)).astype(BF16)

        @pl.when(pl.program_id(0) == 0)
        def _():
            sq_ref[...] = jnp.zeros_like(sq_ref)

        sq_ref[...] += jnp.sum(jnp.sum(diff * diff, axis=1, keepdims=True), axis=0, keepdims=True)

    act = lambda dt: jax.ShapeDtypeStruct((s, D_MODEL), dt)
    x1, h1, h1_t = _mm_rows(cat, w_out0, [x], [od_norm_w], [act(F32), act(BF16), jax.ShapeDtypeStruct((D_MODEL, s), BF16)],
                            residual_and_norm, tm=min(s, 512), name="out0")
    proj1 = _mm(h1, w_in1, mode="nn", out_dtype=BF16, name="proj1", tm=1024, tn=1536, tk=2048)
    qk = _qk_prep(proj1, qw_g, kw_kv, cos_a, sins_a, ones_q, ones_kv)
    ag, o_att = _swa_fwd(qk, proj1, sinks1)
    dy_b, sq = _mm_rows(ag, w_out1, [x1, target], [], [act(BF16), jax.ShapeDtypeStruct((8, 128), F32)],
                        residual_and_loss, tm=min(s, 256), name="out1")

    g_w_out1 = _mm(ag, dy_b, mode="tn", out_dtype=BF16, name="g_w_out1", **MM)
    dag = _mm(dy_b, w_out1, mode="nt", out_dtype=BF16, name="d_ag", **MM)
    dqr, dkc, dkp, dvc, dvp, dsink = _swa_bwd(qk, proj1, dag, sinks1)
    dproj1, dqw, dkw = _swa_bwd_finish(proj1, dqr, o_att, dag, dkc, dkp, dvc, dvp, qw_g, kw_kv, cos_a, sins_a, ones_q, ones_kv)
    g_w_in1 = _mm(h1_t, dproj1, mode="nn", out_dtype=BF16, name="g_w_in1", tm=1024, tn=1536, tk=2048)
    token = emit("layer1", (("od_w_in", g_w_in1, "col"), ("od_w_out", g_w_out1, "row")))
    dh1 = _mm(dproj1, w_in1, mode="nt", out_dtype=BF16, name="d_h1", tm=1024, tn=1024, tk=2304, after=token)
    dx1_b, g_norm1 = _rmsnorm_bwd(x1, od_norm_w, dh1, dy_b, "norm1_bwd", BF16)

    g_w_out0 = _mm(cat, dx1_b, mode="tn", out_dtype=BF16, name="g_w_out0", **MM)
    token = emit("out0", (("ev_w_out", g_w_out0, "row"),))
    dcat = _mm(dx1_b, w_out0, mode="nt", out_dtype=BF16, name="d_cat", after=token, **MM)
    dproj0 = _retention_bwd(proj0, o_ret, states, dcat, cos_r, sin_r, tables)
    dproj0, g_conv = _conv_bwd(proj0, dcat, conv_w, dproj0)
    g_w_in0 = _mm(h0_t, dproj0, mode="nn", out_dtype=BF16, name="g_w_in0", **MM)
    token = emit("in0", (("ev_w_in", g_w_in0, "col"),))
    dh0 = _mm(dproj0, w_in0, mode="nt", out_dtype=BF16, name="d_h0", after=token, **MM)
    grad_x, g_norm0 = _rmsnorm_bwd(x, ev_norm_w, dh0, dx1_b, "norm0_bwd", F32)

    g_qw = dqw[:, 0, :].reshape(Q_HEADS, HEAD_DIM).sum(axis=0)
    g_kw = dkw[0].reshape(KV_HEADS, HEAD_DIM).sum(axis=0)
    g_sinks = dsink[:, :, 0].reshape(Q_HEADS)
    small = dict(ev_norm=g_norm0[0], od_norm=g_norm1[0], conv=g_conv[:3], qw=g_qw, kw=g_kw, sinks=g_sinks)
    return sq[0, 0], grad_x, small


def _pack_small_grads(small):
    pad = lambda v: jnp.pad(v, (0, D_MODEL - v.shape[0]))
    tail = pad(jnp.concatenate([small["qw"], small["kw"], small["sinks"]]))
    rows = [small["ev_norm"], small["od_norm"]] + [pad(small["conv"][t]) for t in range(3)] + [tail]
    rows += [jnp.zeros((D_MODEL,), F32)] * (8 - len(rows))
    return jnp.stack(rows)


class _ReduceScatter:
    def __init__(self, place):
        self.place = place
        self.started = []

    def send(self, tag, grads):
        pieces = [g[None] if kind == "col" else g.reshape(N_CHIPS, g.shape[0] // N_CHIPS, g.shape[1]) for _, g, kind in grads]
        theirs = _exchange_halves(pieces, "exchange_halves_" + tag)
        parts = [_pair_sum(g, t, self.place[:1], "pair_sum_" + nm) for g, t, (nm, _, _) in zip(pieces, theirs, grads)]
        started = _scatter_start("scatter_" + tag, parts)
        self.started.append((tag, [nm for nm, _, _ in grads], started))
        return started[3]

    def finish(self, after):
        names, sums = [], []
        for tag, group, started in self.started:
            parts, stacks = _scatter_finish("scatter_" + tag, started, after)
            sums += [_sum_chips(p, s, self.place, "chip_sum_" + nm) for p, s, nm in zip(parts, stacks, group)]
            names += group
        return dict(zip(names, _share_halves(sums)))


def kernel(x, ev_norm_w, ev_w_in, ev_conv_w, ev_w_out, od_norm_w, od_w_in, od_q_norm_w, od_k_norm_w, od_sinks, od_w_out, loss_target, m_ev_norm_w, m_ev_w_in, m_ev_conv_w, m_ev_w_out, m_od_norm_w, m_od_w_in, m_od_q_norm_w, m_od_k_norm_w, m_od_sinks, m_od_w_out, v_ev_norm_w, v_ev_w_in, v_ev_conv_w, v_ev_w_out, v_od_norm_w, v_od_w_in, v_od_q_norm_w, v_od_k_norm_w, v_od_sinks, v_od_w_out):
    my_chip = 2 * lax.axis_index("x") + lax.axis_index("y")
    place = jnp.stack([lax.axis_index("c"), my_chip]).astype(jnp.int32)
    shard_w = D_MODEL // N_CHIPS
    conv_shard = CONV_WIDTH // N_CHIPS

    small_in = jnp.zeros((8, shard_w), F32)
    small_in = small_in.at[0].set(od_norm_w[0]).at[1:4, :conv_shard].set(ev_conv_w[0])
    small_in = lax.dynamic_update_slice(jnp.zeros((N_CHIPS, 8, shard_w), F32), small_in[None], (my_chip, 0, 0))
    chip = place[1:]
    first_kinds, late_kinds = ("col", "whole"), ("row", "col", "row")
    w_in0_own_place, w_in0_shard = _cast_into_gathered(ev_w_in[0], "col", chip, "cast_w_in0", keep_shard=True)
    first = _gather_start("gather_first", [w_in0_own_place, small_in], first_kinds)
    late_own = [_cast_into_gathered(ev_w_out[0], "row", chip, "cast_w_out0"),
                _cast_into_gathered(od_w_in[0], "col", chip, "cast_w_in1"),
                _cast_into_gathered(od_w_out[0], "row", chip, "cast_w_out1")]
    late = []

    def weights_first(after):
        w_in0, small_all = _gather_finish("gather_first", first, first_kinds, after)
        late.append(_gather_start("gather_late", late_own, late_kinds, after=w_in0))
        od_norm_full = small_all[:, 0, :].reshape(1, D_MODEL)
        conv_full = jnp.transpose(small_all[:, 1:4, :conv_shard], (1, 0, 2)).reshape(3, CONV_WIDTH)
        return w_in0, conv_full, od_norm_full, late[0][3]

    def weights_late(after):
        return _gather_finish("gather_late", late[0], late_kinds, after)

    reduce_scatter = _ReduceScatter(place)
    sq, grad_x, small = _local_step(x[0], loss_target[0], ev_norm_w, od_q_norm_w, od_k_norm_w, od_sinks,
                                    (w_in0_shard, chip, first[3]), weights_first, weights_late, reduce_scatter.send)
    loss = lax.psum(0.5 * sq / D_MODEL, ("x", "y", "c"))

    big = reduce_scatter.finish(grad_x)
    g_ev_w_in, g_ev_w_out, g_od_w_in, g_od_w_out = big["ev_w_in"], big["ev_w_out"], big["od_w_in"], big["od_w_out"]
    tot = _allreduce_small(_pack_small_grads(small))
    g_ev_norm = tot[0:1]
    g_od_norm = lax.dynamic_slice(tot, (1, my_chip * shard_w), (1, shard_w))
    g_conv = lax.dynamic_slice(tot, (2, my_chip * conv_shard), (3, conv_shard))
    g_qw, g_kw, g_sinks = tot[5:6, 0:HEAD_DIM], tot[5:6, HEAD_DIM:2 * HEAD_DIM], tot[5:6, 2 * HEAD_DIM:2 * HEAD_DIM + Q_HEADS]

    upd = {}
    upd["ev_w_in"] = _adamw(ev_w_in[0], g_ev_w_in, m_ev_w_in[0], v_ev_w_in[0], "adamw_ev_w_in")
    upd["ev_w_out"] = _adamw(ev_w_out[0], g_ev_w_out, m_ev_w_out[0], v_ev_w_out[0], "adamw_ev_w_out")
    upd["od_w_in"] = _adamw(od_w_in[0], g_od_w_in, m_od_w_in[0], v_od_w_in[0], "adamw_od_w_in")
    upd["od_w_out"] = _adamw(od_w_out[0], g_od_w_out, m_od_w_out[0], v_od_w_out[0], "adamw_od_w_out")
    smalls = (("ev_norm_w", ev_norm_w, g_ev_norm, m_ev_norm_w, v_ev_norm_w),
              ("ev_conv_w", ev_conv_w, g_conv, m_ev_conv_w, v_ev_conv_w),
              ("od_norm_w", od_norm_w, g_od_norm, m_od_norm_w, v_od_norm_w),
              ("od_q_norm_w", od_q_norm_w, g_qw, m_od_q_norm_w, v_od_q_norm_w),
              ("od_k_norm_w", od_k_norm_w, g_kw, m_od_k_norm_w, v_od_k_norm_w),
              ("od_sinks", od_sinks, g_sinks, m_od_sinks, v_od_sinks))
    sizes = [w.size for _, w, _, _, _ in smalls]
    padded = 8 * 128 * math.ceil(sum(sizes) / (8 * 128))
    pack = lambda arrs, fill: jnp.concatenate(
        [a.reshape(-1) for a in arrs] + [jnp.full((padded - sum(sizes),), fill, F32)]).reshape(8, padded // 8)
    packed = _adamw(pack([w for _, w, _, _, _ in smalls], 0.0), pack([g for _, _, g, _, _ in smalls], 0.0),
                    pack([m for _, _, _, m, _ in smalls], 0.0), pack([v for _, _, _, _, v in smalls], 1.0), "adamw_small")
    offs = [sum(sizes[:i]) for i in range(len(sizes))]
    for (nm, w, _, _, _), off, size in zip(smalls, offs, sizes):
        upd[nm] = tuple(p.reshape(-1)[off:off + size].reshape(w.shape) for p in packed)
    for nm in ("ev_w_in", "ev_w_out", "od_w_in", "od_w_out"):
        upd[nm] = tuple(u[None] for u in upd[nm])
    order = ("ev_norm_w", "ev_w_in", "ev_conv_w", "ev_w_out", "od_norm_w", "od_w_in", "od_q_norm_w", "od_k_norm_w", "od_sinks", "od_w_out")
    return (loss, grad_x[None], *[upd[nm][0] for nm in order], *[upd[nm][1] for nm in order],
            *[upd[nm][2] for nm in order], *[upd[nm][3] for nm in order])
```

```python
import functools
import math

import jax
import jax.numpy as jnp
from jax import lax
from jax.experimental import pallas as pl
from jax.experimental.pallas import tpu as pltpu

F32 = jnp.float32
BF16 = jnp.bfloat16

D_MODEL = 2048
RET_HEADS = 4
RET_DIM = 256
RET_WIDTH = 1024
CONV_WIDTH = 1024
EVEN_IN = 8192
Q_HEADS = 32
HEAD_DIM = 64
KV_HEADS = 4
KV_WIDTH = 256
ATTN_WIDTH = 2048
ODD_IN = 4608
BLK = 128
ROPE_THETA = 10000.0
EPS = 1e-6
ADAM_LR = 0.001
ADAM_B1 = 0.9
ADAM_B2 = 0.999
ADAM_EPS = 1e-08
ADAM_WD = 0.01
ADAM_STEP = 10
N_CHIPS = 4
N_DEV = 8
VMEM_LIMIT_BYTES = 56 * 1024 * 1024
MESH = pl.DeviceIdType.MESH
ANY = pl.BlockSpec(memory_space=pl.ANY)


def _params(*sem):
    return pltpu.CompilerParams(dimension_semantics=sem, vmem_limit_bytes=VMEM_LIMIT_BYTES)


def _dot(a, b):
    return jnp.dot(a, b, preferred_element_type=F32)


def _dot_nt(a, b):
    return lax.dot_general(a, b, (((1,), (1,)), ((), ())), preferred_element_type=F32)


def _dot_tn(a, b):
    return lax.dot_general(a, b, (((0,), (0,)), ((), ())), preferred_element_type=F32)


def _sigmoid(x):
    return 1.0 / (1.0 + jnp.exp(-x))


def _mm(a, b, *, mode, tm, tn, tk, out_dtype, name, add=None, after=None):
    if mode == "nn":
        (m, k), n = a.shape, b.shape[1]
    elif mode == "nt":
        (m, k), n = a.shape, b.shape[0]
    else:
        (k, m), n = a.shape, b.shape[1]
    tm, tn, tk = min(tm, m), min(tn, n), min(tk, k)
    assert m % tm == 0 and n % tn == 0 and k % tk == 0, (name, m, n, k)
    nk = k // tk
    dot = {"nn": _dot, "nt": _dot_nt, "tn": _dot_tn}[mode]
    a_spec = (pl.BlockSpec((tk, tm), lambda i, j, kk: (kk, i)) if mode == "tn"
              else pl.BlockSpec((tm, tk), lambda i, j, kk: (i, kk)))
    b_spec = (pl.BlockSpec((tn, tk), lambda i, j, kk: (j, kk)) if mode == "nt"
              else pl.BlockSpec((tk, tn), lambda i, j, kk: (kk, j)))
    o_spec = pl.BlockSpec((tm, tn), lambda i, j, kk: (i, j))
    has_add = add is not None

    def body(*refs):
        a_ref, b_ref = refs[0], refs[1]
        add_ref = refs[2] if has_add else None
        o_ref, acc_ref = refs[-2], refs[-1]
        p = dot(a_ref[...], b_ref[...])

        def finish(total):
            if has_add:
                total = total + add_ref[...].astype(F32)
            o_ref[...] = total.astype(out_dtype)

        if nk == 1:
            finish(p)
        else:
            kk = pl.program_id(2)

            @pl.when(kk == 0)
            def _():
                acc_ref[...] = p

            @pl.when(jnp.logical_and(kk > 0, kk < nk - 1))
            def _():
                acc_ref[...] += p

            @pl.when(kk == nk - 1)
            def _():
                finish(acc_ref[...] + p)

    in_specs = [a_spec, b_spec] + ([o_spec] if has_add else []) + ([ANY] if after is not None else [])
    args = (a, b) + ((add,) if has_add else ()) + ((after,) if after is not None else ())
    return pl.pallas_call(
        body, name=name, grid=(m // tm, n // tn, nk), in_specs=in_specs, out_specs=o_spec,
        out_shape=jax.ShapeDtypeStruct((m, n), out_dtype),
        scratch_shapes=[pltpu.VMEM((tm, tn) if nk > 1 else (8, 128), F32)],
        compiler_params=_params("parallel", "parallel", "arbitrary"),
    )(*args)


def _mm_shifted(a, b, shift, *, b_shifted, first, count, total, tm, tn, out_dtype, name, into=None, after=None):
    m, k = a.shape
    tm = min(tm, m)
    assert m % tm == 0
    col = lambda j, shift_ref: (shift_ref[0] + first + j) % total
    extra = [arr for arr in (into, after) if arr is not None]

    def body(shift_ref, a_ref, b_ref, *rest):
        del shift_ref
        rest[-1][...] = _dot(a_ref[...], b_ref[...]).astype(out_dtype)

    return pl.pallas_call(
        body, name=name, out_shape=jax.ShapeDtypeStruct((m, total * tn), out_dtype),
        grid_spec=pltpu.PrefetchScalarGridSpec(
            num_scalar_prefetch=1, grid=(m // tm, count),
            in_specs=[pl.BlockSpec((tm, k), lambda i, j, s: (i, 0)),
                      pl.BlockSpec((k, tn), (lambda i, j, s: (0, col(j, s))) if b_shifted else (lambda i, j, s: (0, j)))]
            + [ANY] * len(extra),
            out_specs=pl.BlockSpec((tm, tn), lambda i, j, s: (i, col(j, s)))),
        input_output_aliases={3: 0} if into is not None else {},
        compiler_params=_params("parallel", "arbitrary"))(shift, a, b, *extra)


def _mm_rows(a, b, rows_in, vecs_in, out_shapes, epilogue, *, tm, name):
    m, k = a.shape
    n = b.shape[1]
    assert m % tm == 0
    row = pl.BlockSpec((tm, n), lambda i: (i, 0))

    def body(a_ref, b_ref, *rest):
        epilogue(_dot(a_ref[...], b_ref[...]), *rest)

    def out_spec(shape):
        if shape == (m, n):
            return row
        if shape == (n, m):
            return pl.BlockSpec((n, tm), lambda i: (0, i))
        return pl.BlockSpec(shape, lambda i: (0, 0))

    out_specs = [out_spec(tuple(s.shape)) for s in out_shapes]
    return pl.pallas_call(
        body, name=name, grid=(m // tm,),
        in_specs=[pl.BlockSpec((tm, k), lambda i: (i, 0)), pl.BlockSpec((k, n), lambda i: (0, 0))] + [row] * len(rows_in)
        + [pl.BlockSpec((1, n), lambda i: (0, 0))] * len(vecs_in),
        out_specs=out_specs, out_shape=out_shapes, compiler_params=_params("arbitrary"),
    )(a, b, *rows_in, *vecs_in)


def _cast_into_gathered(w, kind, chip, name, keep_shard=False):
    r, c = w.shape
    tr = min(r, 512)
    per = r // tr

    def body(chip_ref, w_ref, *outs):
        del chip_ref
        for o_ref in outs:
            o_ref[...] = w_ref[...].astype(BF16)

    if kind == "col":
        shape, out_map = (r, N_CHIPS * c), (lambda i, chip_ref: (i, chip_ref[0]))
    else:
        shape, out_map = (N_CHIPS * r, c), (lambda i, chip_ref: (chip_ref[0] * per + i, 0))
    plain = pl.BlockSpec((tr, c), lambda i, chip_ref: (i, 0))
    out = pl.pallas_call(
        body, name=name,
        out_shape=[jax.ShapeDtypeStruct(shape, BF16)] + ([jax.ShapeDtypeStruct((r, c), BF16)] if keep_shard else []),
        grid_spec=pltpu.PrefetchScalarGridSpec(
            num_scalar_prefetch=1, grid=(per,), in_specs=[plain],
            out_specs=[pl.BlockSpec((tr, c), out_map)] + ([plain] if keep_shard else [])),
        compiler_params=_params("parallel"))(chip, w)
    return out if keep_shard else out[0]


def _rmsnorm(x, w, name, after=None):
    s, d = x.shape
    tr = 256

    def body(x_ref, w_ref, *rest):
        xv = x_ref[...]
        rstd = lax.rsqrt(jnp.mean(xv * xv, axis=-1, keepdims=True) + EPS)
        h = (xv * rstd * w_ref[...]).astype(BF16)
        rest[-2][...] = h
        rest[-1][...] = h.T

    return pl.pallas_call(
        body, name=name, grid=(s // tr,),
        in_specs=[pl.BlockSpec((tr, d), lambda i: (i, 0)), pl.BlockSpec((1, d), lambda i: (0, 0))]
        + ([ANY] if after is not None else []),
        out_specs=[pl.BlockSpec((tr, d), lambda i: (i, 0)), pl.BlockSpec((d, tr), lambda i: (0, i))],
        out_shape=[jax.ShapeDtypeStruct((s, d), BF16), jax.ShapeDtypeStruct((d, s), BF16)],
        compiler_params=_params("parallel"),
    )(x, w, *((after,) if after is not None else ()))


def _rmsnorm_bwd(x, w, dh, dres, name, out_dtype):
    s, d = x.shape
    tr = 256

    def body(x_ref, w_ref, dh_ref, dres_ref, dx_ref, dw_ref):
        xv = x_ref[...]
        rstd = lax.rsqrt(jnp.mean(xv * xv, axis=-1, keepdims=True) + EPS)
        nrm = xv * rstd
        dhv = dh_ref[...].astype(F32)
        dn = dhv * w_ref[...]
        dx = dres_ref[...].astype(F32) + rstd * (dn - nrm * jnp.mean(dn * nrm, axis=-1, keepdims=True))
        dx_ref[...] = dx.astype(out_dtype)

        @pl.when(pl.program_id(0) == 0)
        def _():
            dw_ref[...] = jnp.zeros_like(dw_ref)

        dw_ref[...] += jnp.sum(dhv * nrm, axis=0, keepdims=True)

    row = pl.BlockSpec((tr, d), lambda i: (i, 0))
    return pl.pallas_call(
        body, name=name, grid=(s // tr,),
        in_specs=[row, pl.BlockSpec((1, d), lambda i: (0, 0)), row, row],
        out_specs=[row, pl.BlockSpec((8, d), lambda i: (0, 0))],
        out_shape=[jax.ShapeDtypeStruct((s, d), out_dtype), jax.ShapeDtypeStruct((8, d), F32)],
        compiler_params=_params("arbitrary"))(x, w, dh, dres)


def _adamw(w, g, m, v, name):
    r, c = w.shape
    tr = min(r, 256)
    assert r % tr == 0

    def body(w_ref, g_ref, m_ref, v_ref, g_out, d_ref, nm_ref, nv_ref):
        gv = g_ref[...]
        g_out[...] = gv
        nm = ADAM_B1 * m_ref[...] + (1.0 - ADAM_B1) * gv
        nv = ADAM_B2 * v_ref[...] + (1.0 - ADAM_B2) * (gv * gv)
        m_hat = nm / (1.0 - ADAM_B1 ** ADAM_STEP)
        v_hat = nv / (1.0 - ADAM_B2 ** ADAM_STEP)
        d_ref[...] = -ADAM_LR * (m_hat / (jnp.sqrt(v_hat) + ADAM_EPS) + ADAM_WD * w_ref[...])
        nm_ref[...] = nm
        nv_ref[...] = nv

    spec = pl.BlockSpec((tr, c), lambda i: (i, 0))
    shp = jax.ShapeDtypeStruct((r, c), F32)
    return pl.pallas_call(body, name=name, grid=(r // tr,), in_specs=[spec] * 4, out_specs=[spec] * 4,
                          out_shape=[shp] * 4, compiler_params=_params("parallel"))(w, g, m, v)


def _rope_tables(s, dim):
    inv = 1.0 / (ROPE_THETA ** (jnp.arange(0, dim, 2, dtype=F32) / dim))
    ang = jnp.arange(s).astype(F32)[:, None] * inv[None, :]
    return jnp.cos(ang), jnp.sin(ang)


def _rope_half(x, cos, sin):
    h = x.shape[1] // 2
    x1, x2 = x[:, :h], x[:, h:]
    return jnp.concatenate([x1 * cos - x2 * sin, x2 * cos + x1 * sin], axis=1)


def _unrope_half(dy, cos, sin):
    h = dy.shape[1] // 2
    d1, d2 = dy[:, :h], dy[:, h:]
    return jnp.concatenate([d1 * cos + d2 * sin, d2 * cos - d1 * sin], axis=1)


def _lane(shape):
    return lax.broadcasted_iota(jnp.int32, shape, 1)


def _partner64(x):
    w = x.shape[1]
    first = (_lane(x.shape) % HEAD_DIM) < (HEAD_DIM // 2)
    return jnp.where(first, pltpu.roll(x, w - HEAD_DIM // 2, 1), pltpu.roll(x, HEAD_DIM // 2, 1))


def _tile_lanes(t, reps):
    return t if reps == 1 else jnp.concatenate([t] * reps, axis=1)


def _group_mean(x, ones_bd):
    hi = x.astype(BF16)
    lo = (x - hi.astype(F32)).astype(BF16)
    return _dot(hi, ones_bd) + _dot(lo, ones_bd)


def _block_diag_mean(width):
    idx = jnp.arange(width) // HEAD_DIM
    return jnp.where(idx[:, None] == idx[None, :], 1.0 / HEAD_DIM, 0.0).astype(BF16)


def _retention_tables():
    h = RET_HEADS
    log_g = jnp.log(1.0 - 2.0 ** (-5.0 - jnp.arange(h, dtype=F32)))
    idx = jnp.arange(BLK, dtype=F32)
    diff = idx[:, None] - idx[None, :]
    intra = jnp.where(diff >= 0, jnp.exp(log_g[:, None, None] * jnp.maximum(diff, 0.0)), 0.0).astype(F32)
    q_dec = jnp.exp(log_g[:, None] * (idx[None, :] + 1.0)).astype(F32)[:, :, None]
    k_dec = jnp.exp(log_g[:, None] * (BLK - 1.0 - idx[None, :])).astype(F32)[:, :, None]
    chunk_dec = jnp.exp(log_g * BLK).astype(F32)[:, None, None]
    return intra, q_dec, k_dec, chunk_dec


def _retention_fwd(proj, cos, sin, tables):
    s = proj.shape[0]
    nc = s // BLK
    intra, q_dec, k_dec, chunk_dec = tables

    def body(p_ref, cos_ref, sin_ref, in_ref, qd_ref, kd_ref, cd_ref, o_ref, cat_ref, st_ref, state):
        @pl.when(pl.program_id(0) == 0)
        def _():
            state[...] = jnp.zeros_like(state)

        cosv, sinv = cos_ref[...], sin_ref[...]
        for h in range(RET_HEADS):
            c0 = h * RET_DIM
            q = p_ref[:, c0:c0 + RET_DIM].astype(F32)
            k = p_ref[:, RET_WIDTH + c0:RET_WIDTH + c0 + RET_DIM].astype(F32)
            v = p_ref[:, 2 * RET_WIDTH + c0:2 * RET_WIDTH + c0 + RET_DIM]
            g = p_ref[:, 3 * RET_WIDTH + c0:3 * RET_WIDTH + c0 + RET_DIM].astype(F32)
            qb = _rope_half(q, cosv, sinv).astype(BF16)
            kr = _rope_half(k, cosv, sinv) * (RET_DIM ** -0.5)
            kb = kr.astype(BF16)
            scores = _dot_nt(qb, kb) * in_ref[h]
            inner = _dot(scores.astype(BF16), v)
            prev = state[h]
            prev_b = prev.astype(BF16)
            st_ref[h, 0] = prev_b
            o = inner + _dot(qb, prev_b) * qd_ref[h]
            o_ref[:, c0:c0 + RET_DIM] = o
            rstd = lax.rsqrt(jnp.mean(o * o, axis=-1, keepdims=True) + EPS)
            cat_ref[:, c0:c0 + RET_DIM] = (o * rstd * (g * _sigmoid(g))).astype(BF16)
            state[h] = cd_ref[h] * prev + _dot_tn((kr * kd_ref[h]).astype(BF16), v)

    full = lambda shape: pl.BlockSpec(shape, lambda n: (0,) * len(shape))
    return pl.pallas_call(
        body, name="retention_fwd", grid=(nc,),
        in_specs=[pl.BlockSpec((BLK, 4 * RET_WIDTH), lambda n: (n, 0)),
                  pl.BlockSpec((BLK, RET_DIM // 2), lambda n: (n, 0)), pl.BlockSpec((BLK, RET_DIM // 2), lambda n: (n, 0)),
                  full((RET_HEADS, BLK, BLK)), full((RET_HEADS, BLK, 1)), full((RET_HEADS, BLK, 1)), full((RET_HEADS, 1, 1))],
        out_specs=[pl.BlockSpec((BLK, RET_WIDTH), lambda n: (n, 0)), pl.BlockSpec((BLK, RET_WIDTH), lambda n: (n, 0)),
                   pl.BlockSpec((RET_HEADS, 1, RET_DIM, RET_DIM), lambda n: (0, n, 0, 0))],
        out_shape=[jax.ShapeDtypeStruct((s, RET_WIDTH), F32), jax.ShapeDtypeStruct((s, D_MODEL), BF16),
                   jax.ShapeDtypeStruct((RET_HEADS, nc, RET_DIM, RET_DIM), BF16)],
        scratch_shapes=[pltpu.VMEM((RET_HEADS, RET_DIM, RET_DIM), F32)],
        compiler_params=_params("arbitrary"))(proj, cos, sin, intra, q_dec, k_dec, chunk_dec)


def _retention_bwd(proj, o, states, dcat, cos, sin, tables):
    s = proj.shape[0]
    nc = s // BLK
    intra, q_dec, k_dec, chunk_dec = tables

    def body(p_ref, o_ref, st_ref, dc_ref, cos_ref, sin_ref, in_ref, qd_ref, kd_ref, cd_ref, dp_ref, dstate):
        @pl.when(pl.program_id(0) == 0)
        def _():
            dstate[...] = jnp.zeros_like(dstate)

        cosv, sinv = cos_ref[...], sin_ref[...]
        for h in range(RET_HEADS):
            c0 = h * RET_DIM
            q = p_ref[:, c0:c0 + RET_DIM].astype(F32)
            k = p_ref[:, RET_WIDTH + c0:RET_WIDTH + c0 + RET_DIM].astype(F32)
            v = p_ref[:, 2 * RET_WIDTH + c0:2 * RET_WIDTH + c0 + RET_DIM]
            g = p_ref[:, 3 * RET_WIDTH + c0:3 * RET_WIDTH + c0 + RET_DIM].astype(F32)
            o = o_ref[:, c0:c0 + RET_DIM]
            dc = dc_ref[:, c0:c0 + RET_DIM].astype(F32)
            rstd = lax.rsqrt(jnp.mean(o * o, axis=-1, keepdims=True) + EPS)
            nrm = o * rstd
            sg = _sigmoid(g)
            dg = dc * nrm * (sg * (1.0 + g * (1.0 - sg)))
            dn = dc * (g * sg)
            do = rstd * (dn - nrm * jnp.mean(dn * nrm, axis=-1, keepdims=True))
            qb = _rope_half(q, cosv, sinv).astype(BF16)
            kr = _rope_half(k, cosv, sinv) * (RET_DIM ** -0.5)
            kb = kr.astype(BF16)
            mask = in_ref[h]
            qd, kd = qd_ref[h], kd_ref[h]
            prev_b = st_ref[h, 0]
            dnext = dstate[h]
            dnext_b = dnext.astype(BF16)
            att = (_dot_nt(qb, kb) * mask).astype(BF16)
            do_b = do.astype(BF16)
            doq = (do * qd).astype(BF16)
            dv = _dot_tn(att, do_b) + _dot((kr * kd).astype(BF16), dnext_b)
            ds = (_dot_nt(do_b, v) * mask).astype(BF16)
            dqr = _dot(ds, kb) + _dot_nt(doq, prev_b)
            dkr = _dot_tn(ds, qb) + _dot_nt(v, dnext_b) * kd
            dstate[h] = cd_ref[h] * dnext + _dot_tn(qb, doq)
            dq = _unrope_half(dqr, cosv, sinv)
            dk = _unrope_half(dkr * (RET_DIM ** -0.5), cosv, sinv)
            dp_ref[:, c0:c0 + RET_DIM] = dq.astype(BF16)
            dp_ref[:, RET_WIDTH + c0:RET_WIDTH + c0 + RET_DIM] = dk.astype(BF16)
            dp_ref[:, 2 * RET_WIDTH + c0:2 * RET_WIDTH + c0 + RET_DIM] = dv.astype(BF16)
            dp_ref[:, 3 * RET_WIDTH + c0:3 * RET_WIDTH + c0 + RET_DIM] = dg.astype(BF16)

    rev = lambda n: nc - 1 - n
    full = lambda shape: pl.BlockSpec(shape, lambda n: (0,) * len(shape))
    return pl.pallas_call(
        body, name="retention_bwd", grid=(nc,),
        in_specs=[pl.BlockSpec((BLK, 4 * RET_WIDTH), lambda n: (rev(n), 0)),
                  pl.BlockSpec((BLK, RET_WIDTH), lambda n: (rev(n), 0)),
                  pl.BlockSpec((RET_HEADS, 1, RET_DIM, RET_DIM), lambda n: (0, rev(n), 0, 0)),
                  pl.BlockSpec((BLK, RET_WIDTH), lambda n: (rev(n), 0)),
                  pl.BlockSpec((BLK, RET_DIM // 2), lambda n: (rev(n), 0)), pl.BlockSpec((BLK, RET_DIM // 2), lambda n: (rev(n), 0)),
                  full((RET_HEADS, BLK, BLK)), full((RET_HEADS, BLK, 1)), full((RET_HEADS, BLK, 1)), full((RET_HEADS, 1, 1))],
        out_specs=pl.BlockSpec((BLK, 4 * RET_WIDTH), lambda n: (rev(n), 0)),
        out_shape=jax.ShapeDtypeStruct((s, EVEN_IN), BF16),
        scratch_shapes=[pltpu.VMEM((RET_HEADS, RET_DIM, RET_DIM), F32)],
        compiler_params=_params("arbitrary"))(proj, o, states, dcat, cos, sin, intra, q_dec, k_dec, chunk_dec)


CONV_ROWS = 256
HALO = 16


def _conv_pieces(p, halo, conv_w, first):
    rows = p.shape[0]
    gb, gc, u, gv = (p[:, i * CONV_WIDTH:(i + 1) * CONV_WIDTH] for i in range(4))
    cu = gc * u
    hcu = halo[:, CONV_WIDTH:2 * CONV_WIDTH] * halo[:, 2 * CONV_WIDTH:3 * CONV_WIDTH]
    hcu = jnp.where(first, 0.0, hcu)
    r1, r2 = hcu[HALO - 1:HALO], hcu[HALO - 2:HALO - 1]
    row = lax.broadcasted_iota(jnp.int32, cu.shape, 0)
    m1 = jnp.where(row == 0, r1, pltpu.roll(cu, 1, 0))
    m2 = jnp.where(row == 0, r2, jnp.where(row == 1, r1, pltpu.roll(cu, 2, 0)))
    del rows, conv_w
    return gb, gc, u, gv, cu, m1, m2


def _conv_fwd(proj, conv_w, cat):
    s = proj.shape[0]
    per = CONV_ROWS // HALO

    def body(p_ref, halo_ref, w_ref, cat_in, cat_ref):
        del cat_in
        first = pl.program_id(0) == 0
        gb, _, _, gv, cu, m1, m2 = _conv_pieces(p_ref[...].astype(F32), halo_ref[...].astype(F32), None, first)
        conv = w_ref[0:1, :] * m2 + w_ref[1:2, :] * m1 + w_ref[2:3, :] * cu
        cat_ref[...] = (gb * conv * (gv * _sigmoid(gv))).astype(BF16)

    return pl.pallas_call(
        body, name="conv_fwd", grid=(s // CONV_ROWS,),
        in_specs=[pl.BlockSpec((CONV_ROWS, 4 * CONV_WIDTH), lambda i: (i, 1)),
                  pl.BlockSpec((HALO, 4 * CONV_WIDTH), lambda i: (jnp.maximum(i * per - 1, 0), 1)),
                  pl.BlockSpec((3, CONV_WIDTH), lambda i: (0, 0)), ANY],
        out_specs=pl.BlockSpec((CONV_ROWS, CONV_WIDTH), lambda i: (i, 1)),
        out_shape=jax.ShapeDtypeStruct(cat.shape, cat.dtype), input_output_aliases={3: 0},
        compiler_params=_params("parallel"))(proj, proj, conv_w, cat)


def _conv_bwd(proj, dcat, conv_w, dproj):
    s = proj.shape[0]
    per = CONV_ROWS // HALO
    last_halo = s // HALO - 1
    nsteps = s // CONV_ROWS

    def body(p_ref, halo_ref, nxt_ref, dc_ref, dnxt_ref, w_ref, dp_in, dp_ref, dw_ref):
        del dp_in
        i = pl.program_id(0)
        gb, gc, u, gv, cu, m1, m2 = _conv_pieces(p_ref[...].astype(F32), halo_ref[...].astype(F32), None, i == 0)
        w0, w1, w2 = w_ref[0:1, :], w_ref[1:2, :], w_ref[2:3, :]
        conv = w0 * m2 + w1 * m1 + w2 * cu
        dco = dc_ref[...].astype(F32)
        sg = _sigmoid(gv)
        silu = gv * sg
        dgb = dco * conv * silu
        dgv = dco * gb * conv * (sg * (1.0 + gv * (1.0 - sg)))
        dconv = dco * gb * silu
        nxt = nxt_ref[...].astype(F32)
        ngv = nxt[:, 3 * CONV_WIDTH:]
        dnext = dnxt_ref[...].astype(F32) * nxt[:, :CONV_WIDTH] * (ngv * _sigmoid(ngv))
        dnext = jnp.where(i == nsteps - 1, 0.0, dnext)
        n1, n2 = dnext[0:1], dnext[1:2]
        row = lax.broadcasted_iota(jnp.int32, dconv.shape, 0)
        p1 = jnp.where(row == CONV_ROWS - 1, n1, pltpu.roll(dconv, CONV_ROWS - 1, 0))
        p2 = jnp.where(row == CONV_ROWS - 1, n2, jnp.where(row == CONV_ROWS - 2, n1, pltpu.roll(dconv, CONV_ROWS - 2, 0)))
        dcu = w2 * dconv + w1 * p1 + w0 * p2
        dp_ref[...] = jnp.concatenate([dgb, dcu * u, dcu * gc, dgv], axis=1).astype(BF16)

        @pl.when(i == 0)
        def _():
            dw_ref[...] = jnp.zeros_like(dw_ref)

        taps = [jnp.sum(dconv * m, axis=0, keepdims=True) for m in (m2, m1, cu)]
        r8 = lax.broadcasted_iota(jnp.int32, dw_ref.shape, 0)
        dw_ref[...] += jnp.where(r8 == 0, taps[0], jnp.where(r8 == 1, taps[1], jnp.where(r8 == 2, taps[2], 0.0)))

    return pl.pallas_call(
        body, name="conv_bwd", grid=(nsteps,),
        in_specs=[pl.BlockSpec((CONV_ROWS, 4 * CONV_WIDTH), lambda i: (i, 1)),
                  pl.BlockSpec((HALO, 4 * CONV_WIDTH), lambda i: (jnp.maximum(i * per - 1, 0), 1)),
                  pl.BlockSpec((HALO, 4 * CONV_WIDTH), lambda i: (jnp.minimum((i + 1) * per, last_halo), 1)),
                  pl.BlockSpec((CONV_ROWS, CONV_WIDTH), lambda i: (i, 1)),
                  pl.BlockSpec((HALO, CONV_WIDTH), lambda i: (jnp.minimum((i + 1) * per, last_halo), 1)),
                  pl.BlockSpec((3, CONV_WIDTH), lambda i: (0, 0)), ANY],
        out_specs=[pl.BlockSpec((CONV_ROWS, 4 * CONV_WIDTH), lambda i: (i, 1)), pl.BlockSpec((8, CONV_WIDTH), lambda i: (0, 0))],
        out_shape=[jax.ShapeDtypeStruct(dproj.shape, dproj.dtype), jax.ShapeDtypeStruct((8, CONV_WIDTH), F32)],
        input_output_aliases={6: 0},
        compiler_params=_params("arbitrary"))(proj, proj, proj, dcat, dcat, conv_w, dproj)


GROUP_WIDTH = 8 * HEAD_DIM
GROUP_HEADS = 8
SLAB = 128
Q_COL = 0
KV_COL = 4
GATE_COL = 5
K_SLAB0 = ATTN_WIDTH // SLAB
V_SLAB0 = (ATTN_WIDTH + KV_WIDTH) // SLAB
KV_ROWS = 512


def _half_mask(shape, which):
    return (_lane(shape) // HEAD_DIM) == which


def _dup_head(slab, which):
    kept = jnp.where(_half_mask(slab.shape, which), slab, 0.0)
    return kept + pltpu.roll(kept, HEAD_DIM, 1)


def _stack_heads(x):
    parts = []
    for sl in range(GROUP_WIDTH // SLAB):
        slab = x[:, sl * SLAB:(sl + 1) * SLAB]
        parts += [jnp.where(_half_mask(slab.shape, e), slab, 0.0) for e in range(2)]
    return jnp.concatenate(parts, axis=0)


def _unstack_heads(y):
    slabs = []
    for sl in range(GROUP_WIDTH // SLAB):
        a, b = y[(2 * sl) * BLK:(2 * sl + 1) * BLK], y[(2 * sl + 1) * BLK:(2 * sl + 2) * BLK]
        slabs.append(jnp.where(_half_mask(a.shape, 0), a, b))
    return jnp.concatenate(slabs, axis=1)


def _q_prep(q, qw, cosf, sins, ones_bd):
    rstd = lax.rsqrt(_group_mean(q * q, ones_bd) + EPS)
    nrm = q * rstd
    y = nrm * qw
    return nrm, rstd, y * cosf + _partner64(y) * sins


def _band(tri_ref, n):
    own = tri_ref[...] > 0.5
    return own, jnp.where(jnp.logical_and(n == 0, jnp.logical_not(own)), -1e30, 0.0)


def _fold(pair, own):
    return jnp.where(own, pair[:, BLK:], pair[:, :BLK])


def _unfold(folded, own):
    return jnp.concatenate([jnp.where(own, 0.0, folded), jnp.where(own, folded, 0.0)], axis=1)


def _head_probs(raw_scores, sink, own, bias):
    sc = _fold(raw_scores, own) * (HEAD_DIM ** -0.5) + bias
    m = jnp.maximum(jnp.max(sc, axis=-1, keepdims=True), sink)
    p = jnp.exp(sc - m)
    psink = jnp.exp(sink - m)
    inv = 1.0 / (jnp.sum(p, axis=-1, keepdims=True) + psink)
    return p * inv, psink * inv


def _k_prep(k, kw, cosf, sins, ones_bd):
    rstd = lax.rsqrt(_group_mean(k * k, ones_bd) + EPS)
    nrm = k * rstd
    y = nrm * kw
    return nrm, rstd, y * cosf + _partner64(y) * sins


def _qk_prep(proj, qw, kw, cos, sins, ones_q, ones_kv):
    s = proj.shape[0]
    rows = min(KV_ROWS, s)

    def body(p_ref, qw_ref, kw_ref, cos_ref, sin_ref, oq_ref, ok_ref, o_ref):
        j = pl.program_id(1)

        @pl.when(j < KV_COL)
        def _():
            cosf, sinf = _tile_lanes(cos_ref[...], 4), _tile_lanes(sin_ref[...], 4)
            o_ref[...] = _q_prep(p_ref[...].astype(F32), qw_ref[...], cosf, sinf, oq_ref[...])[2].astype(BF16)

        @pl.when(j == KV_COL)
        def _():
            cosf, sinf = _tile_lanes(cos_ref[...], 2), _tile_lanes(sin_ref[...], 2)
            kr = _k_prep(p_ref[:, :KV_WIDTH].astype(F32), kw_ref[...], cosf, sinf, ok_ref[...])[2]
            o_ref[...] = jnp.concatenate([kr.astype(BF16), p_ref[:, KV_WIDTH:]], axis=1)

    full = lambda shape: pl.BlockSpec(shape, lambda i, j: (0,) * len(shape))
    tab = pl.BlockSpec((rows, SLAB), lambda i, j: (i, 0))
    blk = pl.BlockSpec((rows, GROUP_WIDTH), lambda i, j: (i, j))
    return pl.pallas_call(
        body, name="qk_prep", grid=(s // rows, KV_COL + 1),
        in_specs=[blk, full((1, GROUP_WIDTH)), full((1, KV_WIDTH)), tab, tab, full((GROUP_WIDTH, GROUP_WIDTH)),
                  full((KV_WIDTH, KV_WIDTH))],
        out_specs=blk, out_shape=jax.ShapeDtypeStruct((s, ATTN_WIDTH + 2 * KV_WIDTH), BF16),
        compiler_params=_params("parallel", "arbitrary"))(proj, qw, kw, cos, sins, ones_q, ones_kv)


def _keys_values(kc_ref, kp_ref, vc_ref, vp_ref, which):
    dup = lambda ref: _dup_head(ref[...].astype(F32), which)
    return (jnp.concatenate([dup(kp_ref), dup(kc_ref)], axis=0).astype(BF16),
            jnp.concatenate([dup(vp_ref), dup(vc_ref)], axis=0).astype(BF16))


PAIR = 2


def _swa_specs():
    prev = lambda n: jnp.maximum(n - 1, 0)
    slab = lambda col0, row: pl.BlockSpec((BLK, SLAB), lambda gp, n: (row(n), col0 + gp))
    cur = lambda n: n
    full = lambda shape: pl.BlockSpec(shape, lambda gp, n: (0,) * len(shape))
    gate = lambda t: pl.BlockSpec((BLK, GROUP_WIDTH), lambda gp, n: (n, GATE_COL + PAIR * gp + t))
    return dict(
        sinks=pl.BlockSpec(memory_space=pltpu.SMEM), gate0=gate(0), gate1=gate(1),
        kc=slab(K_SLAB0, cur), kp=slab(K_SLAB0, prev), vc=slab(V_SLAB0, cur), vp=slab(V_SLAB0, prev),
        tri=full((BLK, BLK)), pair=pl.BlockSpec((BLK, PAIR * GROUP_WIDTH), lambda gp, n: (n, gp)))


def _lower_triangle():
    return jnp.tril(jnp.ones((BLK, BLK), F32))


def _swa_fwd(qk, proj, sinks):
    s = proj.shape[0]
    nb = s // BLK
    sp = _swa_specs()

    def body(sink_ref, q_ref, kc_ref, kp_ref, vc_ref, vp_ref, gate0_ref, gate1_ref, tri_ref, ag_ref, o_ref):
        gp, n = pl.program_id(0), pl.program_id(1)
        own, bias = _band(tri_ref, n)
        for t, gate_ref in enumerate((gate0_ref, gate1_ref)):
            cols = slice(t * GROUP_WIDTH, (t + 1) * GROUP_WIDTH)
            first_head = (PAIR * gp + t) * GROUP_HEADS
            kcat, vcat = _keys_values(kc_ref, kp_ref, vc_ref, vp_ref, t)
            scores = _dot_nt(_stack_heads(q_ref[:, cols]), kcat)
            probs = []
            for j in range(GROUP_HEADS):
                p, _ = _head_probs(scores[j * BLK:(j + 1) * BLK], sink_ref[first_head + j], own, bias)
                probs.append(_unfold(p, own).astype(BF16))
            o = _unstack_heads(_dot(jnp.concatenate(probs, axis=0), vcat))
            gate = gate_ref[...].astype(F32)
            o_ref[:, cols] = o.astype(BF16)
            ag_ref[:, cols] = (o * (gate * _sigmoid(gate))).astype(BF16)

    shp = jax.ShapeDtypeStruct((s, ATTN_WIDTH), BF16)
    return pl.pallas_call(
        body, name="swa_fwd", grid=(KV_HEADS // PAIR, nb),
        in_specs=[sp["sinks"], sp["pair"], sp["kc"], sp["kp"], sp["vc"], sp["vp"], sp["gate0"], sp["gate1"], sp["tri"]],
        out_specs=[sp["pair"], sp["pair"]], out_shape=[shp, shp],
        compiler_params=_params("parallel", "arbitrary"),
    )(sinks, qk, qk, qk, qk, qk, proj, proj, _lower_triangle())


def _swa_bwd(qk, proj, dag, sinks):
    s = proj.shape[0]
    nb = s // BLK
    sp = _swa_specs()

    def body(sink_ref, q_ref, kc_ref, kp_ref, vc_ref, vp_ref, gate0_ref, gate1_ref, dag_ref, tri_ref,
             dq_ref, dkc_ref, dkp_ref, dvc_ref, dvp_ref, dsink_ref):
        gp, n = pl.program_id(0), pl.program_id(1)
        own, bias = _band(tri_ref, n)

        @pl.when(n == 0)
        def _():
            dsink_ref[...] = jnp.zeros_like(dsink_ref)

        for t, gate_ref in enumerate((gate0_ref, gate1_ref)):
            cols = slice(t * GROUP_WIDTH, (t + 1) * GROUP_WIDTH)
            first_head = (PAIR * gp + t) * GROUP_HEADS
            kcat, vcat = _keys_values(kc_ref, kp_ref, vc_ref, vp_ref, t)
            gate = gate_ref[...].astype(F32)
            do = dag_ref[:, cols].astype(F32) * (gate * _sigmoid(gate))
            q_stack = _stack_heads(q_ref[:, cols])
            do_stack = _stack_heads(do).astype(BF16)
            scores = _dot_nt(q_stack, kcat)
            dprobs = _dot_nt(do_stack, vcat)
            probs, dscores, dsinks = [], [], []
            for j in range(GROUP_HEADS):
                rows = slice(j * BLK, (j + 1) * BLK)
                p, psink = _head_probs(scores[rows], sink_ref[first_head + j], own, bias)
                dp = _fold(dprobs[rows], own)
                delta = jnp.sum(p * dp, axis=-1, keepdims=True)
                probs.append(_unfold(p, own).astype(BF16))
                dscores.append(_unfold(p * (dp - delta) * (HEAD_DIM ** -0.5), own).astype(BF16))
                dsinks.append(-jnp.sum(psink * delta, axis=0, keepdims=True))
            ds = jnp.concatenate(dscores, axis=0)
            dk = _dot_tn(ds, q_stack)
            dv = _dot_tn(jnp.concatenate(probs, axis=0), do_stack)
            dk = dk + pltpu.roll(dk, HEAD_DIM, 1)
            dv = dv + pltpu.roll(dv, HEAD_DIM, 1)
            dkp_ref[t], dkc_ref[t] = dk[:BLK], dk[BLK:]
            dvp_ref[t], dvc_ref[t] = dv[:BLK], dv[BLK:]
            dq_ref[:, cols] = _unstack_heads(_dot(ds, kcat)).astype(BF16)
            r8 = lax.broadcasted_iota(jnp.int32, (8, SLAB), 0)
            upd = jnp.zeros((8, SLAB), F32)
            for j in range(GROUP_HEADS):
                upd = jnp.where(r8 == j, dsinks[j], upd)
            dsink_ref[t] += upd

    cur_out = pl.BlockSpec((PAIR, BLK, SLAB), lambda gp, n: (gp, n, 0))
    prev_out = pl.BlockSpec((PAIR, BLK, SLAB), lambda gp, n: (gp, (n + nb - 1) % nb, 0))
    kv_shape = jax.ShapeDtypeStruct((KV_HEADS, s, SLAB), F32)
    return pl.pallas_call(
        body, name="swa_bwd", grid=(KV_HEADS // PAIR, nb),
        in_specs=[sp["sinks"], sp["pair"], sp["kc"], sp["kp"], sp["vc"], sp["vp"], sp["gate0"], sp["gate1"], sp["pair"],
                  sp["tri"]],
        out_specs=[sp["pair"], cur_out, prev_out, cur_out, prev_out, pl.BlockSpec((PAIR, 8, SLAB), lambda gp, n: (gp, 0, 0))],
        out_shape=[jax.ShapeDtypeStruct((s, ATTN_WIDTH), BF16), kv_shape, kv_shape, kv_shape, kv_shape,
                   jax.ShapeDtypeStruct((KV_HEADS, 8, SLAB), F32)],
        compiler_params=_params("parallel", "arbitrary"),
    )(sinks, qk, qk, qk, qk, qk, proj, proj, dag, _lower_triangle())


def _swa_bwd_finish(proj, dqr, o, dag, dkc, dkp, dvc, dvp, qw, kw, cos, sins, ones_q, ones_kv):
    s = proj.shape[0]
    rows = min(KV_ROWS, s)
    n_q = KV_COL
    q_of = lambda j: jnp.clip(j - 1, 0, n_q - 1)
    gate_of = lambda j: jnp.clip(j - 1 - n_q, 0, n_q - 1)

    def body(kv_ref, p_ref, dqr_ref, o_ref, dag_ref, dkc_ref, dkp_ref, dvc_ref, dvp_ref, qw_ref, kw_ref, cos_ref, sin_ref,
             oq_ref, ok_ref, dp_ref, dqw_ref, dkw_ref):
        j, i = pl.program_id(0), pl.program_id(1)

        @pl.when(j == 0)
        def _():
            @pl.when(i == 0)
            def _():
                dkw_ref[...] = jnp.zeros_like(dkw_ref)

            def assemble(cur_ref, prv_ref):
                tot = [cur_ref[h] + prv_ref[h] for h in range(KV_HEADS)]
                first = _half_mask(tot[0].shape, 0)
                return jnp.concatenate([jnp.where(first, tot[0], tot[1]), jnp.where(first, tot[2], tot[3])], axis=1)

            dkr = assemble(dkc_ref, dkp_ref)
            dv = assemble(dvc_ref, dvp_ref)
            cosf, sinf = _tile_lanes(cos_ref[...], 2), _tile_lanes(sin_ref[...], 2)
            nrm, rstd, _ = _k_prep(kv_ref[:, :KV_WIDTH].astype(F32), kw_ref[...], cosf, sinf, ok_ref[...])
            dy = dkr * cosf + _partner64(dkr * sinf)
            dn = dy * kw_ref[...]
            dk = rstd * (dn - nrm * _group_mean(dn * nrm, ok_ref[...]))
            dp_ref[...] = jnp.concatenate([dk, dv], axis=1).astype(BF16)
            dkw_ref[...] += jnp.sum(dy * nrm, axis=0, keepdims=True)

        @pl.when(jnp.logical_and(j >= 1, j <= n_q))
        def _():
            @pl.when(i == 0)
            def _():
                dqw_ref[...] = jnp.zeros_like(dqw_ref)

            cosf, sinf = _tile_lanes(cos_ref[...], 4), _tile_lanes(sin_ref[...], 4)
            nrm, rstd, _ = _q_prep(p_ref[...].astype(F32), qw_ref[...], cosf, sinf, oq_ref[...])
            dq = dqr_ref[...].astype(F32)
            dy = dq * cosf + _partner64(dq * sinf)
            dn = dy * qw_ref[...]
            dp_ref[...] = (rstd * (dn - nrm * _group_mean(dn * nrm, oq_ref[...]))).astype(BF16)
            dqw_ref[0] += jnp.sum(dy * nrm, axis=0, keepdims=True)

        @pl.when(j > n_q)
        def _():
            gate = p_ref[...].astype(F32)
            sg = _sigmoid(gate)
            dp_ref[...] = (dag_ref[...].astype(F32) * o_ref[...].astype(F32) * (sg * (1.0 + gate * (1.0 - sg)))).astype(BF16)

    first_pass = lambda j, i: jnp.where(j == 0, i, 0)
    acc = pl.BlockSpec((KV_HEADS, rows, SLAB), lambda j, i: (0, first_pass(j, i), 0))
    full = lambda shape: pl.BlockSpec(shape, lambda j, i: (0,) * len(shape))
    tab = pl.BlockSpec((rows, SLAB), lambda j, i: (i, 0))
    out_col = lambda j: jnp.where(j == 0, KV_COL, jnp.where(j <= n_q, j - 1, j))
    return pl.pallas_call(
        body, name="swa_bwd_finish", grid=(2 * n_q + 1, s // rows),
        in_specs=[pl.BlockSpec((rows, GROUP_WIDTH), lambda j, i: (first_pass(j, i), KV_COL)),
                  pl.BlockSpec((rows, GROUP_WIDTH), lambda j, i: (jnp.where(j == 0, 0, i), jnp.where(j <= n_q, q_of(j), j))),
                  pl.BlockSpec((rows, GROUP_WIDTH), lambda j, i: (jnp.where(jnp.logical_and(j >= 1, j <= n_q), i, 0), q_of(j))),
                  pl.BlockSpec((rows, GROUP_WIDTH), lambda j, i: (jnp.where(j > n_q, i, 0), gate_of(j))),
                  pl.BlockSpec((rows, GROUP_WIDTH), lambda j, i: (jnp.where(j > n_q, i, 0), gate_of(j))),
                  acc, acc, acc, acc, full((1, GROUP_WIDTH)), full((1, KV_WIDTH)), tab, tab,
                  full((GROUP_WIDTH, GROUP_WIDTH)), full((KV_WIDTH, KV_WIDTH))],
        out_specs=[pl.BlockSpec((rows, GROUP_WIDTH), lambda j, i: (i, out_col(j))),
                   pl.BlockSpec((1, 8, GROUP_WIDTH), lambda j, i: (q_of(j), 0, 0)), pl.BlockSpec((8, KV_WIDTH), lambda j, i: (0, 0))],
        out_shape=[jax.ShapeDtypeStruct((s, ODD_IN), BF16), jax.ShapeDtypeStruct((n_q, 8, GROUP_WIDTH), F32),
                   jax.ShapeDtypeStruct((8, KV_WIDTH), F32)],
        compiler_params=_params("arbitrary", "arbitrary"),
    )(proj, proj, dqr, o, dag, dkc, dkp, dvc, dvp, qw, kw, cos, sins, ones_q, ones_kv)


def _place():
    x, y, c = lax.axis_index("x"), lax.axis_index("y"), lax.axis_index("c")
    return x, y, c


OTHER_CHIPS = ((1, 0), (0, 1), (1, 1))


def _half_rows(ref, half, rows):
    return ref.at[pl.ds(pl.multiple_of(half * (rows // 2), 8), rows // 2)]


DMA_CHUNK_BYTES = 1 << 20
BF16_TILE_ROWS = 16


def _n_chunks(ref):
    rows = ref.shape[-2]
    nbytes = math.prod(ref.shape) * jnp.dtype(ref.dtype).itemsize
    n = 1
    while 2 * n * DMA_CHUNK_BYTES <= nbytes and rows % (2 * n * BF16_TILE_ROWS) == 0:
        n *= 2
    return n


def _row_chunk(ref, k, n):
    rows = ref.shape[-2] // n
    return ref.at[pl.ds(k * rows, rows)] if len(ref.shape) == 2 else ref.at[:, pl.ds(k * rows, rows)]


def _push(src, dst, send_sem, recv_sem, device_id):
    n = _n_chunks(src)
    for k in range(n):
        pltpu.make_async_remote_copy(src_ref=_row_chunk(src, k, n), dst_ref=_row_chunk(dst, k, n), send_sem=send_sem,
                                     recv_sem=recv_sem, device_id=device_id, device_id_type=MESH).start()
    return pltpu.make_async_remote_copy(src_ref=src, dst_ref=dst, send_sem=send_sem, recv_sem=recv_sem,
                                        device_id=device_id, device_id_type=MESH)


def _copy(src, dst, sem):
    n = _n_chunks(src)
    for k in range(n):
        pltpu.make_async_copy(_row_chunk(src, k, n), _row_chunk(dst, k, n), sem).start()
    return pltpu.make_async_copy(src, dst, sem)


HBM = pl.BlockSpec(memory_space=pltpu.HBM)
SEM = pl.BlockSpec(memory_space=pltpu.SEMAPHORE)
SPLIT_COPY_EFFECT = pltpu.SideEffectType.DATAFLOW_SIDE_EFFECTING


def _in_hbm(a):
    return pltpu.with_memory_space_constraint(a, pltpu.HBM)


def _start_copies(name, arrays, plan, n_copies, after=None):
    n = len(arrays)

    def body(*refs):
        send_sem, recv_sem = refs[-n - 3], refs[-n - 2]
        for k, (src, dst, peer) in enumerate(plan(refs[:n])):
            _push(src, dst, send_sem.at[k], recv_sem.at[k], peer)
        refs[-1][...] = jnp.zeros_like(refs[-1])

    dma = pltpu.SemaphoreType.DMA((n_copies,))
    outs = pl.pallas_call(
        body, name=name,
        out_shape=(dma, dma, *[pltpu.HBM(a.shape, a.dtype) for a in arrays], jax.ShapeDtypeStruct((8, 128), F32)),
        in_specs=[HBM] * n + ([ANY] if after is not None else []),
        out_specs=(SEM, SEM, *[HBM] * n, pl.BlockSpec(memory_space=pltpu.VMEM)),
        input_output_aliases={i: i + 2 for i in range(n)},
        compiler_params=pltpu.CompilerParams(has_side_effects=SPLIT_COPY_EFFECT),
    )(*[_in_hbm(a) for a in arrays], *((after,) if after is not None else ()))
    return outs[0], outs[1], list(outs[2:2 + n]), outs[-1]


def _wait_copies(name, send_sem, recv_sem, arrays, plan, after):
    n = len(arrays)

    def body(*refs):
        send_ref, recv_ref = refs[n], refs[n + 1]
        for k, (src, dst, peer) in enumerate(plan(refs[:n])):
            cp = pltpu.make_async_remote_copy(src_ref=src, dst_ref=dst, send_sem=send_ref.at[k], recv_sem=recv_ref.at[k],
                                              device_id=peer, device_id_type=MESH)
            cp.wait_send()
            cp.wait_recv()

    return list(pl.pallas_call(
        body, name=name, out_shape=tuple(pltpu.HBM(a.shape, a.dtype) for a in arrays),
        in_specs=[HBM] * n + [SEM, SEM, ANY], out_specs=tuple([HBM] * n),
        input_output_aliases={i: i for i in range(n)},
        compiler_params=pltpu.CompilerParams(has_side_effects=SPLIT_COPY_EFFECT),
    )(*arrays, send_sem, recv_sem, after))


def _gather_region(full, kind, chip, half=None):
    if kind == "whole":
        return full.at[chip]
    if kind == "col":
        rows, width = full.shape[0], full.shape[1] // N_CHIPS
        piece = full.at[:, pl.ds(pl.multiple_of(chip * width, 128), width)]
    else:
        rows = full.shape[0] // N_CHIPS
        piece = full.at[pl.ds(pl.multiple_of(chip * rows, BF16_TILE_ROWS), rows)]
    return piece if half is None else _half_rows(piece, half, rows)


def _gather_plan(kinds):
    def plan(fulls):
        x, y, c = _place()
        copies = []
        for fx, fy in OTHER_CHIPS:
            for full, kind in zip(fulls, kinds):
                mine = _gather_region(full, kind, 2 * x + y, c)
                copies.append((mine, mine, (x ^ fx, y ^ fy, c)))
        return copies

    return plan


def _gather_start(name, fulls, kinds, after=None):
    return _start_copies(name, list(fulls), _gather_plan(kinds), 3 * len(kinds), after)


def _gather_finish(name, started, kinds, after):
    send_sem, recv_sem, fulls, _ = started
    n = len(kinds)
    fulls = _wait_copies(name + "_wait", send_sem, recv_sem, fulls, _gather_plan(kinds), after)
    split = [i for i, kind in enumerate(kinds) if kind != "whole"]

    def body(*refs):
        out = refs[n:2 * n]
        send, recv = refs[2 * n:]
        x, y, c = _place()
        pushes = []
        for r, (fx, fy) in enumerate(OTHER_CHIPS):
            chip = 2 * (x ^ fx) + (y ^ fy)
            for j, i in enumerate(split):
                landed = _gather_region(out[i], kinds[i], chip, c)
                pushes.append(_push(landed, landed, send.at[r * len(split) + j], recv.at[r * len(split) + j], (x, y, 1 - c)))
        for cp in pushes:
            cp.wait_recv()
        for cp in pushes:
            cp.wait_send()

    dma = pltpu.SemaphoreType.DMA
    return pl.pallas_call(
        body, name=name + "_share", in_specs=[ANY] * n, out_specs=[ANY] * n,
        out_shape=[jax.ShapeDtypeStruct(a.shape, a.dtype) for a in fulls],
        input_output_aliases={i: i for i in range(n)},
        scratch_shapes=[dma((3 * len(split),)), dma((3 * len(split),))],
        compiler_params=pltpu.CompilerParams(has_side_effects=True),
    )(*fulls)


def _allreduce_small(v):
    def body(v_ref, out_ref, buf, send_sems, recv_sems):
        x, y, c = _place()
        me = 4 * x + 2 * y + c
        buf[me] = v_ref[...]
        copies = []
        for r in range(1, N_DEV):
            peer = (x ^ (r >> 2), y ^ ((r >> 1) & 1), c ^ (r & 1))
            cp = pltpu.make_async_remote_copy(src_ref=v_ref, dst_ref=buf.at[me], send_sem=send_sems.at[r - 1],
                                              recv_sem=recv_sems.at[r - 1], device_id=peer, device_id_type=MESH)
            cp.start()
            copies.append(cp)
        for cp in copies:
            cp.wait_recv()
        for cp in copies:
            cp.wait_send()
        total = buf[0]
        for d in range(1, N_DEV):
            total = total + buf[d]
        out_ref[...] = total

    vm = pl.BlockSpec(memory_space=pltpu.VMEM)
    return pl.pallas_call(
        body, name="allreduce_small", in_specs=[vm], out_specs=vm, out_shape=jax.ShapeDtypeStruct(v.shape, v.dtype),
        scratch_shapes=[pltpu.VMEM((N_DEV,) + v.shape, v.dtype), pltpu.SemaphoreType.DMA((N_DEV - 1,)),
                        pltpu.SemaphoreType.DMA((N_DEV - 1,))],
        compiler_params=pltpu.CompilerParams(has_side_effects=True),
    )(v)


def _exchange_halves(grads, name):
    n = len(grads)

    def body(*refs):
        g, theirs = refs[:n], refs[n:2 * n]
        send_sem, recv_sem = refs[2 * n:]
        x, y, c = _place()
        copies = []
        for i in range(n):
            half = g[i].shape[1] // 2
            src = g[i].at[:, pl.ds(pl.multiple_of((1 - c) * half, BF16_TILE_ROWS), half)]
            copies.append(_push(src, theirs[i], send_sem.at[i], recv_sem.at[i], (x, y, 1 - c)))
        for cp in copies:
            cp.wait_recv()
            cp.wait_send()

    dma = pltpu.SemaphoreType.DMA
    return pl.pallas_call(
        body, name=name, in_specs=[ANY] * n, out_specs=[ANY] * n,
        out_shape=[jax.ShapeDtypeStruct((a.shape[0], a.shape[1] // 2, a.shape[2]), a.dtype) for a in grads],
        scratch_shapes=[dma((n,)), dma((n,))],
        compiler_params=pltpu.CompilerParams(has_side_effects=True),
    )(*grads)


def _pair_sum(g, theirs, core, name):
    pieces, half, cols = theirs.shape
    tr = min(half, 256)
    per = half // tr

    def body(core_ref, g_ref, t_ref, o_ref):
        del core_ref
        o_ref[...] = (g_ref[...].astype(F32) + t_ref[...].astype(F32)).astype(BF16)

    spec = pl.BlockSpec((1, tr, cols), lambda p, i, core_ref: (p, i, 0))
    return pl.pallas_call(
        body, name=name, out_shape=jax.ShapeDtypeStruct(theirs.shape, BF16),
        grid_spec=pltpu.PrefetchScalarGridSpec(
            num_scalar_prefetch=1, grid=(pieces, per),
            in_specs=[pl.BlockSpec((1, tr, cols), lambda p, i, core_ref: (p, core_ref[0] * per + i, 0)), spec],
            out_specs=spec),
        compiler_params=_params("parallel", "parallel"))(core, g, theirs)


def _scatter_plan(n):
    def plan(refs):
        parts, stacks = refs[:n], refs[n:]
        x, y, c = _place()
        copies = []
        for fx, fy in OTHER_CHIPS:
            chip = 2 * (x ^ fx) + (y ^ fy)
            for part, stack in zip(parts, stacks):
                if part.shape[0] == N_CHIPS:
                    piece = part.at[chip]
                else:
                    width = part.shape[2] // N_CHIPS
                    piece = part.at[0].at[:, pl.ds(pl.multiple_of(chip * width, 128), width)]
                copies.append((piece, stack.at[2 * x + y], (x ^ fx, y ^ fy, c)))
        return copies

    return plan


def _scatter_start(name, parts, after=None):
    def landing(a):
        return (N_CHIPS, a.shape[1], a.shape[2] if a.shape[0] == N_CHIPS else a.shape[2] // N_CHIPS)

    stacks = [lax.empty(landing(a), a.dtype) for a in parts]
    return _start_copies(name, list(parts) + stacks, _scatter_plan(len(parts)), 3 * len(parts), after)


def _scatter_finish(name, started, after):
    send_sem, recv_sem, arrays, _ = started
    n = len(arrays) // 2
    arrays = _wait_copies(name + "_wait", send_sem, recv_sem, arrays, _scatter_plan(n), after)
    return arrays[:n], arrays[n:]


def _sum_chips(part, stack, place, name):
    _, r, c = stack.shape
    tr = 256
    per = r // tr

    def body(place_ref, own_ref, a_ref, b_ref, c_ref, o_ref):
        del place_ref
        total = own_ref[0].astype(F32)
        for ref in (a_ref, b_ref, c_ref):
            total = total + ref[0].astype(F32)
        o_ref[...] = total

    if part.shape[0] == N_CHIPS:
        own = pl.BlockSpec((1, tr, c), lambda i, pr: (pr[1], i, 0))
    else:
        own = pl.BlockSpec((1, tr, c), lambda i, pr: (0, i, pr[1]))
    other = lambda flip: pl.BlockSpec((1, tr, c), lambda i, pr: (pr[1] ^ flip, i, 0))
    return pl.pallas_call(
        body, name=name, out_shape=jax.ShapeDtypeStruct((2 * r, c), F32),
        grid_spec=pltpu.PrefetchScalarGridSpec(
            num_scalar_prefetch=1, grid=(per,), in_specs=[own, other(2), other(1), other(3)],
            out_specs=pl.BlockSpec((tr, c), lambda i, pr: (pr[0] * per + i, 0))),
        compiler_params=_params("parallel"))(place, part, stack, stack, stack)


def _share_halves(fulls):
    n = len(fulls)

    def body(*refs):
        dst = refs[n:2 * n]
        send_sem, recv_sem = refs[2 * n:]
        x, y, c = _place()
        copies = []
        for i in range(n):
            mine = _half_rows(dst[i], c, dst[i].shape[0])
            copies.append(_push(mine, mine, send_sem.at[i], recv_sem.at[i], (x, y, 1 - c)))
        for cp in copies:
            cp.wait_recv()
            cp.wait_send()

    dma = pltpu.SemaphoreType.DMA
    return pl.pallas_call(
        body, name="share_halves", in_specs=[ANY] * n, out_specs=[ANY] * n,
        out_shape=[jax.ShapeDtypeStruct(a.shape, a.dtype) for a in fulls],
        input_output_aliases={i: i for i in range(n)}, scratch_shapes=[dma((n,)), dma((n,))],
        compiler_params=pltpu.CompilerParams(has_side_effects=True),
    )(*fulls)


MM = dict(tm=1024, tn=1024, tk=2048)
MM_LONG_K = dict(tm=1024, tn=1024, tk=4096)


def _local_step(x, target, ev_norm_w, q_norm_w, k_norm_w, sinks, own_first, weights_first, weights_late, emit):
    s = x.shape[0]
    cos_r, sin_r = _rope_tables(s, RET_DIM)
    cos_a, sin_a = _rope_tables(s, HEAD_DIM)
    cos_a = jnp.tile(cos_a, (1, 4))
    sins_a = jnp.tile(jnp.concatenate([-sin_a, sin_a], axis=1), (1, 2))
    tables = _retention_tables()
    ones_q, ones_kv = _block_diag_mean(GROUP_WIDTH), _block_diag_mean(KV_WIDTH)
    qw_g = jnp.tile(q_norm_w, (1, GROUP_WIDTH // HEAD_DIM))
    kw_kv = jnp.tile(k_norm_w, (1, KV_WIDTH // HEAD_DIM))
    sinks1 = sinks.reshape(Q_HEADS)

    own_w_in0, own_block, start_token = own_first
    h0, h0_t = _rmsnorm(x, ev_norm_w, "norm0", after=start_token)
    shifted = dict(shift=own_block * (own_w_in0.shape[1] // MM["tn"]), total=EVEN_IN // MM["tn"], tm=MM["tm"], tn=MM["tn"],
                   out_dtype=BF16)
    own_blocks = own_w_in0.shape[1] // MM["tn"]
    proj0 = _mm_shifted(h0, own_w_in0, b_shifted=False, first=0, count=own_blocks, name="proj0_own", **shifted)
    w_in0, conv_w, od_norm_w, token = weights_first(proj0)
    proj0 = _mm_shifted(h0, w_in0, b_shifted=True, first=own_blocks, count=shifted["total"] - own_blocks, name="proj0_rest",
                        into=proj0, after=token, **shifted)
    o_ret, cat, states = _retention_fwd(proj0, cos_r, sin_r, tables)
    cat = _conv_fwd(proj0, conv_w, cat)
    w_out0, w_in1, w_out1 = weights_late(cat)

    def residual_and_norm(prod, x_ref, w_ref, x1_ref, h1_ref, h1t_ref):
        x1v = x_ref[...] + prod
        x1_ref[...] = x1v
        rstd = lax.rsqrt(jnp.mean(x1v * x1v, axis=-1, keepdims=True) + EPS)
        h1v = (x1v * rstd * w_ref[...]).astype(BF16)
        h1_ref[...] = h1v
        h1t_ref[...] = h1v.T

    def residual_and_loss(prod, x1_ref, t_ref, dyb_ref, sq_ref):
        diff = (x1_ref[...] + prod) - t_ref[...]
        dyb_ref[...] = (diff * (1.0 / D_MODEL)).astype(BF16)

        @pl.when(pl.program_id(0) == 0)
        def _():
            sq_ref[...] = jnp.zeros_like(sq_ref)

        sq_ref[...] += jnp.sum(jnp.sum(diff * diff, axis=1, keepdims=True), axis=0, keepdims=True)

    act = lambda dt: jax.ShapeDtypeStruct((s, D_MODEL), dt)
    x1, h1, h1_t = _mm_rows(cat, w_out0, [x], [od_norm_w], [act(F32), act(BF16), jax.ShapeDtypeStruct((D_MODEL, s), BF16)],
                            residual_and_norm, tm=min(s, 512), name="out0")
    proj1 = _mm(h1, w_in1, mode="nn", out_dtype=BF16, name="proj1", tm=1024, tn=1536, tk=2048)
    qk = _qk_prep(proj1, qw_g, kw_kv, cos_a, sins_a, ones_q, ones_kv)
    ag, o_att = _swa_fwd(qk, proj1, sinks1)
    dy_b, sq = _mm_rows(ag, w_out1, [x1, target], [], [act(BF16), jax.ShapeDtypeStruct((8, 128), F32)],
                        residual_and_loss, tm=min(s, 256), name="out1")

    g_w_out1 = _mm(ag, dy_b, mode="tn", out_dtype=BF16, name="g_w_out1", **MM)
    dag = _mm(dy_b, w_out1, mode="nt", out_dtype=BF16, name="d_ag", **MM)
    dqr, dkc, dkp, dvc, dvp, dsink = _swa_bwd(qk, proj1, dag, sinks1)
    dproj1, dqw, dkw = _swa_bwd_finish(proj1, dqr, o_att, dag, dkc, dkp, dvc, dvp, qw_g, kw_kv, cos_a, sins_a, ones_q, ones_kv)
    g_w_in1 = _mm(h1_t, dproj1, mode="nn", out_dtype=BF16, name="g_w_in1", tm=1024, tn=768, tk=4096)
    token = emit("layer1", (("od_w_in", g_w_in1, "col"), ("od_w_out", g_w_out1, "row")))
    dh1 = _mm(dproj1, w_in1, mode="nt", out_dtype=BF16, name="d_h1", tm=1024, tn=1024, tk=ODD_IN, after=token)
    dx1_b, g_norm1 = _rmsnorm_bwd(x1, od_norm_w, dh1, dy_b, "norm1_bwd", BF16)

    g_w_out0 = _mm(cat, dx1_b, mode="tn", out_dtype=BF16, name="g_w_out0", **MM)
    token = emit("out0", (("ev_w_out", g_w_out0, "row"),))
    dcat = _mm(dx1_b, w_out0, mode="nt", out_dtype=BF16, name="d_cat", after=token, **MM)
    dproj0 = _retention_bwd(proj0, o_ret, states, dcat, cos_r, sin_r, tables)
    dproj0, g_conv = _conv_bwd(proj0, dcat, conv_w, dproj0)
    g_w_in0 = _mm(h0_t, dproj0, mode="nn", out_dtype=BF16, name="g_w_in0", **MM_LONG_K)
    token = emit("in0", (("ev_w_in", g_w_in0, "col"),))
    dh0 = _mm(dproj0, w_in0, mode="nt", out_dtype=BF16, name="d_h0", after=token, **MM_LONG_K)
    grad_x, g_norm0 = _rmsnorm_bwd(x, ev_norm_w, dh0, dx1_b, "norm0_bwd", F32)

    g_qw = dqw[:, 0, :].reshape(Q_HEADS, HEAD_DIM).sum(axis=0)
    g_kw = dkw[0].reshape(KV_HEADS, HEAD_DIM).sum(axis=0)
    g_sinks = dsink[:, :, 0].reshape(Q_HEADS)
    small = dict(ev_norm=g_norm0[0], od_norm=g_norm1[0], conv=g_conv[:3], qw=g_qw, kw=g_kw, sinks=g_sinks)
    return sq[0, 0], grad_x, small


def _pack_small_grads(small):
    pad = lambda v: jnp.pad(v, (0, D_MODEL - v.shape[0]))
    tail = pad(jnp.concatenate([small["qw"], small["kw"], small["sinks"]]))
    rows = [small["ev_norm"], small["od_norm"]] + [pad(small["conv"][t]) for t in range(3)] + [tail]
    rows += [jnp.zeros((D_MODEL,), F32)] * (8 - len(rows))
    return jnp.stack(rows)


class _ReduceScatter:
    def __init__(self, place):
        self.place = place
        self.started = []

    def send(self, tag, grads):
        pieces = [g[None] if kind == "col" else g.reshape(N_CHIPS, g.shape[0] // N_CHIPS, g.shape[1]) for _, g, kind in grads]
        theirs = _exchange_halves(pieces, "exchange_halves_" + tag)
        parts = [_pair_sum(g, t, self.place[:1], "pair_sum_" + nm) for g, t, (nm, _, _) in zip(pieces, theirs, grads)]
        started = _scatter_start("scatter_" + tag, parts)
        self.started.append((tag, [nm for nm, _, _ in grads], started))
        return started[3]

    def finish(self, after):
        names, sums = [], []
        for tag, group, started in self.started:
            parts, stacks = _scatter_finish("scatter_" + tag, started, after)
            sums += [_sum_chips(p, s, self.place, "chip_sum_" + nm) for p, s, nm in zip(parts, stacks, group)]
            names += group
        return dict(zip(names, _share_halves(sums)))


def kernel(x, ev_norm_w, ev_w_in, ev_conv_w, ev_w_out, od_norm_w, od_w_in, od_q_norm_w, od_k_norm_w, od_sinks, od_w_out, loss_target, m_ev_norm_w, m_ev_w_in, m_ev_conv_w, m_ev_w_out, m_od_norm_w, m_od_w_in, m_od_q_norm_w, m_od_k_norm_w, m_od_sinks, m_od_w_out, v_ev_norm_w, v_ev_w_in, v_ev_conv_w, v_ev_w_out, v_od_norm_w, v_od_w_in, v_od_q_norm_w, v_od_k_norm_w, v_od_sinks, v_od_w_out):
    my_chip = 2 * lax.axis_index("x") + lax.axis_index("y")
    place = jnp.stack([lax.axis_index("c"), my_chip]).astype(jnp.int32)
    shard_w = D_MODEL // N_CHIPS
    conv_shard = CONV_WIDTH // N_CHIPS

    small_in = jnp.zeros((8, shard_w), F32)
    small_in = small_in.at[0].set(od_norm_w[0]).at[1:4, :conv_shard].set(ev_conv_w[0])
    small_in = lax.dynamic_update_slice(jnp.zeros((N_CHIPS, 8, shard_w), F32), small_in[None], (my_chip, 0, 0))
    chip = place[1:]
    first_kinds, late_kinds = ("col", "whole"), ("row", "col", "row")
    w_in0_own_place, w_in0_shard = _cast_into_gathered(ev_w_in[0], "col", chip, "cast_w_in0", keep_shard=True)
    first = _gather_start("gather_first", [w_in0_own_place, small_in], first_kinds)
    late_own = [_cast_into_gathered(ev_w_out[0], "row", chip, "cast_w_out0"),
                _cast_into_gathered(od_w_in[0], "col", chip, "cast_w_in1"),
                _cast_into_gathered(od_w_out[0], "row", chip, "cast_w_out1")]
    late = []

    def weights_first(after):
        w_in0, small_all = _gather_finish("gather_first", first, first_kinds, after)
        late.append(_gather_start("gather_late", late_own, late_kinds, after=w_in0))
        od_norm_full = small_all[:, 0, :].reshape(1, D_MODEL)
        conv_full = jnp.transpose(small_all[:, 1:4, :conv_shard], (1, 0, 2)).reshape(3, CONV_WIDTH)
        return w_in0, conv_full, od_norm_full, late[0][3]

    def weights_late(after):
        return _gather_finish("gather_late", late[0], late_kinds, after)

    reduce_scatter = _ReduceScatter(place)
    sq, grad_x, small = _local_step(x[0], loss_target[0], ev_norm_w, od_q_norm_w, od_k_norm_w, od_sinks,
                                    (w_in0_shard, chip, first[3]), weights_first, weights_late, reduce_scatter.send)
    loss = lax.psum(0.5 * sq / D_MODEL, ("x", "y", "c"))

    big = reduce_scatter.finish(grad_x)
    g_ev_w_in, g_ev_w_out, g_od_w_in, g_od_w_out = big["ev_w_in"], big["ev_w_out"], big["od_w_in"], big["od_w_out"]
    tot = _allreduce_small(_pack_small_grads(small))
    g_ev_norm = tot[0:1]
    g_od_norm = lax.dynamic_slice(tot, (1, my_chip * shard_w), (1, shard_w))
    g_conv = lax.dynamic_slice(tot, (2, my_chip * conv_shard), (3, conv_shard))
    g_qw, g_kw, g_sinks = tot[5:6, 0:HEAD_DIM], tot[5:6, HEAD_DIM:2 * HEAD_DIM], tot[5:6, 2 * HEAD_DIM:2 * HEAD_DIM + Q_HEADS]

    upd = {}
    upd["ev_w_in"] = _adamw(ev_w_in[0], g_ev_w_in, m_ev_w_in[0], v_ev_w_in[0], "adamw_ev_w_in")
    upd["ev_w_out"] = _adamw(ev_w_out[0], g_ev_w_out, m_ev_w_out[0], v_ev_w_out[0], "adamw_ev_w_out")
    upd["od_w_in"] = _adamw(od_w_in[0], g_od_w_in, m_od_w_in[0], v_od_w_in[0], "adamw_od_w_in")
    upd["od_w_out"] = _adamw(od_w_out[0], g_od_w_out, m_od_w_out[0], v_od_w_out[0], "adamw_od_w_out")
    smalls = (("ev_norm_w", ev_norm_w, g_ev_norm, m_ev_norm_w, v_ev_norm_w),
              ("ev_conv_w", ev_conv_w, g_conv, m_ev_conv_w, v_ev_conv_w),
              ("od_norm_w", od_norm_w, g_od_norm, m_od_norm_w, v_od_norm_w),
              ("od_q_norm_w", od_q_norm_w, g_qw, m_od_q_norm_w, v_od_q_norm_w),
              ("od_k_norm_w", od_k_norm_w, g_kw, m_od_k_norm_w, v_od_k_norm_w),
              ("od_sinks", od_sinks, g_sinks, m_od_sinks, v_od_sinks))
    sizes = [w.size for _, w, _, _, _ in smalls]
    padded = 8 * 128 * math.ceil(sum(sizes) / (8 * 128))
    pack = lambda arrs, fill: jnp.concatenate(
        [a.reshape(-1) for a in arrs] + [jnp.full((padded - sum(sizes),), fill, F32)]).reshape(8, padded // 8)
    packed = _adamw(pack([w for _, w, _, _, _ in smalls], 0.0), pack([g for _, _, g, _, _ in smalls], 0.0),
                    pack([m for _, _, _, m, _ in smalls], 0.0), pack([v for _, _, _, _, v in smalls], 1.0), "adamw_small")
    offs = [sum(sizes[:i]) for i in range(len(sizes))]
    for (nm, w, _, _, _), off, size in zip(smalls, offs, sizes):
        upd[nm] = tuple(p.reshape(-1)[off:off + size].reshape(w.shape) for p in packed)
    for nm in ("ev_w_in", "ev_w_out", "od_w_in", "od_w_out"):
        upd[nm] = tuple(u[None] for u in upd[nm])
    order = ("ev_norm_w", "ev_w_in", "ev_conv_w", "ev_w_out", "od_norm_w", "od_w_in", "od_q_norm_w", "od_k_norm_w", "od_sinks", "od_w_out")
    return (loss, grad_x[None], *[upd[nm][0] for nm in order], *[upd[nm][1] for nm in order],
            *[upd[nm][2] for nm in order], *[upd[nm][3] for nm in order])
```

```python
import functools
import math

import jax
import jax.numpy as jnp
from jax import lax
from jax.experimental import pallas as pl
from jax.experimental.pallas import tpu as pltpu

F32 = jnp.float32
BF16 = jnp.bfloat16

D_MODEL = 2048
RET_HEADS = 4
RET_DIM = 256
RET_WIDTH = 1024
CONV_WIDTH = 1024
EVEN_IN = 8192
Q_HEADS = 32
HEAD_DIM = 64
KV_HEADS = 4
KV_WIDTH = 256
ATTN_WIDTH = 2048
ODD_IN = 4608
BLK = 128
ROPE_THETA = 10000.0
EPS = 1e-6
ADAM_LR = 0.001
ADAM_B1 = 0.9
ADAM_B2 = 0.999
ADAM_EPS = 1e-08
ADAM_WD = 0.01
ADAM_STEP = 10
N_CHIPS = 4
N_DEV = 8
VMEM_LIMIT_BYTES = 56 * 1024 * 1024
MESH = pl.DeviceIdType.MESH
ANY = pl.BlockSpec(memory_space=pl.ANY)


def _params(*sem):
    return pltpu.CompilerParams(dimension_semantics=sem, vmem_limit_bytes=VMEM_LIMIT_BYTES)


def _dot(a, b):
    return jnp.dot(a, b, preferred_element_type=F32)


def _dot_nt(a, b):
    return lax.dot_general(a, b, (((1,), (1,)), ((), ())), preferred_element_type=F32)


def _dot_tn(a, b):
    return lax.dot_general(a, b, (((0,), (0,)), ((), ())), preferred_element_type=F32)


def _sigmoid(x):
    return 1.0 / (1.0 + jnp.exp(-x))


def _mm(a, b, *, mode, tm, tn, tk, out_dtype, name, add=None, after=None):
    if mode == "nn":
        (m, k), n = a.shape, b.shape[1]
    elif mode == "nt":
        (m, k), n = a.shape, b.shape[0]
    else:
        (k, m), n = a.shape, b.shape[1]
    tm, tn, tk = min(tm, m), min(tn, n), min(tk, k)
    assert m % tm == 0 and n % tn == 0 and k % tk == 0, (name, m, n, k)
    nk = k // tk
    dot = {"nn": _dot, "nt": _dot_nt, "tn": _dot_tn}[mode]
    a_spec = (pl.BlockSpec((tk, tm), lambda i, j, kk: (kk, i)) if mode == "tn"
              else pl.BlockSpec((tm, tk), lambda i, j, kk: (i, kk)))
    b_spec = (pl.BlockSpec((tn, tk), lambda i, j, kk: (j, kk)) if mode == "nt"
              else pl.BlockSpec((tk, tn), lambda i, j, kk: (kk, j)))
    o_spec = pl.BlockSpec((tm, tn), lambda i, j, kk: (i, j))
    has_add = add is not None

    def body(*refs):
        a_ref, b_ref = refs[0], refs[1]
        add_ref = refs[2] if has_add else None
        o_ref, acc_ref = refs[-2], refs[-1]
        p = dot(a_ref[...], b_ref[...])

        def finish(total):
            if has_add:
                total = total + add_ref[...].astype(F32)
            o_ref[...] = total.astype(out_dtype)

        if nk == 1:
            finish(p)
        else:
            kk = pl.program_id(2)

            @pl.when(kk == 0)
            def _():
                acc_ref[...] = p

            @pl.when(jnp.logical_and(kk > 0, kk < nk - 1))
            def _():
                acc_ref[...] += p

            @pl.when(kk == nk - 1)
            def _():
                finish(acc_ref[...] + p)

    in_specs = [a_spec, b_spec] + ([o_spec] if has_add else []) + ([ANY] if after is not None else [])
    args = (a, b) + ((add,) if has_add else ()) + ((after,) if after is not None else ())
    return pl.pallas_call(
        body, name=name, grid=(m // tm, n // tn, nk), in_specs=in_specs, out_specs=o_spec,
        out_shape=jax.ShapeDtypeStruct((m, n), out_dtype),
        scratch_shapes=[pltpu.VMEM((tm, tn) if nk > 1 else (8, 128), F32)],
        compiler_params=_params("parallel", "parallel", "arbitrary"),
    )(*args)


def _mm_shifted(a, b, shift, *, b_shifted, first, count, total, tm, tn, out_dtype, name, into=None, after=None):
    m, k = a.shape
    tm = min(tm, m)
    assert m % tm == 0
    col = lambda j, shift_ref: (shift_ref[0] + first + j) % total
    extra = [arr for arr in (into, after) if arr is not None]

    def body(shift_ref, a_ref, b_ref, *rest):
        del shift_ref
        rest[-1][...] = _dot(a_ref[...], b_ref[...]).astype(out_dtype)

    return pl.pallas_call(
        body, name=name, out_shape=jax.ShapeDtypeStruct((m, total * tn), out_dtype),
        grid_spec=pltpu.PrefetchScalarGridSpec(
            num_scalar_prefetch=1, grid=(m // tm, count),
            in_specs=[pl.BlockSpec((tm, k), lambda i, j, s: (i, 0)),
                      pl.BlockSpec((k, tn), (lambda i, j, s: (0, col(j, s))) if b_shifted else (lambda i, j, s: (0, j)))]
            + [ANY] * len(extra),
            out_specs=pl.BlockSpec((tm, tn), lambda i, j, s: (i, col(j, s)))),
        input_output_aliases={3: 0} if into is not None else {},
        compiler_params=_params("parallel", "arbitrary"))(shift, a, b, *extra)


def _mm_rows(a, b, rows_in, vecs_in, out_shapes, epilogue, *, tm, name):
    m, k = a.shape
    n = b.shape[1]
    assert m % tm == 0
    row = pl.BlockSpec((tm, n), lambda i: (i, 0))

    def body(a_ref, b_ref, *rest):
        epilogue(_dot(a_ref[...], b_ref[...]), *rest)

    out_specs = [row if tuple(s.shape) == (m, n) else pl.BlockSpec(s.shape, lambda i: (0, 0)) for s in out_shapes]
    return pl.pallas_call(
        body, name=name, grid=(m // tm,),
        in_specs=[pl.BlockSpec((tm, k), lambda i: (i, 0)), pl.BlockSpec((k, n), lambda i: (0, 0))] + [row] * len(rows_in)
        + [pl.BlockSpec((1, n), lambda i: (0, 0))] * len(vecs_in),
        out_specs=out_specs, out_shape=out_shapes, compiler_params=_params("arbitrary"),
    )(a, b, *rows_in, *vecs_in)


def _cast_into_gathered(w, kind, chip, name, keep_shard=False):
    r, c = w.shape
    tr = min(r, 512)
    per = r // tr

    def body(chip_ref, w_ref, *outs):
        del chip_ref
        for o_ref in outs:
            o_ref[...] = w_ref[...].astype(BF16)

    if kind == "col":
        shape, out_map = (r, N_CHIPS * c), (lambda i, chip_ref: (i, chip_ref[0]))
    else:
        shape, out_map = (N_CHIPS * r, c), (lambda i, chip_ref: (chip_ref[0] * per + i, 0))
    plain = pl.BlockSpec((tr, c), lambda i, chip_ref: (i, 0))
    out = pl.pallas_call(
        body, name=name,
        out_shape=[jax.ShapeDtypeStruct(shape, BF16)] + ([jax.ShapeDtypeStruct((r, c), BF16)] if keep_shard else []),
        grid_spec=pltpu.PrefetchScalarGridSpec(
            num_scalar_prefetch=1, grid=(per,), in_specs=[plain],
            out_specs=[pl.BlockSpec((tr, c), out_map)] + ([plain] if keep_shard else [])),
        compiler_params=_params("parallel"))(chip, w)
    return out if keep_shard else out[0]


def _rmsnorm(x, w, name, after=None):
    s, d = x.shape
    tr = 256

    def body(x_ref, w_ref, *rest):
        xv = x_ref[...]
        rstd = lax.rsqrt(jnp.mean(xv * xv, axis=-1, keepdims=True) + EPS)
        rest[-1][...] = (xv * rstd * w_ref[...]).astype(BF16)

    return pl.pallas_call(
        body, name=name, grid=(s // tr,),
        in_specs=[pl.BlockSpec((tr, d), lambda i: (i, 0)), pl.BlockSpec((1, d), lambda i: (0, 0))]
        + ([ANY] if after is not None else []),
        out_specs=pl.BlockSpec((tr, d), lambda i: (i, 0)),
        out_shape=jax.ShapeDtypeStruct((s, d), BF16), compiler_params=_params("parallel"),
    )(x, w, *((after,) if after is not None else ()))


def _rmsnorm_bwd(x, w, dh, dres, name, out_dtype):
    s, d = x.shape
    tr = 256

    def body(x_ref, w_ref, dh_ref, dres_ref, dx_ref, dw_ref):
        xv = x_ref[...]
        rstd = lax.rsqrt(jnp.mean(xv * xv, axis=-1, keepdims=True) + EPS)
        nrm = xv * rstd
        dhv = dh_ref[...].astype(F32)
        dn = dhv * w_ref[...]
        dx = dres_ref[...].astype(F32) + rstd * (dn - nrm * jnp.mean(dn * nrm, axis=-1, keepdims=True))
        dx_ref[...] = dx.astype(out_dtype)

        @pl.when(pl.program_id(0) == 0)
        def _():
            dw_ref[...] = jnp.zeros_like(dw_ref)

        dw_ref[...] += jnp.sum(dhv * nrm, axis=0, keepdims=True)

    row = pl.BlockSpec((tr, d), lambda i: (i, 0))
    return pl.pallas_call(
        body, name=name, grid=(s // tr,),
        in_specs=[row, pl.BlockSpec((1, d), lambda i: (0, 0)), row, row],
        out_specs=[row, pl.BlockSpec((8, d), lambda i: (0, 0))],
        out_shape=[jax.ShapeDtypeStruct((s, d), out_dtype), jax.ShapeDtypeStruct((8, d), F32)],
        compiler_params=_params("arbitrary"))(x, w, dh, dres)


def _adamw(w, g, m, v, name):
    r, c = w.shape
    tr = min(r, 256)
    assert r % tr == 0

    def body(w_ref, g_ref, m_ref, v_ref, g_out, d_ref, nm_ref, nv_ref):
        gv = g_ref[...]
        g_out[...] = gv
        nm = ADAM_B1 * m_ref[...] + (1.0 - ADAM_B1) * gv
        nv = ADAM_B2 * v_ref[...] + (1.0 - ADAM_B2) * (gv * gv)
        m_hat = nm / (1.0 - ADAM_B1 ** ADAM_STEP)
        v_hat = nv / (1.0 - ADAM_B2 ** ADAM_STEP)
        d_ref[...] = -ADAM_LR * (m_hat / (jnp.sqrt(v_hat) + ADAM_EPS) + ADAM_WD * w_ref[...])
        nm_ref[...] = nm
        nv_ref[...] = nv

    spec = pl.BlockSpec((tr, c), lambda i: (i, 0))
    shp = jax.ShapeDtypeStruct((r, c), F32)
    return pl.pallas_call(body, name=name, grid=(r // tr,), in_specs=[spec] * 4, out_specs=[spec] * 4,
                          out_shape=[shp] * 4, compiler_params=_params("parallel"))(w, g, m, v)


def _rope_tables(s, dim):
    inv = 1.0 / (ROPE_THETA ** (jnp.arange(0, dim, 2, dtype=F32) / dim))
    ang = jnp.arange(s).astype(F32)[:, None] * inv[None, :]
    return jnp.cos(ang), jnp.sin(ang)


def _rope_half(x, cos, sin):
    h = x.shape[1] // 2
    x1, x2 = x[:, :h], x[:, h:]
    return jnp.concatenate([x1 * cos - x2 * sin, x2 * cos + x1 * sin], axis=1)


def _unrope_half(dy, cos, sin):
    h = dy.shape[1] // 2
    d1, d2 = dy[:, :h], dy[:, h:]
    return jnp.concatenate([d1 * cos + d2 * sin, d2 * cos - d1 * sin], axis=1)


def _lane(shape):
    return lax.broadcasted_iota(jnp.int32, shape, 1)


def _partner64(x):
    w = x.shape[1]
    first = (_lane(x.shape) % HEAD_DIM) < (HEAD_DIM // 2)
    return jnp.where(first, pltpu.roll(x, w - HEAD_DIM // 2, 1), pltpu.roll(x, HEAD_DIM // 2, 1))


def _tile_lanes(t, reps):
    return t if reps == 1 else jnp.concatenate([t] * reps, axis=1)


def _group_mean(x, ones_bd):
    hi = x.astype(BF16)
    lo = (x - hi.astype(F32)).astype(BF16)
    return _dot(hi, ones_bd) + _dot(lo, ones_bd)


def _block_diag_mean(width):
    idx = jnp.arange(width) // HEAD_DIM
    return jnp.where(idx[:, None] == idx[None, :], 1.0 / HEAD_DIM, 0.0).astype(BF16)


def _retention_tables():
    h = RET_HEADS
    log_g = jnp.log(1.0 - 2.0 ** (-5.0 - jnp.arange(h, dtype=F32)))
    idx = jnp.arange(BLK, dtype=F32)
    diff = idx[:, None] - idx[None, :]
    intra = jnp.where(diff >= 0, jnp.exp(log_g[:, None, None] * jnp.maximum(diff, 0.0)), 0.0).astype(F32)
    q_dec = jnp.exp(log_g[:, None] * (idx[None, :] + 1.0)).astype(F32)[:, :, None]
    k_dec = jnp.exp(log_g[:, None] * (BLK - 1.0 - idx[None, :])).astype(F32)[:, :, None]
    chunk_dec = jnp.exp(log_g * BLK).astype(F32)[:, None, None]
    return intra, q_dec, k_dec, chunk_dec


def _retention_fwd(proj, cos, sin, tables):
    s = proj.shape[0]
    nc = s // BLK
    intra, q_dec, k_dec, chunk_dec = tables

    def body(p_ref, cos_ref, sin_ref, in_ref, qd_ref, kd_ref, cd_ref, o_ref, cat_ref, st_ref, state):
        @pl.when(pl.program_id(0) == 0)
        def _():
            state[...] = jnp.zeros_like(state)

        cosv, sinv = cos_ref[...], sin_ref[...]
        for h in range(RET_HEADS):
            c0 = h * RET_DIM
            q = p_ref[:, c0:c0 + RET_DIM].astype(F32)
            k = p_ref[:, RET_WIDTH + c0:RET_WIDTH + c0 + RET_DIM].astype(F32)
            v = p_ref[:, 2 * RET_WIDTH + c0:2 * RET_WIDTH + c0 + RET_DIM]
            g = p_ref[:, 3 * RET_WIDTH + c0:3 * RET_WIDTH + c0 + RET_DIM].astype(F32)
            qb = _rope_half(q, cosv, sinv).astype(BF16)
            kr = _rope_half(k, cosv, sinv) * (RET_DIM ** -0.5)
            kb = kr.astype(BF16)
            scores = _dot_nt(qb, kb) * in_ref[h]
            inner = _dot(scores.astype(BF16), v)
            prev = state[h]
            prev_b = prev.astype(BF16)
            st_ref[h, 0] = prev_b
            o = inner + _dot(qb, prev_b) * qd_ref[h]
            o_ref[:, c0:c0 + RET_DIM] = o
            rstd = lax.rsqrt(jnp.mean(o * o, axis=-1, keepdims=True) + EPS)
            cat_ref[:, c0:c0 + RET_DIM] = (o * rstd * (g * _sigmoid(g))).astype(BF16)
            state[h] = cd_ref[h] * prev + _dot_tn((kr * kd_ref[h]).astype(BF16), v)

    full = lambda shape: pl.BlockSpec(shape, lambda n: (0,) * len(shape))
    return pl.pallas_call(
        body, name="retention_fwd", grid=(nc,),
        in_specs=[pl.BlockSpec((BLK, 4 * RET_WIDTH), lambda n: (n, 0)),
                  pl.BlockSpec((BLK, RET_DIM // 2), lambda n: (n, 0)), pl.BlockSpec((BLK, RET_DIM // 2), lambda n: (n, 0)),
                  full((RET_HEADS, BLK, BLK)), full((RET_HEADS, BLK, 1)), full((RET_HEADS, BLK, 1)), full((RET_HEADS, 1, 1))],
        out_specs=[pl.BlockSpec((BLK, RET_WIDTH), lambda n: (n, 0)), pl.BlockSpec((BLK, RET_WIDTH), lambda n: (n, 0)),
                   pl.BlockSpec((RET_HEADS, 1, RET_DIM, RET_DIM), lambda n: (0, n, 0, 0))],
        out_shape=[jax.ShapeDtypeStruct((s, RET_WIDTH), F32), jax.ShapeDtypeStruct((s, D_MODEL), BF16),
                   jax.ShapeDtypeStruct((RET_HEADS, nc, RET_DIM, RET_DIM), BF16)],
        scratch_shapes=[pltpu.VMEM((RET_HEADS, RET_DIM, RET_DIM), F32)],
        compiler_params=_params("arbitrary"))(proj, cos, sin, intra, q_dec, k_dec, chunk_dec)


def _retention_bwd(proj, o, states, dcat, cos, sin, tables):
    s = proj.shape[0]
    nc = s // BLK
    intra, q_dec, k_dec, chunk_dec = tables

    def body(p_ref, o_ref, st_ref, dc_ref, cos_ref, sin_ref, in_ref, qd_ref, kd_ref, cd_ref, dp_ref, dstate):
        @pl.when(pl.program_id(0) == 0)
        def _():
            dstate[...] = jnp.zeros_like(dstate)

        cosv, sinv = cos_ref[...], sin_ref[...]
        for h in range(RET_HEADS):
            c0 = h * RET_DIM
            q = p_ref[:, c0:c0 + RET_DIM].astype(F32)
            k = p_ref[:, RET_WIDTH + c0:RET_WIDTH + c0 + RET_DIM].astype(F32)
            v = p_ref[:, 2 * RET_WIDTH + c0:2 * RET_WIDTH + c0 + RET_DIM]
            g = p_ref[:, 3 * RET_WIDTH + c0:3 * RET_WIDTH + c0 + RET_DIM].astype(F32)
            o = o_ref[:, c0:c0 + RET_DIM]
            dc = dc_ref[:, c0:c0 + RET_DIM].astype(F32)
            rstd = lax.rsqrt(jnp.mean(o * o, axis=-1, keepdims=True) + EPS)
            nrm = o * rstd
            sg = _sigmoid(g)
            dg = dc * nrm * (sg * (1.0 + g * (1.0 - sg)))
            dn = dc * (g * sg)
            do = rstd * (dn - nrm * jnp.mean(dn * nrm, axis=-1, keepdims=True))
            qb = _rope_half(q, cosv, sinv).astype(BF16)
            kr = _rope_half(k, cosv, sinv) * (RET_DIM ** -0.5)
            kb = kr.astype(BF16)
            mask = in_ref[h]
            qd, kd = qd_ref[h], kd_ref[h]
            prev_b = st_ref[h, 0]
            dnext = dstate[h]
            dnext_b = dnext.astype(BF16)
            att = (_dot_nt(qb, kb) * mask).astype(BF16)
            do_b = do.astype(BF16)
            doq = (do * qd).astype(BF16)
            dv = _dot_tn(att, do_b) + _dot((kr * kd).astype(BF16), dnext_b)
            ds = (_dot_nt(do_b, v) * mask).astype(BF16)
            dqr = _dot(ds, kb) + _dot_nt(doq, prev_b)
            dkr = _dot_tn(ds, qb) + _dot_nt(v, dnext_b) * kd
            dstate[h] = cd_ref[h] * dnext + _dot_tn(qb, doq)
            dq = _unrope_half(dqr, cosv, sinv)
            dk = _unrope_half(dkr * (RET_DIM ** -0.5), cosv, sinv)
            dp_ref[:, c0:c0 + RET_DIM] = dq.astype(BF16)
            dp_ref[:, RET_WIDTH + c0:RET_WIDTH + c0 + RET_DIM] = dk.astype(BF16)
            dp_ref[:, 2 * RET_WIDTH + c0:2 * RET_WIDTH + c0 + RET_DIM] = dv.astype(BF16)
            dp_ref[:, 3 * RET_WIDTH + c0:3 * RET_WIDTH + c0 + RET_DIM] = dg.astype(BF16)

    rev = lambda n: nc - 1 - n
    full = lambda shape: pl.BlockSpec(shape, lambda n: (0,) * len(shape))
    return pl.pallas_call(
        body, name="retention_bwd", grid=(nc,),
        in_specs=[pl.BlockSpec((BLK, 4 * RET_WIDTH), lambda n: (rev(n), 0)),
                  pl.BlockSpec((BLK, RET_WIDTH), lambda n: (rev(n), 0)),
                  pl.BlockSpec((RET_HEADS, 1, RET_DIM, RET_DIM), lambda n: (0, rev(n), 0, 0)),
                  pl.BlockSpec((BLK, RET_WIDTH), lambda n: (rev(n), 0)),
                  pl.BlockSpec((BLK, RET_DIM // 2), lambda n: (rev(n), 0)), pl.BlockSpec((BLK, RET_DIM // 2), lambda n: (rev(n), 0)),
                  full((RET_HEADS, BLK, BLK)), full((RET_HEADS, BLK, 1)), full((RET_HEADS, BLK, 1)), full((RET_HEADS, 1, 1))],
        out_specs=pl.BlockSpec((BLK, 4 * RET_WIDTH), lambda n: (rev(n), 0)),
        out_shape=jax.ShapeDtypeStruct((s, EVEN_IN), BF16),
        scratch_shapes=[pltpu.VMEM((RET_HEADS, RET_DIM, RET_DIM), F32)],
        compiler_params=_params("arbitrary"))(proj, o, states, dcat, cos, sin, intra, q_dec, k_dec, chunk_dec)


CONV_ROWS = 256
HALO = 16


def _conv_pieces(p, halo, conv_w, first):
    rows = p.shape[0]
    gb, gc, u, gv = (p[:, i * CONV_WIDTH:(i + 1) * CONV_WIDTH] for i in range(4))
    cu = gc * u
    hcu = halo[:, CONV_WIDTH:2 * CONV_WIDTH] * halo[:, 2 * CONV_WIDTH:3 * CONV_WIDTH]
    hcu = jnp.where(first, 0.0, hcu)
    r1, r2 = hcu[HALO - 1:HALO], hcu[HALO - 2:HALO - 1]
    row = lax.broadcasted_iota(jnp.int32, cu.shape, 0)
    m1 = jnp.where(row == 0, r1, pltpu.roll(cu, 1, 0))
    m2 = jnp.where(row == 0, r2, jnp.where(row == 1, r1, pltpu.roll(cu, 2, 0)))
    del rows, conv_w
    return gb, gc, u, gv, cu, m1, m2


def _conv_fwd(proj, conv_w, cat):
    s = proj.shape[0]
    per = CONV_ROWS // HALO

    def body(p_ref, halo_ref, w_ref, cat_in, cat_ref):
        del cat_in
        first = pl.program_id(0) == 0
        gb, _, _, gv, cu, m1, m2 = _conv_pieces(p_ref[...].astype(F32), halo_ref[...].astype(F32), None, first)
        conv = w_ref[0:1, :] * m2 + w_ref[1:2, :] * m1 + w_ref[2:3, :] * cu
        cat_ref[...] = (gb * conv * (gv * _sigmoid(gv))).astype(BF16)

    return pl.pallas_call(
        body, name="conv_fwd", grid=(s // CONV_ROWS,),
        in_specs=[pl.BlockSpec((CONV_ROWS, 4 * CONV_WIDTH), lambda i: (i, 1)),
                  pl.BlockSpec((HALO, 4 * CONV_WIDTH), lambda i: (jnp.maximum(i * per - 1, 0), 1)),
                  pl.BlockSpec((3, CONV_WIDTH), lambda i: (0, 0)), ANY],
        out_specs=pl.BlockSpec((CONV_ROWS, CONV_WIDTH), lambda i: (i, 1)),
        out_shape=jax.ShapeDtypeStruct(cat.shape, cat.dtype), input_output_aliases={3: 0},
        compiler_params=_params("parallel"))(proj, proj, conv_w, cat)


def _conv_bwd(proj, dcat, conv_w, dproj):
    s = proj.shape[0]
    per = CONV_ROWS // HALO
    last_halo = s // HALO - 1
    nsteps = s // CONV_ROWS

    def body(p_ref, halo_ref, nxt_ref, dc_ref, dnxt_ref, w_ref, dp_in, dp_ref, dw_ref):
        del dp_in
        i = pl.program_id(0)
        gb, gc, u, gv, cu, m1, m2 = _conv_pieces(p_ref[...].astype(F32), halo_ref[...].astype(F32), None, i == 0)
        w0, w1, w2 = w_ref[0:1, :], w_ref[1:2, :], w_ref[2:3, :]
        conv = w0 * m2 + w1 * m1 + w2 * cu
        dco = dc_ref[...].astype(F32)
        sg = _sigmoid(gv)
        silu = gv * sg
        dgb = dco * conv * silu
        dgv = dco * gb * conv * (sg * (1.0 + gv * (1.0 - sg)))
        dconv = dco * gb * silu
        nxt = nxt_ref[...].astype(F32)
        ngv = nxt[:, 3 * CONV_WIDTH:]
        dnext = dnxt_ref[...].astype(F32) * nxt[:, :CONV_WIDTH] * (ngv * _sigmoid(ngv))
        dnext = jnp.where(i == nsteps - 1, 0.0, dnext)
        n1, n2 = dnext[0:1], dnext[1:2]
        row = lax.broadcasted_iota(jnp.int32, dconv.shape, 0)
        p1 = jnp.where(row == CONV_ROWS - 1, n1, pltpu.roll(dconv, CONV_ROWS - 1, 0))
        p2 = jnp.where(row == CONV_ROWS - 1, n2, jnp.where(row == CONV_ROWS - 2, n1, pltpu.roll(dconv, CONV_ROWS - 2, 0)))
        dcu = w2 * dconv + w1 * p1 + w0 * p2
        dp_ref[...] = jnp.concatenate([dgb, dcu * u, dcu * gc, dgv], axis=1).astype(BF16)

        @pl.when(i == 0)
        def _():
            dw_ref[...] = jnp.zeros_like(dw_ref)

        taps = [jnp.sum(dconv * m, axis=0, keepdims=True) for m in (m2, m1, cu)]
        r8 = lax.broadcasted_iota(jnp.int32, dw_ref.shape, 0)
        dw_ref[...] += jnp.where(r8 == 0, taps[0], jnp.where(r8 == 1, taps[1], jnp.where(r8 == 2, taps[2], 0.0)))

    return pl.pallas_call(
        body, name="conv_bwd", grid=(nsteps,),
        in_specs=[pl.BlockSpec((CONV_ROWS, 4 * CONV_WIDTH), lambda i: (i, 1)),
                  pl.BlockSpec((HALO, 4 * CONV_WIDTH), lambda i: (jnp.maximum(i * per - 1, 0), 1)),
                  pl.BlockSpec((HALO, 4 * CONV_WIDTH), lambda i: (jnp.minimum((i + 1) * per, last_halo), 1)),
                  pl.BlockSpec((CONV_ROWS, CONV_WIDTH), lambda i: (i, 1)),
                  pl.BlockSpec((HALO, CONV_WIDTH), lambda i: (jnp.minimum((i + 1) * per, last_halo), 1)),
                  pl.BlockSpec((3, CONV_WIDTH), lambda i: (0, 0)), ANY],
        out_specs=[pl.BlockSpec((CONV_ROWS, 4 * CONV_WIDTH), lambda i: (i, 1)), pl.BlockSpec((8, CONV_WIDTH), lambda i: (0, 0))],
        out_shape=[jax.ShapeDtypeStruct(dproj.shape, dproj.dtype), jax.ShapeDtypeStruct((8, CONV_WIDTH), F32)],
        input_output_aliases={6: 0},
        compiler_params=_params("arbitrary"))(proj, proj, proj, dcat, dcat, conv_w, dproj)


GROUP_WIDTH = 8 * HEAD_DIM
GROUP_HEADS = 8
SLAB = 128
Q_COL = 0
KV_COL = 4
GATE_COL = 5
K_SLAB0 = ATTN_WIDTH // SLAB
V_SLAB0 = (ATTN_WIDTH + KV_WIDTH) // SLAB
KV_ROWS = 512


def _half_mask(shape, which):
    return (_lane(shape) // HEAD_DIM) == which


def _dup_head(slab, which):
    kept = jnp.where(_half_mask(slab.shape, which), slab, 0.0)
    return kept + pltpu.roll(kept, HEAD_DIM, 1)


def _stack_heads(x):
    parts = []
    for sl in range(GROUP_WIDTH // SLAB):
        slab = x[:, sl * SLAB:(sl + 1) * SLAB]
        parts += [jnp.where(_half_mask(slab.shape, e), slab, 0.0) for e in range(2)]
    return jnp.concatenate(parts, axis=0)


def _unstack_heads(y):
    slabs = []
    for sl in range(GROUP_WIDTH // SLAB):
        a, b = y[(2 * sl) * BLK:(2 * sl + 1) * BLK], y[(2 * sl + 1) * BLK:(2 * sl + 2) * BLK]
        slabs.append(jnp.where(_half_mask(a.shape, 0), a, b))
    return jnp.concatenate(slabs, axis=1)


def _q_prep(q, qw, cosf, sins, ones_bd):
    rstd = lax.rsqrt(_group_mean(q * q, ones_bd) + EPS)
    nrm = q * rstd
    y = nrm * qw
    return nrm, rstd, y * cosf + _partner64(y) * sins


def _band(tri_ref, n):
    own = tri_ref[...] > 0.5
    return own, jnp.where(jnp.logical_and(n == 0, jnp.logical_not(own)), -1e30, 0.0)


def _fold(pair, own):
    return jnp.where(own, pair[:, BLK:], pair[:, :BLK])


def _unfold(folded, own):
    return jnp.concatenate([jnp.where(own, 0.0, folded), jnp.where(own, folded, 0.0)], axis=1)


def _head_probs(raw_scores, sink, own, bias):
    sc = _fold(raw_scores, own) * (HEAD_DIM ** -0.5) + bias
    m = jnp.maximum(jnp.max(sc, axis=-1, keepdims=True), sink)
    p = jnp.exp(sc - m)
    psink = jnp.exp(sink - m)
    inv = 1.0 / (jnp.sum(p, axis=-1, keepdims=True) + psink)
    return p * inv, psink * inv


def _k_prep(k, kw, cosf, sins, ones_bd):
    rstd = lax.rsqrt(_group_mean(k * k, ones_bd) + EPS)
    nrm = k * rstd
    y = nrm * kw
    return nrm, rstd, y * cosf + _partner64(y) * sins


def _qk_prep(proj, qw, kw, cos, sins, ones_q, ones_kv):
    s = proj.shape[0]
    rows = min(KV_ROWS, s)

    def body(p_ref, qw_ref, kw_ref, cos_ref, sin_ref, oq_ref, ok_ref, o_ref):
        j = pl.program_id(1)

        @pl.when(j < KV_COL)
        def _():
            cosf, sinf = _tile_lanes(cos_ref[...], 4), _tile_lanes(sin_ref[...], 4)
            o_ref[...] = _q_prep(p_ref[...].astype(F32), qw_ref[...], cosf, sinf, oq_ref[...])[2].astype(BF16)

        @pl.when(j == KV_COL)
        def _():
            cosf, sinf = _tile_lanes(cos_ref[...], 2), _tile_lanes(sin_ref[...], 2)
            kr = _k_prep(p_ref[:, :KV_WIDTH].astype(F32), kw_ref[...], cosf, sinf, ok_ref[...])[2]
            o_ref[...] = jnp.concatenate([kr.astype(BF16), p_ref[:, KV_WIDTH:]], axis=1)

    full = lambda shape: pl.BlockSpec(shape, lambda i, j: (0,) * len(shape))
    tab = pl.BlockSpec((rows, SLAB), lambda i, j: (i, 0))
    blk = pl.BlockSpec((rows, GROUP_WIDTH), lambda i, j: (i, j))
    return pl.pallas_call(
        body, name="qk_prep", grid=(s // rows, KV_COL + 1),
        in_specs=[blk, full((1, GROUP_WIDTH)), full((1, KV_WIDTH)), tab, tab, full((GROUP_WIDTH, GROUP_WIDTH)),
                  full((KV_WIDTH, KV_WIDTH))],
        out_specs=blk, out_shape=jax.ShapeDtypeStruct((s, ATTN_WIDTH + 2 * KV_WIDTH), BF16),
        compiler_params=_params("parallel", "arbitrary"))(proj, qw, kw, cos, sins, ones_q, ones_kv)


def _keys_values(kc_ref, kp_ref, vc_ref, vp_ref, head):
    lanes = slice((head // 2) * SLAB, (head // 2 + 1) * SLAB)
    dup = lambda ref: _dup_head(ref[:, lanes].astype(F32), head % 2)
    return (jnp.concatenate([dup(kp_ref), dup(kc_ref)], axis=0).astype(BF16),
            jnp.concatenate([dup(vp_ref), dup(vc_ref)], axis=0).astype(BF16))


FWD_STEP_HEADS = 4
BWD_STEP_HEADS = 2


def _swa_specs(heads):
    kv_width = heads * HEAD_DIM
    prev = lambda n: jnp.maximum(n - 1, 0)
    kv = lambda col0, row: pl.BlockSpec((BLK, kv_width), lambda gs, n: (row(n), col0 // kv_width + gs))
    cur = lambda n: n
    full = lambda shape: pl.BlockSpec(shape, lambda gs, n: (0,) * len(shape))
    gate = lambda t: pl.BlockSpec((BLK, GROUP_WIDTH), lambda gs, n: (n, GATE_COL + heads * gs + t))
    return dict(
        sinks=pl.BlockSpec(memory_space=pltpu.SMEM), gates=[gate(t) for t in range(heads)],
        kc=kv(ATTN_WIDTH, cur), kp=kv(ATTN_WIDTH, prev), vc=kv(ATTN_WIDTH + KV_WIDTH, cur), vp=kv(ATTN_WIDTH + KV_WIDTH, prev),
        tri=full((BLK, BLK)), step=pl.BlockSpec((BLK, heads * GROUP_WIDTH), lambda gs, n: (n, gs)))


def _lower_triangle():
    return jnp.tril(jnp.ones((BLK, BLK), F32))


def _swa_fwd(qk, proj, sinks):
    s = proj.shape[0]
    nb = s // BLK
    heads = FWD_STEP_HEADS
    sp = _swa_specs(heads)

    def body(sink_ref, q_ref, kc_ref, kp_ref, vc_ref, vp_ref, *rest):
        gate_refs, (tri_ref, ag_ref, o_ref) = rest[:heads], rest[heads:]
        gs, n = pl.program_id(0), pl.program_id(1)
        own, bias = _band(tri_ref, n)
        for t, gate_ref in enumerate(gate_refs):
            cols = slice(t * GROUP_WIDTH, (t + 1) * GROUP_WIDTH)
            first_head = (heads * gs + t) * GROUP_HEADS
            kcat, vcat = _keys_values(kc_ref, kp_ref, vc_ref, vp_ref, t)
            scores = _dot_nt(_stack_heads(q_ref[:, cols]), kcat)
            probs = []
            for j in range(GROUP_HEADS):
                p, _ = _head_probs(scores[j * BLK:(j + 1) * BLK], sink_ref[first_head + j], own, bias)
                probs.append(_unfold(p, own).astype(BF16))
            o = _unstack_heads(_dot(jnp.concatenate(probs, axis=0), vcat))
            gate = gate_ref[...].astype(F32)
            o_ref[:, cols] = o.astype(BF16)
            ag_ref[:, cols] = (o * (gate * _sigmoid(gate))).astype(BF16)

    shp = jax.ShapeDtypeStruct((s, ATTN_WIDTH), BF16)
    return pl.pallas_call(
        body, name="swa_fwd", grid=(KV_HEADS // heads, nb),
        in_specs=[sp["sinks"], sp["step"], sp["kc"], sp["kp"], sp["vc"], sp["vp"], *sp["gates"], sp["tri"]],
        out_specs=[sp["step"], sp["step"]], out_shape=[shp, shp],
        compiler_params=_params("parallel", "arbitrary"),
    )(sinks, qk, qk, qk, qk, qk, *[proj] * heads, _lower_triangle())


def _swa_bwd(qk, proj, dag, sinks):
    s = proj.shape[0]
    nb = s // BLK
    heads = BWD_STEP_HEADS
    sp = _swa_specs(heads)

    def body(sink_ref, q_ref, kc_ref, kp_ref, vc_ref, vp_ref, *rest):
        gate_refs = rest[:heads]
        dag_ref, tri_ref, dq_ref, dkc_ref, dkp_ref, dvc_ref, dvp_ref, dsink_ref = rest[heads:]
        gs, n = pl.program_id(0), pl.program_id(1)
        own, bias = _band(tri_ref, n)

        @pl.when(n == 0)
        def _():
            dsink_ref[...] = jnp.zeros_like(dsink_ref)

        for t, gate_ref in enumerate(gate_refs):
            cols = slice(t * GROUP_WIDTH, (t + 1) * GROUP_WIDTH)
            first_head = (heads * gs + t) * GROUP_HEADS
            kcat, vcat = _keys_values(kc_ref, kp_ref, vc_ref, vp_ref, t)
            gate = gate_ref[...].astype(F32)
            do = dag_ref[:, cols].astype(F32) * (gate * _sigmoid(gate))
            q_stack = _stack_heads(q_ref[:, cols])
            do_stack = _stack_heads(do).astype(BF16)
            scores = _dot_nt(q_stack, kcat)
            dprobs = _dot_nt(do_stack, vcat)
            probs, dscores, dsinks = [], [], []
            for j in range(GROUP_HEADS):
                rows = slice(j * BLK, (j + 1) * BLK)
                p, psink = _head_probs(scores[rows], sink_ref[first_head + j], own, bias)
                dp = _fold(dprobs[rows], own)
                delta = jnp.sum(p * dp, axis=-1, keepdims=True)
                probs.append(_unfold(p, own).astype(BF16))
                dscores.append(_unfold(p * (dp - delta) * (HEAD_DIM ** -0.5), own).astype(BF16))
                dsinks.append(-jnp.sum(psink * delta, axis=0, keepdims=True))
            ds = jnp.concatenate(dscores, axis=0)
            dk = _dot_tn(ds, q_stack)
            dv = _dot_tn(jnp.concatenate(probs, axis=0), do_stack)
            dk = dk + pltpu.roll(dk, HEAD_DIM, 1)
            dv = dv + pltpu.roll(dv, HEAD_DIM, 1)
            dkp_ref[t], dkc_ref[t] = dk[:BLK], dk[BLK:]
            dvp_ref[t], dvc_ref[t] = dv[:BLK], dv[BLK:]
            dq_ref[:, cols] = _unstack_heads(_dot(ds, kcat)).astype(BF16)
            r8 = lax.broadcasted_iota(jnp.int32, (8, SLAB), 0)
            upd = jnp.zeros((8, SLAB), F32)
            for j in range(GROUP_HEADS):
                upd = jnp.where(r8 == j, dsinks[j], upd)
            dsink_ref[t] += upd

    cur_out = pl.BlockSpec((heads, BLK, SLAB), lambda gs, n: (gs, n, 0))
    prev_out = pl.BlockSpec((heads, BLK, SLAB), lambda gs, n: (gs, (n + nb - 1) % nb, 0))
    kv_shape = jax.ShapeDtypeStruct((KV_HEADS, s, SLAB), F32)
    return pl.pallas_call(
        body, name="swa_bwd", grid=(KV_HEADS // heads, nb),
        in_specs=[sp["sinks"], sp["step"], sp["kc"], sp["kp"], sp["vc"], sp["vp"], *sp["gates"], sp["step"], sp["tri"]],
        out_specs=[sp["step"], cur_out, prev_out, cur_out, prev_out,
                   pl.BlockSpec((heads, 8, SLAB), lambda gs, n: (gs, 0, 0))],
        out_shape=[jax.ShapeDtypeStruct((s, ATTN_WIDTH), BF16), kv_shape, kv_shape, kv_shape, kv_shape,
                   jax.ShapeDtypeStruct((KV_HEADS, 8, SLAB), F32)],
        compiler_params=_params("parallel", "arbitrary"),
    )(sinks, qk, qk, qk, qk, qk, *[proj] * heads, dag, _lower_triangle())


def _swa_bwd_finish(proj, dqr, o, dag, dkc, dkp, dvc, dvp, qw, kw, cos, sins, ones_q, ones_kv):
    s = proj.shape[0]
    rows = min(KV_ROWS, s)
    n_q = KV_COL
    q_of = lambda j: jnp.clip(j - 1, 0, n_q - 1)
    gate_of = lambda j: jnp.clip(j - 1 - n_q, 0, n_q - 1)

    def body(kv_ref, p_ref, dqr_ref, o_ref, dag_ref, dkc_ref, dkp_ref, dvc_ref, dvp_ref, qw_ref, kw_ref, cos_ref, sin_ref,
             oq_ref, ok_ref, dp_ref, dqw_ref, dkw_ref):
        j, i = pl.program_id(0), pl.program_id(1)

        @pl.when(j == 0)
        def _():
            @pl.when(i == 0)
            def _():
                dkw_ref[...] = jnp.zeros_like(dkw_ref)

            def assemble(cur_ref, prv_ref):
                tot = [cur_ref[h] + prv_ref[h] for h in range(KV_HEADS)]
                first = _half_mask(tot[0].shape, 0)
                return jnp.concatenate([jnp.where(first, tot[0], tot[1]), jnp.where(first, tot[2], tot[3])], axis=1)

            dkr = assemble(dkc_ref, dkp_ref)
            dv = assemble(dvc_ref, dvp_ref)
            cosf, sinf = _tile_lanes(cos_ref[...], 2), _tile_lanes(sin_ref[...], 2)
            nrm, rstd, _ = _k_prep(kv_ref[:, :KV_WIDTH].astype(F32), kw_ref[...], cosf, sinf, ok_ref[...])
            dy = dkr * cosf + _partner64(dkr * sinf)
            dn = dy * kw_ref[...]
            dk = rstd * (dn - nrm * _group_mean(dn * nrm, ok_ref[...]))
            dp_ref[...] = jnp.concatenate([dk, dv], axis=1).astype(BF16)
            dkw_ref[...] += jnp.sum(dy * nrm, axis=0, keepdims=True)

        @pl.when(jnp.logical_and(j >= 1, j <= n_q))
        def _():
            @pl.when(i == 0)
            def _():
                dqw_ref[...] = jnp.zeros_like(dqw_ref)

            cosf, sinf = _tile_lanes(cos_ref[...], 4), _tile_lanes(sin_ref[...], 4)
            nrm, rstd, _ = _q_prep(p_ref[...].astype(F32), qw_ref[...], cosf, sinf, oq_ref[...])
            dq = dqr_ref[...].astype(F32)
            dy = dq * cosf + _partner64(dq * sinf)
            dn = dy * qw_ref[...]
            dp_ref[...] = (rstd * (dn - nrm * _group_mean(dn * nrm, oq_ref[...]))).astype(BF16)
            dqw_ref[0] += jnp.sum(dy * nrm, axis=0, keepdims=True)

        @pl.when(j > n_q)
        def _():
            gate = p_ref[...].astype(F32)
            sg = _sigmoid(gate)
            dp_ref[...] = (dag_ref[...].astype(F32) * o_ref[...].astype(F32) * (sg * (1.0 + gate * (1.0 - sg)))).astype(BF16)

    first_pass = lambda j, i: jnp.where(j == 0, i, 0)
    acc = pl.BlockSpec((KV_HEADS, rows, SLAB), lambda j, i: (0, first_pass(j, i), 0))
    full = lambda shape: pl.BlockSpec(shape, lambda j, i: (0,) * len(shape))
    tab = pl.BlockSpec((rows, SLAB), lambda j, i: (i, 0))
    out_col = lambda j: jnp.where(j == 0, KV_COL, jnp.where(j <= n_q, j - 1, j))
    return pl.pallas_call(
        body, name="swa_bwd_finish", grid=(2 * n_q + 1, s // rows),
        in_specs=[pl.BlockSpec((rows, GROUP_WIDTH), lambda j, i: (first_pass(j, i), KV_COL)),
                  pl.BlockSpec((rows, GROUP_WIDTH), lambda j, i: (jnp.where(j == 0, 0, i), jnp.where(j <= n_q, q_of(j), j))),
                  pl.BlockSpec((rows, GROUP_WIDTH), lambda j, i: (jnp.where(jnp.logical_and(j >= 1, j <= n_q), i, 0), q_of(j))),
                  pl.BlockSpec((rows, GROUP_WIDTH), lambda j, i: (jnp.where(j > n_q, i, 0), gate_of(j))),
                  pl.BlockSpec((rows, GROUP_WIDTH), lambda j, i: (jnp.where(j > n_q, i, 0), gate_of(j))),
                  acc, acc, acc, acc, full((1, GROUP_WIDTH)), full((1, KV_WIDTH)), tab, tab,
                  full((GROUP_WIDTH, GROUP_WIDTH)), full((KV_WIDTH, KV_WIDTH))],
        out_specs=[pl.BlockSpec((rows, GROUP_WIDTH), lambda j, i: (i, out_col(j))),
                   pl.BlockSpec((1, 8, GROUP_WIDTH), lambda j, i: (q_of(j), 0, 0)), pl.BlockSpec((8, KV_WIDTH), lambda j, i: (0, 0))],
        out_shape=[jax.ShapeDtypeStruct((s, ODD_IN), BF16), jax.ShapeDtypeStruct((n_q, 8, GROUP_WIDTH), F32),
                   jax.ShapeDtypeStruct((8, KV_WIDTH), F32)],
        compiler_params=_params("arbitrary", "arbitrary"),
    )(proj, proj, dqr, o, dag, dkc, dkp, dvc, dvp, qw, kw, cos, sins, ones_q, ones_kv)


def _place():
    x, y, c = lax.axis_index("x"), lax.axis_index("y"), lax.axis_index("c")
    return x, y, c


OTHER_CHIPS = ((1, 0), (0, 1), (1, 1))


def _half_rows(ref, half, rows):
    return ref.at[pl.ds(pl.multiple_of(half * (rows // 2), 8), rows // 2)]


DMA_CHUNK_BYTES = 1 << 20
BF16_TILE_ROWS = 16


def _n_chunks(ref):
    rows = ref.shape[-2]
    nbytes = math.prod(ref.shape) * jnp.dtype(ref.dtype).itemsize
    n = 1
    while 2 * n * DMA_CHUNK_BYTES <= nbytes and rows % (2 * n * BF16_TILE_ROWS) == 0:
        n *= 2
    return n


def _row_chunk(ref, k, n):
    rows = ref.shape[-2] // n
    return ref.at[pl.ds(k * rows, rows)] if len(ref.shape) == 2 else ref.at[:, pl.ds(k * rows, rows)]


def _push(src, dst, send_sem, recv_sem, device_id):
    n = _n_chunks(src)
    for k in range(n):
        pltpu.make_async_remote_copy(src_ref=_row_chunk(src, k, n), dst_ref=_row_chunk(dst, k, n), send_sem=send_sem,
                                     recv_sem=recv_sem, device_id=device_id, device_id_type=MESH).start()
    return pltpu.make_async_remote_copy(src_ref=src, dst_ref=dst, send_sem=send_sem, recv_sem=recv_sem,
                                        device_id=device_id, device_id_type=MESH)


def _copy(src, dst, sem):
    n = _n_chunks(src)
    for k in range(n):
        pltpu.make_async_copy(_row_chunk(src, k, n), _row_chunk(dst, k, n), sem).start()
    return pltpu.make_async_copy(src, dst, sem)


HBM = pl.BlockSpec(memory_space=pltpu.HBM)
SEM = pl.BlockSpec(memory_space=pltpu.SEMAPHORE)
SPLIT_COPY_EFFECT = pltpu.SideEffectType.DATAFLOW_SIDE_EFFECTING


def _in_hbm(a):
    return pltpu.with_memory_space_constraint(a, pltpu.HBM)


def _start_copies(name, arrays, plan, n_copies, after=None):
    n = len(arrays)

    def body(*refs):
        send_sem, recv_sem = refs[-n - 3], refs[-n - 2]
        for k, (src, dst, peer) in enumerate(plan(refs[:n])):
            _push(src, dst, send_sem.at[k], recv_sem.at[k], peer)
        refs[-1][...] = jnp.zeros_like(refs[-1])

    dma = pltpu.SemaphoreType.DMA((n_copies,))
    outs = pl.pallas_call(
        body, name=name,
        out_shape=(dma, dma, *[pltpu.HBM(a.shape, a.dtype) for a in arrays], jax.ShapeDtypeStruct((8, 128), F32)),
        in_specs=[HBM] * n + ([ANY] if after is not None else []),
        out_specs=(SEM, SEM, *[HBM] * n, pl.BlockSpec(memory_space=pltpu.VMEM)),
        input_output_aliases={i: i + 2 for i in range(n)},
        compiler_params=pltpu.CompilerParams(has_side_effects=SPLIT_COPY_EFFECT),
    )(*[_in_hbm(a) for a in arrays], *((after,) if after is not None else ()))
    return outs[0], outs[1], list(outs[2:2 + n]), outs[-1]


def _wait_copies(name, send_sem, recv_sem, arrays, plan, after):
    n = len(arrays)

    def body(*refs):
        send_ref, recv_ref = refs[n], refs[n + 1]
        for k, (src, dst, peer) in enumerate(plan(refs[:n])):
            cp = pltpu.make_async_remote_copy(src_ref=src, dst_ref=dst, send_sem=send_ref.at[k], recv_sem=recv_ref.at[k],
                                              device_id=peer, device_id_type=MESH)
            cp.wait_send()
            cp.wait_recv()

    return list(pl.pallas_call(
        body, name=name, out_shape=tuple(pltpu.HBM(a.shape, a.dtype) for a in arrays),
        in_specs=[HBM] * n + [SEM, SEM, ANY], out_specs=tuple([HBM] * n),
        input_output_aliases={i: i for i in range(n)},
        compiler_params=pltpu.CompilerParams(has_side_effects=SPLIT_COPY_EFFECT),
    )(*arrays, send_sem, recv_sem, after))


def _gather_region(full, kind, chip, half=None):
    if kind == "whole":
        return full.at[chip]
    if kind == "col":
        rows, width = full.shape[0], full.shape[1] // N_CHIPS
        piece = full.at[:, pl.ds(pl.multiple_of(chip * width, 128), width)]
    else:
        rows = full.shape[0] // N_CHIPS
        piece = full.at[pl.ds(pl.multiple_of(chip * rows, BF16_TILE_ROWS), rows)]
    return piece if half is None else _half_rows(piece, half, rows)


def _gather_plan(kinds):
    def plan(fulls):
        x, y, c = _place()
        copies = []
        for fx, fy in OTHER_CHIPS:
            for full, kind in zip(fulls, kinds):
                mine = _gather_region(full, kind, 2 * x + y, c)
                copies.append((mine, mine, (x ^ fx, y ^ fy, c)))
        return copies

    return plan


def _gather_start(name, fulls, kinds, after=None):
    return _start_copies(name, list(fulls), _gather_plan(kinds), 3 * len(kinds), after)


def _gather_finish(name, started, kinds, after):
    send_sem, recv_sem, fulls, _ = started
    n = len(kinds)
    fulls = _wait_copies(name + "_wait", send_sem, recv_sem, fulls, _gather_plan(kinds), after)
    split = [i for i, kind in enumerate(kinds) if kind != "whole"]

    def body(*refs):
        out = refs[n:2 * n]
        send, recv = refs[2 * n:]
        x, y, c = _place()
        pushes = []
        for r, (fx, fy) in enumerate(OTHER_CHIPS):
            chip = 2 * (x ^ fx) + (y ^ fy)
            for j, i in enumerate(split):
                landed = _gather_region(out[i], kinds[i], chip, c)
                pushes.append(_push(landed, landed, send.at[r * len(split) + j], recv.at[r * len(split) + j], (x, y, 1 - c)))
        for cp in pushes:
            cp.wait_recv()
        for cp in pushes:
            cp.wait_send()

    dma = pltpu.SemaphoreType.DMA
    return pl.pallas_call(
        body, name=name + "_share", in_specs=[ANY] * n, out_specs=[ANY] * n,
        out_shape=[jax.ShapeDtypeStruct(a.shape, a.dtype) for a in fulls],
        input_output_aliases={i: i for i in range(n)},
        scratch_shapes=[dma((3 * len(split),)), dma((3 * len(split),))],
        compiler_params=pltpu.CompilerParams(has_side_effects=True),
    )(*fulls)


def _allreduce_small(v):
    def body(v_ref, out_ref, buf, send_sems, recv_sems):
        x, y, c = _place()
        me = 4 * x + 2 * y + c
        buf[me] = v_ref[...]
        copies = []
        for r in range(1, N_DEV):
            peer = (x ^ (r >> 2), y ^ ((r >> 1) & 1), c ^ (r & 1))
            cp = pltpu.make_async_remote_copy(src_ref=v_ref, dst_ref=buf.at[me], send_sem=send_sems.at[r - 1],
                                              recv_sem=recv_sems.at[r - 1], device_id=peer, device_id_type=MESH)
            cp.start()
            copies.append(cp)
        for cp in copies:
            cp.wait_recv()
        for cp in copies:
            cp.wait_send()
        total = buf[0]
        for d in range(1, N_DEV):
            total = total + buf[d]
        out_ref[...] = total

    vm = pl.BlockSpec(memory_space=pltpu.VMEM)
    return pl.pallas_call(
        body, name="allreduce_small", in_specs=[vm], out_specs=vm, out_shape=jax.ShapeDtypeStruct(v.shape, v.dtype),
        scratch_shapes=[pltpu.VMEM((N_DEV,) + v.shape, v.dtype), pltpu.SemaphoreType.DMA((N_DEV - 1,)),
                        pltpu.SemaphoreType.DMA((N_DEV - 1,))],
        compiler_params=pltpu.CompilerParams(has_side_effects=True),
    )(v)


def _exchange_halves(grads, name):
    n = len(grads)

    def body(*refs):
        g, theirs = refs[:n], refs[n:2 * n]
        send_sem, recv_sem = refs[2 * n:]
        x, y, c = _place()
        copies = []
        for i in range(n):
            half = g[i].shape[1] // 2
            src = g[i].at[:, pl.ds(pl.multiple_of((1 - c) * half, BF16_TILE_ROWS), half)]
            copies.append(_push(src, theirs[i], send_sem.at[i], recv_sem.at[i], (x, y, 1 - c)))
        for cp in copies:
            cp.wait_recv()
            cp.wait_send()

    dma = pltpu.SemaphoreType.DMA
    return pl.pallas_call(
        body, name=name, in_specs=[ANY] * n, out_specs=[ANY] * n,
        out_shape=[jax.ShapeDtypeStruct((a.shape[0], a.shape[1] // 2, a.shape[2]), a.dtype) for a in grads],
        scratch_shapes=[dma((n,)), dma((n,))],
        compiler_params=pltpu.CompilerParams(has_side_effects=True),
    )(*grads)


def _pair_sum(g, theirs, core, name):
    pieces, half, cols = theirs.shape
    tr = min(half, 256)
    per = half // tr

    def body(core_ref, g_ref, t_ref, o_ref):
        del core_ref
        o_ref[...] = (g_ref[...].astype(F32) + t_ref[...].astype(F32)).astype(BF16)

    spec = pl.BlockSpec((1, tr, cols), lambda p, i, core_ref: (p, i, 0))
    return pl.pallas_call(
        body, name=name, out_shape=jax.ShapeDtypeStruct(theirs.shape, BF16),
        grid_spec=pltpu.PrefetchScalarGridSpec(
            num_scalar_prefetch=1, grid=(pieces, per),
            in_specs=[pl.BlockSpec((1, tr, cols), lambda p, i, core_ref: (p, core_ref[0] * per + i, 0)), spec],
            out_specs=spec),
        compiler_params=_params("parallel", "parallel"))(core, g, theirs)


def _scatter_plan(n):
    def plan(refs):
        parts, stacks = refs[:n], refs[n:]
        x, y, c = _place()
        copies = []
        for fx, fy in OTHER_CHIPS:
            chip = 2 * (x ^ fx) + (y ^ fy)
            for part, stack in zip(parts, stacks):
                if part.shape[0] == N_CHIPS:
                    piece = part.at[chip]
                else:
                    width = part.shape[2] // N_CHIPS
                    piece = part.at[0].at[:, pl.ds(pl.multiple_of(chip * width, 128), width)]
                copies.append((piece, stack.at[2 * x + y], (x ^ fx, y ^ fy, c)))
        return copies

    return plan


def _scatter_start(name, parts, after=None):
    def landing(a):
        return (N_CHIPS, a.shape[1], a.shape[2] if a.shape[0] == N_CHIPS else a.shape[2] // N_CHIPS)

    stacks = [lax.empty(landing(a), a.dtype) for a in parts]
    return _start_copies(name, list(parts) + stacks, _scatter_plan(len(parts)), 3 * len(parts), after)


def _scatter_finish(name, started, after):
    send_sem, recv_sem, arrays, _ = started
    n = len(arrays) // 2
    arrays = _wait_copies(name + "_wait", send_sem, recv_sem, arrays, _scatter_plan(n), after)
    return arrays[:n], arrays[n:]


def _sum_chips(part, stack, place, name):
    _, r, c = stack.shape
    tr = 256
    per = r // tr

    def body(place_ref, own_ref, a_ref, b_ref, c_ref, o_ref):
        del place_ref
        total = own_ref[0].astype(F32)
        for ref in (a_ref, b_ref, c_ref):
            total = total + ref[0].astype(F32)
        o_ref[...] = total

    if part.shape[0] == N_CHIPS:
        own = pl.BlockSpec((1, tr, c), lambda i, pr: (pr[1], i, 0))
    else:
        own = pl.BlockSpec((1, tr, c), lambda i, pr: (0, i, pr[1]))
    other = lambda flip: pl.BlockSpec((1, tr, c), lambda i, pr: (pr[1] ^ flip, i, 0))
    return pl.pallas_call(
        body, name=name, out_shape=jax.ShapeDtypeStruct((2 * r, c), F32),
        grid_spec=pltpu.PrefetchScalarGridSpec(
            num_scalar_prefetch=1, grid=(per,), in_specs=[own, other(2), other(1), other(3)],
            out_specs=pl.BlockSpec((tr, c), lambda i, pr: (pr[0] * per + i, 0))),
        compiler_params=_params("parallel"))(place, part, stack, stack, stack)


def _share_halves(fulls):
    n = len(fulls)

    def body(*refs):
        dst = refs[n:2 * n]
        send_sem, recv_sem = refs[2 * n:]
        x, y, c = _place()
        copies = []
        for i in range(n):
            mine = _half_rows(dst[i], c, dst[i].shape[0])
            copies.append(_push(mine, mine, send_sem.at[i], recv_sem.at[i], (x, y, 1 - c)))
        for cp in copies:
            cp.wait_recv()
            cp.wait_send()

    dma = pltpu.SemaphoreType.DMA
    return pl.pallas_call(
        body, name="share_halves", in_specs=[ANY] * n, out_specs=[ANY] * n,
        out_shape=[jax.ShapeDtypeStruct(a.shape, a.dtype) for a in fulls],
        input_output_aliases={i: i for i in range(n)}, scratch_shapes=[dma((n,)), dma((n,))],
        compiler_params=pltpu.CompilerParams(has_side_effects=True),
    )(*fulls)


MM = dict(tm=1024, tn=1024, tk=2048)
MM_LONG_K = dict(tm=1024, tn=1024, tk=4096)


def _local_step(x, target, ev_norm_w, q_norm_w, k_norm_w, sinks, own_first, weights_first, weights_late, emit):
    s = x.shape[0]
    cos_r, sin_r = _rope_tables(s, RET_DIM)
    cos_a, sin_a = _rope_tables(s, HEAD_DIM)
    cos_a = jnp.tile(cos_a, (1, 4))
    sins_a = jnp.tile(jnp.concatenate([-sin_a, sin_a], axis=1), (1, 2))
    tables = _retention_tables()
    ones_q, ones_kv = _block_diag_mean(GROUP_WIDTH), _block_diag_mean(KV_WIDTH)
    qw_g = jnp.tile(q_norm_w, (1, GROUP_WIDTH // HEAD_DIM))
    kw_kv = jnp.tile(k_norm_w, (1, KV_WIDTH // HEAD_DIM))
    sinks1 = sinks.reshape(Q_HEADS)

    own_w_in0, own_block, start_token = own_first
    h0 = _rmsnorm(x, ev_norm_w, "norm0", after=start_token)
    shifted = dict(shift=own_block * (own_w_in0.shape[1] // MM["tn"]), total=EVEN_IN // MM["tn"], tm=MM["tm"], tn=MM["tn"],
                   out_dtype=BF16)
    own_blocks = own_w_in0.shape[1] // MM["tn"]
    proj0 = _mm_shifted(h0, own_w_in0, b_shifted=False, first=0, count=own_blocks, name="proj0_own", **shifted)
    w_in0, conv_w, od_norm_w, token = weights_first(proj0)
    proj0 = _mm_shifted(h0, w_in0, b_shifted=True, first=own_blocks, count=shifted["total"] - own_blocks, name="proj0_rest",
                        into=proj0, after=token, **shifted)
    o_ret, cat, states = _retention_fwd(proj0, cos_r, sin_r, tables)
    cat = _conv_fwd(proj0, conv_w, cat)
    w_out0, w_in1, w_out1 = weights_late(cat)

    def residual_and_norm(prod, x_ref, w_ref, x1_ref, h1_ref):
        x1v = x_ref[...] + prod
        x1_ref[...] = x1v
        rstd = lax.rsqrt(jnp.mean(x1v * x1v, axis=-1, keepdims=True) + EPS)
        h1_ref[...] = (x1v * rstd * w_ref[...]).astype(BF16)

    def residual_and_loss(prod, x1_ref, t_ref, dyb_ref, sq_ref):
        diff = (x1_ref[...] + prod) - t_ref[...]
        dyb_ref[...] = (diff * (1.0 / D_MODEL)).astype(BF16)

        @pl.when(pl.program_id(0) == 0)
        def _():
            sq_ref[...] = jnp.zeros_like(sq_ref)

        sq_ref[...] += jnp.sum(jnp.sum(diff * diff, axis=1, keepdims=True), axis=0, keepdims=True)

    act = lambda dt: jax.ShapeDtypeStruct((s, D_MODEL), dt)
    x1, h1 = _mm_rows(cat, w_out0, [x], [od_norm_w], [act(F32), act(BF16)], residual_and_norm, tm=min(s, 512), name="out0")
    proj1 = _mm(h1, w_in1, mode="nn", out_dtype=BF16, name="proj1", tm=1024, tn=1536, tk=2048)
    qk = _qk_prep(proj1, qw_g, kw_kv, cos_a, sins_a, ones_q, ones_kv)
    ag, o_att = _swa_fwd(qk, proj1, sinks1)
    dy_b, sq = _mm_rows(ag, w_out1, [x1, target], [], [act(BF16), jax.ShapeDtypeStruct((8, 128), F32)],
                        residual_and_loss, tm=min(s, 256), name="out1")

    g_w_out1 = _mm(ag, dy_b, mode="tn", out_dtype=BF16, name="g_w_out1", **MM_LONG_K)
    dag = _mm(dy_b, w_out1, mode="nt", out_dtype=BF16, name="d_ag", **MM)
    dqr, dkc, dkp, dvc, dvp, dsink = _swa_bwd(qk, proj1, dag, sinks1)
    dproj1, dqw, dkw = _swa_bwd_finish(proj1, dqr, o_att, dag, dkc, dkp, dvc, dvp, qw_g, kw_kv, cos_a, sins_a, ones_q, ones_kv)
    g_w_in1 = _mm(h1, dproj1, mode="tn", out_dtype=BF16, name="g_w_in1", tm=1024, tn=768, tk=4096)
    token = emit("layer1", (("od_w_in", g_w_in1, "col"), ("od_w_out", g_w_out1, "row")))
    dh1 = _mm(dproj1, w_in1, mode="nt", out_dtype=BF16, name="d_h1", tm=1024, tn=1024, tk=ODD_IN, after=token)
    dx1_b, g_norm1 = _rmsnorm_bwd(x1, od_norm_w, dh1, dy_b, "norm1_bwd", BF16)

    g_w_out0 = _mm(cat, dx1_b, mode="tn", out_dtype=BF16, name="g_w_out0", **MM_LONG_K)
    token = emit("out0", (("ev_w_out", g_w_out0, "row"),))
    dcat = _mm(dx1_b, w_out0, mode="nt", out_dtype=BF16, name="d_cat", after=token, **MM)
    dproj0 = _retention_bwd(proj0, o_ret, states, dcat, cos_r, sin_r, tables)
    dproj0, g_conv = _conv_bwd(proj0, dcat, conv_w, dproj0)
    g_w_in0 = _mm(h0, dproj0, mode="tn", out_dtype=BF16, name="g_w_in0", **MM_LONG_K)
    token = emit("in0", (("ev_w_in", g_w_in0, "col"),))
    dh0 = _mm(dproj0, w_in0, mode="nt", out_dtype=BF16, name="d_h0", after=token, **MM_LONG_K)
    grad_x, g_norm0 = _rmsnorm_bwd(x, ev_norm_w, dh0, dx1_b, "norm0_bwd", F32)

    g_qw = dqw[:, 0, :].reshape(Q_HEADS, HEAD_DIM).sum(axis=0)
    g_kw = dkw[0].reshape(KV_HEADS, HEAD_DIM).sum(axis=0)
    g_sinks = dsink[:, :, 0].reshape(Q_HEADS)
    small = dict(ev_norm=g_norm0[0], od_norm=g_norm1[0], conv=g_conv[:3], qw=g_qw, kw=g_kw, sinks=g_sinks)
    return sq[0, 0], grad_x, small


def _pack_small_grads(small):
    pad = lambda v: jnp.pad(v, (0, D_MODEL - v.shape[0]))
    tail = pad(jnp.concatenate([small["qw"], small["kw"], small["sinks"]]))
    rows = [small["ev_norm"], small["od_norm"]] + [pad(small["conv"][t]) for t in range(3)] + [tail]
    rows += [jnp.zeros((D_MODEL,), F32)] * (8 - len(rows))
    return jnp.stack(rows)


class _ReduceScatter:
    def __init__(self, place):
        self.place = place
        self.started = []

    def send(self, tag, grads):
        pieces = [g[None] if kind == "col" else g.reshape(N_CHIPS, g.shape[0] // N_CHIPS, g.shape[1]) for _, g, kind in grads]
        theirs = _exchange_halves(pieces, "exchange_halves_" + tag)
        parts = [_pair_sum(g, t, self.place[:1], "pair_sum_" + nm) for g, t, (nm, _, _) in zip(pieces, theirs, grads)]
        started = _scatter_start("scatter_" + tag, parts)
        self.started.append((tag, [nm for nm, _, _ in grads], started))
        return started[3]

    def finish(self, after):
        names, sums = [], []
        for tag, group, started in self.started:
            parts, stacks = _scatter_finish("scatter_" + tag, started, after)
            sums += [_sum_chips(p, s, self.place, "chip_sum_" + nm) for p, s, nm in zip(parts, stacks, group)]
            names += group
        return dict(zip(names, _share_halves(sums)))


def kernel(x, ev_norm_w, ev_w_in, ev_conv_w, ev_w_out, od_norm_w, od_w_in, od_q_norm_w, od_k_norm_w, od_sinks, od_w_out, loss_target, m_ev_norm_w, m_ev_w_in, m_ev_conv_w, m_ev_w_out, m_od_norm_w, m_od_w_in, m_od_q_norm_w, m_od_k_norm_w, m_od_sinks, m_od_w_out, v_ev_norm_w, v_ev_w_in, v_ev_conv_w, v_ev_w_out, v_od_norm_w, v_od_w_in, v_od_q_norm_w, v_od_k_norm_w, v_od_sinks, v_od_w_out):
    my_chip = 2 * lax.axis_index("x") + lax.axis_index("y")
    place = jnp.stack([lax.axis_index("c"), my_chip]).astype(jnp.int32)
    shard_w = D_MODEL // N_CHIPS
    conv_shard = CONV_WIDTH // N_CHIPS

    small_in = jnp.zeros((8, shard_w), F32)
    small_in = small_in.at[0].set(od_norm_w[0]).at[1:4, :conv_shard].set(ev_conv_w[0])
    small_in = lax.dynamic_update_slice(jnp.zeros((N_CHIPS, 8, shard_w), F32), small_in[None], (my_chip, 0, 0))
    chip = place[1:]
    first_kinds, late_kinds = ("col", "whole"), ("row", "col", "row")
    w_in0_own_place, w_in0_shard = _cast_into_gathered(ev_w_in[0], "col", chip, "cast_w_in0", keep_shard=True)
    first = _gather_start("gather_first", [w_in0_own_place, small_in], first_kinds)
    late_own = [_cast_into_gathered(ev_w_out[0], "row", chip, "cast_w_out0"),
                _cast_into_gathered(od_w_in[0], "col", chip, "cast_w_in1"),
                _cast_into_gathered(od_w_out[0], "row", chip, "cast_w_out1")]
    late = []

    def weights_first(after):
        w_in0, small_all = _gather_finish("gather_first", first, first_kinds, after)
        late.append(_gather_start("gather_late", late_own, late_kinds, after=w_in0))
        od_norm_full = small_all[:, 0, :].reshape(1, D_MODEL)
        conv_full = jnp.transpose(small_all[:, 1:4, :conv_shard], (1, 0, 2)).reshape(3, CONV_WIDTH)
        return w_in0, conv_full, od_norm_full, late[0][3]

    def weights_late(after):
        return _gather_finish("gather_late", late[0], late_kinds, after)

    reduce_scatter = _ReduceScatter(place)
    sq, grad_x, small = _local_step(x[0], loss_target[0], ev_norm_w, od_q_norm_w, od_k_norm_w, od_sinks,
                                    (w_in0_shard, chip, first[3]), weights_first, weights_late, reduce_scatter.send)
    loss = lax.psum(0.5 * sq / D_MODEL, ("x", "y", "c"))

    big = reduce_scatter.finish(grad_x)
    g_ev_w_in, g_ev_w_out, g_od_w_in, g_od_w_out = big["ev_w_in"], big["ev_w_out"], big["od_w_in"], big["od_w_out"]
    tot = _allreduce_small(_pack_small_grads(small))
    g_ev_norm = tot[0:1]
    g_od_norm = lax.dynamic_slice(tot, (1, my_chip * shard_w), (1, shard_w))
    g_conv = lax.dynamic_slice(tot, (2, my_chip * conv_shard), (3, conv_shard))
    g_qw, g_kw, g_sinks = tot[5:6, 0:HEAD_DIM], tot[5:6, HEAD_DIM:2 * HEAD_DIM], tot[5:6, 2 * HEAD_DIM:2 * HEAD_DIM + Q_HEADS]

    upd = {}
    upd["ev_w_in"] = _adamw(ev_w_in[0], g_ev_w_in, m_ev_w_in[0], v_ev_w_in[0], "adamw_ev_w_in")
    upd["ev_w_out"] = _adamw(ev_w_out[0], g_ev_w_out, m_ev_w_out[0], v_ev_w_out[0], "adamw_ev_w_out")
    upd["od_w_in"] = _adamw(od_w_in[0], g_od_w_in, m_od_w_in[0], v_od_w_in[0], "adamw_od_w_in")
    upd["od_w_out"] = _adamw(od_w_out[0], g_od_w_out, m_od_w_out[0], v_od_w_out[0], "adamw_od_w_out")
    smalls = (("ev_norm_w", ev_norm_w, g_ev_norm, m_ev_norm_w, v_ev_norm_w),
              ("ev_conv_w", ev_conv_w, g_conv, m_ev_conv_w, v_ev_conv_w),
              ("od_norm_w", od_norm_w, g_od_norm, m_od_norm_w, v_od_norm_w),
              ("od_q_norm_w", od_q_norm_w, g_qw, m_od_q_norm_w, v_od_q_norm_w),
              ("od_k_norm_w", od_k_norm_w, g_kw, m_od_k_norm_w, v_od_k_norm_w),
              ("od_sinks", od_sinks, g_sinks, m_od_sinks, v_od_sinks))
    sizes = [w.size for _, w, _, _, _ in smalls]
    padded = 8 * 128 * math.ceil(sum(sizes) / (8 * 128))
    pack = lambda arrs, fill: jnp.concatenate(
        [a.reshape(-1) for a in arrs] + [jnp.full((padded - sum(sizes),), fill, F32)]).reshape(8, padded // 8)
    packed = _adamw(pack([w for _, w, _, _, _ in smalls], 0.0), pack([g for _, _, g, _, _ in smalls], 0.0),
                    pack([m for _, _, _, m, _ in smalls], 0.0), pack([v for _, _, _, _, v in smalls], 1.0), "adamw_small")
    offs = [sum(sizes[:i]) for i in range(len(sizes))]
    for (nm, w, _, _, _), off, size in zip(smalls, offs, sizes):
        upd[nm] = tuple(p.reshape(-1)[off:off + size].reshape(w.shape) for p in packed)
    for nm in ("ev_w_in", "ev_w_out", "od_w_in", "od_w_out"):
        upd[nm] = tuple(u[None] for u in upd[nm])
    order = ("ev_norm_w", "ev_w_in", "ev_conv_w", "ev_w_out", "od_norm_w", "od_w_in", "od_q_norm_w", "od_k_norm_w", "od_sinks", "od_w_out")
    return (loss, grad_x[None], *[upd[nm][0] for nm in order], *[upd[nm][1] for nm in order],
            *[upd[nm][2] for nm in order], *[upd[nm][3] for nm in order])
```

```python
import functools
import math

import jax
import jax.numpy as jnp
from jax import lax
from jax.experimental import pallas as pl
from jax.experimental.pallas import tpu as pltpu

F32 = jnp.float32
BF16 = jnp.bfloat16

D_MODEL = 2048
RET_HEADS = 4
RET_DIM = 256
RET_WIDTH = 1024
CONV_WIDTH = 1024
EVEN_IN = 8192
Q_HEADS = 32
HEAD_DIM = 64
KV_HEADS = 4
KV_WIDTH = 256
ATTN_WIDTH = 2048
ODD_IN = 4608
BLK = 128
ROPE_THETA = 10000.0
EPS = 1e-6
ADAM_LR = 0.001
ADAM_B1 = 0.9
ADAM_B2 = 0.999
ADAM_EPS = 1e-08
ADAM_WD = 0.01
ADAM_STEP = 10
N_CHIPS = 4
N_DEV = 8
VMEM_LIMIT_BYTES = 56 * 1024 * 1024
MESH = pl.DeviceIdType.MESH
ANY = pl.BlockSpec(memory_space=pl.ANY)


def _params(*sem):
    return pltpu.CompilerParams(dimension_semantics=sem, vmem_limit_bytes=VMEM_LIMIT_BYTES)


def _dot(a, b):
    return jnp.dot(a, b, preferred_element_type=F32)


def _dot_nt(a, b):
    return lax.dot_general(a, b, (((1,), (1,)), ((), ())), preferred_element_type=F32)


def _dot_tn(a, b):
    return lax.dot_general(a, b, (((0,), (0,)), ((), ())), preferred_element_type=F32)


def _sigmoid(x):
    return 1.0 / (1.0 + jnp.exp(-x))


def _mm(a, b, *, mode, tm, tn, tk, out_dtype, name, add=None, after=None):
    if mode == "nn":
        (m, k), n = a.shape, b.shape[1]
    elif mode == "nt":
        (m, k), n = a.shape, b.shape[0]
    else:
        (k, m), n = a.shape, b.shape[1]
    tm, tn, tk = min(tm, m), min(tn, n), min(tk, k)
    assert m % tm == 0 and n % tn == 0 and k % tk == 0, (name, m, n, k)
    nk = k // tk
    dot = {"nn": _dot, "nt": _dot_nt, "tn": _dot_tn}[mode]
    a_spec = (pl.BlockSpec((tk, tm), lambda i, j, kk: (kk, i)) if mode == "tn"
              else pl.BlockSpec((tm, tk), lambda i, j, kk: (i, kk)))
    b_spec = (pl.BlockSpec((tn, tk), lambda i, j, kk: (j, kk)) if mode == "nt"
              else pl.BlockSpec((tk, tn), lambda i, j, kk: (kk, j)))
    o_spec = pl.BlockSpec((tm, tn), lambda i, j, kk: (i, j))
    has_add = add is not None

    def body(*refs):
        a_ref, b_ref = refs[0], refs[1]
        add_ref = refs[2] if has_add else None
        o_ref, acc_ref = refs[-2], refs[-1]
        p = dot(a_ref[...], b_ref[...])

        def finish(total):
            if has_add:
                total = total + add_ref[...].astype(F32)
            o_ref[...] = total.astype(out_dtype)

        if nk == 1:
            finish(p)
        else:
            kk = pl.program_id(2)

            @pl.when(kk == 0)
            def _():
                acc_ref[...] = p

            @pl.when(jnp.logical_and(kk > 0, kk < nk - 1))
            def _():
                acc_ref[...] += p

            @pl.when(kk == nk - 1)
            def _():
                finish(acc_ref[...] + p)

    in_specs = [a_spec, b_spec] + ([o_spec] if has_add else []) + ([ANY] if after is not None else [])
    args = (a, b) + ((add,) if has_add else ()) + ((after,) if after is not None else ())
    return pl.pallas_call(
        body, name=name, grid=(m // tm, n // tn, nk), in_specs=in_specs, out_specs=o_spec,
        out_shape=jax.ShapeDtypeStruct((m, n), out_dtype),
        scratch_shapes=[pltpu.VMEM((tm, tn) if nk > 1 else (8, 128), F32)],
        compiler_params=_params("parallel", "parallel", "arbitrary"),
    )(*args)


def _mm_shifted(a, b, shift, *, b_shifted, first, count, total, tm, tn, out_dtype, name, into=None, after=None):
    m, k = a.shape
    tm = min(tm, m)
    assert m % tm == 0
    col = lambda j, shift_ref: (shift_ref[0] + first + j) % total
    extra = [arr for arr in (into, after) if arr is not None]

    def body(shift_ref, a_ref, b_ref, *rest):
        del shift_ref
        rest[-1][...] = _dot(a_ref[...], b_ref[...]).astype(out_dtype)

    return pl.pallas_call(
        body, name=name, out_shape=jax.ShapeDtypeStruct((m, total * tn), out_dtype),
        grid_spec=pltpu.PrefetchScalarGridSpec(
            num_scalar_prefetch=1, grid=(m // tm, count),
            in_specs=[pl.BlockSpec((tm, k), lambda i, j, s: (i, 0)),
                      pl.BlockSpec((k, tn), (lambda i, j, s: (0, col(j, s))) if b_shifted else (lambda i, j, s: (0, j)))]
            + [ANY] * len(extra),
            out_specs=pl.BlockSpec((tm, tn), lambda i, j, s: (i, col(j, s)))),
        input_output_aliases={3: 0} if into is not None else {},
        compiler_params=_params("parallel", "arbitrary"))(shift, a, b, *extra)


def _mm_rows(a, b, rows_in, vecs_in, out_shapes, epilogue, *, tm, name):
    m, k = a.shape
    n = b.shape[1]
    assert m % tm == 0
    row = pl.BlockSpec((tm, n), lambda i: (i, 0))

    def body(a_ref, b_ref, *rest):
        epilogue(_dot(a_ref[...], b_ref[...]), *rest)

    out_specs = [row if tuple(s.shape) == (m, n) else pl.BlockSpec(s.shape, lambda i: (0, 0)) for s in out_shapes]
    return pl.pallas_call(
        body, name=name, grid=(m // tm,),
        in_specs=[pl.BlockSpec((tm, k), lambda i: (i, 0)), pl.BlockSpec((k, n), lambda i: (0, 0))] + [row] * len(rows_in)
        + [pl.BlockSpec((1, n), lambda i: (0, 0))] * len(vecs_in),
        out_specs=out_specs, out_shape=out_shapes, compiler_params=_params("arbitrary"),
    )(a, b, *rows_in, *vecs_in)


def _cast_into_gathered(w, kind, chip, name, keep_shard=False):
    r, c = w.shape
    tr = min(r, 512)
    per = r // tr

    def body(chip_ref, w_ref, *outs):
        del chip_ref
        for o_ref in outs:
            o_ref[...] = w_ref[...].astype(BF16)

    if kind == "col":
        shape, out_map = (r, N_CHIPS * c), (lambda i, chip_ref: (i, chip_ref[0]))
    else:
        shape, out_map = (N_CHIPS * r, c), (lambda i, chip_ref: (chip_ref[0] * per + i, 0))
    plain = pl.BlockSpec((tr, c), lambda i, chip_ref: (i, 0))
    out = pl.pallas_call(
        body, name=name,
        out_shape=[jax.ShapeDtypeStruct(shape, BF16)] + ([jax.ShapeDtypeStruct((r, c), BF16)] if keep_shard else []),
        grid_spec=pltpu.PrefetchScalarGridSpec(
            num_scalar_prefetch=1, grid=(per,), in_specs=[plain],
            out_specs=[pl.BlockSpec((tr, c), out_map)] + ([plain] if keep_shard else [])),
        compiler_params=_params("parallel"))(chip, w)
    return out if keep_shard else out[0]


NORM_ROWS = 512


def _rmsnorm(x, w, name, after=None):
    s, d = x.shape
    tr = NORM_ROWS

    def body(x_ref, w_ref, *rest):
        xv = x_ref[...]
        rstd = lax.rsqrt(jnp.mean(xv * xv, axis=-1, keepdims=True) + EPS)
        rest[-1][...] = (xv * rstd * w_ref[...]).astype(BF16)

    return pl.pallas_call(
        body, name=name, grid=(s // tr,),
        in_specs=[pl.BlockSpec((tr, d), lambda i: (i, 0)), pl.BlockSpec((1, d), lambda i: (0, 0))]
        + ([ANY] if after is not None else []),
        out_specs=pl.BlockSpec((tr, d), lambda i: (i, 0)),
        out_shape=jax.ShapeDtypeStruct((s, d), BF16), compiler_params=_params("parallel"),
    )(x, w, *((after,) if after is not None else ()))


def _rmsnorm_bwd(x, w, dh, dres, name, out_dtype):
    s, d = x.shape
    tr = NORM_ROWS

    def body(x_ref, w_ref, dh_ref, dres_ref, dx_ref, dw_ref):
        xv = x_ref[...]
        rstd = lax.rsqrt(jnp.mean(xv * xv, axis=-1, keepdims=True) + EPS)
        nrm = xv * rstd
        dhv = dh_ref[...].astype(F32)
        dn = dhv * w_ref[...]
        dx = dres_ref[...].astype(F32) + rstd * (dn - nrm * jnp.mean(dn * nrm, axis=-1, keepdims=True))
        dx_ref[...] = dx.astype(out_dtype)

        @pl.when(pl.program_id(0) == 0)
        def _():
            dw_ref[...] = jnp.zeros_like(dw_ref)

        dw_ref[...] += jnp.sum(dhv * nrm, axis=0, keepdims=True)

    row = pl.BlockSpec((tr, d), lambda i: (i, 0))
    return pl.pallas_call(
        body, name=name, grid=(s // tr,),
        in_specs=[row, pl.BlockSpec((1, d), lambda i: (0, 0)), row, row],
        out_specs=[row, pl.BlockSpec((8, d), lambda i: (0, 0))],
        out_shape=[jax.ShapeDtypeStruct((s, d), out_dtype), jax.ShapeDtypeStruct((8, d), F32)],
        compiler_params=_params("arbitrary"))(x, w, dh, dres)


def _adamw(w, g, m, v, name):
    r, c = w.shape
    tr = min(r, 256)
    assert r % tr == 0

    def body(w_ref, g_ref, m_ref, v_ref, g_out, d_ref, nm_ref, nv_ref):
        gv = g_ref[...]
        g_out[...] = gv
        nm = ADAM_B1 * m_ref[...] + (1.0 - ADAM_B1) * gv
        nv = ADAM_B2 * v_ref[...] + (1.0 - ADAM_B2) * (gv * gv)
        m_hat = nm / (1.0 - ADAM_B1 ** ADAM_STEP)
        v_hat = nv / (1.0 - ADAM_B2 ** ADAM_STEP)
        d_ref[...] = -ADAM_LR * (m_hat / (jnp.sqrt(v_hat) + ADAM_EPS) + ADAM_WD * w_ref[...])
        nm_ref[...] = nm
        nv_ref[...] = nv

    spec = pl.BlockSpec((tr, c), lambda i: (i, 0))
    shp = jax.ShapeDtypeStruct((r, c), F32)
    return pl.pallas_call(body, name=name, grid=(r // tr,), in_specs=[spec] * 4, out_specs=[spec] * 4,
                          out_shape=[shp] * 4, compiler_params=_params("parallel"))(w, g, m, v)


def _rope_tables(s, dim):
    inv = 1.0 / (ROPE_THETA ** (jnp.arange(0, dim, 2, dtype=F32) / dim))
    ang = jnp.arange(s).astype(F32)[:, None] * inv[None, :]
    return jnp.cos(ang), jnp.sin(ang)


def _rope_half(x, cos, sin):
    h = x.shape[1] // 2
    x1, x2 = x[:, :h], x[:, h:]
    return jnp.concatenate([x1 * cos - x2 * sin, x2 * cos + x1 * sin], axis=1)


def _unrope_half(dy, cos, sin):
    h = dy.shape[1] // 2
    d1, d2 = dy[:, :h], dy[:, h:]
    return jnp.concatenate([d1 * cos + d2 * sin, d2 * cos - d1 * sin], axis=1)


def _lane(shape):
    return lax.broadcasted_iota(jnp.int32, shape, 1)


def _partner64(x):
    w = x.shape[1]
    first = (_lane(x.shape) % HEAD_DIM) < (HEAD_DIM // 2)
    return jnp.where(first, pltpu.roll(x, w - HEAD_DIM // 2, 1), pltpu.roll(x, HEAD_DIM // 2, 1))


def _tile_lanes(t, reps):
    return t if reps == 1 else jnp.concatenate([t] * reps, axis=1)


def _group_mean(x, ones_bd, passes=2):
    hi = x.astype(BF16)
    if passes == 1:
        return _dot(hi, ones_bd)
    lo = (x - hi.astype(F32)).astype(BF16)
    return _dot(hi, ones_bd) + _dot(lo, ones_bd)


def _block_diag_mean(width):
    idx = jnp.arange(width) // HEAD_DIM
    return jnp.where(idx[:, None] == idx[None, :], 1.0 / HEAD_DIM, 0.0).astype(BF16)


def _retention_tables():
    h = RET_HEADS
    log_g = jnp.log(1.0 - 2.0 ** (-5.0 - jnp.arange(h, dtype=F32)))
    idx = jnp.arange(BLK, dtype=F32)
    diff = idx[:, None] - idx[None, :]
    intra = jnp.where(diff >= 0, jnp.exp(log_g[:, None, None] * jnp.maximum(diff, 0.0)), 0.0).astype(F32)
    q_dec = jnp.exp(log_g[:, None] * (idx[None, :] + 1.0)).astype(F32)[:, :, None]
    k_dec = jnp.exp(log_g[:, None] * (BLK - 1.0 - idx[None, :])).astype(F32)[:, :, None]
    chunk_dec = jnp.exp(log_g * BLK).astype(F32)[:, None, None]
    return intra, q_dec, k_dec, chunk_dec


def _retention_fwd(proj, cos, sin, tables):
    s = proj.shape[0]
    nc = s // BLK
    intra, q_dec, k_dec, chunk_dec = tables

    def body(p_ref, cos_ref, sin_ref, in_ref, qd_ref, kd_ref, cd_ref, o_ref, cat_ref, st_ref, state):
        @pl.when(pl.program_id(0) == 0)
        def _():
            state[...] = jnp.zeros_like(state)

        cosv, sinv = cos_ref[...], sin_ref[...]
        for h in range(RET_HEADS):
            c0 = h * RET_DIM
            q = p_ref[:, c0:c0 + RET_DIM].astype(F32)
            k = p_ref[:, RET_WIDTH + c0:RET_WIDTH + c0 + RET_DIM].astype(F32)
            v = p_ref[:, 2 * RET_WIDTH + c0:2 * RET_WIDTH + c0 + RET_DIM]
            g = p_ref[:, 3 * RET_WIDTH + c0:3 * RET_WIDTH + c0 + RET_DIM].astype(F32)
            qb = _rope_half(q, cosv, sinv).astype(BF16)
            kr = _rope_half(k, cosv, sinv) * (RET_DIM ** -0.5)
            kb = kr.astype(BF16)
            scores = _dot_nt(qb, kb) * in_ref[h]
            inner = _dot(scores.astype(BF16), v)
            prev = state[h]
            prev_b = prev.astype(BF16)
            st_ref[h, 0] = prev_b
            o = inner + _dot(qb, prev_b) * qd_ref[h]
            o_ref[:, c0:c0 + RET_DIM] = o
            rstd = lax.rsqrt(jnp.mean(o * o, axis=-1, keepdims=True) + EPS)
            cat_ref[:, c0:c0 + RET_DIM] = (o * rstd * (g * _sigmoid(g))).astype(BF16)
            state[h] = cd_ref[h] * prev + _dot_tn((kr * kd_ref[h]).astype(BF16), v)

    full = lambda shape: pl.BlockSpec(shape, lambda n: (0,) * len(shape))
    return pl.pallas_call(
        body, name="retention_fwd", grid=(nc,),
        in_specs=[pl.BlockSpec((BLK, 4 * RET_WIDTH), lambda n: (n, 0)),
                  pl.BlockSpec((BLK, RET_DIM // 2), lambda n: (n, 0)), pl.BlockSpec((BLK, RET_DIM // 2), lambda n: (n, 0)),
                  full((RET_HEADS, BLK, BLK)), full((RET_HEADS, BLK, 1)), full((RET_HEADS, BLK, 1)), full((RET_HEADS, 1, 1))],
        out_specs=[pl.BlockSpec((BLK, RET_WIDTH), lambda n: (n, 0)), pl.BlockSpec((BLK, RET_WIDTH), lambda n: (n, 0)),
                   pl.BlockSpec((RET_HEADS, 1, RET_DIM, RET_DIM), lambda n: (0, n, 0, 0))],
        out_shape=[jax.ShapeDtypeStruct((s, RET_WIDTH), F32), jax.ShapeDtypeStruct((s, D_MODEL), BF16),
                   jax.ShapeDtypeStruct((RET_HEADS, nc, RET_DIM, RET_DIM), BF16)],
        scratch_shapes=[pltpu.VMEM((RET_HEADS, RET_DIM, RET_DIM), F32)],
        compiler_params=_params("arbitrary"))(proj, cos, sin, intra, q_dec, k_dec, chunk_dec)


def _retention_bwd(proj, o, states, dcat, cos, sin, tables):
    s = proj.shape[0]
    nc = s // BLK
    intra, q_dec, k_dec, chunk_dec = tables

    def body(p_ref, o_ref, st_ref, dc_ref, cos_ref, sin_ref, in_ref, qd_ref, kd_ref, cd_ref, dp_ref, dstate):
        @pl.when(pl.program_id(0) == 0)
        def _():
            dstate[...] = jnp.zeros_like(dstate)

        cosv, sinv = cos_ref[...], sin_ref[...]
        for h in range(RET_HEADS):
            c0 = h * RET_DIM
            q = p_ref[:, c0:c0 + RET_DIM].astype(F32)
            k = p_ref[:, RET_WIDTH + c0:RET_WIDTH + c0 + RET_DIM].astype(F32)
            v = p_ref[:, 2 * RET_WIDTH + c0:2 * RET_WIDTH + c0 + RET_DIM]
            g = p_ref[:, 3 * RET_WIDTH + c0:3 * RET_WIDTH + c0 + RET_DIM].astype(F32)
            o = o_ref[:, c0:c0 + RET_DIM]
            dc = dc_ref[:, c0:c0 + RET_DIM].astype(F32)
            rstd = lax.rsqrt(jnp.mean(o * o, axis=-1, keepdims=True) + EPS)
            nrm = o * rstd
            sg = _sigmoid(g)
            dg = dc * nrm * (sg * (1.0 + g * (1.0 - sg)))
            dn = dc * (g * sg)
            do = rstd * (dn - nrm * jnp.mean(dn * nrm, axis=-1, keepdims=True))
            qb = _rope_half(q, cosv, sinv).astype(BF16)
            kr = _rope_half(k, cosv, sinv) * (RET_DIM ** -0.5)
            kb = kr.astype(BF16)
            mask = in_ref[h]
            qd, kd = qd_ref[h], kd_ref[h]
            prev_b = st_ref[h, 0]
            dnext = dstate[h]
            dnext_b = dnext.astype(BF16)
            att = (_dot_nt(qb, kb) * mask).astype(BF16)
            do_b = do.astype(BF16)
            doq = (do * qd).astype(BF16)
            dv = _dot_tn(att, do_b) + _dot((kr * kd).astype(BF16), dnext_b)
            ds = (_dot_nt(do_b, v) * mask).astype(BF16)
            dqr = _dot(ds, kb) + _dot_nt(doq, prev_b)
            dkr = _dot_tn(ds, qb) + _dot_nt(v, dnext_b) * kd
            dstate[h] = cd_ref[h] * dnext + _dot_tn(qb, doq)
            dq = _unrope_half(dqr, cosv, sinv)
            dk = _unrope_half(dkr * (RET_DIM ** -0.5), cosv, sinv)
            dp_ref[:, c0:c0 + RET_DIM] = dq.astype(BF16)
            dp_ref[:, RET_WIDTH + c0:RET_WIDTH + c0 + RET_DIM] = dk.astype(BF16)
            dp_ref[:, 2 * RET_WIDTH + c0:2 * RET_WIDTH + c0 + RET_DIM] = dv.astype(BF16)
            dp_ref[:, 3 * RET_WIDTH + c0:3 * RET_WIDTH + c0 + RET_DIM] = dg.astype(BF16)

    rev = lambda n: nc - 1 - n
    full = lambda shape: pl.BlockSpec(shape, lambda n: (0,) * len(shape))
    return pl.pallas_call(
        body, name="retention_bwd", grid=(nc,),
        in_specs=[pl.BlockSpec((BLK, 4 * RET_WIDTH), lambda n: (rev(n), 0)),
                  pl.BlockSpec((BLK, RET_WIDTH), lambda n: (rev(n), 0)),
                  pl.BlockSpec((RET_HEADS, 1, RET_DIM, RET_DIM), lambda n: (0, rev(n), 0, 0)),
                  pl.BlockSpec((BLK, RET_WIDTH), lambda n: (rev(n), 0)),
                  pl.BlockSpec((BLK, RET_DIM // 2), lambda n: (rev(n), 0)), pl.BlockSpec((BLK, RET_DIM // 2), lambda n: (rev(n), 0)),
                  full((RET_HEADS, BLK, BLK)), full((RET_HEADS, BLK, 1)), full((RET_HEADS, BLK, 1)), full((RET_HEADS, 1, 1))],
        out_specs=pl.BlockSpec((BLK, 4 * RET_WIDTH), lambda n: (rev(n), 0)),
        out_shape=jax.ShapeDtypeStruct((s, EVEN_IN), BF16),
        scratch_shapes=[pltpu.VMEM((RET_HEADS, RET_DIM, RET_DIM), F32)],
        compiler_params=_params("arbitrary"))(proj, o, states, dcat, cos, sin, intra, q_dec, k_dec, chunk_dec)


CONV_ROWS = 256
HALO = 16


def _conv_pieces(p, halo, conv_w, first):
    rows = p.shape[0]
    gb, gc, u, gv = (p[:, i * CONV_WIDTH:(i + 1) * CONV_WIDTH] for i in range(4))
    cu = gc * u
    hcu = halo[:, CONV_WIDTH:2 * CONV_WIDTH] * halo[:, 2 * CONV_WIDTH:3 * CONV_WIDTH]
    hcu = jnp.where(first, 0.0, hcu)
    r1, r2 = hcu[HALO - 1:HALO], hcu[HALO - 2:HALO - 1]
    row = lax.broadcasted_iota(jnp.int32, cu.shape, 0)
    m1 = jnp.where(row == 0, r1, pltpu.roll(cu, 1, 0))
    m2 = jnp.where(row == 0, r2, jnp.where(row == 1, r1, pltpu.roll(cu, 2, 0)))
    del rows, conv_w
    return gb, gc, u, gv, cu, m1, m2


def _conv_fwd(proj, conv_w, cat):
    s = proj.shape[0]
    per = CONV_ROWS // HALO

    def body(p_ref, halo_ref, w_ref, cat_in, cat_ref):
        del cat_in
        first = pl.program_id(0) == 0
        gb, _, _, gv, cu, m1, m2 = _conv_pieces(p_ref[...].astype(F32), halo_ref[...].astype(F32), None, first)
        conv = w_ref[0:1, :] * m2 + w_ref[1:2, :] * m1 + w_ref[2:3, :] * cu
        cat_ref[...] = (gb * conv * (gv * _sigmoid(gv))).astype(BF16)

    return pl.pallas_call(
        body, name="conv_fwd", grid=(s // CONV_ROWS,),
        in_specs=[pl.BlockSpec((CONV_ROWS, 4 * CONV_WIDTH), lambda i: (i, 1)),
                  pl.BlockSpec((HALO, 4 * CONV_WIDTH), lambda i: (jnp.maximum(i * per - 1, 0), 1)),
                  pl.BlockSpec((3, CONV_WIDTH), lambda i: (0, 0)), ANY],
        out_specs=pl.BlockSpec((CONV_ROWS, CONV_WIDTH), lambda i: (i, 1)),
        out_shape=jax.ShapeDtypeStruct(cat.shape, cat.dtype), input_output_aliases={3: 0},
        compiler_params=_params("parallel"))(proj, proj, conv_w, cat)


def _conv_bwd(proj, dcat, conv_w, dproj):
    s = proj.shape[0]
    per = CONV_ROWS // HALO
    last_halo = s // HALO - 1
    nsteps = s // CONV_ROWS

    def body(p_ref, halo_ref, nxt_ref, dc_ref, dnxt_ref, w_ref, dp_in, dp_ref, dw_ref):
        del dp_in
        i = pl.program_id(0)
        gb, gc, u, gv, cu, m1, m2 = _conv_pieces(p_ref[...].astype(F32), halo_ref[...].astype(F32), None, i == 0)
        w0, w1, w2 = w_ref[0:1, :], w_ref[1:2, :], w_ref[2:3, :]
        conv = w0 * m2 + w1 * m1 + w2 * cu
        dco = dc_ref[...].astype(F32)
        sg = _sigmoid(gv)
        silu = gv * sg
        dgb = dco * conv * silu
        dgv = dco * gb * conv * (sg * (1.0 + gv * (1.0 - sg)))
        dconv = dco * gb * silu
        nxt = nxt_ref[...].astype(F32)
        ngv = nxt[:, 3 * CONV_WIDTH:]
        dnext = dnxt_ref[...].astype(F32) * nxt[:, :CONV_WIDTH] * (ngv * _sigmoid(ngv))
        dnext = jnp.where(i == nsteps - 1, 0.0, dnext)
        n1, n2 = dnext[0:1], dnext[1:2]
        row = lax.broadcasted_iota(jnp.int32, dconv.shape, 0)
        p1 = jnp.where(row == CONV_ROWS - 1, n1, pltpu.roll(dconv, CONV_ROWS - 1, 0))
        p2 = jnp.where(row == CONV_ROWS - 1, n2, jnp.where(row == CONV_ROWS - 2, n1, pltpu.roll(dconv, CONV_ROWS - 2, 0)))
        dcu = w2 * dconv + w1 * p1 + w0 * p2
        dp_ref[...] = jnp.concatenate([dgb, dcu * u, dcu * gc, dgv], axis=1).astype(BF16)

        @pl.when(i == 0)
        def _():
            dw_ref[...] = jnp.zeros_like(dw_ref)

        taps = [jnp.sum(dconv * m, axis=0, keepdims=True) for m in (m2, m1, cu)]
        r8 = lax.broadcasted_iota(jnp.int32, dw_ref.shape, 0)
        dw_ref[...] += jnp.where(r8 == 0, taps[0], jnp.where(r8 == 1, taps[1], jnp.where(r8 == 2, taps[2], 0.0)))

    return pl.pallas_call(
        body, name="conv_bwd", grid=(nsteps,),
        in_specs=[pl.BlockSpec((CONV_ROWS, 4 * CONV_WIDTH), lambda i: (i, 1)),
                  pl.BlockSpec((HALO, 4 * CONV_WIDTH), lambda i: (jnp.maximum(i * per - 1, 0), 1)),
                  pl.BlockSpec((HALO, 4 * CONV_WIDTH), lambda i: (jnp.minimum((i + 1) * per, last_halo), 1)),
                  pl.BlockSpec((CONV_ROWS, CONV_WIDTH), lambda i: (i, 1)),
                  pl.BlockSpec((HALO, CONV_WIDTH), lambda i: (jnp.minimum((i + 1) * per, last_halo), 1)),
                  pl.BlockSpec((3, CONV_WIDTH), lambda i: (0, 0)), ANY],
        out_specs=[pl.BlockSpec((CONV_ROWS, 4 * CONV_WIDTH), lambda i: (i, 1)), pl.BlockSpec((8, CONV_WIDTH), lambda i: (0, 0))],
        out_shape=[jax.ShapeDtypeStruct(dproj.shape, dproj.dtype), jax.ShapeDtypeStruct((8, CONV_WIDTH), F32)],
        input_output_aliases={6: 0},
        compiler_params=_params("arbitrary"))(proj, proj, proj, dcat, dcat, conv_w, dproj)


GROUP_WIDTH = 8 * HEAD_DIM
GROUP_HEADS = 8
SLAB = 128
Q_COL = 0
KV_COL = 4
GATE_COL = 5
K_SLAB0 = ATTN_WIDTH // SLAB
V_SLAB0 = (ATTN_WIDTH + KV_WIDTH) // SLAB
KV_ROWS = 1024


def _half_mask(shape, which):
    return (_lane(shape) // HEAD_DIM) == which


def _dup_head(slab, which):
    kept = jnp.where(_half_mask(slab.shape, which), slab, 0.0)
    return kept + pltpu.roll(kept, HEAD_DIM, 1)


def _stack_heads(x):
    parts = []
    for sl in range(GROUP_WIDTH // SLAB):
        slab = x[:, sl * SLAB:(sl + 1) * SLAB]
        parts += [jnp.where(_half_mask(slab.shape, e), slab, 0.0) for e in range(2)]
    return jnp.concatenate(parts, axis=0)


def _unstack_heads(y):
    slabs = []
    for sl in range(GROUP_WIDTH // SLAB):
        a, b = y[(2 * sl) * BLK:(2 * sl + 1) * BLK], y[(2 * sl + 1) * BLK:(2 * sl + 2) * BLK]
        slabs.append(jnp.where(_half_mask(a.shape, 0), a, b))
    return jnp.concatenate(slabs, axis=1)


def _q_prep(q, qw, cosf, sins, ones_bd):
    rstd = lax.rsqrt(_group_mean(q * q, ones_bd) + EPS)
    nrm = q * rstd
    y = nrm * qw
    return nrm, rstd, y * cosf + _partner64(y) * sins


def _band(tri_ref, n):
    own = tri_ref[...] > 0.5
    return own, jnp.where(jnp.logical_and(n == 0, jnp.logical_not(own)), -1e30, 0.0)


def _fold(pair, own):
    return jnp.where(own, pair[:, BLK:], pair[:, :BLK])


def _unfold(folded, own):
    return jnp.concatenate([jnp.where(own, 0.0, folded), jnp.where(own, folded, 0.0)], axis=1)


def _head_probs(raw_scores, sink, own, bias):
    sc = _fold(raw_scores, own) * (HEAD_DIM ** -0.5) + bias
    m = jnp.maximum(jnp.max(sc, axis=-1, keepdims=True), sink)
    p = jnp.exp(sc - m)
    psink = jnp.exp(sink - m)
    inv = 1.0 / (jnp.sum(p, axis=-1, keepdims=True) + psink)
    return p * inv, psink * inv


def _k_prep(k, kw, cosf, sins, ones_bd):
    rstd = lax.rsqrt(_group_mean(k * k, ones_bd) + EPS)
    nrm = k * rstd
    y = nrm * kw
    return nrm, rstd, y * cosf + _partner64(y) * sins


def _qk_prep(proj, qw, kw, cos, sins, ones_q, ones_kv):
    s = proj.shape[0]
    rows = min(KV_ROWS, s)

    def body(p_ref, qw_ref, kw_ref, cos_ref, sin_ref, oq_ref, ok_ref, o_ref):
        j = pl.program_id(1)

        @pl.when(j < KV_COL)
        def _():
            cosf, sinf = _tile_lanes(cos_ref[...], 4), _tile_lanes(sin_ref[...], 4)
            o_ref[...] = _q_prep(p_ref[...].astype(F32), qw_ref[...], cosf, sinf, oq_ref[...])[2].astype(BF16)

        @pl.when(j == KV_COL)
        def _():
            cosf, sinf = _tile_lanes(cos_ref[...], 2), _tile_lanes(sin_ref[...], 2)
            kr = _k_prep(p_ref[:, :KV_WIDTH].astype(F32), kw_ref[...], cosf, sinf, ok_ref[...])[2]
            o_ref[...] = jnp.concatenate([kr.astype(BF16), p_ref[:, KV_WIDTH:]], axis=1)

    full = lambda shape: pl.BlockSpec(shape, lambda i, j: (0,) * len(shape))
    tab = pl.BlockSpec((rows, SLAB), lambda i, j: (i, 0))
    blk = pl.BlockSpec((rows, GROUP_WIDTH), lambda i, j: (i, j))
    return pl.pallas_call(
        body, name="qk_prep", grid=(s // rows, KV_COL + 1),
        in_specs=[blk, full((1, GROUP_WIDTH)), full((1, KV_WIDTH)), tab, tab, full((GROUP_WIDTH, GROUP_WIDTH)),
                  full((KV_WIDTH, KV_WIDTH))],
        out_specs=blk, out_shape=jax.ShapeDtypeStruct((s, ATTN_WIDTH + 2 * KV_WIDTH), BF16),
        compiler_params=_params("parallel", "arbitrary"))(proj, qw, kw, cos, sins, ones_q, ones_kv)


def _keys_values(kc_ref, kp_ref, vc_ref, vp_ref, head):
    lanes = slice((head // 2) * SLAB, (head // 2 + 1) * SLAB)
    dup = lambda ref: _dup_head(ref[:, lanes].astype(F32), head % 2)
    return (jnp.concatenate([dup(kp_ref), dup(kc_ref)], axis=0).astype(BF16),
            jnp.concatenate([dup(vp_ref), dup(vc_ref)], axis=0).astype(BF16))


FWD_STEP_HEADS = 4
BWD_STEP_HEADS = 2


def _swa_specs(heads):
    kv_width = heads * HEAD_DIM
    prev = lambda n: jnp.maximum(n - 1, 0)
    kv = lambda col0, row: pl.BlockSpec((BLK, kv_width), lambda gs, n: (row(n), col0 // kv_width + gs))
    cur = lambda n: n
    full = lambda shape: pl.BlockSpec(shape, lambda gs, n: (0,) * len(shape))
    gate = lambda t: pl.BlockSpec((BLK, GROUP_WIDTH), lambda gs, n: (n, GATE_COL + heads * gs + t))
    return dict(
        sinks=pl.BlockSpec(memory_space=pltpu.SMEM), gates=[gate(t) for t in range(heads)],
        kc=kv(ATTN_WIDTH, cur), kp=kv(ATTN_WIDTH, prev), vc=kv(ATTN_WIDTH + KV_WIDTH, cur), vp=kv(ATTN_WIDTH + KV_WIDTH, prev),
        tri=full((BLK, BLK)), step=pl.BlockSpec((BLK, heads * GROUP_WIDTH), lambda gs, n: (n, gs)))


def _lower_triangle():
    return jnp.tril(jnp.ones((BLK, BLK), F32))


def _swa_fwd(qk, proj, sinks):
    s = proj.shape[0]
    nb = s // BLK
    heads = FWD_STEP_HEADS
    sp = _swa_specs(heads)

    def body(sink_ref, q_ref, kc_ref, kp_ref, vc_ref, vp_ref, *rest):
        gate_refs, (tri_ref, ag_ref, o_ref) = rest[:heads], rest[heads:]
        gs, n = pl.program_id(0), pl.program_id(1)
        own, bias = _band(tri_ref, n)
        for t, gate_ref in enumerate(gate_refs):
            cols = slice(t * GROUP_WIDTH, (t + 1) * GROUP_WIDTH)
            first_head = (heads * gs + t) * GROUP_HEADS
            kcat, vcat = _keys_values(kc_ref, kp_ref, vc_ref, vp_ref, t)
            scores = _dot_nt(_stack_heads(q_ref[:, cols]), kcat)
            probs = []
            for j in range(GROUP_HEADS):
                p, _ = _head_probs(scores[j * BLK:(j + 1) * BLK], sink_ref[first_head + j], own, bias)
                probs.append(_unfold(p, own).astype(BF16))
            o = _unstack_heads(_dot(jnp.concatenate(probs, axis=0), vcat))
            gate = gate_ref[...].astype(F32)
            o_ref[:, cols] = o.astype(BF16)
            ag_ref[:, cols] = (o * (gate * _sigmoid(gate))).astype(BF16)

    shp = jax.ShapeDtypeStruct((s, ATTN_WIDTH), BF16)
    return pl.pallas_call(
        body, name="swa_fwd", grid=(KV_HEADS // heads, nb),
        in_specs=[sp["sinks"], sp["step"], sp["kc"], sp["kp"], sp["vc"], sp["vp"], *sp["gates"], sp["tri"]],
        out_specs=[sp["step"], sp["step"]], out_shape=[shp, shp],
        compiler_params=_params("parallel", "arbitrary"),
    )(sinks, qk, qk, qk, qk, qk, *[proj] * heads, _lower_triangle())


def _swa_bwd(qk, proj, dag, sinks):
    s = proj.shape[0]
    nb = s // BLK
    heads = BWD_STEP_HEADS
    sp = _swa_specs(heads)

    def body(sink_ref, q_ref, kc_ref, kp_ref, vc_ref, vp_ref, *rest):
        gate_refs = rest[:heads]
        dag_ref, tri_ref, dq_ref, dkc_ref, dkp_ref, dvc_ref, dvp_ref, dsink_ref = rest[heads:]
        gs, n = pl.program_id(0), pl.program_id(1)
        own, bias = _band(tri_ref, n)

        @pl.when(n == 0)
        def _():
            dsink_ref[...] = jnp.zeros_like(dsink_ref)

        for t, gate_ref in enumerate(gate_refs):
            cols = slice(t * GROUP_WIDTH, (t + 1) * GROUP_WIDTH)
            first_head = (heads * gs + t) * GROUP_HEADS
            kcat, vcat = _keys_values(kc_ref, kp_ref, vc_ref, vp_ref, t)
            gate = gate_ref[...].astype(F32)
            do = dag_ref[:, cols].astype(F32) * (gate * _sigmoid(gate))
            q_stack = _stack_heads(q_ref[:, cols])
            do_stack = _stack_heads(do).astype(BF16)
            scores = _dot_nt(q_stack, kcat)
            dprobs = _dot_nt(do_stack, vcat)
            probs, dscores, dsinks = [], [], []
            for j in range(GROUP_HEADS):
                rows = slice(j * BLK, (j + 1) * BLK)
                p, psink = _head_probs(scores[rows], sink_ref[first_head + j], own, bias)
                dp = _fold(dprobs[rows], own)
                delta = jnp.sum(p * dp, axis=-1, keepdims=True)
                probs.append(_unfold(p, own).astype(BF16))
                dscores.append(_unfold(p * (dp - delta) * (HEAD_DIM ** -0.5), own).astype(BF16))
                dsinks.append(-jnp.sum(psink * delta, axis=0, keepdims=True))
            ds = jnp.concatenate(dscores, axis=0)
            dk = _dot_tn(ds, q_stack)
            dv = _dot_tn(jnp.concatenate(probs, axis=0), do_stack)
            dk = dk + pltpu.roll(dk, HEAD_DIM, 1)
            dv = dv + pltpu.roll(dv, HEAD_DIM, 1)
            dkp_ref[t], dkc_ref[t] = dk[:BLK], dk[BLK:]
            dvp_ref[t], dvc_ref[t] = dv[:BLK], dv[BLK:]
            dq_ref[:, cols] = _unstack_heads(_dot(ds, kcat)).astype(BF16)
            r8 = lax.broadcasted_iota(jnp.int32, (8, SLAB), 0)
            upd = jnp.zeros((8, SLAB), F32)
            for j in range(GROUP_HEADS):
                upd = jnp.where(r8 == j, dsinks[j], upd)
            dsink_ref[t] += upd

    cur_out = pl.BlockSpec((heads, BLK, SLAB), lambda gs, n: (gs, n, 0))
    prev_out = pl.BlockSpec((heads, BLK, SLAB), lambda gs, n: (gs, (n + nb - 1) % nb, 0))
    kv_shape = jax.ShapeDtypeStruct((KV_HEADS, s, SLAB), F32)
    return pl.pallas_call(
        body, name="swa_bwd", grid=(KV_HEADS // heads, nb),
        in_specs=[sp["sinks"], sp["step"], sp["kc"], sp["kp"], sp["vc"], sp["vp"], *sp["gates"], sp["step"], sp["tri"]],
        out_specs=[sp["step"], cur_out, prev_out, cur_out, prev_out,
                   pl.BlockSpec((heads, 8, SLAB), lambda gs, n: (gs, 0, 0))],
        out_shape=[jax.ShapeDtypeStruct((s, ATTN_WIDTH), BF16), kv_shape, kv_shape, kv_shape, kv_shape,
                   jax.ShapeDtypeStruct((KV_HEADS, 8, SLAB), F32)],
        compiler_params=_params("parallel", "arbitrary"),
    )(sinks, qk, qk, qk, qk, qk, *[proj] * heads, dag, _lower_triangle())


def _swa_bwd_finish(proj, dqr, o, dag, dkc, dkp, dvc, dvp, qw, kw, cos, sins, ones_q, ones_kv):
    s = proj.shape[0]
    rows = min(KV_ROWS, s)
    n_q = KV_COL
    q_of = lambda j: jnp.clip(j - 1, 0, n_q - 1)
    gate_of = lambda j: jnp.clip(j - 1 - n_q, 0, n_q - 1)

    def body(kv_ref, p_ref, dqr_ref, o_ref, dag_ref, dkc_ref, dkp_ref, dvc_ref, dvp_ref, qw_ref, kw_ref, cos_ref, sin_ref,
             oq_ref, ok_ref, dp_ref, dqw_ref, dkw_ref):
        j, i = pl.program_id(0), pl.program_id(1)

        @pl.when(j == 0)
        def _():
            @pl.when(i == 0)
            def _():
                dkw_ref[...] = jnp.zeros_like(dkw_ref)

            def assemble(cur_ref, prv_ref):
                tot = [cur_ref[h] + prv_ref[h] for h in range(KV_HEADS)]
                first = _half_mask(tot[0].shape, 0)
                return jnp.concatenate([jnp.where(first, tot[0], tot[1]), jnp.where(first, tot[2], tot[3])], axis=1)

            dkr = assemble(dkc_ref, dkp_ref)
            dv = assemble(dvc_ref, dvp_ref)
            cosf, sinf = _tile_lanes(cos_ref[...], 2), _tile_lanes(sin_ref[...], 2)
            nrm, rstd, _ = _k_prep(kv_ref[:, :KV_WIDTH].astype(F32), kw_ref[...], cosf, sinf, ok_ref[...])
            dy = dkr * cosf + _partner64(dkr * sinf)
            dn = dy * kw_ref[...]
            dk = rstd * (dn - nrm * _group_mean(dn * nrm, ok_ref[...], passes=1))
            dp_ref[...] = jnp.concatenate([dk, dv], axis=1).astype(BF16)
            dkw_ref[...] += jnp.sum(dy * nrm, axis=0, keepdims=True)

        @pl.when(jnp.logical_and(j >= 1, j <= n_q))
        def _():
            @pl.when(i == 0)
            def _():
                dqw_ref[...] = jnp.zeros_like(dqw_ref)

            cosf, sinf = _tile_lanes(cos_ref[...], 4), _tile_lanes(sin_ref[...], 4)
            nrm, rstd, _ = _q_prep(p_ref[...].astype(F32), qw_ref[...], cosf, sinf, oq_ref[...])
            dq = dqr_ref[...].astype(F32)
            dy = dq * cosf + _partner64(dq * sinf)
            dn = dy * qw_ref[...]
            dp_ref[...] = (rstd * (dn - nrm * _group_mean(dn * nrm, oq_ref[...], passes=1))).astype(BF16)
            dqw_ref[0] += jnp.sum(dy * nrm, axis=0, keepdims=True)

        @pl.when(j > n_q)
        def _():
            gate = p_ref[...].astype(F32)
            sg = _sigmoid(gate)
            dp_ref[...] = (dag_ref[...].astype(F32) * o_ref[...].astype(F32) * (sg * (1.0 + gate * (1.0 - sg)))).astype(BF16)

    first_pass = lambda j, i: jnp.where(j == 0, i, 0)
    acc = pl.BlockSpec((KV_HEADS, rows, SLAB), lambda j, i: (0, first_pass(j, i), 0))
    full = lambda shape: pl.BlockSpec(shape, lambda j, i: (0,) * len(shape))
    tab = pl.BlockSpec((rows, SLAB), lambda j, i: (i, 0))
    out_col = lambda j: jnp.where(j == 0, KV_COL, jnp.where(j <= n_q, j - 1, j))
    return pl.pallas_call(
        body, name="swa_bwd_finish", grid=(2 * n_q + 1, s // rows),
        in_specs=[pl.BlockSpec((rows, GROUP_WIDTH), lambda j, i: (first_pass(j, i), KV_COL)),
                  pl.BlockSpec((rows, GROUP_WIDTH), lambda j, i: (jnp.where(j == 0, 0, i), jnp.where(j <= n_q, q_of(j), j))),
                  pl.BlockSpec((rows, GROUP_WIDTH), lambda j, i: (jnp.where(jnp.logical_and(j >= 1, j <= n_q), i, 0), q_of(j))),
                  pl.BlockSpec((rows, GROUP_WIDTH), lambda j, i: (jnp.where(j > n_q, i, 0), gate_of(j))),
                  pl.BlockSpec((rows, GROUP_WIDTH), lambda j, i: (jnp.where(j > n_q, i, 0), gate_of(j))),
                  acc, acc, acc, acc, full((1, GROUP_WIDTH)), full((1, KV_WIDTH)), tab, tab,
                  full((GROUP_WIDTH, GROUP_WIDTH)), full((KV_WIDTH, KV_WIDTH))],
        out_specs=[pl.BlockSpec((rows, GROUP_WIDTH), lambda j, i: (i, out_col(j))),
                   pl.BlockSpec((1, 8, GROUP_WIDTH), lambda j, i: (q_of(j), 0, 0)), pl.BlockSpec((8, KV_WIDTH), lambda j, i: (0, 0))],
        out_shape=[jax.ShapeDtypeStruct((s, ODD_IN), BF16), jax.ShapeDtypeStruct((n_q, 8, GROUP_WIDTH), F32),
                   jax.ShapeDtypeStruct((8, KV_WIDTH), F32)],
        compiler_params=_params("arbitrary", "arbitrary"),
    )(proj, proj, dqr, o, dag, dkc, dkp, dvc, dvp, qw, kw, cos, sins, ones_q, ones_kv)


def _place():
    x, y, c = lax.axis_index("x"), lax.axis_index("y"), lax.axis_index("c")
    return x, y, c


OTHER_CHIPS = ((1, 0), (0, 1), (1, 1))


def _half_rows(ref, half, rows):
    return ref.at[pl.ds(pl.multiple_of(half * (rows // 2), 8), rows // 2)]


DMA_CHUNK_BYTES = 1 << 20
BF16_TILE_ROWS = 16


def _n_chunks(ref):
    rows = ref.shape[-2]
    nbytes = math.prod(ref.shape) * jnp.dtype(ref.dtype).itemsize
    n = 1
    while 2 * n * DMA_CHUNK_BYTES <= nbytes and rows % (2 * n * BF16_TILE_ROWS) == 0:
        n *= 2
    return n


def _row_chunk(ref, k, n):
    rows = ref.shape[-2] // n
    return ref.at[pl.ds(k * rows, rows)] if len(ref.shape) == 2 else ref.at[:, pl.ds(k * rows, rows)]


def _push(src, dst, send_sem, recv_sem, device_id):
    n = _n_chunks(src)
    for k in range(n):
        pltpu.make_async_remote_copy(src_ref=_row_chunk(src, k, n), dst_ref=_row_chunk(dst, k, n), send_sem=send_sem,
                                     recv_sem=recv_sem, device_id=device_id, device_id_type=MESH).start()
    return pltpu.make_async_remote_copy(src_ref=src, dst_ref=dst, send_sem=send_sem, recv_sem=recv_sem,
                                        device_id=device_id, device_id_type=MESH)


def _copy(src, dst, sem):
    n = _n_chunks(src)
    for k in range(n):
        pltpu.make_async_copy(_row_chunk(src, k, n), _row_chunk(dst, k, n), sem).start()
    return pltpu.make_async_copy(src, dst, sem)


HBM = pl.BlockSpec(memory_space=pltpu.HBM)
SEM = pl.BlockSpec(memory_space=pltpu.SEMAPHORE)
SPLIT_COPY_EFFECT = pltpu.SideEffectType.DATAFLOW_SIDE_EFFECTING


def _in_hbm(a):
    return pltpu.with_memory_space_constraint(a, pltpu.HBM)


def _start_copies(name, arrays, plan, n_copies, after=None):
    n = len(arrays)

    def body(*refs):
        send_sem, recv_sem = refs[-n - 3], refs[-n - 2]
        for k, (src, dst, peer) in enumerate(plan(refs[:n])):
            _push(src, dst, send_sem.at[k], recv_sem.at[k], peer)
        refs[-1][...] = jnp.zeros_like(refs[-1])

    dma = pltpu.SemaphoreType.DMA((n_copies,))
    outs = pl.pallas_call(
        body, name=name,
        out_shape=(dma, dma, *[pltpu.HBM(a.shape, a.dtype) for a in arrays], jax.ShapeDtypeStruct((8, 128), F32)),
        in_specs=[HBM] * n + ([ANY] if after is not None else []),
        out_specs=(SEM, SEM, *[HBM] * n, pl.BlockSpec(memory_space=pltpu.VMEM)),
        input_output_aliases={i: i + 2 for i in range(n)},
        compiler_params=pltpu.CompilerParams(has_side_effects=SPLIT_COPY_EFFECT),
    )(*[_in_hbm(a) for a in arrays], *((after,) if after is not None else ()))
    return outs[0], outs[1], list(outs[2:2 + n]), outs[-1]


def _wait_copies(name, send_sem, recv_sem, arrays, plan, after):
    n = len(arrays)

    def body(*refs):
        send_ref, recv_ref = refs[n], refs[n + 1]
        for k, (src, dst, peer) in enumerate(plan(refs[:n])):
            cp = pltpu.make_async_remote_copy(src_ref=src, dst_ref=dst, send_sem=send_ref.at[k], recv_sem=recv_ref.at[k],
                                              device_id=peer, device_id_type=MESH)
            cp.wait_send()
            cp.wait_recv()

    return list(pl.pallas_call(
        body, name=name, out_shape=tuple(pltpu.HBM(a.shape, a.dtype) for a in arrays),
        in_specs=[HBM] * n + [SEM, SEM, ANY], out_specs=tuple([HBM] * n),
        input_output_aliases={i: i for i in range(n)},
        compiler_params=pltpu.CompilerParams(has_side_effects=SPLIT_COPY_EFFECT),
    )(*arrays, send_sem, recv_sem, after))


def _gather_region(full, kind, chip, half=None):
    if kind == "whole":
        return full.at[chip]
    if kind == "col":
        rows, width = full.shape[0], full.shape[1] // N_CHIPS
        piece = full.at[:, pl.ds(pl.multiple_of(chip * width, 128), width)]
    else:
        rows = full.shape[0] // N_CHIPS
        piece = full.at[pl.ds(pl.multiple_of(chip * rows, BF16_TILE_ROWS), rows)]
    return piece if half is None else _half_rows(piece, half, rows)


def _gather_plan(kinds):
    def plan(fulls):
        x, y, c = _place()
        copies = []
        for fx, fy in OTHER_CHIPS:
            for full, kind in zip(fulls, kinds):
                mine = _gather_region(full, kind, 2 * x + y, c)
                copies.append((mine, mine, (x ^ fx, y ^ fy, c)))
        return copies

    return plan


def _gather_start(name, fulls, kinds, after=None):
    return _start_copies(name, list(fulls), _gather_plan(kinds), 3 * len(kinds), after)


def _gather_finish(name, started, kinds, after):
    send_sem, recv_sem, fulls, _ = started
    n = len(kinds)
    fulls = _wait_copies(name + "_wait", send_sem, recv_sem, fulls, _gather_plan(kinds), after)
    split = [i for i, kind in enumerate(kinds) if kind != "whole"]

    def body(*refs):
        out = refs[n:2 * n]
        send, recv = refs[2 * n:]
        x, y, c = _place()
        pushes = []
        for r, (fx, fy) in enumerate(OTHER_CHIPS):
            chip = 2 * (x ^ fx) + (y ^ fy)
            for j, i in enumerate(split):
                landed = _gather_region(out[i], kinds[i], chip, c)
                pushes.append(_push(landed, landed, send.at[r * len(split) + j], recv.at[r * len(split) + j], (x, y, 1 - c)))
        for cp in pushes:
            cp.wait_recv()
        for cp in pushes:
            cp.wait_send()

    dma = pltpu.SemaphoreType.DMA
    return pl.pallas_call(
        body, name=name + "_share", in_specs=[ANY] * n, out_specs=[ANY] * n,
        out_shape=[jax.ShapeDtypeStruct(a.shape, a.dtype) for a in fulls],
        input_output_aliases={i: i for i in range(n)},
        scratch_shapes=[dma((3 * len(split),)), dma((3 * len(split),))],
        compiler_params=pltpu.CompilerParams(has_side_effects=True),
    )(*fulls)


def _allreduce_small(v):
    def body(v_ref, out_ref, buf, send_sems, recv_sems):
        x, y, c = _place()
        me = 4 * x + 2 * y + c
        buf[me] = v_ref[...]
        copies = []
        for r in range(1, N_DEV):
            peer = (x ^ (r >> 2), y ^ ((r >> 1) & 1), c ^ (r & 1))
            cp = pltpu.make_async_remote_copy(src_ref=v_ref, dst_ref=buf.at[me], send_sem=send_sems.at[r - 1],
                                              recv_sem=recv_sems.at[r - 1], device_id=peer, device_id_type=MESH)
            cp.start()
            copies.append(cp)
        for cp in copies:
            cp.wait_recv()
        for cp in copies:
            cp.wait_send()
        total = buf[0]
        for d in range(1, N_DEV):
            total = total + buf[d]
        out_ref[...] = total

    vm = pl.BlockSpec(memory_space=pltpu.VMEM)
    return pl.pallas_call(
        body, name="allreduce_small", in_specs=[vm], out_specs=vm, out_shape=jax.ShapeDtypeStruct(v.shape, v.dtype),
        scratch_shapes=[pltpu.VMEM((N_DEV,) + v.shape, v.dtype), pltpu.SemaphoreType.DMA((N_DEV - 1,)),
                        pltpu.SemaphoreType.DMA((N_DEV - 1,))],
        compiler_params=pltpu.CompilerParams(has_side_effects=True),
    )(v)


def _exchange_halves(grads, name):
    n = len(grads)

    def body(*refs):
        g, theirs = refs[:n], refs[n:2 * n]
        send_sem, recv_sem = refs[2 * n:]
        x, y, c = _place()
        copies = []
        for i in range(n):
            half = g[i].shape[1] // 2
            src = g[i].at[:, pl.ds(pl.multiple_of((1 - c) * half, BF16_TILE_ROWS), half)]
            copies.append(_push(src, theirs[i], send_sem.at[i], recv_sem.at[i], (x, y, 1 - c)))
        for cp in copies:
            cp.wait_recv()
            cp.wait_send()

    dma = pltpu.SemaphoreType.DMA
    return pl.pallas_call(
        body, name=name, in_specs=[ANY] * n, out_specs=[ANY] * n,
        out_shape=[jax.ShapeDtypeStruct((a.shape[0], a.shape[1] // 2, a.shape[2]), a.dtype) for a in grads],
        scratch_shapes=[dma((n,)), dma((n,))],
        compiler_params=pltpu.CompilerParams(has_side_effects=True),
    )(*grads)


def _pair_sum(g, theirs, core, name):
    pieces, half, cols = theirs.shape
    tr = min(half, 256)
    per = half // tr

    def body(core_ref, g_ref, t_ref, o_ref):
        del core_ref
        o_ref[...] = (g_ref[...].astype(F32) + t_ref[...].astype(F32)).astype(BF16)

    spec = pl.BlockSpec((1, tr, cols), lambda p, i, core_ref: (p, i, 0))
    return pl.pallas_call(
        body, name=name, out_shape=jax.ShapeDtypeStruct(theirs.shape, BF16),
        grid_spec=pltpu.PrefetchScalarGridSpec(
            num_scalar_prefetch=1, grid=(pieces, per),
            in_specs=[pl.BlockSpec((1, tr, cols), lambda p, i, core_ref: (p, core_ref[0] * per + i, 0)), spec],
            out_specs=spec),
        compiler_params=_params("parallel", "parallel"))(core, g, theirs)


def _scatter_plan(n):
    def plan(refs):
        parts, stacks = refs[:n], refs[n:]
        x, y, c = _place()
        copies = []
        for fx, fy in OTHER_CHIPS:
            chip = 2 * (x ^ fx) + (y ^ fy)
            for part, stack in zip(parts, stacks):
                if part.shape[0] == N_CHIPS:
                    piece = part.at[chip]
                else:
                    width = part.shape[2] // N_CHIPS
                    piece = part.at[0].at[:, pl.ds(pl.multiple_of(chip * width, 128), width)]
                copies.append((piece, stack.at[2 * x + y], (x ^ fx, y ^ fy, c)))
        return copies

    return plan


def _scatter_start(name, parts, after=None):
    def landing(a):
        return (N_CHIPS, a.shape[1], a.shape[2] if a.shape[0] == N_CHIPS else a.shape[2] // N_CHIPS)

    stacks = [lax.empty(landing(a), a.dtype) for a in parts]
    return _start_copies(name, list(parts) + stacks, _scatter_plan(len(parts)), 3 * len(parts), after)


def _scatter_finish(name, started, after):
    send_sem, recv_sem, arrays, _ = started
    n = len(arrays) // 2
    arrays = _wait_copies(name + "_wait", send_sem, recv_sem, arrays, _scatter_plan(n), after)
    return arrays[:n], arrays[n:]


def _sum_chips(part, stack, place, name):
    _, r, c = stack.shape
    tr = 256
    per = r // tr

    def body(place_ref, own_ref, a_ref, b_ref, c_ref, o_ref):
        del place_ref
        total = own_ref[0].astype(F32)
        for ref in (a_ref, b_ref, c_ref):
            total = total + ref[0].astype(F32)
        o_ref[...] = total

    if part.shape[0] == N_CHIPS:
        own = pl.BlockSpec((1, tr, c), lambda i, pr: (pr[1], i, 0))
    else:
        own = pl.BlockSpec((1, tr, c), lambda i, pr: (0, i, pr[1]))
    other = lambda flip: pl.BlockSpec((1, tr, c), lambda i, pr: (pr[1] ^ flip, i, 0))
    return pl.pallas_call(
        body, name=name, out_shape=jax.ShapeDtypeStruct((2 * r, c), F32),
        grid_spec=pltpu.PrefetchScalarGridSpec(
            num_scalar_prefetch=1, grid=(per,), in_specs=[own, other(2), other(1), other(3)],
            out_specs=pl.BlockSpec((tr, c), lambda i, pr: (pr[0] * per + i, 0))),
        compiler_params=_params("parallel"))(place, part, stack, stack, stack)


def _share_halves(fulls):
    n = len(fulls)

    def body(*refs):
        dst = refs[n:2 * n]
        send_sem, recv_sem = refs[2 * n:]
        x, y, c = _place()
        copies = []
        for i in range(n):
            mine = _half_rows(dst[i], c, dst[i].shape[0])
            copies.append(_push(mine, mine, send_sem.at[i], recv_sem.at[i], (x, y, 1 - c)))
        for cp in copies:
            cp.wait_recv()
            cp.wait_send()

    dma = pltpu.SemaphoreType.DMA
    return pl.pallas_call(
        body, name="share_halves", in_specs=[ANY] * n, out_specs=[ANY] * n,
        out_shape=[jax.ShapeDtypeStruct(a.shape, a.dtype) for a in fulls],
        input_output_aliases={i: i for i in range(n)}, scratch_shapes=[dma((n,)), dma((n,))],
        compiler_params=pltpu.CompilerParams(has_side_effects=True),
    )(*fulls)


MM = dict(tm=1024, tn=1024, tk=2048)
MM_LONG_K = dict(tm=1024, tn=1024, tk=4096)


def _local_step(x, target, ev_norm_w, q_norm_w, k_norm_w, sinks, own_first, weights_first, weights_late, emit):
    s = x.shape[0]
    cos_r, sin_r = _rope_tables(s, RET_DIM)
    cos_a, sin_a = _rope_tables(s, HEAD_DIM)
    cos_a = jnp.tile(cos_a, (1, 4))
    sins_a = jnp.tile(jnp.concatenate([-sin_a, sin_a], axis=1), (1, 2))
    tables = _retention_tables()
    ones_q, ones_kv = _block_diag_mean(GROUP_WIDTH), _block_diag_mean(KV_WIDTH)
    qw_g = jnp.tile(q_norm_w, (1, GROUP_WIDTH // HEAD_DIM))
    kw_kv = jnp.tile(k_norm_w, (1, KV_WIDTH // HEAD_DIM))
    sinks1 = sinks.reshape(Q_HEADS)

    own_w_in0, own_block, start_token = own_first
    h0 = _rmsnorm(x, ev_norm_w, "norm0", after=start_token)
    shifted = dict(shift=own_block * (own_w_in0.shape[1] // MM["tn"]), total=EVEN_IN // MM["tn"], tm=MM["tm"], tn=MM["tn"],
                   out_dtype=BF16)
    own_blocks = own_w_in0.shape[1] // MM["tn"]
    proj0 = _mm_shifted(h0, own_w_in0, b_shifted=False, first=0, count=own_blocks, name="proj0_own", **shifted)
    w_in0, conv_w, od_norm_w, token = weights_first(proj0)
    proj0 = _mm_shifted(h0, w_in0, b_shifted=True, first=own_blocks, count=shifted["total"] - own_blocks, name="proj0_rest",
                        into=proj0, after=token, **shifted)
    o_ret, cat, states = _retention_fwd(proj0, cos_r, sin_r, tables)
    cat = _conv_fwd(proj0, conv_w, cat)
    w_out0, w_in1, w_out1 = weights_late(cat)

    def residual_and_norm(prod, x_ref, w_ref, x1_ref, h1_ref):
        x1v = x_ref[...] + prod
        x1_ref[...] = x1v
        rstd = lax.rsqrt(jnp.mean(x1v * x1v, axis=-1, keepdims=True) + EPS)
        h1_ref[...] = (x1v * rstd * w_ref[...]).astype(BF16)

    def residual_and_loss(prod, x1_ref, t_ref, dyb_ref, sq_ref):
        diff = (x1_ref[...] + prod) - t_ref[...]
        dyb_ref[...] = (diff * (1.0 / D_MODEL)).astype(BF16)

        @pl.when(pl.program_id(0) == 0)
        def _():
            sq_ref[...] = jnp.zeros_like(sq_ref)

        sq_ref[...] += jnp.sum(jnp.sum(diff * diff, axis=1, keepdims=True), axis=0, keepdims=True)

    act = lambda dt: jax.ShapeDtypeStruct((s, D_MODEL), dt)
    x1, h1 = _mm_rows(cat, w_out0, [x], [od_norm_w], [act(F32), act(BF16)], residual_and_norm, tm=min(s, 512), name="out0")
    proj1 = _mm(h1, w_in1, mode="nn", out_dtype=BF16, name="proj1", tm=1024, tn=1536, tk=2048)
    qk = _qk_prep(proj1, qw_g, kw_kv, cos_a, sins_a, ones_q, ones_kv)
    ag, o_att = _swa_fwd(qk, proj1, sinks1)
    dy_b, sq = _mm_rows(ag, w_out1, [x1, target], [], [act(BF16), jax.ShapeDtypeStruct((8, 128), F32)],
                        residual_and_loss, tm=min(s, 256), name="out1")

    g_w_out1 = _mm(ag, dy_b, mode="tn", out_dtype=BF16, name="g_w_out1", **MM_LONG_K)
    dag = _mm(dy_b, w_out1, mode="nt", out_dtype=BF16, name="d_ag", **MM)
    dqr, dkc, dkp, dvc, dvp, dsink = _swa_bwd(qk, proj1, dag, sinks1)
    dproj1, dqw, dkw = _swa_bwd_finish(proj1, dqr, o_att, dag, dkc, dkp, dvc, dvp, qw_g, kw_kv, cos_a, sins_a, ones_q, ones_kv)
    g_w_in1 = _mm(h1, dproj1, mode="tn", out_dtype=BF16, name="g_w_in1", tm=1024, tn=768, tk=4096)
    token = emit("layer1", (("od_w_in", g_w_in1, "col"), ("od_w_out", g_w_out1, "row")))
    dh1 = _mm(dproj1, w_in1, mode="nt", out_dtype=BF16, name="d_h1", tm=1024, tn=1024, tk=ODD_IN, after=token)
    dx1_b, g_norm1 = _rmsnorm_bwd(x1, od_norm_w, dh1, dy_b, "norm1_bwd", BF16)

    g_w_out0 = _mm(cat, dx1_b, mode="tn", out_dtype=BF16, name="g_w_out0", **MM_LONG_K)
    token = emit("out0", (("ev_w_out", g_w_out0, "row"),))
    dcat = _mm(dx1_b, w_out0, mode="nt", out_dtype=BF16, name="d_cat", after=token, **MM)
    dproj0 = _retention_bwd(proj0, o_ret, states, dcat, cos_r, sin_r, tables)
    dproj0, g_conv = _conv_bwd(proj0, dcat, conv_w, dproj0)
    g_w_in0 = _mm(h0, dproj0, mode="tn", out_dtype=BF16, name="g_w_in0", **MM_LONG_K)
    token = emit("in0", (("ev_w_in", g_w_in0, "col"),))
    dh0 = _mm(dproj0, w_in0, mode="nt", out_dtype=BF16, name="d_h0", after=token, **MM_LONG_K)
    grad_x, g_norm0 = _rmsnorm_bwd(x, ev_norm_w, dh0, dx1_b, "norm0_bwd", F32)

    g_qw = dqw[:, 0, :].reshape(Q_HEADS, HEAD_DIM).sum(axis=0)
    g_kw = dkw[0].reshape(KV_HEADS, HEAD_DIM).sum(axis=0)
    g_sinks = dsink[:, :, 0].reshape(Q_HEADS)
    small = dict(ev_norm=g_norm0[0], od_norm=g_norm1[0], conv=g_conv[:3], qw=g_qw, kw=g_kw, sinks=g_sinks)
    return sq[0, 0], grad_x, small


def _pack_small_grads(small):
    pad = lambda v: jnp.pad(v, (0, D_MODEL - v.shape[0]))
    tail = pad(jnp.concatenate([small["qw"], small["kw"], small["sinks"]]))
    rows = [small["ev_norm"], small["od_norm"]] + [pad(small["conv"][t]) for t in range(3)] + [tail]
    rows += [jnp.zeros((D_MODEL,), F32)] * (8 - len(rows))
    return jnp.stack(rows)


class _ReduceScatter:
    def __init__(self, place):
        self.place = place
        self.started = []

    def send(self, tag, grads):
        pieces = [g[None] if kind == "col" else g.reshape(N_CHIPS, g.shape[0] // N_CHIPS, g.shape[1]) for _, g, kind in grads]
        theirs = _exchange_halves(pieces, "exchange_halves_" + tag)
        parts = [_pair_sum(g, t, self.place[:1], "pair_sum_" + nm) for g, t, (nm, _, _) in zip(pieces, theirs, grads)]
        started = _scatter_start("scatter_" + tag, parts)
        self.started.append((tag, [nm for nm, _, _ in grads], started))
        return started[3]

    def finish(self, after):
        names, sums = [], []
        for tag, group, started in self.started:
            parts, stacks = _scatter_finish("scatter_" + tag, started, after)
            sums += [_sum_chips(p, s, self.place, "chip_sum_" + nm) for p, s, nm in zip(parts, stacks, group)]
            names += group
        return dict(zip(names, _share_halves(sums)))


def kernel(x, ev_norm_w, ev_w_in, ev_conv_w, ev_w_out, od_norm_w, od_w_in, od_q_norm_w, od_k_norm_w, od_sinks, od_w_out, loss_target, m_ev_norm_w, m_ev_w_in, m_ev_conv_w, m_ev_w_out, m_od_norm_w, m_od_w_in, m_od_q_norm_w, m_od_k_norm_w, m_od_sinks, m_od_w_out, v_ev_norm_w, v_ev_w_in, v_ev_conv_w, v_ev_w_out, v_od_norm_w, v_od_w_in, v_od_q_norm_w, v_od_k_norm_w, v_od_sinks, v_od_w_out):
    my_chip = 2 * lax.axis_index("x") + lax.axis_index("y")
    place = jnp.stack([lax.axis_index("c"), my_chip]).astype(jnp.int32)
    shard_w = D_MODEL // N_CHIPS
    conv_shard = CONV_WIDTH // N_CHIPS

    small_in = jnp.zeros((8, shard_w), F32)
    small_in = small_in.at[0].set(od_norm_w[0]).at[1:4, :conv_shard].set(ev_conv_w[0])
    small_in = lax.dynamic_update_slice(jnp.zeros((N_CHIPS, 8, shard_w), F32), small_in[None], (my_chip, 0, 0))
    chip = place[1:]
    first_kinds, late_kinds = ("col", "whole"), ("row", "col", "row")
    w_in0_own_place, w_in0_shard = _cast_into_gathered(ev_w_in[0], "col", chip, "cast_w_in0", keep_shard=True)
    first = _gather_start("gather_first", [w_in0_own_place, small_in], first_kinds)
    late_own = [_cast_into_gathered(ev_w_out[0], "row", chip, "cast_w_out0"),
                _cast_into_gathered(od_w_in[0], "col", chip, "cast_w_in1"),
                _cast_into_gathered(od_w_out[0], "row", chip, "cast_w_out1")]
    late = []

    def weights_first(after):
        w_in0, small_all = _gather_finish("gather_first", first, first_kinds, after)
        late.append(_gather_start("gather_late", late_own, late_kinds, after=w_in0))
        od_norm_full = small_all[:, 0, :].reshape(1, D_MODEL)
        conv_full = jnp.transpose(small_all[:, 1:4, :conv_shard], (1, 0, 2)).reshape(3, CONV_WIDTH)
        return w_in0, conv_full, od_norm_full, late[0][3]

    def weights_late(after):
        return _gather_finish("gather_late", late[0], late_kinds, after)

    reduce_scatter = _ReduceScatter(place)
    sq, grad_x, small = _local_step(x[0], loss_target[0], ev_norm_w, od_q_norm_w, od_k_norm_w, od_sinks,
                                    (w_in0_shard, chip, first[3]), weights_first, weights_late, reduce_scatter.send)
    loss = lax.psum(0.5 * sq / D_MODEL, ("x", "y", "c"))

    big = reduce_scatter.finish(grad_x)
    g_ev_w_in, g_ev_w_out, g_od_w_in, g_od_w_out = big["ev_w_in"], big["ev_w_out"], big["od_w_in"], big["od_w_out"]
    tot = _allreduce_small(_pack_small_grads(small))
    g_ev_norm = tot[0:1]
    g_od_norm = lax.dynamic_slice(tot, (1, my_chip * shard_w), (1, shard_w))
    g_conv = lax.dynamic_slice(tot, (2, my_chip * conv_shard), (3, conv_shard))
    g_qw, g_kw, g_sinks = tot[5:6, 0:HEAD_DIM], tot[5:6, HEAD_DIM:2 * HEAD_DIM], tot[5:6, 2 * HEAD_DIM:2 * HEAD_DIM + Q_HEADS]

    upd = {}
    upd["ev_w_in"] = _adamw(ev_w_in[0], g_ev_w_in, m_ev_w_in[0], v_ev_w_in[0], "adamw_ev_w_in")
    upd["ev_w_out"] = _adamw(ev_w_out[0], g_ev_w_out, m_ev_w_out[0], v_ev_w_out[0], "adamw_ev_w_out")
    upd["od_w_in"] = _adamw(od_w_in[0], g_od_w_in, m_od_w_in[0], v_od_w_in[0], "adamw_od_w_in")
    upd["od_w_out"] = _adamw(od_w_out[0], g_od_w_out, m_od_w_out[0], v_od_w_out[0], "adamw_od_w_out")
    smalls = (("ev_norm_w", ev_norm_w, g_ev_norm, m_ev_norm_w, v_ev_norm_w),
              ("ev_conv_w", ev_conv_w, g_conv, m_ev_conv_w, v_ev_conv_w),
              ("od_norm_w", od_norm_w, g_od_norm, m_od_norm_w, v_od_norm_w),
              ("od_q_norm_w", od_q_norm_w, g_qw, m_od_q_norm_w, v_od_q_norm_w),
              ("od_k_norm_w", od_k_norm_w, g_kw, m_od_k_norm_w, v_od_k_norm_w),
              ("od_sinks", od_sinks, g_sinks, m_od_sinks, v_od_sinks))
    sizes = [w.size for _, w, _, _, _ in smalls]
    padded = 8 * 128 * math.ceil(sum(sizes) / (8 * 128))
    pack = lambda arrs, fill: jnp.concatenate(
        [a.reshape(-1) for a in arrs] + [jnp.full((padded - sum(sizes),), fill, F32)]).reshape(8, padded // 8)
    packed = _adamw(pack([w for _, w, _, _, _ in smalls], 0.0), pack([g for _, _, g, _, _ in smalls], 0.0),
                    pack([m for _, _, _, m, _ in smalls], 0.0), pack([v for _, _, _, _, v in smalls], 1.0), "adamw_small")
    offs = [sum(sizes[:i]) for i in range(len(sizes))]
    for (nm, w, _, _, _), off, size in zip(smalls, offs, sizes):
        upd[nm] = tuple(p.reshape(-1)[off:off + size].reshape(w.shape) for p in packed)
    for nm in ("ev_w_in", "ev_w_out", "od_w_in", "od_w_out"):
        upd[nm] = tuple(u[None] for u in upd[nm])
    order = ("ev_norm_w", "ev_w_in", "ev_conv_w", "ev_w_out", "od_norm_w", "od_w_in", "od_q_norm_w", "od_k_norm_w", "od_sinks", "od_w_out")
    return (loss, grad_x[None], *[upd[nm][0] for nm in order], *[upd[nm][1] for nm in order],
            *[upd[nm][2] for nm in order], *[upd[nm][3] for nm in order])
```

```python
import functools
import math

import jax
import jax.numpy as jnp
from jax import lax
from jax.experimental import pallas as pl
from jax.experimental.pallas import tpu as pltpu

F32 = jnp.float32
BF16 = jnp.bfloat16

D_MODEL = 2048
RET_HEADS = 4
RET_DIM = 256
RET_WIDTH = 1024
CONV_WIDTH = 1024
EVEN_IN = 8192
Q_HEADS = 32
HEAD_DIM = 64
KV_HEADS = 4
KV_WIDTH = 256
ATTN_WIDTH = 2048
ODD_IN = 4608
BLK = 128
ROPE_THETA = 10000.0
EPS = 1e-6
ADAM_LR = 0.001
ADAM_B1 = 0.9
ADAM_B2 = 0.999
ADAM_EPS = 1e-08
ADAM_WD = 0.01
ADAM_STEP = 10
N_CHIPS = 4
N_DEV = 8
VMEM_LIMIT_BYTES = 56 * 1024 * 1024
MESH = pl.DeviceIdType.MESH
ANY = pl.BlockSpec(memory_space=pl.ANY)


def _params(*sem):
    return pltpu.CompilerParams(dimension_semantics=sem, vmem_limit_bytes=VMEM_LIMIT_BYTES)


def _dot(a, b):
    return jnp.dot(a, b, preferred_element_type=F32)


def _dot_nt(a, b):
    return lax.dot_general(a, b, (((1,), (1,)), ((), ())), preferred_element_type=F32)


def _dot_tn(a, b):
    return lax.dot_general(a, b, (((0,), (0,)), ((), ())), preferred_element_type=F32)


def _sigmoid(x):
    return 1.0 / (1.0 + jnp.exp(-x))


def _mm(a, b, *, mode, tm, tn, tk, out_dtype, name, add=None, after=None):
    if mode == "nn":
        (m, k), n = a.shape, b.shape[1]
    elif mode == "nt":
        (m, k), n = a.shape, b.shape[0]
    else:
        (k, m), n = a.shape, b.shape[1]
    tm, tn, tk = min(tm, m), min(tn, n), min(tk, k)
    assert m % tm == 0 and n % tn == 0 and k % tk == 0, (name, m, n, k)
    nk = k // tk
    dot = {"nn": _dot, "nt": _dot_nt, "tn": _dot_tn}[mode]
    a_spec = (pl.BlockSpec((tk, tm), lambda i, j, kk: (kk, i)) if mode == "tn"
              else pl.BlockSpec((tm, tk), lambda i, j, kk: (i, kk)))
    b_spec = (pl.BlockSpec((tn, tk), lambda i, j, kk: (j, kk)) if mode == "nt"
              else pl.BlockSpec((tk, tn), lambda i, j, kk: (kk, j)))
    o_spec = pl.BlockSpec((tm, tn), lambda i, j, kk: (i, j))
    has_add = add is not None

    def body(*refs):
        a_ref, b_ref = refs[0], refs[1]
        add_ref = refs[2] if has_add else None
        o_ref, acc_ref = refs[-2], refs[-1]
        p = dot(a_ref[...], b_ref[...])

        def finish(total):
            if has_add:
                total = total + add_ref[...].astype(F32)
            o_ref[...] = total.astype(out_dtype)

        if nk == 1:
            finish(p)
        else:
            kk = pl.program_id(2)

            @pl.when(kk == 0)
            def _():
                acc_ref[...] = p

            @pl.when(jnp.logical_and(kk > 0, kk < nk - 1))
            def _():
                acc_ref[...] += p

            @pl.when(kk == nk - 1)
            def _():
                finish(acc_ref[...] + p)

    in_specs = [a_spec, b_spec] + ([o_spec] if has_add else []) + ([ANY] if after is not None else [])
    args = (a, b) + ((add,) if has_add else ()) + ((after,) if after is not None else ())
    return pl.pallas_call(
        body, name=name, grid=(m // tm, n // tn, nk), in_specs=in_specs, out_specs=o_spec,
        out_shape=jax.ShapeDtypeStruct((m, n), out_dtype),
        scratch_shapes=[pltpu.VMEM((tm, tn) if nk > 1 else (8, 128), F32)],
        compiler_params=_params("parallel", "parallel", "arbitrary"),
    )(*args)


def _mm_shifted(a, b, shift, *, b_shifted, first, count, total, tm, tn, out_dtype, name, into=None, after=None):
    m, k = a.shape
    tm = min(tm, m)
    assert m % tm == 0
    col = lambda j, shift_ref: (shift_ref[0] + first + j) % total
    extra = [arr for arr in (into, after) if arr is not None]

    def body(shift_ref, a_ref, b_ref, *rest):
        del shift_ref
        rest[-1][...] = _dot(a_ref[...], b_ref[...]).astype(out_dtype)

    return pl.pallas_call(
        body, name=name, out_shape=jax.ShapeDtypeStruct((m, total * tn), out_dtype),
        grid_spec=pltpu.PrefetchScalarGridSpec(
            num_scalar_prefetch=1, grid=(m // tm, count),
            in_specs=[pl.BlockSpec((tm, k), lambda i, j, s: (i, 0)),
                      pl.BlockSpec((k, tn), (lambda i, j, s: (0, col(j, s))) if b_shifted else (lambda i, j, s: (0, j)))]
            + [ANY] * len(extra),
            out_specs=pl.BlockSpec((tm, tn), lambda i, j, s: (i, col(j, s)))),
        input_output_aliases={3: 0} if into is not None else {},
        compiler_params=_params("parallel", "arbitrary"))(shift, a, b, *extra)


def _mm_rows(a, b, rows_in, vecs_in, out_shapes, epilogue, *, tm, name):
    m, k = a.shape
    n = b.shape[1]
    assert m % tm == 0
    row = pl.BlockSpec((tm, n), lambda i: (i, 0))

    def body(a_ref, b_ref, *rest):
        epilogue(_dot(a_ref[...], b_ref[...]), *rest)

    out_specs = [row if tuple(s.shape) == (m, n) else pl.BlockSpec(s.shape, lambda i: (0, 0)) for s in out_shapes]
    return pl.pallas_call(
        body, name=name, grid=(m // tm,),
        in_specs=[pl.BlockSpec((tm, k), lambda i: (i, 0)), pl.BlockSpec((k, n), lambda i: (0, 0))] + [row] * len(rows_in)
        + [pl.BlockSpec((1, n), lambda i: (0, 0))] * len(vecs_in),
        out_specs=out_specs, out_shape=out_shapes, compiler_params=_params("arbitrary"),
    )(a, b, *rows_in, *vecs_in)


def _cast_into_gathered(w, kind, chip, name, keep_shard=False):
    r, c = w.shape
    tr = min(r, 512)
    per = r // tr

    def body(chip_ref, w_ref, *outs):
        del chip_ref
        for o_ref in outs:
            o_ref[...] = w_ref[...].astype(BF16)

    if kind == "col":
        shape, out_map = (r, N_CHIPS * c), (lambda i, chip_ref: (i, chip_ref[0]))
    else:
        shape, out_map = (N_CHIPS * r, c), (lambda i, chip_ref: (chip_ref[0] * per + i, 0))
    plain = pl.BlockSpec((tr, c), lambda i, chip_ref: (i, 0))
    out = pl.pallas_call(
        body, name=name,
        out_shape=[jax.ShapeDtypeStruct(shape, BF16)] + ([jax.ShapeDtypeStruct((r, c), BF16)] if keep_shard else []),
        grid_spec=pltpu.PrefetchScalarGridSpec(
            num_scalar_prefetch=1, grid=(per,), in_specs=[plain],
            out_specs=[pl.BlockSpec((tr, c), out_map)] + ([plain] if keep_shard else [])),
        compiler_params=_params("parallel"))(chip, w)
    return out if keep_shard else out[0]


NORM_ROWS = 512


def _rmsnorm(x, w, name, after=None):
    s, d = x.shape
    tr = NORM_ROWS

    def body(x_ref, w_ref, *rest):
        xv = x_ref[...]
        rstd = lax.rsqrt(jnp.mean(xv * xv, axis=-1, keepdims=True) + EPS)
        rest[-1][...] = (xv * rstd * w_ref[...]).astype(BF16)

    return pl.pallas_call(
        body, name=name, grid=(s // tr,),
        in_specs=[pl.BlockSpec((tr, d), lambda i: (i, 0)), pl.BlockSpec((1, d), lambda i: (0, 0))]
        + ([ANY] if after is not None else []),
        out_specs=pl.BlockSpec((tr, d), lambda i: (i, 0)),
        out_shape=jax.ShapeDtypeStruct((s, d), BF16), compiler_params=_params("parallel"),
    )(x, w, *((after,) if after is not None else ()))


def _rmsnorm_bwd(x, w, dh, dres, name, out_dtype):
    s, d = x.shape
    tr = NORM_ROWS

    def body(x_ref, w_ref, dh_ref, dres_ref, dx_ref, dw_ref):
        xv = x_ref[...]
        rstd = lax.rsqrt(jnp.mean(xv * xv, axis=-1, keepdims=True) + EPS)
        nrm = xv * rstd
        dhv = dh_ref[...].astype(F32)
        dn = dhv * w_ref[...]
        dx = dres_ref[...].astype(F32) + rstd * (dn - nrm * jnp.mean(dn * nrm, axis=-1, keepdims=True))
        dx_ref[...] = dx.astype(out_dtype)

        @pl.when(pl.program_id(0) == 0)
        def _():
            dw_ref[...] = jnp.zeros_like(dw_ref)

        dw_ref[...] += jnp.sum(dhv * nrm, axis=0, keepdims=True)

    row = pl.BlockSpec((tr, d), lambda i: (i, 0))
    return pl.pallas_call(
        body, name=name, grid=(s // tr,),
        in_specs=[row, pl.BlockSpec((1, d), lambda i: (0, 0)), row, row],
        out_specs=[row, pl.BlockSpec((8, d), lambda i: (0, 0))],
        out_shape=[jax.ShapeDtypeStruct((s, d), out_dtype), jax.ShapeDtypeStruct((8, d), F32)],
        compiler_params=_params("arbitrary"))(x, w, dh, dres)


def _adamw(w, g, m, v, name):
    r, c = w.shape
    tr = min(r, 256)
    assert r % tr == 0

    def body(w_ref, g_ref, m_ref, v_ref, g_out, d_ref, nm_ref, nv_ref):
        gv = g_ref[...]
        g_out[...] = gv
        nm = ADAM_B1 * m_ref[...] + (1.0 - ADAM_B1) * gv
        nv = ADAM_B2 * v_ref[...] + (1.0 - ADAM_B2) * (gv * gv)
        m_hat = nm / (1.0 - ADAM_B1 ** ADAM_STEP)
        v_hat = nv / (1.0 - ADAM_B2 ** ADAM_STEP)
        d_ref[...] = -ADAM_LR * (m_hat / (jnp.sqrt(v_hat) + ADAM_EPS) + ADAM_WD * w_ref[...])
        nm_ref[...] = nm
        nv_ref[...] = nv

    spec = pl.BlockSpec((tr, c), lambda i: (i, 0))
    shp = jax.ShapeDtypeStruct((r, c), F32)
    return pl.pallas_call(body, name=name, grid=(r // tr,), in_specs=[spec] * 4, out_specs=[spec] * 4,
                          out_shape=[shp] * 4, compiler_params=_params("parallel"))(w, g, m, v)


def _rope_tables(s, dim):
    inv = 1.0 / (ROPE_THETA ** (jnp.arange(0, dim, 2, dtype=F32) / dim))
    ang = jnp.arange(s).astype(F32)[:, None] * inv[None, :]
    return jnp.cos(ang), jnp.sin(ang)


def _rope_half(x, cos, sin):
    h = x.shape[1] // 2
    x1, x2 = x[:, :h], x[:, h:]
    return jnp.concatenate([x1 * cos - x2 * sin, x2 * cos + x1 * sin], axis=1)


def _unrope_half(dy, cos, sin):
    h = dy.shape[1] // 2
    d1, d2 = dy[:, :h], dy[:, h:]
    return jnp.concatenate([d1 * cos + d2 * sin, d2 * cos - d1 * sin], axis=1)


def _lane(shape):
    return lax.broadcasted_iota(jnp.int32, shape, 1)


def _partner64(x):
    w = x.shape[1]
    first = (_lane(x.shape) % HEAD_DIM) < (HEAD_DIM // 2)
    return jnp.where(first, pltpu.roll(x, w - HEAD_DIM // 2, 1), pltpu.roll(x, HEAD_DIM // 2, 1))


def _tile_lanes(t, reps):
    return t if reps == 1 else jnp.concatenate([t] * reps, axis=1)


def _group_mean(x, ones_bd, passes=2):
    hi = x.astype(BF16)
    if passes == 1:
        return _dot(hi, ones_bd)
    lo = (x - hi.astype(F32)).astype(BF16)
    return _dot(hi, ones_bd) + _dot(lo, ones_bd)


def _block_diag_mean(width):
    idx = jnp.arange(width) // HEAD_DIM
    return jnp.where(idx[:, None] == idx[None, :], 1.0 / HEAD_DIM, 0.0).astype(BF16)


RET_STEP = 4


def _retention_tables():
    h = RET_HEADS
    log_g = jnp.log(1.0 - 2.0 ** (-5.0 - jnp.arange(h, dtype=F32)))
    idx = jnp.arange(BLK, dtype=F32)
    diff = idx[:, None] - idx[None, :]
    intra = jnp.where(diff >= 0, jnp.exp(log_g[:, None, None] * jnp.maximum(diff, 0.0)), 0.0).astype(F32)
    q_dec = jnp.exp(log_g[:, None] * (idx[None, :] + 1.0)).astype(F32)[:, :, None]
    k_dec = jnp.exp(log_g[:, None] * (BLK - 1.0 - idx[None, :])).astype(F32)[:, :, None]
    chunk_dec = jnp.exp(log_g * BLK).astype(F32)[:, None, None]
    return intra, q_dec, k_dec, chunk_dec


def _retention_fwd(proj, cos, sin, tables):
    s = proj.shape[0]
    nc = s // BLK
    intra, q_dec, k_dec, chunk_dec = tables

    def body(p_ref, cos_ref, sin_ref, in_ref, qd_ref, kd_ref, cd_ref, o_ref, cat_ref, st_ref, state):
        @pl.when(pl.program_id(0) == 0)
        def _():
            state[...] = jnp.zeros_like(state)

        for c in range(RET_STEP):
            rows = slice(c * BLK, (c + 1) * BLK)
            cosv, sinv = cos_ref[rows, :], sin_ref[rows, :]
            for h in range(RET_HEADS):
                c0 = h * RET_DIM
                q = p_ref[rows, c0:c0 + RET_DIM].astype(F32)
                k = p_ref[rows, RET_WIDTH + c0:RET_WIDTH + c0 + RET_DIM].astype(F32)
                v = p_ref[rows, 2 * RET_WIDTH + c0:2 * RET_WIDTH + c0 + RET_DIM]
                g = p_ref[rows, 3 * RET_WIDTH + c0:3 * RET_WIDTH + c0 + RET_DIM].astype(F32)
                qb = _rope_half(q, cosv, sinv).astype(BF16)
                kr = _rope_half(k, cosv, sinv) * (RET_DIM ** -0.5)
                kb = kr.astype(BF16)
                scores = _dot_nt(qb, kb) * in_ref[h]
                inner = _dot(scores.astype(BF16), v)
                prev = state[h]
                prev_b = prev.astype(BF16)
                st_ref[h, c] = prev_b
                o = inner + _dot(qb, prev_b) * qd_ref[h]
                o_ref[rows, c0:c0 + RET_DIM] = o
                rstd = lax.rsqrt(jnp.mean(o * o, axis=-1, keepdims=True) + EPS)
                cat_ref[rows, c0:c0 + RET_DIM] = (o * rstd * (g * _sigmoid(g))).astype(BF16)
                state[h] = cd_ref[h] * prev + _dot_tn((kr * kd_ref[h]).astype(BF16), v)

    full = lambda shape: pl.BlockSpec(shape, lambda n: (0,) * len(shape))
    step = RET_STEP * BLK
    return pl.pallas_call(
        body, name="retention_fwd", grid=(nc // RET_STEP,),
        in_specs=[pl.BlockSpec((step, 4 * RET_WIDTH), lambda n: (n, 0)),
                  pl.BlockSpec((step, RET_DIM // 2), lambda n: (n, 0)), pl.BlockSpec((step, RET_DIM // 2), lambda n: (n, 0)),
                  full((RET_HEADS, BLK, BLK)), full((RET_HEADS, BLK, 1)), full((RET_HEADS, BLK, 1)), full((RET_HEADS, 1, 1))],
        out_specs=[pl.BlockSpec((step, RET_WIDTH), lambda n: (n, 0)), pl.BlockSpec((step, RET_WIDTH), lambda n: (n, 0)),
                   pl.BlockSpec((RET_HEADS, RET_STEP, RET_DIM, RET_DIM), lambda n: (0, n, 0, 0))],
        out_shape=[jax.ShapeDtypeStruct((s, RET_WIDTH), F32), jax.ShapeDtypeStruct((s, D_MODEL), BF16),
                   jax.ShapeDtypeStruct((RET_HEADS, nc, RET_DIM, RET_DIM), BF16)],
        scratch_shapes=[pltpu.VMEM((RET_HEADS, RET_DIM, RET_DIM), F32)],
        compiler_params=_params("arbitrary"))(proj, cos, sin, intra, q_dec, k_dec, chunk_dec)


def _retention_bwd(proj, o, states, dcat, cos, sin, tables):
    s = proj.shape[0]
    nc = s // BLK
    intra, q_dec, k_dec, chunk_dec = tables

    def body(p_ref, o_ref, st_ref, dc_ref, cos_ref, sin_ref, in_ref, qd_ref, kd_ref, cd_ref, dp_ref, dstate):
        @pl.when(pl.program_id(0) == 0)
        def _():
            dstate[...] = jnp.zeros_like(dstate)

        for c in reversed(range(RET_STEP)):
            rows = slice(c * BLK, (c + 1) * BLK)
            cosv, sinv = cos_ref[rows, :], sin_ref[rows, :]
            for h in range(RET_HEADS):
                c0 = h * RET_DIM
                q = p_ref[rows, c0:c0 + RET_DIM].astype(F32)
                k = p_ref[rows, RET_WIDTH + c0:RET_WIDTH + c0 + RET_DIM].astype(F32)
                v = p_ref[rows, 2 * RET_WIDTH + c0:2 * RET_WIDTH + c0 + RET_DIM]
                g = p_ref[rows, 3 * RET_WIDTH + c0:3 * RET_WIDTH + c0 + RET_DIM].astype(F32)
                o = o_ref[rows, c0:c0 + RET_DIM]
                dc = dc_ref[rows, c0:c0 + RET_DIM].astype(F32)
                rstd = lax.rsqrt(jnp.mean(o * o, axis=-1, keepdims=True) + EPS)
                nrm = o * rstd
                sg = _sigmoid(g)
                dg = dc * nrm * (sg * (1.0 + g * (1.0 - sg)))
                dn = dc * (g * sg)
                do = rstd * (dn - nrm * jnp.mean(dn * nrm, axis=-1, keepdims=True))
                qb = _rope_half(q, cosv, sinv).astype(BF16)
                kr = _rope_half(k, cosv, sinv) * (RET_DIM ** -0.5)
                kb = kr.astype(BF16)
                mask = in_ref[h]
                qd, kd = qd_ref[h], kd_ref[h]
                prev_b = st_ref[h, c]
                dnext = dstate[h]
                dnext_b = dnext.astype(BF16)
                att = (_dot_nt(qb, kb) * mask).astype(BF16)
                do_b = do.astype(BF16)
                doq = (do * qd).astype(BF16)
                dv = _dot_tn(att, do_b) + _dot((kr * kd).astype(BF16), dnext_b)
                ds = (_dot_nt(do_b, v) * mask).astype(BF16)
                dqr = _dot(ds, kb) + _dot_nt(doq, prev_b)
                dkr = _dot_tn(ds, qb) + _dot_nt(v, dnext_b) * kd
                dstate[h] = cd_ref[h] * dnext + _dot_tn(qb, doq)
                dq = _unrope_half(dqr, cosv, sinv)
                dk = _unrope_half(dkr * (RET_DIM ** -0.5), cosv, sinv)
                dp_ref[rows, c0:c0 + RET_DIM] = dq.astype(BF16)
                dp_ref[rows, RET_WIDTH + c0:RET_WIDTH + c0 + RET_DIM] = dk.astype(BF16)
                dp_ref[rows, 2 * RET_WIDTH + c0:2 * RET_WIDTH + c0 + RET_DIM] = dv.astype(BF16)
                dp_ref[rows, 3 * RET_WIDTH + c0:3 * RET_WIDTH + c0 + RET_DIM] = dg.astype(BF16)

    steps = nc // RET_STEP
    rev = lambda n: steps - 1 - n
    full = lambda shape: pl.BlockSpec(shape, lambda n: (0,) * len(shape))
    step = RET_STEP * BLK
    return pl.pallas_call(
        body, name="retention_bwd", grid=(steps,),
        in_specs=[pl.BlockSpec((step, 4 * RET_WIDTH), lambda n: (rev(n), 0)),
                  pl.BlockSpec((step, RET_WIDTH), lambda n: (rev(n), 0)),
                  pl.BlockSpec((RET_HEADS, RET_STEP, RET_DIM, RET_DIM), lambda n: (0, rev(n), 0, 0)),
                  pl.BlockSpec((step, RET_WIDTH), lambda n: (rev(n), 0)),
                  pl.BlockSpec((step, RET_DIM // 2), lambda n: (rev(n), 0)), pl.BlockSpec((step, RET_DIM // 2), lambda n: (rev(n), 0)),
                  full((RET_HEADS, BLK, BLK)), full((RET_HEADS, BLK, 1)), full((RET_HEADS, BLK, 1)), full((RET_HEADS, 1, 1))],
        out_specs=pl.BlockSpec((step, 4 * RET_WIDTH), lambda n: (rev(n), 0)),
        out_shape=jax.ShapeDtypeStruct((s, EVEN_IN), BF16),
        scratch_shapes=[pltpu.VMEM((RET_HEADS, RET_DIM, RET_DIM), F32)],
        compiler_params=_params("arbitrary"))(proj, o, states, dcat, cos, sin, intra, q_dec, k_dec, chunk_dec)


CONV_ROWS = 256
HALO = 16


def _conv_pieces(p, halo, conv_w, first):
    rows = p.shape[0]
    gb, gc, u, gv = (p[:, i * CONV_WIDTH:(i + 1) * CONV_WIDTH] for i in range(4))
    cu = gc * u
    hcu = halo[:, CONV_WIDTH:2 * CONV_WIDTH] * halo[:, 2 * CONV_WIDTH:3 * CONV_WIDTH]
    hcu = jnp.where(first, 0.0, hcu)
    r1, r2 = hcu[HALO - 1:HALO], hcu[HALO - 2:HALO - 1]
    row = lax.broadcasted_iota(jnp.int32, cu.shape, 0)
    m1 = jnp.where(row == 0, r1, pltpu.roll(cu, 1, 0))
    m2 = jnp.where(row == 0, r2, jnp.where(row == 1, r1, pltpu.roll(cu, 2, 0)))
    del rows, conv_w
    return gb, gc, u, gv, cu, m1, m2


def _conv_fwd(proj, conv_w, cat):
    s = proj.shape[0]
    per = CONV_ROWS // HALO

    def body(p_ref, halo_ref, w_ref, cat_in, cat_ref):
        del cat_in
        first = pl.program_id(0) == 0
        gb, _, _, gv, cu, m1, m2 = _conv_pieces(p_ref[...].astype(F32), halo_ref[...].astype(F32), None, first)
        conv = w_ref[0:1, :] * m2 + w_ref[1:2, :] * m1 + w_ref[2:3, :] * cu
        cat_ref[...] = (gb * conv * (gv * _sigmoid(gv))).astype(BF16)

    return pl.pallas_call(
        body, name="conv_fwd", grid=(s // CONV_ROWS,),
        in_specs=[pl.BlockSpec((CONV_ROWS, 4 * CONV_WIDTH), lambda i: (i, 1)),
                  pl.BlockSpec((HALO, 4 * CONV_WIDTH), lambda i: (jnp.maximum(i * per - 1, 0), 1)),
                  pl.BlockSpec((3, CONV_WIDTH), lambda i: (0, 0)), ANY],
        out_specs=pl.BlockSpec((CONV_ROWS, CONV_WIDTH), lambda i: (i, 1)),
        out_shape=jax.ShapeDtypeStruct(cat.shape, cat.dtype), input_output_aliases={3: 0},
        compiler_params=_params("parallel"))(proj, proj, conv_w, cat)


def _conv_bwd(proj, dcat, conv_w, dproj):
    s = proj.shape[0]
    per = CONV_ROWS // HALO
    last_halo = s // HALO - 1
    nsteps = s // CONV_ROWS

    def body(p_ref, halo_ref, nxt_ref, dc_ref, dnxt_ref, w_ref, dp_in, dp_ref, dw_ref):
        del dp_in
        i = pl.program_id(0)
        gb, gc, u, gv, cu, m1, m2 = _conv_pieces(p_ref[...].astype(F32), halo_ref[...].astype(F32), None, i == 0)
        w0, w1, w2 = w_ref[0:1, :], w_ref[1:2, :], w_ref[2:3, :]
        conv = w0 * m2 + w1 * m1 + w2 * cu
        dco = dc_ref[...].astype(F32)
        sg = _sigmoid(gv)
        silu = gv * sg
        dgb = dco * conv * silu
        dgv = dco * gb * conv * (sg * (1.0 + gv * (1.0 - sg)))
        dconv = dco * gb * silu
        nxt = nxt_ref[...].astype(F32)
        ngv = nxt[:, 3 * CONV_WIDTH:]
        dnext = dnxt_ref[...].astype(F32) * nxt[:, :CONV_WIDTH] * (ngv * _sigmoid(ngv))
        dnext = jnp.where(i == nsteps - 1, 0.0, dnext)
        n1, n2 = dnext[0:1], dnext[1:2]
        row = lax.broadcasted_iota(jnp.int32, dconv.shape, 0)
        p1 = jnp.where(row == CONV_ROWS - 1, n1, pltpu.roll(dconv, CONV_ROWS - 1, 0))
        p2 = jnp.where(row == CONV_ROWS - 1, n2, jnp.where(row == CONV_ROWS - 2, n1, pltpu.roll(dconv, CONV_ROWS - 2, 0)))
        dcu = w2 * dconv + w1 * p1 + w0 * p2
        dp_ref[...] = jnp.concatenate([dgb, dcu * u, dcu * gc, dgv], axis=1).astype(BF16)

        @pl.when(i == 0)
        def _():
            dw_ref[...] = jnp.zeros_like(dw_ref)

        taps = [jnp.sum(dconv * m, axis=0, keepdims=True) for m in (m2, m1, cu)]
        r8 = lax.broadcasted_iota(jnp.int32, dw_ref.shape, 0)
        dw_ref[...] += jnp.where(r8 == 0, taps[0], jnp.where(r8 == 1, taps[1], jnp.where(r8 == 2, taps[2], 0.0)))

    return pl.pallas_call(
        body, name="conv_bwd", grid=(nsteps,),
        in_specs=[pl.BlockSpec((CONV_ROWS, 4 * CONV_WIDTH), lambda i: (i, 1)),
                  pl.BlockSpec((HALO, 4 * CONV_WIDTH), lambda i: (jnp.maximum(i * per - 1, 0), 1)),
                  pl.BlockSpec((HALO, 4 * CONV_WIDTH), lambda i: (jnp.minimum((i + 1) * per, last_halo), 1)),
                  pl.BlockSpec((CONV_ROWS, CONV_WIDTH), lambda i: (i, 1)),
                  pl.BlockSpec((HALO, CONV_WIDTH), lambda i: (jnp.minimum((i + 1) * per, last_halo), 1)),
                  pl.BlockSpec((3, CONV_WIDTH), lambda i: (0, 0)), ANY],
        out_specs=[pl.BlockSpec((CONV_ROWS, 4 * CONV_WIDTH), lambda i: (i, 1)), pl.BlockSpec((8, CONV_WIDTH), lambda i: (0, 0))],
        out_shape=[jax.ShapeDtypeStruct(dproj.shape, dproj.dtype), jax.ShapeDtypeStruct((8, CONV_WIDTH), F32)],
        input_output_aliases={6: 0},
        compiler_params=_params("arbitrary"))(proj, proj, proj, dcat, dcat, conv_w, dproj)


GROUP_WIDTH = 8 * HEAD_DIM
GROUP_HEADS = 8
SLAB = 128
Q_COL = 0
KV_COL = 4
GATE_COL = 5
K_SLAB0 = ATTN_WIDTH // SLAB
V_SLAB0 = (ATTN_WIDTH + KV_WIDTH) // SLAB
KV_ROWS = 1024


def _half_mask(shape, which):
    return (_lane(shape) // HEAD_DIM) == which


def _dup_head(slab, which):
    kept = jnp.where(_half_mask(slab.shape, which), slab, 0.0)
    return kept + pltpu.roll(kept, HEAD_DIM, 1)


def _stack_heads(x):
    parts = []
    for sl in range(GROUP_WIDTH // SLAB):
        slab = x[:, sl * SLAB:(sl + 1) * SLAB]
        parts += [jnp.where(_half_mask(slab.shape, e), slab, 0.0) for e in range(2)]
    return jnp.concatenate(parts, axis=0)


def _unstack_heads(y):
    slabs = []
    for sl in range(GROUP_WIDTH // SLAB):
        a, b = y[(2 * sl) * BLK:(2 * sl + 1) * BLK], y[(2 * sl + 1) * BLK:(2 * sl + 2) * BLK]
        slabs.append(jnp.where(_half_mask(a.shape, 0), a, b))
    return jnp.concatenate(slabs, axis=1)


def _q_prep(q, qw, cosf, sins, ones_bd):
    rstd = lax.rsqrt(_group_mean(q * q, ones_bd) + EPS)
    nrm = q * rstd
    y = nrm * qw
    return nrm, rstd, y * cosf + _partner64(y) * sins


def _band(tri_ref, n):
    own = tri_ref[...] > 0.5
    return own, jnp.where(jnp.logical_and(n == 0, jnp.logical_not(own)), -1e30, 0.0)


def _fold(pair, own):
    return jnp.where(own, pair[:, BLK:], pair[:, :BLK])


def _unfold(folded, own):
    return jnp.concatenate([jnp.where(own, 0.0, folded), jnp.where(own, folded, 0.0)], axis=1)


def _head_probs(raw_scores, sink, own, bias):
    sc = _fold(raw_scores, own) * (HEAD_DIM ** -0.5) + bias
    m = jnp.maximum(jnp.max(sc, axis=-1, keepdims=True), sink)
    p = jnp.exp(sc - m)
    psink = jnp.exp(sink - m)
    inv = 1.0 / (jnp.sum(p, axis=-1, keepdims=True) + psink)
    return p * inv, psink * inv


def _k_prep(k, kw, cosf, sins, ones_bd):
    rstd = lax.rsqrt(_group_mean(k * k, ones_bd) + EPS)
    nrm = k * rstd
    y = nrm * kw
    return nrm, rstd, y * cosf + _partner64(y) * sins


def _qk_prep(proj, qw, kw, cos, sins, ones_q, ones_kv):
    s = proj.shape[0]
    rows = min(KV_ROWS, s)

    def body(p_ref, qw_ref, kw_ref, cos_ref, sin_ref, oq_ref, ok_ref, o_ref):
        j = pl.program_id(1)

        @pl.when(j < KV_COL)
        def _():
            cosf, sinf = _tile_lanes(cos_ref[...], 4), _tile_lanes(sin_ref[...], 4)
            o_ref[...] = _q_prep(p_ref[...].astype(F32), qw_ref[...], cosf, sinf, oq_ref[...])[2].astype(BF16)

        @pl.when(j == KV_COL)
        def _():
            cosf, sinf = _tile_lanes(cos_ref[...], 2), _tile_lanes(sin_ref[...], 2)
            kr = _k_prep(p_ref[:, :KV_WIDTH].astype(F32), kw_ref[...], cosf, sinf, ok_ref[...])[2]
            o_ref[...] = jnp.concatenate([kr.astype(BF16), p_ref[:, KV_WIDTH:]], axis=1)

    full = lambda shape: pl.BlockSpec(shape, lambda i, j: (0,) * len(shape))
    tab = pl.BlockSpec((rows, SLAB), lambda i, j: (i, 0))
    blk = pl.BlockSpec((rows, GROUP_WIDTH), lambda i, j: (i, j))
    return pl.pallas_call(
        body, name="qk_prep", grid=(s // rows, KV_COL + 1),
        in_specs=[blk, full((1, GROUP_WIDTH)), full((1, KV_WIDTH)), tab, tab, full((GROUP_WIDTH, GROUP_WIDTH)),
                  full((KV_WIDTH, KV_WIDTH))],
        out_specs=blk, out_shape=jax.ShapeDtypeStruct((s, ATTN_WIDTH + 2 * KV_WIDTH), BF16),
        compiler_params=_params("parallel", "arbitrary"))(proj, qw, kw, cos, sins, ones_q, ones_kv)


def _keys_values(kc_ref, kp_ref, vc_ref, vp_ref, head):
    lanes = slice((head // 2) * SLAB, (head // 2 + 1) * SLAB)
    dup = lambda ref: _dup_head(ref[:, lanes].astype(F32), head % 2)
    return (jnp.concatenate([dup(kp_ref), dup(kc_ref)], axis=0).astype(BF16),
            jnp.concatenate([dup(vp_ref), dup(vc_ref)], axis=0).astype(BF16))


FWD_STEP_HEADS = 4
BWD_STEP_HEADS = 2


def _swa_specs(heads):
    kv_width = heads * HEAD_DIM
    prev = lambda n: jnp.maximum(n - 1, 0)
    kv = lambda col0, row: pl.BlockSpec((BLK, kv_width), lambda gs, n: (row(n), col0 // kv_width + gs))
    cur = lambda n: n
    full = lambda shape: pl.BlockSpec(shape, lambda gs, n: (0,) * len(shape))
    gate = lambda t: pl.BlockSpec((BLK, GROUP_WIDTH), lambda gs, n: (n, GATE_COL + heads * gs + t))
    return dict(
        sinks=pl.BlockSpec(memory_space=pltpu.SMEM), gates=[gate(t) for t in range(heads)],
        kc=kv(ATTN_WIDTH, cur), kp=kv(ATTN_WIDTH, prev), vc=kv(ATTN_WIDTH + KV_WIDTH, cur), vp=kv(ATTN_WIDTH + KV_WIDTH, prev),
        tri=full((BLK, BLK)), step=pl.BlockSpec((BLK, heads * GROUP_WIDTH), lambda gs, n: (n, gs)))


def _lower_triangle():
    return jnp.tril(jnp.ones((BLK, BLK), F32))


def _swa_fwd(qk, proj, sinks):
    s = proj.shape[0]
    nb = s // BLK
    heads = FWD_STEP_HEADS
    sp = _swa_specs(heads)

    def body(sink_ref, q_ref, kc_ref, kp_ref, vc_ref, vp_ref, *rest):
        gate_refs, (tri_ref, ag_ref, o_ref) = rest[:heads], rest[heads:]
        gs, n = pl.program_id(0), pl.program_id(1)
        own, bias = _band(tri_ref, n)
        for t, gate_ref in enumerate(gate_refs):
            cols = slice(t * GROUP_WIDTH, (t + 1) * GROUP_WIDTH)
            first_head = (heads * gs + t) * GROUP_HEADS
            kcat, vcat = _keys_values(kc_ref, kp_ref, vc_ref, vp_ref, t)
            scores = _dot_nt(_stack_heads(q_ref[:, cols]), kcat)
            probs = []
            for j in range(GROUP_HEADS):
                p, _ = _head_probs(scores[j * BLK:(j + 1) * BLK], sink_ref[first_head + j], own, bias)
                probs.append(_unfold(p, own).astype(BF16))
            o = _unstack_heads(_dot(jnp.concatenate(probs, axis=0), vcat))
            gate = gate_ref[...].astype(F32)
            o_ref[:, cols] = o.astype(BF16)
            ag_ref[:, cols] = (o * (gate * _sigmoid(gate))).astype(BF16)

    shp = jax.ShapeDtypeStruct((s, ATTN_WIDTH), BF16)
    return pl.pallas_call(
        body, name="swa_fwd", grid=(KV_HEADS // heads, nb),
        in_specs=[sp["sinks"], sp["step"], sp["kc"], sp["kp"], sp["vc"], sp["vp"], *sp["gates"], sp["tri"]],
        out_specs=[sp["step"], sp["step"]], out_shape=[shp, shp],
        compiler_params=_params("parallel", "arbitrary"),
    )(sinks, qk, qk, qk, qk, qk, *[proj] * heads, _lower_triangle())


def _swa_bwd(qk, proj, dag, sinks):
    s = proj.shape[0]
    nb = s // BLK
    heads = BWD_STEP_HEADS
    sp = _swa_specs(heads)

    def body(sink_ref, q_ref, kc_ref, kp_ref, vc_ref, vp_ref, *rest):
        gate_refs = rest[:heads]
        dag_ref, tri_ref, dq_ref, dkc_ref, dkp_ref, dvc_ref, dvp_ref, dsink_ref = rest[heads:]
        gs, n = pl.program_id(0), pl.program_id(1)
        own, bias = _band(tri_ref, n)

        @pl.when(n == 0)
        def _():
            dsink_ref[...] = jnp.zeros_like(dsink_ref)

        for t, gate_ref in enumerate(gate_refs):
            cols = slice(t * GROUP_WIDTH, (t + 1) * GROUP_WIDTH)
            first_head = (heads * gs + t) * GROUP_HEADS
            kcat, vcat = _keys_values(kc_ref, kp_ref, vc_ref, vp_ref, t)
            gate = gate_ref[...].astype(F32)
            do = dag_ref[:, cols].astype(F32) * (gate * _sigmoid(gate))
            q_stack = _stack_heads(q_ref[:, cols])
            do_stack = _stack_heads(do).astype(BF16)
            scores = _dot_nt(q_stack, kcat)
            dprobs = _dot_nt(do_stack, vcat)
            probs, dscores, dsinks = [], [], []
            for j in range(GROUP_HEADS):
                rows = slice(j * BLK, (j + 1) * BLK)
                p, psink = _head_probs(scores[rows], sink_ref[first_head + j], own, bias)
                dp = _fold(dprobs[rows], own)
                delta = jnp.sum(p * dp, axis=-1, keepdims=True)
                probs.append(_unfold(p, own).astype(BF16))
                dscores.append(_unfold(p * (dp - delta) * (HEAD_DIM ** -0.5), own).astype(BF16))
                dsinks.append(-jnp.sum(psink * delta, axis=0, keepdims=True))
            ds = jnp.concatenate(dscores, axis=0)
            dk = _dot_tn(ds, q_stack)
            dv = _dot_tn(jnp.concatenate(probs, axis=0), do_stack)
            dk = dk + pltpu.roll(dk, HEAD_DIM, 1)
            dv = dv + pltpu.roll(dv, HEAD_DIM, 1)
            dkp_ref[t], dkc_ref[t] = dk[:BLK], dk[BLK:]
            dvp_ref[t], dvc_ref[t] = dv[:BLK], dv[BLK:]
            dq_ref[:, cols] = _unstack_heads(_dot(ds, kcat)).astype(BF16)
            r8 = lax.broadcasted_iota(jnp.int32, (8, SLAB), 0)
            upd = jnp.zeros((8, SLAB), F32)
            for j in range(GROUP_HEADS):
                upd = jnp.where(r8 == j, dsinks[j], upd)
            dsink_ref[t] += upd

    cur_out = pl.BlockSpec((heads, BLK, SLAB), lambda gs, n: (gs, n, 0))
    prev_out = pl.BlockSpec((heads, BLK, SLAB), lambda gs, n: (gs, (n + nb - 1) % nb, 0))
    kv_shape = jax.ShapeDtypeStruct((KV_HEADS, s, SLAB), F32)
    return pl.pallas_call(
        body, name="swa_bwd", grid=(KV_HEADS // heads, nb),
        in_specs=[sp["sinks"], sp["step"], sp["kc"], sp["kp"], sp["vc"], sp["vp"], *sp["gates"], sp["step"], sp["tri"]],
        out_specs=[sp["step"], cur_out, prev_out, cur_out, prev_out,
                   pl.BlockSpec((heads, 8, SLAB), lambda gs, n: (gs, 0, 0))],
        out_shape=[jax.ShapeDtypeStruct((s, ATTN_WIDTH), BF16), kv_shape, kv_shape, kv_shape, kv_shape,
                   jax.ShapeDtypeStruct((KV_HEADS, 8, SLAB), F32)],
        compiler_params=_params("parallel", "arbitrary"),
    )(sinks, qk, qk, qk, qk, qk, *[proj] * heads, dag, _lower_triangle())


def _swa_bwd_finish(proj, dqr, o, dag, dkc, dkp, dvc, dvp, qw, kw, cos, sins, ones_q, ones_kv):
    s = proj.shape[0]
    rows = min(KV_ROWS, s)
    n_q = KV_COL
    q_of = lambda j: jnp.clip(j - 1, 0, n_q - 1)
    gate_of = lambda j: jnp.clip(j - 1 - n_q, 0, n_q - 1)

    def body(kv_ref, p_ref, dqr_ref, o_ref, dag_ref, dkc_ref, dkp_ref, dvc_ref, dvp_ref, qw_ref, kw_ref, cos_ref, sin_ref,
             oq_ref, ok_ref, dp_ref, dqw_ref, dkw_ref):
        j, i = pl.program_id(0), pl.program_id(1)

        @pl.when(j == 0)
        def _():
            @pl.when(i == 0)
            def _():
                dkw_ref[...] = jnp.zeros_like(dkw_ref)

            def assemble(cur_ref, prv_ref):
                tot = [cur_ref[h] + prv_ref[h] for h in range(KV_HEADS)]
                first = _half_mask(tot[0].shape, 0)
                return jnp.concatenate([jnp.where(first, tot[0], tot[1]), jnp.where(first, tot[2], tot[3])], axis=1)

            dkr = assemble(dkc_ref, dkp_ref)
            dv = assemble(dvc_ref, dvp_ref)
            cosf, sinf = _tile_lanes(cos_ref[...], 2), _tile_lanes(sin_ref[...], 2)
            nrm, rstd, _ = _k_prep(kv_ref[:, :KV_WIDTH].astype(F32), kw_ref[...], cosf, sinf, ok_ref[...])
            dy = dkr * cosf + _partner64(dkr * sinf)
            dn = dy * kw_ref[...]
            dk = rstd * (dn - nrm * _group_mean(dn * nrm, ok_ref[...], passes=1))
            dp_ref[...] = jnp.concatenate([dk, dv], axis=1).astype(BF16)
            dkw_ref[...] += jnp.sum(dy * nrm, axis=0, keepdims=True)

        @pl.when(jnp.logical_and(j >= 1, j <= n_q))
        def _():
            @pl.when(i == 0)
            def _():
                dqw_ref[...] = jnp.zeros_like(dqw_ref)

            cosf, sinf = _tile_lanes(cos_ref[...], 4), _tile_lanes(sin_ref[...], 4)
            nrm, rstd, _ = _q_prep(p_ref[...].astype(F32), qw_ref[...], cosf, sinf, oq_ref[...])
            dq = dqr_ref[...].astype(F32)
            dy = dq * cosf + _partner64(dq * sinf)
            dn = dy * qw_ref[...]
            dp_ref[...] = (rstd * (dn - nrm * _group_mean(dn * nrm, oq_ref[...], passes=1))).astype(BF16)
            dqw_ref[0] += jnp.sum(dy * nrm, axis=0, keepdims=True)

        @pl.when(j > n_q)
        def _():
            gate = p_ref[...].astype(F32)
            sg = _sigmoid(gate)
            dp_ref[...] = (dag_ref[...].astype(F32) * o_ref[...].astype(F32) * (sg * (1.0 + gate * (1.0 - sg)))).astype(BF16)

    first_pass = lambda j, i: jnp.where(j == 0, i, 0)
    acc = pl.BlockSpec((KV_HEADS, rows, SLAB), lambda j, i: (0, first_pass(j, i), 0))
    full = lambda shape: pl.BlockSpec(shape, lambda j, i: (0,) * len(shape))
    tab = pl.BlockSpec((rows, SLAB), lambda j, i: (i, 0))
    out_col = lambda j: jnp.where(j == 0, KV_COL, jnp.where(j <= n_q, j - 1, j))
    return pl.pallas_call(
        body, name="swa_bwd_finish", grid=(2 * n_q + 1, s // rows),
        in_specs=[pl.BlockSpec((rows, GROUP_WIDTH), lambda j, i: (first_pass(j, i), KV_COL)),
                  pl.BlockSpec((rows, GROUP_WIDTH), lambda j, i: (jnp.where(j == 0, 0, i), jnp.where(j <= n_q, q_of(j), j))),
                  pl.BlockSpec((rows, GROUP_WIDTH), lambda j, i: (jnp.where(jnp.logical_and(j >= 1, j <= n_q), i, 0), q_of(j))),
                  pl.BlockSpec((rows, GROUP_WIDTH), lambda j, i: (jnp.where(j > n_q, i, 0), gate_of(j))),
                  pl.BlockSpec((rows, GROUP_WIDTH), lambda j, i: (jnp.where(j > n_q, i, 0), gate_of(j))),
                  acc, acc, acc, acc, full((1, GROUP_WIDTH)), full((1, KV_WIDTH)), tab, tab,
                  full((GROUP_WIDTH, GROUP_WIDTH)), full((KV_WIDTH, KV_WIDTH))],
        out_specs=[pl.BlockSpec((rows, GROUP_WIDTH), lambda j, i: (i, out_col(j))),
                   pl.BlockSpec((1, 8, GROUP_WIDTH), lambda j, i: (q_of(j), 0, 0)), pl.BlockSpec((8, KV_WIDTH), lambda j, i: (0, 0))],
        out_shape=[jax.ShapeDtypeStruct((s, ODD_IN), BF16), jax.ShapeDtypeStruct((n_q, 8, GROUP_WIDTH), F32),
                   jax.ShapeDtypeStruct((8, KV_WIDTH), F32)],
        compiler_params=_params("arbitrary", "arbitrary"),
    )(proj, proj, dqr, o, dag, dkc, dkp, dvc, dvp, qw, kw, cos, sins, ones_q, ones_kv)


def _place():
    x, y, c = lax.axis_index("x"), lax.axis_index("y"), lax.axis_index("c")
    return x, y, c


OTHER_CHIPS = ((1, 0), (0, 1), (1, 1))


def _half_rows(ref, half, rows):
    return ref.at[pl.ds(pl.multiple_of(half * (rows // 2), 8), rows // 2)]


DMA_CHUNK_BYTES = 1 << 20
BF16_TILE_ROWS = 16


def _n_chunks(ref):
    rows = ref.shape[-2]
    nbytes = math.prod(ref.shape) * jnp.dtype(ref.dtype).itemsize
    n = 1
    while 2 * n * DMA_CHUNK_BYTES <= nbytes and rows % (2 * n * BF16_TILE_ROWS) == 0:
        n *= 2
    return n


def _row_chunk(ref, k, n):
    rows = ref.shape[-2] // n
    return ref.at[pl.ds(k * rows, rows)] if len(ref.shape) == 2 else ref.at[:, pl.ds(k * rows, rows)]


def _push(src, dst, send_sem, recv_sem, device_id):
    n = _n_chunks(src)
    for k in range(n):
        pltpu.make_async_remote_copy(src_ref=_row_chunk(src, k, n), dst_ref=_row_chunk(dst, k, n), send_sem=send_sem,
                                     recv_sem=recv_sem, device_id=device_id, device_id_type=MESH).start()
    return pltpu.make_async_remote_copy(src_ref=src, dst_ref=dst, send_sem=send_sem, recv_sem=recv_sem,
                                        device_id=device_id, device_id_type=MESH)


def _copy(src, dst, sem):
    n = _n_chunks(src)
    for k in range(n):
        pltpu.make_async_copy(_row_chunk(src, k, n), _row_chunk(dst, k, n), sem).start()
    return pltpu.make_async_copy(src, dst, sem)


HBM = pl.BlockSpec(memory_space=pltpu.HBM)
SEM = pl.BlockSpec(memory_space=pltpu.SEMAPHORE)
SPLIT_COPY_EFFECT = pltpu.SideEffectType.DATAFLOW_SIDE_EFFECTING


def _in_hbm(a):
    return pltpu.with_memory_space_constraint(a, pltpu.HBM)


def _start_copies(name, arrays, plan, n_copies, after=None):
    n = len(arrays)

    def body(*refs):
        send_sem, recv_sem = refs[-n - 3], refs[-n - 2]
        for k, (src, dst, peer) in enumerate(plan(refs[:n])):
            _push(src, dst, send_sem.at[k], recv_sem.at[k], peer)
        refs[-1][...] = jnp.zeros_like(refs[-1])

    dma = pltpu.SemaphoreType.DMA((n_copies,))
    outs = pl.pallas_call(
        body, name=name,
        out_shape=(dma, dma, *[pltpu.HBM(a.shape, a.dtype) for a in arrays], jax.ShapeDtypeStruct((8, 128), F32)),
        in_specs=[HBM] * n + ([ANY] if after is not None else []),
        out_specs=(SEM, SEM, *[HBM] * n, pl.BlockSpec(memory_space=pltpu.VMEM)),
        input_output_aliases={i: i + 2 for i in range(n)},
        compiler_params=pltpu.CompilerParams(has_side_effects=SPLIT_COPY_EFFECT),
    )(*[_in_hbm(a) for a in arrays], *((after,) if after is not None else ()))
    return outs[0], outs[1], list(outs[2:2 + n]), outs[-1]


def _wait_copies(name, send_sem, recv_sem, arrays, plan, after):
    n = len(arrays)

    def body(*refs):
        send_ref, recv_ref = refs[n], refs[n + 1]
        for k, (src, dst, peer) in enumerate(plan(refs[:n])):
            cp = pltpu.make_async_remote_copy(src_ref=src, dst_ref=dst, send_sem=send_ref.at[k], recv_sem=recv_ref.at[k],
                                              device_id=peer, device_id_type=MESH)
            cp.wait_send()
            cp.wait_recv()

    return list(pl.pallas_call(
        body, name=name, out_shape=tuple(pltpu.HBM(a.shape, a.dtype) for a in arrays),
        in_specs=[HBM] * n + [SEM, SEM, ANY], out_specs=tuple([HBM] * n),
        input_output_aliases={i: i for i in range(n)},
        compiler_params=pltpu.CompilerParams(has_side_effects=SPLIT_COPY_EFFECT),
    )(*arrays, send_sem, recv_sem, after))


def _gather_region(full, kind, chip, half=None):
    if kind == "whole":
        return full.at[chip]
    if kind == "col":
        rows, width = full.shape[0], full.shape[1] // N_CHIPS
        piece = full.at[:, pl.ds(pl.multiple_of(chip * width, 128), width)]
    else:
        rows = full.shape[0] // N_CHIPS
        piece = full.at[pl.ds(pl.multiple_of(chip * rows, BF16_TILE_ROWS), rows)]
    return piece if half is None else _half_rows(piece, half, rows)


def _gather_plan(kinds):
    def plan(fulls):
        x, y, c = _place()
        copies = []
        for fx, fy in OTHER_CHIPS:
            for full, kind in zip(fulls, kinds):
                mine = _gather_region(full, kind, 2 * x + y, c)
                copies.append((mine, mine, (x ^ fx, y ^ fy, c)))
        return copies

    return plan


def _gather_start(name, fulls, kinds, after=None):
    return _start_copies(name, list(fulls), _gather_plan(kinds), 3 * len(kinds), after)


def _gather_finish(name, started, kinds, after):
    send_sem, recv_sem, fulls, _ = started
    n = len(kinds)
    fulls = _wait_copies(name + "_wait", send_sem, recv_sem, fulls, _gather_plan(kinds), after)
    split = [i for i, kind in enumerate(kinds) if kind != "whole"]

    def body(*refs):
        out = refs[n:2 * n]
        send, recv = refs[2 * n:]
        x, y, c = _place()
        pushes = []
        for r, (fx, fy) in enumerate(OTHER_CHIPS):
            chip = 2 * (x ^ fx) + (y ^ fy)
            for j, i in enumerate(split):
                landed = _gather_region(out[i], kinds[i], chip, c)
                pushes.append(_push(landed, landed, send.at[r * len(split) + j], recv.at[r * len(split) + j], (x, y, 1 - c)))
        for cp in pushes:
            cp.wait_recv()
        for cp in pushes:
            cp.wait_send()

    dma = pltpu.SemaphoreType.DMA
    return pl.pallas_call(
        body, name=name + "_share", in_specs=[ANY] * n, out_specs=[ANY] * n,
        out_shape=[jax.ShapeDtypeStruct(a.shape, a.dtype) for a in fulls],
        input_output_aliases={i: i for i in range(n)},
        scratch_shapes=[dma((3 * len(split),)), dma((3 * len(split),))],
        compiler_params=pltpu.CompilerParams(has_side_effects=True),
    )(*fulls)


def _allreduce_small(v):
    def body(v_ref, out_ref, buf, send_sems, recv_sems):
        x, y, c = _place()
        me = 4 * x + 2 * y + c
        buf[me] = v_ref[...]
        copies = []
        for r in range(1, N_DEV):
            peer = (x ^ (r >> 2), y ^ ((r >> 1) & 1), c ^ (r & 1))
            cp = pltpu.make_async_remote_copy(src_ref=v_ref, dst_ref=buf.at[me], send_sem=send_sems.at[r - 1],
                                              recv_sem=recv_sems.at[r - 1], device_id=peer, device_id_type=MESH)
            cp.start()
            copies.append(cp)
        for cp in copies:
            cp.wait_recv()
        for cp in copies:
            cp.wait_send()
        total = buf[0]
        for d in range(1, N_DEV):
            total = total + buf[d]
        out_ref[...] = total

    vm = pl.BlockSpec(memory_space=pltpu.VMEM)
    return pl.pallas_call(
        body, name="allreduce_small", in_specs=[vm], out_specs=vm, out_shape=jax.ShapeDtypeStruct(v.shape, v.dtype),
        scratch_shapes=[pltpu.VMEM((N_DEV,) + v.shape, v.dtype), pltpu.SemaphoreType.DMA((N_DEV - 1,)),
                        pltpu.SemaphoreType.DMA((N_DEV - 1,))],
        compiler_params=pltpu.CompilerParams(has_side_effects=True),
    )(v)


def _exchange_halves(grads, name):
    n = len(grads)

    def body(*refs):
        g, theirs = refs[:n], refs[n:2 * n]
        send_sem, recv_sem = refs[2 * n:]
        x, y, c = _place()
        copies = []
        for i in range(n):
            half = g[i].shape[1] // 2
            src = g[i].at[:, pl.ds(pl.multiple_of((1 - c) * half, BF16_TILE_ROWS), half)]
            copies.append(_push(src, theirs[i], send_sem.at[i], recv_sem.at[i], (x, y, 1 - c)))
        for cp in copies:
            cp.wait_recv()
            cp.wait_send()

    dma = pltpu.SemaphoreType.DMA
    return pl.pallas_call(
        body, name=name, in_specs=[ANY] * n, out_specs=[ANY] * n,
        out_shape=[jax.ShapeDtypeStruct((a.shape[0], a.shape[1] // 2, a.shape[2]), a.dtype) for a in grads],
        scratch_shapes=[dma((n,)), dma((n,))],
        compiler_params=pltpu.CompilerParams(has_side_effects=True),
    )(*grads)


def _pair_sum(g, theirs, core, name):
    pieces, half, cols = theirs.shape
    tr = min(half, 256)
    per = half // tr

    def body(core_ref, g_ref, t_ref, o_ref):
        del core_ref
        o_ref[...] = (g_ref[...].astype(F32) + t_ref[...].astype(F32)).astype(BF16)

    spec = pl.BlockSpec((1, tr, cols), lambda p, i, core_ref: (p, i, 0))
    return pl.pallas_call(
        body, name=name, out_shape=jax.ShapeDtypeStruct(theirs.shape, BF16),
        grid_spec=pltpu.PrefetchScalarGridSpec(
            num_scalar_prefetch=1, grid=(pieces, per),
            in_specs=[pl.BlockSpec((1, tr, cols), lambda p, i, core_ref: (p, core_ref[0] * per + i, 0)), spec],
            out_specs=spec),
        compiler_params=_params("parallel", "parallel"))(core, g, theirs)


def _scatter_plan(n):
    def plan(refs):
        parts, stacks = refs[:n], refs[n:]
        x, y, c = _place()
        copies = []
        for fx, fy in OTHER_CHIPS:
            chip = 2 * (x ^ fx) + (y ^ fy)
            for part, stack in zip(parts, stacks):
                if part.shape[0] == N_CHIPS:
                    piece = part.at[chip]
                else:
                    width = part.shape[2] // N_CHIPS
                    piece = part.at[0].at[:, pl.ds(pl.multiple_of(chip * width, 128), width)]
                copies.append((piece, stack.at[2 * x + y], (x ^ fx, y ^ fy, c)))
        return copies

    return plan


def _scatter_start(name, parts, after=None):
    def landing(a):
        return (N_CHIPS, a.shape[1], a.shape[2] if a.shape[0] == N_CHIPS else a.shape[2] // N_CHIPS)

    stacks = [lax.empty(landing(a), a.dtype) for a in parts]
    return _start_copies(name, list(parts) + stacks, _scatter_plan(len(parts)), 3 * len(parts), after)


def _scatter_finish(name, started, after):
    send_sem, recv_sem, arrays, _ = started
    n = len(arrays) // 2
    arrays = _wait_copies(name + "_wait", send_sem, recv_sem, arrays, _scatter_plan(n), after)
    return arrays[:n], arrays[n:]


def _sum_chips(part, stack, place, name):
    _, r, c = stack.shape
    tr = 256
    per = r // tr

    def body(place_ref, own_ref, a_ref, b_ref, c_ref, o_ref):
        del place_ref
        total = own_ref[0].astype(F32)
        for ref in (a_ref, b_ref, c_ref):
            total = total + ref[0].astype(F32)
        o_ref[...] = total

    if part.shape[0] == N_CHIPS:
        own = pl.BlockSpec((1, tr, c), lambda i, pr: (pr[1], i, 0))
    else:
        own = pl.BlockSpec((1, tr, c), lambda i, pr: (0, i, pr[1]))
    other = lambda flip: pl.BlockSpec((1, tr, c), lambda i, pr: (pr[1] ^ flip, i, 0))
    return pl.pallas_call(
        body, name=name, out_shape=jax.ShapeDtypeStruct((2 * r, c), F32),
        grid_spec=pltpu.PrefetchScalarGridSpec(
            num_scalar_prefetch=1, grid=(per,), in_specs=[own, other(2), other(1), other(3)],
            out_specs=pl.BlockSpec((tr, c), lambda i, pr: (pr[0] * per + i, 0))),
        compiler_params=_params("parallel"))(place, part, stack, stack, stack)


def _share_halves(fulls):
    n = len(fulls)

    def body(*refs):
        dst = refs[n:2 * n]
        send_sem, recv_sem = refs[2 * n:]
        x, y, c = _place()
        copies = []
        for i in range(n):
            mine = _half_rows(dst[i], c, dst[i].shape[0])
            copies.append(_push(mine, mine, send_sem.at[i], recv_sem.at[i], (x, y, 1 - c)))
        for cp in copies:
            cp.wait_recv()
            cp.wait_send()

    dma = pltpu.SemaphoreType.DMA
    return pl.pallas_call(
        body, name="share_halves", in_specs=[ANY] * n, out_specs=[ANY] * n,
        out_shape=[jax.ShapeDtypeStruct(a.shape, a.dtype) for a in fulls],
        input_output_aliases={i: i for i in range(n)}, scratch_shapes=[dma((n,)), dma((n,))],
        compiler_params=pltpu.CompilerParams(has_side_effects=True),
    )(*fulls)


MM = dict(tm=2048, tn=1024, tk=2048)
MM_LONG_K = dict(tm=1024, tn=1024, tk=4096)


def _local_step(x, target, ev_norm_w, q_norm_w, k_norm_w, sinks, own_first, weights_first, weights_late, emit):
    s = x.shape[0]
    cos_r, sin_r = _rope_tables(s, RET_DIM)
    cos_a, sin_a = _rope_tables(s, HEAD_DIM)
    cos_a = jnp.tile(cos_a, (1, 4))
    sins_a = jnp.tile(jnp.concatenate([-sin_a, sin_a], axis=1), (1, 2))
    tables = _retention_tables()
    ones_q, ones_kv = _block_diag_mean(GROUP_WIDTH), _block_diag_mean(KV_WIDTH)
    qw_g = jnp.tile(q_norm_w, (1, GROUP_WIDTH // HEAD_DIM))
    kw_kv = jnp.tile(k_norm_w, (1, KV_WIDTH // HEAD_DIM))
    sinks1 = sinks.reshape(Q_HEADS)

    own_w_in0, own_block, start_token = own_first
    h0 = _rmsnorm(x, ev_norm_w, "norm0", after=start_token)
    shifted = dict(shift=own_block * (own_w_in0.shape[1] // MM["tn"]), total=EVEN_IN // MM["tn"], tm=MM["tm"], tn=MM["tn"],
                   out_dtype=BF16)
    own_blocks = own_w_in0.shape[1] // MM["tn"]
    proj0 = _mm_shifted(h0, own_w_in0, b_shifted=False, first=0, count=own_blocks, name="proj0_own", **shifted)
    w_in0, conv_w, od_norm_w, token = weights_first(proj0)
    proj0 = _mm_shifted(h0, w_in0, b_shifted=True, first=own_blocks, count=shifted["total"] - own_blocks, name="proj0_rest",
                        into=proj0, after=token, **shifted)
    o_ret, cat, states = _retention_fwd(proj0, cos_r, sin_r, tables)
    cat = _conv_fwd(proj0, conv_w, cat)
    w_out0, w_in1, w_out1 = weights_late(cat)

    def residual_and_norm(prod, x_ref, w_ref, x1_ref, h1_ref):
        x1v = x_ref[...] + prod
        x1_ref[...] = x1v
        rstd = lax.rsqrt(jnp.mean(x1v * x1v, axis=-1, keepdims=True) + EPS)
        h1_ref[...] = (x1v * rstd * w_ref[...]).astype(BF16)

    def residual_and_loss(prod, x1_ref, t_ref, dyb_ref, sq_ref):
        diff = (x1_ref[...] + prod) - t_ref[...]
        dyb_ref[...] = (diff * (1.0 / D_MODEL)).astype(BF16)

        @pl.when(pl.program_id(0) == 0)
        def _():
            sq_ref[...] = jnp.zeros_like(sq_ref)

        sq_ref[...] += jnp.sum(jnp.sum(diff * diff, axis=1, keepdims=True), axis=0, keepdims=True)

    act = lambda dt: jax.ShapeDtypeStruct((s, D_MODEL), dt)
    x1, h1 = _mm_rows(cat, w_out0, [x], [od_norm_w], [act(F32), act(BF16)], residual_and_norm, tm=min(s, 512), name="out0")
    proj1 = _mm(h1, w_in1, mode="nn", out_dtype=BF16, name="proj1", tm=2048, tn=1536, tk=2048)
    qk = _qk_prep(proj1, qw_g, kw_kv, cos_a, sins_a, ones_q, ones_kv)
    ag, o_att = _swa_fwd(qk, proj1, sinks1)
    dy_b, sq = _mm_rows(ag, w_out1, [x1, target], [], [act(BF16), jax.ShapeDtypeStruct((8, 128), F32)],
                        residual_and_loss, tm=min(s, 512), name="out1")

    g_w_out1 = _mm(ag, dy_b, mode="tn", out_dtype=BF16, name="g_w_out1", **MM_LONG_K)
    dag = _mm(dy_b, w_out1, mode="nt", out_dtype=BF16, name="d_ag", **MM)
    dqr, dkc, dkp, dvc, dvp, dsink = _swa_bwd(qk, proj1, dag, sinks1)
    dproj1, dqw, dkw = _swa_bwd_finish(proj1, dqr, o_att, dag, dkc, dkp, dvc, dvp, qw_g, kw_kv, cos_a, sins_a, ones_q, ones_kv)
    g_w_in1 = _mm(h1, dproj1, mode="tn", out_dtype=BF16, name="g_w_in1", tm=1024, tn=768, tk=4096)
    token = emit("layer1", (("od_w_in", g_w_in1, "col"), ("od_w_out", g_w_out1, "row")))
    dh1 = _mm(dproj1, w_in1, mode="nt", out_dtype=BF16, name="d_h1", tm=1024, tn=1024, tk=ODD_IN, after=token)
    dx1_b, g_norm1 = _rmsnorm_bwd(x1, od_norm_w, dh1, dy_b, "norm1_bwd", BF16)

    g_w_out0 = _mm(cat, dx1_b, mode="tn", out_dtype=BF16, name="g_w_out0", **MM_LONG_K)
    token = emit("out0", (("ev_w_out", g_w_out0, "row"),))
    dcat = _mm(dx1_b, w_out0, mode="nt", out_dtype=BF16, name="d_cat", after=token, **MM)
    dproj0 = _retention_bwd(proj0, o_ret, states, dcat, cos_r, sin_r, tables)
    dproj0, g_conv = _conv_bwd(proj0, dcat, conv_w, dproj0)
    g_w_in0 = _mm(h0, dproj0, mode="tn", out_dtype=BF16, name="g_w_in0", **MM_LONG_K)
    token = emit("in0", (("ev_w_in", g_w_in0, "col"),))
    dh0 = _mm(dproj0, w_in0, mode="nt", out_dtype=BF16, name="d_h0", after=token, **MM_LONG_K)
    grad_x, g_norm0 = _rmsnorm_bwd(x, ev_norm_w, dh0, dx1_b, "norm0_bwd", F32)

    g_qw = dqw[:, 0, :].reshape(Q_HEADS, HEAD_DIM).sum(axis=0)
    g_kw = dkw[0].reshape(KV_HEADS, HEAD_DIM).sum(axis=0)
    g_sinks = dsink[:, :, 0].reshape(Q_HEADS)
    small = dict(ev_norm=g_norm0[0], od_norm=g_norm1[0], conv=g_conv[:3], qw=g_qw, kw=g_kw, sinks=g_sinks)
    return sq[0, 0], grad_x, small


def _pack_small_grads(small):
    pad = lambda v: jnp.pad(v, (0, D_MODEL - v.shape[0]))
    tail = pad(jnp.concatenate([small["qw"], small["kw"], small["sinks"]]))
    rows = [small["ev_norm"], small["od_norm"]] + [pad(small["conv"][t]) for t in range(3)] + [tail]
    rows += [jnp.zeros((D_MODEL,), F32)] * (8 - len(rows))
    return jnp.stack(rows)


class _ReduceScatter:
    def __init__(self, place):
        self.place = place
        self.started = []

    def send(self, tag, grads):
        pieces = [g[None] if kind == "col" else g.reshape(N_CHIPS, g.shape[0] // N_CHIPS, g.shape[1]) for _, g, kind in grads]
        theirs = _exchange_halves(pieces, "exchange_halves_" + tag)
        parts = [_pair_sum(g, t, self.place[:1], "pair_sum_" + nm) for g, t, (nm, _, _) in zip(pieces, theirs, grads)]
        started = _scatter_start("scatter_" + tag, parts)
        self.started.append((tag, [nm for nm, _, _ in grads], started))
        return started[3]

    def finish(self, after):
        names, sums = [], []
        for tag, group, started in self.started:
            parts, stacks = _scatter_finish("scatter_" + tag, started, after)
            sums += [_sum_chips(p, s, self.place, "chip_sum_" + nm) for p, s, nm in zip(parts, stacks, group)]
            names += group
        return dict(zip(names, _share_halves(sums)))


def kernel(x, ev_norm_w, ev_w_in, ev_conv_w, ev_w_out, od_norm_w, od_w_in, od_q_norm_w, od_k_norm_w, od_sinks, od_w_out, loss_target, m_ev_norm_w, m_ev_w_in, m_ev_conv_w, m_ev_w_out, m_od_norm_w, m_od_w_in, m_od_q_norm_w, m_od_k_norm_w, m_od_sinks, m_od_w_out, v_ev_norm_w, v_ev_w_in, v_ev_conv_w, v_ev_w_out, v_od_norm_w, v_od_w_in, v_od_q_norm_w, v_od_k_norm_w, v_od_sinks, v_od_w_out):
    my_chip = 2 * lax.axis_index("x") + lax.axis_index("y")
    place = jnp.stack([lax.axis_index("c"), my_chip]).astype(jnp.int32)
    shard_w = D_MODEL // N_CHIPS
    conv_shard = CONV_WIDTH // N_CHIPS

    small_in = jnp.zeros((8, shard_w), F32)
    small_in = small_in.at[0].set(od_norm_w[0]).at[1:4, :conv_shard].set(ev_conv_w[0])
    small_in = lax.dynamic_update_slice(jnp.zeros((N_CHIPS, 8, shard_w), F32), small_in[None], (my_chip, 0, 0))
    chip = place[1:]
    first_kinds, late_kinds = ("col", "whole"), ("row", "col", "row")
    w_in0_own_place, w_in0_shard = _cast_into_gathered(ev_w_in[0], "col", chip, "cast_w_in0", keep_shard=True)
    first = _gather_start("gather_first", [w_in0_own_place, small_in], first_kinds)
    late_own = [_cast_into_gathered(ev_w_out[0], "row", chip, "cast_w_out0"),
                _cast_into_gathered(od_w_in[0], "col", chip, "cast_w_in1"),
                _cast_into_gathered(od_w_out[0], "row", chip, "cast_w_out1")]
    late = []

    def weights_first(after):
        w_in0, small_all = _gather_finish("gather_first", first, first_kinds, after)
        late.append(_gather_start("gather_late", late_own, late_kinds, after=w_in0))
        od_norm_full = small_all[:, 0, :].reshape(1, D_MODEL)
        conv_full = jnp.transpose(small_all[:, 1:4, :conv_shard], (1, 0, 2)).reshape(3, CONV_WIDTH)
        return w_in0, conv_full, od_norm_full, late[0][3]

    def weights_late(after):
        return _gather_finish("gather_late", late[0], late_kinds, after)

    reduce_scatter = _ReduceScatter(place)
    sq, grad_x, small = _local_step(x[0], loss_target[0], ev_norm_w, od_q_norm_w, od_k_norm_w, od_sinks,
                                    (w_in0_shard, chip, first[3]), weights_first, weights_late, reduce_scatter.send)
    loss = lax.psum(0.5 * sq / D_MODEL, ("x", "y", "c"))

    big = reduce_scatter.finish(grad_x)
    g_ev_w_in, g_ev_w_out, g_od_w_in, g_od_w_out = big["ev_w_in"], big["ev_w_out"], big["od_w_in"], big["od_w_out"]
    tot = _allreduce_small(_pack_small_grads(small))
    g_ev_norm = tot[0:1]
    g_od_norm = lax.dynamic_slice(tot, (1, my_chip * shard_w), (1, shard_w))
    g_conv = lax.dynamic_slice(tot, (2, my_chip * conv_shard), (3, conv_shard))
    g_qw, g_kw, g_sinks = tot[5:6, 0:HEAD_DIM], tot[5:6, HEAD_DIM:2 * HEAD_DIM], tot[5:6, 2 * HEAD_DIM:2 * HEAD_DIM + Q_HEADS]

    upd = {}
    upd["ev_w_in"] = _adamw(ev_w_in[0], g_ev_w_in, m_ev_w_in[0], v_ev_w_in[0], "adamw_ev_w_in")
    upd["ev_w_out"] = _adamw(ev_w_out[0], g_ev_w_out, m_ev_w_out[0], v_ev_w_out[0], "adamw_ev_w_out")
    upd["od_w_in"] = _adamw(od_w_in[0], g_od_w_in, m_od_w_in[0], v_od_w_in[0], "adamw_od_w_in")
    upd["od_w_out"] = _adamw(od_w_out[0], g_od_w_out, m_od_w_out[0], v_od_w_out[0], "adamw_od_w_out")
    smalls = (("ev_norm_w", ev_norm_w, g_ev_norm, m_ev_norm_w, v_ev_norm_w),
              ("ev_conv_w", ev_conv_w, g_conv, m_ev_conv_w, v_ev_conv_w),
              ("od_norm_w", od_norm_w, g_od_norm, m_od_norm_w, v_od_norm_w),
              ("od_q_norm_w", od_q_norm_w, g_qw, m_od_q_norm_w, v_od_q_norm_w),
              ("od_k_norm_w", od_k_norm_w, g_kw, m_od_k_norm_w, v_od_k_norm_w),
              ("od_sinks", od_sinks, g_sinks, m_od_sinks, v_od_sinks))
    sizes = [w.size for _, w, _, _, _ in smalls]
    padded = 8 * 128 * math.ceil(sum(sizes) / (8 * 128))
    pack = lambda arrs, fill: jnp.concatenate(
        [a.reshape(-1) for a in arrs] + [jnp.full((padded - sum(sizes),), fill, F32)]).reshape(8, padded // 8)
    packed = _adamw(pack([w for _, w, _, _, _ in smalls], 0.0), pack([g for _, _, g, _, _ in smalls], 0.0),
                    pack([m for _, _, _, m, _ in smalls], 0.0), pack([v for _, _, _, _, v in smalls], 1.0), "adamw_small")
    offs = [sum(sizes[:i]) for i in range(len(sizes))]
    for (nm, w, _, _, _), off, size in zip(smalls, offs, sizes):
        upd[nm] = tuple(p.reshape(-1)[off:off + size].reshape(w.shape) for p in packed)
    for nm in ("ev_w_in", "ev_w_out", "od_w_in", "od_w_out"):
        upd[nm] = tuple(u[None] for u in upd[nm])
    order = ("ev_norm_w", "ev_w_in", "ev_conv_w", "ev_w_out", "od_norm_w", "od_w_in", "od_q_norm_w", "od_k_norm_w", "od_sinks", "od_w_out")
    return (loss, grad_x[None], *[upd[nm][0] for nm in order], *[upd[nm][1] for nm in order],
            *[upd[nm][2] for nm in order], *[upd[nm][3] for nm in order])
```

```python
import functools
import math

import jax
import jax.numpy as jnp
import numpy as np
from jax import lax
from jax.experimental import pallas as pl
from jax.experimental.pallas import tpu as pltpu

F32 = jnp.float32
BF16 = jnp.bfloat16

D_MODEL = 2048
RET_HEADS = 4
RET_DIM = 256
RET_WIDTH = 1024
CONV_WIDTH = 1024
EVEN_IN = 8192
Q_HEADS = 32
HEAD_DIM = 64
KV_HEADS = 4
KV_WIDTH = 256
ATTN_WIDTH = 2048
ODD_IN = 4608
BLK = 128
ROPE_THETA = 10000.0
EPS = 1e-6
ADAM_LR = 0.001
ADAM_B1 = 0.9
ADAM_B2 = 0.999
ADAM_EPS = 1e-08
ADAM_WD = 0.01
ADAM_STEP = 10
N_CHIPS = 4
N_DEV = 8
VMEM_LIMIT_BYTES = 56 * 1024 * 1024
MESH = pl.DeviceIdType.MESH
ANY = pl.BlockSpec(memory_space=pl.ANY)


def _params(*sem):
    return pltpu.CompilerParams(dimension_semantics=sem, vmem_limit_bytes=VMEM_LIMIT_BYTES)


def _dot(a, b):
    return jnp.dot(a, b, preferred_element_type=F32)


def _dot_nt(a, b):
    return lax.dot_general(a, b, (((1,), (1,)), ((), ())), preferred_element_type=F32)


def _dot_tn(a, b):
    return lax.dot_general(a, b, (((0,), (0,)), ((), ())), preferred_element_type=F32)


def _sigmoid(x):
    return 1.0 / (1.0 + jnp.exp(-x))


def _mm(a, b, *, mode, tm, tn, tk, out_dtype, name, add=None, after=None):
    if mode == "nn":
        (m, k), n = a.shape, b.shape[1]
    elif mode == "nt":
        (m, k), n = a.shape, b.shape[0]
    else:
        (k, m), n = a.shape, b.shape[1]
    tm, tn, tk = min(tm, m), min(tn, n), min(tk, k)
    assert m % tm == 0 and n % tn == 0 and k % tk == 0, (name, m, n, k)
    nk = k // tk
    dot = {"nn": _dot, "nt": _dot_nt, "tn": _dot_tn}[mode]
    a_spec = (pl.BlockSpec((tk, tm), lambda i, j, kk: (kk, i)) if mode == "tn"
              else pl.BlockSpec((tm, tk), lambda i, j, kk: (i, kk)))
    b_spec = (pl.BlockSpec((tn, tk), lambda i, j, kk: (j, kk)) if mode == "nt"
              else pl.BlockSpec((tk, tn), lambda i, j, kk: (kk, j)))
    o_spec = pl.BlockSpec((tm, tn), lambda i, j, kk: (i, j))
    has_add = add is not None

    def body(*refs):
        a_ref, b_ref = refs[0], refs[1]
        add_ref = refs[2] if has_add else None
        o_ref, acc_ref = refs[-2], refs[-1]
        p = dot(a_ref[...], b_ref[...])

        def finish(total):
            if has_add:
                total = total + add_ref[...].astype(F32)
            o_ref[...] = total.astype(out_dtype)

        if nk == 1:
            finish(p)
        else:
            kk = pl.program_id(2)

            @pl.when(kk == 0)
            def _():
                acc_ref[...] = p

            @pl.when(jnp.logical_and(kk > 0, kk < nk - 1))
            def _():
                acc_ref[...] += p

            @pl.when(kk == nk - 1)
            def _():
                finish(acc_ref[...] + p)

    in_specs = [a_spec, b_spec] + ([o_spec] if has_add else []) + ([ANY] if after is not None else [])
    args = (a, b) + ((add,) if has_add else ()) + ((after,) if after is not None else ())
    return pl.pallas_call(
        body, name=name, grid=(m // tm, n // tn, nk), in_specs=in_specs, out_specs=o_spec,
        out_shape=jax.ShapeDtypeStruct((m, n), out_dtype),
        scratch_shapes=[pltpu.VMEM((tm, tn) if nk > 1 else (8, 128), F32)],
        compiler_params=_params("parallel", "parallel", "arbitrary"),
    )(*args)


def _mm_shifted(a, b, shift, *, b_shifted, first, count, total, tm, tn, out_dtype, name, into=None, after=None):
    m, k = a.shape
    tm = min(tm, m)
    assert m % tm == 0
    col = lambda j, shift_ref: (shift_ref[0] + first + j) % total
    extra = [arr for arr in (into, after) if arr is not None]

    def body(shift_ref, a_ref, b_ref, *rest):
        del shift_ref
        rest[-1][...] = _dot(a_ref[...], b_ref[...]).astype(out_dtype)

    return pl.pallas_call(
        body, name=name, out_shape=jax.ShapeDtypeStruct((m, total * tn), out_dtype),
        grid_spec=pltpu.PrefetchScalarGridSpec(
            num_scalar_prefetch=1, grid=(m // tm, count),
            in_specs=[pl.BlockSpec((tm, k), lambda i, j, s: (i, 0)),
                      pl.BlockSpec((k, tn), (lambda i, j, s: (0, col(j, s))) if b_shifted else (lambda i, j, s: (0, j)))]
            + [ANY] * len(extra),
            out_specs=pl.BlockSpec((tm, tn), lambda i, j, s: (i, col(j, s)))),
        input_output_aliases={3: 0} if into is not None else {},
        compiler_params=_params("parallel", "arbitrary"))(shift, a, b, *extra)


def _mm_rows(a, b, rows_in, vecs_in, out_shapes, epilogue, *, tm, name):
    m, k = a.shape
    n = b.shape[1]
    assert m % tm == 0
    row = pl.BlockSpec((tm, n), lambda i: (i, 0))

    def body(a_ref, b_ref, *rest):
        epilogue(_dot(a_ref[...], b_ref[...]), *rest)

    out_specs = [row if tuple(s.shape) == (m, n) else pl.BlockSpec(s.shape, lambda i: (0, 0)) for s in out_shapes]
    return pl.pallas_call(
        body, name=name, grid=(m // tm,),
        in_specs=[pl.BlockSpec((tm, k), lambda i: (i, 0)), pl.BlockSpec((k, n), lambda i: (0, 0))] + [row] * len(rows_in)
        + [pl.BlockSpec((1, n), lambda i: (0, 0))] * len(vecs_in),
        out_specs=out_specs, out_shape=out_shapes, compiler_params=_params("arbitrary"),
    )(a, b, *rows_in, *vecs_in)


def _cast_into_gathered(w, kind, chip, name, keep_shard=False):
    r, c = w.shape
    tr = min(r, 512)
    per = r // tr

    def body(chip_ref, w_ref, *outs):
        del chip_ref
        for o_ref in outs:
            o_ref[...] = w_ref[...].astype(BF16)

    if kind == "col":
        shape, out_map = (r, N_CHIPS * c), (lambda i, chip_ref: (i, chip_ref[0]))
    else:
        shape, out_map = (N_CHIPS * r, c), (lambda i, chip_ref: (chip_ref[0] * per + i, 0))
    plain = pl.BlockSpec((tr, c), lambda i, chip_ref: (i, 0))
    out = pl.pallas_call(
        body, name=name,
        out_shape=[jax.ShapeDtypeStruct(shape, BF16)] + ([jax.ShapeDtypeStruct((r, c), BF16)] if keep_shard else []),
        grid_spec=pltpu.PrefetchScalarGridSpec(
            num_scalar_prefetch=1, grid=(per,), in_specs=[plain],
            out_specs=[pl.BlockSpec((tr, c), out_map)] + ([plain] if keep_shard else [])),
        compiler_params=_params("parallel"))(chip, w)
    return out if keep_shard else out[0]


NORM_ROWS = 512


def _rmsnorm(x, w, name, after=None):
    s, d = x.shape
    tr = NORM_ROWS

    def body(x_ref, w_ref, *rest):
        xv = x_ref[...]
        rstd = lax.rsqrt(jnp.mean(xv * xv, axis=-1, keepdims=True) + EPS)
        rest[-1][...] = (xv * rstd * w_ref[...]).astype(BF16)

    return pl.pallas_call(
        body, name=name, grid=(s // tr,),
        in_specs=[pl.BlockSpec((tr, d), lambda i: (i, 0)), pl.BlockSpec((1, d), lambda i: (0, 0))]
        + ([ANY] if after is not None else []),
        out_specs=pl.BlockSpec((tr, d), lambda i: (i, 0)),
        out_shape=jax.ShapeDtypeStruct((s, d), BF16), compiler_params=_params("parallel"),
    )(x, w, *((after,) if after is not None else ()))


def _rmsnorm_bwd(x, w, dh, dres, name, out_dtype):
    s, d = x.shape
    tr = NORM_ROWS

    def body(x_ref, w_ref, dh_ref, dres_ref, dx_ref, dw_ref):
        xv = x_ref[...]
        rstd = lax.rsqrt(jnp.mean(xv * xv, axis=-1, keepdims=True) + EPS)
        nrm = xv * rstd
        dhv = dh_ref[...].astype(F32)
        dn = dhv * w_ref[...]
        dx = dres_ref[...].astype(F32) + rstd * (dn - nrm * jnp.mean(dn * nrm, axis=-1, keepdims=True))
        dx_ref[...] = dx.astype(out_dtype)

        @pl.when(pl.program_id(0) == 0)
        def _():
            dw_ref[...] = jnp.zeros_like(dw_ref)

        dw_ref[...] += jnp.sum(dhv * nrm, axis=0, keepdims=True)

    row = pl.BlockSpec((tr, d), lambda i: (i, 0))
    return pl.pallas_call(
        body, name=name, grid=(s // tr,),
        in_specs=[row, pl.BlockSpec((1, d), lambda i: (0, 0)), row, row],
        out_specs=[row, pl.BlockSpec((8, d), lambda i: (0, 0))],
        out_shape=[jax.ShapeDtypeStruct((s, d), out_dtype), jax.ShapeDtypeStruct((8, d), F32)],
        compiler_params=_params("arbitrary"))(x, w, dh, dres)


def _adamw(w, g, m, v, name):
    r, c = w.shape
    tr = min(r, 256)
    assert r % tr == 0

    def body(w_ref, g_ref, m_ref, v_ref, g_out, d_ref, nm_ref, nv_ref):
        gv = g_ref[...]
        g_out[...] = gv
        nm = ADAM_B1 * m_ref[...] + (1.0 - ADAM_B1) * gv
        nv = ADAM_B2 * v_ref[...] + (1.0 - ADAM_B2) * (gv * gv)
        m_hat = nm / (1.0 - ADAM_B1 ** ADAM_STEP)
        v_hat = nv / (1.0 - ADAM_B2 ** ADAM_STEP)
        d_ref[...] = -ADAM_LR * (m_hat / (jnp.sqrt(v_hat) + ADAM_EPS) + ADAM_WD * w_ref[...])
        nm_ref[...] = nm
        nv_ref[...] = nv

    spec = pl.BlockSpec((tr, c), lambda i: (i, 0))
    shp = jax.ShapeDtypeStruct((r, c), F32)
    return pl.pallas_call(body, name=name, grid=(r // tr,), in_specs=[spec] * 4, out_specs=[spec] * 4,
                          out_shape=[shp] * 4, compiler_params=_params("parallel"))(w, g, m, v)


def _rope_tables(s, dim):
    inv = (1.0 / (ROPE_THETA ** (np.arange(0, dim, 2, dtype=np.float64) / dim))).astype(np.float32)
    ang = (np.arange(s, dtype=np.float32)[:, None] * inv[None, :]).astype(np.float64)
    return np.cos(ang).astype(np.float32), np.sin(ang).astype(np.float32)


def _rope_half(x, cos, sin):
    h = x.shape[1] // 2
    x1, x2 = x[:, :h], x[:, h:]
    return jnp.concatenate([x1 * cos - x2 * sin, x2 * cos + x1 * sin], axis=1)


def _unrope_half(dy, cos, sin):
    h = dy.shape[1] // 2
    d1, d2 = dy[:, :h], dy[:, h:]
    return jnp.concatenate([d1 * cos + d2 * sin, d2 * cos - d1 * sin], axis=1)


def _lane(shape):
    return lax.broadcasted_iota(jnp.int32, shape, 1)


def _partner64(x):
    w = x.shape[1]
    first = (_lane(x.shape) % HEAD_DIM) < (HEAD_DIM // 2)
    return jnp.where(first, pltpu.roll(x, w - HEAD_DIM // 2, 1), pltpu.roll(x, HEAD_DIM // 2, 1))


def _tile_lanes(t, reps):
    return t if reps == 1 else jnp.concatenate([t] * reps, axis=1)


def _group_mean(x, ones_bd, passes=2):
    hi = x.astype(BF16)
    if passes == 1:
        return _dot(hi, ones_bd)
    lo = (x - hi.astype(F32)).astype(BF16)
    return _dot(hi, ones_bd) + _dot(lo, ones_bd)


def _block_diag_mean(width):
    idx = jnp.arange(width) // HEAD_DIM
    return jnp.where(idx[:, None] == idx[None, :], 1.0 / HEAD_DIM, 0.0).astype(BF16)


RET_STEP = 4


def _retention_tables():
    h = RET_HEADS
    log_g = jnp.log(1.0 - 2.0 ** (-5.0 - jnp.arange(h, dtype=F32)))
    idx = jnp.arange(BLK, dtype=F32)
    diff = idx[:, None] - idx[None, :]
    intra = jnp.where(diff >= 0, jnp.exp(log_g[:, None, None] * jnp.maximum(diff, 0.0)), 0.0).astype(F32)
    q_dec = jnp.exp(log_g[:, None] * (idx[None, :] + 1.0)).astype(F32)[:, :, None]
    k_dec = jnp.exp(log_g[:, None] * (BLK - 1.0 - idx[None, :])).astype(F32)[:, :, None]
    chunk_dec = jnp.exp(log_g * BLK).astype(F32)[:, None, None]
    return intra, q_dec, k_dec, chunk_dec


def _retention_fwd(proj, cos, sin, tables):
    s = proj.shape[0]
    nc = s // BLK
    intra, q_dec, k_dec, chunk_dec = tables

    def body(p_ref, cos_ref, sin_ref, in_ref, qd_ref, kd_ref, cd_ref, o_ref, cat_ref, st_ref, state):
        @pl.when(pl.program_id(0) == 0)
        def _():
            state[...] = jnp.zeros_like(state)

        for c in range(RET_STEP):
            rows = slice(c * BLK, (c + 1) * BLK)
            cosv, sinv = cos_ref[rows, :], sin_ref[rows, :]
            for h in range(RET_HEADS):
                c0 = h * RET_DIM
                q = p_ref[rows, c0:c0 + RET_DIM].astype(F32)
                k = p_ref[rows, RET_WIDTH + c0:RET_WIDTH + c0 + RET_DIM].astype(F32)
                v = p_ref[rows, 2 * RET_WIDTH + c0:2 * RET_WIDTH + c0 + RET_DIM]
                g = p_ref[rows, 3 * RET_WIDTH + c0:3 * RET_WIDTH + c0 + RET_DIM].astype(F32)
                qb = _rope_half(q, cosv, sinv).astype(BF16)
                kr = _rope_half(k, cosv, sinv) * (RET_DIM ** -0.5)
                kb = kr.astype(BF16)
                scores = _dot_nt(qb, kb) * in_ref[h]
                inner = _dot(scores.astype(BF16), v)
                prev = state[h]
                prev_b = prev.astype(BF16)
                st_ref[h, c] = prev_b
                o = inner + _dot(qb, prev_b) * qd_ref[h]
                o_ref[rows, c0:c0 + RET_DIM] = o
                rstd = lax.rsqrt(jnp.mean(o * o, axis=-1, keepdims=True) + EPS)
                cat_ref[rows, c0:c0 + RET_DIM] = (o * rstd * (g * _sigmoid(g))).astype(BF16)
                state[h] = cd_ref[h] * prev + _dot_tn((kr * kd_ref[h]).astype(BF16), v)

    full = lambda shape: pl.BlockSpec(shape, lambda n: (0,) * len(shape))
    step = RET_STEP * BLK
    return pl.pallas_call(
        body, name="retention_fwd", grid=(nc // RET_STEP,),
        in_specs=[pl.BlockSpec((step, 4 * RET_WIDTH), lambda n: (n, 0)),
                  pl.BlockSpec((step, RET_DIM // 2), lambda n: (n, 0)), pl.BlockSpec((step, RET_DIM // 2), lambda n: (n, 0)),
                  full((RET_HEADS, BLK, BLK)), full((RET_HEADS, BLK, 1)), full((RET_HEADS, BLK, 1)), full((RET_HEADS, 1, 1))],
        out_specs=[pl.BlockSpec((step, RET_WIDTH), lambda n: (n, 0)), pl.BlockSpec((step, RET_WIDTH), lambda n: (n, 0)),
                   pl.BlockSpec((RET_HEADS, RET_STEP, RET_DIM, RET_DIM), lambda n: (0, n, 0, 0))],
        out_shape=[jax.ShapeDtypeStruct((s, RET_WIDTH), F32), jax.ShapeDtypeStruct((s, D_MODEL), BF16),
                   jax.ShapeDtypeStruct((RET_HEADS, nc, RET_DIM, RET_DIM), BF16)],
        scratch_shapes=[pltpu.VMEM((RET_HEADS, RET_DIM, RET_DIM), F32)],
        compiler_params=_params("arbitrary"))(proj, cos, sin, intra, q_dec, k_dec, chunk_dec)


def _retention_bwd(proj, o, states, dcat, cos, sin, tables):
    s = proj.shape[0]
    nc = s // BLK
    intra, q_dec, k_dec, chunk_dec = tables

    def body(p_ref, o_ref, st_ref, dc_ref, cos_ref, sin_ref, in_ref, qd_ref, kd_ref, cd_ref, dp_ref, dstate):
        @pl.when(pl.program_id(0) == 0)
        def _():
            dstate[...] = jnp.zeros_like(dstate)

        for c in reversed(range(RET_STEP)):
            rows = slice(c * BLK, (c + 1) * BLK)
            cosv, sinv = cos_ref[rows, :], sin_ref[rows, :]
            for h in range(RET_HEADS):
                c0 = h * RET_DIM
                q = p_ref[rows, c0:c0 + RET_DIM].astype(F32)
                k = p_ref[rows, RET_WIDTH + c0:RET_WIDTH + c0 + RET_DIM].astype(F32)
                v = p_ref[rows, 2 * RET_WIDTH + c0:2 * RET_WIDTH + c0 + RET_DIM]
                g = p_ref[rows, 3 * RET_WIDTH + c0:3 * RET_WIDTH + c0 + RET_DIM].astype(F32)
                o = o_ref[rows, c0:c0 + RET_DIM]
                dc = dc_ref[rows, c0:c0 + RET_DIM].astype(F32)
                rstd = lax.rsqrt(jnp.mean(o * o, axis=-1, keepdims=True) + EPS)
                nrm = o * rstd
                sg = _sigmoid(g)
                dg = dc * nrm * (sg * (1.0 + g * (1.0 - sg)))
                dn = dc * (g * sg)
                do = rstd * (dn - nrm * jnp.mean(dn * nrm, axis=-1, keepdims=True))
                qb = _rope_half(q, cosv, sinv).astype(BF16)
                kr = _rope_half(k, cosv, sinv) * (RET_DIM ** -0.5)
                kb = kr.astype(BF16)
                mask = in_ref[h]
                qd, kd = qd_ref[h], kd_ref[h]
                prev_b = st_ref[h, c]
                dnext = dstate[h]
                dnext_b = dnext.astype(BF16)
                att = (_dot_nt(qb, kb) * mask).astype(BF16)
                do_b = do.astype(BF16)
                doq = (do * qd).astype(BF16)
                dv = _dot_tn(att, do_b) + _dot((kr * kd).astype(BF16), dnext_b)
                ds = (_dot_nt(do_b, v) * mask).astype(BF16)
                dqr = _dot(ds, kb) + _dot_nt(doq, prev_b)
                dkr = _dot_tn(ds, qb) + _dot_nt(v, dnext_b) * kd
                dstate[h] = cd_ref[h] * dnext + _dot_tn(qb, doq)
                dq = _unrope_half(dqr, cosv, sinv)
                dk = _unrope_half(dkr * (RET_DIM ** -0.5), cosv, sinv)
                dp_ref[rows, c0:c0 + RET_DIM] = dq.astype(BF16)
                dp_ref[rows, RET_WIDTH + c0:RET_WIDTH + c0 + RET_DIM] = dk.astype(BF16)
                dp_ref[rows, 2 * RET_WIDTH + c0:2 * RET_WIDTH + c0 + RET_DIM] = dv.astype(BF16)
                dp_ref[rows, 3 * RET_WIDTH + c0:3 * RET_WIDTH + c0 + RET_DIM] = dg.astype(BF16)

    steps = nc // RET_STEP
    rev = lambda n: steps - 1 - n
    full = lambda shape: pl.BlockSpec(shape, lambda n: (0,) * len(shape))
    step = RET_STEP * BLK
    return pl.pallas_call(
        body, name="retention_bwd", grid=(steps,),
        in_specs=[pl.BlockSpec((step, 4 * RET_WIDTH), lambda n: (rev(n), 0)),
                  pl.BlockSpec((step, RET_WIDTH), lambda n: (rev(n), 0)),
                  pl.BlockSpec((RET_HEADS, RET_STEP, RET_DIM, RET_DIM), lambda n: (0, rev(n), 0, 0)),
                  pl.BlockSpec((step, RET_WIDTH), lambda n: (rev(n), 0)),
                  pl.BlockSpec((step, RET_DIM // 2), lambda n: (rev(n), 0)), pl.BlockSpec((step, RET_DIM // 2), lambda n: (rev(n), 0)),
                  full((RET_HEADS, BLK, BLK)), full((RET_HEADS, BLK, 1)), full((RET_HEADS, BLK, 1)), full((RET_HEADS, 1, 1))],
        out_specs=pl.BlockSpec((step, 4 * RET_WIDTH), lambda n: (rev(n), 0)),
        out_shape=jax.ShapeDtypeStruct((s, EVEN_IN), BF16),
        scratch_shapes=[pltpu.VMEM((RET_HEADS, RET_DIM, RET_DIM), F32)],
        compiler_params=_params("arbitrary"))(proj, o, states, dcat, cos, sin, intra, q_dec, k_dec, chunk_dec)


CONV_ROWS = 256
HALO = 16


def _conv_pieces(p, halo, conv_w, first):
    rows = p.shape[0]
    gb, gc, u, gv = (p[:, i * CONV_WIDTH:(i + 1) * CONV_WIDTH] for i in range(4))
    cu = gc * u
    hcu = halo[:, CONV_WIDTH:2 * CONV_WIDTH] * halo[:, 2 * CONV_WIDTH:3 * CONV_WIDTH]
    hcu = jnp.where(first, 0.0, hcu)
    r1, r2 = hcu[HALO - 1:HALO], hcu[HALO - 2:HALO - 1]
    row = lax.broadcasted_iota(jnp.int32, cu.shape, 0)
    m1 = jnp.where(row == 0, r1, pltpu.roll(cu, 1, 0))
    m2 = jnp.where(row == 0, r2, jnp.where(row == 1, r1, pltpu.roll(cu, 2, 0)))
    del rows, conv_w
    return gb, gc, u, gv, cu, m1, m2


def _conv_fwd(proj, conv_w, cat):
    s = proj.shape[0]
    per = CONV_ROWS // HALO

    def body(p_ref, halo_ref, w_ref, cat_in, cat_ref):
        del cat_in
        first = pl.program_id(0) == 0
        gb, _, _, gv, cu, m1, m2 = _conv_pieces(p_ref[...].astype(F32), halo_ref[...].astype(F32), None, first)
        conv = w_ref[0:1, :] * m2 + w_ref[1:2, :] * m1 + w_ref[2:3, :] * cu
        cat_ref[...] = (gb * conv * (gv * _sigmoid(gv))).astype(BF16)

    return pl.pallas_call(
        body, name="conv_fwd", grid=(s // CONV_ROWS,),
        in_specs=[pl.BlockSpec((CONV_ROWS, 4 * CONV_WIDTH), lambda i: (i, 1)),
                  pl.BlockSpec((HALO, 4 * CONV_WIDTH), lambda i: (jnp.maximum(i * per - 1, 0), 1)),
                  pl.BlockSpec((3, CONV_WIDTH), lambda i: (0, 0)), ANY],
        out_specs=pl.BlockSpec((CONV_ROWS, CONV_WIDTH), lambda i: (i, 1)),
        out_shape=jax.ShapeDtypeStruct(cat.shape, cat.dtype), input_output_aliases={3: 0},
        compiler_params=_params("parallel"))(proj, proj, conv_w, cat)


def _conv_bwd(proj, dcat, conv_w, dproj):
    s = proj.shape[0]
    per = CONV_ROWS // HALO
    last_halo = s // HALO - 1
    nsteps = s // CONV_ROWS

    def body(p_ref, halo_ref, nxt_ref, dc_ref, dnxt_ref, w_ref, dp_in, dp_ref, dw_ref):
        del dp_in
        i = pl.program_id(0)
        gb, gc, u, gv, cu, m1, m2 = _conv_pieces(p_ref[...].astype(F32), halo_ref[...].astype(F32), None, i == 0)
        w0, w1, w2 = w_ref[0:1, :], w_ref[1:2, :], w_ref[2:3, :]
        conv = w0 * m2 + w1 * m1 + w2 * cu
        dco = dc_ref[...].astype(F32)
        sg = _sigmoid(gv)
        silu = gv * sg
        dgb = dco * conv * silu
        dgv = dco * gb * conv * (sg * (1.0 + gv * (1.0 - sg)))
        dconv = dco * gb * silu
        nxt = nxt_ref[...].astype(F32)
        ngv = nxt[:, 3 * CONV_WIDTH:]
        dnext = dnxt_ref[...].astype(F32) * nxt[:, :CONV_WIDTH] * (ngv * _sigmoid(ngv))
        dnext = jnp.where(i == nsteps - 1, 0.0, dnext)
        n1, n2 = dnext[0:1], dnext[1:2]
        row = lax.broadcasted_iota(jnp.int32, dconv.shape, 0)
        p1 = jnp.where(row == CONV_ROWS - 1, n1, pltpu.roll(dconv, CONV_ROWS - 1, 0))
        p2 = jnp.where(row == CONV_ROWS - 1, n2, jnp.where(row == CONV_ROWS - 2, n1, pltpu.roll(dconv, CONV_ROWS - 2, 0)))
        dcu = w2 * dconv + w1 * p1 + w0 * p2
        dp_ref[...] = jnp.concatenate([dgb, dcu * u, dcu * gc, dgv], axis=1).astype(BF16)

        @pl.when(i == 0)
        def _():
            dw_ref[...] = jnp.zeros_like(dw_ref)

        taps = [jnp.sum(dconv * m, axis=0, keepdims=True) for m in (m2, m1, cu)]
        r8 = lax.broadcasted_iota(jnp.int32, dw_ref.shape, 0)
        dw_ref[...] += jnp.where(r8 == 0, taps[0], jnp.where(r8 == 1, taps[1], jnp.where(r8 == 2, taps[2], 0.0)))

    return pl.pallas_call(
        body, name="conv_bwd", grid=(nsteps,),
        in_specs=[pl.BlockSpec((CONV_ROWS, 4 * CONV_WIDTH), lambda i: (i, 1)),
                  pl.BlockSpec((HALO, 4 * CONV_WIDTH), lambda i: (jnp.maximum(i * per - 1, 0), 1)),
                  pl.BlockSpec((HALO, 4 * CONV_WIDTH), lambda i: (jnp.minimum((i + 1) * per, last_halo), 1)),
                  pl.BlockSpec((CONV_ROWS, CONV_WIDTH), lambda i: (i, 1)),
                  pl.BlockSpec((HALO, CONV_WIDTH), lambda i: (jnp.minimum((i + 1) * per, last_halo), 1)),
                  pl.BlockSpec((3, CONV_WIDTH), lambda i: (0, 0)), ANY],
        out_specs=[pl.BlockSpec((CONV_ROWS, 4 * CONV_WIDTH), lambda i: (i, 1)), pl.BlockSpec((8, CONV_WIDTH), lambda i: (0, 0))],
        out_shape=[jax.ShapeDtypeStruct(dproj.shape, dproj.dtype), jax.ShapeDtypeStruct((8, CONV_WIDTH), F32)],
        input_output_aliases={6: 0},
        compiler_params=_params("arbitrary"))(proj, proj, proj, dcat, dcat, conv_w, dproj)


GROUP_WIDTH = 8 * HEAD_DIM
GROUP_HEADS = 8
SLAB = 128
Q_COL = 0
KV_COL = 4
GATE_COL = 5
K_SLAB0 = ATTN_WIDTH // SLAB
V_SLAB0 = (ATTN_WIDTH + KV_WIDTH) // SLAB
ATTN_SCALE = HEAD_DIM ** -0.5
KV_ROWS = 1024


def _half_mask(shape, which):
    return (_lane(shape) // HEAD_DIM) == which


def _dup_head(slab, which):
    kept = jnp.where(_half_mask(slab.shape, which), slab, 0.0)
    return kept + pltpu.roll(kept, HEAD_DIM, 1)


def _stack_heads(x):
    parts = []
    for sl in range(GROUP_WIDTH // SLAB):
        slab = x[:, sl * SLAB:(sl + 1) * SLAB]
        parts += [jnp.where(_half_mask(slab.shape, e), slab, 0.0) for e in range(2)]
    return jnp.concatenate(parts, axis=0)


def _unstack_heads(y):
    slabs = []
    for sl in range(GROUP_WIDTH // SLAB):
        a, b = y[(2 * sl) * BLK:(2 * sl + 1) * BLK], y[(2 * sl + 1) * BLK:(2 * sl + 2) * BLK]
        slabs.append(jnp.where(_half_mask(a.shape, 0), a, b))
    return jnp.concatenate(slabs, axis=1)


def _q_prep(q, qw, cosf, sins, ones_bd):
    rstd = lax.rsqrt(_group_mean(q * q, ones_bd) + EPS)
    nrm = q * rstd
    y = nrm * qw
    return nrm, rstd, y * cosf + _partner64(y) * sins


def _band(tri_ref, n):
    own = tri_ref[...] > 0.5
    return own, jnp.where(jnp.logical_and(n == 0, jnp.logical_not(own)), -1e30, 0.0)


def _fold(pair, own):
    return jnp.where(own, pair[:, BLK:], pair[:, :BLK])


def _unfold(folded, own):
    return jnp.concatenate([jnp.where(own, 0.0, folded), jnp.where(own, folded, 0.0)], axis=1)


def _head_probs(raw_scores, sink, own, bias):
    sc = _fold(raw_scores, own) + bias
    m = jnp.maximum(jnp.max(sc, axis=-1, keepdims=True), sink)
    p = jnp.exp(sc - m)
    psink = jnp.exp(sink - m)
    inv = 1.0 / (jnp.sum(p, axis=-1, keepdims=True) + psink)
    return p * inv, psink * inv


def _k_prep(k, kw, cosf, sins, ones_bd):
    rstd = lax.rsqrt(_group_mean(k * k, ones_bd) + EPS)
    nrm = k * rstd
    y = nrm * kw
    return nrm, rstd, y * cosf + _partner64(y) * sins


def _qk_prep(proj, qw, kw, cos, sins, ones_q, ones_kv):
    s = proj.shape[0]
    rows = min(KV_ROWS, s)

    def body(p_ref, qw_ref, kw_ref, cos_ref, sin_ref, oq_ref, ok_ref, o_ref):
        j = pl.program_id(1)

        @pl.when(j < KV_COL)
        def _():
            cosf, sinf = _tile_lanes(cos_ref[...], 4), _tile_lanes(sin_ref[...], 4)
            roped = _q_prep(p_ref[...].astype(F32), qw_ref[...], cosf, sinf, oq_ref[...])[2]
            o_ref[...] = (roped * ATTN_SCALE).astype(BF16)

        @pl.when(j == KV_COL)
        def _():
            cosf, sinf = _tile_lanes(cos_ref[...], 2), _tile_lanes(sin_ref[...], 2)
            kr = _k_prep(p_ref[:, :KV_WIDTH].astype(F32), kw_ref[...], cosf, sinf, ok_ref[...])[2]
            o_ref[...] = jnp.concatenate([kr.astype(BF16), p_ref[:, KV_WIDTH:]], axis=1)

    full = lambda shape: pl.BlockSpec(shape, lambda i, j: (0,) * len(shape))
    tab = pl.BlockSpec((rows, SLAB), lambda i, j: (i, 0))
    blk = pl.BlockSpec((rows, GROUP_WIDTH), lambda i, j: (i, j))
    return pl.pallas_call(
        body, name="qk_prep", grid=(s // rows, KV_COL + 1),
        in_specs=[blk, full((1, GROUP_WIDTH)), full((1, KV_WIDTH)), tab, tab, full((GROUP_WIDTH, GROUP_WIDTH)),
                  full((KV_WIDTH, KV_WIDTH))],
        out_specs=blk, out_shape=jax.ShapeDtypeStruct((s, ATTN_WIDTH + 2 * KV_WIDTH), BF16),
        compiler_params=_params("parallel", "arbitrary"))(proj, qw, kw, cos, sins, ones_q, ones_kv)


def _keys_values(kc_ref, kp_ref, vc_ref, vp_ref, head):
    lanes = slice((head // 2) * SLAB, (head // 2 + 1) * SLAB)
    dup = lambda ref: _dup_head(ref[:, lanes].astype(F32), head % 2)
    return (jnp.concatenate([dup(kp_ref), dup(kc_ref)], axis=0).astype(BF16),
            jnp.concatenate([dup(vp_ref), dup(vc_ref)], axis=0).astype(BF16))


FWD_STEP_HEADS = 4
BWD_STEP_HEADS = 2


def _swa_specs(heads):
    kv_width = heads * HEAD_DIM
    prev = lambda n: jnp.maximum(n - 1, 0)
    kv = lambda col0, row: pl.BlockSpec((BLK, kv_width), lambda gs, n: (row(n), col0 // kv_width + gs))
    cur = lambda n: n
    full = lambda shape: pl.BlockSpec(shape, lambda gs, n: (0,) * len(shape))
    gate = lambda t: pl.BlockSpec((BLK, GROUP_WIDTH), lambda gs, n: (n, GATE_COL + heads * gs + t))
    return dict(
        sinks=pl.BlockSpec(memory_space=pltpu.SMEM), gates=[gate(t) for t in range(heads)],
        kc=kv(ATTN_WIDTH, cur), kp=kv(ATTN_WIDTH, prev), vc=kv(ATTN_WIDTH + KV_WIDTH, cur), vp=kv(ATTN_WIDTH + KV_WIDTH, prev),
        tri=full((BLK, BLK)), step=pl.BlockSpec((BLK, heads * GROUP_WIDTH), lambda gs, n: (n, gs)))


def _lower_triangle():
    return jnp.tril(jnp.ones((BLK, BLK), F32))


def _swa_fwd(qk, proj, sinks):
    s = proj.shape[0]
    nb = s // BLK
    heads = FWD_STEP_HEADS
    sp = _swa_specs(heads)

    def body(sink_ref, q_ref, kc_ref, kp_ref, vc_ref, vp_ref, *rest):
        gate_refs, (tri_ref, ag_ref, o_ref) = rest[:heads], rest[heads:]
        gs, n = pl.program_id(0), pl.program_id(1)
        own, bias = _band(tri_ref, n)
        for t, gate_ref in enumerate(gate_refs):
            cols = slice(t * GROUP_WIDTH, (t + 1) * GROUP_WIDTH)
            first_head = (heads * gs + t) * GROUP_HEADS
            kcat, vcat = _keys_values(kc_ref, kp_ref, vc_ref, vp_ref, t)
            scores = _dot_nt(_stack_heads(q_ref[:, cols]), kcat)
            probs = []
            for j in range(GROUP_HEADS):
                p, _ = _head_probs(scores[j * BLK:(j + 1) * BLK], sink_ref[first_head + j], own, bias)
                probs.append(_unfold(p, own).astype(BF16))
            o = _unstack_heads(_dot(jnp.concatenate(probs, axis=0), vcat))
            gate = gate_ref[...].astype(F32)
            o_ref[:, cols] = o.astype(BF16)
            ag_ref[:, cols] = (o * (gate * _sigmoid(gate))).astype(BF16)

    shp = jax.ShapeDtypeStruct((s, ATTN_WIDTH), BF16)
    return pl.pallas_call(
        body, name="swa_fwd", grid=(KV_HEADS // heads, nb),
        in_specs=[sp["sinks"], sp["step"], sp["kc"], sp["kp"], sp["vc"], sp["vp"], *sp["gates"], sp["tri"]],
        out_specs=[sp["step"], sp["step"]], out_shape=[shp, shp],
        compiler_params=_params("parallel", "arbitrary"),
    )(sinks, qk, qk, qk, qk, qk, *[proj] * heads, _lower_triangle())


def _swa_bwd(qk, proj, dag, sinks):
    s = proj.shape[0]
    nb = s // BLK
    heads = BWD_STEP_HEADS
    sp = _swa_specs(heads)

    def body(sink_ref, q_ref, kc_ref, kp_ref, vc_ref, vp_ref, *rest):
        gate_refs = rest[:heads]
        dag_ref, tri_ref, dq_ref, dkc_ref, dkp_ref, dvc_ref, dvp_ref, dsink_ref = rest[heads:]
        gs, n = pl.program_id(0), pl.program_id(1)
        own, bias = _band(tri_ref, n)

        @pl.when(n == 0)
        def _():
            dsink_ref[...] = jnp.zeros_like(dsink_ref)

        for t, gate_ref in enumerate(gate_refs):
            cols = slice(t * GROUP_WIDTH, (t + 1) * GROUP_WIDTH)
            first_head = (heads * gs + t) * GROUP_HEADS
            kcat, vcat = _keys_values(kc_ref, kp_ref, vc_ref, vp_ref, t)
            gate = gate_ref[...].astype(F32)
            do = dag_ref[:, cols].astype(F32) * (gate * _sigmoid(gate))
            q_stack = _stack_heads(q_ref[:, cols])
            do_stack = _stack_heads(do).astype(BF16)
            scores = _dot_nt(q_stack, kcat)
            dprobs = _dot_nt(do_stack, vcat)
            probs, dscores, dsinks = [], [], []
            for j in range(GROUP_HEADS):
                rows = slice(j * BLK, (j + 1) * BLK)
                p, psink = _head_probs(scores[rows], sink_ref[first_head + j], own, bias)
                dp = _fold(dprobs[rows], own)
                delta = jnp.sum(p * dp, axis=-1, keepdims=True)
                probs.append(_unfold(p, own).astype(BF16))
                dscores.append(_unfold(p * (dp - delta), own).astype(BF16))
                dsinks.append(-jnp.sum(psink * delta, axis=0, keepdims=True))
            ds = jnp.concatenate(dscores, axis=0)
            dk = _dot_tn(ds, q_stack)
            dv = _dot_tn(jnp.concatenate(probs, axis=0), do_stack)
            dk = dk + pltpu.roll(dk, HEAD_DIM, 1)
            dv = dv + pltpu.roll(dv, HEAD_DIM, 1)
            dkp_ref[t], dkc_ref[t] = dk[:BLK], dk[BLK:]
            dvp_ref[t], dvc_ref[t] = dv[:BLK], dv[BLK:]
            dq_ref[:, cols] = _unstack_heads(_dot(ds, kcat)).astype(BF16)
            r8 = lax.broadcasted_iota(jnp.int32, (8, SLAB), 0)
            upd = jnp.zeros((8, SLAB), F32)
            for j in range(GROUP_HEADS):
                upd = jnp.where(r8 == j, dsinks[j], upd)
            dsink_ref[t] += upd

    cur_out = pl.BlockSpec((heads, BLK, SLAB), lambda gs, n: (gs, n, 0))
    prev_out = pl.BlockSpec((heads, BLK, SLAB), lambda gs, n: (gs, (n + nb - 1) % nb, 0))
    kv_shape = jax.ShapeDtypeStruct((KV_HEADS, s, SLAB), F32)
    return pl.pallas_call(
        body, name="swa_bwd", grid=(KV_HEADS // heads, nb),
        in_specs=[sp["sinks"], sp["step"], sp["kc"], sp["kp"], sp["vc"], sp["vp"], *sp["gates"], sp["step"], sp["tri"]],
        out_specs=[sp["step"], cur_out, prev_out, cur_out, prev_out,
                   pl.BlockSpec((heads, 8, SLAB), lambda gs, n: (gs, 0, 0))],
        out_shape=[jax.ShapeDtypeStruct((s, ATTN_WIDTH), BF16), kv_shape, kv_shape, kv_shape, kv_shape,
                   jax.ShapeDtypeStruct((KV_HEADS, 8, SLAB), F32)],
        compiler_params=_params("parallel", "arbitrary"),
    )(sinks, qk, qk, qk, qk, qk, *[proj] * heads, dag, _lower_triangle())


def _swa_bwd_finish(proj, dqr, o, dag, dkc, dkp, dvc, dvp, qw, kw, cos, sins, ones_q, ones_kv):
    s = proj.shape[0]
    rows = min(KV_ROWS, s)
    n_q = KV_COL
    q_of = lambda j: jnp.clip(j - 1, 0, n_q - 1)
    gate_of = lambda j: jnp.clip(j - 1 - n_q, 0, n_q - 1)

    def body(kv_ref, p_ref, dqr_ref, o_ref, dag_ref, dkc_ref, dkp_ref, dvc_ref, dvp_ref, qw_ref, kw_ref, cos_ref, sin_ref,
             oq_ref, ok_ref, dp_ref, dqw_ref, dkw_ref):
        j, i = pl.program_id(0), pl.program_id(1)

        @pl.when(j == 0)
        def _():
            @pl.when(i == 0)
            def _():
                dkw_ref[...] = jnp.zeros_like(dkw_ref)

            def assemble(cur_ref, prv_ref):
                tot = [cur_ref[h] + prv_ref[h] for h in range(KV_HEADS)]
                first = _half_mask(tot[0].shape, 0)
                return jnp.concatenate([jnp.where(first, tot[0], tot[1]), jnp.where(first, tot[2], tot[3])], axis=1)

            dkr = assemble(dkc_ref, dkp_ref)
            dv = assemble(dvc_ref, dvp_ref)
            cosf, sinf = _tile_lanes(cos_ref[...], 2), _tile_lanes(sin_ref[...], 2)
            nrm, rstd, _ = _k_prep(kv_ref[:, :KV_WIDTH].astype(F32), kw_ref[...], cosf, sinf, ok_ref[...])
            dy = dkr * cosf + _partner64(dkr * sinf)
            dn = dy * kw_ref[...]
            dk = rstd * (dn - nrm * _group_mean(dn * nrm, ok_ref[...], passes=1))
            dp_ref[...] = jnp.concatenate([dk, dv], axis=1).astype(BF16)
            dkw_ref[...] += jnp.sum(dy * nrm, axis=0, keepdims=True)

        @pl.when(jnp.logical_and(j >= 1, j <= n_q))
        def _():
            @pl.when(i == 0)
            def _():
                dqw_ref[...] = jnp.zeros_like(dqw_ref)

            cosf, sinf = _tile_lanes(cos_ref[...], 4), _tile_lanes(sin_ref[...], 4)
            nrm, rstd, _ = _q_prep(p_ref[...].astype(F32), qw_ref[...], cosf, sinf, oq_ref[...])
            dq = dqr_ref[...].astype(F32) * ATTN_SCALE
            dy = dq * cosf + _partner64(dq * sinf)
            dn = dy * qw_ref[...]
            dp_ref[...] = (rstd * (dn - nrm * _group_mean(dn * nrm, oq_ref[...], passes=1))).astype(BF16)
            dqw_ref[0] += jnp.sum(dy * nrm, axis=0, keepdims=True)

        @pl.when(j > n_q)
        def _():
            gate = p_ref[...].astype(F32)
            sg = _sigmoid(gate)
            dp_ref[...] = (dag_ref[...].astype(F32) * o_ref[...].astype(F32) * (sg * (1.0 + gate * (1.0 - sg)))).astype(BF16)

    first_pass = lambda j, i: jnp.where(j == 0, i, 0)
    acc = pl.BlockSpec((KV_HEADS, rows, SLAB), lambda j, i: (0, first_pass(j, i), 0))
    full = lambda shape: pl.BlockSpec(shape, lambda j, i: (0,) * len(shape))
    tab = pl.BlockSpec((rows, SLAB), lambda j, i: (i, 0))
    out_col = lambda j: jnp.where(j == 0, KV_COL, jnp.where(j <= n_q, j - 1, j))
    return pl.pallas_call(
        body, name="swa_bwd_finish", grid=(2 * n_q + 1, s // rows),
        in_specs=[pl.BlockSpec((rows, GROUP_WIDTH), lambda j, i: (first_pass(j, i), KV_COL)),
                  pl.BlockSpec((rows, GROUP_WIDTH), lambda j, i: (jnp.where(j == 0, 0, i), jnp.where(j <= n_q, q_of(j), j))),
                  pl.BlockSpec((rows, GROUP_WIDTH), lambda j, i: (jnp.where(jnp.logical_and(j >= 1, j <= n_q), i, 0), q_of(j))),
                  pl.BlockSpec((rows, GROUP_WIDTH), lambda j, i: (jnp.where(j > n_q, i, 0), gate_of(j))),
                  pl.BlockSpec((rows, GROUP_WIDTH), lambda j, i: (jnp.where(j > n_q, i, 0), gate_of(j))),
                  acc, acc, acc, acc, full((1, GROUP_WIDTH)), full((1, KV_WIDTH)), tab, tab,
                  full((GROUP_WIDTH, GROUP_WIDTH)), full((KV_WIDTH, KV_WIDTH))],
        out_specs=[pl.BlockSpec((rows, GROUP_WIDTH), lambda j, i: (i, out_col(j))),
                   pl.BlockSpec((1, 8, GROUP_WIDTH), lambda j, i: (q_of(j), 0, 0)), pl.BlockSpec((8, KV_WIDTH), lambda j, i: (0, 0))],
        out_shape=[jax.ShapeDtypeStruct((s, ODD_IN), BF16), jax.ShapeDtypeStruct((n_q, 8, GROUP_WIDTH), F32),
                   jax.ShapeDtypeStruct((8, KV_WIDTH), F32)],
        compiler_params=_params("arbitrary", "arbitrary"),
    )(proj, proj, dqr, o, dag, dkc, dkp, dvc, dvp, qw, kw, cos, sins, ones_q, ones_kv)


def _place():
    x, y, c = lax.axis_index("x"), lax.axis_index("y"), lax.axis_index("c")
    return x, y, c


OTHER_CHIPS = ((1, 0), (0, 1), (1, 1))


def _half_rows(ref, half, rows):
    return ref.at[pl.ds(pl.multiple_of(half * (rows // 2), 8), rows // 2)]


DMA_CHUNK_BYTES = 1 << 20
BF16_TILE_ROWS = 16


def _n_chunks(ref):
    rows = ref.shape[-2]
    nbytes = math.prod(ref.shape) * jnp.dtype(ref.dtype).itemsize
    n = 1
    while 2 * n * DMA_CHUNK_BYTES <= nbytes and rows % (2 * n * BF16_TILE_ROWS) == 0:
        n *= 2
    return n


def _row_chunk(ref, k, n):
    rows = ref.shape[-2] // n
    return ref.at[pl.ds(k * rows, rows)] if len(ref.shape) == 2 else ref.at[:, pl.ds(k * rows, rows)]


def _push(src, dst, send_sem, recv_sem, device_id):
    n = _n_chunks(src)
    for k in range(n):
        pltpu.make_async_remote_copy(src_ref=_row_chunk(src, k, n), dst_ref=_row_chunk(dst, k, n), send_sem=send_sem,
                                     recv_sem=recv_sem, device_id=device_id, device_id_type=MESH).start()
    return pltpu.make_async_remote_copy(src_ref=src, dst_ref=dst, send_sem=send_sem, recv_sem=recv_sem,
                                        device_id=device_id, device_id_type=MESH)


def _copy(src, dst, sem):
    n = _n_chunks(src)
    for k in range(n):
        pltpu.make_async_copy(_row_chunk(src, k, n), _row_chunk(dst, k, n), sem).start()
    return pltpu.make_async_copy(src, dst, sem)


HBM = pl.BlockSpec(memory_space=pltpu.HBM)
SEM = pl.BlockSpec(memory_space=pltpu.SEMAPHORE)
SPLIT_COPY_EFFECT = pltpu.SideEffectType.DATAFLOW_SIDE_EFFECTING


def _in_hbm(a):
    return pltpu.with_memory_space_constraint(a, pltpu.HBM)


def _start_copies(name, arrays, plan, n_copies, after=None):
    n = len(arrays)

    def body(*refs):
        send_sem, recv_sem = refs[-n - 3], refs[-n - 2]
        for k, (src, dst, peer) in enumerate(plan(refs[:n])):
            _push(src, dst, send_sem.at[k], recv_sem.at[k], peer)
        refs[-1][...] = jnp.zeros_like(refs[-1])

    dma = pltpu.SemaphoreType.DMA((n_copies,))
    outs = pl.pallas_call(
        body, name=name,
        out_shape=(dma, dma, *[pltpu.HBM(a.shape, a.dtype) for a in arrays], jax.ShapeDtypeStruct((8, 128), F32)),
        in_specs=[HBM] * n + ([ANY] if after is not None else []),
        out_specs=(SEM, SEM, *[HBM] * n, pl.BlockSpec(memory_space=pltpu.VMEM)),
        input_output_aliases={i: i + 2 for i in range(n)},
        compiler_params=pltpu.CompilerParams(has_side_effects=SPLIT_COPY_EFFECT),
    )(*[_in_hbm(a) for a in arrays], *((after,) if after is not None else ()))
    return outs[0], outs[1], list(outs[2:2 + n]), outs[-1]


def _wait_copies(name, send_sem, recv_sem, arrays, plan, after):
    n = len(arrays)

    def body(*refs):
        send_ref, recv_ref = refs[n], refs[n + 1]
        for k, (src, dst, peer) in enumerate(plan(refs[:n])):
            cp = pltpu.make_async_remote_copy(src_ref=src, dst_ref=dst, send_sem=send_ref.at[k], recv_sem=recv_ref.at[k],
                                              device_id=peer, device_id_type=MESH)
            cp.wait_send()
            cp.wait_recv()

    return list(pl.pallas_call(
        body, name=name, out_shape=tuple(pltpu.HBM(a.shape, a.dtype) for a in arrays),
        in_specs=[HBM] * n + [SEM, SEM, ANY], out_specs=tuple([HBM] * n),
        input_output_aliases={i: i for i in range(n)},
        compiler_params=pltpu.CompilerParams(has_side_effects=SPLIT_COPY_EFFECT),
    )(*arrays, send_sem, recv_sem, after))


def _gather_region(full, kind, chip, half=None):
    if kind == "whole":
        return full.at[chip]
    if kind == "col":
        rows, width = full.shape[0], full.shape[1] // N_CHIPS
        piece = full.at[:, pl.ds(pl.multiple_of(chip * width, 128), width)]
    else:
        rows = full.shape[0] // N_CHIPS
        piece = full.at[pl.ds(pl.multiple_of(chip * rows, BF16_TILE_ROWS), rows)]
    return piece if half is None else _half_rows(piece, half, rows)


def _gather_plan(kinds):
    def plan(fulls):
        x, y, c = _place()
        copies = []
        for fx, fy in OTHER_CHIPS:
            for full, kind in zip(fulls, kinds):
                mine = _gather_region(full, kind, 2 * x + y, c)
                copies.append((mine, mine, (x ^ fx, y ^ fy, c)))
        return copies

    return plan


def _gather_start(name, fulls, kinds, after=None):
    return _start_copies(name, list(fulls), _gather_plan(kinds), 3 * len(kinds), after)


def _gather_finish(name, started, kinds, after):
    send_sem, recv_sem, fulls, _ = started
    n = len(kinds)
    fulls = _wait_copies(name + "_wait", send_sem, recv_sem, fulls, _gather_plan(kinds), after)
    split = [i for i, kind in enumerate(kinds) if kind != "whole"]

    def body(*refs):
        out = refs[n:2 * n]
        send, recv = refs[2 * n:]
        x, y, c = _place()
        pushes = []
        for r, (fx, fy) in enumerate(OTHER_CHIPS):
            chip = 2 * (x ^ fx) + (y ^ fy)
            for j, i in enumerate(split):
                landed = _gather_region(out[i], kinds[i], chip, c)
                pushes.append(_push(landed, landed, send.at[r * len(split) + j], recv.at[r * len(split) + j], (x, y, 1 - c)))
        for cp in pushes:
            cp.wait_recv()
        for cp in pushes:
            cp.wait_send()

    dma = pltpu.SemaphoreType.DMA
    return pl.pallas_call(
        body, name=name + "_share", in_specs=[ANY] * n, out_specs=[ANY] * n,
        out_shape=[jax.ShapeDtypeStruct(a.shape, a.dtype) for a in fulls],
        input_output_aliases={i: i for i in range(n)},
        scratch_shapes=[dma((3 * len(split),)), dma((3 * len(split),))],
        compiler_params=pltpu.CompilerParams(has_side_effects=True),
    )(*fulls)


def _allreduce_small(v):
    def body(v_ref, out_ref, buf, send_sems, recv_sems):
        x, y, c = _place()
        me = 4 * x + 2 * y + c
        buf[me] = v_ref[...]
        copies = []
        for r in range(1, N_DEV):
            peer = (x ^ (r >> 2), y ^ ((r >> 1) & 1), c ^ (r & 1))
            cp = pltpu.make_async_remote_copy(src_ref=v_ref, dst_ref=buf.at[me], send_sem=send_sems.at[r - 1],
                                              recv_sem=recv_sems.at[r - 1], device_id=peer, device_id_type=MESH)
            cp.start()
            copies.append(cp)
        for cp in copies:
            cp.wait_recv()
        for cp in copies:
            cp.wait_send()
        total = buf[0]
        for d in range(1, N_DEV):
            total = total + buf[d]
        out_ref[...] = total

    vm = pl.BlockSpec(memory_space=pltpu.VMEM)
    return pl.pallas_call(
        body, name="allreduce_small", in_specs=[vm], out_specs=vm, out_shape=jax.ShapeDtypeStruct(v.shape, v.dtype),
        scratch_shapes=[pltpu.VMEM((N_DEV,) + v.shape, v.dtype), pltpu.SemaphoreType.DMA((N_DEV - 1,)),
                        pltpu.SemaphoreType.DMA((N_DEV - 1,))],
        compiler_params=pltpu.CompilerParams(has_side_effects=True),
    )(v)


def _exchange_halves(grads, name):
    n = len(grads)

    def body(*refs):
        g, theirs = refs[:n], refs[n:2 * n]
        send_sem, recv_sem = refs[2 * n:]
        x, y, c = _place()
        copies = []
        for i in range(n):
            half = g[i].shape[1] // 2
            src = g[i].at[:, pl.ds(pl.multiple_of((1 - c) * half, BF16_TILE_ROWS), half)]
            copies.append(_push(src, theirs[i], send_sem.at[i], recv_sem.at[i], (x, y, 1 - c)))
        for cp in copies:
            cp.wait_recv()
            cp.wait_send()

    dma = pltpu.SemaphoreType.DMA
    return pl.pallas_call(
        body, name=name, in_specs=[ANY] * n, out_specs=[ANY] * n,
        out_shape=[jax.ShapeDtypeStruct((a.shape[0], a.shape[1] // 2, a.shape[2]), a.dtype) for a in grads],
        scratch_shapes=[dma((n,)), dma((n,))],
        compiler_params=pltpu.CompilerParams(has_side_effects=True),
    )(*grads)


def _pair_sum(g, theirs, core, name):
    pieces, half, cols = theirs.shape
    tr = min(half, 256)
    per = half // tr

    def body(core_ref, g_ref, t_ref, o_ref):
        del core_ref
        o_ref[...] = (g_ref[...].astype(F32) + t_ref[...].astype(F32)).astype(BF16)

    spec = pl.BlockSpec((1, tr, cols), lambda p, i, core_ref: (p, i, 0))
    return pl.pallas_call(
        body, name=name, out_shape=jax.ShapeDtypeStruct(theirs.shape, BF16),
        grid_spec=pltpu.PrefetchScalarGridSpec(
            num_scalar_prefetch=1, grid=(pieces, per),
            in_specs=[pl.BlockSpec((1, tr, cols), lambda p, i, core_ref: (p, core_ref[0] * per + i, 0)), spec],
            out_specs=spec),
        compiler_params=_params("parallel", "parallel"))(core, g, theirs)


def _scatter_plan(n):
    def plan(refs):
        parts, stacks = refs[:n], refs[n:]
        x, y, c = _place()
        copies = []
        for fx, fy in OTHER_CHIPS:
            chip = 2 * (x ^ fx) + (y ^ fy)
            for part, stack in zip(parts, stacks):
                if part.shape[0] == N_CHIPS:
                    piece = part.at[chip]
                else:
                    width = part.shape[2] // N_CHIPS
                    piece = part.at[0].at[:, pl.ds(pl.multiple_of(chip * width, 128), width)]
                copies.append((piece, stack.at[2 * x + y], (x ^ fx, y ^ fy, c)))
        return copies

    return plan


def _scatter_start(name, parts, after=None):
    def landing(a):
        return (N_CHIPS, a.shape[1], a.shape[2] if a.shape[0] == N_CHIPS else a.shape[2] // N_CHIPS)

    stacks = [lax.empty(landing(a), a.dtype) for a in parts]
    return _start_copies(name, list(parts) + stacks, _scatter_plan(len(parts)), 3 * len(parts), after)


def _scatter_finish(name, started, after):
    send_sem, recv_sem, arrays, _ = started
    n = len(arrays) // 2
    arrays = _wait_copies(name + "_wait", send_sem, recv_sem, arrays, _scatter_plan(n), after)
    return arrays[:n], arrays[n:]


def _sum_chips(part, stack, place, name):
    _, r, c = stack.shape
    tr = 256
    per = r // tr

    def body(place_ref, own_ref, a_ref, b_ref, c_ref, o_ref):
        del place_ref
        total = own_ref[0].astype(F32)
        for ref in (a_ref, b_ref, c_ref):
            total = total + ref[0].astype(F32)
        o_ref[...] = total

    if part.shape[0] == N_CHIPS:
        own = pl.BlockSpec((1, tr, c), lambda i, pr: (pr[1], i, 0))
    else:
        own = pl.BlockSpec((1, tr, c), lambda i, pr: (0, i, pr[1]))
    other = lambda flip: pl.BlockSpec((1, tr, c), lambda i, pr: (pr[1] ^ flip, i, 0))
    return pl.pallas_call(
        body, name=name, out_shape=jax.ShapeDtypeStruct((2 * r, c), F32),
        grid_spec=pltpu.PrefetchScalarGridSpec(
            num_scalar_prefetch=1, grid=(per,), in_specs=[own, other(2), other(1), other(3)],
            out_specs=pl.BlockSpec((tr, c), lambda i, pr: (pr[0] * per + i, 0))),
        compiler_params=_params("parallel"))(place, part, stack, stack, stack)


def _share_halves(fulls):
    n = len(fulls)

    def body(*refs):
        dst = refs[n:2 * n]
        send_sem, recv_sem = refs[2 * n:]
        x, y, c = _place()
        copies = []
        for i in range(n):
            mine = _half_rows(dst[i], c, dst[i].shape[0])
            copies.append(_push(mine, mine, send_sem.at[i], recv_sem.at[i], (x, y, 1 - c)))
        for cp in copies:
            cp.wait_recv()
            cp.wait_send()

    dma = pltpu.SemaphoreType.DMA
    return pl.pallas_call(
        body, name="share_halves", in_specs=[ANY] * n, out_specs=[ANY] * n,
        out_shape=[jax.ShapeDtypeStruct(a.shape, a.dtype) for a in fulls],
        input_output_aliases={i: i for i in range(n)}, scratch_shapes=[dma((n,)), dma((n,))],
        compiler_params=pltpu.CompilerParams(has_side_effects=True),
    )(*fulls)


MM = dict(tm=2048, tn=1024, tk=2048)
MM_LONG_K = dict(tm=1024, tn=1024, tk=4096)


def _local_step(x, target, ev_norm_w, q_norm_w, k_norm_w, sinks, own_first, weights_first, weights_late, emit):
    s = x.shape[0]
    cos_r, sin_r = _rope_tables(s, RET_DIM)
    cos_a, sin_a = _rope_tables(s, HEAD_DIM)
    cos_a = np.tile(cos_a, (1, 4))
    sins_a = np.tile(np.concatenate([-sin_a, sin_a], axis=1), (1, 2))
    tables = _retention_tables()
    ones_q, ones_kv = _block_diag_mean(GROUP_WIDTH), _block_diag_mean(KV_WIDTH)
    qw_g = jnp.tile(q_norm_w, (1, GROUP_WIDTH // HEAD_DIM))
    kw_kv = jnp.tile(k_norm_w, (1, KV_WIDTH // HEAD_DIM))
    sinks1 = sinks.reshape(Q_HEADS)

    own_w_in0, own_block, start_token = own_first
    h0 = _rmsnorm(x, ev_norm_w, "norm0", after=start_token)
    shifted = dict(shift=own_block * (own_w_in0.shape[1] // MM["tn"]), total=EVEN_IN // MM["tn"], tm=MM["tm"], tn=MM["tn"],
                   out_dtype=BF16)
    own_blocks = own_w_in0.shape[1] // MM["tn"]
    proj0 = _mm_shifted(h0, own_w_in0, b_shifted=False, first=0, count=own_blocks, name="proj0_own", **shifted)
    w_in0, conv_w, od_norm_w, token = weights_first(proj0)
    proj0 = _mm_shifted(h0, w_in0, b_shifted=True, first=own_blocks, count=shifted["total"] - own_blocks, name="proj0_rest",
                        into=proj0, after=token, **shifted)
    o_ret, cat, states = _retention_fwd(proj0, cos_r, sin_r, tables)
    cat = _conv_fwd(proj0, conv_w, cat)
    w_out0, w_in1, w_out1 = weights_late(cat)

    def residual_and_norm(prod, x_ref, w_ref, x1_ref, h1_ref):
        x1v = x_ref[...] + prod
        x1_ref[...] = x1v
        rstd = lax.rsqrt(jnp.mean(x1v * x1v, axis=-1, keepdims=True) + EPS)
        h1_ref[...] = (x1v * rstd * w_ref[...]).astype(BF16)

    def residual_and_loss(prod, x1_ref, t_ref, dyb_ref, sq_ref):
        diff = (x1_ref[...] + prod) - t_ref[...]
        dyb_ref[...] = (diff * (1.0 / D_MODEL)).astype(BF16)

        @pl.when(pl.program_id(0) == 0)
        def _():
            sq_ref[...] = jnp.zeros_like(sq_ref)

        sq_ref[...] += jnp.sum(jnp.sum(diff * diff, axis=1, keepdims=True), axis=0, keepdims=True)

    act = lambda dt: jax.ShapeDtypeStruct((s, D_MODEL), dt)
    x1, h1 = _mm_rows(cat, w_out0, [x], [od_norm_w], [act(F32), act(BF16)], residual_and_norm, tm=min(s, 512), name="out0")
    proj1 = _mm(h1, w_in1, mode="nn", out_dtype=BF16, name="proj1", tm=2048, tn=1536, tk=2048)
    qk = _qk_prep(proj1, qw_g, kw_kv, cos_a, sins_a, ones_q, ones_kv)
    ag, o_att = _swa_fwd(qk, proj1, sinks1)
    dy_b, sq = _mm_rows(ag, w_out1, [x1, target], [], [act(BF16), jax.ShapeDtypeStruct((8, 128), F32)],
                        residual_and_loss, tm=min(s, 512), name="out1")

    g_w_out1 = _mm(ag, dy_b, mode="tn", out_dtype=BF16, name="g_w_out1", **MM_LONG_K)
    dag = _mm(dy_b, w_out1, mode="nt", out_dtype=BF16, name="d_ag", **MM)
    dqr, dkc, dkp, dvc, dvp, dsink = _swa_bwd(qk, proj1, dag, sinks1)
    dproj1, dqw, dkw = _swa_bwd_finish(proj1, dqr, o_att, dag, dkc, dkp, dvc, dvp, qw_g, kw_kv, cos_a, sins_a, ones_q, ones_kv)
    g_w_in1 = _mm(h1, dproj1, mode="tn", out_dtype=BF16, name="g_w_in1", tm=1024, tn=768, tk=4096)
    token = emit("layer1", (("od_w_in", g_w_in1, "col"), ("od_w_out", g_w_out1, "row")))
    dh1 = _mm(dproj1, w_in1, mode="nt", out_dtype=BF16, name="d_h1", tm=1024, tn=1024, tk=ODD_IN, after=token)
    dx1_b, g_norm1 = _rmsnorm_bwd(x1, od_norm_w, dh1, dy_b, "norm1_bwd", BF16)

    g_w_out0 = _mm(cat, dx1_b, mode="tn", out_dtype=BF16, name="g_w_out0", **MM_LONG_K)
    token = emit("out0", (("ev_w_out", g_w_out0, "row"),))
    dcat = _mm(dx1_b, w_out0, mode="nt", out_dtype=BF16, name="d_cat", after=token, **MM)
    dproj0 = _retention_bwd(proj0, o_ret, states, dcat, cos_r, sin_r, tables)
    dproj0, g_conv = _conv_bwd(proj0, dcat, conv_w, dproj0)
    g_w_in0 = _mm(h0, dproj0, mode="tn", out_dtype=BF16, name="g_w_in0", **MM_LONG_K)
    token = emit("in0", (("ev_w_in", g_w_in0, "col"),))
    dh0 = _mm(dproj0, w_in0, mode="nt", out_dtype=BF16, name="d_h0", after=token, **MM_LONG_K)
    grad_x, g_norm0 = _rmsnorm_bwd(x, ev_norm_w, dh0, dx1_b, "norm0_bwd", F32)

    g_qw = dqw[:, 0, :].reshape(Q_HEADS, HEAD_DIM).sum(axis=0)
    g_kw = dkw[0].reshape(KV_HEADS, HEAD_DIM).sum(axis=0)
    g_sinks = dsink[:, :, 0].reshape(Q_HEADS)
    small = dict(ev_norm=g_norm0[0], od_norm=g_norm1[0], conv=g_conv[:3], qw=g_qw, kw=g_kw, sinks=g_sinks)
    return sq[0, 0], grad_x, small


def _pack_small_grads(small, sq):
    pad = lambda v: jnp.pad(v, (0, D_MODEL - v.shape[0]))
    tail = pad(jnp.concatenate([small["qw"], small["kw"], small["sinks"]]))
    rows = [small["ev_norm"], small["od_norm"]] + [pad(small["conv"][t]) for t in range(3)] + [tail, pad(sq.reshape(1))]
    rows += [jnp.zeros((D_MODEL,), F32)] * (8 - len(rows))
    return jnp.stack(rows)


class _ReduceScatter:
    def __init__(self, place):
        self.place = place
        self.started = []

    def send(self, tag, grads):
        pieces = [g[None] if kind == "col" else g.reshape(N_CHIPS, g.shape[0] // N_CHIPS, g.shape[1]) for _, g, kind in grads]
        theirs = _exchange_halves(pieces, "exchange_halves_" + tag)
        parts = [_pair_sum(g, t, self.place[:1], "pair_sum_" + nm) for g, t, (nm, _, _) in zip(pieces, theirs, grads)]
        started = _scatter_start("scatter_" + tag, parts)
        self.started.append((tag, [nm for nm, _, _ in grads], started))
        return started[3]

    def finish(self, after):
        names, sums = [], []
        for tag, group, started in self.started:
            parts, stacks = _scatter_finish("scatter_" + tag, started, after)
            sums += [_sum_chips(p, s, self.place, "chip_sum_" + nm) for p, s, nm in zip(parts, stacks, group)]
            names += group
        return dict(zip(names, _share_halves(sums)))


def kernel(x, ev_norm_w, ev_w_in, ev_conv_w, ev_w_out, od_norm_w, od_w_in, od_q_norm_w, od_k_norm_w, od_sinks, od_w_out, loss_target, m_ev_norm_w, m_ev_w_in, m_ev_conv_w, m_ev_w_out, m_od_norm_w, m_od_w_in, m_od_q_norm_w, m_od_k_norm_w, m_od_sinks, m_od_w_out, v_ev_norm_w, v_ev_w_in, v_ev_conv_w, v_ev_w_out, v_od_norm_w, v_od_w_in, v_od_q_norm_w, v_od_k_norm_w, v_od_sinks, v_od_w_out):
    my_chip = 2 * lax.axis_index("x") + lax.axis_index("y")
    place = jnp.stack([lax.axis_index("c"), my_chip]).astype(jnp.int32)
    shard_w = D_MODEL // N_CHIPS
    conv_shard = CONV_WIDTH // N_CHIPS

    small_in = jnp.zeros((8, shard_w), F32)
    small_in = small_in.at[0].set(od_norm_w[0]).at[1:4, :conv_shard].set(ev_conv_w[0])
    small_in = lax.dynamic_update_slice(jnp.zeros((N_CHIPS, 8, shard_w), F32), small_in[None], (my_chip, 0, 0))
    chip = place[1:]
    first_kinds, late_kinds = ("col", "whole"), ("row", "col", "row")
    w_in0_own_place, w_in0_shard = _cast_into_gathered(ev_w_in[0], "col", chip, "cast_w_in0", keep_shard=True)
    first = _gather_start("gather_first", [w_in0_own_place, small_in], first_kinds)
    late_own = [_cast_into_gathered(ev_w_out[0], "row", chip, "cast_w_out0"),
                _cast_into_gathered(od_w_in[0], "col", chip, "cast_w_in1"),
                _cast_into_gathered(od_w_out[0], "row", chip, "cast_w_out1")]
    late = []

    def weights_first(after):
        w_in0, small_all = _gather_finish("gather_first", first, first_kinds, after)
        late.append(_gather_start("gather_late", late_own, late_kinds, after=w_in0))
        od_norm_full = small_all[:, 0, :].reshape(1, D_MODEL)
        conv_full = jnp.transpose(small_all[:, 1:4, :conv_shard], (1, 0, 2)).reshape(3, CONV_WIDTH)
        return w_in0, conv_full, od_norm_full, late[0][3]

    def weights_late(after):
        return _gather_finish("gather_late", late[0], late_kinds, after)

    reduce_scatter = _ReduceScatter(place)
    sq, grad_x, small = _local_step(x[0], loss_target[0], ev_norm_w, od_q_norm_w, od_k_norm_w, od_sinks,
                                    (w_in0_shard, chip, first[3]), weights_first, weights_late, reduce_scatter.send)

    big = reduce_scatter.finish(grad_x)
    g_ev_w_in, g_ev_w_out, g_od_w_in, g_od_w_out = big["ev_w_in"], big["ev_w_out"], big["od_w_in"], big["od_w_out"]
    tot = _allreduce_small(_pack_small_grads(small, sq))
    loss = 0.5 * tot[6, 0] / D_MODEL
    g_ev_norm = tot[0:1]
    g_od_norm = lax.dynamic_slice(tot, (1, my_chip * shard_w), (1, shard_w))
    g_conv = lax.dynamic_slice(tot, (2, my_chip * conv_shard), (3, conv_shard))
    g_qw, g_kw, g_sinks = tot[5:6, 0:HEAD_DIM], tot[5:6, HEAD_DIM:2 * HEAD_DIM], tot[5:6, 2 * HEAD_DIM:2 * HEAD_DIM + Q_HEADS]

    upd = {}
    upd["ev_w_in"] = _adamw(ev_w_in[0], g_ev_w_in, m_ev_w_in[0], v_ev_w_in[0], "adamw_ev_w_in")
    upd["ev_w_out"] = _adamw(ev_w_out[0], g_ev_w_out, m_ev_w_out[0], v_ev_w_out[0], "adamw_ev_w_out")
    upd["od_w_in"] = _adamw(od_w_in[0], g_od_w_in, m_od_w_in[0], v_od_w_in[0], "adamw_od_w_in")
    upd["od_w_out"] = _adamw(od_w_out[0], g_od_w_out, m_od_w_out[0], v_od_w_out[0], "adamw_od_w_out")
    smalls = (("ev_norm_w", ev_norm_w, g_ev_norm, m_ev_norm_w, v_ev_norm_w),
              ("ev_conv_w", ev_conv_w, g_conv, m_ev_conv_w, v_ev_conv_w),
              ("od_norm_w", od_norm_w, g_od_norm, m_od_norm_w, v_od_norm_w),
              ("od_q_norm_w", od_q_norm_w, g_qw, m_od_q_norm_w, v_od_q_norm_w),
              ("od_k_norm_w", od_k_norm_w, g_kw, m_od_k_norm_w, v_od_k_norm_w),
              ("od_sinks", od_sinks, g_sinks, m_od_sinks, v_od_sinks))
    sizes = [w.size for _, w, _, _, _ in smalls]
    padded = 8 * 128 * math.ceil(sum(sizes) / (8 * 128))
    pack = lambda arrs, fill: jnp.concatenate(
        [a.reshape(-1) for a in arrs] + [jnp.full((padded - sum(sizes),), fill, F32)]).reshape(8, padded // 8)
    packed = _adamw(pack([w for _, w, _, _, _ in smalls], 0.0), pack([g for _, _, g, _, _ in smalls], 0.0),
                    pack([m for _, _, _, m, _ in smalls], 0.0), pack([v for _, _, _, _, v in smalls], 1.0), "adamw_small")
    offs = [sum(sizes[:i]) for i in range(len(sizes))]
    for (nm, w, _, _, _), off, size in zip(smalls, offs, sizes):
        upd[nm] = tuple(p.reshape(-1)[off:off + size].reshape(w.shape) for p in packed)
    for nm in ("ev_w_in", "ev_w_out", "od_w_in", "od_w_out"):
        upd[nm] = tuple(u[None] for u in upd[nm])
    order = ("ev_norm_w", "ev_w_in", "ev_conv_w", "ev_w_out", "od_norm_w", "od_w_in", "od_q_norm_w", "od_k_norm_w", "od_sinks", "od_w_out")
    return (loss, grad_x[None], *[upd[nm][0] for nm in order], *[upd[nm][1] for nm in order],
            *[upd[nm][2] for nm in order], *[upd[nm][3] for nm in order])
```

```python
import functools
import math

import jax
import jax.numpy as jnp
import numpy as np
from jax import lax
from jax.experimental import pallas as pl
from jax.experimental.pallas import tpu as pltpu

F32 = jnp.float32
BF16 = jnp.bfloat16

D_MODEL = 2048
RET_HEADS = 4
RET_DIM = 256
RET_WIDTH = 1024
CONV_WIDTH = 1024
EVEN_IN = 8192
Q_HEADS = 32
HEAD_DIM = 64
KV_HEADS = 4
KV_WIDTH = 256
ATTN_WIDTH = 2048
ODD_IN = 4608
BLK = 128
ROPE_THETA = 10000.0
EPS = 1e-6
ADAM_LR = 0.001
ADAM_B1 = 0.9
ADAM_B2 = 0.999
ADAM_EPS = 1e-08
ADAM_WD = 0.01
ADAM_STEP = 10
N_CHIPS = 4
N_DEV = 8
VMEM_LIMIT_BYTES = 56 * 1024 * 1024
MESH = pl.DeviceIdType.MESH
ANY = pl.BlockSpec(memory_space=pl.ANY)


def _params(*sem):
    return pltpu.CompilerParams(dimension_semantics=sem, vmem_limit_bytes=VMEM_LIMIT_BYTES)


def _dot(a, b):
    return jnp.dot(a, b, preferred_element_type=F32)


def _dot_nt(a, b):
    return lax.dot_general(a, b, (((1,), (1,)), ((), ())), preferred_element_type=F32)


def _dot_tn(a, b):
    return lax.dot_general(a, b, (((0,), (0,)), ((), ())), preferred_element_type=F32)


def _sigmoid(x):
    return 1.0 / (1.0 + jnp.exp(-x))


def _mm(a, b, *, mode, tm, tn, tk, out_dtype, name, add=None, after=None):
    if mode == "nn":
        (m, k), n = a.shape, b.shape[1]
    elif mode == "nt":
        (m, k), n = a.shape, b.shape[0]
    else:
        (k, m), n = a.shape, b.shape[1]
    tm, tn, tk = min(tm, m), min(tn, n), min(tk, k)
    assert m % tm == 0 and n % tn == 0 and k % tk == 0, (name, m, n, k)
    nk = k // tk
    dot = {"nn": _dot, "nt": _dot_nt, "tn": _dot_tn}[mode]
    a_spec = (pl.BlockSpec((tk, tm), lambda i, j, kk: (kk, i)) if mode == "tn"
              else pl.BlockSpec((tm, tk), lambda i, j, kk: (i, kk)))
    b_spec = (pl.BlockSpec((tn, tk), lambda i, j, kk: (j, kk)) if mode == "nt"
              else pl.BlockSpec((tk, tn), lambda i, j, kk: (kk, j)))
    o_spec = pl.BlockSpec((tm, tn), lambda i, j, kk: (i, j))
    has_add = add is not None

    def body(*refs):
        a_ref, b_ref = refs[0], refs[1]
        add_ref = refs[2] if has_add else None
        o_ref, acc_ref = refs[-2], refs[-1]
        p = dot(a_ref[...], b_ref[...])

        def finish(total):
            if has_add:
                total = total + add_ref[...].astype(F32)
            o_ref[...] = total.astype(out_dtype)

        if nk == 1:
            finish(p)
        else:
            kk = pl.program_id(2)

            @pl.when(kk == 0)
            def _():
                acc_ref[...] = p

            @pl.when(jnp.logical_and(kk > 0, kk < nk - 1))
            def _():
                acc_ref[...] += p

            @pl.when(kk == nk - 1)
            def _():
                finish(acc_ref[...] + p)

    in_specs = [a_spec, b_spec] + ([o_spec] if has_add else []) + ([ANY] if after is not None else [])
    args = (a, b) + ((add,) if has_add else ()) + ((after,) if after is not None else ())
    return pl.pallas_call(
        body, name=name, grid=(m // tm, n // tn, nk), in_specs=in_specs, out_specs=o_spec,
        out_shape=jax.ShapeDtypeStruct((m, n), out_dtype),
        scratch_shapes=[pltpu.VMEM((tm, tn) if nk > 1 else (8, 128), F32)],
        compiler_params=_params("parallel", "parallel", "arbitrary"),
    )(*args)


def _mm_shifted(a, b, shift, *, b_shifted, first, count, total, tm, tn, out_dtype, name, into=None, after=None):
    m, k = a.shape
    tm = min(tm, m)
    assert m % tm == 0
    col = lambda j, shift_ref: (shift_ref[0] + first + j) % total
    extra = [arr for arr in (into, after) if arr is not None]

    def body(shift_ref, a_ref, b_ref, *rest):
        del shift_ref
        rest[-1][...] = _dot(a_ref[...], b_ref[...]).astype(out_dtype)

    return pl.pallas_call(
        body, name=name, out_shape=jax.ShapeDtypeStruct((m, total * tn), out_dtype),
        grid_spec=pltpu.PrefetchScalarGridSpec(
            num_scalar_prefetch=1, grid=(m // tm, count),
            in_specs=[pl.BlockSpec((tm, k), lambda i, j, s: (i, 0)),
                      pl.BlockSpec((k, tn), (lambda i, j, s: (0, col(j, s))) if b_shifted else (lambda i, j, s: (0, j)))]
            + [ANY] * len(extra),
            out_specs=pl.BlockSpec((tm, tn), lambda i, j, s: (i, col(j, s)))),
        input_output_aliases={3: 0} if into is not None else {},
        compiler_params=_params("parallel", "arbitrary"))(shift, a, b, *extra)


def _mm_rows(a, b, rows_in, vecs_in, out_shapes, epilogue, *, tm, name):
    m, k = a.shape
    n = b.shape[1]
    assert m % tm == 0
    row = pl.BlockSpec((tm, n), lambda i: (i, 0))

    def body(a_ref, b_ref, *rest):
        epilogue(_dot(a_ref[...], b_ref[...]), *rest)

    out_specs = [row if tuple(s.shape) == (m, n) else pl.BlockSpec(s.shape, lambda i: (0, 0)) for s in out_shapes]
    return pl.pallas_call(
        body, name=name, grid=(m // tm,),
        in_specs=[pl.BlockSpec((tm, k), lambda i: (i, 0)), pl.BlockSpec((k, n), lambda i: (0, 0))] + [row] * len(rows_in)
        + [pl.BlockSpec((1, n), lambda i: (0, 0))] * len(vecs_in),
        out_specs=out_specs, out_shape=out_shapes, compiler_params=_params("arbitrary"),
    )(a, b, *rows_in, *vecs_in)


def _cast_into_gathered(w, kind, chip, name, keep_shard=False):
    r, c = w.shape
    tr = min(r, 512)
    per = r // tr

    def body(chip_ref, w_ref, *outs):
        del chip_ref
        for o_ref in outs:
            o_ref[...] = w_ref[...].astype(BF16)

    if kind == "col":
        shape, out_map = (r, N_CHIPS * c), (lambda i, chip_ref: (i, chip_ref[0]))
    else:
        shape, out_map = (N_CHIPS * r, c), (lambda i, chip_ref: (chip_ref[0] * per + i, 0))
    plain = pl.BlockSpec((tr, c), lambda i, chip_ref: (i, 0))
    out = pl.pallas_call(
        body, name=name,
        out_shape=[jax.ShapeDtypeStruct(shape, BF16)] + ([jax.ShapeDtypeStruct((r, c), BF16)] if keep_shard else []),
        grid_spec=pltpu.PrefetchScalarGridSpec(
            num_scalar_prefetch=1, grid=(per,), in_specs=[plain],
            out_specs=[pl.BlockSpec((tr, c), out_map)] + ([plain] if keep_shard else [])),
        compiler_params=_params("parallel"))(chip, w)
    return out if keep_shard else out[0]


NORM_ROWS = 512


def _rmsnorm(x, w, name, after=None):
    s, d = x.shape
    tr = NORM_ROWS

    def body(x_ref, w_ref, *rest):
        xv = x_ref[...]
        rstd = lax.rsqrt(jnp.mean(xv * xv, axis=-1, keepdims=True) + EPS)
        rest[-1][...] = (xv * rstd * w_ref[...]).astype(BF16)

    return pl.pallas_call(
        body, name=name, grid=(s // tr,),
        in_specs=[pl.BlockSpec((tr, d), lambda i: (i, 0)), pl.BlockSpec((1, d), lambda i: (0, 0))]
        + ([ANY] if after is not None else []),
        out_specs=pl.BlockSpec((tr, d), lambda i: (i, 0)),
        out_shape=jax.ShapeDtypeStruct((s, d), BF16), compiler_params=_params("parallel"),
    )(x, w, *((after,) if after is not None else ()))


def _rmsnorm_bwd(x, w, dh, dres, name, out_dtype):
    s, d = x.shape
    tr = NORM_ROWS

    def body(x_ref, w_ref, dh_ref, dres_ref, dx_ref, dw_ref):
        xv = x_ref[...]
        rstd = lax.rsqrt(jnp.mean(xv * xv, axis=-1, keepdims=True) + EPS)
        nrm = xv * rstd
        dhv = dh_ref[...].astype(F32)
        dn = dhv * w_ref[...]
        dx = dres_ref[...].astype(F32) + rstd * (dn - nrm * jnp.mean(dn * nrm, axis=-1, keepdims=True))
        dx_ref[...] = dx.astype(out_dtype)

        @pl.when(pl.program_id(0) == 0)
        def _():
            dw_ref[...] = jnp.zeros_like(dw_ref)

        dw_ref[...] += jnp.sum(dhv * nrm, axis=0, keepdims=True)

    row = pl.BlockSpec((tr, d), lambda i: (i, 0))
    return pl.pallas_call(
        body, name=name, grid=(s // tr,),
        in_specs=[row, pl.BlockSpec((1, d), lambda i: (0, 0)), row, row],
        out_specs=[row, pl.BlockSpec((8, d), lambda i: (0, 0))],
        out_shape=[jax.ShapeDtypeStruct((s, d), out_dtype), jax.ShapeDtypeStruct((8, d), F32)],
        compiler_params=_params("arbitrary"))(x, w, dh, dres)


def _adamw(w, g, m, v, name):
    r, c = w.shape
    tr = min(r, 256)
    assert r % tr == 0

    def body(w_ref, g_ref, m_ref, v_ref, g_out, d_ref, nm_ref, nv_ref):
        gv = g_ref[...]
        g_out[...] = gv
        nm = ADAM_B1 * m_ref[...] + (1.0 - ADAM_B1) * gv
        nv = ADAM_B2 * v_ref[...] + (1.0 - ADAM_B2) * (gv * gv)
        m_hat = nm / (1.0 - ADAM_B1 ** ADAM_STEP)
        v_hat = nv / (1.0 - ADAM_B2 ** ADAM_STEP)
        d_ref[...] = -ADAM_LR * (m_hat / (jnp.sqrt(v_hat) + ADAM_EPS) + ADAM_WD * w_ref[...])
        nm_ref[...] = nm
        nv_ref[...] = nv

    spec = pl.BlockSpec((tr, c), lambda i: (i, 0))
    shp = jax.ShapeDtypeStruct((r, c), F32)
    return pl.pallas_call(body, name=name, grid=(r // tr,), in_specs=[spec] * 4, out_specs=[spec] * 4,
                          out_shape=[shp] * 4, compiler_params=_params("parallel"))(w, g, m, v)


def _rope_tables(s, dim):
    inv = (1.0 / (ROPE_THETA ** (np.arange(0, dim, 2, dtype=np.float64) / dim))).astype(np.float32)
    ang = (np.arange(s, dtype=np.float32)[:, None] * inv[None, :]).astype(np.float64)
    return np.cos(ang).astype(np.float32), np.sin(ang).astype(np.float32)


def _rope_half(x, cos, sin):
    h = x.shape[1] // 2
    x1, x2 = x[:, :h], x[:, h:]
    return jnp.concatenate([x1 * cos - x2 * sin, x2 * cos + x1 * sin], axis=1)


def _unrope_half(dy, cos, sin):
    h = dy.shape[1] // 2
    d1, d2 = dy[:, :h], dy[:, h:]
    return jnp.concatenate([d1 * cos + d2 * sin, d2 * cos - d1 * sin], axis=1)


def _lane(shape):
    return lax.broadcasted_iota(jnp.int32, shape, 1)


def _partner64(x):
    w = x.shape[1]
    first = (_lane(x.shape) % HEAD_DIM) < (HEAD_DIM // 2)
    return jnp.where(first, pltpu.roll(x, w - HEAD_DIM // 2, 1), pltpu.roll(x, HEAD_DIM // 2, 1))


def _tile_lanes(t, reps):
    return t if reps == 1 else jnp.concatenate([t] * reps, axis=1)


def _group_mean(x, ones_bd, passes=2):
    hi = x.astype(BF16)
    if passes == 1:
        return _dot(hi, ones_bd)
    lo = (x - hi.astype(F32)).astype(BF16)
    return _dot(hi, ones_bd) + _dot(lo, ones_bd)


def _block_diag_mean(width):
    idx = jnp.arange(width) // HEAD_DIM
    return jnp.where(idx[:, None] == idx[None, :], 1.0 / HEAD_DIM, 0.0).astype(BF16)


RET_STEP = 4


def _retention_tables():
    h = RET_HEADS
    log_g = jnp.log(1.0 - 2.0 ** (-5.0 - jnp.arange(h, dtype=F32)))
    idx = jnp.arange(BLK, dtype=F32)
    diff = idx[:, None] - idx[None, :]
    intra = jnp.where(diff >= 0, jnp.exp(log_g[:, None, None] * jnp.maximum(diff, 0.0)), 0.0).astype(F32)
    q_dec = jnp.exp(log_g[:, None] * (idx[None, :] + 1.0)).astype(F32)[:, :, None]
    k_dec = jnp.exp(log_g[:, None] * (BLK - 1.0 - idx[None, :])).astype(F32)[:, :, None]
    chunk_dec = jnp.exp(log_g * BLK).astype(F32)[:, None, None]
    return intra, q_dec, k_dec, chunk_dec


def _retention_fwd(proj, cos, sin, tables):
    s = proj.shape[0]
    nc = s // BLK
    intra, q_dec, k_dec, chunk_dec = tables

    def body(p_ref, cos_ref, sin_ref, in_ref, qd_ref, kd_ref, cd_ref, o_ref, cat_ref, st_ref, state):
        @pl.when(pl.program_id(0) == 0)
        def _():
            state[...] = jnp.zeros_like(state)

        for c in range(RET_STEP):
            rows = slice(c * BLK, (c + 1) * BLK)
            cosv, sinv = cos_ref[rows, :], sin_ref[rows, :]
            for h in range(RET_HEADS):
                c0 = h * RET_DIM
                q = p_ref[rows, c0:c0 + RET_DIM].astype(F32)
                k = p_ref[rows, RET_WIDTH + c0:RET_WIDTH + c0 + RET_DIM].astype(F32)
                v = p_ref[rows, 2 * RET_WIDTH + c0:2 * RET_WIDTH + c0 + RET_DIM]
                g = p_ref[rows, 3 * RET_WIDTH + c0:3 * RET_WIDTH + c0 + RET_DIM].astype(F32)
                qb = _rope_half(q, cosv, sinv).astype(BF16)
                kr = _rope_half(k, cosv, sinv) * (RET_DIM ** -0.5)
                kb = kr.astype(BF16)
                scores = _dot_nt(qb, kb) * in_ref[h]
                inner = _dot(scores.astype(BF16), v)
                prev = state[h]
                prev_b = prev.astype(BF16)
                st_ref[h, c] = prev_b
                o = inner + _dot(qb, prev_b) * qd_ref[h]
                o_ref[rows, c0:c0 + RET_DIM] = o
                rstd = lax.rsqrt(jnp.mean(o * o, axis=-1, keepdims=True) + EPS)
                cat_ref[rows, c0:c0 + RET_DIM] = (o * rstd * (g * _sigmoid(g))).astype(BF16)
                state[h] = cd_ref[h] * prev + _dot_tn((kr * kd_ref[h]).astype(BF16), v)

    full = lambda shape: pl.BlockSpec(shape, lambda n: (0,) * len(shape))
    step = RET_STEP * BLK
    return pl.pallas_call(
        body, name="retention_fwd", grid=(nc // RET_STEP,),
        in_specs=[pl.BlockSpec((step, 4 * RET_WIDTH), lambda n: (n, 0)),
                  pl.BlockSpec((step, RET_DIM // 2), lambda n: (n, 0)), pl.BlockSpec((step, RET_DIM // 2), lambda n: (n, 0)),
                  full((RET_HEADS, BLK, BLK)), full((RET_HEADS, BLK, 1)), full((RET_HEADS, BLK, 1)), full((RET_HEADS, 1, 1))],
        out_specs=[pl.BlockSpec((step, RET_WIDTH), lambda n: (n, 0)), pl.BlockSpec((step, RET_WIDTH), lambda n: (n, 0)),
                   pl.BlockSpec((RET_HEADS, RET_STEP, RET_DIM, RET_DIM), lambda n: (0, n, 0, 0))],
        out_shape=[jax.ShapeDtypeStruct((s, RET_WIDTH), F32), jax.ShapeDtypeStruct((s, D_MODEL), BF16),
                   jax.ShapeDtypeStruct((RET_HEADS, nc, RET_DIM, RET_DIM), BF16)],
        scratch_shapes=[pltpu.VMEM((RET_HEADS, RET_DIM, RET_DIM), F32)],
        compiler_params=_params("arbitrary"))(proj, cos, sin, intra, q_dec, k_dec, chunk_dec)


def _retention_bwd(proj, o, states, dcat, cos, sin, tables):
    s = proj.shape[0]
    nc = s // BLK
    intra, q_dec, k_dec, chunk_dec = tables

    def body(p_ref, o_ref, st_ref, dc_ref, cos_ref, sin_ref, in_ref, qd_ref, kd_ref, cd_ref, dp_ref, dstate):
        @pl.when(pl.program_id(0) == 0)
        def _():
            dstate[...] = jnp.zeros_like(dstate)

        for c in reversed(range(RET_STEP)):
            rows = slice(c * BLK, (c + 1) * BLK)
            cosv, sinv = cos_ref[rows, :], sin_ref[rows, :]
            for h in range(RET_HEADS):
                c0 = h * RET_DIM
                q = p_ref[rows, c0:c0 + RET_DIM].astype(F32)
                k = p_ref[rows, RET_WIDTH + c0:RET_WIDTH + c0 + RET_DIM].astype(F32)
                v = p_ref[rows, 2 * RET_WIDTH + c0:2 * RET_WIDTH + c0 + RET_DIM]
                g = p_ref[rows, 3 * RET_WIDTH + c0:3 * RET_WIDTH + c0 + RET_DIM].astype(F32)
                o = o_ref[rows, c0:c0 + RET_DIM]
                dc = dc_ref[rows, c0:c0 + RET_DIM].astype(F32)
                rstd = lax.rsqrt(jnp.mean(o * o, axis=-1, keepdims=True) + EPS)
                nrm = o * rstd
                sg = _sigmoid(g)
                dg = dc * nrm * (sg * (1.0 + g * (1.0 - sg)))
                dn = dc * (g * sg)
                do = rstd * (dn - nrm * jnp.mean(dn * nrm, axis=-1, keepdims=True))
                qb = _rope_half(q, cosv, sinv).astype(BF16)
                kr = _rope_half(k, cosv, sinv) * (RET_DIM ** -0.5)
                kb = kr.astype(BF16)
                mask = in_ref[h]
                qd, kd = qd_ref[h], kd_ref[h]
                prev_b = st_ref[h, c]
                dnext = dstate[h]
                dnext_b = dnext.astype(BF16)
                att = (_dot_nt(qb, kb) * mask).astype(BF16)
                do_b = do.astype(BF16)
                doq = (do * qd).astype(BF16)
                dv = _dot_tn(att, do_b) + _dot((kr * kd).astype(BF16), dnext_b)
                ds = (_dot_nt(do_b, v) * mask).astype(BF16)
                dqr = _dot(ds, kb) + _dot_nt(doq, prev_b)
                dkr = _dot_tn(ds, qb) + _dot_nt(v, dnext_b) * kd
                dstate[h] = cd_ref[h] * dnext + _dot_tn(qb, doq)
                dq = _unrope_half(dqr, cosv, sinv)
                dk = _unrope_half(dkr * (RET_DIM ** -0.5), cosv, sinv)
                dp_ref[rows, c0:c0 + RET_DIM] = dq.astype(BF16)
                dp_ref[rows, RET_WIDTH + c0:RET_WIDTH + c0 + RET_DIM] = dk.astype(BF16)
                dp_ref[rows, 2 * RET_WIDTH + c0:2 * RET_WIDTH + c0 + RET_DIM] = dv.astype(BF16)
                dp_ref[rows, 3 * RET_WIDTH + c0:3 * RET_WIDTH + c0 + RET_DIM] = dg.astype(BF16)

    steps = nc // RET_STEP
    rev = lambda n: steps - 1 - n
    full = lambda shape: pl.BlockSpec(shape, lambda n: (0,) * len(shape))
    step = RET_STEP * BLK
    return pl.pallas_call(
        body, name="retention_bwd", grid=(steps,),
        in_specs=[pl.BlockSpec((step, 4 * RET_WIDTH), lambda n: (rev(n), 0)),
                  pl.BlockSpec((step, RET_WIDTH), lambda n: (rev(n), 0)),
                  pl.BlockSpec((RET_HEADS, RET_STEP, RET_DIM, RET_DIM), lambda n: (0, rev(n), 0, 0)),
                  pl.BlockSpec((step, RET_WIDTH), lambda n: (rev(n), 0)),
                  pl.BlockSpec((step, RET_DIM // 2), lambda n: (rev(n), 0)), pl.BlockSpec((step, RET_DIM // 2), lambda n: (rev(n), 0)),
                  full((RET_HEADS, BLK, BLK)), full((RET_HEADS, BLK, 1)), full((RET_HEADS, BLK, 1)), full((RET_HEADS, 1, 1))],
        out_specs=pl.BlockSpec((step, 4 * RET_WIDTH), lambda n: (rev(n), 0)),
        out_shape=jax.ShapeDtypeStruct((s, EVEN_IN), BF16),
        scratch_shapes=[pltpu.VMEM((RET_HEADS, RET_DIM, RET_DIM), F32)],
        compiler_params=_params("arbitrary"))(proj, o, states, dcat, cos, sin, intra, q_dec, k_dec, chunk_dec)


CONV_ROWS = 256
HALO = 16


def _conv_pieces(p, halo, conv_w, first):
    rows = p.shape[0]
    gb, gc, u, gv = (p[:, i * CONV_WIDTH:(i + 1) * CONV_WIDTH] for i in range(4))
    cu = gc * u
    hcu = halo[:, CONV_WIDTH:2 * CONV_WIDTH] * halo[:, 2 * CONV_WIDTH:3 * CONV_WIDTH]
    hcu = jnp.where(first, 0.0, hcu)
    r1, r2 = hcu[HALO - 1:HALO], hcu[HALO - 2:HALO - 1]
    row = lax.broadcasted_iota(jnp.int32, cu.shape, 0)
    m1 = jnp.where(row == 0, r1, pltpu.roll(cu, 1, 0))
    m2 = jnp.where(row == 0, r2, jnp.where(row == 1, r1, pltpu.roll(cu, 2, 0)))
    del rows, conv_w
    return gb, gc, u, gv, cu, m1, m2


def _conv_fwd(proj, conv_w, cat):
    s = proj.shape[0]
    per = CONV_ROWS // HALO

    def body(p_ref, halo_ref, w_ref, cat_in, cat_ref):
        del cat_in
        first = pl.program_id(0) == 0
        gb, _, _, gv, cu, m1, m2 = _conv_pieces(p_ref[...].astype(F32), halo_ref[...].astype(F32), None, first)
        conv = w_ref[0:1, :] * m2 + w_ref[1:2, :] * m1 + w_ref[2:3, :] * cu
        cat_ref[...] = (gb * conv * (gv * _sigmoid(gv))).astype(BF16)

    return pl.pallas_call(
        body, name="conv_fwd", grid=(s // CONV_ROWS,),
        in_specs=[pl.BlockSpec((CONV_ROWS, 4 * CONV_WIDTH), lambda i: (i, 1)),
                  pl.BlockSpec((HALO, 4 * CONV_WIDTH), lambda i: (jnp.maximum(i * per - 1, 0), 1)),
                  pl.BlockSpec((3, CONV_WIDTH), lambda i: (0, 0)), ANY],
        out_specs=pl.BlockSpec((CONV_ROWS, CONV_WIDTH), lambda i: (i, 1)),
        out_shape=jax.ShapeDtypeStruct(cat.shape, cat.dtype), input_output_aliases={3: 0},
        compiler_params=_params("parallel"))(proj, proj, conv_w, cat)


def _conv_bwd(proj, dcat, conv_w, dproj):
    s = proj.shape[0]
    per = CONV_ROWS // HALO
    last_halo = s // HALO - 1
    nsteps = s // CONV_ROWS

    def body(p_ref, halo_ref, nxt_ref, dc_ref, dnxt_ref, w_ref, dp_in, dp_ref, dw_ref):
        del dp_in
        i = pl.program_id(0)
        gb, gc, u, gv, cu, m1, m2 = _conv_pieces(p_ref[...].astype(F32), halo_ref[...].astype(F32), None, i == 0)
        w0, w1, w2 = w_ref[0:1, :], w_ref[1:2, :], w_ref[2:3, :]
        conv = w0 * m2 + w1 * m1 + w2 * cu
        dco = dc_ref[...].astype(F32)
        sg = _sigmoid(gv)
        silu = gv * sg
        dgb = dco * conv * silu
        dgv = dco * gb * conv * (sg * (1.0 + gv * (1.0 - sg)))
        dconv = dco * gb * silu
        nxt = nxt_ref[...].astype(F32)
        ngv = nxt[:, 3 * CONV_WIDTH:]
        dnext = dnxt_ref[...].astype(F32) * nxt[:, :CONV_WIDTH] * (ngv * _sigmoid(ngv))
        dnext = jnp.where(i == nsteps - 1, 0.0, dnext)
        n1, n2 = dnext[0:1], dnext[1:2]
        row = lax.broadcasted_iota(jnp.int32, dconv.shape, 0)
        p1 = jnp.where(row == CONV_ROWS - 1, n1, pltpu.roll(dconv, CONV_ROWS - 1, 0))
        p2 = jnp.where(row == CONV_ROWS - 1, n2, jnp.where(row == CONV_ROWS - 2, n1, pltpu.roll(dconv, CONV_ROWS - 2, 0)))
        dcu = w2 * dconv + w1 * p1 + w0 * p2
        dp_ref[...] = jnp.concatenate([dgb, dcu * u, dcu * gc, dgv], axis=1).astype(BF16)

        @pl.when(i == 0)
        def _():
            dw_ref[...] = jnp.zeros_like(dw_ref)

        taps = [jnp.sum(dconv * m, axis=0, keepdims=True) for m in (m2, m1, cu)]
        r8 = lax.broadcasted_iota(jnp.int32, dw_ref.shape, 0)
        dw_ref[...] += jnp.where(r8 == 0, taps[0], jnp.where(r8 == 1, taps[1], jnp.where(r8 == 2, taps[2], 0.0)))

    return pl.pallas_call(
        body, name="conv_bwd", grid=(nsteps,),
        in_specs=[pl.BlockSpec((CONV_ROWS, 4 * CONV_WIDTH), lambda i: (i, 1)),
                  pl.BlockSpec((HALO, 4 * CONV_WIDTH), lambda i: (jnp.maximum(i * per - 1, 0), 1)),
                  pl.BlockSpec((HALO, 4 * CONV_WIDTH), lambda i: (jnp.minimum((i + 1) * per, last_halo), 1)),
                  pl.BlockSpec((CONV_ROWS, CONV_WIDTH), lambda i: (i, 1)),
                  pl.BlockSpec((HALO, CONV_WIDTH), lambda i: (jnp.minimum((i + 1) * per, last_halo), 1)),
                  pl.BlockSpec((3, CONV_WIDTH), lambda i: (0, 0)), ANY],
        out_specs=[pl.BlockSpec((CONV_ROWS, 4 * CONV_WIDTH), lambda i: (i, 1)), pl.BlockSpec((8, CONV_WIDTH), lambda i: (0, 0))],
        out_shape=[jax.ShapeDtypeStruct(dproj.shape, dproj.dtype), jax.ShapeDtypeStruct((8, CONV_WIDTH), F32)],
        input_output_aliases={6: 0},
        compiler_params=_params("arbitrary"))(proj, proj, proj, dcat, dcat, conv_w, dproj)


GROUP_WIDTH = 8 * HEAD_DIM
GROUP_HEADS = 8
SLAB = 128
Q_COL = 0
KV_COL = 4
GATE_COL = 5
K_SLAB0 = ATTN_WIDTH // SLAB
V_SLAB0 = (ATTN_WIDTH + KV_WIDTH) // SLAB
ATTN_SCALE = HEAD_DIM ** -0.5
KV_ROWS = 1024


def _half_mask(shape, which):
    return (_lane(shape) // HEAD_DIM) == which


def _dup_head(slab, which):
    kept = jnp.where(_half_mask(slab.shape, which), slab, 0.0)
    return kept + pltpu.roll(kept, HEAD_DIM, 1)


def _stack_heads(x):
    parts = []
    for sl in range(GROUP_WIDTH // SLAB):
        slab = x[:, sl * SLAB:(sl + 1) * SLAB]
        parts += [jnp.where(_half_mask(slab.shape, e), slab, 0.0) for e in range(2)]
    return jnp.concatenate(parts, axis=0)


def _unstack_heads(y):
    slabs = []
    for sl in range(GROUP_WIDTH // SLAB):
        a, b = y[(2 * sl) * BLK:(2 * sl + 1) * BLK], y[(2 * sl + 1) * BLK:(2 * sl + 2) * BLK]
        slabs.append(jnp.where(_half_mask(a.shape, 0), a, b))
    return jnp.concatenate(slabs, axis=1)


def _q_prep(q, qw, cosf, sins, ones_bd):
    rstd = lax.rsqrt(_group_mean(q * q, ones_bd) + EPS)
    nrm = q * rstd
    y = nrm * qw
    return nrm, rstd, y * cosf + _partner64(y) * sins


def _band(tri_ref, n):
    own = tri_ref[...] > 0.5
    return own, jnp.where(jnp.logical_and(n == 0, jnp.logical_not(own)), -1e30, 0.0)


def _fold(pair, own):
    return jnp.where(own, pair[:, BLK:], pair[:, :BLK])


def _unfold(folded, own):
    return jnp.concatenate([jnp.where(own, 0.0, folded), jnp.where(own, folded, 0.0)], axis=1)


def _head_probs(raw_scores, sink, own, bias):
    sc = _fold(raw_scores, own) + bias
    m = jnp.maximum(jnp.max(sc, axis=-1, keepdims=True), sink)
    p = jnp.exp(sc - m)
    psink = jnp.exp(sink - m)
    inv = 1.0 / (jnp.sum(p, axis=-1, keepdims=True) + psink)
    return p * inv, psink * inv


def _k_prep(k, kw, cosf, sins, ones_bd):
    rstd = lax.rsqrt(_group_mean(k * k, ones_bd) + EPS)
    nrm = k * rstd
    y = nrm * kw
    return nrm, rstd, y * cosf + _partner64(y) * sins


def _qk_prep(proj, qw, kw, cos, sins, ones_q, ones_kv):
    s = proj.shape[0]
    rows = min(KV_ROWS, s)

    def body(p_ref, qw_ref, kw_ref, cos_ref, sin_ref, oq_ref, ok_ref, o_ref):
        j = pl.program_id(1)

        @pl.when(j < KV_COL)
        def _():
            cosf, sinf = _tile_lanes(cos_ref[...], 4), _tile_lanes(sin_ref[...], 4)
            roped = _q_prep(p_ref[...].astype(F32), qw_ref[...], cosf, sinf, oq_ref[...])[2]
            o_ref[...] = (roped * ATTN_SCALE).astype(BF16)

        @pl.when(j == KV_COL)
        def _():
            cosf, sinf = _tile_lanes(cos_ref[...], 2), _tile_lanes(sin_ref[...], 2)
            kr = _k_prep(p_ref[:, :KV_WIDTH].astype(F32), kw_ref[...], cosf, sinf, ok_ref[...])[2]
            o_ref[...] = jnp.concatenate([kr.astype(BF16), p_ref[:, KV_WIDTH:]], axis=1)

    full = lambda shape: pl.BlockSpec(shape, lambda i, j: (0,) * len(shape))
    tab = pl.BlockSpec((rows, SLAB), lambda i, j: (i, 0))
    blk = pl.BlockSpec((rows, GROUP_WIDTH), lambda i, j: (i, j))
    return pl.pallas_call(
        body, name="qk_prep", grid=(s // rows, KV_COL + 1),
        in_specs=[blk, full((1, GROUP_WIDTH)), full((1, KV_WIDTH)), tab, tab, full((GROUP_WIDTH, GROUP_WIDTH)),
                  full((KV_WIDTH, KV_WIDTH))],
        out_specs=blk, out_shape=jax.ShapeDtypeStruct((s, ATTN_WIDTH + 2 * KV_WIDTH), BF16),
        compiler_params=_params("parallel", "arbitrary"))(proj, qw, kw, cos, sins, ones_q, ones_kv)


def _keys_values(kc_ref, kp_ref, vc_ref, vp_ref, head):
    lanes = slice((head // 2) * SLAB, (head // 2 + 1) * SLAB)
    dup = lambda ref: _dup_head(ref[:, lanes].astype(F32), head % 2)
    return (jnp.concatenate([dup(kp_ref), dup(kc_ref)], axis=0).astype(BF16),
            jnp.concatenate([dup(vp_ref), dup(vc_ref)], axis=0).astype(BF16))


FWD_STEP_HEADS = 4
BWD_STEP_HEADS = 2


def _swa_specs(heads):
    kv_width = heads * HEAD_DIM
    prev = lambda n: jnp.maximum(n - 1, 0)
    kv = lambda col0, row: pl.BlockSpec((BLK, kv_width), lambda gs, n: (row(n), col0 // kv_width + gs))
    cur = lambda n: n
    full = lambda shape: pl.BlockSpec(shape, lambda gs, n: (0,) * len(shape))
    gate = lambda t: pl.BlockSpec((BLK, GROUP_WIDTH), lambda gs, n: (n, GATE_COL + heads * gs + t))
    return dict(
        sinks=pl.BlockSpec(memory_space=pltpu.SMEM), gates=[gate(t) for t in range(heads)],
        kc=kv(ATTN_WIDTH, cur), kp=kv(ATTN_WIDTH, prev), vc=kv(ATTN_WIDTH + KV_WIDTH, cur), vp=kv(ATTN_WIDTH + KV_WIDTH, prev),
        tri=full((BLK, BLK)), step=pl.BlockSpec((BLK, heads * GROUP_WIDTH), lambda gs, n: (n, gs)))


def _lower_triangle():
    return jnp.tril(jnp.ones((BLK, BLK), F32))


def _swa_fwd(qk, proj, sinks):
    s = proj.shape[0]
    nb = s // BLK
    heads = FWD_STEP_HEADS
    sp = _swa_specs(heads)

    def body(sink_ref, q_ref, kc_ref, kp_ref, vc_ref, vp_ref, *rest):
        gate_refs, (tri_ref, ag_ref, o_ref) = rest[:heads], rest[heads:]
        gs, n = pl.program_id(0), pl.program_id(1)
        own, bias = _band(tri_ref, n)
        for t, gate_ref in enumerate(gate_refs):
            cols = slice(t * GROUP_WIDTH, (t + 1) * GROUP_WIDTH)
            first_head = (heads * gs + t) * GROUP_HEADS
            kcat, vcat = _keys_values(kc_ref, kp_ref, vc_ref, vp_ref, t)
            scores = _dot_nt(_stack_heads(q_ref[:, cols]), kcat)
            probs = []
            for j in range(GROUP_HEADS):
                p, _ = _head_probs(scores[j * BLK:(j + 1) * BLK], sink_ref[first_head + j], own, bias)
                probs.append(_unfold(p, own).astype(BF16))
            o = _unstack_heads(_dot(jnp.concatenate(probs, axis=0), vcat))
            gate = gate_ref[...].astype(F32)
            o_ref[:, cols] = o.astype(BF16)
            ag_ref[:, cols] = (o * (gate * _sigmoid(gate))).astype(BF16)

    shp = jax.ShapeDtypeStruct((s, ATTN_WIDTH), BF16)
    return pl.pallas_call(
        body, name="swa_fwd", grid=(KV_HEADS // heads, nb),
        in_specs=[sp["sinks"], sp["step"], sp["kc"], sp["kp"], sp["vc"], sp["vp"], *sp["gates"], sp["tri"]],
        out_specs=[sp["step"], sp["step"]], out_shape=[shp, shp],
        compiler_params=_params("parallel", "arbitrary"),
    )(sinks, qk, qk, qk, qk, qk, *[proj] * heads, _lower_triangle())


def _swa_bwd(qk, proj, dag, sinks):
    s = proj.shape[0]
    nb = s // BLK
    heads = BWD_STEP_HEADS
    sp = _swa_specs(heads)

    def body(sink_ref, q_ref, kc_ref, kp_ref, vc_ref, vp_ref, *rest):
        gate_refs = rest[:heads]
        dag_ref, tri_ref, dq_ref, dkc_ref, dkp_ref, dvc_ref, dvp_ref, dsink_ref = rest[heads:]
        gs, n = pl.program_id(0), pl.program_id(1)
        own, bias = _band(tri_ref, n)

        @pl.when(n == 0)
        def _():
            dsink_ref[...] = jnp.zeros_like(dsink_ref)

        for t, gate_ref in enumerate(gate_refs):
            cols = slice(t * GROUP_WIDTH, (t + 1) * GROUP_WIDTH)
            first_head = (heads * gs + t) * GROUP_HEADS
            kcat, vcat = _keys_values(kc_ref, kp_ref, vc_ref, vp_ref, t)
            gate = gate_ref[...].astype(F32)
            do = dag_ref[:, cols].astype(F32) * (gate * _sigmoid(gate))
            q_stack = _stack_heads(q_ref[:, cols])
            do_stack = _stack_heads(do).astype(BF16)
            scores = _dot_nt(q_stack, kcat)
            dprobs = _dot_nt(do_stack, vcat)
            probs, dscores, dsinks = [], [], []
            for j in range(GROUP_HEADS):
                rows = slice(j * BLK, (j + 1) * BLK)
                p, psink = _head_probs(scores[rows], sink_ref[first_head + j], own, bias)
                dp = _fold(dprobs[rows], own)
                delta = jnp.sum(p * dp, axis=-1, keepdims=True)
                probs.append(_unfold(p, own).astype(BF16))
                dscores.append(_unfold(p * (dp - delta), own).astype(BF16))
                dsinks.append(-jnp.sum(psink * delta, axis=0, keepdims=True))
            ds = jnp.concatenate(dscores, axis=0)
            dk = _dot_tn(ds, q_stack)
            dv = _dot_tn(jnp.concatenate(probs, axis=0), do_stack)
            dk = dk + pltpu.roll(dk, HEAD_DIM, 1)
            dv = dv + pltpu.roll(dv, HEAD_DIM, 1)
            dkp_ref[t], dkc_ref[t] = dk[:BLK], dk[BLK:]
            dvp_ref[t], dvc_ref[t] = dv[:BLK], dv[BLK:]
            dq_ref[:, cols] = _unstack_heads(_dot(ds, kcat)).astype(BF16)
            r8 = lax.broadcasted_iota(jnp.int32, (8, SLAB), 0)
            upd = jnp.zeros((8, SLAB), F32)
            for j in range(GROUP_HEADS):
                upd = jnp.where(r8 == j, dsinks[j], upd)
            dsink_ref[t] += upd

    cur_out = pl.BlockSpec((heads, BLK, SLAB), lambda gs, n: (gs, n, 0))
    prev_out = pl.BlockSpec((heads, BLK, SLAB), lambda gs, n: (gs, (n + nb - 1) % nb, 0))
    kv_shape = jax.ShapeDtypeStruct((KV_HEADS, s, SLAB), F32)
    return pl.pallas_call(
        body, name="swa_bwd", grid=(KV_HEADS // heads, nb),
        in_specs=[sp["sinks"], sp["step"], sp["kc"], sp["kp"], sp["vc"], sp["vp"], *sp["gates"], sp["step"], sp["tri"]],
        out_specs=[sp["step"], cur_out, prev_out, cur_out, prev_out,
                   pl.BlockSpec((heads, 8, SLAB), lambda gs, n: (gs, 0, 0))],
        out_shape=[jax.ShapeDtypeStruct((s, ATTN_WIDTH), BF16), kv_shape, kv_shape, kv_shape, kv_shape,
                   jax.ShapeDtypeStruct((KV_HEADS, 8, SLAB), F32)],
        compiler_params=_params("parallel", "arbitrary"),
    )(sinks, qk, qk, qk, qk, qk, *[proj] * heads, dag, _lower_triangle())


def _swa_bwd_finish(proj, dqr, o, dag, dkc, dkp, dvc, dvp, qw, kw, cos, sins, ones_q, ones_kv):
    s = proj.shape[0]
    rows = min(KV_ROWS, s)
    n_q = KV_COL
    q_of = lambda j: jnp.clip(j - 1, 0, n_q - 1)
    gate_of = lambda j: jnp.clip(j - 1 - n_q, 0, n_q - 1)

    def body(kv_ref, p_ref, dqr_ref, o_ref, dag_ref, dkc_ref, dkp_ref, dvc_ref, dvp_ref, qw_ref, kw_ref, cos_ref, sin_ref,
             oq_ref, ok_ref, dp_ref, dqw_ref, dkw_ref):
        j, i = pl.program_id(0), pl.program_id(1)

        @pl.when(j == 0)
        def _():
            @pl.when(i == 0)
            def _():
                dkw_ref[...] = jnp.zeros_like(dkw_ref)

            def assemble(cur_ref, prv_ref):
                tot = [cur_ref[h] + prv_ref[h] for h in range(KV_HEADS)]
                first = _half_mask(tot[0].shape, 0)
                return jnp.concatenate([jnp.where(first, tot[0], tot[1]), jnp.where(first, tot[2], tot[3])], axis=1)

            dkr = assemble(dkc_ref, dkp_ref)
            dv = assemble(dvc_ref, dvp_ref)
            cosf, sinf = _tile_lanes(cos_ref[...], 2), _tile_lanes(sin_ref[...], 2)
            nrm, rstd, _ = _k_prep(kv_ref[:, :KV_WIDTH].astype(F32), kw_ref[...], cosf, sinf, ok_ref[...])
            dy = dkr * cosf + _partner64(dkr * sinf)
            dn = dy * kw_ref[...]
            dk = rstd * (dn - nrm * _group_mean(dn * nrm, ok_ref[...], passes=1))
            dp_ref[...] = jnp.concatenate([dk, dv], axis=1).astype(BF16)
            dkw_ref[...] += jnp.sum(dy * nrm, axis=0, keepdims=True)

        @pl.when(jnp.logical_and(j >= 1, j <= n_q))
        def _():
            @pl.when(i == 0)
            def _():
                dqw_ref[...] = jnp.zeros_like(dqw_ref)

            cosf, sinf = _tile_lanes(cos_ref[...], 4), _tile_lanes(sin_ref[...], 4)
            nrm, rstd, _ = _q_prep(p_ref[...].astype(F32), qw_ref[...], cosf, sinf, oq_ref[...])
            dq = dqr_ref[...].astype(F32) * ATTN_SCALE
            dy = dq * cosf + _partner64(dq * sinf)
            dn = dy * qw_ref[...]
            dp_ref[...] = (rstd * (dn - nrm * _group_mean(dn * nrm, oq_ref[...], passes=1))).astype(BF16)
            dqw_ref[0] += jnp.sum(dy * nrm, axis=0, keepdims=True)

        @pl.when(j > n_q)
        def _():
            gate = p_ref[...].astype(F32)
            sg = _sigmoid(gate)
            dp_ref[...] = (dag_ref[...].astype(F32) * o_ref[...].astype(F32) * (sg * (1.0 + gate * (1.0 - sg)))).astype(BF16)

    first_pass = lambda j, i: jnp.where(j == 0, i, 0)
    acc = pl.BlockSpec((KV_HEADS, rows, SLAB), lambda j, i: (0, first_pass(j, i), 0))
    full = lambda shape: pl.BlockSpec(shape, lambda j, i: (0,) * len(shape))
    tab = pl.BlockSpec((rows, SLAB), lambda j, i: (i, 0))
    out_col = lambda j: jnp.where(j == 0, KV_COL, jnp.where(j <= n_q, j - 1, j))
    return pl.pallas_call(
        body, name="swa_bwd_finish", grid=(2 * n_q + 1, s // rows),
        in_specs=[pl.BlockSpec((rows, GROUP_WIDTH), lambda j, i: (first_pass(j, i), KV_COL)),
                  pl.BlockSpec((rows, GROUP_WIDTH), lambda j, i: (jnp.where(j == 0, 0, i), jnp.where(j <= n_q, q_of(j), j))),
                  pl.BlockSpec((rows, GROUP_WIDTH), lambda j, i: (jnp.where(jnp.logical_and(j >= 1, j <= n_q), i, 0), q_of(j))),
                  pl.BlockSpec((rows, GROUP_WIDTH), lambda j, i: (jnp.where(j > n_q, i, 0), gate_of(j))),
                  pl.BlockSpec((rows, GROUP_WIDTH), lambda j, i: (jnp.where(j > n_q, i, 0), gate_of(j))),
                  acc, acc, acc, acc, full((1, GROUP_WIDTH)), full((1, KV_WIDTH)), tab, tab,
                  full((GROUP_WIDTH, GROUP_WIDTH)), full((KV_WIDTH, KV_WIDTH))],
        out_specs=[pl.BlockSpec((rows, GROUP_WIDTH), lambda j, i: (i, out_col(j))),
                   pl.BlockSpec((1, 8, GROUP_WIDTH), lambda j, i: (q_of(j), 0, 0)), pl.BlockSpec((8, KV_WIDTH), lambda j, i: (0, 0))],
        out_shape=[jax.ShapeDtypeStruct((s, ODD_IN), BF16), jax.ShapeDtypeStruct((n_q, 8, GROUP_WIDTH), F32),
                   jax.ShapeDtypeStruct((8, KV_WIDTH), F32)],
        compiler_params=_params("arbitrary", "arbitrary"),
    )(proj, proj, dqr, o, dag, dkc, dkp, dvc, dvp, qw, kw, cos, sins, ones_q, ones_kv)


def _place():
    x, y, c = lax.axis_index("x"), lax.axis_index("y"), lax.axis_index("c")
    return x, y, c


OTHER_CHIPS = ((1, 0), (0, 1), (1, 1))


def _half_rows(ref, half, rows):
    return ref.at[pl.ds(pl.multiple_of(half * (rows // 2), 8), rows // 2)]


DMA_CHUNK_BYTES = 1 << 20
BF16_TILE_ROWS = 16


def _n_chunks(ref):
    rows = ref.shape[-2]
    nbytes = math.prod(ref.shape) * jnp.dtype(ref.dtype).itemsize
    n = 1
    while 2 * n * DMA_CHUNK_BYTES <= nbytes and rows % (2 * n * BF16_TILE_ROWS) == 0:
        n *= 2
    return n


def _row_chunk(ref, k, n):
    rows = ref.shape[-2] // n
    return ref.at[pl.ds(k * rows, rows)] if len(ref.shape) == 2 else ref.at[:, pl.ds(k * rows, rows)]


def _push(src, dst, send_sem, recv_sem, device_id):
    n = _n_chunks(src)
    for k in range(n):
        pltpu.make_async_remote_copy(src_ref=_row_chunk(src, k, n), dst_ref=_row_chunk(dst, k, n), send_sem=send_sem,
                                     recv_sem=recv_sem, device_id=device_id, device_id_type=MESH).start()
    return pltpu.make_async_remote_copy(src_ref=src, dst_ref=dst, send_sem=send_sem, recv_sem=recv_sem,
                                        device_id=device_id, device_id_type=MESH)


def _copy(src, dst, sem):
    n = _n_chunks(src)
    for k in range(n):
        pltpu.make_async_copy(_row_chunk(src, k, n), _row_chunk(dst, k, n), sem).start()
    return pltpu.make_async_copy(src, dst, sem)


HBM = pl.BlockSpec(memory_space=pltpu.HBM)
SEM = pl.BlockSpec(memory_space=pltpu.SEMAPHORE)
SPLIT_COPY_EFFECT = pltpu.SideEffectType.DATAFLOW_SIDE_EFFECTING


def _in_hbm(a):
    return pltpu.with_memory_space_constraint(a, pltpu.HBM)


def _start_copies(name, arrays, plan, n_copies, after=None):
    n = len(arrays)

    def body(*refs):
        send_sem, recv_sem = refs[-n - 3], refs[-n - 2]
        for k, (src, dst, peer) in enumerate(plan(refs[:n])):
            _push(src, dst, send_sem.at[k], recv_sem.at[k], peer)
        refs[-1][...] = jnp.zeros_like(refs[-1])

    dma = pltpu.SemaphoreType.DMA((n_copies,))
    outs = pl.pallas_call(
        body, name=name,
        out_shape=(dma, dma, *[pltpu.HBM(a.shape, a.dtype) for a in arrays], jax.ShapeDtypeStruct((8, 128), F32)),
        in_specs=[HBM] * n + ([ANY] if after is not None else []),
        out_specs=(SEM, SEM, *[HBM] * n, pl.BlockSpec(memory_space=pltpu.VMEM)),
        input_output_aliases={i: i + 2 for i in range(n)},
        compiler_params=pltpu.CompilerParams(has_side_effects=SPLIT_COPY_EFFECT),
    )(*[_in_hbm(a) for a in arrays], *((after,) if after is not None else ()))
    return outs[0], outs[1], list(outs[2:2 + n]), outs[-1]


def _wait_copies(name, send_sem, recv_sem, arrays, plan, after):
    n = len(arrays)

    def body(*refs):
        send_ref, recv_ref = refs[n], refs[n + 1]
        for k, (src, dst, peer) in enumerate(plan(refs[:n])):
            cp = pltpu.make_async_remote_copy(src_ref=src, dst_ref=dst, send_sem=send_ref.at[k], recv_sem=recv_ref.at[k],
                                              device_id=peer, device_id_type=MESH)
            cp.wait_send()
            cp.wait_recv()

    return list(pl.pallas_call(
        body, name=name, out_shape=tuple(pltpu.HBM(a.shape, a.dtype) for a in arrays),
        in_specs=[HBM] * n + [SEM, SEM, ANY], out_specs=tuple([HBM] * n),
        input_output_aliases={i: i for i in range(n)},
        compiler_params=pltpu.CompilerParams(has_side_effects=SPLIT_COPY_EFFECT),
    )(*arrays, send_sem, recv_sem, after))


def _gather_region(full, kind, chip, half=None):
    if kind == "whole":
        return full.at[chip]
    if kind == "col":
        rows, width = full.shape[0], full.shape[1] // N_CHIPS
        piece = full.at[:, pl.ds(pl.multiple_of(chip * width, 128), width)]
    else:
        rows = full.shape[0] // N_CHIPS
        piece = full.at[pl.ds(pl.multiple_of(chip * rows, BF16_TILE_ROWS), rows)]
    return piece if half is None else _half_rows(piece, half, rows)


def _gather_plan(kinds):
    def plan(fulls):
        x, y, c = _place()
        copies = []
        for fx, fy in OTHER_CHIPS:
            for full, kind in zip(fulls, kinds):
                mine = _gather_region(full, kind, 2 * x + y, c)
                copies.append((mine, mine, (x ^ fx, y ^ fy, c)))
        return copies

    return plan


def _gather_start(name, fulls, kinds, after=None):
    return _start_copies(name, list(fulls), _gather_plan(kinds), 3 * len(kinds), after)


def _gather_finish(name, started, kinds, after):
    send_sem, recv_sem, fulls, _ = started
    n = len(kinds)
    fulls = _wait_copies(name + "_wait", send_sem, recv_sem, fulls, _gather_plan(kinds), after)
    split = [i for i, kind in enumerate(kinds) if kind != "whole"]

    def body(*refs):
        out = refs[n:2 * n]
        send, recv = refs[2 * n:]
        x, y, c = _place()
        pushes = []
        for r, (fx, fy) in enumerate(OTHER_CHIPS):
            chip = 2 * (x ^ fx) + (y ^ fy)
            for j, i in enumerate(split):
                landed = _gather_region(out[i], kinds[i], chip, c)
                pushes.append(_push(landed, landed, send.at[r * len(split) + j], recv.at[r * len(split) + j], (x, y, 1 - c)))
        for cp in pushes:
            cp.wait_recv()
        for cp in pushes:
            cp.wait_send()

    dma = pltpu.SemaphoreType.DMA
    return pl.pallas_call(
        body, name=name + "_share", in_specs=[ANY] * n, out_specs=[ANY] * n,
        out_shape=[jax.ShapeDtypeStruct(a.shape, a.dtype) for a in fulls],
        input_output_aliases={i: i for i in range(n)},
        scratch_shapes=[dma((3 * len(split),)), dma((3 * len(split),))],
        compiler_params=pltpu.CompilerParams(has_side_effects=True),
    )(*fulls)


def _allreduce_small(v):
    def body(v_ref, out_ref, buf, send_sems, recv_sems):
        x, y, c = _place()
        me = 4 * x + 2 * y + c
        buf[me] = v_ref[...]
        copies = []
        for r in range(1, N_DEV):
            peer = (x ^ (r >> 2), y ^ ((r >> 1) & 1), c ^ (r & 1))
            cp = pltpu.make_async_remote_copy(src_ref=v_ref, dst_ref=buf.at[me], send_sem=send_sems.at[r - 1],
                                              recv_sem=recv_sems.at[r - 1], device_id=peer, device_id_type=MESH)
            cp.start()
            copies.append(cp)
        for cp in copies:
            cp.wait_recv()
        for cp in copies:
            cp.wait_send()
        total = buf[0]
        for d in range(1, N_DEV):
            total = total + buf[d]
        out_ref[...] = total

    vm = pl.BlockSpec(memory_space=pltpu.VMEM)
    return pl.pallas_call(
        body, name="allreduce_small", in_specs=[vm], out_specs=vm, out_shape=jax.ShapeDtypeStruct(v.shape, v.dtype),
        scratch_shapes=[pltpu.VMEM((N_DEV,) + v.shape, v.dtype), pltpu.SemaphoreType.DMA((N_DEV - 1,)),
                        pltpu.SemaphoreType.DMA((N_DEV - 1,))],
        compiler_params=pltpu.CompilerParams(has_side_effects=True),
    )(v)


def _exchange_halves(grads, name):
    n = len(grads)

    def body(*refs):
        g, theirs = refs[:n], refs[n:2 * n]
        send_sem, recv_sem = refs[2 * n:]
        x, y, c = _place()
        copies = []
        for i in range(n):
            half = g[i].shape[1] // 2
            src = g[i].at[:, pl.ds(pl.multiple_of((1 - c) * half, BF16_TILE_ROWS), half)]
            copies.append(_push(src, theirs[i], send_sem.at[i], recv_sem.at[i], (x, y, 1 - c)))
        for cp in copies:
            cp.wait_recv()
            cp.wait_send()

    dma = pltpu.SemaphoreType.DMA
    return pl.pallas_call(
        body, name=name, in_specs=[ANY] * n, out_specs=[ANY] * n,
        out_shape=[jax.ShapeDtypeStruct((a.shape[0], a.shape[1] // 2, a.shape[2]), a.dtype) for a in grads],
        scratch_shapes=[dma((n,)), dma((n,))],
        compiler_params=pltpu.CompilerParams(has_side_effects=True),
    )(*grads)


def _pair_sum(g, theirs, core, name):
    pieces, half, cols = theirs.shape
    tr = min(half, 256)
    per = half // tr

    def body(core_ref, g_ref, t_ref, o_ref):
        del core_ref
        o_ref[...] = (g_ref[...].astype(F32) + t_ref[...].astype(F32)).astype(BF16)

    spec = pl.BlockSpec((1, tr, cols), lambda p, i, core_ref: (p, i, 0))
    return pl.pallas_call(
        body, name=name, out_shape=jax.ShapeDtypeStruct(theirs.shape, BF16),
        grid_spec=pltpu.PrefetchScalarGridSpec(
            num_scalar_prefetch=1, grid=(pieces, per),
            in_specs=[pl.BlockSpec((1, tr, cols), lambda p, i, core_ref: (p, core_ref[0] * per + i, 0)), spec],
            out_specs=spec),
        compiler_params=_params("parallel", "parallel"))(core, g, theirs)


def _scatter_plan(n):
    def plan(refs):
        parts, stacks = refs[:n], refs[n:]
        x, y, c = _place()
        copies = []
        for fx, fy in OTHER_CHIPS:
            chip = 2 * (x ^ fx) + (y ^ fy)
            for part, stack in zip(parts, stacks):
                if part.shape[0] == N_CHIPS:
                    piece = part.at[chip]
                else:
                    width = part.shape[2] // N_CHIPS
                    piece = part.at[0].at[:, pl.ds(pl.multiple_of(chip * width, 128), width)]
                copies.append((piece, stack.at[2 * x + y], (x ^ fx, y ^ fy, c)))
        return copies

    return plan


def _scatter_start(name, parts, after=None):
    def landing(a):
        return (N_CHIPS, a.shape[1], a.shape[2] if a.shape[0] == N_CHIPS else a.shape[2] // N_CHIPS)

    stacks = [lax.empty(landing(a), a.dtype) for a in parts]
    return _start_copies(name, list(parts) + stacks, _scatter_plan(len(parts)), 3 * len(parts), after)


def _scatter_finish(name, started, after):
    send_sem, recv_sem, arrays, _ = started
    n = len(arrays) // 2
    arrays = _wait_copies(name + "_wait", send_sem, recv_sem, arrays, _scatter_plan(n), after)
    return arrays[:n], arrays[n:]


def _sum_chips(part, stack, place, name):
    _, r, c = stack.shape
    tr = 256
    per = r // tr

    def body(place_ref, own_ref, a_ref, b_ref, c_ref, o_ref):
        del place_ref
        total = own_ref[0].astype(F32)
        for ref in (a_ref, b_ref, c_ref):
            total = total + ref[0].astype(F32)
        o_ref[...] = total

    if part.shape[0] == N_CHIPS:
        own = pl.BlockSpec((1, tr, c), lambda i, pr: (pr[1], i, 0))
    else:
        own = pl.BlockSpec((1, tr, c), lambda i, pr: (0, i, pr[1]))
    other = lambda flip: pl.BlockSpec((1, tr, c), lambda i, pr: (pr[1] ^ flip, i, 0))
    return pl.pallas_call(
        body, name=name, out_shape=jax.ShapeDtypeStruct((2 * r, c), F32),
        grid_spec=pltpu.PrefetchScalarGridSpec(
            num_scalar_prefetch=1, grid=(per,), in_specs=[own, other(2), other(1), other(3)],
            out_specs=pl.BlockSpec((tr, c), lambda i, pr: (pr[0] * per + i, 0))),
        compiler_params=_params("parallel"))(place, part, stack, stack, stack)


def _share_halves(fulls):
    n = len(fulls)

    def body(*refs):
        dst = refs[n:2 * n]
        send_sem, recv_sem = refs[2 * n:]
        x, y, c = _place()
        copies = []
        for i in range(n):
            mine = _half_rows(dst[i], c, dst[i].shape[0])
            copies.append(_push(mine, mine, send_sem.at[i], recv_sem.at[i], (x, y, 1 - c)))
        for cp in copies:
            cp.wait_recv()
            cp.wait_send()

    dma = pltpu.SemaphoreType.DMA
    return pl.pallas_call(
        body, name="share_halves", in_specs=[ANY] * n, out_specs=[ANY] * n,
        out_shape=[jax.ShapeDtypeStruct(a.shape, a.dtype) for a in fulls],
        input_output_aliases={i: i for i in range(n)}, scratch_shapes=[dma((n,)), dma((n,))],
        compiler_params=pltpu.CompilerParams(has_side_effects=True),
    )(*fulls)


MM = dict(tm=2048, tn=1024, tk=2048)
MM_LONG_K = dict(tm=1024, tn=1024, tk=4096)


def _local_step(x, target, ev_norm_w, q_norm_w, k_norm_w, sinks, own_first, weights_first, weights_late, emit):
    s = x.shape[0]
    cos_r, sin_r = _rope_tables(s, RET_DIM)
    cos_a, sin_a = _rope_tables(s, HEAD_DIM)
    cos_a = np.tile(cos_a, (1, 4))
    sins_a = np.tile(np.concatenate([-sin_a, sin_a], axis=1), (1, 2))
    tables = _retention_tables()
    ones_q, ones_kv = _block_diag_mean(GROUP_WIDTH), _block_diag_mean(KV_WIDTH)
    qw_g = jnp.tile(q_norm_w, (1, GROUP_WIDTH // HEAD_DIM))
    kw_kv = jnp.tile(k_norm_w, (1, KV_WIDTH // HEAD_DIM))
    sinks1 = sinks.reshape(Q_HEADS)

    own_w_in0, own_block, start_token = own_first
    h0 = _rmsnorm(x, ev_norm_w, "norm0", after=start_token)
    shifted = dict(shift=own_block * (own_w_in0.shape[1] // MM["tn"]), total=EVEN_IN // MM["tn"], tm=MM["tm"], tn=MM["tn"],
                   out_dtype=BF16)
    own_blocks = own_w_in0.shape[1] // MM["tn"]
    proj0 = _mm_shifted(h0, own_w_in0, b_shifted=False, first=0, count=own_blocks, name="proj0_own", **shifted)
    w_in0, conv_w, od_norm_w, token = weights_first(proj0)
    proj0 = _mm_shifted(h0, w_in0, b_shifted=True, first=own_blocks, count=shifted["total"] - own_blocks, name="proj0_rest",
                        into=proj0, after=token, **shifted)
    o_ret, cat, states = _retention_fwd(proj0, cos_r, sin_r, tables)
    cat = _conv_fwd(proj0, conv_w, cat)
    w_out0, w_in1, w_out1 = weights_late(cat)

    def residual_and_norm(prod, x_ref, w_ref, x1_ref, h1_ref):
        x1v = x_ref[...] + prod
        x1_ref[...] = x1v
        rstd = lax.rsqrt(jnp.mean(x1v * x1v, axis=-1, keepdims=True) + EPS)
        h1_ref[...] = (x1v * rstd * w_ref[...]).astype(BF16)

    def residual_and_loss(prod, x1_ref, t_ref, dyb_ref, sq_ref):
        diff = (x1_ref[...] + prod) - t_ref[...]
        dyb_ref[...] = (diff * (1.0 / D_MODEL)).astype(BF16)

        @pl.when(pl.program_id(0) == 0)
        def _():
            sq_ref[...] = jnp.zeros_like(sq_ref)

        sq_ref[...] += jnp.sum(jnp.sum(diff * diff, axis=1, keepdims=True), axis=0, keepdims=True)

    act = lambda dt: jax.ShapeDtypeStruct((s, D_MODEL), dt)
    x1, h1 = _mm_rows(cat, w_out0, [x], [od_norm_w], [act(F32), act(BF16)], residual_and_norm, tm=min(s, 512), name="out0")
    proj1 = _mm(h1, w_in1, mode="nn", out_dtype=BF16, name="proj1", tm=2048, tn=1536, tk=2048)
    qk = _qk_prep(proj1, qw_g, kw_kv, cos_a, sins_a, ones_q, ones_kv)
    ag, o_att = _swa_fwd(qk, proj1, sinks1)
    dy_b, sq = _mm_rows(ag, w_out1, [x1, target], [], [act(BF16), jax.ShapeDtypeStruct((8, 128), F32)],
                        residual_and_loss, tm=min(s, 512), name="out1")

    g_w_out1 = _mm(ag, dy_b, mode="tn", out_dtype=BF16, name="g_w_out1", **MM_LONG_K)
    dag = _mm(dy_b, w_out1, mode="nt", out_dtype=BF16, name="d_ag", **MM)
    dqr, dkc, dkp, dvc, dvp, dsink = _swa_bwd(qk, proj1, dag, sinks1)
    dproj1, dqw, dkw = _swa_bwd_finish(proj1, dqr, o_att, dag, dkc, dkp, dvc, dvp, qw_g, kw_kv, cos_a, sins_a, ones_q, ones_kv)
    g_w_in1 = _mm(h1, dproj1, mode="tn", out_dtype=BF16, name="g_w_in1", tm=1024, tn=768, tk=4096)
    dh1 = _mm(dproj1, w_in1, mode="nt", out_dtype=BF16, name="d_h1", tm=1024, tn=1024, tk=ODD_IN)
    dx1_b, g_norm1 = _rmsnorm_bwd(x1, od_norm_w, dh1, dy_b, "norm1_bwd", BF16)

    g_w_out0 = _mm(cat, dx1_b, mode="tn", out_dtype=BF16, name="g_w_out0", **MM_LONG_K)
    token = emit("upper", (("od_w_in", g_w_in1, "col"), ("od_w_out", g_w_out1, "row"), ("ev_w_out", g_w_out0, "row")))
    dcat = _mm(dx1_b, w_out0, mode="nt", out_dtype=BF16, name="d_cat", after=token, **MM)
    dproj0 = _retention_bwd(proj0, o_ret, states, dcat, cos_r, sin_r, tables)
    dproj0, g_conv = _conv_bwd(proj0, dcat, conv_w, dproj0)
    g_w_in0 = _mm(h0, dproj0, mode="tn", out_dtype=BF16, name="g_w_in0", **MM_LONG_K)
    token = emit("in0", (("ev_w_in", g_w_in0, "col"),))
    dh0 = _mm(dproj0, w_in0, mode="nt", out_dtype=BF16, name="d_h0", after=token, **MM_LONG_K)
    grad_x, g_norm0 = _rmsnorm_bwd(x, ev_norm_w, dh0, dx1_b, "norm0_bwd", F32)

    g_qw = dqw[:, 0, :].reshape(Q_HEADS, HEAD_DIM).sum(axis=0)
    g_kw = dkw[0].reshape(KV_HEADS, HEAD_DIM).sum(axis=0)
    g_sinks = dsink[:, :, 0].reshape(Q_HEADS)
    small = dict(ev_norm=g_norm0[0], od_norm=g_norm1[0], conv=g_conv[:3], qw=g_qw, kw=g_kw, sinks=g_sinks)
    return sq[0, 0], grad_x, small


def _pack_small_grads(small, sq):
    pad = lambda v: jnp.pad(v, (0, D_MODEL - v.shape[0]))
    tail = pad(jnp.concatenate([small["qw"], small["kw"], small["sinks"]]))
    rows = [small["ev_norm"], small["od_norm"]] + [pad(small["conv"][t]) for t in range(3)] + [tail, pad(sq.reshape(1))]
    rows += [jnp.zeros((D_MODEL,), F32)] * (8 - len(rows))
    return jnp.stack(rows)


class _ReduceScatter:
    def __init__(self, place):
        self.place = place
        self.started = []

    def send(self, tag, grads):
        pieces = [g[None] if kind == "col" else g.reshape(N_CHIPS, g.shape[0] // N_CHIPS, g.shape[1]) for _, g, kind in grads]
        theirs = _exchange_halves(pieces, "exchange_halves_" + tag)
        parts = [_pair_sum(g, t, self.place[:1], "pair_sum_" + nm) for g, t, (nm, _, _) in zip(pieces, theirs, grads)]
        started = _scatter_start("scatter_" + tag, parts)
        self.started.append((tag, [nm for nm, _, _ in grads], started))
        return started[3]

    def finish(self, after):
        names, sums = [], []
        for tag, group, started in self.started:
            parts, stacks = _scatter_finish("scatter_" + tag, started, after)
            sums += [_sum_chips(p, s, self.place, "chip_sum_" + nm) for p, s, nm in zip(parts, stacks, group)]
            names += group
        return dict(zip(names, _share_halves(sums)))


def kernel(x, ev_norm_w, ev_w_in, ev_conv_w, ev_w_out, od_norm_w, od_w_in, od_q_norm_w, od_k_norm_w, od_sinks, od_w_out, loss_target, m_ev_norm_w, m_ev_w_in, m_ev_conv_w, m_ev_w_out, m_od_norm_w, m_od_w_in, m_od_q_norm_w, m_od_k_norm_w, m_od_sinks, m_od_w_out, v_ev_norm_w, v_ev_w_in, v_ev_conv_w, v_ev_w_out, v_od_norm_w, v_od_w_in, v_od_q_norm_w, v_od_k_norm_w, v_od_sinks, v_od_w_out):
    my_chip = 2 * lax.axis_index("x") + lax.axis_index("y")
    place = jnp.stack([lax.axis_index("c"), my_chip]).astype(jnp.int32)
    shard_w = D_MODEL // N_CHIPS
    conv_shard = CONV_WIDTH // N_CHIPS

    small_in = jnp.zeros((8, shard_w), F32)
    small_in = small_in.at[0].set(od_norm_w[0]).at[1:4, :conv_shard].set(ev_conv_w[0])
    small_in = lax.dynamic_update_slice(jnp.zeros((N_CHIPS, 8, shard_w), F32), small_in[None], (my_chip, 0, 0))
    chip = place[1:]
    first_kinds, late_kinds = ("col", "whole"), ("row", "col", "row")
    w_in0_own_place, w_in0_shard = _cast_into_gathered(ev_w_in[0], "col", chip, "cast_w_in0", keep_shard=True)
    first = _gather_start("gather_first", [w_in0_own_place, small_in], first_kinds)
    late_own = [_cast_into_gathered(ev_w_out[0], "row", chip, "cast_w_out0"),
                _cast_into_gathered(od_w_in[0], "col", chip, "cast_w_in1"),
                _cast_into_gathered(od_w_out[0], "row", chip, "cast_w_out1")]
    late = []

    def weights_first(after):
        w_in0, small_all = _gather_finish("gather_first", first, first_kinds, after)
        late.append(_gather_start("gather_late", late_own, late_kinds, after=w_in0))
        od_norm_full = small_all[:, 0, :].reshape(1, D_MODEL)
        conv_full = jnp.transpose(small_all[:, 1:4, :conv_shard], (1, 0, 2)).reshape(3, CONV_WIDTH)
        return w_in0, conv_full, od_norm_full, late[0][3]

    def weights_late(after):
        return _gather_finish("gather_late", late[0], late_kinds, after)

    reduce_scatter = _ReduceScatter(place)
    sq, grad_x, small = _local_step(x[0], loss_target[0], ev_norm_w, od_q_norm_w, od_k_norm_w, od_sinks,
                                    (w_in0_shard, chip, first[3]), weights_first, weights_late, reduce_scatter.send)

    big = reduce_scatter.finish(grad_x)
    g_ev_w_in, g_ev_w_out, g_od_w_in, g_od_w_out = big["ev_w_in"], big["ev_w_out"], big["od_w_in"], big["od_w_out"]
    tot = _allreduce_small(_pack_small_grads(small, sq))
    loss = 0.5 * tot[6, 0] / D_MODEL
    g_ev_norm = tot[0:1]
    g_od_norm = lax.dynamic_slice(tot, (1, my_chip * shard_w), (1, shard_w))
    g_conv = lax.dynamic_slice(tot, (2, my_chip * conv_shard), (3, conv_shard))
    g_qw, g_kw, g_sinks = tot[5:6, 0:HEAD_DIM], tot[5:6, HEAD_DIM:2 * HEAD_DIM], tot[5:6, 2 * HEAD_DIM:2 * HEAD_DIM + Q_HEADS]

    upd = {}
    upd["ev_w_in"] = _adamw(ev_w_in[0], g_ev_w_in, m_ev_w_in[0], v_ev_w_in[0], "adamw_ev_w_in")
    upd["ev_w_out"] = _adamw(ev_w_out[0], g_ev_w_out, m_ev_w_out[0], v_ev_w_out[0], "adamw_ev_w_out")
    upd["od_w_in"] = _adamw(od_w_in[0], g_od_w_in, m_od_w_in[0], v_od_w_in[0], "adamw_od_w_in")
    upd["od_w_out"] = _adamw(od_w_out[0], g_od_w_out, m_od_w_out[0], v_od_w_out[0], "adamw_od_w_out")
    smalls = (("ev_norm_w", ev_norm_w, g_ev_norm, m_ev_norm_w, v_ev_norm_w),
              ("ev_conv_w", ev_conv_w, g_conv, m_ev_conv_w, v_ev_conv_w),
              ("od_norm_w", od_norm_w, g_od_norm, m_od_norm_w, v_od_norm_w),
              ("od_q_norm_w", od_q_norm_w, g_qw, m_od_q_norm_w, v_od_q_norm_w),
              ("od_k_norm_w", od_k_norm_w, g_kw, m_od_k_norm_w, v_od_k_norm_w),
              ("od_sinks", od_sinks, g_sinks, m_od_sinks, v_od_sinks))
    sizes = [w.size for _, w, _, _, _ in smalls]
    padded = 8 * 128 * math.ceil(sum(sizes) / (8 * 128))
    pack = lambda arrs, fill: jnp.concatenate(
        [a.reshape(-1) for a in arrs] + [jnp.full((padded - sum(sizes),), fill, F32)]).reshape(8, padded // 8)
    packed = _adamw(pack([w for _, w, _, _, _ in smalls], 0.0), pack([g for _, _, g, _, _ in smalls], 0.0),
                    pack([m for _, _, _, m, _ in smalls], 0.0), pack([v for _, _, _, _, v in smalls], 1.0), "adamw_small")
    offs = [sum(sizes[:i]) for i in range(len(sizes))]
    for (nm, w, _, _, _), off, size in zip(smalls, offs, sizes):
        upd[nm] = tuple(p.reshape(-1)[off:off + size].reshape(w.shape) for p in packed)
    for nm in ("ev_w_in", "ev_w_out", "od_w_in", "od_w_out"):
        upd[nm] = tuple(u[None] for u in upd[nm])
    order = ("ev_norm_w", "ev_w_in", "ev_conv_w", "ev_w_out", "od_norm_w", "od_w_in", "od_q_norm_w", "od_k_norm_w", "od_sinks", "od_w_out")
    return (loss, grad_x[None], *[upd[nm][0] for nm in order], *[upd[nm][1] for nm in order],
            *[upd[nm][2] for nm in order], *[upd[nm][3] for nm in order])
```

```python
import math

import jax
import jax.numpy as jnp
import numpy as np
from jax import lax
from jax.experimental import pallas as pl
from jax.experimental.pallas import tpu as pltpu

F32 = jnp.float32
BF16 = jnp.bfloat16

D_MODEL = 2048
RET_HEADS = 4
RET_DIM = 256
RET_WIDTH = 1024
CONV_WIDTH = 1024
EVEN_IN = 8192
Q_HEADS = 32
HEAD_DIM = 64
KV_HEADS = 4
KV_WIDTH = 256
ATTN_WIDTH = 2048
ODD_IN = 4608
BLK = 128
ROPE_THETA = 10000.0
EPS = 1e-6
ADAM_LR = 0.001
ADAM_B1 = 0.9
ADAM_B2 = 0.999
ADAM_EPS = 1e-08
ADAM_WD = 0.01
ADAM_STEP = 10
N_CHIPS = 4
N_DEV = 8
VMEM_LIMIT_BYTES = 56 * 1024 * 1024
MESH = pl.DeviceIdType.MESH
ANY = pl.BlockSpec(memory_space=pl.ANY)


def _params(*sem):
    return pltpu.CompilerParams(dimension_semantics=sem, vmem_limit_bytes=VMEM_LIMIT_BYTES)


def _dot(a, b):
    return jnp.dot(a, b, preferred_element_type=F32)


def _dot_nt(a, b):
    return lax.dot_general(a, b, (((1,), (1,)), ((), ())), preferred_element_type=F32)


def _dot_tn(a, b):
    return lax.dot_general(a, b, (((0,), (0,)), ((), ())), preferred_element_type=F32)


def _sigmoid(x):
    return 1.0 / (1.0 + jnp.exp(-x))


def _mm(a, b, *, mode, tm, tn, tk, out_dtype, name, add=None, after=None):
    if mode == "nn":
        (m, k), n = a.shape, b.shape[1]
    elif mode == "nt":
        (m, k), n = a.shape, b.shape[0]
    else:
        (k, m), n = a.shape, b.shape[1]
    tm, tn, tk = min(tm, m), min(tn, n), min(tk, k)
    assert m % tm == 0 and n % tn == 0 and k % tk == 0, (name, m, n, k)
    nk = k // tk
    dot = {"nn": _dot, "nt": _dot_nt, "tn": _dot_tn}[mode]
    a_spec = (pl.BlockSpec((tk, tm), lambda i, j, kk: (kk, i)) if mode == "tn"
              else pl.BlockSpec((tm, tk), lambda i, j, kk: (i, kk)))
    b_spec = (pl.BlockSpec((tn, tk), lambda i, j, kk: (j, kk)) if mode == "nt"
              else pl.BlockSpec((tk, tn), lambda i, j, kk: (kk, j)))
    o_spec = pl.BlockSpec((tm, tn), lambda i, j, kk: (i, j))
    has_add = add is not None

    def body(*refs):
        a_ref, b_ref = refs[0], refs[1]
        add_ref = refs[2] if has_add else None
        o_ref, acc_ref = refs[-2], refs[-1]
        p = dot(a_ref[...], b_ref[...])

        def finish(total):
            if has_add:
                total = total + add_ref[...].astype(F32)
            o_ref[...] = total.astype(out_dtype)

        if nk == 1:
            finish(p)
        else:
            kk = pl.program_id(2)

            @pl.when(kk == 0)
            def _():
                acc_ref[...] = p

            @pl.when(jnp.logical_and(kk > 0, kk < nk - 1))
            def _():
                acc_ref[...] += p

            @pl.when(kk == nk - 1)
            def _():
                finish(acc_ref[...] + p)

    in_specs = [a_spec, b_spec] + ([o_spec] if has_add else []) + ([ANY] if after is not None else [])
    args = (a, b) + ((add,) if has_add else ()) + ((after,) if after is not None else ())
    return pl.pallas_call(
        body, name=name, grid=(m // tm, n // tn, nk), in_specs=in_specs, out_specs=o_spec,
        out_shape=jax.ShapeDtypeStruct((m, n), out_dtype),
        scratch_shapes=[pltpu.VMEM((tm, tn) if nk > 1 else (8, 128), F32)],
        compiler_params=_params("parallel", "parallel", "arbitrary"),
    )(*args)


def _mm_shifted(a, b, shift, *, b_shifted, first, count, total, tm, tn, out_dtype, name, into=None, after=None):
    m, k = a.shape
    tm = min(tm, m)
    assert m % tm == 0
    col = lambda j, shift_ref: (shift_ref[0] + first + j) % total
    extra = [arr for arr in (into, after) if arr is not None]

    def body(shift_ref, a_ref, b_ref, *rest):
        del shift_ref
        rest[-1][...] = _dot(a_ref[...], b_ref[...]).astype(out_dtype)

    return pl.pallas_call(
        body, name=name, out_shape=jax.ShapeDtypeStruct((m, total * tn), out_dtype),
        grid_spec=pltpu.PrefetchScalarGridSpec(
            num_scalar_prefetch=1, grid=(m // tm, count),
            in_specs=[pl.BlockSpec((tm, k), lambda i, j, s: (i, 0)),
                      pl.BlockSpec((k, tn), (lambda i, j, s: (0, col(j, s))) if b_shifted else (lambda i, j, s: (0, j)))]
            + [ANY] * len(extra),
            out_specs=pl.BlockSpec((tm, tn), lambda i, j, s: (i, col(j, s)))),
        input_output_aliases={3: 0} if into is not None else {},
        compiler_params=_params("parallel", "arbitrary"))(shift, a, b, *extra)


def _mm_rows(a, b, rows_in, vecs_in, out_shapes, epilogue, *, tm, name):
    m, k = a.shape
    n = b.shape[1]
    assert m % tm == 0
    row = pl.BlockSpec((tm, n), lambda i: (i, 0))

    def body(a_ref, b_ref, *rest):
        epilogue(_dot(a_ref[...], b_ref[...]), *rest)

    out_specs = [row if tuple(s.shape) == (m, n) else pl.BlockSpec(s.shape, lambda i: (0, 0)) for s in out_shapes]
    return pl.pallas_call(
        body, name=name, grid=(m // tm,),
        in_specs=[pl.BlockSpec((tm, k), lambda i: (i, 0)), pl.BlockSpec((k, n), lambda i: (0, 0))] + [row] * len(rows_in)
        + [pl.BlockSpec((1, n), lambda i: (0, 0))] * len(vecs_in),
        out_specs=out_specs, out_shape=out_shapes, compiler_params=_params("arbitrary"),
    )(a, b, *rows_in, *vecs_in)


def _cast_into_gathered(w, kind, chip, name, keep_shard=False):
    r, c = w.shape
    tr = min(r, 512)
    per = r // tr

    def body(chip_ref, w_ref, *outs):
        del chip_ref
        for o_ref in outs:
            o_ref[...] = w_ref[...].astype(BF16)

    if kind == "col":
        shape, out_map = (r, N_CHIPS * c), (lambda i, chip_ref: (i, chip_ref[0]))
    else:
        shape, out_map = (N_CHIPS * r, c), (lambda i, chip_ref: (chip_ref[0] * per + i, 0))
    plain = pl.BlockSpec((tr, c), lambda i, chip_ref: (i, 0))
    out = pl.pallas_call(
        body, name=name,
        out_shape=[jax.ShapeDtypeStruct(shape, BF16)] + ([jax.ShapeDtypeStruct((r, c), BF16)] if keep_shard else []),
        grid_spec=pltpu.PrefetchScalarGridSpec(
            num_scalar_prefetch=1, grid=(per,), in_specs=[plain],
            out_specs=[pl.BlockSpec((tr, c), out_map)] + ([plain] if keep_shard else [])),
        compiler_params=_params("parallel"))(chip, w)
    return out if keep_shard else out[0]


NORM_ROWS = 512


def _rmsnorm(x, w, name, after=None):
    s, d = x.shape
    tr = NORM_ROWS

    def body(x_ref, w_ref, *rest):
        xv = x_ref[...]
        rstd = lax.rsqrt(jnp.mean(xv * xv, axis=-1, keepdims=True) + EPS)
        rest[-1][...] = (xv * rstd * w_ref[...]).astype(BF16)

    return pl.pallas_call(
        body, name=name, grid=(s // tr,),
        in_specs=[pl.BlockSpec((tr, d), lambda i: (i, 0)), pl.BlockSpec((1, d), lambda i: (0, 0))]
        + ([ANY] if after is not None else []),
        out_specs=pl.BlockSpec((tr, d), lambda i: (i, 0)),
        out_shape=jax.ShapeDtypeStruct((s, d), BF16), compiler_params=_params("parallel"),
    )(x, w, *((after,) if after is not None else ()))


def _rmsnorm_bwd(x, w, dh, dres, name, out_dtype):
    s, d = x.shape
    tr = NORM_ROWS

    def body(x_ref, w_ref, dh_ref, dres_ref, dx_ref, dw_ref):
        xv = x_ref[...]
        rstd = lax.rsqrt(jnp.mean(xv * xv, axis=-1, keepdims=True) + EPS)
        nrm = xv * rstd
        dhv = dh_ref[...].astype(F32)
        dn = dhv * w_ref[...]
        dx = dres_ref[...].astype(F32) + rstd * (dn - nrm * jnp.mean(dn * nrm, axis=-1, keepdims=True))
        dx_ref[...] = dx.astype(out_dtype)

        @pl.when(pl.program_id(0) == 0)
        def _():
            dw_ref[...] = jnp.zeros_like(dw_ref)

        dw_ref[...] += jnp.sum(dhv * nrm, axis=0, keepdims=True)

    row = pl.BlockSpec((tr, d), lambda i: (i, 0))
    return pl.pallas_call(
        body, name=name, grid=(s // tr,),
        in_specs=[row, pl.BlockSpec((1, d), lambda i: (0, 0)), row, row],
        out_specs=[row, pl.BlockSpec((8, d), lambda i: (0, 0))],
        out_shape=[jax.ShapeDtypeStruct((s, d), out_dtype), jax.ShapeDtypeStruct((8, d), F32)],
        compiler_params=_params("arbitrary"))(x, w, dh, dres)


def _adamw(w, g, m, v, name):
    r, c = w.shape
    tr = min(r, 256)
    assert r % tr == 0

    def body(w_ref, g_ref, m_ref, v_ref, g_out, d_ref, nm_ref, nv_ref):
        gv = g_ref[...]
        g_out[...] = gv
        nm = ADAM_B1 * m_ref[...] + (1.0 - ADAM_B1) * gv
        nv = ADAM_B2 * v_ref[...] + (1.0 - ADAM_B2) * (gv * gv)
        m_hat = nm / (1.0 - ADAM_B1 ** ADAM_STEP)
        v_hat = nv / (1.0 - ADAM_B2 ** ADAM_STEP)
        d_ref[...] = -ADAM_LR * (m_hat / (jnp.sqrt(v_hat) + ADAM_EPS) + ADAM_WD * w_ref[...])
        nm_ref[...] = nm
        nv_ref[...] = nv

    spec = pl.BlockSpec((tr, c), lambda i: (i, 0))
    shp = jax.ShapeDtypeStruct((r, c), F32)
    return pl.pallas_call(body, name=name, grid=(r // tr,), in_specs=[spec] * 4, out_specs=[spec] * 4,
                          out_shape=[shp] * 4, compiler_params=_params("parallel"))(w, g, m, v)


def _rope_tables(s, dim):
    inv = (1.0 / (ROPE_THETA ** (np.arange(0, dim, 2, dtype=np.float64) / dim))).astype(np.float32)
    ang = (np.arange(s, dtype=np.float32)[:, None] * inv[None, :]).astype(np.float64)
    return np.cos(ang).astype(np.float32), np.sin(ang).astype(np.float32)


def _rope_half(x, cos, sin):
    h = x.shape[1] // 2
    x1, x2 = x[:, :h], x[:, h:]
    return jnp.concatenate([x1 * cos - x2 * sin, x2 * cos + x1 * sin], axis=1)


def _unrope_half(dy, cos, sin):
    h = dy.shape[1] // 2
    d1, d2 = dy[:, :h], dy[:, h:]
    return jnp.concatenate([d1 * cos + d2 * sin, d2 * cos - d1 * sin], axis=1)


def _lane(shape):
    return lax.broadcasted_iota(jnp.int32, shape, 1)


def _partner64(x):
    w = x.shape[1]
    first = (_lane(x.shape) % HEAD_DIM) < (HEAD_DIM // 2)
    return jnp.where(first, pltpu.roll(x, w - HEAD_DIM // 2, 1), pltpu.roll(x, HEAD_DIM // 2, 1))


def _tile_lanes(t, reps):
    return t if reps == 1 else jnp.concatenate([t] * reps, axis=1)


def _group_mean(x, ones_bd, passes=2):
    hi = x.astype(BF16)
    if passes == 1:
        return _dot(hi, ones_bd)
    lo = (x - hi.astype(F32)).astype(BF16)
    return _dot(hi, ones_bd) + _dot(lo, ones_bd)


def _block_diag_mean(width):
    idx = jnp.arange(width) // HEAD_DIM
    return jnp.where(idx[:, None] == idx[None, :], 1.0 / HEAD_DIM, 0.0).astype(BF16)


RET_STEP = 4


def _retention_tables():
    h = RET_HEADS
    log_g = jnp.log(1.0 - 2.0 ** (-5.0 - jnp.arange(h, dtype=F32)))
    idx = jnp.arange(BLK, dtype=F32)
    diff = idx[:, None] - idx[None, :]
    intra = jnp.where(diff >= 0, jnp.exp(log_g[:, None, None] * jnp.maximum(diff, 0.0)), 0.0).astype(F32)
    q_dec = jnp.exp(log_g[:, None] * (idx[None, :] + 1.0)).astype(F32)[:, :, None]
    k_dec = jnp.exp(log_g[:, None] * (BLK - 1.0 - idx[None, :])).astype(F32)[:, :, None]
    chunk_dec = jnp.exp(log_g * BLK).astype(F32)[:, None, None]
    return intra, q_dec, k_dec, chunk_dec


def _retention_fwd(proj, cos, sin, tables):
    s = proj.shape[0]
    nc = s // BLK
    intra, q_dec, k_dec, chunk_dec = tables

    def body(p_ref, cos_ref, sin_ref, in_ref, qd_ref, kd_ref, cd_ref, o_ref, cat_ref, st_ref, state):
        @pl.when(pl.program_id(0) == 0)
        def _():
            state[...] = jnp.zeros_like(state)

        for c in range(RET_STEP):
            rows = slice(c * BLK, (c + 1) * BLK)
            cosv, sinv = cos_ref[rows, :], sin_ref[rows, :]
            for h in range(RET_HEADS):
                c0 = h * RET_DIM
                q = p_ref[rows, c0:c0 + RET_DIM].astype(F32)
                k = p_ref[rows, RET_WIDTH + c0:RET_WIDTH + c0 + RET_DIM].astype(F32)
                v = p_ref[rows, 2 * RET_WIDTH + c0:2 * RET_WIDTH + c0 + RET_DIM]
                g = p_ref[rows, 3 * RET_WIDTH + c0:3 * RET_WIDTH + c0 + RET_DIM].astype(F32)
                qb = _rope_half(q, cosv, sinv).astype(BF16)
                kr = _rope_half(k, cosv, sinv) * (RET_DIM ** -0.5)
                kb = kr.astype(BF16)
                scores = _dot_nt(qb, kb) * in_ref[h]
                inner = _dot(scores.astype(BF16), v)
                prev = state[h]
                prev_b = prev.astype(BF16)
                st_ref[h, c] = prev_b
                o = inner + _dot(qb, prev_b) * qd_ref[h]
                o_ref[rows, c0:c0 + RET_DIM] = o
                rstd = lax.rsqrt(jnp.mean(o * o, axis=-1, keepdims=True) + EPS)
                cat_ref[rows, c0:c0 + RET_DIM] = (o * rstd * (g * _sigmoid(g))).astype(BF16)
                state[h] = cd_ref[h] * prev + _dot_tn((kr * kd_ref[h]).astype(BF16), v)

    full = lambda shape: pl.BlockSpec(shape, lambda n: (0,) * len(shape))
    step = RET_STEP * BLK
    return pl.pallas_call(
        body, name="retention_fwd", grid=(nc // RET_STEP,),
        in_specs=[pl.BlockSpec((step, 4 * RET_WIDTH), lambda n: (n, 0)),
                  pl.BlockSpec((step, RET_DIM // 2), lambda n: (n, 0)), pl.BlockSpec((step, RET_DIM // 2), lambda n: (n, 0)),
                  full((RET_HEADS, BLK, BLK)), full((RET_HEADS, BLK, 1)), full((RET_HEADS, BLK, 1)), full((RET_HEADS, 1, 1))],
        out_specs=[pl.BlockSpec((step, RET_WIDTH), lambda n: (n, 0)), pl.BlockSpec((step, RET_WIDTH), lambda n: (n, 0)),
                   pl.BlockSpec((RET_HEADS, RET_STEP, RET_DIM, RET_DIM), lambda n: (0, n, 0, 0))],
        out_shape=[jax.ShapeDtypeStruct((s, RET_WIDTH), F32), jax.ShapeDtypeStruct((s, D_MODEL), BF16),
                   jax.ShapeDtypeStruct((RET_HEADS, nc, RET_DIM, RET_DIM), BF16)],
        scratch_shapes=[pltpu.VMEM((RET_HEADS, RET_DIM, RET_DIM), F32)],
        compiler_params=_params("arbitrary"))(proj, cos, sin, intra, q_dec, k_dec, chunk_dec)


def _retention_bwd(proj, o, states, dcat, cos, sin, tables):
    s = proj.shape[0]
    nc = s // BLK
    intra, q_dec, k_dec, chunk_dec = tables

    def body(p_ref, o_ref, st_ref, dc_ref, cos_ref, sin_ref, in_ref, qd_ref, kd_ref, cd_ref, dp_ref, dstate):
        @pl.when(pl.program_id(0) == 0)
        def _():
            dstate[...] = jnp.zeros_like(dstate)

        for c in reversed(range(RET_STEP)):
            rows = slice(c * BLK, (c + 1) * BLK)
            cosv, sinv = cos_ref[rows, :], sin_ref[rows, :]
            for h in range(RET_HEADS):
                c0 = h * RET_DIM
                q = p_ref[rows, c0:c0 + RET_DIM].astype(F32)
                k = p_ref[rows, RET_WIDTH + c0:RET_WIDTH + c0 + RET_DIM].astype(F32)
                v = p_ref[rows, 2 * RET_WIDTH + c0:2 * RET_WIDTH + c0 + RET_DIM]
                g = p_ref[rows, 3 * RET_WIDTH + c0:3 * RET_WIDTH + c0 + RET_DIM].astype(F32)
                o = o_ref[rows, c0:c0 + RET_DIM]
                dc = dc_ref[rows, c0:c0 + RET_DIM].astype(F32)
                rstd = lax.rsqrt(jnp.mean(o * o, axis=-1, keepdims=True) + EPS)
                nrm = o * rstd
                sg = _sigmoid(g)
                dg = dc * nrm * (sg * (1.0 + g * (1.0 - sg)))
                dn = dc * (g * sg)
                do = rstd * (dn - nrm * jnp.mean(dn * nrm, axis=-1, keepdims=True))
                qb = _rope_half(q, cosv, sinv).astype(BF16)
                kr = _rope_half(k, cosv, sinv) * (RET_DIM ** -0.5)
                kb = kr.astype(BF16)
                mask = in_ref[h]
                qd, kd = qd_ref[h], kd_ref[h]
                prev_b = st_ref[h, c]
                dnext = dstate[h]
                dnext_b = dnext.astype(BF16)
                att = (_dot_nt(qb, kb) * mask).astype(BF16)
                do_b = do.astype(BF16)
                doq = (do * qd).astype(BF16)
                dv = _dot_tn(att, do_b) + _dot((kr * kd).astype(BF16), dnext_b)
                ds = (_dot_nt(do_b, v) * mask).astype(BF16)
                dqr = _dot(ds, kb) + _dot_nt(doq, prev_b)
                dkr = _dot_tn(ds, qb) + _dot_nt(v, dnext_b) * kd
                dstate[h] = cd_ref[h] * dnext + _dot_tn(qb, doq)
                dq = _unrope_half(dqr, cosv, sinv)
                dk = _unrope_half(dkr * (RET_DIM ** -0.5), cosv, sinv)
                dp_ref[rows, c0:c0 + RET_DIM] = dq.astype(BF16)
                dp_ref[rows, RET_WIDTH + c0:RET_WIDTH + c0 + RET_DIM] = dk.astype(BF16)
                dp_ref[rows, 2 * RET_WIDTH + c0:2 * RET_WIDTH + c0 + RET_DIM] = dv.astype(BF16)
                dp_ref[rows, 3 * RET_WIDTH + c0:3 * RET_WIDTH + c0 + RET_DIM] = dg.astype(BF16)

    steps = nc // RET_STEP
    rev = lambda n: steps - 1 - n
    full = lambda shape: pl.BlockSpec(shape, lambda n: (0,) * len(shape))
    step = RET_STEP * BLK
    return pl.pallas_call(
        body, name="retention_bwd", grid=(steps,),
        in_specs=[pl.BlockSpec((step, 4 * RET_WIDTH), lambda n: (rev(n), 0)),
                  pl.BlockSpec((step, RET_WIDTH), lambda n: (rev(n), 0)),
                  pl.BlockSpec((RET_HEADS, RET_STEP, RET_DIM, RET_DIM), lambda n: (0, rev(n), 0, 0)),
                  pl.BlockSpec((step, RET_WIDTH), lambda n: (rev(n), 0)),
                  pl.BlockSpec((step, RET_DIM // 2), lambda n: (rev(n), 0)), pl.BlockSpec((step, RET_DIM // 2), lambda n: (rev(n), 0)),
                  full((RET_HEADS, BLK, BLK)), full((RET_HEADS, BLK, 1)), full((RET_HEADS, BLK, 1)), full((RET_HEADS, 1, 1))],
        out_specs=pl.BlockSpec((step, 4 * RET_WIDTH), lambda n: (rev(n), 0)),
        out_shape=jax.ShapeDtypeStruct((s, EVEN_IN), BF16),
        scratch_shapes=[pltpu.VMEM((RET_HEADS, RET_DIM, RET_DIM), F32)],
        compiler_params=_params("arbitrary"))(proj, o, states, dcat, cos, sin, intra, q_dec, k_dec, chunk_dec)


CONV_ROWS = 256
HALO = 16


def _conv_pieces(p, halo, first):
    gb, gc, u, gv = (p[:, i * CONV_WIDTH:(i + 1) * CONV_WIDTH] for i in range(4))
    cu = gc * u
    hcu = halo[:, CONV_WIDTH:2 * CONV_WIDTH] * halo[:, 2 * CONV_WIDTH:3 * CONV_WIDTH]
    hcu = jnp.where(first, 0.0, hcu)
    r1, r2 = hcu[HALO - 1:HALO], hcu[HALO - 2:HALO - 1]
    row = lax.broadcasted_iota(jnp.int32, cu.shape, 0)
    m1 = jnp.where(row == 0, r1, pltpu.roll(cu, 1, 0))
    m2 = jnp.where(row == 0, r2, jnp.where(row == 1, r1, pltpu.roll(cu, 2, 0)))
    return gb, gc, u, gv, cu, m1, m2


def _conv_fwd(proj, conv_w, cat):
    s = proj.shape[0]
    per = CONV_ROWS // HALO

    def body(p_ref, halo_ref, w_ref, cat_in, cat_ref):
        del cat_in
        first = pl.program_id(0) == 0
        gb, _, _, gv, cu, m1, m2 = _conv_pieces(p_ref[...].astype(F32), halo_ref[...].astype(F32), first)
        conv = w_ref[0:1, :] * m2 + w_ref[1:2, :] * m1 + w_ref[2:3, :] * cu
        cat_ref[...] = (gb * conv * (gv * _sigmoid(gv))).astype(BF16)

    return pl.pallas_call(
        body, name="conv_fwd", grid=(s // CONV_ROWS,),
        in_specs=[pl.BlockSpec((CONV_ROWS, 4 * CONV_WIDTH), lambda i: (i, 1)),
                  pl.BlockSpec((HALO, 4 * CONV_WIDTH), lambda i: (jnp.maximum(i * per - 1, 0), 1)),
                  pl.BlockSpec((3, CONV_WIDTH), lambda i: (0, 0)), ANY],
        out_specs=pl.BlockSpec((CONV_ROWS, CONV_WIDTH), lambda i: (i, 1)),
        out_shape=jax.ShapeDtypeStruct(cat.shape, cat.dtype), input_output_aliases={3: 0},
        compiler_params=_params("parallel"))(proj, proj, conv_w, cat)


def _conv_bwd(proj, dcat, conv_w, dproj):
    s = proj.shape[0]
    per = CONV_ROWS // HALO
    last_halo = s // HALO - 1
    nsteps = s // CONV_ROWS

    def body(p_ref, halo_ref, nxt_ref, dc_ref, dnxt_ref, w_ref, dp_in, dp_ref, dw_ref):
        del dp_in
        i = pl.program_id(0)
        gb, gc, u, gv, cu, m1, m2 = _conv_pieces(p_ref[...].astype(F32), halo_ref[...].astype(F32), i == 0)
        w0, w1, w2 = w_ref[0:1, :], w_ref[1:2, :], w_ref[2:3, :]
        conv = w0 * m2 + w1 * m1 + w2 * cu
        dco = dc_ref[...].astype(F32)
        sg = _sigmoid(gv)
        silu = gv * sg
        dgb = dco * conv * silu
        dco_gb = dco * gb
        dgv = dco_gb * conv * (sg * (1.0 + gv * (1.0 - sg)))
        dconv = dco_gb * silu
        nxt = nxt_ref[...].astype(F32)
        ngv = nxt[:, 3 * CONV_WIDTH:]
        dnext = dnxt_ref[...].astype(F32) * nxt[:, :CONV_WIDTH] * (ngv * _sigmoid(ngv))
        dnext = jnp.where(i == nsteps - 1, 0.0, dnext)
        n1, n2 = dnext[0:1], dnext[1:2]
        row = lax.broadcasted_iota(jnp.int32, dconv.shape, 0)
        p1 = jnp.where(row == CONV_ROWS - 1, n1, pltpu.roll(dconv, CONV_ROWS - 1, 0))
        p2 = jnp.where(row == CONV_ROWS - 1, n2, jnp.where(row == CONV_ROWS - 2, n1, pltpu.roll(dconv, CONV_ROWS - 2, 0)))
        dcu = w2 * dconv + w1 * p1 + w0 * p2
        dp_ref[...] = jnp.concatenate([dgb, dcu * u, dcu * gc, dgv], axis=1).astype(BF16)

        @pl.when(i == 0)
        def _():
            dw_ref[...] = jnp.zeros_like(dw_ref)

        taps = [jnp.sum(dconv * m, axis=0, keepdims=True) for m in (m2, m1, cu)]
        r8 = lax.broadcasted_iota(jnp.int32, dw_ref.shape, 0)
        dw_ref[...] += jnp.where(r8 == 0, taps[0], jnp.where(r8 == 1, taps[1], jnp.where(r8 == 2, taps[2], 0.0)))

    return pl.pallas_call(
        body, name="conv_bwd", grid=(nsteps,),
        in_specs=[pl.BlockSpec((CONV_ROWS, 4 * CONV_WIDTH), lambda i: (i, 1)),
                  pl.BlockSpec((HALO, 4 * CONV_WIDTH), lambda i: (jnp.maximum(i * per - 1, 0), 1)),
                  pl.BlockSpec((HALO, 4 * CONV_WIDTH), lambda i: (jnp.minimum((i + 1) * per, last_halo), 1)),
                  pl.BlockSpec((CONV_ROWS, CONV_WIDTH), lambda i: (i, 1)),
                  pl.BlockSpec((HALO, CONV_WIDTH), lambda i: (jnp.minimum((i + 1) * per, last_halo), 1)),
                  pl.BlockSpec((3, CONV_WIDTH), lambda i: (0, 0)), ANY],
        out_specs=[pl.BlockSpec((CONV_ROWS, 4 * CONV_WIDTH), lambda i: (i, 1)), pl.BlockSpec((8, CONV_WIDTH), lambda i: (0, 0))],
        out_shape=[jax.ShapeDtypeStruct(dproj.shape, dproj.dtype), jax.ShapeDtypeStruct((8, CONV_WIDTH), F32)],
        input_output_aliases={6: 0},
        compiler_params=_params("arbitrary"))(proj, proj, proj, dcat, dcat, conv_w, dproj)


GROUP_HEADS = Q_HEADS // KV_HEADS
GROUP_WIDTH = GROUP_HEADS * HEAD_DIM
SLAB = 2 * HEAD_DIM
KV_COL = ATTN_WIDTH // GROUP_WIDTH
GATE_COL = KV_COL + 1
ATTN_SCALE = HEAD_DIM ** -0.5
KV_ROWS = 1024


def _half_mask(shape, which):
    return (_lane(shape) // HEAD_DIM) == which


def _dup_head(slab, which):
    kept = jnp.where(_half_mask(slab.shape, which), slab, 0.0)
    return kept + pltpu.roll(kept, HEAD_DIM, 1)


def _stack_heads(x):
    parts = []
    for sl in range(GROUP_WIDTH // SLAB):
        slab = x[:, sl * SLAB:(sl + 1) * SLAB]
        parts += [jnp.where(_half_mask(slab.shape, e), slab, 0.0) for e in range(2)]
    return jnp.concatenate(parts, axis=0)


def _unstack_heads(y):
    slabs = []
    for sl in range(GROUP_WIDTH // SLAB):
        a, b = y[(2 * sl) * BLK:(2 * sl + 1) * BLK], y[(2 * sl + 1) * BLK:(2 * sl + 2) * BLK]
        slabs.append(jnp.where(_half_mask(a.shape, 0), a, b))
    return jnp.concatenate(slabs, axis=1)


def _q_prep(q, qw, cosf, sins, ones_bd):
    rstd = lax.rsqrt(_group_mean(q * q, ones_bd) + EPS)
    nrm = q * rstd
    y = nrm * qw
    return nrm, rstd, y * cosf + _partner64(y) * sins


def _band(tri_ref, n):
    own = tri_ref[...] > 0.5
    return own, jnp.where(jnp.logical_and(n == 0, jnp.logical_not(own)), -1e30, 0.0)


def _fold(pair, own):
    return jnp.where(own, pair[:, BLK:], pair[:, :BLK])


def _unfold(folded, own):
    return jnp.concatenate([jnp.where(own, 0.0, folded), jnp.where(own, folded, 0.0)], axis=1)


def _head_probs(raw_scores, sink, own, bias):
    sc = _fold(raw_scores, own) + bias
    m = jnp.maximum(jnp.max(sc, axis=-1, keepdims=True), sink)
    p = jnp.exp(sc - m)
    psink = jnp.exp(sink - m)
    inv = 1.0 / (jnp.sum(p, axis=-1, keepdims=True) + psink)
    return p * inv, psink * inv


def _k_prep(k, kw, cosf, sins, ones_bd):
    rstd = lax.rsqrt(_group_mean(k * k, ones_bd) + EPS)
    nrm = k * rstd
    y = nrm * kw
    return nrm, rstd, y * cosf + _partner64(y) * sins


def _qk_prep(proj, qw, kw, cos, sins, ones_q, ones_kv):
    s = proj.shape[0]
    rows = min(KV_ROWS, s)

    def body(p_ref, qw_ref, kw_ref, cos_ref, sin_ref, oq_ref, ok_ref, o_ref):
        j = pl.program_id(1)

        @pl.when(j < KV_COL)
        def _():
            cosf, sinf = _tile_lanes(cos_ref[...], 4), _tile_lanes(sin_ref[...], 4)
            roped = _q_prep(p_ref[...].astype(F32), qw_ref[...], cosf, sinf, oq_ref[...])[2]
            o_ref[...] = (roped * ATTN_SCALE).astype(BF16)

        @pl.when(j == KV_COL)
        def _():
            cosf, sinf = _tile_lanes(cos_ref[...], 2), _tile_lanes(sin_ref[...], 2)
            kr = _k_prep(p_ref[:, :KV_WIDTH].astype(F32), kw_ref[...], cosf, sinf, ok_ref[...])[2]
            o_ref[...] = jnp.concatenate([kr.astype(BF16), p_ref[:, KV_WIDTH:]], axis=1)

    full = lambda shape: pl.BlockSpec(shape, lambda i, j: (0,) * len(shape))
    tab = pl.BlockSpec((rows, SLAB), lambda i, j: (i, 0))
    blk = pl.BlockSpec((rows, GROUP_WIDTH), lambda i, j: (i, j))
    return pl.pallas_call(
        body, name="qk_prep", grid=(s // rows, KV_COL + 1),
        in_specs=[blk, full((1, GROUP_WIDTH)), full((1, KV_WIDTH)), tab, tab, full((GROUP_WIDTH, GROUP_WIDTH)),
                  full((KV_WIDTH, KV_WIDTH))],
        out_specs=blk, out_shape=jax.ShapeDtypeStruct((s, ATTN_WIDTH + 2 * KV_WIDTH), BF16),
        compiler_params=_params("parallel", "arbitrary"))(proj, qw, kw, cos, sins, ones_q, ones_kv)


def _keys_values(kc_ref, kp_ref, vc_ref, vp_ref, head):
    lanes = slice((head // 2) * SLAB, (head // 2 + 1) * SLAB)
    dup = lambda ref: _dup_head(ref[:, lanes].astype(F32), head % 2)
    return (jnp.concatenate([dup(kp_ref), dup(kc_ref)], axis=0).astype(BF16),
            jnp.concatenate([dup(vp_ref), dup(vc_ref)], axis=0).astype(BF16))


FWD_STEP_HEADS = 4
BWD_STEP_HEADS = 2


def _swa_specs(heads):
    kv_width = heads * HEAD_DIM
    prev = lambda n: jnp.maximum(n - 1, 0)
    kv = lambda col0, row: pl.BlockSpec((BLK, kv_width), lambda gs, n: (row(n), col0 // kv_width + gs))
    cur = lambda n: n
    full = lambda shape: pl.BlockSpec(shape, lambda gs, n: (0,) * len(shape))
    gate = lambda t: pl.BlockSpec((BLK, GROUP_WIDTH), lambda gs, n: (n, GATE_COL + heads * gs + t))
    return dict(
        sinks=pl.BlockSpec(memory_space=pltpu.SMEM), gates=[gate(t) for t in range(heads)],
        kc=kv(ATTN_WIDTH, cur), kp=kv(ATTN_WIDTH, prev), vc=kv(ATTN_WIDTH + KV_WIDTH, cur), vp=kv(ATTN_WIDTH + KV_WIDTH, prev),
        tri=full((BLK, BLK)), step=pl.BlockSpec((BLK, heads * GROUP_WIDTH), lambda gs, n: (n, gs)))


def _lower_triangle():
    return jnp.tril(jnp.ones((BLK, BLK), F32))


def _swa_fwd(qk, proj, sinks):
    s = proj.shape[0]
    nb = s // BLK
    heads = FWD_STEP_HEADS
    sp = _swa_specs(heads)

    def body(sink_ref, q_ref, kc_ref, kp_ref, vc_ref, vp_ref, *rest):
        gate_refs, (tri_ref, ag_ref, o_ref) = rest[:heads], rest[heads:]
        gs, n = pl.program_id(0), pl.program_id(1)
        own, bias = _band(tri_ref, n)
        for t, gate_ref in enumerate(gate_refs):
            cols = slice(t * GROUP_WIDTH, (t + 1) * GROUP_WIDTH)
            first_head = (heads * gs + t) * GROUP_HEADS
            kcat, vcat = _keys_values(kc_ref, kp_ref, vc_ref, vp_ref, t)
            scores = _dot_nt(_stack_heads(q_ref[:, cols]), kcat)
            probs = []
            for j in range(GROUP_HEADS):
                p, _ = _head_probs(scores[j * BLK:(j + 1) * BLK], sink_ref[first_head + j], own, bias)
                probs.append(_unfold(p, own).astype(BF16))
            o = _unstack_heads(_dot(jnp.concatenate(probs, axis=0), vcat))
            gate = gate_ref[...].astype(F32)
            o_ref[:, cols] = o.astype(BF16)
            ag_ref[:, cols] = (o * (gate * _sigmoid(gate))).astype(BF16)

    shp = jax.ShapeDtypeStruct((s, ATTN_WIDTH), BF16)
    return pl.pallas_call(
        body, name="swa_fwd", grid=(KV_HEADS // heads, nb),
        in_specs=[sp["sinks"], sp["step"], sp["kc"], sp["kp"], sp["vc"], sp["vp"], *sp["gates"], sp["tri"]],
        out_specs=[sp["step"], sp["step"]], out_shape=[shp, shp],
        compiler_params=_params("parallel", "arbitrary"),
    )(sinks, qk, qk, qk, qk, qk, *[proj] * heads, _lower_triangle())


def _swa_bwd(qk, proj, dag, sinks):
    s = proj.shape[0]
    nb = s // BLK
    heads = BWD_STEP_HEADS
    sp = _swa_specs(heads)

    def body(sink_ref, q_ref, kc_ref, kp_ref, vc_ref, vp_ref, *rest):
        gate_refs = rest[:heads]
        dag_ref, tri_ref, dq_ref, dkc_ref, dkp_ref, dvc_ref, dvp_ref, dsink_ref = rest[heads:]
        gs, n = pl.program_id(0), pl.program_id(1)
        own, bias = _band(tri_ref, n)

        @pl.when(n == 0)
        def _():
            dsink_ref[...] = jnp.zeros_like(dsink_ref)

        for t, gate_ref in enumerate(gate_refs):
            cols = slice(t * GROUP_WIDTH, (t + 1) * GROUP_WIDTH)
            first_head = (heads * gs + t) * GROUP_HEADS
            kcat, vcat = _keys_values(kc_ref, kp_ref, vc_ref, vp_ref, t)
            gate = gate_ref[...].astype(F32)
            do = dag_ref[:, cols].astype(F32) * (gate * _sigmoid(gate))
            q_stack = _stack_heads(q_ref[:, cols])
            do_stack = _stack_heads(do).astype(BF16)
            scores = _dot_nt(q_stack, kcat)
            dprobs = _dot_nt(do_stack, vcat)
            probs, dscores, dsinks = [], [], []
            for j in range(GROUP_HEADS):
                rows = slice(j * BLK, (j + 1) * BLK)
                p, psink = _head_probs(scores[rows], sink_ref[first_head + j], own, bias)
                dp = _fold(dprobs[rows], own)
                delta = jnp.sum(p * dp, axis=-1, keepdims=True)
                probs.append(_unfold(p, own).astype(BF16))
                dscores.append(_unfold(p * (dp - delta), own).astype(BF16))
                dsinks.append(-jnp.sum(psink * delta, axis=0, keepdims=True))
            ds = jnp.concatenate(dscores, axis=0)
            dk = _dot_tn(ds, q_stack)
            dv = _dot_tn(jnp.concatenate(probs, axis=0), do_stack)
            dk = dk + pltpu.roll(dk, HEAD_DIM, 1)
            dv = dv + pltpu.roll(dv, HEAD_DIM, 1)
            dkp_ref[t], dkc_ref[t] = dk[:BLK], dk[BLK:]
            dvp_ref[t], dvc_ref[t] = dv[:BLK], dv[BLK:]
            dq_ref[:, cols] = _unstack_heads(_dot(ds, kcat)).astype(BF16)
            r8 = lax.broadcasted_iota(jnp.int32, (8, SLAB), 0)
            upd = jnp.zeros((8, SLAB), F32)
            for j in range(GROUP_HEADS):
                upd = jnp.where(r8 == j, dsinks[j], upd)
            dsink_ref[t] += upd

    cur_out = pl.BlockSpec((heads, BLK, SLAB), lambda gs, n: (gs, n, 0))
    prev_out = pl.BlockSpec((heads, BLK, SLAB), lambda gs, n: (gs, (n + nb - 1) % nb, 0))
    kv_shape = jax.ShapeDtypeStruct((KV_HEADS, s, SLAB), F32)
    return pl.pallas_call(
        body, name="swa_bwd", grid=(KV_HEADS // heads, nb),
        in_specs=[sp["sinks"], sp["step"], sp["kc"], sp["kp"], sp["vc"], sp["vp"], *sp["gates"], sp["step"], sp["tri"]],
        out_specs=[sp["step"], cur_out, prev_out, cur_out, prev_out,
                   pl.BlockSpec((heads, 8, SLAB), lambda gs, n: (gs, 0, 0))],
        out_shape=[jax.ShapeDtypeStruct((s, ATTN_WIDTH), BF16), kv_shape, kv_shape, kv_shape, kv_shape,
                   jax.ShapeDtypeStruct((KV_HEADS, 8, SLAB), F32)],
        compiler_params=_params("parallel", "arbitrary"),
    )(sinks, qk, qk, qk, qk, qk, *[proj] * heads, dag, _lower_triangle())


def _swa_bwd_finish(proj, dqr, o, dag, dkc, dkp, dvc, dvp, qw, kw, cos, sins, ones_q, ones_kv):
    s = proj.shape[0]
    rows = min(KV_ROWS, s)
    n_q = KV_COL
    q_of = lambda j: jnp.clip(j - 1, 0, n_q - 1)
    gate_of = lambda j: jnp.clip(j - 1 - n_q, 0, n_q - 1)

    def body(kv_ref, p_ref, dqr_ref, o_ref, dag_ref, dkc_ref, dkp_ref, dvc_ref, dvp_ref, qw_ref, kw_ref, cos_ref, sin_ref,
             oq_ref, ok_ref, dp_ref, dqw_ref, dkw_ref):
        j, i = pl.program_id(0), pl.program_id(1)

        @pl.when(j == 0)
        def _():
            @pl.when(i == 0)
            def _():
                dkw_ref[...] = jnp.zeros_like(dkw_ref)

            def assemble(cur_ref, prv_ref):
                tot = [cur_ref[h] + prv_ref[h] for h in range(KV_HEADS)]
                first = _half_mask(tot[0].shape, 0)
                return jnp.concatenate([jnp.where(first, tot[0], tot[1]), jnp.where(first, tot[2], tot[3])], axis=1)

            dkr = assemble(dkc_ref, dkp_ref)
            dv = assemble(dvc_ref, dvp_ref)
            cosf, sinf = _tile_lanes(cos_ref[...], 2), _tile_lanes(sin_ref[...], 2)
            nrm, rstd, _ = _k_prep(kv_ref[:, :KV_WIDTH].astype(F32), kw_ref[...], cosf, sinf, ok_ref[...])
            dy = dkr * cosf + _partner64(dkr * sinf)
            dn = dy * kw_ref[...]
            dk = rstd * (dn - nrm * _group_mean(dn * nrm, ok_ref[...], passes=1))
            dp_ref[...] = jnp.concatenate([dk, dv], axis=1).astype(BF16)
            dkw_ref[...] += jnp.sum(dy * nrm, axis=0, keepdims=True)

        @pl.when(jnp.logical_and(j >= 1, j <= n_q))
        def _():
            @pl.when(i == 0)
            def _():
                dqw_ref[...] = jnp.zeros_like(dqw_ref)

            cosf, sinf = _tile_lanes(cos_ref[...], 4), _tile_lanes(sin_ref[...], 4)
            nrm, rstd, _ = _q_prep(p_ref[...].astype(F32), qw_ref[...], cosf, sinf, oq_ref[...])
            dq = dqr_ref[...].astype(F32) * ATTN_SCALE
            dy = dq * cosf + _partner64(dq * sinf)
            dn = dy * qw_ref[...]
            dp_ref[...] = (rstd * (dn - nrm * _group_mean(dn * nrm, oq_ref[...], passes=1))).astype(BF16)
            dqw_ref[0] += jnp.sum(dy * nrm, axis=0, keepdims=True)

        @pl.when(j > n_q)
        def _():
            gate = p_ref[...].astype(F32)
            sg = _sigmoid(gate)
            dp_ref[...] = (dag_ref[...].astype(F32) * o_ref[...].astype(F32) * (sg * (1.0 + gate * (1.0 - sg)))).astype(BF16)

    first_pass = lambda j, i: jnp.where(j == 0, i, 0)
    acc = pl.BlockSpec((KV_HEADS, rows, SLAB), lambda j, i: (0, first_pass(j, i), 0))
    full = lambda shape: pl.BlockSpec(shape, lambda j, i: (0,) * len(shape))
    tab = pl.BlockSpec((rows, SLAB), lambda j, i: (i, 0))
    out_col = lambda j: jnp.where(j == 0, KV_COL, jnp.where(j <= n_q, j - 1, j))
    return pl.pallas_call(
        body, name="swa_bwd_finish", grid=(2 * n_q + 1, s // rows),
        in_specs=[pl.BlockSpec((rows, GROUP_WIDTH), lambda j, i: (first_pass(j, i), KV_COL)),
                  pl.BlockSpec((rows, GROUP_WIDTH), lambda j, i: (jnp.where(j == 0, 0, i), jnp.where(j <= n_q, q_of(j), j))),
                  pl.BlockSpec((rows, GROUP_WIDTH), lambda j, i: (jnp.where(jnp.logical_and(j >= 1, j <= n_q), i, 0), q_of(j))),
                  pl.BlockSpec((rows, GROUP_WIDTH), lambda j, i: (jnp.where(j > n_q, i, 0), gate_of(j))),
                  pl.BlockSpec((rows, GROUP_WIDTH), lambda j, i: (jnp.where(j > n_q, i, 0), gate_of(j))),
                  acc, acc, acc, acc, full((1, GROUP_WIDTH)), full((1, KV_WIDTH)), tab, tab,
                  full((GROUP_WIDTH, GROUP_WIDTH)), full((KV_WIDTH, KV_WIDTH))],
        out_specs=[pl.BlockSpec((rows, GROUP_WIDTH), lambda j, i: (i, out_col(j))),
                   pl.BlockSpec((1, 8, GROUP_WIDTH), lambda j, i: (q_of(j), 0, 0)), pl.BlockSpec((8, KV_WIDTH), lambda j, i: (0, 0))],
        out_shape=[jax.ShapeDtypeStruct((s, ODD_IN), BF16), jax.ShapeDtypeStruct((n_q, 8, GROUP_WIDTH), F32),
                   jax.ShapeDtypeStruct((8, KV_WIDTH), F32)],
        compiler_params=_params("arbitrary", "arbitrary"),
    )(proj, proj, dqr, o, dag, dkc, dkp, dvc, dvp, qw, kw, cos, sins, ones_q, ones_kv)


def _place():
    x, y, c = lax.axis_index("x"), lax.axis_index("y"), lax.axis_index("c")
    return x, y, c


OTHER_CHIPS = ((1, 0), (0, 1), (1, 1))


def _half_rows(ref, half, rows):
    return ref.at[pl.ds(pl.multiple_of(half * (rows // 2), 8), rows // 2)]


DMA_CHUNK_BYTES = 1 << 20
BF16_TILE_ROWS = 16


def _n_chunks(ref):
    rows = ref.shape[-2]
    nbytes = math.prod(ref.shape) * jnp.dtype(ref.dtype).itemsize
    n = 1
    while 2 * n * DMA_CHUNK_BYTES <= nbytes and rows % (2 * n * BF16_TILE_ROWS) == 0:
        n *= 2
    return n


def _row_chunk(ref, k, n):
    rows = ref.shape[-2] // n
    return ref.at[pl.ds(k * rows, rows)] if len(ref.shape) == 2 else ref.at[:, pl.ds(k * rows, rows)]


def _push(src, dst, send_sem, recv_sem, device_id):
    n = _n_chunks(src)
    for k in range(n):
        pltpu.make_async_remote_copy(src_ref=_row_chunk(src, k, n), dst_ref=_row_chunk(dst, k, n), send_sem=send_sem,
                                     recv_sem=recv_sem, device_id=device_id, device_id_type=MESH).start()
    return pltpu.make_async_remote_copy(src_ref=src, dst_ref=dst, send_sem=send_sem, recv_sem=recv_sem,
                                        device_id=device_id, device_id_type=MESH)


HBM = pl.BlockSpec(memory_space=pltpu.HBM)
SEM = pl.BlockSpec(memory_space=pltpu.SEMAPHORE)
SPLIT_COPY_EFFECT = pltpu.SideEffectType.DATAFLOW_SIDE_EFFECTING


def _in_hbm(a):
    return pltpu.with_memory_space_constraint(a, pltpu.HBM)


def _start_copies(name, arrays, plan, n_copies, after=None):
    n = len(arrays)

    def body(*refs):
        send_sem, recv_sem = refs[-n - 3], refs[-n - 2]
        for k, (src, dst, peer) in enumerate(plan(refs[:n])):
            _push(src, dst, send_sem.at[k], recv_sem.at[k], peer)
        refs[-1][...] = jnp.zeros_like(refs[-1])

    dma = pltpu.SemaphoreType.DMA((n_copies,))
    outs = pl.pallas_call(
        body, name=name,
        out_shape=(dma, dma, *[pltpu.HBM(a.shape, a.dtype) for a in arrays], jax.ShapeDtypeStruct((8, 128), F32)),
        in_specs=[HBM] * n + ([ANY] if after is not None else []),
        out_specs=(SEM, SEM, *[HBM] * n, pl.BlockSpec(memory_space=pltpu.VMEM)),
        input_output_aliases={i: i + 2 for i in range(n)},
        compiler_params=pltpu.CompilerParams(has_side_effects=SPLIT_COPY_EFFECT),
    )(*[_in_hbm(a) for a in arrays], *((after,) if after is not None else ()))
    return outs[0], outs[1], list(outs[2:2 + n]), outs[-1]


def _wait_copies(name, send_sem, recv_sem, arrays, plan, after):
    n = len(arrays)

    def body(*refs):
        send_ref, recv_ref = refs[n], refs[n + 1]
        for k, (src, dst, peer) in enumerate(plan(refs[:n])):
            cp = pltpu.make_async_remote_copy(src_ref=src, dst_ref=dst, send_sem=send_ref.at[k], recv_sem=recv_ref.at[k],
                                              device_id=peer, device_id_type=MESH)
            cp.wait_send()
            cp.wait_recv()

    return list(pl.pallas_call(
        body, name=name, out_shape=tuple(pltpu.HBM(a.shape, a.dtype) for a in arrays),
        in_specs=[HBM] * n + [SEM, SEM, ANY], out_specs=tuple([HBM] * n),
        input_output_aliases={i: i for i in range(n)},
        compiler_params=pltpu.CompilerParams(has_side_effects=SPLIT_COPY_EFFECT),
    )(*arrays, send_sem, recv_sem, after))


def _gather_region(full, kind, chip, half=None):
    if kind == "whole":
        return full.at[chip]
    if kind == "col":
        rows, width = full.shape[0], full.shape[1] // N_CHIPS
        piece = full.at[:, pl.ds(pl.multiple_of(chip * width, 128), width)]
    else:
        rows = full.shape[0] // N_CHIPS
        piece = full.at[pl.ds(pl.multiple_of(chip * rows, BF16_TILE_ROWS), rows)]
    return piece if half is None else _half_rows(piece, half, rows)


def _gather_plan(kinds):
    def plan(fulls):
        x, y, c = _place()
        copies = []
        for fx, fy in OTHER_CHIPS:
            for full, kind in zip(fulls, kinds):
                mine = _gather_region(full, kind, 2 * x + y, c)
                copies.append((mine, mine, (x ^ fx, y ^ fy, c)))
        return copies

    return plan


def _gather_start(name, fulls, kinds, after=None):
    return _start_copies(name, list(fulls), _gather_plan(kinds), 3 * len(kinds), after)


def _gather_finish(name, started, kinds, after):
    send_sem, recv_sem, fulls, _ = started
    n = len(kinds)
    fulls = _wait_copies(name + "_wait", send_sem, recv_sem, fulls, _gather_plan(kinds), after)
    split = [i for i, kind in enumerate(kinds) if kind != "whole"]

    def body(*refs):
        out = refs[n:2 * n]
        send, recv = refs[2 * n:]
        x, y, c = _place()
        pushes = []
        for r, (fx, fy) in enumerate(OTHER_CHIPS):
            chip = 2 * (x ^ fx) + (y ^ fy)
            for j, i in enumerate(split):
                landed = _gather_region(out[i], kinds[i], chip, c)
                pushes.append(_push(landed, landed, send.at[r * len(split) + j], recv.at[r * len(split) + j], (x, y, 1 - c)))
        for cp in pushes:
            cp.wait_recv()
        for cp in pushes:
            cp.wait_send()

    dma = pltpu.SemaphoreType.DMA
    return pl.pallas_call(
        body, name=name + "_share", in_specs=[ANY] * n, out_specs=[ANY] * n,
        out_shape=[jax.ShapeDtypeStruct(a.shape, a.dtype) for a in fulls],
        input_output_aliases={i: i for i in range(n)},
        scratch_shapes=[dma((3 * len(split),)), dma((3 * len(split),))],
        compiler_params=pltpu.CompilerParams(has_side_effects=True),
    )(*fulls)


def _allreduce_small(v):
    def body(v_ref, out_ref, buf, send_sems, recv_sems):
        x, y, c = _place()
        me = 4 * x + 2 * y + c
        buf[me] = v_ref[...]
        copies = []
        for r in range(1, N_DEV):
            peer = (x ^ (r >> 2), y ^ ((r >> 1) & 1), c ^ (r & 1))
            cp = pltpu.make_async_remote_copy(src_ref=v_ref, dst_ref=buf.at[me], send_sem=send_sems.at[r - 1],
                                              recv_sem=recv_sems.at[r - 1], device_id=peer, device_id_type=MESH)
            cp.start()
            copies.append(cp)
        for cp in copies:
            cp.wait_recv()
        for cp in copies:
            cp.wait_send()
        total = buf[0]
        for d in range(1, N_DEV):
            total = total + buf[d]
        out_ref[...] = total

    vm = pl.BlockSpec(memory_space=pltpu.VMEM)
    return pl.pallas_call(
        body, name="allreduce_small", in_specs=[vm], out_specs=vm, out_shape=jax.ShapeDtypeStruct(v.shape, v.dtype),
        scratch_shapes=[pltpu.VMEM((N_DEV,) + v.shape, v.dtype), pltpu.SemaphoreType.DMA((N_DEV - 1,)),
                        pltpu.SemaphoreType.DMA((N_DEV - 1,))],
        compiler_params=pltpu.CompilerParams(has_side_effects=True),
    )(v)


def _exchange_halves(grads, name):
    n = len(grads)

    def body(*refs):
        g, theirs = refs[:n], refs[n:2 * n]
        send_sem, recv_sem = refs[2 * n:]
        x, y, c = _place()
        copies = []
        for i in range(n):
            half = g[i].shape[1] // 2
            src = g[i].at[:, pl.ds(pl.multiple_of((1 - c) * half, BF16_TILE_ROWS), half)]
            copies.append(_push(src, theirs[i], send_sem.at[i], recv_sem.at[i], (x, y, 1 - c)))
        for cp in copies:
            cp.wait_recv()
            cp.wait_send()

    dma = pltpu.SemaphoreType.DMA
    return pl.pallas_call(
        body, name=name, in_specs=[ANY] * n, out_specs=[ANY] * n,
        out_shape=[jax.ShapeDtypeStruct((a.shape[0], a.shape[1] // 2, a.shape[2]), a.dtype) for a in grads],
        scratch_shapes=[dma((n,)), dma((n,))],
        compiler_params=pltpu.CompilerParams(has_side_effects=True),
    )(*grads)


def _pair_sum(g, theirs, core, name):
    pieces, half, cols = theirs.shape
    tr = min(half, 256)
    per = half // tr

    def body(core_ref, g_ref, t_ref, o_ref):
        del core_ref
        o_ref[...] = (g_ref[...].astype(F32) + t_ref[...].astype(F32)).astype(BF16)

    spec = pl.BlockSpec((1, tr, cols), lambda p, i, core_ref: (p, i, 0))
    return pl.pallas_call(
        body, name=name, out_shape=jax.ShapeDtypeStruct(theirs.shape, BF16),
        grid_spec=pltpu.PrefetchScalarGridSpec(
            num_scalar_prefetch=1, grid=(pieces, per),
            in_specs=[pl.BlockSpec((1, tr, cols), lambda p, i, core_ref: (p, core_ref[0] * per + i, 0)), spec],
            out_specs=spec),
        compiler_params=_params("parallel", "parallel"))(core, g, theirs)


def _scatter_plan(n):
    def plan(refs):
        parts, stacks = refs[:n], refs[n:]
        x, y, c = _place()
        copies = []
        for fx, fy in OTHER_CHIPS:
            chip = 2 * (x ^ fx) + (y ^ fy)
            for part, stack in zip(parts, stacks):
                if part.shape[0] == N_CHIPS:
                    piece = part.at[chip]
                else:
                    width = part.shape[2] // N_CHIPS
                    piece = part.at[0].at[:, pl.ds(pl.multiple_of(chip * width, 128), width)]
                copies.append((piece, stack.at[2 * x + y], (x ^ fx, y ^ fy, c)))
        return copies

    return plan


def _scatter_start(name, parts, after=None):
    def landing(a):
        return (N_CHIPS, a.shape[1], a.shape[2] if a.shape[0] == N_CHIPS else a.shape[2] // N_CHIPS)

    stacks = [lax.empty(landing(a), a.dtype) for a in parts]
    return _start_copies(name, list(parts) + stacks, _scatter_plan(len(parts)), 3 * len(parts), after)


def _scatter_finish(name, started, after):
    send_sem, recv_sem, arrays, _ = started
    n = len(arrays) // 2
    arrays = _wait_copies(name + "_wait", send_sem, recv_sem, arrays, _scatter_plan(n), after)
    return arrays[:n], arrays[n:]


def _sum_chips(part, stack, place, name):
    _, r, c = stack.shape
    tr = 256
    per = r // tr

    def body(place_ref, own_ref, a_ref, b_ref, c_ref, o_ref):
        del place_ref
        total = own_ref[0].astype(F32)
        for ref in (a_ref, b_ref, c_ref):
            total = total + ref[0].astype(F32)
        o_ref[...] = total

    if part.shape[0] == N_CHIPS:
        own = pl.BlockSpec((1, tr, c), lambda i, pr: (pr[1], i, 0))
    else:
        own = pl.BlockSpec((1, tr, c), lambda i, pr: (0, i, pr[1]))
    other = lambda flip: pl.BlockSpec((1, tr, c), lambda i, pr: (pr[1] ^ flip, i, 0))
    return pl.pallas_call(
        body, name=name, out_shape=jax.ShapeDtypeStruct((2 * r, c), F32),
        grid_spec=pltpu.PrefetchScalarGridSpec(
            num_scalar_prefetch=1, grid=(per,), in_specs=[own, other(2), other(1), other(3)],
            out_specs=pl.BlockSpec((tr, c), lambda i, pr: (pr[0] * per + i, 0))),
        compiler_params=_params("parallel"))(place, part, stack, stack, stack)


def _share_halves(fulls):
    n = len(fulls)

    def body(*refs):
        dst = refs[n:2 * n]
        send_sem, recv_sem = refs[2 * n:]
        x, y, c = _place()
        copies = []
        for i in range(n):
            mine = _half_rows(dst[i], c, dst[i].shape[0])
            copies.append(_push(mine, mine, send_sem.at[i], recv_sem.at[i], (x, y, 1 - c)))
        for cp in copies:
            cp.wait_recv()
            cp.wait_send()

    dma = pltpu.SemaphoreType.DMA
    return pl.pallas_call(
        body, name="share_halves", in_specs=[ANY] * n, out_specs=[ANY] * n,
        out_shape=[jax.ShapeDtypeStruct(a.shape, a.dtype) for a in fulls],
        input_output_aliases={i: i for i in range(n)}, scratch_shapes=[dma((n,)), dma((n,))],
        compiler_params=pltpu.CompilerParams(has_side_effects=True),
    )(*fulls)


MM = dict(tm=2048, tn=1024, tk=2048)
MM_LONG_K = dict(tm=1024, tn=1024, tk=4096)


def _local_step(x, target, ev_norm_w, q_norm_w, k_norm_w, sinks, own_first, weights_first, weights_late, emit):
    s = x.shape[0]
    cos_r, sin_r = _rope_tables(s, RET_DIM)
    cos_a, sin_a = _rope_tables(s, HEAD_DIM)
    cos_a = np.tile(cos_a, (1, 4))
    sins_a = np.tile(np.concatenate([-sin_a, sin_a], axis=1), (1, 2))
    tables = _retention_tables()
    ones_q, ones_kv = _block_diag_mean(GROUP_WIDTH), _block_diag_mean(KV_WIDTH)
    qw_g = jnp.tile(q_norm_w, (1, GROUP_WIDTH // HEAD_DIM))
    kw_kv = jnp.tile(k_norm_w, (1, KV_WIDTH // HEAD_DIM))
    sinks1 = sinks.reshape(Q_HEADS)

    own_w_in0, own_block, start_token = own_first
    h0 = _rmsnorm(x, ev_norm_w, "norm0", after=start_token)
    shifted = dict(shift=own_block * (own_w_in0.shape[1] // MM["tn"]), total=EVEN_IN // MM["tn"], tm=MM["tm"], tn=MM["tn"],
                   out_dtype=BF16)
    own_blocks = own_w_in0.shape[1] // MM["tn"]
    proj0 = _mm_shifted(h0, own_w_in0, b_shifted=False, first=0, count=own_blocks, name="proj0_own", **shifted)
    w_in0, conv_w, od_norm_w, token = weights_first(proj0)
    proj0 = _mm_shifted(h0, w_in0, b_shifted=True, first=own_blocks, count=shifted["total"] - own_blocks, name="proj0_rest",
                        into=proj0, after=token, **shifted)
    o_ret, cat, states = _retention_fwd(proj0, cos_r, sin_r, tables)
    cat = _conv_fwd(proj0, conv_w, cat)
    w_out0, w_in1, w_out1 = weights_late(cat)

    def residual_and_norm(prod, x_ref, w_ref, x1_ref, h1_ref):
        x1v = x_ref[...] + prod
        x1_ref[...] = x1v
        rstd = lax.rsqrt(jnp.mean(x1v * x1v, axis=-1, keepdims=True) + EPS)
        h1_ref[...] = (x1v * rstd * w_ref[...]).astype(BF16)

    def residual_and_loss(prod, x1_ref, t_ref, dyb_ref, sq_ref):
        diff = (x1_ref[...] + prod) - t_ref[...]
        dyb_ref[...] = (diff * (1.0 / D_MODEL)).astype(BF16)

        @pl.when(pl.program_id(0) == 0)
        def _():
            sq_ref[...] = jnp.zeros_like(sq_ref)

        sq_ref[...] += jnp.sum(jnp.sum(diff * diff, axis=1, keepdims=True), axis=0, keepdims=True)

    act = lambda dt: jax.ShapeDtypeStruct((s, D_MODEL), dt)
    x1, h1 = _mm_rows(cat, w_out0, [x], [od_norm_w], [act(F32), act(BF16)], residual_and_norm, tm=min(s, 512), name="out0")
    proj1 = _mm(h1, w_in1, mode="nn", out_dtype=BF16, name="proj1", tm=2048, tn=1536, tk=2048)
    qk = _qk_prep(proj1, qw_g, kw_kv, cos_a, sins_a, ones_q, ones_kv)
    ag, o_att = _swa_fwd(qk, proj1, sinks1)
    dy_b, sq = _mm_rows(ag, w_out1, [x1, target], [], [act(BF16), jax.ShapeDtypeStruct((8, 128), F32)],
                        residual_and_loss, tm=min(s, 512), name="out1")

    g_w_out1 = _mm(ag, dy_b, mode="tn", out_dtype=BF16, name="g_w_out1", **MM_LONG_K)
    dag = _mm(dy_b, w_out1, mode="nt", out_dtype=BF16, name="d_ag", **MM)
    dqr, dkc, dkp, dvc, dvp, dsink = _swa_bwd(qk, proj1, dag, sinks1)
    dproj1, dqw, dkw = _swa_bwd_finish(proj1, dqr, o_att, dag, dkc, dkp, dvc, dvp, qw_g, kw_kv, cos_a, sins_a, ones_q, ones_kv)
    g_w_in1 = _mm(h1, dproj1, mode="tn", out_dtype=BF16, name="g_w_in1", tm=1024, tn=768, tk=4096)
    dh1 = _mm(dproj1, w_in1, mode="nt", out_dtype=BF16, name="d_h1", tm=1024, tn=1024, tk=ODD_IN)
    dx1_b, g_norm1 = _rmsnorm_bwd(x1, od_norm_w, dh1, dy_b, "norm1_bwd", BF16)

    g_w_out0 = _mm(cat, dx1_b, mode="tn", out_dtype=BF16, name="g_w_out0", **MM_LONG_K)
    token = emit("upper", (("od_w_in", g_w_in1, "col"), ("od_w_out", g_w_out1, "row"), ("ev_w_out", g_w_out0, "row")))
    dcat = _mm(dx1_b, w_out0, mode="nt", out_dtype=BF16, name="d_cat", after=token, **MM)
    dproj0 = _retention_bwd(proj0, o_ret, states, dcat, cos_r, sin_r, tables)
    dproj0, g_conv = _conv_bwd(proj0, dcat, conv_w, dproj0)
    g_w_in0 = _mm(h0, dproj0, mode="tn", out_dtype=BF16, name="g_w_in0", **MM_LONG_K)
    token = emit("in0", (("ev_w_in", g_w_in0, "col"),))
    dh0 = _mm(dproj0, w_in0, mode="nt", out_dtype=BF16, name="d_h0", after=token, **MM_LONG_K)
    grad_x, g_norm0 = _rmsnorm_bwd(x, ev_norm_w, dh0, dx1_b, "norm0_bwd", F32)

    g_qw = dqw[:, 0, :].reshape(Q_HEADS, HEAD_DIM).sum(axis=0)
    g_kw = dkw[0].reshape(KV_HEADS, HEAD_DIM).sum(axis=0)
    g_sinks = dsink[:, :, 0].reshape(Q_HEADS)
    small = dict(ev_norm=g_norm0[0], od_norm=g_norm1[0], conv=g_conv[:3], qw=g_qw, kw=g_kw, sinks=g_sinks)
    return sq[0, 0], grad_x, small


def _pack_small_grads(small, sq):
    pad = lambda v: jnp.pad(v, (0, D_MODEL - v.shape[0]))
    tail = pad(jnp.concatenate([small["qw"], small["kw"], small["sinks"]]))
    rows = [small["ev_norm"], small["od_norm"]] + [pad(small["conv"][t]) for t in range(3)] + [tail, pad(sq.reshape(1))]
    rows += [jnp.zeros((D_MODEL,), F32)] * (8 - len(rows))
    return jnp.stack(rows)


class _ReduceScatter:
    def __init__(self, place):
        self.place = place
        self.started = []

    def send(self, tag, grads):
        pieces = [g[None] if kind == "col" else g.reshape(N_CHIPS, g.shape[0] // N_CHIPS, g.shape[1]) for _, g, kind in grads]
        theirs = _exchange_halves(pieces, "exchange_halves_" + tag)
        parts = [_pair_sum(g, t, self.place[:1], "pair_sum_" + nm) for g, t, (nm, _, _) in zip(pieces, theirs, grads)]
        started = _scatter_start("scatter_" + tag, parts)
        self.started.append((tag, [nm for nm, _, _ in grads], started))
        return started[3]

    def finish(self, after):
        names, sums = [], []
        for tag, group, started in self.started:
            parts, stacks = _scatter_finish("scatter_" + tag, started, after)
            sums += [_sum_chips(p, s, self.place, "chip_sum_" + nm) for p, s, nm in zip(parts, stacks, group)]
            names += group
        return dict(zip(names, _share_halves(sums)))


def kernel(x, ev_norm_w, ev_w_in, ev_conv_w, ev_w_out, od_norm_w, od_w_in, od_q_norm_w, od_k_norm_w, od_sinks, od_w_out, loss_target, m_ev_norm_w, m_ev_w_in, m_ev_conv_w, m_ev_w_out, m_od_norm_w, m_od_w_in, m_od_q_norm_w, m_od_k_norm_w, m_od_sinks, m_od_w_out, v_ev_norm_w, v_ev_w_in, v_ev_conv_w, v_ev_w_out, v_od_norm_w, v_od_w_in, v_od_q_norm_w, v_od_k_norm_w, v_od_sinks, v_od_w_out):
    my_chip = 2 * lax.axis_index("x") + lax.axis_index("y")
    place = jnp.stack([lax.axis_index("c"), my_chip]).astype(jnp.int32)
    shard_w = D_MODEL // N_CHIPS
    conv_shard = CONV_WIDTH // N_CHIPS

    small_in = jnp.zeros((8, shard_w), F32)
    small_in = small_in.at[0].set(od_norm_w[0]).at[1:4, :conv_shard].set(ev_conv_w[0])
    small_in = lax.dynamic_update_slice(jnp.zeros((N_CHIPS, 8, shard_w), F32), small_in[None], (my_chip, 0, 0))
    chip = place[1:]
    first_kinds, late_kinds = ("col", "whole"), ("row", "col", "row")
    w_in0_own_place, w_in0_shard = _cast_into_gathered(ev_w_in[0], "col", chip, "cast_w_in0", keep_shard=True)
    first = _gather_start("gather_first", [w_in0_own_place, small_in], first_kinds)
    late_own = [_cast_into_gathered(ev_w_out[0], "row", chip, "cast_w_out0"),
                _cast_into_gathered(od_w_in[0], "col", chip, "cast_w_in1"),
                _cast_into_gathered(od_w_out[0], "row", chip, "cast_w_out1")]
    late = []

    def weights_first(after):
        w_in0, small_all = _gather_finish("gather_first", first, first_kinds, after)
        late.append(_gather_start("gather_late", late_own, late_kinds, after=w_in0))
        od_norm_full = small_all[:, 0, :].reshape(1, D_MODEL)
        conv_full = jnp.transpose(small_all[:, 1:4, :conv_shard], (1, 0, 2)).reshape(3, CONV_WIDTH)
        return w_in0, conv_full, od_norm_full, late[0][3]

    def weights_late(after):
        return _gather_finish("gather_late", late[0], late_kinds, after)

    reduce_scatter = _ReduceScatter(place)
    sq, grad_x, small = _local_step(x[0], loss_target[0], ev_norm_w, od_q_norm_w, od_k_norm_w, od_sinks,
                                    (w_in0_shard, chip, first[3]), weights_first, weights_late, reduce_scatter.send)

    big = reduce_scatter.finish(grad_x)
    g_ev_w_in, g_ev_w_out, g_od_w_in, g_od_w_out = big["ev_w_in"], big["ev_w_out"], big["od_w_in"], big["od_w_out"]
    tot = _allreduce_small(_pack_small_grads(small, sq))
    loss = 0.5 * tot[6, 0] / D_MODEL
    g_ev_norm = tot[0:1]
    g_od_norm = lax.dynamic_slice(tot, (1, my_chip * shard_w), (1, shard_w))
    g_conv = lax.dynamic_slice(tot, (2, my_chip * conv_shard), (3, conv_shard))
    g_qw, g_kw, g_sinks = tot[5:6, 0:HEAD_DIM], tot[5:6, HEAD_DIM:2 * HEAD_DIM], tot[5:6, 2 * HEAD_DIM:2 * HEAD_DIM + Q_HEADS]

    upd = {}
    upd["ev_w_in"] = _adamw(ev_w_in[0], g_ev_w_in, m_ev_w_in[0], v_ev_w_in[0], "adamw_ev_w_in")
    upd["ev_w_out"] = _adamw(ev_w_out[0], g_ev_w_out, m_ev_w_out[0], v_ev_w_out[0], "adamw_ev_w_out")
    upd["od_w_in"] = _adamw(od_w_in[0], g_od_w_in, m_od_w_in[0], v_od_w_in[0], "adamw_od_w_in")
    upd["od_w_out"] = _adamw(od_w_out[0], g_od_w_out, m_od_w_out[0], v_od_w_out[0], "adamw_od_w_out")
    smalls = (("ev_norm_w", ev_norm_w, g_ev_norm, m_ev_norm_w, v_ev_norm_w),
              ("ev_conv_w", ev_conv_w, g_conv, m_ev_conv_w, v_ev_conv_w),
              ("od_norm_w", od_norm_w, g_od_norm, m_od_norm_w, v_od_norm_w),
              ("od_q_norm_w", od_q_norm_w, g_qw, m_od_q_norm_w, v_od_q_norm_w),
              ("od_k_norm_w", od_k_norm_w, g_kw, m_od_k_norm_w, v_od_k_norm_w),
              ("od_sinks", od_sinks, g_sinks, m_od_sinks, v_od_sinks))
    sizes = [w.size for _, w, _, _, _ in smalls]
    padded = 8 * 128 * math.ceil(sum(sizes) / (8 * 128))
    pack = lambda arrs, fill: jnp.concatenate(
        [a.reshape(-1) for a in arrs] + [jnp.full((padded - sum(sizes),), fill, F32)]).reshape(8, padded // 8)
    packed = _adamw(pack([w for _, w, _, _, _ in smalls], 0.0), pack([g for _, _, g, _, _ in smalls], 0.0),
                    pack([m for _, _, _, m, _ in smalls], 0.0), pack([v for _, _, _, _, v in smalls], 1.0), "adamw_small")
    offs = [sum(sizes[:i]) for i in range(len(sizes))]
    for (nm, w, _, _, _), off, size in zip(smalls, offs, sizes):
        upd[nm] = tuple(p.reshape(-1)[off:off + size].reshape(w.shape) for p in packed)
    for nm in ("ev_w_in", "ev_w_out", "od_w_in", "od_w_out"):
        upd[nm] = tuple(u[None] for u in upd[nm])
    order = ("ev_norm_w", "ev_w_in", "ev_conv_w", "ev_w_out", "od_norm_w", "od_w_in", "od_q_norm_w", "od_k_norm_w", "od_sinks", "od_w_out")
    return (loss, grad_x[None], *[upd[nm][0] for nm in order], *[upd[nm][1] for nm in order],
            *[upd[nm][2] for nm in order], *[upd[nm][3] for nm in order])
```

```python
import math

import jax
import jax.numpy as jnp
import numpy as np
from jax import lax
from jax.experimental import pallas as pl
from jax.experimental.pallas import tpu as pltpu

F32 = jnp.float32
BF16 = jnp.bfloat16

D_MODEL = 2048
RET_HEADS = 4
RET_DIM = 256
RET_WIDTH = 1024
CONV_WIDTH = 1024
EVEN_IN = 8192
Q_HEADS = 32
HEAD_DIM = 64
KV_HEADS = 4
KV_WIDTH = 256
ATTN_WIDTH = 2048
ODD_IN = 4608
BLK = 128
ROPE_THETA = 10000.0
EPS = 1e-6
ADAM_LR = 0.001
ADAM_B1 = 0.9
ADAM_B2 = 0.999
ADAM_EPS = 1e-08
ADAM_WD = 0.01
ADAM_STEP = 10
N_CHIPS = 4
N_DEV = 8
VMEM_LIMIT_BYTES = 56 * 1024 * 1024
MESH = pl.DeviceIdType.MESH
ANY = pl.BlockSpec(memory_space=pl.ANY)


def _params(*sem):
    return pltpu.CompilerParams(dimension_semantics=sem, vmem_limit_bytes=VMEM_LIMIT_BYTES)


def _dot(a, b):
    return jnp.dot(a, b, preferred_element_type=F32)


def _dot_nt(a, b):
    return lax.dot_general(a, b, (((1,), (1,)), ((), ())), preferred_element_type=F32)


def _dot_tn(a, b):
    return lax.dot_general(a, b, (((0,), (0,)), ((), ())), preferred_element_type=F32)


def _sigmoid(x):
    return 1.0 / (1.0 + jnp.exp(-x))


def _mm(a, b, *, mode, tm, tn, tk, out_dtype, name, add=None, after=None):
    if mode == "nn":
        (m, k), n = a.shape, b.shape[1]
    elif mode == "nt":
        (m, k), n = a.shape, b.shape[0]
    else:
        (k, m), n = a.shape, b.shape[1]
    tm, tn, tk = min(tm, m), min(tn, n), min(tk, k)
    assert m % tm == 0 and n % tn == 0 and k % tk == 0, (name, m, n, k)
    nk = k // tk
    dot = {"nn": _dot, "nt": _dot_nt, "tn": _dot_tn}[mode]
    a_spec = (pl.BlockSpec((tk, tm), lambda i, j, kk: (kk, i)) if mode == "tn"
              else pl.BlockSpec((tm, tk), lambda i, j, kk: (i, kk)))
    b_spec = (pl.BlockSpec((tn, tk), lambda i, j, kk: (j, kk)) if mode == "nt"
              else pl.BlockSpec((tk, tn), lambda i, j, kk: (kk, j)))
    o_spec = pl.BlockSpec((tm, tn), lambda i, j, kk: (i, j))
    has_add = add is not None

    def body(*refs):
        a_ref, b_ref = refs[0], refs[1]
        add_ref = refs[2] if has_add else None
        o_ref, acc_ref = refs[-2], refs[-1]
        p = dot(a_ref[...], b_ref[...])

        def finish(total):
            if has_add:
                total = total + add_ref[...].astype(F32)
            o_ref[...] = total.astype(out_dtype)

        if nk == 1:
            finish(p)
        else:
            kk = pl.program_id(2)

            @pl.when(kk == 0)
            def _():
                acc_ref[...] = p

            @pl.when(jnp.logical_and(kk > 0, kk < nk - 1))
            def _():
                acc_ref[...] += p

            @pl.when(kk == nk - 1)
            def _():
                finish(acc_ref[...] + p)

    in_specs = [a_spec, b_spec] + ([o_spec] if has_add else []) + ([ANY] if after is not None else [])
    args = (a, b) + ((add,) if has_add else ()) + ((after,) if after is not None else ())
    return pl.pallas_call(
        body, name=name, grid=(m // tm, n // tn, nk), in_specs=in_specs, out_specs=o_spec,
        out_shape=jax.ShapeDtypeStruct((m, n), out_dtype),
        scratch_shapes=[pltpu.VMEM((tm, tn) if nk > 1 else (8, 128), F32)],
        compiler_params=_params("parallel", "parallel", "arbitrary"),
    )(*args)


def _mm_shifted(a, b, shift, *, b_shifted, first, count, total, tm, tn, out_dtype, name, into=None, after=None):
    m, k = a.shape
    tm = min(tm, m)
    assert m % tm == 0
    col = lambda j, shift_ref: (shift_ref[0] + first + j) % total
    extra = [arr for arr in (into, after) if arr is not None]

    def body(shift_ref, a_ref, b_ref, *rest):
        del shift_ref
        rest[-1][...] = _dot(a_ref[...], b_ref[...]).astype(out_dtype)

    return pl.pallas_call(
        body, name=name, out_shape=jax.ShapeDtypeStruct((m, total * tn), out_dtype),
        grid_spec=pltpu.PrefetchScalarGridSpec(
            num_scalar_prefetch=1, grid=(m // tm, count),
            in_specs=[pl.BlockSpec((tm, k), lambda i, j, s: (i, 0)),
                      pl.BlockSpec((k, tn), (lambda i, j, s: (0, col(j, s))) if b_shifted else (lambda i, j, s: (0, j)))]
            + [ANY] * len(extra),
            out_specs=pl.BlockSpec((tm, tn), lambda i, j, s: (i, col(j, s)))),
        input_output_aliases={3: 0} if into is not None else {},
        compiler_params=_params("parallel", "arbitrary"))(shift, a, b, *extra)


def _mm_rows(a, b, rows_in, vecs_in, out_shapes, epilogue, *, tm, name):
    m, k = a.shape
    n = b.shape[1]
    assert m % tm == 0
    row = pl.BlockSpec((tm, n), lambda i: (i, 0))

    def body(a_ref, b_ref, *rest):
        epilogue(_dot(a_ref[...], b_ref[...]), *rest)

    out_specs = [row if tuple(s.shape) == (m, n) else pl.BlockSpec(s.shape, lambda i: (0, 0)) for s in out_shapes]
    return pl.pallas_call(
        body, name=name, grid=(m // tm,),
        in_specs=[pl.BlockSpec((tm, k), lambda i: (i, 0)), pl.BlockSpec((k, n), lambda i: (0, 0))] + [row] * len(rows_in)
        + [pl.BlockSpec((1, n), lambda i: (0, 0))] * len(vecs_in),
        out_specs=out_specs, out_shape=out_shapes, compiler_params=_params("arbitrary"),
    )(a, b, *rows_in, *vecs_in)


def _cast_into_gathered(w, kind, chip, name, keep_shard=False):
    r, c = w.shape
    tr = min(r, 512)
    per = r // tr

    def body(chip_ref, w_ref, *outs):
        del chip_ref
        for o_ref in outs:
            o_ref[...] = w_ref[...].astype(BF16)

    if kind == "col":
        shape, out_map = (r, N_CHIPS * c), (lambda i, chip_ref: (i, chip_ref[0]))
    else:
        shape, out_map = (N_CHIPS * r, c), (lambda i, chip_ref: (chip_ref[0] * per + i, 0))
    plain = pl.BlockSpec((tr, c), lambda i, chip_ref: (i, 0))
    out = pl.pallas_call(
        body, name=name,
        out_shape=[jax.ShapeDtypeStruct(shape, BF16)] + ([jax.ShapeDtypeStruct((r, c), BF16)] if keep_shard else []),
        grid_spec=pltpu.PrefetchScalarGridSpec(
            num_scalar_prefetch=1, grid=(per,), in_specs=[plain],
            out_specs=[pl.BlockSpec((tr, c), out_map)] + ([plain] if keep_shard else [])),
        compiler_params=_params("parallel"))(chip, w)
    return out if keep_shard else out[0]


NORM_ROWS = 512


def _rmsnorm(x, w, name, after=None):
    s, d = x.shape
    tr = NORM_ROWS

    def body(x_ref, w_ref, *rest):
        xv = x_ref[...]
        rstd = lax.rsqrt(jnp.mean(xv * xv, axis=-1, keepdims=True) + EPS)
        rest[-1][...] = (xv * rstd * w_ref[...]).astype(BF16)

    return pl.pallas_call(
        body, name=name, grid=(s // tr,),
        in_specs=[pl.BlockSpec((tr, d), lambda i: (i, 0)), pl.BlockSpec((1, d), lambda i: (0, 0))]
        + ([ANY] if after is not None else []),
        out_specs=pl.BlockSpec((tr, d), lambda i: (i, 0)),
        out_shape=jax.ShapeDtypeStruct((s, d), BF16), compiler_params=_params("parallel"),
    )(x, w, *((after,) if after is not None else ()))


def _rmsnorm_bwd(x, w, dh, dres, name, out_dtype):
    s, d = x.shape
    tr = NORM_ROWS

    def body(x_ref, w_ref, dh_ref, dres_ref, dx_ref, dw_ref):
        xv = x_ref[...]
        rstd = lax.rsqrt(jnp.mean(xv * xv, axis=-1, keepdims=True) + EPS)
        nrm = xv * rstd
        dhv = dh_ref[...].astype(F32)
        dn = dhv * w_ref[...]
        dx = dres_ref[...].astype(F32) + rstd * (dn - nrm * jnp.mean(dn * nrm, axis=-1, keepdims=True))
        dx_ref[...] = dx.astype(out_dtype)

        @pl.when(pl.program_id(0) == 0)
        def _():
            dw_ref[...] = jnp.zeros_like(dw_ref)

        dw_ref[...] += jnp.sum(dhv * nrm, axis=0, keepdims=True)

    row = pl.BlockSpec((tr, d), lambda i: (i, 0))
    return pl.pallas_call(
        body, name=name, grid=(s // tr,),
        in_specs=[row, pl.BlockSpec((1, d), lambda i: (0, 0)), row, row],
        out_specs=[row, pl.BlockSpec((8, d), lambda i: (0, 0))],
        out_shape=[jax.ShapeDtypeStruct((s, d), out_dtype), jax.ShapeDtypeStruct((8, d), F32)],
        compiler_params=_params("arbitrary"))(x, w, dh, dres)


def _adamw(w, g, m, v, name):
    r, c = w.shape
    tr = min(r, 256)
    assert r % tr == 0

    def body(w_ref, g_ref, m_ref, v_ref, g_out, d_ref, nm_ref, nv_ref):
        gv = g_ref[...]
        g_out[...] = gv
        nm = ADAM_B1 * m_ref[...] + (1.0 - ADAM_B1) * gv
        nv = ADAM_B2 * v_ref[...] + (1.0 - ADAM_B2) * (gv * gv)
        m_hat = nm / (1.0 - ADAM_B1 ** ADAM_STEP)
        v_hat = nv / (1.0 - ADAM_B2 ** ADAM_STEP)
        d_ref[...] = -ADAM_LR * (m_hat / (jnp.sqrt(v_hat) + ADAM_EPS) + ADAM_WD * w_ref[...])
        nm_ref[...] = nm
        nv_ref[...] = nv

    spec = pl.BlockSpec((tr, c), lambda i: (i, 0))
    shp = jax.ShapeDtypeStruct((r, c), F32)
    return pl.pallas_call(body, name=name, grid=(r // tr,), in_specs=[spec] * 4, out_specs=[spec] * 4,
                          out_shape=[shp] * 4, compiler_params=_params("parallel"))(w, g, m, v)


def _rope_tables(s, dim):
    inv = (1.0 / (ROPE_THETA ** (np.arange(0, dim, 2, dtype=np.float64) / dim))).astype(np.float32)
    ang = (np.arange(s, dtype=np.float32)[:, None] * inv[None, :]).astype(np.float64)
    return np.cos(ang).astype(np.float32), np.sin(ang).astype(np.float32)


def _rope_half(x, cos, sin):
    h = x.shape[1] // 2
    x1, x2 = x[:, :h], x[:, h:]
    return jnp.concatenate([x1 * cos - x2 * sin, x2 * cos + x1 * sin], axis=1)


def _unrope_half(dy, cos, sin):
    h = dy.shape[1] // 2
    d1, d2 = dy[:, :h], dy[:, h:]
    return jnp.concatenate([d1 * cos + d2 * sin, d2 * cos - d1 * sin], axis=1)


def _lane(shape):
    return lax.broadcasted_iota(jnp.int32, shape, 1)


def _partner64(x):
    w = x.shape[1]
    first = (_lane(x.shape) % HEAD_DIM) < (HEAD_DIM // 2)
    return jnp.where(first, pltpu.roll(x, w - HEAD_DIM // 2, 1), pltpu.roll(x, HEAD_DIM // 2, 1))


def _tile_lanes(t, reps):
    return t if reps == 1 else jnp.concatenate([t] * reps, axis=1)


def _group_mean(x, ones_bd, passes=2):
    hi = x.astype(BF16)
    if passes == 1:
        return _dot(hi, ones_bd)
    lo = (x - hi.astype(F32)).astype(BF16)
    return _dot(hi, ones_bd) + _dot(lo, ones_bd)


def _block_diag_mean(width):
    idx = jnp.arange(width) // HEAD_DIM
    return jnp.where(idx[:, None] == idx[None, :], 1.0 / HEAD_DIM, 0.0).astype(BF16)


RET_STEP = 4


def _retention_tables():
    h = RET_HEADS
    log_g = jnp.log(1.0 - 2.0 ** (-5.0 - jnp.arange(h, dtype=F32)))
    idx = jnp.arange(BLK, dtype=F32)
    diff = idx[:, None] - idx[None, :]
    intra = jnp.where(diff >= 0, jnp.exp(log_g[:, None, None] * jnp.maximum(diff, 0.0)), 0.0).astype(F32)
    q_dec = jnp.exp(log_g[:, None] * (idx[None, :] + 1.0)).astype(F32)[:, :, None]
    k_dec = jnp.exp(log_g[:, None] * (BLK - 1.0 - idx[None, :])).astype(F32)[:, :, None]
    chunk_dec = jnp.exp(log_g * BLK).astype(F32)[:, None, None]
    return intra, q_dec, k_dec, chunk_dec


def _retention_fwd(proj, cos, sin, tables, after=None):
    s = proj.shape[0]
    nc = s // BLK
    intra, q_dec, k_dec, chunk_dec = tables

    def body(p_ref, cos_ref, sin_ref, in_ref, qd_ref, kd_ref, cd_ref, *rest):
        o_ref, cat_ref, st_ref, state = rest[-4:]

        @pl.when(pl.program_id(0) == 0)
        def _():
            state[...] = jnp.zeros_like(state)

        for c in range(RET_STEP):
            rows = slice(c * BLK, (c + 1) * BLK)
            cosv, sinv = cos_ref[rows, :], sin_ref[rows, :]
            for h in range(RET_HEADS):
                c0 = h * RET_DIM
                q = p_ref[rows, c0:c0 + RET_DIM].astype(F32)
                k = p_ref[rows, RET_WIDTH + c0:RET_WIDTH + c0 + RET_DIM].astype(F32)
                v = p_ref[rows, 2 * RET_WIDTH + c0:2 * RET_WIDTH + c0 + RET_DIM]
                g = p_ref[rows, 3 * RET_WIDTH + c0:3 * RET_WIDTH + c0 + RET_DIM].astype(F32)
                qb = _rope_half(q, cosv, sinv).astype(BF16)
                kr = _rope_half(k, cosv, sinv) * (RET_DIM ** -0.5)
                kb = kr.astype(BF16)
                scores = _dot_nt(qb, kb) * in_ref[h]
                inner = _dot(scores.astype(BF16), v)
                prev = state[h]
                prev_b = prev.astype(BF16)
                st_ref[h, c] = prev_b
                o = inner + _dot(qb, prev_b) * qd_ref[h]
                o_ref[rows, c0:c0 + RET_DIM] = o
                rstd = lax.rsqrt(jnp.mean(o * o, axis=-1, keepdims=True) + EPS)
                cat_ref[rows, c0:c0 + RET_DIM] = (o * rstd * (g * _sigmoid(g))).astype(BF16)
                state[h] = cd_ref[h] * prev + _dot_tn((kr * kd_ref[h]).astype(BF16), v)

    full = lambda shape: pl.BlockSpec(shape, lambda n: (0,) * len(shape))
    step = RET_STEP * BLK
    return pl.pallas_call(
        body, name="retention_fwd", grid=(nc // RET_STEP,),
        in_specs=[pl.BlockSpec((step, 4 * RET_WIDTH), lambda n: (n, 0)),
                  pl.BlockSpec((step, RET_DIM // 2), lambda n: (n, 0)), pl.BlockSpec((step, RET_DIM // 2), lambda n: (n, 0)),
                  full((RET_HEADS, BLK, BLK)), full((RET_HEADS, BLK, 1)), full((RET_HEADS, BLK, 1)), full((RET_HEADS, 1, 1))]
        + ([ANY] if after is not None else []),
        out_specs=[pl.BlockSpec((step, RET_WIDTH), lambda n: (n, 0)), pl.BlockSpec((step, RET_WIDTH), lambda n: (n, 0)),
                   pl.BlockSpec((RET_HEADS, RET_STEP, RET_DIM, RET_DIM), lambda n: (0, n, 0, 0))],
        out_shape=[jax.ShapeDtypeStruct((s, RET_WIDTH), F32), jax.ShapeDtypeStruct((s, D_MODEL), BF16),
                   jax.ShapeDtypeStruct((RET_HEADS, nc, RET_DIM, RET_DIM), BF16)],
        scratch_shapes=[pltpu.VMEM((RET_HEADS, RET_DIM, RET_DIM), F32)],
        compiler_params=_params("arbitrary"),
    )(proj, cos, sin, intra, q_dec, k_dec, chunk_dec, *((after,) if after is not None else ()))


def _retention_bwd(proj, o, states, dcat, cos, sin, tables):
    s = proj.shape[0]
    nc = s // BLK
    intra, q_dec, k_dec, chunk_dec = tables

    def body(p_ref, o_ref, st_ref, dc_ref, cos_ref, sin_ref, in_ref, qd_ref, kd_ref, cd_ref, dp_ref, dstate):
        @pl.when(pl.program_id(0) == 0)
        def _():
            dstate[...] = jnp.zeros_like(dstate)

        for c in reversed(range(RET_STEP)):
            rows = slice(c * BLK, (c + 1) * BLK)
            cosv, sinv = cos_ref[rows, :], sin_ref[rows, :]
            for h in range(RET_HEADS):
                c0 = h * RET_DIM
                q = p_ref[rows, c0:c0 + RET_DIM].astype(F32)
                k = p_ref[rows, RET_WIDTH + c0:RET_WIDTH + c0 + RET_DIM].astype(F32)
                v = p_ref[rows, 2 * RET_WIDTH + c0:2 * RET_WIDTH + c0 + RET_DIM]
                g = p_ref[rows, 3 * RET_WIDTH + c0:3 * RET_WIDTH + c0 + RET_DIM].astype(F32)
                o = o_ref[rows, c0:c0 + RET_DIM]
                dc = dc_ref[rows, c0:c0 + RET_DIM].astype(F32)
                rstd = lax.rsqrt(jnp.mean(o * o, axis=-1, keepdims=True) + EPS)
                nrm = o * rstd
                sg = _sigmoid(g)
                dg = dc * nrm * (sg * (1.0 + g * (1.0 - sg)))
                dn = dc * (g * sg)
                do = rstd * (dn - nrm * jnp.mean(dn * nrm, axis=-1, keepdims=True))
                qb = _rope_half(q, cosv, sinv).astype(BF16)
                kr = _rope_half(k, cosv, sinv) * (RET_DIM ** -0.5)
                kb = kr.astype(BF16)
                mask = in_ref[h]
                qd, kd = qd_ref[h], kd_ref[h]
                prev_b = st_ref[h, c]
                dnext = dstate[h]
                dnext_b = dnext.astype(BF16)
                att = (_dot_nt(qb, kb) * mask).astype(BF16)
                do_b = do.astype(BF16)
                doq = (do * qd).astype(BF16)
                dv = _dot_tn(att, do_b) + _dot((kr * kd).astype(BF16), dnext_b)
                ds = (_dot_nt(do_b, v) * mask).astype(BF16)
                dqr = _dot(ds, kb) + _dot_nt(doq, prev_b)
                dkr = _dot_tn(ds, qb) + _dot_nt(v, dnext_b) * kd
                dstate[h] = cd_ref[h] * dnext + _dot_tn(qb, doq)
                dq = _unrope_half(dqr, cosv, sinv)
                dk = _unrope_half(dkr * (RET_DIM ** -0.5), cosv, sinv)
                dp_ref[rows, c0:c0 + RET_DIM] = dq.astype(BF16)
                dp_ref[rows, RET_WIDTH + c0:RET_WIDTH + c0 + RET_DIM] = dk.astype(BF16)
                dp_ref[rows, 2 * RET_WIDTH + c0:2 * RET_WIDTH + c0 + RET_DIM] = dv.astype(BF16)
                dp_ref[rows, 3 * RET_WIDTH + c0:3 * RET_WIDTH + c0 + RET_DIM] = dg.astype(BF16)

    steps = nc // RET_STEP
    rev = lambda n: steps - 1 - n
    full = lambda shape: pl.BlockSpec(shape, lambda n: (0,) * len(shape))
    step = RET_STEP * BLK
    return pl.pallas_call(
        body, name="retention_bwd", grid=(steps,),
        in_specs=[pl.BlockSpec((step, 4 * RET_WIDTH), lambda n: (rev(n), 0)),
                  pl.BlockSpec((step, RET_WIDTH), lambda n: (rev(n), 0)),
                  pl.BlockSpec((RET_HEADS, RET_STEP, RET_DIM, RET_DIM), lambda n: (0, rev(n), 0, 0)),
                  pl.BlockSpec((step, RET_WIDTH), lambda n: (rev(n), 0)),
                  pl.BlockSpec((step, RET_DIM // 2), lambda n: (rev(n), 0)), pl.BlockSpec((step, RET_DIM // 2), lambda n: (rev(n), 0)),
                  full((RET_HEADS, BLK, BLK)), full((RET_HEADS, BLK, 1)), full((RET_HEADS, BLK, 1)), full((RET_HEADS, 1, 1))],
        out_specs=pl.BlockSpec((step, 4 * RET_WIDTH), lambda n: (rev(n), 0)),
        out_shape=jax.ShapeDtypeStruct((s, EVEN_IN), BF16),
        scratch_shapes=[pltpu.VMEM((RET_HEADS, RET_DIM, RET_DIM), F32)],
        compiler_params=_params("arbitrary"))(proj, o, states, dcat, cos, sin, intra, q_dec, k_dec, chunk_dec)


CONV_ROWS = 256
HALO = 16


def _conv_pieces(p, halo, first):
    gb, gc, u, gv = (p[:, i * CONV_WIDTH:(i + 1) * CONV_WIDTH] for i in range(4))
    cu = gc * u
    hcu = halo[:, CONV_WIDTH:2 * CONV_WIDTH] * halo[:, 2 * CONV_WIDTH:3 * CONV_WIDTH]
    hcu = jnp.where(first, 0.0, hcu)
    r1, r2 = hcu[HALO - 1:HALO], hcu[HALO - 2:HALO - 1]
    row = lax.broadcasted_iota(jnp.int32, cu.shape, 0)
    m1 = jnp.where(row == 0, r1, pltpu.roll(cu, 1, 0))
    m2 = jnp.where(row == 0, r2, jnp.where(row == 1, r1, pltpu.roll(cu, 2, 0)))
    return gb, gc, u, gv, cu, m1, m2


def _conv_fwd(proj, conv_w, cat):
    s = proj.shape[0]
    per = CONV_ROWS // HALO

    def body(p_ref, halo_ref, w_ref, cat_in, cat_ref):
        del cat_in
        first = pl.program_id(0) == 0
        gb, _, _, gv, cu, m1, m2 = _conv_pieces(p_ref[...].astype(F32), halo_ref[...].astype(F32), first)
        conv = w_ref[0:1, :] * m2 + w_ref[1:2, :] * m1 + w_ref[2:3, :] * cu
        cat_ref[...] = (gb * conv * (gv * _sigmoid(gv))).astype(BF16)

    return pl.pallas_call(
        body, name="conv_fwd", grid=(s // CONV_ROWS,),
        in_specs=[pl.BlockSpec((CONV_ROWS, 4 * CONV_WIDTH), lambda i: (i, 1)),
                  pl.BlockSpec((HALO, 4 * CONV_WIDTH), lambda i: (jnp.maximum(i * per - 1, 0), 1)),
                  pl.BlockSpec((3, CONV_WIDTH), lambda i: (0, 0)), ANY],
        out_specs=pl.BlockSpec((CONV_ROWS, CONV_WIDTH), lambda i: (i, 1)),
        out_shape=jax.ShapeDtypeStruct(cat.shape, cat.dtype), input_output_aliases={3: 0},
        compiler_params=_params("parallel"))(proj, proj, conv_w, cat)


def _conv_bwd(proj, dcat, conv_w, dproj):
    s = proj.shape[0]
    per = CONV_ROWS // HALO
    last_halo = s // HALO - 1
    nsteps = s // CONV_ROWS

    def body(p_ref, halo_ref, nxt_ref, dc_ref, dnxt_ref, w_ref, dp_in, dp_ref, dw_ref):
        del dp_in
        i = pl.program_id(0)
        gb, gc, u, gv, cu, m1, m2 = _conv_pieces(p_ref[...].astype(F32), halo_ref[...].astype(F32), i == 0)
        w0, w1, w2 = w_ref[0:1, :], w_ref[1:2, :], w_ref[2:3, :]
        conv = w0 * m2 + w1 * m1 + w2 * cu
        dco = dc_ref[...].astype(F32)
        sg = _sigmoid(gv)
        silu = gv * sg
        dgb = dco * conv * silu
        dco_gb = dco * gb
        dgv = dco_gb * conv * (sg * (1.0 + gv * (1.0 - sg)))
        dconv = dco_gb * silu
        nxt = nxt_ref[...].astype(F32)
        ngv = nxt[:, 3 * CONV_WIDTH:]
        dnext = dnxt_ref[...].astype(F32) * nxt[:, :CONV_WIDTH] * (ngv * _sigmoid(ngv))
        dnext = jnp.where(i == nsteps - 1, 0.0, dnext)
        n1, n2 = dnext[0:1], dnext[1:2]
        row = lax.broadcasted_iota(jnp.int32, dconv.shape, 0)
        p1 = jnp.where(row == CONV_ROWS - 1, n1, pltpu.roll(dconv, CONV_ROWS - 1, 0))
        p2 = jnp.where(row == CONV_ROWS - 1, n2, jnp.where(row == CONV_ROWS - 2, n1, pltpu.roll(dconv, CONV_ROWS - 2, 0)))
        dcu = w2 * dconv + w1 * p1 + w0 * p2
        dp_ref[...] = jnp.concatenate([dgb, dcu * u, dcu * gc, dgv], axis=1).astype(BF16)

        @pl.when(i == 0)
        def _():
            dw_ref[...] = jnp.zeros_like(dw_ref)

        taps = [jnp.sum(dconv * m, axis=0, keepdims=True) for m in (m2, m1, cu)]
        r8 = lax.broadcasted_iota(jnp.int32, dw_ref.shape, 0)
        dw_ref[...] += jnp.where(r8 == 0, taps[0], jnp.where(r8 == 1, taps[1], jnp.where(r8 == 2, taps[2], 0.0)))

    return pl.pallas_call(
        body, name="conv_bwd", grid=(nsteps,),
        in_specs=[pl.BlockSpec((CONV_ROWS, 4 * CONV_WIDTH), lambda i: (i, 1)),
                  pl.BlockSpec((HALO, 4 * CONV_WIDTH), lambda i: (jnp.maximum(i * per - 1, 0), 1)),
                  pl.BlockSpec((HALO, 4 * CONV_WIDTH), lambda i: (jnp.minimum((i + 1) * per, last_halo), 1)),
                  pl.BlockSpec((CONV_ROWS, CONV_WIDTH), lambda i: (i, 1)),
                  pl.BlockSpec((HALO, CONV_WIDTH), lambda i: (jnp.minimum((i + 1) * per, last_halo), 1)),
                  pl.BlockSpec((3, CONV_WIDTH), lambda i: (0, 0)), ANY],
        out_specs=[pl.BlockSpec((CONV_ROWS, 4 * CONV_WIDTH), lambda i: (i, 1)), pl.BlockSpec((8, CONV_WIDTH), lambda i: (0, 0))],
        out_shape=[jax.ShapeDtypeStruct(dproj.shape, dproj.dtype), jax.ShapeDtypeStruct((8, CONV_WIDTH), F32)],
        input_output_aliases={6: 0},
        compiler_params=_params("arbitrary"))(proj, proj, proj, dcat, dcat, conv_w, dproj)


GROUP_HEADS = Q_HEADS // KV_HEADS
GROUP_WIDTH = GROUP_HEADS * HEAD_DIM
SLAB = 2 * HEAD_DIM
KV_COL = ATTN_WIDTH // GROUP_WIDTH
GATE_COL = KV_COL + 1
ATTN_SCALE = HEAD_DIM ** -0.5
KV_ROWS = 1024


def _half_mask(shape, which):
    return (_lane(shape) // HEAD_DIM) == which


def _dup_head(slab, which):
    kept = jnp.where(_half_mask(slab.shape, which), slab, 0.0)
    return kept + pltpu.roll(kept, HEAD_DIM, 1)


def _stack_heads(x):
    parts = []
    for sl in range(GROUP_WIDTH // SLAB):
        slab = x[:, sl * SLAB:(sl + 1) * SLAB]
        parts += [jnp.where(_half_mask(slab.shape, e), slab, 0.0) for e in range(2)]
    return jnp.concatenate(parts, axis=0)


def _unstack_heads(y):
    slabs = []
    for sl in range(GROUP_WIDTH // SLAB):
        a, b = y[(2 * sl) * BLK:(2 * sl + 1) * BLK], y[(2 * sl + 1) * BLK:(2 * sl + 2) * BLK]
        slabs.append(jnp.where(_half_mask(a.shape, 0), a, b))
    return jnp.concatenate(slabs, axis=1)


def _q_prep(q, qw, cosf, sins, ones_bd):
    rstd = lax.rsqrt(_group_mean(q * q, ones_bd) + EPS)
    nrm = q * rstd
    y = nrm * qw
    return nrm, rstd, y * cosf + _partner64(y) * sins


def _band(tri_ref, n):
    own = tri_ref[...] > 0.5
    return own, jnp.where(jnp.logical_and(n == 0, jnp.logical_not(own)), -1e30, 0.0)


def _fold(pair, own):
    return jnp.where(own, pair[:, BLK:], pair[:, :BLK])


def _unfold(folded, own):
    return jnp.concatenate([jnp.where(own, 0.0, folded), jnp.where(own, folded, 0.0)], axis=1)


def _head_probs(raw_scores, sink, own, bias):
    sc = _fold(raw_scores, own) + bias
    m = jnp.maximum(jnp.max(sc, axis=-1, keepdims=True), sink)
    p = jnp.exp(sc - m)
    psink = jnp.exp(sink - m)
    inv = 1.0 / (jnp.sum(p, axis=-1, keepdims=True) + psink)
    return p * inv, psink * inv


def _k_prep(k, kw, cosf, sins, ones_bd):
    rstd = lax.rsqrt(_group_mean(k * k, ones_bd) + EPS)
    nrm = k * rstd
    y = nrm * kw
    return nrm, rstd, y * cosf + _partner64(y) * sins


def _qk_prep(proj, qw, kw, cos, sins, ones_q, ones_kv):
    s = proj.shape[0]
    rows = min(KV_ROWS, s)

    def body(p_ref, qw_ref, kw_ref, cos_ref, sin_ref, oq_ref, ok_ref, o_ref):
        j = pl.program_id(1)

        @pl.when(j < KV_COL)
        def _():
            cosf, sinf = _tile_lanes(cos_ref[...], 4), _tile_lanes(sin_ref[...], 4)
            roped = _q_prep(p_ref[...].astype(F32), qw_ref[...], cosf, sinf, oq_ref[...])[2]
            o_ref[...] = (roped * ATTN_SCALE).astype(BF16)

        @pl.when(j == KV_COL)
        def _():
            cosf, sinf = _tile_lanes(cos_ref[...], 2), _tile_lanes(sin_ref[...], 2)
            kr = _k_prep(p_ref[:, :KV_WIDTH].astype(F32), kw_ref[...], cosf, sinf, ok_ref[...])[2]
            o_ref[...] = jnp.concatenate([kr.astype(BF16), p_ref[:, KV_WIDTH:]], axis=1)

    full = lambda shape: pl.BlockSpec(shape, lambda i, j: (0,) * len(shape))
    tab = pl.BlockSpec((rows, SLAB), lambda i, j: (i, 0))
    blk = pl.BlockSpec((rows, GROUP_WIDTH), lambda i, j: (i, j))
    return pl.pallas_call(
        body, name="qk_prep", grid=(s // rows, KV_COL + 1),
        in_specs=[blk, full((1, GROUP_WIDTH)), full((1, KV_WIDTH)), tab, tab, full((GROUP_WIDTH, GROUP_WIDTH)),
                  full((KV_WIDTH, KV_WIDTH))],
        out_specs=blk, out_shape=jax.ShapeDtypeStruct((s, ATTN_WIDTH + 2 * KV_WIDTH), BF16),
        compiler_params=_params("parallel", "arbitrary"))(proj, qw, kw, cos, sins, ones_q, ones_kv)


def _keys_values(kc_ref, kp_ref, vc_ref, vp_ref, head):
    lanes = slice((head // 2) * SLAB, (head // 2 + 1) * SLAB)
    dup = lambda ref: _dup_head(ref[:, lanes].astype(F32), head % 2)
    return (jnp.concatenate([dup(kp_ref), dup(kc_ref)], axis=0).astype(BF16),
            jnp.concatenate([dup(vp_ref), dup(vc_ref)], axis=0).astype(BF16))


FWD_STEP_HEADS = 4
BWD_STEP_HEADS = 2


def _swa_specs(heads):
    kv_width = heads * HEAD_DIM
    prev = lambda n: jnp.maximum(n - 1, 0)
    kv = lambda col0, row: pl.BlockSpec((BLK, kv_width), lambda gs, n: (row(n), col0 // kv_width + gs))
    cur = lambda n: n
    full = lambda shape: pl.BlockSpec(shape, lambda gs, n: (0,) * len(shape))
    gate = lambda t: pl.BlockSpec((BLK, GROUP_WIDTH), lambda gs, n: (n, GATE_COL + heads * gs + t))
    return dict(
        sinks=pl.BlockSpec(memory_space=pltpu.SMEM), gates=[gate(t) for t in range(heads)],
        kc=kv(ATTN_WIDTH, cur), kp=kv(ATTN_WIDTH, prev), vc=kv(ATTN_WIDTH + KV_WIDTH, cur), vp=kv(ATTN_WIDTH + KV_WIDTH, prev),
        tri=full((BLK, BLK)), step=pl.BlockSpec((BLK, heads * GROUP_WIDTH), lambda gs, n: (n, gs)))


def _lower_triangle():
    return jnp.tril(jnp.ones((BLK, BLK), F32))


def _swa_fwd(qk, proj, sinks):
    s = proj.shape[0]
    nb = s // BLK
    heads = FWD_STEP_HEADS
    sp = _swa_specs(heads)

    def body(sink_ref, q_ref, kc_ref, kp_ref, vc_ref, vp_ref, *rest):
        gate_refs, (tri_ref, ag_ref, o_ref) = rest[:heads], rest[heads:]
        gs, n = pl.program_id(0), pl.program_id(1)
        own, bias = _band(tri_ref, n)
        for t, gate_ref in enumerate(gate_refs):
            cols = slice(t * GROUP_WIDTH, (t + 1) * GROUP_WIDTH)
            first_head = (heads * gs + t) * GROUP_HEADS
            kcat, vcat = _keys_values(kc_ref, kp_ref, vc_ref, vp_ref, t)
            scores = _dot_nt(_stack_heads(q_ref[:, cols]), kcat)
            probs = []
            for j in range(GROUP_HEADS):
                p, _ = _head_probs(scores[j * BLK:(j + 1) * BLK], sink_ref[first_head + j], own, bias)
                probs.append(_unfold(p, own).astype(BF16))
            o = _unstack_heads(_dot(jnp.concatenate(probs, axis=0), vcat))
            gate = gate_ref[...].astype(F32)
            o_ref[:, cols] = o.astype(BF16)
            ag_ref[:, cols] = (o * (gate * _sigmoid(gate))).astype(BF16)

    shp = jax.ShapeDtypeStruct((s, ATTN_WIDTH), BF16)
    return pl.pallas_call(
        body, name="swa_fwd", grid=(KV_HEADS // heads, nb),
        in_specs=[sp["sinks"], sp["step"], sp["kc"], sp["kp"], sp["vc"], sp["vp"], *sp["gates"], sp["tri"]],
        out_specs=[sp["step"], sp["step"]], out_shape=[shp, shp],
        compiler_params=_params("parallel", "arbitrary"),
    )(sinks, qk, qk, qk, qk, qk, *[proj] * heads, _lower_triangle())


def _swa_bwd(qk, proj, dag, sinks):
    s = proj.shape[0]
    nb = s // BLK
    heads = BWD_STEP_HEADS
    sp = _swa_specs(heads)

    def body(sink_ref, q_ref, kc_ref, kp_ref, vc_ref, vp_ref, *rest):
        gate_refs = rest[:heads]
        dag_ref, tri_ref, dq_ref, dkc_ref, dkp_ref, dvc_ref, dvp_ref, dsink_ref = rest[heads:]
        gs, n = pl.program_id(0), pl.program_id(1)
        own, bias = _band(tri_ref, n)

        @pl.when(n == 0)
        def _():
            dsink_ref[...] = jnp.zeros_like(dsink_ref)

        for t, gate_ref in enumerate(gate_refs):
            cols = slice(t * GROUP_WIDTH, (t + 1) * GROUP_WIDTH)
            first_head = (heads * gs + t) * GROUP_HEADS
            kcat, vcat = _keys_values(kc_ref, kp_ref, vc_ref, vp_ref, t)
            gate = gate_ref[...].astype(F32)
            do = dag_ref[:, cols].astype(F32) * (gate * _sigmoid(gate))
            q_stack = _stack_heads(q_ref[:, cols])
            do_stack = _stack_heads(do).astype(BF16)
            scores = _dot_nt(q_stack, kcat)
            dprobs = _dot_nt(do_stack, vcat)
            probs, dscores, dsinks = [], [], []
            for j in range(GROUP_HEADS):
                rows = slice(j * BLK, (j + 1) * BLK)
                p, psink = _head_probs(scores[rows], sink_ref[first_head + j], own, bias)
                dp = _fold(dprobs[rows], own)
                delta = jnp.sum(p * dp, axis=-1, keepdims=True)
                probs.append(_unfold(p, own).astype(BF16))
                dscores.append(_unfold(p * (dp - delta), own).astype(BF16))
                dsinks.append(-jnp.sum(psink * delta, axis=0, keepdims=True))
            ds = jnp.concatenate(dscores, axis=0)
            dk = _dot_tn(ds, q_stack)
            dv = _dot_tn(jnp.concatenate(probs, axis=0), do_stack)
            dk = dk + pltpu.roll(dk, HEAD_DIM, 1)
            dv = dv + pltpu.roll(dv, HEAD_DIM, 1)
            dkp_ref[t], dkc_ref[t] = dk[:BLK], dk[BLK:]
            dvp_ref[t], dvc_ref[t] = dv[:BLK], dv[BLK:]
            dq_ref[:, cols] = _unstack_heads(_dot(ds, kcat)).astype(BF16)
            r8 = lax.broadcasted_iota(jnp.int32, (8, SLAB), 0)
            upd = jnp.zeros((8, SLAB), F32)
            for j in range(GROUP_HEADS):
                upd = jnp.where(r8 == j, dsinks[j], upd)
            dsink_ref[t] += upd

    cur_out = pl.BlockSpec((heads, BLK, SLAB), lambda gs, n: (gs, n, 0))
    prev_out = pl.BlockSpec((heads, BLK, SLAB), lambda gs, n: (gs, (n + nb - 1) % nb, 0))
    kv_shape = jax.ShapeDtypeStruct((KV_HEADS, s, SLAB), F32)
    return pl.pallas_call(
        body, name="swa_bwd", grid=(KV_HEADS // heads, nb),
        in_specs=[sp["sinks"], sp["step"], sp["kc"], sp["kp"], sp["vc"], sp["vp"], *sp["gates"], sp["step"], sp["tri"]],
        out_specs=[sp["step"], cur_out, prev_out, cur_out, prev_out,
                   pl.BlockSpec((heads, 8, SLAB), lambda gs, n: (gs, 0, 0))],
        out_shape=[jax.ShapeDtypeStruct((s, ATTN_WIDTH), BF16), kv_shape, kv_shape, kv_shape, kv_shape,
                   jax.ShapeDtypeStruct((KV_HEADS, 8, SLAB), F32)],
        compiler_params=_params("parallel", "arbitrary"),
    )(sinks, qk, qk, qk, qk, qk, *[proj] * heads, dag, _lower_triangle())


def _swa_bwd_finish(proj, dqr, o, dag, dkc, dkp, dvc, dvp, qw, kw, cos, sins, ones_q, ones_kv):
    s = proj.shape[0]
    rows = min(KV_ROWS, s)
    n_q = KV_COL
    q_of = lambda j: jnp.clip(j - 1, 0, n_q - 1)
    gate_of = lambda j: jnp.clip(j - 1 - n_q, 0, n_q - 1)

    def body(kv_ref, p_ref, dqr_ref, o_ref, dag_ref, dkc_ref, dkp_ref, dvc_ref, dvp_ref, qw_ref, kw_ref, cos_ref, sin_ref,
             oq_ref, ok_ref, dp_ref, dqw_ref, dkw_ref):
        j, i = pl.program_id(0), pl.program_id(1)

        @pl.when(j == 0)
        def _():
            @pl.when(i == 0)
            def _():
                dkw_ref[...] = jnp.zeros_like(dkw_ref)

            def assemble(cur_ref, prv_ref):
                tot = [cur_ref[h] + prv_ref[h] for h in range(KV_HEADS)]
                first = _half_mask(tot[0].shape, 0)
                return jnp.concatenate([jnp.where(first, tot[0], tot[1]), jnp.where(first, tot[2], tot[3])], axis=1)

            dkr = assemble(dkc_ref, dkp_ref)
            dv = assemble(dvc_ref, dvp_ref)
            cosf, sinf = _tile_lanes(cos_ref[...], 2), _tile_lanes(sin_ref[...], 2)
            nrm, rstd, _ = _k_prep(kv_ref[:, :KV_WIDTH].astype(F32), kw_ref[...], cosf, sinf, ok_ref[...])
            dy = dkr * cosf + _partner64(dkr * sinf)
            dn = dy * kw_ref[...]
            dk = rstd * (dn - nrm * _group_mean(dn * nrm, ok_ref[...], passes=1))
            dp_ref[...] = jnp.concatenate([dk, dv], axis=1).astype(BF16)
            dkw_ref[...] += jnp.sum(dy * nrm, axis=0, keepdims=True)

        @pl.when(jnp.logical_and(j >= 1, j <= n_q))
        def _():
            @pl.when(i == 0)
            def _():
                dqw_ref[...] = jnp.zeros_like(dqw_ref)

            cosf, sinf = _tile_lanes(cos_ref[...], 4), _tile_lanes(sin_ref[...], 4)
            nrm, rstd, _ = _q_prep(p_ref[...].astype(F32), qw_ref[...], cosf, sinf, oq_ref[...])
            dq = dqr_ref[...].astype(F32) * ATTN_SCALE
            dy = dq * cosf + _partner64(dq * sinf)
            dn = dy * qw_ref[...]
            dp_ref[...] = (rstd * (dn - nrm * _group_mean(dn * nrm, oq_ref[...], passes=1))).astype(BF16)
            dqw_ref[0] += jnp.sum(dy * nrm, axis=0, keepdims=True)

        @pl.when(j > n_q)
        def _():
            gate = p_ref[...].astype(F32)
            sg = _sigmoid(gate)
            dp_ref[...] = (dag_ref[...].astype(F32) * o_ref[...].astype(F32) * (sg * (1.0 + gate * (1.0 - sg)))).astype(BF16)

    first_pass = lambda j, i: jnp.where(j == 0, i, 0)
    acc = pl.BlockSpec((KV_HEADS, rows, SLAB), lambda j, i: (0, first_pass(j, i), 0))
    full = lambda shape: pl.BlockSpec(shape, lambda j, i: (0,) * len(shape))
    tab = pl.BlockSpec((rows, SLAB), lambda j, i: (i, 0))
    out_col = lambda j: jnp.where(j == 0, KV_COL, jnp.where(j <= n_q, j - 1, j))
    return pl.pallas_call(
        body, name="swa_bwd_finish", grid=(2 * n_q + 1, s // rows),
        in_specs=[pl.BlockSpec((rows, GROUP_WIDTH), lambda j, i: (first_pass(j, i), KV_COL)),
                  pl.BlockSpec((rows, GROUP_WIDTH), lambda j, i: (jnp.where(j == 0, 0, i), jnp.where(j <= n_q, q_of(j), j))),
                  pl.BlockSpec((rows, GROUP_WIDTH), lambda j, i: (jnp.where(jnp.logical_and(j >= 1, j <= n_q), i, 0), q_of(j))),
                  pl.BlockSpec((rows, GROUP_WIDTH), lambda j, i: (jnp.where(j > n_q, i, 0), gate_of(j))),
                  pl.BlockSpec((rows, GROUP_WIDTH), lambda j, i: (jnp.where(j > n_q, i, 0), gate_of(j))),
                  acc, acc, acc, acc, full((1, GROUP_WIDTH)), full((1, KV_WIDTH)), tab, tab,
                  full((GROUP_WIDTH, GROUP_WIDTH)), full((KV_WIDTH, KV_WIDTH))],
        out_specs=[pl.BlockSpec((rows, GROUP_WIDTH), lambda j, i: (i, out_col(j))),
                   pl.BlockSpec((1, 8, GROUP_WIDTH), lambda j, i: (q_of(j), 0, 0)), pl.BlockSpec((8, KV_WIDTH), lambda j, i: (0, 0))],
        out_shape=[jax.ShapeDtypeStruct((s, ODD_IN), BF16), jax.ShapeDtypeStruct((n_q, 8, GROUP_WIDTH), F32),
                   jax.ShapeDtypeStruct((8, KV_WIDTH), F32)],
        compiler_params=_params("arbitrary", "arbitrary"),
    )(proj, proj, dqr, o, dag, dkc, dkp, dvc, dvp, qw, kw, cos, sins, ones_q, ones_kv)


def _place():
    x, y, c = lax.axis_index("x"), lax.axis_index("y"), lax.axis_index("c")
    return x, y, c


OTHER_CHIPS = ((1, 0), (0, 1), (1, 1))


def _half_rows(ref, half, rows):
    return ref.at[pl.ds(pl.multiple_of(half * (rows // 2), 8), rows // 2)]


DMA_CHUNK_BYTES = 1 << 20
BF16_TILE_ROWS = 16


def _n_chunks(ref):
    rows = ref.shape[-2]
    nbytes = math.prod(ref.shape) * jnp.dtype(ref.dtype).itemsize
    n = 1
    while 2 * n * DMA_CHUNK_BYTES <= nbytes and rows % (2 * n * BF16_TILE_ROWS) == 0:
        n *= 2
    return n


def _row_chunk(ref, k, n):
    rows = ref.shape[-2] // n
    return ref.at[pl.ds(k * rows, rows)] if len(ref.shape) == 2 else ref.at[:, pl.ds(k * rows, rows)]


def _push(src, dst, send_sem, recv_sem, device_id):
    n = _n_chunks(src)
    for k in range(n):
        pltpu.make_async_remote_copy(src_ref=_row_chunk(src, k, n), dst_ref=_row_chunk(dst, k, n), send_sem=send_sem,
                                     recv_sem=recv_sem, device_id=device_id, device_id_type=MESH).start()
    return pltpu.make_async_remote_copy(src_ref=src, dst_ref=dst, send_sem=send_sem, recv_sem=recv_sem,
                                        device_id=device_id, device_id_type=MESH)


HBM = pl.BlockSpec(memory_space=pltpu.HBM)
SEM = pl.BlockSpec(memory_space=pltpu.SEMAPHORE)
SPLIT_COPY_EFFECT = pltpu.SideEffectType.DATAFLOW_SIDE_EFFECTING


def _in_hbm(a):
    return pltpu.with_memory_space_constraint(a, pltpu.HBM)


def _start_copies(name, arrays, plan, n_copies, after=None):
    n = len(arrays)

    def body(*refs):
        send_sem, recv_sem = refs[-n - 3], refs[-n - 2]
        for k, (src, dst, peer) in enumerate(plan(refs[:n])):
            _push(src, dst, send_sem.at[k], recv_sem.at[k], peer)
        refs[-1][...] = jnp.zeros_like(refs[-1])

    dma = pltpu.SemaphoreType.DMA((n_copies,))
    outs = pl.pallas_call(
        body, name=name,
        out_shape=(dma, dma, *[pltpu.HBM(a.shape, a.dtype) for a in arrays], jax.ShapeDtypeStruct((8, 128), F32)),
        in_specs=[HBM] * n + ([ANY] if after is not None else []),
        out_specs=(SEM, SEM, *[HBM] * n, pl.BlockSpec(memory_space=pltpu.VMEM)),
        input_output_aliases={i: i + 2 for i in range(n)},
        compiler_params=pltpu.CompilerParams(has_side_effects=SPLIT_COPY_EFFECT),
    )(*[_in_hbm(a) for a in arrays], *((after,) if after is not None else ()))
    return outs[0], outs[1], list(outs[2:2 + n]), outs[-1]


def _wait_copies(name, send_sem, recv_sem, arrays, plan, after):
    n = len(arrays)
    after = list(after) if isinstance(after, (list, tuple)) else [after]

    def body(*refs):
        send_ref, recv_ref = refs[n], refs[n + 1]
        for k, (src, dst, peer) in enumerate(plan(refs[:n])):
            cp = pltpu.make_async_remote_copy(src_ref=src, dst_ref=dst, send_sem=send_ref.at[k], recv_sem=recv_ref.at[k],
                                              device_id=peer, device_id_type=MESH)
            cp.wait_send()
            cp.wait_recv()

    return list(pl.pallas_call(
        body, name=name, out_shape=tuple(pltpu.HBM(a.shape, a.dtype) for a in arrays),
        in_specs=[HBM] * n + [SEM, SEM] + [ANY] * len(after), out_specs=tuple([HBM] * n),
        input_output_aliases={i: i for i in range(n)},
        compiler_params=pltpu.CompilerParams(has_side_effects=SPLIT_COPY_EFFECT),
    )(*arrays, send_sem, recv_sem, *after))


def _gather_region(full, kind, chip, half=None):
    if kind == "whole":
        return full.at[chip]
    if kind == "col":
        rows, width = full.shape[0], full.shape[1] // N_CHIPS
        piece = full.at[:, pl.ds(pl.multiple_of(chip * width, 128), width)]
    else:
        rows = full.shape[0] // N_CHIPS
        piece = full.at[pl.ds(pl.multiple_of(chip * rows, BF16_TILE_ROWS), rows)]
    return piece if half is None else _half_rows(piece, half, rows)


def _gather_plan(kinds):
    def plan(fulls):
        x, y, c = _place()
        copies = []
        for fx, fy in OTHER_CHIPS:
            for full, kind in zip(fulls, kinds):
                mine = _gather_region(full, kind, 2 * x + y, c)
                copies.append((mine, mine, (x ^ fx, y ^ fy, c)))
        return copies

    return plan


def _gather_start(name, fulls, kinds, after=None):
    return _start_copies(name, list(fulls), _gather_plan(kinds), 3 * len(kinds), after)


def _pass_on_plan(kinds):
    split = [i for i, kind in enumerate(kinds) if kind != "whole"]

    def plan(fulls):
        x, y, c = _place()
        copies = []
        for fx, fy in OTHER_CHIPS:
            chip = 2 * (x ^ fx) + (y ^ fy)
            for i in split:
                landed = _gather_region(fulls[i], kinds[i], chip, c)
                copies.append((landed, landed, (x, y, 1 - c)))
        return copies

    return plan, 3 * len(split)


def _gather_arrive(name, started, kinds, after):
    send_sem, recv_sem, fulls, _ = started
    fulls = _wait_copies(name + "_wait", send_sem, recv_sem, fulls, _gather_plan(kinds), after)
    plan, count = _pass_on_plan(kinds)
    return _start_copies(name + "_pass_on", fulls, plan, count)


def _gather_done(name, passed_on, kinds, after):
    send_sem, recv_sem, fulls, _ = passed_on
    return _wait_copies(name + "_pass_on_wait", send_sem, recv_sem, fulls, _pass_on_plan(kinds)[0], after)


def _allreduce_small(v):
    def body(v_ref, out_ref, buf, send_sems, recv_sems):
        x, y, c = _place()
        me = 4 * x + 2 * y + c
        buf[me] = v_ref[...]
        copies = []
        for r in range(1, N_DEV):
            peer = (x ^ (r >> 2), y ^ ((r >> 1) & 1), c ^ (r & 1))
            cp = pltpu.make_async_remote_copy(src_ref=v_ref, dst_ref=buf.at[me], send_sem=send_sems.at[r - 1],
                                              recv_sem=recv_sems.at[r - 1], device_id=peer, device_id_type=MESH)
            cp.start()
            copies.append(cp)
        for cp in copies:
            cp.wait_recv()
        for cp in copies:
            cp.wait_send()
        total = buf[0]
        for d in range(1, N_DEV):
            total = total + buf[d]
        out_ref[...] = total

    vm = pl.BlockSpec(memory_space=pltpu.VMEM)
    return pl.pallas_call(
        body, name="allreduce_small", in_specs=[vm], out_specs=vm, out_shape=jax.ShapeDtypeStruct(v.shape, v.dtype),
        scratch_shapes=[pltpu.VMEM((N_DEV,) + v.shape, v.dtype), pltpu.SemaphoreType.DMA((N_DEV - 1,)),
                        pltpu.SemaphoreType.DMA((N_DEV - 1,))],
        compiler_params=pltpu.CompilerParams(has_side_effects=True),
    )(v)


def _exchange_halves(grads, name):
    n = len(grads)

    def body(*refs):
        g, theirs = refs[:n], refs[n:2 * n]
        send_sem, recv_sem = refs[2 * n:]
        x, y, c = _place()
        copies = []
        for i in range(n):
            half = g[i].shape[1] // 2
            src = g[i].at[:, pl.ds(pl.multiple_of((1 - c) * half, BF16_TILE_ROWS), half)]
            copies.append(_push(src, theirs[i], send_sem.at[i], recv_sem.at[i], (x, y, 1 - c)))
        for cp in copies:
            cp.wait_recv()
            cp.wait_send()

    dma = pltpu.SemaphoreType.DMA
    return pl.pallas_call(
        body, name=name, in_specs=[ANY] * n, out_specs=[ANY] * n,
        out_shape=[jax.ShapeDtypeStruct((a.shape[0], a.shape[1] // 2, a.shape[2]), a.dtype) for a in grads],
        scratch_shapes=[dma((n,)), dma((n,))],
        compiler_params=pltpu.CompilerParams(has_side_effects=True),
    )(*grads)


def _pair_sum(g, theirs, core, name):
    pieces, half, cols = theirs.shape
    tr = min(half, 256)
    per = half // tr

    def body(core_ref, g_ref, t_ref, o_ref):
        del core_ref
        o_ref[...] = (g_ref[...].astype(F32) + t_ref[...].astype(F32)).astype(BF16)

    spec = pl.BlockSpec((1, tr, cols), lambda p, i, core_ref: (p, i, 0))
    return pl.pallas_call(
        body, name=name, out_shape=jax.ShapeDtypeStruct(theirs.shape, BF16),
        grid_spec=pltpu.PrefetchScalarGridSpec(
            num_scalar_prefetch=1, grid=(pieces, per),
            in_specs=[pl.BlockSpec((1, tr, cols), lambda p, i, core_ref: (p, core_ref[0] * per + i, 0)), spec],
            out_specs=spec),
        compiler_params=_params("parallel", "parallel"))(core, g, theirs)


def _scatter_plan(n):
    def plan(refs):
        parts, stacks = refs[:n], refs[n:]
        x, y, c = _place()
        copies = []
        for fx, fy in OTHER_CHIPS:
            chip = 2 * (x ^ fx) + (y ^ fy)
            for part, stack in zip(parts, stacks):
                if part.shape[0] == N_CHIPS:
                    piece = part.at[chip]
                else:
                    width = part.shape[2] // N_CHIPS
                    piece = part.at[0].at[:, pl.ds(pl.multiple_of(chip * width, 128), width)]
                copies.append((piece, stack.at[2 * x + y], (x ^ fx, y ^ fy, c)))
        return copies

    return plan


def _scatter_start(name, parts, after=None):
    def landing(a):
        return (N_CHIPS, a.shape[1], a.shape[2] if a.shape[0] == N_CHIPS else a.shape[2] // N_CHIPS)

    stacks = [lax.empty(landing(a), a.dtype) for a in parts]
    return _start_copies(name, list(parts) + stacks, _scatter_plan(len(parts)), 3 * len(parts), after)


def _scatter_finish(name, started, after):
    send_sem, recv_sem, arrays, _ = started
    n = len(arrays) // 2
    arrays = _wait_copies(name + "_wait", send_sem, recv_sem, arrays, _scatter_plan(n), after)
    return arrays[:n], arrays[n:]


def _sum_chips(part, stack, place, name):
    _, r, c = stack.shape
    tr = 256
    per = r // tr

    def body(place_ref, own_ref, a_ref, b_ref, c_ref, o_ref):
        del place_ref
        total = own_ref[0].astype(F32)
        for ref in (a_ref, b_ref, c_ref):
            total = total + ref[0].astype(F32)
        o_ref[...] = total

    if part.shape[0] == N_CHIPS:
        own = pl.BlockSpec((1, tr, c), lambda i, pr: (pr[1], i, 0))
    else:
        own = pl.BlockSpec((1, tr, c), lambda i, pr: (0, i, pr[1]))
    other = lambda flip: pl.BlockSpec((1, tr, c), lambda i, pr: (pr[1] ^ flip, i, 0))
    return pl.pallas_call(
        body, name=name, out_shape=jax.ShapeDtypeStruct((2 * r, c), F32),
        grid_spec=pltpu.PrefetchScalarGridSpec(
            num_scalar_prefetch=1, grid=(per,), in_specs=[own, other(2), other(1), other(3)],
            out_specs=pl.BlockSpec((tr, c), lambda i, pr: (pr[0] * per + i, 0))),
        compiler_params=_params("parallel"))(place, part, stack, stack, stack)


def _own_half_plan(fulls):
    x, y, c = _place()
    return [(mine, mine, (x, y, 1 - c)) for mine in (_half_rows(full, c, full.shape[0]) for full in fulls)]


MM = dict(tm=2048, tn=1024, tk=2048)
MM_LONG_K = dict(tm=1024, tn=1024, tk=4096)


def _local_step(x, target, ev_norm_w, q_norm_w, k_norm_w, sinks, own_first, weights_first, late_arrived, weights_late, emit):
    s = x.shape[0]
    cos_r, sin_r = _rope_tables(s, RET_DIM)
    cos_a, sin_a = _rope_tables(s, HEAD_DIM)
    cos_a = np.tile(cos_a, (1, 4))
    sins_a = np.tile(np.concatenate([-sin_a, sin_a], axis=1), (1, 2))
    tables = _retention_tables()
    ones_q, ones_kv = _block_diag_mean(GROUP_WIDTH), _block_diag_mean(KV_WIDTH)
    qw_g = jnp.tile(q_norm_w, (1, GROUP_WIDTH // HEAD_DIM))
    kw_kv = jnp.tile(k_norm_w, (1, KV_WIDTH // HEAD_DIM))
    sinks1 = sinks.reshape(Q_HEADS)

    own_w_in0, own_block, start_token = own_first
    h0 = _rmsnorm(x, ev_norm_w, "norm0", after=start_token)
    shifted = dict(shift=own_block * (own_w_in0.shape[1] // MM["tn"]), total=EVEN_IN // MM["tn"], tm=MM["tm"], tn=MM["tn"],
                   out_dtype=BF16)
    own_blocks = own_w_in0.shape[1] // MM["tn"]
    proj0 = _mm_shifted(h0, own_w_in0, b_shifted=False, first=0, count=own_blocks, name="proj0_own", **shifted)
    w_in0, conv_w, od_norm_w, token = weights_first(proj0)
    proj0 = _mm_shifted(h0, w_in0, b_shifted=True, first=own_blocks, count=shifted["total"] - own_blocks, name="proj0_rest",
                        into=proj0, after=token, **shifted)
    o_ret, cat, states = _retention_fwd(proj0, cos_r, sin_r, tables, after=late_arrived(proj0))
    cat = _conv_fwd(proj0, conv_w, cat)
    w_out0, w_in1, w_out1 = weights_late(cat)

    def residual_and_norm(prod, x_ref, w_ref, x1_ref, h1_ref):
        x1v = x_ref[...] + prod
        x1_ref[...] = x1v
        rstd = lax.rsqrt(jnp.mean(x1v * x1v, axis=-1, keepdims=True) + EPS)
        h1_ref[...] = (x1v * rstd * w_ref[...]).astype(BF16)

    def residual_and_loss(prod, x1_ref, t_ref, dyb_ref, sq_ref):
        diff = (x1_ref[...] + prod) - t_ref[...]
        dyb_ref[...] = (diff * (1.0 / D_MODEL)).astype(BF16)

        @pl.when(pl.program_id(0) == 0)
        def _():
            sq_ref[...] = jnp.zeros_like(sq_ref)

        sq_ref[...] += jnp.sum(jnp.sum(diff * diff, axis=1, keepdims=True), axis=0, keepdims=True)

    act = lambda dt: jax.ShapeDtypeStruct((s, D_MODEL), dt)
    x1, h1 = _mm_rows(cat, w_out0, [x], [od_norm_w], [act(F32), act(BF16)], residual_and_norm, tm=min(s, 512), name="out0")
    proj1 = _mm(h1, w_in1, mode="nn", out_dtype=BF16, name="proj1", tm=2048, tn=1536, tk=2048)
    qk = _qk_prep(proj1, qw_g, kw_kv, cos_a, sins_a, ones_q, ones_kv)
    ag, o_att = _swa_fwd(qk, proj1, sinks1)
    dy_b, sq = _mm_rows(ag, w_out1, [x1, target], [], [act(BF16), jax.ShapeDtypeStruct((8, 128), F32)],
                        residual_and_loss, tm=min(s, 512), name="out1")

    g_w_out1 = _mm(ag, dy_b, mode="tn", out_dtype=BF16, name="g_w_out1", **MM_LONG_K)
    dag = _mm(dy_b, w_out1, mode="nt", out_dtype=BF16, name="d_ag", **MM)
    dqr, dkc, dkp, dvc, dvp, dsink = _swa_bwd(qk, proj1, dag, sinks1)
    dproj1, dqw, dkw = _swa_bwd_finish(proj1, dqr, o_att, dag, dkc, dkp, dvc, dvp, qw_g, kw_kv, cos_a, sins_a, ones_q, ones_kv)
    g_w_in1 = _mm(h1, dproj1, mode="tn", out_dtype=BF16, name="g_w_in1", tm=1024, tn=768, tk=4096)
    dh1 = _mm(dproj1, w_in1, mode="nt", out_dtype=BF16, name="d_h1", tm=1024, tn=1024, tk=ODD_IN)
    dx1_b, g_norm1 = _rmsnorm_bwd(x1, od_norm_w, dh1, dy_b, "norm1_bwd", BF16)

    g_w_out0 = _mm(cat, dx1_b, mode="tn", out_dtype=BF16, name="g_w_out0", **MM_LONG_K)
    token = emit("upper", (("od_w_in", g_w_in1, "col"), ("od_w_out", g_w_out1, "row"), ("ev_w_out", g_w_out0, "row")))
    dcat = _mm(dx1_b, w_out0, mode="nt", out_dtype=BF16, name="d_cat", after=token, **MM)
    dproj0 = _retention_bwd(proj0, o_ret, states, dcat, cos_r, sin_r, tables)
    dproj0, g_conv = _conv_bwd(proj0, dcat, conv_w, dproj0)
    g_w_in0 = _mm(h0, dproj0, mode="tn", out_dtype=BF16, name="g_w_in0", **MM_LONG_K)
    token = emit("in0", (("ev_w_in", g_w_in0, "col"),))
    dh0 = _mm(dproj0, w_in0, mode="nt", out_dtype=BF16, name="d_h0", after=token, **MM_LONG_K)
    grad_x, g_norm0 = _rmsnorm_bwd(x, ev_norm_w, dh0, dx1_b, "norm0_bwd", F32)

    g_qw = dqw[:, 0, :].reshape(Q_HEADS, HEAD_DIM).sum(axis=0)
    g_kw = dkw[0].reshape(KV_HEADS, HEAD_DIM).sum(axis=0)
    g_sinks = dsink[:, :, 0].reshape(Q_HEADS)
    small = dict(ev_norm=g_norm0[0], od_norm=g_norm1[0], conv=g_conv[:3], qw=g_qw, kw=g_kw, sinks=g_sinks)
    return sq[0, 0], grad_x, small


def _pack_small_grads(small, sq):
    pad = lambda v: jnp.pad(v, (0, D_MODEL - v.shape[0]))
    tail = pad(jnp.concatenate([small["qw"], small["kw"], small["sinks"]]))
    rows = [small["ev_norm"], small["od_norm"]] + [pad(small["conv"][t]) for t in range(3)] + [tail, pad(sq.reshape(1))]
    rows += [jnp.zeros((D_MODEL,), F32)] * (8 - len(rows))
    return jnp.stack(rows)


class _ReduceScatter:
    def __init__(self, place):
        self.place = place
        self.started = []

    def send(self, tag, grads):
        pieces = [g[None] if kind == "col" else g.reshape(N_CHIPS, g.shape[0] // N_CHIPS, g.shape[1]) for _, g, kind in grads]
        theirs = _exchange_halves(pieces, "exchange_halves_" + tag)
        parts = [_pair_sum(g, t, self.place[:1], "pair_sum_" + nm) for g, t, (nm, _, _) in zip(pieces, theirs, grads)]
        started = _scatter_start("scatter_" + tag, parts)
        self.started.append((tag, [nm for nm, _, _ in grads], started))
        return started[3]

    def sum_up(self, after):
        self.sharing = {}
        for tag, group, started in self.started:
            parts, stacks = _scatter_finish("scatter_" + tag, started, after)
            sums = [_sum_chips(p, s, self.place, "chip_sum_" + nm) for p, s, nm in zip(parts, stacks, group)]
            self.sharing[tag] = (group, _start_copies("share_" + tag, sums, _own_half_plan, len(sums)))

    def result(self, tag, after):
        group, (send_sem, recv_sem, sums, _) = self.sharing[tag]
        return dict(zip(group, _wait_copies("share_" + tag + "_wait", send_sem, recv_sem, sums, _own_half_plan, after)))


def kernel(x, ev_norm_w, ev_w_in, ev_conv_w, ev_w_out, od_norm_w, od_w_in, od_q_norm_w, od_k_norm_w, od_sinks, od_w_out, loss_target, m_ev_norm_w, m_ev_w_in, m_ev_conv_w, m_ev_w_out, m_od_norm_w, m_od_w_in, m_od_q_norm_w, m_od_k_norm_w, m_od_sinks, m_od_w_out, v_ev_norm_w, v_ev_w_in, v_ev_conv_w, v_ev_w_out, v_od_norm_w, v_od_w_in, v_od_q_norm_w, v_od_k_norm_w, v_od_sinks, v_od_w_out):
    my_chip = 2 * lax.axis_index("x") + lax.axis_index("y")
    place = jnp.stack([lax.axis_index("c"), my_chip]).astype(jnp.int32)
    shard_w = D_MODEL // N_CHIPS
    conv_shard = CONV_WIDTH // N_CHIPS

    small_in = jnp.zeros((8, shard_w), F32)
    small_in = small_in.at[0].set(od_norm_w[0]).at[1:4, :conv_shard].set(ev_conv_w[0])
    small_in = lax.dynamic_update_slice(jnp.zeros((N_CHIPS, 8, shard_w), F32), small_in[None], (my_chip, 0, 0))
    chip = place[1:]
    first_kinds, late_kinds = ("col", "whole"), ("row", "col", "row")
    w_in0_own_place, w_in0_shard = _cast_into_gathered(ev_w_in[0], "col", chip, "cast_w_in0", keep_shard=True)
    first = _gather_start("gather_first", [w_in0_own_place, small_in], first_kinds)
    late_own = [_cast_into_gathered(ev_w_out[0], "row", chip, "cast_w_out0"),
                _cast_into_gathered(od_w_in[0], "col", chip, "cast_w_in1"),
                _cast_into_gathered(od_w_out[0], "row", chip, "cast_w_out1")]
    late = []

    def weights_first(after):
        passed_on = _gather_arrive("gather_first", first, first_kinds, [after] + late_own)
        w_in0, small_all = _gather_done("gather_first", passed_on, first_kinds, passed_on[3])
        late.append(_gather_start("gather_late", late_own, late_kinds, after=w_in0))
        od_norm_full = small_all[:, 0, :].reshape(1, D_MODEL)
        conv_full = jnp.transpose(small_all[:, 1:4, :conv_shard], (1, 0, 2)).reshape(3, CONV_WIDTH)
        return w_in0, conv_full, od_norm_full, late[0][3]

    def late_arrived(after):
        late.append(_gather_arrive("gather_late", late[0], late_kinds, after))
        return late[1][3]

    def weights_late(after):
        return _gather_done("gather_late", late[1], late_kinds, after)

    reduce_scatter = _ReduceScatter(place)
    sq, grad_x, small = _local_step(x[0], loss_target[0], ev_norm_w, od_q_norm_w, od_k_norm_w, od_sinks,
                                    (w_in0_shard, chip, first[3]), weights_first, late_arrived, weights_late,
                                    reduce_scatter.send)

    reduce_scatter.sum_up(grad_x)
    upd = {}
    upper = reduce_scatter.result("upper", grad_x)
    upd["od_w_in"] = _adamw(od_w_in[0], upper["od_w_in"], m_od_w_in[0], v_od_w_in[0], "adamw_od_w_in")
    upd["od_w_out"] = _adamw(od_w_out[0], upper["od_w_out"], m_od_w_out[0], v_od_w_out[0], "adamw_od_w_out")
    upd["ev_w_out"] = _adamw(ev_w_out[0], upper["ev_w_out"], m_ev_w_out[0], v_ev_w_out[0], "adamw_ev_w_out")
    g_ev_w_in = reduce_scatter.result("in0", upd["ev_w_out"][1])["ev_w_in"]
    upd["ev_w_in"] = _adamw(ev_w_in[0], g_ev_w_in, m_ev_w_in[0], v_ev_w_in[0], "adamw_ev_w_in")
    tot = _allreduce_small(_pack_small_grads(small, sq))
    loss = 0.5 * tot[6, 0] / D_MODEL
    g_ev_norm = tot[0:1]
    g_od_norm = lax.dynamic_slice(tot, (1, my_chip * shard_w), (1, shard_w))
    g_conv = lax.dynamic_slice(tot, (2, my_chip * conv_shard), (3, conv_shard))
    g_qw, g_kw, g_sinks = tot[5:6, 0:HEAD_DIM], tot[5:6, HEAD_DIM:2 * HEAD_DIM], tot[5:6, 2 * HEAD_DIM:2 * HEAD_DIM + Q_HEADS]

    smalls = (("ev_norm_w", ev_norm_w, g_ev_norm, m_ev_norm_w, v_ev_norm_w),
              ("ev_conv_w", ev_conv_w, g_conv, m_ev_conv_w, v_ev_conv_w),
              ("od_norm_w", od_norm_w, g_od_norm, m_od_norm_w, v_od_norm_w),
              ("od_q_norm_w", od_q_norm_w, g_qw, m_od_q_norm_w, v_od_q_norm_w),
              ("od_k_norm_w", od_k_norm_w, g_kw, m_od_k_norm_w, v_od_k_norm_w),
              ("od_sinks", od_sinks, g_sinks, m_od_sinks, v_od_sinks))
    sizes = [w.size for _, w, _, _, _ in smalls]
    padded = 8 * 128 * math.ceil(sum(sizes) / (8 * 128))
    pack = lambda arrs, fill: jnp.concatenate(
        [a.reshape(-1) for a in arrs] + [jnp.full((padded - sum(sizes),), fill, F32)]).reshape(8, padded // 8)
    packed = _adamw(pack([w for _, w, _, _, _ in smalls], 0.0), pack([g for _, _, g, _, _ in smalls], 0.0),
                    pack([m for _, _, _, m, _ in smalls], 0.0), pack([v for _, _, _, _, v in smalls], 1.0), "adamw_small")
    offs = [sum(sizes[:i]) for i in range(len(sizes))]
    for (nm, w, _, _, _), off, size in zip(smalls, offs, sizes):
        upd[nm] = tuple(p.reshape(-1)[off:off + size].reshape(w.shape) for p in packed)
    for nm in ("ev_w_in", "ev_w_out", "od_w_in", "od_w_out"):
        upd[nm] = tuple(u[None] for u in upd[nm])
    order = ("ev_norm_w", "ev_w_in", "ev_conv_w", "ev_w_out", "od_norm_w", "od_w_in", "od_q_norm_w", "od_k_norm_w", "od_sinks", "od_w_out")
    return (loss, grad_x[None], *[upd[nm][0] for nm in order], *[upd[nm][1] for nm in order],
            *[upd[nm][2] for nm in order], *[upd[nm][3] for nm in order])
```

```python
import math

import jax
import jax.numpy as jnp
import numpy as np
from jax import lax
from jax.experimental import pallas as pl
from jax.experimental.pallas import tpu as pltpu

F32 = jnp.float32
BF16 = jnp.bfloat16

D_MODEL = 2048
RET_HEADS = 4
RET_DIM = 256
RET_WIDTH = 1024
CONV_WIDTH = 1024
EVEN_IN = 8192
Q_HEADS = 32
HEAD_DIM = 64
KV_HEADS = 4
KV_WIDTH = 256
ATTN_WIDTH = 2048
ODD_IN = 4608
BLK = 128
ROPE_THETA = 10000.0
EPS = 1e-6
ADAM_LR = 0.001
ADAM_B1 = 0.9
ADAM_B2 = 0.999
ADAM_EPS = 1e-08
ADAM_WD = 0.01
ADAM_STEP = 10
N_CHIPS = 4
N_DEV = 8
VMEM_LIMIT_BYTES = 56 * 1024 * 1024
MESH = pl.DeviceIdType.MESH
ANY = pl.BlockSpec(memory_space=pl.ANY)


def _params(*sem):
    return pltpu.CompilerParams(dimension_semantics=sem, vmem_limit_bytes=VMEM_LIMIT_BYTES)


def _dot(a, b):
    return jnp.dot(a, b, preferred_element_type=F32)


def _dot_nt(a, b):
    return lax.dot_general(a, b, (((1,), (1,)), ((), ())), preferred_element_type=F32)


def _dot_tn(a, b):
    return lax.dot_general(a, b, (((0,), (0,)), ((), ())), preferred_element_type=F32)


def _sigmoid(x):
    return 1.0 / (1.0 + jnp.exp(-x))


def _mm(a, b, *, mode, tm, tn, tk, out_dtype, name, add=None, after=None):
    if mode == "nn":
        (m, k), n = a.shape, b.shape[1]
    elif mode == "nt":
        (m, k), n = a.shape, b.shape[0]
    else:
        (k, m), n = a.shape, b.shape[1]
    tm, tn, tk = min(tm, m), min(tn, n), min(tk, k)
    assert m % tm == 0 and n % tn == 0 and k % tk == 0, (name, m, n, k)
    nk = k // tk
    dot = {"nn": _dot, "nt": _dot_nt, "tn": _dot_tn}[mode]
    a_spec = (pl.BlockSpec((tk, tm), lambda i, j, kk: (kk, i)) if mode == "tn"
              else pl.BlockSpec((tm, tk), lambda i, j, kk: (i, kk)))
    b_spec = (pl.BlockSpec((tn, tk), lambda i, j, kk: (j, kk)) if mode == "nt"
              else pl.BlockSpec((tk, tn), lambda i, j, kk: (kk, j)))
    o_spec = pl.BlockSpec((tm, tn), lambda i, j, kk: (i, j))
    has_add = add is not None

    def body(*refs):
        a_ref, b_ref = refs[0], refs[1]
        add_ref = refs[2] if has_add else None
        o_ref, acc_ref = refs[-2], refs[-1]
        p = dot(a_ref[...], b_ref[...])

        def finish(total):
            if has_add:
                total = total + add_ref[...].astype(F32)
            o_ref[...] = total.astype(out_dtype)

        if nk == 1:
            finish(p)
        else:
            kk = pl.program_id(2)

            @pl.when(kk == 0)
            def _():
                acc_ref[...] = p

            @pl.when(jnp.logical_and(kk > 0, kk < nk - 1))
            def _():
                acc_ref[...] += p

            @pl.when(kk == nk - 1)
            def _():
                finish(acc_ref[...] + p)

    in_specs = [a_spec, b_spec] + ([o_spec] if has_add else []) + ([ANY] if after is not None else [])
    args = (a, b) + ((add,) if has_add else ()) + ((after,) if after is not None else ())
    return pl.pallas_call(
        body, name=name, grid=(m // tm, n // tn, nk), in_specs=in_specs, out_specs=o_spec,
        out_shape=jax.ShapeDtypeStruct((m, n), out_dtype),
        scratch_shapes=[pltpu.VMEM((tm, tn) if nk > 1 else (8, 128), F32)],
        compiler_params=_params("parallel", "parallel", "arbitrary"),
    )(*args)


def _mm_shifted(a, b, shift, *, b_shifted, first, count, total, tm, tn, out_dtype, name, into=None, after=None):
    m, k = a.shape
    tm = min(tm, m)
    assert m % tm == 0
    col = lambda j, shift_ref: (shift_ref[0] + first + j) % total
    extra = [arr for arr in (into, after) if arr is not None]

    def body(shift_ref, a_ref, b_ref, *rest):
        del shift_ref
        rest[-1][...] = _dot(a_ref[...], b_ref[...]).astype(out_dtype)

    return pl.pallas_call(
        body, name=name, out_shape=jax.ShapeDtypeStruct((m, total * tn), out_dtype),
        grid_spec=pltpu.PrefetchScalarGridSpec(
            num_scalar_prefetch=1, grid=(m // tm, count),
            in_specs=[pl.BlockSpec((tm, k), lambda i, j, s: (i, 0)),
                      pl.BlockSpec((k, tn), (lambda i, j, s: (0, col(j, s))) if b_shifted else (lambda i, j, s: (0, j)))]
            + [ANY] * len(extra),
            out_specs=pl.BlockSpec((tm, tn), lambda i, j, s: (i, col(j, s)))),
        input_output_aliases={3: 0} if into is not None else {},
        compiler_params=_params("parallel", "arbitrary"))(shift, a, b, *extra)


def _mm_rows(a, b, rows_in, vecs_in, out_shapes, epilogue, *, tm, name):
    m, k = a.shape
    n = b.shape[1]
    assert m % tm == 0
    row = pl.BlockSpec((tm, n), lambda i: (i, 0))

    def body(a_ref, b_ref, *rest):
        epilogue(_dot(a_ref[...], b_ref[...]), *rest)

    out_specs = [row if tuple(s.shape) == (m, n) else pl.BlockSpec(s.shape, lambda i: (0, 0)) for s in out_shapes]
    return pl.pallas_call(
        body, name=name, grid=(m // tm,),
        in_specs=[pl.BlockSpec((tm, k), lambda i: (i, 0)), pl.BlockSpec((k, n), lambda i: (0, 0))] + [row] * len(rows_in)
        + [pl.BlockSpec((1, n), lambda i: (0, 0))] * len(vecs_in),
        out_specs=out_specs, out_shape=out_shapes, compiler_params=_params("arbitrary"),
    )(a, b, *rows_in, *vecs_in)


def _cast_into_gathered(w, kind, chip, name, keep_shard=False):
    r, c = w.shape
    tr = min(r, 512)
    per = r // tr

    def body(chip_ref, w_ref, *outs):
        del chip_ref
        for o_ref in outs:
            o_ref[...] = w_ref[...].astype(BF16)

    if kind == "col":
        shape, out_map = (r, N_CHIPS * c), (lambda i, chip_ref: (i, chip_ref[0]))
    else:
        shape, out_map = (N_CHIPS * r, c), (lambda i, chip_ref: (chip_ref[0] * per + i, 0))
    plain = pl.BlockSpec((tr, c), lambda i, chip_ref: (i, 0))
    out = pl.pallas_call(
        body, name=name,
        out_shape=[jax.ShapeDtypeStruct(shape, BF16)] + ([jax.ShapeDtypeStruct((r, c), BF16)] if keep_shard else []),
        grid_spec=pltpu.PrefetchScalarGridSpec(
            num_scalar_prefetch=1, grid=(per,), in_specs=[plain],
            out_specs=[pl.BlockSpec((tr, c), out_map)] + ([plain] if keep_shard else [])),
        compiler_params=_params("parallel"))(chip, w)
    return out if keep_shard else out[0]


NORM_ROWS = 512


def _rmsnorm(x, w, name, after=None):
    s, d = x.shape
    tr = NORM_ROWS

    def body(x_ref, w_ref, *rest):
        xv = x_ref[...]
        rstd = lax.rsqrt(jnp.mean(xv * xv, axis=-1, keepdims=True) + EPS)
        rest[-1][...] = (xv * rstd * w_ref[...]).astype(BF16)

    return pl.pallas_call(
        body, name=name, grid=(s // tr,),
        in_specs=[pl.BlockSpec((tr, d), lambda i: (i, 0)), pl.BlockSpec((1, d), lambda i: (0, 0))]
        + ([ANY] if after is not None else []),
        out_specs=pl.BlockSpec((tr, d), lambda i: (i, 0)),
        out_shape=jax.ShapeDtypeStruct((s, d), BF16), compiler_params=_params("parallel"),
    )(x, w, *((after,) if after is not None else ()))


def _rmsnorm_bwd(x, w, dh, dres, name, out_dtype):
    s, d = x.shape
    tr = NORM_ROWS

    def body(x_ref, w_ref, dh_ref, dres_ref, dx_ref, dw_ref):
        xv = x_ref[...]
        rstd = lax.rsqrt(jnp.mean(xv * xv, axis=-1, keepdims=True) + EPS)
        nrm = xv * rstd
        dhv = dh_ref[...].astype(F32)
        dn = dhv * w_ref[...]
        dx = dres_ref[...].astype(F32) + rstd * (dn - nrm * jnp.mean(dn * nrm, axis=-1, keepdims=True))
        dx_ref[...] = dx.astype(out_dtype)

        @pl.when(pl.program_id(0) == 0)
        def _():
            dw_ref[...] = jnp.zeros_like(dw_ref)

        dw_ref[...] += jnp.sum(dhv * nrm, axis=0, keepdims=True)

    row = pl.BlockSpec((tr, d), lambda i: (i, 0))
    return pl.pallas_call(
        body, name=name, grid=(s // tr,),
        in_specs=[row, pl.BlockSpec((1, d), lambda i: (0, 0)), row, row],
        out_specs=[row, pl.BlockSpec((8, d), lambda i: (0, 0))],
        out_shape=[jax.ShapeDtypeStruct((s, d), out_dtype), jax.ShapeDtypeStruct((8, d), F32)],
        compiler_params=_params("arbitrary"))(x, w, dh, dres)


def _adamw(w, g, m, v, name):
    r, c = w.shape
    tr = min(r, 256)
    assert r % tr == 0

    def body(w_ref, g_ref, m_ref, v_ref, g_out, d_ref, nm_ref, nv_ref):
        gv = g_ref[...]
        g_out[...] = gv
        nm = ADAM_B1 * m_ref[...] + (1.0 - ADAM_B1) * gv
        nv = ADAM_B2 * v_ref[...] + (1.0 - ADAM_B2) * (gv * gv)
        m_hat = nm / (1.0 - ADAM_B1 ** ADAM_STEP)
        v_hat = nv / (1.0 - ADAM_B2 ** ADAM_STEP)
        d_ref[...] = -ADAM_LR * (m_hat / (jnp.sqrt(v_hat) + ADAM_EPS) + ADAM_WD * w_ref[...])
        nm_ref[...] = nm
        nv_ref[...] = nv

    spec = pl.BlockSpec((tr, c), lambda i: (i, 0))
    shp = jax.ShapeDtypeStruct((r, c), F32)
    return pl.pallas_call(body, name=name, grid=(r // tr,), in_specs=[spec] * 4, out_specs=[spec] * 4,
                          out_shape=[shp] * 4, compiler_params=_params("parallel"))(w, g, m, v)


def _rope_tables(s, dim):
    inv = (1.0 / (ROPE_THETA ** (np.arange(0, dim, 2, dtype=np.float64) / dim))).astype(np.float32)
    ang = (np.arange(s, dtype=np.float32)[:, None] * inv[None, :]).astype(np.float64)
    return np.cos(ang).astype(np.float32), np.sin(ang).astype(np.float32)


def _rope_half(x, cos, sin):
    h = x.shape[1] // 2
    x1, x2 = x[:, :h], x[:, h:]
    return jnp.concatenate([x1 * cos - x2 * sin, x2 * cos + x1 * sin], axis=1)


def _unrope_half(dy, cos, sin):
    h = dy.shape[1] // 2
    d1, d2 = dy[:, :h], dy[:, h:]
    return jnp.concatenate([d1 * cos + d2 * sin, d2 * cos - d1 * sin], axis=1)


def _lane(shape):
    return lax.broadcasted_iota(jnp.int32, shape, 1)


def _partner64(x):
    w = x.shape[1]
    first = (_lane(x.shape) % HEAD_DIM) < (HEAD_DIM // 2)
    return jnp.where(first, pltpu.roll(x, w - HEAD_DIM // 2, 1), pltpu.roll(x, HEAD_DIM // 2, 1))


def _tile_lanes(t, reps):
    return t if reps == 1 else jnp.concatenate([t] * reps, axis=1)


def _group_mean(x, ones_bd, passes=2):
    hi = x.astype(BF16)
    if passes == 1:
        return _dot(hi, ones_bd)
    lo = (x - hi.astype(F32)).astype(BF16)
    return _dot(hi, ones_bd) + _dot(lo, ones_bd)


def _block_diag_mean(width):
    idx = jnp.arange(width) // HEAD_DIM
    return jnp.where(idx[:, None] == idx[None, :], 1.0 / HEAD_DIM, 0.0).astype(BF16)


RET_STEP = 4


def _retention_tables():
    h = RET_HEADS
    log_g = jnp.log(1.0 - 2.0 ** (-5.0 - jnp.arange(h, dtype=F32)))
    idx = jnp.arange(BLK, dtype=F32)
    diff = idx[:, None] - idx[None, :]
    intra = jnp.where(diff >= 0, jnp.exp(log_g[:, None, None] * jnp.maximum(diff, 0.0)), 0.0).astype(F32)
    q_dec = jnp.exp(log_g[:, None] * (idx[None, :] + 1.0)).astype(F32)[:, :, None]
    k_dec = jnp.exp(log_g[:, None] * (BLK - 1.0 - idx[None, :])).astype(F32)[:, :, None]
    chunk_dec = jnp.exp(log_g * BLK).astype(F32)[:, None, None]
    return intra, q_dec, k_dec, chunk_dec


def _retention_fwd(proj, cos, sin, tables, after=None):
    s = proj.shape[0]
    nc = s // BLK
    intra, q_dec, k_dec, chunk_dec = tables

    def body(p_ref, cos_ref, sin_ref, in_ref, qd_ref, kd_ref, cd_ref, *rest):
        o_ref, cat_ref, st_ref, state = rest[-4:]

        @pl.when(pl.program_id(0) == 0)
        def _():
            state[...] = jnp.zeros_like(state)

        for c in range(RET_STEP):
            rows = slice(c * BLK, (c + 1) * BLK)
            cosv, sinv = cos_ref[rows, :], sin_ref[rows, :]
            for h in range(RET_HEADS):
                c0 = h * RET_DIM
                q = p_ref[rows, c0:c0 + RET_DIM].astype(F32)
                k = p_ref[rows, RET_WIDTH + c0:RET_WIDTH + c0 + RET_DIM].astype(F32)
                v = p_ref[rows, 2 * RET_WIDTH + c0:2 * RET_WIDTH + c0 + RET_DIM]
                g = p_ref[rows, 3 * RET_WIDTH + c0:3 * RET_WIDTH + c0 + RET_DIM].astype(F32)
                qb = _rope_half(q, cosv, sinv).astype(BF16)
                kr = _rope_half(k, cosv, sinv) * (RET_DIM ** -0.5)
                kb = kr.astype(BF16)
                scores = _dot_nt(qb, kb) * in_ref[h]
                inner = _dot(scores.astype(BF16), v)
                prev = state[h]
                prev_b = prev.astype(BF16)
                st_ref[h, c] = prev_b
                o = inner + _dot(qb, prev_b) * qd_ref[h]
                o_ref[rows, c0:c0 + RET_DIM] = o
                rstd = lax.rsqrt(jnp.mean(o * o, axis=-1, keepdims=True) + EPS)
                cat_ref[rows, c0:c0 + RET_DIM] = (o * rstd * (g * _sigmoid(g))).astype(BF16)
                state[h] = cd_ref[h] * prev + _dot_tn((kr * kd_ref[h]).astype(BF16), v)

    full = lambda shape: pl.BlockSpec(shape, lambda n: (0,) * len(shape))
    step = RET_STEP * BLK
    return pl.pallas_call(
        body, name="retention_fwd", grid=(nc // RET_STEP,),
        in_specs=[pl.BlockSpec((step, 4 * RET_WIDTH), lambda n: (n, 0)),
                  pl.BlockSpec((step, RET_DIM // 2), lambda n: (n, 0)), pl.BlockSpec((step, RET_DIM // 2), lambda n: (n, 0)),
                  full((RET_HEADS, BLK, BLK)), full((RET_HEADS, BLK, 1)), full((RET_HEADS, BLK, 1)), full((RET_HEADS, 1, 1))]
        + ([ANY] if after is not None else []),
        out_specs=[pl.BlockSpec((step, RET_WIDTH), lambda n: (n, 0)), pl.BlockSpec((step, RET_WIDTH), lambda n: (n, 0)),
                   pl.BlockSpec((RET_HEADS, RET_STEP, RET_DIM, RET_DIM), lambda n: (0, n, 0, 0))],
        out_shape=[jax.ShapeDtypeStruct((s, RET_WIDTH), F32), jax.ShapeDtypeStruct((s, D_MODEL), BF16),
                   jax.ShapeDtypeStruct((RET_HEADS, nc, RET_DIM, RET_DIM), BF16)],
        scratch_shapes=[pltpu.VMEM((RET_HEADS, RET_DIM, RET_DIM), F32)],
        compiler_params=_params("arbitrary"),
    )(proj, cos, sin, intra, q_dec, k_dec, chunk_dec, *((after,) if after is not None else ()))


def _retention_bwd(proj, o, states, dcat, cos, sin, tables):
    s = proj.shape[0]
    nc = s // BLK
    intra, q_dec, k_dec, chunk_dec = tables

    def body(p_ref, o_ref, st_ref, dc_ref, cos_ref, sin_ref, in_ref, qd_ref, kd_ref, cd_ref, dp_ref, dstate):
        @pl.when(pl.program_id(0) == 0)
        def _():
            dstate[...] = jnp.zeros_like(dstate)

        for c in reversed(range(RET_STEP)):
            rows = slice(c * BLK, (c + 1) * BLK)
            cosv, sinv = cos_ref[rows, :], sin_ref[rows, :]
            for h in range(RET_HEADS):
                c0 = h * RET_DIM
                q = p_ref[rows, c0:c0 + RET_DIM].astype(F32)
                k = p_ref[rows, RET_WIDTH + c0:RET_WIDTH + c0 + RET_DIM].astype(F32)
                v = p_ref[rows, 2 * RET_WIDTH + c0:2 * RET_WIDTH + c0 + RET_DIM]
                g = p_ref[rows, 3 * RET_WIDTH + c0:3 * RET_WIDTH + c0 + RET_DIM].astype(F32)
                o = o_ref[rows, c0:c0 + RET_DIM]
                dc = dc_ref[rows, c0:c0 + RET_DIM].astype(F32)
                rstd = lax.rsqrt(jnp.mean(o * o, axis=-1, keepdims=True) + EPS)
                nrm = o * rstd
                sg = _sigmoid(g)
                dg = dc * nrm * (sg * (1.0 + g * (1.0 - sg)))
                dn = dc * (g * sg)
                do = rstd * (dn - nrm * jnp.mean(dn * nrm, axis=-1, keepdims=True))
                qb = _rope_half(q, cosv, sinv).astype(BF16)
                kr = _rope_half(k, cosv, sinv) * (RET_DIM ** -0.5)
                kb = kr.astype(BF16)
                mask = in_ref[h]
                qd, kd = qd_ref[h], kd_ref[h]
                prev_b = st_ref[h, c]
                dnext = dstate[h]
                dnext_b = dnext.astype(BF16)
                att = (_dot_nt(qb, kb) * mask).astype(BF16)
                do_b = do.astype(BF16)
                doq = (do * qd).astype(BF16)
                dv = _dot_tn(att, do_b) + _dot((kr * kd).astype(BF16), dnext_b)
                ds = (_dot_nt(do_b, v) * mask).astype(BF16)
                dqr = _dot(ds, kb) + _dot_nt(doq, prev_b)
                dkr = _dot_tn(ds, qb) + _dot_nt(v, dnext_b) * kd
                dstate[h] = cd_ref[h] * dnext + _dot_tn(qb, doq)
                dq = _unrope_half(dqr, cosv, sinv)
                dk = _unrope_half(dkr * (RET_DIM ** -0.5), cosv, sinv)
                dp_ref[rows, c0:c0 + RET_DIM] = dq.astype(BF16)
                dp_ref[rows, RET_WIDTH + c0:RET_WIDTH + c0 + RET_DIM] = dk.astype(BF16)
                dp_ref[rows, 2 * RET_WIDTH + c0:2 * RET_WIDTH + c0 + RET_DIM] = dv.astype(BF16)
                dp_ref[rows, 3 * RET_WIDTH + c0:3 * RET_WIDTH + c0 + RET_DIM] = dg.astype(BF16)

    steps = nc // RET_STEP
    rev = lambda n: steps - 1 - n
    full = lambda shape: pl.BlockSpec(shape, lambda n: (0,) * len(shape))
    step = RET_STEP * BLK
    return pl.pallas_call(
        body, name="retention_bwd", grid=(steps,),
        in_specs=[pl.BlockSpec((step, 4 * RET_WIDTH), lambda n: (rev(n), 0)),
                  pl.BlockSpec((step, RET_WIDTH), lambda n: (rev(n), 0)),
                  pl.BlockSpec((RET_HEADS, RET_STEP, RET_DIM, RET_DIM), lambda n: (0, rev(n), 0, 0)),
                  pl.BlockSpec((step, RET_WIDTH), lambda n: (rev(n), 0)),
                  pl.BlockSpec((step, RET_DIM // 2), lambda n: (rev(n), 0)), pl.BlockSpec((step, RET_DIM // 2), lambda n: (rev(n), 0)),
                  full((RET_HEADS, BLK, BLK)), full((RET_HEADS, BLK, 1)), full((RET_HEADS, BLK, 1)), full((RET_HEADS, 1, 1))],
        out_specs=pl.BlockSpec((step, 4 * RET_WIDTH), lambda n: (rev(n), 0)),
        out_shape=jax.ShapeDtypeStruct((s, EVEN_IN), BF16),
        scratch_shapes=[pltpu.VMEM((RET_HEADS, RET_DIM, RET_DIM), F32)],
        compiler_params=_params("arbitrary"))(proj, o, states, dcat, cos, sin, intra, q_dec, k_dec, chunk_dec)


CONV_ROWS = 256
HALO = 16


def _conv_pieces(p, halo, first):
    gb, gc, u, gv = (p[:, i * CONV_WIDTH:(i + 1) * CONV_WIDTH] for i in range(4))
    cu = gc * u
    hcu = halo[:, CONV_WIDTH:2 * CONV_WIDTH] * halo[:, 2 * CONV_WIDTH:3 * CONV_WIDTH]
    hcu = jnp.where(first, 0.0, hcu)
    r1, r2 = hcu[HALO - 1:HALO], hcu[HALO - 2:HALO - 1]
    row = lax.broadcasted_iota(jnp.int32, cu.shape, 0)
    m1 = jnp.where(row == 0, r1, pltpu.roll(cu, 1, 0))
    m2 = jnp.where(row == 0, r2, jnp.where(row == 1, r1, pltpu.roll(cu, 2, 0)))
    return gb, gc, u, gv, cu, m1, m2


def _conv_fwd(proj, conv_w, cat):
    s = proj.shape[0]
    per = CONV_ROWS // HALO

    def body(p_ref, halo_ref, w_ref, cat_in, cat_ref):
        del cat_in
        first = pl.program_id(0) == 0
        gb, _, _, gv, cu, m1, m2 = _conv_pieces(p_ref[...].astype(F32), halo_ref[...].astype(F32), first)
        conv = w_ref[0:1, :] * m2 + w_ref[1:2, :] * m1 + w_ref[2:3, :] * cu
        cat_ref[...] = (gb * conv * (gv * _sigmoid(gv))).astype(BF16)

    return pl.pallas_call(
        body, name="conv_fwd", grid=(s // CONV_ROWS,),
        in_specs=[pl.BlockSpec((CONV_ROWS, 4 * CONV_WIDTH), lambda i: (i, 1)),
                  pl.BlockSpec((HALO, 4 * CONV_WIDTH), lambda i: (jnp.maximum(i * per - 1, 0), 1)),
                  pl.BlockSpec((3, CONV_WIDTH), lambda i: (0, 0)), ANY],
        out_specs=pl.BlockSpec((CONV_ROWS, CONV_WIDTH), lambda i: (i, 1)),
        out_shape=jax.ShapeDtypeStruct(cat.shape, cat.dtype), input_output_aliases={3: 0},
        compiler_params=_params("parallel"))(proj, proj, conv_w, cat)


def _conv_bwd(proj, dcat, conv_w, dproj):
    s = proj.shape[0]
    per = CONV_ROWS // HALO
    last_halo = s // HALO - 1
    nsteps = s // CONV_ROWS

    def body(p_ref, halo_ref, nxt_ref, dc_ref, dnxt_ref, w_ref, dp_in, dp_ref, dw_ref):
        del dp_in
        i = pl.program_id(0)
        gb, gc, u, gv, cu, m1, m2 = _conv_pieces(p_ref[...].astype(F32), halo_ref[...].astype(F32), i == 0)
        w0, w1, w2 = w_ref[0:1, :], w_ref[1:2, :], w_ref[2:3, :]
        conv = w0 * m2 + w1 * m1 + w2 * cu
        dco = dc_ref[...].astype(F32)
        sg = _sigmoid(gv)
        silu = gv * sg
        dgb = dco * conv * silu
        dco_gb = dco * gb
        dgv = dco_gb * conv * (sg * (1.0 + gv * (1.0 - sg)))
        dconv = dco_gb * silu
        nxt = nxt_ref[...].astype(F32)
        ngv = nxt[:, 3 * CONV_WIDTH:]
        dnext = dnxt_ref[...].astype(F32) * nxt[:, :CONV_WIDTH] * (ngv * _sigmoid(ngv))
        dnext = jnp.where(i == nsteps - 1, 0.0, dnext)
        n1, n2 = dnext[0:1], dnext[1:2]
        row = lax.broadcasted_iota(jnp.int32, dconv.shape, 0)
        p1 = jnp.where(row == CONV_ROWS - 1, n1, pltpu.roll(dconv, CONV_ROWS - 1, 0))
        p2 = jnp.where(row == CONV_ROWS - 1, n2, jnp.where(row == CONV_ROWS - 2, n1, pltpu.roll(dconv, CONV_ROWS - 2, 0)))
        dcu = w2 * dconv + w1 * p1 + w0 * p2
        dp_ref[...] = jnp.concatenate([dgb, dcu * u, dcu * gc, dgv], axis=1).astype(BF16)

        @pl.when(i == 0)
        def _():
            dw_ref[...] = jnp.zeros_like(dw_ref)

        taps = [jnp.sum(dconv * m, axis=0, keepdims=True) for m in (m2, m1, cu)]
        r8 = lax.broadcasted_iota(jnp.int32, dw_ref.shape, 0)
        dw_ref[...] += jnp.where(r8 == 0, taps[0], jnp.where(r8 == 1, taps[1], jnp.where(r8 == 2, taps[2], 0.0)))

    return pl.pallas_call(
        body, name="conv_bwd", grid=(nsteps,),
        in_specs=[pl.BlockSpec((CONV_ROWS, 4 * CONV_WIDTH), lambda i: (i, 1)),
                  pl.BlockSpec((HALO, 4 * CONV_WIDTH), lambda i: (jnp.maximum(i * per - 1, 0), 1)),
                  pl.BlockSpec((HALO, 4 * CONV_WIDTH), lambda i: (jnp.minimum((i + 1) * per, last_halo), 1)),
                  pl.BlockSpec((CONV_ROWS, CONV_WIDTH), lambda i: (i, 1)),
                  pl.BlockSpec((HALO, CONV_WIDTH), lambda i: (jnp.minimum((i + 1) * per, last_halo), 1)),
                  pl.BlockSpec((3, CONV_WIDTH), lambda i: (0, 0)), ANY],
        out_specs=[pl.BlockSpec((CONV_ROWS, 4 * CONV_WIDTH), lambda i: (i, 1)), pl.BlockSpec((8, CONV_WIDTH), lambda i: (0, 0))],
        out_shape=[jax.ShapeDtypeStruct(dproj.shape, dproj.dtype), jax.ShapeDtypeStruct((8, CONV_WIDTH), F32)],
        input_output_aliases={6: 0},
        compiler_params=_params("arbitrary"))(proj, proj, proj, dcat, dcat, conv_w, dproj)


GROUP_HEADS = Q_HEADS // KV_HEADS
GROUP_WIDTH = GROUP_HEADS * HEAD_DIM
SLAB = 2 * HEAD_DIM
KV_COL = ATTN_WIDTH // GROUP_WIDTH
GATE_COL = KV_COL + 1
ATTN_SCALE = HEAD_DIM ** -0.5
KV_ROWS = 1024


def _half_mask(shape, which):
    return (_lane(shape) // HEAD_DIM) == which


def _dup_head(slab, which):
    kept = jnp.where(_half_mask(slab.shape, which), slab, 0.0)
    return kept + pltpu.roll(kept, HEAD_DIM, 1)


def _stack_heads(x):
    parts = []
    for sl in range(GROUP_WIDTH // SLAB):
        slab = x[:, sl * SLAB:(sl + 1) * SLAB]
        parts += [jnp.where(_half_mask(slab.shape, e), slab, 0.0) for e in range(2)]
    return jnp.concatenate(parts, axis=0)


def _unstack_heads(y):
    slabs = []
    for sl in range(GROUP_WIDTH // SLAB):
        a, b = y[(2 * sl) * BLK:(2 * sl + 1) * BLK], y[(2 * sl + 1) * BLK:(2 * sl + 2) * BLK]
        slabs.append(jnp.where(_half_mask(a.shape, 0), a, b))
    return jnp.concatenate(slabs, axis=1)


def _q_prep(q, qw, cosf, sins, ones_bd):
    rstd = lax.rsqrt(_group_mean(q * q, ones_bd) + EPS)
    nrm = q * rstd
    y = nrm * qw
    return nrm, rstd, y * cosf + _partner64(y) * sins


def _band(tri_ref, n):
    own = tri_ref[...] > 0.5
    return own, jnp.where(jnp.logical_and(n == 0, jnp.logical_not(own)), -1e30, 0.0)


def _fold(pair, own):
    return jnp.where(own, pair[:, BLK:], pair[:, :BLK])


def _unfold(folded, own):
    return jnp.concatenate([jnp.where(own, 0.0, folded), jnp.where(own, folded, 0.0)], axis=1)


def _head_probs(raw_scores, sink, own, bias):
    sc = _fold(raw_scores, own) + bias
    m = jnp.maximum(jnp.max(sc, axis=-1, keepdims=True), sink)
    p = jnp.exp(sc - m)
    psink = jnp.exp(sink - m)
    inv = 1.0 / (jnp.sum(p, axis=-1, keepdims=True) + psink)
    return p * inv, psink * inv


def _k_prep(k, kw, cosf, sins, ones_bd):
    rstd = lax.rsqrt(_group_mean(k * k, ones_bd) + EPS)
    nrm = k * rstd
    y = nrm * kw
    return nrm, rstd, y * cosf + _partner64(y) * sins


def _qk_prep(proj, qw, kw, cos, sins, ones_q, ones_kv):
    s = proj.shape[0]
    rows = min(KV_ROWS, s)

    def body(p_ref, qw_ref, kw_ref, cos_ref, sin_ref, oq_ref, ok_ref, o_ref):
        j = pl.program_id(1)

        @pl.when(j < KV_COL)
        def _():
            cosf, sinf = _tile_lanes(cos_ref[...], 4), _tile_lanes(sin_ref[...], 4)
            roped = _q_prep(p_ref[...].astype(F32), qw_ref[...], cosf, sinf, oq_ref[...])[2]
            o_ref[...] = (roped * ATTN_SCALE).astype(BF16)

        @pl.when(j == KV_COL)
        def _():
            cosf, sinf = _tile_lanes(cos_ref[...], 2), _tile_lanes(sin_ref[...], 2)
            kr = _k_prep(p_ref[:, :KV_WIDTH].astype(F32), kw_ref[...], cosf, sinf, ok_ref[...])[2]
            o_ref[...] = jnp.concatenate([kr.astype(BF16), p_ref[:, KV_WIDTH:]], axis=1)

    full = lambda shape: pl.BlockSpec(shape, lambda i, j: (0,) * len(shape))
    tab = pl.BlockSpec((rows, SLAB), lambda i, j: (i, 0))
    blk = pl.BlockSpec((rows, GROUP_WIDTH), lambda i, j: (i, j))
    return pl.pallas_call(
        body, name="qk_prep", grid=(s // rows, KV_COL + 1),
        in_specs=[blk, full((1, GROUP_WIDTH)), full((1, KV_WIDTH)), tab, tab, full((GROUP_WIDTH, GROUP_WIDTH)),
                  full((KV_WIDTH, KV_WIDTH))],
        out_specs=blk, out_shape=jax.ShapeDtypeStruct((s, ATTN_WIDTH + 2 * KV_WIDTH), BF16),
        compiler_params=_params("parallel", "arbitrary"))(proj, qw, kw, cos, sins, ones_q, ones_kv)


def _keys_values(kc_ref, kp_ref, vc_ref, vp_ref, head):
    lanes = slice((head // 2) * SLAB, (head // 2 + 1) * SLAB)
    dup = lambda ref: _dup_head(ref[:, lanes].astype(F32), head % 2)
    return (jnp.concatenate([dup(kp_ref), dup(kc_ref)], axis=0).astype(BF16),
            jnp.concatenate([dup(vp_ref), dup(vc_ref)], axis=0).astype(BF16))


FWD_STEP_HEADS = 4
BWD_STEP_HEADS = 2


def _swa_specs(heads):
    kv_width = heads * HEAD_DIM
    prev = lambda n: jnp.maximum(n - 1, 0)
    kv = lambda col0, row: pl.BlockSpec((BLK, kv_width), lambda gs, n: (row(n), col0 // kv_width + gs))
    cur = lambda n: n
    full = lambda shape: pl.BlockSpec(shape, lambda gs, n: (0,) * len(shape))
    gate = lambda t: pl.BlockSpec((BLK, GROUP_WIDTH), lambda gs, n: (n, GATE_COL + heads * gs + t))
    return dict(
        sinks=pl.BlockSpec(memory_space=pltpu.SMEM), gates=[gate(t) for t in range(heads)],
        kc=kv(ATTN_WIDTH, cur), kp=kv(ATTN_WIDTH, prev), vc=kv(ATTN_WIDTH + KV_WIDTH, cur), vp=kv(ATTN_WIDTH + KV_WIDTH, prev),
        tri=full((BLK, BLK)), step=pl.BlockSpec((BLK, heads * GROUP_WIDTH), lambda gs, n: (n, gs)))


def _lower_triangle():
    return jnp.tril(jnp.ones((BLK, BLK), F32))


def _swa_fwd(qk, proj, sinks):
    s = proj.shape[0]
    nb = s // BLK
    heads = FWD_STEP_HEADS
    sp = _swa_specs(heads)

    def body(sink_ref, q_ref, kc_ref, kp_ref, vc_ref, vp_ref, *rest):
        gate_refs, (tri_ref, ag_ref, o_ref) = rest[:heads], rest[heads:]
        gs, n = pl.program_id(0), pl.program_id(1)
        own, bias = _band(tri_ref, n)
        for t, gate_ref in enumerate(gate_refs):
            cols = slice(t * GROUP_WIDTH, (t + 1) * GROUP_WIDTH)
            first_head = (heads * gs + t) * GROUP_HEADS
            kcat, vcat = _keys_values(kc_ref, kp_ref, vc_ref, vp_ref, t)
            scores = _dot_nt(_stack_heads(q_ref[:, cols]), kcat)
            probs = []
            for j in range(GROUP_HEADS):
                p, _ = _head_probs(scores[j * BLK:(j + 1) * BLK], sink_ref[first_head + j], own, bias)
                probs.append(_unfold(p, own).astype(BF16))
            o = _unstack_heads(_dot(jnp.concatenate(probs, axis=0), vcat))
            gate = gate_ref[...].astype(F32)
            o_ref[:, cols] = o.astype(BF16)
            ag_ref[:, cols] = (o * (gate * _sigmoid(gate))).astype(BF16)

    shp = jax.ShapeDtypeStruct((s, ATTN_WIDTH), BF16)
    return pl.pallas_call(
        body, name="swa_fwd", grid=(KV_HEADS // heads, nb),
        in_specs=[sp["sinks"], sp["step"], sp["kc"], sp["kp"], sp["vc"], sp["vp"], *sp["gates"], sp["tri"]],
        out_specs=[sp["step"], sp["step"]], out_shape=[shp, shp],
        compiler_params=_params("parallel", "arbitrary"),
    )(sinks, qk, qk, qk, qk, qk, *[proj] * heads, _lower_triangle())


def _swa_bwd(qk, proj, dag, sinks):
    s = proj.shape[0]
    nb = s // BLK
    heads = BWD_STEP_HEADS
    sp = _swa_specs(heads)

    def body(sink_ref, q_ref, kc_ref, kp_ref, vc_ref, vp_ref, *rest):
        gate_refs = rest[:heads]
        dag_ref, tri_ref, dq_ref, dkc_ref, dkp_ref, dvc_ref, dvp_ref, dsink_ref = rest[heads:]
        gs, n = pl.program_id(0), pl.program_id(1)
        own, bias = _band(tri_ref, n)

        @pl.when(n == 0)
        def _():
            dsink_ref[...] = jnp.zeros_like(dsink_ref)

        for t, gate_ref in enumerate(gate_refs):
            cols = slice(t * GROUP_WIDTH, (t + 1) * GROUP_WIDTH)
            first_head = (heads * gs + t) * GROUP_HEADS
            kcat, vcat = _keys_values(kc_ref, kp_ref, vc_ref, vp_ref, t)
            gate = gate_ref[...].astype(F32)
            do = dag_ref[:, cols].astype(F32) * (gate * _sigmoid(gate))
            q_stack = _stack_heads(q_ref[:, cols])
            do_stack = _stack_heads(do).astype(BF16)
            scores = _dot_nt(q_stack, kcat)
            dprobs = _dot_nt(do_stack, vcat)
            probs, dscores, dsinks = [], [], []
            for j in range(GROUP_HEADS):
                rows = slice(j * BLK, (j + 1) * BLK)
                p, psink = _head_probs(scores[rows], sink_ref[first_head + j], own, bias)
                dp = _fold(dprobs[rows], own)
                delta = jnp.sum(p * dp, axis=-1, keepdims=True)
                probs.append(_unfold(p, own).astype(BF16))
                dscores.append(_unfold(p * (dp - delta), own).astype(BF16))
                dsinks.append(-jnp.sum(psink * delta, axis=0, keepdims=True))
            ds = jnp.concatenate(dscores, axis=0)
            dk = _dot_tn(ds, q_stack)
            dv = _dot_tn(jnp.concatenate(probs, axis=0), do_stack)
            dk = dk + pltpu.roll(dk, HEAD_DIM, 1)
            dv = dv + pltpu.roll(dv, HEAD_DIM, 1)
            dkp_ref[t], dkc_ref[t] = dk[:BLK], dk[BLK:]
            dvp_ref[t], dvc_ref[t] = dv[:BLK], dv[BLK:]
            dq_ref[:, cols] = _unstack_heads(_dot(ds, kcat)).astype(BF16)
            r8 = lax.broadcasted_iota(jnp.int32, (8, SLAB), 0)
            upd = jnp.zeros((8, SLAB), F32)
            for j in range(GROUP_HEADS):
                upd = jnp.where(r8 == j, dsinks[j], upd)
            dsink_ref[t] += upd

    cur_out = pl.BlockSpec((heads, BLK, SLAB), lambda gs, n: (gs, n, 0))
    prev_out = pl.BlockSpec((heads, BLK, SLAB), lambda gs, n: (gs, (n + nb - 1) % nb, 0))
    kv_shape = jax.ShapeDtypeStruct((KV_HEADS, s, SLAB), F32)
    return pl.pallas_call(
        body, name="swa_bwd", grid=(KV_HEADS // heads, nb),
        in_specs=[sp["sinks"], sp["step"], sp["kc"], sp["kp"], sp["vc"], sp["vp"], *sp["gates"], sp["step"], sp["tri"]],
        out_specs=[sp["step"], cur_out, prev_out, cur_out, prev_out,
                   pl.BlockSpec((heads, 8, SLAB), lambda gs, n: (gs, 0, 0))],
        out_shape=[jax.ShapeDtypeStruct((s, ATTN_WIDTH), BF16), kv_shape, kv_shape, kv_shape, kv_shape,
                   jax.ShapeDtypeStruct((KV_HEADS, 8, SLAB), F32)],
        compiler_params=_params("parallel", "arbitrary"),
    )(sinks, qk, qk, qk, qk, qk, *[proj] * heads, dag, _lower_triangle())


def _swa_bwd_finish(proj, dqr, o, dag, dkc, dkp, dvc, dvp, qw, kw, cos, sins, ones_q, ones_kv):
    s = proj.shape[0]
    rows = min(KV_ROWS, s)
    n_q = KV_COL
    q_of = lambda j: jnp.clip(j - 1, 0, n_q - 1)
    gate_of = lambda j: jnp.clip(j - 1 - n_q, 0, n_q - 1)

    def body(kv_ref, p_ref, dqr_ref, o_ref, dag_ref, dkc_ref, dkp_ref, dvc_ref, dvp_ref, qw_ref, kw_ref, cos_ref, sin_ref,
             oq_ref, ok_ref, dp_ref, dqw_ref, dkw_ref):
        j, i = pl.program_id(0), pl.program_id(1)

        @pl.when(j == 0)
        def _():
            @pl.when(i == 0)
            def _():
                dkw_ref[...] = jnp.zeros_like(dkw_ref)

            def assemble(cur_ref, prv_ref):
                tot = [cur_ref[h] + prv_ref[h] for h in range(KV_HEADS)]
                first = _half_mask(tot[0].shape, 0)
                return jnp.concatenate([jnp.where(first, tot[0], tot[1]), jnp.where(first, tot[2], tot[3])], axis=1)

            dkr = assemble(dkc_ref, dkp_ref)
            dv = assemble(dvc_ref, dvp_ref)
            cosf, sinf = _tile_lanes(cos_ref[...], 2), _tile_lanes(sin_ref[...], 2)
            nrm, rstd, _ = _k_prep(kv_ref[:, :KV_WIDTH].astype(F32), kw_ref[...], cosf, sinf, ok_ref[...])
            dy = dkr * cosf + _partner64(dkr * sinf)
            dn = dy * kw_ref[...]
            dk = rstd * (dn - nrm * _group_mean(dn * nrm, ok_ref[...], passes=1))
            dp_ref[...] = jnp.concatenate([dk, dv], axis=1).astype(BF16)
            dkw_ref[...] += jnp.sum(dy * nrm, axis=0, keepdims=True)

        @pl.when(jnp.logical_and(j >= 1, j <= n_q))
        def _():
            @pl.when(i == 0)
            def _():
                dqw_ref[...] = jnp.zeros_like(dqw_ref)

            cosf, sinf = _tile_lanes(cos_ref[...], 4), _tile_lanes(sin_ref[...], 4)
            nrm, rstd, _ = _q_prep(p_ref[...].astype(F32), qw_ref[...], cosf, sinf, oq_ref[...])
            dq = dqr_ref[...].astype(F32) * ATTN_SCALE
            dy = dq * cosf + _partner64(dq * sinf)
            dn = dy * qw_ref[...]
            dp_ref[...] = (rstd * (dn - nrm * _group_mean(dn * nrm, oq_ref[...], passes=1))).astype(BF16)
            dqw_ref[0] += jnp.sum(dy * nrm, axis=0, keepdims=True)

        @pl.when(j > n_q)
        def _():
            gate = p_ref[...].astype(F32)
            sg = _sigmoid(gate)
            dp_ref[...] = (dag_ref[...].astype(F32) * o_ref[...].astype(F32) * (sg * (1.0 + gate * (1.0 - sg)))).astype(BF16)

    first_pass = lambda j, i: jnp.where(j == 0, i, 0)
    acc = pl.BlockSpec((KV_HEADS, rows, SLAB), lambda j, i: (0, first_pass(j, i), 0))
    full = lambda shape: pl.BlockSpec(shape, lambda j, i: (0,) * len(shape))
    tab = pl.BlockSpec((rows, SLAB), lambda j, i: (i, 0))
    out_col = lambda j: jnp.where(j == 0, KV_COL, jnp.where(j <= n_q, j - 1, j))
    return pl.pallas_call(
        body, name="swa_bwd_finish", grid=(2 * n_q + 1, s // rows),
        in_specs=[pl.BlockSpec((rows, GROUP_WIDTH), lambda j, i: (first_pass(j, i), KV_COL)),
                  pl.BlockSpec((rows, GROUP_WIDTH), lambda j, i: (jnp.where(j == 0, 0, i), jnp.where(j <= n_q, q_of(j), j))),
                  pl.BlockSpec((rows, GROUP_WIDTH), lambda j, i: (jnp.where(jnp.logical_and(j >= 1, j <= n_q), i, 0), q_of(j))),
                  pl.BlockSpec((rows, GROUP_WIDTH), lambda j, i: (jnp.where(j > n_q, i, 0), gate_of(j))),
                  pl.BlockSpec((rows, GROUP_WIDTH), lambda j, i: (jnp.where(j > n_q, i, 0), gate_of(j))),
                  acc, acc, acc, acc, full((1, GROUP_WIDTH)), full((1, KV_WIDTH)), tab, tab,
                  full((GROUP_WIDTH, GROUP_WIDTH)), full((KV_WIDTH, KV_WIDTH))],
        out_specs=[pl.BlockSpec((rows, GROUP_WIDTH), lambda j, i: (i, out_col(j))),
                   pl.BlockSpec((1, 8, GROUP_WIDTH), lambda j, i: (q_of(j), 0, 0)), pl.BlockSpec((8, KV_WIDTH), lambda j, i: (0, 0))],
        out_shape=[jax.ShapeDtypeStruct((s, ODD_IN), BF16), jax.ShapeDtypeStruct((n_q, 8, GROUP_WIDTH), F32),
                   jax.ShapeDtypeStruct((8, KV_WIDTH), F32)],
        compiler_params=_params("arbitrary", "arbitrary"),
    )(proj, proj, dqr, o, dag, dkc, dkp, dvc, dvp, qw, kw, cos, sins, ones_q, ones_kv)


def _place():
    x, y, c = lax.axis_index("x"), lax.axis_index("y"), lax.axis_index("c")
    return x, y, c


OTHER_CHIPS = ((1, 0), (0, 1), (1, 1))


def _half_rows(ref, half, rows):
    return ref.at[pl.ds(pl.multiple_of(half * (rows // 2), 8), rows // 2)]


DMA_CHUNK_BYTES = 1 << 20
BF16_TILE_ROWS = 16


def _n_chunks(ref):
    rows = ref.shape[-2]
    nbytes = math.prod(ref.shape) * jnp.dtype(ref.dtype).itemsize
    n = 1
    while 2 * n * DMA_CHUNK_BYTES <= nbytes and rows % (2 * n * BF16_TILE_ROWS) == 0:
        n *= 2
    return n


def _row_chunk(ref, k, n):
    rows = ref.shape[-2] // n
    return ref.at[pl.ds(k * rows, rows)] if len(ref.shape) == 2 else ref.at[:, pl.ds(k * rows, rows)]


def _push(src, dst, send_sem, recv_sem, device_id):
    n = _n_chunks(src)
    for k in range(n):
        pltpu.make_async_remote_copy(src_ref=_row_chunk(src, k, n), dst_ref=_row_chunk(dst, k, n), send_sem=send_sem,
                                     recv_sem=recv_sem, device_id=device_id, device_id_type=MESH).start()
    return pltpu.make_async_remote_copy(src_ref=src, dst_ref=dst, send_sem=send_sem, recv_sem=recv_sem,
                                        device_id=device_id, device_id_type=MESH)


HBM = pl.BlockSpec(memory_space=pltpu.HBM)
SEM = pl.BlockSpec(memory_space=pltpu.SEMAPHORE)
SPLIT_COPY_EFFECT = pltpu.SideEffectType.DATAFLOW_SIDE_EFFECTING


def _in_hbm(a):
    return pltpu.with_memory_space_constraint(a, pltpu.HBM)


def _start_copies(name, arrays, plan, n_copies, after=None):
    n = len(arrays)

    def body(*refs):
        send_sem, recv_sem = refs[-n - 3], refs[-n - 2]
        for k, (src, dst, peer) in enumerate(plan(refs[:n])):
            _push(src, dst, send_sem.at[k], recv_sem.at[k], peer)
        refs[-1][...] = jnp.zeros_like(refs[-1])

    dma = pltpu.SemaphoreType.DMA((n_copies,))
    outs = pl.pallas_call(
        body, name=name,
        out_shape=(dma, dma, *[pltpu.HBM(a.shape, a.dtype) for a in arrays], jax.ShapeDtypeStruct((8, 128), F32)),
        in_specs=[HBM] * n + ([ANY] if after is not None else []),
        out_specs=(SEM, SEM, *[HBM] * n, pl.BlockSpec(memory_space=pltpu.VMEM)),
        input_output_aliases={i: i + 2 for i in range(n)},
        compiler_params=pltpu.CompilerParams(has_side_effects=SPLIT_COPY_EFFECT),
    )(*[_in_hbm(a) for a in arrays], *((after,) if after is not None else ()))
    return outs[0], outs[1], list(outs[2:2 + n]), outs[-1]


def _wait_copies(name, send_sem, recv_sem, arrays, plan, after):
    n = len(arrays)
    after = list(after) if isinstance(after, (list, tuple)) else [after]

    def body(*refs):
        send_ref, recv_ref = refs[n], refs[n + 1]
        for k, (src, dst, peer) in enumerate(plan(refs[:n])):
            cp = pltpu.make_async_remote_copy(src_ref=src, dst_ref=dst, send_sem=send_ref.at[k], recv_sem=recv_ref.at[k],
                                              device_id=peer, device_id_type=MESH)
            cp.wait_send()
            cp.wait_recv()

    return list(pl.pallas_call(
        body, name=name, out_shape=tuple(pltpu.HBM(a.shape, a.dtype) for a in arrays),
        in_specs=[HBM] * n + [SEM, SEM] + [ANY] * len(after), out_specs=tuple([HBM] * n),
        input_output_aliases={i: i for i in range(n)},
        compiler_params=pltpu.CompilerParams(has_side_effects=SPLIT_COPY_EFFECT),
    )(*arrays, send_sem, recv_sem, *after))


def _gather_region(full, kind, chip, half=None):
    if kind == "whole":
        return full.at[chip]
    if kind == "col":
        rows, width = full.shape[0], full.shape[1] // N_CHIPS
        piece = full.at[:, pl.ds(pl.multiple_of(chip * width, 128), width)]
    else:
        rows = full.shape[0] // N_CHIPS
        piece = full.at[pl.ds(pl.multiple_of(chip * rows, BF16_TILE_ROWS), rows)]
    return piece if half is None else _half_rows(piece, half, rows)


def _gather_plan(kinds):
    def plan(fulls):
        x, y, c = _place()
        copies = []
        for fx, fy in OTHER_CHIPS:
            for full, kind in zip(fulls, kinds):
                mine = _gather_region(full, kind, 2 * x + y, c)
                copies.append((mine, mine, (x ^ fx, y ^ fy, c)))
        return copies

    return plan


def _gather_start(name, fulls, kinds, after=None):
    return _start_copies(name, list(fulls), _gather_plan(kinds), 3 * len(kinds), after)


def _pass_on_plan(kinds):
    split = [i for i, kind in enumerate(kinds) if kind != "whole"]

    def plan(fulls):
        x, y, c = _place()
        copies = []
        for fx, fy in OTHER_CHIPS:
            chip = 2 * (x ^ fx) + (y ^ fy)
            for i in split:
                landed = _gather_region(fulls[i], kinds[i], chip, c)
                copies.append((landed, landed, (x, y, 1 - c)))
        return copies

    return plan, 3 * len(split)


def _gather_arrive(name, started, kinds, after):
    send_sem, recv_sem, fulls, _ = started
    fulls = _wait_copies(name + "_wait", send_sem, recv_sem, fulls, _gather_plan(kinds), after)
    plan, count = _pass_on_plan(kinds)
    return _start_copies(name + "_pass_on", fulls, plan, count)


def _gather_done(name, passed_on, kinds, after):
    send_sem, recv_sem, fulls, _ = passed_on
    return _wait_copies(name + "_pass_on_wait", send_sem, recv_sem, fulls, _pass_on_plan(kinds)[0], after)


def _allreduce_small(v):
    def body(v_ref, out_ref, buf, send_sems, recv_sems):
        x, y, c = _place()
        me = 4 * x + 2 * y + c
        buf[me] = v_ref[...]
        copies = []
        for r in range(1, N_DEV):
            peer = (x ^ (r >> 2), y ^ ((r >> 1) & 1), c ^ (r & 1))
            cp = pltpu.make_async_remote_copy(src_ref=v_ref, dst_ref=buf.at[me], send_sem=send_sems.at[r - 1],
                                              recv_sem=recv_sems.at[r - 1], device_id=peer, device_id_type=MESH)
            cp.start()
            copies.append(cp)
        for cp in copies:
            cp.wait_recv()
        for cp in copies:
            cp.wait_send()
        total = buf[0]
        for d in range(1, N_DEV):
            total = total + buf[d]
        out_ref[...] = total

    vm = pl.BlockSpec(memory_space=pltpu.VMEM)
    return pl.pallas_call(
        body, name="allreduce_small", in_specs=[vm], out_specs=vm, out_shape=jax.ShapeDtypeStruct(v.shape, v.dtype),
        scratch_shapes=[pltpu.VMEM((N_DEV,) + v.shape, v.dtype), pltpu.SemaphoreType.DMA((N_DEV - 1,)),
                        pltpu.SemaphoreType.DMA((N_DEV - 1,))],
        compiler_params=pltpu.CompilerParams(has_side_effects=True),
    )(v)


def _exchange_halves(grads, name):
    n = len(grads)

    def body(*refs):
        g, theirs = refs[:n], refs[n:2 * n]
        send_sem, recv_sem = refs[2 * n:]
        x, y, c = _place()
        copies = []
        for i in range(n):
            half = g[i].shape[1] // 2
            src = g[i].at[:, pl.ds(pl.multiple_of((1 - c) * half, BF16_TILE_ROWS), half)]
            copies.append(_push(src, theirs[i], send_sem.at[i], recv_sem.at[i], (x, y, 1 - c)))
        for cp in copies:
            cp.wait_recv()
            cp.wait_send()

    dma = pltpu.SemaphoreType.DMA
    return pl.pallas_call(
        body, name=name, in_specs=[ANY] * n, out_specs=[ANY] * n,
        out_shape=[jax.ShapeDtypeStruct((a.shape[0], a.shape[1] // 2, a.shape[2]), a.dtype) for a in grads],
        scratch_shapes=[dma((n,)), dma((n,))],
        compiler_params=pltpu.CompilerParams(has_side_effects=True),
    )(*grads)


def _pair_sum(g, theirs, core, name):
    pieces, half, cols = theirs.shape
    tr = min(half, 256)
    per = half // tr

    def body(core_ref, g_ref, t_ref, o_ref):
        del core_ref
        o_ref[...] = (g_ref[...].astype(F32) + t_ref[...].astype(F32)).astype(BF16)

    spec = pl.BlockSpec((1, tr, cols), lambda p, i, core_ref: (p, i, 0))
    return pl.pallas_call(
        body, name=name, out_shape=jax.ShapeDtypeStruct(theirs.shape, BF16),
        grid_spec=pltpu.PrefetchScalarGridSpec(
            num_scalar_prefetch=1, grid=(pieces, per),
            in_specs=[pl.BlockSpec((1, tr, cols), lambda p, i, core_ref: (p, core_ref[0] * per + i, 0)), spec],
            out_specs=spec),
        compiler_params=_params("parallel", "parallel"))(core, g, theirs)


def _scatter_plan(n):
    def plan(refs):
        parts, stacks = refs[:n], refs[n:]
        x, y, c = _place()
        copies = []
        for fx, fy in OTHER_CHIPS:
            chip = 2 * (x ^ fx) + (y ^ fy)
            for part, stack in zip(parts, stacks):
                if part.shape[0] == N_CHIPS:
                    piece = part.at[chip]
                else:
                    width = part.shape[2] // N_CHIPS
                    piece = part.at[0].at[:, pl.ds(pl.multiple_of(chip * width, 128), width)]
                copies.append((piece, stack.at[2 * x + y], (x ^ fx, y ^ fy, c)))
        return copies

    return plan


def _scatter_start(name, parts, after=None):
    def landing(a):
        return (N_CHIPS, a.shape[1], a.shape[2] if a.shape[0] == N_CHIPS else a.shape[2] // N_CHIPS)

    stacks = [lax.empty(landing(a), a.dtype) for a in parts]
    return _start_copies(name, list(parts) + stacks, _scatter_plan(len(parts)), 3 * len(parts), after)


def _scatter_finish(name, started, after):
    send_sem, recv_sem, arrays, _ = started
    n = len(arrays) // 2
    arrays = _wait_copies(name + "_wait", send_sem, recv_sem, arrays, _scatter_plan(n), after)
    return arrays[:n], arrays[n:]


def _sum_chips(part, stack, place, name):
    _, r, c = stack.shape
    tr = 256
    per = r // tr

    def body(place_ref, own_ref, a_ref, b_ref, c_ref, o_ref):
        del place_ref
        total = own_ref[0].astype(F32)
        for ref in (a_ref, b_ref, c_ref):
            total = total + ref[0].astype(F32)
        o_ref[...] = total

    if part.shape[0] == N_CHIPS:
        own = pl.BlockSpec((1, tr, c), lambda i, pr: (pr[1], i, 0))
    else:
        own = pl.BlockSpec((1, tr, c), lambda i, pr: (0, i, pr[1]))
    other = lambda flip: pl.BlockSpec((1, tr, c), lambda i, pr: (pr[1] ^ flip, i, 0))
    return pl.pallas_call(
        body, name=name, out_shape=jax.ShapeDtypeStruct((2 * r, c), F32),
        grid_spec=pltpu.PrefetchScalarGridSpec(
            num_scalar_prefetch=1, grid=(per,), in_specs=[own, other(2), other(1), other(3)],
            out_specs=pl.BlockSpec((tr, c), lambda i, pr: (pr[0] * per + i, 0))),
        compiler_params=_params("parallel"))(place, part, stack, stack, stack)


def _own_half_plan(fulls):
    x, y, c = _place()
    return [(mine, mine, (x, y, 1 - c)) for mine in (_half_rows(full, c, full.shape[0]) for full in fulls)]


MM = dict(tm=2048, tn=1024, tk=2048)
MM_LONG_K = dict(tm=1024, tn=1024, tk=4096)


def _local_step(x, target, ev_norm_w, q_norm_w, k_norm_w, sinks, own_first, weights_first, late_arrived, weights_late, emit):
    s = x.shape[0]
    cos_r, sin_r = _rope_tables(s, RET_DIM)
    cos_a, sin_a = _rope_tables(s, HEAD_DIM)
    cos_a = np.tile(cos_a, (1, 4))
    sins_a = np.tile(np.concatenate([-sin_a, sin_a], axis=1), (1, 2))
    tables = _retention_tables()
    ones_q, ones_kv = _block_diag_mean(GROUP_WIDTH), _block_diag_mean(KV_WIDTH)
    qw_g = jnp.tile(q_norm_w, (1, GROUP_WIDTH // HEAD_DIM))
    kw_kv = jnp.tile(k_norm_w, (1, KV_WIDTH // HEAD_DIM))
    sinks1 = sinks.reshape(Q_HEADS)

    own_w_in0, own_block, start_token = own_first
    h0 = _rmsnorm(x, ev_norm_w, "norm0", after=start_token)
    shifted = dict(shift=own_block * (own_w_in0.shape[1] // MM["tn"]), total=EVEN_IN // MM["tn"], tm=MM["tm"], tn=MM["tn"],
                   out_dtype=BF16)
    own_blocks = own_w_in0.shape[1] // MM["tn"]
    proj0 = _mm_shifted(h0, own_w_in0, b_shifted=False, first=0, count=own_blocks, name="proj0_own", **shifted)
    w_in0, conv_w, od_norm_w, token = weights_first(proj0)
    proj0 = _mm_shifted(h0, w_in0, b_shifted=True, first=own_blocks, count=shifted["total"] - own_blocks, name="proj0_rest",
                        into=proj0, after=token, **shifted)
    o_ret, cat, states = _retention_fwd(proj0, cos_r, sin_r, tables, after=late_arrived(proj0))
    cat = _conv_fwd(proj0, conv_w, cat)
    w_out0, w_in1, w_out1 = weights_late(cat)

    def residual_and_norm(prod, x_ref, w_ref, x1_ref, h1_ref):
        x1v = x_ref[...] + prod
        x1_ref[...] = x1v
        rstd = lax.rsqrt(jnp.mean(x1v * x1v, axis=-1, keepdims=True) + EPS)
        h1_ref[...] = (x1v * rstd * w_ref[...]).astype(BF16)

    def residual_and_loss(prod, x1_ref, t_ref, dyb_ref, sq_ref):
        diff = (x1_ref[...] + prod) - t_ref[...]
        dyb_ref[...] = (diff * (1.0 / D_MODEL)).astype(BF16)

        @pl.when(pl.program_id(0) == 0)
        def _():
            sq_ref[...] = jnp.zeros_like(sq_ref)

        sq_ref[...] += jnp.sum(jnp.sum(diff * diff, axis=1, keepdims=True), axis=0, keepdims=True)

    act = lambda dt: jax.ShapeDtypeStruct((s, D_MODEL), dt)
    x1, h1 = _mm_rows(cat, w_out0, [x], [od_norm_w], [act(F32), act(BF16)], residual_and_norm, tm=min(s, 512), name="out0")
    proj1 = _mm(h1, w_in1, mode="nn", out_dtype=BF16, name="proj1", tm=2048, tn=1536, tk=2048)
    qk = _qk_prep(proj1, qw_g, kw_kv, cos_a, sins_a, ones_q, ones_kv)
    ag, o_att = _swa_fwd(qk, proj1, sinks1)
    dy_b, sq = _mm_rows(ag, w_out1, [x1, target], [], [act(BF16), jax.ShapeDtypeStruct((8, 128), F32)],
                        residual_and_loss, tm=min(s, 512), name="out1")

    g_w_out1 = _mm(ag, dy_b, mode="tn", out_dtype=BF16, name="g_w_out1", **MM_LONG_K)
    dag = _mm(dy_b, w_out1, mode="nt", out_dtype=BF16, name="d_ag", **MM)
    dqr, dkc, dkp, dvc, dvp, dsink = _swa_bwd(qk, proj1, dag, sinks1)
    dproj1, dqw, dkw = _swa_bwd_finish(proj1, dqr, o_att, dag, dkc, dkp, dvc, dvp, qw_g, kw_kv, cos_a, sins_a, ones_q, ones_kv)
    g_w_in1 = _mm(h1, dproj1, mode="tn", out_dtype=BF16, name="g_w_in1", tm=1024, tn=768, tk=4096)
    dh1 = _mm(dproj1, w_in1, mode="nt", out_dtype=BF16, name="d_h1", tm=1024, tn=1024, tk=ODD_IN)
    dx1_b, g_norm1 = _rmsnorm_bwd(x1, od_norm_w, dh1, dy_b, "norm1_bwd", BF16)

    g_w_out0 = _mm(cat, dx1_b, mode="tn", out_dtype=BF16, name="g_w_out0", **MM_LONG_K)
    token = emit("upper", (("od_w_in", g_w_in1, "col"), ("od_w_out", g_w_out1, "row"), ("ev_w_out", g_w_out0, "row")))
    dcat = _mm(dx1_b, w_out0, mode="nt", out_dtype=BF16, name="d_cat", after=token, **MM)
    dproj0 = _retention_bwd(proj0, o_ret, states, dcat, cos_r, sin_r, tables)
    dproj0, g_conv = _conv_bwd(proj0, dcat, conv_w, dproj0)
    g_w_in0 = _mm(h0, dproj0, mode="tn", out_dtype=BF16, name="g_w_in0", **MM_LONG_K)
    token = emit("in0", (("ev_w_in", g_w_in0, "col"),))
    dh0 = _mm(dproj0, w_in0, mode="nt", out_dtype=BF16, name="d_h0", after=token, **MM_LONG_K)
    grad_x, g_norm0 = _rmsnorm_bwd(x, ev_norm_w, dh0, dx1_b, "norm0_bwd", F32)

    g_qw = dqw[:, 0, :].reshape(Q_HEADS, HEAD_DIM).sum(axis=0)
    g_kw = dkw[0].reshape(KV_HEADS, HEAD_DIM).sum(axis=0)
    g_sinks = dsink[:, :, 0].reshape(Q_HEADS)
    small = dict(ev_norm=g_norm0[0], od_norm=g_norm1[0], conv=g_conv[:3], qw=g_qw, kw=g_kw, sinks=g_sinks)
    return sq[0, 0], grad_x, small


def _pack_small_grads(small, sq):
    pad = lambda v: jnp.pad(v, (0, D_MODEL - v.shape[0]))
    tail = pad(jnp.concatenate([small["qw"], small["kw"], small["sinks"]]))
    rows = [small["ev_norm"], small["od_norm"]] + [pad(small["conv"][t]) for t in range(3)] + [tail, pad(sq.reshape(1))]
    rows += [jnp.zeros((D_MODEL,), F32)] * (8 - len(rows))
    return jnp.stack(rows)


class _ReduceScatter:
    def __init__(self, place):
        self.place = place
        self.started = []
        self.sharing = {}

    def send(self, tag, grads):
        pieces = [g[None] if kind == "col" else g.reshape(N_CHIPS, g.shape[0] // N_CHIPS, g.shape[1]) for _, g, kind in grads]
        theirs = _exchange_halves(pieces, "exchange_halves_" + tag)
        parts = [_pair_sum(g, t, self.place[:1], "pair_sum_" + nm) for g, t, (nm, _, _) in zip(pieces, theirs, grads)]
        started = _scatter_start("scatter_" + tag, parts)
        token = self.sum_up(started[3])
        self.started.append((tag, [nm for nm, _, _ in grads], started))
        return started[3] if token is None else token

    def sum_up(self, after):
        token = None
        for tag, group, started in self.started:
            if tag not in self.sharing:
                parts, stacks = _scatter_finish("scatter_" + tag, started, after)
                sums = [_sum_chips(p, s, self.place, "chip_sum_" + nm) for p, s, nm in zip(parts, stacks, group)]
                self.sharing[tag] = (group, _start_copies("share_" + tag, sums, _own_half_plan, len(sums)))
                token = self.sharing[tag][1][3]
        return token

    def result(self, tag, after):
        group, (send_sem, recv_sem, sums, _) = self.sharing[tag]
        return dict(zip(group, _wait_copies("share_" + tag + "_wait", send_sem, recv_sem, sums, _own_half_plan, after)))


def kernel(x, ev_norm_w, ev_w_in, ev_conv_w, ev_w_out, od_norm_w, od_w_in, od_q_norm_w, od_k_norm_w, od_sinks, od_w_out, loss_target, m_ev_norm_w, m_ev_w_in, m_ev_conv_w, m_ev_w_out, m_od_norm_w, m_od_w_in, m_od_q_norm_w, m_od_k_norm_w, m_od_sinks, m_od_w_out, v_ev_norm_w, v_ev_w_in, v_ev_conv_w, v_ev_w_out, v_od_norm_w, v_od_w_in, v_od_q_norm_w, v_od_k_norm_w, v_od_sinks, v_od_w_out):
    my_chip = 2 * lax.axis_index("x") + lax.axis_index("y")
    place = jnp.stack([lax.axis_index("c"), my_chip]).astype(jnp.int32)
    shard_w = D_MODEL // N_CHIPS
    conv_shard = CONV_WIDTH // N_CHIPS

    small_in = jnp.zeros((8, shard_w), F32)
    small_in = small_in.at[0].set(od_norm_w[0]).at[1:4, :conv_shard].set(ev_conv_w[0])
    small_in = lax.dynamic_update_slice(jnp.zeros((N_CHIPS, 8, shard_w), F32), small_in[None], (my_chip, 0, 0))
    chip = place[1:]
    first_kinds, late_kinds = ("col", "whole"), ("row", "col", "row")
    w_in0_own_place, w_in0_shard = _cast_into_gathered(ev_w_in[0], "col", chip, "cast_w_in0", keep_shard=True)
    first = _gather_start("gather_first", [w_in0_own_place, small_in], first_kinds)
    late_own = [_cast_into_gathered(ev_w_out[0], "row", chip, "cast_w_out0"),
                _cast_into_gathered(od_w_in[0], "col", chip, "cast_w_in1"),
                _cast_into_gathered(od_w_out[0], "row", chip, "cast_w_out1")]
    late = []

    def weights_first(after):
        passed_on = _gather_arrive("gather_first", first, first_kinds, [after] + late_own)
        w_in0, small_all = _gather_done("gather_first", passed_on, first_kinds, passed_on[3])
        late.append(_gather_start("gather_late", late_own, late_kinds, after=w_in0))
        od_norm_full = small_all[:, 0, :].reshape(1, D_MODEL)
        conv_full = jnp.transpose(small_all[:, 1:4, :conv_shard], (1, 0, 2)).reshape(3, CONV_WIDTH)
        return w_in0, conv_full, od_norm_full, late[0][3]

    def late_arrived(after):
        late.append(_gather_arrive("gather_late", late[0], late_kinds, after))
        return late[1][3]

    def weights_late(after):
        return _gather_done("gather_late", late[1], late_kinds, after)

    reduce_scatter = _ReduceScatter(place)
    sq, grad_x, small = _local_step(x[0], loss_target[0], ev_norm_w, od_q_norm_w, od_k_norm_w, od_sinks,
                                    (w_in0_shard, chip, first[3]), weights_first, late_arrived, weights_late,
                                    reduce_scatter.send)

    token = reduce_scatter.sum_up(grad_x)
    upd = {}
    upper = reduce_scatter.result("upper", token)
    upd["od_w_in"] = _adamw(od_w_in[0], upper["od_w_in"], m_od_w_in[0], v_od_w_in[0], "adamw_od_w_in")
    upd["od_w_out"] = _adamw(od_w_out[0], upper["od_w_out"], m_od_w_out[0], v_od_w_out[0], "adamw_od_w_out")
    upd["ev_w_out"] = _adamw(ev_w_out[0], upper["ev_w_out"], m_ev_w_out[0], v_ev_w_out[0], "adamw_ev_w_out")
    g_ev_w_in = reduce_scatter.result("in0", [upd[nm][1] for nm in ("od_w_in", "od_w_out", "ev_w_out")])["ev_w_in"]
    upd["ev_w_in"] = _adamw(ev_w_in[0], g_ev_w_in, m_ev_w_in[0], v_ev_w_in[0], "adamw_ev_w_in")
    tot = _allreduce_small(_pack_small_grads(small, sq))
    loss = 0.5 * tot[6, 0] / D_MODEL
    g_ev_norm = tot[0:1]
    g_od_norm = lax.dynamic_slice(tot, (1, my_chip * shard_w), (1, shard_w))
    g_conv = lax.dynamic_slice(tot, (2, my_chip * conv_shard), (3, conv_shard))
    g_qw, g_kw, g_sinks = tot[5:6, 0:HEAD_DIM], tot[5:6, HEAD_DIM:2 * HEAD_DIM], tot[5:6, 2 * HEAD_DIM:2 * HEAD_DIM + Q_HEADS]

    smalls = (("ev_norm_w", ev_norm_w, g_ev_norm, m_ev_norm_w, v_ev_norm_w),
              ("ev_conv_w", ev_conv_w, g_conv, m_ev_conv_w, v_ev_conv_w),
              ("od_norm_w", od_norm_w, g_od_norm, m_od_norm_w, v_od_norm_w),
              ("od_q_norm_w", od_q_norm_w, g_qw, m_od_q_norm_w, v_od_q_norm_w),
              ("od_k_norm_w", od_k_norm_w, g_kw, m_od_k_norm_w, v_od_k_norm_w),
              ("od_sinks", od_sinks, g_sinks, m_od_sinks, v_od_sinks))
    sizes = [w.size for _, w, _, _, _ in smalls]
    padded = 8 * 128 * math.ceil(sum(sizes) / (8 * 128))
    pack = lambda arrs, fill: jnp.concatenate(
        [a.reshape(-1) for a in arrs] + [jnp.full((padded - sum(sizes),), fill, F32)]).reshape(8, padded // 8)
    packed = _adamw(pack([w for _, w, _, _, _ in smalls], 0.0), pack([g for _, _, g, _, _ in smalls], 0.0),
                    pack([m for _, _, _, m, _ in smalls], 0.0), pack([v for _, _, _, _, v in smalls], 1.0), "adamw_small")
    offs = [sum(sizes[:i]) for i in range(len(sizes))]
    for (nm, w, _, _, _), off, size in zip(smalls, offs, sizes):
        upd[nm] = tuple(p.reshape(-1)[off:off + size].reshape(w.shape) for p in packed)
    for nm in ("ev_w_in", "ev_w_out", "od_w_in", "od_w_out"):
        upd[nm] = tuple(u[None] for u in upd[nm])
    order = ("ev_norm_w", "ev_w_in", "ev_conv_w", "ev_w_out", "od_norm_w", "od_w_in", "od_q_norm_w", "od_k_norm_w", "od_sinks", "od_w_out")
    return (loss, grad_x[None], *[upd[nm][0] for nm in order], *[upd[nm][1] for nm in order],
            *[upd[nm][2] for nm in order], *[upd[nm][3] for nm in order])
```

```python
import math

import jax
import jax.numpy as jnp
import numpy as np
from jax import lax
from jax.experimental import pallas as pl
from jax.experimental.pallas import tpu as pltpu

F32 = jnp.float32
BF16 = jnp.bfloat16

D_MODEL = 2048
RET_HEADS = 4
RET_DIM = 256
RET_WIDTH = 1024
CONV_WIDTH = 1024
EVEN_IN = 8192
Q_HEADS = 32
HEAD_DIM = 64
KV_HEADS = 4
KV_WIDTH = 256
ATTN_WIDTH = 2048
ODD_IN = 4608
BLK = 128
ROPE_THETA = 10000.0
EPS = 1e-6
ADAM_LR = 0.001
ADAM_B1 = 0.9
ADAM_B2 = 0.999
ADAM_EPS = 1e-08
ADAM_WD = 0.01
ADAM_STEP = 10
N_CHIPS = 4
N_DEV = 8
VMEM_LIMIT_BYTES = 56 * 1024 * 1024
MESH = pl.DeviceIdType.MESH
ANY = pl.BlockSpec(memory_space=pl.ANY)


def _params(*sem):
    return pltpu.CompilerParams(dimension_semantics=sem, vmem_limit_bytes=VMEM_LIMIT_BYTES)


def _dot(a, b):
    return jnp.dot(a, b, preferred_element_type=F32)


def _dot_nt(a, b):
    return lax.dot_general(a, b, (((1,), (1,)), ((), ())), preferred_element_type=F32)


def _dot_tn(a, b):
    return lax.dot_general(a, b, (((0,), (0,)), ((), ())), preferred_element_type=F32)


def _sigmoid(x):
    return 1.0 / (1.0 + jnp.exp(-x))


def _mm(a, b, *, mode, tm, tn, tk, out_dtype, name, add=None, after=None):
    if mode == "nn":
        (m, k), n = a.shape, b.shape[1]
    elif mode == "nt":
        (m, k), n = a.shape, b.shape[0]
    else:
        (k, m), n = a.shape, b.shape[1]
    tm, tn, tk = min(tm, m), min(tn, n), min(tk, k)
    assert m % tm == 0 and n % tn == 0 and k % tk == 0, (name, m, n, k)
    nk = k // tk
    dot = {"nn": _dot, "nt": _dot_nt, "tn": _dot_tn}[mode]
    a_spec = (pl.BlockSpec((tk, tm), lambda i, j, kk: (kk, i)) if mode == "tn"
              else pl.BlockSpec((tm, tk), lambda i, j, kk: (i, kk)))
    b_spec = (pl.BlockSpec((tn, tk), lambda i, j, kk: (j, kk)) if mode == "nt"
              else pl.BlockSpec((tk, tn), lambda i, j, kk: (kk, j)))
    o_spec = pl.BlockSpec((tm, tn), lambda i, j, kk: (i, j))
    has_add = add is not None

    def body(*refs):
        a_ref, b_ref = refs[0], refs[1]
        add_ref = refs[2] if has_add else None
        o_ref, acc_ref = refs[-2], refs[-1]
        p = dot(a_ref[...], b_ref[...])

        def finish(total):
            if has_add:
                total = total + add_ref[...].astype(F32)
            o_ref[...] = total.astype(out_dtype)

        if nk == 1:
            finish(p)
        else:
            kk = pl.program_id(2)

            @pl.when(kk == 0)
            def _():
                acc_ref[...] = p

            @pl.when(jnp.logical_and(kk > 0, kk < nk - 1))
            def _():
                acc_ref[...] += p

            @pl.when(kk == nk - 1)
            def _():
                finish(acc_ref[...] + p)

    in_specs = [a_spec, b_spec] + ([o_spec] if has_add else []) + ([ANY] if after is not None else [])
    args = (a, b) + ((add,) if has_add else ()) + ((after,) if after is not None else ())
    return pl.pallas_call(
        body, name=name, grid=(m // tm, n // tn, nk), in_specs=in_specs, out_specs=o_spec,
        out_shape=jax.ShapeDtypeStruct((m, n), out_dtype),
        scratch_shapes=[pltpu.VMEM((tm, tn) if nk > 1 else (8, 128), F32)],
        compiler_params=_params("parallel", "parallel", "arbitrary"),
    )(*args)


def _mm_shifted(a, b, shift, *, b_shifted, first, count, total, tm, tn, out_dtype, name, into=None, after=None):
    m, k = a.shape
    tm = min(tm, m)
    assert m % tm == 0
    col = lambda j, shift_ref: (shift_ref[0] + first + j) % total
    extra = [arr for arr in (into, after) if arr is not None]

    def body(shift_ref, a_ref, b_ref, *rest):
        del shift_ref
        rest[-1][...] = _dot(a_ref[...], b_ref[...]).astype(out_dtype)

    return pl.pallas_call(
        body, name=name, out_shape=jax.ShapeDtypeStruct((m, total * tn), out_dtype),
        grid_spec=pltpu.PrefetchScalarGridSpec(
            num_scalar_prefetch=1, grid=(m // tm, count),
            in_specs=[pl.BlockSpec((tm, k), lambda i, j, s: (i, 0)),
                      pl.BlockSpec((k, tn), (lambda i, j, s: (0, col(j, s))) if b_shifted else (lambda i, j, s: (0, j)))]
            + [ANY] * len(extra),
            out_specs=pl.BlockSpec((tm, tn), lambda i, j, s: (i, col(j, s)))),
        input_output_aliases={3: 0} if into is not None else {},
        compiler_params=_params("parallel", "arbitrary"))(shift, a, b, *extra)


def _mm_rows(a, b, rows_in, vecs_in, out_shapes, epilogue, *, tm, name):
    m, k = a.shape
    n = b.shape[1]
    assert m % tm == 0
    row = pl.BlockSpec((tm, n), lambda i: (i, 0))

    def body(a_ref, b_ref, *rest):
        epilogue(_dot(a_ref[...], b_ref[...]), *rest)

    out_specs = [row if tuple(s.shape) == (m, n) else pl.BlockSpec(s.shape, lambda i: (0, 0)) for s in out_shapes]
    return pl.pallas_call(
        body, name=name, grid=(m // tm,),
        in_specs=[pl.BlockSpec((tm, k), lambda i: (i, 0)), pl.BlockSpec((k, n), lambda i: (0, 0))] + [row] * len(rows_in)
        + [pl.BlockSpec((1, n), lambda i: (0, 0))] * len(vecs_in),
        out_specs=out_specs, out_shape=out_shapes, compiler_params=_params("arbitrary"),
    )(a, b, *rows_in, *vecs_in)


def _cast_into_gathered(w, kind, chip, name, keep_shard=False):
    r, c = w.shape
    tr = min(r, 512)
    per = r // tr

    def body(chip_ref, w_ref, *outs):
        del chip_ref
        for o_ref in outs:
            o_ref[...] = w_ref[...].astype(BF16)

    if kind == "col":
        shape, out_map = (r, N_CHIPS * c), (lambda i, chip_ref: (i, chip_ref[0]))
    else:
        shape, out_map = (N_CHIPS * r, c), (lambda i, chip_ref: (chip_ref[0] * per + i, 0))
    plain = pl.BlockSpec((tr, c), lambda i, chip_ref: (i, 0))
    out = pl.pallas_call(
        body, name=name,
        out_shape=[jax.ShapeDtypeStruct(shape, BF16)] + ([jax.ShapeDtypeStruct((r, c), BF16)] if keep_shard else []),
        grid_spec=pltpu.PrefetchScalarGridSpec(
            num_scalar_prefetch=1, grid=(per,), in_specs=[plain],
            out_specs=[pl.BlockSpec((tr, c), out_map)] + ([plain] if keep_shard else [])),
        compiler_params=_params("parallel"))(chip, w)
    return out if keep_shard else out[0]


NORM_ROWS = 512


def _rmsnorm(x, w, name, after=None):
    s, d = x.shape
    tr = NORM_ROWS

    def body(x_ref, w_ref, *rest):
        xv = x_ref[...]
        rstd = lax.rsqrt(jnp.mean(xv * xv, axis=-1, keepdims=True) + EPS)
        rest[-1][...] = (xv * rstd * w_ref[...]).astype(BF16)

    return pl.pallas_call(
        body, name=name, grid=(s // tr,),
        in_specs=[pl.BlockSpec((tr, d), lambda i: (i, 0)), pl.BlockSpec((1, d), lambda i: (0, 0))]
        + ([ANY] if after is not None else []),
        out_specs=pl.BlockSpec((tr, d), lambda i: (i, 0)),
        out_shape=jax.ShapeDtypeStruct((s, d), BF16), compiler_params=_params("parallel"),
    )(x, w, *((after,) if after is not None else ()))


def _rmsnorm_bwd(x, w, dh, dres, name, out_dtype):
    s, d = x.shape
    tr = NORM_ROWS

    def body(x_ref, w_ref, dh_ref, dres_ref, dx_ref, dw_ref):
        xv = x_ref[...]
        rstd = lax.rsqrt(jnp.mean(xv * xv, axis=-1, keepdims=True) + EPS)
        nrm = xv * rstd
        dhv = dh_ref[...].astype(F32)
        dn = dhv * w_ref[...]
        dx = dres_ref[...].astype(F32) + rstd * (dn - nrm * jnp.mean(dn * nrm, axis=-1, keepdims=True))
        dx_ref[...] = dx.astype(out_dtype)

        @pl.when(pl.program_id(0) == 0)
        def _():
            dw_ref[...] = jnp.zeros_like(dw_ref)

        dw_ref[...] += jnp.sum(dhv * nrm, axis=0, keepdims=True)

    row = pl.BlockSpec((tr, d), lambda i: (i, 0))
    return pl.pallas_call(
        body, name=name, grid=(s // tr,),
        in_specs=[row, pl.BlockSpec((1, d), lambda i: (0, 0)), row, row],
        out_specs=[row, pl.BlockSpec((8, d), lambda i: (0, 0))],
        out_shape=[jax.ShapeDtypeStruct((s, d), out_dtype), jax.ShapeDtypeStruct((8, d), F32)],
        compiler_params=_params("arbitrary"))(x, w, dh, dres)


def _adamw_math(w, g, m, v):
    nm = ADAM_B1 * m + (1.0 - ADAM_B1) * g
    nv = ADAM_B2 * v + (1.0 - ADAM_B2) * (g * g)
    m_hat = nm / (1.0 - ADAM_B1 ** ADAM_STEP)
    v_hat = nv / (1.0 - ADAM_B2 ** ADAM_STEP)
    return -ADAM_LR * (m_hat / (jnp.sqrt(v_hat) + ADAM_EPS) + ADAM_WD * w), nm, nv


def _adamw(w, g, m, v, name):
    r, c = w.shape
    tr = min(r, 256)
    assert r % tr == 0

    def body(w_ref, g_ref, m_ref, v_ref, g_out, d_ref, nm_ref, nv_ref):
        gv = g_ref[...]
        g_out[...] = gv
        d_ref[...], nm_ref[...], nv_ref[...] = _adamw_math(w_ref[...], gv, m_ref[...], v_ref[...])

    spec = pl.BlockSpec((tr, c), lambda i: (i, 0))
    shp = jax.ShapeDtypeStruct((r, c), F32)
    return pl.pallas_call(body, name=name, grid=(r // tr,), in_specs=[spec] * 4, out_specs=[spec] * 4,
                          out_shape=[shp] * 4, compiler_params=_params("parallel"))(w, g, m, v)


def _rope_tables(s, dim):
    inv = (1.0 / (ROPE_THETA ** (np.arange(0, dim, 2, dtype=np.float64) / dim))).astype(np.float32)
    ang = (np.arange(s, dtype=np.float32)[:, None] * inv[None, :]).astype(np.float64)
    return np.cos(ang).astype(np.float32), np.sin(ang).astype(np.float32)


def _rope_half(x, cos, sin):
    h = x.shape[1] // 2
    x1, x2 = x[:, :h], x[:, h:]
    return jnp.concatenate([x1 * cos - x2 * sin, x2 * cos + x1 * sin], axis=1)


def _unrope_half(dy, cos, sin):
    h = dy.shape[1] // 2
    d1, d2 = dy[:, :h], dy[:, h:]
    return jnp.concatenate([d1 * cos + d2 * sin, d2 * cos - d1 * sin], axis=1)


def _lane(shape):
    return lax.broadcasted_iota(jnp.int32, shape, 1)


def _partner64(x):
    w = x.shape[1]
    first = (_lane(x.shape) % HEAD_DIM) < (HEAD_DIM // 2)
    return jnp.where(first, pltpu.roll(x, w - HEAD_DIM // 2, 1), pltpu.roll(x, HEAD_DIM // 2, 1))


def _tile_lanes(t, reps):
    return t if reps == 1 else jnp.concatenate([t] * reps, axis=1)


def _group_mean(x, ones_bd, passes=2):
    hi = x.astype(BF16)
    if passes == 1:
        return _dot(hi, ones_bd)
    lo = (x - hi.astype(F32)).astype(BF16)
    return _dot(hi, ones_bd) + _dot(lo, ones_bd)


def _block_diag_mean(width):
    idx = jnp.arange(width) // HEAD_DIM
    return jnp.where(idx[:, None] == idx[None, :], 1.0 / HEAD_DIM, 0.0).astype(BF16)


RET_STEP = 4


def _retention_tables():
    h = RET_HEADS
    log_g = jnp.log(1.0 - 2.0 ** (-5.0 - jnp.arange(h, dtype=F32)))
    idx = jnp.arange(BLK, dtype=F32)
    diff = idx[:, None] - idx[None, :]
    intra = jnp.where(diff >= 0, jnp.exp(log_g[:, None, None] * jnp.maximum(diff, 0.0)), 0.0).astype(F32)
    q_dec = jnp.exp(log_g[:, None] * (idx[None, :] + 1.0)).astype(F32)[:, :, None]
    k_dec = jnp.exp(log_g[:, None] * (BLK - 1.0 - idx[None, :])).astype(F32)[:, :, None]
    chunk_dec = jnp.exp(log_g * BLK).astype(F32)[:, None, None]
    return intra, q_dec, k_dec, chunk_dec


def _retention_fwd(proj, cos, sin, tables, after=None):
    s = proj.shape[0]
    nc = s // BLK
    intra, q_dec, k_dec, chunk_dec = tables

    def body(p_ref, cos_ref, sin_ref, in_ref, qd_ref, kd_ref, cd_ref, *rest):
        o_ref, cat_ref, st_ref, state = rest[-4:]

        @pl.when(pl.program_id(0) == 0)
        def _():
            state[...] = jnp.zeros_like(state)

        for c in range(RET_STEP):
            rows = slice(c * BLK, (c + 1) * BLK)
            cosv, sinv = cos_ref[rows, :], sin_ref[rows, :]
            for h in range(RET_HEADS):
                c0 = h * RET_DIM
                q = p_ref[rows, c0:c0 + RET_DIM].astype(F32)
                k = p_ref[rows, RET_WIDTH + c0:RET_WIDTH + c0 + RET_DIM].astype(F32)
                v = p_ref[rows, 2 * RET_WIDTH + c0:2 * RET_WIDTH + c0 + RET_DIM]
                g = p_ref[rows, 3 * RET_WIDTH + c0:3 * RET_WIDTH + c0 + RET_DIM].astype(F32)
                qb = _rope_half(q, cosv, sinv).astype(BF16)
                kr = _rope_half(k, cosv, sinv) * (RET_DIM ** -0.5)
                kb = kr.astype(BF16)
                scores = _dot_nt(qb, kb) * in_ref[h]
                inner = _dot(scores.astype(BF16), v)
                prev = state[h]
                prev_b = prev.astype(BF16)
                st_ref[h, c] = prev_b
                o = inner + _dot(qb, prev_b) * qd_ref[h]
                o_ref[rows, c0:c0 + RET_DIM] = o
                rstd = lax.rsqrt(jnp.mean(o * o, axis=-1, keepdims=True) + EPS)
                cat_ref[rows, c0:c0 + RET_DIM] = (o * rstd * (g * _sigmoid(g))).astype(BF16)
                state[h] = cd_ref[h] * prev + _dot_tn((kr * kd_ref[h]).astype(BF16), v)

    full = lambda shape: pl.BlockSpec(shape, lambda n: (0,) * len(shape))
    step = RET_STEP * BLK
    return pl.pallas_call(
        body, name="retention_fwd", grid=(nc // RET_STEP,),
        in_specs=[pl.BlockSpec((step, 4 * RET_WIDTH), lambda n: (n, 0)),
                  pl.BlockSpec((step, RET_DIM // 2), lambda n: (n, 0)), pl.BlockSpec((step, RET_DIM // 2), lambda n: (n, 0)),
                  full((RET_HEADS, BLK, BLK)), full((RET_HEADS, BLK, 1)), full((RET_HEADS, BLK, 1)), full((RET_HEADS, 1, 1))]
        + ([ANY] if after is not None else []),
        out_specs=[pl.BlockSpec((step, RET_WIDTH), lambda n: (n, 0)), pl.BlockSpec((step, RET_WIDTH), lambda n: (n, 0)),
                   pl.BlockSpec((RET_HEADS, RET_STEP, RET_DIM, RET_DIM), lambda n: (0, n, 0, 0))],
        out_shape=[jax.ShapeDtypeStruct((s, RET_WIDTH), F32), jax.ShapeDtypeStruct((s, D_MODEL), BF16),
                   jax.ShapeDtypeStruct((RET_HEADS, nc, RET_DIM, RET_DIM), BF16)],
        scratch_shapes=[pltpu.VMEM((RET_HEADS, RET_DIM, RET_DIM), F32)],
        compiler_params=_params("arbitrary"),
    )(proj, cos, sin, intra, q_dec, k_dec, chunk_dec, *((after,) if after is not None else ()))


def _retention_bwd(proj, o, states, dcat, cos, sin, tables):
    s = proj.shape[0]
    nc = s // BLK
    intra, q_dec, k_dec, chunk_dec = tables

    def body(p_ref, o_ref, st_ref, dc_ref, cos_ref, sin_ref, in_ref, qd_ref, kd_ref, cd_ref, dp_ref, dstate):
        @pl.when(pl.program_id(0) == 0)
        def _():
            dstate[...] = jnp.zeros_like(dstate)

        for c in reversed(range(RET_STEP)):
            rows = slice(c * BLK, (c + 1) * BLK)
            cosv, sinv = cos_ref[rows, :], sin_ref[rows, :]
            for h in range(RET_HEADS):
                c0 = h * RET_DIM
                q = p_ref[rows, c0:c0 + RET_DIM].astype(F32)
                k = p_ref[rows, RET_WIDTH + c0:RET_WIDTH + c0 + RET_DIM].astype(F32)
                v = p_ref[rows, 2 * RET_WIDTH + c0:2 * RET_WIDTH + c0 + RET_DIM]
                g = p_ref[rows, 3 * RET_WIDTH + c0:3 * RET_WIDTH + c0 + RET_DIM].astype(F32)
                o = o_ref[rows, c0:c0 + RET_DIM]
                dc = dc_ref[rows, c0:c0 + RET_DIM].astype(F32)
                rstd = lax.rsqrt(jnp.mean(o * o, axis=-1, keepdims=True) + EPS)
                nrm = o * rstd
                sg = _sigmoid(g)
                dg = dc * nrm * (sg * (1.0 + g * (1.0 - sg)))
                dn = dc * (g * sg)
                do = rstd * (dn - nrm * jnp.mean(dn * nrm, axis=-1, keepdims=True))
                qb = _rope_half(q, cosv, sinv).astype(BF16)
                kr = _rope_half(k, cosv, sinv) * (RET_DIM ** -0.5)
                kb = kr.astype(BF16)
                mask = in_ref[h]
                qd, kd = qd_ref[h], kd_ref[h]
                prev_b = st_ref[h, c]
                dnext = dstate[h]
                dnext_b = dnext.astype(BF16)
                att = (_dot_nt(qb, kb) * mask).astype(BF16)
                do_b = do.astype(BF16)
                doq = (do * qd).astype(BF16)
                dv = _dot_tn(att, do_b) + _dot((kr * kd).astype(BF16), dnext_b)
                ds = (_dot_nt(do_b, v) * mask).astype(BF16)
                dqr = _dot(ds, kb) + _dot_nt(doq, prev_b)
                dkr = _dot_tn(ds, qb) + _dot_nt(v, dnext_b) * kd
                dstate[h] = cd_ref[h] * dnext + _dot_tn(qb, doq)
                dq = _unrope_half(dqr, cosv, sinv)
                dk = _unrope_half(dkr * (RET_DIM ** -0.5), cosv, sinv)
                dp_ref[rows, c0:c0 + RET_DIM] = dq.astype(BF16)
                dp_ref[rows, RET_WIDTH + c0:RET_WIDTH + c0 + RET_DIM] = dk.astype(BF16)
                dp_ref[rows, 2 * RET_WIDTH + c0:2 * RET_WIDTH + c0 + RET_DIM] = dv.astype(BF16)
                dp_ref[rows, 3 * RET_WIDTH + c0:3 * RET_WIDTH + c0 + RET_DIM] = dg.astype(BF16)

    steps = nc // RET_STEP
    rev = lambda n: steps - 1 - n
    full = lambda shape: pl.BlockSpec(shape, lambda n: (0,) * len(shape))
    step = RET_STEP * BLK
    return pl.pallas_call(
        body, name="retention_bwd", grid=(steps,),
        in_specs=[pl.BlockSpec((step, 4 * RET_WIDTH), lambda n: (rev(n), 0)),
                  pl.BlockSpec((step, RET_WIDTH), lambda n: (rev(n), 0)),
                  pl.BlockSpec((RET_HEADS, RET_STEP, RET_DIM, RET_DIM), lambda n: (0, rev(n), 0, 0)),
                  pl.BlockSpec((step, RET_WIDTH), lambda n: (rev(n), 0)),
                  pl.BlockSpec((step, RET_DIM // 2), lambda n: (rev(n), 0)), pl.BlockSpec((step, RET_DIM // 2), lambda n: (rev(n), 0)),
                  full((RET_HEADS, BLK, BLK)), full((RET_HEADS, BLK, 1)), full((RET_HEADS, BLK, 1)), full((RET_HEADS, 1, 1))],
        out_specs=pl.BlockSpec((step, 4 * RET_WIDTH), lambda n: (rev(n), 0)),
        out_shape=jax.ShapeDtypeStruct((s, EVEN_IN), BF16),
        scratch_shapes=[pltpu.VMEM((RET_HEADS, RET_DIM, RET_DIM), F32)],
        compiler_params=_params("arbitrary"))(proj, o, states, dcat, cos, sin, intra, q_dec, k_dec, chunk_dec)


CONV_ROWS = 256
HALO = 16


def _conv_pieces(p, halo, first):
    gb, gc, u, gv = (p[:, i * CONV_WIDTH:(i + 1) * CONV_WIDTH] for i in range(4))
    cu = gc * u
    hcu = halo[:, CONV_WIDTH:2 * CONV_WIDTH] * halo[:, 2 * CONV_WIDTH:3 * CONV_WIDTH]
    hcu = jnp.where(first, 0.0, hcu)
    r1, r2 = hcu[HALO - 1:HALO], hcu[HALO - 2:HALO - 1]
    row = lax.broadcasted_iota(jnp.int32, cu.shape, 0)
    m1 = jnp.where(row == 0, r1, pltpu.roll(cu, 1, 0))
    m2 = jnp.where(row == 0, r2, jnp.where(row == 1, r1, pltpu.roll(cu, 2, 0)))
    return gb, gc, u, gv, cu, m1, m2


def _conv_fwd(proj, conv_w, cat):
    s = proj.shape[0]
    per = CONV_ROWS // HALO

    def body(p_ref, halo_ref, w_ref, cat_in, cat_ref):
        del cat_in
        first = pl.program_id(0) == 0
        gb, _, _, gv, cu, m1, m2 = _conv_pieces(p_ref[...].astype(F32), halo_ref[...].astype(F32), first)
        conv = w_ref[0:1, :] * m2 + w_ref[1:2, :] * m1 + w_ref[2:3, :] * cu
        cat_ref[...] = (gb * conv * (gv * _sigmoid(gv))).astype(BF16)

    return pl.pallas_call(
        body, name="conv_fwd", grid=(s // CONV_ROWS,),
        in_specs=[pl.BlockSpec((CONV_ROWS, 4 * CONV_WIDTH), lambda i: (i, 1)),
                  pl.BlockSpec((HALO, 4 * CONV_WIDTH), lambda i: (jnp.maximum(i * per - 1, 0), 1)),
                  pl.BlockSpec((3, CONV_WIDTH), lambda i: (0, 0)), ANY],
        out_specs=pl.BlockSpec((CONV_ROWS, CONV_WIDTH), lambda i: (i, 1)),
        out_shape=jax.ShapeDtypeStruct(cat.shape, cat.dtype), input_output_aliases={3: 0},
        compiler_params=_params("parallel"))(proj, proj, conv_w, cat)


def _conv_bwd(proj, dcat, conv_w, dproj):
    s = proj.shape[0]
    per = CONV_ROWS // HALO
    last_halo = s // HALO - 1
    nsteps = s // CONV_ROWS

    def body(p_ref, halo_ref, nxt_ref, dc_ref, dnxt_ref, w_ref, dp_in, dp_ref, dw_ref):
        del dp_in
        i = pl.program_id(0)
        gb, gc, u, gv, cu, m1, m2 = _conv_pieces(p_ref[...].astype(F32), halo_ref[...].astype(F32), i == 0)
        w0, w1, w2 = w_ref[0:1, :], w_ref[1:2, :], w_ref[2:3, :]
        conv = w0 * m2 + w1 * m1 + w2 * cu
        dco = dc_ref[...].astype(F32)
        sg = _sigmoid(gv)
        silu = gv * sg
        dgb = dco * conv * silu
        dco_gb = dco * gb
        dgv = dco_gb * conv * (sg * (1.0 + gv * (1.0 - sg)))
        dconv = dco_gb * silu
        nxt = nxt_ref[...].astype(F32)
        ngv = nxt[:, 3 * CONV_WIDTH:]
        dnext = dnxt_ref[...].astype(F32) * nxt[:, :CONV_WIDTH] * (ngv * _sigmoid(ngv))
        dnext = jnp.where(i == nsteps - 1, 0.0, dnext)
        n1, n2 = dnext[0:1], dnext[1:2]
        row = lax.broadcasted_iota(jnp.int32, dconv.shape, 0)
        p1 = jnp.where(row == CONV_ROWS - 1, n1, pltpu.roll(dconv, CONV_ROWS - 1, 0))
        p2 = jnp.where(row == CONV_ROWS - 1, n2, jnp.where(row == CONV_ROWS - 2, n1, pltpu.roll(dconv, CONV_ROWS - 2, 0)))
        dcu = w2 * dconv + w1 * p1 + w0 * p2
        dp_ref[...] = jnp.concatenate([dgb, dcu * u, dcu * gc, dgv], axis=1).astype(BF16)

        @pl.when(i == 0)
        def _():
            dw_ref[...] = jnp.zeros_like(dw_ref)

        taps = [jnp.sum(dconv * m, axis=0, keepdims=True) for m in (m2, m1, cu)]
        r8 = lax.broadcasted_iota(jnp.int32, dw_ref.shape, 0)
        dw_ref[...] += jnp.where(r8 == 0, taps[0], jnp.where(r8 == 1, taps[1], jnp.where(r8 == 2, taps[2], 0.0)))

    return pl.pallas_call(
        body, name="conv_bwd", grid=(nsteps,),
        in_specs=[pl.BlockSpec((CONV_ROWS, 4 * CONV_WIDTH), lambda i: (i, 1)),
                  pl.BlockSpec((HALO, 4 * CONV_WIDTH), lambda i: (jnp.maximum(i * per - 1, 0), 1)),
                  pl.BlockSpec((HALO, 4 * CONV_WIDTH), lambda i: (jnp.minimum((i + 1) * per, last_halo), 1)),
                  pl.BlockSpec((CONV_ROWS, CONV_WIDTH), lambda i: (i, 1)),
                  pl.BlockSpec((HALO, CONV_WIDTH), lambda i: (jnp.minimum((i + 1) * per, last_halo), 1)),
                  pl.BlockSpec((3, CONV_WIDTH), lambda i: (0, 0)), ANY],
        out_specs=[pl.BlockSpec((CONV_ROWS, 4 * CONV_WIDTH), lambda i: (i, 1)), pl.BlockSpec((8, CONV_WIDTH), lambda i: (0, 0))],
        out_shape=[jax.ShapeDtypeStruct(dproj.shape, dproj.dtype), jax.ShapeDtypeStruct((8, CONV_WIDTH), F32)],
        input_output_aliases={6: 0},
        compiler_params=_params("arbitrary"))(proj, proj, proj, dcat, dcat, conv_w, dproj)


GROUP_HEADS = Q_HEADS // KV_HEADS
GROUP_WIDTH = GROUP_HEADS * HEAD_DIM
SLAB = 2 * HEAD_DIM
KV_COL = ATTN_WIDTH // GROUP_WIDTH
GATE_COL = KV_COL + 1
ATTN_SCALE = HEAD_DIM ** -0.5
KV_ROWS = 1024


def _half_mask(shape, which):
    return (_lane(shape) // HEAD_DIM) == which


def _dup_head(slab, which):
    kept = jnp.where(_half_mask(slab.shape, which), slab, 0.0)
    return kept + pltpu.roll(kept, HEAD_DIM, 1)


def _stack_heads(x):
    parts = []
    for sl in range(GROUP_WIDTH // SLAB):
        slab = x[:, sl * SLAB:(sl + 1) * SLAB]
        parts += [jnp.where(_half_mask(slab.shape, e), slab, 0.0) for e in range(2)]
    return jnp.concatenate(parts, axis=0)


def _unstack_heads(y):
    slabs = []
    for sl in range(GROUP_WIDTH // SLAB):
        a, b = y[(2 * sl) * BLK:(2 * sl + 1) * BLK], y[(2 * sl + 1) * BLK:(2 * sl + 2) * BLK]
        slabs.append(jnp.where(_half_mask(a.shape, 0), a, b))
    return jnp.concatenate(slabs, axis=1)


def _q_prep(q, qw, cosf, sins, ones_bd):
    rstd = lax.rsqrt(_group_mean(q * q, ones_bd) + EPS)
    nrm = q * rstd
    y = nrm * qw
    return nrm, rstd, y * cosf + _partner64(y) * sins


def _band(tri_ref, n):
    own = tri_ref[...] > 0.5
    return own, jnp.where(jnp.logical_and(n == 0, jnp.logical_not(own)), -1e30, 0.0)


def _fold(pair, own):
    return jnp.where(own, pair[:, BLK:], pair[:, :BLK])


def _unfold(folded, own):
    return jnp.concatenate([jnp.where(own, 0.0, folded), jnp.where(own, folded, 0.0)], axis=1)


def _head_probs(raw_scores, sink, own, bias):
    sc = _fold(raw_scores, own) + bias
    m = jnp.maximum(jnp.max(sc, axis=-1, keepdims=True), sink)
    p = jnp.exp(sc - m)
    psink = jnp.exp(sink - m)
    inv = 1.0 / (jnp.sum(p, axis=-1, keepdims=True) + psink)
    return p * inv, psink * inv


def _k_prep(k, kw, cosf, sins, ones_bd):
    rstd = lax.rsqrt(_group_mean(k * k, ones_bd) + EPS)
    nrm = k * rstd
    y = nrm * kw
    return nrm, rstd, y * cosf + _partner64(y) * sins


def _qk_prep(proj, qw, kw, cos, sins, ones_q, ones_kv):
    s = proj.shape[0]
    rows = min(KV_ROWS, s)

    def body(p_ref, qw_ref, kw_ref, cos_ref, sin_ref, oq_ref, ok_ref, o_ref):
        j = pl.program_id(1)

        @pl.when(j < KV_COL)
        def _():
            cosf, sinf = _tile_lanes(cos_ref[...], 4), _tile_lanes(sin_ref[...], 4)
            roped = _q_prep(p_ref[...].astype(F32), qw_ref[...], cosf, sinf, oq_ref[...])[2]
            o_ref[...] = (roped * ATTN_SCALE).astype(BF16)

        @pl.when(j == KV_COL)
        def _():
            cosf, sinf = _tile_lanes(cos_ref[...], 2), _tile_lanes(sin_ref[...], 2)
            kr = _k_prep(p_ref[:, :KV_WIDTH].astype(F32), kw_ref[...], cosf, sinf, ok_ref[...])[2]
            o_ref[...] = jnp.concatenate([kr.astype(BF16), p_ref[:, KV_WIDTH:]], axis=1)

    full = lambda shape: pl.BlockSpec(shape, lambda i, j: (0,) * len(shape))
    tab = pl.BlockSpec((rows, SLAB), lambda i, j: (i, 0))
    blk = pl.BlockSpec((rows, GROUP_WIDTH), lambda i, j: (i, j))
    return pl.pallas_call(
        body, name="qk_prep", grid=(s // rows, KV_COL + 1),
        in_specs=[blk, full((1, GROUP_WIDTH)), full((1, KV_WIDTH)), tab, tab, full((GROUP_WIDTH, GROUP_WIDTH)),
                  full((KV_WIDTH, KV_WIDTH))],
        out_specs=blk, out_shape=jax.ShapeDtypeStruct((s, ATTN_WIDTH + 2 * KV_WIDTH), BF16),
        compiler_params=_params("parallel", "arbitrary"))(proj, qw, kw, cos, sins, ones_q, ones_kv)


def _keys_values(kc_ref, kp_ref, vc_ref, vp_ref, head):
    lanes = slice((head // 2) * SLAB, (head // 2 + 1) * SLAB)
    dup = lambda ref: _dup_head(ref[:, lanes].astype(F32), head % 2)
    return (jnp.concatenate([dup(kp_ref), dup(kc_ref)], axis=0).astype(BF16),
            jnp.concatenate([dup(vp_ref), dup(vc_ref)], axis=0).astype(BF16))


FWD_STEP_HEADS = 4
BWD_STEP_HEADS = 2


def _swa_specs(heads):
    kv_width = heads * HEAD_DIM
    prev = lambda n: jnp.maximum(n - 1, 0)
    kv = lambda col0, row: pl.BlockSpec((BLK, kv_width), lambda gs, n: (row(n), col0 // kv_width + gs))
    cur = lambda n: n
    full = lambda shape: pl.BlockSpec(shape, lambda gs, n: (0,) * len(shape))
    gate = lambda t: pl.BlockSpec((BLK, GROUP_WIDTH), lambda gs, n: (n, GATE_COL + heads * gs + t))
    return dict(
        sinks=pl.BlockSpec(memory_space=pltpu.SMEM), gates=[gate(t) for t in range(heads)],
        kc=kv(ATTN_WIDTH, cur), kp=kv(ATTN_WIDTH, prev), vc=kv(ATTN_WIDTH + KV_WIDTH, cur), vp=kv(ATTN_WIDTH + KV_WIDTH, prev),
        tri=full((BLK, BLK)), step=pl.BlockSpec((BLK, heads * GROUP_WIDTH), lambda gs, n: (n, gs)))


def _lower_triangle():
    return jnp.tril(jnp.ones((BLK, BLK), F32))


def _swa_fwd(qk, proj, sinks):
    s = proj.shape[0]
    nb = s // BLK
    heads = FWD_STEP_HEADS
    sp = _swa_specs(heads)

    def body(sink_ref, q_ref, kc_ref, kp_ref, vc_ref, vp_ref, *rest):
        gate_refs, (tri_ref, ag_ref, o_ref) = rest[:heads], rest[heads:]
        gs, n = pl.program_id(0), pl.program_id(1)
        own, bias = _band(tri_ref, n)
        for t, gate_ref in enumerate(gate_refs):
            cols = slice(t * GROUP_WIDTH, (t + 1) * GROUP_WIDTH)
            first_head = (heads * gs + t) * GROUP_HEADS
            kcat, vcat = _keys_values(kc_ref, kp_ref, vc_ref, vp_ref, t)
            scores = _dot_nt(_stack_heads(q_ref[:, cols]), kcat)
            probs = []
            for j in range(GROUP_HEADS):
                p, _ = _head_probs(scores[j * BLK:(j + 1) * BLK], sink_ref[first_head + j], own, bias)
                probs.append(_unfold(p, own).astype(BF16))
            o = _unstack_heads(_dot(jnp.concatenate(probs, axis=0), vcat))
            gate = gate_ref[...].astype(F32)
            o_ref[:, cols] = o.astype(BF16)
            ag_ref[:, cols] = (o * (gate * _sigmoid(gate))).astype(BF16)

    shp = jax.ShapeDtypeStruct((s, ATTN_WIDTH), BF16)
    return pl.pallas_call(
        body, name="swa_fwd", grid=(KV_HEADS // heads, nb),
        in_specs=[sp["sinks"], sp["step"], sp["kc"], sp["kp"], sp["vc"], sp["vp"], *sp["gates"], sp["tri"]],
        out_specs=[sp["step"], sp["step"]], out_shape=[shp, shp],
        compiler_params=_params("parallel", "arbitrary"),
    )(sinks, qk, qk, qk, qk, qk, *[proj] * heads, _lower_triangle())


def _swa_bwd(qk, proj, dag, sinks):
    s = proj.shape[0]
    nb = s // BLK
    heads = BWD_STEP_HEADS
    sp = _swa_specs(heads)

    def body(sink_ref, q_ref, kc_ref, kp_ref, vc_ref, vp_ref, *rest):
        gate_refs = rest[:heads]
        dag_ref, tri_ref, dq_ref, dkc_ref, dkp_ref, dvc_ref, dvp_ref, dsink_ref = rest[heads:]
        gs, n = pl.program_id(0), pl.program_id(1)
        own, bias = _band(tri_ref, n)

        @pl.when(n == 0)
        def _():
            dsink_ref[...] = jnp.zeros_like(dsink_ref)

        for t, gate_ref in enumerate(gate_refs):
            cols = slice(t * GROUP_WIDTH, (t + 1) * GROUP_WIDTH)
            first_head = (heads * gs + t) * GROUP_HEADS
            kcat, vcat = _keys_values(kc_ref, kp_ref, vc_ref, vp_ref, t)
            gate = gate_ref[...].astype(F32)
            do = dag_ref[:, cols].astype(F32) * (gate * _sigmoid(gate))
            q_stack = _stack_heads(q_ref[:, cols])
            do_stack = _stack_heads(do).astype(BF16)
            scores = _dot_nt(q_stack, kcat)
            dprobs = _dot_nt(do_stack, vcat)
            probs, dscores, dsinks = [], [], []
            for j in range(GROUP_HEADS):
                rows = slice(j * BLK, (j + 1) * BLK)
                p, psink = _head_probs(scores[rows], sink_ref[first_head + j], own, bias)
                dp = _fold(dprobs[rows], own)
                delta = jnp.sum(p * dp, axis=-1, keepdims=True)
                probs.append(_unfold(p, own).astype(BF16))
                dscores.append(_unfold(p * (dp - delta), own).astype(BF16))
                dsinks.append(-jnp.sum(psink * delta, axis=0, keepdims=True))
            ds = jnp.concatenate(dscores, axis=0)
            dk = _dot_tn(ds, q_stack)
            dv = _dot_tn(jnp.concatenate(probs, axis=0), do_stack)
            dk = dk + pltpu.roll(dk, HEAD_DIM, 1)
            dv = dv + pltpu.roll(dv, HEAD_DIM, 1)
            dkp_ref[t], dkc_ref[t] = dk[:BLK], dk[BLK:]
            dvp_ref[t], dvc_ref[t] = dv[:BLK], dv[BLK:]
            dq_ref[:, cols] = _unstack_heads(_dot(ds, kcat)).astype(BF16)
            r8 = lax.broadcasted_iota(jnp.int32, (8, SLAB), 0)
            upd = jnp.zeros((8, SLAB), F32)
            for j in range(GROUP_HEADS):
                upd = jnp.where(r8 == j, dsinks[j], upd)
            dsink_ref[t] += upd

    cur_out = pl.BlockSpec((heads, BLK, SLAB), lambda gs, n: (gs, n, 0))
    prev_out = pl.BlockSpec((heads, BLK, SLAB), lambda gs, n: (gs, (n + nb - 1) % nb, 0))
    kv_shape = jax.ShapeDtypeStruct((KV_HEADS, s, SLAB), F32)
    return pl.pallas_call(
        body, name="swa_bwd", grid=(KV_HEADS // heads, nb),
        in_specs=[sp["sinks"], sp["step"], sp["kc"], sp["kp"], sp["vc"], sp["vp"], *sp["gates"], sp["step"], sp["tri"]],
        out_specs=[sp["step"], cur_out, prev_out, cur_out, prev_out,
                   pl.BlockSpec((heads, 8, SLAB), lambda gs, n: (gs, 0, 0))],
        out_shape=[jax.ShapeDtypeStruct((s, ATTN_WIDTH), BF16), kv_shape, kv_shape, kv_shape, kv_shape,
                   jax.ShapeDtypeStruct((KV_HEADS, 8, SLAB), F32)],
        compiler_params=_params("parallel", "arbitrary"),
    )(sinks, qk, qk, qk, qk, qk, *[proj] * heads, dag, _lower_triangle())


def _swa_bwd_finish(proj, dqr, o, dag, dkc, dkp, dvc, dvp, qw, kw, cos, sins, ones_q, ones_kv):
    s = proj.shape[0]
    rows = min(KV_ROWS, s)
    n_q = KV_COL
    q_of = lambda j: jnp.clip(j - 1, 0, n_q - 1)
    gate_of = lambda j: jnp.clip(j - 1 - n_q, 0, n_q - 1)

    def body(kv_ref, p_ref, dqr_ref, o_ref, dag_ref, dkc_ref, dkp_ref, dvc_ref, dvp_ref, qw_ref, kw_ref, cos_ref, sin_ref,
             oq_ref, ok_ref, dp_ref, dqw_ref, dkw_ref):
        j, i = pl.program_id(0), pl.program_id(1)

        @pl.when(j == 0)
        def _():
            @pl.when(i == 0)
            def _():
                dkw_ref[...] = jnp.zeros_like(dkw_ref)

            def assemble(cur_ref, prv_ref):
                tot = [cur_ref[h] + prv_ref[h] for h in range(KV_HEADS)]
                first = _half_mask(tot[0].shape, 0)
                return jnp.concatenate([jnp.where(first, tot[0], tot[1]), jnp.where(first, tot[2], tot[3])], axis=1)

            dkr = assemble(dkc_ref, dkp_ref)
            dv = assemble(dvc_ref, dvp_ref)
            cosf, sinf = _tile_lanes(cos_ref[...], 2), _tile_lanes(sin_ref[...], 2)
            nrm, rstd, _ = _k_prep(kv_ref[:, :KV_WIDTH].astype(F32), kw_ref[...], cosf, sinf, ok_ref[...])
            dy = dkr * cosf + _partner64(dkr * sinf)
            dn = dy * kw_ref[...]
            dk = rstd * (dn - nrm * _group_mean(dn * nrm, ok_ref[...], passes=1))
            dp_ref[...] = jnp.concatenate([dk, dv], axis=1).astype(BF16)
            dkw_ref[...] += jnp.sum(dy * nrm, axis=0, keepdims=True)

        @pl.when(jnp.logical_and(j >= 1, j <= n_q))
        def _():
            @pl.when(i == 0)
            def _():
                dqw_ref[...] = jnp.zeros_like(dqw_ref)

            cosf, sinf = _tile_lanes(cos_ref[...], 4), _tile_lanes(sin_ref[...], 4)
            nrm, rstd, _ = _q_prep(p_ref[...].astype(F32), qw_ref[...], cosf, sinf, oq_ref[...])
            dq = dqr_ref[...].astype(F32) * ATTN_SCALE
            dy = dq * cosf + _partner64(dq * sinf)
            dn = dy * qw_ref[...]
            dp_ref[...] = (rstd * (dn - nrm * _group_mean(dn * nrm, oq_ref[...], passes=1))).astype(BF16)
            dqw_ref[0] += jnp.sum(dy * nrm, axis=0, keepdims=True)

        @pl.when(j > n_q)
        def _():
            gate = p_ref[...].astype(F32)
            sg = _sigmoid(gate)
            dp_ref[...] = (dag_ref[...].astype(F32) * o_ref[...].astype(F32) * (sg * (1.0 + gate * (1.0 - sg)))).astype(BF16)

    first_pass = lambda j, i: jnp.where(j == 0, i, 0)
    acc = pl.BlockSpec((KV_HEADS, rows, SLAB), lambda j, i: (0, first_pass(j, i), 0))
    full = lambda shape: pl.BlockSpec(shape, lambda j, i: (0,) * len(shape))
    tab = pl.BlockSpec((rows, SLAB), lambda j, i: (i, 0))
    out_col = lambda j: jnp.where(j == 0, KV_COL, jnp.where(j <= n_q, j - 1, j))
    return pl.pallas_call(
        body, name="swa_bwd_finish", grid=(2 * n_q + 1, s // rows),
        in_specs=[pl.BlockSpec((rows, GROUP_WIDTH), lambda j, i: (first_pass(j, i), KV_COL)),
                  pl.BlockSpec((rows, GROUP_WIDTH), lambda j, i: (jnp.where(j == 0, 0, i), jnp.where(j <= n_q, q_of(j), j))),
                  pl.BlockSpec((rows, GROUP_WIDTH), lambda j, i: (jnp.where(jnp.logical_and(j >= 1, j <= n_q), i, 0), q_of(j))),
                  pl.BlockSpec((rows, GROUP_WIDTH), lambda j, i: (jnp.where(j > n_q, i, 0), gate_of(j))),
                  pl.BlockSpec((rows, GROUP_WIDTH), lambda j, i: (jnp.where(j > n_q, i, 0), gate_of(j))),
                  acc, acc, acc, acc, full((1, GROUP_WIDTH)), full((1, KV_WIDTH)), tab, tab,
                  full((GROUP_WIDTH, GROUP_WIDTH)), full((KV_WIDTH, KV_WIDTH))],
        out_specs=[pl.BlockSpec((rows, GROUP_WIDTH), lambda j, i: (i, out_col(j))),
                   pl.BlockSpec((1, 8, GROUP_WIDTH), lambda j, i: (q_of(j), 0, 0)), pl.BlockSpec((8, KV_WIDTH), lambda j, i: (0, 0))],
        out_shape=[jax.ShapeDtypeStruct((s, ODD_IN), BF16), jax.ShapeDtypeStruct((n_q, 8, GROUP_WIDTH), F32),
                   jax.ShapeDtypeStruct((8, KV_WIDTH), F32)],
        compiler_params=_params("arbitrary", "arbitrary"),
    )(proj, proj, dqr, o, dag, dkc, dkp, dvc, dvp, qw, kw, cos, sins, ones_q, ones_kv)


def _place():
    x, y, c = lax.axis_index("x"), lax.axis_index("y"), lax.axis_index("c")
    return x, y, c


OTHER_CHIPS = ((1, 0), (0, 1), (1, 1))


def _half_rows(ref, half, rows):
    return ref.at[pl.ds(pl.multiple_of(half * (rows // 2), 8), rows // 2)]


DMA_CHUNK_BYTES = 1 << 20
BF16_TILE_ROWS = 16


def _n_chunks(ref):
    rows = ref.shape[-2]
    nbytes = math.prod(ref.shape) * jnp.dtype(ref.dtype).itemsize
    n = 1
    while 2 * n * DMA_CHUNK_BYTES <= nbytes and rows % (2 * n * BF16_TILE_ROWS) == 0:
        n *= 2
    return n


def _row_chunk(ref, k, n):
    rows = ref.shape[-2] // n
    return ref.at[pl.ds(k * rows, rows)] if len(ref.shape) == 2 else ref.at[:, pl.ds(k * rows, rows)]


def _push(src, dst, send_sem, recv_sem, device_id):
    n = _n_chunks(src)
    for k in range(n):
        pltpu.make_async_remote_copy(src_ref=_row_chunk(src, k, n), dst_ref=_row_chunk(dst, k, n), send_sem=send_sem,
                                     recv_sem=recv_sem, device_id=device_id, device_id_type=MESH).start()
    return pltpu.make_async_remote_copy(src_ref=src, dst_ref=dst, send_sem=send_sem, recv_sem=recv_sem,
                                        device_id=device_id, device_id_type=MESH)


HBM = pl.BlockSpec(memory_space=pltpu.HBM)
SEM = pl.BlockSpec(memory_space=pltpu.SEMAPHORE)
SPLIT_COPY_EFFECT = pltpu.SideEffectType.DATAFLOW_SIDE_EFFECTING


def _in_hbm(a):
    return pltpu.with_memory_space_constraint(a, pltpu.HBM)


def _start_copies(name, arrays, plan, n_copies, after=None):
    n = len(arrays)

    def body(*refs):
        send_sem, recv_sem = refs[-n - 3], refs[-n - 2]
        for k, (src, dst, peer) in enumerate(plan(refs[:n])):
            _push(src, dst, send_sem.at[k], recv_sem.at[k], peer)
        refs[-1][...] = jnp.zeros_like(refs[-1])

    dma = pltpu.SemaphoreType.DMA((n_copies,))
    outs = pl.pallas_call(
        body, name=name,
        out_shape=(dma, dma, *[pltpu.HBM(a.shape, a.dtype) for a in arrays], jax.ShapeDtypeStruct((8, 128), F32)),
        in_specs=[HBM] * n + ([ANY] if after is not None else []),
        out_specs=(SEM, SEM, *[HBM] * n, pl.BlockSpec(memory_space=pltpu.VMEM)),
        input_output_aliases={i: i + 2 for i in range(n)},
        compiler_params=pltpu.CompilerParams(has_side_effects=SPLIT_COPY_EFFECT),
    )(*[_in_hbm(a) for a in arrays], *((after,) if after is not None else ()))
    return outs[0], outs[1], list(outs[2:2 + n]), outs[-1]


def _wait_copies(name, send_sem, recv_sem, arrays, plan, after):
    n = len(arrays)
    after = list(after) if isinstance(after, (list, tuple)) else [after]

    def body(*refs):
        send_ref, recv_ref = refs[n], refs[n + 1]
        for k, (src, dst, peer) in enumerate(plan(refs[:n])):
            cp = pltpu.make_async_remote_copy(src_ref=src, dst_ref=dst, send_sem=send_ref.at[k], recv_sem=recv_ref.at[k],
                                              device_id=peer, device_id_type=MESH)
            cp.wait_send()
            cp.wait_recv()

    return list(pl.pallas_call(
        body, name=name, out_shape=tuple(pltpu.HBM(a.shape, a.dtype) for a in arrays),
        in_specs=[HBM] * n + [SEM, SEM] + [ANY] * len(after), out_specs=tuple([HBM] * n),
        input_output_aliases={i: i for i in range(n)},
        compiler_params=pltpu.CompilerParams(has_side_effects=SPLIT_COPY_EFFECT),
    )(*arrays, send_sem, recv_sem, *after))


def _gather_region(full, kind, chip, half=None):
    if kind == "whole":
        return full.at[chip]
    if kind == "col":
        rows, width = full.shape[0], full.shape[1] // N_CHIPS
        piece = full.at[:, pl.ds(pl.multiple_of(chip * width, 128), width)]
    else:
        rows = full.shape[0] // N_CHIPS
        piece = full.at[pl.ds(pl.multiple_of(chip * rows, BF16_TILE_ROWS), rows)]
    return piece if half is None else _half_rows(piece, half, rows)


def _gather_plan(kinds):
    def plan(fulls):
        x, y, c = _place()
        copies = []
        for fx, fy in OTHER_CHIPS:
            for full, kind in zip(fulls, kinds):
                mine = _gather_region(full, kind, 2 * x + y, c)
                copies.append((mine, mine, (x ^ fx, y ^ fy, c)))
        return copies

    return plan


def _gather_start(name, fulls, kinds, after=None):
    return _start_copies(name, list(fulls), _gather_plan(kinds), 3 * len(kinds), after)


def _pass_on_plan(kinds):
    split = [i for i, kind in enumerate(kinds) if kind != "whole"]

    def plan(fulls):
        x, y, c = _place()
        copies = []
        for fx, fy in OTHER_CHIPS:
            chip = 2 * (x ^ fx) + (y ^ fy)
            for i in split:
                landed = _gather_region(fulls[i], kinds[i], chip, c)
                copies.append((landed, landed, (x, y, 1 - c)))
        return copies

    return plan, 3 * len(split)


def _gather_arrive(name, started, kinds, after):
    send_sem, recv_sem, fulls, _ = started
    fulls = _wait_copies(name + "_wait", send_sem, recv_sem, fulls, _gather_plan(kinds), after)
    plan, count = _pass_on_plan(kinds)
    return _start_copies(name + "_pass_on", fulls, plan, count)


def _gather_done(name, passed_on, kinds, after):
    send_sem, recv_sem, fulls, _ = passed_on
    return _wait_copies(name + "_pass_on_wait", send_sem, recv_sem, fulls, _pass_on_plan(kinds)[0], after)


def _allreduce_small(v):
    def body(v_ref, out_ref, buf, send_sems, recv_sems):
        x, y, c = _place()
        me = 4 * x + 2 * y + c
        buf[me] = v_ref[...]
        copies = []
        for r in range(1, N_DEV):
            peer = (x ^ (r >> 2), y ^ ((r >> 1) & 1), c ^ (r & 1))
            cp = pltpu.make_async_remote_copy(src_ref=v_ref, dst_ref=buf.at[me], send_sem=send_sems.at[r - 1],
                                              recv_sem=recv_sems.at[r - 1], device_id=peer, device_id_type=MESH)
            cp.start()
            copies.append(cp)
        for cp in copies:
            cp.wait_recv()
        for cp in copies:
            cp.wait_send()
        total = buf[0]
        for d in range(1, N_DEV):
            total = total + buf[d]
        out_ref[...] = total

    vm = pl.BlockSpec(memory_space=pltpu.VMEM)
    return pl.pallas_call(
        body, name="allreduce_small", in_specs=[vm], out_specs=vm, out_shape=jax.ShapeDtypeStruct(v.shape, v.dtype),
        scratch_shapes=[pltpu.VMEM((N_DEV,) + v.shape, v.dtype), pltpu.SemaphoreType.DMA((N_DEV - 1,)),
                        pltpu.SemaphoreType.DMA((N_DEV - 1,))],
        compiler_params=pltpu.CompilerParams(has_side_effects=True),
    )(v)


def _exchange_halves(grads, name):
    n = len(grads)

    def body(*refs):
        g, theirs = refs[:n], refs[n:2 * n]
        send_sem, recv_sem = refs[2 * n:]
        x, y, c = _place()
        copies = []
        for i in range(n):
            half = g[i].shape[1] // 2
            src = g[i].at[:, pl.ds(pl.multiple_of((1 - c) * half, BF16_TILE_ROWS), half)]
            copies.append(_push(src, theirs[i], send_sem.at[i], recv_sem.at[i], (x, y, 1 - c)))
        for cp in copies:
            cp.wait_recv()
            cp.wait_send()

    dma = pltpu.SemaphoreType.DMA
    return pl.pallas_call(
        body, name=name, in_specs=[ANY] * n, out_specs=[ANY] * n,
        out_shape=[jax.ShapeDtypeStruct((a.shape[0], a.shape[1] // 2, a.shape[2]), a.dtype) for a in grads],
        scratch_shapes=[dma((n,)), dma((n,))],
        compiler_params=pltpu.CompilerParams(has_side_effects=True),
    )(*grads)


def _pair_sum(g, theirs, core, name):
    pieces, half, cols = theirs.shape
    tr = min(half, 256)
    per = half // tr

    def body(core_ref, g_ref, t_ref, o_ref):
        del core_ref
        o_ref[...] = (g_ref[...].astype(F32) + t_ref[...].astype(F32)).astype(BF16)

    spec = pl.BlockSpec((1, tr, cols), lambda p, i, core_ref: (p, i, 0))
    return pl.pallas_call(
        body, name=name, out_shape=jax.ShapeDtypeStruct(theirs.shape, BF16),
        grid_spec=pltpu.PrefetchScalarGridSpec(
            num_scalar_prefetch=1, grid=(pieces, per),
            in_specs=[pl.BlockSpec((1, tr, cols), lambda p, i, core_ref: (p, core_ref[0] * per + i, 0)), spec],
            out_specs=spec),
        compiler_params=_params("parallel", "parallel"))(core, g, theirs)


def _scatter_plan(n):
    def plan(refs):
        parts, stacks = refs[:n], refs[n:]
        x, y, c = _place()
        copies = []
        for fx, fy in OTHER_CHIPS:
            chip = 2 * (x ^ fx) + (y ^ fy)
            for part, stack in zip(parts, stacks):
                if part.shape[0] == N_CHIPS:
                    piece = part.at[chip]
                else:
                    width = part.shape[2] // N_CHIPS
                    piece = part.at[0].at[:, pl.ds(pl.multiple_of(chip * width, 128), width)]
                copies.append((piece, stack.at[2 * x + y], (x ^ fx, y ^ fy, c)))
        return copies

    return plan


def _scatter_start(name, parts, after=None):
    def landing(a):
        return (N_CHIPS, a.shape[1], a.shape[2] if a.shape[0] == N_CHIPS else a.shape[2] // N_CHIPS)

    stacks = [lax.empty(landing(a), a.dtype) for a in parts]
    return _start_copies(name, list(parts) + stacks, _scatter_plan(len(parts)), 3 * len(parts), after)


def _scatter_finish(name, started, after):
    send_sem, recv_sem, arrays, _ = started
    n = len(arrays) // 2
    arrays = _wait_copies(name + "_wait", send_sem, recv_sem, arrays, _scatter_plan(n), after)
    return arrays[:n], arrays[n:]


def _sum_chips(part, stack, place, name):
    _, r, c = stack.shape
    tr = 256
    per = r // tr

    def body(place_ref, own_ref, a_ref, b_ref, c_ref, o_ref):
        del place_ref
        total = own_ref[0].astype(F32)
        for ref in (a_ref, b_ref, c_ref):
            total = total + ref[0].astype(F32)
        o_ref[...] = total

    if part.shape[0] == N_CHIPS:
        own = pl.BlockSpec((1, tr, c), lambda i, pr: (pr[1], i, 0))
    else:
        own = pl.BlockSpec((1, tr, c), lambda i, pr: (0, i, pr[1]))
    other = lambda flip: pl.BlockSpec((1, tr, c), lambda i, pr: (pr[1] ^ flip, i, 0))
    return pl.pallas_call(
        body, name=name, out_shape=jax.ShapeDtypeStruct((2 * r, c), F32),
        grid_spec=pltpu.PrefetchScalarGridSpec(
            num_scalar_prefetch=1, grid=(per,), in_specs=[own, other(2), other(1), other(3)],
            out_specs=pl.BlockSpec((tr, c), lambda i, pr: (pr[0] * per + i, 0))),
        compiler_params=_params("parallel"))(place, part, stack, stack, stack)


def _own_half_plan(fulls):
    x, y, c = _place()
    return [(mine, mine, (x, y, 1 - c)) for mine in (_half_rows(full, c, full.shape[0]) for full in fulls)]


MM = dict(tm=2048, tn=1024, tk=2048)
MM_LONG_K = dict(tm=1024, tn=1024, tk=4096)


def _local_step(x, target, ev_norm_w, q_norm_w, k_norm_w, sinks, own_first, weights_first, late_arrived, weights_late, emit):
    s = x.shape[0]
    cos_r, sin_r = _rope_tables(s, RET_DIM)
    cos_a, sin_a = _rope_tables(s, HEAD_DIM)
    cos_a = np.tile(cos_a, (1, 4))
    sins_a = np.tile(np.concatenate([-sin_a, sin_a], axis=1), (1, 2))
    tables = _retention_tables()
    ones_q, ones_kv = _block_diag_mean(GROUP_WIDTH), _block_diag_mean(KV_WIDTH)
    qw_g = jnp.tile(q_norm_w, (1, GROUP_WIDTH // HEAD_DIM))
    kw_kv = jnp.tile(k_norm_w, (1, KV_WIDTH // HEAD_DIM))
    sinks1 = sinks.reshape(Q_HEADS)

    own_w_in0, own_block, start_token = own_first
    h0 = _rmsnorm(x, ev_norm_w, "norm0", after=start_token)
    shifted = dict(shift=own_block * (own_w_in0.shape[1] // MM["tn"]), total=EVEN_IN // MM["tn"], tm=MM["tm"], tn=MM["tn"],
                   out_dtype=BF16)
    own_blocks = own_w_in0.shape[1] // MM["tn"]
    proj0 = _mm_shifted(h0, own_w_in0, b_shifted=False, first=0, count=own_blocks, name="proj0_own", **shifted)
    w_in0, conv_w, od_norm_w, token = weights_first(proj0)
    proj0 = _mm_shifted(h0, w_in0, b_shifted=True, first=own_blocks, count=shifted["total"] - own_blocks, name="proj0_rest",
                        into=proj0, after=token, **shifted)
    o_ret, cat, states = _retention_fwd(proj0, cos_r, sin_r, tables, after=late_arrived(proj0))
    cat = _conv_fwd(proj0, conv_w, cat)
    w_out0, w_in1, w_out1 = weights_late(cat)

    def residual_and_norm(prod, x_ref, w_ref, x1_ref, h1_ref):
        x1v = x_ref[...] + prod
        x1_ref[...] = x1v
        rstd = lax.rsqrt(jnp.mean(x1v * x1v, axis=-1, keepdims=True) + EPS)
        h1_ref[...] = (x1v * rstd * w_ref[...]).astype(BF16)

    def residual_and_loss(prod, x1_ref, t_ref, dyb_ref, sq_ref):
        diff = (x1_ref[...] + prod) - t_ref[...]
        dyb_ref[...] = (diff * (1.0 / D_MODEL)).astype(BF16)

        @pl.when(pl.program_id(0) == 0)
        def _():
            sq_ref[...] = jnp.zeros_like(sq_ref)

        sq_ref[...] += jnp.sum(jnp.sum(diff * diff, axis=1, keepdims=True), axis=0, keepdims=True)

    act = lambda dt: jax.ShapeDtypeStruct((s, D_MODEL), dt)
    x1, h1 = _mm_rows(cat, w_out0, [x], [od_norm_w], [act(F32), act(BF16)], residual_and_norm, tm=min(s, 512), name="out0")
    proj1 = _mm(h1, w_in1, mode="nn", out_dtype=BF16, name="proj1", tm=2048, tn=1536, tk=2048)
    qk = _qk_prep(proj1, qw_g, kw_kv, cos_a, sins_a, ones_q, ones_kv)
    ag, o_att = _swa_fwd(qk, proj1, sinks1)
    dy_b, sq = _mm_rows(ag, w_out1, [x1, target], [], [act(BF16), jax.ShapeDtypeStruct((8, 128), F32)],
                        residual_and_loss, tm=min(s, 512), name="out1")

    g_w_out1 = _mm(ag, dy_b, mode="tn", out_dtype=BF16, name="g_w_out1", **MM_LONG_K)
    dag = _mm(dy_b, w_out1, mode="nt", out_dtype=BF16, name="d_ag", **MM)
    dqr, dkc, dkp, dvc, dvp, dsink = _swa_bwd(qk, proj1, dag, sinks1)
    dproj1, dqw, dkw = _swa_bwd_finish(proj1, dqr, o_att, dag, dkc, dkp, dvc, dvp, qw_g, kw_kv, cos_a, sins_a, ones_q, ones_kv)
    g_w_in1 = _mm(h1, dproj1, mode="tn", out_dtype=BF16, name="g_w_in1", tm=1024, tn=768, tk=4096)
    dh1 = _mm(dproj1, w_in1, mode="nt", out_dtype=BF16, name="d_h1", tm=1024, tn=1024, tk=ODD_IN)
    dx1_b, g_norm1 = _rmsnorm_bwd(x1, od_norm_w, dh1, dy_b, "norm1_bwd", BF16)

    g_w_out0 = _mm(cat, dx1_b, mode="tn", out_dtype=BF16, name="g_w_out0", **MM_LONG_K)
    token = emit("upper", (("od_w_in", g_w_in1, "col"), ("od_w_out", g_w_out1, "row"), ("ev_w_out", g_w_out0, "row")))
    dcat = _mm(dx1_b, w_out0, mode="nt", out_dtype=BF16, name="d_cat", after=token, **MM)
    dproj0 = _retention_bwd(proj0, o_ret, states, dcat, cos_r, sin_r, tables)
    dproj0, g_conv = _conv_bwd(proj0, dcat, conv_w, dproj0)
    g_w_in0 = _mm(h0, dproj0, mode="tn", out_dtype=BF16, name="g_w_in0", **MM_LONG_K)
    token = emit("in0", (("ev_w_in", g_w_in0, "col"),))
    dh0 = _mm(dproj0, w_in0, mode="nt", out_dtype=BF16, name="d_h0", after=token, **MM_LONG_K)
    grad_x, g_norm0 = _rmsnorm_bwd(x, ev_norm_w, dh0, dx1_b, "norm0_bwd", F32)

    g_qw = dqw[:, 0, :].reshape(Q_HEADS, HEAD_DIM).sum(axis=0)
    g_kw = dkw[0].reshape(KV_HEADS, HEAD_DIM).sum(axis=0)
    g_sinks = dsink[:, :, 0].reshape(Q_HEADS)
    small = dict(ev_norm=g_norm0[0], od_norm=g_norm1[0], conv=g_conv[:3], qw=g_qw, kw=g_kw, sinks=g_sinks)
    return sq[0, 0], grad_x, small


LANE_TILE = 128


def _pack_small_grads(small, sq):
    pad = lambda v, n: jnp.pad(v, (0, n - v.shape[0]))
    tail = jnp.concatenate([pad(small["qw"], LANE_TILE), pad(small["kw"], LANE_TILE), small["sinks"]])
    rows = [small["ev_norm"], small["od_norm"]] + [small["conv"][t] for t in range(3)] + [tail, sq.reshape(1)]
    rows += [jnp.zeros((1,), F32)] * (8 - len(rows))
    return jnp.stack([pad(r, D_MODEL) for r in rows])


def _adamw_small(tot, chip, params):
    n = len(params)

    def body(chip_ref, tot_ref, *refs):
        ins, outs = refs[:3 * n], refs[3 * n:]
        c = chip_ref[0]

        def own(rows, width):
            blocks = [tot_ref[rows, k * width:(k + 1) * width] for k in range(N_CHIPS)]
            g = blocks[-1]
            for k in reversed(range(N_CHIPS - 1)):
                g = jnp.where(c == k, blocks[k], g)
            return g

        grads = [tot_ref[0:1, :], own(slice(2, 5), CONV_WIDTH // N_CHIPS), own(slice(1, 2), D_MODEL // N_CHIPS),
                 tot_ref[5:6, 0:HEAD_DIM], tot_ref[5:6, LANE_TILE:LANE_TILE + HEAD_DIM],
                 tot_ref[5:6, 2 * LANE_TILE:2 * LANE_TILE + Q_HEADS]]
        for i, g in enumerate(grads):
            w_ref, m_ref, v_ref = ins[3 * i:3 * i + 3]
            g_out, d_out, nm_out, nv_out = outs[4 * i:4 * i + 4]
            at = (0,) if len(w_ref.shape) == 3 else ()
            delta, nm, nv = _adamw_math(w_ref[at] if at else w_ref[...], g, m_ref[at] if at else m_ref[...],
                                        v_ref[at] if at else v_ref[...])
            for ref, val in ((g_out, g), (d_out, delta), (nm_out, nm), (nv_out, nv)):
                if at:
                    ref[0] = val
                else:
                    ref[...] = val

    vm = pl.BlockSpec(memory_space=pltpu.VMEM)
    flat = [a for p in params for a in p]
    outs = pl.pallas_call(
        body, name="adamw_small", in_specs=[pl.BlockSpec(memory_space=pltpu.SMEM), vm] + [vm] * len(flat),
        out_specs=[vm] * (4 * n), out_shape=[jax.ShapeDtypeStruct(p[0].shape, F32) for p in params for _ in range(4)],
    )(chip, tot, *flat)
    return [tuple(outs[4 * i:4 * i + 4]) for i in range(n)]


class _ReduceScatter:
    def __init__(self, place):
        self.place = place
        self.started = []
        self.sharing = {}

    def send(self, tag, grads):
        pieces = [g[None] if kind == "col" else g.reshape(N_CHIPS, g.shape[0] // N_CHIPS, g.shape[1]) for _, g, kind in grads]
        theirs = _exchange_halves(pieces, "exchange_halves_" + tag)
        parts = [_pair_sum(g, t, self.place[:1], "pair_sum_" + nm) for g, t, (nm, _, _) in zip(pieces, theirs, grads)]
        started = _scatter_start("scatter_" + tag, parts)
        token = self.sum_up(started[3])
        self.started.append((tag, [nm for nm, _, _ in grads], started))
        return started[3] if token is None else token

    def sum_up(self, after):
        token = None
        for tag, group, started in self.started:
            if tag not in self.sharing:
                parts, stacks = _scatter_finish("scatter_" + tag, started, after)
                sums = [_sum_chips(p, s, self.place, "chip_sum_" + nm) for p, s, nm in zip(parts, stacks, group)]
                self.sharing[tag] = (group, _start_copies("share_" + tag, sums, _own_half_plan, len(sums)))
                token = self.sharing[tag][1][3]
        return token

    def result(self, tag, after):
        group, (send_sem, recv_sem, sums, _) = self.sharing[tag]
        return dict(zip(group, _wait_copies("share_" + tag + "_wait", send_sem, recv_sem, sums, _own_half_plan, after)))


def kernel(x, ev_norm_w, ev_w_in, ev_conv_w, ev_w_out, od_norm_w, od_w_in, od_q_norm_w, od_k_norm_w, od_sinks, od_w_out, loss_target, m_ev_norm_w, m_ev_w_in, m_ev_conv_w, m_ev_w_out, m_od_norm_w, m_od_w_in, m_od_q_norm_w, m_od_k_norm_w, m_od_sinks, m_od_w_out, v_ev_norm_w, v_ev_w_in, v_ev_conv_w, v_ev_w_out, v_od_norm_w, v_od_w_in, v_od_q_norm_w, v_od_k_norm_w, v_od_sinks, v_od_w_out):
    my_chip = 2 * lax.axis_index("x") + lax.axis_index("y")
    place = jnp.stack([lax.axis_index("c"), my_chip]).astype(jnp.int32)
    shard_w = D_MODEL // N_CHIPS
    conv_shard = CONV_WIDTH // N_CHIPS

    small_in = jnp.zeros((8, shard_w), F32)
    small_in = small_in.at[0].set(od_norm_w[0]).at[1:4, :conv_shard].set(ev_conv_w[0])
    small_in = lax.dynamic_update_slice(jnp.zeros((N_CHIPS, 8, shard_w), F32), small_in[None], (my_chip, 0, 0))
    chip = place[1:]
    first_kinds, late_kinds = ("col", "whole"), ("row", "col", "row")
    w_in0_own_place, w_in0_shard = _cast_into_gathered(ev_w_in[0], "col", chip, "cast_w_in0", keep_shard=True)
    first = _gather_start("gather_first", [w_in0_own_place, small_in], first_kinds)
    late_own = [_cast_into_gathered(ev_w_out[0], "row", chip, "cast_w_out0"),
                _cast_into_gathered(od_w_in[0], "col", chip, "cast_w_in1"),
                _cast_into_gathered(od_w_out[0], "row", chip, "cast_w_out1")]
    late = []

    def weights_first(after):
        passed_on = _gather_arrive("gather_first", first, first_kinds, [after] + late_own)
        w_in0, small_all = _gather_done("gather_first", passed_on, first_kinds, passed_on[3])
        late.append(_gather_start("gather_late", late_own, late_kinds, after=w_in0))
        od_norm_full = small_all[:, 0, :].reshape(1, D_MODEL)
        conv_full = jnp.transpose(small_all[:, 1:4, :conv_shard], (1, 0, 2)).reshape(3, CONV_WIDTH)
        return w_in0, conv_full, od_norm_full, late[0][3]

    def late_arrived(after):
        late.append(_gather_arrive("gather_late", late[0], late_kinds, after))
        return late[1][3]

    def weights_late(after):
        return _gather_done("gather_late", late[1], late_kinds, after)

    reduce_scatter = _ReduceScatter(place)
    sq, grad_x, small = _local_step(x[0], loss_target[0], ev_norm_w, od_q_norm_w, od_k_norm_w, od_sinks,
                                    (w_in0_shard, chip, first[3]), weights_first, late_arrived, weights_late,
                                    reduce_scatter.send)

    token = reduce_scatter.sum_up(grad_x)
    upd = {}
    upper = reduce_scatter.result("upper", token)
    upd["od_w_in"] = _adamw(od_w_in[0], upper["od_w_in"], m_od_w_in[0], v_od_w_in[0], "adamw_od_w_in")
    upd["od_w_out"] = _adamw(od_w_out[0], upper["od_w_out"], m_od_w_out[0], v_od_w_out[0], "adamw_od_w_out")
    upd["ev_w_out"] = _adamw(ev_w_out[0], upper["ev_w_out"], m_ev_w_out[0], v_ev_w_out[0], "adamw_ev_w_out")
    g_ev_w_in = reduce_scatter.result("in0", [upd[nm][1] for nm in ("od_w_in", "od_w_out", "ev_w_out")])["ev_w_in"]
    upd["ev_w_in"] = _adamw(ev_w_in[0], g_ev_w_in, m_ev_w_in[0], v_ev_w_in[0], "adamw_ev_w_in")
    tot = _allreduce_small(_pack_small_grads(small, sq))
    loss = 0.5 * tot[6, 0] / D_MODEL

    smalls = (("ev_norm_w", ev_norm_w, m_ev_norm_w, v_ev_norm_w), ("ev_conv_w", ev_conv_w, m_ev_conv_w, v_ev_conv_w),
              ("od_norm_w", od_norm_w, m_od_norm_w, v_od_norm_w), ("od_q_norm_w", od_q_norm_w, m_od_q_norm_w, v_od_q_norm_w),
              ("od_k_norm_w", od_k_norm_w, m_od_k_norm_w, v_od_k_norm_w), ("od_sinks", od_sinks, m_od_sinks, v_od_sinks))
    for (nm, _, _, _), result in zip(smalls, _adamw_small(tot, chip, [p[1:] for p in smalls])):
        upd[nm] = result
    for nm in ("ev_w_in", "ev_w_out", "od_w_in", "od_w_out"):
        upd[nm] = tuple(u[None] for u in upd[nm])
    order = ("ev_norm_w", "ev_w_in", "ev_conv_w", "ev_w_out", "od_norm_w", "od_w_in", "od_q_norm_w", "od_k_norm_w", "od_sinks", "od_w_out")
    return (loss, grad_x[None], *[upd[nm][0] for nm in order], *[upd[nm][1] for nm in order],
            *[upd[nm][2] for nm in order], *[upd[nm][3] for nm in order])
```

```python
import math

import jax
import jax.numpy as jnp
import numpy as np
from jax import lax
from jax.experimental import pallas as pl
from jax.experimental.pallas import tpu as pltpu

F32 = jnp.float32
BF16 = jnp.bfloat16

D_MODEL = 2048
RET_HEADS = 4
RET_DIM = 256
RET_WIDTH = 1024
CONV_WIDTH = 1024
EVEN_IN = 8192
Q_HEADS = 32
HEAD_DIM = 64
KV_HEADS = 4
KV_WIDTH = 256
ATTN_WIDTH = 2048
ODD_IN = 4608
BLK = 128
ROPE_THETA = 10000.0
EPS = 1e-6
ADAM_LR = 0.001
ADAM_B1 = 0.9
ADAM_B2 = 0.999
ADAM_EPS = 1e-08
ADAM_WD = 0.01
ADAM_STEP = 10
N_CHIPS = 4
N_DEV = 8
VMEM_LIMIT_BYTES = 56 * 1024 * 1024
MESH = pl.DeviceIdType.MESH
ANY = pl.BlockSpec(memory_space=pl.ANY)


def _params(*sem):
    return pltpu.CompilerParams(dimension_semantics=sem, vmem_limit_bytes=VMEM_LIMIT_BYTES)


def _dot(a, b):
    return jnp.dot(a, b, preferred_element_type=F32)


def _dot_nt(a, b):
    return lax.dot_general(a, b, (((1,), (1,)), ((), ())), preferred_element_type=F32)


def _dot_tn(a, b):
    return lax.dot_general(a, b, (((0,), (0,)), ((), ())), preferred_element_type=F32)


def _sigmoid(x):
    return 1.0 / (1.0 + jnp.exp(-x))


def _mm(a, b, *, mode, tm, tn, tk, out_dtype, name, add=None, after=None):
    if mode == "nn":
        (m, k), n = a.shape, b.shape[1]
    elif mode == "nt":
        (m, k), n = a.shape, b.shape[0]
    else:
        (k, m), n = a.shape, b.shape[1]
    tm, tn, tk = min(tm, m), min(tn, n), min(tk, k)
    assert m % tm == 0 and n % tn == 0 and k % tk == 0, (name, m, n, k)
    nk = k // tk
    dot = {"nn": _dot, "nt": _dot_nt, "tn": _dot_tn}[mode]
    a_spec = (pl.BlockSpec((tk, tm), lambda i, j, kk: (kk, i)) if mode == "tn"
              else pl.BlockSpec((tm, tk), lambda i, j, kk: (i, kk)))
    b_spec = (pl.BlockSpec((tn, tk), lambda i, j, kk: (j, kk)) if mode == "nt"
              else pl.BlockSpec((tk, tn), lambda i, j, kk: (kk, j)))
    o_spec = pl.BlockSpec((tm, tn), lambda i, j, kk: (i, j))
    has_add = add is not None

    def body(*refs):
        a_ref, b_ref = refs[0], refs[1]
        add_ref = refs[2] if has_add else None
        o_ref, acc_ref = refs[-2], refs[-1]
        p = dot(a_ref[...], b_ref[...])

        def finish(total):
            if has_add:
                total = total + add_ref[...].astype(F32)
            o_ref[...] = total.astype(out_dtype)

        if nk == 1:
            finish(p)
        else:
            kk = pl.program_id(2)

            @pl.when(kk == 0)
            def _():
                acc_ref[...] = p

            @pl.when(jnp.logical_and(kk > 0, kk < nk - 1))
            def _():
                acc_ref[...] += p

            @pl.when(kk == nk - 1)
            def _():
                finish(acc_ref[...] + p)

    in_specs = [a_spec, b_spec] + ([o_spec] if has_add else []) + ([ANY] if after is not None else [])
    args = (a, b) + ((add,) if has_add else ()) + ((after,) if after is not None else ())
    return pl.pallas_call(
        body, name=name, grid=(m // tm, n // tn, nk), in_specs=in_specs, out_specs=o_spec,
        out_shape=jax.ShapeDtypeStruct((m, n), out_dtype),
        scratch_shapes=[pltpu.VMEM((tm, tn) if nk > 1 else (8, 128), F32)],
        compiler_params=_params("parallel", "parallel", "arbitrary"),
    )(*args)


def _mm_shifted(a, b, shift, *, b_shifted, first, count, total, tm, tn, out_dtype, name, stride=1, into=None, after=None):
    m, k = a.shape
    tm = min(tm, m)
    assert m % tm == 0
    col = lambda j, shift_ref: (shift_ref[0] + first + stride * j) % total
    extra = [arr for arr in (into, after) if arr is not None]

    def body(shift_ref, a_ref, b_ref, *rest):
        del shift_ref
        rest[-1][...] = _dot(a_ref[...], b_ref[...]).astype(out_dtype)

    return pl.pallas_call(
        body, name=name, out_shape=jax.ShapeDtypeStruct((m, total * tn), out_dtype),
        grid_spec=pltpu.PrefetchScalarGridSpec(
            num_scalar_prefetch=1, grid=(m // tm, count),
            in_specs=[pl.BlockSpec((tm, k), lambda i, j, s: (i, 0)),
                      pl.BlockSpec((k, tn), (lambda i, j, s: (0, col(j, s))) if b_shifted else (lambda i, j, s: (0, j)))]
            + [ANY] * len(extra),
            out_specs=pl.BlockSpec((tm, tn), lambda i, j, s: (i, col(j, s)))),
        input_output_aliases={3: 0} if into is not None else {},
        compiler_params=_params("parallel", "arbitrary"))(shift, a, b, *extra)


def _mm_rows(a, b, rows_in, vecs_in, out_shapes, epilogue, *, tm, name):
    m, k = a.shape
    n = b.shape[1]
    assert m % tm == 0
    row = pl.BlockSpec((tm, n), lambda i: (i, 0))

    def body(a_ref, b_ref, *rest):
        epilogue(_dot(a_ref[...], b_ref[...]), *rest)

    out_specs = [row if tuple(s.shape) == (m, n) else pl.BlockSpec(s.shape, lambda i: (0, 0)) for s in out_shapes]
    return pl.pallas_call(
        body, name=name, grid=(m // tm,),
        in_specs=[pl.BlockSpec((tm, k), lambda i: (i, 0)), pl.BlockSpec((k, n), lambda i: (0, 0))] + [row] * len(rows_in)
        + [pl.BlockSpec((1, n), lambda i: (0, 0))] * len(vecs_in),
        out_specs=out_specs, out_shape=out_shapes, compiler_params=_params("arbitrary"),
    )(a, b, *rows_in, *vecs_in)


def _cast_into_gathered(w, kind, chip, name, keep_shard=False):
    r, c = w.shape
    tr = min(r, 512)
    per = r // tr

    def body(chip_ref, w_ref, *outs):
        del chip_ref
        for o_ref in outs:
            o_ref[...] = w_ref[...].astype(BF16)

    if kind == "col":
        shape, out_map = (r, N_CHIPS * c), (lambda i, chip_ref: (i, chip_ref[0]))
    else:
        shape, out_map = (N_CHIPS * r, c), (lambda i, chip_ref: (chip_ref[0] * per + i, 0))
    plain = pl.BlockSpec((tr, c), lambda i, chip_ref: (i, 0))
    out = pl.pallas_call(
        body, name=name,
        out_shape=[jax.ShapeDtypeStruct(shape, BF16)] + ([jax.ShapeDtypeStruct((r, c), BF16)] if keep_shard else []),
        grid_spec=pltpu.PrefetchScalarGridSpec(
            num_scalar_prefetch=1, grid=(per,), in_specs=[plain],
            out_specs=[pl.BlockSpec((tr, c), out_map)] + ([plain] if keep_shard else [])),
        compiler_params=_params("parallel"))(chip, w)
    return out if keep_shard else out[0]


NORM_ROWS = 512


def _rmsnorm(x, w, name, after=None):
    s, d = x.shape
    tr = NORM_ROWS

    def body(x_ref, w_ref, *rest):
        xv = x_ref[...]
        rstd = lax.rsqrt(jnp.mean(xv * xv, axis=-1, keepdims=True) + EPS)
        rest[-1][...] = (xv * rstd * w_ref[...]).astype(BF16)

    return pl.pallas_call(
        body, name=name, grid=(s // tr,),
        in_specs=[pl.BlockSpec((tr, d), lambda i: (i, 0)), pl.BlockSpec((1, d), lambda i: (0, 0))]
        + ([ANY] if after is not None else []),
        out_specs=pl.BlockSpec((tr, d), lambda i: (i, 0)),
        out_shape=jax.ShapeDtypeStruct((s, d), BF16), compiler_params=_params("parallel"),
    )(x, w, *((after,) if after is not None else ()))


def _rmsnorm_bwd(x, w, dh, dres, name, out_dtype):
    s, d = x.shape
    tr = NORM_ROWS

    def body(x_ref, w_ref, dh_ref, dres_ref, dx_ref, dw_ref):
        xv = x_ref[...]
        rstd = lax.rsqrt(jnp.mean(xv * xv, axis=-1, keepdims=True) + EPS)
        nrm = xv * rstd
        dhv = dh_ref[...].astype(F32)
        dn = dhv * w_ref[...]
        dx = dres_ref[...].astype(F32) + rstd * (dn - nrm * jnp.mean(dn * nrm, axis=-1, keepdims=True))
        dx_ref[...] = dx.astype(out_dtype)

        @pl.when(pl.program_id(0) == 0)
        def _():
            dw_ref[...] = jnp.zeros_like(dw_ref)

        dw_ref[...] += jnp.sum(dhv * nrm, axis=0, keepdims=True)

    row = pl.BlockSpec((tr, d), lambda i: (i, 0))
    return pl.pallas_call(
        body, name=name, grid=(s // tr,),
        in_specs=[row, pl.BlockSpec((1, d), lambda i: (0, 0)), row, row],
        out_specs=[row, pl.BlockSpec((8, d), lambda i: (0, 0))],
        out_shape=[jax.ShapeDtypeStruct((s, d), out_dtype), jax.ShapeDtypeStruct((8, d), F32)],
        compiler_params=_params("arbitrary"))(x, w, dh, dres)


def _adamw_math(w, g, m, v):
    nm = ADAM_B1 * m + (1.0 - ADAM_B1) * g
    nv = ADAM_B2 * v + (1.0 - ADAM_B2) * (g * g)
    m_hat = nm / (1.0 - ADAM_B1 ** ADAM_STEP)
    v_hat = nv / (1.0 - ADAM_B2 ** ADAM_STEP)
    return -ADAM_LR * (m_hat / (jnp.sqrt(v_hat) + ADAM_EPS) + ADAM_WD * w), nm, nv


def _adamw(w, g, m, v, name):
    r, c = w.shape
    tr = min(r, 256)
    assert r % tr == 0

    def body(w_ref, g_ref, m_ref, v_ref, g_out, d_ref, nm_ref, nv_ref):
        gv = g_ref[...]
        g_out[...] = gv
        d_ref[...], nm_ref[...], nv_ref[...] = _adamw_math(w_ref[...], gv, m_ref[...], v_ref[...])

    spec = pl.BlockSpec((tr, c), lambda i: (i, 0))
    shp = jax.ShapeDtypeStruct((r, c), F32)
    return pl.pallas_call(body, name=name, grid=(r // tr,), in_specs=[spec] * 4, out_specs=[spec] * 4,
                          out_shape=[shp] * 4, compiler_params=_params("parallel"))(w, g, m, v)


def _rope_tables(s, dim):
    inv = (1.0 / (ROPE_THETA ** (np.arange(0, dim, 2, dtype=np.float64) / dim))).astype(np.float32)
    ang = (np.arange(s, dtype=np.float32)[:, None] * inv[None, :]).astype(np.float64)
    return np.cos(ang).astype(np.float32), np.sin(ang).astype(np.float32)


def _rope_half(x, cos, sin):
    h = x.shape[1] // 2
    x1, x2 = x[:, :h], x[:, h:]
    return jnp.concatenate([x1 * cos - x2 * sin, x2 * cos + x1 * sin], axis=1)


def _unrope_half(dy, cos, sin):
    h = dy.shape[1] // 2
    d1, d2 = dy[:, :h], dy[:, h:]
    return jnp.concatenate([d1 * cos + d2 * sin, d2 * cos - d1 * sin], axis=1)


def _lane(shape):
    return lax.broadcasted_iota(jnp.int32, shape, 1)


def _partner64(x):
    w = x.shape[1]
    first = (_lane(x.shape) % HEAD_DIM) < (HEAD_DIM // 2)
    return jnp.where(first, pltpu.roll(x, w - HEAD_DIM // 2, 1), pltpu.roll(x, HEAD_DIM // 2, 1))


def _tile_lanes(t, reps):
    return t if reps == 1 else jnp.concatenate([t] * reps, axis=1)


def _group_mean(x, ones_bd, passes=2):
    hi = x.astype(BF16)
    if passes == 1:
        return _dot(hi, ones_bd)
    lo = (x - hi.astype(F32)).astype(BF16)
    return _dot(hi, ones_bd) + _dot(lo, ones_bd)


def _block_diag_mean(width):
    idx = jnp.arange(width) // HEAD_DIM
    return jnp.where(idx[:, None] == idx[None, :], 1.0 / HEAD_DIM, 0.0).astype(BF16)


RET_STEP = 4


def _retention_tables():
    h = RET_HEADS
    log_g = jnp.log(1.0 - 2.0 ** (-5.0 - jnp.arange(h, dtype=F32)))
    idx = jnp.arange(BLK, dtype=F32)
    diff = idx[:, None] - idx[None, :]
    intra = jnp.where(diff >= 0, jnp.exp(log_g[:, None, None] * jnp.maximum(diff, 0.0)), 0.0).astype(F32)
    q_dec = jnp.exp(log_g[:, None] * (idx[None, :] + 1.0)).astype(F32)[:, :, None]
    k_dec = jnp.exp(log_g[:, None] * (BLK - 1.0 - idx[None, :])).astype(F32)[:, :, None]
    chunk_dec = jnp.exp(log_g * BLK).astype(F32)[:, None, None]
    return intra, q_dec, k_dec, chunk_dec


def _retention_fwd(proj, cos, sin, tables, after=None):
    s = proj.shape[0]
    nc = s // BLK
    intra, q_dec, k_dec, chunk_dec = tables

    def body(p_ref, cos_ref, sin_ref, in_ref, qd_ref, kd_ref, cd_ref, *rest):
        o_ref, cat_ref, st_ref, state = rest[-4:]

        @pl.when(pl.program_id(0) == 0)
        def _():
            state[...] = jnp.zeros_like(state)

        for c in range(RET_STEP):
            rows = slice(c * BLK, (c + 1) * BLK)
            cosv, sinv = cos_ref[rows, :], sin_ref[rows, :]
            for h in range(RET_HEADS):
                c0 = h * RET_DIM
                q = p_ref[rows, c0:c0 + RET_DIM].astype(F32)
                k = p_ref[rows, RET_WIDTH + c0:RET_WIDTH + c0 + RET_DIM].astype(F32)
                v = p_ref[rows, 2 * RET_WIDTH + c0:2 * RET_WIDTH + c0 + RET_DIM]
                g = p_ref[rows, 3 * RET_WIDTH + c0:3 * RET_WIDTH + c0 + RET_DIM].astype(F32)
                qb = _rope_half(q, cosv, sinv).astype(BF16)
                kr = _rope_half(k, cosv, sinv) * (RET_DIM ** -0.5)
                kb = kr.astype(BF16)
                scores = _dot_nt(qb, kb) * in_ref[h]
                inner = _dot(scores.astype(BF16), v)
                prev = state[h]
                prev_b = prev.astype(BF16)
                st_ref[h, c] = prev_b
                o = inner + _dot(qb, prev_b) * qd_ref[h]
                o_ref[rows, c0:c0 + RET_DIM] = o
                rstd = lax.rsqrt(jnp.mean(o * o, axis=-1, keepdims=True) + EPS)
                cat_ref[rows, c0:c0 + RET_DIM] = (o * rstd * (g * _sigmoid(g))).astype(BF16)
                state[h] = cd_ref[h] * prev + _dot_tn((kr * kd_ref[h]).astype(BF16), v)

    full = lambda shape: pl.BlockSpec(shape, lambda n: (0,) * len(shape))
    step = RET_STEP * BLK
    return pl.pallas_call(
        body, name="retention_fwd", grid=(nc // RET_STEP,),
        in_specs=[pl.BlockSpec((step, 4 * RET_WIDTH), lambda n: (n, 0)),
                  pl.BlockSpec((step, RET_DIM // 2), lambda n: (n, 0)), pl.BlockSpec((step, RET_DIM // 2), lambda n: (n, 0)),
                  full((RET_HEADS, BLK, BLK)), full((RET_HEADS, BLK, 1)), full((RET_HEADS, BLK, 1)), full((RET_HEADS, 1, 1))]
        + ([ANY] if after is not None else []),
        out_specs=[pl.BlockSpec((step, RET_WIDTH), lambda n: (n, 0)), pl.BlockSpec((step, RET_WIDTH), lambda n: (n, 0)),
                   pl.BlockSpec((RET_HEADS, RET_STEP, RET_DIM, RET_DIM), lambda n: (0, n, 0, 0))],
        out_shape=[jax.ShapeDtypeStruct((s, RET_WIDTH), F32), jax.ShapeDtypeStruct((s, D_MODEL), BF16),
                   jax.ShapeDtypeStruct((RET_HEADS, nc, RET_DIM, RET_DIM), BF16)],
        scratch_shapes=[pltpu.VMEM((RET_HEADS, RET_DIM, RET_DIM), F32)],
        compiler_params=_params("arbitrary"),
    )(proj, cos, sin, intra, q_dec, k_dec, chunk_dec, *((after,) if after is not None else ()))


def _retention_bwd(proj, o, states, dcat, cos, sin, tables):
    s = proj.shape[0]
    nc = s // BLK
    intra, q_dec, k_dec, chunk_dec = tables

    def body(p_ref, o_ref, st_ref, dc_ref, cos_ref, sin_ref, in_ref, qd_ref, kd_ref, cd_ref, dp_ref, dstate):
        @pl.when(pl.program_id(0) == 0)
        def _():
            dstate[...] = jnp.zeros_like(dstate)

        for c in reversed(range(RET_STEP)):
            rows = slice(c * BLK, (c + 1) * BLK)
            cosv, sinv = cos_ref[rows, :], sin_ref[rows, :]
            for h in range(RET_HEADS):
                c0 = h * RET_DIM
                q = p_ref[rows, c0:c0 + RET_DIM].astype(F32)
                k = p_ref[rows, RET_WIDTH + c0:RET_WIDTH + c0 + RET_DIM].astype(F32)
                v = p_ref[rows, 2 * RET_WIDTH + c0:2 * RET_WIDTH + c0 + RET_DIM]
                g = p_ref[rows, 3 * RET_WIDTH + c0:3 * RET_WIDTH + c0 + RET_DIM].astype(F32)
                o = o_ref[rows, c0:c0 + RET_DIM]
                dc = dc_ref[rows, c0:c0 + RET_DIM].astype(F32)
                rstd = lax.rsqrt(jnp.mean(o * o, axis=-1, keepdims=True) + EPS)
                nrm = o * rstd
                sg = _sigmoid(g)
                dg = dc * nrm * (sg * (1.0 + g * (1.0 - sg)))
                dn = dc * (g * sg)
                do = rstd * (dn - nrm * jnp.mean(dn * nrm, axis=-1, keepdims=True))
                qb = _rope_half(q, cosv, sinv).astype(BF16)
                kr = _rope_half(k, cosv, sinv) * (RET_DIM ** -0.5)
                kb = kr.astype(BF16)
                mask = in_ref[h]
                qd, kd = qd_ref[h], kd_ref[h]
                prev_b = st_ref[h, c]
                dnext = dstate[h]
                dnext_b = dnext.astype(BF16)
                att = (_dot_nt(qb, kb) * mask).astype(BF16)
                do_b = do.astype(BF16)
                doq = (do * qd).astype(BF16)
                dv = _dot_tn(att, do_b) + _dot((kr * kd).astype(BF16), dnext_b)
                ds = (_dot_nt(do_b, v) * mask).astype(BF16)
                dqr = _dot(ds, kb) + _dot_nt(doq, prev_b)
                dkr = _dot_tn(ds, qb) + _dot_nt(v, dnext_b) * kd
                dstate[h] = cd_ref[h] * dnext + _dot_tn(qb, doq)
                dq = _unrope_half(dqr, cosv, sinv)
                dk = _unrope_half(dkr * (RET_DIM ** -0.5), cosv, sinv)
                dp_ref[rows, c0:c0 + RET_DIM] = dq.astype(BF16)
                dp_ref[rows, RET_WIDTH + c0:RET_WIDTH + c0 + RET_DIM] = dk.astype(BF16)
                dp_ref[rows, 2 * RET_WIDTH + c0:2 * RET_WIDTH + c0 + RET_DIM] = dv.astype(BF16)
                dp_ref[rows, 3 * RET_WIDTH + c0:3 * RET_WIDTH + c0 + RET_DIM] = dg.astype(BF16)

    steps = nc // RET_STEP
    rev = lambda n: steps - 1 - n
    full = lambda shape: pl.BlockSpec(shape, lambda n: (0,) * len(shape))
    step = RET_STEP * BLK
    return pl.pallas_call(
        body, name="retention_bwd", grid=(steps,),
        in_specs=[pl.BlockSpec((step, 4 * RET_WIDTH), lambda n: (rev(n), 0)),
                  pl.BlockSpec((step, RET_WIDTH), lambda n: (rev(n), 0)),
                  pl.BlockSpec((RET_HEADS, RET_STEP, RET_DIM, RET_DIM), lambda n: (0, rev(n), 0, 0)),
                  pl.BlockSpec((step, RET_WIDTH), lambda n: (rev(n), 0)),
                  pl.BlockSpec((step, RET_DIM // 2), lambda n: (rev(n), 0)), pl.BlockSpec((step, RET_DIM // 2), lambda n: (rev(n), 0)),
                  full((RET_HEADS, BLK, BLK)), full((RET_HEADS, BLK, 1)), full((RET_HEADS, BLK, 1)), full((RET_HEADS, 1, 1))],
        out_specs=pl.BlockSpec((step, 4 * RET_WIDTH), lambda n: (rev(n), 0)),
        out_shape=jax.ShapeDtypeStruct((s, EVEN_IN), BF16),
        scratch_shapes=[pltpu.VMEM((RET_HEADS, RET_DIM, RET_DIM), F32)],
        compiler_params=_params("arbitrary"))(proj, o, states, dcat, cos, sin, intra, q_dec, k_dec, chunk_dec)


CONV_ROWS = 256
HALO = 16


def _conv_pieces(p, halo, first):
    gb, gc, u, gv = (p[:, i * CONV_WIDTH:(i + 1) * CONV_WIDTH] for i in range(4))
    cu = gc * u
    hcu = halo[:, CONV_WIDTH:2 * CONV_WIDTH] * halo[:, 2 * CONV_WIDTH:3 * CONV_WIDTH]
    hcu = jnp.where(first, 0.0, hcu)
    r1, r2 = hcu[HALO - 1:HALO], hcu[HALO - 2:HALO - 1]
    row = lax.broadcasted_iota(jnp.int32, cu.shape, 0)
    m1 = jnp.where(row == 0, r1, pltpu.roll(cu, 1, 0))
    m2 = jnp.where(row == 0, r2, jnp.where(row == 1, r1, pltpu.roll(cu, 2, 0)))
    return gb, gc, u, gv, cu, m1, m2


def _conv_fwd(proj, conv_w, cat):
    s = proj.shape[0]
    per = CONV_ROWS // HALO

    def body(p_ref, halo_ref, w_ref, cat_in, cat_ref):
        del cat_in
        first = pl.program_id(0) == 0
        gb, _, _, gv, cu, m1, m2 = _conv_pieces(p_ref[...].astype(F32), halo_ref[...].astype(F32), first)
        conv = w_ref[0:1, :] * m2 + w_ref[1:2, :] * m1 + w_ref[2:3, :] * cu
        cat_ref[...] = (gb * conv * (gv * _sigmoid(gv))).astype(BF16)

    return pl.pallas_call(
        body, name="conv_fwd", grid=(s // CONV_ROWS,),
        in_specs=[pl.BlockSpec((CONV_ROWS, 4 * CONV_WIDTH), lambda i: (i, 1)),
                  pl.BlockSpec((HALO, 4 * CONV_WIDTH), lambda i: (jnp.maximum(i * per - 1, 0), 1)),
                  pl.BlockSpec((3, CONV_WIDTH), lambda i: (0, 0)), ANY],
        out_specs=pl.BlockSpec((CONV_ROWS, CONV_WIDTH), lambda i: (i, 1)),
        out_shape=jax.ShapeDtypeStruct(cat.shape, cat.dtype), input_output_aliases={3: 0},
        compiler_params=_params("parallel"))(proj, proj, conv_w, cat)


def _conv_bwd(proj, dcat, conv_w, dproj):
    s = proj.shape[0]
    per = CONV_ROWS // HALO
    last_halo = s // HALO - 1
    nsteps = s // CONV_ROWS

    def body(p_ref, halo_ref, nxt_ref, dc_ref, dnxt_ref, w_ref, dp_in, dp_ref, dw_ref):
        del dp_in
        i = pl.program_id(0)
        gb, gc, u, gv, cu, m1, m2 = _conv_pieces(p_ref[...].astype(F32), halo_ref[...].astype(F32), i == 0)
        w0, w1, w2 = w_ref[0:1, :], w_ref[1:2, :], w_ref[2:3, :]
        conv = w0 * m2 + w1 * m1 + w2 * cu
        dco = dc_ref[...].astype(F32)
        sg = _sigmoid(gv)
        silu = gv * sg
        dgb = dco * conv * silu
        dco_gb = dco * gb
        dgv = dco_gb * conv * (sg * (1.0 + gv * (1.0 - sg)))
        dconv = dco_gb * silu
        nxt = nxt_ref[...].astype(F32)
        ngv = nxt[:, 3 * CONV_WIDTH:]
        dnext = dnxt_ref[...].astype(F32) * nxt[:, :CONV_WIDTH] * (ngv * _sigmoid(ngv))
        dnext = jnp.where(i == nsteps - 1, 0.0, dnext)
        n1, n2 = dnext[0:1], dnext[1:2]
        row = lax.broadcasted_iota(jnp.int32, dconv.shape, 0)
        p1 = jnp.where(row == CONV_ROWS - 1, n1, pltpu.roll(dconv, CONV_ROWS - 1, 0))
        p2 = jnp.where(row == CONV_ROWS - 1, n2, jnp.where(row == CONV_ROWS - 2, n1, pltpu.roll(dconv, CONV_ROWS - 2, 0)))
        dcu = w2 * dconv + w1 * p1 + w0 * p2
        dp_ref[...] = jnp.concatenate([dgb, dcu * u, dcu * gc, dgv], axis=1).astype(BF16)

        @pl.when(i == 0)
        def _():
            dw_ref[...] = jnp.zeros_like(dw_ref)

        taps = [jnp.sum(dconv * m, axis=0, keepdims=True) for m in (m2, m1, cu)]
        r8 = lax.broadcasted_iota(jnp.int32, dw_ref.shape, 0)
        dw_ref[...] += jnp.where(r8 == 0, taps[0], jnp.where(r8 == 1, taps[1], jnp.where(r8 == 2, taps[2], 0.0)))

    return pl.pallas_call(
        body, name="conv_bwd", grid=(nsteps,),
        in_specs=[pl.BlockSpec((CONV_ROWS, 4 * CONV_WIDTH), lambda i: (i, 1)),
                  pl.BlockSpec((HALO, 4 * CONV_WIDTH), lambda i: (jnp.maximum(i * per - 1, 0), 1)),
                  pl.BlockSpec((HALO, 4 * CONV_WIDTH), lambda i: (jnp.minimum((i + 1) * per, last_halo), 1)),
                  pl.BlockSpec((CONV_ROWS, CONV_WIDTH), lambda i: (i, 1)),
                  pl.BlockSpec((HALO, CONV_WIDTH), lambda i: (jnp.minimum((i + 1) * per, last_halo), 1)),
                  pl.BlockSpec((3, CONV_WIDTH), lambda i: (0, 0)), ANY],
        out_specs=[pl.BlockSpec((CONV_ROWS, 4 * CONV_WIDTH), lambda i: (i, 1)), pl.BlockSpec((8, CONV_WIDTH), lambda i: (0, 0))],
        out_shape=[jax.ShapeDtypeStruct(dproj.shape, dproj.dtype), jax.ShapeDtypeStruct((8, CONV_WIDTH), F32)],
        input_output_aliases={6: 0},
        compiler_params=_params("arbitrary"))(proj, proj, proj, dcat, dcat, conv_w, dproj)


GROUP_HEADS = Q_HEADS // KV_HEADS
GROUP_WIDTH = GROUP_HEADS * HEAD_DIM
SLAB = 2 * HEAD_DIM
KV_COL = ATTN_WIDTH // GROUP_WIDTH
GATE_COL = KV_COL + 1
ATTN_SCALE = HEAD_DIM ** -0.5
KV_ROWS = 1024


def _half_mask(shape, which):
    return (_lane(shape) // HEAD_DIM) == which


def _dup_head(slab, which):
    kept = jnp.where(_half_mask(slab.shape, which), slab, 0.0)
    return kept + pltpu.roll(kept, HEAD_DIM, 1)


def _stack_heads(x):
    parts = []
    for sl in range(GROUP_WIDTH // SLAB):
        slab = x[:, sl * SLAB:(sl + 1) * SLAB]
        parts += [jnp.where(_half_mask(slab.shape, e), slab, 0.0) for e in range(2)]
    return jnp.concatenate(parts, axis=0)


def _unstack_heads(y):
    slabs = []
    for sl in range(GROUP_WIDTH // SLAB):
        a, b = y[(2 * sl) * BLK:(2 * sl + 1) * BLK], y[(2 * sl + 1) * BLK:(2 * sl + 2) * BLK]
        slabs.append(jnp.where(_half_mask(a.shape, 0), a, b))
    return jnp.concatenate(slabs, axis=1)


def _q_prep(q, qw, cosf, sins, ones_bd):
    rstd = lax.rsqrt(_group_mean(q * q, ones_bd) + EPS)
    nrm = q * rstd
    y = nrm * qw
    return nrm, rstd, y * cosf + _partner64(y) * sins


def _band(tri_ref, n):
    own = tri_ref[...] > 0.5
    return own, jnp.where(jnp.logical_and(n == 0, jnp.logical_not(own)), -1e30, 0.0)


def _fold(pair, own):
    return jnp.where(own, pair[:, BLK:], pair[:, :BLK])


def _unfold(folded, own):
    return jnp.concatenate([jnp.where(own, 0.0, folded), jnp.where(own, folded, 0.0)], axis=1)


def _head_probs(raw_scores, sink, own, bias):
    sc = _fold(raw_scores, own) + bias
    m = jnp.maximum(jnp.max(sc, axis=-1, keepdims=True), sink)
    p = jnp.exp(sc - m)
    psink = jnp.exp(sink - m)
    inv = 1.0 / (jnp.sum(p, axis=-1, keepdims=True) + psink)
    return p * inv, psink * inv


def _k_prep(k, kw, cosf, sins, ones_bd):
    rstd = lax.rsqrt(_group_mean(k * k, ones_bd) + EPS)
    nrm = k * rstd
    y = nrm * kw
    return nrm, rstd, y * cosf + _partner64(y) * sins


def _qk_prep(proj, qw, kw, cos, sins, ones_q, ones_kv):
    s = proj.shape[0]
    rows = min(KV_ROWS, s)

    def body(p_ref, qw_ref, kw_ref, cos_ref, sin_ref, oq_ref, ok_ref, o_ref):
        j = pl.program_id(1)

        @pl.when(j < KV_COL)
        def _():
            cosf, sinf = _tile_lanes(cos_ref[...], 4), _tile_lanes(sin_ref[...], 4)
            roped = _q_prep(p_ref[...].astype(F32), qw_ref[...], cosf, sinf, oq_ref[...])[2]
            o_ref[...] = (roped * ATTN_SCALE).astype(BF16)

        @pl.when(j == KV_COL)
        def _():
            cosf, sinf = _tile_lanes(cos_ref[...], 2), _tile_lanes(sin_ref[...], 2)
            kr = _k_prep(p_ref[:, :KV_WIDTH].astype(F32), kw_ref[...], cosf, sinf, ok_ref[...])[2]
            o_ref[...] = jnp.concatenate([kr.astype(BF16), p_ref[:, KV_WIDTH:]], axis=1)

    full = lambda shape: pl.BlockSpec(shape, lambda i, j: (0,) * len(shape))
    tab = pl.BlockSpec((rows, SLAB), lambda i, j: (i, 0))
    blk = pl.BlockSpec((rows, GROUP_WIDTH), lambda i, j: (i, j))
    return pl.pallas_call(
        body, name="qk_prep", grid=(s // rows, KV_COL + 1),
        in_specs=[blk, full((1, GROUP_WIDTH)), full((1, KV_WIDTH)), tab, tab, full((GROUP_WIDTH, GROUP_WIDTH)),
                  full((KV_WIDTH, KV_WIDTH))],
        out_specs=blk, out_shape=jax.ShapeDtypeStruct((s, ATTN_WIDTH + 2 * KV_WIDTH), BF16),
        compiler_params=_params("parallel", "arbitrary"))(proj, qw, kw, cos, sins, ones_q, ones_kv)


def _keys_values(kc_ref, kp_ref, vc_ref, vp_ref, head):
    lanes = slice((head // 2) * SLAB, (head // 2 + 1) * SLAB)
    dup = lambda ref: _dup_head(ref[:, lanes].astype(F32), head % 2)
    return (jnp.concatenate([dup(kp_ref), dup(kc_ref)], axis=0).astype(BF16),
            jnp.concatenate([dup(vp_ref), dup(vc_ref)], axis=0).astype(BF16))


FWD_STEP_HEADS = 4
BWD_STEP_HEADS = 2


def _swa_specs(heads):
    kv_width = heads * HEAD_DIM
    prev = lambda n: jnp.maximum(n - 1, 0)
    kv = lambda col0, row: pl.BlockSpec((BLK, kv_width), lambda gs, n: (row(n), col0 // kv_width + gs))
    cur = lambda n: n
    full = lambda shape: pl.BlockSpec(shape, lambda gs, n: (0,) * len(shape))
    gate = lambda t: pl.BlockSpec((BLK, GROUP_WIDTH), lambda gs, n: (n, GATE_COL + heads * gs + t))
    return dict(
        sinks=pl.BlockSpec(memory_space=pltpu.SMEM), gates=[gate(t) for t in range(heads)],
        kc=kv(ATTN_WIDTH, cur), kp=kv(ATTN_WIDTH, prev), vc=kv(ATTN_WIDTH + KV_WIDTH, cur), vp=kv(ATTN_WIDTH + KV_WIDTH, prev),
        tri=full((BLK, BLK)), step=pl.BlockSpec((BLK, heads * GROUP_WIDTH), lambda gs, n: (n, gs)))


def _lower_triangle():
    return jnp.tril(jnp.ones((BLK, BLK), F32))


def _swa_fwd(qk, proj, sinks):
    s = proj.shape[0]
    nb = s // BLK
    heads = FWD_STEP_HEADS
    sp = _swa_specs(heads)

    def body(sink_ref, q_ref, kc_ref, kp_ref, vc_ref, vp_ref, *rest):
        gate_refs, (tri_ref, ag_ref, o_ref) = rest[:heads], rest[heads:]
        gs, n = pl.program_id(0), pl.program_id(1)
        own, bias = _band(tri_ref, n)
        for t, gate_ref in enumerate(gate_refs):
            cols = slice(t * GROUP_WIDTH, (t + 1) * GROUP_WIDTH)
            first_head = (heads * gs + t) * GROUP_HEADS
            kcat, vcat = _keys_values(kc_ref, kp_ref, vc_ref, vp_ref, t)
            scores = _dot_nt(_stack_heads(q_ref[:, cols]), kcat)
            probs = []
            for j in range(GROUP_HEADS):
                p, _ = _head_probs(scores[j * BLK:(j + 1) * BLK], sink_ref[first_head + j], own, bias)
                probs.append(_unfold(p, own).astype(BF16))
            o = _unstack_heads(_dot(jnp.concatenate(probs, axis=0), vcat))
            gate = gate_ref[...].astype(F32)
            o_ref[:, cols] = o.astype(BF16)
            ag_ref[:, cols] = (o * (gate * _sigmoid(gate))).astype(BF16)

    shp = jax.ShapeDtypeStruct((s, ATTN_WIDTH), BF16)
    return pl.pallas_call(
        body, name="swa_fwd", grid=(KV_HEADS // heads, nb),
        in_specs=[sp["sinks"], sp["step"], sp["kc"], sp["kp"], sp["vc"], sp["vp"], *sp["gates"], sp["tri"]],
        out_specs=[sp["step"], sp["step"]], out_shape=[shp, shp],
        compiler_params=_params("parallel", "arbitrary"),
    )(sinks, qk, qk, qk, qk, qk, *[proj] * heads, _lower_triangle())


def _swa_bwd(qk, proj, dag, sinks):
    s = proj.shape[0]
    nb = s // BLK
    heads = BWD_STEP_HEADS
    sp = _swa_specs(heads)

    def body(sink_ref, q_ref, kc_ref, kp_ref, vc_ref, vp_ref, *rest):
        gate_refs = rest[:heads]
        dag_ref, tri_ref, dq_ref, dkc_ref, dkp_ref, dvc_ref, dvp_ref, dsink_ref = rest[heads:]
        gs, n = pl.program_id(0), pl.program_id(1)
        own, bias = _band(tri_ref, n)

        @pl.when(n == 0)
        def _():
            dsink_ref[...] = jnp.zeros_like(dsink_ref)

        for t, gate_ref in enumerate(gate_refs):
            cols = slice(t * GROUP_WIDTH, (t + 1) * GROUP_WIDTH)
            first_head = (heads * gs + t) * GROUP_HEADS
            kcat, vcat = _keys_values(kc_ref, kp_ref, vc_ref, vp_ref, t)
            gate = gate_ref[...].astype(F32)
            do = dag_ref[:, cols].astype(F32) * (gate * _sigmoid(gate))
            q_stack = _stack_heads(q_ref[:, cols])
            do_stack = _stack_heads(do).astype(BF16)
            scores = _dot_nt(q_stack, kcat)
            dprobs = _dot_nt(do_stack, vcat)
            probs, dscores, dsinks = [], [], []
            for j in range(GROUP_HEADS):
                rows = slice(j * BLK, (j + 1) * BLK)
                p, psink = _head_probs(scores[rows], sink_ref[first_head + j], own, bias)
                dp = _fold(dprobs[rows], own)
                delta = jnp.sum(p * dp, axis=-1, keepdims=True)
                probs.append(_unfold(p, own).astype(BF16))
                dscores.append(_unfold(p * (dp - delta), own).astype(BF16))
                dsinks.append(-jnp.sum(psink * delta, axis=0, keepdims=True))
            ds = jnp.concatenate(dscores, axis=0)
            dk = _dot_tn(ds, q_stack)
            dv = _dot_tn(jnp.concatenate(probs, axis=0), do_stack)
            dk = dk + pltpu.roll(dk, HEAD_DIM, 1)
            dv = dv + pltpu.roll(dv, HEAD_DIM, 1)
            dkp_ref[t], dkc_ref[t] = dk[:BLK], dk[BLK:]
            dvp_ref[t], dvc_ref[t] = dv[:BLK], dv[BLK:]
            dq_ref[:, cols] = _unstack_heads(_dot(ds, kcat)).astype(BF16)
            r8 = lax.broadcasted_iota(jnp.int32, (8, SLAB), 0)
            upd = jnp.zeros((8, SLAB), F32)
            for j in range(GROUP_HEADS):
                upd = jnp.where(r8 == j, dsinks[j], upd)
            dsink_ref[t] += upd

    cur_out = pl.BlockSpec((heads, BLK, SLAB), lambda gs, n: (gs, n, 0))
    prev_out = pl.BlockSpec((heads, BLK, SLAB), lambda gs, n: (gs, (n + nb - 1) % nb, 0))
    kv_shape = jax.ShapeDtypeStruct((KV_HEADS, s, SLAB), F32)
    return pl.pallas_call(
        body, name="swa_bwd", grid=(KV_HEADS // heads, nb),
        in_specs=[sp["sinks"], sp["step"], sp["kc"], sp["kp"], sp["vc"], sp["vp"], *sp["gates"], sp["step"], sp["tri"]],
        out_specs=[sp["step"], cur_out, prev_out, cur_out, prev_out,
                   pl.BlockSpec((heads, 8, SLAB), lambda gs, n: (gs, 0, 0))],
        out_shape=[jax.ShapeDtypeStruct((s, ATTN_WIDTH), BF16), kv_shape, kv_shape, kv_shape, kv_shape,
                   jax.ShapeDtypeStruct((KV_HEADS, 8, SLAB), F32)],
        compiler_params=_params("parallel", "arbitrary"),
    )(sinks, qk, qk, qk, qk, qk, *[proj] * heads, dag, _lower_triangle())


def _swa_bwd_finish(proj, dqr, o, dag, dkc, dkp, dvc, dvp, qw, kw, cos, sins, ones_q, ones_kv):
    s = proj.shape[0]
    rows = min(KV_ROWS, s)
    n_q = KV_COL
    q_of = lambda j: jnp.clip(j - 1, 0, n_q - 1)
    gate_of = lambda j: jnp.clip(j - 1 - n_q, 0, n_q - 1)

    def body(kv_ref, p_ref, dqr_ref, o_ref, dag_ref, dkc_ref, dkp_ref, dvc_ref, dvp_ref, qw_ref, kw_ref, cos_ref, sin_ref,
             oq_ref, ok_ref, dp_ref, dqw_ref, dkw_ref):
        j, i = pl.program_id(0), pl.program_id(1)

        @pl.when(j == 0)
        def _():
            @pl.when(i == 0)
            def _():
                dkw_ref[...] = jnp.zeros_like(dkw_ref)

            def assemble(cur_ref, prv_ref):
                tot = [cur_ref[h] + prv_ref[h] for h in range(KV_HEADS)]
                first = _half_mask(tot[0].shape, 0)
                return jnp.concatenate([jnp.where(first, tot[0], tot[1]), jnp.where(first, tot[2], tot[3])], axis=1)

            dkr = assemble(dkc_ref, dkp_ref)
            dv = assemble(dvc_ref, dvp_ref)
            cosf, sinf = _tile_lanes(cos_ref[...], 2), _tile_lanes(sin_ref[...], 2)
            nrm, rstd, _ = _k_prep(kv_ref[:, :KV_WIDTH].astype(F32), kw_ref[...], cosf, sinf, ok_ref[...])
            dy = dkr * cosf + _partner64(dkr * sinf)
            dn = dy * kw_ref[...]
            dk = rstd * (dn - nrm * _group_mean(dn * nrm, ok_ref[...], passes=1))
            dp_ref[...] = jnp.concatenate([dk, dv], axis=1).astype(BF16)
            dkw_ref[...] += jnp.sum(dy * nrm, axis=0, keepdims=True)

        @pl.when(jnp.logical_and(j >= 1, j <= n_q))
        def _():
            @pl.when(i == 0)
            def _():
                dqw_ref[...] = jnp.zeros_like(dqw_ref)

            cosf, sinf = _tile_lanes(cos_ref[...], 4), _tile_lanes(sin_ref[...], 4)
            nrm, rstd, _ = _q_prep(p_ref[...].astype(F32), qw_ref[...], cosf, sinf, oq_ref[...])
            dq = dqr_ref[...].astype(F32) * ATTN_SCALE
            dy = dq * cosf + _partner64(dq * sinf)
            dn = dy * qw_ref[...]
            dp_ref[...] = (rstd * (dn - nrm * _group_mean(dn * nrm, oq_ref[...], passes=1))).astype(BF16)
            dqw_ref[0] += jnp.sum(dy * nrm, axis=0, keepdims=True)

        @pl.when(j > n_q)
        def _():
            gate = p_ref[...].astype(F32)
            sg = _sigmoid(gate)
            dp_ref[...] = (dag_ref[...].astype(F32) * o_ref[...].astype(F32) * (sg * (1.0 + gate * (1.0 - sg)))).astype(BF16)

    first_pass = lambda j, i: jnp.where(j == 0, i, 0)
    acc = pl.BlockSpec((KV_HEADS, rows, SLAB), lambda j, i: (0, first_pass(j, i), 0))
    full = lambda shape: pl.BlockSpec(shape, lambda j, i: (0,) * len(shape))
    tab = pl.BlockSpec((rows, SLAB), lambda j, i: (i, 0))
    out_col = lambda j: jnp.where(j == 0, KV_COL, jnp.where(j <= n_q, j - 1, j))
    return pl.pallas_call(
        body, name="swa_bwd_finish", grid=(2 * n_q + 1, s // rows),
        in_specs=[pl.BlockSpec((rows, GROUP_WIDTH), lambda j, i: (first_pass(j, i), KV_COL)),
                  pl.BlockSpec((rows, GROUP_WIDTH), lambda j, i: (jnp.where(j == 0, 0, i), jnp.where(j <= n_q, q_of(j), j))),
                  pl.BlockSpec((rows, GROUP_WIDTH), lambda j, i: (jnp.where(jnp.logical_and(j >= 1, j <= n_q), i, 0), q_of(j))),
                  pl.BlockSpec((rows, GROUP_WIDTH), lambda j, i: (jnp.where(j > n_q, i, 0), gate_of(j))),
                  pl.BlockSpec((rows, GROUP_WIDTH), lambda j, i: (jnp.where(j > n_q, i, 0), gate_of(j))),
                  acc, acc, acc, acc, full((1, GROUP_WIDTH)), full((1, KV_WIDTH)), tab, tab,
                  full((GROUP_WIDTH, GROUP_WIDTH)), full((KV_WIDTH, KV_WIDTH))],
        out_specs=[pl.BlockSpec((rows, GROUP_WIDTH), lambda j, i: (i, out_col(j))),
                   pl.BlockSpec((1, 8, GROUP_WIDTH), lambda j, i: (q_of(j), 0, 0)), pl.BlockSpec((8, KV_WIDTH), lambda j, i: (0, 0))],
        out_shape=[jax.ShapeDtypeStruct((s, ODD_IN), BF16), jax.ShapeDtypeStruct((n_q, 8, GROUP_WIDTH), F32),
                   jax.ShapeDtypeStruct((8, KV_WIDTH), F32)],
        compiler_params=_params("arbitrary", "arbitrary"),
    )(proj, proj, dqr, o, dag, dkc, dkp, dvc, dvp, qw, kw, cos, sins, ones_q, ones_kv)


def _place():
    x, y, c = lax.axis_index("x"), lax.axis_index("y"), lax.axis_index("c")
    return x, y, c


OTHER_CHIPS = ((1, 0), (0, 1), (1, 1))


def _half_rows(ref, half, rows):
    return ref.at[pl.ds(pl.multiple_of(half * (rows // 2), 8), rows // 2)]


DMA_CHUNK_BYTES = 1 << 20
BF16_TILE_ROWS = 16


def _n_chunks(ref):
    rows = ref.shape[-2]
    nbytes = math.prod(ref.shape) * jnp.dtype(ref.dtype).itemsize
    n = 1
    while 2 * n * DMA_CHUNK_BYTES <= nbytes and rows % (2 * n * BF16_TILE_ROWS) == 0:
        n *= 2
    return n


def _row_chunk(ref, k, n):
    rows = ref.shape[-2] // n
    return ref.at[pl.ds(k * rows, rows)] if len(ref.shape) == 2 else ref.at[:, pl.ds(k * rows, rows)]


def _push(src, dst, send_sem, recv_sem, device_id):
    n = _n_chunks(src)
    for k in range(n):
        pltpu.make_async_remote_copy(src_ref=_row_chunk(src, k, n), dst_ref=_row_chunk(dst, k, n), send_sem=send_sem,
                                     recv_sem=recv_sem, device_id=device_id, device_id_type=MESH).start()
    return pltpu.make_async_remote_copy(src_ref=src, dst_ref=dst, send_sem=send_sem, recv_sem=recv_sem,
                                        device_id=device_id, device_id_type=MESH)


HBM = pl.BlockSpec(memory_space=pltpu.HBM)
SEM = pl.BlockSpec(memory_space=pltpu.SEMAPHORE)
SPLIT_COPY_EFFECT = pltpu.SideEffectType.DATAFLOW_SIDE_EFFECTING


def _in_hbm(a):
    return pltpu.with_memory_space_constraint(a, pltpu.HBM)


def _start_copies(name, arrays, plan, n_copies, after=None):
    n = len(arrays)

    def body(*refs):
        send_sem, recv_sem = refs[-n - 3], refs[-n - 2]
        for k, (src, dst, peer) in enumerate(plan(refs[:n])):
            _push(src, dst, send_sem.at[k], recv_sem.at[k], peer)
        refs[-1][...] = jnp.zeros_like(refs[-1])

    dma = pltpu.SemaphoreType.DMA((n_copies,))
    outs = pl.pallas_call(
        body, name=name,
        out_shape=(dma, dma, *[pltpu.HBM(a.shape, a.dtype) for a in arrays], jax.ShapeDtypeStruct((8, 128), F32)),
        in_specs=[HBM] * n + ([ANY] if after is not None else []),
        out_specs=(SEM, SEM, *[HBM] * n, pl.BlockSpec(memory_space=pltpu.VMEM)),
        input_output_aliases={i: i + 2 for i in range(n)},
        compiler_params=pltpu.CompilerParams(has_side_effects=SPLIT_COPY_EFFECT),
    )(*[_in_hbm(a) for a in arrays], *((after,) if after is not None else ()))
    return outs[0], outs[1], list(outs[2:2 + n]), outs[-1]


def _wait_copies(name, send_sem, recv_sem, arrays, plan, after):
    n = len(arrays)
    after = list(after) if isinstance(after, (list, tuple)) else [after]

    def body(*refs):
        send_ref, recv_ref = refs[n], refs[n + 1]
        for k, (src, dst, peer) in enumerate(plan(refs[:n])):
            cp = pltpu.make_async_remote_copy(src_ref=src, dst_ref=dst, send_sem=send_ref.at[k], recv_sem=recv_ref.at[k],
                                              device_id=peer, device_id_type=MESH)
            cp.wait_send()
            cp.wait_recv()

    return list(pl.pallas_call(
        body, name=name, out_shape=tuple(pltpu.HBM(a.shape, a.dtype) for a in arrays),
        in_specs=[HBM] * n + [SEM, SEM] + [ANY] * len(after), out_specs=tuple([HBM] * n),
        input_output_aliases={i: i for i in range(n)},
        compiler_params=pltpu.CompilerParams(has_side_effects=SPLIT_COPY_EFFECT),
    )(*arrays, send_sem, recv_sem, *after))


def _gather_region(full, kind, chip, half=None):
    if kind == "whole":
        return full.at[chip]
    if kind == "col" or isinstance(kind, tuple):
        part, parts = (0, 1) if kind == "col" else kind[1:]
        rows, width = full.shape[0], full.shape[1] // N_CHIPS
        piece = full.at[:, pl.ds(pl.multiple_of(chip * width + part * (width // parts), LANE_TILE), width // parts)]
    else:
        rows = full.shape[0] // N_CHIPS
        piece = full.at[pl.ds(pl.multiple_of(chip * rows, BF16_TILE_ROWS), rows)]
    return piece if half is None else _half_rows(piece, half, rows)


def _gather_plan(kinds):
    def plan(fulls):
        x, y, c = _place()
        copies = []
        for fx, fy in OTHER_CHIPS:
            for full, kind in zip(fulls, kinds):
                mine = _gather_region(full, kind, 2 * x + y, c)
                copies.append((mine, mine, (x ^ fx, y ^ fy, c)))
        return copies

    return plan


def _gather_start(name, fulls, kinds, after=None):
    return _start_copies(name, list(fulls), _gather_plan(kinds), 3 * len(kinds), after)


def _pass_on_plan(kinds):
    split = [i for i, kind in enumerate(kinds) if kind != "whole"]

    def plan(fulls):
        x, y, c = _place()
        copies = []
        for fx, fy in OTHER_CHIPS:
            chip = 2 * (x ^ fx) + (y ^ fy)
            for i in split:
                landed = _gather_region(fulls[i], kinds[i], chip, c)
                copies.append((landed, landed, (x, y, 1 - c)))
        return copies

    return plan, 3 * len(split)


def _gather_arrive(name, started, kinds, after, arrays=None):
    send_sem, recv_sem, fulls, _ = started
    fulls = _wait_copies(name + "_wait", send_sem, recv_sem, fulls if arrays is None else arrays, _gather_plan(kinds), after)
    plan, count = _pass_on_plan(kinds)
    return _start_copies(name + "_pass_on", fulls, plan, count)


def _gather_done(name, passed_on, kinds, after):
    send_sem, recv_sem, fulls, _ = passed_on
    return _wait_copies(name + "_pass_on_wait", send_sem, recv_sem, fulls, _pass_on_plan(kinds)[0], after)


def _allreduce_small(v):
    def body(v_ref, out_ref, buf, send_sems, recv_sems):
        x, y, c = _place()
        me = 4 * x + 2 * y + c
        buf[me] = v_ref[...]
        copies = []
        for r in range(1, N_DEV):
            peer = (x ^ (r >> 2), y ^ ((r >> 1) & 1), c ^ (r & 1))
            cp = pltpu.make_async_remote_copy(src_ref=v_ref, dst_ref=buf.at[me], send_sem=send_sems.at[r - 1],
                                              recv_sem=recv_sems.at[r - 1], device_id=peer, device_id_type=MESH)
            cp.start()
            copies.append(cp)
        for cp in copies:
            cp.wait_recv()
        for cp in copies:
            cp.wait_send()
        total = buf[0]
        for d in range(1, N_DEV):
            total = total + buf[d]
        out_ref[...] = total

    vm = pl.BlockSpec(memory_space=pltpu.VMEM)
    return pl.pallas_call(
        body, name="allreduce_small", in_specs=[vm], out_specs=vm, out_shape=jax.ShapeDtypeStruct(v.shape, v.dtype),
        scratch_shapes=[pltpu.VMEM((N_DEV,) + v.shape, v.dtype), pltpu.SemaphoreType.DMA((N_DEV - 1,)),
                        pltpu.SemaphoreType.DMA((N_DEV - 1,))],
        compiler_params=pltpu.CompilerParams(has_side_effects=True),
    )(v)


def _exchange_halves(grads, name):
    n = len(grads)

    def body(*refs):
        g, theirs = refs[:n], refs[n:2 * n]
        send_sem, recv_sem = refs[2 * n:]
        x, y, c = _place()
        copies = []
        for i in range(n):
            half = g[i].shape[1] // 2
            src = g[i].at[:, pl.ds(pl.multiple_of((1 - c) * half, BF16_TILE_ROWS), half)]
            copies.append(_push(src, theirs[i], send_sem.at[i], recv_sem.at[i], (x, y, 1 - c)))
        for cp in copies:
            cp.wait_recv()
            cp.wait_send()

    dma = pltpu.SemaphoreType.DMA
    return pl.pallas_call(
        body, name=name, in_specs=[ANY] * n, out_specs=[ANY] * n,
        out_shape=[jax.ShapeDtypeStruct((a.shape[0], a.shape[1] // 2, a.shape[2]), a.dtype) for a in grads],
        scratch_shapes=[dma((n,)), dma((n,))],
        compiler_params=pltpu.CompilerParams(has_side_effects=True),
    )(*grads)


def _pair_sum(g, theirs, core, name):
    pieces, half, cols = theirs.shape
    tr = min(half, 256)
    per = half // tr

    def body(core_ref, g_ref, t_ref, o_ref):
        del core_ref
        o_ref[...] = (g_ref[...].astype(F32) + t_ref[...].astype(F32)).astype(BF16)

    spec = pl.BlockSpec((1, tr, cols), lambda p, i, core_ref: (p, i, 0))
    return pl.pallas_call(
        body, name=name, out_shape=jax.ShapeDtypeStruct(theirs.shape, BF16),
        grid_spec=pltpu.PrefetchScalarGridSpec(
            num_scalar_prefetch=1, grid=(pieces, per),
            in_specs=[pl.BlockSpec((1, tr, cols), lambda p, i, core_ref: (p, core_ref[0] * per + i, 0)), spec],
            out_specs=spec),
        compiler_params=_params("parallel", "parallel"))(core, g, theirs)


def _scatter_plan(n):
    def plan(refs):
        parts, stacks = refs[:n], refs[n:]
        x, y, c = _place()
        copies = []
        for fx, fy in OTHER_CHIPS:
            chip = 2 * (x ^ fx) + (y ^ fy)
            for part, stack in zip(parts, stacks):
                if part.shape[0] == N_CHIPS:
                    piece = part.at[chip]
                else:
                    width = part.shape[2] // N_CHIPS
                    piece = part.at[0].at[:, pl.ds(pl.multiple_of(chip * width, 128), width)]
                copies.append((piece, stack.at[2 * x + y], (x ^ fx, y ^ fy, c)))
        return copies

    return plan


def _scatter_start(name, parts, after=None):
    def landing(a):
        return (N_CHIPS, a.shape[1], a.shape[2] if a.shape[0] == N_CHIPS else a.shape[2] // N_CHIPS)

    stacks = [lax.empty(landing(a), a.dtype) for a in parts]
    return _start_copies(name, list(parts) + stacks, _scatter_plan(len(parts)), 3 * len(parts), after)


def _scatter_finish(name, started, after):
    send_sem, recv_sem, arrays, _ = started
    n = len(arrays) // 2
    arrays = _wait_copies(name + "_wait", send_sem, recv_sem, arrays, _scatter_plan(n), after)
    return arrays[:n], arrays[n:]


def _sum_chips(part, stack, place, name):
    _, r, c = stack.shape
    tr = 256
    per = r // tr

    def body(place_ref, own_ref, a_ref, b_ref, c_ref, o_ref):
        del place_ref
        total = own_ref[0].astype(F32)
        for ref in (a_ref, b_ref, c_ref):
            total = total + ref[0].astype(F32)
        o_ref[...] = total

    if part.shape[0] == N_CHIPS:
        own = pl.BlockSpec((1, tr, c), lambda i, pr: (pr[1], i, 0))
    else:
        own = pl.BlockSpec((1, tr, c), lambda i, pr: (0, i, pr[1]))
    other = lambda flip: pl.BlockSpec((1, tr, c), lambda i, pr: (pr[1] ^ flip, i, 0))
    return pl.pallas_call(
        body, name=name, out_shape=jax.ShapeDtypeStruct((2 * r, c), F32),
        grid_spec=pltpu.PrefetchScalarGridSpec(
            num_scalar_prefetch=1, grid=(per,), in_specs=[own, other(2), other(1), other(3)],
            out_specs=pl.BlockSpec((tr, c), lambda i, pr: (pr[0] * per + i, 0))),
        compiler_params=_params("parallel"))(place, part, stack, stack, stack)


def _own_half_plan(fulls):
    x, y, c = _place()
    return [(mine, mine, (x, y, 1 - c)) for mine in (_half_rows(full, c, full.shape[0]) for full in fulls)]


MM = dict(tm=2048, tn=1024, tk=2048)
MM_LONG_K = dict(tm=1024, tn=1024, tk=4096)


def _local_step(x, target, ev_norm_w, q_norm_w, k_norm_w, sinks, own_first, weights_first, late_arrived, weights_late, emit):
    s = x.shape[0]
    cos_r, sin_r = _rope_tables(s, RET_DIM)
    cos_a, sin_a = _rope_tables(s, HEAD_DIM)
    cos_a = np.tile(cos_a, (1, 4))
    sins_a = np.tile(np.concatenate([-sin_a, sin_a], axis=1), (1, 2))
    tables = _retention_tables()
    ones_q, ones_kv = _block_diag_mean(GROUP_WIDTH), _block_diag_mean(KV_WIDTH)
    qw_g = jnp.tile(q_norm_w, (1, GROUP_WIDTH // HEAD_DIM))
    kw_kv = jnp.tile(k_norm_w, (1, KV_WIDTH // HEAD_DIM))
    sinks1 = sinks.reshape(Q_HEADS)

    own_w_in0, own_block, start_token = own_first
    h0 = _rmsnorm(x, ev_norm_w, "norm0", after=start_token)
    shifted = dict(shift=own_block * (own_w_in0.shape[1] // MM["tn"]), total=EVEN_IN // MM["tn"], tm=MM["tm"], tn=MM["tn"],
                   out_dtype=BF16)
    own_blocks = own_w_in0.shape[1] // MM["tn"]
    proj0 = _mm_shifted(h0, own_w_in0, b_shifted=False, first=0, count=own_blocks, name="proj0_own", **shifted)
    for part in range(own_blocks):
        w_in0, small_weights, token = weights_first(part, proj0)
        if small_weights is not None:
            conv_w, od_norm_w = small_weights
        proj0 = _mm_shifted(h0, w_in0, b_shifted=True, first=own_blocks + part, stride=own_blocks,
                            count=shifted["total"] // own_blocks - 1, name=f"proj0_rest{part}", into=proj0, after=token, **shifted)
    o_ret, cat, states = _retention_fwd(proj0, cos_r, sin_r, tables, after=late_arrived(proj0))
    cat = _conv_fwd(proj0, conv_w, cat)
    w_out0, w_in1, w_out1 = weights_late(cat)

    def residual_and_norm(prod, x_ref, w_ref, x1_ref, h1_ref):
        x1v = x_ref[...] + prod
        x1_ref[...] = x1v
        rstd = lax.rsqrt(jnp.mean(x1v * x1v, axis=-1, keepdims=True) + EPS)
        h1_ref[...] = (x1v * rstd * w_ref[...]).astype(BF16)

    def residual_and_loss(prod, x1_ref, t_ref, dyb_ref, sq_ref):
        diff = (x1_ref[...] + prod) - t_ref[...]
        dyb_ref[...] = (diff * (1.0 / D_MODEL)).astype(BF16)

        @pl.when(pl.program_id(0) == 0)
        def _():
            sq_ref[...] = jnp.zeros_like(sq_ref)

        sq_ref[...] += jnp.sum(jnp.sum(diff * diff, axis=1, keepdims=True), axis=0, keepdims=True)

    act = lambda dt: jax.ShapeDtypeStruct((s, D_MODEL), dt)
    x1, h1 = _mm_rows(cat, w_out0, [x], [od_norm_w], [act(F32), act(BF16)], residual_and_norm, tm=min(s, 512), name="out0")
    proj1 = _mm(h1, w_in1, mode="nn", out_dtype=BF16, name="proj1", tm=2048, tn=1536, tk=2048)
    qk = _qk_prep(proj1, qw_g, kw_kv, cos_a, sins_a, ones_q, ones_kv)
    ag, o_att = _swa_fwd(qk, proj1, sinks1)
    dy_b, sq = _mm_rows(ag, w_out1, [x1, target], [], [act(BF16), jax.ShapeDtypeStruct((8, 128), F32)],
                        residual_and_loss, tm=min(s, 512), name="out1")

    g_w_out1 = _mm(ag, dy_b, mode="tn", out_dtype=BF16, name="g_w_out1", **MM_LONG_K)
    dag = _mm(dy_b, w_out1, mode="nt", out_dtype=BF16, name="d_ag", **MM)
    dqr, dkc, dkp, dvc, dvp, dsink = _swa_bwd(qk, proj1, dag, sinks1)
    dproj1, dqw, dkw = _swa_bwd_finish(proj1, dqr, o_att, dag, dkc, dkp, dvc, dvp, qw_g, kw_kv, cos_a, sins_a, ones_q, ones_kv)
    g_w_in1 = _mm(h1, dproj1, mode="tn", out_dtype=BF16, name="g_w_in1", tm=1024, tn=768, tk=4096)
    dh1 = _mm(dproj1, w_in1, mode="nt", out_dtype=BF16, name="d_h1", tm=1024, tn=1024, tk=ODD_IN)
    dx1_b, g_norm1 = _rmsnorm_bwd(x1, od_norm_w, dh1, dy_b, "norm1_bwd", BF16)

    g_w_out0 = _mm(cat, dx1_b, mode="tn", out_dtype=BF16, name="g_w_out0", **MM_LONG_K)
    token = emit("upper", (("od_w_in", g_w_in1, "col"), ("od_w_out", g_w_out1, "row"), ("ev_w_out", g_w_out0, "row")))
    dcat = _mm(dx1_b, w_out0, mode="nt", out_dtype=BF16, name="d_cat", after=token, **MM)
    dproj0 = _retention_bwd(proj0, o_ret, states, dcat, cos_r, sin_r, tables)
    dproj0, g_conv = _conv_bwd(proj0, dcat, conv_w, dproj0)
    g_w_in0 = _mm(h0, dproj0, mode="tn", out_dtype=BF16, name="g_w_in0", **MM_LONG_K)
    token = emit("in0", (("ev_w_in", g_w_in0, "col"),))
    dh0 = _mm(dproj0, w_in0, mode="nt", out_dtype=BF16, name="d_h0", after=token, **MM_LONG_K)
    grad_x, g_norm0 = _rmsnorm_bwd(x, ev_norm_w, dh0, dx1_b, "norm0_bwd", F32)

    g_qw = dqw[:, 0, :].reshape(Q_HEADS, HEAD_DIM).sum(axis=0)
    g_kw = dkw[0].reshape(KV_HEADS, HEAD_DIM).sum(axis=0)
    g_sinks = dsink[:, :, 0].reshape(Q_HEADS)
    small = dict(ev_norm=g_norm0[0], od_norm=g_norm1[0], conv=g_conv[:3], qw=g_qw, kw=g_kw, sinks=g_sinks)
    return sq[0, 0], grad_x, small


LANE_TILE = 128


def _pack_small_grads(small, sq):
    pad = lambda v, n: jnp.pad(v, (0, n - v.shape[0]))
    tail = jnp.concatenate([pad(small["qw"], LANE_TILE), pad(small["kw"], LANE_TILE), small["sinks"]])
    rows = [small["ev_norm"], small["od_norm"]] + [small["conv"][t] for t in range(3)] + [tail, sq.reshape(1)]
    rows += [jnp.zeros((1,), F32)] * (8 - len(rows))
    return jnp.stack([pad(r, D_MODEL) for r in rows])


def _adamw_small(tot, chip, params):
    n = len(params)

    def body(chip_ref, tot_ref, *refs):
        ins, outs = refs[:3 * n], refs[3 * n:]
        c = chip_ref[0]

        def own(rows, width):
            blocks = [tot_ref[rows, k * width:(k + 1) * width] for k in range(N_CHIPS)]
            g = blocks[-1]
            for k in reversed(range(N_CHIPS - 1)):
                g = jnp.where(c == k, blocks[k], g)
            return g

        grads = [tot_ref[0:1, :], own(slice(2, 5), CONV_WIDTH // N_CHIPS), own(slice(1, 2), D_MODEL // N_CHIPS),
                 tot_ref[5:6, 0:HEAD_DIM], tot_ref[5:6, LANE_TILE:LANE_TILE + HEAD_DIM],
                 tot_ref[5:6, 2 * LANE_TILE:2 * LANE_TILE + Q_HEADS]]
        for i, g in enumerate(grads):
            w_ref, m_ref, v_ref = ins[3 * i:3 * i + 3]
            g_out, d_out, nm_out, nv_out = outs[4 * i:4 * i + 4]
            at = (0,) if len(w_ref.shape) == 3 else ()
            delta, nm, nv = _adamw_math(w_ref[at] if at else w_ref[...], g, m_ref[at] if at else m_ref[...],
                                        v_ref[at] if at else v_ref[...])
            for ref, val in ((g_out, g), (d_out, delta), (nm_out, nm), (nv_out, nv)):
                if at:
                    ref[0] = val
                else:
                    ref[...] = val

    vm = pl.BlockSpec(memory_space=pltpu.VMEM)
    flat = [a for p in params for a in p]
    outs = pl.pallas_call(
        body, name="adamw_small", in_specs=[pl.BlockSpec(memory_space=pltpu.SMEM), vm] + [vm] * len(flat),
        out_specs=[vm] * (4 * n), out_shape=[jax.ShapeDtypeStruct(p[0].shape, F32) for p in params for _ in range(4)],
    )(chip, tot, *flat)
    return [tuple(outs[4 * i:4 * i + 4]) for i in range(n)]


class _ReduceScatter:
    def __init__(self, place):
        self.place = place
        self.started = []
        self.sharing = {}

    def send(self, tag, grads):
        pieces = [g[None] if kind == "col" else g.reshape(N_CHIPS, g.shape[0] // N_CHIPS, g.shape[1]) for _, g, kind in grads]
        theirs = _exchange_halves(pieces, "exchange_halves_" + tag)
        parts = [_pair_sum(g, t, self.place[:1], "pair_sum_" + nm) for g, t, (nm, _, _) in zip(pieces, theirs, grads)]
        started = _scatter_start("scatter_" + tag, parts)
        token = self.sum_up(started[3])
        self.started.append((tag, [nm for nm, _, _ in grads], started))
        return started[3] if token is None else token

    def sum_up(self, after):
        token = None
        for tag, group, started in self.started:
            if tag not in self.sharing:
                parts, stacks = _scatter_finish("scatter_" + tag, started, after)
                sums = [_sum_chips(p, s, self.place, "chip_sum_" + nm) for p, s, nm in zip(parts, stacks, group)]
                self.sharing[tag] = (group, _start_copies("share_" + tag, sums, _own_half_plan, len(sums)))
                token = self.sharing[tag][1][3]
        return token

    def result(self, tag, after):
        group, (send_sem, recv_sem, sums, _) = self.sharing[tag]
        return dict(zip(group, _wait_copies("share_" + tag + "_wait", send_sem, recv_sem, sums, _own_half_plan, after)))


def kernel(x, ev_norm_w, ev_w_in, ev_conv_w, ev_w_out, od_norm_w, od_w_in, od_q_norm_w, od_k_norm_w, od_sinks, od_w_out, loss_target, m_ev_norm_w, m_ev_w_in, m_ev_conv_w, m_ev_w_out, m_od_norm_w, m_od_w_in, m_od_q_norm_w, m_od_k_norm_w, m_od_sinks, m_od_w_out, v_ev_norm_w, v_ev_w_in, v_ev_conv_w, v_ev_w_out, v_od_norm_w, v_od_w_in, v_od_q_norm_w, v_od_k_norm_w, v_od_sinks, v_od_w_out):
    my_chip = 2 * lax.axis_index("x") + lax.axis_index("y")
    place = jnp.stack([lax.axis_index("c"), my_chip]).astype(jnp.int32)
    shard_w = D_MODEL // N_CHIPS
    conv_shard = CONV_WIDTH // N_CHIPS

    small_in = jnp.zeros((8, shard_w), F32)
    small_in = small_in.at[0].set(od_norm_w[0]).at[1:4, :conv_shard].set(ev_conv_w[0])
    small_in = lax.dynamic_update_slice(jnp.zeros((N_CHIPS, 8, shard_w), F32), small_in[None], (my_chip, 0, 0))
    chip = place[1:]
    first_kinds, second_kinds, late_kinds = (("col", 0, 2), "whole"), (("col", 1, 2),), ("row", "col", "row")
    w_in0_own_place, w_in0_shard = _cast_into_gathered(ev_w_in[0], "col", chip, "cast_w_in0", keep_shard=True)
    first = _gather_start("gather_first", [w_in0_own_place, small_in], first_kinds)
    second = _gather_start("gather_second", [first[2][0]], second_kinds)
    late_own = [_cast_into_gathered(ev_w_out[0], "row", chip, "cast_w_out0"),
                _cast_into_gathered(od_w_in[0], "col", chip, "cast_w_in1"),
                _cast_into_gathered(od_w_out[0], "row", chip, "cast_w_out1")]
    late, w_in0_so_far = [], []

    def weights_first(part, after):
        if part == 0:
            passed_on = _gather_arrive("gather_first", first, first_kinds, [after] + late_own, arrays=[second[2][0], first[2][1]])
            w_in0, small_all = _gather_done("gather_first", passed_on, first_kinds, passed_on[3])
            w_in0_so_far.append(w_in0)
            late.append(_gather_start("gather_late", late_own, late_kinds, after=small_all))
            od_norm_full = small_all[:, 0, :].reshape(1, D_MODEL)
            conv_full = jnp.transpose(small_all[:, 1:4, :conv_shard], (1, 0, 2)).reshape(3, CONV_WIDTH)
            return w_in0, (conv_full, od_norm_full), late[0][3]
        passed_on = _gather_arrive("gather_second", second, second_kinds, after, arrays=w_in0_so_far)
        (w_in0,) = _gather_done("gather_second", passed_on, second_kinds, passed_on[3])
        return w_in0, None, None

    def late_arrived(after):
        late.append(_gather_arrive("gather_late", late[0], late_kinds, after))
        return late[1][3]

    def weights_late(after):
        return _gather_done("gather_late", late[1], late_kinds, after)

    reduce_scatter = _ReduceScatter(place)
    sq, grad_x, small = _local_step(x[0], loss_target[0], ev_norm_w, od_q_norm_w, od_k_norm_w, od_sinks,
                                    (w_in0_shard, chip, second[3]), weights_first, late_arrived, weights_late,
                                    reduce_scatter.send)

    token = reduce_scatter.sum_up(grad_x)
    upd = {}
    upper = reduce_scatter.result("upper", token)
    upd["od_w_in"] = _adamw(od_w_in[0], upper["od_w_in"], m_od_w_in[0], v_od_w_in[0], "adamw_od_w_in")
    upd["od_w_out"] = _adamw(od_w_out[0], upper["od_w_out"], m_od_w_out[0], v_od_w_out[0], "adamw_od_w_out")
    upd["ev_w_out"] = _adamw(ev_w_out[0], upper["ev_w_out"], m_ev_w_out[0], v_ev_w_out[0], "adamw_ev_w_out")
    g_ev_w_in = reduce_scatter.result("in0", [upd[nm][1] for nm in ("od_w_in", "od_w_out", "ev_w_out")])["ev_w_in"]
    upd["ev_w_in"] = _adamw(ev_w_in[0], g_ev_w_in, m_ev_w_in[0], v_ev_w_in[0], "adamw_ev_w_in")
    tot = _allreduce_small(_pack_small_grads(small, sq))
    loss = 0.5 * tot[6, 0] / D_MODEL

    smalls = (("ev_norm_w", ev_norm_w, m_ev_norm_w, v_ev_norm_w), ("ev_conv_w", ev_conv_w, m_ev_conv_w, v_ev_conv_w),
              ("od_norm_w", od_norm_w, m_od_norm_w, v_od_norm_w), ("od_q_norm_w", od_q_norm_w, m_od_q_norm_w, v_od_q_norm_w),
              ("od_k_norm_w", od_k_norm_w, m_od_k_norm_w, v_od_k_norm_w), ("od_sinks", od_sinks, m_od_sinks, v_od_sinks))
    for (nm, _, _, _), result in zip(smalls, _adamw_small(tot, chip, [p[1:] for p in smalls])):
        upd[nm] = result
    for nm in ("ev_w_in", "ev_w_out", "od_w_in", "od_w_out"):
        upd[nm] = tuple(u[None] for u in upd[nm])
    order = ("ev_norm_w", "ev_w_in", "ev_conv_w", "ev_w_out", "od_norm_w", "od_w_in", "od_q_norm_w", "od_k_norm_w", "od_sinks", "od_w_out")
    return (loss, grad_x[None], *[upd[nm][0] for nm in order], *[upd[nm][1] for nm in order],
            *[upd[nm][2] for nm in order], *[upd[nm][3] for nm in order])
```

```python
import math

import jax
import jax.numpy as jnp
import numpy as np
from jax import lax
from jax.experimental import pallas as pl
from jax.experimental.pallas import tpu as pltpu

F32 = jnp.float32
BF16 = jnp.bfloat16

D_MODEL = 2048
RET_HEADS = 4
RET_DIM = 256
RET_WIDTH = 1024
CONV_WIDTH = 1024
EVEN_IN = 8192
Q_HEADS = 32
HEAD_DIM = 64
KV_HEADS = 4
KV_WIDTH = 256
ATTN_WIDTH = 2048
ODD_IN = 4608
BLK = 128
ROPE_THETA = 10000.0
EPS = 1e-6
ADAM_LR = 0.001
ADAM_B1 = 0.9
ADAM_B2 = 0.999
ADAM_EPS = 1e-08
ADAM_WD = 0.01
ADAM_STEP = 10
N_CHIPS = 4
N_DEV = 8
VMEM_LIMIT_BYTES = 56 * 1024 * 1024
MESH = pl.DeviceIdType.MESH
ANY = pl.BlockSpec(memory_space=pl.ANY)


def _params(*sem):
    return pltpu.CompilerParams(dimension_semantics=sem, vmem_limit_bytes=VMEM_LIMIT_BYTES)


def _dot(a, b):
    return jnp.dot(a, b, preferred_element_type=F32)


def _dot_nt(a, b):
    return lax.dot_general(a, b, (((1,), (1,)), ((), ())), preferred_element_type=F32)


def _dot_tn(a, b):
    return lax.dot_general(a, b, (((0,), (0,)), ((), ())), preferred_element_type=F32)


def _sigmoid(x):
    return 1.0 / (1.0 + jnp.exp(-x))


def _mm(a, b, *, mode, tm, tn, tk, out_dtype, name, add=None, after=None, m_blocks=None, into=None):
    if mode == "nn":
        (m, k), n = a.shape, b.shape[1]
    elif mode == "nt":
        (m, k), n = a.shape, b.shape[0]
    else:
        (k, m), n = a.shape, b.shape[1]
    tm, tn, tk = min(tm, m), min(tn, n), min(tk, k)
    assert m % tm == 0 and n % tn == 0 and k % tk == 0, (name, m, n, k)
    nk = k // tk
    first_m, count_m = (0, m // tm) if m_blocks is None else m_blocks
    dot = {"nn": _dot, "nt": _dot_nt, "tn": _dot_tn}[mode]
    a_spec = (pl.BlockSpec((tk, tm), lambda i, j, kk: (kk, i + first_m)) if mode == "tn"
              else pl.BlockSpec((tm, tk), lambda i, j, kk: (i + first_m, kk)))
    b_spec = (pl.BlockSpec((tn, tk), lambda i, j, kk: (j, kk)) if mode == "nt"
              else pl.BlockSpec((tk, tn), lambda i, j, kk: (kk, j)))
    o_spec = pl.BlockSpec((tm, tn), lambda i, j, kk: (i + first_m, j))
    has_add = add is not None

    def body(*refs):
        a_ref, b_ref = refs[0], refs[1]
        add_ref = refs[2] if has_add else None
        o_ref, acc_ref = refs[-2], refs[-1]
        p = dot(a_ref[...], b_ref[...])

        def finish(total):
            if has_add:
                total = total + add_ref[...].astype(F32)
            o_ref[...] = total.astype(out_dtype)

        if nk == 1:
            finish(p)
        else:
            kk = pl.program_id(2)

            @pl.when(kk == 0)
            def _():
                acc_ref[...] = p

            @pl.when(jnp.logical_and(kk > 0, kk < nk - 1))
            def _():
                acc_ref[...] += p

            @pl.when(kk == nk - 1)
            def _():
                finish(acc_ref[...] + p)

    extra = [arr for arr in (after, into) if arr is not None]
    in_specs = [a_spec, b_spec] + ([o_spec] if has_add else []) + [ANY] * len(extra)
    args = (a, b) + ((add,) if has_add else ()) + tuple(extra)
    return pl.pallas_call(
        body, name=name, grid=(count_m, n // tn, nk), in_specs=in_specs, out_specs=o_spec,
        out_shape=jax.ShapeDtypeStruct((m, n), out_dtype),
        input_output_aliases={len(args) - 1: 0} if into is not None else {},
        scratch_shapes=[pltpu.VMEM((tm, tn) if nk > 1 else (8, 128), F32)],
        compiler_params=_params("parallel", "parallel", "arbitrary"),
    )(*args)


def _mm_shifted(a, b, shift, *, b_shifted, first, count, total, tm, tn, out_dtype, name, stride=1, into=None, after=None):
    m, k = a.shape
    tm = min(tm, m)
    assert m % tm == 0
    col = lambda j, shift_ref: (shift_ref[0] + first + stride * j) % total
    extra = [arr for arr in (into, after) if arr is not None]

    def body(shift_ref, a_ref, b_ref, *rest):
        del shift_ref
        rest[-1][...] = _dot(a_ref[...], b_ref[...]).astype(out_dtype)

    return pl.pallas_call(
        body, name=name, out_shape=jax.ShapeDtypeStruct((m, total * tn), out_dtype),
        grid_spec=pltpu.PrefetchScalarGridSpec(
            num_scalar_prefetch=1, grid=(m // tm, count),
            in_specs=[pl.BlockSpec((tm, k), lambda i, j, s: (i, 0)),
                      pl.BlockSpec((k, tn), (lambda i, j, s: (0, col(j, s))) if b_shifted else (lambda i, j, s: (0, j)))]
            + [ANY] * len(extra),
            out_specs=pl.BlockSpec((tm, tn), lambda i, j, s: (i, col(j, s)))),
        input_output_aliases={3: 0} if into is not None else {},
        compiler_params=_params("parallel", "arbitrary"))(shift, a, b, *extra)


def _mm_rows(a, b, rows_in, vecs_in, out_shapes, epilogue, *, tm, name):
    m, k = a.shape
    n = b.shape[1]
    assert m % tm == 0
    row = pl.BlockSpec((tm, n), lambda i: (i, 0))

    def body(a_ref, b_ref, *rest):
        epilogue(_dot(a_ref[...], b_ref[...]), *rest)

    out_specs = [row if tuple(s.shape) == (m, n) else pl.BlockSpec(s.shape, lambda i: (0, 0)) for s in out_shapes]
    return pl.pallas_call(
        body, name=name, grid=(m // tm,),
        in_specs=[pl.BlockSpec((tm, k), lambda i: (i, 0)), pl.BlockSpec((k, n), lambda i: (0, 0))] + [row] * len(rows_in)
        + [pl.BlockSpec((1, n), lambda i: (0, 0))] * len(vecs_in),
        out_specs=out_specs, out_shape=out_shapes, compiler_params=_params("arbitrary"),
    )(a, b, *rows_in, *vecs_in)


def _cast_into_gathered(w, kind, chip, name, keep_shard=False):
    r, c = w.shape
    tr = min(r, 512)
    per = r // tr

    def body(chip_ref, w_ref, *outs):
        del chip_ref
        for o_ref in outs:
            o_ref[...] = w_ref[...].astype(BF16)

    if kind == "col":
        shape, out_map = (r, N_CHIPS * c), (lambda i, chip_ref: (i, chip_ref[0]))
    else:
        shape, out_map = (N_CHIPS * r, c), (lambda i, chip_ref: (chip_ref[0] * per + i, 0))
    plain = pl.BlockSpec((tr, c), lambda i, chip_ref: (i, 0))
    out = pl.pallas_call(
        body, name=name,
        out_shape=[jax.ShapeDtypeStruct(shape, BF16)] + ([jax.ShapeDtypeStruct((r, c), BF16)] if keep_shard else []),
        grid_spec=pltpu.PrefetchScalarGridSpec(
            num_scalar_prefetch=1, grid=(per,), in_specs=[plain],
            out_specs=[pl.BlockSpec((tr, c), out_map)] + ([plain] if keep_shard else [])),
        compiler_params=_params("parallel"))(chip, w)
    return out if keep_shard else out[0]


NORM_ROWS = 512


def _rmsnorm(x, w, name, after=None):
    s, d = x.shape
    tr = NORM_ROWS

    def body(x_ref, w_ref, *rest):
        xv = x_ref[...]
        rstd = lax.rsqrt(jnp.mean(xv * xv, axis=-1, keepdims=True) + EPS)
        rest[-1][...] = (xv * rstd * w_ref[...]).astype(BF16)

    return pl.pallas_call(
        body, name=name, grid=(s // tr,),
        in_specs=[pl.BlockSpec((tr, d), lambda i: (i, 0)), pl.BlockSpec((1, d), lambda i: (0, 0))]
        + ([ANY] if after is not None else []),
        out_specs=pl.BlockSpec((tr, d), lambda i: (i, 0)),
        out_shape=jax.ShapeDtypeStruct((s, d), BF16), compiler_params=_params("parallel"),
    )(x, w, *((after,) if after is not None else ()))


def _rmsnorm_bwd(x, w, dh, dres, name, out_dtype):
    s, d = x.shape
    tr = NORM_ROWS

    def body(x_ref, w_ref, dh_ref, dres_ref, dx_ref, dw_ref):
        xv = x_ref[...]
        rstd = lax.rsqrt(jnp.mean(xv * xv, axis=-1, keepdims=True) + EPS)
        nrm = xv * rstd
        dhv = dh_ref[...].astype(F32)
        dn = dhv * w_ref[...]
        dx = dres_ref[...].astype(F32) + rstd * (dn - nrm * jnp.mean(dn * nrm, axis=-1, keepdims=True))
        dx_ref[...] = dx.astype(out_dtype)

        @pl.when(pl.program_id(0) == 0)
        def _():
            dw_ref[...] = jnp.zeros_like(dw_ref)

        dw_ref[...] += jnp.sum(dhv * nrm, axis=0, keepdims=True)

    row = pl.BlockSpec((tr, d), lambda i: (i, 0))
    return pl.pallas_call(
        body, name=name, grid=(s // tr,),
        in_specs=[row, pl.BlockSpec((1, d), lambda i: (0, 0)), row, row],
        out_specs=[row, pl.BlockSpec((8, d), lambda i: (0, 0))],
        out_shape=[jax.ShapeDtypeStruct((s, d), out_dtype), jax.ShapeDtypeStruct((8, d), F32)],
        compiler_params=_params("arbitrary"))(x, w, dh, dres)


def _adamw_math(w, g, m, v):
    nm = ADAM_B1 * m + (1.0 - ADAM_B1) * g
    nv = ADAM_B2 * v + (1.0 - ADAM_B2) * (g * g)
    m_hat = nm / (1.0 - ADAM_B1 ** ADAM_STEP)
    v_hat = nv / (1.0 - ADAM_B2 ** ADAM_STEP)
    return -ADAM_LR * (m_hat / (jnp.sqrt(v_hat) + ADAM_EPS) + ADAM_WD * w), nm, nv


def _adamw(w, g, m, v, name):
    r, c = w.shape
    tr = min(r, 256)
    assert r % tr == 0

    def body(w_ref, g_ref, m_ref, v_ref, g_out, d_ref, nm_ref, nv_ref):
        gv = g_ref[...]
        g_out[...] = gv
        d_ref[...], nm_ref[...], nv_ref[...] = _adamw_math(w_ref[...], gv, m_ref[...], v_ref[...])

    spec = pl.BlockSpec((tr, c), lambda i: (i, 0))
    shp = jax.ShapeDtypeStruct((r, c), F32)
    return pl.pallas_call(body, name=name, grid=(r // tr,), in_specs=[spec] * 4, out_specs=[spec] * 4,
                          out_shape=[shp] * 4, compiler_params=_params("parallel"))(w, g, m, v)


def _rope_tables(s, dim):
    inv = (1.0 / (ROPE_THETA ** (np.arange(0, dim, 2, dtype=np.float64) / dim))).astype(np.float32)
    ang = (np.arange(s, dtype=np.float32)[:, None] * inv[None, :]).astype(np.float64)
    return np.cos(ang).astype(np.float32), np.sin(ang).astype(np.float32)


def _rope_half(x, cos, sin):
    h = x.shape[1] // 2
    x1, x2 = x[:, :h], x[:, h:]
    return jnp.concatenate([x1 * cos - x2 * sin, x2 * cos + x1 * sin], axis=1)


def _unrope_half(dy, cos, sin):
    h = dy.shape[1] // 2
    d1, d2 = dy[:, :h], dy[:, h:]
    return jnp.concatenate([d1 * cos + d2 * sin, d2 * cos - d1 * sin], axis=1)


def _lane(shape):
    return lax.broadcasted_iota(jnp.int32, shape, 1)


def _partner64(x):
    w = x.shape[1]
    first = (_lane(x.shape) % HEAD_DIM) < (HEAD_DIM // 2)
    return jnp.where(first, pltpu.roll(x, w - HEAD_DIM // 2, 1), pltpu.roll(x, HEAD_DIM // 2, 1))


def _tile_lanes(t, reps):
    return t if reps == 1 else jnp.concatenate([t] * reps, axis=1)


def _group_mean(x, ones_bd, passes=2):
    hi = x.astype(BF16)
    if passes == 1:
        return _dot(hi, ones_bd)
    lo = (x - hi.astype(F32)).astype(BF16)
    return _dot(hi, ones_bd) + _dot(lo, ones_bd)


def _block_diag_mean(width):
    idx = jnp.arange(width) // HEAD_DIM
    return jnp.where(idx[:, None] == idx[None, :], 1.0 / HEAD_DIM, 0.0).astype(BF16)


RET_STEP = 4


def _retention_tables():
    h = RET_HEADS
    log_g = jnp.log(1.0 - 2.0 ** (-5.0 - jnp.arange(h, dtype=F32)))
    idx = jnp.arange(BLK, dtype=F32)
    diff = idx[:, None] - idx[None, :]
    intra = jnp.where(diff >= 0, jnp.exp(log_g[:, None, None] * jnp.maximum(diff, 0.0)), 0.0).astype(F32)
    q_dec = jnp.exp(log_g[:, None] * (idx[None, :] + 1.0)).astype(F32)[:, :, None]
    k_dec = jnp.exp(log_g[:, None] * (BLK - 1.0 - idx[None, :])).astype(F32)[:, :, None]
    chunk_dec = jnp.exp(log_g * BLK).astype(F32)[:, None, None]
    return intra, q_dec, k_dec, chunk_dec


def _retention_fwd(proj, cos, sin, tables, after=None):
    s = proj.shape[0]
    nc = s // BLK
    intra, q_dec, k_dec, chunk_dec = tables

    def body(p_ref, cos_ref, sin_ref, in_ref, qd_ref, kd_ref, cd_ref, *rest):
        o_ref, cat_ref, st_ref, state = rest[-4:]

        @pl.when(pl.program_id(0) == 0)
        def _():
            state[...] = jnp.zeros_like(state)

        for c in range(RET_STEP):
            rows = slice(c * BLK, (c + 1) * BLK)
            cosv, sinv = cos_ref[rows, :], sin_ref[rows, :]
            for h in range(RET_HEADS):
                c0 = h * RET_DIM
                q = p_ref[rows, c0:c0 + RET_DIM].astype(F32)
                k = p_ref[rows, RET_WIDTH + c0:RET_WIDTH + c0 + RET_DIM].astype(F32)
                v = p_ref[rows, 2 * RET_WIDTH + c0:2 * RET_WIDTH + c0 + RET_DIM]
                g = p_ref[rows, 3 * RET_WIDTH + c0:3 * RET_WIDTH + c0 + RET_DIM].astype(F32)
                qb = _rope_half(q, cosv, sinv).astype(BF16)
                kr = _rope_half(k, cosv, sinv) * (RET_DIM ** -0.5)
                kb = kr.astype(BF16)
                scores = _dot_nt(qb, kb) * in_ref[h]
                inner = _dot(scores.astype(BF16), v)
                prev = state[h]
                prev_b = prev.astype(BF16)
                st_ref[h, c] = prev_b
                o = inner + _dot(qb, prev_b) * qd_ref[h]
                o_ref[rows, c0:c0 + RET_DIM] = o
                rstd = lax.rsqrt(jnp.mean(o * o, axis=-1, keepdims=True) + EPS)
                cat_ref[rows, c0:c0 + RET_DIM] = (o * rstd * (g * _sigmoid(g))).astype(BF16)
                state[h] = cd_ref[h] * prev + _dot_tn((kr * kd_ref[h]).astype(BF16), v)

    full = lambda shape: pl.BlockSpec(shape, lambda n: (0,) * len(shape))
    step = RET_STEP * BLK
    return pl.pallas_call(
        body, name="retention_fwd", grid=(nc // RET_STEP,),
        in_specs=[pl.BlockSpec((step, 4 * RET_WIDTH), lambda n: (n, 0)),
                  pl.BlockSpec((step, RET_DIM // 2), lambda n: (n, 0)), pl.BlockSpec((step, RET_DIM // 2), lambda n: (n, 0)),
                  full((RET_HEADS, BLK, BLK)), full((RET_HEADS, BLK, 1)), full((RET_HEADS, BLK, 1)), full((RET_HEADS, 1, 1))]
        + ([ANY] if after is not None else []),
        out_specs=[pl.BlockSpec((step, RET_WIDTH), lambda n: (n, 0)), pl.BlockSpec((step, RET_WIDTH), lambda n: (n, 0)),
                   pl.BlockSpec((RET_HEADS, RET_STEP, RET_DIM, RET_DIM), lambda n: (0, n, 0, 0))],
        out_shape=[jax.ShapeDtypeStruct((s, RET_WIDTH), F32), jax.ShapeDtypeStruct((s, D_MODEL), BF16),
                   jax.ShapeDtypeStruct((RET_HEADS, nc, RET_DIM, RET_DIM), BF16)],
        scratch_shapes=[pltpu.VMEM((RET_HEADS, RET_DIM, RET_DIM), F32)],
        compiler_params=_params("arbitrary"),
    )(proj, cos, sin, intra, q_dec, k_dec, chunk_dec, *((after,) if after is not None else ()))


def _retention_bwd(proj, o, states, dcat, cos, sin, tables):
    s = proj.shape[0]
    nc = s // BLK
    intra, q_dec, k_dec, chunk_dec = tables

    def body(p_ref, o_ref, st_ref, dc_ref, cos_ref, sin_ref, in_ref, qd_ref, kd_ref, cd_ref, dp_ref, dstate):
        @pl.when(pl.program_id(0) == 0)
        def _():
            dstate[...] = jnp.zeros_like(dstate)

        for c in reversed(range(RET_STEP)):
            rows = slice(c * BLK, (c + 1) * BLK)
            cosv, sinv = cos_ref[rows, :], sin_ref[rows, :]
            for h in range(RET_HEADS):
                c0 = h * RET_DIM
                q = p_ref[rows, c0:c0 + RET_DIM].astype(F32)
                k = p_ref[rows, RET_WIDTH + c0:RET_WIDTH + c0 + RET_DIM].astype(F32)
                v = p_ref[rows, 2 * RET_WIDTH + c0:2 * RET_WIDTH + c0 + RET_DIM]
                g = p_ref[rows, 3 * RET_WIDTH + c0:3 * RET_WIDTH + c0 + RET_DIM].astype(F32)
                o = o_ref[rows, c0:c0 + RET_DIM]
                dc = dc_ref[rows, c0:c0 + RET_DIM].astype(F32)
                rstd = lax.rsqrt(jnp.mean(o * o, axis=-1, keepdims=True) + EPS)
                nrm = o * rstd
                sg = _sigmoid(g)
                dg = dc * nrm * (sg * (1.0 + g * (1.0 - sg)))
                dn = dc * (g * sg)
                do = rstd * (dn - nrm * jnp.mean(dn * nrm, axis=-1, keepdims=True))
                qb = _rope_half(q, cosv, sinv).astype(BF16)
                kr = _rope_half(k, cosv, sinv) * (RET_DIM ** -0.5)
                kb = kr.astype(BF16)
                mask = in_ref[h]
                qd, kd = qd_ref[h], kd_ref[h]
                prev_b = st_ref[h, c]
                dnext = dstate[h]
                dnext_b = dnext.astype(BF16)
                att = (_dot_nt(qb, kb) * mask).astype(BF16)
                do_b = do.astype(BF16)
                doq = (do * qd).astype(BF16)
                dv = _dot_tn(att, do_b) + _dot((kr * kd).astype(BF16), dnext_b)
                ds = (_dot_nt(do_b, v) * mask).astype(BF16)
                dqr = _dot(ds, kb) + _dot_nt(doq, prev_b)
                dkr = _dot_tn(ds, qb) + _dot_nt(v, dnext_b) * kd
                dstate[h] = cd_ref[h] * dnext + _dot_tn(qb, doq)
                dq = _unrope_half(dqr, cosv, sinv)
                dk = _unrope_half(dkr * (RET_DIM ** -0.5), cosv, sinv)
                dp_ref[rows, c0:c0 + RET_DIM] = dq.astype(BF16)
                dp_ref[rows, RET_WIDTH + c0:RET_WIDTH + c0 + RET_DIM] = dk.astype(BF16)
                dp_ref[rows, 2 * RET_WIDTH + c0:2 * RET_WIDTH + c0 + RET_DIM] = dv.astype(BF16)
                dp_ref[rows, 3 * RET_WIDTH + c0:3 * RET_WIDTH + c0 + RET_DIM] = dg.astype(BF16)

    steps = nc // RET_STEP
    rev = lambda n: steps - 1 - n
    full = lambda shape: pl.BlockSpec(shape, lambda n: (0,) * len(shape))
    step = RET_STEP * BLK
    return pl.pallas_call(
        body, name="retention_bwd", grid=(steps,),
        in_specs=[pl.BlockSpec((step, 4 * RET_WIDTH), lambda n: (rev(n), 0)),
                  pl.BlockSpec((step, RET_WIDTH), lambda n: (rev(n), 0)),
                  pl.BlockSpec((RET_HEADS, RET_STEP, RET_DIM, RET_DIM), lambda n: (0, rev(n), 0, 0)),
                  pl.BlockSpec((step, RET_WIDTH), lambda n: (rev(n), 0)),
                  pl.BlockSpec((step, RET_DIM // 2), lambda n: (rev(n), 0)), pl.BlockSpec((step, RET_DIM // 2), lambda n: (rev(n), 0)),
                  full((RET_HEADS, BLK, BLK)), full((RET_HEADS, BLK, 1)), full((RET_HEADS, BLK, 1)), full((RET_HEADS, 1, 1))],
        out_specs=pl.BlockSpec((step, 4 * RET_WIDTH), lambda n: (rev(n), 0)),
        out_shape=jax.ShapeDtypeStruct((s, EVEN_IN), BF16),
        scratch_shapes=[pltpu.VMEM((RET_HEADS, RET_DIM, RET_DIM), F32)],
        compiler_params=_params("arbitrary"))(proj, o, states, dcat, cos, sin, intra, q_dec, k_dec, chunk_dec)


CONV_ROWS = 256
HALO = 16


def _conv_pieces(p, halo, first):
    gb, gc, u, gv = (p[:, i * CONV_WIDTH:(i + 1) * CONV_WIDTH] for i in range(4))
    cu = gc * u
    hcu = halo[:, CONV_WIDTH:2 * CONV_WIDTH] * halo[:, 2 * CONV_WIDTH:3 * CONV_WIDTH]
    hcu = jnp.where(first, 0.0, hcu)
    r1, r2 = hcu[HALO - 1:HALO], hcu[HALO - 2:HALO - 1]
    row = lax.broadcasted_iota(jnp.int32, cu.shape, 0)
    m1 = jnp.where(row == 0, r1, pltpu.roll(cu, 1, 0))
    m2 = jnp.where(row == 0, r2, jnp.where(row == 1, r1, pltpu.roll(cu, 2, 0)))
    return gb, gc, u, gv, cu, m1, m2


def _conv_fwd(proj, conv_w, cat):
    s = proj.shape[0]
    per = CONV_ROWS // HALO

    def body(p_ref, halo_ref, w_ref, cat_in, cat_ref):
        del cat_in
        first = pl.program_id(0) == 0
        gb, _, _, gv, cu, m1, m2 = _conv_pieces(p_ref[...].astype(F32), halo_ref[...].astype(F32), first)
        conv = w_ref[0:1, :] * m2 + w_ref[1:2, :] * m1 + w_ref[2:3, :] * cu
        cat_ref[...] = (gb * conv * (gv * _sigmoid(gv))).astype(BF16)

    return pl.pallas_call(
        body, name="conv_fwd", grid=(s // CONV_ROWS,),
        in_specs=[pl.BlockSpec((CONV_ROWS, 4 * CONV_WIDTH), lambda i: (i, 1)),
                  pl.BlockSpec((HALO, 4 * CONV_WIDTH), lambda i: (jnp.maximum(i * per - 1, 0), 1)),
                  pl.BlockSpec((3, CONV_WIDTH), lambda i: (0, 0)), ANY],
        out_specs=pl.BlockSpec((CONV_ROWS, CONV_WIDTH), lambda i: (i, 1)),
        out_shape=jax.ShapeDtypeStruct(cat.shape, cat.dtype), input_output_aliases={3: 0},
        compiler_params=_params("parallel"))(proj, proj, conv_w, cat)


def _conv_bwd(proj, dcat, conv_w, dproj):
    s = proj.shape[0]
    per = CONV_ROWS // HALO
    last_halo = s // HALO - 1
    nsteps = s // CONV_ROWS

    def body(p_ref, halo_ref, nxt_ref, dc_ref, dnxt_ref, w_ref, dp_in, dp_ref, dw_ref):
        del dp_in
        i = pl.program_id(0)
        gb, gc, u, gv, cu, m1, m2 = _conv_pieces(p_ref[...].astype(F32), halo_ref[...].astype(F32), i == 0)
        w0, w1, w2 = w_ref[0:1, :], w_ref[1:2, :], w_ref[2:3, :]
        conv = w0 * m2 + w1 * m1 + w2 * cu
        dco = dc_ref[...].astype(F32)
        sg = _sigmoid(gv)
        silu = gv * sg
        dgb = dco * conv * silu
        dco_gb = dco * gb
        dgv = dco_gb * conv * (sg * (1.0 + gv * (1.0 - sg)))
        dconv = dco_gb * silu
        nxt = nxt_ref[...].astype(F32)
        ngv = nxt[:, 3 * CONV_WIDTH:]
        dnext = dnxt_ref[...].astype(F32) * nxt[:, :CONV_WIDTH] * (ngv * _sigmoid(ngv))
        dnext = jnp.where(i == nsteps - 1, 0.0, dnext)
        n1, n2 = dnext[0:1], dnext[1:2]
        row = lax.broadcasted_iota(jnp.int32, dconv.shape, 0)
        p1 = jnp.where(row == CONV_ROWS - 1, n1, pltpu.roll(dconv, CONV_ROWS - 1, 0))
        p2 = jnp.where(row == CONV_ROWS - 1, n2, jnp.where(row == CONV_ROWS - 2, n1, pltpu.roll(dconv, CONV_ROWS - 2, 0)))
        dcu = w2 * dconv + w1 * p1 + w0 * p2
        dp_ref[...] = jnp.concatenate([dgb, dcu * u, dcu * gc, dgv], axis=1).astype(BF16)

        @pl.when(i == 0)
        def _():
            dw_ref[...] = jnp.zeros_like(dw_ref)

        taps = [jnp.sum(dconv * m, axis=0, keepdims=True) for m in (m2, m1, cu)]
        r8 = lax.broadcasted_iota(jnp.int32, dw_ref.shape, 0)
        dw_ref[...] += jnp.where(r8 == 0, taps[0], jnp.where(r8 == 1, taps[1], jnp.where(r8 == 2, taps[2], 0.0)))

    return pl.pallas_call(
        body, name="conv_bwd", grid=(nsteps,),
        in_specs=[pl.BlockSpec((CONV_ROWS, 4 * CONV_WIDTH), lambda i: (i, 1)),
                  pl.BlockSpec((HALO, 4 * CONV_WIDTH), lambda i: (jnp.maximum(i * per - 1, 0), 1)),
                  pl.BlockSpec((HALO, 4 * CONV_WIDTH), lambda i: (jnp.minimum((i + 1) * per, last_halo), 1)),
                  pl.BlockSpec((CONV_ROWS, CONV_WIDTH), lambda i: (i, 1)),
                  pl.BlockSpec((HALO, CONV_WIDTH), lambda i: (jnp.minimum((i + 1) * per, last_halo), 1)),
                  pl.BlockSpec((3, CONV_WIDTH), lambda i: (0, 0)), ANY],
        out_specs=[pl.BlockSpec((CONV_ROWS, 4 * CONV_WIDTH), lambda i: (i, 1)), pl.BlockSpec((8, CONV_WIDTH), lambda i: (0, 0))],
        out_shape=[jax.ShapeDtypeStruct(dproj.shape, dproj.dtype), jax.ShapeDtypeStruct((8, CONV_WIDTH), F32)],
        input_output_aliases={6: 0},
        compiler_params=_params("arbitrary"))(proj, proj, proj, dcat, dcat, conv_w, dproj)


GROUP_HEADS = Q_HEADS // KV_HEADS
GROUP_WIDTH = GROUP_HEADS * HEAD_DIM
SLAB = 2 * HEAD_DIM
KV_COL = ATTN_WIDTH // GROUP_WIDTH
GATE_COL = KV_COL + 1
ATTN_SCALE = HEAD_DIM ** -0.5
KV_ROWS = 1024


def _half_mask(shape, which):
    return (_lane(shape) // HEAD_DIM) == which


def _dup_head(slab, which):
    kept = jnp.where(_half_mask(slab.shape, which), slab, 0.0)
    return kept + pltpu.roll(kept, HEAD_DIM, 1)


def _stack_heads(x):
    parts = []
    for sl in range(GROUP_WIDTH // SLAB):
        slab = x[:, sl * SLAB:(sl + 1) * SLAB]
        parts += [jnp.where(_half_mask(slab.shape, e), slab, 0.0) for e in range(2)]
    return jnp.concatenate(parts, axis=0)


def _unstack_heads(y):
    slabs = []
    for sl in range(GROUP_WIDTH // SLAB):
        a, b = y[(2 * sl) * BLK:(2 * sl + 1) * BLK], y[(2 * sl + 1) * BLK:(2 * sl + 2) * BLK]
        slabs.append(jnp.where(_half_mask(a.shape, 0), a, b))
    return jnp.concatenate(slabs, axis=1)


def _q_prep(q, qw, cosf, sins, ones_bd):
    rstd = lax.rsqrt(_group_mean(q * q, ones_bd) + EPS)
    nrm = q * rstd
    y = nrm * qw
    return nrm, rstd, y * cosf + _partner64(y) * sins


def _band(tri_ref, n):
    own = tri_ref[...] > 0.5
    return own, jnp.where(jnp.logical_and(n == 0, jnp.logical_not(own)), -1e30, 0.0)


def _fold(pair, own):
    return jnp.where(own, pair[:, BLK:], pair[:, :BLK])


def _unfold(folded, own):
    return jnp.concatenate([jnp.where(own, 0.0, folded), jnp.where(own, folded, 0.0)], axis=1)


def _head_probs(raw_scores, sink, own, bias):
    sc = _fold(raw_scores, own) + bias
    m = jnp.maximum(jnp.max(sc, axis=-1, keepdims=True), sink)
    p = jnp.exp(sc - m)
    psink = jnp.exp(sink - m)
    inv = 1.0 / (jnp.sum(p, axis=-1, keepdims=True) + psink)
    return p * inv, psink * inv


def _k_prep(k, kw, cosf, sins, ones_bd):
    rstd = lax.rsqrt(_group_mean(k * k, ones_bd) + EPS)
    nrm = k * rstd
    y = nrm * kw
    return nrm, rstd, y * cosf + _partner64(y) * sins


def _qk_prep(proj, qw, kw, cos, sins, ones_q, ones_kv):
    s = proj.shape[0]
    rows = min(KV_ROWS, s)

    def body(p_ref, qw_ref, kw_ref, cos_ref, sin_ref, oq_ref, ok_ref, o_ref):
        j = pl.program_id(1)

        @pl.when(j < KV_COL)
        def _():
            cosf, sinf = _tile_lanes(cos_ref[...], 4), _tile_lanes(sin_ref[...], 4)
            roped = _q_prep(p_ref[...].astype(F32), qw_ref[...], cosf, sinf, oq_ref[...])[2]
            o_ref[...] = (roped * ATTN_SCALE).astype(BF16)

        @pl.when(j == KV_COL)
        def _():
            cosf, sinf = _tile_lanes(cos_ref[...], 2), _tile_lanes(sin_ref[...], 2)
            kr = _k_prep(p_ref[:, :KV_WIDTH].astype(F32), kw_ref[...], cosf, sinf, ok_ref[...])[2]
            o_ref[...] = jnp.concatenate([kr.astype(BF16), p_ref[:, KV_WIDTH:]], axis=1)

    full = lambda shape: pl.BlockSpec(shape, lambda i, j: (0,) * len(shape))
    tab = pl.BlockSpec((rows, SLAB), lambda i, j: (i, 0))
    blk = pl.BlockSpec((rows, GROUP_WIDTH), lambda i, j: (i, j))
    return pl.pallas_call(
        body, name="qk_prep", grid=(s // rows, KV_COL + 1),
        in_specs=[blk, full((1, GROUP_WIDTH)), full((1, KV_WIDTH)), tab, tab, full((GROUP_WIDTH, GROUP_WIDTH)),
                  full((KV_WIDTH, KV_WIDTH))],
        out_specs=blk, out_shape=jax.ShapeDtypeStruct((s, ATTN_WIDTH + 2 * KV_WIDTH), BF16),
        compiler_params=_params("parallel", "arbitrary"))(proj, qw, kw, cos, sins, ones_q, ones_kv)


def _keys_values(kc_ref, kp_ref, vc_ref, vp_ref, head):
    lanes = slice((head // 2) * SLAB, (head // 2 + 1) * SLAB)
    dup = lambda ref: _dup_head(ref[:, lanes].astype(F32), head % 2)
    return (jnp.concatenate([dup(kp_ref), dup(kc_ref)], axis=0).astype(BF16),
            jnp.concatenate([dup(vp_ref), dup(vc_ref)], axis=0).astype(BF16))


FWD_STEP_HEADS = 4
BWD_STEP_HEADS = 2


def _swa_specs(heads):
    kv_width = heads * HEAD_DIM
    prev = lambda n: jnp.maximum(n - 1, 0)
    kv = lambda col0, row: pl.BlockSpec((BLK, kv_width), lambda gs, n: (row(n), col0 // kv_width + gs))
    cur = lambda n: n
    full = lambda shape: pl.BlockSpec(shape, lambda gs, n: (0,) * len(shape))
    gate = lambda t: pl.BlockSpec((BLK, GROUP_WIDTH), lambda gs, n: (n, GATE_COL + heads * gs + t))
    return dict(
        sinks=pl.BlockSpec(memory_space=pltpu.SMEM), gates=[gate(t) for t in range(heads)],
        kc=kv(ATTN_WIDTH, cur), kp=kv(ATTN_WIDTH, prev), vc=kv(ATTN_WIDTH + KV_WIDTH, cur), vp=kv(ATTN_WIDTH + KV_WIDTH, prev),
        tri=full((BLK, BLK)), step=pl.BlockSpec((BLK, heads * GROUP_WIDTH), lambda gs, n: (n, gs)))


def _lower_triangle():
    return jnp.tril(jnp.ones((BLK, BLK), F32))


def _swa_fwd(qk, proj, sinks):
    s = proj.shape[0]
    nb = s // BLK
    heads = FWD_STEP_HEADS
    sp = _swa_specs(heads)

    def body(sink_ref, q_ref, kc_ref, kp_ref, vc_ref, vp_ref, *rest):
        gate_refs, (tri_ref, ag_ref, o_ref) = rest[:heads], rest[heads:]
        gs, n = pl.program_id(0), pl.program_id(1)
        own, bias = _band(tri_ref, n)
        for t, gate_ref in enumerate(gate_refs):
            cols = slice(t * GROUP_WIDTH, (t + 1) * GROUP_WIDTH)
            first_head = (heads * gs + t) * GROUP_HEADS
            kcat, vcat = _keys_values(kc_ref, kp_ref, vc_ref, vp_ref, t)
            scores = _dot_nt(_stack_heads(q_ref[:, cols]), kcat)
            probs = []
            for j in range(GROUP_HEADS):
                p, _ = _head_probs(scores[j * BLK:(j + 1) * BLK], sink_ref[first_head + j], own, bias)
                probs.append(_unfold(p, own).astype(BF16))
            o = _unstack_heads(_dot(jnp.concatenate(probs, axis=0), vcat))
            gate = gate_ref[...].astype(F32)
            o_ref[:, cols] = o.astype(BF16)
            ag_ref[:, cols] = (o * (gate * _sigmoid(gate))).astype(BF16)

    shp = jax.ShapeDtypeStruct((s, ATTN_WIDTH), BF16)
    return pl.pallas_call(
        body, name="swa_fwd", grid=(KV_HEADS // heads, nb),
        in_specs=[sp["sinks"], sp["step"], sp["kc"], sp["kp"], sp["vc"], sp["vp"], *sp["gates"], sp["tri"]],
        out_specs=[sp["step"], sp["step"]], out_shape=[shp, shp],
        compiler_params=_params("parallel", "arbitrary"),
    )(sinks, qk, qk, qk, qk, qk, *[proj] * heads, _lower_triangle())


def _swa_bwd(qk, proj, dag, sinks):
    s = proj.shape[0]
    nb = s // BLK
    heads = BWD_STEP_HEADS
    sp = _swa_specs(heads)

    def body(sink_ref, q_ref, kc_ref, kp_ref, vc_ref, vp_ref, *rest):
        gate_refs = rest[:heads]
        dag_ref, tri_ref, dq_ref, dkc_ref, dkp_ref, dvc_ref, dvp_ref, dsink_ref = rest[heads:]
        gs, n = pl.program_id(0), pl.program_id(1)
        own, bias = _band(tri_ref, n)

        @pl.when(n == 0)
        def _():
            dsink_ref[...] = jnp.zeros_like(dsink_ref)

        for t, gate_ref in enumerate(gate_refs):
            cols = slice(t * GROUP_WIDTH, (t + 1) * GROUP_WIDTH)
            first_head = (heads * gs + t) * GROUP_HEADS
            kcat, vcat = _keys_values(kc_ref, kp_ref, vc_ref, vp_ref, t)
            gate = gate_ref[...].astype(F32)
            do = dag_ref[:, cols].astype(F32) * (gate * _sigmoid(gate))
            q_stack = _stack_heads(q_ref[:, cols])
            do_stack = _stack_heads(do).astype(BF16)
            scores = _dot_nt(q_stack, kcat)
            dprobs = _dot_nt(do_stack, vcat)
            probs, dscores, dsinks = [], [], []
            for j in range(GROUP_HEADS):
                rows = slice(j * BLK, (j + 1) * BLK)
                p, psink = _head_probs(scores[rows], sink_ref[first_head + j], own, bias)
                dp = _fold(dprobs[rows], own)
                delta = jnp.sum(p * dp, axis=-1, keepdims=True)
                probs.append(_unfold(p, own).astype(BF16))
                dscores.append(_unfold(p * (dp - delta), own).astype(BF16))
                dsinks.append(-jnp.sum(psink * delta, axis=0, keepdims=True))
            ds = jnp.concatenate(dscores, axis=0)
            dk = _dot_tn(ds, q_stack)
            dv = _dot_tn(jnp.concatenate(probs, axis=0), do_stack)
            dk = dk + pltpu.roll(dk, HEAD_DIM, 1)
            dv = dv + pltpu.roll(dv, HEAD_DIM, 1)
            dkp_ref[t], dkc_ref[t] = dk[:BLK], dk[BLK:]
            dvp_ref[t], dvc_ref[t] = dv[:BLK], dv[BLK:]
            dq_ref[:, cols] = _unstack_heads(_dot(ds, kcat)).astype(BF16)
            r8 = lax.broadcasted_iota(jnp.int32, (8, SLAB), 0)
            upd = jnp.zeros((8, SLAB), F32)
            for j in range(GROUP_HEADS):
                upd = jnp.where(r8 == j, dsinks[j], upd)
            dsink_ref[t] += upd

    cur_out = pl.BlockSpec((heads, BLK, SLAB), lambda gs, n: (gs, n, 0))
    prev_out = pl.BlockSpec((heads, BLK, SLAB), lambda gs, n: (gs, (n + nb - 1) % nb, 0))
    kv_shape = jax.ShapeDtypeStruct((KV_HEADS, s, SLAB), F32)
    return pl.pallas_call(
        body, name="swa_bwd", grid=(KV_HEADS // heads, nb),
        in_specs=[sp["sinks"], sp["step"], sp["kc"], sp["kp"], sp["vc"], sp["vp"], *sp["gates"], sp["step"], sp["tri"]],
        out_specs=[sp["step"], cur_out, prev_out, cur_out, prev_out,
                   pl.BlockSpec((heads, 8, SLAB), lambda gs, n: (gs, 0, 0))],
        out_shape=[jax.ShapeDtypeStruct((s, ATTN_WIDTH), BF16), kv_shape, kv_shape, kv_shape, kv_shape,
                   jax.ShapeDtypeStruct((KV_HEADS, 8, SLAB), F32)],
        compiler_params=_params("parallel", "arbitrary"),
    )(sinks, qk, qk, qk, qk, qk, *[proj] * heads, dag, _lower_triangle())


def _swa_bwd_finish(proj, dqr, o, dag, dkc, dkp, dvc, dvp, qw, kw, cos, sins, ones_q, ones_kv):
    s = proj.shape[0]
    rows = min(KV_ROWS, s)
    n_q = KV_COL
    q_of = lambda j: jnp.clip(j - 1, 0, n_q - 1)
    gate_of = lambda j: jnp.clip(j - 1 - n_q, 0, n_q - 1)

    def body(kv_ref, p_ref, dqr_ref, o_ref, dag_ref, dkc_ref, dkp_ref, dvc_ref, dvp_ref, qw_ref, kw_ref, cos_ref, sin_ref,
             oq_ref, ok_ref, dp_ref, dqw_ref, dkw_ref):
        j, i = pl.program_id(0), pl.program_id(1)

        @pl.when(j == 0)
        def _():
            @pl.when(i == 0)
            def _():
                dkw_ref[...] = jnp.zeros_like(dkw_ref)

            def assemble(cur_ref, prv_ref):
                tot = [cur_ref[h] + prv_ref[h] for h in range(KV_HEADS)]
                first = _half_mask(tot[0].shape, 0)
                return jnp.concatenate([jnp.where(first, tot[0], tot[1]), jnp.where(first, tot[2], tot[3])], axis=1)

            dkr = assemble(dkc_ref, dkp_ref)
            dv = assemble(dvc_ref, dvp_ref)
            cosf, sinf = _tile_lanes(cos_ref[...], 2), _tile_lanes(sin_ref[...], 2)
            nrm, rstd, _ = _k_prep(kv_ref[:, :KV_WIDTH].astype(F32), kw_ref[...], cosf, sinf, ok_ref[...])
            dy = dkr * cosf + _partner64(dkr * sinf)
            dn = dy * kw_ref[...]
            dk = rstd * (dn - nrm * _group_mean(dn * nrm, ok_ref[...], passes=1))
            dp_ref[...] = jnp.concatenate([dk, dv], axis=1).astype(BF16)
            dkw_ref[...] += jnp.sum(dy * nrm, axis=0, keepdims=True)

        @pl.when(jnp.logical_and(j >= 1, j <= n_q))
        def _():
            @pl.when(i == 0)
            def _():
                dqw_ref[...] = jnp.zeros_like(dqw_ref)

            cosf, sinf = _tile_lanes(cos_ref[...], 4), _tile_lanes(sin_ref[...], 4)
            nrm, rstd, _ = _q_prep(p_ref[...].astype(F32), qw_ref[...], cosf, sinf, oq_ref[...])
            dq = dqr_ref[...].astype(F32) * ATTN_SCALE
            dy = dq * cosf + _partner64(dq * sinf)
            dn = dy * qw_ref[...]
            dp_ref[...] = (rstd * (dn - nrm * _group_mean(dn * nrm, oq_ref[...], passes=1))).astype(BF16)
            dqw_ref[0] += jnp.sum(dy * nrm, axis=0, keepdims=True)

        @pl.when(j > n_q)
        def _():
            gate = p_ref[...].astype(F32)
            sg = _sigmoid(gate)
            dp_ref[...] = (dag_ref[...].astype(F32) * o_ref[...].astype(F32) * (sg * (1.0 + gate * (1.0 - sg)))).astype(BF16)

    first_pass = lambda j, i: jnp.where(j == 0, i, 0)
    acc = pl.BlockSpec((KV_HEADS, rows, SLAB), lambda j, i: (0, first_pass(j, i), 0))
    full = lambda shape: pl.BlockSpec(shape, lambda j, i: (0,) * len(shape))
    tab = pl.BlockSpec((rows, SLAB), lambda j, i: (i, 0))
    out_col = lambda j: jnp.where(j == 0, KV_COL, jnp.where(j <= n_q, j - 1, j))
    return pl.pallas_call(
        body, name="swa_bwd_finish", grid=(2 * n_q + 1, s // rows),
        in_specs=[pl.BlockSpec((rows, GROUP_WIDTH), lambda j, i: (first_pass(j, i), KV_COL)),
                  pl.BlockSpec((rows, GROUP_WIDTH), lambda j, i: (jnp.where(j == 0, 0, i), jnp.where(j <= n_q, q_of(j), j))),
                  pl.BlockSpec((rows, GROUP_WIDTH), lambda j, i: (jnp.where(jnp.logical_and(j >= 1, j <= n_q), i, 0), q_of(j))),
                  pl.BlockSpec((rows, GROUP_WIDTH), lambda j, i: (jnp.where(j > n_q, i, 0), gate_of(j))),
                  pl.BlockSpec((rows, GROUP_WIDTH), lambda j, i: (jnp.where(j > n_q, i, 0), gate_of(j))),
                  acc, acc, acc, acc, full((1, GROUP_WIDTH)), full((1, KV_WIDTH)), tab, tab,
                  full((GROUP_WIDTH, GROUP_WIDTH)), full((KV_WIDTH, KV_WIDTH))],
        out_specs=[pl.BlockSpec((rows, GROUP_WIDTH), lambda j, i: (i, out_col(j))),
                   pl.BlockSpec((1, 8, GROUP_WIDTH), lambda j, i: (q_of(j), 0, 0)), pl.BlockSpec((8, KV_WIDTH), lambda j, i: (0, 0))],
        out_shape=[jax.ShapeDtypeStruct((s, ODD_IN), BF16), jax.ShapeDtypeStruct((n_q, 8, GROUP_WIDTH), F32),
                   jax.ShapeDtypeStruct((8, KV_WIDTH), F32)],
        compiler_params=_params("arbitrary", "arbitrary"),
    )(proj, proj, dqr, o, dag, dkc, dkp, dvc, dvp, qw, kw, cos, sins, ones_q, ones_kv)


def _place():
    x, y, c = lax.axis_index("x"), lax.axis_index("y"), lax.axis_index("c")
    return x, y, c


OTHER_CHIPS = ((1, 0), (0, 1), (1, 1))


def _half_rows(ref, half, rows):
    return ref.at[pl.ds(pl.multiple_of(half * (rows // 2), 8), rows // 2)]


DMA_CHUNK_BYTES = 1 << 20
BF16_TILE_ROWS = 16


def _n_chunks(ref):
    rows = ref.shape[-2]
    nbytes = math.prod(ref.shape) * jnp.dtype(ref.dtype).itemsize
    n = 1
    while 2 * n * DMA_CHUNK_BYTES <= nbytes and rows % (2 * n * BF16_TILE_ROWS) == 0:
        n *= 2
    return n


def _row_chunk(ref, k, n):
    rows = ref.shape[-2] // n
    return ref.at[pl.ds(k * rows, rows)] if len(ref.shape) == 2 else ref.at[:, pl.ds(k * rows, rows)]


def _push(src, dst, send_sem, recv_sem, device_id):
    n = _n_chunks(src)
    for k in range(n):
        pltpu.make_async_remote_copy(src_ref=_row_chunk(src, k, n), dst_ref=_row_chunk(dst, k, n), send_sem=send_sem,
                                     recv_sem=recv_sem, device_id=device_id, device_id_type=MESH).start()
    return pltpu.make_async_remote_copy(src_ref=src, dst_ref=dst, send_sem=send_sem, recv_sem=recv_sem,
                                        device_id=device_id, device_id_type=MESH)


HBM = pl.BlockSpec(memory_space=pltpu.HBM)
SEM = pl.BlockSpec(memory_space=pltpu.SEMAPHORE)
SPLIT_COPY_EFFECT = pltpu.SideEffectType.DATAFLOW_SIDE_EFFECTING


def _in_hbm(a):
    return pltpu.with_memory_space_constraint(a, pltpu.HBM)


def _start_copies(name, arrays, plan, n_copies, after=None):
    n = len(arrays)

    def body(*refs):
        send_sem, recv_sem = refs[-n - 3], refs[-n - 2]
        for k, (src, dst, peer) in enumerate(plan(refs[:n])):
            _push(src, dst, send_sem.at[k], recv_sem.at[k], peer)
        refs[-1][...] = jnp.zeros_like(refs[-1])

    dma = pltpu.SemaphoreType.DMA((n_copies,))
    outs = pl.pallas_call(
        body, name=name,
        out_shape=(dma, dma, *[pltpu.HBM(a.shape, a.dtype) for a in arrays], jax.ShapeDtypeStruct((8, 128), F32)),
        in_specs=[HBM] * n + ([ANY] if after is not None else []),
        out_specs=(SEM, SEM, *[HBM] * n, pl.BlockSpec(memory_space=pltpu.VMEM)),
        input_output_aliases={i: i + 2 for i in range(n)},
        compiler_params=pltpu.CompilerParams(has_side_effects=SPLIT_COPY_EFFECT),
    )(*[_in_hbm(a) for a in arrays], *((after,) if after is not None else ()))
    return outs[0], outs[1], list(outs[2:2 + n]), outs[-1]


def _wait_copies(name, send_sem, recv_sem, arrays, plan, after):
    n = len(arrays)
    after = list(after) if isinstance(after, (list, tuple)) else [after]

    def body(*refs):
        send_ref, recv_ref = refs[n], refs[n + 1]
        for k, (src, dst, peer) in enumerate(plan(refs[:n])):
            cp = pltpu.make_async_remote_copy(src_ref=src, dst_ref=dst, send_sem=send_ref.at[k], recv_sem=recv_ref.at[k],
                                              device_id=peer, device_id_type=MESH)
            cp.wait_send()
            cp.wait_recv()

    return list(pl.pallas_call(
        body, name=name, out_shape=tuple(pltpu.HBM(a.shape, a.dtype) for a in arrays),
        in_specs=[HBM] * n + [SEM, SEM] + [ANY] * len(after), out_specs=tuple([HBM] * n),
        input_output_aliases={i: i for i in range(n)},
        compiler_params=pltpu.CompilerParams(has_side_effects=SPLIT_COPY_EFFECT),
    )(*arrays, send_sem, recv_sem, *after))


def _gather_region(full, kind, chip, half=None):
    if kind == "whole":
        return full.at[chip]
    if kind == "col" or isinstance(kind, tuple):
        part, parts = (0, 1) if kind == "col" else kind[1:]
        rows, width = full.shape[0], full.shape[1] // N_CHIPS
        piece = full.at[:, pl.ds(pl.multiple_of(chip * width + part * (width // parts), LANE_TILE), width // parts)]
    else:
        rows = full.shape[0] // N_CHIPS
        piece = full.at[pl.ds(pl.multiple_of(chip * rows, BF16_TILE_ROWS), rows)]
    return piece if half is None else _half_rows(piece, half, rows)


def _gather_plan(kinds):
    def plan(fulls):
        x, y, c = _place()
        copies = []
        for fx, fy in OTHER_CHIPS:
            for full, kind in zip(fulls, kinds):
                mine = _gather_region(full, kind, 2 * x + y, c)
                copies.append((mine, mine, (x ^ fx, y ^ fy, c)))
        return copies

    return plan


def _gather_start(name, fulls, kinds, after=None):
    return _start_copies(name, list(fulls), _gather_plan(kinds), 3 * len(kinds), after)


def _pass_on_plan(kinds):
    split = [i for i, kind in enumerate(kinds) if kind != "whole"]

    def plan(fulls):
        x, y, c = _place()
        copies = []
        for fx, fy in OTHER_CHIPS:
            chip = 2 * (x ^ fx) + (y ^ fy)
            for i in split:
                landed = _gather_region(fulls[i], kinds[i], chip, c)
                copies.append((landed, landed, (x, y, 1 - c)))
        return copies

    return plan, 3 * len(split)


def _gather_arrive(name, started, kinds, after, arrays=None):
    send_sem, recv_sem, fulls, _ = started
    fulls = _wait_copies(name + "_wait", send_sem, recv_sem, fulls if arrays is None else arrays, _gather_plan(kinds), after)
    plan, count = _pass_on_plan(kinds)
    return _start_copies(name + "_pass_on", fulls, plan, count)


def _gather_done(name, passed_on, kinds, after):
    send_sem, recv_sem, fulls, _ = passed_on
    return _wait_copies(name + "_pass_on_wait", send_sem, recv_sem, fulls, _pass_on_plan(kinds)[0], after)


def _allreduce_small(v):
    def body(v_ref, out_ref, buf, send_sems, recv_sems):
        x, y, c = _place()
        me = 4 * x + 2 * y + c
        buf[me] = v_ref[...]
        copies = []
        for r in range(1, N_DEV):
            peer = (x ^ (r >> 2), y ^ ((r >> 1) & 1), c ^ (r & 1))
            cp = pltpu.make_async_remote_copy(src_ref=v_ref, dst_ref=buf.at[me], send_sem=send_sems.at[r - 1],
                                              recv_sem=recv_sems.at[r - 1], device_id=peer, device_id_type=MESH)
            cp.start()
            copies.append(cp)
        for cp in copies:
            cp.wait_recv()
        for cp in copies:
            cp.wait_send()
        total = buf[0]
        for d in range(1, N_DEV):
            total = total + buf[d]
        out_ref[...] = total

    vm = pl.BlockSpec(memory_space=pltpu.VMEM)
    return pl.pallas_call(
        body, name="allreduce_small", in_specs=[vm], out_specs=vm, out_shape=jax.ShapeDtypeStruct(v.shape, v.dtype),
        scratch_shapes=[pltpu.VMEM((N_DEV,) + v.shape, v.dtype), pltpu.SemaphoreType.DMA((N_DEV - 1,)),
                        pltpu.SemaphoreType.DMA((N_DEV - 1,))],
        compiler_params=pltpu.CompilerParams(has_side_effects=True),
    )(v)


def _exchange_plan(n):
    def plan(refs):
        x, y, c = _place()
        copies = []
        for g, theirs in zip(refs[:n], refs[n:]):
            half = g.shape[1] // 2
            src = g.at[:, pl.ds(pl.multiple_of((1 - c) * half, BF16_TILE_ROWS), half)]
            copies.append((src, theirs, (x, y, 1 - c)))
        return copies

    return plan


def _pair_sum(g, theirs, core, name):
    pieces, half, cols = theirs.shape
    tr = min(half, 256)
    per = half // tr

    def body(core_ref, g_ref, t_ref, o_ref):
        del core_ref
        o_ref[...] = (g_ref[...].astype(F32) + t_ref[...].astype(F32)).astype(BF16)

    spec = pl.BlockSpec((1, tr, cols), lambda p, i, core_ref: (p, i, 0))
    return pl.pallas_call(
        body, name=name, out_shape=jax.ShapeDtypeStruct(theirs.shape, BF16),
        grid_spec=pltpu.PrefetchScalarGridSpec(
            num_scalar_prefetch=1, grid=(pieces, per),
            in_specs=[pl.BlockSpec((1, tr, cols), lambda p, i, core_ref: (p, core_ref[0] * per + i, 0)), spec],
            out_specs=spec),
        compiler_params=_params("parallel", "parallel"))(core, g, theirs)


def _scatter_plan(n):
    def plan(refs):
        parts, stacks = refs[:n], refs[n:]
        x, y, c = _place()
        copies = []
        for fx, fy in OTHER_CHIPS:
            chip = 2 * (x ^ fx) + (y ^ fy)
            for part, stack in zip(parts, stacks):
                if part.shape[0] == N_CHIPS:
                    piece = part.at[chip]
                else:
                    width = part.shape[2] // N_CHIPS
                    piece = part.at[0].at[:, pl.ds(pl.multiple_of(chip * width, 128), width)]
                copies.append((piece, stack.at[2 * x + y], (x ^ fx, y ^ fy, c)))
        return copies

    return plan


def _scatter_start(name, parts, after=None):
    def landing(a):
        return (N_CHIPS, a.shape[1], a.shape[2] if a.shape[0] == N_CHIPS else a.shape[2] // N_CHIPS)

    stacks = [lax.empty(landing(a), a.dtype) for a in parts]
    return _start_copies(name, list(parts) + stacks, _scatter_plan(len(parts)), 3 * len(parts), after)


def _scatter_finish(name, started, after):
    send_sem, recv_sem, arrays, _ = started
    n = len(arrays) // 2
    arrays = _wait_copies(name + "_wait", send_sem, recv_sem, arrays, _scatter_plan(n), after)
    return arrays[:n], arrays[n:]


def _sum_chips(part, stack, place, name):
    _, r, c = stack.shape
    tr = 256
    per = r // tr

    def body(place_ref, own_ref, a_ref, b_ref, c_ref, o_ref):
        del place_ref
        total = own_ref[0].astype(F32)
        for ref in (a_ref, b_ref, c_ref):
            total = total + ref[0].astype(F32)
        o_ref[...] = total

    if part.shape[0] == N_CHIPS:
        own = pl.BlockSpec((1, tr, c), lambda i, pr: (pr[1], i, 0))
    else:
        own = pl.BlockSpec((1, tr, c), lambda i, pr: (0, i, pr[1]))
    other = lambda flip: pl.BlockSpec((1, tr, c), lambda i, pr: (pr[1] ^ flip, i, 0))
    return pl.pallas_call(
        body, name=name, out_shape=jax.ShapeDtypeStruct((2 * r, c), F32),
        grid_spec=pltpu.PrefetchScalarGridSpec(
            num_scalar_prefetch=1, grid=(per,), in_specs=[own, other(2), other(1), other(3)],
            out_specs=pl.BlockSpec((tr, c), lambda i, pr: (pr[0] * per + i, 0))),
        compiler_params=_params("parallel"))(place, part, stack, stack, stack)


def _own_half_plan(fulls):
    x, y, c = _place()
    return [(mine, mine, (x, y, 1 - c)) for mine in (_half_rows(full, c, full.shape[0]) for full in fulls)]


MM = dict(tm=2048, tn=1024, tk=2048)
MM_LONG_K = dict(tm=1024, tn=1024, tk=4096)


def _local_step(x, target, ev_norm_w, q_norm_w, k_norm_w, sinks, own_first, weights_first, late_arrived, weights_late, grads_out):
    s = x.shape[0]
    cos_r, sin_r = _rope_tables(s, RET_DIM)
    cos_a, sin_a = _rope_tables(s, HEAD_DIM)
    cos_a = np.tile(cos_a, (1, 4))
    sins_a = np.tile(np.concatenate([-sin_a, sin_a], axis=1), (1, 2))
    tables = _retention_tables()
    ones_q, ones_kv = _block_diag_mean(GROUP_WIDTH), _block_diag_mean(KV_WIDTH)
    qw_g = jnp.tile(q_norm_w, (1, GROUP_WIDTH // HEAD_DIM))
    kw_kv = jnp.tile(k_norm_w, (1, KV_WIDTH // HEAD_DIM))
    sinks1 = sinks.reshape(Q_HEADS)

    own_w_in0, own_block, start_token = own_first
    h0 = _rmsnorm(x, ev_norm_w, "norm0", after=start_token)
    shifted = dict(shift=own_block * (own_w_in0.shape[1] // MM["tn"]), total=EVEN_IN // MM["tn"], tm=MM["tm"], tn=MM["tn"],
                   out_dtype=BF16)
    own_blocks = own_w_in0.shape[1] // MM["tn"]
    proj0 = _mm_shifted(h0, own_w_in0, b_shifted=False, first=0, count=own_blocks, name="proj0_own", **shifted)
    for part in range(own_blocks):
        w_in0, small_weights, token = weights_first(part, proj0)
        if small_weights is not None:
            conv_w, od_norm_w = small_weights
        proj0 = _mm_shifted(h0, w_in0, b_shifted=True, first=own_blocks + part, stride=own_blocks,
                            count=shifted["total"] // own_blocks - 1, name=f"proj0_rest{part}", into=proj0, after=token, **shifted)
    o_ret, cat, states = _retention_fwd(proj0, cos_r, sin_r, tables, after=late_arrived(proj0))
    cat = _conv_fwd(proj0, conv_w, cat)
    w_out0, w_in1, w_out1 = weights_late(cat)

    def residual_and_norm(prod, x_ref, w_ref, x1_ref, h1_ref):
        x1v = x_ref[...] + prod
        x1_ref[...] = x1v
        rstd = lax.rsqrt(jnp.mean(x1v * x1v, axis=-1, keepdims=True) + EPS)
        h1_ref[...] = (x1v * rstd * w_ref[...]).astype(BF16)

    def residual_and_loss(prod, x1_ref, t_ref, dyb_ref, sq_ref):
        diff = (x1_ref[...] + prod) - t_ref[...]
        dyb_ref[...] = (diff * (1.0 / D_MODEL)).astype(BF16)

        @pl.when(pl.program_id(0) == 0)
        def _():
            sq_ref[...] = jnp.zeros_like(sq_ref)

        sq_ref[...] += jnp.sum(jnp.sum(diff * diff, axis=1, keepdims=True), axis=0, keepdims=True)

    act = lambda dt: jax.ShapeDtypeStruct((s, D_MODEL), dt)
    x1, h1 = _mm_rows(cat, w_out0, [x], [od_norm_w], [act(F32), act(BF16)], residual_and_norm, tm=min(s, 512), name="out0")
    proj1 = _mm(h1, w_in1, mode="nn", out_dtype=BF16, name="proj1", tm=2048, tn=1536, tk=2048)
    qk = _qk_prep(proj1, qw_g, kw_kv, cos_a, sins_a, ones_q, ones_kv)
    ag, o_att = _swa_fwd(qk, proj1, sinks1)
    dy_b, sq = _mm_rows(ag, w_out1, [x1, target], [], [act(BF16), jax.ShapeDtypeStruct((8, 128), F32)],
                        residual_and_loss, tm=min(s, 512), name="out1")

    g_w_out1 = _mm(ag, dy_b, mode="tn", out_dtype=BF16, name="g_w_out1", **MM_LONG_K)
    dag = _mm(dy_b, w_out1, mode="nt", out_dtype=BF16, name="d_ag", **MM)
    dqr, dkc, dkp, dvc, dvp, dsink = _swa_bwd(qk, proj1, dag, sinks1)
    dproj1, dqw, dkw = _swa_bwd_finish(proj1, dqr, o_att, dag, dkc, dkp, dvc, dvp, qw_g, kw_kv, cos_a, sins_a, ones_q, ones_kv)
    g_w_in1 = _mm(h1, dproj1, mode="tn", out_dtype=BF16, name="g_w_in1", tm=1024, tn=768, tk=4096)
    dh1 = _mm(dproj1, w_in1, mode="nt", out_dtype=BF16, name="d_h1", tm=1024, tn=1024, tk=ODD_IN)
    dx1_b, g_norm1 = _rmsnorm_bwd(x1, od_norm_w, dh1, dy_b, "norm1_bwd", BF16)

    g_w_out0 = _mm(cat, dx1_b, mode="tn", out_dtype=BF16, name="g_w_out0", **MM_LONG_K)
    token = grads_out.begin("upper", (("od_w_in", g_w_in1, "col"), ("od_w_out", g_w_out1, "row"), ("ev_w_out", g_w_out0, "row")))
    dcat = _mm(dx1_b, w_out0, mode="nt", out_dtype=BF16, name="d_cat", after=token, **MM)
    token = grads_out.send("upper", dcat)
    dproj0 = _retention_bwd(proj0, o_ret, states, dcat, cos_r, sin_r, tables)
    dproj0, g_conv = _conv_bwd(proj0, dcat, conv_w, dproj0)
    g_w_in0 = _mm(h0, dproj0, mode="tn", out_dtype=BF16, name="g_w_in0", after=token, **MM_LONG_K)
    token = grads_out.begin("in0", (("ev_w_in", g_w_in0, "col"),))
    half_blocks = s // (2 * MM_LONG_K["tm"])
    dh0 = _mm(dproj0, w_in0, mode="nt", out_dtype=BF16, name="d_h0_top", after=token, m_blocks=(0, half_blocks), **MM_LONG_K)
    dh0 = _mm(dproj0, w_in0, mode="nt", out_dtype=BF16, name="d_h0_bottom", after=grads_out.send("in0", dh0),
              m_blocks=(half_blocks, half_blocks), into=dh0, **MM_LONG_K)
    grad_x, g_norm0 = _rmsnorm_bwd(x, ev_norm_w, dh0, dx1_b, "norm0_bwd", F32)

    g_qw = dqw[:, 0, :].reshape(Q_HEADS, HEAD_DIM).sum(axis=0)
    g_kw = dkw[0].reshape(KV_HEADS, HEAD_DIM).sum(axis=0)
    g_sinks = dsink[:, :, 0].reshape(Q_HEADS)
    small = dict(ev_norm=g_norm0[0], od_norm=g_norm1[0], conv=g_conv[:3], qw=g_qw, kw=g_kw, sinks=g_sinks)
    return sq[0, 0], grad_x, small


LANE_TILE = 128


def _pack_small_grads(small, sq):
    pad = lambda v, n: jnp.pad(v, (0, n - v.shape[0]))
    tail = jnp.concatenate([pad(small["qw"], LANE_TILE), pad(small["kw"], LANE_TILE), small["sinks"]])
    rows = [small["ev_norm"], small["od_norm"]] + [small["conv"][t] for t in range(3)] + [tail, sq.reshape(1)]
    rows += [jnp.zeros((1,), F32)] * (8 - len(rows))
    return jnp.stack([pad(r, D_MODEL) for r in rows])


def _adamw_small(tot, chip, params):
    n = len(params)

    def body(chip_ref, tot_ref, *refs):
        ins, outs = refs[:3 * n], refs[3 * n:]
        c = chip_ref[0]

        def own(rows, width):
            blocks = [tot_ref[rows, k * width:(k + 1) * width] for k in range(N_CHIPS)]
            g = blocks[-1]
            for k in reversed(range(N_CHIPS - 1)):
                g = jnp.where(c == k, blocks[k], g)
            return g

        grads = [tot_ref[0:1, :], own(slice(2, 5), CONV_WIDTH // N_CHIPS), own(slice(1, 2), D_MODEL // N_CHIPS),
                 tot_ref[5:6, 0:HEAD_DIM], tot_ref[5:6, LANE_TILE:LANE_TILE + HEAD_DIM],
                 tot_ref[5:6, 2 * LANE_TILE:2 * LANE_TILE + Q_HEADS]]
        for i, g in enumerate(grads):
            w_ref, m_ref, v_ref = ins[3 * i:3 * i + 3]
            g_out, d_out, nm_out, nv_out = outs[4 * i:4 * i + 4]
            at = (0,) if len(w_ref.shape) == 3 else ()
            delta, nm, nv = _adamw_math(w_ref[at] if at else w_ref[...], g, m_ref[at] if at else m_ref[...],
                                        v_ref[at] if at else v_ref[...])
            for ref, val in ((g_out, g), (d_out, delta), (nm_out, nm), (nv_out, nv)):
                if at:
                    ref[0] = val
                else:
                    ref[...] = val

    vm = pl.BlockSpec(memory_space=pltpu.VMEM)
    flat = [a for p in params for a in p]
    outs = pl.pallas_call(
        body, name="adamw_small", in_specs=[pl.BlockSpec(memory_space=pltpu.SMEM), vm] + [vm] * len(flat),
        out_specs=[vm] * (4 * n), out_shape=[jax.ShapeDtypeStruct(p[0].shape, F32) for p in params for _ in range(4)],
    )(chip, tot, *flat)
    return [tuple(outs[4 * i:4 * i + 4]) for i in range(n)]


class _ReduceScatter:
    def __init__(self, place):
        self.place = place
        self.exchanging = {}
        self.started = []
        self.sharing = {}

    def begin(self, tag, grads):
        pieces = [g[None] if kind == "col" else g.reshape(N_CHIPS, g.shape[0] // N_CHIPS, g.shape[1]) for _, g, kind in grads]
        landing = [lax.empty((p.shape[0], p.shape[1] // 2, p.shape[2]), p.dtype) for p in pieces]
        started = _start_copies("exchange_" + tag, pieces + landing, _exchange_plan(len(pieces)), len(pieces))
        self.exchanging[tag] = ([nm for nm, _, _ in grads], started)
        return started[3]

    def send(self, tag, after):
        group, (send_sem, recv_sem, arrays, _) = self.exchanging[tag]
        n = len(group)
        arrays = _wait_copies("exchange_" + tag + "_wait", send_sem, recv_sem, arrays, _exchange_plan(n), after)
        parts = [_pair_sum(g, t, self.place[:1], "pair_sum_" + nm) for g, t, nm in zip(arrays[:n], arrays[n:], group)]
        started = _scatter_start("scatter_" + tag, parts)
        token = self.sum_up(started[3])
        self.started.append((tag, group, started))
        return started[3] if token is None else token

    def sum_up(self, after):
        token = None
        for tag, group, started in self.started:
            if tag not in self.sharing:
                parts, stacks = _scatter_finish("scatter_" + tag, started, after)
                sums = [_sum_chips(p, s, self.place, "chip_sum_" + nm) for p, s, nm in zip(parts, stacks, group)]
                self.sharing[tag] = (group, _start_copies("share_" + tag, sums, _own_half_plan, len(sums)))
                token = self.sharing[tag][1][3]
        return token

    def result(self, tag, after):
        group, (send_sem, recv_sem, sums, _) = self.sharing[tag]
        return dict(zip(group, _wait_copies("share_" + tag + "_wait", send_sem, recv_sem, sums, _own_half_plan, after)))


def kernel(x, ev_norm_w, ev_w_in, ev_conv_w, ev_w_out, od_norm_w, od_w_in, od_q_norm_w, od_k_norm_w, od_sinks, od_w_out, loss_target, m_ev_norm_w, m_ev_w_in, m_ev_conv_w, m_ev_w_out, m_od_norm_w, m_od_w_in, m_od_q_norm_w, m_od_k_norm_w, m_od_sinks, m_od_w_out, v_ev_norm_w, v_ev_w_in, v_ev_conv_w, v_ev_w_out, v_od_norm_w, v_od_w_in, v_od_q_norm_w, v_od_k_norm_w, v_od_sinks, v_od_w_out):
    my_chip = 2 * lax.axis_index("x") + lax.axis_index("y")
    place = jnp.stack([lax.axis_index("c"), my_chip]).astype(jnp.int32)
    shard_w = D_MODEL // N_CHIPS
    conv_shard = CONV_WIDTH // N_CHIPS

    small_in = jnp.zeros((8, shard_w), F32)
    small_in = small_in.at[0].set(od_norm_w[0]).at[1:4, :conv_shard].set(ev_conv_w[0])
    small_in = lax.dynamic_update_slice(jnp.zeros((N_CHIPS, 8, shard_w), F32), small_in[None], (my_chip, 0, 0))
    chip = place[1:]
    first_kinds, second_kinds, late_kinds = (("col", 0, 2), "whole"), (("col", 1, 2),), ("row", "col", "row")
    w_in0_own_place, w_in0_shard = _cast_into_gathered(ev_w_in[0], "col", chip, "cast_w_in0", keep_shard=True)
    first = _gather_start("gather_first", [w_in0_own_place, small_in], first_kinds)
    second = _gather_start("gather_second", [first[2][0]], second_kinds)
    late_own = [_cast_into_gathered(ev_w_out[0], "row", chip, "cast_w_out0"),
                _cast_into_gathered(od_w_in[0], "col", chip, "cast_w_in1"),
                _cast_into_gathered(od_w_out[0], "row", chip, "cast_w_out1")]
    late, w_in0_so_far = [], []

    def weights_first(part, after):
        if part == 0:
            passed_on = _gather_arrive("gather_first", first, first_kinds, [after] + late_own, arrays=[second[2][0], first[2][1]])
            w_in0, small_all = _gather_done("gather_first", passed_on, first_kinds, passed_on[3])
            w_in0_so_far.append(w_in0)
            late.append(_gather_start("gather_late", late_own, late_kinds, after=small_all))
            od_norm_full = small_all[:, 0, :].reshape(1, D_MODEL)
            conv_full = jnp.transpose(small_all[:, 1:4, :conv_shard], (1, 0, 2)).reshape(3, CONV_WIDTH)
            return w_in0, (conv_full, od_norm_full), late[0][3]
        passed_on = _gather_arrive("gather_second", second, second_kinds, after, arrays=w_in0_so_far)
        (w_in0,) = _gather_done("gather_second", passed_on, second_kinds, passed_on[3])
        return w_in0, None, None

    def late_arrived(after):
        late.append(_gather_arrive("gather_late", late[0], late_kinds, after))
        return late[1][3]

    def weights_late(after):
        return _gather_done("gather_late", late[1], late_kinds, after)

    reduce_scatter = _ReduceScatter(place)
    sq, grad_x, small = _local_step(x[0], loss_target[0], ev_norm_w, od_q_norm_w, od_k_norm_w, od_sinks,
                                    (w_in0_shard, chip, second[3]), weights_first, late_arrived, weights_late,
                                    reduce_scatter)

    token = reduce_scatter.sum_up(grad_x)
    upd = {}
    upper = reduce_scatter.result("upper", token)
    upd["od_w_in"] = _adamw(od_w_in[0], upper["od_w_in"], m_od_w_in[0], v_od_w_in[0], "adamw_od_w_in")
    upd["od_w_out"] = _adamw(od_w_out[0], upper["od_w_out"], m_od_w_out[0], v_od_w_out[0], "adamw_od_w_out")
    upd["ev_w_out"] = _adamw(ev_w_out[0], upper["ev_w_out"], m_ev_w_out[0], v_ev_w_out[0], "adamw_ev_w_out")
    g_ev_w_in = reduce_scatter.result("in0", [upd[nm][1] for nm in ("od_w_in", "od_w_out", "ev_w_out")])["ev_w_in"]
    upd["ev_w_in"] = _adamw(ev_w_in[0], g_ev_w_in, m_ev_w_in[0], v_ev_w_in[0], "adamw_ev_w_in")
    tot = _allreduce_small(_pack_small_grads(small, sq))
    loss = 0.5 * tot[6, 0] / D_MODEL

    smalls = (("ev_norm_w", ev_norm_w, m_ev_norm_w, v_ev_norm_w), ("ev_conv_w", ev_conv_w, m_ev_conv_w, v_ev_conv_w),
              ("od_norm_w", od_norm_w, m_od_norm_w, v_od_norm_w), ("od_q_norm_w", od_q_norm_w, m_od_q_norm_w, v_od_q_norm_w),
              ("od_k_norm_w", od_k_norm_w, m_od_k_norm_w, v_od_k_norm_w), ("od_sinks", od_sinks, m_od_sinks, v_od_sinks))
    for (nm, _, _, _), result in zip(smalls, _adamw_small(tot, chip, [p[1:] for p in smalls])):
        upd[nm] = result
    for nm in ("ev_w_in", "ev_w_out", "od_w_in", "od_w_out"):
        upd[nm] = tuple(u[None] for u in upd[nm])
    order = ("ev_norm_w", "ev_w_in", "ev_conv_w", "ev_w_out", "od_norm_w", "od_w_in", "od_q_norm_w", "od_k_norm_w", "od_sinks", "od_w_out")
    return (loss, grad_x[None], *[upd[nm][0] for nm in order], *[upd[nm][1] for nm in order],
            *[upd[nm][2] for nm in order], *[upd[nm][3] for nm in order])
```

```python
import math

import jax
import jax.numpy as jnp
import numpy as np
from jax import lax
from jax.experimental import pallas as pl
from jax.experimental.pallas import tpu as pltpu

F32 = jnp.float32
BF16 = jnp.bfloat16

D_MODEL = 2048
RET_HEADS = 4
RET_DIM = 256
RET_WIDTH = 1024
CONV_WIDTH = 1024
EVEN_IN = 8192
Q_HEADS = 32
HEAD_DIM = 64
KV_HEADS = 4
KV_WIDTH = 256
ATTN_WIDTH = 2048
ODD_IN = 4608
BLK = 128
ROPE_THETA = 10000.0
EPS = 1e-6
ADAM_LR = 0.001
ADAM_B1 = 0.9
ADAM_B2 = 0.999
ADAM_EPS = 1e-08
ADAM_WD = 0.01
ADAM_STEP = 10
N_CHIPS = 4
N_DEV = 8
VMEM_LIMIT_BYTES = 56 * 1024 * 1024
MESH = pl.DeviceIdType.MESH
ANY = pl.BlockSpec(memory_space=pl.ANY)


def _params(*sem):
    return pltpu.CompilerParams(dimension_semantics=sem, vmem_limit_bytes=VMEM_LIMIT_BYTES)


def _dot(a, b):
    return jnp.dot(a, b, preferred_element_type=F32)


def _dot_nt(a, b):
    return lax.dot_general(a, b, (((1,), (1,)), ((), ())), preferred_element_type=F32)


def _dot_tn(a, b):
    return lax.dot_general(a, b, (((0,), (0,)), ((), ())), preferred_element_type=F32)


def _sigmoid(x):
    return 1.0 / (1.0 + jnp.exp(-x))


def _mm(a, b, *, mode, tm, tn, tk, out_dtype, name, add=None, after=None, m_blocks=None, into=None):
    if mode == "nn":
        (m, k), n = a.shape, b.shape[1]
    elif mode == "nt":
        (m, k), n = a.shape, b.shape[0]
    else:
        (k, m), n = a.shape, b.shape[1]
    tm, tn, tk = min(tm, m), min(tn, n), min(tk, k)
    assert m % tm == 0 and n % tn == 0 and k % tk == 0, (name, m, n, k)
    nk = k // tk
    first_m, count_m = (0, m // tm) if m_blocks is None else m_blocks
    dot = {"nn": _dot, "nt": _dot_nt, "tn": _dot_tn}[mode]
    a_spec = (pl.BlockSpec((tk, tm), lambda i, j, kk: (kk, i + first_m)) if mode == "tn"
              else pl.BlockSpec((tm, tk), lambda i, j, kk: (i + first_m, kk)))
    b_spec = (pl.BlockSpec((tn, tk), lambda i, j, kk: (j, kk)) if mode == "nt"
              else pl.BlockSpec((tk, tn), lambda i, j, kk: (kk, j)))
    o_spec = pl.BlockSpec((tm, tn), lambda i, j, kk: (i + first_m, j))
    has_add = add is not None

    def body(*refs):
        a_ref, b_ref = refs[0], refs[1]
        add_ref = refs[2] if has_add else None
        o_ref, acc_ref = refs[-2], refs[-1]
        p = dot(a_ref[...], b_ref[...])

        def finish(total):
            if has_add:
                total = total + add_ref[...].astype(F32)
            o_ref[...] = total.astype(out_dtype)

        if nk == 1:
            finish(p)
        else:
            kk = pl.program_id(2)

            @pl.when(kk == 0)
            def _():
                acc_ref[...] = p

            @pl.when(jnp.logical_and(kk > 0, kk < nk - 1))
            def _():
                acc_ref[...] += p

            @pl.when(kk == nk - 1)
            def _():
                finish(acc_ref[...] + p)

    extra = [arr for arr in (after, into) if arr is not None]
    in_specs = [a_spec, b_spec] + ([o_spec] if has_add else []) + [ANY] * len(extra)
    args = (a, b) + ((add,) if has_add else ()) + tuple(extra)
    return pl.pallas_call(
        body, name=name, grid=(count_m, n // tn, nk), in_specs=in_specs, out_specs=o_spec,
        out_shape=jax.ShapeDtypeStruct((m, n), out_dtype),
        input_output_aliases={len(args) - 1: 0} if into is not None else {},
        scratch_shapes=[pltpu.VMEM((tm, tn) if nk > 1 else (8, 128), F32)],
        compiler_params=_params("parallel", "parallel", "arbitrary"),
    )(*args)


def _mm_shifted(a, b, shift, *, b_shifted, first, count, total, tm, tn, out_dtype, name, stride=1, into=None, after=None):
    m, k = a.shape
    tm = min(tm, m)
    assert m % tm == 0
    col = lambda j, shift_ref: (shift_ref[0] + first + stride * j) % total
    extra = [arr for arr in (into, after) if arr is not None]

    def body(shift_ref, a_ref, b_ref, *rest):
        del shift_ref
        rest[-1][...] = _dot(a_ref[...], b_ref[...]).astype(out_dtype)

    return pl.pallas_call(
        body, name=name, out_shape=jax.ShapeDtypeStruct((m, total * tn), out_dtype),
        grid_spec=pltpu.PrefetchScalarGridSpec(
            num_scalar_prefetch=1, grid=(m // tm, count),
            in_specs=[pl.BlockSpec((tm, k), lambda i, j, s: (i, 0)),
                      pl.BlockSpec((k, tn), (lambda i, j, s: (0, col(j, s))) if b_shifted else (lambda i, j, s: (0, j)))]
            + [ANY] * len(extra),
            out_specs=pl.BlockSpec((tm, tn), lambda i, j, s: (i, col(j, s)))),
        input_output_aliases={3: 0} if into is not None else {},
        compiler_params=_params("parallel", "arbitrary"))(shift, a, b, *extra)


def _mm_rows(a, b, rows_in, vecs_in, out_shapes, epilogue, *, tm, name):
    m, k = a.shape
    n = b.shape[1]
    assert m % tm == 0
    row = pl.BlockSpec((tm, n), lambda i: (i, 0))

    def body(a_ref, b_ref, *rest):
        epilogue(_dot(a_ref[...], b_ref[...]), *rest)

    out_specs = [row if tuple(s.shape) == (m, n) else pl.BlockSpec(s.shape, lambda i: (0, 0)) for s in out_shapes]
    return pl.pallas_call(
        body, name=name, grid=(m // tm,),
        in_specs=[pl.BlockSpec((tm, k), lambda i: (i, 0)), pl.BlockSpec((k, n), lambda i: (0, 0))] + [row] * len(rows_in)
        + [pl.BlockSpec((1, n), lambda i: (0, 0))] * len(vecs_in),
        out_specs=out_specs, out_shape=out_shapes, compiler_params=_params("arbitrary"),
    )(a, b, *rows_in, *vecs_in)


def _cast_into_gathered(w, kind, chip, name, keep_shard=False):
    r, c = w.shape
    tr = min(r, 512)
    per = r // tr

    def body(chip_ref, w_ref, *outs):
        del chip_ref
        for o_ref in outs:
            o_ref[...] = w_ref[...].astype(BF16)

    if kind == "col":
        shape, out_map = (r, N_CHIPS * c), (lambda i, chip_ref: (i, chip_ref[0]))
    else:
        shape, out_map = (N_CHIPS * r, c), (lambda i, chip_ref: (chip_ref[0] * per + i, 0))
    plain = pl.BlockSpec((tr, c), lambda i, chip_ref: (i, 0))
    out = pl.pallas_call(
        body, name=name,
        out_shape=[jax.ShapeDtypeStruct(shape, BF16)] + ([jax.ShapeDtypeStruct((r, c), BF16)] if keep_shard else []),
        grid_spec=pltpu.PrefetchScalarGridSpec(
            num_scalar_prefetch=1, grid=(per,), in_specs=[plain],
            out_specs=[pl.BlockSpec((tr, c), out_map)] + ([plain] if keep_shard else [])),
        compiler_params=_params("parallel"))(chip, w)
    return out if keep_shard else out[0]


NORM_ROWS = 512


def _rmsnorm(x, w, name, after=None):
    s, d = x.shape
    tr = NORM_ROWS

    def body(x_ref, w_ref, *rest):
        xv = x_ref[...]
        rstd = lax.rsqrt(jnp.mean(xv * xv, axis=-1, keepdims=True) + EPS)
        rest[-1][...] = (xv * rstd * w_ref[...]).astype(BF16)

    return pl.pallas_call(
        body, name=name, grid=(s // tr,),
        in_specs=[pl.BlockSpec((tr, d), lambda i: (i, 0)), pl.BlockSpec((1, d), lambda i: (0, 0))]
        + ([ANY] if after is not None else []),
        out_specs=pl.BlockSpec((tr, d), lambda i: (i, 0)),
        out_shape=jax.ShapeDtypeStruct((s, d), BF16), compiler_params=_params("parallel"),
    )(x, w, *((after,) if after is not None else ()))


def _rmsnorm_bwd(x, w, dh, dres, name, out_dtype):
    s, d = x.shape
    tr = NORM_ROWS

    def body(x_ref, w_ref, dh_ref, dres_ref, dx_ref, dw_ref):
        xv = x_ref[...]
        rstd = lax.rsqrt(jnp.mean(xv * xv, axis=-1, keepdims=True) + EPS)
        nrm = xv * rstd
        dhv = dh_ref[...].astype(F32)
        dn = dhv * w_ref[...]
        dx = dres_ref[...].astype(F32) + rstd * (dn - nrm * jnp.mean(dn * nrm, axis=-1, keepdims=True))
        dx_ref[...] = dx.astype(out_dtype)

        @pl.when(pl.program_id(0) == 0)
        def _():
            dw_ref[...] = jnp.zeros_like(dw_ref)

        dw_ref[...] += jnp.sum(dhv * nrm, axis=0, keepdims=True)

    row = pl.BlockSpec((tr, d), lambda i: (i, 0))
    return pl.pallas_call(
        body, name=name, grid=(s // tr,),
        in_specs=[row, pl.BlockSpec((1, d), lambda i: (0, 0)), row, row],
        out_specs=[row, pl.BlockSpec((8, d), lambda i: (0, 0))],
        out_shape=[jax.ShapeDtypeStruct((s, d), out_dtype), jax.ShapeDtypeStruct((8, d), F32)],
        compiler_params=_params("arbitrary"))(x, w, dh, dres)


def _adamw_math(w, g, m, v):
    nm = ADAM_B1 * m + (1.0 - ADAM_B1) * g
    nv = ADAM_B2 * v + (1.0 - ADAM_B2) * (g * g)
    m_hat = nm / (1.0 - ADAM_B1 ** ADAM_STEP)
    v_hat = nv / (1.0 - ADAM_B2 ** ADAM_STEP)
    return -ADAM_LR * (m_hat / (jnp.sqrt(v_hat) + ADAM_EPS) + ADAM_WD * w), nm, nv


def _adamw(w, g, m, v, name):
    r, c = w.shape
    tr = min(r, 256)
    assert r % tr == 0

    def body(w_ref, g_ref, m_ref, v_ref, g_out, d_ref, nm_ref, nv_ref):
        gv = g_ref[...]
        g_out[...] = gv
        d_ref[...], nm_ref[...], nv_ref[...] = _adamw_math(w_ref[...], gv, m_ref[...], v_ref[...])

    spec = pl.BlockSpec((tr, c), lambda i: (i, 0))
    shp = jax.ShapeDtypeStruct((r, c), F32)
    return pl.pallas_call(body, name=name, grid=(r // tr,), in_specs=[spec] * 4, out_specs=[spec] * 4,
                          out_shape=[shp] * 4, compiler_params=_params("parallel"))(w, g, m, v)


def _rope_tables(s, dim):
    inv = (1.0 / (ROPE_THETA ** (np.arange(0, dim, 2, dtype=np.float64) / dim))).astype(np.float32)
    ang = (np.arange(s, dtype=np.float32)[:, None] * inv[None, :]).astype(np.float64)
    return np.cos(ang).astype(np.float32), np.sin(ang).astype(np.float32)


def _rope_half(x, cos, sin):
    h = x.shape[1] // 2
    x1, x2 = x[:, :h], x[:, h:]
    return jnp.concatenate([x1 * cos - x2 * sin, x2 * cos + x1 * sin], axis=1)


def _unrope_half(dy, cos, sin):
    h = dy.shape[1] // 2
    d1, d2 = dy[:, :h], dy[:, h:]
    return jnp.concatenate([d1 * cos + d2 * sin, d2 * cos - d1 * sin], axis=1)


def _lane(shape):
    return lax.broadcasted_iota(jnp.int32, shape, 1)


def _partner64(x):
    w = x.shape[1]
    first = (_lane(x.shape) % HEAD_DIM) < (HEAD_DIM // 2)
    return jnp.where(first, pltpu.roll(x, w - HEAD_DIM // 2, 1), pltpu.roll(x, HEAD_DIM // 2, 1))


def _tile_lanes(t, reps):
    return t if reps == 1 else jnp.concatenate([t] * reps, axis=1)


MXU_WIDTH = 256


def _group_mean(x, ones_bd, passes=2):
    width, tile = x.shape[1], ones_bd.shape[0]
    if width > tile:
        return jnp.concatenate([_group_mean(x[:, c:c + tile], ones_bd, passes) for c in range(0, width, tile)], axis=1)
    hi = x.astype(BF16)
    if passes == 1:
        return _dot(hi, ones_bd)
    lo = (x - hi.astype(F32)).astype(BF16)
    return _dot(hi, ones_bd) + _dot(lo, ones_bd)


def _block_diag_mean(width):
    idx = jnp.arange(width) // HEAD_DIM
    return jnp.where(idx[:, None] == idx[None, :], 1.0 / HEAD_DIM, 0.0).astype(BF16)


RET_STEP = 4


def _retention_tables():
    h = RET_HEADS
    log_g = jnp.log(1.0 - 2.0 ** (-5.0 - jnp.arange(h, dtype=F32)))
    idx = jnp.arange(BLK, dtype=F32)
    diff = idx[:, None] - idx[None, :]
    intra = jnp.where(diff >= 0, jnp.exp(log_g[:, None, None] * jnp.maximum(diff, 0.0)), 0.0).astype(F32)
    q_dec = jnp.exp(log_g[:, None] * (idx[None, :] + 1.0)).astype(F32)[:, :, None]
    k_dec = jnp.exp(log_g[:, None] * (BLK - 1.0 - idx[None, :])).astype(F32)[:, :, None]
    chunk_dec = jnp.exp(log_g * BLK).astype(F32)[:, None, None]
    return intra, q_dec, k_dec, chunk_dec


def _retention_fwd(proj, cos, sin, tables, after=None):
    s = proj.shape[0]
    nc = s // BLK
    intra, q_dec, k_dec, chunk_dec = tables

    def body(p_ref, cos_ref, sin_ref, in_ref, qd_ref, kd_ref, cd_ref, *rest):
        o_ref, cat_ref, st_ref, state = rest[-4:]

        @pl.when(pl.program_id(0) == 0)
        def _():
            state[...] = jnp.zeros_like(state)

        for c in range(RET_STEP):
            rows = slice(c * BLK, (c + 1) * BLK)
            cosv, sinv = cos_ref[rows, :], sin_ref[rows, :]
            for h in range(RET_HEADS):
                c0 = h * RET_DIM
                q = p_ref[rows, c0:c0 + RET_DIM].astype(F32)
                k = p_ref[rows, RET_WIDTH + c0:RET_WIDTH + c0 + RET_DIM].astype(F32)
                v = p_ref[rows, 2 * RET_WIDTH + c0:2 * RET_WIDTH + c0 + RET_DIM]
                g = p_ref[rows, 3 * RET_WIDTH + c0:3 * RET_WIDTH + c0 + RET_DIM].astype(F32)
                qb = _rope_half(q, cosv, sinv).astype(BF16)
                kr = _rope_half(k, cosv, sinv) * (RET_DIM ** -0.5)
                kb = kr.astype(BF16)
                scores = _dot_nt(qb, kb) * in_ref[h]
                inner = _dot(scores.astype(BF16), v)
                prev = state[h]
                prev_b = prev.astype(BF16)
                st_ref[h, c] = prev_b
                o = inner + _dot(qb, prev_b) * qd_ref[h]
                o_ref[rows, c0:c0 + RET_DIM] = o
                rstd = lax.rsqrt(jnp.mean(o * o, axis=-1, keepdims=True) + EPS)
                cat_ref[rows, c0:c0 + RET_DIM] = (o * rstd * (g * _sigmoid(g))).astype(BF16)
                state[h] = cd_ref[h] * prev + _dot_tn((kr * kd_ref[h]).astype(BF16), v)

    full = lambda shape: pl.BlockSpec(shape, lambda n: (0,) * len(shape))
    step = RET_STEP * BLK
    return pl.pallas_call(
        body, name="retention_fwd", grid=(nc // RET_STEP,),
        in_specs=[pl.BlockSpec((step, 4 * RET_WIDTH), lambda n: (n, 0)),
                  pl.BlockSpec((step, RET_DIM // 2), lambda n: (n, 0)), pl.BlockSpec((step, RET_DIM // 2), lambda n: (n, 0)),
                  full((RET_HEADS, BLK, BLK)), full((RET_HEADS, BLK, 1)), full((RET_HEADS, BLK, 1)), full((RET_HEADS, 1, 1))]
        + ([ANY] if after is not None else []),
        out_specs=[pl.BlockSpec((step, RET_WIDTH), lambda n: (n, 0)), pl.BlockSpec((step, RET_WIDTH), lambda n: (n, 0)),
                   pl.BlockSpec((RET_HEADS, RET_STEP, RET_DIM, RET_DIM), lambda n: (0, n, 0, 0))],
        out_shape=[jax.ShapeDtypeStruct((s, RET_WIDTH), F32), jax.ShapeDtypeStruct((s, D_MODEL), BF16),
                   jax.ShapeDtypeStruct((RET_HEADS, nc, RET_DIM, RET_DIM), BF16)],
        scratch_shapes=[pltpu.VMEM((RET_HEADS, RET_DIM, RET_DIM), F32)],
        compiler_params=_params("arbitrary"),
    )(proj, cos, sin, intra, q_dec, k_dec, chunk_dec, *((after,) if after is not None else ()))


def _retention_bwd(proj, o, states, dcat, cos, sin, tables):
    s = proj.shape[0]
    nc = s // BLK
    intra, q_dec, k_dec, chunk_dec = tables

    def body(p_ref, o_ref, st_ref, dc_ref, cos_ref, sin_ref, in_ref, qd_ref, kd_ref, cd_ref, dp_ref, dstate):
        @pl.when(pl.program_id(0) == 0)
        def _():
            dstate[...] = jnp.zeros_like(dstate)

        for c in reversed(range(RET_STEP)):
            rows = slice(c * BLK, (c + 1) * BLK)
            cosv, sinv = cos_ref[rows, :], sin_ref[rows, :]
            for h in range(RET_HEADS):
                c0 = h * RET_DIM
                q = p_ref[rows, c0:c0 + RET_DIM].astype(F32)
                k = p_ref[rows, RET_WIDTH + c0:RET_WIDTH + c0 + RET_DIM].astype(F32)
                v = p_ref[rows, 2 * RET_WIDTH + c0:2 * RET_WIDTH + c0 + RET_DIM]
                g = p_ref[rows, 3 * RET_WIDTH + c0:3 * RET_WIDTH + c0 + RET_DIM].astype(F32)
                o = o_ref[rows, c0:c0 + RET_DIM]
                dc = dc_ref[rows, c0:c0 + RET_DIM].astype(F32)
                rstd = lax.rsqrt(jnp.mean(o * o, axis=-1, keepdims=True) + EPS)
                nrm = o * rstd
                sg = _sigmoid(g)
                dg = dc * nrm * (sg * (1.0 + g * (1.0 - sg)))
                dn = dc * (g * sg)
                do = rstd * (dn - nrm * jnp.mean(dn * nrm, axis=-1, keepdims=True))
                qb = _rope_half(q, cosv, sinv).astype(BF16)
                kr = _rope_half(k, cosv, sinv) * (RET_DIM ** -0.5)
                kb = kr.astype(BF16)
                mask = in_ref[h]
                qd, kd = qd_ref[h], kd_ref[h]
                prev_b = st_ref[h, c]
                dnext = dstate[h]
                dnext_b = dnext.astype(BF16)
                att = (_dot_nt(qb, kb) * mask).astype(BF16)
                do_b = do.astype(BF16)
                doq = (do * qd).astype(BF16)
                dv = _dot_tn(att, do_b) + _dot((kr * kd).astype(BF16), dnext_b)
                ds = (_dot_nt(do_b, v) * mask).astype(BF16)
                dqr = _dot(ds, kb) + _dot_nt(doq, prev_b)
                dkr = _dot_tn(ds, qb) + _dot_nt(v, dnext_b) * kd
                dstate[h] = cd_ref[h] * dnext + _dot_tn(qb, doq)
                dq = _unrope_half(dqr, cosv, sinv)
                dk = _unrope_half(dkr * (RET_DIM ** -0.5), cosv, sinv)
                dp_ref[rows, c0:c0 + RET_DIM] = dq.astype(BF16)
                dp_ref[rows, RET_WIDTH + c0:RET_WIDTH + c0 + RET_DIM] = dk.astype(BF16)
                dp_ref[rows, 2 * RET_WIDTH + c0:2 * RET_WIDTH + c0 + RET_DIM] = dv.astype(BF16)
                dp_ref[rows, 3 * RET_WIDTH + c0:3 * RET_WIDTH + c0 + RET_DIM] = dg.astype(BF16)

    steps = nc // RET_STEP
    rev = lambda n: steps - 1 - n
    full = lambda shape: pl.BlockSpec(shape, lambda n: (0,) * len(shape))
    step = RET_STEP * BLK
    return pl.pallas_call(
        body, name="retention_bwd", grid=(steps,),
        in_specs=[pl.BlockSpec((step, 4 * RET_WIDTH), lambda n: (rev(n), 0)),
                  pl.BlockSpec((step, RET_WIDTH), lambda n: (rev(n), 0)),
                  pl.BlockSpec((RET_HEADS, RET_STEP, RET_DIM, RET_DIM), lambda n: (0, rev(n), 0, 0)),
                  pl.BlockSpec((step, RET_WIDTH), lambda n: (rev(n), 0)),
                  pl.BlockSpec((step, RET_DIM // 2), lambda n: (rev(n), 0)), pl.BlockSpec((step, RET_DIM // 2), lambda n: (rev(n), 0)),
                  full((RET_HEADS, BLK, BLK)), full((RET_HEADS, BLK, 1)), full((RET_HEADS, BLK, 1)), full((RET_HEADS, 1, 1))],
        out_specs=pl.BlockSpec((step, 4 * RET_WIDTH), lambda n: (rev(n), 0)),
        out_shape=jax.ShapeDtypeStruct((s, EVEN_IN), BF16),
        scratch_shapes=[pltpu.VMEM((RET_HEADS, RET_DIM, RET_DIM), F32)],
        compiler_params=_params("arbitrary"))(proj, o, states, dcat, cos, sin, intra, q_dec, k_dec, chunk_dec)


CONV_ROWS = 256
HALO = 16


def _conv_pieces(p, halo, first):
    gb, gc, u, gv = (p[:, i * CONV_WIDTH:(i + 1) * CONV_WIDTH] for i in range(4))
    cu = gc * u
    hcu = halo[:, CONV_WIDTH:2 * CONV_WIDTH] * halo[:, 2 * CONV_WIDTH:3 * CONV_WIDTH]
    hcu = jnp.where(first, 0.0, hcu)
    r1, r2 = hcu[HALO - 1:HALO], hcu[HALO - 2:HALO - 1]
    row = lax.broadcasted_iota(jnp.int32, cu.shape, 0)
    m1 = jnp.where(row == 0, r1, pltpu.roll(cu, 1, 0))
    m2 = jnp.where(row == 0, r2, jnp.where(row == 1, r1, pltpu.roll(cu, 2, 0)))
    return gb, gc, u, gv, cu, m1, m2


def _conv_fwd(proj, conv_w, cat):
    s = proj.shape[0]
    per = CONV_ROWS // HALO

    def body(p_ref, halo_ref, w_ref, cat_in, cat_ref):
        del cat_in
        first = pl.program_id(0) == 0
        gb, _, _, gv, cu, m1, m2 = _conv_pieces(p_ref[...].astype(F32), halo_ref[...].astype(F32), first)
        conv = w_ref[0:1, :] * m2 + w_ref[1:2, :] * m1 + w_ref[2:3, :] * cu
        cat_ref[...] = (gb * conv * (gv * _sigmoid(gv))).astype(BF16)

    return pl.pallas_call(
        body, name="conv_fwd", grid=(s // CONV_ROWS,),
        in_specs=[pl.BlockSpec((CONV_ROWS, 4 * CONV_WIDTH), lambda i: (i, 1)),
                  pl.BlockSpec((HALO, 4 * CONV_WIDTH), lambda i: (jnp.maximum(i * per - 1, 0), 1)),
                  pl.BlockSpec((3, CONV_WIDTH), lambda i: (0, 0)), ANY],
        out_specs=pl.BlockSpec((CONV_ROWS, CONV_WIDTH), lambda i: (i, 1)),
        out_shape=jax.ShapeDtypeStruct(cat.shape, cat.dtype), input_output_aliases={3: 0},
        compiler_params=_params("parallel"))(proj, proj, conv_w, cat)


def _conv_bwd(proj, dcat, conv_w, dproj):
    s = proj.shape[0]
    per = CONV_ROWS // HALO
    last_halo = s // HALO - 1
    nsteps = s // CONV_ROWS

    def body(p_ref, halo_ref, nxt_ref, dc_ref, dnxt_ref, w_ref, dp_in, dp_ref, dw_ref):
        del dp_in
        i = pl.program_id(0)
        gb, gc, u, gv, cu, m1, m2 = _conv_pieces(p_ref[...].astype(F32), halo_ref[...].astype(F32), i == 0)
        w0, w1, w2 = w_ref[0:1, :], w_ref[1:2, :], w_ref[2:3, :]
        conv = w0 * m2 + w1 * m1 + w2 * cu
        dco = dc_ref[...].astype(F32)
        sg = _sigmoid(gv)
        silu = gv * sg
        dgb = dco * conv * silu
        dco_gb = dco * gb
        dgv = dco_gb * conv * (sg * (1.0 + gv * (1.0 - sg)))
        dconv = dco_gb * silu
        nxt = nxt_ref[...].astype(F32)
        ngv = nxt[:, 3 * CONV_WIDTH:]
        dnext = dnxt_ref[...].astype(F32) * nxt[:, :CONV_WIDTH] * (ngv * _sigmoid(ngv))
        dnext = jnp.where(i == nsteps - 1, 0.0, dnext)
        n1, n2 = dnext[0:1], dnext[1:2]
        row = lax.broadcasted_iota(jnp.int32, dconv.shape, 0)
        p1 = jnp.where(row == CONV_ROWS - 1, n1, pltpu.roll(dconv, CONV_ROWS - 1, 0))
        p2 = jnp.where(row == CONV_ROWS - 1, n2, jnp.where(row == CONV_ROWS - 2, n1, pltpu.roll(dconv, CONV_ROWS - 2, 0)))
        dcu = w2 * dconv + w1 * p1 + w0 * p2
        dp_ref[...] = jnp.concatenate([dgb, dcu * u, dcu * gc, dgv], axis=1).astype(BF16)

        @pl.when(i == 0)
        def _():
            dw_ref[...] = jnp.zeros_like(dw_ref)

        taps = [jnp.sum(dconv * m, axis=0, keepdims=True) for m in (m2, m1, cu)]
        r8 = lax.broadcasted_iota(jnp.int32, dw_ref.shape, 0)
        dw_ref[...] += jnp.where(r8 == 0, taps[0], jnp.where(r8 == 1, taps[1], jnp.where(r8 == 2, taps[2], 0.0)))

    return pl.pallas_call(
        body, name="conv_bwd", grid=(nsteps,),
        in_specs=[pl.BlockSpec((CONV_ROWS, 4 * CONV_WIDTH), lambda i: (i, 1)),
                  pl.BlockSpec((HALO, 4 * CONV_WIDTH), lambda i: (jnp.maximum(i * per - 1, 0), 1)),
                  pl.BlockSpec((HALO, 4 * CONV_WIDTH), lambda i: (jnp.minimum((i + 1) * per, last_halo), 1)),
                  pl.BlockSpec((CONV_ROWS, CONV_WIDTH), lambda i: (i, 1)),
                  pl.BlockSpec((HALO, CONV_WIDTH), lambda i: (jnp.minimum((i + 1) * per, last_halo), 1)),
                  pl.BlockSpec((3, CONV_WIDTH), lambda i: (0, 0)), ANY],
        out_specs=[pl.BlockSpec((CONV_ROWS, 4 * CONV_WIDTH), lambda i: (i, 1)), pl.BlockSpec((8, CONV_WIDTH), lambda i: (0, 0))],
        out_shape=[jax.ShapeDtypeStruct(dproj.shape, dproj.dtype), jax.ShapeDtypeStruct((8, CONV_WIDTH), F32)],
        input_output_aliases={6: 0},
        compiler_params=_params("arbitrary"))(proj, proj, proj, dcat, dcat, conv_w, dproj)


GROUP_HEADS = Q_HEADS // KV_HEADS
GROUP_WIDTH = GROUP_HEADS * HEAD_DIM
SLAB = 2 * HEAD_DIM
KV_COL = ATTN_WIDTH // GROUP_WIDTH
GATE_COL = KV_COL + 1
ATTN_SCALE = HEAD_DIM ** -0.5
KV_ROWS = 1024


def _half_mask(shape, which):
    return (_lane(shape) // HEAD_DIM) == which


def _dup_head(slab, which):
    kept = jnp.where(_half_mask(slab.shape, which), slab, 0.0)
    return kept + pltpu.roll(kept, HEAD_DIM, 1)


def _stack_heads(x):
    parts = []
    for sl in range(GROUP_WIDTH // SLAB):
        slab = x[:, sl * SLAB:(sl + 1) * SLAB]
        parts += [jnp.where(_half_mask(slab.shape, e), slab, 0.0) for e in range(2)]
    return jnp.concatenate(parts, axis=0)


def _unstack_heads(y):
    slabs = []
    for sl in range(GROUP_WIDTH // SLAB):
        a, b = y[(2 * sl) * BLK:(2 * sl + 1) * BLK], y[(2 * sl + 1) * BLK:(2 * sl + 2) * BLK]
        slabs.append(jnp.where(_half_mask(a.shape, 0), a, b))
    return jnp.concatenate(slabs, axis=1)


def _q_prep(q, qw, cosf, sins, ones_bd):
    rstd = lax.rsqrt(_group_mean(q * q, ones_bd) + EPS)
    nrm = q * rstd
    y = nrm * qw
    return nrm, rstd, y * cosf + _partner64(y) * sins


def _band(tri_ref, n):
    own = tri_ref[...] > 0.5
    return own, jnp.where(jnp.logical_and(n == 0, jnp.logical_not(own)), -1e30, 0.0)


def _fold(pair, own):
    return jnp.where(own, pair[:, BLK:], pair[:, :BLK])


def _unfold(folded, own):
    return jnp.concatenate([jnp.where(own, 0.0, folded), jnp.where(own, folded, 0.0)], axis=1)


def _head_probs(raw_scores, sink, own, bias):
    sc = _fold(raw_scores, own) + bias
    m = jnp.maximum(jnp.max(sc, axis=-1, keepdims=True), sink)
    p = jnp.exp(sc - m)
    psink = jnp.exp(sink - m)
    inv = 1.0 / (jnp.sum(p, axis=-1, keepdims=True) + psink)
    return p * inv, psink * inv


def _k_prep(k, kw, cosf, sins, ones_bd):
    rstd = lax.rsqrt(_group_mean(k * k, ones_bd) + EPS)
    nrm = k * rstd
    y = nrm * kw
    return nrm, rstd, y * cosf + _partner64(y) * sins


def _qk_prep(proj, qw, kw, cos, sins, ones_q, ones_kv):
    s = proj.shape[0]
    rows = min(KV_ROWS, s)

    def body(p_ref, qw_ref, kw_ref, cos_ref, sin_ref, oq_ref, ok_ref, o_ref):
        j = pl.program_id(1)

        @pl.when(j < KV_COL)
        def _():
            cosf, sinf = _tile_lanes(cos_ref[...], 4), _tile_lanes(sin_ref[...], 4)
            roped = _q_prep(p_ref[...].astype(F32), qw_ref[...], cosf, sinf, oq_ref[...])[2]
            o_ref[...] = (roped * ATTN_SCALE).astype(BF16)

        @pl.when(j == KV_COL)
        def _():
            cosf, sinf = _tile_lanes(cos_ref[...], 2), _tile_lanes(sin_ref[...], 2)
            kr = _k_prep(p_ref[:, :KV_WIDTH].astype(F32), kw_ref[...], cosf, sinf, ok_ref[...])[2]
            o_ref[...] = jnp.concatenate([kr.astype(BF16), p_ref[:, KV_WIDTH:]], axis=1)

    full = lambda shape: pl.BlockSpec(shape, lambda i, j: (0,) * len(shape))
    tab = pl.BlockSpec((rows, SLAB), lambda i, j: (i, 0))
    blk = pl.BlockSpec((rows, GROUP_WIDTH), lambda i, j: (i, j))
    return pl.pallas_call(
        body, name="qk_prep", grid=(s // rows, KV_COL + 1),
        in_specs=[blk, full((1, GROUP_WIDTH)), full((1, KV_WIDTH)), tab, tab, full(ones_q.shape), full(ones_kv.shape)],
        out_specs=blk, out_shape=jax.ShapeDtypeStruct((s, ATTN_WIDTH + 2 * KV_WIDTH), BF16),
        compiler_params=_params("parallel", "arbitrary"))(proj, qw, kw, cos, sins, ones_q, ones_kv)


def _keys_values(kc_ref, kp_ref, vc_ref, vp_ref, head):
    lanes = slice((head // 2) * SLAB, (head // 2 + 1) * SLAB)
    dup = lambda ref: _dup_head(ref[:, lanes].astype(F32), head % 2)
    return (jnp.concatenate([dup(kp_ref), dup(kc_ref)], axis=0).astype(BF16),
            jnp.concatenate([dup(vp_ref), dup(vc_ref)], axis=0).astype(BF16))


FWD_STEP_HEADS = 4
BWD_STEP_HEADS = 2


def _swa_specs(heads):
    kv_width = heads * HEAD_DIM
    prev = lambda n: jnp.maximum(n - 1, 0)
    kv = lambda col0, row: pl.BlockSpec((BLK, kv_width), lambda gs, n: (row(n), col0 // kv_width + gs))
    cur = lambda n: n
    full = lambda shape: pl.BlockSpec(shape, lambda gs, n: (0,) * len(shape))
    gate = lambda t: pl.BlockSpec((BLK, GROUP_WIDTH), lambda gs, n: (n, GATE_COL + heads * gs + t))
    return dict(
        sinks=pl.BlockSpec(memory_space=pltpu.SMEM), gates=[gate(t) for t in range(heads)],
        kc=kv(ATTN_WIDTH, cur), kp=kv(ATTN_WIDTH, prev), vc=kv(ATTN_WIDTH + KV_WIDTH, cur), vp=kv(ATTN_WIDTH + KV_WIDTH, prev),
        tri=full((BLK, BLK)), step=pl.BlockSpec((BLK, heads * GROUP_WIDTH), lambda gs, n: (n, gs)))


def _lower_triangle():
    return jnp.tril(jnp.ones((BLK, BLK), F32))


def _swa_fwd(qk, proj, sinks):
    s = proj.shape[0]
    nb = s // BLK
    heads = FWD_STEP_HEADS
    sp = _swa_specs(heads)

    def body(sink_ref, q_ref, kc_ref, kp_ref, vc_ref, vp_ref, *rest):
        gate_refs, (tri_ref, ag_ref, o_ref) = rest[:heads], rest[heads:]
        gs, n = pl.program_id(0), pl.program_id(1)
        own, bias = _band(tri_ref, n)
        for t, gate_ref in enumerate(gate_refs):
            cols = slice(t * GROUP_WIDTH, (t + 1) * GROUP_WIDTH)
            first_head = (heads * gs + t) * GROUP_HEADS
            kcat, vcat = _keys_values(kc_ref, kp_ref, vc_ref, vp_ref, t)
            scores = _dot_nt(_stack_heads(q_ref[:, cols]), kcat)
            probs = []
            for j in range(GROUP_HEADS):
                p, _ = _head_probs(scores[j * BLK:(j + 1) * BLK], sink_ref[first_head + j], own, bias)
                probs.append(_unfold(p.astype(BF16), own))
            o = _unstack_heads(_dot(jnp.concatenate(probs, axis=0), vcat))
            gate = gate_ref[...].astype(F32)
            o_ref[:, cols] = o.astype(BF16)
            ag_ref[:, cols] = (o * (gate * _sigmoid(gate))).astype(BF16)

    shp = jax.ShapeDtypeStruct((s, ATTN_WIDTH), BF16)
    return pl.pallas_call(
        body, name="swa_fwd", grid=(KV_HEADS // heads, nb),
        in_specs=[sp["sinks"], sp["step"], sp["kc"], sp["kp"], sp["vc"], sp["vp"], *sp["gates"], sp["tri"]],
        out_specs=[sp["step"], sp["step"]], out_shape=[shp, shp],
        compiler_params=_params("parallel", "arbitrary"),
    )(sinks, qk, qk, qk, qk, qk, *[proj] * heads, _lower_triangle())


def _swa_bwd(qk, proj, dag, sinks):
    s = proj.shape[0]
    nb = s // BLK
    heads = BWD_STEP_HEADS
    sp = _swa_specs(heads)

    def body(sink_ref, q_ref, kc_ref, kp_ref, vc_ref, vp_ref, *rest):
        gate_refs = rest[:heads]
        dag_ref, tri_ref, dq_ref, dkc_ref, dkp_ref, dvc_ref, dvp_ref, dsink_ref = rest[heads:]
        gs, n = pl.program_id(0), pl.program_id(1)
        own, bias = _band(tri_ref, n)

        @pl.when(n == 0)
        def _():
            dsink_ref[...] = jnp.zeros_like(dsink_ref)

        for t, gate_ref in enumerate(gate_refs):
            cols = slice(t * GROUP_WIDTH, (t + 1) * GROUP_WIDTH)
            first_head = (heads * gs + t) * GROUP_HEADS
            kcat, vcat = _keys_values(kc_ref, kp_ref, vc_ref, vp_ref, t)
            gate = gate_ref[...].astype(F32)
            do = dag_ref[:, cols].astype(F32) * (gate * _sigmoid(gate))
            q_stack = _stack_heads(q_ref[:, cols])
            do_stack = _stack_heads(do).astype(BF16)
            scores = _dot_nt(q_stack, kcat)
            dprobs = _dot_nt(do_stack, vcat)
            probs, dscores, dsinks = [], [], []
            for j in range(GROUP_HEADS):
                rows = slice(j * BLK, (j + 1) * BLK)
                p, psink = _head_probs(scores[rows], sink_ref[first_head + j], own, bias)
                dp = _fold(dprobs[rows], own)
                delta = jnp.sum(p * dp, axis=-1, keepdims=True)
                probs.append(_unfold(p.astype(BF16), own))
                dscores.append(_unfold((p * (dp - delta)).astype(BF16), own))
                dsinks.append(-jnp.sum(psink * delta, axis=0, keepdims=True))
            ds = jnp.concatenate(dscores, axis=0)
            dk = _dot_tn(ds, q_stack)
            dv = _dot_tn(jnp.concatenate(probs, axis=0), do_stack)
            dk = dk + pltpu.roll(dk, HEAD_DIM, 1)
            dv = dv + pltpu.roll(dv, HEAD_DIM, 1)
            dkp_ref[t], dkc_ref[t] = dk[:BLK], dk[BLK:]
            dvp_ref[t], dvc_ref[t] = dv[:BLK], dv[BLK:]
            dq_ref[:, cols] = _unstack_heads(_dot(ds, kcat)).astype(BF16)
            r8 = lax.broadcasted_iota(jnp.int32, (8, SLAB), 0)
            upd = jnp.zeros((8, SLAB), F32)
            for j in range(GROUP_HEADS):
                upd = jnp.where(r8 == j, dsinks[j], upd)
            dsink_ref[t] += upd

    cur_out = pl.BlockSpec((heads, BLK, SLAB), lambda gs, n: (gs, n, 0))
    prev_out = pl.BlockSpec((heads, BLK, SLAB), lambda gs, n: (gs, (n + nb - 1) % nb, 0))
    kv_shape = jax.ShapeDtypeStruct((KV_HEADS, s, SLAB), F32)
    return pl.pallas_call(
        body, name="swa_bwd", grid=(KV_HEADS // heads, nb),
        in_specs=[sp["sinks"], sp["step"], sp["kc"], sp["kp"], sp["vc"], sp["vp"], *sp["gates"], sp["step"], sp["tri"]],
        out_specs=[sp["step"], cur_out, prev_out, cur_out, prev_out,
                   pl.BlockSpec((heads, 8, SLAB), lambda gs, n: (gs, 0, 0))],
        out_shape=[jax.ShapeDtypeStruct((s, ATTN_WIDTH), BF16), kv_shape, kv_shape, kv_shape, kv_shape,
                   jax.ShapeDtypeStruct((KV_HEADS, 8, SLAB), F32)],
        compiler_params=_params("parallel", "arbitrary"),
    )(sinks, qk, qk, qk, qk, qk, *[proj] * heads, dag, _lower_triangle())


def _swa_bwd_finish(proj, dqr, o, dag, dkc, dkp, dvc, dvp, qw, kw, cos, sins, ones_q, ones_kv):
    s = proj.shape[0]
    rows = min(KV_ROWS, s)
    n_q = KV_COL
    q_of = lambda j: jnp.clip(j - 1, 0, n_q - 1)
    gate_of = lambda j: jnp.clip(j - 1 - n_q, 0, n_q - 1)

    def body(kv_ref, p_ref, dqr_ref, o_ref, dag_ref, dkc_ref, dkp_ref, dvc_ref, dvp_ref, qw_ref, kw_ref, cos_ref, sin_ref,
             oq_ref, ok_ref, dp_ref, dqw_ref, dkw_ref):
        j, i = pl.program_id(0), pl.program_id(1)

        @pl.when(j == 0)
        def _():
            @pl.when(i == 0)
            def _():
                dkw_ref[...] = jnp.zeros_like(dkw_ref)

            def assemble(cur_ref, prv_ref):
                tot = [cur_ref[h] + prv_ref[h] for h in range(KV_HEADS)]
                first = _half_mask(tot[0].shape, 0)
                return jnp.concatenate([jnp.where(first, tot[0], tot[1]), jnp.where(first, tot[2], tot[3])], axis=1)

            dkr = assemble(dkc_ref, dkp_ref)
            dv = assemble(dvc_ref, dvp_ref)
            cosf, sinf = _tile_lanes(cos_ref[...], 2), _tile_lanes(sin_ref[...], 2)
            nrm, rstd, _ = _k_prep(kv_ref[:, :KV_WIDTH].astype(F32), kw_ref[...], cosf, sinf, ok_ref[...])
            dy = dkr * cosf + _partner64(dkr * sinf)
            dn = dy * kw_ref[...]
            dk = rstd * (dn - nrm * _group_mean(dn * nrm, ok_ref[...], passes=1))
            dp_ref[...] = jnp.concatenate([dk, dv], axis=1).astype(BF16)
            dkw_ref[...] += jnp.sum(dy * nrm, axis=0, keepdims=True)

        @pl.when(jnp.logical_and(j >= 1, j <= n_q))
        def _():
            @pl.when(i == 0)
            def _():
                dqw_ref[...] = jnp.zeros_like(dqw_ref)

            cosf, sinf = _tile_lanes(cos_ref[...], 4), _tile_lanes(sin_ref[...], 4)
            nrm, rstd, _ = _q_prep(p_ref[...].astype(F32), qw_ref[...], cosf, sinf, oq_ref[...])
            dq = dqr_ref[...].astype(F32) * ATTN_SCALE
            dy = dq * cosf + _partner64(dq * sinf)
            dn = dy * qw_ref[...]
            dp_ref[...] = (rstd * (dn - nrm * _group_mean(dn * nrm, oq_ref[...], passes=1))).astype(BF16)
            dqw_ref[0] += jnp.sum(dy * nrm, axis=0, keepdims=True)

        @pl.when(j > n_q)
        def _():
            gate = p_ref[...].astype(F32)
            sg = _sigmoid(gate)
            dp_ref[...] = (dag_ref[...].astype(F32) * o_ref[...].astype(F32) * (sg * (1.0 + gate * (1.0 - sg)))).astype(BF16)

    first_pass = lambda j, i: jnp.where(j == 0, i, 0)
    acc = pl.BlockSpec((KV_HEADS, rows, SLAB), lambda j, i: (0, first_pass(j, i), 0))
    full = lambda shape: pl.BlockSpec(shape, lambda j, i: (0,) * len(shape))
    tab = pl.BlockSpec((rows, SLAB), lambda j, i: (i, 0))
    out_col = lambda j: jnp.where(j == 0, KV_COL, jnp.where(j <= n_q, j - 1, j))
    return pl.pallas_call(
        body, name="swa_bwd_finish", grid=(2 * n_q + 1, s // rows),
        in_specs=[pl.BlockSpec((rows, GROUP_WIDTH), lambda j, i: (first_pass(j, i), KV_COL)),
                  pl.BlockSpec((rows, GROUP_WIDTH), lambda j, i: (jnp.where(j == 0, 0, i), jnp.where(j <= n_q, q_of(j), j))),
                  pl.BlockSpec((rows, GROUP_WIDTH), lambda j, i: (jnp.where(jnp.logical_and(j >= 1, j <= n_q), i, 0), q_of(j))),
                  pl.BlockSpec((rows, GROUP_WIDTH), lambda j, i: (jnp.where(j > n_q, i, 0), gate_of(j))),
                  pl.BlockSpec((rows, GROUP_WIDTH), lambda j, i: (jnp.where(j > n_q, i, 0), gate_of(j))),
                  acc, acc, acc, acc, full((1, GROUP_WIDTH)), full((1, KV_WIDTH)), tab, tab,
                  full(ones_q.shape), full(ones_kv.shape)],
        out_specs=[pl.BlockSpec((rows, GROUP_WIDTH), lambda j, i: (i, out_col(j))),
                   pl.BlockSpec((1, 8, GROUP_WIDTH), lambda j, i: (q_of(j), 0, 0)), pl.BlockSpec((8, KV_WIDTH), lambda j, i: (0, 0))],
        out_shape=[jax.ShapeDtypeStruct((s, ODD_IN), BF16), jax.ShapeDtypeStruct((n_q, 8, GROUP_WIDTH), F32),
                   jax.ShapeDtypeStruct((8, KV_WIDTH), F32)],
        compiler_params=_params("arbitrary", "arbitrary"),
    )(proj, proj, dqr, o, dag, dkc, dkp, dvc, dvp, qw, kw, cos, sins, ones_q, ones_kv)


def _place():
    x, y, c = lax.axis_index("x"), lax.axis_index("y"), lax.axis_index("c")
    return x, y, c


OTHER_CHIPS = ((1, 0), (0, 1), (1, 1))


def _half_rows(ref, half, rows):
    return ref.at[pl.ds(pl.multiple_of(half * (rows // 2), 8), rows // 2)]


DMA_CHUNK_BYTES = 1 << 20
BF16_TILE_ROWS = 16


def _n_chunks(ref):
    rows = ref.shape[-2]
    nbytes = math.prod(ref.shape) * jnp.dtype(ref.dtype).itemsize
    n = 1
    while 2 * n * DMA_CHUNK_BYTES <= nbytes and rows % (2 * n * BF16_TILE_ROWS) == 0:
        n *= 2
    return n


def _row_chunk(ref, k, n):
    rows = ref.shape[-2] // n
    return ref.at[pl.ds(k * rows, rows)] if len(ref.shape) == 2 else ref.at[:, pl.ds(k * rows, rows)]


def _push(src, dst, send_sem, recv_sem, device_id):
    n = _n_chunks(src)
    for k in range(n):
        pltpu.make_async_remote_copy(src_ref=_row_chunk(src, k, n), dst_ref=_row_chunk(dst, k, n), send_sem=send_sem,
                                     recv_sem=recv_sem, device_id=device_id, device_id_type=MESH).start()
    return pltpu.make_async_remote_copy(src_ref=src, dst_ref=dst, send_sem=send_sem, recv_sem=recv_sem,
                                        device_id=device_id, device_id_type=MESH)


HBM = pl.BlockSpec(memory_space=pltpu.HBM)
SEM = pl.BlockSpec(memory_space=pltpu.SEMAPHORE)
SPLIT_COPY_EFFECT = pltpu.SideEffectType.DATAFLOW_SIDE_EFFECTING


def _in_hbm(a):
    return pltpu.with_memory_space_constraint(a, pltpu.HBM)


def _start_copies(name, arrays, plan, n_copies, after=None):
    n = len(arrays)

    def body(*refs):
        send_sem, recv_sem = refs[-n - 3], refs[-n - 2]
        for k, (src, dst, peer) in enumerate(plan(refs[:n])):
            _push(src, dst, send_sem.at[k], recv_sem.at[k], peer)
        refs[-1][...] = jnp.zeros_like(refs[-1])

    dma = pltpu.SemaphoreType.DMA((n_copies,))
    outs = pl.pallas_call(
        body, name=name,
        out_shape=(dma, dma, *[pltpu.HBM(a.shape, a.dtype) for a in arrays], jax.ShapeDtypeStruct((8, 128), F32)),
        in_specs=[HBM] * n + ([ANY] if after is not None else []),
        out_specs=(SEM, SEM, *[HBM] * n, pl.BlockSpec(memory_space=pltpu.VMEM)),
        input_output_aliases={i: i + 2 for i in range(n)},
        compiler_params=pltpu.CompilerParams(has_side_effects=SPLIT_COPY_EFFECT),
    )(*[_in_hbm(a) for a in arrays], *((after,) if after is not None else ()))
    return outs[0], outs[1], list(outs[2:2 + n]), outs[-1]


def _wait_copies(name, send_sem, recv_sem, arrays, plan, after):
    n = len(arrays)
    after = list(after) if isinstance(after, (list, tuple)) else [after]

    def body(*refs):
        send_ref, recv_ref = refs[n], refs[n + 1]
        for k, (src, dst, peer) in enumerate(plan(refs[:n])):
            cp = pltpu.make_async_remote_copy(src_ref=src, dst_ref=dst, send_sem=send_ref.at[k], recv_sem=recv_ref.at[k],
                                              device_id=peer, device_id_type=MESH)
            cp.wait_send()
            cp.wait_recv()

    return list(pl.pallas_call(
        body, name=name, out_shape=tuple(pltpu.HBM(a.shape, a.dtype) for a in arrays),
        in_specs=[HBM] * n + [SEM, SEM] + [ANY] * len(after), out_specs=tuple([HBM] * n),
        input_output_aliases={i: i for i in range(n)},
        compiler_params=pltpu.CompilerParams(has_side_effects=SPLIT_COPY_EFFECT),
    )(*arrays, send_sem, recv_sem, *after))


def _gather_region(full, kind, chip, half=None):
    if kind == "whole":
        return full.at[chip]
    if kind == "col" or isinstance(kind, tuple):
        part, parts = (0, 1) if kind == "col" else kind[1:]
        rows, width = full.shape[0], full.shape[1] // N_CHIPS
        piece = full.at[:, pl.ds(pl.multiple_of(chip * width + part * (width // parts), LANE_TILE), width // parts)]
    else:
        rows = full.shape[0] // N_CHIPS
        piece = full.at[pl.ds(pl.multiple_of(chip * rows, BF16_TILE_ROWS), rows)]
    return piece if half is None else _half_rows(piece, half, rows)


def _gather_plan(kinds):
    def plan(fulls):
        x, y, c = _place()
        copies = []
        for fx, fy in OTHER_CHIPS:
            for full, kind in zip(fulls, kinds):
                mine = _gather_region(full, kind, 2 * x + y, c)
                copies.append((mine, mine, (x ^ fx, y ^ fy, c)))
        return copies

    return plan


def _gather_start(name, fulls, kinds, after=None):
    return _start_copies(name, list(fulls), _gather_plan(kinds), 3 * len(kinds), after)


def _pass_on_plan(kinds):
    split = [i for i, kind in enumerate(kinds) if kind != "whole"]

    def plan(fulls):
        x, y, c = _place()
        copies = []
        for fx, fy in OTHER_CHIPS:
            chip = 2 * (x ^ fx) + (y ^ fy)
            for i in split:
                landed = _gather_region(fulls[i], kinds[i], chip, c)
                copies.append((landed, landed, (x, y, 1 - c)))
        return copies

    return plan, 3 * len(split)


def _gather_arrive(name, started, kinds, after, arrays=None):
    send_sem, recv_sem, fulls, _ = started
    fulls = _wait_copies(name + "_wait", send_sem, recv_sem, fulls if arrays is None else arrays, _gather_plan(kinds), after)
    plan, count = _pass_on_plan(kinds)
    return _start_copies(name + "_pass_on", fulls, plan, count)


def _gather_done(name, passed_on, kinds, after):
    send_sem, recv_sem, fulls, _ = passed_on
    return _wait_copies(name + "_pass_on_wait", send_sem, recv_sem, fulls, _pass_on_plan(kinds)[0], after)


def _allreduce_small(v):
    def body(v_ref, out_ref, buf, send_sems, recv_sems):
        x, y, c = _place()
        me = 4 * x + 2 * y + c
        buf[me] = v_ref[...]
        copies = []
        for r in range(1, N_DEV):
            peer = (x ^ (r >> 2), y ^ ((r >> 1) & 1), c ^ (r & 1))
            cp = pltpu.make_async_remote_copy(src_ref=v_ref, dst_ref=buf.at[me], send_sem=send_sems.at[r - 1],
                                              recv_sem=recv_sems.at[r - 1], device_id=peer, device_id_type=MESH)
            cp.start()
            copies.append(cp)
        for cp in copies:
            cp.wait_recv()
        for cp in copies:
            cp.wait_send()
        total = buf[0]
        for d in range(1, N_DEV):
            total = total + buf[d]
        out_ref[...] = total

    vm = pl.BlockSpec(memory_space=pltpu.VMEM)
    return pl.pallas_call(
        body, name="allreduce_small", in_specs=[vm], out_specs=vm, out_shape=jax.ShapeDtypeStruct(v.shape, v.dtype),
        scratch_shapes=[pltpu.VMEM((N_DEV,) + v.shape, v.dtype), pltpu.SemaphoreType.DMA((N_DEV - 1,)),
                        pltpu.SemaphoreType.DMA((N_DEV - 1,))],
        compiler_params=pltpu.CompilerParams(has_side_effects=True),
    )(v)


def _exchange_plan(n):
    def plan(refs):
        x, y, c = _place()
        copies = []
        for g, theirs in zip(refs[:n], refs[n:]):
            half = g.shape[1] // 2
            src = g.at[:, pl.ds(pl.multiple_of((1 - c) * half, BF16_TILE_ROWS), half)]
            copies.append((src, theirs, (x, y, 1 - c)))
        return copies

    return plan


def _pair_sum(g, theirs, core, name):
    pieces, half, cols = theirs.shape
    tr = min(half, 256)
    per = half // tr

    def body(core_ref, g_ref, t_ref, o_ref):
        del core_ref
        o_ref[...] = (g_ref[...].astype(F32) + t_ref[...].astype(F32)).astype(BF16)

    spec = pl.BlockSpec((1, tr, cols), lambda p, i, core_ref: (p, i, 0))
    return pl.pallas_call(
        body, name=name, out_shape=jax.ShapeDtypeStruct(theirs.shape, BF16),
        grid_spec=pltpu.PrefetchScalarGridSpec(
            num_scalar_prefetch=1, grid=(pieces, per),
            in_specs=[pl.BlockSpec((1, tr, cols), lambda p, i, core_ref: (p, core_ref[0] * per + i, 0)), spec],
            out_specs=spec),
        compiler_params=_params("parallel", "parallel"))(core, g, theirs)


def _scatter_plan(n):
    def plan(refs):
        parts, stacks = refs[:n], refs[n:]
        x, y, c = _place()
        copies = []
        for fx, fy in OTHER_CHIPS:
            chip = 2 * (x ^ fx) + (y ^ fy)
            for part, stack in zip(parts, stacks):
                if part.shape[0] == N_CHIPS:
                    piece = part.at[chip]
                else:
                    width = part.shape[2] // N_CHIPS
                    piece = part.at[0].at[:, pl.ds(pl.multiple_of(chip * width, 128), width)]
                copies.append((piece, stack.at[2 * x + y], (x ^ fx, y ^ fy, c)))
        return copies

    return plan


def _scatter_start(name, parts, after=None):
    def landing(a):
        return (N_CHIPS, a.shape[1], a.shape[2] if a.shape[0] == N_CHIPS else a.shape[2] // N_CHIPS)

    stacks = [lax.empty(landing(a), a.dtype) for a in parts]
    return _start_copies(name, list(parts) + stacks, _scatter_plan(len(parts)), 3 * len(parts), after)


def _scatter_finish(name, started, after):
    send_sem, recv_sem, arrays, _ = started
    n = len(arrays) // 2
    arrays = _wait_copies(name + "_wait", send_sem, recv_sem, arrays, _scatter_plan(n), after)
    return arrays[:n], arrays[n:]


def _sum_chips(part, stack, place, name):
    _, r, c = stack.shape
    tr = 256
    per = r // tr

    def body(place_ref, own_ref, a_ref, b_ref, c_ref, o_ref):
        del place_ref
        total = own_ref[0].astype(F32)
        for ref in (a_ref, b_ref, c_ref):
            total = total + ref[0].astype(F32)
        o_ref[...] = total

    if part.shape[0] == N_CHIPS:
        own = pl.BlockSpec((1, tr, c), lambda i, pr: (pr[1], i, 0))
    else:
        own = pl.BlockSpec((1, tr, c), lambda i, pr: (0, i, pr[1]))
    other = lambda flip: pl.BlockSpec((1, tr, c), lambda i, pr: (pr[1] ^ flip, i, 0))
    return pl.pallas_call(
        body, name=name, out_shape=jax.ShapeDtypeStruct((2 * r, c), F32),
        grid_spec=pltpu.PrefetchScalarGridSpec(
            num_scalar_prefetch=1, grid=(per,), in_specs=[own, other(2), other(1), other(3)],
            out_specs=pl.BlockSpec((tr, c), lambda i, pr: (pr[0] * per + i, 0))),
        compiler_params=_params("parallel"))(place, part, stack, stack, stack)


def _own_half_plan(fulls):
    x, y, c = _place()
    return [(mine, mine, (x, y, 1 - c)) for mine in (_half_rows(full, c, full.shape[0]) for full in fulls)]


MM = dict(tm=2048, tn=1024, tk=2048)
MM_LONG_K = dict(tm=1024, tn=1024, tk=4096)


def _local_step(x, target, ev_norm_w, q_norm_w, k_norm_w, sinks, own_first, weights_first, late_arrived, weights_late, grads_out):
    s = x.shape[0]
    cos_r, sin_r = _rope_tables(s, RET_DIM)
    cos_a, sin_a = _rope_tables(s, HEAD_DIM)
    cos_a = np.tile(cos_a, (1, 4))
    sins_a = np.tile(np.concatenate([-sin_a, sin_a], axis=1), (1, 2))
    tables = _retention_tables()
    ones_q, ones_kv = _block_diag_mean(min(GROUP_WIDTH, MXU_WIDTH)), _block_diag_mean(min(KV_WIDTH, MXU_WIDTH))
    qw_g = jnp.tile(q_norm_w, (1, GROUP_WIDTH // HEAD_DIM))
    kw_kv = jnp.tile(k_norm_w, (1, KV_WIDTH // HEAD_DIM))
    sinks1 = sinks.reshape(Q_HEADS)

    own_w_in0, own_block, start_token = own_first
    h0 = _rmsnorm(x, ev_norm_w, "norm0", after=start_token)
    shifted = dict(shift=own_block * (own_w_in0.shape[1] // MM["tn"]), total=EVEN_IN // MM["tn"], tm=MM["tm"], tn=MM["tn"],
                   out_dtype=BF16)
    own_blocks = own_w_in0.shape[1] // MM["tn"]
    proj0 = _mm_shifted(h0, own_w_in0, b_shifted=False, first=0, count=own_blocks, name="proj0_own", **shifted)
    for part in range(own_blocks):
        w_in0, small_weights, token = weights_first(part, proj0)
        if small_weights is not None:
            conv_w, od_norm_w = small_weights
        proj0 = _mm_shifted(h0, w_in0, b_shifted=True, first=own_blocks + part, stride=own_blocks,
                            count=shifted["total"] // own_blocks - 1, name=f"proj0_rest{part}", into=proj0, after=token, **shifted)
    o_ret, cat, states = _retention_fwd(proj0, cos_r, sin_r, tables, after=late_arrived(proj0))
    cat = _conv_fwd(proj0, conv_w, cat)
    w_out0, w_in1, w_out1 = weights_late(cat)

    def residual_and_norm(prod, x_ref, w_ref, x1_ref, h1_ref):
        x1v = x_ref[...] + prod
        x1_ref[...] = x1v
        rstd = lax.rsqrt(jnp.mean(x1v * x1v, axis=-1, keepdims=True) + EPS)
        h1_ref[...] = (x1v * rstd * w_ref[...]).astype(BF16)

    def residual_and_loss(prod, x1_ref, t_ref, dyb_ref, sq_ref):
        diff = (x1_ref[...] + prod) - t_ref[...]
        dyb_ref[...] = (diff * (1.0 / D_MODEL)).astype(BF16)

        @pl.when(pl.program_id(0) == 0)
        def _():
            sq_ref[...] = jnp.zeros_like(sq_ref)

        sq_ref[...] += jnp.sum(jnp.sum(diff * diff, axis=1, keepdims=True), axis=0, keepdims=True)

    act = lambda dt: jax.ShapeDtypeStruct((s, D_MODEL), dt)
    x1, h1 = _mm_rows(cat, w_out0, [x], [od_norm_w], [act(F32), act(BF16)], residual_and_norm, tm=min(s, 512), name="out0")
    proj1 = _mm(h1, w_in1, mode="nn", out_dtype=BF16, name="proj1", tm=2048, tn=1536, tk=2048)
    qk = _qk_prep(proj1, qw_g, kw_kv, cos_a, sins_a, ones_q, ones_kv)
    ag, o_att = _swa_fwd(qk, proj1, sinks1)
    dy_b, sq = _mm_rows(ag, w_out1, [x1, target], [], [act(BF16), jax.ShapeDtypeStruct((8, 128), F32)],
                        residual_and_loss, tm=min(s, 512), name="out1")

    g_w_out1 = _mm(ag, dy_b, mode="tn", out_dtype=BF16, name="g_w_out1", **MM_LONG_K)
    dag = _mm(dy_b, w_out1, mode="nt", out_dtype=BF16, name="d_ag", **MM)
    dqr, dkc, dkp, dvc, dvp, dsink = _swa_bwd(qk, proj1, dag, sinks1)
    dproj1, dqw, dkw = _swa_bwd_finish(proj1, dqr, o_att, dag, dkc, dkp, dvc, dvp, qw_g, kw_kv, cos_a, sins_a, ones_q, ones_kv)
    g_w_in1 = _mm(h1, dproj1, mode="tn", out_dtype=BF16, name="g_w_in1", tm=1024, tn=768, tk=4096)
    dh1 = _mm(dproj1, w_in1, mode="nt", out_dtype=BF16, name="d_h1", tm=1024, tn=1024, tk=ODD_IN)
    dx1_b, g_norm1 = _rmsnorm_bwd(x1, od_norm_w, dh1, dy_b, "norm1_bwd", BF16)

    g_w_out0 = _mm(cat, dx1_b, mode="tn", out_dtype=BF16, name="g_w_out0", **MM_LONG_K)
    token = grads_out.begin("upper", (("od_w_in", g_w_in1, "col"), ("od_w_out", g_w_out1, "row"), ("ev_w_out", g_w_out0, "row")))
    dcat = _mm(dx1_b, w_out0, mode="nt", out_dtype=BF16, name="d_cat", after=token, **MM)
    token = grads_out.send("upper", dcat)
    dproj0 = _retention_bwd(proj0, o_ret, states, dcat, cos_r, sin_r, tables)
    dproj0, g_conv = _conv_bwd(proj0, dcat, conv_w, dproj0)
    g_w_in0 = _mm(h0, dproj0, mode="tn", out_dtype=BF16, name="g_w_in0", after=token, **MM_LONG_K)
    token = grads_out.begin("in0", (("ev_w_in", g_w_in0, "col"),))
    half_blocks = s // (2 * MM_LONG_K["tm"])
    dh0 = _mm(dproj0, w_in0, mode="nt", out_dtype=BF16, name="d_h0_top", after=token, m_blocks=(0, half_blocks), **MM_LONG_K)
    dh0 = _mm(dproj0, w_in0, mode="nt", out_dtype=BF16, name="d_h0_bottom", after=grads_out.send("in0", dh0),
              m_blocks=(half_blocks, half_blocks), into=dh0, **MM_LONG_K)
    grad_x, g_norm0 = _rmsnorm_bwd(x, ev_norm_w, dh0, dx1_b, "norm0_bwd", F32)

    g_qw = dqw[:, 0, :].reshape(Q_HEADS, HEAD_DIM).sum(axis=0)
    g_kw = dkw[0].reshape(KV_HEADS, HEAD_DIM).sum(axis=0)
    g_sinks = dsink[:, :, 0].reshape(Q_HEADS)
    small = dict(ev_norm=g_norm0[0], od_norm=g_norm1[0], conv=g_conv[:3], qw=g_qw, kw=g_kw, sinks=g_sinks)
    return sq[0, 0], grad_x, small


LANE_TILE = 128


def _pack_small_grads(small, sq):
    pad = lambda v, n: jnp.pad(v, (0, n - v.shape[0]))
    tail = jnp.concatenate([pad(small["qw"], LANE_TILE), pad(small["kw"], LANE_TILE), small["sinks"]])
    rows = [small["ev_norm"], small["od_norm"]] + [small["conv"][t] for t in range(3)] + [tail, sq.reshape(1)]
    rows += [jnp.zeros((1,), F32)] * (8 - len(rows))
    return jnp.stack([pad(r, D_MODEL) for r in rows])


def _adamw_small(tot, chip, params):
    n = len(params)

    def body(chip_ref, tot_ref, *refs):
        ins, outs = refs[:3 * n], refs[3 * n:]
        c = chip_ref[0]

        def own(rows, width):
            blocks = [tot_ref[rows, k * width:(k + 1) * width] for k in range(N_CHIPS)]
            g = blocks[-1]
            for k in reversed(range(N_CHIPS - 1)):
                g = jnp.where(c == k, blocks[k], g)
            return g

        grads = [tot_ref[0:1, :], own(slice(2, 5), CONV_WIDTH // N_CHIPS), own(slice(1, 2), D_MODEL // N_CHIPS),
                 tot_ref[5:6, 0:HEAD_DIM], tot_ref[5:6, LANE_TILE:LANE_TILE + HEAD_DIM],
                 tot_ref[5:6, 2 * LANE_TILE:2 * LANE_TILE + Q_HEADS]]
        for i, g in enumerate(grads):
            w_ref, m_ref, v_ref = ins[3 * i:3 * i + 3]
            g_out, d_out, nm_out, nv_out = outs[4 * i:4 * i + 4]
            at = (0,) if len(w_ref.shape) == 3 else ()
            delta, nm, nv = _adamw_math(w_ref[at] if at else w_ref[...], g, m_ref[at] if at else m_ref[...],
                                        v_ref[at] if at else v_ref[...])
            for ref, val in ((g_out, g), (d_out, delta), (nm_out, nm), (nv_out, nv)):
                if at:
                    ref[0] = val
                else:
                    ref[...] = val

    vm = pl.BlockSpec(memory_space=pltpu.VMEM)
    flat = [a for p in params for a in p]
    outs = pl.pallas_call(
        body, name="adamw_small", in_specs=[pl.BlockSpec(memory_space=pltpu.SMEM), vm] + [vm] * len(flat),
        out_specs=[vm] * (4 * n), out_shape=[jax.ShapeDtypeStruct(p[0].shape, F32) for p in params for _ in range(4)],
    )(chip, tot, *flat)
    return [tuple(outs[4 * i:4 * i + 4]) for i in range(n)]


class _ReduceScatter:
    def __init__(self, place):
        self.place = place
        self.exchanging = {}
        self.started = []
        self.sharing = {}

    def begin(self, tag, grads):
        pieces = [g[None] if kind == "col" else g.reshape(N_CHIPS, g.shape[0] // N_CHIPS, g.shape[1]) for _, g, kind in grads]
        landing = [lax.empty((p.shape[0], p.shape[1] // 2, p.shape[2]), p.dtype) for p in pieces]
        started = _start_copies("exchange_" + tag, pieces + landing, _exchange_plan(len(pieces)), len(pieces))
        self.exchanging[tag] = ([nm for nm, _, _ in grads], started)
        return started[3]

    def send(self, tag, after):
        group, (send_sem, recv_sem, arrays, _) = self.exchanging[tag]
        n = len(group)
        arrays = _wait_copies("exchange_" + tag + "_wait", send_sem, recv_sem, arrays, _exchange_plan(n), after)
        parts = [_pair_sum(g, t, self.place[:1], "pair_sum_" + nm) for g, t, nm in zip(arrays[:n], arrays[n:], group)]
        started = _scatter_start("scatter_" + tag, parts)
        token = self.sum_up(started[3])
        self.started.append((tag, group, started))
        return started[3] if token is None else token

    def sum_up(self, after):
        token = None
        for tag, group, started in self.started:
            if tag not in self.sharing:
                parts, stacks = _scatter_finish("scatter_" + tag, started, after)
                sums = [_sum_chips(p, s, self.place, "chip_sum_" + nm) for p, s, nm in zip(parts, stacks, group)]
                self.sharing[tag] = (group, _start_copies("share_" + tag, sums, _own_half_plan, len(sums)))
                token = self.sharing[tag][1][3]
        return token

    def result(self, tag, after):
        group, (send_sem, recv_sem, sums, _) = self.sharing[tag]
        return dict(zip(group, _wait_copies("share_" + tag + "_wait", send_sem, recv_sem, sums, _own_half_plan, after)))


def kernel(x, ev_norm_w, ev_w_in, ev_conv_w, ev_w_out, od_norm_w, od_w_in, od_q_norm_w, od_k_norm_w, od_sinks, od_w_out, loss_target, m_ev_norm_w, m_ev_w_in, m_ev_conv_w, m_ev_w_out, m_od_norm_w, m_od_w_in, m_od_q_norm_w, m_od_k_norm_w, m_od_sinks, m_od_w_out, v_ev_norm_w, v_ev_w_in, v_ev_conv_w, v_ev_w_out, v_od_norm_w, v_od_w_in, v_od_q_norm_w, v_od_k_norm_w, v_od_sinks, v_od_w_out):
    my_chip = 2 * lax.axis_index("x") + lax.axis_index("y")
    place = jnp.stack([lax.axis_index("c"), my_chip]).astype(jnp.int32)
    shard_w = D_MODEL // N_CHIPS
    conv_shard = CONV_WIDTH // N_CHIPS

    small_in = jnp.zeros((8, shard_w), F32)
    small_in = small_in.at[0].set(od_norm_w[0]).at[1:4, :conv_shard].set(ev_conv_w[0])
    small_in = lax.dynamic_update_slice(jnp.zeros((N_CHIPS, 8, shard_w), F32), small_in[None], (my_chip, 0, 0))
    chip = place[1:]
    first_kinds, second_kinds, late_kinds = (("col", 0, 2), "whole"), (("col", 1, 2),), ("row", "col", "row")
    w_in0_own_place, w_in0_shard = _cast_into_gathered(ev_w_in[0], "col", chip, "cast_w_in0", keep_shard=True)
    first = _gather_start("gather_first", [w_in0_own_place, small_in], first_kinds)
    second = _gather_start("gather_second", [first[2][0]], second_kinds)
    late_own = [_cast_into_gathered(ev_w_out[0], "row", chip, "cast_w_out0"),
                _cast_into_gathered(od_w_in[0], "col", chip, "cast_w_in1"),
                _cast_into_gathered(od_w_out[0], "row", chip, "cast_w_out1")]
    late, w_in0_so_far = [], []

    def weights_first(part, after):
        if part == 0:
            passed_on = _gather_arrive("gather_first", first, first_kinds, [after] + late_own, arrays=[second[2][0], first[2][1]])
            w_in0, small_all = _gather_done("gather_first", passed_on, first_kinds, passed_on[3])
            w_in0_so_far.append(w_in0)
            late.append(_gather_start("gather_late", late_own, late_kinds, after=small_all))
            od_norm_full = small_all[:, 0, :].reshape(1, D_MODEL)
            conv_full = jnp.transpose(small_all[:, 1:4, :conv_shard], (1, 0, 2)).reshape(3, CONV_WIDTH)
            return w_in0, (conv_full, od_norm_full), late[0][3]
        passed_on = _gather_arrive("gather_second", second, second_kinds, after, arrays=w_in0_so_far)
        (w_in0,) = _gather_done("gather_second", passed_on, second_kinds, passed_on[3])
        return w_in0, None, None

    def late_arrived(after):
        late.append(_gather_arrive("gather_late", late[0], late_kinds, after))
        return late[1][3]

    def weights_late(after):
        return _gather_done("gather_late", late[1], late_kinds, after)

    reduce_scatter = _ReduceScatter(place)
    sq, grad_x, small = _local_step(x[0], loss_target[0], ev_norm_w, od_q_norm_w, od_k_norm_w, od_sinks,
                                    (w_in0_shard, chip, second[3]), weights_first, late_arrived, weights_late,
                                    reduce_scatter)

    token = reduce_scatter.sum_up(grad_x)
    upd = {}
    upper = reduce_scatter.result("upper", token)
    upd["od_w_in"] = _adamw(od_w_in[0], upper["od_w_in"], m_od_w_in[0], v_od_w_in[0], "adamw_od_w_in")
    upd["od_w_out"] = _adamw(od_w_out[0], upper["od_w_out"], m_od_w_out[0], v_od_w_out[0], "adamw_od_w_out")
    upd["ev_w_out"] = _adamw(ev_w_out[0], upper["ev_w_out"], m_ev_w_out[0], v_ev_w_out[0], "adamw_ev_w_out")
    g_ev_w_in = reduce_scatter.result("in0", [upd[nm][1] for nm in ("od_w_in", "od_w_out", "ev_w_out")])["ev_w_in"]
    upd["ev_w_in"] = _adamw(ev_w_in[0], g_ev_w_in, m_ev_w_in[0], v_ev_w_in[0], "adamw_ev_w_in")
    tot = _allreduce_small(_pack_small_grads(small, sq))
    loss = 0.5 * tot[6, 0] / D_MODEL

    smalls = (("ev_norm_w", ev_norm_w, m_ev_norm_w, v_ev_norm_w), ("ev_conv_w", ev_conv_w, m_ev_conv_w, v_ev_conv_w),
              ("od_norm_w", od_norm_w, m_od_norm_w, v_od_norm_w), ("od_q_norm_w", od_q_norm_w, m_od_q_norm_w, v_od_q_norm_w),
              ("od_k_norm_w", od_k_norm_w, m_od_k_norm_w, v_od_k_norm_w), ("od_sinks", od_sinks, m_od_sinks, v_od_sinks))
    for (nm, _, _, _), result in zip(smalls, _adamw_small(tot, chip, [p[1:] for p in smalls])):
        upd[nm] = result
    for nm in ("ev_w_in", "ev_w_out", "od_w_in", "od_w_out"):
        upd[nm] = tuple(u[None] for u in upd[nm])
    order = ("ev_norm_w", "ev_w_in", "ev_conv_w", "ev_w_out", "od_norm_w", "od_w_in", "od_q_norm_w", "od_k_norm_w", "od_sinks", "od_w_out")
    return (loss, grad_x[None], *[upd[nm][0] for nm in order], *[upd[nm][1] for nm in order],
            *[upd[nm][2] for nm in order], *[upd[nm][3] for nm in order])
```

```python
import math

import jax
import jax.numpy as jnp
import numpy as np
from jax import lax
from jax.experimental import pallas as pl
from jax.experimental.pallas import tpu as pltpu

F32 = jnp.float32
BF16 = jnp.bfloat16

D_MODEL = 2048
RET_HEADS = 4
RET_DIM = 256
RET_WIDTH = 1024
CONV_WIDTH = 1024
EVEN_IN = 8192
Q_HEADS = 32
HEAD_DIM = 64
KV_HEADS = 4
KV_WIDTH = 256
ATTN_WIDTH = 2048
ODD_IN = 4608
BLK = 128
ROPE_THETA = 10000.0
EPS = 1e-6
ADAM_LR = 0.001
ADAM_B1 = 0.9
ADAM_B2 = 0.999
ADAM_EPS = 1e-08
ADAM_WD = 0.01
ADAM_STEP = 10
N_CHIPS = 4
N_DEV = 8
VMEM_LIMIT_BYTES = 56 * 1024 * 1024
MESH = pl.DeviceIdType.MESH
ANY = pl.BlockSpec(memory_space=pl.ANY)


def _params(*sem):
    return pltpu.CompilerParams(dimension_semantics=sem, vmem_limit_bytes=VMEM_LIMIT_BYTES)


def _dot(a, b):
    return jnp.dot(a, b, preferred_element_type=F32)


def _dot_nt(a, b):
    return lax.dot_general(a, b, (((1,), (1,)), ((), ())), preferred_element_type=F32)


def _dot_tn(a, b):
    return lax.dot_general(a, b, (((0,), (0,)), ((), ())), preferred_element_type=F32)


def _sigmoid(x):
    return 1.0 / (1.0 + jnp.exp(-x))


def _mm(a, b, *, mode, tm, tn, tk, out_dtype, name, add=None, after=None, m_blocks=None, into=None):
    if mode == "nn":
        (m, k), n = a.shape, b.shape[1]
    elif mode == "nt":
        (m, k), n = a.shape, b.shape[0]
    else:
        (k, m), n = a.shape, b.shape[1]
    tm, tn, tk = min(tm, m), min(tn, n), min(tk, k)
    assert m % tm == 0 and n % tn == 0 and k % tk == 0, (name, m, n, k)
    nk = k // tk
    first_m, count_m = (0, m // tm) if m_blocks is None else m_blocks
    dot = {"nn": _dot, "nt": _dot_nt, "tn": _dot_tn}[mode]
    a_spec = (pl.BlockSpec((tk, tm), lambda i, j, kk: (kk, i + first_m)) if mode == "tn"
              else pl.BlockSpec((tm, tk), lambda i, j, kk: (i + first_m, kk)))
    b_spec = (pl.BlockSpec((tn, tk), lambda i, j, kk: (j, kk)) if mode == "nt"
              else pl.BlockSpec((tk, tn), lambda i, j, kk: (kk, j)))
    o_spec = pl.BlockSpec((tm, tn), lambda i, j, kk: (i + first_m, j))
    has_add = add is not None

    def body(*refs):
        a_ref, b_ref = refs[0], refs[1]
        add_ref = refs[2] if has_add else None
        o_ref, acc_ref = refs[-2], refs[-1]
        p = dot(a_ref[...], b_ref[...])

        def finish(total):
            if has_add:
                total = total + add_ref[...].astype(F32)
            o_ref[...] = total.astype(out_dtype)

        if nk == 1:
            finish(p)
        else:
            kk = pl.program_id(2)

            @pl.when(kk == 0)
            def _():
                acc_ref[...] = p

            @pl.when(jnp.logical_and(kk > 0, kk < nk - 1))
            def _():
                acc_ref[...] += p

            @pl.when(kk == nk - 1)
            def _():
                finish(acc_ref[...] + p)

    extra = [arr for arr in (after, into) if arr is not None]
    in_specs = [a_spec, b_spec] + ([o_spec] if has_add else []) + [ANY] * len(extra)
    args = (a, b) + ((add,) if has_add else ()) + tuple(extra)
    return pl.pallas_call(
        body, name=name, grid=(count_m, n // tn, nk), in_specs=in_specs, out_specs=o_spec,
        out_shape=jax.ShapeDtypeStruct((m, n), out_dtype),
        input_output_aliases={len(args) - 1: 0} if into is not None else {},
        scratch_shapes=[pltpu.VMEM((tm, tn) if nk > 1 else (8, 128), F32)],
        compiler_params=_params("parallel", "parallel", "arbitrary"),
    )(*args)


def _mm_shifted(a, b, shift, *, b_shifted, first, count, total, tm, tn, out_dtype, name, stride=1, into=None, after=None):
    m, k = a.shape
    tm = min(tm, m)
    assert m % tm == 0
    col = lambda j, shift_ref: (shift_ref[0] + first + stride * j) % total
    extra = [arr for arr in (into, after) if arr is not None]

    def body(shift_ref, a_ref, b_ref, *rest):
        del shift_ref
        rest[-1][...] = _dot(a_ref[...], b_ref[...]).astype(out_dtype)

    return pl.pallas_call(
        body, name=name, out_shape=jax.ShapeDtypeStruct((m, total * tn), out_dtype),
        grid_spec=pltpu.PrefetchScalarGridSpec(
            num_scalar_prefetch=1, grid=(m // tm, count),
            in_specs=[pl.BlockSpec((tm, k), lambda i, j, s: (i, 0)),
                      pl.BlockSpec((k, tn), (lambda i, j, s: (0, col(j, s))) if b_shifted else (lambda i, j, s: (0, j)))]
            + [ANY] * len(extra),
            out_specs=pl.BlockSpec((tm, tn), lambda i, j, s: (i, col(j, s)))),
        input_output_aliases={3: 0} if into is not None else {},
        compiler_params=_params("parallel", "arbitrary"))(shift, a, b, *extra)


def _mm_rows(a, b, rows_in, vecs_in, out_shapes, epilogue, *, tm, name):
    m, k = a.shape
    n = b.shape[1]
    assert m % tm == 0
    row = pl.BlockSpec((tm, n), lambda i: (i, 0))

    def body(a_ref, b_ref, *rest):
        epilogue(_dot(a_ref[...], b_ref[...]), *rest)

    out_specs = [row if tuple(s.shape) == (m, n) else pl.BlockSpec(s.shape, lambda i: (0, 0)) for s in out_shapes]
    return pl.pallas_call(
        body, name=name, grid=(m // tm,),
        in_specs=[pl.BlockSpec((tm, k), lambda i: (i, 0)), pl.BlockSpec((k, n), lambda i: (0, 0))] + [row] * len(rows_in)
        + [pl.BlockSpec((1, n), lambda i: (0, 0))] * len(vecs_in),
        out_specs=out_specs, out_shape=out_shapes, compiler_params=_params("arbitrary"),
    )(a, b, *rows_in, *vecs_in)


def _cast_into_gathered(w, kind, chip, name, keep_shard=False):
    r, c = w.shape
    tr = min(r, 512)
    per = r // tr

    def body(chip_ref, w_ref, *outs):
        del chip_ref
        for o_ref in outs:
            o_ref[...] = w_ref[...].astype(BF16)

    if kind == "col":
        shape, out_map = (r, N_CHIPS * c), (lambda i, chip_ref: (i, chip_ref[0]))
    else:
        shape, out_map = (N_CHIPS * r, c), (lambda i, chip_ref: (chip_ref[0] * per + i, 0))
    plain = pl.BlockSpec((tr, c), lambda i, chip_ref: (i, 0))
    out = pl.pallas_call(
        body, name=name,
        out_shape=[jax.ShapeDtypeStruct(shape, BF16)] + ([jax.ShapeDtypeStruct((r, c), BF16)] if keep_shard else []),
        grid_spec=pltpu.PrefetchScalarGridSpec(
            num_scalar_prefetch=1, grid=(per,), in_specs=[plain],
            out_specs=[pl.BlockSpec((tr, c), out_map)] + ([plain] if keep_shard else [])),
        compiler_params=_params("parallel"))(chip, w)
    return out if keep_shard else out[0]


NORM_ROWS = 512


def _rmsnorm(x, w, name, after=None):
    s, d = x.shape
    tr = NORM_ROWS

    def body(x_ref, w_ref, *rest):
        xv = x_ref[...]
        rstd = lax.rsqrt(jnp.mean(xv * xv, axis=-1, keepdims=True) + EPS)
        rest[-1][...] = (xv * rstd * w_ref[...]).astype(BF16)

    return pl.pallas_call(
        body, name=name, grid=(s // tr,),
        in_specs=[pl.BlockSpec((tr, d), lambda i: (i, 0)), pl.BlockSpec((1, d), lambda i: (0, 0))]
        + ([ANY] if after is not None else []),
        out_specs=pl.BlockSpec((tr, d), lambda i: (i, 0)),
        out_shape=jax.ShapeDtypeStruct((s, d), BF16), compiler_params=_params("parallel"),
    )(x, w, *((after,) if after is not None else ()))


def _rmsnorm_bwd(x, w, dh, dres, name, out_dtype):
    s, d = x.shape
    tr = NORM_ROWS

    def body(x_ref, w_ref, dh_ref, dres_ref, dx_ref, dw_ref):
        xv = x_ref[...]
        rstd = lax.rsqrt(jnp.mean(xv * xv, axis=-1, keepdims=True) + EPS)
        nrm = xv * rstd
        dhv = dh_ref[...].astype(F32)
        dn = dhv * w_ref[...]
        dx = dres_ref[...].astype(F32) + rstd * (dn - nrm * jnp.mean(dn * nrm, axis=-1, keepdims=True))
        dx_ref[...] = dx.astype(out_dtype)

        @pl.when(pl.program_id(0) == 0)
        def _():
            dw_ref[...] = jnp.zeros_like(dw_ref)

        dw_ref[...] += jnp.sum(dhv * nrm, axis=0, keepdims=True)

    row = pl.BlockSpec((tr, d), lambda i: (i, 0))
    return pl.pallas_call(
        body, name=name, grid=(s // tr,),
        in_specs=[row, pl.BlockSpec((1, d), lambda i: (0, 0)), row, row],
        out_specs=[row, pl.BlockSpec((8, d), lambda i: (0, 0))],
        out_shape=[jax.ShapeDtypeStruct((s, d), out_dtype), jax.ShapeDtypeStruct((8, d), F32)],
        compiler_params=_params("arbitrary"))(x, w, dh, dres)


def _adamw_math(w, g, m, v):
    nm = ADAM_B1 * m + (1.0 - ADAM_B1) * g
    nv = ADAM_B2 * v + (1.0 - ADAM_B2) * (g * g)
    m_hat = nm / (1.0 - ADAM_B1 ** ADAM_STEP)
    v_hat = nv / (1.0 - ADAM_B2 ** ADAM_STEP)
    return -ADAM_LR * (m_hat / (jnp.sqrt(v_hat) + ADAM_EPS) + ADAM_WD * w), nm, nv


def _adamw(w, g, m, v, name):
    r, c = w.shape
    tr = min(r, 256)
    assert r % tr == 0

    def body(w_ref, g_ref, m_ref, v_ref, g_out, d_ref, nm_ref, nv_ref):
        gv = g_ref[...]
        g_out[...] = gv
        d_ref[...], nm_ref[...], nv_ref[...] = _adamw_math(w_ref[...], gv, m_ref[...], v_ref[...])

    spec = pl.BlockSpec((tr, c), lambda i: (i, 0))
    shp = jax.ShapeDtypeStruct((r, c), F32)
    return pl.pallas_call(body, name=name, grid=(r // tr,), in_specs=[spec] * 4, out_specs=[spec] * 4,
                          out_shape=[shp] * 4, compiler_params=_params("parallel"))(w, g, m, v)


def _rope_tables(s, dim):
    inv = (1.0 / (ROPE_THETA ** (np.arange(0, dim, 2, dtype=np.float64) / dim))).astype(np.float32)
    ang = (np.arange(s, dtype=np.float32)[:, None] * inv[None, :]).astype(np.float64)
    return np.cos(ang).astype(np.float32), np.sin(ang).astype(np.float32)


def _rope_half(x, cos, sin):
    h = x.shape[1] // 2
    x1, x2 = x[:, :h], x[:, h:]
    return jnp.concatenate([x1 * cos - x2 * sin, x2 * cos + x1 * sin], axis=1)


def _unrope_half(dy, cos, sin):
    h = dy.shape[1] // 2
    d1, d2 = dy[:, :h], dy[:, h:]
    return jnp.concatenate([d1 * cos + d2 * sin, d2 * cos - d1 * sin], axis=1)


def _lane(shape):
    return lax.broadcasted_iota(jnp.int32, shape, 1)


def _partner64(x):
    w = x.shape[1]
    first = (_lane(x.shape) % HEAD_DIM) < (HEAD_DIM // 2)
    return jnp.where(first, pltpu.roll(x, w - HEAD_DIM // 2, 1), pltpu.roll(x, HEAD_DIM // 2, 1))


def _tile_lanes(t, reps):
    return t if reps == 1 else jnp.concatenate([t] * reps, axis=1)


MXU_WIDTH = 256


def _group_mean(x, ones_bd, passes=2):
    width, tile = x.shape[1], ones_bd.shape[0]
    if width > tile:
        return jnp.concatenate([_group_mean(x[:, c:c + tile], ones_bd, passes) for c in range(0, width, tile)], axis=1)
    hi = x.astype(BF16)
    if passes == 1:
        return _dot(hi, ones_bd)
    lo = (x - hi.astype(F32)).astype(BF16)
    return _dot(hi, ones_bd) + _dot(lo, ones_bd)


def _block_diag_mean(width):
    idx = jnp.arange(width) // HEAD_DIM
    return jnp.where(idx[:, None] == idx[None, :], 1.0 / HEAD_DIM, 0.0).astype(BF16)


RET_STEP = 4


def _retention_tables():
    h = RET_HEADS
    log_g = jnp.log(1.0 - 2.0 ** (-5.0 - jnp.arange(h, dtype=F32)))
    idx = jnp.arange(BLK, dtype=F32)
    diff = idx[:, None] - idx[None, :]
    intra = jnp.where(diff >= 0, jnp.exp(log_g[:, None, None] * jnp.maximum(diff, 0.0)), 0.0).astype(F32)
    q_dec = jnp.exp(log_g[:, None] * (idx[None, :] + 1.0)).astype(F32)[:, :, None]
    k_dec = jnp.exp(log_g[:, None] * (BLK - 1.0 - idx[None, :])).astype(F32)[:, :, None]
    chunk_dec = jnp.exp(log_g * BLK).astype(F32)[:, None, None]
    return intra, q_dec, k_dec, chunk_dec


def _retention_fwd(proj, cos, sin, tables, after=None):
    s = proj.shape[0]
    nc = s // BLK
    intra, q_dec, k_dec, chunk_dec = tables

    def body(p_ref, cos_ref, sin_ref, in_ref, qd_ref, kd_ref, cd_ref, *rest):
        o_ref, cat_ref, st_ref, state = rest[-4:]

        @pl.when(pl.program_id(0) == 0)
        def _():
            state[...] = jnp.zeros_like(state)

        for c in range(RET_STEP):
            rows = slice(c * BLK, (c + 1) * BLK)
            cosv, sinv = cos_ref[rows, :], sin_ref[rows, :]
            for h in range(RET_HEADS):
                c0 = h * RET_DIM
                q = p_ref[rows, c0:c0 + RET_DIM].astype(F32)
                k = p_ref[rows, RET_WIDTH + c0:RET_WIDTH + c0 + RET_DIM].astype(F32)
                v = p_ref[rows, 2 * RET_WIDTH + c0:2 * RET_WIDTH + c0 + RET_DIM]
                g = p_ref[rows, 3 * RET_WIDTH + c0:3 * RET_WIDTH + c0 + RET_DIM].astype(F32)
                qb = _rope_half(q, cosv, sinv).astype(BF16)
                kr = _rope_half(k, cosv, sinv) * (RET_DIM ** -0.5)
                kb = kr.astype(BF16)
                scores = _dot_nt(qb, kb) * in_ref[h]
                inner = _dot(scores.astype(BF16), v)
                prev = state[h]
                prev_b = prev.astype(BF16)
                st_ref[h, c] = prev_b
                o = inner + _dot(qb, prev_b) * qd_ref[h]
                o_ref[rows, c0:c0 + RET_DIM] = o
                rstd = lax.rsqrt(jnp.mean(o * o, axis=-1, keepdims=True) + EPS)
                cat_ref[rows, c0:c0 + RET_DIM] = (o * rstd * (g * _sigmoid(g))).astype(BF16)
                state[h] = cd_ref[h] * prev + _dot_tn((kr * kd_ref[h]).astype(BF16), v)

    full = lambda shape: pl.BlockSpec(shape, lambda n: (0,) * len(shape))
    step = RET_STEP * BLK
    return pl.pallas_call(
        body, name="retention_fwd", grid=(nc // RET_STEP,),
        in_specs=[pl.BlockSpec((step, 4 * RET_WIDTH), lambda n: (n, 0)),
                  pl.BlockSpec((step, RET_DIM // 2), lambda n: (n, 0)), pl.BlockSpec((step, RET_DIM // 2), lambda n: (n, 0)),
                  full((RET_HEADS, BLK, BLK)), full((RET_HEADS, BLK, 1)), full((RET_HEADS, BLK, 1)), full((RET_HEADS, 1, 1))]
        + ([ANY] if after is not None else []),
        out_specs=[pl.BlockSpec((step, RET_WIDTH), lambda n: (n, 0)), pl.BlockSpec((step, RET_WIDTH), lambda n: (n, 0)),
                   pl.BlockSpec((RET_HEADS, RET_STEP, RET_DIM, RET_DIM), lambda n: (0, n, 0, 0))],
        out_shape=[jax.ShapeDtypeStruct((s, RET_WIDTH), F32), jax.ShapeDtypeStruct((s, D_MODEL), BF16),
                   jax.ShapeDtypeStruct((RET_HEADS, nc, RET_DIM, RET_DIM), BF16)],
        scratch_shapes=[pltpu.VMEM((RET_HEADS, RET_DIM, RET_DIM), F32)],
        compiler_params=_params("arbitrary"),
    )(proj, cos, sin, intra, q_dec, k_dec, chunk_dec, *((after,) if after is not None else ()))


def _retention_bwd(proj, o, states, dcat, cos, sin, tables):
    s = proj.shape[0]
    nc = s // BLK
    intra, q_dec, k_dec, chunk_dec = tables

    def body(p_ref, o_ref, st_ref, dc_ref, cos_ref, sin_ref, in_ref, qd_ref, kd_ref, cd_ref, dp_ref, dstate):
        @pl.when(pl.program_id(0) == 0)
        def _():
            dstate[...] = jnp.zeros_like(dstate)

        for c in reversed(range(RET_STEP)):
            rows = slice(c * BLK, (c + 1) * BLK)
            cosv, sinv = cos_ref[rows, :], sin_ref[rows, :]
            for h in range(RET_HEADS):
                c0 = h * RET_DIM
                q = p_ref[rows, c0:c0 + RET_DIM].astype(F32)
                k = p_ref[rows, RET_WIDTH + c0:RET_WIDTH + c0 + RET_DIM].astype(F32)
                v = p_ref[rows, 2 * RET_WIDTH + c0:2 * RET_WIDTH + c0 + RET_DIM]
                g = p_ref[rows, 3 * RET_WIDTH + c0:3 * RET_WIDTH + c0 + RET_DIM].astype(F32)
                o = o_ref[rows, c0:c0 + RET_DIM]
                dc = dc_ref[rows, c0:c0 + RET_DIM].astype(F32)
                rstd = lax.rsqrt(jnp.mean(o * o, axis=-1, keepdims=True) + EPS)
                nrm = o * rstd
                sg = _sigmoid(g)
                dg = dc * nrm * (sg * (1.0 + g * (1.0 - sg)))
                dn = dc * (g * sg)
                do = rstd * (dn - nrm * jnp.mean(dn * nrm, axis=-1, keepdims=True))
                qb = _rope_half(q, cosv, sinv).astype(BF16)
                kr = _rope_half(k, cosv, sinv) * (RET_DIM ** -0.5)
                kb = kr.astype(BF16)
                mask = in_ref[h]
                qd, kd = qd_ref[h], kd_ref[h]
                prev_b = st_ref[h, c]
                dnext = dstate[h]
                dnext_b = dnext.astype(BF16)
                att = (_dot_nt(qb, kb) * mask).astype(BF16)
                do_b = do.astype(BF16)
                doq = (do * qd).astype(BF16)
                dv = _dot_tn(att, do_b) + _dot((kr * kd).astype(BF16), dnext_b)
                ds = (_dot_nt(do_b, v) * mask).astype(BF16)
                dqr = _dot(ds, kb) + _dot_nt(doq, prev_b)
                dkr = _dot_tn(ds, qb) + _dot_nt(v, dnext_b) * kd
                dstate[h] = cd_ref[h] * dnext + _dot_tn(qb, doq)
                dq = _unrope_half(dqr, cosv, sinv)
                dk = _unrope_half(dkr * (RET_DIM ** -0.5), cosv, sinv)
                dp_ref[rows, c0:c0 + RET_DIM] = dq.astype(BF16)
                dp_ref[rows, RET_WIDTH + c0:RET_WIDTH + c0 + RET_DIM] = dk.astype(BF16)
                dp_ref[rows, 2 * RET_WIDTH + c0:2 * RET_WIDTH + c0 + RET_DIM] = dv.astype(BF16)
                dp_ref[rows, 3 * RET_WIDTH + c0:3 * RET_WIDTH + c0 + RET_DIM] = dg.astype(BF16)

    steps = nc // RET_STEP
    rev = lambda n: steps - 1 - n
    full = lambda shape: pl.BlockSpec(shape, lambda n: (0,) * len(shape))
    step = RET_STEP * BLK
    return pl.pallas_call(
        body, name="retention_bwd", grid=(steps,),
        in_specs=[pl.BlockSpec((step, 4 * RET_WIDTH), lambda n: (rev(n), 0)),
                  pl.BlockSpec((step, RET_WIDTH), lambda n: (rev(n), 0)),
                  pl.BlockSpec((RET_HEADS, RET_STEP, RET_DIM, RET_DIM), lambda n: (0, rev(n), 0, 0)),
                  pl.BlockSpec((step, RET_WIDTH), lambda n: (rev(n), 0)),
                  pl.BlockSpec((step, RET_DIM // 2), lambda n: (rev(n), 0)), pl.BlockSpec((step, RET_DIM // 2), lambda n: (rev(n), 0)),
                  full((RET_HEADS, BLK, BLK)), full((RET_HEADS, BLK, 1)), full((RET_HEADS, BLK, 1)), full((RET_HEADS, 1, 1))],
        out_specs=pl.BlockSpec((step, 4 * RET_WIDTH), lambda n: (rev(n), 0)),
        out_shape=jax.ShapeDtypeStruct((s, EVEN_IN), BF16),
        scratch_shapes=[pltpu.VMEM((RET_HEADS, RET_DIM, RET_DIM), F32)],
        compiler_params=_params("arbitrary"))(proj, o, states, dcat, cos, sin, intra, q_dec, k_dec, chunk_dec)


CONV_ROWS = 256
HALO = 16


def _conv_pieces(p, halo, first):
    gb, gc, u, gv = (p[:, i * CONV_WIDTH:(i + 1) * CONV_WIDTH] for i in range(4))
    cu = gc * u
    hcu = halo[:, CONV_WIDTH:2 * CONV_WIDTH] * halo[:, 2 * CONV_WIDTH:3 * CONV_WIDTH]
    hcu = jnp.where(first, 0.0, hcu)
    r1, r2 = hcu[HALO - 1:HALO], hcu[HALO - 2:HALO - 1]
    row = lax.broadcasted_iota(jnp.int32, cu.shape, 0)
    m1 = jnp.where(row == 0, r1, pltpu.roll(cu, 1, 0))
    m2 = jnp.where(row == 0, r2, jnp.where(row == 1, r1, pltpu.roll(cu, 2, 0)))
    return gb, gc, u, gv, cu, m1, m2


def _conv_fwd(proj, conv_w, cat):
    s = proj.shape[0]
    per = CONV_ROWS // HALO

    def body(p_ref, halo_ref, w_ref, cat_in, cat_ref):
        del cat_in
        first = pl.program_id(0) == 0
        gb, _, _, gv, cu, m1, m2 = _conv_pieces(p_ref[...].astype(F32), halo_ref[...].astype(F32), first)
        conv = w_ref[0:1, :] * m2 + w_ref[1:2, :] * m1 + w_ref[2:3, :] * cu
        cat_ref[...] = (gb * conv * (gv * _sigmoid(gv))).astype(BF16)

    return pl.pallas_call(
        body, name="conv_fwd", grid=(s // CONV_ROWS,),
        in_specs=[pl.BlockSpec((CONV_ROWS, 4 * CONV_WIDTH), lambda i: (i, 1)),
                  pl.BlockSpec((HALO, 4 * CONV_WIDTH), lambda i: (jnp.maximum(i * per - 1, 0), 1)),
                  pl.BlockSpec((3, CONV_WIDTH), lambda i: (0, 0)), ANY],
        out_specs=pl.BlockSpec((CONV_ROWS, CONV_WIDTH), lambda i: (i, 1)),
        out_shape=jax.ShapeDtypeStruct(cat.shape, cat.dtype), input_output_aliases={3: 0},
        compiler_params=_params("parallel"))(proj, proj, conv_w, cat)


def _conv_bwd(proj, dcat, conv_w, dproj):
    s = proj.shape[0]
    per = CONV_ROWS // HALO
    last_halo = s // HALO - 1
    nsteps = s // CONV_ROWS

    def body(p_ref, halo_ref, nxt_ref, dc_ref, dnxt_ref, w_ref, dp_in, dp_ref, dw_ref):
        del dp_in
        i = pl.program_id(0)
        gb, gc, u, gv, cu, m1, m2 = _conv_pieces(p_ref[...].astype(F32), halo_ref[...].astype(F32), i == 0)
        w0, w1, w2 = w_ref[0:1, :], w_ref[1:2, :], w_ref[2:3, :]
        conv = w0 * m2 + w1 * m1 + w2 * cu
        dco = dc_ref[...].astype(F32)
        sg = _sigmoid(gv)
        silu = gv * sg
        dgb = dco * conv * silu
        dco_gb = dco * gb
        dgv = dco_gb * conv * (sg * (1.0 + gv * (1.0 - sg)))
        dconv = dco_gb * silu
        nxt = nxt_ref[...].astype(F32)
        ngv = nxt[:, 3 * CONV_WIDTH:]
        dnext = dnxt_ref[...].astype(F32) * nxt[:, :CONV_WIDTH] * (ngv * _sigmoid(ngv))
        dnext = jnp.where(i == nsteps - 1, 0.0, dnext)
        n1, n2 = dnext[0:1], dnext[1:2]
        row = lax.broadcasted_iota(jnp.int32, dconv.shape, 0)
        p1 = jnp.where(row == CONV_ROWS - 1, n1, pltpu.roll(dconv, CONV_ROWS - 1, 0))
        p2 = jnp.where(row == CONV_ROWS - 1, n2, jnp.where(row == CONV_ROWS - 2, n1, pltpu.roll(dconv, CONV_ROWS - 2, 0)))
        dcu = w2 * dconv + w1 * p1 + w0 * p2
        dp_ref[...] = jnp.concatenate([dgb, dcu * u, dcu * gc, dgv], axis=1).astype(BF16)

        @pl.when(i == 0)
        def _():
            dw_ref[...] = jnp.zeros_like(dw_ref)

        taps = [jnp.sum(dconv * m, axis=0, keepdims=True) for m in (m2, m1, cu)]
        r8 = lax.broadcasted_iota(jnp.int32, dw_ref.shape, 0)
        dw_ref[...] += jnp.where(r8 == 0, taps[0], jnp.where(r8 == 1, taps[1], jnp.where(r8 == 2, taps[2], 0.0)))

    return pl.pallas_call(
        body, name="conv_bwd", grid=(nsteps,),
        in_specs=[pl.BlockSpec((CONV_ROWS, 4 * CONV_WIDTH), lambda i: (i, 1)),
                  pl.BlockSpec((HALO, 4 * CONV_WIDTH), lambda i: (jnp.maximum(i * per - 1, 0), 1)),
                  pl.BlockSpec((HALO, 4 * CONV_WIDTH), lambda i: (jnp.minimum((i + 1) * per, last_halo), 1)),
                  pl.BlockSpec((CONV_ROWS, CONV_WIDTH), lambda i: (i, 1)),
                  pl.BlockSpec((HALO, CONV_WIDTH), lambda i: (jnp.minimum((i + 1) * per, last_halo), 1)),
                  pl.BlockSpec((3, CONV_WIDTH), lambda i: (0, 0)), ANY],
        out_specs=[pl.BlockSpec((CONV_ROWS, 4 * CONV_WIDTH), lambda i: (i, 1)), pl.BlockSpec((8, CONV_WIDTH), lambda i: (0, 0))],
        out_shape=[jax.ShapeDtypeStruct(dproj.shape, dproj.dtype), jax.ShapeDtypeStruct((8, CONV_WIDTH), F32)],
        input_output_aliases={6: 0},
        compiler_params=_params("arbitrary"))(proj, proj, proj, dcat, dcat, conv_w, dproj)


GROUP_HEADS = Q_HEADS // KV_HEADS
GROUP_WIDTH = GROUP_HEADS * HEAD_DIM
SLAB = 2 * HEAD_DIM
KV_COL = ATTN_WIDTH // GROUP_WIDTH
GATE_COL = KV_COL + 1
ATTN_SCALE = HEAD_DIM ** -0.5
KV_ROWS = 1024


def _half_mask(shape, which):
    return (_lane(shape) // HEAD_DIM) == which


def _dup_head(slab, which):
    kept = jnp.where(_half_mask(slab.shape, which), slab, 0.0)
    return kept + pltpu.roll(kept, HEAD_DIM, 1)


def _stack_heads(x):
    parts = []
    for sl in range(GROUP_WIDTH // SLAB):
        slab = x[:, sl * SLAB:(sl + 1) * SLAB]
        parts += [jnp.where(_half_mask(slab.shape, e), slab, 0.0) for e in range(2)]
    return jnp.concatenate(parts, axis=0)


def _unstack_heads(y):
    slabs = []
    for sl in range(GROUP_WIDTH // SLAB):
        a, b = y[(2 * sl) * BLK:(2 * sl + 1) * BLK], y[(2 * sl + 1) * BLK:(2 * sl + 2) * BLK]
        slabs.append(jnp.where(_half_mask(a.shape, 0), a, b))
    return jnp.concatenate(slabs, axis=1)


def _q_prep(q, qw, cosf, sins, ones_bd):
    rstd = lax.rsqrt(_group_mean(q * q, ones_bd) + EPS)
    nrm = q * rstd
    y = nrm * qw
    return nrm, rstd, y * cosf + _partner64(y) * sins


def _band(tri_ref, n):
    own = tri_ref[...] > 0.5
    return own, jnp.where(jnp.logical_and(n == 0, jnp.logical_not(own)), -1e30, 0.0)


def _fold(pair, own):
    return jnp.where(own, pair[:, BLK:], pair[:, :BLK])


def _unfold(folded, own):
    return jnp.concatenate([jnp.where(own, 0.0, folded), jnp.where(own, folded, 0.0)], axis=1)


def _head_probs(raw_scores, sink, own, bias):
    sc = _fold(raw_scores, own) + bias
    m = jnp.maximum(jnp.max(sc, axis=-1, keepdims=True), sink)
    p = jnp.exp(sc - m)
    psink = jnp.exp(sink - m)
    inv = 1.0 / (jnp.sum(p, axis=-1, keepdims=True) + psink)
    return p * inv, psink * inv


def _k_prep(k, kw, cosf, sins, ones_bd):
    rstd = lax.rsqrt(_group_mean(k * k, ones_bd) + EPS)
    nrm = k * rstd
    y = nrm * kw
    return nrm, rstd, y * cosf + _partner64(y) * sins


def _qk_prep(proj, qw, kw, cos, sins, ones_q, ones_kv):
    s = proj.shape[0]
    rows = min(KV_ROWS, s)

    def body(p_ref, qw_ref, kw_ref, cos_ref, sin_ref, oq_ref, ok_ref, o_ref):
        j = pl.program_id(1)

        @pl.when(j < KV_COL)
        def _():
            cosf, sinf = _tile_lanes(cos_ref[...], 4), _tile_lanes(sin_ref[...], 4)
            roped = _q_prep(p_ref[...].astype(F32), qw_ref[...], cosf, sinf, oq_ref[...])[2]
            o_ref[...] = (roped * ATTN_SCALE).astype(BF16)

        @pl.when(j == KV_COL)
        def _():
            cosf, sinf = _tile_lanes(cos_ref[...], 2), _tile_lanes(sin_ref[...], 2)
            kr = _k_prep(p_ref[:, :KV_WIDTH].astype(F32), kw_ref[...], cosf, sinf, ok_ref[...])[2]
            o_ref[...] = jnp.concatenate([kr.astype(BF16), p_ref[:, KV_WIDTH:]], axis=1)

    full = lambda shape: pl.BlockSpec(shape, lambda i, j: (0,) * len(shape))
    tab = pl.BlockSpec((rows, SLAB), lambda i, j: (i, 0))
    blk = pl.BlockSpec((rows, GROUP_WIDTH), lambda i, j: (i, j))
    return pl.pallas_call(
        body, name="qk_prep", grid=(s // rows, KV_COL + 1),
        in_specs=[blk, full((1, GROUP_WIDTH)), full((1, KV_WIDTH)), tab, tab, full(ones_q.shape), full(ones_kv.shape)],
        out_specs=blk, out_shape=jax.ShapeDtypeStruct((s, ATTN_WIDTH + 2 * KV_WIDTH), BF16),
        compiler_params=_params("parallel", "arbitrary"))(proj, qw, kw, cos, sins, ones_q, ones_kv)


def _keys_values(kc_ref, kp_ref, vc_ref, vp_ref, head):
    lanes = slice((head // 2) * SLAB, (head // 2 + 1) * SLAB)
    dup = lambda ref: _dup_head(ref[:, lanes].astype(F32), head % 2)
    return (jnp.concatenate([dup(kp_ref), dup(kc_ref)], axis=0).astype(BF16),
            jnp.concatenate([dup(vp_ref), dup(vc_ref)], axis=0).astype(BF16))


FWD_STEP_HEADS = 4
BWD_STEP_HEADS = 2


def _swa_specs(heads):
    kv_width = heads * HEAD_DIM
    prev = lambda n: jnp.maximum(n - 1, 0)
    kv = lambda col0, row: pl.BlockSpec((BLK, kv_width), lambda gs, n: (row(n), col0 // kv_width + gs))
    cur = lambda n: n
    full = lambda shape: pl.BlockSpec(shape, lambda gs, n: (0,) * len(shape))
    gate = lambda t: pl.BlockSpec((BLK, GROUP_WIDTH), lambda gs, n: (n, GATE_COL + heads * gs + t))
    return dict(
        sinks=pl.BlockSpec(memory_space=pltpu.SMEM), gates=[gate(t) for t in range(heads)],
        kc=kv(ATTN_WIDTH, cur), kp=kv(ATTN_WIDTH, prev), vc=kv(ATTN_WIDTH + KV_WIDTH, cur), vp=kv(ATTN_WIDTH + KV_WIDTH, prev),
        tri=full((BLK, BLK)), step=pl.BlockSpec((BLK, heads * GROUP_WIDTH), lambda gs, n: (n, gs)))


def _lower_triangle():
    return jnp.tril(jnp.ones((BLK, BLK), F32))


def _swa_fwd(qk, proj, sinks):
    s = proj.shape[0]
    nb = s // BLK
    heads = FWD_STEP_HEADS
    sp = _swa_specs(heads)

    def body(sink_ref, q_ref, kc_ref, kp_ref, vc_ref, vp_ref, *rest):
        gate_refs, (tri_ref, ag_ref, o_ref) = rest[:heads], rest[heads:]
        gs, n = pl.program_id(0), pl.program_id(1)
        own, bias = _band(tri_ref, n)
        for t, gate_ref in enumerate(gate_refs):
            cols = slice(t * GROUP_WIDTH, (t + 1) * GROUP_WIDTH)
            first_head = (heads * gs + t) * GROUP_HEADS
            kcat, vcat = _keys_values(kc_ref, kp_ref, vc_ref, vp_ref, t)
            scores = _dot_nt(_stack_heads(q_ref[:, cols]), kcat)
            probs = []
            for j in range(GROUP_HEADS):
                p, _ = _head_probs(scores[j * BLK:(j + 1) * BLK], sink_ref[first_head + j], own, bias)
                probs.append(_unfold(p, own).astype(BF16))
            o = _unstack_heads(_dot(jnp.concatenate(probs, axis=0), vcat))
            gate = gate_ref[...].astype(F32)
            o_ref[:, cols] = o.astype(BF16)
            ag_ref[:, cols] = (o * (gate * _sigmoid(gate))).astype(BF16)

    shp = jax.ShapeDtypeStruct((s, ATTN_WIDTH), BF16)
    return pl.pallas_call(
        body, name="swa_fwd", grid=(KV_HEADS // heads, nb),
        in_specs=[sp["sinks"], sp["step"], sp["kc"], sp["kp"], sp["vc"], sp["vp"], *sp["gates"], sp["tri"]],
        out_specs=[sp["step"], sp["step"]], out_shape=[shp, shp],
        compiler_params=_params("parallel", "arbitrary"),
    )(sinks, qk, qk, qk, qk, qk, *[proj] * heads, _lower_triangle())


def _swa_bwd(qk, proj, dag, sinks):
    s = proj.shape[0]
    nb = s // BLK
    heads = BWD_STEP_HEADS
    sp = _swa_specs(heads)

    def body(sink_ref, q_ref, kc_ref, kp_ref, vc_ref, vp_ref, *rest):
        gate_refs = rest[:heads]
        dag_ref, tri_ref, dq_ref, dkc_ref, dkp_ref, dvc_ref, dvp_ref, dsink_ref = rest[heads:]
        gs, n = pl.program_id(0), pl.program_id(1)
        own, bias = _band(tri_ref, n)

        @pl.when(n == 0)
        def _():
            dsink_ref[...] = jnp.zeros_like(dsink_ref)

        for t, gate_ref in enumerate(gate_refs):
            cols = slice(t * GROUP_WIDTH, (t + 1) * GROUP_WIDTH)
            first_head = (heads * gs + t) * GROUP_HEADS
            kcat, vcat = _keys_values(kc_ref, kp_ref, vc_ref, vp_ref, t)
            gate = gate_ref[...].astype(F32)
            do = dag_ref[:, cols].astype(F32) * (gate * _sigmoid(gate))
            q_stack = _stack_heads(q_ref[:, cols])
            do_stack = _stack_heads(do).astype(BF16)
            scores = _dot_nt(q_stack, kcat)
            dprobs = _dot_nt(do_stack, vcat)
            probs, dscores, dsinks = [], [], []
            for j in range(GROUP_HEADS):
                rows = slice(j * BLK, (j + 1) * BLK)
                p, psink = _head_probs(scores[rows], sink_ref[first_head + j], own, bias)
                dp = _fold(dprobs[rows], own)
                delta = jnp.sum(p * dp, axis=-1, keepdims=True)
                probs.append(_unfold(p, own).astype(BF16))
                dscores.append(_unfold(p * (dp - delta), own).astype(BF16))
                dsinks.append(-jnp.sum(psink * delta, axis=0, keepdims=True))
            ds = jnp.concatenate(dscores, axis=0)
            dk = _dot_tn(ds, q_stack)
            dv = _dot_tn(jnp.concatenate(probs, axis=0), do_stack)
            dk = dk + pltpu.roll(dk, HEAD_DIM, 1)
            dv = dv + pltpu.roll(dv, HEAD_DIM, 1)
            dkp_ref[t], dkc_ref[t] = dk[:BLK], dk[BLK:]
            dvp_ref[t], dvc_ref[t] = dv[:BLK], dv[BLK:]
            dq_ref[:, cols] = _unstack_heads(_dot(ds, kcat)).astype(BF16)
            r8 = lax.broadcasted_iota(jnp.int32, (8, SLAB), 0)
            upd = jnp.zeros((8, SLAB), F32)
            for j in range(GROUP_HEADS):
                upd = jnp.where(r8 == j, dsinks[j], upd)
            dsink_ref[t] += upd

    cur_out = pl.BlockSpec((heads, BLK, SLAB), lambda gs, n: (gs, n, 0))
    prev_out = pl.BlockSpec((heads, BLK, SLAB), lambda gs, n: (gs, (n + nb - 1) % nb, 0))
    kv_shape = jax.ShapeDtypeStruct((KV_HEADS, s, SLAB), F32)
    return pl.pallas_call(
        body, name="swa_bwd", grid=(KV_HEADS // heads, nb),
        in_specs=[sp["sinks"], sp["step"], sp["kc"], sp["kp"], sp["vc"], sp["vp"], *sp["gates"], sp["step"], sp["tri"]],
        out_specs=[sp["step"], cur_out, prev_out, cur_out, prev_out,
                   pl.BlockSpec((heads, 8, SLAB), lambda gs, n: (gs, 0, 0))],
        out_shape=[jax.ShapeDtypeStruct((s, ATTN_WIDTH), BF16), kv_shape, kv_shape, kv_shape, kv_shape,
                   jax.ShapeDtypeStruct((KV_HEADS, 8, SLAB), F32)],
        compiler_params=_params("parallel", "arbitrary"),
    )(sinks, qk, qk, qk, qk, qk, *[proj] * heads, dag, _lower_triangle())


def _swa_bwd_finish(proj, dqr, o, dag, dkc, dkp, dvc, dvp, qw, kw, cos, sins, ones_q, ones_kv):
    s = proj.shape[0]
    rows = min(KV_ROWS, s)
    n_q = KV_COL
    q_of = lambda j: jnp.clip(j - 1, 0, n_q - 1)
    gate_of = lambda j: jnp.clip(j - 1 - n_q, 0, n_q - 1)

    def body(kv_ref, p_ref, dqr_ref, o_ref, dag_ref, dkc_ref, dkp_ref, dvc_ref, dvp_ref, qw_ref, kw_ref, cos_ref, sin_ref,
             oq_ref, ok_ref, dp_ref, dqw_ref, dkw_ref):
        j, i = pl.program_id(0), pl.program_id(1)

        @pl.when(j == 0)
        def _():
            @pl.when(i == 0)
            def _():
                dkw_ref[...] = jnp.zeros_like(dkw_ref)

            def assemble(cur_ref, prv_ref):
                tot = [cur_ref[h] + prv_ref[h] for h in range(KV_HEADS)]
                first = _half_mask(tot[0].shape, 0)
                return jnp.concatenate([jnp.where(first, tot[0], tot[1]), jnp.where(first, tot[2], tot[3])], axis=1)

            dkr = assemble(dkc_ref, dkp_ref)
            dv = assemble(dvc_ref, dvp_ref)
            cosf, sinf = _tile_lanes(cos_ref[...], 2), _tile_lanes(sin_ref[...], 2)
            nrm, rstd, _ = _k_prep(kv_ref[:, :KV_WIDTH].astype(F32), kw_ref[...], cosf, sinf, ok_ref[...])
            dy = dkr * cosf + _partner64(dkr * sinf)
            dn = dy * kw_ref[...]
            dk = rstd * (dn - nrm * _group_mean(dn * nrm, ok_ref[...], passes=1))
            dp_ref[...] = jnp.concatenate([dk, dv], axis=1).astype(BF16)
            dkw_ref[...] += jnp.sum(dy * nrm, axis=0, keepdims=True)

        @pl.when(jnp.logical_and(j >= 1, j <= n_q))
        def _():
            @pl.when(i == 0)
            def _():
                dqw_ref[...] = jnp.zeros_like(dqw_ref)

            cosf, sinf = _tile_lanes(cos_ref[...], 4), _tile_lanes(sin_ref[...], 4)
            nrm, rstd, _ = _q_prep(p_ref[...].astype(F32), qw_ref[...], cosf, sinf, oq_ref[...])
            dq = dqr_ref[...].astype(F32) * ATTN_SCALE
            dy = dq * cosf + _partner64(dq * sinf)
            dn = dy * qw_ref[...]
            dp_ref[...] = (rstd * (dn - nrm * _group_mean(dn * nrm, oq_ref[...], passes=1))).astype(BF16)
            dqw_ref[0] += jnp.sum(dy * nrm, axis=0, keepdims=True)

        @pl.when(j > n_q)
        def _():
            gate = p_ref[...].astype(F32)
            sg = _sigmoid(gate)
            dp_ref[...] = (dag_ref[...].astype(F32) * o_ref[...].astype(F32) * (sg * (1.0 + gate * (1.0 - sg)))).astype(BF16)

    first_pass = lambda j, i: jnp.where(j == 0, i, 0)
    acc = pl.BlockSpec((KV_HEADS, rows, SLAB), lambda j, i: (0, first_pass(j, i), 0))
    full = lambda shape: pl.BlockSpec(shape, lambda j, i: (0,) * len(shape))
    tab = pl.BlockSpec((rows, SLAB), lambda j, i: (i, 0))
    out_col = lambda j: jnp.where(j == 0, KV_COL, jnp.where(j <= n_q, j - 1, j))
    return pl.pallas_call(
        body, name="swa_bwd_finish", grid=(2 * n_q + 1, s // rows),
        in_specs=[pl.BlockSpec((rows, GROUP_WIDTH), lambda j, i: (first_pass(j, i), KV_COL)),
                  pl.BlockSpec((rows, GROUP_WIDTH), lambda j, i: (jnp.where(j == 0, 0, i), jnp.where(j <= n_q, q_of(j), j))),
                  pl.BlockSpec((rows, GROUP_WIDTH), lambda j, i: (jnp.where(jnp.logical_and(j >= 1, j <= n_q), i, 0), q_of(j))),
                  pl.BlockSpec((rows, GROUP_WIDTH), lambda j, i: (jnp.where(j > n_q, i, 0), gate_of(j))),
                  pl.BlockSpec((rows, GROUP_WIDTH), lambda j, i: (jnp.where(j > n_q, i, 0), gate_of(j))),
                  acc, acc, acc, acc, full((1, GROUP_WIDTH)), full((1, KV_WIDTH)), tab, tab,
                  full(ones_q.shape), full(ones_kv.shape)],
        out_specs=[pl.BlockSpec((rows, GROUP_WIDTH), lambda j, i: (i, out_col(j))),
                   pl.BlockSpec((1, 8, GROUP_WIDTH), lambda j, i: (q_of(j), 0, 0)), pl.BlockSpec((8, KV_WIDTH), lambda j, i: (0, 0))],
        out_shape=[jax.ShapeDtypeStruct((s, ODD_IN), BF16), jax.ShapeDtypeStruct((n_q, 8, GROUP_WIDTH), F32),
                   jax.ShapeDtypeStruct((8, KV_WIDTH), F32)],
        compiler_params=_params("arbitrary", "arbitrary"),
    )(proj, proj, dqr, o, dag, dkc, dkp, dvc, dvp, qw, kw, cos, sins, ones_q, ones_kv)


def _place():
    x, y, c = lax.axis_index("x"), lax.axis_index("y"), lax.axis_index("c")
    return x, y, c


OTHER_CHIPS = ((1, 0), (0, 1), (1, 1))


def _half_rows(ref, half, rows):
    return ref.at[pl.ds(pl.multiple_of(half * (rows // 2), 8), rows // 2)]


DMA_CHUNK_BYTES = 1 << 20
BF16_TILE_ROWS = 16


def _n_chunks(ref):
    rows = ref.shape[-2]
    nbytes = math.prod(ref.shape) * jnp.dtype(ref.dtype).itemsize
    n = 1
    while 2 * n * DMA_CHUNK_BYTES <= nbytes and rows % (2 * n * BF16_TILE_ROWS) == 0:
        n *= 2
    return n


def _row_chunk(ref, k, n):
    rows = ref.shape[-2] // n
    return ref.at[pl.ds(k * rows, rows)] if len(ref.shape) == 2 else ref.at[:, pl.ds(k * rows, rows)]


def _push(src, dst, send_sem, recv_sem, device_id):
    n = _n_chunks(src)
    for k in range(n):
        pltpu.make_async_remote_copy(src_ref=_row_chunk(src, k, n), dst_ref=_row_chunk(dst, k, n), send_sem=send_sem,
                                     recv_sem=recv_sem, device_id=device_id, device_id_type=MESH).start()
    return pltpu.make_async_remote_copy(src_ref=src, dst_ref=dst, send_sem=send_sem, recv_sem=recv_sem,
                                        device_id=device_id, device_id_type=MESH)


HBM = pl.BlockSpec(memory_space=pltpu.HBM)
SEM = pl.BlockSpec(memory_space=pltpu.SEMAPHORE)
SPLIT_COPY_EFFECT = pltpu.SideEffectType.DATAFLOW_SIDE_EFFECTING


def _in_hbm(a):
    return pltpu.with_memory_space_constraint(a, pltpu.HBM)


def _start_copies(name, arrays, plan, n_copies, after=None):
    n = len(arrays)

    def body(*refs):
        send_sem, recv_sem = refs[-n - 3], refs[-n - 2]
        for k, (src, dst, peer) in enumerate(plan(refs[:n])):
            _push(src, dst, send_sem.at[k], recv_sem.at[k], peer)
        refs[-1][...] = jnp.zeros_like(refs[-1])

    dma = pltpu.SemaphoreType.DMA((n_copies,))
    outs = pl.pallas_call(
        body, name=name,
        out_shape=(dma, dma, *[pltpu.HBM(a.shape, a.dtype) for a in arrays], jax.ShapeDtypeStruct((8, 128), F32)),
        in_specs=[HBM] * n + ([ANY] if after is not None else []),
        out_specs=(SEM, SEM, *[HBM] * n, pl.BlockSpec(memory_space=pltpu.VMEM)),
        input_output_aliases={i: i + 2 for i in range(n)},
        compiler_params=pltpu.CompilerParams(has_side_effects=SPLIT_COPY_EFFECT),
    )(*[_in_hbm(a) for a in arrays], *((after,) if after is not None else ()))
    return outs[0], outs[1], list(outs[2:2 + n]), outs[-1]


def _wait_copies(name, send_sem, recv_sem, arrays, plan, after):
    n = len(arrays)
    after = list(after) if isinstance(after, (list, tuple)) else [after]

    def body(*refs):
        send_ref, recv_ref = refs[n], refs[n + 1]
        for k, (src, dst, peer) in enumerate(plan(refs[:n])):
            cp = pltpu.make_async_remote_copy(src_ref=src, dst_ref=dst, send_sem=send_ref.at[k], recv_sem=recv_ref.at[k],
                                              device_id=peer, device_id_type=MESH)
            cp.wait_send()
            cp.wait_recv()

    return list(pl.pallas_call(
        body, name=name, out_shape=tuple(pltpu.HBM(a.shape, a.dtype) for a in arrays),
        in_specs=[HBM] * n + [SEM, SEM] + [ANY] * len(after), out_specs=tuple([HBM] * n),
        input_output_aliases={i: i for i in range(n)},
        compiler_params=pltpu.CompilerParams(has_side_effects=SPLIT_COPY_EFFECT),
    )(*arrays, send_sem, recv_sem, *after))


def _gather_region(full, kind, chip, half=None):
    if kind == "whole":
        return full.at[chip]
    if kind == "col" or isinstance(kind, tuple):
        part, parts = (0, 1) if kind == "col" else kind[1:]
        rows, width = full.shape[0], full.shape[1] // N_CHIPS
        piece = full.at[:, pl.ds(pl.multiple_of(chip * width + part * (width // parts), LANE_TILE), width // parts)]
    else:
        rows = full.shape[0] // N_CHIPS
        piece = full.at[pl.ds(pl.multiple_of(chip * rows, BF16_TILE_ROWS), rows)]
    return piece if half is None else _half_rows(piece, half, rows)


def _gather_plan(kinds):
    def plan(fulls):
        x, y, c = _place()
        copies = []
        for fx, fy in OTHER_CHIPS:
            for full, kind in zip(fulls, kinds):
                mine = _gather_region(full, kind, 2 * x + y, c)
                copies.append((mine, mine, (x ^ fx, y ^ fy, c)))
        return copies

    return plan


def _gather_start(name, fulls, kinds, after=None):
    return _start_copies(name, list(fulls), _gather_plan(kinds), 3 * len(kinds), after)


def _pass_on_plan(kinds):
    split = [i for i, kind in enumerate(kinds) if kind != "whole"]

    def plan(fulls):
        x, y, c = _place()
        copies = []
        for fx, fy in OTHER_CHIPS:
            chip = 2 * (x ^ fx) + (y ^ fy)
            for i in split:
                landed = _gather_region(fulls[i], kinds[i], chip, c)
                copies.append((landed, landed, (x, y, 1 - c)))
        return copies

    return plan, 3 * len(split)


def _gather_arrive(name, started, kinds, after, arrays=None):
    send_sem, recv_sem, fulls, _ = started
    fulls = _wait_copies(name + "_wait", send_sem, recv_sem, fulls if arrays is None else arrays, _gather_plan(kinds), after)
    plan, count = _pass_on_plan(kinds)
    return _start_copies(name + "_pass_on", fulls, plan, count)


def _gather_done(name, passed_on, kinds, after):
    send_sem, recv_sem, fulls, _ = passed_on
    return _wait_copies(name + "_pass_on_wait", send_sem, recv_sem, fulls, _pass_on_plan(kinds)[0], after)


def _allreduce_small(v):
    def body(v_ref, out_ref, buf, send_sems, recv_sems):
        x, y, c = _place()
        me = 4 * x + 2 * y + c
        buf[me] = v_ref[...]
        copies = []
        for r in range(1, N_DEV):
            peer = (x ^ (r >> 2), y ^ ((r >> 1) & 1), c ^ (r & 1))
            cp = pltpu.make_async_remote_copy(src_ref=v_ref, dst_ref=buf.at[me], send_sem=send_sems.at[r - 1],
                                              recv_sem=recv_sems.at[r - 1], device_id=peer, device_id_type=MESH)
            cp.start()
            copies.append(cp)
        for cp in copies:
            cp.wait_recv()
        for cp in copies:
            cp.wait_send()
        total = buf[0]
        for d in range(1, N_DEV):
            total = total + buf[d]
        out_ref[...] = total

    vm = pl.BlockSpec(memory_space=pltpu.VMEM)
    return pl.pallas_call(
        body, name="allreduce_small", in_specs=[vm], out_specs=vm, out_shape=jax.ShapeDtypeStruct(v.shape, v.dtype),
        scratch_shapes=[pltpu.VMEM((N_DEV,) + v.shape, v.dtype), pltpu.SemaphoreType.DMA((N_DEV - 1,)),
                        pltpu.SemaphoreType.DMA((N_DEV - 1,))],
        compiler_params=pltpu.CompilerParams(has_side_effects=True),
    )(v)


def _exchange_plan(n):
    def plan(refs):
        x, y, c = _place()
        copies = []
        for g, theirs in zip(refs[:n], refs[n:]):
            half = g.shape[1] // 2
            src = g.at[:, pl.ds(pl.multiple_of((1 - c) * half, BF16_TILE_ROWS), half)]
            copies.append((src, theirs, (x, y, 1 - c)))
        return copies

    return plan


def _pair_sum(g, theirs, core, name):
    pieces, half, cols = theirs.shape
    tr = min(half, 256)
    per = half // tr

    def body(core_ref, g_ref, t_ref, o_ref):
        del core_ref
        o_ref[...] = (g_ref[...].astype(F32) + t_ref[...].astype(F32)).astype(BF16)

    spec = pl.BlockSpec((1, tr, cols), lambda p, i, core_ref: (p, i, 0))
    return pl.pallas_call(
        body, name=name, out_shape=jax.ShapeDtypeStruct(theirs.shape, BF16),
        grid_spec=pltpu.PrefetchScalarGridSpec(
            num_scalar_prefetch=1, grid=(pieces, per),
            in_specs=[pl.BlockSpec((1, tr, cols), lambda p, i, core_ref: (p, core_ref[0] * per + i, 0)), spec],
            out_specs=spec),
        compiler_params=_params("parallel", "parallel"))(core, g, theirs)


def _scatter_plan(n):
    def plan(refs):
        parts, stacks = refs[:n], refs[n:]
        x, y, c = _place()
        copies = []
        for fx, fy in OTHER_CHIPS:
            chip = 2 * (x ^ fx) + (y ^ fy)
            for part, stack in zip(parts, stacks):
                if part.shape[0] == N_CHIPS:
                    piece = part.at[chip]
                else:
                    width = part.shape[2] // N_CHIPS
                    piece = part.at[0].at[:, pl.ds(pl.multiple_of(chip * width, 128), width)]
                copies.append((piece, stack.at[2 * x + y], (x ^ fx, y ^ fy, c)))
        return copies

    return plan


def _scatter_start(name, parts, after=None):
    def landing(a):
        return (N_CHIPS, a.shape[1], a.shape[2] if a.shape[0] == N_CHIPS else a.shape[2] // N_CHIPS)

    stacks = [lax.empty(landing(a), a.dtype) for a in parts]
    return _start_copies(name, list(parts) + stacks, _scatter_plan(len(parts)), 3 * len(parts), after)


def _scatter_finish(name, started, after):
    send_sem, recv_sem, arrays, _ = started
    n = len(arrays) // 2
    arrays = _wait_copies(name + "_wait", send_sem, recv_sem, arrays, _scatter_plan(n), after)
    return arrays[:n], arrays[n:]


def _sum_chips(part, stack, place, name):
    _, r, c = stack.shape
    tr = 256
    per = r // tr

    def body(place_ref, own_ref, a_ref, b_ref, c_ref, o_ref):
        del place_ref
        total = own_ref[0].astype(F32)
        for ref in (a_ref, b_ref, c_ref):
            total = total + ref[0].astype(F32)
        o_ref[...] = total

    if part.shape[0] == N_CHIPS:
        own = pl.BlockSpec((1, tr, c), lambda i, pr: (pr[1], i, 0))
    else:
        own = pl.BlockSpec((1, tr, c), lambda i, pr: (0, i, pr[1]))
    other = lambda flip: pl.BlockSpec((1, tr, c), lambda i, pr: (pr[1] ^ flip, i, 0))
    return pl.pallas_call(
        body, name=name, out_shape=jax.ShapeDtypeStruct((2 * r, c), F32),
        grid_spec=pltpu.PrefetchScalarGridSpec(
            num_scalar_prefetch=1, grid=(per,), in_specs=[own, other(2), other(1), other(3)],
            out_specs=pl.BlockSpec((tr, c), lambda i, pr: (pr[0] * per + i, 0))),
        compiler_params=_params("parallel"))(place, part, stack, stack, stack)


def _own_half_plan(fulls):
    x, y, c = _place()
    return [(mine, mine, (x, y, 1 - c)) for mine in (_half_rows(full, c, full.shape[0]) for full in fulls)]


MM = dict(tm=2048, tn=1024, tk=2048)
MM_LONG_K = dict(tm=1024, tn=1024, tk=4096)


def _local_step(x, target, ev_norm_w, q_norm_w, k_norm_w, sinks, own_first, weights_first, late_arrived, weights_late, grads_out):
    s = x.shape[0]
    cos_r, sin_r = _rope_tables(s, RET_DIM)
    cos_a, sin_a = _rope_tables(s, HEAD_DIM)
    cos_a = np.tile(cos_a, (1, 4))
    sins_a = np.tile(np.concatenate([-sin_a, sin_a], axis=1), (1, 2))
    tables = _retention_tables()
    ones_q, ones_kv = _block_diag_mean(min(GROUP_WIDTH, MXU_WIDTH)), _block_diag_mean(min(KV_WIDTH, MXU_WIDTH))
    qw_g = jnp.tile(q_norm_w, (1, GROUP_WIDTH // HEAD_DIM))
    kw_kv = jnp.tile(k_norm_w, (1, KV_WIDTH // HEAD_DIM))
    sinks1 = sinks.reshape(Q_HEADS)

    own_w_in0, own_block, start_token = own_first
    h0 = _rmsnorm(x, ev_norm_w, "norm0", after=start_token)
    shifted = dict(shift=own_block * (own_w_in0.shape[1] // MM["tn"]), total=EVEN_IN // MM["tn"], tm=MM["tm"], tn=MM["tn"],
                   out_dtype=BF16)
    own_blocks = own_w_in0.shape[1] // MM["tn"]
    proj0 = _mm_shifted(h0, own_w_in0, b_shifted=False, first=0, count=own_blocks, name="proj0_own", **shifted)
    for part in range(own_blocks):
        w_in0, small_weights, token = weights_first(part, proj0)
        if small_weights is not None:
            conv_w, od_norm_w = small_weights
        proj0 = _mm_shifted(h0, w_in0, b_shifted=True, first=own_blocks + part, stride=own_blocks,
                            count=shifted["total"] // own_blocks - 1, name=f"proj0_rest{part}", into=proj0, after=token, **shifted)
    o_ret, cat, states = _retention_fwd(proj0, cos_r, sin_r, tables, after=late_arrived(proj0))
    cat = _conv_fwd(proj0, conv_w, cat)
    w_out0, w_in1, w_out1 = weights_late(cat)

    def residual_and_norm(prod, x_ref, w_ref, x1_ref, h1_ref):
        x1v = x_ref[...] + prod
        x1_ref[...] = x1v
        rstd = lax.rsqrt(jnp.mean(x1v * x1v, axis=-1, keepdims=True) + EPS)
        h1_ref[...] = (x1v * rstd * w_ref[...]).astype(BF16)

    def residual_and_loss(prod, x1_ref, t_ref, dyb_ref, sq_ref):
        diff = (x1_ref[...] + prod) - t_ref[...]
        dyb_ref[...] = (diff * (1.0 / D_MODEL)).astype(BF16)

        @pl.when(pl.program_id(0) == 0)
        def _():
            sq_ref[...] = jnp.zeros_like(sq_ref)

        sq_ref[...] += jnp.sum(jnp.sum(diff * diff, axis=1, keepdims=True), axis=0, keepdims=True)

    act = lambda dt: jax.ShapeDtypeStruct((s, D_MODEL), dt)
    x1, h1 = _mm_rows(cat, w_out0, [x], [od_norm_w], [act(F32), act(BF16)], residual_and_norm, tm=min(s, 512), name="out0")
    proj1 = _mm(h1, w_in1, mode="nn", out_dtype=BF16, name="proj1", tm=2048, tn=1536, tk=2048)
    qk = _qk_prep(proj1, qw_g, kw_kv, cos_a, sins_a, ones_q, ones_kv)
    ag, o_att = _swa_fwd(qk, proj1, sinks1)
    dy_b, sq = _mm_rows(ag, w_out1, [x1, target], [], [act(BF16), jax.ShapeDtypeStruct((8, 128), F32)],
                        residual_and_loss, tm=min(s, 512), name="out1")

    g_w_out1 = _mm(ag, dy_b, mode="tn", out_dtype=BF16, name="g_w_out1", **MM_LONG_K)
    dag = _mm(dy_b, w_out1, mode="nt", out_dtype=BF16, name="d_ag", **MM)
    dqr, dkc, dkp, dvc, dvp, dsink = _swa_bwd(qk, proj1, dag, sinks1)
    dproj1, dqw, dkw = _swa_bwd_finish(proj1, dqr, o_att, dag, dkc, dkp, dvc, dvp, qw_g, kw_kv, cos_a, sins_a, ones_q, ones_kv)
    g_w_in1 = _mm(h1, dproj1, mode="tn", out_dtype=BF16, name="g_w_in1", tm=1024, tn=768, tk=4096)
    dh1 = _mm(dproj1, w_in1, mode="nt", out_dtype=BF16, name="d_h1", tm=1024, tn=1024, tk=ODD_IN)
    dx1_b, g_norm1 = _rmsnorm_bwd(x1, od_norm_w, dh1, dy_b, "norm1_bwd", BF16)

    g_w_out0 = _mm(cat, dx1_b, mode="tn", out_dtype=BF16, name="g_w_out0", **MM_LONG_K)
    token = grads_out.begin("upper", (("od_w_in", g_w_in1, "col"), ("od_w_out", g_w_out1, "row"), ("ev_w_out", g_w_out0, "row")))
    dcat = _mm(dx1_b, w_out0, mode="nt", out_dtype=BF16, name="d_cat", after=token, **MM)
    token = grads_out.send("upper", dcat)
    dproj0 = _retention_bwd(proj0, o_ret, states, dcat, cos_r, sin_r, tables)
    dproj0, g_conv = _conv_bwd(proj0, dcat, conv_w, dproj0)
    g_w_in0 = _mm(h0, dproj0, mode="tn", out_dtype=BF16, name="g_w_in0", after=token, **MM_LONG_K)
    token = grads_out.begin("in0", (("ev_w_in", g_w_in0, "col"),))
    half_blocks = s // (2 * MM_LONG_K["tm"])
    dh0 = _mm(dproj0, w_in0, mode="nt", out_dtype=BF16, name="d_h0_top", after=token, m_blocks=(0, half_blocks), **MM_LONG_K)
    dh0 = _mm(dproj0, w_in0, mode="nt", out_dtype=BF16, name="d_h0_bottom", after=grads_out.send("in0", dh0),
              m_blocks=(half_blocks, half_blocks), into=dh0, **MM_LONG_K)
    grad_x, g_norm0 = _rmsnorm_bwd(x, ev_norm_w, dh0, dx1_b, "norm0_bwd", F32)

    g_qw = dqw[:, 0, :].reshape(Q_HEADS, HEAD_DIM).sum(axis=0)
    g_kw = dkw[0].reshape(KV_HEADS, HEAD_DIM).sum(axis=0)
    g_sinks = dsink[:, :, 0].reshape(Q_HEADS)
    small = dict(ev_norm=g_norm0[0], od_norm=g_norm1[0], conv=g_conv[:3], qw=g_qw, kw=g_kw, sinks=g_sinks)
    return sq[0, 0], grad_x, small


LANE_TILE = 128


def _pack_small_grads(small, sq):
    pad = lambda v, n: jnp.pad(v, (0, n - v.shape[0]))
    tail = jnp.concatenate([pad(small["qw"], LANE_TILE), pad(small["kw"], LANE_TILE), small["sinks"]])
    rows = [small["ev_norm"], small["od_norm"]] + [small["conv"][t] for t in range(3)] + [tail, sq.reshape(1)]
    rows += [jnp.zeros((1,), F32)] * (8 - len(rows))
    return jnp.stack([pad(r, D_MODEL) for r in rows])


def _adamw_small(tot, chip, params):
    n = len(params)

    def body(chip_ref, tot_ref, *refs):
        ins, outs = refs[:3 * n], refs[3 * n:]
        c = chip_ref[0]

        def own(rows, width):
            blocks = [tot_ref[rows, k * width:(k + 1) * width] for k in range(N_CHIPS)]
            g = blocks[-1]
            for k in reversed(range(N_CHIPS - 1)):
                g = jnp.where(c == k, blocks[k], g)
            return g

        grads = [tot_ref[0:1, :], own(slice(2, 5), CONV_WIDTH // N_CHIPS), own(slice(1, 2), D_MODEL // N_CHIPS),
                 tot_ref[5:6, 0:HEAD_DIM], tot_ref[5:6, LANE_TILE:LANE_TILE + HEAD_DIM],
                 tot_ref[5:6, 2 * LANE_TILE:2 * LANE_TILE + Q_HEADS]]
        for i, g in enumerate(grads):
            w_ref, m_ref, v_ref = ins[3 * i:3 * i + 3]
            g_out, d_out, nm_out, nv_out = outs[4 * i:4 * i + 4]
            at = (0,) if len(w_ref.shape) == 3 else ()
            delta, nm, nv = _adamw_math(w_ref[at] if at else w_ref[...], g, m_ref[at] if at else m_ref[...],
                                        v_ref[at] if at else v_ref[...])
            for ref, val in ((g_out, g), (d_out, delta), (nm_out, nm), (nv_out, nv)):
                if at:
                    ref[0] = val
                else:
                    ref[...] = val

    vm = pl.BlockSpec(memory_space=pltpu.VMEM)
    flat = [a for p in params for a in p]
    outs = pl.pallas_call(
        body, name="adamw_small", in_specs=[pl.BlockSpec(memory_space=pltpu.SMEM), vm] + [vm] * len(flat),
        out_specs=[vm] * (4 * n), out_shape=[jax.ShapeDtypeStruct(p[0].shape, F32) for p in params for _ in range(4)],
    )(chip, tot, *flat)
    return [tuple(outs[4 * i:4 * i + 4]) for i in range(n)]


class _ReduceScatter:
    def __init__(self, place):
        self.place = place
        self.exchanging = {}
        self.started = []
        self.sharing = {}

    def begin(self, tag, grads):
        pieces = [g[None] if kind == "col" else g.reshape(N_CHIPS, g.shape[0] // N_CHIPS, g.shape[1]) for _, g, kind in grads]
        landing = [lax.empty((p.shape[0], p.shape[1] // 2, p.shape[2]), p.dtype) for p in pieces]
        started = _start_copies("exchange_" + tag, pieces + landing, _exchange_plan(len(pieces)), len(pieces))
        self.exchanging[tag] = ([nm for nm, _, _ in grads], started)
        return started[3]

    def send(self, tag, after):
        group, (send_sem, recv_sem, arrays, _) = self.exchanging[tag]
        n = len(group)
        arrays = _wait_copies("exchange_" + tag + "_wait", send_sem, recv_sem, arrays, _exchange_plan(n), after)
        parts = [_pair_sum(g, t, self.place[:1], "pair_sum_" + nm) for g, t, nm in zip(arrays[:n], arrays[n:], group)]
        started = _scatter_start("scatter_" + tag, parts)
        token = self.sum_up(started[3])
        self.started.append((tag, group, started))
        return started[3] if token is None else token

    def sum_up(self, after):
        token = None
        for tag, group, started in self.started:
            if tag not in self.sharing:
                parts, stacks = _scatter_finish("scatter_" + tag, started, after)
                sums = [_sum_chips(p, s, self.place, "chip_sum_" + nm) for p, s, nm in zip(parts, stacks, group)]
                self.sharing[tag] = (group, _start_copies("share_" + tag, sums, _own_half_plan, len(sums)))
                token = self.sharing[tag][1][3]
        return token

    def result(self, tag, after):
        group, (send_sem, recv_sem, sums, _) = self.sharing[tag]
        return dict(zip(group, _wait_copies("share_" + tag + "_wait", send_sem, recv_sem, sums, _own_half_plan, after)))


def kernel(x, ev_norm_w, ev_w_in, ev_conv_w, ev_w_out, od_norm_w, od_w_in, od_q_norm_w, od_k_norm_w, od_sinks, od_w_out, loss_target, m_ev_norm_w, m_ev_w_in, m_ev_conv_w, m_ev_w_out, m_od_norm_w, m_od_w_in, m_od_q_norm_w, m_od_k_norm_w, m_od_sinks, m_od_w_out, v_ev_norm_w, v_ev_w_in, v_ev_conv_w, v_ev_w_out, v_od_norm_w, v_od_w_in, v_od_q_norm_w, v_od_k_norm_w, v_od_sinks, v_od_w_out):
    my_chip = 2 * lax.axis_index("x") + lax.axis_index("y")
    place = jnp.stack([lax.axis_index("c"), my_chip]).astype(jnp.int32)
    shard_w = D_MODEL // N_CHIPS
    conv_shard = CONV_WIDTH // N_CHIPS

    small_in = jnp.zeros((8, shard_w), F32)
    small_in = small_in.at[0].set(od_norm_w[0]).at[1:4, :conv_shard].set(ev_conv_w[0])
    small_in = lax.dynamic_update_slice(jnp.zeros((N_CHIPS, 8, shard_w), F32), small_in[None], (my_chip, 0, 0))
    chip = place[1:]
    first_kinds, second_kinds, late_kinds = (("col", 0, 2), "whole"), (("col", 1, 2),), ("row", "col", "row")
    w_in0_own_place, w_in0_shard = _cast_into_gathered(ev_w_in[0], "col", chip, "cast_w_in0", keep_shard=True)
    first = _gather_start("gather_first", [w_in0_own_place, small_in], first_kinds)
    second = _gather_start("gather_second", [first[2][0]], second_kinds)
    late_own = [_cast_into_gathered(ev_w_out[0], "row", chip, "cast_w_out0"),
                _cast_into_gathered(od_w_in[0], "col", chip, "cast_w_in1"),
                _cast_into_gathered(od_w_out[0], "row", chip, "cast_w_out1")]
    late, w_in0_so_far = [], []

    def weights_first(part, after):
        if part == 0:
            passed_on = _gather_arrive("gather_first", first, first_kinds, [after] + late_own, arrays=[second[2][0], first[2][1]])
            w_in0, small_all = _gather_done("gather_first", passed_on, first_kinds, passed_on[3])
            w_in0_so_far.append(w_in0)
            late.append(_gather_start("gather_late", late_own, late_kinds, after=small_all))
            od_norm_full = small_all[:, 0, :].reshape(1, D_MODEL)
            conv_full = jnp.transpose(small_all[:, 1:4, :conv_shard], (1, 0, 2)).reshape(3, CONV_WIDTH)
            return w_in0, (conv_full, od_norm_full), late[0][3]
        passed_on = _gather_arrive("gather_second", second, second_kinds, after, arrays=w_in0_so_far)
        (w_in0,) = _gather_done("gather_second", passed_on, second_kinds, passed_on[3])
        return w_in0, None, None

    def late_arrived(after):
        late.append(_gather_arrive("gather_late", late[0], late_kinds, after))
        return late[1][3]

    def weights_late(after):
        return _gather_done("gather_late", late[1], late_kinds, after)

    reduce_scatter = _ReduceScatter(place)
    sq, grad_x, small = _local_step(x[0], loss_target[0], ev_norm_w, od_q_norm_w, od_k_norm_w, od_sinks,
                                    (w_in0_shard, chip, second[3]), weights_first, late_arrived, weights_late,
                                    reduce_scatter)

    token = reduce_scatter.sum_up(grad_x)
    upd = {}
    upper = reduce_scatter.result("upper", token)
    upd["od_w_in"] = _adamw(od_w_in[0], upper["od_w_in"], m_od_w_in[0], v_od_w_in[0], "adamw_od_w_in")
    upd["od_w_out"] = _adamw(od_w_out[0], upper["od_w_out"], m_od_w_out[0], v_od_w_out[0], "adamw_od_w_out")
    upd["ev_w_out"] = _adamw(ev_w_out[0], upper["ev_w_out"], m_ev_w_out[0], v_ev_w_out[0], "adamw_ev_w_out")
    g_ev_w_in = reduce_scatter.result("in0", [upd[nm][1] for nm in ("od_w_in", "od_w_out", "ev_w_out")])["ev_w_in"]
    upd["ev_w_in"] = _adamw(ev_w_in[0], g_ev_w_in, m_ev_w_in[0], v_ev_w_in[0], "adamw_ev_w_in")
    tot = _allreduce_small(_pack_small_grads(small, sq))
    loss = 0.5 * tot[6, 0] / D_MODEL

    smalls = (("ev_norm_w", ev_norm_w, m_ev_norm_w, v_ev_norm_w), ("ev_conv_w", ev_conv_w, m_ev_conv_w, v_ev_conv_w),
              ("od_norm_w", od_norm_w, m_od_norm_w, v_od_norm_w), ("od_q_norm_w", od_q_norm_w, m_od_q_norm_w, v_od_q_norm_w),
              ("od_k_norm_w", od_k_norm_w, m_od_k_norm_w, v_od_k_norm_w), ("od_sinks", od_sinks, m_od_sinks, v_od_sinks))
    for (nm, _, _, _), result in zip(smalls, _adamw_small(tot, chip, [p[1:] for p in smalls])):
        upd[nm] = result
    for nm in ("ev_w_in", "ev_w_out", "od_w_in", "od_w_out"):
        upd[nm] = tuple(u[None] for u in upd[nm])
    order = ("ev_norm_w", "ev_w_in", "ev_conv_w", "ev_w_out", "od_norm_w", "od_w_in", "od_q_norm_w", "od_k_norm_w", "od_sinks", "od_w_out")
    return (loss, grad_x[None], *[upd[nm][0] for nm in order], *[upd[nm][1] for nm in order],
            *[upd[nm][2] for nm in order], *[upd[nm][3] for nm in order])
```

```python
import math

import jax
import jax.numpy as jnp
import numpy as np
from jax import lax
from jax.experimental import pallas as pl
from jax.experimental.pallas import tpu as pltpu

F32 = jnp.float32
BF16 = jnp.bfloat16

D_MODEL = 2048
RET_HEADS = 4
RET_DIM = 256
RET_WIDTH = 1024
CONV_WIDTH = 1024
EVEN_IN = 8192
Q_HEADS = 32
HEAD_DIM = 64
KV_HEADS = 4
KV_WIDTH = 256
ATTN_WIDTH = 2048
ODD_IN = 4608
BLK = 128
ROPE_THETA = 10000.0
EPS = 1e-6
ADAM_LR = 0.001
ADAM_B1 = 0.9
ADAM_B2 = 0.999
ADAM_EPS = 1e-08
ADAM_WD = 0.01
ADAM_STEP = 10
N_CHIPS = 4
N_DEV = 8
VMEM_LIMIT_BYTES = 56 * 1024 * 1024
MESH = pl.DeviceIdType.MESH
ANY = pl.BlockSpec(memory_space=pl.ANY)


def _params(*sem):
    return pltpu.CompilerParams(dimension_semantics=sem, vmem_limit_bytes=VMEM_LIMIT_BYTES)


def _dot(a, b):
    return jnp.dot(a, b, preferred_element_type=F32)


def _dot_nt(a, b):
    return lax.dot_general(a, b, (((1,), (1,)), ((), ())), preferred_element_type=F32)


def _dot_tn(a, b):
    return lax.dot_general(a, b, (((0,), (0,)), ((), ())), preferred_element_type=F32)


def _sigmoid(x):
    return 1.0 / (1.0 + jnp.exp(-x))


def _mm(a, b, *, mode, tm, tn, tk, out_dtype, name, add=None, after=None, m_blocks=None, into=None):
    if mode == "nn":
        (m, k), n = a.shape, b.shape[1]
    elif mode == "nt":
        (m, k), n = a.shape, b.shape[0]
    else:
        (k, m), n = a.shape, b.shape[1]
    tm, tn, tk = min(tm, m), min(tn, n), min(tk, k)
    assert m % tm == 0 and n % tn == 0 and k % tk == 0, (name, m, n, k)
    nk = k // tk
    first_m, count_m = (0, m // tm) if m_blocks is None else m_blocks
    dot = {"nn": _dot, "nt": _dot_nt, "tn": _dot_tn}[mode]
    a_spec = (pl.BlockSpec((tk, tm), lambda i, j, kk: (kk, i + first_m)) if mode == "tn"
              else pl.BlockSpec((tm, tk), lambda i, j, kk: (i + first_m, kk)))
    b_spec = (pl.BlockSpec((tn, tk), lambda i, j, kk: (j, kk)) if mode == "nt"
              else pl.BlockSpec((tk, tn), lambda i, j, kk: (kk, j)))
    o_spec = pl.BlockSpec((tm, tn), lambda i, j, kk: (i + first_m, j))
    has_add = add is not None

    def body(*refs):
        a_ref, b_ref = refs[0], refs[1]
        add_ref = refs[2] if has_add else None
        o_ref, acc_ref = refs[-2], refs[-1]
        p = dot(a_ref[...], b_ref[...])

        def finish(total):
            if has_add:
                total = total + add_ref[...].astype(F32)
            o_ref[...] = total.astype(out_dtype)

        if nk == 1:
            finish(p)
        else:
            kk = pl.program_id(2)

            @pl.when(kk == 0)
            def _():
                acc_ref[...] = p

            @pl.when(jnp.logical_and(kk > 0, kk < nk - 1))
            def _():
                acc_ref[...] += p

            @pl.when(kk == nk - 1)
            def _():
                finish(acc_ref[...] + p)

    extra = [arr for arr in (after, into) if arr is not None]
    in_specs = [a_spec, b_spec] + ([o_spec] if has_add else []) + [ANY] * len(extra)
    args = (a, b) + ((add,) if has_add else ()) + tuple(extra)
    return pl.pallas_call(
        body, name=name, grid=(count_m, n // tn, nk), in_specs=in_specs, out_specs=o_spec,
        out_shape=jax.ShapeDtypeStruct((m, n), out_dtype),
        input_output_aliases={len(args) - 1: 0} if into is not None else {},
        scratch_shapes=[pltpu.VMEM((tm, tn) if nk > 1 else (8, 128), F32)],
        compiler_params=_params("parallel", "parallel", "arbitrary"),
    )(*args)


def _mm_shifted(a, b, shift, *, b_shifted, first, count, total, tm, tn, out_dtype, name, stride=1, into=None, after=None):
    m, k = a.shape
    tm = min(tm, m)
    assert m % tm == 0
    col = lambda j, shift_ref: (shift_ref[0] + first + stride * j) % total
    extra = [arr for arr in (into, after) if arr is not None]

    def body(shift_ref, a_ref, b_ref, *rest):
        del shift_ref
        rest[-1][...] = _dot(a_ref[...], b_ref[...]).astype(out_dtype)

    return pl.pallas_call(
        body, name=name, out_shape=jax.ShapeDtypeStruct((m, total * tn), out_dtype),
        grid_spec=pltpu.PrefetchScalarGridSpec(
            num_scalar_prefetch=1, grid=(m // tm, count),
            in_specs=[pl.BlockSpec((tm, k), lambda i, j, s: (i, 0)),
                      pl.BlockSpec((k, tn), (lambda i, j, s: (0, col(j, s))) if b_shifted else (lambda i, j, s: (0, j)))]
            + [ANY] * len(extra),
            out_specs=pl.BlockSpec((tm, tn), lambda i, j, s: (i, col(j, s)))),
        input_output_aliases={3: 0} if into is not None else {},
        compiler_params=_params("parallel", "arbitrary"))(shift, a, b, *extra)


def _mm_rows(a, b, rows_in, vecs_in, out_shapes, epilogue, *, tm, name):
    m, k = a.shape
    n = b.shape[1]
    assert m % tm == 0
    row = pl.BlockSpec((tm, n), lambda i: (i, 0))

    def body(a_ref, b_ref, *rest):
        epilogue(_dot(a_ref[...], b_ref[...]), *rest)

    out_specs = [row if tuple(s.shape) == (m, n) else pl.BlockSpec(s.shape, lambda i: (0, 0)) for s in out_shapes]
    return pl.pallas_call(
        body, name=name, grid=(m // tm,),
        in_specs=[pl.BlockSpec((tm, k), lambda i: (i, 0)), pl.BlockSpec((k, n), lambda i: (0, 0))] + [row] * len(rows_in)
        + [pl.BlockSpec((1, n), lambda i: (0, 0))] * len(vecs_in),
        out_specs=out_specs, out_shape=out_shapes, compiler_params=_params("arbitrary"),
    )(a, b, *rows_in, *vecs_in)


def _cast_into_gathered(w, kind, chip, name, keep_shard=False):
    r, c = w.shape
    tr = min(r, 512)
    per = r // tr

    def body(chip_ref, w_ref, *outs):
        del chip_ref
        for o_ref in outs:
            o_ref[...] = w_ref[...].astype(BF16)

    if kind == "col":
        shape, out_map = (r, N_CHIPS * c), (lambda i, chip_ref: (i, chip_ref[0]))
    else:
        shape, out_map = (N_CHIPS * r, c), (lambda i, chip_ref: (chip_ref[0] * per + i, 0))
    plain = pl.BlockSpec((tr, c), lambda i, chip_ref: (i, 0))
    out = pl.pallas_call(
        body, name=name,
        out_shape=[jax.ShapeDtypeStruct(shape, BF16)] + ([jax.ShapeDtypeStruct((r, c), BF16)] if keep_shard else []),
        grid_spec=pltpu.PrefetchScalarGridSpec(
            num_scalar_prefetch=1, grid=(per,), in_specs=[plain],
            out_specs=[pl.BlockSpec((tr, c), out_map)] + ([plain] if keep_shard else [])),
        compiler_params=_params("parallel"))(chip, w)
    return out if keep_shard else out[0]


NORM_ROWS = 512


def _rmsnorm(x, w, name, after=None):
    s, d = x.shape
    tr = NORM_ROWS

    def body(x_ref, w_ref, *rest):
        xv = x_ref[...]
        rstd = lax.rsqrt(jnp.mean(xv * xv, axis=-1, keepdims=True) + EPS)
        rest[-1][...] = (xv * rstd * w_ref[...]).astype(BF16)

    return pl.pallas_call(
        body, name=name, grid=(s // tr,),
        in_specs=[pl.BlockSpec((tr, d), lambda i: (i, 0)), pl.BlockSpec((1, d), lambda i: (0, 0))]
        + ([ANY] if after is not None else []),
        out_specs=pl.BlockSpec((tr, d), lambda i: (i, 0)),
        out_shape=jax.ShapeDtypeStruct((s, d), BF16), compiler_params=_params("parallel"),
    )(x, w, *((after,) if after is not None else ()))


def _rmsnorm_bwd(x, w, dh, dres, name, out_dtype):
    s, d = x.shape
    tr = NORM_ROWS

    def body(x_ref, w_ref, dh_ref, dres_ref, dx_ref, dw_ref):
        xv = x_ref[...]
        rstd = lax.rsqrt(jnp.mean(xv * xv, axis=-1, keepdims=True) + EPS)
        nrm = xv * rstd
        dhv = dh_ref[...].astype(F32)
        dn = dhv * w_ref[...]
        dx = dres_ref[...].astype(F32) + rstd * (dn - nrm * jnp.mean(dn * nrm, axis=-1, keepdims=True))
        dx_ref[...] = dx.astype(out_dtype)

        @pl.when(pl.program_id(0) == 0)
        def _():
            dw_ref[...] = jnp.zeros_like(dw_ref)

        dw_ref[...] += jnp.sum(dhv * nrm, axis=0, keepdims=True)

    row = pl.BlockSpec((tr, d), lambda i: (i, 0))
    return pl.pallas_call(
        body, name=name, grid=(s // tr,),
        in_specs=[row, pl.BlockSpec((1, d), lambda i: (0, 0)), row, row],
        out_specs=[row, pl.BlockSpec((8, d), lambda i: (0, 0))],
        out_shape=[jax.ShapeDtypeStruct((s, d), out_dtype), jax.ShapeDtypeStruct((8, d), F32)],
        compiler_params=_params("arbitrary"))(x, w, dh, dres)


def _adamw_math(w, g, m, v):
    nm = ADAM_B1 * m + (1.0 - ADAM_B1) * g
    nv = ADAM_B2 * v + (1.0 - ADAM_B2) * (g * g)
    m_hat = nm / (1.0 - ADAM_B1 ** ADAM_STEP)
    v_hat = nv / (1.0 - ADAM_B2 ** ADAM_STEP)
    return -ADAM_LR * (m_hat / (jnp.sqrt(v_hat) + ADAM_EPS) + ADAM_WD * w), nm, nv


def _adamw(w, g, m, v, name):
    r, c = w.shape
    tr = min(r, 256)
    assert r % tr == 0

    def body(w_ref, g_ref, m_ref, v_ref, g_out, d_ref, nm_ref, nv_ref):
        gv = g_ref[...]
        g_out[...] = gv
        d_ref[...], nm_ref[...], nv_ref[...] = _adamw_math(w_ref[...], gv, m_ref[...], v_ref[...])

    spec = pl.BlockSpec((tr, c), lambda i: (i, 0))
    shp = jax.ShapeDtypeStruct((r, c), F32)
    return pl.pallas_call(body, name=name, grid=(r // tr,), in_specs=[spec] * 4, out_specs=[spec] * 4,
                          out_shape=[shp] * 4, compiler_params=_params("parallel"))(w, g, m, v)


def _rope_tables(s, dim):
    inv = (1.0 / (ROPE_THETA ** (np.arange(0, dim, 2, dtype=np.float64) / dim))).astype(np.float32)
    ang = (np.arange(s, dtype=np.float32)[:, None] * inv[None, :]).astype(np.float64)
    return np.cos(ang).astype(np.float32), np.sin(ang).astype(np.float32)


def _rope_half(x, cos, sin):
    h = x.shape[1] // 2
    x1, x2 = x[:, :h], x[:, h:]
    return jnp.concatenate([x1 * cos - x2 * sin, x2 * cos + x1 * sin], axis=1)


def _unrope_half(dy, cos, sin):
    h = dy.shape[1] // 2
    d1, d2 = dy[:, :h], dy[:, h:]
    return jnp.concatenate([d1 * cos + d2 * sin, d2 * cos - d1 * sin], axis=1)


def _lane(shape):
    return lax.broadcasted_iota(jnp.int32, shape, 1)


def _partner64(x):
    w = x.shape[1]
    first = (_lane(x.shape) % HEAD_DIM) < (HEAD_DIM // 2)
    return jnp.where(first, pltpu.roll(x, w - HEAD_DIM // 2, 1), pltpu.roll(x, HEAD_DIM // 2, 1))


def _tile_lanes(t, reps):
    return t if reps == 1 else jnp.concatenate([t] * reps, axis=1)


MXU_WIDTH = 256


def _group_mean(x, ones_bd, passes=2):
    width, tile = x.shape[1], ones_bd.shape[0]
    if width > tile:
        return jnp.concatenate([_group_mean(x[:, c:c + tile], ones_bd, passes) for c in range(0, width, tile)], axis=1)
    hi = x.astype(BF16)
    if passes == 1:
        return _dot(hi, ones_bd)
    lo = (x - hi.astype(F32)).astype(BF16)
    return _dot(hi, ones_bd) + _dot(lo, ones_bd)


def _block_diag_mean(width):
    idx = jnp.arange(width) // HEAD_DIM
    return jnp.where(idx[:, None] == idx[None, :], 1.0 / HEAD_DIM, 0.0).astype(BF16)


RET_STEP = 4


def _retention_tables():
    h = RET_HEADS
    log_g = jnp.log(1.0 - 2.0 ** (-5.0 - jnp.arange(h, dtype=F32)))
    idx = jnp.arange(BLK, dtype=F32)
    diff = idx[:, None] - idx[None, :]
    intra = jnp.where(diff >= 0, jnp.exp(log_g[:, None, None] * jnp.maximum(diff, 0.0)), 0.0).astype(F32)
    q_dec = jnp.exp(log_g[:, None] * (idx[None, :] + 1.0)).astype(F32)[:, :, None]
    k_dec = jnp.exp(log_g[:, None] * (BLK - 1.0 - idx[None, :])).astype(F32)[:, :, None]
    chunk_dec = jnp.exp(log_g * BLK).astype(F32)[:, None, None]
    return intra, q_dec, k_dec, chunk_dec


def _retention_fwd(proj, cos, sin, tables, after=None):
    s = proj.shape[0]
    nc = s // BLK
    intra, q_dec, k_dec, chunk_dec = tables

    def body(p_ref, cos_ref, sin_ref, in_ref, qd_ref, kd_ref, cd_ref, *rest):
        o_ref, cat_ref, st_ref, state = rest[-4:]

        @pl.when(pl.program_id(0) == 0)
        def _():
            state[...] = jnp.zeros_like(state)

        for c in range(RET_STEP):
            rows = slice(c * BLK, (c + 1) * BLK)
            cosv, sinv = cos_ref[rows, :], sin_ref[rows, :]
            for h in range(RET_HEADS):
                c0 = h * RET_DIM
                q = p_ref[rows, c0:c0 + RET_DIM].astype(F32)
                k = p_ref[rows, RET_WIDTH + c0:RET_WIDTH + c0 + RET_DIM].astype(F32)
                v = p_ref[rows, 2 * RET_WIDTH + c0:2 * RET_WIDTH + c0 + RET_DIM]
                g = p_ref[rows, 3 * RET_WIDTH + c0:3 * RET_WIDTH + c0 + RET_DIM].astype(F32)
                qb = _rope_half(q, cosv, sinv).astype(BF16)
                kr = _rope_half(k, cosv, sinv) * (RET_DIM ** -0.5)
                kb = kr.astype(BF16)
                scores = _dot_nt(qb, kb) * in_ref[h]
                inner = _dot(scores.astype(BF16), v)
                prev = state[h]
                prev_b = prev.astype(BF16)
                st_ref[h, c] = prev_b
                o = inner + _dot(qb, prev_b) * qd_ref[h]
                o_ref[rows, c0:c0 + RET_DIM] = o
                rstd = lax.rsqrt(jnp.mean(o * o, axis=-1, keepdims=True) + EPS)
                cat_ref[rows, c0:c0 + RET_DIM] = (o * rstd * (g * _sigmoid(g))).astype(BF16)
                state[h] = cd_ref[h] * prev + _dot_tn((kr * kd_ref[h]).astype(BF16), v)

    full = lambda shape: pl.BlockSpec(shape, lambda n: (0,) * len(shape))
    step = RET_STEP * BLK
    return pl.pallas_call(
        body, name="retention_fwd", grid=(nc // RET_STEP,),
        in_specs=[pl.BlockSpec((step, 4 * RET_WIDTH), lambda n: (n, 0)),
                  pl.BlockSpec((step, RET_DIM // 2), lambda n: (n, 0)), pl.BlockSpec((step, RET_DIM // 2), lambda n: (n, 0)),
                  full((RET_HEADS, BLK, BLK)), full((RET_HEADS, BLK, 1)), full((RET_HEADS, BLK, 1)), full((RET_HEADS, 1, 1))]
        + ([ANY] if after is not None else []),
        out_specs=[pl.BlockSpec((step, RET_WIDTH), lambda n: (n, 0)), pl.BlockSpec((step, RET_WIDTH), lambda n: (n, 0)),
                   pl.BlockSpec((RET_HEADS, RET_STEP, RET_DIM, RET_DIM), lambda n: (0, n, 0, 0))],
        out_shape=[jax.ShapeDtypeStruct((s, RET_WIDTH), F32), jax.ShapeDtypeStruct((s, D_MODEL), BF16),
                   jax.ShapeDtypeStruct((RET_HEADS, nc, RET_DIM, RET_DIM), BF16)],
        scratch_shapes=[pltpu.VMEM((RET_HEADS, RET_DIM, RET_DIM), F32)],
        compiler_params=_params("arbitrary"),
    )(proj, cos, sin, intra, q_dec, k_dec, chunk_dec, *((after,) if after is not None else ()))


def _retention_bwd(proj, o, states, dcat, cos, sin, tables):
    s = proj.shape[0]
    nc = s // BLK
    intra, q_dec, k_dec, chunk_dec = tables

    def body(p_ref, o_ref, st_ref, dc_ref, cos_ref, sin_ref, in_ref, qd_ref, kd_ref, cd_ref, dp_ref, dstate):
        @pl.when(pl.program_id(0) == 0)
        def _():
            dstate[...] = jnp.zeros_like(dstate)

        for c in reversed(range(RET_STEP)):
            rows = slice(c * BLK, (c + 1) * BLK)
            cosv, sinv = cos_ref[rows, :], sin_ref[rows, :]
            for h in range(RET_HEADS):
                c0 = h * RET_DIM
                q = p_ref[rows, c0:c0 + RET_DIM].astype(F32)
                k = p_ref[rows, RET_WIDTH + c0:RET_WIDTH + c0 + RET_DIM].astype(F32)
                v = p_ref[rows, 2 * RET_WIDTH + c0:2 * RET_WIDTH + c0 + RET_DIM]
                g = p_ref[rows, 3 * RET_WIDTH + c0:3 * RET_WIDTH + c0 + RET_DIM].astype(F32)
                o = o_ref[rows, c0:c0 + RET_DIM]
                dc = dc_ref[rows, c0:c0 + RET_DIM].astype(F32)
                rstd = lax.rsqrt(jnp.mean(o * o, axis=-1, keepdims=True) + EPS)
                nrm = o * rstd
                sg = _sigmoid(g)
                dg = dc * nrm * (sg * (1.0 + g * (1.0 - sg)))
                dn = dc * (g * sg)
                do = rstd * (dn - nrm * jnp.mean(dn * nrm, axis=-1, keepdims=True))
                qb = _rope_half(q, cosv, sinv).astype(BF16)
                kr = _rope_half(k, cosv, sinv) * (RET_DIM ** -0.5)
                kb = kr.astype(BF16)
                mask = in_ref[h]
                qd, kd = qd_ref[h], kd_ref[h]
                prev_b = st_ref[h, c]
                dnext = dstate[h]
                dnext_b = dnext.astype(BF16)
                att = (_dot_nt(qb, kb) * mask).astype(BF16)
                do_b = do.astype(BF16)
                doq = (do * qd).astype(BF16)
                dv = _dot_tn(att, do_b) + _dot((kr * kd).astype(BF16), dnext_b)
                ds = (_dot_nt(do_b, v) * mask).astype(BF16)
                dqr = _dot(ds, kb) + _dot_nt(doq, prev_b)
                dkr = _dot_tn(ds, qb) + _dot_nt(v, dnext_b) * kd
                dstate[h] = cd_ref[h] * dnext + _dot_tn(qb, doq)
                dq = _unrope_half(dqr, cosv, sinv)
                dk = _unrope_half(dkr * (RET_DIM ** -0.5), cosv, sinv)
                dp_ref[rows, c0:c0 + RET_DIM] = dq.astype(BF16)
                dp_ref[rows, RET_WIDTH + c0:RET_WIDTH + c0 + RET_DIM] = dk.astype(BF16)
                dp_ref[rows, 2 * RET_WIDTH + c0:2 * RET_WIDTH + c0 + RET_DIM] = dv.astype(BF16)
                dp_ref[rows, 3 * RET_WIDTH + c0:3 * RET_WIDTH + c0 + RET_DIM] = dg.astype(BF16)

    steps = nc // RET_STEP
    rev = lambda n: steps - 1 - n
    full = lambda shape: pl.BlockSpec(shape, lambda n: (0,) * len(shape))
    step = RET_STEP * BLK
    return pl.pallas_call(
        body, name="retention_bwd", grid=(steps,),
        in_specs=[pl.BlockSpec((step, 4 * RET_WIDTH), lambda n: (rev(n), 0)),
                  pl.BlockSpec((step, RET_WIDTH), lambda n: (rev(n), 0)),
                  pl.BlockSpec((RET_HEADS, RET_STEP, RET_DIM, RET_DIM), lambda n: (0, rev(n), 0, 0)),
                  pl.BlockSpec((step, RET_WIDTH), lambda n: (rev(n), 0)),
                  pl.BlockSpec((step, RET_DIM // 2), lambda n: (rev(n), 0)), pl.BlockSpec((step, RET_DIM // 2), lambda n: (rev(n), 0)),
                  full((RET_HEADS, BLK, BLK)), full((RET_HEADS, BLK, 1)), full((RET_HEADS, BLK, 1)), full((RET_HEADS, 1, 1))],
        out_specs=pl.BlockSpec((step, 4 * RET_WIDTH), lambda n: (rev(n), 0)),
        out_shape=jax.ShapeDtypeStruct((s, EVEN_IN), BF16),
        scratch_shapes=[pltpu.VMEM((RET_HEADS, RET_DIM, RET_DIM), F32)],
        compiler_params=_params("arbitrary"))(proj, o, states, dcat, cos, sin, intra, q_dec, k_dec, chunk_dec)


CONV_ROWS = 256
HALO = 16


def _conv_pieces(p, halo, first):
    gb, gc, u, gv = (p[:, i * CONV_WIDTH:(i + 1) * CONV_WIDTH] for i in range(4))
    cu = gc * u
    hcu = halo[:, CONV_WIDTH:2 * CONV_WIDTH] * halo[:, 2 * CONV_WIDTH:3 * CONV_WIDTH]
    hcu = jnp.where(first, 0.0, hcu)
    r1, r2 = hcu[HALO - 1:HALO], hcu[HALO - 2:HALO - 1]
    row = lax.broadcasted_iota(jnp.int32, cu.shape, 0)
    m1 = jnp.where(row == 0, r1, pltpu.roll(cu, 1, 0))
    m2 = jnp.where(row == 0, r2, jnp.where(row == 1, r1, pltpu.roll(cu, 2, 0)))
    return gb, gc, u, gv, cu, m1, m2


def _conv_fwd(proj, conv_w, cat):
    s = proj.shape[0]
    per = CONV_ROWS // HALO

    def body(p_ref, halo_ref, w_ref, cat_in, cat_ref):
        del cat_in
        first = pl.program_id(0) == 0
        gb, _, _, gv, cu, m1, m2 = _conv_pieces(p_ref[...].astype(F32), halo_ref[...].astype(F32), first)
        conv = w_ref[0:1, :] * m2 + w_ref[1:2, :] * m1 + w_ref[2:3, :] * cu
        cat_ref[...] = (gb * conv * (gv * _sigmoid(gv))).astype(BF16)

    return pl.pallas_call(
        body, name="conv_fwd", grid=(s // CONV_ROWS,),
        in_specs=[pl.BlockSpec((CONV_ROWS, 4 * CONV_WIDTH), lambda i: (i, 1)),
                  pl.BlockSpec((HALO, 4 * CONV_WIDTH), lambda i: (jnp.maximum(i * per - 1, 0), 1)),
                  pl.BlockSpec((3, CONV_WIDTH), lambda i: (0, 0)), ANY],
        out_specs=pl.BlockSpec((CONV_ROWS, CONV_WIDTH), lambda i: (i, 1)),
        out_shape=jax.ShapeDtypeStruct(cat.shape, cat.dtype), input_output_aliases={3: 0},
        compiler_params=_params("parallel"))(proj, proj, conv_w, cat)


def _conv_bwd(proj, dcat, conv_w, dproj):
    s = proj.shape[0]
    per = CONV_ROWS // HALO
    last_halo = s // HALO - 1
    nsteps = s // CONV_ROWS

    def body(p_ref, halo_ref, nxt_ref, dc_ref, dnxt_ref, w_ref, dp_in, dp_ref, dw_ref):
        del dp_in
        i = pl.program_id(0)
        gb, gc, u, gv, cu, m1, m2 = _conv_pieces(p_ref[...].astype(F32), halo_ref[...].astype(F32), i == 0)
        w0, w1, w2 = w_ref[0:1, :], w_ref[1:2, :], w_ref[2:3, :]
        conv = w0 * m2 + w1 * m1 + w2 * cu
        dco = dc_ref[...].astype(F32)
        sg = _sigmoid(gv)
        silu = gv * sg
        dgb = dco * conv * silu
        dco_gb = dco * gb
        dgv = dco_gb * conv * (sg * (1.0 + gv * (1.0 - sg)))
        dconv = dco_gb * silu
        nxt = nxt_ref[...].astype(F32)
        ngv = nxt[:, 3 * CONV_WIDTH:]
        dnext = dnxt_ref[...].astype(F32) * nxt[:, :CONV_WIDTH] * (ngv * _sigmoid(ngv))
        dnext = jnp.where(i == nsteps - 1, 0.0, dnext)
        n1, n2 = dnext[0:1], dnext[1:2]
        row = lax.broadcasted_iota(jnp.int32, dconv.shape, 0)
        p1 = jnp.where(row == CONV_ROWS - 1, n1, pltpu.roll(dconv, CONV_ROWS - 1, 0))
        p2 = jnp.where(row == CONV_ROWS - 1, n2, jnp.where(row == CONV_ROWS - 2, n1, pltpu.roll(dconv, CONV_ROWS - 2, 0)))
        dcu = w2 * dconv + w1 * p1 + w0 * p2
        dp_ref[...] = jnp.concatenate([dgb, dcu * u, dcu * gc, dgv], axis=1).astype(BF16)

        @pl.when(i == 0)
        def _():
            dw_ref[...] = jnp.zeros_like(dw_ref)

        taps = [jnp.sum(dconv * m, axis=0, keepdims=True) for m in (m2, m1, cu)]
        r8 = lax.broadcasted_iota(jnp.int32, dw_ref.shape, 0)
        dw_ref[...] += jnp.where(r8 == 0, taps[0], jnp.where(r8 == 1, taps[1], jnp.where(r8 == 2, taps[2], 0.0)))

    return pl.pallas_call(
        body, name="conv_bwd", grid=(nsteps,),
        in_specs=[pl.BlockSpec((CONV_ROWS, 4 * CONV_WIDTH), lambda i: (i, 1)),
                  pl.BlockSpec((HALO, 4 * CONV_WIDTH), lambda i: (jnp.maximum(i * per - 1, 0), 1)),
                  pl.BlockSpec((HALO, 4 * CONV_WIDTH), lambda i: (jnp.minimum((i + 1) * per, last_halo), 1)),
                  pl.BlockSpec((CONV_ROWS, CONV_WIDTH), lambda i: (i, 1)),
                  pl.BlockSpec((HALO, CONV_WIDTH), lambda i: (jnp.minimum((i + 1) * per, last_halo), 1)),
                  pl.BlockSpec((3, CONV_WIDTH), lambda i: (0, 0)), ANY],
        out_specs=[pl.BlockSpec((CONV_ROWS, 4 * CONV_WIDTH), lambda i: (i, 1)), pl.BlockSpec((8, CONV_WIDTH), lambda i: (0, 0))],
        out_shape=[jax.ShapeDtypeStruct(dproj.shape, dproj.dtype), jax.ShapeDtypeStruct((8, CONV_WIDTH), F32)],
        input_output_aliases={6: 0},
        compiler_params=_params("arbitrary"))(proj, proj, proj, dcat, dcat, conv_w, dproj)


GROUP_HEADS = Q_HEADS // KV_HEADS
GROUP_WIDTH = GROUP_HEADS * HEAD_DIM
SLAB = 2 * HEAD_DIM
KV_COL = ATTN_WIDTH // GROUP_WIDTH
GATE_COL = KV_COL + 1
ATTN_SCALE = HEAD_DIM ** -0.5
KV_ROWS = 1024


def _half_mask(shape, which):
    return (_lane(shape) // HEAD_DIM) == which


def _dup_head(slab, which):
    kept = jnp.where(_half_mask(slab.shape, which), slab, 0.0)
    return kept + pltpu.roll(kept, HEAD_DIM, 1)


def _stack_heads(x):
    parts = []
    for sl in range(GROUP_WIDTH // SLAB):
        slab = x[:, sl * SLAB:(sl + 1) * SLAB]
        parts += [jnp.where(_half_mask(slab.shape, e), slab, 0.0) for e in range(2)]
    return jnp.concatenate(parts, axis=0)


def _unstack_heads(y):
    slabs = []
    for sl in range(GROUP_WIDTH // SLAB):
        a, b = y[(2 * sl) * BLK:(2 * sl + 1) * BLK], y[(2 * sl + 1) * BLK:(2 * sl + 2) * BLK]
        slabs.append(jnp.where(_half_mask(a.shape, 0), a, b))
    return jnp.concatenate(slabs, axis=1)


def _q_prep(q, qw, cosf, sins, ones_bd):
    rstd = lax.rsqrt(_group_mean(q * q, ones_bd) + EPS)
    nrm = q * rstd
    y = nrm * qw
    return nrm, rstd, y * cosf + _partner64(y) * sins


def _band(tri_ref, n):
    own = tri_ref[...] > 0.5
    return own, jnp.where(jnp.logical_and(n == 0, jnp.logical_not(own)), -1e30, 0.0)


def _fold(pair, own):
    return jnp.where(own, pair[:, BLK:], pair[:, :BLK])


def _unfold(folded, own):
    return jnp.concatenate([jnp.where(own, 0.0, folded), jnp.where(own, folded, 0.0)], axis=1)


def _head_probs(raw_scores, sink, own, bias):
    sc = _fold(raw_scores, own) + bias
    m = jnp.maximum(jnp.max(sc, axis=-1, keepdims=True), sink)
    p = jnp.exp(sc - m)
    psink = jnp.exp(sink - m)
    inv = 1.0 / (jnp.sum(p, axis=-1, keepdims=True) + psink)
    return p * inv, psink * inv


def _k_prep(k, kw, cosf, sins, ones_bd):
    rstd = lax.rsqrt(_group_mean(k * k, ones_bd) + EPS)
    nrm = k * rstd
    y = nrm * kw
    return nrm, rstd, y * cosf + _partner64(y) * sins


def _qk_prep(proj, qw, kw, cos, sins, ones_q, ones_kv):
    s = proj.shape[0]
    rows = min(KV_ROWS, s)

    def body(p_ref, qw_ref, kw_ref, cos_ref, sin_ref, oq_ref, ok_ref, o_ref):
        j = pl.program_id(1)

        @pl.when(j < KV_COL)
        def _():
            cosf, sinf = _tile_lanes(cos_ref[...], 4), _tile_lanes(sin_ref[...], 4)
            roped = _q_prep(p_ref[...].astype(F32), qw_ref[...], cosf, sinf, oq_ref[...])[2]
            o_ref[...] = (roped * ATTN_SCALE).astype(BF16)

        @pl.when(j == KV_COL)
        def _():
            cosf, sinf = _tile_lanes(cos_ref[...], 2), _tile_lanes(sin_ref[...], 2)
            kr = _k_prep(p_ref[:, :KV_WIDTH].astype(F32), kw_ref[...], cosf, sinf, ok_ref[...])[2]
            o_ref[...] = jnp.concatenate([kr.astype(BF16), p_ref[:, KV_WIDTH:]], axis=1)

    full = lambda shape: pl.BlockSpec(shape, lambda i, j: (0,) * len(shape))
    tab = pl.BlockSpec((rows, SLAB), lambda i, j: (i, 0))
    blk = pl.BlockSpec((rows, GROUP_WIDTH), lambda i, j: (i, j))
    return pl.pallas_call(
        body, name="qk_prep", grid=(s // rows, KV_COL + 1),
        in_specs=[blk, full((1, GROUP_WIDTH)), full((1, KV_WIDTH)), tab, tab, full(ones_q.shape), full(ones_kv.shape)],
        out_specs=blk, out_shape=jax.ShapeDtypeStruct((s, ATTN_WIDTH + 2 * KV_WIDTH), BF16),
        compiler_params=_params("parallel", "arbitrary"))(proj, qw, kw, cos, sins, ones_q, ones_kv)


def _keys_values(kc_ref, kp_ref, vc_ref, vp_ref, head):
    lanes = slice((head // 2) * SLAB, (head // 2 + 1) * SLAB)
    dup = lambda ref: _dup_head(ref[:, lanes].astype(F32), head % 2)
    return (jnp.concatenate([dup(kp_ref), dup(kc_ref)], axis=0).astype(BF16),
            jnp.concatenate([dup(vp_ref), dup(vc_ref)], axis=0).astype(BF16))


FWD_STEP_HEADS = 4
BWD_STEP_HEADS = 2


def _swa_specs(heads):
    kv_width = heads * HEAD_DIM
    prev = lambda n: jnp.maximum(n - 1, 0)
    kv = lambda col0, row: pl.BlockSpec((BLK, kv_width), lambda gs, n: (row(n), col0 // kv_width + gs))
    cur = lambda n: n
    full = lambda shape: pl.BlockSpec(shape, lambda gs, n: (0,) * len(shape))
    gate = lambda t: pl.BlockSpec((BLK, GROUP_WIDTH), lambda gs, n: (n, GATE_COL + heads * gs + t))
    return dict(
        sinks=pl.BlockSpec(memory_space=pltpu.SMEM), gates=[gate(t) for t in range(heads)],
        kc=kv(ATTN_WIDTH, cur), kp=kv(ATTN_WIDTH, prev), vc=kv(ATTN_WIDTH + KV_WIDTH, cur), vp=kv(ATTN_WIDTH + KV_WIDTH, prev),
        tri=full((BLK, BLK)), step=pl.BlockSpec((BLK, heads * GROUP_WIDTH), lambda gs, n: (n, gs)))


def _lower_triangle():
    return jnp.tril(jnp.ones((BLK, BLK), F32))


def _swa_fwd(qk, proj, sinks):
    s = proj.shape[0]
    nb = s // BLK
    heads = FWD_STEP_HEADS
    sp = _swa_specs(heads)

    def body(sink_ref, q_ref, kc_ref, kp_ref, vc_ref, vp_ref, *rest):
        gate_refs, (tri_ref, ag_ref, o_ref) = rest[:heads], rest[heads:]
        gs, n = pl.program_id(0), pl.program_id(1)
        own, bias = _band(tri_ref, n)
        for t, gate_ref in enumerate(gate_refs):
            cols = slice(t * GROUP_WIDTH, (t + 1) * GROUP_WIDTH)
            first_head = (heads * gs + t) * GROUP_HEADS
            kcat, vcat = _keys_values(kc_ref, kp_ref, vc_ref, vp_ref, t)
            scores = _dot_nt(_stack_heads(q_ref[:, cols]), kcat)
            probs = []
            for j in range(GROUP_HEADS):
                p, _ = _head_probs(scores[j * BLK:(j + 1) * BLK], sink_ref[first_head + j], own, bias)
                probs.append(_unfold(p, own).astype(BF16))
            o = _unstack_heads(_dot(jnp.concatenate(probs, axis=0), vcat))
            gate = gate_ref[...].astype(F32)
            o_ref[:, cols] = o.astype(BF16)
            ag_ref[:, cols] = (o * (gate * _sigmoid(gate))).astype(BF16)

    shp = jax.ShapeDtypeStruct((s, ATTN_WIDTH), BF16)
    return pl.pallas_call(
        body, name="swa_fwd", grid=(KV_HEADS // heads, nb),
        in_specs=[sp["sinks"], sp["step"], sp["kc"], sp["kp"], sp["vc"], sp["vp"], *sp["gates"], sp["tri"]],
        out_specs=[sp["step"], sp["step"]], out_shape=[shp, shp],
        compiler_params=_params("parallel", "arbitrary"),
    )(sinks, qk, qk, qk, qk, qk, *[proj] * heads, _lower_triangle())


def _swa_bwd(qk, proj, dag, sinks):
    s = proj.shape[0]
    nb = s // BLK
    heads = BWD_STEP_HEADS
    sp = _swa_specs(heads)

    def body(sink_ref, q_ref, kc_ref, kp_ref, vc_ref, vp_ref, *rest):
        gate_refs = rest[:heads]
        dag_ref, tri_ref, dq_ref, dkc_ref, dkp_ref, dvc_ref, dvp_ref, dsink_ref = rest[heads:]
        gs, n = pl.program_id(0), pl.program_id(1)
        own, bias = _band(tri_ref, n)

        @pl.when(n == 0)
        def _():
            dsink_ref[...] = jnp.zeros_like(dsink_ref)

        for t, gate_ref in enumerate(gate_refs):
            cols = slice(t * GROUP_WIDTH, (t + 1) * GROUP_WIDTH)
            first_head = (heads * gs + t) * GROUP_HEADS
            kcat, vcat = _keys_values(kc_ref, kp_ref, vc_ref, vp_ref, t)
            gate = gate_ref[...].astype(F32)
            do = dag_ref[:, cols].astype(F32) * (gate * _sigmoid(gate))
            q_stack = _stack_heads(q_ref[:, cols])
            do_stack = _stack_heads(do).astype(BF16)
            scores = _dot_nt(q_stack, kcat)
            dprobs = _dot_nt(do_stack, vcat)
            probs, dscores, dsinks = [], [], []
            for j in range(GROUP_HEADS):
                rows = slice(j * BLK, (j + 1) * BLK)
                p, psink = _head_probs(scores[rows], sink_ref[first_head + j], own, bias)
                dp = _fold(dprobs[rows], own)
                delta = jnp.sum(p * dp, axis=-1, keepdims=True)
                probs.append(_unfold(p, own).astype(BF16))
                dscores.append(_unfold(p * (dp - delta), own).astype(BF16))
                dsinks.append(-jnp.sum(psink * delta, axis=0, keepdims=True))
            ds = jnp.concatenate(dscores, axis=0)
            dk = _dot_tn(ds, q_stack)
            dv = _dot_tn(jnp.concatenate(probs, axis=0), do_stack)
            dk = dk + pltpu.roll(dk, HEAD_DIM, 1)
            dv = dv + pltpu.roll(dv, HEAD_DIM, 1)
            dkp_ref[t], dkc_ref[t] = dk[:BLK], dk[BLK:]
            dvp_ref[t], dvc_ref[t] = dv[:BLK], dv[BLK:]
            dq_ref[:, cols] = _unstack_heads(_dot(ds, kcat)).astype(BF16)
            r8 = lax.broadcasted_iota(jnp.int32, (8, SLAB), 0)
            upd = jnp.zeros((8, SLAB), F32)
            for j in range(GROUP_HEADS):
                upd = jnp.where(r8 == j, dsinks[j], upd)
            dsink_ref[t] += upd

    cur_out = pl.BlockSpec((heads, BLK, SLAB), lambda gs, n: (gs, n, 0))
    prev_out = pl.BlockSpec((heads, BLK, SLAB), lambda gs, n: (gs, (n + nb - 1) % nb, 0))
    kv_shape = jax.ShapeDtypeStruct((KV_HEADS, s, SLAB), F32)
    return pl.pallas_call(
        body, name="swa_bwd", grid=(KV_HEADS // heads, nb),
        in_specs=[sp["sinks"], sp["step"], sp["kc"], sp["kp"], sp["vc"], sp["vp"], *sp["gates"], sp["step"], sp["tri"]],
        out_specs=[sp["step"], cur_out, prev_out, cur_out, prev_out,
                   pl.BlockSpec((heads, 8, SLAB), lambda gs, n: (gs, 0, 0))],
        out_shape=[jax.ShapeDtypeStruct((s, ATTN_WIDTH), BF16), kv_shape, kv_shape, kv_shape, kv_shape,
                   jax.ShapeDtypeStruct((KV_HEADS, 8, SLAB), F32)],
        compiler_params=_params("parallel", "arbitrary"),
    )(sinks, qk, qk, qk, qk, qk, *[proj] * heads, dag, _lower_triangle())


def _swa_bwd_finish(proj, dqr, o, dag, dkc, dkp, dvc, dvp, qw, kw, cos, sins, ones_q, ones_kv):
    s = proj.shape[0]
    rows = min(KV_ROWS, s)
    n_q = KV_COL
    q_of = lambda j: jnp.clip(j - 1, 0, n_q - 1)
    gate_of = lambda j: jnp.clip(j - 1 - n_q, 0, n_q - 1)

    def body(kv_ref, p_ref, dqr_ref, o_ref, dag_ref, dkc_ref, dkp_ref, dvc_ref, dvp_ref, qw_ref, kw_ref, cos_ref, sin_ref,
             oq_ref, ok_ref, dp_ref, dqw_ref, dkw_ref):
        j, i = pl.program_id(0), pl.program_id(1)

        @pl.when(j == 0)
        def _():
            @pl.when(i == 0)
            def _():
                dkw_ref[...] = jnp.zeros_like(dkw_ref)

            def assemble(cur_ref, prv_ref):
                tot = [cur_ref[h] + prv_ref[h] for h in range(KV_HEADS)]
                first = _half_mask(tot[0].shape, 0)
                return jnp.concatenate([jnp.where(first, tot[0], tot[1]), jnp.where(first, tot[2], tot[3])], axis=1)

            dkr = assemble(dkc_ref, dkp_ref)
            dv = assemble(dvc_ref, dvp_ref)
            cosf, sinf = _tile_lanes(cos_ref[...], 2), _tile_lanes(sin_ref[...], 2)
            nrm, rstd, _ = _k_prep(kv_ref[:, :KV_WIDTH].astype(F32), kw_ref[...], cosf, sinf, ok_ref[...])
            dy = dkr * cosf + _partner64(dkr * sinf)
            dn = dy * kw_ref[...]
            dk = rstd * (dn - nrm * _group_mean(dn * nrm, ok_ref[...], passes=1))
            dp_ref[...] = jnp.concatenate([dk, dv], axis=1).astype(BF16)
            dkw_ref[...] += jnp.sum(dy * nrm, axis=0, keepdims=True)

        @pl.when(jnp.logical_and(j >= 1, j <= n_q))
        def _():
            @pl.when(i == 0)
            def _():
                dqw_ref[...] = jnp.zeros_like(dqw_ref)

            cosf, sinf = _tile_lanes(cos_ref[...], 4), _tile_lanes(sin_ref[...], 4)
            nrm, rstd, _ = _q_prep(p_ref[...].astype(F32), qw_ref[...], cosf, sinf, oq_ref[...])
            dq = dqr_ref[...].astype(F32) * ATTN_SCALE
            dy = dq * cosf + _partner64(dq * sinf)
            dn = dy * qw_ref[...]
            dp_ref[...] = (rstd * (dn - nrm * _group_mean(dn * nrm, oq_ref[...], passes=1))).astype(BF16)
            dqw_ref[0] += jnp.sum(dy * nrm, axis=0, keepdims=True)

        @pl.when(j > n_q)
        def _():
            gate = p_ref[...].astype(F32)
            sg = _sigmoid(gate)
            dp_ref[...] = (dag_ref[...].astype(F32) * o_ref[...].astype(F32) * (sg * (1.0 + gate * (1.0 - sg)))).astype(BF16)

    first_pass = lambda j, i: jnp.where(j == 0, i, 0)
    acc = pl.BlockSpec((KV_HEADS, rows, SLAB), lambda j, i: (0, first_pass(j, i), 0))
    full = lambda shape: pl.BlockSpec(shape, lambda j, i: (0,) * len(shape))
    tab = pl.BlockSpec((rows, SLAB), lambda j, i: (i, 0))
    out_col = lambda j: jnp.where(j == 0, KV_COL, jnp.where(j <= n_q, j - 1, j))
    return pl.pallas_call(
        body, name="swa_bwd_finish", grid=(2 * n_q + 1, s // rows),
        in_specs=[pl.BlockSpec((rows, GROUP_WIDTH), lambda j, i: (first_pass(j, i), KV_COL)),
                  pl.BlockSpec((rows, GROUP_WIDTH), lambda j, i: (jnp.where(j == 0, 0, i), jnp.where(j <= n_q, q_of(j), j))),
                  pl.BlockSpec((rows, GROUP_WIDTH), lambda j, i: (jnp.where(jnp.logical_and(j >= 1, j <= n_q), i, 0), q_of(j))),
                  pl.BlockSpec((rows, GROUP_WIDTH), lambda j, i: (jnp.where(j > n_q, i, 0), gate_of(j))),
                  pl.BlockSpec((rows, GROUP_WIDTH), lambda j, i: (jnp.where(j > n_q, i, 0), gate_of(j))),
                  acc, acc, acc, acc, full((1, GROUP_WIDTH)), full((1, KV_WIDTH)), tab, tab,
                  full(ones_q.shape), full(ones_kv.shape)],
        out_specs=[pl.BlockSpec((rows, GROUP_WIDTH), lambda j, i: (i, out_col(j))),
                   pl.BlockSpec((1, 8, GROUP_WIDTH), lambda j, i: (q_of(j), 0, 0)), pl.BlockSpec((8, KV_WIDTH), lambda j, i: (0, 0))],
        out_shape=[jax.ShapeDtypeStruct((s, ODD_IN), BF16), jax.ShapeDtypeStruct((n_q, 8, GROUP_WIDTH), F32),
                   jax.ShapeDtypeStruct((8, KV_WIDTH), F32)],
        compiler_params=_params("arbitrary", "arbitrary"),
    )(proj, proj, dqr, o, dag, dkc, dkp, dvc, dvp, qw, kw, cos, sins, ones_q, ones_kv)


def _place():
    x, y, c = lax.axis_index("x"), lax.axis_index("y"), lax.axis_index("c")
    return x, y, c


OTHER_CHIPS = ((1, 0), (0, 1), (1, 1))


def _half_rows(ref, half, rows):
    return ref.at[pl.ds(pl.multiple_of(half * (rows // 2), 8), rows // 2)]


DMA_CHUNK_BYTES = 1 << 20
BF16_TILE_ROWS = 16


def _n_chunks(ref):
    rows = ref.shape[-2]
    nbytes = math.prod(ref.shape) * jnp.dtype(ref.dtype).itemsize
    n = 1
    while 2 * n * DMA_CHUNK_BYTES <= nbytes and rows % (2 * n * BF16_TILE_ROWS) == 0:
        n *= 2
    return n


def _row_chunk(ref, k, n):
    rows = ref.shape[-2] // n
    return ref.at[pl.ds(k * rows, rows)] if len(ref.shape) == 2 else ref.at[:, pl.ds(k * rows, rows)]


def _push(src, dst, send_sem, recv_sem, device_id):
    n = _n_chunks(src)
    for k in range(n):
        pltpu.make_async_remote_copy(src_ref=_row_chunk(src, k, n), dst_ref=_row_chunk(dst, k, n), send_sem=send_sem,
                                     recv_sem=recv_sem, device_id=device_id, device_id_type=MESH).start()
    return pltpu.make_async_remote_copy(src_ref=src, dst_ref=dst, send_sem=send_sem, recv_sem=recv_sem,
                                        device_id=device_id, device_id_type=MESH)


HBM = pl.BlockSpec(memory_space=pltpu.HBM)
SEM = pl.BlockSpec(memory_space=pltpu.SEMAPHORE)
SPLIT_COPY_EFFECT = pltpu.SideEffectType.DATAFLOW_SIDE_EFFECTING


def _in_hbm(a):
    return pltpu.with_memory_space_constraint(a, pltpu.HBM)


def _start_copies(name, arrays, plan, n_copies, after=None):
    n = len(arrays)

    def body(*refs):
        send_sem, recv_sem = refs[-n - 3], refs[-n - 2]
        for k, (src, dst, peer) in enumerate(plan(refs[:n])):
            _push(src, dst, send_sem.at[k], recv_sem.at[k], peer)
        refs[-1][...] = jnp.zeros_like(refs[-1])

    dma = pltpu.SemaphoreType.DMA((n_copies,))
    outs = pl.pallas_call(
        body, name=name,
        out_shape=(dma, dma, *[pltpu.HBM(a.shape, a.dtype) for a in arrays], jax.ShapeDtypeStruct((8, 128), F32)),
        in_specs=[HBM] * n + ([ANY] if after is not None else []),
        out_specs=(SEM, SEM, *[HBM] * n, pl.BlockSpec(memory_space=pltpu.VMEM)),
        input_output_aliases={i: i + 2 for i in range(n)},
        compiler_params=pltpu.CompilerParams(has_side_effects=SPLIT_COPY_EFFECT),
    )(*[_in_hbm(a) for a in arrays], *((after,) if after is not None else ()))
    return outs[0], outs[1], list(outs[2:2 + n]), outs[-1]


def _wait_copies(name, send_sem, recv_sem, arrays, plan, after):
    n = len(arrays)
    after = list(after) if isinstance(after, (list, tuple)) else [after]

    def body(*refs):
        send_ref, recv_ref = refs[n], refs[n + 1]
        for k, (src, dst, peer) in enumerate(plan(refs[:n])):
            cp = pltpu.make_async_remote_copy(src_ref=src, dst_ref=dst, send_sem=send_ref.at[k], recv_sem=recv_ref.at[k],
                                              device_id=peer, device_id_type=MESH)
            cp.wait_send()
            cp.wait_recv()

    return list(pl.pallas_call(
        body, name=name, out_shape=tuple(pltpu.HBM(a.shape, a.dtype) for a in arrays),
        in_specs=[HBM] * n + [SEM, SEM] + [ANY] * len(after), out_specs=tuple([HBM] * n),
        input_output_aliases={i: i for i in range(n)},
        compiler_params=pltpu.CompilerParams(has_side_effects=SPLIT_COPY_EFFECT),
    )(*arrays, send_sem, recv_sem, *after))


def _gather_region(full, kind, chip, half=None):
    if kind == "whole":
        return full.at[chip]
    if kind == "col" or isinstance(kind, tuple):
        part, parts = (0, 1) if kind == "col" else kind[1:]
        rows, width = full.shape[0], full.shape[1] // N_CHIPS
        piece = full.at[:, pl.ds(pl.multiple_of(chip * width + part * (width // parts), LANE_TILE), width // parts)]
    else:
        rows = full.shape[0] // N_CHIPS
        piece = full.at[pl.ds(pl.multiple_of(chip * rows, BF16_TILE_ROWS), rows)]
    return piece if half is None else _half_rows(piece, half, rows)


def _gather_plan(kinds):
    def plan(fulls):
        x, y, c = _place()
        copies = []
        for fx, fy in OTHER_CHIPS:
            for full, kind in zip(fulls, kinds):
                mine = _gather_region(full, kind, 2 * x + y, c)
                copies.append((mine, mine, (x ^ fx, y ^ fy, c)))
        return copies

    return plan


def _gather_start(name, fulls, kinds, after=None):
    return _start_copies(name, list(fulls), _gather_plan(kinds), 3 * len(kinds), after)


def _pass_on_plan(kinds):
    split = [i for i, kind in enumerate(kinds) if kind != "whole"]

    def plan(fulls):
        x, y, c = _place()
        copies = []
        for fx, fy in OTHER_CHIPS:
            chip = 2 * (x ^ fx) + (y ^ fy)
            for i in split:
                landed = _gather_region(fulls[i], kinds[i], chip, c)
                copies.append((landed, landed, (x, y, 1 - c)))
        return copies

    return plan, 3 * len(split)


def _gather_arrive(name, started, kinds, after, arrays=None):
    send_sem, recv_sem, fulls, _ = started
    fulls = _wait_copies(name + "_wait", send_sem, recv_sem, fulls if arrays is None else arrays, _gather_plan(kinds), after)
    plan, count = _pass_on_plan(kinds)
    return _start_copies(name + "_pass_on", fulls, plan, count)


def _gather_done(name, passed_on, kinds, after):
    send_sem, recv_sem, fulls, _ = passed_on
    return _wait_copies(name + "_pass_on_wait", send_sem, recv_sem, fulls, _pass_on_plan(kinds)[0], after)


def _allreduce_small(v):
    def body(v_ref, out_ref, buf, send_sems, recv_sems):
        x, y, c = _place()
        me = 4 * x + 2 * y + c
        buf[me] = v_ref[...]
        copies = []
        for r in range(1, N_DEV):
            peer = (x ^ (r >> 2), y ^ ((r >> 1) & 1), c ^ (r & 1))
            cp = pltpu.make_async_remote_copy(src_ref=v_ref, dst_ref=buf.at[me], send_sem=send_sems.at[r - 1],
                                              recv_sem=recv_sems.at[r - 1], device_id=peer, device_id_type=MESH)
            cp.start()
            copies.append(cp)
        for cp in copies:
            cp.wait_recv()
        for cp in copies:
            cp.wait_send()
        total = buf[0]
        for d in range(1, N_DEV):
            total = total + buf[d]
        out_ref[...] = total

    vm = pl.BlockSpec(memory_space=pltpu.VMEM)
    return pl.pallas_call(
        body, name="allreduce_small", in_specs=[vm], out_specs=vm, out_shape=jax.ShapeDtypeStruct(v.shape, v.dtype),
        scratch_shapes=[pltpu.VMEM((N_DEV,) + v.shape, v.dtype), pltpu.SemaphoreType.DMA((N_DEV - 1,)),
                        pltpu.SemaphoreType.DMA((N_DEV - 1,))],
        compiler_params=pltpu.CompilerParams(has_side_effects=True),
    )(v)


def _exchange_plan(n):
    def plan(refs):
        x, y, c = _place()
        copies = []
        for g, theirs in zip(refs[:n], refs[n:]):
            half = g.shape[1] // 2
            src = g.at[:, pl.ds(pl.multiple_of((1 - c) * half, BF16_TILE_ROWS), half)]
            copies.append((src, theirs, (x, y, 1 - c)))
        return copies

    return plan


def _pair_sum(g, theirs, core, name):
    pieces, half, cols = theirs.shape
    tr = min(half, 256)
    per = half // tr

    def body(core_ref, g_ref, t_ref, o_ref):
        del core_ref
        o_ref[...] = (g_ref[...].astype(F32) + t_ref[...].astype(F32)).astype(BF16)

    spec = pl.BlockSpec((1, tr, cols), lambda p, i, core_ref: (p, i, 0))
    return pl.pallas_call(
        body, name=name, out_shape=jax.ShapeDtypeStruct(theirs.shape, BF16),
        grid_spec=pltpu.PrefetchScalarGridSpec(
            num_scalar_prefetch=1, grid=(pieces, per),
            in_specs=[pl.BlockSpec((1, tr, cols), lambda p, i, core_ref: (p, core_ref[0] * per + i, 0)), spec],
            out_specs=spec),
        compiler_params=_params("parallel", "parallel"))(core, g, theirs)


def _scatter_plan(n):
    def plan(refs):
        parts, stacks = refs[:n], refs[n:]
        x, y, c = _place()
        copies = []
        for fx, fy in OTHER_CHIPS:
            chip = 2 * (x ^ fx) + (y ^ fy)
            for part, stack in zip(parts, stacks):
                if part.shape[0] == N_CHIPS:
                    piece = part.at[chip]
                else:
                    width = part.shape[2] // N_CHIPS
                    piece = part.at[0].at[:, pl.ds(pl.multiple_of(chip * width, 128), width)]
                copies.append((piece, stack.at[2 * x + y], (x ^ fx, y ^ fy, c)))
        return copies

    return plan


def _scatter_start(name, parts, after=None):
    def landing(a):
        return (N_CHIPS, a.shape[1], a.shape[2] if a.shape[0] == N_CHIPS else a.shape[2] // N_CHIPS)

    stacks = [lax.empty(landing(a), a.dtype) for a in parts]
    return _start_copies(name, list(parts) + stacks, _scatter_plan(len(parts)), 3 * len(parts), after)


def _scatter_finish(name, started, after):
    send_sem, recv_sem, arrays, _ = started
    n = len(arrays) // 2
    arrays = _wait_copies(name + "_wait", send_sem, recv_sem, arrays, _scatter_plan(n), after)
    return arrays[:n], arrays[n:]


def _sum_chips(part, stack, place, name):
    _, r, c = stack.shape
    tr = 256
    per = r // tr

    def body(place_ref, own_ref, a_ref, b_ref, c_ref, o_ref):
        del place_ref
        total = own_ref[0].astype(F32)
        for ref in (a_ref, b_ref, c_ref):
            total = total + ref[0].astype(F32)
        o_ref[...] = total

    if part.shape[0] == N_CHIPS:
        own = pl.BlockSpec((1, tr, c), lambda i, pr: (pr[1], i, 0))
    else:
        own = pl.BlockSpec((1, tr, c), lambda i, pr: (0, i, pr[1]))
    other = lambda flip: pl.BlockSpec((1, tr, c), lambda i, pr: (pr[1] ^ flip, i, 0))
    return pl.pallas_call(
        body, name=name, out_shape=jax.ShapeDtypeStruct((2 * r, c), F32),
        grid_spec=pltpu.PrefetchScalarGridSpec(
            num_scalar_prefetch=1, grid=(per,), in_specs=[own, other(2), other(1), other(3)],
            out_specs=pl.BlockSpec((tr, c), lambda i, pr: (pr[0] * per + i, 0))),
        compiler_params=_params("parallel"))(place, part, stack, stack, stack)


def _own_half_plan(fulls):
    x, y, c = _place()
    return [(mine, mine, (x, y, 1 - c)) for mine in (_half_rows(full, c, full.shape[0]) for full in fulls)]


MM = dict(tm=2048, tn=1024, tk=2048)
MM_LONG_K = dict(tm=1024, tn=1024, tk=4096)


def _local_step(x, target, ev_norm_w, q_norm_w, k_norm_w, sinks, own_first, weights_first, late_arrived, weights_late, grads_out):
    s = x.shape[0]
    cos_r, sin_r = _rope_tables(s, RET_DIM)
    cos_a, sin_a = _rope_tables(s, HEAD_DIM)
    cos_a = np.tile(cos_a, (1, 4))
    sins_a = np.tile(np.concatenate([-sin_a, sin_a], axis=1), (1, 2))
    tables = _retention_tables()
    ones_q, ones_kv = _block_diag_mean(min(GROUP_WIDTH, MXU_WIDTH)), _block_diag_mean(min(KV_WIDTH, MXU_WIDTH))
    qw_g = jnp.tile(q_norm_w, (1, GROUP_WIDTH // HEAD_DIM))
    kw_kv = jnp.tile(k_norm_w, (1, KV_WIDTH // HEAD_DIM))
    sinks1 = sinks.reshape(Q_HEADS)

    own_w_in0, own_block, start_token = own_first
    h0 = _rmsnorm(x, ev_norm_w, "norm0", after=start_token)
    shifted = dict(shift=own_block * (own_w_in0.shape[1] // MM["tn"]), total=EVEN_IN // MM["tn"], tm=MM["tm"], tn=MM["tn"],
                   out_dtype=BF16)
    own_blocks = own_w_in0.shape[1] // MM["tn"]
    proj0 = _mm_shifted(h0, own_w_in0, b_shifted=False, first=0, count=own_blocks, name="proj0_own", **shifted)
    for part in range(own_blocks):
        w_in0, small_weights, token = weights_first(part, proj0)
        if small_weights is not None:
            conv_w, od_norm_w = small_weights
        proj0 = _mm_shifted(h0, w_in0, b_shifted=True, first=own_blocks + part, stride=own_blocks,
                            count=shifted["total"] // own_blocks - 1, name=f"proj0_rest{part}", into=proj0, after=token, **shifted)
    o_ret, cat, states = _retention_fwd(proj0, cos_r, sin_r, tables, after=late_arrived(proj0))
    cat = _conv_fwd(proj0, conv_w, cat)
    w_out0, w_in1, w_out1 = weights_late(cat)

    def residual_and_norm(prod, x_ref, w_ref, x1_ref, h1_ref):
        x1v = x_ref[...] + prod
        x1_ref[...] = x1v
        rstd = lax.rsqrt(jnp.mean(x1v * x1v, axis=-1, keepdims=True) + EPS)
        h1_ref[...] = (x1v * rstd * w_ref[...]).astype(BF16)

    def residual_and_loss(prod, x1_ref, t_ref, dyb_ref, sq_ref):
        diff = (x1_ref[...] + prod) - t_ref[...]
        dyb_ref[...] = (diff * (1.0 / D_MODEL)).astype(BF16)

        @pl.when(pl.program_id(0) == 0)
        def _():
            sq_ref[...] = jnp.zeros_like(sq_ref)

        sq_ref[...] += jnp.sum(jnp.sum(diff * diff, axis=1, keepdims=True), axis=0, keepdims=True)

    act = lambda dt: jax.ShapeDtypeStruct((s, D_MODEL), dt)
    x1, h1 = _mm_rows(cat, w_out0, [x], [od_norm_w], [act(F32), act(BF16)], residual_and_norm, tm=min(s, 512), name="out0")
    proj1 = _mm(h1, w_in1, mode="nn", out_dtype=BF16, name="proj1", tm=2048, tn=1536, tk=2048)
    qk = _qk_prep(proj1, qw_g, kw_kv, cos_a, sins_a, ones_q, ones_kv)
    ag, o_att = _swa_fwd(qk, proj1, sinks1)
    dy_b, sq = _mm_rows(ag, w_out1, [x1, target], [], [act(BF16), jax.ShapeDtypeStruct((8, 128), F32)],
                        residual_and_loss, tm=min(s, 512), name="out1")

    g_w_out1 = _mm(ag, dy_b, mode="tn", out_dtype=BF16, name="g_w_out1", **MM_LONG_K)
    dag = _mm(dy_b, w_out1, mode="nt", out_dtype=BF16, name="d_ag", **MM)
    dqr, dkc, dkp, dvc, dvp, dsink = _swa_bwd(qk, proj1, dag, sinks1)
    dproj1, dqw, dkw = _swa_bwd_finish(proj1, dqr, o_att, dag, dkc, dkp, dvc, dvp, qw_g, kw_kv, cos_a, sins_a, ones_q, ones_kv)
    g_w_in1 = _mm(h1, dproj1, mode="tn", out_dtype=BF16, name="g_w_in1", tm=1024, tn=768, tk=4096)
    dh1 = _mm(dproj1, w_in1, mode="nt", out_dtype=BF16, name="d_h1", tm=1024, tn=1024, tk=ODD_IN)
    dx1_b, g_norm1 = _rmsnorm_bwd(x1, od_norm_w, dh1, dy_b, "norm1_bwd", BF16)

    g_w_out0 = _mm(cat, dx1_b, mode="tn", out_dtype=BF16, name="g_w_out0", **MM_LONG_K)
    token = grads_out.begin("upper", (("od_w_in", g_w_in1, "col"), ("od_w_out", g_w_out1, "row"), ("ev_w_out", g_w_out0, "row")))
    dcat = _mm(dx1_b, w_out0, mode="nt", out_dtype=BF16, name="d_cat", after=token, **MM)
    token = grads_out.send("upper", dcat)
    dproj0 = _retention_bwd(proj0, o_ret, states, dcat, cos_r, sin_r, tables)
    dproj0, g_conv = _conv_bwd(proj0, dcat, conv_w, dproj0)
    g_w_in0 = _mm(h0, dproj0, mode="tn", out_dtype=BF16, name="g_w_in0", after=token, **MM_LONG_K)
    token = grads_out.begin("in0", (("ev_w_in", g_w_in0, "col"),))
    half_blocks = s // (2 * MM_LONG_K["tm"])
    dh0 = _mm(dproj0, w_in0, mode="nt", out_dtype=BF16, name="d_h0_top", after=token, m_blocks=(0, half_blocks), **MM_LONG_K)
    dh0 = _mm(dproj0, w_in0, mode="nt", out_dtype=BF16, name="d_h0_bottom", after=grads_out.send("in0", dh0),
              m_blocks=(half_blocks, half_blocks), into=dh0, **MM_LONG_K)
    grad_x, g_norm0 = _rmsnorm_bwd(x, ev_norm_w, dh0, dx1_b, "norm0_bwd", F32)

    g_qw = dqw[:, 0, :].reshape(Q_HEADS, HEAD_DIM).sum(axis=0)
    g_kw = dkw[0].reshape(KV_HEADS, HEAD_DIM).sum(axis=0)
    g_sinks = dsink[:, :, 0].reshape(Q_HEADS)
    small = dict(ev_norm=g_norm0[0], od_norm=g_norm1[0], conv=g_conv[:3], qw=g_qw, kw=g_kw, sinks=g_sinks)
    return sq[0, 0], grad_x, small


LANE_TILE = 128


def _pack_small_grads(small, sq):
    pad = lambda v, n: jnp.pad(v, (0, n - v.shape[0]))
    tail = jnp.concatenate([pad(small["qw"], LANE_TILE), pad(small["kw"], LANE_TILE), small["sinks"]])
    rows = [small["ev_norm"], small["od_norm"]] + [small["conv"][t] for t in range(3)] + [tail, sq.reshape(1)]
    rows += [jnp.zeros((1,), F32)] * (8 - len(rows))
    return jnp.stack([pad(r, D_MODEL) for r in rows])


def _adamw_small(tot, chip, params):
    n = len(params)

    def body(chip_ref, tot_ref, *refs):
        ins, outs = refs[:3 * n], refs[3 * n:]
        c = chip_ref[0]

        def own(rows, width):
            blocks = [tot_ref[rows, k * width:(k + 1) * width] for k in range(N_CHIPS)]
            g = blocks[-1]
            for k in reversed(range(N_CHIPS - 1)):
                g = jnp.where(c == k, blocks[k], g)
            return g

        grads = [tot_ref[0:1, :], own(slice(2, 5), CONV_WIDTH // N_CHIPS), own(slice(1, 2), D_MODEL // N_CHIPS),
                 tot_ref[5:6, 0:HEAD_DIM], tot_ref[5:6, LANE_TILE:LANE_TILE + HEAD_DIM],
                 tot_ref[5:6, 2 * LANE_TILE:2 * LANE_TILE + Q_HEADS]]
        for i, g in enumerate(grads):
            w_ref, m_ref, v_ref = ins[3 * i:3 * i + 3]
            g_out, d_out, nm_out, nv_out = outs[4 * i:4 * i + 4]
            at = (0,) if len(w_ref.shape) == 3 else ()
            delta, nm, nv = _adamw_math(w_ref[at] if at else w_ref[...], g, m_ref[at] if at else m_ref[...],
                                        v_ref[at] if at else v_ref[...])
            for ref, val in ((g_out, g), (d_out, delta), (nm_out, nm), (nv_out, nv)):
                if at:
                    ref[0] = val
                else:
                    ref[...] = val

    vm = pl.BlockSpec(memory_space=pltpu.VMEM)
    flat = [a for p in params for a in p]
    outs = pl.pallas_call(
        body, name="adamw_small", in_specs=[pl.BlockSpec(memory_space=pltpu.SMEM), vm] + [vm] * len(flat),
        out_specs=[vm] * (4 * n), out_shape=[jax.ShapeDtypeStruct(p[0].shape, F32) for p in params for _ in range(4)],
    )(chip, tot, *flat)
    return [tuple(outs[4 * i:4 * i + 4]) for i in range(n)]


class _ReduceScatter:
    def __init__(self, place):
        self.place = place
        self.exchanging = {}
        self.started = []
        self.sharing = {}

    def begin(self, tag, grads):
        pieces = [g[None] if kind == "col" else g.reshape(N_CHIPS, g.shape[0] // N_CHIPS, g.shape[1]) for _, g, kind in grads]
        landing = [lax.empty((p.shape[0], p.shape[1] // 2, p.shape[2]), p.dtype) for p in pieces]
        started = _start_copies("exchange_" + tag, pieces + landing, _exchange_plan(len(pieces)), len(pieces))
        self.exchanging[tag] = ([nm for nm, _, _ in grads], started)
        return started[3]

    def send(self, tag, after):
        group, (send_sem, recv_sem, arrays, _) = self.exchanging[tag]
        n = len(group)
        arrays = _wait_copies("exchange_" + tag + "_wait", send_sem, recv_sem, arrays, _exchange_plan(n), after)
        parts = [_pair_sum(g, t, self.place[:1], "pair_sum_" + nm) for g, t, nm in zip(arrays[:n], arrays[n:], group)]
        started = _scatter_start("scatter_" + tag, parts)
        token = self.sum_up(started[3])
        self.started.append((tag, group, started))
        return started[3] if token is None else token

    def sum_up(self, after):
        token = None
        for tag, group, started in self.started:
            if tag not in self.sharing:
                parts, stacks = _scatter_finish("scatter_" + tag, started, after)
                sums = [_sum_chips(p, s, self.place, "chip_sum_" + nm) for p, s, nm in zip(parts, stacks, group)]
                self.sharing[tag] = (group, _start_copies("share_" + tag, sums, _own_half_plan, len(sums)))
                token = self.sharing[tag][1][3]
        return token

    def result(self, tag, after):
        group, (send_sem, recv_sem, sums, _) = self.sharing[tag]
        return dict(zip(group, _wait_copies("share_" + tag + "_wait", send_sem, recv_sem, sums, _own_half_plan, after)))


def kernel(x, ev_norm_w, ev_w_in, ev_conv_w, ev_w_out, od_norm_w, od_w_in, od_q_norm_w, od_k_norm_w, od_sinks, od_w_out, loss_target, m_ev_norm_w, m_ev_w_in, m_ev_conv_w, m_ev_w_out, m_od_norm_w, m_od_w_in, m_od_q_norm_w, m_od_k_norm_w, m_od_sinks, m_od_w_out, v_ev_norm_w, v_ev_w_in, v_ev_conv_w, v_ev_w_out, v_od_norm_w, v_od_w_in, v_od_q_norm_w, v_od_k_norm_w, v_od_sinks, v_od_w_out):
    my_chip = 2 * lax.axis_index("x") + lax.axis_index("y")
    place = jnp.stack([lax.axis_index("c"), my_chip]).astype(jnp.int32)
    shard_w = D_MODEL // N_CHIPS
    conv_shard = CONV_WIDTH // N_CHIPS

    small_in = jnp.zeros((8, shard_w), F32)
    small_in = small_in.at[0].set(od_norm_w[0]).at[1:4, :conv_shard].set(ev_conv_w[0])
    small_in = lax.dynamic_update_slice(jnp.zeros((N_CHIPS, 8, shard_w), F32), small_in[None], (my_chip, 0, 0))
    chip = place[1:]
    first_kinds, second_kinds, late_kinds = (("col", 0, 2), "whole"), (("col", 1, 2),), ("row", "col", "row")
    w_in0_own_place, w_in0_shard = _cast_into_gathered(ev_w_in[0], "col", chip, "cast_w_in0", keep_shard=True)
    first = _gather_start("gather_first", [w_in0_own_place, small_in], first_kinds)
    second = _gather_start("gather_second", [first[2][0]], second_kinds)
    late_own = [_cast_into_gathered(ev_w_out[0], "row", chip, "cast_w_out0"),
                _cast_into_gathered(od_w_in[0], "col", chip, "cast_w_in1"),
                _cast_into_gathered(od_w_out[0], "row", chip, "cast_w_out1")]
    late, w_in0_so_far = [], []

    def weights_first(part, after):
        if part == 0:
            passed_on = _gather_arrive("gather_first", first, first_kinds, after, arrays=[second[2][0], first[2][1]])
            w_in0, small_all = _gather_done("gather_first", passed_on, first_kinds, [passed_on[3]] + late_own)
            w_in0_so_far.append(w_in0)
            late.append(_gather_start("gather_late", late_own, late_kinds, after=small_all))
            od_norm_full = small_all[:, 0, :].reshape(1, D_MODEL)
            conv_full = jnp.transpose(small_all[:, 1:4, :conv_shard], (1, 0, 2)).reshape(3, CONV_WIDTH)
            return w_in0, (conv_full, od_norm_full), late[0][3]
        passed_on = _gather_arrive("gather_second", second, second_kinds, after, arrays=w_in0_so_far)
        (w_in0,) = _gather_done("gather_second", passed_on, second_kinds, passed_on[3])
        return w_in0, None, None

    def late_arrived(after):
        late.append(_gather_arrive("gather_late", late[0], late_kinds, after))
        return late[1][3]

    def weights_late(after):
        return _gather_done("gather_late", late[1], late_kinds, after)

    reduce_scatter = _ReduceScatter(place)
    sq, grad_x, small = _local_step(x[0], loss_target[0], ev_norm_w, od_q_norm_w, od_k_norm_w, od_sinks,
                                    (w_in0_shard, chip, second[3]), weights_first, late_arrived, weights_late,
                                    reduce_scatter)

    token = reduce_scatter.sum_up(grad_x)
    upd = {}
    upper = reduce_scatter.result("upper", token)
    upd["od_w_in"] = _adamw(od_w_in[0], upper["od_w_in"], m_od_w_in[0], v_od_w_in[0], "adamw_od_w_in")
    upd["od_w_out"] = _adamw(od_w_out[0], upper["od_w_out"], m_od_w_out[0], v_od_w_out[0], "adamw_od_w_out")
    upd["ev_w_out"] = _adamw(ev_w_out[0], upper["ev_w_out"], m_ev_w_out[0], v_ev_w_out[0], "adamw_ev_w_out")
    g_ev_w_in = reduce_scatter.result("in0", [upd[nm][1] for nm in ("od_w_in", "od_w_out", "ev_w_out")])["ev_w_in"]
    upd["ev_w_in"] = _adamw(ev_w_in[0], g_ev_w_in, m_ev_w_in[0], v_ev_w_in[0], "adamw_ev_w_in")
    tot = _allreduce_small(_pack_small_grads(small, sq))
    loss = 0.5 * tot[6, 0] / D_MODEL

    smalls = (("ev_norm_w", ev_norm_w, m_ev_norm_w, v_ev_norm_w), ("ev_conv_w", ev_conv_w, m_ev_conv_w, v_ev_conv_w),
              ("od_norm_w", od_norm_w, m_od_norm_w, v_od_norm_w), ("od_q_norm_w", od_q_norm_w, m_od_q_norm_w, v_od_q_norm_w),
              ("od_k_norm_w", od_k_norm_w, m_od_k_norm_w, v_od_k_norm_w), ("od_sinks", od_sinks, m_od_sinks, v_od_sinks))
    for (nm, _, _, _), result in zip(smalls, _adamw_small(tot, chip, [p[1:] for p in smalls])):
        upd[nm] = result
    for nm in ("ev_w_in", "ev_w_out", "od_w_in", "od_w_out"):
        upd[nm] = tuple(u[None] for u in upd[nm])
    order = ("ev_norm_w", "ev_w_in", "ev_conv_w", "ev_w_out", "od_norm_w", "od_w_in", "od_q_norm_w", "od_k_norm_w", "od_sinks", "od_w_out")
    return (loss, grad_x[None], *[upd[nm][0] for nm in order], *[upd[nm][1] for nm in order],
            *[upd[nm][2] for nm in order], *[upd[nm][3] for nm in order])
```

```python
import math

import jax
import jax.numpy as jnp
import numpy as np
from jax import lax
from jax.experimental import pallas as pl
from jax.experimental.pallas import tpu as pltpu

F32 = jnp.float32
BF16 = jnp.bfloat16

D_MODEL = 2048
RET_HEADS = 4
RET_DIM = 256
RET_WIDTH = 1024
CONV_WIDTH = 1024
EVEN_IN = 8192
Q_HEADS = 32
HEAD_DIM = 64
KV_HEADS = 4
KV_WIDTH = 256
ATTN_WIDTH = 2048
ODD_IN = 4608
BLK = 128
ROPE_THETA = 10000.0
EPS = 1e-6
ADAM_LR = 0.001
ADAM_B1 = 0.9
ADAM_B2 = 0.999
ADAM_EPS = 1e-08
ADAM_WD = 0.01
ADAM_STEP = 10
N_CHIPS = 4
N_DEV = 8
VMEM_LIMIT_BYTES = 56 * 1024 * 1024
MESH = pl.DeviceIdType.MESH
ANY = pl.BlockSpec(memory_space=pl.ANY)


def _params(*sem):
    return pltpu.CompilerParams(dimension_semantics=sem, vmem_limit_bytes=VMEM_LIMIT_BYTES)


def _dot(a, b):
    return jnp.dot(a, b, preferred_element_type=F32)


def _dot_nt(a, b):
    return lax.dot_general(a, b, (((1,), (1,)), ((), ())), preferred_element_type=F32)


def _dot_tn(a, b):
    return lax.dot_general(a, b, (((0,), (0,)), ((), ())), preferred_element_type=F32)


def _sigmoid(x):
    return 1.0 / (1.0 + jnp.exp(-x))


def _mm(a, b, *, mode, tm, tn, tk, out_dtype, name, add=None, after=None, m_blocks=None, into=None):
    if mode == "nn":
        (m, k), n = a.shape, b.shape[1]
    elif mode == "nt":
        (m, k), n = a.shape, b.shape[0]
    else:
        (k, m), n = a.shape, b.shape[1]
    tm, tn, tk = min(tm, m), min(tn, n), min(tk, k)
    assert m % tm == 0 and n % tn == 0 and k % tk == 0, (name, m, n, k)
    nk = k // tk
    first_m, count_m = (0, m // tm) if m_blocks is None else m_blocks
    dot = {"nn": _dot, "nt": _dot_nt, "tn": _dot_tn}[mode]
    a_spec = (pl.BlockSpec((tk, tm), lambda i, j, kk: (kk, i + first_m)) if mode == "tn"
              else pl.BlockSpec((tm, tk), lambda i, j, kk: (i + first_m, kk)))
    b_spec = (pl.BlockSpec((tn, tk), lambda i, j, kk: (j, kk)) if mode == "nt"
              else pl.BlockSpec((tk, tn), lambda i, j, kk: (kk, j)))
    o_spec = pl.BlockSpec((tm, tn), lambda i, j, kk: (i + first_m, j))
    has_add = add is not None

    def body(*refs):
        a_ref, b_ref = refs[0], refs[1]
        add_ref = refs[2] if has_add else None
        o_ref, acc_ref = refs[-2], refs[-1]
        p = dot(a_ref[...], b_ref[...])

        def finish(total):
            if has_add:
                total = total + add_ref[...].astype(F32)
            o_ref[...] = total.astype(out_dtype)

        if nk == 1:
            finish(p)
        else:
            kk = pl.program_id(2)

            @pl.when(kk == 0)
            def _():
                acc_ref[...] = p

            @pl.when(jnp.logical_and(kk > 0, kk < nk - 1))
            def _():
                acc_ref[...] += p

            @pl.when(kk == nk - 1)
            def _():
                finish(acc_ref[...] + p)

    extra = [arr for arr in (after, into) if arr is not None]
    in_specs = [a_spec, b_spec] + ([o_spec] if has_add else []) + [ANY] * len(extra)
    args = (a, b) + ((add,) if has_add else ()) + tuple(extra)
    return pl.pallas_call(
        body, name=name, grid=(count_m, n // tn, nk), in_specs=in_specs, out_specs=o_spec,
        out_shape=jax.ShapeDtypeStruct((m, n), out_dtype),
        input_output_aliases={len(args) - 1: 0} if into is not None else {},
        scratch_shapes=[pltpu.VMEM((tm, tn) if nk > 1 else (8, 128), F32)],
        compiler_params=_params("parallel", "parallel", "arbitrary"),
    )(*args)


def _mm_shifted(a, b, shift, *, b_shifted, first, count, total, tm, tn, out_dtype, name, stride=1, into=None, after=None):
    m, k = a.shape
    tm = min(tm, m)
    assert m % tm == 0
    col = lambda j, shift_ref: (shift_ref[0] + first + stride * j) % total
    extra = [arr for arr in (into, after) if arr is not None]

    def body(shift_ref, a_ref, b_ref, *rest):
        del shift_ref
        rest[-1][...] = _dot(a_ref[...], b_ref[...]).astype(out_dtype)

    return pl.pallas_call(
        body, name=name, out_shape=jax.ShapeDtypeStruct((m, total * tn), out_dtype),
        grid_spec=pltpu.PrefetchScalarGridSpec(
            num_scalar_prefetch=1, grid=(m // tm, count),
            in_specs=[pl.BlockSpec((tm, k), lambda i, j, s: (i, 0)),
                      pl.BlockSpec((k, tn), (lambda i, j, s: (0, col(j, s))) if b_shifted else (lambda i, j, s: (0, j)))]
            + [ANY] * len(extra),
            out_specs=pl.BlockSpec((tm, tn), lambda i, j, s: (i, col(j, s)))),
        input_output_aliases={3: 0} if into is not None else {},
        compiler_params=_params("parallel", "arbitrary"))(shift, a, b, *extra)


def _mm_rows(a, b, rows_in, vecs_in, out_shapes, epilogue, *, tm, name):
    m, k = a.shape
    n = b.shape[1]
    assert m % tm == 0
    row = pl.BlockSpec((tm, n), lambda i: (i, 0))

    def body(a_ref, b_ref, *rest):
        epilogue(_dot(a_ref[...], b_ref[...]), *rest)

    out_specs = [row if tuple(s.shape) == (m, n) else pl.BlockSpec(s.shape, lambda i: (0, 0)) for s in out_shapes]
    return pl.pallas_call(
        body, name=name, grid=(m // tm,),
        in_specs=[pl.BlockSpec((tm, k), lambda i: (i, 0)), pl.BlockSpec((k, n), lambda i: (0, 0))] + [row] * len(rows_in)
        + [pl.BlockSpec((1, n), lambda i: (0, 0))] * len(vecs_in),
        out_specs=out_specs, out_shape=out_shapes, compiler_params=_params("arbitrary"),
    )(a, b, *rows_in, *vecs_in)


def _cast_into_gathered(w, kind, chip, name, keep_shard=False):
    r, c = w.shape
    tr = min(r, 512)
    per = r // tr

    def body(chip_ref, w_ref, *outs):
        del chip_ref
        for o_ref in outs:
            o_ref[...] = w_ref[...].astype(BF16)

    if kind == "col":
        shape, out_map = (r, N_CHIPS * c), (lambda i, chip_ref: (i, chip_ref[0]))
    else:
        shape, out_map = (N_CHIPS * r, c), (lambda i, chip_ref: (chip_ref[0] * per + i, 0))
    plain = pl.BlockSpec((tr, c), lambda i, chip_ref: (i, 0))
    out = pl.pallas_call(
        body, name=name,
        out_shape=[jax.ShapeDtypeStruct(shape, BF16)] + ([jax.ShapeDtypeStruct((r, c), BF16)] if keep_shard else []),
        grid_spec=pltpu.PrefetchScalarGridSpec(
            num_scalar_prefetch=1, grid=(per,), in_specs=[plain],
            out_specs=[pl.BlockSpec((tr, c), out_map)] + ([plain] if keep_shard else [])),
        compiler_params=_params("parallel"))(chip, w)
    return out if keep_shard else out[0]


NORM_ROWS = 512


def _rmsnorm(x, w, name, after=None):
    s, d = x.shape
    tr = NORM_ROWS

    def body(x_ref, w_ref, *rest):
        xv = x_ref[...]
        rstd = lax.rsqrt(jnp.mean(xv * xv, axis=-1, keepdims=True) + EPS)
        rest[-1][...] = (xv * rstd * w_ref[...]).astype(BF16)

    return pl.pallas_call(
        body, name=name, grid=(s // tr,),
        in_specs=[pl.BlockSpec((tr, d), lambda i: (i, 0)), pl.BlockSpec((1, d), lambda i: (0, 0))]
        + ([ANY] if after is not None else []),
        out_specs=pl.BlockSpec((tr, d), lambda i: (i, 0)),
        out_shape=jax.ShapeDtypeStruct((s, d), BF16), compiler_params=_params("parallel"),
    )(x, w, *((after,) if after is not None else ()))


def _rmsnorm_bwd(x, w, dh, dres, name, out_dtype):
    s, d = x.shape
    tr = NORM_ROWS

    def body(x_ref, w_ref, dh_ref, dres_ref, dx_ref, dw_ref):
        xv = x_ref[...]
        rstd = lax.rsqrt(jnp.mean(xv * xv, axis=-1, keepdims=True) + EPS)
        nrm = xv * rstd
        dhv = dh_ref[...].astype(F32)
        dn = dhv * w_ref[...]
        dx = dres_ref[...].astype(F32) + rstd * (dn - nrm * jnp.mean(dn * nrm, axis=-1, keepdims=True))
        dx_ref[...] = dx.astype(out_dtype)

        @pl.when(pl.program_id(0) == 0)
        def _():
            dw_ref[...] = jnp.zeros_like(dw_ref)

        dw_ref[...] += jnp.sum(dhv * nrm, axis=0, keepdims=True)

    row = pl.BlockSpec((tr, d), lambda i: (i, 0))
    return pl.pallas_call(
        body, name=name, grid=(s // tr,),
        in_specs=[row, pl.BlockSpec((1, d), lambda i: (0, 0)), row, row],
        out_specs=[row, pl.BlockSpec((8, d), lambda i: (0, 0))],
        out_shape=[jax.ShapeDtypeStruct((s, d), out_dtype), jax.ShapeDtypeStruct((8, d), F32)],
        compiler_params=_params("arbitrary"))(x, w, dh, dres)


def _adamw_math(w, g, m, v):
    nm = ADAM_B1 * m + (1.0 - ADAM_B1) * g
    nv = ADAM_B2 * v + (1.0 - ADAM_B2) * (g * g)
    m_hat = nm / (1.0 - ADAM_B1 ** ADAM_STEP)
    v_hat = nv / (1.0 - ADAM_B2 ** ADAM_STEP)
    return -ADAM_LR * (m_hat / (jnp.sqrt(v_hat) + ADAM_EPS) + ADAM_WD * w), nm, nv


def _adamw(w, g, m, v, name):
    r, c = w.shape
    tr = min(r, 256)
    assert r % tr == 0

    def body(w_ref, g_ref, m_ref, v_ref, g_out, d_ref, nm_ref, nv_ref):
        gv = g_ref[...]
        g_out[...] = gv
        d_ref[...], nm_ref[...], nv_ref[...] = _adamw_math(w_ref[...], gv, m_ref[...], v_ref[...])

    spec = pl.BlockSpec((tr, c), lambda i: (i, 0))
    shp = jax.ShapeDtypeStruct((r, c), F32)
    return pl.pallas_call(body, name=name, grid=(r // tr,), in_specs=[spec] * 4, out_specs=[spec] * 4,
                          out_shape=[shp] * 4, compiler_params=_params("parallel"))(w, g, m, v)


def _rope_tables(s, dim):
    inv = (1.0 / (ROPE_THETA ** (np.arange(0, dim, 2, dtype=np.float64) / dim))).astype(np.float32)
    ang = (np.arange(s, dtype=np.float32)[:, None] * inv[None, :]).astype(np.float64)
    return np.cos(ang).astype(np.float32), np.sin(ang).astype(np.float32)


def _rope_half(x, cos, sin):
    h = x.shape[1] // 2
    x1, x2 = x[:, :h], x[:, h:]
    return jnp.concatenate([x1 * cos - x2 * sin, x2 * cos + x1 * sin], axis=1)


def _unrope_half(dy, cos, sin):
    h = dy.shape[1] // 2
    d1, d2 = dy[:, :h], dy[:, h:]
    return jnp.concatenate([d1 * cos + d2 * sin, d2 * cos - d1 * sin], axis=1)


def _lane(shape):
    return lax.broadcasted_iota(jnp.int32, shape, 1)


def _partner64(x):
    w = x.shape[1]
    first = (_lane(x.shape) % HEAD_DIM) < (HEAD_DIM // 2)
    return jnp.where(first, pltpu.roll(x, w - HEAD_DIM // 2, 1), pltpu.roll(x, HEAD_DIM // 2, 1))


def _tile_lanes(t, reps):
    return t if reps == 1 else jnp.concatenate([t] * reps, axis=1)


MXU_WIDTH = 256


def _group_mean(x, ones_bd, passes=2):
    width, tile = x.shape[1], ones_bd.shape[0]
    if width > tile:
        return jnp.concatenate([_group_mean(x[:, c:c + tile], ones_bd, passes) for c in range(0, width, tile)], axis=1)
    hi = x.astype(BF16)
    if passes == 1:
        return _dot(hi, ones_bd)
    lo = (x - hi.astype(F32)).astype(BF16)
    return _dot(hi, ones_bd) + _dot(lo, ones_bd)


def _block_diag_mean(width):
    idx = jnp.arange(width) // HEAD_DIM
    return jnp.where(idx[:, None] == idx[None, :], 1.0 / HEAD_DIM, 0.0).astype(BF16)


RET_STEP = 4


def _retention_tables():
    h = RET_HEADS
    log_g = jnp.log(1.0 - 2.0 ** (-5.0 - jnp.arange(h, dtype=F32)))
    idx = jnp.arange(BLK, dtype=F32)
    diff = idx[:, None] - idx[None, :]
    intra = jnp.where(diff >= 0, jnp.exp(log_g[:, None, None] * jnp.maximum(diff, 0.0)), 0.0).astype(F32)
    q_dec = jnp.exp(log_g[:, None] * (idx[None, :] + 1.0)).astype(F32)[:, :, None]
    k_dec = jnp.exp(log_g[:, None] * (BLK - 1.0 - idx[None, :])).astype(F32)[:, :, None]
    chunk_dec = jnp.exp(log_g * BLK).astype(F32)[:, None, None]
    return intra, q_dec, k_dec, chunk_dec


def _retention_fwd(proj, cos, sin, tables, after=None):
    s = proj.shape[0]
    nc = s // BLK
    intra, q_dec, k_dec, chunk_dec = tables

    def body(p_ref, cos_ref, sin_ref, in_ref, qd_ref, kd_ref, cd_ref, *rest):
        o_ref, cat_ref, st_ref, state = rest[-4:]

        @pl.when(pl.program_id(0) == 0)
        def _():
            state[...] = jnp.zeros_like(state)

        for c in range(RET_STEP):
            rows = slice(c * BLK, (c + 1) * BLK)
            cosv, sinv = cos_ref[rows, :], sin_ref[rows, :]
            for h in range(RET_HEADS):
                c0 = h * RET_DIM
                q = p_ref[rows, c0:c0 + RET_DIM].astype(F32)
                k = p_ref[rows, RET_WIDTH + c0:RET_WIDTH + c0 + RET_DIM].astype(F32)
                v = p_ref[rows, 2 * RET_WIDTH + c0:2 * RET_WIDTH + c0 + RET_DIM]
                g = p_ref[rows, 3 * RET_WIDTH + c0:3 * RET_WIDTH + c0 + RET_DIM].astype(F32)
                qb = _rope_half(q, cosv, sinv).astype(BF16)
                kr = _rope_half(k, cosv, sinv) * (RET_DIM ** -0.5)
                kb = kr.astype(BF16)
                scores = _dot_nt(qb, kb) * in_ref[h]
                inner = _dot(scores.astype(BF16), v)
                prev = state[h]
                prev_b = prev.astype(BF16)
                st_ref[h, c] = prev_b
                o = inner + _dot(qb, prev_b) * qd_ref[h]
                o_ref[rows, c0:c0 + RET_DIM] = o
                rstd = lax.rsqrt(jnp.mean(o * o, axis=-1, keepdims=True) + EPS)
                cat_ref[rows, c0:c0 + RET_DIM] = (o * rstd * (g * _sigmoid(g))).astype(BF16)
                state[h] = cd_ref[h] * prev + _dot_tn((kr * kd_ref[h]).astype(BF16), v)

    full = lambda shape: pl.BlockSpec(shape, lambda n: (0,) * len(shape))
    step = RET_STEP * BLK
    return pl.pallas_call(
        body, name="retention_fwd", grid=(nc // RET_STEP,),
        in_specs=[pl.BlockSpec((step, 4 * RET_WIDTH), lambda n: (n, 0)),
                  pl.BlockSpec((step, RET_DIM // 2), lambda n: (n, 0)), pl.BlockSpec((step, RET_DIM // 2), lambda n: (n, 0)),
                  full((RET_HEADS, BLK, BLK)), full((RET_HEADS, BLK, 1)), full((RET_HEADS, BLK, 1)), full((RET_HEADS, 1, 1))]
        + ([ANY] if after is not None else []),
        out_specs=[pl.BlockSpec((step, RET_WIDTH), lambda n: (n, 0)), pl.BlockSpec((step, RET_WIDTH), lambda n: (n, 0)),
                   pl.BlockSpec((RET_HEADS, RET_STEP, RET_DIM, RET_DIM), lambda n: (0, n, 0, 0))],
        out_shape=[jax.ShapeDtypeStruct((s, RET_WIDTH), F32), jax.ShapeDtypeStruct((s, D_MODEL), BF16),
                   jax.ShapeDtypeStruct((RET_HEADS, nc, RET_DIM, RET_DIM), BF16)],
        scratch_shapes=[pltpu.VMEM((RET_HEADS, RET_DIM, RET_DIM), F32)],
        compiler_params=_params("arbitrary"),
    )(proj, cos, sin, intra, q_dec, k_dec, chunk_dec, *((after,) if after is not None else ()))


def _retention_bwd(proj, o, states, dcat, cos, sin, tables):
    s = proj.shape[0]
    nc = s // BLK
    intra, q_dec, k_dec, chunk_dec = tables

    def body(p_ref, o_ref, st_ref, dc_ref, cos_ref, sin_ref, in_ref, qd_ref, kd_ref, cd_ref, dp_ref, dstate):
        @pl.when(pl.program_id(0) == 0)
        def _():
            dstate[...] = jnp.zeros_like(dstate)

        for c in reversed(range(RET_STEP)):
            rows = slice(c * BLK, (c + 1) * BLK)
            cosv, sinv = cos_ref[rows, :], sin_ref[rows, :]
            for h in range(RET_HEADS):
                c0 = h * RET_DIM
                q = p_ref[rows, c0:c0 + RET_DIM].astype(F32)
                k = p_ref[rows, RET_WIDTH + c0:RET_WIDTH + c0 + RET_DIM].astype(F32)
                v = p_ref[rows, 2 * RET_WIDTH + c0:2 * RET_WIDTH + c0 + RET_DIM]
                g = p_ref[rows, 3 * RET_WIDTH + c0:3 * RET_WIDTH + c0 + RET_DIM].astype(F32)
                o = o_ref[rows, c0:c0 + RET_DIM]
                dc = dc_ref[rows, c0:c0 + RET_DIM].astype(F32)
                rstd = lax.rsqrt(jnp.mean(o * o, axis=-1, keepdims=True) + EPS)
                nrm = o * rstd
                sg = _sigmoid(g)
                dg = dc * nrm * (sg * (1.0 + g * (1.0 - sg)))
                dn = dc * (g * sg)
                do = rstd * (dn - nrm * jnp.mean(dn * nrm, axis=-1, keepdims=True))
                qb = _rope_half(q, cosv, sinv).astype(BF16)
                kr = _rope_half(k, cosv, sinv) * (RET_DIM ** -0.5)
                kb = kr.astype(BF16)
                mask = in_ref[h]
                qd, kd = qd_ref[h], kd_ref[h]
                prev_b = st_ref[h, c]
                dnext = dstate[h]
                dnext_b = dnext.astype(BF16)
                att = (_dot_nt(qb, kb) * mask).astype(BF16)
                do_b = do.astype(BF16)
                doq = (do * qd).astype(BF16)
                dv = _dot_tn(att, do_b) + _dot((kr * kd).astype(BF16), dnext_b)
                ds = (_dot_nt(do_b, v) * mask).astype(BF16)
                dqr = _dot(ds, kb) + _dot_nt(doq, prev_b)
                dkr = _dot_tn(ds, qb) + _dot_nt(v, dnext_b) * kd
                dstate[h] = cd_ref[h] * dnext + _dot_tn(qb, doq)
                dq = _unrope_half(dqr, cosv, sinv)
                dk = _unrope_half(dkr * (RET_DIM ** -0.5), cosv, sinv)
                dp_ref[rows, c0:c0 + RET_DIM] = dq.astype(BF16)
                dp_ref[rows, RET_WIDTH + c0:RET_WIDTH + c0 + RET_DIM] = dk.astype(BF16)
                dp_ref[rows, 2 * RET_WIDTH + c0:2 * RET_WIDTH + c0 + RET_DIM] = dv.astype(BF16)
                dp_ref[rows, 3 * RET_WIDTH + c0:3 * RET_WIDTH + c0 + RET_DIM] = dg.astype(BF16)

    steps = nc // RET_STEP
    rev = lambda n: steps - 1 - n
    full = lambda shape: pl.BlockSpec(shape, lambda n: (0,) * len(shape))
    step = RET_STEP * BLK
    return pl.pallas_call(
        body, name="retention_bwd", grid=(steps,),
        in_specs=[pl.BlockSpec((step, 4 * RET_WIDTH), lambda n: (rev(n), 0)),
                  pl.BlockSpec((step, RET_WIDTH), lambda n: (rev(n), 0)),
                  pl.BlockSpec((RET_HEADS, RET_STEP, RET_DIM, RET_DIM), lambda n: (0, rev(n), 0, 0)),
                  pl.BlockSpec((step, RET_WIDTH), lambda n: (rev(n), 0)),
                  pl.BlockSpec((step, RET_DIM // 2), lambda n: (rev(n), 0)), pl.BlockSpec((step, RET_DIM // 2), lambda n: (rev(n), 0)),
                  full((RET_HEADS, BLK, BLK)), full((RET_HEADS, BLK, 1)), full((RET_HEADS, BLK, 1)), full((RET_HEADS, 1, 1))],
        out_specs=pl.BlockSpec((step, 4 * RET_WIDTH), lambda n: (rev(n), 0)),
        out_shape=jax.ShapeDtypeStruct((s, EVEN_IN), BF16),
        scratch_shapes=[pltpu.VMEM((RET_HEADS, RET_DIM, RET_DIM), F32)],
        compiler_params=_params("arbitrary"))(proj, o, states, dcat, cos, sin, intra, q_dec, k_dec, chunk_dec)


CONV_ROWS = 256
HALO = 16


def _conv_pieces(p, halo, first):
    gb, gc, u, gv = (p[:, i * CONV_WIDTH:(i + 1) * CONV_WIDTH] for i in range(4))
    cu = gc * u
    hcu = halo[:, CONV_WIDTH:2 * CONV_WIDTH] * halo[:, 2 * CONV_WIDTH:3 * CONV_WIDTH]
    hcu = jnp.where(first, 0.0, hcu)
    r1, r2 = hcu[HALO - 1:HALO], hcu[HALO - 2:HALO - 1]
    row = lax.broadcasted_iota(jnp.int32, cu.shape, 0)
    m1 = jnp.where(row == 0, r1, pltpu.roll(cu, 1, 0))
    m2 = jnp.where(row == 0, r2, jnp.where(row == 1, r1, pltpu.roll(cu, 2, 0)))
    return gb, gc, u, gv, cu, m1, m2


def _conv_fwd(proj, conv_w, cat):
    s = proj.shape[0]
    per = CONV_ROWS // HALO

    def body(p_ref, halo_ref, w_ref, cat_in, cat_ref):
        del cat_in
        first = pl.program_id(0) == 0
        gb, _, _, gv, cu, m1, m2 = _conv_pieces(p_ref[...].astype(F32), halo_ref[...].astype(F32), first)
        conv = w_ref[0:1, :] * m2 + w_ref[1:2, :] * m1 + w_ref[2:3, :] * cu
        cat_ref[...] = (gb * conv * (gv * _sigmoid(gv))).astype(BF16)

    return pl.pallas_call(
        body, name="conv_fwd", grid=(s // CONV_ROWS,),
        in_specs=[pl.BlockSpec((CONV_ROWS, 4 * CONV_WIDTH), lambda i: (i, 1)),
                  pl.BlockSpec((HALO, 4 * CONV_WIDTH), lambda i: (jnp.maximum(i * per - 1, 0), 1)),
                  pl.BlockSpec((3, CONV_WIDTH), lambda i: (0, 0)), ANY],
        out_specs=pl.BlockSpec((CONV_ROWS, CONV_WIDTH), lambda i: (i, 1)),
        out_shape=jax.ShapeDtypeStruct(cat.shape, cat.dtype), input_output_aliases={3: 0},
        compiler_params=_params("parallel"))(proj, proj, conv_w, cat)


def _conv_bwd(proj, dcat, conv_w, dproj):
    s = proj.shape[0]
    per = CONV_ROWS // HALO
    last_halo = s // HALO - 1
    nsteps = s // CONV_ROWS

    def body(p_ref, halo_ref, nxt_ref, dc_ref, dnxt_ref, w_ref, dp_in, dp_ref, dw_ref):
        del dp_in
        i = pl.program_id(0)
        gb, gc, u, gv, cu, m1, m2 = _conv_pieces(p_ref[...].astype(F32), halo_ref[...].astype(F32), i == 0)
        w0, w1, w2 = w_ref[0:1, :], w_ref[1:2, :], w_ref[2:3, :]
        conv = w0 * m2 + w1 * m1 + w2 * cu
        dco = dc_ref[...].astype(F32)
        sg = _sigmoid(gv)
        silu = gv * sg
        dgb = dco * conv * silu
        dco_gb = dco * gb
        dgv = dco_gb * conv * (sg * (1.0 + gv * (1.0 - sg)))
        dconv = dco_gb * silu
        nxt = nxt_ref[...].astype(F32)
        ngv = nxt[:, 3 * CONV_WIDTH:]
        dnext = dnxt_ref[...].astype(F32) * nxt[:, :CONV_WIDTH] * (ngv * _sigmoid(ngv))
        dnext = jnp.where(i == nsteps - 1, 0.0, dnext)
        n1, n2 = dnext[0:1], dnext[1:2]
        row = lax.broadcasted_iota(jnp.int32, dconv.shape, 0)
        p1 = jnp.where(row == CONV_ROWS - 1, n1, pltpu.roll(dconv, CONV_ROWS - 1, 0))
        p2 = jnp.where(row == CONV_ROWS - 1, n2, jnp.where(row == CONV_ROWS - 2, n1, pltpu.roll(dconv, CONV_ROWS - 2, 0)))
        dcu = w2 * dconv + w1 * p1 + w0 * p2
        dp_ref[...] = jnp.concatenate([dgb, dcu * u, dcu * gc, dgv], axis=1).astype(BF16)

        @pl.when(i == 0)
        def _():
            dw_ref[...] = jnp.zeros_like(dw_ref)

        taps = [jnp.sum(dconv * m, axis=0, keepdims=True) for m in (m2, m1, cu)]
        r8 = lax.broadcasted_iota(jnp.int32, dw_ref.shape, 0)
        dw_ref[...] += jnp.where(r8 == 0, taps[0], jnp.where(r8 == 1, taps[1], jnp.where(r8 == 2, taps[2], 0.0)))

    return pl.pallas_call(
        body, name="conv_bwd", grid=(nsteps,),
        in_specs=[pl.BlockSpec((CONV_ROWS, 4 * CONV_WIDTH), lambda i: (i, 1)),
                  pl.BlockSpec((HALO, 4 * CONV_WIDTH), lambda i: (jnp.maximum(i * per - 1, 0), 1)),
                  pl.BlockSpec((HALO, 4 * CONV_WIDTH), lambda i: (jnp.minimum((i + 1) * per, last_halo), 1)),
                  pl.BlockSpec((CONV_ROWS, CONV_WIDTH), lambda i: (i, 1)),
                  pl.BlockSpec((HALO, CONV_WIDTH), lambda i: (jnp.minimum((i + 1) * per, last_halo), 1)),
                  pl.BlockSpec((3, CONV_WIDTH), lambda i: (0, 0)), ANY],
        out_specs=[pl.BlockSpec((CONV_ROWS, 4 * CONV_WIDTH), lambda i: (i, 1)), pl.BlockSpec((8, CONV_WIDTH), lambda i: (0, 0))],
        out_shape=[jax.ShapeDtypeStruct(dproj.shape, dproj.dtype), jax.ShapeDtypeStruct((8, CONV_WIDTH), F32)],
        input_output_aliases={6: 0},
        compiler_params=_params("arbitrary"))(proj, proj, proj, dcat, dcat, conv_w, dproj)


GROUP_HEADS = Q_HEADS // KV_HEADS
GROUP_WIDTH = GROUP_HEADS * HEAD_DIM
SLAB = 2 * HEAD_DIM
KV_COL = ATTN_WIDTH // GROUP_WIDTH
GATE_COL = KV_COL + 1
ATTN_SCALE = HEAD_DIM ** -0.5
KV_ROWS = 1024


def _half_mask(shape, which):
    return (_lane(shape) // HEAD_DIM) == which


def _dup_head(slab, which):
    kept = jnp.where(_half_mask(slab.shape, which), slab, 0.0)
    return kept + pltpu.roll(kept, HEAD_DIM, 1)


def _stack_heads(x):
    parts = []
    for sl in range(GROUP_WIDTH // SLAB):
        slab = x[:, sl * SLAB:(sl + 1) * SLAB]
        parts += [jnp.where(_half_mask(slab.shape, e), slab, 0.0) for e in range(2)]
    return jnp.concatenate(parts, axis=0)


def _unstack_heads(y):
    slabs = []
    for sl in range(GROUP_WIDTH // SLAB):
        a, b = y[(2 * sl) * BLK:(2 * sl + 1) * BLK], y[(2 * sl + 1) * BLK:(2 * sl + 2) * BLK]
        slabs.append(jnp.where(_half_mask(a.shape, 0), a, b))
    return jnp.concatenate(slabs, axis=1)


def _q_prep(q, qw, cosf, sins, ones_bd):
    rstd = lax.rsqrt(_group_mean(q * q, ones_bd) + EPS)
    nrm = q * rstd
    y = nrm * qw
    return nrm, rstd, y * cosf + _partner64(y) * sins


def _band(tri_ref, n):
    own = tri_ref[...] > 0.5
    return own, jnp.where(jnp.logical_and(n == 0, jnp.logical_not(own)), -1e30, 0.0)


def _fold(pair, own):
    return jnp.where(own, pair[:, BLK:], pair[:, :BLK])


def _unfold(folded, own):
    return jnp.concatenate([jnp.where(own, 0.0, folded), jnp.where(own, folded, 0.0)], axis=1)


def _head_probs(raw_scores, sink, own, bias):
    sc = _fold(raw_scores, own) + bias
    m = jnp.maximum(jnp.max(sc, axis=-1, keepdims=True), sink)
    p = jnp.exp(sc - m)
    psink = jnp.exp(sink - m)
    inv = 1.0 / (jnp.sum(p, axis=-1, keepdims=True) + psink)
    return p * inv, psink * inv


def _k_prep(k, kw, cosf, sins, ones_bd):
    rstd = lax.rsqrt(_group_mean(k * k, ones_bd) + EPS)
    nrm = k * rstd
    y = nrm * kw
    return nrm, rstd, y * cosf + _partner64(y) * sins


def _qk_prep(proj, qw, kw, cos, sins, ones_q, ones_kv):
    s = proj.shape[0]
    rows = min(KV_ROWS, s)

    def body(p_ref, qw_ref, kw_ref, cos_ref, sin_ref, oq_ref, ok_ref, o_ref):
        j = pl.program_id(1)

        @pl.when(j < KV_COL)
        def _():
            cosf, sinf = _tile_lanes(cos_ref[...], 4), _tile_lanes(sin_ref[...], 4)
            roped = _q_prep(p_ref[...].astype(F32), qw_ref[...], cosf, sinf, oq_ref[...])[2]
            o_ref[...] = (roped * ATTN_SCALE).astype(BF16)

        @pl.when(j == KV_COL)
        def _():
            cosf, sinf = _tile_lanes(cos_ref[...], 2), _tile_lanes(sin_ref[...], 2)
            kr = _k_prep(p_ref[:, :KV_WIDTH].astype(F32), kw_ref[...], cosf, sinf, ok_ref[...])[2]
            o_ref[...] = jnp.concatenate([kr.astype(BF16), p_ref[:, KV_WIDTH:]], axis=1)

    full = lambda shape: pl.BlockSpec(shape, lambda i, j: (0,) * len(shape))
    tab = pl.BlockSpec((rows, SLAB), lambda i, j: (i, 0))
    blk = pl.BlockSpec((rows, GROUP_WIDTH), lambda i, j: (i, j))
    return pl.pallas_call(
        body, name="qk_prep", grid=(s // rows, KV_COL + 1),
        in_specs=[blk, full((1, GROUP_WIDTH)), full((1, KV_WIDTH)), tab, tab, full(ones_q.shape), full(ones_kv.shape)],
        out_specs=blk, out_shape=jax.ShapeDtypeStruct((s, ATTN_WIDTH + 2 * KV_WIDTH), BF16),
        compiler_params=_params("parallel", "arbitrary"))(proj, qw, kw, cos, sins, ones_q, ones_kv)


def _keys_values(kc_ref, kp_ref, vc_ref, vp_ref, head):
    lanes = slice((head // 2) * SLAB, (head // 2 + 1) * SLAB)
    dup = lambda ref: _dup_head(ref[:, lanes].astype(F32), head % 2)
    return (jnp.concatenate([dup(kp_ref), dup(kc_ref)], axis=0).astype(BF16),
            jnp.concatenate([dup(vp_ref), dup(vc_ref)], axis=0).astype(BF16))


FWD_STEP_HEADS = 4
BWD_STEP_HEADS = 2


def _swa_specs(heads):
    kv_width = heads * HEAD_DIM
    prev = lambda n: jnp.maximum(n - 1, 0)
    kv = lambda col0, row: pl.BlockSpec((BLK, kv_width), lambda gs, n: (row(n), col0 // kv_width + gs))
    cur = lambda n: n
    full = lambda shape: pl.BlockSpec(shape, lambda gs, n: (0,) * len(shape))
    gate = lambda t: pl.BlockSpec((BLK, GROUP_WIDTH), lambda gs, n: (n, GATE_COL + heads * gs + t))
    return dict(
        sinks=pl.BlockSpec(memory_space=pltpu.SMEM), gates=[gate(t) for t in range(heads)],
        kc=kv(ATTN_WIDTH, cur), kp=kv(ATTN_WIDTH, prev), vc=kv(ATTN_WIDTH + KV_WIDTH, cur), vp=kv(ATTN_WIDTH + KV_WIDTH, prev),
        tri=full((BLK, BLK)), step=pl.BlockSpec((BLK, heads * GROUP_WIDTH), lambda gs, n: (n, gs)))


def _lower_triangle():
    return jnp.tril(jnp.ones((BLK, BLK), F32))


def _swa_fwd(qk, proj, sinks):
    s = proj.shape[0]
    nb = s // BLK
    heads = FWD_STEP_HEADS
    sp = _swa_specs(heads)

    def body(sink_ref, q_ref, kc_ref, kp_ref, vc_ref, vp_ref, *rest):
        gate_refs, (tri_ref, ag_ref, o_ref) = rest[:heads], rest[heads:]
        gs, n = pl.program_id(0), pl.program_id(1)
        own, bias = _band(tri_ref, n)
        for t, gate_ref in enumerate(gate_refs):
            cols = slice(t * GROUP_WIDTH, (t + 1) * GROUP_WIDTH)
            first_head = (heads * gs + t) * GROUP_HEADS
            kcat, vcat = _keys_values(kc_ref, kp_ref, vc_ref, vp_ref, t)
            scores = _dot_nt(_stack_heads(q_ref[:, cols]), kcat)
            probs = []
            for j in range(GROUP_HEADS):
                p, _ = _head_probs(scores[j * BLK:(j + 1) * BLK], sink_ref[first_head + j], own, bias)
                probs.append(_unfold(p, own).astype(BF16))
            o = _unstack_heads(_dot(jnp.concatenate(probs, axis=0), vcat))
            gate = gate_ref[...].astype(F32)
            o_ref[:, cols] = o.astype(BF16)
            ag_ref[:, cols] = (o * (gate * _sigmoid(gate))).astype(BF16)

    shp = jax.ShapeDtypeStruct((s, ATTN_WIDTH), BF16)
    return pl.pallas_call(
        body, name="swa_fwd", grid=(KV_HEADS // heads, nb),
        in_specs=[sp["sinks"], sp["step"], sp["kc"], sp["kp"], sp["vc"], sp["vp"], *sp["gates"], sp["tri"]],
        out_specs=[sp["step"], sp["step"]], out_shape=[shp, shp],
        compiler_params=_params("parallel", "arbitrary"),
    )(sinks, qk, qk, qk, qk, qk, *[proj] * heads, _lower_triangle())


def _swa_bwd(qk, proj, dag, sinks):
    s = proj.shape[0]
    nb = s // BLK
    heads = BWD_STEP_HEADS
    sp = _swa_specs(heads)

    def body(sink_ref, q_ref, kc_ref, kp_ref, vc_ref, vp_ref, *rest):
        gate_refs = rest[:heads]
        dag_ref, tri_ref, dq_ref, dkc_ref, dkp_ref, dvc_ref, dvp_ref, dsink_ref = rest[heads:]
        gs, n = pl.program_id(0), pl.program_id(1)
        own, bias = _band(tri_ref, n)

        @pl.when(n == 0)
        def _():
            dsink_ref[...] = jnp.zeros_like(dsink_ref)

        for t, gate_ref in enumerate(gate_refs):
            cols = slice(t * GROUP_WIDTH, (t + 1) * GROUP_WIDTH)
            first_head = (heads * gs + t) * GROUP_HEADS
            kcat, vcat = _keys_values(kc_ref, kp_ref, vc_ref, vp_ref, t)
            gate = gate_ref[...].astype(F32)
            do = dag_ref[:, cols].astype(F32) * (gate * _sigmoid(gate))
            q_stack = _stack_heads(q_ref[:, cols])
            do_stack = _stack_heads(do).astype(BF16)
            scores = _dot_nt(q_stack, kcat)
            dprobs = _dot_nt(do_stack, vcat)
            probs, dscores, dsinks = [], [], []
            for j in range(GROUP_HEADS):
                rows = slice(j * BLK, (j + 1) * BLK)
                p, psink = _head_probs(scores[rows], sink_ref[first_head + j], own, bias)
                dp = _fold(dprobs[rows], own)
                delta = jnp.sum(p * dp, axis=-1, keepdims=True)
                probs.append(_unfold(p, own).astype(BF16))
                dscores.append(_unfold(p * (dp - delta), own).astype(BF16))
                dsinks.append(-jnp.sum(psink * delta, axis=0, keepdims=True))
            ds = jnp.concatenate(dscores, axis=0)
            dk = _dot_tn(ds, q_stack)
            dv = _dot_tn(jnp.concatenate(probs, axis=0), do_stack)
            dk = dk + pltpu.roll(dk, HEAD_DIM, 1)
            dv = dv + pltpu.roll(dv, HEAD_DIM, 1)
            dkp_ref[t], dkc_ref[t] = dk[:BLK], dk[BLK:]
            dvp_ref[t], dvc_ref[t] = dv[:BLK], dv[BLK:]
            dq_ref[:, cols] = _unstack_heads(_dot(ds, kcat)).astype(BF16)
            r8 = lax.broadcasted_iota(jnp.int32, (8, SLAB), 0)
            upd = jnp.zeros((8, SLAB), F32)
            for j in range(GROUP_HEADS):
                upd = jnp.where(r8 == j, dsinks[j], upd)
            dsink_ref[t] += upd

    cur_out = pl.BlockSpec((heads, BLK, SLAB), lambda gs, n: (gs, n, 0))
    prev_out = pl.BlockSpec((heads, BLK, SLAB), lambda gs, n: (gs, (n + nb - 1) % nb, 0))
    kv_shape = jax.ShapeDtypeStruct((KV_HEADS, s, SLAB), F32)
    return pl.pallas_call(
        body, name="swa_bwd", grid=(KV_HEADS // heads, nb),
        in_specs=[sp["sinks"], sp["step"], sp["kc"], sp["kp"], sp["vc"], sp["vp"], *sp["gates"], sp["step"], sp["tri"]],
        out_specs=[sp["step"], cur_out, prev_out, cur_out, prev_out,
                   pl.BlockSpec((heads, 8, SLAB), lambda gs, n: (gs, 0, 0))],
        out_shape=[jax.ShapeDtypeStruct((s, ATTN_WIDTH), BF16), kv_shape, kv_shape, kv_shape, kv_shape,
                   jax.ShapeDtypeStruct((KV_HEADS, 8, SLAB), F32)],
        compiler_params=_params("parallel", "arbitrary"),
    )(sinks, qk, qk, qk, qk, qk, *[proj] * heads, dag, _lower_triangle())


def _swa_bwd_finish(proj, dqr, o, dag, dkc, dkp, dvc, dvp, qw, kw, cos, sins, ones_q, ones_kv):
    s = proj.shape[0]
    rows = min(KV_ROWS, s)
    n_q = KV_COL
    q_of = lambda j: jnp.clip(j - 1, 0, n_q - 1)
    gate_of = lambda j: jnp.clip(j - 1 - n_q, 0, n_q - 1)

    def body(kv_ref, p_ref, dqr_ref, o_ref, dag_ref, dkc_ref, dkp_ref, dvc_ref, dvp_ref, qw_ref, kw_ref, cos_ref, sin_ref,
             oq_ref, ok_ref, dp_ref, dqw_ref, dkw_ref):
        j, i = pl.program_id(0), pl.program_id(1)

        @pl.when(j == 0)
        def _():
            @pl.when(i == 0)
            def _():
                dkw_ref[...] = jnp.zeros_like(dkw_ref)

            def assemble(cur_ref, prv_ref):
                tot = [cur_ref[h] + prv_ref[h] for h in range(KV_HEADS)]
                first = _half_mask(tot[0].shape, 0)
                return jnp.concatenate([jnp.where(first, tot[0], tot[1]), jnp.where(first, tot[2], tot[3])], axis=1)

            dkr = assemble(dkc_ref, dkp_ref)
            dv = assemble(dvc_ref, dvp_ref)
            cosf, sinf = _tile_lanes(cos_ref[...], 2), _tile_lanes(sin_ref[...], 2)
            nrm, rstd, _ = _k_prep(kv_ref[:, :KV_WIDTH].astype(F32), kw_ref[...], cosf, sinf, ok_ref[...])
            dy = dkr * cosf + _partner64(dkr * sinf)
            dn = dy * kw_ref[...]
            dk = rstd * (dn - nrm * _group_mean(dn * nrm, ok_ref[...], passes=1))
            dp_ref[...] = jnp.concatenate([dk, dv], axis=1).astype(BF16)
            dkw_ref[...] += jnp.sum(dy * nrm, axis=0, keepdims=True)

        @pl.when(jnp.logical_and(j >= 1, j <= n_q))
        def _():
            @pl.when(i == 0)
            def _():
                dqw_ref[...] = jnp.zeros_like(dqw_ref)

            cosf, sinf = _tile_lanes(cos_ref[...], 4), _tile_lanes(sin_ref[...], 4)
            nrm, rstd, _ = _q_prep(p_ref[...].astype(F32), qw_ref[...], cosf, sinf, oq_ref[...])
            dq = dqr_ref[...].astype(F32) * ATTN_SCALE
            dy = dq * cosf + _partner64(dq * sinf)
            dn = dy * qw_ref[...]
            dp_ref[...] = (rstd * (dn - nrm * _group_mean(dn * nrm, oq_ref[...], passes=1))).astype(BF16)
            dqw_ref[0] += jnp.sum(dy * nrm, axis=0, keepdims=True)

        @pl.when(j > n_q)
        def _():
            gate = p_ref[...].astype(F32)
            sg = _sigmoid(gate)
            dp_ref[...] = (dag_ref[...].astype(F32) * o_ref[...].astype(F32) * (sg * (1.0 + gate * (1.0 - sg)))).astype(BF16)

    first_pass = lambda j, i: jnp.where(j == 0, i, 0)
    acc = pl.BlockSpec((KV_HEADS, rows, SLAB), lambda j, i: (0, first_pass(j, i), 0))
    full = lambda shape: pl.BlockSpec(shape, lambda j, i: (0,) * len(shape))
    tab = pl.BlockSpec((rows, SLAB), lambda j, i: (i, 0))
    out_col = lambda j: jnp.where(j == 0, KV_COL, jnp.where(j <= n_q, j - 1, j))
    return pl.pallas_call(
        body, name="swa_bwd_finish", grid=(2 * n_q + 1, s // rows),
        in_specs=[pl.BlockSpec((rows, GROUP_WIDTH), lambda j, i: (first_pass(j, i), KV_COL)),
                  pl.BlockSpec((rows, GROUP_WIDTH), lambda j, i: (jnp.where(j == 0, 0, i), jnp.where(j <= n_q, q_of(j), j))),
                  pl.BlockSpec((rows, GROUP_WIDTH), lambda j, i: (jnp.where(jnp.logical_and(j >= 1, j <= n_q), i, 0), q_of(j))),
                  pl.BlockSpec((rows, GROUP_WIDTH), lambda j, i: (jnp.where(j > n_q, i, 0), gate_of(j))),
                  pl.BlockSpec((rows, GROUP_WIDTH), lambda j, i: (jnp.where(j > n_q, i, 0), gate_of(j))),
                  acc, acc, acc, acc, full((1, GROUP_WIDTH)), full((1, KV_WIDTH)), tab, tab,
                  full(ones_q.shape), full(ones_kv.shape)],
        out_specs=[pl.BlockSpec((rows, GROUP_WIDTH), lambda j, i: (i, out_col(j))),
                   pl.BlockSpec((1, 8, GROUP_WIDTH), lambda j, i: (q_of(j), 0, 0)), pl.BlockSpec((8, KV_WIDTH), lambda j, i: (0, 0))],
        out_shape=[jax.ShapeDtypeStruct((s, ODD_IN), BF16), jax.ShapeDtypeStruct((n_q, 8, GROUP_WIDTH), F32),
                   jax.ShapeDtypeStruct((8, KV_WIDTH), F32)],
        compiler_params=_params("arbitrary", "arbitrary"),
    )(proj, proj, dqr, o, dag, dkc, dkp, dvc, dvp, qw, kw, cos, sins, ones_q, ones_kv)


def _place():
    x, y, c = lax.axis_index("x"), lax.axis_index("y"), lax.axis_index("c")
    return x, y, c


OTHER_CHIPS = ((1, 0), (0, 1), (1, 1))


def _half_rows(ref, half, rows):
    return ref.at[pl.ds(pl.multiple_of(half * (rows // 2), 8), rows // 2)]


DMA_CHUNK_BYTES = 1 << 20
BF16_TILE_ROWS = 16


def _n_chunks(ref):
    rows = ref.shape[-2]
    nbytes = math.prod(ref.shape) * jnp.dtype(ref.dtype).itemsize
    n = 1
    while 2 * n * DMA_CHUNK_BYTES <= nbytes and rows % (2 * n * BF16_TILE_ROWS) == 0:
        n *= 2
    return n


def _row_chunk(ref, k, n):
    rows = ref.shape[-2] // n
    return ref.at[pl.ds(k * rows, rows)] if len(ref.shape) == 2 else ref.at[:, pl.ds(k * rows, rows)]


def _push(src, dst, send_sem, recv_sem, device_id):
    n = _n_chunks(src)
    for k in range(n):
        pltpu.make_async_remote_copy(src_ref=_row_chunk(src, k, n), dst_ref=_row_chunk(dst, k, n), send_sem=send_sem,
                                     recv_sem=recv_sem, device_id=device_id, device_id_type=MESH).start()
    return pltpu.make_async_remote_copy(src_ref=src, dst_ref=dst, send_sem=send_sem, recv_sem=recv_sem,
                                        device_id=device_id, device_id_type=MESH)


HBM = pl.BlockSpec(memory_space=pltpu.HBM)
SEM = pl.BlockSpec(memory_space=pltpu.SEMAPHORE)
SPLIT_COPY_EFFECT = pltpu.SideEffectType.DATAFLOW_SIDE_EFFECTING


def _in_hbm(a):
    return pltpu.with_memory_space_constraint(a, pltpu.HBM)


def _start_copies(name, arrays, plan, n_copies, after=None):
    n = len(arrays)

    def body(*refs):
        send_sem, recv_sem = refs[-n - 3], refs[-n - 2]
        for k, (src, dst, peer) in enumerate(plan(refs[:n])):
            _push(src, dst, send_sem.at[k], recv_sem.at[k], peer)
        refs[-1][...] = jnp.zeros_like(refs[-1])

    dma = pltpu.SemaphoreType.DMA((n_copies,))
    outs = pl.pallas_call(
        body, name=name,
        out_shape=(dma, dma, *[pltpu.HBM(a.shape, a.dtype) for a in arrays], jax.ShapeDtypeStruct((8, 128), F32)),
        in_specs=[HBM] * n + ([ANY] if after is not None else []),
        out_specs=(SEM, SEM, *[HBM] * n, pl.BlockSpec(memory_space=pltpu.VMEM)),
        input_output_aliases={i: i + 2 for i in range(n)},
        compiler_params=pltpu.CompilerParams(has_side_effects=SPLIT_COPY_EFFECT),
    )(*[_in_hbm(a) for a in arrays], *((after,) if after is not None else ()))
    return outs[0], outs[1], list(outs[2:2 + n]), outs[-1]


def _wait_copies(name, send_sem, recv_sem, arrays, plan, after):
    n = len(arrays)
    after = list(after) if isinstance(after, (list, tuple)) else [after]

    def body(*refs):
        send_ref, recv_ref = refs[n], refs[n + 1]
        for k, (src, dst, peer) in enumerate(plan(refs[:n])):
            cp = pltpu.make_async_remote_copy(src_ref=src, dst_ref=dst, send_sem=send_ref.at[k], recv_sem=recv_ref.at[k],
                                              device_id=peer, device_id_type=MESH)
            cp.wait_send()
            cp.wait_recv()

    return list(pl.pallas_call(
        body, name=name, out_shape=tuple(pltpu.HBM(a.shape, a.dtype) for a in arrays),
        in_specs=[HBM] * n + [SEM, SEM] + [ANY] * len(after), out_specs=tuple([HBM] * n),
        input_output_aliases={i: i for i in range(n)},
        compiler_params=pltpu.CompilerParams(has_side_effects=SPLIT_COPY_EFFECT),
    )(*arrays, send_sem, recv_sem, *after))


def _gather_region(full, kind, chip, half=None):
    if kind == "whole":
        return full.at[chip]
    if kind == "col" or isinstance(kind, tuple):
        part, parts = (0, 1) if kind == "col" else kind[1:]
        rows, width = full.shape[0], full.shape[1] // N_CHIPS
        piece = full.at[:, pl.ds(pl.multiple_of(chip * width + part * (width // parts), LANE_TILE), width // parts)]
    else:
        rows = full.shape[0] // N_CHIPS
        piece = full.at[pl.ds(pl.multiple_of(chip * rows, BF16_TILE_ROWS), rows)]
    return piece if half is None else _half_rows(piece, half, rows)


def _gather_plan(kinds):
    def plan(fulls):
        x, y, c = _place()
        copies = []
        for fx, fy in OTHER_CHIPS:
            for full, kind in zip(fulls, kinds):
                mine = _gather_region(full, kind, 2 * x + y, c)
                copies.append((mine, mine, (x ^ fx, y ^ fy, c)))
        return copies

    return plan


def _gather_start(name, fulls, kinds, after=None):
    return _start_copies(name, list(fulls), _gather_plan(kinds), 3 * len(kinds), after)


def _pass_on_plan(kinds):
    split = [i for i, kind in enumerate(kinds) if kind != "whole"]

    def plan(fulls):
        x, y, c = _place()
        copies = []
        for fx, fy in OTHER_CHIPS:
            chip = 2 * (x ^ fx) + (y ^ fy)
            for i in split:
                landed = _gather_region(fulls[i], kinds[i], chip, c)
                copies.append((landed, landed, (x, y, 1 - c)))
        return copies

    return plan, 3 * len(split)


def _gather_arrive(name, started, kinds, after, arrays=None):
    send_sem, recv_sem, fulls, _ = started
    fulls = _wait_copies(name + "_wait", send_sem, recv_sem, fulls if arrays is None else arrays, _gather_plan(kinds), after)
    plan, count = _pass_on_plan(kinds)
    return _start_copies(name + "_pass_on", fulls, plan, count)


def _gather_done(name, passed_on, kinds, after):
    send_sem, recv_sem, fulls, _ = passed_on
    return _wait_copies(name + "_pass_on_wait", send_sem, recv_sem, fulls, _pass_on_plan(kinds)[0], after)


def _allreduce_small(v):
    def body(v_ref, out_ref, buf, send_sems, recv_sems):
        x, y, c = _place()
        me = 4 * x + 2 * y + c
        buf[me] = v_ref[...]
        copies = []
        for r in range(1, N_DEV):
            peer = (x ^ (r >> 2), y ^ ((r >> 1) & 1), c ^ (r & 1))
            cp = pltpu.make_async_remote_copy(src_ref=v_ref, dst_ref=buf.at[me], send_sem=send_sems.at[r - 1],
                                              recv_sem=recv_sems.at[r - 1], device_id=peer, device_id_type=MESH)
            cp.start()
            copies.append(cp)
        for cp in copies:
            cp.wait_recv()
        for cp in copies:
            cp.wait_send()
        total = buf[0]
        for d in range(1, N_DEV):
            total = total + buf[d]
        out_ref[...] = total

    vm = pl.BlockSpec(memory_space=pltpu.VMEM)
    return pl.pallas_call(
        body, name="allreduce_small", in_specs=[vm], out_specs=vm, out_shape=jax.ShapeDtypeStruct(v.shape, v.dtype),
        scratch_shapes=[pltpu.VMEM((N_DEV,) + v.shape, v.dtype), pltpu.SemaphoreType.DMA((N_DEV - 1,)),
                        pltpu.SemaphoreType.DMA((N_DEV - 1,))],
        compiler_params=pltpu.CompilerParams(has_side_effects=True),
    )(v)


def _exchange_plan(n):
    def plan(refs):
        x, y, c = _place()
        copies = []
        for g, theirs in zip(refs[:n], refs[n:]):
            half = g.shape[1] // 2
            src = g.at[:, pl.ds(pl.multiple_of((1 - c) * half, BF16_TILE_ROWS), half)]
            copies.append((src, theirs, (x, y, 1 - c)))
        return copies

    return plan


def _pair_sum(g, theirs, core, name):
    pieces, half, cols = theirs.shape
    tr = min(half, 256)
    per = half // tr

    def body(core_ref, g_ref, t_ref, o_ref):
        del core_ref
        o_ref[...] = (g_ref[...].astype(F32) + t_ref[...].astype(F32)).astype(BF16)

    spec = pl.BlockSpec((1, tr, cols), lambda p, i, core_ref: (p, i, 0))
    return pl.pallas_call(
        body, name=name, out_shape=jax.ShapeDtypeStruct(theirs.shape, BF16),
        grid_spec=pltpu.PrefetchScalarGridSpec(
            num_scalar_prefetch=1, grid=(pieces, per),
            in_specs=[pl.BlockSpec((1, tr, cols), lambda p, i, core_ref: (p, core_ref[0] * per + i, 0)), spec],
            out_specs=spec),
        compiler_params=_params("parallel", "parallel"))(core, g, theirs)


def _scatter_plan(n):
    def plan(refs):
        parts, stacks = refs[:n], refs[n:]
        x, y, c = _place()
        copies = []
        for fx, fy in OTHER_CHIPS:
            chip = 2 * (x ^ fx) + (y ^ fy)
            for part, stack in zip(parts, stacks):
                if part.shape[0] == N_CHIPS:
                    piece = part.at[chip]
                else:
                    width = part.shape[2] // N_CHIPS
                    piece = part.at[0].at[:, pl.ds(pl.multiple_of(chip * width, 128), width)]
                copies.append((piece, stack.at[2 * x + y], (x ^ fx, y ^ fy, c)))
        return copies

    return plan


def _scatter_start(name, parts, after=None):
    def landing(a):
        return (N_CHIPS, a.shape[1], a.shape[2] if a.shape[0] == N_CHIPS else a.shape[2] // N_CHIPS)

    stacks = [lax.empty(landing(a), a.dtype) for a in parts]
    return _start_copies(name, list(parts) + stacks, _scatter_plan(len(parts)), 3 * len(parts), after)


def _scatter_finish(name, started, after):
    send_sem, recv_sem, arrays, _ = started
    n = len(arrays) // 2
    arrays = _wait_copies(name + "_wait", send_sem, recv_sem, arrays, _scatter_plan(n), after)
    return arrays[:n], arrays[n:]


def _sum_chips(part, stack, place, name):
    _, r, c = stack.shape
    tr = 256
    per = r // tr

    def body(place_ref, own_ref, a_ref, b_ref, c_ref, o_ref):
        del place_ref
        total = own_ref[0].astype(F32)
        for ref in (a_ref, b_ref, c_ref):
            total = total + ref[0].astype(F32)
        o_ref[...] = total

    if part.shape[0] == N_CHIPS:
        own = pl.BlockSpec((1, tr, c), lambda i, pr: (pr[1], i, 0))
    else:
        own = pl.BlockSpec((1, tr, c), lambda i, pr: (0, i, pr[1]))
    other = lambda flip: pl.BlockSpec((1, tr, c), lambda i, pr: (pr[1] ^ flip, i, 0))
    return pl.pallas_call(
        body, name=name, out_shape=jax.ShapeDtypeStruct((2 * r, c), F32),
        grid_spec=pltpu.PrefetchScalarGridSpec(
            num_scalar_prefetch=1, grid=(per,), in_specs=[own, other(2), other(1), other(3)],
            out_specs=pl.BlockSpec((tr, c), lambda i, pr: (pr[0] * per + i, 0))),
        compiler_params=_params("parallel"))(place, part, stack, stack, stack)


def _own_half_plan(fulls):
    x, y, c = _place()
    return [(mine, mine, (x, y, 1 - c)) for mine in (_half_rows(full, c, full.shape[0]) for full in fulls)]


MM = dict(tm=2048, tn=1024, tk=2048)
MM_LONG_K = dict(tm=1024, tn=1024, tk=4096)


def _local_step(x, target, ev_norm_w, q_norm_w, k_norm_w, sinks, own_first, weights_first, late_arrived, weights_late, grads_out):
    s = x.shape[0]
    cos_r, sin_r = _rope_tables(s, RET_DIM)
    cos_a, sin_a = _rope_tables(s, HEAD_DIM)
    cos_a = np.tile(cos_a, (1, 4))
    sins_a = np.tile(np.concatenate([-sin_a, sin_a], axis=1), (1, 2))
    tables = _retention_tables()
    ones_q, ones_kv = _block_diag_mean(min(GROUP_WIDTH, MXU_WIDTH)), _block_diag_mean(min(KV_WIDTH, MXU_WIDTH))
    qw_g = jnp.tile(q_norm_w, (1, GROUP_WIDTH // HEAD_DIM))
    kw_kv = jnp.tile(k_norm_w, (1, KV_WIDTH // HEAD_DIM))
    sinks1 = sinks.reshape(Q_HEADS)

    own_w_in0, own_block, start_token = own_first
    h0 = _rmsnorm(x, ev_norm_w, "norm0", after=start_token)
    shifted = dict(shift=own_block * (own_w_in0.shape[1] // MM["tn"]), total=EVEN_IN // MM["tn"], tm=MM["tm"], tn=MM["tn"],
                   out_dtype=BF16)
    own_blocks = own_w_in0.shape[1] // MM["tn"]
    proj0 = _mm_shifted(h0, own_w_in0, b_shifted=False, first=0, count=own_blocks, name="proj0_own", **shifted)
    others = shifted["total"] // own_blocks - 1
    for stage, (part, skip, count) in enumerate(((0, 0, others - 1), (0, others - 1, 1), (1, 0, others))):
        w_in0, small_weights, token = weights_first(stage, proj0)
        if small_weights is not None:
            conv_w, od_norm_w = small_weights
        proj0 = _mm_shifted(h0, w_in0, b_shifted=True, first=own_blocks * (1 + skip) + part, stride=own_blocks, count=count,
                            name=f"proj0_rest{stage}", into=proj0, after=token, **shifted)
    o_ret, cat, states = _retention_fwd(proj0, cos_r, sin_r, tables, after=late_arrived(proj0))
    cat = _conv_fwd(proj0, conv_w, cat)
    w_out0, w_in1, w_out1 = weights_late(cat)

    def residual_and_norm(prod, x_ref, w_ref, x1_ref, h1_ref):
        x1v = x_ref[...] + prod
        x1_ref[...] = x1v
        rstd = lax.rsqrt(jnp.mean(x1v * x1v, axis=-1, keepdims=True) + EPS)
        h1_ref[...] = (x1v * rstd * w_ref[...]).astype(BF16)

    def residual_and_loss(prod, x1_ref, t_ref, dyb_ref, sq_ref):
        diff = (x1_ref[...] + prod) - t_ref[...]
        dyb_ref[...] = (diff * (1.0 / D_MODEL)).astype(BF16)

        @pl.when(pl.program_id(0) == 0)
        def _():
            sq_ref[...] = jnp.zeros_like(sq_ref)

        sq_ref[...] += jnp.sum(jnp.sum(diff * diff, axis=1, keepdims=True), axis=0, keepdims=True)

    act = lambda dt: jax.ShapeDtypeStruct((s, D_MODEL), dt)
    x1, h1 = _mm_rows(cat, w_out0, [x], [od_norm_w], [act(F32), act(BF16)], residual_and_norm, tm=min(s, 512), name="out0")
    proj1 = _mm(h1, w_in1, mode="nn", out_dtype=BF16, name="proj1", tm=2048, tn=1536, tk=2048)
    qk = _qk_prep(proj1, qw_g, kw_kv, cos_a, sins_a, ones_q, ones_kv)
    ag, o_att = _swa_fwd(qk, proj1, sinks1)
    dy_b, sq = _mm_rows(ag, w_out1, [x1, target], [], [act(BF16), jax.ShapeDtypeStruct((8, 128), F32)],
                        residual_and_loss, tm=min(s, 512), name="out1")

    g_w_out1 = _mm(ag, dy_b, mode="tn", out_dtype=BF16, name="g_w_out1", **MM_LONG_K)
    dag = _mm(dy_b, w_out1, mode="nt", out_dtype=BF16, name="d_ag", **MM)
    dqr, dkc, dkp, dvc, dvp, dsink = _swa_bwd(qk, proj1, dag, sinks1)
    dproj1, dqw, dkw = _swa_bwd_finish(proj1, dqr, o_att, dag, dkc, dkp, dvc, dvp, qw_g, kw_kv, cos_a, sins_a, ones_q, ones_kv)
    g_w_in1 = _mm(h1, dproj1, mode="tn", out_dtype=BF16, name="g_w_in1", tm=1024, tn=768, tk=4096)
    dh1 = _mm(dproj1, w_in1, mode="nt", out_dtype=BF16, name="d_h1", tm=1024, tn=1024, tk=ODD_IN)
    dx1_b, g_norm1 = _rmsnorm_bwd(x1, od_norm_w, dh1, dy_b, "norm1_bwd", BF16)

    g_w_out0 = _mm(cat, dx1_b, mode="tn", out_dtype=BF16, name="g_w_out0", **MM_LONG_K)
    token = grads_out.begin("upper", (("od_w_in", g_w_in1, "col"), ("od_w_out", g_w_out1, "row"), ("ev_w_out", g_w_out0, "row")))
    dcat = _mm(dx1_b, w_out0, mode="nt", out_dtype=BF16, name="d_cat", after=token, **MM)
    token = grads_out.send("upper", dcat)
    dproj0 = _retention_bwd(proj0, o_ret, states, dcat, cos_r, sin_r, tables)
    dproj0, g_conv = _conv_bwd(proj0, dcat, conv_w, dproj0)
    g_w_in0 = _mm(h0, dproj0, mode="tn", out_dtype=BF16, name="g_w_in0", after=token, **MM_LONG_K)
    token = grads_out.begin("in0", (("ev_w_in", g_w_in0, "col"),))
    half_blocks = s // (2 * MM_LONG_K["tm"])
    dh0 = _mm(dproj0, w_in0, mode="nt", out_dtype=BF16, name="d_h0_top", after=token, m_blocks=(0, half_blocks), **MM_LONG_K)
    dh0 = _mm(dproj0, w_in0, mode="nt", out_dtype=BF16, name="d_h0_bottom", after=grads_out.send("in0", dh0),
              m_blocks=(half_blocks, half_blocks), into=dh0, **MM_LONG_K)
    grad_x, g_norm0 = _rmsnorm_bwd(x, ev_norm_w, dh0, dx1_b, "norm0_bwd", F32)

    g_qw = dqw[:, 0, :].reshape(Q_HEADS, HEAD_DIM).sum(axis=0)
    g_kw = dkw[0].reshape(KV_HEADS, HEAD_DIM).sum(axis=0)
    g_sinks = dsink[:, :, 0].reshape(Q_HEADS)
    small = dict(ev_norm=g_norm0[0], od_norm=g_norm1[0], conv=g_conv[:3], qw=g_qw, kw=g_kw, sinks=g_sinks)
    return sq[0, 0], grad_x, small


LANE_TILE = 128


def _pack_small_grads(small, sq):
    pad = lambda v, n: jnp.pad(v, (0, n - v.shape[0]))
    tail = jnp.concatenate([pad(small["qw"], LANE_TILE), pad(small["kw"], LANE_TILE), small["sinks"]])
    rows = [small["ev_norm"], small["od_norm"]] + [small["conv"][t] for t in range(3)] + [tail, sq.reshape(1)]
    rows += [jnp.zeros((1,), F32)] * (8 - len(rows))
    return jnp.stack([pad(r, D_MODEL) for r in rows])


def _adamw_small(tot, chip, params):
    n = len(params)

    def body(chip_ref, tot_ref, *refs):
        ins, outs = refs[:3 * n], refs[3 * n:]
        c = chip_ref[0]

        def own(rows, width):
            blocks = [tot_ref[rows, k * width:(k + 1) * width] for k in range(N_CHIPS)]
            g = blocks[-1]
            for k in reversed(range(N_CHIPS - 1)):
                g = jnp.where(c == k, blocks[k], g)
            return g

        grads = [tot_ref[0:1, :], own(slice(2, 5), CONV_WIDTH // N_CHIPS), own(slice(1, 2), D_MODEL // N_CHIPS),
                 tot_ref[5:6, 0:HEAD_DIM], tot_ref[5:6, LANE_TILE:LANE_TILE + HEAD_DIM],
                 tot_ref[5:6, 2 * LANE_TILE:2 * LANE_TILE + Q_HEADS]]
        for i, g in enumerate(grads):
            w_ref, m_ref, v_ref = ins[3 * i:3 * i + 3]
            g_out, d_out, nm_out, nv_out = outs[4 * i:4 * i + 4]
            at = (0,) if len(w_ref.shape) == 3 else ()
            delta, nm, nv = _adamw_math(w_ref[at] if at else w_ref[...], g, m_ref[at] if at else m_ref[...],
                                        v_ref[at] if at else v_ref[...])
            for ref, val in ((g_out, g), (d_out, delta), (nm_out, nm), (nv_out, nv)):
                if at:
                    ref[0] = val
                else:
                    ref[...] = val

    vm = pl.BlockSpec(memory_space=pltpu.VMEM)
    flat = [a for p in params for a in p]
    outs = pl.pallas_call(
        body, name="adamw_small", in_specs=[pl.BlockSpec(memory_space=pltpu.SMEM), vm] + [vm] * len(flat),
        out_specs=[vm] * (4 * n), out_shape=[jax.ShapeDtypeStruct(p[0].shape, F32) for p in params for _ in range(4)],
    )(chip, tot, *flat)
    return [tuple(outs[4 * i:4 * i + 4]) for i in range(n)]


class _ReduceScatter:
    def __init__(self, place):
        self.place = place
        self.exchanging = {}
        self.started = []
        self.sharing = {}

    def begin(self, tag, grads):
        pieces = [g[None] if kind == "col" else g.reshape(N_CHIPS, g.shape[0] // N_CHIPS, g.shape[1]) for _, g, kind in grads]
        landing = [lax.empty((p.shape[0], p.shape[1] // 2, p.shape[2]), p.dtype) for p in pieces]
        started = _start_copies("exchange_" + tag, pieces + landing, _exchange_plan(len(pieces)), len(pieces))
        self.exchanging[tag] = ([nm for nm, _, _ in grads], started)
        return started[3]

    def send(self, tag, after):
        group, (send_sem, recv_sem, arrays, _) = self.exchanging[tag]
        n = len(group)
        arrays = _wait_copies("exchange_" + tag + "_wait", send_sem, recv_sem, arrays, _exchange_plan(n), after)
        parts = [_pair_sum(g, t, self.place[:1], "pair_sum_" + nm) for g, t, nm in zip(arrays[:n], arrays[n:], group)]
        started = _scatter_start("scatter_" + tag, parts)
        token = self.sum_up(started[3])
        self.started.append((tag, group, started))
        return started[3] if token is None else token

    def sum_up(self, after):
        token = None
        for tag, group, started in self.started:
            if tag not in self.sharing:
                parts, stacks = _scatter_finish("scatter_" + tag, started, after)
                sums = [_sum_chips(p, s, self.place, "chip_sum_" + nm) for p, s, nm in zip(parts, stacks, group)]
                self.sharing[tag] = (group, _start_copies("share_" + tag, sums, _own_half_plan, len(sums)))
                token = self.sharing[tag][1][3]
        return token

    def result(self, tag, after):
        group, (send_sem, recv_sem, sums, _) = self.sharing[tag]
        return dict(zip(group, _wait_copies("share_" + tag + "_wait", send_sem, recv_sem, sums, _own_half_plan, after)))


def kernel(x, ev_norm_w, ev_w_in, ev_conv_w, ev_w_out, od_norm_w, od_w_in, od_q_norm_w, od_k_norm_w, od_sinks, od_w_out, loss_target, m_ev_norm_w, m_ev_w_in, m_ev_conv_w, m_ev_w_out, m_od_norm_w, m_od_w_in, m_od_q_norm_w, m_od_k_norm_w, m_od_sinks, m_od_w_out, v_ev_norm_w, v_ev_w_in, v_ev_conv_w, v_ev_w_out, v_od_norm_w, v_od_w_in, v_od_q_norm_w, v_od_k_norm_w, v_od_sinks, v_od_w_out):
    my_chip = 2 * lax.axis_index("x") + lax.axis_index("y")
    place = jnp.stack([lax.axis_index("c"), my_chip]).astype(jnp.int32)
    shard_w = D_MODEL // N_CHIPS
    conv_shard = CONV_WIDTH // N_CHIPS

    small_in = jnp.zeros((8, shard_w), F32)
    small_in = small_in.at[0].set(od_norm_w[0]).at[1:4, :conv_shard].set(ev_conv_w[0])
    small_in = lax.dynamic_update_slice(jnp.zeros((N_CHIPS, 8, shard_w), F32), small_in[None], (my_chip, 0, 0))
    chip = place[1:]
    first_kinds, second_kinds, late_kinds = (("col", 0, 2), "whole"), (("col", 1, 2),), ("row", "col", "row")
    w_in0_own_place, w_in0_shard = _cast_into_gathered(ev_w_in[0], "col", chip, "cast_w_in0", keep_shard=True)
    first = _gather_start("gather_first", [w_in0_own_place, small_in], first_kinds)
    second = _gather_start("gather_second", [first[2][0]], second_kinds)
    late_own = [_cast_into_gathered(ev_w_out[0], "row", chip, "cast_w_out0"),
                _cast_into_gathered(od_w_in[0], "col", chip, "cast_w_in1"),
                _cast_into_gathered(od_w_out[0], "row", chip, "cast_w_out1")]
    late, w_in0_so_far, second_on = [], [], []

    def weights_first(stage, after):
        if stage == 0:
            passed_on = _gather_arrive("gather_first", first, first_kinds, after, arrays=[second[2][0], first[2][1]])
            w_in0, small_all = _gather_done("gather_first", passed_on, first_kinds, [passed_on[3]] + late_own)
            w_in0_so_far.append(w_in0)
            late.append(_gather_start("gather_late", late_own, late_kinds, after=small_all))
            od_norm_full = small_all[:, 0, :].reshape(1, D_MODEL)
            conv_full = jnp.transpose(small_all[:, 1:4, :conv_shard], (1, 0, 2)).reshape(3, CONV_WIDTH)
            return w_in0, (conv_full, od_norm_full), late[0][3]
        if stage == 1:
            second_on.append(_gather_arrive("gather_second", second, second_kinds, after, arrays=w_in0_so_far))
            return second_on[0][2][0], None, second_on[0][3]
        (w_in0,) = _gather_done("gather_second", second_on[0], second_kinds, after)
        return w_in0, None, None

    def late_arrived(after):
        late.append(_gather_arrive("gather_late", late[0], late_kinds, after))
        return late[1][3]

    def weights_late(after):
        return _gather_done("gather_late", late[1], late_kinds, after)

    reduce_scatter = _ReduceScatter(place)
    sq, grad_x, small = _local_step(x[0], loss_target[0], ev_norm_w, od_q_norm_w, od_k_norm_w, od_sinks,
                                    (w_in0_shard, chip, second[3]), weights_first, late_arrived, weights_late,
                                    reduce_scatter)

    token = reduce_scatter.sum_up(grad_x)
    upd = {}
    upper = reduce_scatter.result("upper", token)
    upd["od_w_in"] = _adamw(od_w_in[0], upper["od_w_in"], m_od_w_in[0], v_od_w_in[0], "adamw_od_w_in")
    upd["od_w_out"] = _adamw(od_w_out[0], upper["od_w_out"], m_od_w_out[0], v_od_w_out[0], "adamw_od_w_out")
    upd["ev_w_out"] = _adamw(ev_w_out[0], upper["ev_w_out"], m_ev_w_out[0], v_ev_w_out[0], "adamw_ev_w_out")
    g_ev_w_in = reduce_scatter.result("in0", [upd[nm][1] for nm in ("od_w_in", "od_w_out", "ev_w_out")])["ev_w_in"]
    upd["ev_w_in"] = _adamw(ev_w_in[0], g_ev_w_in, m_ev_w_in[0], v_ev_w_in[0], "adamw_ev_w_in")
    tot = _allreduce_small(_pack_small_grads(small, sq))
    loss = 0.5 * tot[6, 0] / D_MODEL

    smalls = (("ev_norm_w", ev_norm_w, m_ev_norm_w, v_ev_norm_w), ("ev_conv_w", ev_conv_w, m_ev_conv_w, v_ev_conv_w),
              ("od_norm_w", od_norm_w, m_od_norm_w, v_od_norm_w), ("od_q_norm_w", od_q_norm_w, m_od_q_norm_w, v_od_q_norm_w),
              ("od_k_norm_w", od_k_norm_w, m_od_k_norm_w, v_od_k_norm_w), ("od_sinks", od_sinks, m_od_sinks, v_od_sinks))
    for (nm, _, _, _), result in zip(smalls, _adamw_small(tot, chip, [p[1:] for p in smalls])):
        upd[nm] = result
    for nm in ("ev_w_in", "ev_w_out", "od_w_in", "od_w_out"):
        upd[nm] = tuple(u[None] for u in upd[nm])
    order = ("ev_norm_w", "ev_w_in", "ev_conv_w", "ev_w_out", "od_norm_w", "od_w_in", "od_q_norm_w", "od_k_norm_w", "od_sinks", "od_w_out")
    return (loss, grad_x[None], *[upd[nm][0] for nm in order], *[upd[nm][1] for nm in order],
            *[upd[nm][2] for nm in order], *[upd[nm][3] for nm in order])
```

```python
import math

import jax
import jax.numpy as jnp
import numpy as np
from jax import lax
from jax.experimental import pallas as pl
from jax.experimental.pallas import tpu as pltpu

F32 = jnp.float32
BF16 = jnp.bfloat16

D_MODEL = 2048
RET_HEADS = 4
RET_DIM = 256
RET_WIDTH = 1024
CONV_WIDTH = 1024
EVEN_IN = 8192
Q_HEADS = 32
HEAD_DIM = 64
KV_HEADS = 4
KV_WIDTH = 256
ATTN_WIDTH = 2048
ODD_IN = 4608
BLK = 128
ROPE_THETA = 10000.0
EPS = 1e-6
ADAM_LR = 0.001
ADAM_B1 = 0.9
ADAM_B2 = 0.999
ADAM_EPS = 1e-08
ADAM_WD = 0.01
ADAM_STEP = 10
N_CHIPS = 4
N_DEV = 8
VMEM_LIMIT_BYTES = 56 * 1024 * 1024
MESH = pl.DeviceIdType.MESH
ANY = pl.BlockSpec(memory_space=pl.ANY)


def _params(*sem):
    return pltpu.CompilerParams(dimension_semantics=sem, vmem_limit_bytes=VMEM_LIMIT_BYTES)


def _dot(a, b):
    return jnp.dot(a, b, preferred_element_type=F32)


def _dot_nt(a, b):
    return lax.dot_general(a, b, (((1,), (1,)), ((), ())), preferred_element_type=F32)


def _dot_tn(a, b):
    return lax.dot_general(a, b, (((0,), (0,)), ((), ())), preferred_element_type=F32)


def _sigmoid(x):
    return 1.0 / (1.0 + jnp.exp(-x))


def _mm(a, b, *, mode, tm, tn, tk, out_dtype, name, add=None, after=None, m_blocks=None, into=None):
    if mode == "nn":
        (m, k), n = a.shape, b.shape[1]
    elif mode == "nt":
        (m, k), n = a.shape, b.shape[0]
    else:
        (k, m), n = a.shape, b.shape[1]
    tm, tn, tk = min(tm, m), min(tn, n), min(tk, k)
    assert m % tm == 0 and n % tn == 0 and k % tk == 0, (name, m, n, k)
    nk = k // tk
    first_m, count_m = (0, m // tm) if m_blocks is None else m_blocks
    dot = {"nn": _dot, "nt": _dot_nt, "tn": _dot_tn}[mode]
    a_spec = (pl.BlockSpec((tk, tm), lambda i, j, kk: (kk, i + first_m)) if mode == "tn"
              else pl.BlockSpec((tm, tk), lambda i, j, kk: (i + first_m, kk)))
    b_spec = (pl.BlockSpec((tn, tk), lambda i, j, kk: (j, kk)) if mode == "nt"
              else pl.BlockSpec((tk, tn), lambda i, j, kk: (kk, j)))
    o_spec = pl.BlockSpec((tm, tn), lambda i, j, kk: (i + first_m, j))
    has_add = add is not None

    def body(*refs):
        a_ref, b_ref = refs[0], refs[1]
        add_ref = refs[2] if has_add else None
        o_ref, acc_ref = refs[-2], refs[-1]
        p = dot(a_ref[...], b_ref[...])

        def finish(total):
            if has_add:
                total = total + add_ref[...].astype(F32)
            o_ref[...] = total.astype(out_dtype)

        if nk == 1:
            finish(p)
        else:
            kk = pl.program_id(2)

            @pl.when(kk == 0)
            def _():
                acc_ref[...] = p

            @pl.when(jnp.logical_and(kk > 0, kk < nk - 1))
            def _():
                acc_ref[...] += p

            @pl.when(kk == nk - 1)
            def _():
                finish(acc_ref[...] + p)

    extra = [arr for arr in (after, into) if arr is not None]
    in_specs = [a_spec, b_spec] + ([o_spec] if has_add else []) + [ANY] * len(extra)
    args = (a, b) + ((add,) if has_add else ()) + tuple(extra)
    return pl.pallas_call(
        body, name=name, grid=(count_m, n // tn, nk), in_specs=in_specs, out_specs=o_spec,
        out_shape=jax.ShapeDtypeStruct((m, n), out_dtype),
        input_output_aliases={len(args) - 1: 0} if into is not None else {},
        scratch_shapes=[pltpu.VMEM((tm, tn) if nk > 1 else (8, 128), F32)],
        compiler_params=_params("parallel", "parallel", "arbitrary"),
    )(*args)


def _mm_shifted(a, b, shift, *, b_shifted, first, count, total, tm, tn, out_dtype, name, stride=1, into=None, after=None):
    m, k = a.shape
    tm = min(tm, m)
    assert m % tm == 0
    col = lambda j, shift_ref: (shift_ref[0] + first + stride * j) % total
    extra = [arr for arr in (into, after) if arr is not None]

    def body(shift_ref, a_ref, b_ref, *rest):
        del shift_ref
        rest[-1][...] = _dot(a_ref[...], b_ref[...]).astype(out_dtype)

    return pl.pallas_call(
        body, name=name, out_shape=jax.ShapeDtypeStruct((m, total * tn), out_dtype),
        grid_spec=pltpu.PrefetchScalarGridSpec(
            num_scalar_prefetch=1, grid=(m // tm, count),
            in_specs=[pl.BlockSpec((tm, k), lambda i, j, s: (i, 0)),
                      pl.BlockSpec((k, tn), (lambda i, j, s: (0, col(j, s))) if b_shifted else (lambda i, j, s: (0, j)))]
            + [ANY] * len(extra),
            out_specs=pl.BlockSpec((tm, tn), lambda i, j, s: (i, col(j, s)))),
        input_output_aliases={3: 0} if into is not None else {},
        compiler_params=_params("parallel", "arbitrary"))(shift, a, b, *extra)


def _mm_rows(a, b, rows_in, vecs_in, out_shapes, epilogue, *, tm, name):
    m, k = a.shape
    n = b.shape[1]
    assert m % tm == 0
    row = pl.BlockSpec((tm, n), lambda i: (i, 0))

    def body(a_ref, b_ref, *rest):
        epilogue(_dot(a_ref[...], b_ref[...]), *rest)

    out_specs = [row if tuple(s.shape) == (m, n) else pl.BlockSpec(s.shape, lambda i: (0, 0)) for s in out_shapes]
    return pl.pallas_call(
        body, name=name, grid=(m // tm,),
        in_specs=[pl.BlockSpec((tm, k), lambda i: (i, 0)), pl.BlockSpec((k, n), lambda i: (0, 0))] + [row] * len(rows_in)
        + [pl.BlockSpec((1, n), lambda i: (0, 0))] * len(vecs_in),
        out_specs=out_specs, out_shape=out_shapes, compiler_params=_params("arbitrary"),
    )(a, b, *rows_in, *vecs_in)


def _cast_into_gathered(w, kind, chip, name, keep_shard=False):
    r, c = w.shape
    tr = min(r, 512)
    per = r // tr

    def body(chip_ref, w_ref, *outs):
        del chip_ref
        for o_ref in outs:
            o_ref[...] = w_ref[...].astype(BF16)

    if kind == "col":
        shape, out_map = (r, N_CHIPS * c), (lambda i, chip_ref: (i, chip_ref[0]))
    else:
        shape, out_map = (N_CHIPS * r, c), (lambda i, chip_ref: (chip_ref[0] * per + i, 0))
    plain = pl.BlockSpec((tr, c), lambda i, chip_ref: (i, 0))
    out = pl.pallas_call(
        body, name=name,
        out_shape=[jax.ShapeDtypeStruct(shape, BF16)] + ([jax.ShapeDtypeStruct((r, c), BF16)] if keep_shard else []),
        grid_spec=pltpu.PrefetchScalarGridSpec(
            num_scalar_prefetch=1, grid=(per,), in_specs=[plain],
            out_specs=[pl.BlockSpec((tr, c), out_map)] + ([plain] if keep_shard else [])),
        compiler_params=_params("parallel"))(chip, w)
    return out if keep_shard else out[0]


NORM_ROWS = 512


def _rmsnorm(x, w, name, after=None):
    s, d = x.shape
    tr = NORM_ROWS

    def body(x_ref, w_ref, *rest):
        xv = x_ref[...]
        rstd = lax.rsqrt(jnp.mean(xv * xv, axis=-1, keepdims=True) + EPS)
        rest[-1][...] = (xv * rstd * w_ref[...]).astype(BF16)

    return pl.pallas_call(
        body, name=name, grid=(s // tr,),
        in_specs=[pl.BlockSpec((tr, d), lambda i: (i, 0)), pl.BlockSpec((1, d), lambda i: (0, 0))]
        + ([ANY] if after is not None else []),
        out_specs=pl.BlockSpec((tr, d), lambda i: (i, 0)),
        out_shape=jax.ShapeDtypeStruct((s, d), BF16), compiler_params=_params("parallel"),
    )(x, w, *((after,) if after is not None else ()))


def _rmsnorm_bwd(x, w, dh, dres, name, out_dtype):
    s, d = x.shape
    tr = NORM_ROWS

    def body(x_ref, w_ref, dh_ref, dres_ref, dx_ref, dw_ref):
        xv = x_ref[...]
        rstd = lax.rsqrt(jnp.mean(xv * xv, axis=-1, keepdims=True) + EPS)
        nrm = xv * rstd
        dhv = dh_ref[...].astype(F32)
        dn = dhv * w_ref[...]
        dx = dres_ref[...].astype(F32) + rstd * (dn - nrm * jnp.mean(dn * nrm, axis=-1, keepdims=True))
        dx_ref[...] = dx.astype(out_dtype)

        @pl.when(pl.program_id(0) == 0)
        def _():
            dw_ref[...] = jnp.zeros_like(dw_ref)

        dw_ref[...] += jnp.sum(dhv * nrm, axis=0, keepdims=True)

    row = pl.BlockSpec((tr, d), lambda i: (i, 0))
    return pl.pallas_call(
        body, name=name, grid=(s // tr,),
        in_specs=[row, pl.BlockSpec((1, d), lambda i: (0, 0)), row, row],
        out_specs=[row, pl.BlockSpec((8, d), lambda i: (0, 0))],
        out_shape=[jax.ShapeDtypeStruct((s, d), out_dtype), jax.ShapeDtypeStruct((8, d), F32)],
        compiler_params=_params("arbitrary"))(x, w, dh, dres)


def _adamw_math(w, g, m, v):
    nm = ADAM_B1 * m + (1.0 - ADAM_B1) * g
    nv = ADAM_B2 * v + (1.0 - ADAM_B2) * (g * g)
    m_hat = nm / (1.0 - ADAM_B1 ** ADAM_STEP)
    v_hat = nv / (1.0 - ADAM_B2 ** ADAM_STEP)
    return -ADAM_LR * (m_hat / (jnp.sqrt(v_hat) + ADAM_EPS) + ADAM_WD * w), nm, nv


ADAMW_ROWS = 128
ADAMW_READ_SLOTS = 4
ADAMW_WRITE_SLOTS = 2


def _adamw(w, g, m, v, name):
    r, c = w.shape
    tr = min(r, ADAMW_ROWS)
    assert r % tr == 0
    steps = r // tr
    n_in, n_out = ADAMW_READ_SLOTS, ADAMW_WRITE_SLOTS

    def body(w_ref, g_ref, m_ref, v_ref, g_out, d_ref, nm_ref, nv_ref, ins, outs, in_sem, out_sem):
        def reads(s):
            return [pltpu.make_async_copy(src.at[pl.ds(s * tr, tr)], ins.at[s % n_in, k], in_sem.at[s % n_in, k])
                    for k, src in enumerate((w_ref, g_ref, m_ref, v_ref))]

        def writes(s):
            return [pltpu.make_async_copy(outs.at[s % n_out, k], dst.at[pl.ds(s * tr, tr)], out_sem.at[s % n_out, k])
                    for k, dst in enumerate((g_out, d_ref, nm_ref, nv_ref))]

        for s in range(min(n_in - 1, steps)):
            for cp in reads(s):
                cp.start()
        for s in range(steps):
            if s + n_in - 1 < steps:
                for cp in reads(s + n_in - 1):
                    cp.start()
            for cp in reads(s):
                cp.wait()
            if s >= n_out:
                for cp in writes(s - n_out):
                    cp.wait()
            i, o = s % n_in, s % n_out
            gv = ins[i, 1]
            outs[o, 0] = gv
            outs[o, 1], outs[o, 2], outs[o, 3] = _adamw_math(ins[i, 0], gv, ins[i, 2], ins[i, 3])
            for cp in writes(s):
                cp.start()
        for s in range(max(steps - n_out, 0), steps):
            for cp in writes(s):
                cp.wait()

    shp = jax.ShapeDtypeStruct((r, c), F32)
    return pl.pallas_call(
        body, name=name, in_specs=[ANY] * 4, out_specs=[ANY] * 4, out_shape=[shp] * 4,
        scratch_shapes=[pltpu.VMEM((n_in, 4, tr, c), F32), pltpu.VMEM((n_out, 4, tr, c), F32),
                        pltpu.SemaphoreType.DMA((n_in, 4)), pltpu.SemaphoreType.DMA((n_out, 4))],
        compiler_params=_params())(w, g, m, v)


def _rope_tables(s, dim):
    inv = (1.0 / (ROPE_THETA ** (np.arange(0, dim, 2, dtype=np.float64) / dim))).astype(np.float32)
    ang = (np.arange(s, dtype=np.float32)[:, None] * inv[None, :]).astype(np.float64)
    return np.cos(ang).astype(np.float32), np.sin(ang).astype(np.float32)


def _rope_half(x, cos, sin):
    h = x.shape[1] // 2
    x1, x2 = x[:, :h], x[:, h:]
    return jnp.concatenate([x1 * cos - x2 * sin, x2 * cos + x1 * sin], axis=1)


def _unrope_half(dy, cos, sin):
    h = dy.shape[1] // 2
    d1, d2 = dy[:, :h], dy[:, h:]
    return jnp.concatenate([d1 * cos + d2 * sin, d2 * cos - d1 * sin], axis=1)


def _lane(shape):
    return lax.broadcasted_iota(jnp.int32, shape, 1)


def _partner64(x):
    w = x.shape[1]
    first = (_lane(x.shape) % HEAD_DIM) < (HEAD_DIM // 2)
    return jnp.where(first, pltpu.roll(x, w - HEAD_DIM // 2, 1), pltpu.roll(x, HEAD_DIM // 2, 1))


def _tile_lanes(t, reps):
    return t if reps == 1 else jnp.concatenate([t] * reps, axis=1)


MXU_WIDTH = 256


def _group_mean(x, ones_bd, passes=2):
    width, tile = x.shape[1], ones_bd.shape[0]
    if width > tile:
        return jnp.concatenate([_group_mean(x[:, c:c + tile], ones_bd, passes) for c in range(0, width, tile)], axis=1)
    hi = x.astype(BF16)
    if passes == 1:
        return _dot(hi, ones_bd)
    lo = (x - hi.astype(F32)).astype(BF16)
    return _dot(hi, ones_bd) + _dot(lo, ones_bd)


def _block_diag_mean(width):
    idx = jnp.arange(width) // HEAD_DIM
    return jnp.where(idx[:, None] == idx[None, :], 1.0 / HEAD_DIM, 0.0).astype(BF16)


RET_STEP = 4


def _retention_tables():
    h = RET_HEADS
    log_g = jnp.log(1.0 - 2.0 ** (-5.0 - jnp.arange(h, dtype=F32)))
    idx = jnp.arange(BLK, dtype=F32)
    diff = idx[:, None] - idx[None, :]
    intra = jnp.where(diff >= 0, jnp.exp(log_g[:, None, None] * jnp.maximum(diff, 0.0)), 0.0).astype(F32)
    q_dec = jnp.exp(log_g[:, None] * (idx[None, :] + 1.0)).astype(F32)[:, :, None]
    k_dec = jnp.exp(log_g[:, None] * (BLK - 1.0 - idx[None, :])).astype(F32)[:, :, None]
    chunk_dec = jnp.exp(log_g * BLK).astype(F32)[:, None, None]
    return intra, q_dec, k_dec, chunk_dec


def _retention_fwd(proj, cos, sin, tables, after=None):
    s = proj.shape[0]
    nc = s // BLK
    intra, q_dec, k_dec, chunk_dec = tables

    def body(p_ref, cos_ref, sin_ref, in_ref, qd_ref, kd_ref, cd_ref, *rest):
        o_ref, cat_ref, st_ref, state = rest[-4:]

        @pl.when(pl.program_id(0) == 0)
        def _():
            state[...] = jnp.zeros_like(state)

        for c in range(RET_STEP):
            rows = slice(c * BLK, (c + 1) * BLK)
            cosv, sinv = cos_ref[rows, :], sin_ref[rows, :]
            for h in range(RET_HEADS):
                c0 = h * RET_DIM
                q = p_ref[rows, c0:c0 + RET_DIM].astype(F32)
                k = p_ref[rows, RET_WIDTH + c0:RET_WIDTH + c0 + RET_DIM].astype(F32)
                v = p_ref[rows, 2 * RET_WIDTH + c0:2 * RET_WIDTH + c0 + RET_DIM]
                g = p_ref[rows, 3 * RET_WIDTH + c0:3 * RET_WIDTH + c0 + RET_DIM].astype(F32)
                qb = _rope_half(q, cosv, sinv).astype(BF16)
                kr = _rope_half(k, cosv, sinv) * (RET_DIM ** -0.5)
                kb = kr.astype(BF16)
                scores = _dot_nt(qb, kb) * in_ref[h]
                inner = _dot(scores.astype(BF16), v)
                prev = state[h]
                prev_b = prev.astype(BF16)
                st_ref[h, c] = prev_b
                o = inner + _dot(qb, prev_b) * qd_ref[h]
                o_ref[rows, c0:c0 + RET_DIM] = o
                rstd = lax.rsqrt(jnp.mean(o * o, axis=-1, keepdims=True) + EPS)
                cat_ref[rows, c0:c0 + RET_DIM] = (o * rstd * (g * _sigmoid(g))).astype(BF16)
                state[h] = cd_ref[h] * prev + _dot_tn((kr * kd_ref[h]).astype(BF16), v)

    full = lambda shape: pl.BlockSpec(shape, lambda n: (0,) * len(shape))
    step = RET_STEP * BLK
    return pl.pallas_call(
        body, name="retention_fwd", grid=(nc // RET_STEP,),
        in_specs=[pl.BlockSpec((step, 4 * RET_WIDTH), lambda n: (n, 0)),
                  pl.BlockSpec((step, RET_DIM // 2), lambda n: (n, 0)), pl.BlockSpec((step, RET_DIM // 2), lambda n: (n, 0)),
                  full((RET_HEADS, BLK, BLK)), full((RET_HEADS, BLK, 1)), full((RET_HEADS, BLK, 1)), full((RET_HEADS, 1, 1))]
        + ([ANY] if after is not None else []),
        out_specs=[pl.BlockSpec((step, RET_WIDTH), lambda n: (n, 0)), pl.BlockSpec((step, RET_WIDTH), lambda n: (n, 0)),
                   pl.BlockSpec((RET_HEADS, RET_STEP, RET_DIM, RET_DIM), lambda n: (0, n, 0, 0))],
        out_shape=[jax.ShapeDtypeStruct((s, RET_WIDTH), F32), jax.ShapeDtypeStruct((s, D_MODEL), BF16),
                   jax.ShapeDtypeStruct((RET_HEADS, nc, RET_DIM, RET_DIM), BF16)],
        scratch_shapes=[pltpu.VMEM((RET_HEADS, RET_DIM, RET_DIM), F32)],
        compiler_params=_params("arbitrary"),
    )(proj, cos, sin, intra, q_dec, k_dec, chunk_dec, *((after,) if after is not None else ()))


def _retention_bwd(proj, o, states, dcat, cos, sin, tables):
    s = proj.shape[0]
    nc = s // BLK
    intra, q_dec, k_dec, chunk_dec = tables

    def body(p_ref, o_ref, st_ref, dc_ref, cos_ref, sin_ref, in_ref, qd_ref, kd_ref, cd_ref, dp_ref, dstate):
        @pl.when(pl.program_id(0) == 0)
        def _():
            dstate[...] = jnp.zeros_like(dstate)

        for c in reversed(range(RET_STEP)):
            rows = slice(c * BLK, (c + 1) * BLK)
            cosv, sinv = cos_ref[rows, :], sin_ref[rows, :]
            for h in range(RET_HEADS):
                c0 = h * RET_DIM
                q = p_ref[rows, c0:c0 + RET_DIM].astype(F32)
                k = p_ref[rows, RET_WIDTH + c0:RET_WIDTH + c0 + RET_DIM].astype(F32)
                v = p_ref[rows, 2 * RET_WIDTH + c0:2 * RET_WIDTH + c0 + RET_DIM]
                g = p_ref[rows, 3 * RET_WIDTH + c0:3 * RET_WIDTH + c0 + RET_DIM].astype(F32)
                o = o_ref[rows, c0:c0 + RET_DIM]
                dc = dc_ref[rows, c0:c0 + RET_DIM].astype(F32)
                rstd = lax.rsqrt(jnp.mean(o * o, axis=-1, keepdims=True) + EPS)
                nrm = o * rstd
                sg = _sigmoid(g)
                dg = dc * nrm * (sg * (1.0 + g * (1.0 - sg)))
                dn = dc * (g * sg)
                do = rstd * (dn - nrm * jnp.mean(dn * nrm, axis=-1, keepdims=True))
                qb = _rope_half(q, cosv, sinv).astype(BF16)
                kr = _rope_half(k, cosv, sinv) * (RET_DIM ** -0.5)
                kb = kr.astype(BF16)
                mask = in_ref[h]
                qd, kd = qd_ref[h], kd_ref[h]
                prev_b = st_ref[h, c]
                dnext = dstate[h]
                dnext_b = dnext.astype(BF16)
                att = (_dot_nt(qb, kb) * mask).astype(BF16)
                do_b = do.astype(BF16)
                doq = (do * qd).astype(BF16)
                dv = _dot_tn(att, do_b) + _dot((kr * kd).astype(BF16), dnext_b)
                ds = (_dot_nt(do_b, v) * mask).astype(BF16)
                dqr = _dot(ds, kb) + _dot_nt(doq, prev_b)
                dkr = _dot_tn(ds, qb) + _dot_nt(v, dnext_b) * kd
                dstate[h] = cd_ref[h] * dnext + _dot_tn(qb, doq)
                dq = _unrope_half(dqr, cosv, sinv)
                dk = _unrope_half(dkr * (RET_DIM ** -0.5), cosv, sinv)
                dp_ref[rows, c0:c0 + RET_DIM] = dq.astype(BF16)
                dp_ref[rows, RET_WIDTH + c0:RET_WIDTH + c0 + RET_DIM] = dk.astype(BF16)
                dp_ref[rows, 2 * RET_WIDTH + c0:2 * RET_WIDTH + c0 + RET_DIM] = dv.astype(BF16)
                dp_ref[rows, 3 * RET_WIDTH + c0:3 * RET_WIDTH + c0 + RET_DIM] = dg.astype(BF16)

    steps = nc // RET_STEP
    rev = lambda n: steps - 1 - n
    full = lambda shape: pl.BlockSpec(shape, lambda n: (0,) * len(shape))
    step = RET_STEP * BLK
    return pl.pallas_call(
        body, name="retention_bwd", grid=(steps,),
        in_specs=[pl.BlockSpec((step, 4 * RET_WIDTH), lambda n: (rev(n), 0)),
                  pl.BlockSpec((step, RET_WIDTH), lambda n: (rev(n), 0)),
                  pl.BlockSpec((RET_HEADS, RET_STEP, RET_DIM, RET_DIM), lambda n: (0, rev(n), 0, 0)),
                  pl.BlockSpec((step, RET_WIDTH), lambda n: (rev(n), 0)),
                  pl.BlockSpec((step, RET_DIM // 2), lambda n: (rev(n), 0)), pl.BlockSpec((step, RET_DIM // 2), lambda n: (rev(n), 0)),
                  full((RET_HEADS, BLK, BLK)), full((RET_HEADS, BLK, 1)), full((RET_HEADS, BLK, 1)), full((RET_HEADS, 1, 1))],
        out_specs=pl.BlockSpec((step, 4 * RET_WIDTH), lambda n: (rev(n), 0)),
        out_shape=jax.ShapeDtypeStruct((s, EVEN_IN), BF16),
        scratch_shapes=[pltpu.VMEM((RET_HEADS, RET_DIM, RET_DIM), F32)],
        compiler_params=_params("arbitrary"))(proj, o, states, dcat, cos, sin, intra, q_dec, k_dec, chunk_dec)


CONV_ROWS = 256
HALO = 16


def _conv_pieces(p, halo, first):
    gb, gc, u, gv = (p[:, i * CONV_WIDTH:(i + 1) * CONV_WIDTH] for i in range(4))
    cu = gc * u
    hcu = halo[:, CONV_WIDTH:2 * CONV_WIDTH] * halo[:, 2 * CONV_WIDTH:3 * CONV_WIDTH]
    hcu = jnp.where(first, 0.0, hcu)
    r1, r2 = hcu[HALO - 1:HALO], hcu[HALO - 2:HALO - 1]
    row = lax.broadcasted_iota(jnp.int32, cu.shape, 0)
    m1 = jnp.where(row == 0, r1, pltpu.roll(cu, 1, 0))
    m2 = jnp.where(row == 0, r2, jnp.where(row == 1, r1, pltpu.roll(cu, 2, 0)))
    return gb, gc, u, gv, cu, m1, m2


def _conv_fwd(proj, conv_w, cat):
    s = proj.shape[0]
    per = CONV_ROWS // HALO

    def body(p_ref, halo_ref, w_ref, cat_in, cat_ref):
        del cat_in
        first = pl.program_id(0) == 0
        gb, _, _, gv, cu, m1, m2 = _conv_pieces(p_ref[...].astype(F32), halo_ref[...].astype(F32), first)
        conv = w_ref[0:1, :] * m2 + w_ref[1:2, :] * m1 + w_ref[2:3, :] * cu
        cat_ref[...] = (gb * conv * (gv * _sigmoid(gv))).astype(BF16)

    return pl.pallas_call(
        body, name="conv_fwd", grid=(s // CONV_ROWS,),
        in_specs=[pl.BlockSpec((CONV_ROWS, 4 * CONV_WIDTH), lambda i: (i, 1)),
                  pl.BlockSpec((HALO, 4 * CONV_WIDTH), lambda i: (jnp.maximum(i * per - 1, 0), 1)),
                  pl.BlockSpec((3, CONV_WIDTH), lambda i: (0, 0)), ANY],
        out_specs=pl.BlockSpec((CONV_ROWS, CONV_WIDTH), lambda i: (i, 1)),
        out_shape=jax.ShapeDtypeStruct(cat.shape, cat.dtype), input_output_aliases={3: 0},
        compiler_params=_params("parallel"))(proj, proj, conv_w, cat)


def _conv_bwd(proj, dcat, conv_w, dproj):
    s = proj.shape[0]
    per = CONV_ROWS // HALO
    last_halo = s // HALO - 1
    nsteps = s // CONV_ROWS

    def body(p_ref, halo_ref, nxt_ref, dc_ref, dnxt_ref, w_ref, dp_in, dp_ref, dw_ref):
        del dp_in
        i = pl.program_id(0)
        gb, gc, u, gv, cu, m1, m2 = _conv_pieces(p_ref[...].astype(F32), halo_ref[...].astype(F32), i == 0)
        w0, w1, w2 = w_ref[0:1, :], w_ref[1:2, :], w_ref[2:3, :]
        conv = w0 * m2 + w1 * m1 + w2 * cu
        dco = dc_ref[...].astype(F32)
        sg = _sigmoid(gv)
        silu = gv * sg
        dgb = dco * conv * silu
        dco_gb = dco * gb
        dgv = dco_gb * conv * (sg * (1.0 + gv * (1.0 - sg)))
        dconv = dco_gb * silu
        nxt = nxt_ref[...].astype(F32)
        ngv = nxt[:, 3 * CONV_WIDTH:]
        dnext = dnxt_ref[...].astype(F32) * nxt[:, :CONV_WIDTH] * (ngv * _sigmoid(ngv))
        dnext = jnp.where(i == nsteps - 1, 0.0, dnext)
        n1, n2 = dnext[0:1], dnext[1:2]
        row = lax.broadcasted_iota(jnp.int32, dconv.shape, 0)
        p1 = jnp.where(row == CONV_ROWS - 1, n1, pltpu.roll(dconv, CONV_ROWS - 1, 0))
        p2 = jnp.where(row == CONV_ROWS - 1, n2, jnp.where(row == CONV_ROWS - 2, n1, pltpu.roll(dconv, CONV_ROWS - 2, 0)))
        dcu = w2 * dconv + w1 * p1 + w0 * p2
        dp_ref[...] = jnp.concatenate([dgb, dcu * u, dcu * gc, dgv], axis=1).astype(BF16)

        @pl.when(i == 0)
        def _():
            dw_ref[...] = jnp.zeros_like(dw_ref)

        taps = [jnp.sum(dconv * m, axis=0, keepdims=True) for m in (m2, m1, cu)]
        r8 = lax.broadcasted_iota(jnp.int32, dw_ref.shape, 0)
        dw_ref[...] += jnp.where(r8 == 0, taps[0], jnp.where(r8 == 1, taps[1], jnp.where(r8 == 2, taps[2], 0.0)))

    return pl.pallas_call(
        body, name="conv_bwd", grid=(nsteps,),
        in_specs=[pl.BlockSpec((CONV_ROWS, 4 * CONV_WIDTH), lambda i: (i, 1)),
                  pl.BlockSpec((HALO, 4 * CONV_WIDTH), lambda i: (jnp.maximum(i * per - 1, 0), 1)),
                  pl.BlockSpec((HALO, 4 * CONV_WIDTH), lambda i: (jnp.minimum((i + 1) * per, last_halo), 1)),
                  pl.BlockSpec((CONV_ROWS, CONV_WIDTH), lambda i: (i, 1)),
                  pl.BlockSpec((HALO, CONV_WIDTH), lambda i: (jnp.minimum((i + 1) * per, last_halo), 1)),
                  pl.BlockSpec((3, CONV_WIDTH), lambda i: (0, 0)), ANY],
        out_specs=[pl.BlockSpec((CONV_ROWS, 4 * CONV_WIDTH), lambda i: (i, 1)), pl.BlockSpec((8, CONV_WIDTH), lambda i: (0, 0))],
        out_shape=[jax.ShapeDtypeStruct(dproj.shape, dproj.dtype), jax.ShapeDtypeStruct((8, CONV_WIDTH), F32)],
        input_output_aliases={6: 0},
        compiler_params=_params("arbitrary"))(proj, proj, proj, dcat, dcat, conv_w, dproj)


GROUP_HEADS = Q_HEADS // KV_HEADS
GROUP_WIDTH = GROUP_HEADS * HEAD_DIM
SLAB = 2 * HEAD_DIM
KV_COL = ATTN_WIDTH // GROUP_WIDTH
GATE_COL = KV_COL + 1
ATTN_SCALE = HEAD_DIM ** -0.5
KV_ROWS = 1024


def _half_mask(shape, which):
    return (_lane(shape) // HEAD_DIM) == which


def _dup_head(slab, which):
    kept = jnp.where(_half_mask(slab.shape, which), slab, 0.0)
    return kept + pltpu.roll(kept, HEAD_DIM, 1)


def _stack_heads(x):
    parts = []
    for sl in range(GROUP_WIDTH // SLAB):
        slab = x[:, sl * SLAB:(sl + 1) * SLAB]
        parts += [jnp.where(_half_mask(slab.shape, e), slab, 0.0) for e in range(2)]
    return jnp.concatenate(parts, axis=0)


def _unstack_heads(y):
    slabs = []
    for sl in range(GROUP_WIDTH // SLAB):
        a, b = y[(2 * sl) * BLK:(2 * sl + 1) * BLK], y[(2 * sl + 1) * BLK:(2 * sl + 2) * BLK]
        slabs.append(jnp.where(_half_mask(a.shape, 0), a, b))
    return jnp.concatenate(slabs, axis=1)


def _q_prep(q, qw, cosf, sins, ones_bd):
    rstd = lax.rsqrt(_group_mean(q * q, ones_bd) + EPS)
    nrm = q * rstd
    y = nrm * qw
    return nrm, rstd, y * cosf + _partner64(y) * sins


def _band(tri_ref, n):
    own = tri_ref[...] > 0.5
    return own, jnp.where(jnp.logical_and(n == 0, jnp.logical_not(own)), -1e30, 0.0)


def _fold(pair, own):
    return jnp.where(own, pair[:, BLK:], pair[:, :BLK])


def _unfold(folded, own):
    return jnp.concatenate([jnp.where(own, 0.0, folded), jnp.where(own, folded, 0.0)], axis=1)


def _head_probs(raw_scores, sink, own, bias):
    sc = _fold(raw_scores, own) + bias
    m = jnp.maximum(jnp.max(sc, axis=-1, keepdims=True), sink)
    p = jnp.exp(sc - m)
    psink = jnp.exp(sink - m)
    inv = 1.0 / (jnp.sum(p, axis=-1, keepdims=True) + psink)
    return p * inv, psink * inv


def _k_prep(k, kw, cosf, sins, ones_bd):
    rstd = lax.rsqrt(_group_mean(k * k, ones_bd) + EPS)
    nrm = k * rstd
    y = nrm * kw
    return nrm, rstd, y * cosf + _partner64(y) * sins


def _qk_prep(proj, qw, kw, cos, sins, ones_q, ones_kv):
    s = proj.shape[0]
    rows = min(KV_ROWS, s)

    def body(p_ref, qw_ref, kw_ref, cos_ref, sin_ref, oq_ref, ok_ref, o_ref):
        j = pl.program_id(1)

        @pl.when(j < KV_COL)
        def _():
            cosf, sinf = _tile_lanes(cos_ref[...], 4), _tile_lanes(sin_ref[...], 4)
            roped = _q_prep(p_ref[...].astype(F32), qw_ref[...], cosf, sinf, oq_ref[...])[2]
            o_ref[...] = (roped * ATTN_SCALE).astype(BF16)

        @pl.when(j == KV_COL)
        def _():
            cosf, sinf = _tile_lanes(cos_ref[...], 2), _tile_lanes(sin_ref[...], 2)
            kr = _k_prep(p_ref[:, :KV_WIDTH].astype(F32), kw_ref[...], cosf, sinf, ok_ref[...])[2]
            o_ref[...] = jnp.concatenate([kr.astype(BF16), p_ref[:, KV_WIDTH:]], axis=1)

    full = lambda shape: pl.BlockSpec(shape, lambda i, j: (0,) * len(shape))
    tab = pl.BlockSpec((rows, SLAB), lambda i, j: (i, 0))
    blk = pl.BlockSpec((rows, GROUP_WIDTH), lambda i, j: (i, j))
    return pl.pallas_call(
        body, name="qk_prep", grid=(s // rows, KV_COL + 1),
        in_specs=[blk, full((1, GROUP_WIDTH)), full((1, KV_WIDTH)), tab, tab, full(ones_q.shape), full(ones_kv.shape)],
        out_specs=blk, out_shape=jax.ShapeDtypeStruct((s, ATTN_WIDTH + 2 * KV_WIDTH), BF16),
        compiler_params=_params("parallel", "arbitrary"))(proj, qw, kw, cos, sins, ones_q, ones_kv)


def _keys_values(kc_ref, kp_ref, vc_ref, vp_ref, head):
    lanes = slice((head // 2) * SLAB, (head // 2 + 1) * SLAB)
    dup = lambda ref: _dup_head(ref[:, lanes].astype(F32), head % 2)
    return (jnp.concatenate([dup(kp_ref), dup(kc_ref)], axis=0).astype(BF16),
            jnp.concatenate([dup(vp_ref), dup(vc_ref)], axis=0).astype(BF16))


FWD_STEP_HEADS = 4
BWD_STEP_HEADS = 2


def _swa_specs(heads):
    kv_width = heads * HEAD_DIM
    prev = lambda n: jnp.maximum(n - 1, 0)
    kv = lambda col0, row: pl.BlockSpec((BLK, kv_width), lambda gs, n: (row(n), col0 // kv_width + gs))
    cur = lambda n: n
    full = lambda shape: pl.BlockSpec(shape, lambda gs, n: (0,) * len(shape))
    gate = lambda t: pl.BlockSpec((BLK, GROUP_WIDTH), lambda gs, n: (n, GATE_COL + heads * gs + t))
    return dict(
        sinks=pl.BlockSpec(memory_space=pltpu.SMEM), gates=[gate(t) for t in range(heads)],
        kc=kv(ATTN_WIDTH, cur), kp=kv(ATTN_WIDTH, prev), vc=kv(ATTN_WIDTH + KV_WIDTH, cur), vp=kv(ATTN_WIDTH + KV_WIDTH, prev),
        tri=full((BLK, BLK)), step=pl.BlockSpec((BLK, heads * GROUP_WIDTH), lambda gs, n: (n, gs)))


def _lower_triangle():
    return jnp.tril(jnp.ones((BLK, BLK), F32))


def _swa_fwd(qk, proj, sinks):
    s = proj.shape[0]
    nb = s // BLK
    heads = FWD_STEP_HEADS
    sp = _swa_specs(heads)

    def body(sink_ref, q_ref, kc_ref, kp_ref, vc_ref, vp_ref, *rest):
        gate_refs, (tri_ref, ag_ref, o_ref) = rest[:heads], rest[heads:]
        gs, n = pl.program_id(0), pl.program_id(1)
        own, bias = _band(tri_ref, n)
        for t, gate_ref in enumerate(gate_refs):
            cols = slice(t * GROUP_WIDTH, (t + 1) * GROUP_WIDTH)
            first_head = (heads * gs + t) * GROUP_HEADS
            kcat, vcat = _keys_values(kc_ref, kp_ref, vc_ref, vp_ref, t)
            scores = _dot_nt(_stack_heads(q_ref[:, cols]), kcat)
            probs = []
            for j in range(GROUP_HEADS):
                p, _ = _head_probs(scores[j * BLK:(j + 1) * BLK], sink_ref[first_head + j], own, bias)
                probs.append(_unfold(p, own).astype(BF16))
            o = _unstack_heads(_dot(jnp.concatenate(probs, axis=0), vcat))
            gate = gate_ref[...].astype(F32)
            o_ref[:, cols] = o.astype(BF16)
            ag_ref[:, cols] = (o * (gate * _sigmoid(gate))).astype(BF16)

    shp = jax.ShapeDtypeStruct((s, ATTN_WIDTH), BF16)
    return pl.pallas_call(
        body, name="swa_fwd", grid=(KV_HEADS // heads, nb),
        in_specs=[sp["sinks"], sp["step"], sp["kc"], sp["kp"], sp["vc"], sp["vp"], *sp["gates"], sp["tri"]],
        out_specs=[sp["step"], sp["step"]], out_shape=[shp, shp],
        compiler_params=_params("parallel", "arbitrary"),
    )(sinks, qk, qk, qk, qk, qk, *[proj] * heads, _lower_triangle())


def _swa_bwd(qk, proj, dag, sinks):
    s = proj.shape[0]
    nb = s // BLK
    heads = BWD_STEP_HEADS
    sp = _swa_specs(heads)

    def body(sink_ref, q_ref, kc_ref, kp_ref, vc_ref, vp_ref, *rest):
        gate_refs = rest[:heads]
        dag_ref, tri_ref, dq_ref, dkc_ref, dkp_ref, dvc_ref, dvp_ref, dsink_ref = rest[heads:]
        gs, n = pl.program_id(0), pl.program_id(1)
        own, bias = _band(tri_ref, n)

        @pl.when(n == 0)
        def _():
            dsink_ref[...] = jnp.zeros_like(dsink_ref)

        for t, gate_ref in enumerate(gate_refs):
            cols = slice(t * GROUP_WIDTH, (t + 1) * GROUP_WIDTH)
            first_head = (heads * gs + t) * GROUP_HEADS
            kcat, vcat = _keys_values(kc_ref, kp_ref, vc_ref, vp_ref, t)
            gate = gate_ref[...].astype(F32)
            do = dag_ref[:, cols].astype(F32) * (gate * _sigmoid(gate))
            q_stack = _stack_heads(q_ref[:, cols])
            do_stack = _stack_heads(do).astype(BF16)
            scores = _dot_nt(q_stack, kcat)
            dprobs = _dot_nt(do_stack, vcat)
            probs, dscores, dsinks = [], [], []
            for j in range(GROUP_HEADS):
                rows = slice(j * BLK, (j + 1) * BLK)
                p, psink = _head_probs(scores[rows], sink_ref[first_head + j], own, bias)
                dp = _fold(dprobs[rows], own)
                delta = jnp.sum(p * dp, axis=-1, keepdims=True)
                probs.append(_unfold(p, own).astype(BF16))
                dscores.append(_unfold(p * (dp - delta), own).astype(BF16))
                dsinks.append(-jnp.sum(psink * delta, axis=0, keepdims=True))
            ds = jnp.concatenate(dscores, axis=0)
            dk = _dot_tn(ds, q_stack)
            dv = _dot_tn(jnp.concatenate(probs, axis=0), do_stack)
            dk = dk + pltpu.roll(dk, HEAD_DIM, 1)
            dv = dv + pltpu.roll(dv, HEAD_DIM, 1)
            dkp_ref[t], dkc_ref[t] = dk[:BLK], dk[BLK:]
            dvp_ref[t], dvc_ref[t] = dv[:BLK], dv[BLK:]
            dq_ref[:, cols] = _unstack_heads(_dot(ds, kcat)).astype(BF16)
            r8 = lax.broadcasted_iota(jnp.int32, (8, SLAB), 0)
            upd = jnp.zeros((8, SLAB), F32)
            for j in range(GROUP_HEADS):
                upd = jnp.where(r8 == j, dsinks[j], upd)
            dsink_ref[t] += upd

    cur_out = pl.BlockSpec((heads, BLK, SLAB), lambda gs, n: (gs, n, 0))
    prev_out = pl.BlockSpec((heads, BLK, SLAB), lambda gs, n: (gs, (n + nb - 1) % nb, 0))
    kv_shape = jax.ShapeDtypeStruct((KV_HEADS, s, SLAB), F32)
    return pl.pallas_call(
        body, name="swa_bwd", grid=(KV_HEADS // heads, nb),
        in_specs=[sp["sinks"], sp["step"], sp["kc"], sp["kp"], sp["vc"], sp["vp"], *sp["gates"], sp["step"], sp["tri"]],
        out_specs=[sp["step"], cur_out, prev_out, cur_out, prev_out,
                   pl.BlockSpec((heads, 8, SLAB), lambda gs, n: (gs, 0, 0))],
        out_shape=[jax.ShapeDtypeStruct((s, ATTN_WIDTH), BF16), kv_shape, kv_shape, kv_shape, kv_shape,
                   jax.ShapeDtypeStruct((KV_HEADS, 8, SLAB), F32)],
        compiler_params=_params("parallel", "arbitrary"),
    )(sinks, qk, qk, qk, qk, qk, *[proj] * heads, dag, _lower_triangle())


def _swa_bwd_finish(proj, dqr, o, dag, dkc, dkp, dvc, dvp, qw, kw, cos, sins, ones_q, ones_kv):
    s = proj.shape[0]
    rows = min(KV_ROWS, s)
    n_q = KV_COL
    q_of = lambda j: jnp.clip(j - 1, 0, n_q - 1)
    gate_of = lambda j: jnp.clip(j - 1 - n_q, 0, n_q - 1)

    def body(kv_ref, p_ref, dqr_ref, o_ref, dag_ref, dkc_ref, dkp_ref, dvc_ref, dvp_ref, qw_ref, kw_ref, cos_ref, sin_ref,
             oq_ref, ok_ref, dp_ref, dqw_ref, dkw_ref):
        j, i = pl.program_id(0), pl.program_id(1)

        @pl.when(j == 0)
        def _():
            @pl.when(i == 0)
            def _():
                dkw_ref[...] = jnp.zeros_like(dkw_ref)

            def assemble(cur_ref, prv_ref):
                tot = [cur_ref[h] + prv_ref[h] for h in range(KV_HEADS)]
                first = _half_mask(tot[0].shape, 0)
                return jnp.concatenate([jnp.where(first, tot[0], tot[1]), jnp.where(first, tot[2], tot[3])], axis=1)

            dkr = assemble(dkc_ref, dkp_ref)
            dv = assemble(dvc_ref, dvp_ref)
            cosf, sinf = _tile_lanes(cos_ref[...], 2), _tile_lanes(sin_ref[...], 2)
            nrm, rstd, _ = _k_prep(kv_ref[:, :KV_WIDTH].astype(F32), kw_ref[...], cosf, sinf, ok_ref[...])
            dy = dkr * cosf + _partner64(dkr * sinf)
            dn = dy * kw_ref[...]
            dk = rstd * (dn - nrm * _group_mean(dn * nrm, ok_ref[...], passes=1))
            dp_ref[...] = jnp.concatenate([dk, dv], axis=1).astype(BF16)
            dkw_ref[...] += jnp.sum(dy * nrm, axis=0, keepdims=True)

        @pl.when(jnp.logical_and(j >= 1, j <= n_q))
        def _():
            @pl.when(i == 0)
            def _():
                dqw_ref[...] = jnp.zeros_like(dqw_ref)

            cosf, sinf = _tile_lanes(cos_ref[...], 4), _tile_lanes(sin_ref[...], 4)
            nrm, rstd, _ = _q_prep(p_ref[...].astype(F32), qw_ref[...], cosf, sinf, oq_ref[...])
            dq = dqr_ref[...].astype(F32) * ATTN_SCALE
            dy = dq * cosf + _partner64(dq * sinf)
            dn = dy * qw_ref[...]
            dp_ref[...] = (rstd * (dn - nrm * _group_mean(dn * nrm, oq_ref[...], passes=1))).astype(BF16)
            dqw_ref[0] += jnp.sum(dy * nrm, axis=0, keepdims=True)

        @pl.when(j > n_q)
        def _():
            gate = p_ref[...].astype(F32)
            sg = _sigmoid(gate)
            dp_ref[...] = (dag_ref[...].astype(F32) * o_ref[...].astype(F32) * (sg * (1.0 + gate * (1.0 - sg)))).astype(BF16)

    first_pass = lambda j, i: jnp.where(j == 0, i, 0)
    acc = pl.BlockSpec((KV_HEADS, rows, SLAB), lambda j, i: (0, first_pass(j, i), 0))
    full = lambda shape: pl.BlockSpec(shape, lambda j, i: (0,) * len(shape))
    tab = pl.BlockSpec((rows, SLAB), lambda j, i: (i, 0))
    out_col = lambda j: jnp.where(j == 0, KV_COL, jnp.where(j <= n_q, j - 1, j))
    return pl.pallas_call(
        body, name="swa_bwd_finish", grid=(2 * n_q + 1, s // rows),
        in_specs=[pl.BlockSpec((rows, GROUP_WIDTH), lambda j, i: (first_pass(j, i), KV_COL)),
                  pl.BlockSpec((rows, GROUP_WIDTH), lambda j, i: (jnp.where(j == 0, 0, i), jnp.where(j <= n_q, q_of(j), j))),
                  pl.BlockSpec((rows, GROUP_WIDTH), lambda j, i: (jnp.where(jnp.logical_and(j >= 1, j <= n_q), i, 0), q_of(j))),
                  pl.BlockSpec((rows, GROUP_WIDTH), lambda j, i: (jnp.where(j > n_q, i, 0), gate_of(j))),
                  pl.BlockSpec((rows, GROUP_WIDTH), lambda j, i: (jnp.where(j > n_q, i, 0), gate_of(j))),
                  acc, acc, acc, acc, full((1, GROUP_WIDTH)), full((1, KV_WIDTH)), tab, tab,
                  full(ones_q.shape), full(ones_kv.shape)],
        out_specs=[pl.BlockSpec((rows, GROUP_WIDTH), lambda j, i: (i, out_col(j))),
                   pl.BlockSpec((1, 8, GROUP_WIDTH), lambda j, i: (q_of(j), 0, 0)), pl.BlockSpec((8, KV_WIDTH), lambda j, i: (0, 0))],
        out_shape=[jax.ShapeDtypeStruct((s, ODD_IN), BF16), jax.ShapeDtypeStruct((n_q, 8, GROUP_WIDTH), F32),
                   jax.ShapeDtypeStruct((8, KV_WIDTH), F32)],
        compiler_params=_params("arbitrary", "arbitrary"),
    )(proj, proj, dqr, o, dag, dkc, dkp, dvc, dvp, qw, kw, cos, sins, ones_q, ones_kv)


def _place():
    x, y, c = lax.axis_index("x"), lax.axis_index("y"), lax.axis_index("c")
    return x, y, c


OTHER_CHIPS = ((1, 0), (0, 1), (1, 1))


def _half_rows(ref, half, rows):
    return ref.at[pl.ds(pl.multiple_of(half * (rows // 2), 8), rows // 2)]


DMA_CHUNK_BYTES = 1 << 20
BF16_TILE_ROWS = 16


def _n_chunks(ref):
    rows = ref.shape[-2]
    nbytes = math.prod(ref.shape) * jnp.dtype(ref.dtype).itemsize
    n = 1
    while 2 * n * DMA_CHUNK_BYTES <= nbytes and rows % (2 * n * BF16_TILE_ROWS) == 0:
        n *= 2
    return n


def _row_chunk(ref, k, n):
    rows = ref.shape[-2] // n
    return ref.at[pl.ds(k * rows, rows)] if len(ref.shape) == 2 else ref.at[:, pl.ds(k * rows, rows)]


def _push(src, dst, send_sem, recv_sem, device_id):
    n = _n_chunks(src)
    for k in range(n):
        pltpu.make_async_remote_copy(src_ref=_row_chunk(src, k, n), dst_ref=_row_chunk(dst, k, n), send_sem=send_sem,
                                     recv_sem=recv_sem, device_id=device_id, device_id_type=MESH).start()
    return pltpu.make_async_remote_copy(src_ref=src, dst_ref=dst, send_sem=send_sem, recv_sem=recv_sem,
                                        device_id=device_id, device_id_type=MESH)


HBM = pl.BlockSpec(memory_space=pltpu.HBM)
SEM = pl.BlockSpec(memory_space=pltpu.SEMAPHORE)
SPLIT_COPY_EFFECT = pltpu.SideEffectType.DATAFLOW_SIDE_EFFECTING


def _in_hbm(a):
    return pltpu.with_memory_space_constraint(a, pltpu.HBM)


def _start_copies(name, arrays, plan, n_copies, after=None):
    n = len(arrays)

    def body(*refs):
        send_sem, recv_sem = refs[-n - 3], refs[-n - 2]
        for k, (src, dst, peer) in enumerate(plan(refs[:n])):
            _push(src, dst, send_sem.at[k], recv_sem.at[k], peer)
        refs[-1][...] = jnp.zeros_like(refs[-1])

    dma = pltpu.SemaphoreType.DMA((n_copies,))
    outs = pl.pallas_call(
        body, name=name,
        out_shape=(dma, dma, *[pltpu.HBM(a.shape, a.dtype) for a in arrays], jax.ShapeDtypeStruct((8, 128), F32)),
        in_specs=[HBM] * n + ([ANY] if after is not None else []),
        out_specs=(SEM, SEM, *[HBM] * n, pl.BlockSpec(memory_space=pltpu.VMEM)),
        input_output_aliases={i: i + 2 for i in range(n)},
        compiler_params=pltpu.CompilerParams(has_side_effects=SPLIT_COPY_EFFECT),
    )(*[_in_hbm(a) for a in arrays], *((after,) if after is not None else ()))
    return outs[0], outs[1], list(outs[2:2 + n]), outs[-1]


def _wait_copies(name, send_sem, recv_sem, arrays, plan, after):
    n = len(arrays)
    after = list(after) if isinstance(after, (list, tuple)) else [after]

    def body(*refs):
        send_ref, recv_ref = refs[n], refs[n + 1]
        for k, (src, dst, peer) in enumerate(plan(refs[:n])):
            cp = pltpu.make_async_remote_copy(src_ref=src, dst_ref=dst, send_sem=send_ref.at[k], recv_sem=recv_ref.at[k],
                                              device_id=peer, device_id_type=MESH)
            cp.wait_send()
            cp.wait_recv()

    return list(pl.pallas_call(
        body, name=name, out_shape=tuple(pltpu.HBM(a.shape, a.dtype) for a in arrays),
        in_specs=[HBM] * n + [SEM, SEM] + [ANY] * len(after), out_specs=tuple([HBM] * n),
        input_output_aliases={i: i for i in range(n)},
        compiler_params=pltpu.CompilerParams(has_side_effects=SPLIT_COPY_EFFECT),
    )(*arrays, send_sem, recv_sem, *after))


def _gather_region(full, kind, chip, half=None):
    if kind == "whole":
        return full.at[chip]
    if kind == "col" or isinstance(kind, tuple):
        part, parts = (0, 1) if kind == "col" else kind[1:]
        rows, width = full.shape[0], full.shape[1] // N_CHIPS
        piece = full.at[:, pl.ds(pl.multiple_of(chip * width + part * (width // parts), LANE_TILE), width // parts)]
    else:
        rows = full.shape[0] // N_CHIPS
        piece = full.at[pl.ds(pl.multiple_of(chip * rows, BF16_TILE_ROWS), rows)]
    return piece if half is None else _half_rows(piece, half, rows)


def _gather_plan(kinds):
    def plan(fulls):
        x, y, c = _place()
        copies = []
        for fx, fy in OTHER_CHIPS:
            for full, kind in zip(fulls, kinds):
                mine = _gather_region(full, kind, 2 * x + y, c)
                copies.append((mine, mine, (x ^ fx, y ^ fy, c)))
        return copies

    return plan


def _gather_start(name, fulls, kinds, after=None):
    return _start_copies(name, list(fulls), _gather_plan(kinds), 3 * len(kinds), after)


def _pass_on_plan(kinds):
    split = [i for i, kind in enumerate(kinds) if kind != "whole"]

    def plan(fulls):
        x, y, c = _place()
        copies = []
        for fx, fy in OTHER_CHIPS:
            chip = 2 * (x ^ fx) + (y ^ fy)
            for i in split:
                landed = _gather_region(fulls[i], kinds[i], chip, c)
                copies.append((landed, landed, (x, y, 1 - c)))
        return copies

    return plan, 3 * len(split)


def _gather_arrive(name, started, kinds, after, arrays=None):
    send_sem, recv_sem, fulls, _ = started
    fulls = _wait_copies(name + "_wait", send_sem, recv_sem, fulls if arrays is None else arrays, _gather_plan(kinds), after)
    plan, count = _pass_on_plan(kinds)
    return _start_copies(name + "_pass_on", fulls, plan, count)


def _gather_done(name, passed_on, kinds, after):
    send_sem, recv_sem, fulls, _ = passed_on
    return _wait_copies(name + "_pass_on_wait", send_sem, recv_sem, fulls, _pass_on_plan(kinds)[0], after)


def _allreduce_small(v):
    def body(v_ref, out_ref, buf, send_sems, recv_sems):
        x, y, c = _place()
        me = 4 * x + 2 * y + c
        buf[me] = v_ref[...]
        copies = []
        for r in range(1, N_DEV):
            peer = (x ^ (r >> 2), y ^ ((r >> 1) & 1), c ^ (r & 1))
            cp = pltpu.make_async_remote_copy(src_ref=v_ref, dst_ref=buf.at[me], send_sem=send_sems.at[r - 1],
                                              recv_sem=recv_sems.at[r - 1], device_id=peer, device_id_type=MESH)
            cp.start()
            copies.append(cp)
        for cp in copies:
            cp.wait_recv()
        for cp in copies:
            cp.wait_send()
        total = buf[0]
        for d in range(1, N_DEV):
            total = total + buf[d]
        out_ref[...] = total

    vm = pl.BlockSpec(memory_space=pltpu.VMEM)
    return pl.pallas_call(
        body, name="allreduce_small", in_specs=[vm], out_specs=vm, out_shape=jax.ShapeDtypeStruct(v.shape, v.dtype),
        scratch_shapes=[pltpu.VMEM((N_DEV,) + v.shape, v.dtype), pltpu.SemaphoreType.DMA((N_DEV - 1,)),
                        pltpu.SemaphoreType.DMA((N_DEV - 1,))],
        compiler_params=pltpu.CompilerParams(has_side_effects=True),
    )(v)


def _exchange_plan(n):
    def plan(refs):
        x, y, c = _place()
        copies = []
        for g, theirs in zip(refs[:n], refs[n:]):
            half = g.shape[1] // 2
            src = g.at[:, pl.ds(pl.multiple_of((1 - c) * half, BF16_TILE_ROWS), half)]
            copies.append((src, theirs, (x, y, 1 - c)))
        return copies

    return plan


def _pair_sum(g, theirs, core, name):
    pieces, half, cols = theirs.shape
    tr = min(half, 256)
    per = half // tr

    def body(core_ref, g_ref, t_ref, o_ref):
        del core_ref
        o_ref[...] = (g_ref[...].astype(F32) + t_ref[...].astype(F32)).astype(BF16)

    spec = pl.BlockSpec((1, tr, cols), lambda p, i, core_ref: (p, i, 0))
    return pl.pallas_call(
        body, name=name, out_shape=jax.ShapeDtypeStruct(theirs.shape, BF16),
        grid_spec=pltpu.PrefetchScalarGridSpec(
            num_scalar_prefetch=1, grid=(pieces, per),
            in_specs=[pl.BlockSpec((1, tr, cols), lambda p, i, core_ref: (p, core_ref[0] * per + i, 0)), spec],
            out_specs=spec),
        compiler_params=_params("parallel", "parallel"))(core, g, theirs)


def _scatter_plan(n):
    def plan(refs):
        parts, stacks = refs[:n], refs[n:]
        x, y, c = _place()
        copies = []
        for fx, fy in OTHER_CHIPS:
            chip = 2 * (x ^ fx) + (y ^ fy)
            for part, stack in zip(parts, stacks):
                if part.shape[0] == N_CHIPS:
                    piece = part.at[chip]
                else:
                    width = part.shape[2] // N_CHIPS
                    piece = part.at[0].at[:, pl.ds(pl.multiple_of(chip * width, 128), width)]
                copies.append((piece, stack.at[2 * x + y], (x ^ fx, y ^ fy, c)))
        return copies

    return plan


def _scatter_start(name, parts, after=None):
    def landing(a):
        return (N_CHIPS, a.shape[1], a.shape[2] if a.shape[0] == N_CHIPS else a.shape[2] // N_CHIPS)

    stacks = [lax.empty(landing(a), a.dtype) for a in parts]
    return _start_copies(name, list(parts) + stacks, _scatter_plan(len(parts)), 3 * len(parts), after)


def _scatter_finish(name, started, after):
    send_sem, recv_sem, arrays, _ = started
    n = len(arrays) // 2
    arrays = _wait_copies(name + "_wait", send_sem, recv_sem, arrays, _scatter_plan(n), after)
    return arrays[:n], arrays[n:]


def _sum_chips(part, stack, place, name):
    _, r, c = stack.shape
    tr = 256
    per = r // tr

    def body(place_ref, own_ref, a_ref, b_ref, c_ref, o_ref):
        del place_ref
        total = own_ref[0].astype(F32)
        for ref in (a_ref, b_ref, c_ref):
            total = total + ref[0].astype(F32)
        o_ref[...] = total

    if part.shape[0] == N_CHIPS:
        own = pl.BlockSpec((1, tr, c), lambda i, pr: (pr[1], i, 0))
    else:
        own = pl.BlockSpec((1, tr, c), lambda i, pr: (0, i, pr[1]))
    other = lambda flip: pl.BlockSpec((1, tr, c), lambda i, pr: (pr[1] ^ flip, i, 0))
    return pl.pallas_call(
        body, name=name, out_shape=jax.ShapeDtypeStruct((2 * r, c), F32),
        grid_spec=pltpu.PrefetchScalarGridSpec(
            num_scalar_prefetch=1, grid=(per,), in_specs=[own, other(2), other(1), other(3)],
            out_specs=pl.BlockSpec((tr, c), lambda i, pr: (pr[0] * per + i, 0))),
        compiler_params=_params("parallel"))(place, part, stack, stack, stack)


def _own_half_plan(fulls):
    x, y, c = _place()
    return [(mine, mine, (x, y, 1 - c)) for mine in (_half_rows(full, c, full.shape[0]) for full in fulls)]


MM = dict(tm=2048, tn=1024, tk=2048)
MM_LONG_K = dict(tm=1024, tn=1024, tk=4096)


def _local_step(x, target, ev_norm_w, q_norm_w, k_norm_w, sinks, own_first, weights_first, late_arrived, weights_late, grads_out):
    s = x.shape[0]
    cos_r, sin_r = _rope_tables(s, RET_DIM)
    cos_a, sin_a = _rope_tables(s, HEAD_DIM)
    cos_a = np.tile(cos_a, (1, 4))
    sins_a = np.tile(np.concatenate([-sin_a, sin_a], axis=1), (1, 2))
    tables = _retention_tables()
    ones_q, ones_kv = _block_diag_mean(min(GROUP_WIDTH, MXU_WIDTH)), _block_diag_mean(min(KV_WIDTH, MXU_WIDTH))
    qw_g = jnp.tile(q_norm_w, (1, GROUP_WIDTH // HEAD_DIM))
    kw_kv = jnp.tile(k_norm_w, (1, KV_WIDTH // HEAD_DIM))
    sinks1 = sinks.reshape(Q_HEADS)

    own_w_in0, own_block, start_token = own_first
    h0 = _rmsnorm(x, ev_norm_w, "norm0", after=start_token)
    shifted = dict(shift=own_block * (own_w_in0.shape[1] // MM["tn"]), total=EVEN_IN // MM["tn"], tm=MM["tm"], tn=MM["tn"],
                   out_dtype=BF16)
    own_blocks = own_w_in0.shape[1] // MM["tn"]
    proj0 = _mm_shifted(h0, own_w_in0, b_shifted=False, first=0, count=own_blocks, name="proj0_own", **shifted)
    for part in range(own_blocks):
        w_in0, small_weights, token = weights_first(part, proj0)
        if small_weights is not None:
            conv_w, od_norm_w = small_weights
        proj0 = _mm_shifted(h0, w_in0, b_shifted=True, first=own_blocks + part, stride=own_blocks,
                            count=shifted["total"] // own_blocks - 1, name=f"proj0_rest{part}", into=proj0, after=token, **shifted)
    o_ret, cat, states = _retention_fwd(proj0, cos_r, sin_r, tables, after=late_arrived(proj0))
    cat = _conv_fwd(proj0, conv_w, cat)
    w_out0, w_in1, w_out1 = weights_late(cat)

    def residual_and_norm(prod, x_ref, w_ref, x1_ref, h1_ref):
        x1v = x_ref[...] + prod
        x1_ref[...] = x1v
        rstd = lax.rsqrt(jnp.mean(x1v * x1v, axis=-1, keepdims=True) + EPS)
        h1_ref[...] = (x1v * rstd * w_ref[...]).astype(BF16)

    def residual_and_loss(prod, x1_ref, t_ref, dyb_ref, sq_ref):
        diff = (x1_ref[...] + prod) - t_ref[...]
        dyb_ref[...] = (diff * (1.0 / D_MODEL)).astype(BF16)

        @pl.when(pl.program_id(0) == 0)
        def _():
            sq_ref[...] = jnp.zeros_like(sq_ref)

        sq_ref[...] += jnp.sum(jnp.sum(diff * diff, axis=1, keepdims=True), axis=0, keepdims=True)

    act = lambda dt: jax.ShapeDtypeStruct((s, D_MODEL), dt)
    x1, h1 = _mm_rows(cat, w_out0, [x], [od_norm_w], [act(F32), act(BF16)], residual_and_norm, tm=min(s, 512), name="out0")
    proj1 = _mm(h1, w_in1, mode="nn", out_dtype=BF16, name="proj1", tm=2048, tn=1536, tk=2048)
    qk = _qk_prep(proj1, qw_g, kw_kv, cos_a, sins_a, ones_q, ones_kv)
    ag, o_att = _swa_fwd(qk, proj1, sinks1)
    dy_b, sq = _mm_rows(ag, w_out1, [x1, target], [], [act(BF16), jax.ShapeDtypeStruct((8, 128), F32)],
                        residual_and_loss, tm=min(s, 512), name="out1")

    g_w_out1 = _mm(ag, dy_b, mode="tn", out_dtype=BF16, name="g_w_out1", **MM_LONG_K)
    dag = _mm(dy_b, w_out1, mode="nt", out_dtype=BF16, name="d_ag", **MM)
    dqr, dkc, dkp, dvc, dvp, dsink = _swa_bwd(qk, proj1, dag, sinks1)
    dproj1, dqw, dkw = _swa_bwd_finish(proj1, dqr, o_att, dag, dkc, dkp, dvc, dvp, qw_g, kw_kv, cos_a, sins_a, ones_q, ones_kv)
    g_w_in1 = _mm(h1, dproj1, mode="tn", out_dtype=BF16, name="g_w_in1", tm=1024, tn=768, tk=4096)
    dh1 = _mm(dproj1, w_in1, mode="nt", out_dtype=BF16, name="d_h1", tm=1024, tn=1024, tk=ODD_IN)
    dx1_b, g_norm1 = _rmsnorm_bwd(x1, od_norm_w, dh1, dy_b, "norm1_bwd", BF16)

    g_w_out0 = _mm(cat, dx1_b, mode="tn", out_dtype=BF16, name="g_w_out0", **MM_LONG_K)
    token = grads_out.begin("upper", (("od_w_in", g_w_in1, "col"), ("od_w_out", g_w_out1, "row"), ("ev_w_out", g_w_out0, "row")))
    dcat = _mm(dx1_b, w_out0, mode="nt", out_dtype=BF16, name="d_cat", after=token, **MM)
    token = grads_out.send("upper", dcat)
    dproj0 = _retention_bwd(proj0, o_ret, states, dcat, cos_r, sin_r, tables)
    dproj0, g_conv = _conv_bwd(proj0, dcat, conv_w, dproj0)
    g_w_in0 = _mm(h0, dproj0, mode="tn", out_dtype=BF16, name="g_w_in0", after=token, **MM_LONG_K)
    token = grads_out.begin("in0", (("ev_w_in", g_w_in0, "col"),))
    half_blocks = s // (2 * MM_LONG_K["tm"])
    dh0 = _mm(dproj0, w_in0, mode="nt", out_dtype=BF16, name="d_h0_top", after=token, m_blocks=(0, half_blocks), **MM_LONG_K)
    dh0 = _mm(dproj0, w_in0, mode="nt", out_dtype=BF16, name="d_h0_bottom", after=grads_out.send("in0", dh0),
              m_blocks=(half_blocks, half_blocks), into=dh0, **MM_LONG_K)
    grad_x, g_norm0 = _rmsnorm_bwd(x, ev_norm_w, dh0, dx1_b, "norm0_bwd", F32)

    g_qw = dqw[:, 0, :].reshape(Q_HEADS, HEAD_DIM).sum(axis=0)
    g_kw = dkw[0].reshape(KV_HEADS, HEAD_DIM).sum(axis=0)
    g_sinks = dsink[:, :, 0].reshape(Q_HEADS)
    small = dict(ev_norm=g_norm0[0], od_norm=g_norm1[0], conv=g_conv[:3], qw=g_qw, kw=g_kw, sinks=g_sinks)
    return sq[0, 0], grad_x, small


LANE_TILE = 128


def _pack_small_grads(small, sq):
    pad = lambda v, n: jnp.pad(v, (0, n - v.shape[0]))
    tail = jnp.concatenate([pad(small["qw"], LANE_TILE), pad(small["kw"], LANE_TILE), small["sinks"]])
    rows = [small["ev_norm"], small["od_norm"]] + [small["conv"][t] for t in range(3)] + [tail, sq.reshape(1)]
    rows += [jnp.zeros((1,), F32)] * (8 - len(rows))
    return jnp.stack([pad(r, D_MODEL) for r in rows])


def _adamw_small(tot, chip, params):
    n = len(params)

    def body(chip_ref, tot_ref, *refs):
        ins, outs = refs[:3 * n], refs[3 * n:]
        c = chip_ref[0]

        def own(rows, width):
            blocks = [tot_ref[rows, k * width:(k + 1) * width] for k in range(N_CHIPS)]
            g = blocks[-1]
            for k in reversed(range(N_CHIPS - 1)):
                g = jnp.where(c == k, blocks[k], g)
            return g

        grads = [tot_ref[0:1, :], own(slice(2, 5), CONV_WIDTH // N_CHIPS), own(slice(1, 2), D_MODEL // N_CHIPS),
                 tot_ref[5:6, 0:HEAD_DIM], tot_ref[5:6, LANE_TILE:LANE_TILE + HEAD_DIM],
                 tot_ref[5:6, 2 * LANE_TILE:2 * LANE_TILE + Q_HEADS]]
        for i, g in enumerate(grads):
            w_ref, m_ref, v_ref = ins[3 * i:3 * i + 3]
            g_out, d_out, nm_out, nv_out = outs[4 * i:4 * i + 4]
            at = (0,) if len(w_ref.shape) == 3 else ()
            delta, nm, nv = _adamw_math(w_ref[at] if at else w_ref[...], g, m_ref[at] if at else m_ref[...],
                                        v_ref[at] if at else v_ref[...])
            for ref, val in ((g_out, g), (d_out, delta), (nm_out, nm), (nv_out, nv)):
                if at:
                    ref[0] = val
                else:
                    ref[...] = val

    vm = pl.BlockSpec(memory_space=pltpu.VMEM)
    flat = [a for p in params for a in p]
    outs = pl.pallas_call(
        body, name="adamw_small", in_specs=[pl.BlockSpec(memory_space=pltpu.SMEM), vm] + [vm] * len(flat),
        out_specs=[vm] * (4 * n), out_shape=[jax.ShapeDtypeStruct(p[0].shape, F32) for p in params for _ in range(4)],
    )(chip, tot, *flat)
    return [tuple(outs[4 * i:4 * i + 4]) for i in range(n)]


class _ReduceScatter:
    def __init__(self, place):
        self.place = place
        self.exchanging = {}
        self.started = []
        self.sharing = {}

    def begin(self, tag, grads):
        pieces = [g[None] if kind == "col" else g.reshape(N_CHIPS, g.shape[0] // N_CHIPS, g.shape[1]) for _, g, kind in grads]
        landing = [lax.empty((p.shape[0], p.shape[1] // 2, p.shape[2]), p.dtype) for p in pieces]
        started = _start_copies("exchange_" + tag, pieces + landing, _exchange_plan(len(pieces)), len(pieces))
        self.exchanging[tag] = ([nm for nm, _, _ in grads], started)
        return started[3]

    def send(self, tag, after):
        group, (send_sem, recv_sem, arrays, _) = self.exchanging[tag]
        n = len(group)
        arrays = _wait_copies("exchange_" + tag + "_wait", send_sem, recv_sem, arrays, _exchange_plan(n), after)
        parts = [_pair_sum(g, t, self.place[:1], "pair_sum_" + nm) for g, t, nm in zip(arrays[:n], arrays[n:], group)]
        started = _scatter_start("scatter_" + tag, parts)
        token = self.sum_up(started[3])
        self.started.append((tag, group, started))
        return started[3] if token is None else token

    def sum_up(self, after):
        token = None
        for tag, group, started in self.started:
            if tag not in self.sharing:
                parts, stacks = _scatter_finish("scatter_" + tag, started, after)
                sums = [_sum_chips(p, s, self.place, "chip_sum_" + nm) for p, s, nm in zip(parts, stacks, group)]
                self.sharing[tag] = (group, _start_copies("share_" + tag, sums, _own_half_plan, len(sums)))
                token = self.sharing[tag][1][3]
        return token

    def result(self, tag, after):
        group, (send_sem, recv_sem, sums, _) = self.sharing[tag]
        return dict(zip(group, _wait_copies("share_" + tag + "_wait", send_sem, recv_sem, sums, _own_half_plan, after)))


def kernel(x, ev_norm_w, ev_w_in, ev_conv_w, ev_w_out, od_norm_w, od_w_in, od_q_norm_w, od_k_norm_w, od_sinks, od_w_out, loss_target, m_ev_norm_w, m_ev_w_in, m_ev_conv_w, m_ev_w_out, m_od_norm_w, m_od_w_in, m_od_q_norm_w, m_od_k_norm_w, m_od_sinks, m_od_w_out, v_ev_norm_w, v_ev_w_in, v_ev_conv_w, v_ev_w_out, v_od_norm_w, v_od_w_in, v_od_q_norm_w, v_od_k_norm_w, v_od_sinks, v_od_w_out):
    my_chip = 2 * lax.axis_index("x") + lax.axis_index("y")
    place = jnp.stack([lax.axis_index("c"), my_chip]).astype(jnp.int32)
    shard_w = D_MODEL // N_CHIPS
    conv_shard = CONV_WIDTH // N_CHIPS

    small_in = jnp.zeros((8, shard_w), F32)
    small_in = small_in.at[0].set(od_norm_w[0]).at[1:4, :conv_shard].set(ev_conv_w[0])
    small_in = lax.dynamic_update_slice(jnp.zeros((N_CHIPS, 8, shard_w), F32), small_in[None], (my_chip, 0, 0))
    chip = place[1:]
    first_kinds, second_kinds, late_kinds = (("col", 0, 2), "whole"), (("col", 1, 2),), ("row", "col", "row")
    w_in0_own_place, w_in0_shard = _cast_into_gathered(ev_w_in[0], "col", chip, "cast_w_in0", keep_shard=True)
    first = _gather_start("gather_first", [w_in0_own_place, small_in], first_kinds)
    second = _gather_start("gather_second", [first[2][0]], second_kinds)
    late_own = [_cast_into_gathered(ev_w_out[0], "row", chip, "cast_w_out0"),
                _cast_into_gathered(od_w_in[0], "col", chip, "cast_w_in1"),
                _cast_into_gathered(od_w_out[0], "row", chip, "cast_w_out1")]
    late, w_in0_so_far = [], []

    def weights_first(part, after):
        if part == 0:
            passed_on = _gather_arrive("gather_first", first, first_kinds, after, arrays=[second[2][0], first[2][1]])
            w_in0, small_all = _gather_done("gather_first", passed_on, first_kinds, [passed_on[3]] + late_own)
            w_in0_so_far.append(w_in0)
            late.append(_gather_start("gather_late", late_own, late_kinds, after=small_all))
            od_norm_full = small_all[:, 0, :].reshape(1, D_MODEL)
            conv_full = jnp.transpose(small_all[:, 1:4, :conv_shard], (1, 0, 2)).reshape(3, CONV_WIDTH)
            return w_in0, (conv_full, od_norm_full), late[0][3]
        passed_on = _gather_arrive("gather_second", second, second_kinds, after, arrays=w_in0_so_far)
        (w_in0,) = _gather_done("gather_second", passed_on, second_kinds, passed_on[3])
        return w_in0, None, None

    def late_arrived(after):
        late.append(_gather_arrive("gather_late", late[0], late_kinds, after))
        return late[1][3]

    def weights_late(after):
        return _gather_done("gather_late", late[1], late_kinds, after)

    reduce_scatter = _ReduceScatter(place)
    sq, grad_x, small = _local_step(x[0], loss_target[0], ev_norm_w, od_q_norm_w, od_k_norm_w, od_sinks,
                                    (w_in0_shard, chip, second[3]), weights_first, late_arrived, weights_late,
                                    reduce_scatter)

    token = reduce_scatter.sum_up(grad_x)
    upd = {}
    upper = reduce_scatter.result("upper", token)
    upd["od_w_in"] = _adamw(od_w_in[0], upper["od_w_in"], m_od_w_in[0], v_od_w_in[0], "adamw_od_w_in")
    upd["od_w_out"] = _adamw(od_w_out[0], upper["od_w_out"], m_od_w_out[0], v_od_w_out[0], "adamw_od_w_out")
    upd["ev_w_out"] = _adamw(ev_w_out[0], upper["ev_w_out"], m_ev_w_out[0], v_ev_w_out[0], "adamw_ev_w_out")
    g_ev_w_in = reduce_scatter.result("in0", [upd[nm][1] for nm in ("od_w_in", "od_w_out", "ev_w_out")])["ev_w_in"]
    upd["ev_w_in"] = _adamw(ev_w_in[0], g_ev_w_in, m_ev_w_in[0], v_ev_w_in[0], "adamw_ev_w_in")
    tot = _allreduce_small(_pack_small_grads(small, sq))
    loss = 0.5 * tot[6, 0] / D_MODEL

    smalls = (("ev_norm_w", ev_norm_w, m_ev_norm_w, v_ev_norm_w), ("ev_conv_w", ev_conv_w, m_ev_conv_w, v_ev_conv_w),
              ("od_norm_w", od_norm_w, m_od_norm_w, v_od_norm_w), ("od_q_norm_w", od_q_norm_w, m_od_q_norm_w, v_od_q_norm_w),
              ("od_k_norm_w", od_k_norm_w, m_od_k_norm_w, v_od_k_norm_w), ("od_sinks", od_sinks, m_od_sinks, v_od_sinks))
    for (nm, _, _, _), result in zip(smalls, _adamw_small(tot, chip, [p[1:] for p in smalls])):
        upd[nm] = result
    for nm in ("ev_w_in", "ev_w_out", "od_w_in", "od_w_out"):
        upd[nm] = tuple(u[None] for u in upd[nm])
    order = ("ev_norm_w", "ev_w_in", "ev_conv_w", "ev_w_out", "od_norm_w", "od_w_in", "od_q_norm_w", "od_k_norm_w", "od_sinks", "od_w_out")
    return (loss, grad_x[None], *[upd[nm][0] for nm in order], *[upd[nm][1] for nm in order],
            *[upd[nm][2] for nm in order], *[upd[nm][3] for nm in order])
```

```python
import math

import jax
import jax.numpy as jnp
import numpy as np
from jax import lax
from jax.experimental import pallas as pl
from jax.experimental.pallas import tpu as pltpu

F32 = jnp.float32
BF16 = jnp.bfloat16

D_MODEL = 2048
RET_HEADS = 4
RET_DIM = 256
RET_WIDTH = 1024
CONV_WIDTH = 1024
EVEN_IN = 8192
Q_HEADS = 32
HEAD_DIM = 64
KV_HEADS = 4
KV_WIDTH = 256
ATTN_WIDTH = 2048
ODD_IN = 4608
BLK = 128
ROPE_THETA = 10000.0
EPS = 1e-6
ADAM_LR = 0.001
ADAM_B1 = 0.9
ADAM_B2 = 0.999
ADAM_EPS = 1e-08
ADAM_WD = 0.01
ADAM_STEP = 10
N_CHIPS = 4
N_DEV = 8
VMEM_LIMIT_BYTES = 56 * 1024 * 1024
MESH = pl.DeviceIdType.MESH
ANY = pl.BlockSpec(memory_space=pl.ANY)


def _params(*sem):
    return pltpu.CompilerParams(dimension_semantics=sem, vmem_limit_bytes=VMEM_LIMIT_BYTES)


def _dot(a, b):
    return jnp.dot(a, b, preferred_element_type=F32)


def _dot_nt(a, b):
    return lax.dot_general(a, b, (((1,), (1,)), ((), ())), preferred_element_type=F32)


def _dot_tn(a, b):
    return lax.dot_general(a, b, (((0,), (0,)), ((), ())), preferred_element_type=F32)


def _sigmoid(x):
    return 1.0 / (1.0 + jnp.exp(-x))


def _mm(a, b, *, mode, tm, tn, tk, out_dtype, name, add=None, after=None, m_blocks=None, into=None):
    if mode == "nn":
        (m, k), n = a.shape, b.shape[1]
    elif mode == "nt":
        (m, k), n = a.shape, b.shape[0]
    else:
        (k, m), n = a.shape, b.shape[1]
    tm, tn, tk = min(tm, m), min(tn, n), min(tk, k)
    assert m % tm == 0 and n % tn == 0 and k % tk == 0, (name, m, n, k)
    nk = k // tk
    first_m, count_m = (0, m // tm) if m_blocks is None else m_blocks
    dot = {"nn": _dot, "nt": _dot_nt, "tn": _dot_tn}[mode]
    a_spec = (pl.BlockSpec((tk, tm), lambda i, j, kk: (kk, i + first_m)) if mode == "tn"
              else pl.BlockSpec((tm, tk), lambda i, j, kk: (i + first_m, kk)))
    b_spec = (pl.BlockSpec((tn, tk), lambda i, j, kk: (j, kk)) if mode == "nt"
              else pl.BlockSpec((tk, tn), lambda i, j, kk: (kk, j)))
    o_spec = pl.BlockSpec((tm, tn), lambda i, j, kk: (i + first_m, j))
    has_add = add is not None

    def body(*refs):
        a_ref, b_ref = refs[0], refs[1]
        add_ref = refs[2] if has_add else None
        o_ref, acc_ref = refs[-2], refs[-1]
        p = dot(a_ref[...], b_ref[...])

        def finish(total):
            if has_add:
                total = total + add_ref[...].astype(F32)
            o_ref[...] = total.astype(out_dtype)

        if nk == 1:
            finish(p)
        else:
            kk = pl.program_id(2)

            @pl.when(kk == 0)
            def _():
                acc_ref[...] = p

            @pl.when(jnp.logical_and(kk > 0, kk < nk - 1))
            def _():
                acc_ref[...] += p

            @pl.when(kk == nk - 1)
            def _():
                finish(acc_ref[...] + p)

    extra = [arr for arr in (after, into) if arr is not None]
    in_specs = [a_spec, b_spec] + ([o_spec] if has_add else []) + [ANY] * len(extra)
    args = (a, b) + ((add,) if has_add else ()) + tuple(extra)
    return pl.pallas_call(
        body, name=name, grid=(count_m, n // tn, nk), in_specs=in_specs, out_specs=o_spec,
        out_shape=jax.ShapeDtypeStruct((m, n), out_dtype),
        input_output_aliases={len(args) - 1: 0} if into is not None else {},
        scratch_shapes=[pltpu.VMEM((tm, tn) if nk > 1 else (8, 128), F32)],
        compiler_params=_params("parallel", "parallel", "arbitrary"),
    )(*args)


def _mm_shifted(a, b, shift, *, b_shifted, first, count, total, tm, tn, out_dtype, name, stride=1, into=None, after=None):
    m, k = a.shape
    tm = min(tm, m)
    assert m % tm == 0
    col = lambda j, shift_ref: (shift_ref[0] + first + stride * j) % total
    extra = [arr for arr in (into, after) if arr is not None]

    def body(shift_ref, a_ref, b_ref, *rest):
        del shift_ref
        rest[-1][...] = _dot(a_ref[...], b_ref[...]).astype(out_dtype)

    return pl.pallas_call(
        body, name=name, out_shape=jax.ShapeDtypeStruct((m, total * tn), out_dtype),
        grid_spec=pltpu.PrefetchScalarGridSpec(
            num_scalar_prefetch=1, grid=(m // tm, count),
            in_specs=[pl.BlockSpec((tm, k), lambda i, j, s: (i, 0)),
                      pl.BlockSpec((k, tn), (lambda i, j, s: (0, col(j, s))) if b_shifted else (lambda i, j, s: (0, j)))]
            + [ANY] * len(extra),
            out_specs=pl.BlockSpec((tm, tn), lambda i, j, s: (i, col(j, s)))),
        input_output_aliases={3: 0} if into is not None else {},
        compiler_params=_params("parallel", "arbitrary"))(shift, a, b, *extra)


def _mm_rows(a, b, rows_in, vecs_in, out_shapes, epilogue, *, tm, name):
    m, k = a.shape
    n = b.shape[1]
    assert m % tm == 0
    row = pl.BlockSpec((tm, n), lambda i: (i, 0))

    def body(a_ref, b_ref, *rest):
        epilogue(_dot(a_ref[...], b_ref[...]), *rest)

    out_specs = [row if tuple(s.shape) == (m, n) else pl.BlockSpec(s.shape, lambda i: (0, 0)) for s in out_shapes]
    return pl.pallas_call(
        body, name=name, grid=(m // tm,),
        in_specs=[pl.BlockSpec((tm, k), lambda i: (i, 0)), pl.BlockSpec((k, n), lambda i: (0, 0))] + [row] * len(rows_in)
        + [pl.BlockSpec((1, n), lambda i: (0, 0))] * len(vecs_in),
        out_specs=out_specs, out_shape=out_shapes, compiler_params=_params("arbitrary"),
    )(a, b, *rows_in, *vecs_in)


def _cast_into_gathered(w, kind, chip, name, keep_shard=False):
    r, c = w.shape
    tr = min(r, 512)
    per = r // tr

    def body(chip_ref, w_ref, *outs):
        del chip_ref
        for o_ref in outs:
            o_ref[...] = w_ref[...].astype(BF16)

    if kind == "col":
        shape, out_map = (r, N_CHIPS * c), (lambda i, chip_ref: (i, chip_ref[0]))
    else:
        shape, out_map = (N_CHIPS * r, c), (lambda i, chip_ref: (chip_ref[0] * per + i, 0))
    plain = pl.BlockSpec((tr, c), lambda i, chip_ref: (i, 0))
    out = pl.pallas_call(
        body, name=name,
        out_shape=[jax.ShapeDtypeStruct(shape, BF16)] + ([jax.ShapeDtypeStruct((r, c), BF16)] if keep_shard else []),
        grid_spec=pltpu.PrefetchScalarGridSpec(
            num_scalar_prefetch=1, grid=(per,), in_specs=[plain],
            out_specs=[pl.BlockSpec((tr, c), out_map)] + ([plain] if keep_shard else [])),
        compiler_params=_params("parallel"))(chip, w)
    return out if keep_shard else out[0]


NORM_ROWS = 512


def _rmsnorm(x, w, name, after=None):
    s, d = x.shape
    tr = NORM_ROWS

    def body(x_ref, w_ref, *rest):
        xv = x_ref[...]
        rstd = lax.rsqrt(jnp.mean(xv * xv, axis=-1, keepdims=True) + EPS)
        rest[-1][...] = (xv * rstd * w_ref[...]).astype(BF16)

    return pl.pallas_call(
        body, name=name, grid=(s // tr,),
        in_specs=[pl.BlockSpec((tr, d), lambda i: (i, 0)), pl.BlockSpec((1, d), lambda i: (0, 0))]
        + ([ANY] if after is not None else []),
        out_specs=pl.BlockSpec((tr, d), lambda i: (i, 0)),
        out_shape=jax.ShapeDtypeStruct((s, d), BF16), compiler_params=_params("parallel"),
    )(x, w, *((after,) if after is not None else ()))


def _rmsnorm_bwd(x, w, dh, dres, name, out_dtype):
    s, d = x.shape
    tr = NORM_ROWS

    def body(x_ref, w_ref, dh_ref, dres_ref, dx_ref, dw_ref):
        xv = x_ref[...]
        rstd = lax.rsqrt(jnp.mean(xv * xv, axis=-1, keepdims=True) + EPS)
        nrm = xv * rstd
        dhv = dh_ref[...].astype(F32)
        dn = dhv * w_ref[...]
        dx = dres_ref[...].astype(F32) + rstd * (dn - nrm * jnp.mean(dn * nrm, axis=-1, keepdims=True))
        dx_ref[...] = dx.astype(out_dtype)

        @pl.when(pl.program_id(0) == 0)
        def _():
            dw_ref[...] = jnp.zeros_like(dw_ref)

        dw_ref[...] += jnp.sum(dhv * nrm, axis=0, keepdims=True)

    row = pl.BlockSpec((tr, d), lambda i: (i, 0))
    return pl.pallas_call(
        body, name=name, grid=(s // tr,),
        in_specs=[row, pl.BlockSpec((1, d), lambda i: (0, 0)), row, row],
        out_specs=[row, pl.BlockSpec((8, d), lambda i: (0, 0))],
        out_shape=[jax.ShapeDtypeStruct((s, d), out_dtype), jax.ShapeDtypeStruct((8, d), F32)],
        compiler_params=_params("arbitrary"))(x, w, dh, dres)


def _adamw_math(w, g, m, v):
    nm = ADAM_B1 * m + (1.0 - ADAM_B1) * g
    nv = ADAM_B2 * v + (1.0 - ADAM_B2) * (g * g)
    m_hat = nm / (1.0 - ADAM_B1 ** ADAM_STEP)
    v_hat = nv / (1.0 - ADAM_B2 ** ADAM_STEP)
    return -ADAM_LR * (m_hat / (jnp.sqrt(v_hat) + ADAM_EPS) + ADAM_WD * w), nm, nv


def _adamw(w, g, m, v, name):
    r, c = w.shape
    tr = min(r, 256)
    assert r % tr == 0

    def body(w_ref, g_ref, m_ref, v_ref, g_out, d_ref, nm_ref, nv_ref):
        gv = g_ref[...]
        g_out[...] = gv
        d_ref[...], nm_ref[...], nv_ref[...] = _adamw_math(w_ref[...], gv, m_ref[...], v_ref[...])

    spec = pl.BlockSpec((tr, c), lambda i: (i, 0))
    shp = jax.ShapeDtypeStruct((r, c), F32)
    return pl.pallas_call(body, name=name, grid=(r // tr,), in_specs=[spec] * 4, out_specs=[spec] * 4,
                          out_shape=[shp] * 4, compiler_params=_params("parallel"))(w, g, m, v)


def _rope_tables(s, dim):
    inv = (1.0 / (ROPE_THETA ** (np.arange(0, dim, 2, dtype=np.float64) / dim))).astype(np.float32)
    ang = (np.arange(s, dtype=np.float32)[:, None] * inv[None, :]).astype(np.float64)
    return np.cos(ang).astype(np.float32), np.sin(ang).astype(np.float32)


def _rope_half(x, cos, sin):
    h = x.shape[1] // 2
    x1, x2 = x[:, :h], x[:, h:]
    return jnp.concatenate([x1 * cos - x2 * sin, x2 * cos + x1 * sin], axis=1)


def _unrope_half(dy, cos, sin):
    h = dy.shape[1] // 2
    d1, d2 = dy[:, :h], dy[:, h:]
    return jnp.concatenate([d1 * cos + d2 * sin, d2 * cos - d1 * sin], axis=1)


def _lane(shape):
    return lax.broadcasted_iota(jnp.int32, shape, 1)


def _partner64(x):
    w = x.shape[1]
    first = (_lane(x.shape) % HEAD_DIM) < (HEAD_DIM // 2)
    return jnp.where(first, pltpu.roll(x, w - HEAD_DIM // 2, 1), pltpu.roll(x, HEAD_DIM // 2, 1))


def _tile_lanes(t, reps):
    return t if reps == 1 else jnp.concatenate([t] * reps, axis=1)


MXU_WIDTH = 256


def _group_mean(x, ones_bd, passes=2):
    width, tile = x.shape[1], ones_bd.shape[0]
    if width > tile:
        return jnp.concatenate([_group_mean(x[:, c:c + tile], ones_bd, passes) for c in range(0, width, tile)], axis=1)
    hi = x.astype(BF16)
    if passes == 1:
        return _dot(hi, ones_bd)
    lo = (x - hi.astype(F32)).astype(BF16)
    return _dot(hi, ones_bd) + _dot(lo, ones_bd)


def _block_diag_mean(width):
    idx = jnp.arange(width) // HEAD_DIM
    return jnp.where(idx[:, None] == idx[None, :], 1.0 / HEAD_DIM, 0.0).astype(BF16)


RET_STEP = 4


def _retention_tables():
    h = RET_HEADS
    log_g = jnp.log(1.0 - 2.0 ** (-5.0 - jnp.arange(h, dtype=F32)))
    idx = jnp.arange(BLK, dtype=F32)
    diff = idx[:, None] - idx[None, :]
    intra = jnp.where(diff >= 0, jnp.exp(log_g[:, None, None] * jnp.maximum(diff, 0.0)), 0.0).astype(F32)
    q_dec = jnp.exp(log_g[:, None] * (idx[None, :] + 1.0)).astype(F32)[:, :, None]
    k_dec = jnp.exp(log_g[:, None] * (BLK - 1.0 - idx[None, :])).astype(F32)[:, :, None]
    chunk_dec = jnp.exp(log_g * BLK).astype(F32)[:, None, None]
    return intra, q_dec, k_dec, chunk_dec


def _retention_fwd(proj, cos, sin, tables, after=None):
    s = proj.shape[0]
    nc = s // BLK
    intra, q_dec, k_dec, chunk_dec = tables

    def body(p_ref, cos_ref, sin_ref, in_ref, qd_ref, kd_ref, cd_ref, *rest):
        o_ref, cat_ref, st_ref, state = rest[-4:]

        @pl.when(pl.program_id(0) == 0)
        def _():
            state[...] = jnp.zeros_like(state)

        for c in range(RET_STEP):
            rows = slice(c * BLK, (c + 1) * BLK)
            cosv, sinv = cos_ref[rows, :], sin_ref[rows, :]
            for h in range(RET_HEADS):
                c0 = h * RET_DIM
                q = p_ref[rows, c0:c0 + RET_DIM].astype(F32)
                k = p_ref[rows, RET_WIDTH + c0:RET_WIDTH + c0 + RET_DIM].astype(F32)
                v = p_ref[rows, 2 * RET_WIDTH + c0:2 * RET_WIDTH + c0 + RET_DIM]
                g = p_ref[rows, 3 * RET_WIDTH + c0:3 * RET_WIDTH + c0 + RET_DIM].astype(F32)
                qb = _rope_half(q, cosv, sinv).astype(BF16)
                kr = _rope_half(k, cosv, sinv) * (RET_DIM ** -0.5)
                kb = kr.astype(BF16)
                scores = _dot_nt(qb, kb) * in_ref[h]
                inner = _dot(scores.astype(BF16), v)
                prev = state[h]
                prev_b = prev.astype(BF16)
                st_ref[h, c] = prev_b
                o = inner + _dot(qb, prev_b) * qd_ref[h]
                o_ref[rows, c0:c0 + RET_DIM] = o
                rstd = lax.rsqrt(jnp.mean(o * o, axis=-1, keepdims=True) + EPS)
                cat_ref[rows, c0:c0 + RET_DIM] = (o * rstd * (g * _sigmoid(g))).astype(BF16)
                state[h] = cd_ref[h] * prev + _dot_tn((kr * kd_ref[h]).astype(BF16), v)

    full = lambda shape: pl.BlockSpec(shape, lambda n: (0,) * len(shape))
    step = RET_STEP * BLK
    return pl.pallas_call(
        body, name="retention_fwd", grid=(nc // RET_STEP,),
        in_specs=[pl.BlockSpec((step, 4 * RET_WIDTH), lambda n: (n, 0)),
                  pl.BlockSpec((step, RET_DIM // 2), lambda n: (n, 0)), pl.BlockSpec((step, RET_DIM // 2), lambda n: (n, 0)),
                  full((RET_HEADS, BLK, BLK)), full((RET_HEADS, BLK, 1)), full((RET_HEADS, BLK, 1)), full((RET_HEADS, 1, 1))]
        + ([ANY] if after is not None else []),
        out_specs=[pl.BlockSpec((step, RET_WIDTH), lambda n: (n, 0)), pl.BlockSpec((step, RET_WIDTH), lambda n: (n, 0)),
                   pl.BlockSpec((RET_HEADS, RET_STEP, RET_DIM, RET_DIM), lambda n: (0, n, 0, 0))],
        out_shape=[jax.ShapeDtypeStruct((s, RET_WIDTH), F32), jax.ShapeDtypeStruct((s, D_MODEL), BF16),
                   jax.ShapeDtypeStruct((RET_HEADS, nc, RET_DIM, RET_DIM), BF16)],
        scratch_shapes=[pltpu.VMEM((RET_HEADS, RET_DIM, RET_DIM), F32)],
        compiler_params=_params("arbitrary"),
    )(proj, cos, sin, intra, q_dec, k_dec, chunk_dec, *((after,) if after is not None else ()))


def _retention_bwd(proj, o, states, dcat, cos, sin, tables):
    s = proj.shape[0]
    nc = s // BLK
    intra, q_dec, k_dec, chunk_dec = tables

    def body(p_ref, o_ref, st_ref, dc_ref, cos_ref, sin_ref, in_ref, qd_ref, kd_ref, cd_ref, dp_ref, dstate):
        @pl.when(pl.program_id(0) == 0)
        def _():
            dstate[...] = jnp.zeros_like(dstate)

        for c in reversed(range(RET_STEP)):
            rows = slice(c * BLK, (c + 1) * BLK)
            cosv, sinv = cos_ref[rows, :], sin_ref[rows, :]
            for h in range(RET_HEADS):
                c0 = h * RET_DIM
                q = p_ref[rows, c0:c0 + RET_DIM].astype(F32)
                k = p_ref[rows, RET_WIDTH + c0:RET_WIDTH + c0 + RET_DIM].astype(F32)
                v = p_ref[rows, 2 * RET_WIDTH + c0:2 * RET_WIDTH + c0 + RET_DIM]
                g = p_ref[rows, 3 * RET_WIDTH + c0:3 * RET_WIDTH + c0 + RET_DIM].astype(F32)
                o = o_ref[rows, c0:c0 + RET_DIM]
                dc = dc_ref[rows, c0:c0 + RET_DIM].astype(F32)
                rstd = lax.rsqrt(jnp.mean(o * o, axis=-1, keepdims=True) + EPS)
                nrm = o * rstd
                sg = _sigmoid(g)
                dg = dc * nrm * (sg * (1.0 + g * (1.0 - sg)))
                dn = dc * (g * sg)
                do = rstd * (dn - nrm * jnp.mean(dn * nrm, axis=-1, keepdims=True))
                qb = _rope_half(q, cosv, sinv).astype(BF16)
                kr = _rope_half(k, cosv, sinv) * (RET_DIM ** -0.5)
                kb = kr.astype(BF16)
                mask = in_ref[h]
                qd, kd = qd_ref[h], kd_ref[h]
                prev_b = st_ref[h, c]
                dnext = dstate[h]
                dnext_b = dnext.astype(BF16)
                att = (_dot_nt(qb, kb) * mask).astype(BF16)
                do_b = do.astype(BF16)
                doq = (do * qd).astype(BF16)
                dv = _dot_tn(att, do_b) + _dot((kr * kd).astype(BF16), dnext_b)
                ds = (_dot_nt(do_b, v) * mask).astype(BF16)
                dqr = _dot(ds, kb) + _dot_nt(doq, prev_b)
                dkr = _dot_tn(ds, qb) + _dot_nt(v, dnext_b) * kd
                dstate[h] = cd_ref[h] * dnext + _dot_tn(qb, doq)
                dq = _unrope_half(dqr, cosv, sinv)
                dk = _unrope_half(dkr * (RET_DIM ** -0.5), cosv, sinv)
                dp_ref[rows, c0:c0 + RET_DIM] = dq.astype(BF16)
                dp_ref[rows, RET_WIDTH + c0:RET_WIDTH + c0 + RET_DIM] = dk.astype(BF16)
                dp_ref[rows, 2 * RET_WIDTH + c0:2 * RET_WIDTH + c0 + RET_DIM] = dv.astype(BF16)
                dp_ref[rows, 3 * RET_WIDTH + c0:3 * RET_WIDTH + c0 + RET_DIM] = dg.astype(BF16)

    steps = nc // RET_STEP
    rev = lambda n: steps - 1 - n
    full = lambda shape: pl.BlockSpec(shape, lambda n: (0,) * len(shape))
    step = RET_STEP * BLK
    return pl.pallas_call(
        body, name="retention_bwd", grid=(steps,),
        in_specs=[pl.BlockSpec((step, 4 * RET_WIDTH), lambda n: (rev(n), 0)),
                  pl.BlockSpec((step, RET_WIDTH), lambda n: (rev(n), 0)),
                  pl.BlockSpec((RET_HEADS, RET_STEP, RET_DIM, RET_DIM), lambda n: (0, rev(n), 0, 0)),
                  pl.BlockSpec((step, RET_WIDTH), lambda n: (rev(n), 0)),
                  pl.BlockSpec((step, RET_DIM // 2), lambda n: (rev(n), 0)), pl.BlockSpec((step, RET_DIM // 2), lambda n: (rev(n), 0)),
                  full((RET_HEADS, BLK, BLK)), full((RET_HEADS, BLK, 1)), full((RET_HEADS, BLK, 1)), full((RET_HEADS, 1, 1))],
        out_specs=pl.BlockSpec((step, 4 * RET_WIDTH), lambda n: (rev(n), 0)),
        out_shape=jax.ShapeDtypeStruct((s, EVEN_IN), BF16),
        scratch_shapes=[pltpu.VMEM((RET_HEADS, RET_DIM, RET_DIM), F32)],
        compiler_params=_params("arbitrary"))(proj, o, states, dcat, cos, sin, intra, q_dec, k_dec, chunk_dec)


CONV_ROWS = 256
HALO = 16


def _conv_pieces(p, halo, first):
    gb, gc, u, gv = (p[:, i * CONV_WIDTH:(i + 1) * CONV_WIDTH] for i in range(4))
    cu = gc * u
    hcu = halo[:, CONV_WIDTH:2 * CONV_WIDTH] * halo[:, 2 * CONV_WIDTH:3 * CONV_WIDTH]
    hcu = jnp.where(first, 0.0, hcu)
    r1, r2 = hcu[HALO - 1:HALO], hcu[HALO - 2:HALO - 1]
    row = lax.broadcasted_iota(jnp.int32, cu.shape, 0)
    m1 = jnp.where(row == 0, r1, pltpu.roll(cu, 1, 0))
    m2 = jnp.where(row == 0, r2, jnp.where(row == 1, r1, pltpu.roll(cu, 2, 0)))
    return gb, gc, u, gv, cu, m1, m2


def _conv_fwd(proj, conv_w, cat):
    s = proj.shape[0]
    per = CONV_ROWS // HALO

    def body(p_ref, halo_ref, w_ref, cat_in, cat_ref):
        del cat_in
        first = pl.program_id(0) == 0
        gb, _, _, gv, cu, m1, m2 = _conv_pieces(p_ref[...].astype(F32), halo_ref[...].astype(F32), first)
        conv = w_ref[0:1, :] * m2 + w_ref[1:2, :] * m1 + w_ref[2:3, :] * cu
        cat_ref[...] = (gb * conv * (gv * _sigmoid(gv))).astype(BF16)

    return pl.pallas_call(
        body, name="conv_fwd", grid=(s // CONV_ROWS,),
        in_specs=[pl.BlockSpec((CONV_ROWS, 4 * CONV_WIDTH), lambda i: (i, 1)),
                  pl.BlockSpec((HALO, 4 * CONV_WIDTH), lambda i: (jnp.maximum(i * per - 1, 0), 1)),
                  pl.BlockSpec((3, CONV_WIDTH), lambda i: (0, 0)), ANY],
        out_specs=pl.BlockSpec((CONV_ROWS, CONV_WIDTH), lambda i: (i, 1)),
        out_shape=jax.ShapeDtypeStruct(cat.shape, cat.dtype), input_output_aliases={3: 0},
        compiler_params=_params("parallel"))(proj, proj, conv_w, cat)


def _conv_bwd(proj, dcat, conv_w, dproj):
    s = proj.shape[0]
    per = CONV_ROWS // HALO
    last_halo = s // HALO - 1
    nsteps = s // CONV_ROWS

    def body(p_ref, halo_ref, nxt_ref, dc_ref, dnxt_ref, w_ref, dp_in, dp_ref, dw_ref):
        del dp_in
        i = pl.program_id(0)
        gb, gc, u, gv, cu, m1, m2 = _conv_pieces(p_ref[...].astype(F32), halo_ref[...].astype(F32), i == 0)
        w0, w1, w2 = w_ref[0:1, :], w_ref[1:2, :], w_ref[2:3, :]
        conv = w0 * m2 + w1 * m1 + w2 * cu
        dco = dc_ref[...].astype(F32)
        sg = _sigmoid(gv)
        silu = gv * sg
        dgb = dco * conv * silu
        dco_gb = dco * gb
        dgv = dco_gb * conv * (sg * (1.0 + gv * (1.0 - sg)))
        dconv = dco_gb * silu
        nxt = nxt_ref[...].astype(F32)
        ngv = nxt[:, 3 * CONV_WIDTH:]
        dnext = dnxt_ref[...].astype(F32) * nxt[:, :CONV_WIDTH] * (ngv * _sigmoid(ngv))
        dnext = jnp.where(i == nsteps - 1, 0.0, dnext)
        n1, n2 = dnext[0:1], dnext[1:2]
        row = lax.broadcasted_iota(jnp.int32, dconv.shape, 0)
        p1 = jnp.where(row == CONV_ROWS - 1, n1, pltpu.roll(dconv, CONV_ROWS - 1, 0))
        p2 = jnp.where(row == CONV_ROWS - 1, n2, jnp.where(row == CONV_ROWS - 2, n1, pltpu.roll(dconv, CONV_ROWS - 2, 0)))
        dcu = w2 * dconv + w1 * p1 + w0 * p2
        dp_ref[...] = jnp.concatenate([dgb, dcu * u, dcu * gc, dgv], axis=1).astype(BF16)

        @pl.when(i == 0)
        def _():
            dw_ref[...] = jnp.zeros_like(dw_ref)

        taps = [jnp.sum(dconv * m, axis=0, keepdims=True) for m in (m2, m1, cu)]
        r8 = lax.broadcasted_iota(jnp.int32, dw_ref.shape, 0)
        dw_ref[...] += jnp.where(r8 == 0, taps[0], jnp.where(r8 == 1, taps[1], jnp.where(r8 == 2, taps[2], 0.0)))

    return pl.pallas_call(
        body, name="conv_bwd", grid=(nsteps,),
        in_specs=[pl.BlockSpec((CONV_ROWS, 4 * CONV_WIDTH), lambda i: (i, 1)),
                  pl.BlockSpec((HALO, 4 * CONV_WIDTH), lambda i: (jnp.maximum(i * per - 1, 0), 1)),
                  pl.BlockSpec((HALO, 4 * CONV_WIDTH), lambda i: (jnp.minimum((i + 1) * per, last_halo), 1)),
                  pl.BlockSpec((CONV_ROWS, CONV_WIDTH), lambda i: (i, 1)),
                  pl.BlockSpec((HALO, CONV_WIDTH), lambda i: (jnp.minimum((i + 1) * per, last_halo), 1)),
                  pl.BlockSpec((3, CONV_WIDTH), lambda i: (0, 0)), ANY],
        out_specs=[pl.BlockSpec((CONV_ROWS, 4 * CONV_WIDTH), lambda i: (i, 1)), pl.BlockSpec((8, CONV_WIDTH), lambda i: (0, 0))],
        out_shape=[jax.ShapeDtypeStruct(dproj.shape, dproj.dtype), jax.ShapeDtypeStruct((8, CONV_WIDTH), F32)],
        input_output_aliases={6: 0},
        compiler_params=_params("arbitrary"))(proj, proj, proj, dcat, dcat, conv_w, dproj)


GROUP_HEADS = Q_HEADS // KV_HEADS
GROUP_WIDTH = GROUP_HEADS * HEAD_DIM
SLAB = 2 * HEAD_DIM
KV_COL = ATTN_WIDTH // GROUP_WIDTH
GATE_COL = KV_COL + 1
ATTN_SCALE = HEAD_DIM ** -0.5
KV_ROWS = 1024


def _half_mask(shape, which):
    return (_lane(shape) // HEAD_DIM) == which


def _dup_head(slab, which):
    kept = jnp.where(_half_mask(slab.shape, which), slab, 0.0)
    return kept + pltpu.roll(kept, HEAD_DIM, 1)


def _stack_heads(x):
    parts = []
    for sl in range(GROUP_WIDTH // SLAB):
        slab = x[:, sl * SLAB:(sl + 1) * SLAB]
        parts += [jnp.where(_half_mask(slab.shape, e), slab, 0.0) for e in range(2)]
    return jnp.concatenate(parts, axis=0)


def _unstack_heads(y):
    slabs = []
    for sl in range(GROUP_WIDTH // SLAB):
        a, b = y[(2 * sl) * BLK:(2 * sl + 1) * BLK], y[(2 * sl + 1) * BLK:(2 * sl + 2) * BLK]
        slabs.append(jnp.where(_half_mask(a.shape, 0), a, b))
    return jnp.concatenate(slabs, axis=1)


def _q_prep(q, qw, cosf, sins, ones_bd):
    rstd = lax.rsqrt(_group_mean(q * q, ones_bd) + EPS)
    nrm = q * rstd
    y = nrm * qw
    return nrm, rstd, y * cosf + _partner64(y) * sins


def _band(tri_ref, n):
    own = tri_ref[...] > 0.5
    return own, jnp.where(jnp.logical_and(n == 0, jnp.logical_not(own)), -1e30, 0.0)


def _fold(pair, own):
    return jnp.where(own, pair[:, BLK:], pair[:, :BLK])


def _unfold(folded, own):
    return jnp.concatenate([jnp.where(own, 0.0, folded), jnp.where(own, folded, 0.0)], axis=1)


def _head_probs(raw_scores, sink, own, bias):
    sc = _fold(raw_scores, own) + bias
    m = jnp.maximum(jnp.max(sc, axis=-1, keepdims=True), sink)
    p = jnp.exp(sc - m)
    psink = jnp.exp(sink - m)
    inv = 1.0 / (jnp.sum(p, axis=-1, keepdims=True) + psink)
    return p * inv, psink * inv


def _k_prep(k, kw, cosf, sins, ones_bd):
    rstd = lax.rsqrt(_group_mean(k * k, ones_bd) + EPS)
    nrm = k * rstd
    y = nrm * kw
    return nrm, rstd, y * cosf + _partner64(y) * sins


def _qk_prep(proj, qw, kw, cos, sins, ones_q, ones_kv):
    s = proj.shape[0]
    rows = min(KV_ROWS, s)

    def body(p_ref, qw_ref, kw_ref, cos_ref, sin_ref, oq_ref, ok_ref, o_ref):
        j = pl.program_id(1)

        @pl.when(j < KV_COL)
        def _():
            cosf, sinf = _tile_lanes(cos_ref[...], 4), _tile_lanes(sin_ref[...], 4)
            roped = _q_prep(p_ref[...].astype(F32), qw_ref[...], cosf, sinf, oq_ref[...])[2]
            o_ref[...] = (roped * ATTN_SCALE).astype(BF16)

        @pl.when(j == KV_COL)
        def _():
            cosf, sinf = _tile_lanes(cos_ref[...], 2), _tile_lanes(sin_ref[...], 2)
            kr = _k_prep(p_ref[:, :KV_WIDTH].astype(F32), kw_ref[...], cosf, sinf, ok_ref[...])[2]
            o_ref[...] = jnp.concatenate([kr.astype(BF16), p_ref[:, KV_WIDTH:]], axis=1)

    full = lambda shape: pl.BlockSpec(shape, lambda i, j: (0,) * len(shape))
    tab = pl.BlockSpec((rows, SLAB), lambda i, j: (i, 0))
    blk = pl.BlockSpec((rows, GROUP_WIDTH), lambda i, j: (i, j))
    return pl.pallas_call(
        body, name="qk_prep", grid=(s // rows, KV_COL + 1),
        in_specs=[blk, full((1, GROUP_WIDTH)), full((1, KV_WIDTH)), tab, tab, full(ones_q.shape), full(ones_kv.shape)],
        out_specs=blk, out_shape=jax.ShapeDtypeStruct((s, ATTN_WIDTH + 2 * KV_WIDTH), BF16),
        compiler_params=_params("parallel", "arbitrary"))(proj, qw, kw, cos, sins, ones_q, ones_kv)


def _keys_values(kc_ref, kp_ref, vc_ref, vp_ref, head):
    lanes = slice((head // 2) * SLAB, (head // 2 + 1) * SLAB)
    dup = lambda ref: _dup_head(ref[:, lanes].astype(F32), head % 2)
    return (jnp.concatenate([dup(kp_ref), dup(kc_ref)], axis=0).astype(BF16),
            jnp.concatenate([dup(vp_ref), dup(vc_ref)], axis=0).astype(BF16))


FWD_STEP_HEADS = 4
BWD_STEP_HEADS = 2


def _swa_specs(heads):
    kv_width = heads * HEAD_DIM
    prev = lambda n: jnp.maximum(n - 1, 0)
    kv = lambda col0, row: pl.BlockSpec((BLK, kv_width), lambda gs, n: (row(n), col0 // kv_width + gs))
    cur = lambda n: n
    full = lambda shape: pl.BlockSpec(shape, lambda gs, n: (0,) * len(shape))
    gate = lambda t: pl.BlockSpec((BLK, GROUP_WIDTH), lambda gs, n: (n, GATE_COL + heads * gs + t))
    return dict(
        sinks=pl.BlockSpec(memory_space=pltpu.SMEM), gates=[gate(t) for t in range(heads)],
        kc=kv(ATTN_WIDTH, cur), kp=kv(ATTN_WIDTH, prev), vc=kv(ATTN_WIDTH + KV_WIDTH, cur), vp=kv(ATTN_WIDTH + KV_WIDTH, prev),
        tri=full((BLK, BLK)), step=pl.BlockSpec((BLK, heads * GROUP_WIDTH), lambda gs, n: (n, gs)))


def _lower_triangle():
    return jnp.tril(jnp.ones((BLK, BLK), F32))


def _swa_fwd(qk, proj, sinks):
    s = proj.shape[0]
    nb = s // BLK
    heads = FWD_STEP_HEADS
    sp = _swa_specs(heads)

    def body(sink_ref, q_ref, kc_ref, kp_ref, vc_ref, vp_ref, *rest):
        gate_refs, (tri_ref, ag_ref, o_ref) = rest[:heads], rest[heads:]
        gs, n = pl.program_id(0), pl.program_id(1)
        own, bias = _band(tri_ref, n)
        for t, gate_ref in enumerate(gate_refs):
            cols = slice(t * GROUP_WIDTH, (t + 1) * GROUP_WIDTH)
            first_head = (heads * gs + t) * GROUP_HEADS
            kcat, vcat = _keys_values(kc_ref, kp_ref, vc_ref, vp_ref, t)
            scores = _dot_nt(_stack_heads(q_ref[:, cols]), kcat)
            probs = []
            for j in range(GROUP_HEADS):
                p, _ = _head_probs(scores[j * BLK:(j + 1) * BLK], sink_ref[first_head + j], own, bias)
                probs.append(_unfold(p, own).astype(BF16))
            o = _unstack_heads(_dot(jnp.concatenate(probs, axis=0), vcat))
            gate = gate_ref[...].astype(F32)
            o_ref[:, cols] = o.astype(BF16)
            ag_ref[:, cols] = (o * (gate * _sigmoid(gate))).astype(BF16)

    shp = jax.ShapeDtypeStruct((s, ATTN_WIDTH), BF16)
    return pl.pallas_call(
        body, name="swa_fwd", grid=(KV_HEADS // heads, nb),
        in_specs=[sp["sinks"], sp["step"], sp["kc"], sp["kp"], sp["vc"], sp["vp"], *sp["gates"], sp["tri"]],
        out_specs=[sp["step"], sp["step"]], out_shape=[shp, shp],
        compiler_params=_params("parallel", "arbitrary"),
    )(sinks, qk, qk, qk, qk, qk, *[proj] * heads, _lower_triangle())


def _swa_bwd(qk, proj, dag, sinks):
    s = proj.shape[0]
    nb = s // BLK
    heads = BWD_STEP_HEADS
    sp = _swa_specs(heads)

    def body(sink_ref, q_ref, kc_ref, kp_ref, vc_ref, vp_ref, *rest):
        gate_refs = rest[:heads]
        dag_ref, tri_ref, dq_ref, dkc_ref, dkp_ref, dvc_ref, dvp_ref, dsink_ref = rest[heads:]
        gs, n = pl.program_id(0), pl.program_id(1)
        own, bias = _band(tri_ref, n)

        @pl.when(n == 0)
        def _():
            dsink_ref[...] = jnp.zeros_like(dsink_ref)

        for t, gate_ref in enumerate(gate_refs):
            cols = slice(t * GROUP_WIDTH, (t + 1) * GROUP_WIDTH)
            first_head = (heads * gs + t) * GROUP_HEADS
            kcat, vcat = _keys_values(kc_ref, kp_ref, vc_ref, vp_ref, t)
            gate = gate_ref[...].astype(F32)
            do = dag_ref[:, cols].astype(F32) * (gate * _sigmoid(gate))
            q_stack = _stack_heads(q_ref[:, cols])
            do_stack = _stack_heads(do).astype(BF16)
            scores = _dot_nt(q_stack, kcat)
            dprobs = _dot_nt(do_stack, vcat)
            probs, dscores, dsinks = [], [], []
            for j in range(GROUP_HEADS):
                rows = slice(j * BLK, (j + 1) * BLK)
                p, psink = _head_probs(scores[rows], sink_ref[first_head + j], own, bias)
                dp = _fold(dprobs[rows], own)
                delta = jnp.sum(p * dp, axis=-1, keepdims=True)
                probs.append(_unfold(p, own).astype(BF16))
                dscores.append(_unfold(p * (dp - delta), own).astype(BF16))
                dsinks.append(-jnp.sum(psink * delta, axis=0, keepdims=True))
            ds = jnp.concatenate(dscores, axis=0)
            dk = _dot_tn(ds, q_stack)
            dv = _dot_tn(jnp.concatenate(probs, axis=0), do_stack)
            dk = dk + pltpu.roll(dk, HEAD_DIM, 1)
            dv = dv + pltpu.roll(dv, HEAD_DIM, 1)
            dkp_ref[t], dkc_ref[t] = dk[:BLK], dk[BLK:]
            dvp_ref[t], dvc_ref[t] = dv[:BLK], dv[BLK:]
            dq_ref[:, cols] = _unstack_heads(_dot(ds, kcat)).astype(BF16)
            r8 = lax.broadcasted_iota(jnp.int32, (8, SLAB), 0)
            upd = jnp.zeros((8, SLAB), F32)
            for j in range(GROUP_HEADS):
                upd = jnp.where(r8 == j, dsinks[j], upd)
            dsink_ref[t] += upd

    cur_out = pl.BlockSpec((heads, BLK, SLAB), lambda gs, n: (gs, n, 0))
    prev_out = pl.BlockSpec((heads, BLK, SLAB), lambda gs, n: (gs, (n + nb - 1) % nb, 0))
    kv_shape = jax.ShapeDtypeStruct((KV_HEADS, s, SLAB), F32)
    return pl.pallas_call(
        body, name="swa_bwd", grid=(KV_HEADS // heads, nb),
        in_specs=[sp["sinks"], sp["step"], sp["kc"], sp["kp"], sp["vc"], sp["vp"], *sp["gates"], sp["step"], sp["tri"]],
        out_specs=[sp["step"], cur_out, prev_out, cur_out, prev_out,
                   pl.BlockSpec((heads, 8, SLAB), lambda gs, n: (gs, 0, 0))],
        out_shape=[jax.ShapeDtypeStruct((s, ATTN_WIDTH), BF16), kv_shape, kv_shape, kv_shape, kv_shape,
                   jax.ShapeDtypeStruct((KV_HEADS, 8, SLAB), F32)],
        compiler_params=_params("parallel", "arbitrary"),
    )(sinks, qk, qk, qk, qk, qk, *[proj] * heads, dag, _lower_triangle())


def _swa_bwd_finish(proj, dqr, o, dag, dkc, dkp, dvc, dvp, qw, kw, cos, sins, ones_q, ones_kv):
    s = proj.shape[0]
    rows = min(KV_ROWS, s)
    n_q = KV_COL
    q_of = lambda j: jnp.clip(j - 1, 0, n_q - 1)
    gate_of = lambda j: jnp.clip(j - 1 - n_q, 0, n_q - 1)

    def body(kv_ref, p_ref, dqr_ref, o_ref, dag_ref, dkc_ref, dkp_ref, dvc_ref, dvp_ref, qw_ref, kw_ref, cos_ref, sin_ref,
             oq_ref, ok_ref, dp_ref, dqw_ref, dkw_ref):
        j, i = pl.program_id(0), pl.program_id(1)

        @pl.when(j == 0)
        def _():
            @pl.when(i == 0)
            def _():
                dkw_ref[...] = jnp.zeros_like(dkw_ref)

            def assemble(cur_ref, prv_ref):
                tot = [cur_ref[h] + prv_ref[h] for h in range(KV_HEADS)]
                first = _half_mask(tot[0].shape, 0)
                return jnp.concatenate([jnp.where(first, tot[0], tot[1]), jnp.where(first, tot[2], tot[3])], axis=1)

            dkr = assemble(dkc_ref, dkp_ref)
            dv = assemble(dvc_ref, dvp_ref)
            cosf, sinf = _tile_lanes(cos_ref[...], 2), _tile_lanes(sin_ref[...], 2)
            nrm, rstd, _ = _k_prep(kv_ref[:, :KV_WIDTH].astype(F32), kw_ref[...], cosf, sinf, ok_ref[...])
            dy = dkr * cosf + _partner64(dkr * sinf)
            dn = dy * kw_ref[...]
            dk = rstd * (dn - nrm * _group_mean(dn * nrm, ok_ref[...], passes=1))
            dp_ref[...] = jnp.concatenate([dk, dv], axis=1).astype(BF16)
            dkw_ref[...] += jnp.sum(dy * nrm, axis=0, keepdims=True)

        @pl.when(jnp.logical_and(j >= 1, j <= n_q))
        def _():
            @pl.when(i == 0)
            def _():
                dqw_ref[...] = jnp.zeros_like(dqw_ref)

            cosf, sinf = _tile_lanes(cos_ref[...], 4), _tile_lanes(sin_ref[...], 4)
            nrm, rstd, _ = _q_prep(p_ref[...].astype(F32), qw_ref[...], cosf, sinf, oq_ref[...])
            dq = dqr_ref[...].astype(F32) * ATTN_SCALE
            dy = dq * cosf + _partner64(dq * sinf)
            dn = dy * qw_ref[...]
            dp_ref[...] = (rstd * (dn - nrm * _group_mean(dn * nrm, oq_ref[...], passes=1))).astype(BF16)
            dqw_ref[0] += jnp.sum(dy * nrm, axis=0, keepdims=True)

        @pl.when(j > n_q)
        def _():
            gate = p_ref[...].astype(F32)
            sg = _sigmoid(gate)
            dp_ref[...] = (dag_ref[...].astype(F32) * o_ref[...].astype(F32) * (sg * (1.0 + gate * (1.0 - sg)))).astype(BF16)

    first_pass = lambda j, i: jnp.where(j == 0, i, 0)
    acc = pl.BlockSpec((KV_HEADS, rows, SLAB), lambda j, i: (0, first_pass(j, i), 0))
    full = lambda shape: pl.BlockSpec(shape, lambda j, i: (0,) * len(shape))
    tab = pl.BlockSpec((rows, SLAB), lambda j, i: (i, 0))
    out_col = lambda j: jnp.where(j == 0, KV_COL, jnp.where(j <= n_q, j - 1, j))
    return pl.pallas_call(
        body, name="swa_bwd_finish", grid=(2 * n_q + 1, s // rows),
        in_specs=[pl.BlockSpec((rows, GROUP_WIDTH), lambda j, i: (first_pass(j, i), KV_COL)),
                  pl.BlockSpec((rows, GROUP_WIDTH), lambda j, i: (jnp.where(j == 0, 0, i), jnp.where(j <= n_q, q_of(j), j))),
                  pl.BlockSpec((rows, GROUP_WIDTH), lambda j, i: (jnp.where(jnp.logical_and(j >= 1, j <= n_q), i, 0), q_of(j))),
                  pl.BlockSpec((rows, GROUP_WIDTH), lambda j, i: (jnp.where(j > n_q, i, 0), gate_of(j))),
                  pl.BlockSpec((rows, GROUP_WIDTH), lambda j, i: (jnp.where(j > n_q, i, 0), gate_of(j))),
                  acc, acc, acc, acc, full((1, GROUP_WIDTH)), full((1, KV_WIDTH)), tab, tab,
                  full(ones_q.shape), full(ones_kv.shape)],
        out_specs=[pl.BlockSpec((rows, GROUP_WIDTH), lambda j, i: (i, out_col(j))),
                   pl.BlockSpec((1, 8, GROUP_WIDTH), lambda j, i: (q_of(j), 0, 0)), pl.BlockSpec((8, KV_WIDTH), lambda j, i: (0, 0))],
        out_shape=[jax.ShapeDtypeStruct((s, ODD_IN), BF16), jax.ShapeDtypeStruct((n_q, 8, GROUP_WIDTH), F32),
                   jax.ShapeDtypeStruct((8, KV_WIDTH), F32)],
        compiler_params=_params("arbitrary", "arbitrary"),
    )(proj, proj, dqr, o, dag, dkc, dkp, dvc, dvp, qw, kw, cos, sins, ones_q, ones_kv)


def _place():
    x, y, c = lax.axis_index("x"), lax.axis_index("y"), lax.axis_index("c")
    return x, y, c


OTHER_CHIPS = ((1, 0), (0, 1), (1, 1))


def _half_rows(ref, half, rows):
    return ref.at[pl.ds(pl.multiple_of(half * (rows // 2), 8), rows // 2)]


DMA_CHUNK_BYTES = 1 << 20
BF16_TILE_ROWS = 16


def _n_chunks(ref):
    rows = ref.shape[-2]
    nbytes = math.prod(ref.shape) * jnp.dtype(ref.dtype).itemsize
    n = 1
    while 2 * n * DMA_CHUNK_BYTES <= nbytes and rows % (2 * n * BF16_TILE_ROWS) == 0:
        n *= 2
    return n


def _row_chunk(ref, k, n):
    rows = ref.shape[-2] // n
    return ref.at[pl.ds(k * rows, rows)] if len(ref.shape) == 2 else ref.at[:, pl.ds(k * rows, rows)]


def _push(src, dst, send_sem, recv_sem, device_id):
    n = _n_chunks(src)
    for k in range(n):
        pltpu.make_async_remote_copy(src_ref=_row_chunk(src, k, n), dst_ref=_row_chunk(dst, k, n), send_sem=send_sem,
                                     recv_sem=recv_sem, device_id=device_id, device_id_type=MESH).start()
    return pltpu.make_async_remote_copy(src_ref=src, dst_ref=dst, send_sem=send_sem, recv_sem=recv_sem,
                                        device_id=device_id, device_id_type=MESH)


HBM = pl.BlockSpec(memory_space=pltpu.HBM)
SEM = pl.BlockSpec(memory_space=pltpu.SEMAPHORE)
SPLIT_COPY_EFFECT = pltpu.SideEffectType.DATAFLOW_SIDE_EFFECTING


def _in_hbm(a):
    return pltpu.with_memory_space_constraint(a, pltpu.HBM)


def _start_copies(name, arrays, plan, n_copies, after=None):
    n = len(arrays)

    def body(*refs):
        send_sem, recv_sem = refs[-n - 3], refs[-n - 2]
        for k, (src, dst, peer) in enumerate(plan(refs[:n])):
            _push(src, dst, send_sem.at[k], recv_sem.at[k], peer)
        refs[-1][...] = jnp.zeros_like(refs[-1])

    dma = pltpu.SemaphoreType.DMA((n_copies,))
    outs = pl.pallas_call(
        body, name=name,
        out_shape=(dma, dma, *[pltpu.HBM(a.shape, a.dtype) for a in arrays], jax.ShapeDtypeStruct((8, 128), F32)),
        in_specs=[HBM] * n + ([ANY] if after is not None else []),
        out_specs=(SEM, SEM, *[HBM] * n, pl.BlockSpec(memory_space=pltpu.VMEM)),
        input_output_aliases={i: i + 2 for i in range(n)},
        compiler_params=pltpu.CompilerParams(has_side_effects=SPLIT_COPY_EFFECT),
    )(*[_in_hbm(a) for a in arrays], *((after,) if after is not None else ()))
    return outs[0], outs[1], list(outs[2:2 + n]), outs[-1]


def _wait_copies(name, send_sem, recv_sem, arrays, plan, after):
    n = len(arrays)
    after = list(after) if isinstance(after, (list, tuple)) else [after]

    def body(*refs):
        send_ref, recv_ref = refs[n], refs[n + 1]
        for k, (src, dst, peer) in enumerate(plan(refs[:n])):
            cp = pltpu.make_async_remote_copy(src_ref=src, dst_ref=dst, send_sem=send_ref.at[k], recv_sem=recv_ref.at[k],
                                              device_id=peer, device_id_type=MESH)
            cp.wait_send()
            cp.wait_recv()

    return list(pl.pallas_call(
        body, name=name, out_shape=tuple(pltpu.HBM(a.shape, a.dtype) for a in arrays),
        in_specs=[HBM] * n + [SEM, SEM] + [ANY] * len(after), out_specs=tuple([HBM] * n),
        input_output_aliases={i: i for i in range(n)},
        compiler_params=pltpu.CompilerParams(has_side_effects=SPLIT_COPY_EFFECT),
    )(*arrays, send_sem, recv_sem, *after))


def _gather_region(full, kind, chip, half=None):
    if kind == "whole":
        return full.at[chip]
    if kind == "col" or isinstance(kind, tuple):
        part, parts = (0, 1) if kind == "col" else kind[1:]
        rows, width = full.shape[0], full.shape[1] // N_CHIPS
        piece = full.at[:, pl.ds(pl.multiple_of(chip * width + part * (width // parts), LANE_TILE), width // parts)]
    else:
        rows = full.shape[0] // N_CHIPS
        piece = full.at[pl.ds(pl.multiple_of(chip * rows, BF16_TILE_ROWS), rows)]
    return piece if half is None else _half_rows(piece, half, rows)


def _gather_plan(kinds):
    def plan(fulls):
        x, y, c = _place()
        copies = []
        for fx, fy in OTHER_CHIPS:
            for full, kind in zip(fulls, kinds):
                mine = _gather_region(full, kind, 2 * x + y, c)
                copies.append((mine, mine, (x ^ fx, y ^ fy, c)))
        return copies

    return plan


def _gather_start(name, fulls, kinds, after=None):
    return _start_copies(name, list(fulls), _gather_plan(kinds), 3 * len(kinds), after)


def _pass_on_plan(kinds):
    split = [i for i, kind in enumerate(kinds) if kind != "whole"]

    def plan(fulls):
        x, y, c = _place()
        copies = []
        for fx, fy in OTHER_CHIPS:
            chip = 2 * (x ^ fx) + (y ^ fy)
            for i in split:
                landed = _gather_region(fulls[i], kinds[i], chip, c)
                copies.append((landed, landed, (x, y, 1 - c)))
        return copies

    return plan, 3 * len(split)


def _gather_arrive(name, started, kinds, after, arrays=None):
    send_sem, recv_sem, fulls, _ = started
    fulls = _wait_copies(name + "_wait", send_sem, recv_sem, fulls if arrays is None else arrays, _gather_plan(kinds), after)
    plan, count = _pass_on_plan(kinds)
    return _start_copies(name + "_pass_on", fulls, plan, count)


def _gather_done(name, passed_on, kinds, after):
    send_sem, recv_sem, fulls, _ = passed_on
    return _wait_copies(name + "_pass_on_wait", send_sem, recv_sem, fulls, _pass_on_plan(kinds)[0], after)


def _allreduce_small(v):
    def body(v_ref, out_ref, buf, send_sems, recv_sems):
        x, y, c = _place()
        me = 4 * x + 2 * y + c
        buf[me] = v_ref[...]
        copies = []
        for r in range(1, N_DEV):
            peer = (x ^ (r >> 2), y ^ ((r >> 1) & 1), c ^ (r & 1))
            cp = pltpu.make_async_remote_copy(src_ref=v_ref, dst_ref=buf.at[me], send_sem=send_sems.at[r - 1],
                                              recv_sem=recv_sems.at[r - 1], device_id=peer, device_id_type=MESH)
            cp.start()
            copies.append(cp)
        for cp in copies:
            cp.wait_recv()
        for cp in copies:
            cp.wait_send()
        total = buf[0]
        for d in range(1, N_DEV):
            total = total + buf[d]
        out_ref[...] = total

    vm = pl.BlockSpec(memory_space=pltpu.VMEM)
    return pl.pallas_call(
        body, name="allreduce_small", in_specs=[vm], out_specs=vm, out_shape=jax.ShapeDtypeStruct(v.shape, v.dtype),
        scratch_shapes=[pltpu.VMEM((N_DEV,) + v.shape, v.dtype), pltpu.SemaphoreType.DMA((N_DEV - 1,)),
                        pltpu.SemaphoreType.DMA((N_DEV - 1,))],
        compiler_params=pltpu.CompilerParams(has_side_effects=True),
    )(v)


def _exchange_plan(n):
    def plan(refs):
        x, y, c = _place()
        copies = []
        for g, theirs in zip(refs[:n], refs[n:]):
            half = g.shape[1] // 2
            src = g.at[:, pl.ds(pl.multiple_of((1 - c) * half, BF16_TILE_ROWS), half)]
            copies.append((src, theirs, (x, y, 1 - c)))
        return copies

    return plan


SUM_ROWS = 128


def _pair_sum(g, theirs, core, name):
    pieces, half, cols = theirs.shape
    tr = min(half, SUM_ROWS)
    per = half // tr

    def body(core_ref, g_ref, t_ref, o_ref):
        del core_ref
        o_ref[...] = (g_ref[...].astype(F32) + t_ref[...].astype(F32)).astype(BF16)

    spec = pl.BlockSpec((1, tr, cols), lambda p, i, core_ref: (p, i, 0))
    return pl.pallas_call(
        body, name=name, out_shape=jax.ShapeDtypeStruct(theirs.shape, BF16),
        grid_spec=pltpu.PrefetchScalarGridSpec(
            num_scalar_prefetch=1, grid=(pieces, per),
            in_specs=[pl.BlockSpec((1, tr, cols), lambda p, i, core_ref: (p, core_ref[0] * per + i, 0)), spec],
            out_specs=spec),
        compiler_params=_params("parallel", "parallel"))(core, g, theirs)


def _scatter_plan(n):
    def plan(refs):
        parts, stacks = refs[:n], refs[n:]
        x, y, c = _place()
        copies = []
        for fx, fy in OTHER_CHIPS:
            chip = 2 * (x ^ fx) + (y ^ fy)
            for part, stack in zip(parts, stacks):
                if part.shape[0] == N_CHIPS:
                    piece = part.at[chip]
                else:
                    width = part.shape[2] // N_CHIPS
                    piece = part.at[0].at[:, pl.ds(pl.multiple_of(chip * width, 128), width)]
                copies.append((piece, stack.at[2 * x + y], (x ^ fx, y ^ fy, c)))
        return copies

    return plan


def _scatter_start(name, parts, after=None):
    def landing(a):
        return (N_CHIPS, a.shape[1], a.shape[2] if a.shape[0] == N_CHIPS else a.shape[2] // N_CHIPS)

    stacks = [lax.empty(landing(a), a.dtype) for a in parts]
    return _start_copies(name, list(parts) + stacks, _scatter_plan(len(parts)), 3 * len(parts), after)


def _scatter_finish(name, started, after):
    send_sem, recv_sem, arrays, _ = started
    n = len(arrays) // 2
    arrays = _wait_copies(name + "_wait", send_sem, recv_sem, arrays, _scatter_plan(n), after)
    return arrays[:n], arrays[n:]


def _sum_chips(part, stack, place, name):
    _, r, c = stack.shape
    tr = SUM_ROWS
    per = r // tr

    def body(place_ref, own_ref, a_ref, b_ref, c_ref, o_ref):
        del place_ref
        total = own_ref[0].astype(F32)
        for ref in (a_ref, b_ref, c_ref):
            total = total + ref[0].astype(F32)
        o_ref[...] = total

    if part.shape[0] == N_CHIPS:
        own = pl.BlockSpec((1, tr, c), lambda i, pr: (pr[1], i, 0))
    else:
        own = pl.BlockSpec((1, tr, c), lambda i, pr: (0, i, pr[1]))
    other = lambda flip: pl.BlockSpec((1, tr, c), lambda i, pr: (pr[1] ^ flip, i, 0))
    return pl.pallas_call(
        body, name=name, out_shape=jax.ShapeDtypeStruct((2 * r, c), F32),
        grid_spec=pltpu.PrefetchScalarGridSpec(
            num_scalar_prefetch=1, grid=(per,), in_specs=[own, other(2), other(1), other(3)],
            out_specs=pl.BlockSpec((tr, c), lambda i, pr: (pr[0] * per + i, 0))),
        compiler_params=_params("parallel"))(place, part, stack, stack, stack)


def _own_half_plan(fulls):
    x, y, c = _place()
    return [(mine, mine, (x, y, 1 - c)) for mine in (_half_rows(full, c, full.shape[0]) for full in fulls)]


MM = dict(tm=2048, tn=1024, tk=2048)
MM_LONG_K = dict(tm=1024, tn=1024, tk=4096)


def _local_step(x, target, ev_norm_w, q_norm_w, k_norm_w, sinks, own_first, weights_first, late_arrived, weights_late, grads_out):
    s = x.shape[0]
    cos_r, sin_r = _rope_tables(s, RET_DIM)
    cos_a, sin_a = _rope_tables(s, HEAD_DIM)
    cos_a = np.tile(cos_a, (1, 4))
    sins_a = np.tile(np.concatenate([-sin_a, sin_a], axis=1), (1, 2))
    tables = _retention_tables()
    ones_q, ones_kv = _block_diag_mean(min(GROUP_WIDTH, MXU_WIDTH)), _block_diag_mean(min(KV_WIDTH, MXU_WIDTH))
    qw_g = jnp.tile(q_norm_w, (1, GROUP_WIDTH // HEAD_DIM))
    kw_kv = jnp.tile(k_norm_w, (1, KV_WIDTH // HEAD_DIM))
    sinks1 = sinks.reshape(Q_HEADS)

    own_w_in0, own_block, start_token = own_first
    h0 = _rmsnorm(x, ev_norm_w, "norm0", after=start_token)
    shifted = dict(shift=own_block * (own_w_in0.shape[1] // MM["tn"]), total=EVEN_IN // MM["tn"], tm=MM["tm"], tn=MM["tn"],
                   out_dtype=BF16)
    own_blocks = own_w_in0.shape[1] // MM["tn"]
    proj0 = _mm_shifted(h0, own_w_in0, b_shifted=False, first=0, count=own_blocks, name="proj0_own", **shifted)
    for part in range(own_blocks):
        w_in0, small_weights, token = weights_first(part, proj0)
        if small_weights is not None:
            conv_w, od_norm_w = small_weights
        proj0 = _mm_shifted(h0, w_in0, b_shifted=True, first=own_blocks + part, stride=own_blocks,
                            count=shifted["total"] // own_blocks - 1, name=f"proj0_rest{part}", into=proj0, after=token, **shifted)
    o_ret, cat, states = _retention_fwd(proj0, cos_r, sin_r, tables, after=late_arrived(proj0))
    cat = _conv_fwd(proj0, conv_w, cat)
    w_out0, w_in1, w_out1 = weights_late(cat)

    def residual_and_norm(prod, x_ref, w_ref, x1_ref, h1_ref):
        x1v = x_ref[...] + prod
        x1_ref[...] = x1v
        rstd = lax.rsqrt(jnp.mean(x1v * x1v, axis=-1, keepdims=True) + EPS)
        h1_ref[...] = (x1v * rstd * w_ref[...]).astype(BF16)

    def residual_and_loss(prod, x1_ref, t_ref, dyb_ref, sq_ref):
        diff = (x1_ref[...] + prod) - t_ref[...]
        dyb_ref[...] = (diff * (1.0 / D_MODEL)).astype(BF16)

        @pl.when(pl.program_id(0) == 0)
        def _():
            sq_ref[...] = jnp.zeros_like(sq_ref)

        sq_ref[...] += jnp.sum(jnp.sum(diff * diff, axis=1, keepdims=True), axis=0, keepdims=True)

    act = lambda dt: jax.ShapeDtypeStruct((s, D_MODEL), dt)
    x1, h1 = _mm_rows(cat, w_out0, [x], [od_norm_w], [act(F32), act(BF16)], residual_and_norm, tm=min(s, 512), name="out0")
    proj1 = _mm(h1, w_in1, mode="nn", out_dtype=BF16, name="proj1", tm=2048, tn=1536, tk=2048)
    qk = _qk_prep(proj1, qw_g, kw_kv, cos_a, sins_a, ones_q, ones_kv)
    ag, o_att = _swa_fwd(qk, proj1, sinks1)
    dy_b, sq = _mm_rows(ag, w_out1, [x1, target], [], [act(BF16), jax.ShapeDtypeStruct((8, 128), F32)],
                        residual_and_loss, tm=min(s, 512), name="out1")

    g_w_out1 = _mm(ag, dy_b, mode="tn", out_dtype=BF16, name="g_w_out1", **MM_LONG_K)
    dag = _mm(dy_b, w_out1, mode="nt", out_dtype=BF16, name="d_ag", **MM)
    dqr, dkc, dkp, dvc, dvp, dsink = _swa_bwd(qk, proj1, dag, sinks1)
    dproj1, dqw, dkw = _swa_bwd_finish(proj1, dqr, o_att, dag, dkc, dkp, dvc, dvp, qw_g, kw_kv, cos_a, sins_a, ones_q, ones_kv)
    g_w_in1 = _mm(h1, dproj1, mode="tn", out_dtype=BF16, name="g_w_in1", tm=1024, tn=768, tk=4096)
    dh1 = _mm(dproj1, w_in1, mode="nt", out_dtype=BF16, name="d_h1", tm=1024, tn=1024, tk=ODD_IN)
    dx1_b, g_norm1 = _rmsnorm_bwd(x1, od_norm_w, dh1, dy_b, "norm1_bwd", BF16)

    g_w_out0 = _mm(cat, dx1_b, mode="tn", out_dtype=BF16, name="g_w_out0", **MM_LONG_K)
    token = grads_out.begin("upper", (("od_w_in", g_w_in1, "col"), ("od_w_out", g_w_out1, "row"), ("ev_w_out", g_w_out0, "row")))
    dcat = _mm(dx1_b, w_out0, mode="nt", out_dtype=BF16, name="d_cat", after=token, **MM)
    token = grads_out.send("upper", dcat)
    dproj0 = _retention_bwd(proj0, o_ret, states, dcat, cos_r, sin_r, tables)
    dproj0, g_conv = _conv_bwd(proj0, dcat, conv_w, dproj0)
    g_w_in0 = _mm(h0, dproj0, mode="tn", out_dtype=BF16, name="g_w_in0", after=token, **MM_LONG_K)
    token = grads_out.begin("in0", (("ev_w_in", g_w_in0, "col"),))
    half_blocks = s // (2 * MM_LONG_K["tm"])
    dh0 = _mm(dproj0, w_in0, mode="nt", out_dtype=BF16, name="d_h0_top", after=token, m_blocks=(0, half_blocks), **MM_LONG_K)
    dh0 = _mm(dproj0, w_in0, mode="nt", out_dtype=BF16, name="d_h0_bottom", after=grads_out.send("in0", dh0),
              m_blocks=(half_blocks, half_blocks), into=dh0, **MM_LONG_K)
    grad_x, g_norm0 = _rmsnorm_bwd(x, ev_norm_w, dh0, dx1_b, "norm0_bwd", F32)

    g_qw = dqw[:, 0, :].reshape(Q_HEADS, HEAD_DIM).sum(axis=0)
    g_kw = dkw[0].reshape(KV_HEADS, HEAD_DIM).sum(axis=0)
    g_sinks = dsink[:, :, 0].reshape(Q_HEADS)
    small = dict(ev_norm=g_norm0[0], od_norm=g_norm1[0], conv=g_conv[:3], qw=g_qw, kw=g_kw, sinks=g_sinks)
    return sq[0, 0], grad_x, small


LANE_TILE = 128


def _pack_small_grads(small, sq):
    pad = lambda v, n: jnp.pad(v, (0, n - v.shape[0]))
    tail = jnp.concatenate([pad(small["qw"], LANE_TILE), pad(small["kw"], LANE_TILE), small["sinks"]])
    rows = [small["ev_norm"], small["od_norm"]] + [small["conv"][t] for t in range(3)] + [tail, sq.reshape(1)]
    rows += [jnp.zeros((1,), F32)] * (8 - len(rows))
    return jnp.stack([pad(r, D_MODEL) for r in rows])


def _adamw_small(tot, chip, params):
    n = len(params)

    def body(chip_ref, tot_ref, *refs):
        ins, outs = refs[:3 * n], refs[3 * n:]
        c = chip_ref[0]

        def own(rows, width):
            blocks = [tot_ref[rows, k * width:(k + 1) * width] for k in range(N_CHIPS)]
            g = blocks[-1]
            for k in reversed(range(N_CHIPS - 1)):
                g = jnp.where(c == k, blocks[k], g)
            return g

        grads = [tot_ref[0:1, :], own(slice(2, 5), CONV_WIDTH // N_CHIPS), own(slice(1, 2), D_MODEL // N_CHIPS),
                 tot_ref[5:6, 0:HEAD_DIM], tot_ref[5:6, LANE_TILE:LANE_TILE + HEAD_DIM],
                 tot_ref[5:6, 2 * LANE_TILE:2 * LANE_TILE + Q_HEADS]]
        for i, g in enumerate(grads):
            w_ref, m_ref, v_ref = ins[3 * i:3 * i + 3]
            g_out, d_out, nm_out, nv_out = outs[4 * i:4 * i + 4]
            at = (0,) if len(w_ref.shape) == 3 else ()
            delta, nm, nv = _adamw_math(w_ref[at] if at else w_ref[...], g, m_ref[at] if at else m_ref[...],
                                        v_ref[at] if at else v_ref[...])
            for ref, val in ((g_out, g), (d_out, delta), (nm_out, nm), (nv_out, nv)):
                if at:
                    ref[0] = val
                else:
                    ref[...] = val

    vm = pl.BlockSpec(memory_space=pltpu.VMEM)
    flat = [a for p in params for a in p]
    outs = pl.pallas_call(
        body, name="adamw_small", in_specs=[pl.BlockSpec(memory_space=pltpu.SMEM), vm] + [vm] * len(flat),
        out_specs=[vm] * (4 * n), out_shape=[jax.ShapeDtypeStruct(p[0].shape, F32) for p in params for _ in range(4)],
    )(chip, tot, *flat)
    return [tuple(outs[4 * i:4 * i + 4]) for i in range(n)]


class _ReduceScatter:
    def __init__(self, place):
        self.place = place
        self.exchanging = {}
        self.started = []
        self.sharing = {}

    def begin(self, tag, grads):
        pieces = [g[None] if kind == "col" else g.reshape(N_CHIPS, g.shape[0] // N_CHIPS, g.shape[1]) for _, g, kind in grads]
        landing = [lax.empty((p.shape[0], p.shape[1] // 2, p.shape[2]), p.dtype) for p in pieces]
        started = _start_copies("exchange_" + tag, pieces + landing, _exchange_plan(len(pieces)), len(pieces))
        self.exchanging[tag] = ([nm for nm, _, _ in grads], started)
        return started[3]

    def send(self, tag, after):
        group, (send_sem, recv_sem, arrays, _) = self.exchanging[tag]
        n = len(group)
        arrays = _wait_copies("exchange_" + tag + "_wait", send_sem, recv_sem, arrays, _exchange_plan(n), after)
        parts = [_pair_sum(g, t, self.place[:1], "pair_sum_" + nm) for g, t, nm in zip(arrays[:n], arrays[n:], group)]
        started = _scatter_start("scatter_" + tag, parts)
        token = self.sum_up(started[3])
        self.started.append((tag, group, started))
        return started[3] if token is None else token

    def sum_up(self, after):
        token = None
        for tag, group, started in self.started:
            if tag not in self.sharing:
                parts, stacks = _scatter_finish("scatter_" + tag, started, after)
                sums = [_sum_chips(p, s, self.place, "chip_sum_" + nm) for p, s, nm in zip(parts, stacks, group)]
                self.sharing[tag] = (group, _start_copies("share_" + tag, sums, _own_half_plan, len(sums)))
                token = self.sharing[tag][1][3]
        return token

    def result(self, tag, after):
        group, (send_sem, recv_sem, sums, _) = self.sharing[tag]
        return dict(zip(group, _wait_copies("share_" + tag + "_wait", send_sem, recv_sem, sums, _own_half_plan, after)))


def kernel(x, ev_norm_w, ev_w_in, ev_conv_w, ev_w_out, od_norm_w, od_w_in, od_q_norm_w, od_k_norm_w, od_sinks, od_w_out, loss_target, m_ev_norm_w, m_ev_w_in, m_ev_conv_w, m_ev_w_out, m_od_norm_w, m_od_w_in, m_od_q_norm_w, m_od_k_norm_w, m_od_sinks, m_od_w_out, v_ev_norm_w, v_ev_w_in, v_ev_conv_w, v_ev_w_out, v_od_norm_w, v_od_w_in, v_od_q_norm_w, v_od_k_norm_w, v_od_sinks, v_od_w_out):
    my_chip = 2 * lax.axis_index("x") + lax.axis_index("y")
    place = jnp.stack([lax.axis_index("c"), my_chip]).astype(jnp.int32)
    shard_w = D_MODEL // N_CHIPS
    conv_shard = CONV_WIDTH // N_CHIPS

    small_in = jnp.zeros((8, shard_w), F32)
    small_in = small_in.at[0].set(od_norm_w[0]).at[1:4, :conv_shard].set(ev_conv_w[0])
    small_in = lax.dynamic_update_slice(jnp.zeros((N_CHIPS, 8, shard_w), F32), small_in[None], (my_chip, 0, 0))
    chip = place[1:]
    first_kinds, second_kinds, late_kinds = (("col", 0, 2), "whole"), (("col", 1, 2),), ("row", "col", "row")
    w_in0_own_place, w_in0_shard = _cast_into_gathered(ev_w_in[0], "col", chip, "cast_w_in0", keep_shard=True)
    first = _gather_start("gather_first", [w_in0_own_place, small_in], first_kinds)
    second = _gather_start("gather_second", [first[2][0]], second_kinds)
    late_own = [_cast_into_gathered(ev_w_out[0], "row", chip, "cast_w_out0"),
                _cast_into_gathered(od_w_in[0], "col", chip, "cast_w_in1"),
                _cast_into_gathered(od_w_out[0], "row", chip, "cast_w_out1")]
    late, w_in0_so_far = [], []

    def weights_first(part, after):
        if part == 0:
            passed_on = _gather_arrive("gather_first", first, first_kinds, after, arrays=[second[2][0], first[2][1]])
            w_in0, small_all = _gather_done("gather_first", passed_on, first_kinds, [passed_on[3]] + late_own)
            w_in0_so_far.append(w_in0)
            late.append(_gather_start("gather_late", late_own, late_kinds, after=small_all))
            od_norm_full = small_all[:, 0, :].reshape(1, D_MODEL)
            conv_full = jnp.transpose(small_all[:, 1:4, :conv_shard], (1, 0, 2)).reshape(3, CONV_WIDTH)
            return w_in0, (conv_full, od_norm_full), late[0][3]
        passed_on = _gather_arrive("gather_second", second, second_kinds, after, arrays=w_in0_so_far)
        (w_in0,) = _gather_done("gather_second", passed_on, second_kinds, passed_on[3])
        return w_in0, None, None

    def late_arrived(after):
        late.append(_gather_arrive("gather_late", late[0], late_kinds, after))
        return late[1][3]

    def weights_late(after):
        return _gather_done("gather_late", late[1], late_kinds, after)

    reduce_scatter = _ReduceScatter(place)
    sq, grad_x, small = _local_step(x[0], loss_target[0], ev_norm_w, od_q_norm_w, od_k_norm_w, od_sinks,
                                    (w_in0_shard, chip, second[3]), weights_first, late_arrived, weights_late,
                                    reduce_scatter)

    token = reduce_scatter.sum_up(grad_x)
    upd = {}
    upper = reduce_scatter.result("upper", token)
    upd["od_w_in"] = _adamw(od_w_in[0], upper["od_w_in"], m_od_w_in[0], v_od_w_in[0], "adamw_od_w_in")
    upd["od_w_out"] = _adamw(od_w_out[0], upper["od_w_out"], m_od_w_out[0], v_od_w_out[0], "adamw_od_w_out")
    upd["ev_w_out"] = _adamw(ev_w_out[0], upper["ev_w_out"], m_ev_w_out[0], v_ev_w_out[0], "adamw_ev_w_out")
    g_ev_w_in = reduce_scatter.result("in0", [upd[nm][1] for nm in ("od_w_in", "od_w_out", "ev_w_out")])["ev_w_in"]
    upd["ev_w_in"] = _adamw(ev_w_in[0], g_ev_w_in, m_ev_w_in[0], v_ev_w_in[0], "adamw_ev_w_in")
    tot = _allreduce_small(_pack_small_grads(small, sq))
    loss = 0.5 * tot[6, 0] / D_MODEL

    smalls = (("ev_norm_w", ev_norm_w, m_ev_norm_w, v_ev_norm_w), ("ev_conv_w", ev_conv_w, m_ev_conv_w, v_ev_conv_w),
              ("od_norm_w", od_norm_w, m_od_norm_w, v_od_norm_w), ("od_q_norm_w", od_q_norm_w, m_od_q_norm_w, v_od_q_norm_w),
              ("od_k_norm_w", od_k_norm_w, m_od_k_norm_w, v_od_k_norm_w), ("od_sinks", od_sinks, m_od_sinks, v_od_sinks))
    for (nm, _, _, _), result in zip(smalls, _adamw_small(tot, chip, [p[1:] for p in smalls])):
        upd[nm] = result
    for nm in ("ev_w_in", "ev_w_out", "od_w_in", "od_w_out"):
        upd[nm] = tuple(u[None] for u in upd[nm])
    order = ("ev_norm_w", "ev_w_in", "ev_conv_w", "ev_w_out", "od_norm_w", "od_w_in", "od_q_norm_w", "od_k_norm_w", "od_sinks", "od_w_out")
    return (loss, grad_x[None], *[upd[nm][0] for nm in order], *[upd[nm][1] for nm in order],
            *[upd[nm][2] for nm in order], *[upd[nm][3] for nm in order])
```

```python
import math

import jax
import jax.numpy as jnp
import numpy as np
from jax import lax
from jax.experimental import pallas as pl
from jax.experimental.pallas import tpu as pltpu

F32 = jnp.float32
BF16 = jnp.bfloat16

D_MODEL = 2048
RET_HEADS = 4
RET_DIM = 256
RET_WIDTH = 1024
CONV_WIDTH = 1024
EVEN_IN = 8192
Q_HEADS = 32
HEAD_DIM = 64
KV_HEADS = 4
KV_WIDTH = 256
ATTN_WIDTH = 2048
ODD_IN = 4608
BLK = 128
ROPE_THETA = 10000.0
EPS = 1e-6
ADAM_LR = 0.001
ADAM_B1 = 0.9
ADAM_B2 = 0.999
ADAM_EPS = 1e-08
ADAM_WD = 0.01
ADAM_STEP = 10
N_CHIPS = 4
N_DEV = 8
VMEM_LIMIT_BYTES = 56 * 1024 * 1024
MESH = pl.DeviceIdType.MESH
ANY = pl.BlockSpec(memory_space=pl.ANY)


def _params(*sem):
    return pltpu.CompilerParams(dimension_semantics=sem, vmem_limit_bytes=VMEM_LIMIT_BYTES)


def _dot(a, b):
    return jnp.dot(a, b, preferred_element_type=F32)


def _dot_nt(a, b):
    return lax.dot_general(a, b, (((1,), (1,)), ((), ())), preferred_element_type=F32)


def _dot_tn(a, b):
    return lax.dot_general(a, b, (((0,), (0,)), ((), ())), preferred_element_type=F32)


def _sigmoid(x):
    return 1.0 / (1.0 + jnp.exp(-x))


def _mm(a, b, *, mode, tm, tn, tk, out_dtype, name, add=None, after=None, m_blocks=None, into=None):
    if mode == "nn":
        (m, k), n = a.shape, b.shape[1]
    elif mode == "nt":
        (m, k), n = a.shape, b.shape[0]
    else:
        (k, m), n = a.shape, b.shape[1]
    tm, tn, tk = min(tm, m), min(tn, n), min(tk, k)
    assert m % tm == 0 and n % tn == 0 and k % tk == 0, (name, m, n, k)
    nk = k // tk
    first_m, count_m = (0, m // tm) if m_blocks is None else m_blocks
    dot = {"nn": _dot, "nt": _dot_nt, "tn": _dot_tn}[mode]
    a_spec = (pl.BlockSpec((tk, tm), lambda i, j, kk: (kk, i + first_m)) if mode == "tn"
              else pl.BlockSpec((tm, tk), lambda i, j, kk: (i + first_m, kk)))
    b_spec = (pl.BlockSpec((tn, tk), lambda i, j, kk: (j, kk)) if mode == "nt"
              else pl.BlockSpec((tk, tn), lambda i, j, kk: (kk, j)))
    o_spec = pl.BlockSpec((tm, tn), lambda i, j, kk: (i + first_m, j))
    has_add = add is not None

    def body(*refs):
        a_ref, b_ref = refs[0], refs[1]
        add_ref = refs[2] if has_add else None
        o_ref, acc_ref = refs[-2], refs[-1]
        p = dot(a_ref[...], b_ref[...])

        def finish(total):
            if has_add:
                total = total + add_ref[...].astype(F32)
            o_ref[...] = total.astype(out_dtype)

        if nk == 1:
            finish(p)
        else:
            kk = pl.program_id(2)

            @pl.when(kk == 0)
            def _():
                acc_ref[...] = p

            @pl.when(jnp.logical_and(kk > 0, kk < nk - 1))
            def _():
                acc_ref[...] += p

            @pl.when(kk == nk - 1)
            def _():
                finish(acc_ref[...] + p)

    extra = [arr for arr in (after, into) if arr is not None]
    in_specs = [a_spec, b_spec] + ([o_spec] if has_add else []) + [ANY] * len(extra)
    args = (a, b) + ((add,) if has_add else ()) + tuple(extra)
    return pl.pallas_call(
        body, name=name, grid=(count_m, n // tn, nk), in_specs=in_specs, out_specs=o_spec,
        out_shape=jax.ShapeDtypeStruct((m, n), out_dtype),
        input_output_aliases={len(args) - 1: 0} if into is not None else {},
        scratch_shapes=[pltpu.VMEM((tm, tn) if nk > 1 else (8, 128), F32)],
        compiler_params=_params("parallel", "parallel", "arbitrary"),
    )(*args)


def _mm_shifted(a, b, shift, *, b_shifted, first, count, total, tm, tn, out_dtype, name, stride=1, into=None, after=None):
    m, k = a.shape
    tm = min(tm, m)
    assert m % tm == 0
    col = lambda j, shift_ref: (shift_ref[0] + first + stride * j) % total
    extra = [arr for arr in (into, after) if arr is not None]

    def body(shift_ref, a_ref, b_ref, *rest):
        del shift_ref
        rest[-1][...] = _dot(a_ref[...], b_ref[...]).astype(out_dtype)

    return pl.pallas_call(
        body, name=name, out_shape=jax.ShapeDtypeStruct((m, total * tn), out_dtype),
        grid_spec=pltpu.PrefetchScalarGridSpec(
            num_scalar_prefetch=1, grid=(m // tm, count),
            in_specs=[pl.BlockSpec((tm, k), lambda i, j, s: (i, 0)),
                      pl.BlockSpec((k, tn), (lambda i, j, s: (0, col(j, s))) if b_shifted else (lambda i, j, s: (0, j)))]
            + [ANY] * len(extra),
            out_specs=pl.BlockSpec((tm, tn), lambda i, j, s: (i, col(j, s)))),
        input_output_aliases={3: 0} if into is not None else {},
        compiler_params=_params("parallel", "arbitrary"))(shift, a, b, *extra)


def _mm_rows(a, b, rows_in, vecs_in, out_shapes, epilogue, *, tm, name):
    m, k = a.shape
    n = b.shape[1]
    assert m % tm == 0
    row = pl.BlockSpec((tm, n), lambda i: (i, 0))

    def body(a_ref, b_ref, *rest):
        epilogue(_dot(a_ref[...], b_ref[...]), *rest)

    out_specs = [row if tuple(s.shape) == (m, n) else pl.BlockSpec(s.shape, lambda i: (0, 0)) for s in out_shapes]
    return pl.pallas_call(
        body, name=name, grid=(m // tm,),
        in_specs=[pl.BlockSpec((tm, k), lambda i: (i, 0)), pl.BlockSpec((k, n), lambda i: (0, 0))] + [row] * len(rows_in)
        + [pl.BlockSpec((1, n), lambda i: (0, 0))] * len(vecs_in),
        out_specs=out_specs, out_shape=out_shapes, compiler_params=_params("arbitrary"),
    )(a, b, *rows_in, *vecs_in)


def _cast_into_gathered(w, kind, chip, name, keep_shard=False):
    r, c = w.shape
    tr = min(r, 512)
    per = r // tr

    def body(chip_ref, w_ref, *outs):
        del chip_ref
        for o_ref in outs:
            o_ref[...] = w_ref[...].astype(BF16)

    if kind == "col":
        shape, out_map = (r, N_CHIPS * c), (lambda i, chip_ref: (i, chip_ref[0]))
    else:
        shape, out_map = (N_CHIPS * r, c), (lambda i, chip_ref: (chip_ref[0] * per + i, 0))
    plain = pl.BlockSpec((tr, c), lambda i, chip_ref: (i, 0))
    out = pl.pallas_call(
        body, name=name,
        out_shape=[jax.ShapeDtypeStruct(shape, BF16)] + ([jax.ShapeDtypeStruct((r, c), BF16)] if keep_shard else []),
        grid_spec=pltpu.PrefetchScalarGridSpec(
            num_scalar_prefetch=1, grid=(per,), in_specs=[plain],
            out_specs=[pl.BlockSpec((tr, c), out_map)] + ([plain] if keep_shard else [])),
        compiler_params=_params("parallel"))(chip, w)
    return out if keep_shard else out[0]


NORM_ROWS = 512


def _rmsnorm(x, w, name, after=None):
    s, d = x.shape
    tr = NORM_ROWS

    def body(x_ref, w_ref, *rest):
        xv = x_ref[...]
        rstd = lax.rsqrt(jnp.mean(xv * xv, axis=-1, keepdims=True) + EPS)
        rest[-1][...] = (xv * rstd * w_ref[...]).astype(BF16)

    return pl.pallas_call(
        body, name=name, grid=(s // tr,),
        in_specs=[pl.BlockSpec((tr, d), lambda i: (i, 0)), pl.BlockSpec((1, d), lambda i: (0, 0))]
        + ([ANY] if after is not None else []),
        out_specs=pl.BlockSpec((tr, d), lambda i: (i, 0)),
        out_shape=jax.ShapeDtypeStruct((s, d), BF16), compiler_params=_params("parallel"),
    )(x, w, *((after,) if after is not None else ()))


def _rmsnorm_bwd(x, w, dh, dres, name, out_dtype):
    s, d = x.shape
    tr = NORM_ROWS

    def body(x_ref, w_ref, dh_ref, dres_ref, dx_ref, dw_ref):
        xv = x_ref[...]
        rstd = lax.rsqrt(jnp.mean(xv * xv, axis=-1, keepdims=True) + EPS)
        nrm = xv * rstd
        dhv = dh_ref[...].astype(F32)
        dn = dhv * w_ref[...]
        dx = dres_ref[...].astype(F32) + rstd * (dn - nrm * jnp.mean(dn * nrm, axis=-1, keepdims=True))
        dx_ref[...] = dx.astype(out_dtype)

        @pl.when(pl.program_id(0) == 0)
        def _():
            dw_ref[...] = jnp.zeros_like(dw_ref)

        dw_ref[...] += jnp.sum(dhv * nrm, axis=0, keepdims=True)

    row = pl.BlockSpec((tr, d), lambda i: (i, 0))
    return pl.pallas_call(
        body, name=name, grid=(s // tr,),
        in_specs=[row, pl.BlockSpec((1, d), lambda i: (0, 0)), row, row],
        out_specs=[row, pl.BlockSpec((8, d), lambda i: (0, 0))],
        out_shape=[jax.ShapeDtypeStruct((s, d), out_dtype), jax.ShapeDtypeStruct((8, d), F32)],
        compiler_params=_params("arbitrary"))(x, w, dh, dres)


def _adamw_math(w, g, m, v):
    nm = ADAM_B1 * m + (1.0 - ADAM_B1) * g
    nv = ADAM_B2 * v + (1.0 - ADAM_B2) * (g * g)
    m_hat = nm / (1.0 - ADAM_B1 ** ADAM_STEP)
    v_hat = nv / (1.0 - ADAM_B2 ** ADAM_STEP)
    return -ADAM_LR * (m_hat / (jnp.sqrt(v_hat) + ADAM_EPS) + ADAM_WD * w), nm, nv


def _adamw(w, g, m, v, name):
    r, c = w.shape
    tr = min(r, 256)
    assert r % tr == 0

    def body(w_ref, g_ref, m_ref, v_ref, g_out, d_ref, nm_ref, nv_ref):
        gv = g_ref[...]
        g_out[...] = gv
        d_ref[...], nm_ref[...], nv_ref[...] = _adamw_math(w_ref[...], gv, m_ref[...], v_ref[...])

    spec = pl.BlockSpec((tr, c), lambda i: (i, 0))
    shp = jax.ShapeDtypeStruct((r, c), F32)
    return pl.pallas_call(body, name=name, grid=(r // tr,), in_specs=[spec] * 4, out_specs=[spec] * 4,
                          out_shape=[shp] * 4, compiler_params=_params("parallel"))(w, g, m, v)


def _rope_tables(s, dim):
    inv = (1.0 / (ROPE_THETA ** (np.arange(0, dim, 2, dtype=np.float64) / dim))).astype(np.float32)
    ang = (np.arange(s, dtype=np.float32)[:, None] * inv[None, :]).astype(np.float64)
    return np.cos(ang).astype(np.float32), np.sin(ang).astype(np.float32)


def _rope_half(x, cos, sin):
    h = x.shape[1] // 2
    x1, x2 = x[:, :h], x[:, h:]
    return jnp.concatenate([x1 * cos - x2 * sin, x2 * cos + x1 * sin], axis=1)


def _unrope_half(dy, cos, sin):
    h = dy.shape[1] // 2
    d1, d2 = dy[:, :h], dy[:, h:]
    return jnp.concatenate([d1 * cos + d2 * sin, d2 * cos - d1 * sin], axis=1)


def _lane(shape):
    return lax.broadcasted_iota(jnp.int32, shape, 1)


def _partner64(x):
    w = x.shape[1]
    first = (_lane(x.shape) % HEAD_DIM) < (HEAD_DIM // 2)
    return jnp.where(first, pltpu.roll(x, w - HEAD_DIM // 2, 1), pltpu.roll(x, HEAD_DIM // 2, 1))


def _tile_lanes(t, reps):
    return t if reps == 1 else jnp.concatenate([t] * reps, axis=1)


MXU_WIDTH = 256


def _group_mean(x, ones_bd, passes=2):
    width, tile = x.shape[1], ones_bd.shape[0]
    if width > tile:
        return jnp.concatenate([_group_mean(x[:, c:c + tile], ones_bd, passes) for c in range(0, width, tile)], axis=1)
    hi = x.astype(BF16)
    if passes == 1:
        return _dot(hi, ones_bd)
    lo = (x - hi.astype(F32)).astype(BF16)
    return _dot(hi, ones_bd) + _dot(lo, ones_bd)


def _block_diag_mean(width):
    idx = jnp.arange(width) // HEAD_DIM
    return jnp.where(idx[:, None] == idx[None, :], 1.0 / HEAD_DIM, 0.0).astype(BF16)


RET_STEP = 4


def _retention_tables():
    h = RET_HEADS
    log_g = jnp.log(1.0 - 2.0 ** (-5.0 - jnp.arange(h, dtype=F32)))
    idx = jnp.arange(BLK, dtype=F32)
    diff = idx[:, None] - idx[None, :]
    intra = jnp.where(diff >= 0, jnp.exp(log_g[:, None, None] * jnp.maximum(diff, 0.0)), 0.0).astype(F32)
    q_dec = jnp.exp(log_g[:, None] * (idx[None, :] + 1.0)).astype(F32)[:, :, None]
    k_dec = jnp.exp(log_g[:, None] * (BLK - 1.0 - idx[None, :])).astype(F32)[:, :, None]
    chunk_dec = jnp.exp(log_g * BLK).astype(F32)[:, None, None]
    return intra, q_dec, k_dec, chunk_dec


def _retention_fwd(proj, cos, sin, tables, after=None):
    s = proj.shape[0]
    nc = s // BLK
    intra, q_dec, k_dec, chunk_dec = tables

    def body(p_ref, cos_ref, sin_ref, in_ref, qd_ref, kd_ref, cd_ref, *rest):
        o_ref, cat_ref, st_ref, state = rest[-4:]

        @pl.when(pl.program_id(0) == 0)
        def _():
            state[...] = jnp.zeros_like(state)

        for c in range(RET_STEP):
            rows = slice(c * BLK, (c + 1) * BLK)
            cosv, sinv = cos_ref[rows, :], sin_ref[rows, :]
            for h in range(RET_HEADS):
                c0 = h * RET_DIM
                q = p_ref[rows, c0:c0 + RET_DIM].astype(F32)
                k = p_ref[rows, RET_WIDTH + c0:RET_WIDTH + c0 + RET_DIM].astype(F32)
                v = p_ref[rows, 2 * RET_WIDTH + c0:2 * RET_WIDTH + c0 + RET_DIM]
                g = p_ref[rows, 3 * RET_WIDTH + c0:3 * RET_WIDTH + c0 + RET_DIM].astype(F32)
                qb = _rope_half(q, cosv, sinv).astype(BF16)
                kr = _rope_half(k, cosv, sinv) * (RET_DIM ** -0.5)
                kb = kr.astype(BF16)
                scores = _dot_nt(qb, kb) * in_ref[h]
                inner = _dot(scores.astype(BF16), v)
                prev = state[h]
                prev_b = prev.astype(BF16)
                st_ref[h, c] = prev_b
                o = inner + _dot(qb, prev_b) * qd_ref[h]
                o_ref[rows, c0:c0 + RET_DIM] = o
                rstd = lax.rsqrt(jnp.mean(o * o, axis=-1, keepdims=True) + EPS)
                cat_ref[rows, c0:c0 + RET_DIM] = (o * rstd * (g * _sigmoid(g))).astype(BF16)
                state[h] = cd_ref[h] * prev + _dot_tn((kr * kd_ref[h]).astype(BF16), v)

    full = lambda shape: pl.BlockSpec(shape, lambda n: (0,) * len(shape))
    step = RET_STEP * BLK
    return pl.pallas_call(
        body, name="retention_fwd", grid=(nc // RET_STEP,),
        in_specs=[pl.BlockSpec((step, 4 * RET_WIDTH), lambda n: (n, 0)),
                  pl.BlockSpec((step, RET_DIM // 2), lambda n: (n, 0)), pl.BlockSpec((step, RET_DIM // 2), lambda n: (n, 0)),
                  full((RET_HEADS, BLK, BLK)), full((RET_HEADS, BLK, 1)), full((RET_HEADS, BLK, 1)), full((RET_HEADS, 1, 1))]
        + ([ANY] if after is not None else []),
        out_specs=[pl.BlockSpec((step, RET_WIDTH), lambda n: (n, 0)), pl.BlockSpec((step, RET_WIDTH), lambda n: (n, 0)),
                   pl.BlockSpec((RET_HEADS, RET_STEP, RET_DIM, RET_DIM), lambda n: (0, n, 0, 0))],
        out_shape=[jax.ShapeDtypeStruct((s, RET_WIDTH), F32), jax.ShapeDtypeStruct((s, D_MODEL), BF16),
                   jax.ShapeDtypeStruct((RET_HEADS, nc, RET_DIM, RET_DIM), BF16)],
        scratch_shapes=[pltpu.VMEM((RET_HEADS, RET_DIM, RET_DIM), F32)],
        compiler_params=_params("arbitrary"),
    )(proj, cos, sin, intra, q_dec, k_dec, chunk_dec, *((after,) if after is not None else ()))


def _retention_bwd(proj, o, states, dcat, cos, sin, tables):
    s = proj.shape[0]
    nc = s // BLK
    intra, q_dec, k_dec, chunk_dec = tables

    def body(p_ref, o_ref, st_ref, dc_ref, cos_ref, sin_ref, in_ref, qd_ref, kd_ref, cd_ref, dp_ref, dstate):
        @pl.when(pl.program_id(0) == 0)
        def _():
            dstate[...] = jnp.zeros_like(dstate)

        for c in reversed(range(RET_STEP)):
            rows = slice(c * BLK, (c + 1) * BLK)
            cosv, sinv = cos_ref[rows, :], sin_ref[rows, :]
            for h in range(RET_HEADS):
                c0 = h * RET_DIM
                q = p_ref[rows, c0:c0 + RET_DIM].astype(F32)
                k = p_ref[rows, RET_WIDTH + c0:RET_WIDTH + c0 + RET_DIM].astype(F32)
                v = p_ref[rows, 2 * RET_WIDTH + c0:2 * RET_WIDTH + c0 + RET_DIM]
                g = p_ref[rows, 3 * RET_WIDTH + c0:3 * RET_WIDTH + c0 + RET_DIM].astype(F32)
                o = o_ref[rows, c0:c0 + RET_DIM]
                dc = dc_ref[rows, c0:c0 + RET_DIM].astype(F32)
                rstd = lax.rsqrt(jnp.mean(o * o, axis=-1, keepdims=True) + EPS)
                nrm = o * rstd
                sg = _sigmoid(g)
                dg = dc * nrm * (sg * (1.0 + g * (1.0 - sg)))
                dn = dc * (g * sg)
                do = rstd * (dn - nrm * jnp.mean(dn * nrm, axis=-1, keepdims=True))
                qb = _rope_half(q, cosv, sinv).astype(BF16)
                kr = _rope_half(k, cosv, sinv) * (RET_DIM ** -0.5)
                kb = kr.astype(BF16)
                mask = in_ref[h]
                qd, kd = qd_ref[h], kd_ref[h]
                prev_b = st_ref[h, c]
                dnext = dstate[h]
                dnext_b = dnext.astype(BF16)
                att = (_dot_nt(qb, kb) * mask).astype(BF16)
                do_b = do.astype(BF16)
                doq = (do * qd).astype(BF16)
                dv = _dot_tn(att, do_b) + _dot((kr * kd).astype(BF16), dnext_b)
                ds = (_dot_nt(do_b, v) * mask).astype(BF16)
                dqr = _dot(ds, kb) + _dot_nt(doq, prev_b)
                dkr = _dot_tn(ds, qb) + _dot_nt(v, dnext_b) * kd
                dstate[h] = cd_ref[h] * dnext + _dot_tn(qb, doq)
                dq = _unrope_half(dqr, cosv, sinv)
                dk = _unrope_half(dkr * (RET_DIM ** -0.5), cosv, sinv)
                dp_ref[rows, c0:c0 + RET_DIM] = dq.astype(BF16)
                dp_ref[rows, RET_WIDTH + c0:RET_WIDTH + c0 + RET_DIM] = dk.astype(BF16)
                dp_ref[rows, 2 * RET_WIDTH + c0:2 * RET_WIDTH + c0 + RET_DIM] = dv.astype(BF16)
                dp_ref[rows, 3 * RET_WIDTH + c0:3 * RET_WIDTH + c0 + RET_DIM] = dg.astype(BF16)

    steps = nc // RET_STEP
    rev = lambda n: steps - 1 - n
    full = lambda shape: pl.BlockSpec(shape, lambda n: (0,) * len(shape))
    step = RET_STEP * BLK
    return pl.pallas_call(
        body, name="retention_bwd", grid=(steps,),
        in_specs=[pl.BlockSpec((step, 4 * RET_WIDTH), lambda n: (rev(n), 0)),
                  pl.BlockSpec((step, RET_WIDTH), lambda n: (rev(n), 0)),
                  pl.BlockSpec((RET_HEADS, RET_STEP, RET_DIM, RET_DIM), lambda n: (0, rev(n), 0, 0)),
                  pl.BlockSpec((step, RET_WIDTH), lambda n: (rev(n), 0)),
                  pl.BlockSpec((step, RET_DIM // 2), lambda n: (rev(n), 0)), pl.BlockSpec((step, RET_DIM // 2), lambda n: (rev(n), 0)),
                  full((RET_HEADS, BLK, BLK)), full((RET_HEADS, BLK, 1)), full((RET_HEADS, BLK, 1)), full((RET_HEADS, 1, 1))],
        out_specs=pl.BlockSpec((step, 4 * RET_WIDTH), lambda n: (rev(n), 0)),
        out_shape=jax.ShapeDtypeStruct((s, EVEN_IN), BF16),
        scratch_shapes=[pltpu.VMEM((RET_HEADS, RET_DIM, RET_DIM), F32)],
        compiler_params=_params("arbitrary"))(proj, o, states, dcat, cos, sin, intra, q_dec, k_dec, chunk_dec)


CONV_ROWS = 256
HALO = 16


def _conv_pieces(p, halo, first):
    gb, gc, u, gv = (p[:, i * CONV_WIDTH:(i + 1) * CONV_WIDTH] for i in range(4))
    cu = gc * u
    hcu = halo[:, CONV_WIDTH:2 * CONV_WIDTH] * halo[:, 2 * CONV_WIDTH:3 * CONV_WIDTH]
    hcu = jnp.where(first, 0.0, hcu)
    r1, r2 = hcu[HALO - 1:HALO], hcu[HALO - 2:HALO - 1]
    row = lax.broadcasted_iota(jnp.int32, cu.shape, 0)
    m1 = jnp.where(row == 0, r1, pltpu.roll(cu, 1, 0))
    m2 = jnp.where(row == 0, r2, jnp.where(row == 1, r1, pltpu.roll(cu, 2, 0)))
    return gb, gc, u, gv, cu, m1, m2


def _conv_fwd(proj, conv_w, cat):
    s = proj.shape[0]
    per = CONV_ROWS // HALO

    def body(p_ref, halo_ref, w_ref, cat_in, cat_ref):
        del cat_in
        first = pl.program_id(0) == 0
        gb, _, _, gv, cu, m1, m2 = _conv_pieces(p_ref[...].astype(F32), halo_ref[...].astype(F32), first)
        conv = w_ref[0:1, :] * m2 + w_ref[1:2, :] * m1 + w_ref[2:3, :] * cu
        cat_ref[...] = (gb * conv * (gv * _sigmoid(gv))).astype(BF16)

    return pl.pallas_call(
        body, name="conv_fwd", grid=(s // CONV_ROWS,),
        in_specs=[pl.BlockSpec((CONV_ROWS, 4 * CONV_WIDTH), lambda i: (i, 1)),
                  pl.BlockSpec((HALO, 4 * CONV_WIDTH), lambda i: (jnp.maximum(i * per - 1, 0), 1)),
                  pl.BlockSpec((3, CONV_WIDTH), lambda i: (0, 0)), ANY],
        out_specs=pl.BlockSpec((CONV_ROWS, CONV_WIDTH), lambda i: (i, 1)),
        out_shape=jax.ShapeDtypeStruct(cat.shape, cat.dtype), input_output_aliases={3: 0},
        compiler_params=_params("parallel"))(proj, proj, conv_w, cat)


def _conv_bwd(proj, dcat, conv_w, dproj):
    s = proj.shape[0]
    per = CONV_ROWS // HALO
    last_halo = s // HALO - 1
    nsteps = s // CONV_ROWS

    def body(p_ref, halo_ref, nxt_ref, dc_ref, dnxt_ref, w_ref, dp_in, dp_ref, dw_ref):
        del dp_in
        i = pl.program_id(0)
        gb, gc, u, gv, cu, m1, m2 = _conv_pieces(p_ref[...].astype(F32), halo_ref[...].astype(F32), i == 0)
        w0, w1, w2 = w_ref[0:1, :], w_ref[1:2, :], w_ref[2:3, :]
        conv = w0 * m2 + w1 * m1 + w2 * cu
        dco = dc_ref[...].astype(F32)
        sg = _sigmoid(gv)
        silu = gv * sg
        dgb = dco * conv * silu
        dco_gb = dco * gb
        dgv = dco_gb * conv * (sg * (1.0 + gv * (1.0 - sg)))
        dconv = dco_gb * silu
        nxt = nxt_ref[...].astype(F32)
        ngv = nxt[:, 3 * CONV_WIDTH:]
        dnext = dnxt_ref[...].astype(F32) * nxt[:, :CONV_WIDTH] * (ngv * _sigmoid(ngv))
        dnext = jnp.where(i == nsteps - 1, 0.0, dnext)
        n1, n2 = dnext[0:1], dnext[1:2]
        row = lax.broadcasted_iota(jnp.int32, dconv.shape, 0)
        p1 = jnp.where(row == CONV_ROWS - 1, n1, pltpu.roll(dconv, CONV_ROWS - 1, 0))
        p2 = jnp.where(row == CONV_ROWS - 1, n2, jnp.where(row == CONV_ROWS - 2, n1, pltpu.roll(dconv, CONV_ROWS - 2, 0)))
        dcu = w2 * dconv + w1 * p1 + w0 * p2
        dp_ref[...] = jnp.concatenate([dgb, dcu * u, dcu * gc, dgv], axis=1).astype(BF16)

        @pl.when(i == 0)
        def _():
            dw_ref[...] = jnp.zeros_like(dw_ref)

        taps = [jnp.sum(dconv * m, axis=0, keepdims=True) for m in (m2, m1, cu)]
        r8 = lax.broadcasted_iota(jnp.int32, dw_ref.shape, 0)
        dw_ref[...] += jnp.where(r8 == 0, taps[0], jnp.where(r8 == 1, taps[1], jnp.where(r8 == 2, taps[2], 0.0)))

    return pl.pallas_call(
        body, name="conv_bwd", grid=(nsteps,),
        in_specs=[pl.BlockSpec((CONV_ROWS, 4 * CONV_WIDTH), lambda i: (i, 1)),
                  pl.BlockSpec((HALO, 4 * CONV_WIDTH), lambda i: (jnp.maximum(i * per - 1, 0), 1)),
                  pl.BlockSpec((HALO, 4 * CONV_WIDTH), lambda i: (jnp.minimum((i + 1) * per, last_halo), 1)),
                  pl.BlockSpec((CONV_ROWS, CONV_WIDTH), lambda i: (i, 1)),
                  pl.BlockSpec((HALO, CONV_WIDTH), lambda i: (jnp.minimum((i + 1) * per, last_halo), 1)),
                  pl.BlockSpec((3, CONV_WIDTH), lambda i: (0, 0)), ANY],
        out_specs=[pl.BlockSpec((CONV_ROWS, 4 * CONV_WIDTH), lambda i: (i, 1)), pl.BlockSpec((8, CONV_WIDTH), lambda i: (0, 0))],
        out_shape=[jax.ShapeDtypeStruct(dproj.shape, dproj.dtype), jax.ShapeDtypeStruct((8, CONV_WIDTH), F32)],
        input_output_aliases={6: 0},
        compiler_params=_params("arbitrary"))(proj, proj, proj, dcat, dcat, conv_w, dproj)


GROUP_HEADS = Q_HEADS // KV_HEADS
GROUP_WIDTH = GROUP_HEADS * HEAD_DIM
SLAB = 2 * HEAD_DIM
KV_COL = ATTN_WIDTH // GROUP_WIDTH
GATE_COL = KV_COL + 1
ATTN_SCALE = HEAD_DIM ** -0.5
KV_ROWS = 1024


def _half_mask(shape, which):
    return (_lane(shape) // HEAD_DIM) == which


def _dup_head(slab, which):
    kept = jnp.where(_half_mask(slab.shape, which), slab, 0.0)
    return kept + pltpu.roll(kept, HEAD_DIM, 1)


def _stack_heads(x):
    parts = []
    for sl in range(GROUP_WIDTH // SLAB):
        slab = x[:, sl * SLAB:(sl + 1) * SLAB]
        parts += [jnp.where(_half_mask(slab.shape, e), slab, 0.0) for e in range(2)]
    return jnp.concatenate(parts, axis=0)


def _unstack_heads(y):
    slabs = []
    for sl in range(GROUP_WIDTH // SLAB):
        a, b = y[(2 * sl) * BLK:(2 * sl + 1) * BLK], y[(2 * sl + 1) * BLK:(2 * sl + 2) * BLK]
        slabs.append(jnp.where(_half_mask(a.shape, 0), a, b))
    return jnp.concatenate(slabs, axis=1)


def _q_prep(q, qw, cosf, sins, ones_bd):
    rstd = lax.rsqrt(_group_mean(q * q, ones_bd) + EPS)
    nrm = q * rstd
    y = nrm * qw
    return nrm, rstd, y * cosf + _partner64(y) * sins


def _band(tri_ref, n):
    own = tri_ref[...] > 0.5
    return own, jnp.where(jnp.logical_and(n == 0, jnp.logical_not(own)), -1e30, 0.0)


def _fold(pair, own):
    return jnp.where(own, pair[:, BLK:], pair[:, :BLK])


def _unfold(folded, own):
    return jnp.concatenate([jnp.where(own, 0.0, folded), jnp.where(own, folded, 0.0)], axis=1)


def _head_probs(raw_scores, sink, own, bias):
    sc = _fold(raw_scores, own) + bias
    m = jnp.maximum(jnp.max(sc, axis=-1, keepdims=True), sink)
    p = jnp.exp(sc - m)
    psink = jnp.exp(sink - m)
    inv = 1.0 / (jnp.sum(p, axis=-1, keepdims=True) + psink)
    return p * inv, psink * inv


def _k_prep(k, kw, cosf, sins, ones_bd):
    rstd = lax.rsqrt(_group_mean(k * k, ones_bd) + EPS)
    nrm = k * rstd
    y = nrm * kw
    return nrm, rstd, y * cosf + _partner64(y) * sins


def _qk_prep(proj, qw, kw, cos, sins, ones_q, ones_kv):
    s = proj.shape[0]
    rows = min(KV_ROWS, s)

    def body(p_ref, qw_ref, kw_ref, cos_ref, sin_ref, oq_ref, ok_ref, o_ref):
        j = pl.program_id(1)

        @pl.when(j < KV_COL)
        def _():
            cosf, sinf = _tile_lanes(cos_ref[...], 4), _tile_lanes(sin_ref[...], 4)
            roped = _q_prep(p_ref[...].astype(F32), qw_ref[...], cosf, sinf, oq_ref[...])[2]
            o_ref[...] = (roped * ATTN_SCALE).astype(BF16)

        @pl.when(j == KV_COL)
        def _():
            cosf, sinf = _tile_lanes(cos_ref[...], 2), _tile_lanes(sin_ref[...], 2)
            kr = _k_prep(p_ref[:, :KV_WIDTH].astype(F32), kw_ref[...], cosf, sinf, ok_ref[...])[2]
            o_ref[...] = jnp.concatenate([kr.astype(BF16), p_ref[:, KV_WIDTH:]], axis=1)

    full = lambda shape: pl.BlockSpec(shape, lambda i, j: (0,) * len(shape))
    tab = pl.BlockSpec((rows, SLAB), lambda i, j: (i, 0))
    blk = pl.BlockSpec((rows, GROUP_WIDTH), lambda i, j: (i, j))
    return pl.pallas_call(
        body, name="qk_prep", grid=(s // rows, KV_COL + 1),
        in_specs=[blk, full((1, GROUP_WIDTH)), full((1, KV_WIDTH)), tab, tab, full(ones_q.shape), full(ones_kv.shape)],
        out_specs=blk, out_shape=jax.ShapeDtypeStruct((s, ATTN_WIDTH + 2 * KV_WIDTH), BF16),
        compiler_params=_params("parallel", "arbitrary"))(proj, qw, kw, cos, sins, ones_q, ones_kv)


def _keys_values(kc_ref, kp_ref, vc_ref, vp_ref, head):
    lanes = slice((head // 2) * SLAB, (head // 2 + 1) * SLAB)
    dup = lambda ref: _dup_head(ref[:, lanes].astype(F32), head % 2)
    return (jnp.concatenate([dup(kp_ref), dup(kc_ref)], axis=0).astype(BF16),
            jnp.concatenate([dup(vp_ref), dup(vc_ref)], axis=0).astype(BF16))


FWD_STEP_HEADS = 4
BWD_STEP_HEADS = 2


def _swa_specs(heads):
    kv_width = heads * HEAD_DIM
    prev = lambda n: jnp.maximum(n - 1, 0)
    kv = lambda col0, row: pl.BlockSpec((BLK, kv_width), lambda gs, n: (row(n), col0 // kv_width + gs))
    cur = lambda n: n
    full = lambda shape: pl.BlockSpec(shape, lambda gs, n: (0,) * len(shape))
    gate = lambda t: pl.BlockSpec((BLK, GROUP_WIDTH), lambda gs, n: (n, GATE_COL + heads * gs + t))
    return dict(
        sinks=pl.BlockSpec(memory_space=pltpu.SMEM), gates=[gate(t) for t in range(heads)],
        kc=kv(ATTN_WIDTH, cur), kp=kv(ATTN_WIDTH, prev), vc=kv(ATTN_WIDTH + KV_WIDTH, cur), vp=kv(ATTN_WIDTH + KV_WIDTH, prev),
        tri=full((BLK, BLK)), step=pl.BlockSpec((BLK, heads * GROUP_WIDTH), lambda gs, n: (n, gs)))


def _lower_triangle():
    return jnp.tril(jnp.ones((BLK, BLK), F32))


def _swa_fwd(qk, proj, sinks):
    s = proj.shape[0]
    nb = s // BLK
    heads = FWD_STEP_HEADS
    sp = _swa_specs(heads)

    def body(sink_ref, q_ref, kc_ref, kp_ref, vc_ref, vp_ref, *rest):
        gate_refs, (tri_ref, ag_ref, o_ref) = rest[:heads], rest[heads:]
        gs, n = pl.program_id(0), pl.program_id(1)
        own, bias = _band(tri_ref, n)
        for t, gate_ref in enumerate(gate_refs):
            cols = slice(t * GROUP_WIDTH, (t + 1) * GROUP_WIDTH)
            first_head = (heads * gs + t) * GROUP_HEADS
            kcat, vcat = _keys_values(kc_ref, kp_ref, vc_ref, vp_ref, t)
            scores = _dot_nt(_stack_heads(q_ref[:, cols]), kcat)
            probs = []
            for j in range(GROUP_HEADS):
                p, _ = _head_probs(scores[j * BLK:(j + 1) * BLK], sink_ref[first_head + j], own, bias)
                probs.append(_unfold(p, own).astype(BF16))
            o = _unstack_heads(_dot(jnp.concatenate(probs, axis=0), vcat))
            gate = gate_ref[...].astype(F32)
            o_ref[:, cols] = o.astype(BF16)
            ag_ref[:, cols] = (o * (gate * _sigmoid(gate))).astype(BF16)

    shp = jax.ShapeDtypeStruct((s, ATTN_WIDTH), BF16)
    return pl.pallas_call(
        body, name="swa_fwd", grid=(KV_HEADS // heads, nb),
        in_specs=[sp["sinks"], sp["step"], sp["kc"], sp["kp"], sp["vc"], sp["vp"], *sp["gates"], sp["tri"]],
        out_specs=[sp["step"], sp["step"]], out_shape=[shp, shp],
        compiler_params=_params("parallel", "arbitrary"),
    )(sinks, qk, qk, qk, qk, qk, *[proj] * heads, _lower_triangle())


def _swa_bwd(qk, proj, dag, sinks):
    s = proj.shape[0]
    nb = s // BLK
    heads = BWD_STEP_HEADS
    sp = _swa_specs(heads)

    def body(sink_ref, q_ref, kc_ref, kp_ref, vc_ref, vp_ref, *rest):
        gate_refs = rest[:heads]
        dag_ref, tri_ref, dq_ref, dkc_ref, dkp_ref, dvc_ref, dvp_ref, dsink_ref = rest[heads:]
        gs, n = pl.program_id(0), pl.program_id(1)
        own, bias = _band(tri_ref, n)

        @pl.when(n == 0)
        def _():
            dsink_ref[...] = jnp.zeros_like(dsink_ref)

        for t, gate_ref in enumerate(gate_refs):
            cols = slice(t * GROUP_WIDTH, (t + 1) * GROUP_WIDTH)
            first_head = (heads * gs + t) * GROUP_HEADS
            kcat, vcat = _keys_values(kc_ref, kp_ref, vc_ref, vp_ref, t)
            gate = gate_ref[...].astype(F32)
            do = dag_ref[:, cols].astype(F32) * (gate * _sigmoid(gate))
            q_stack = _stack_heads(q_ref[:, cols])
            do_stack = _stack_heads(do).astype(BF16)
            scores = _dot_nt(q_stack, kcat)
            dprobs = _dot_nt(do_stack, vcat)
            probs, dscores, dsinks = [], [], []
            for j in range(GROUP_HEADS):
                rows = slice(j * BLK, (j + 1) * BLK)
                p, psink = _head_probs(scores[rows], sink_ref[first_head + j], own, bias)
                dp = _fold(dprobs[rows], own)
                delta = jnp.sum(p * dp, axis=-1, keepdims=True)
                probs.append(_unfold(p, own).astype(BF16))
                dscores.append(_unfold(p * (dp - delta), own).astype(BF16))
                dsinks.append(-jnp.sum(psink * delta, axis=0, keepdims=True))
            ds = jnp.concatenate(dscores, axis=0)
            dk = _dot_tn(ds, q_stack)
            dv = _dot_tn(jnp.concatenate(probs, axis=0), do_stack)
            dk = dk + pltpu.roll(dk, HEAD_DIM, 1)
            dv = dv + pltpu.roll(dv, HEAD_DIM, 1)
            dkp_ref[t], dkc_ref[t] = dk[:BLK], dk[BLK:]
            dvp_ref[t], dvc_ref[t] = dv[:BLK], dv[BLK:]
            dq_ref[:, cols] = _unstack_heads(_dot(ds, kcat)).astype(BF16)
            r8 = lax.broadcasted_iota(jnp.int32, (8, SLAB), 0)
            upd = jnp.zeros((8, SLAB), F32)
            for j in range(GROUP_HEADS):
                upd = jnp.where(r8 == j, dsinks[j], upd)
            dsink_ref[t] += upd

    cur_out = pl.BlockSpec((heads, BLK, SLAB), lambda gs, n: (gs, n, 0))
    prev_out = pl.BlockSpec((heads, BLK, SLAB), lambda gs, n: (gs, (n + nb - 1) % nb, 0))
    kv_shape = jax.ShapeDtypeStruct((KV_HEADS, s, SLAB), F32)
    return pl.pallas_call(
        body, name="swa_bwd", grid=(KV_HEADS // heads, nb),
        in_specs=[sp["sinks"], sp["step"], sp["kc"], sp["kp"], sp["vc"], sp["vp"], *sp["gates"], sp["step"], sp["tri"]],
        out_specs=[sp["step"], cur_out, prev_out, cur_out, prev_out,
                   pl.BlockSpec((heads, 8, SLAB), lambda gs, n: (gs, 0, 0))],
        out_shape=[jax.ShapeDtypeStruct((s, ATTN_WIDTH), BF16), kv_shape, kv_shape, kv_shape, kv_shape,
                   jax.ShapeDtypeStruct((KV_HEADS, 8, SLAB), F32)],
        compiler_params=_params("parallel", "arbitrary"),
    )(sinks, qk, qk, qk, qk, qk, *[proj] * heads, dag, _lower_triangle())


def _swa_bwd_finish(proj, dqr, o, dag, dkc, dkp, dvc, dvp, qw, kw, cos, sins, ones_q, ones_kv):
    s = proj.shape[0]
    rows = min(KV_ROWS, s)
    n_q = KV_COL
    q_of = lambda j: jnp.clip(j - 1, 0, n_q - 1)
    gate_of = lambda j: jnp.clip(j - 1 - n_q, 0, n_q - 1)

    def body(kv_ref, p_ref, dqr_ref, o_ref, dag_ref, dkc_ref, dkp_ref, dvc_ref, dvp_ref, qw_ref, kw_ref, cos_ref, sin_ref,
             oq_ref, ok_ref, dp_ref, dqw_ref, dkw_ref):
        j, i = pl.program_id(0), pl.program_id(1)

        @pl.when(j == 0)
        def _():
            @pl.when(i == 0)
            def _():
                dkw_ref[...] = jnp.zeros_like(dkw_ref)

            def assemble(cur_ref, prv_ref):
                tot = [cur_ref[h] + prv_ref[h] for h in range(KV_HEADS)]
                first = _half_mask(tot[0].shape, 0)
                return jnp.concatenate([jnp.where(first, tot[0], tot[1]), jnp.where(first, tot[2], tot[3])], axis=1)

            dkr = assemble(dkc_ref, dkp_ref)
            dv = assemble(dvc_ref, dvp_ref)
            cosf, sinf = _tile_lanes(cos_ref[...], 2), _tile_lanes(sin_ref[...], 2)
            nrm, rstd, _ = _k_prep(kv_ref[:, :KV_WIDTH].astype(F32), kw_ref[...], cosf, sinf, ok_ref[...])
            dy = dkr * cosf + _partner64(dkr * sinf)
            dn = dy * kw_ref[...]
            dk = rstd * (dn - nrm * _group_mean(dn * nrm, ok_ref[...], passes=1))
            dp_ref[...] = jnp.concatenate([dk, dv], axis=1).astype(BF16)
            dkw_ref[...] += jnp.sum(dy * nrm, axis=0, keepdims=True)

        @pl.when(jnp.logical_and(j >= 1, j <= n_q))
        def _():
            @pl.when(i == 0)
            def _():
                dqw_ref[...] = jnp.zeros_like(dqw_ref)

            cosf, sinf = _tile_lanes(cos_ref[...], 4), _tile_lanes(sin_ref[...], 4)
            nrm, rstd, _ = _q_prep(p_ref[...].astype(F32), qw_ref[...], cosf, sinf, oq_ref[...])
            dq = dqr_ref[...].astype(F32) * ATTN_SCALE
            dy = dq * cosf + _partner64(dq * sinf)
            dn = dy * qw_ref[...]
            dp_ref[...] = (rstd * (dn - nrm * _group_mean(dn * nrm, oq_ref[...], passes=1))).astype(BF16)
            dqw_ref[0] += jnp.sum(dy * nrm, axis=0, keepdims=True)

        @pl.when(j > n_q)
        def _():
            gate = p_ref[...].astype(F32)
            sg = _sigmoid(gate)
            dp_ref[...] = (dag_ref[...].astype(F32) * o_ref[...].astype(F32) * (sg * (1.0 + gate * (1.0 - sg)))).astype(BF16)

    first_pass = lambda j, i: jnp.where(j == 0, i, 0)
    acc = pl.BlockSpec((KV_HEADS, rows, SLAB), lambda j, i: (0, first_pass(j, i), 0))
    full = lambda shape: pl.BlockSpec(shape, lambda j, i: (0,) * len(shape))
    tab = pl.BlockSpec((rows, SLAB), lambda j, i: (i, 0))
    out_col = lambda j: jnp.where(j == 0, KV_COL, jnp.where(j <= n_q, j - 1, j))
    return pl.pallas_call(
        body, name="swa_bwd_finish", grid=(2 * n_q + 1, s // rows),
        in_specs=[pl.BlockSpec((rows, GROUP_WIDTH), lambda j, i: (first_pass(j, i), KV_COL)),
                  pl.BlockSpec((rows, GROUP_WIDTH), lambda j, i: (jnp.where(j == 0, 0, i), jnp.where(j <= n_q, q_of(j), j))),
                  pl.BlockSpec((rows, GROUP_WIDTH), lambda j, i: (jnp.where(jnp.logical_and(j >= 1, j <= n_q), i, 0), q_of(j))),
                  pl.BlockSpec((rows, GROUP_WIDTH), lambda j, i: (jnp.where(j > n_q, i, 0), gate_of(j))),
                  pl.BlockSpec((rows, GROUP_WIDTH), lambda j, i: (jnp.where(j > n_q, i, 0), gate_of(j))),
                  acc, acc, acc, acc, full((1, GROUP_WIDTH)), full((1, KV_WIDTH)), tab, tab,
                  full(ones_q.shape), full(ones_kv.shape)],
        out_specs=[pl.BlockSpec((rows, GROUP_WIDTH), lambda j, i: (i, out_col(j))),
                   pl.BlockSpec((1, 8, GROUP_WIDTH), lambda j, i: (q_of(j), 0, 0)), pl.BlockSpec((8, KV_WIDTH), lambda j, i: (0, 0))],
        out_shape=[jax.ShapeDtypeStruct((s, ODD_IN), BF16), jax.ShapeDtypeStruct((n_q, 8, GROUP_WIDTH), F32),
                   jax.ShapeDtypeStruct((8, KV_WIDTH), F32)],
        compiler_params=_params("arbitrary", "arbitrary"),
    )(proj, proj, dqr, o, dag, dkc, dkp, dvc, dvp, qw, kw, cos, sins, ones_q, ones_kv)


def _place():
    x, y, c = lax.axis_index("x"), lax.axis_index("y"), lax.axis_index("c")
    return x, y, c


OTHER_CHIPS = ((1, 0), (0, 1), (1, 1))


def _half_rows(ref, half, rows):
    return ref.at[pl.ds(pl.multiple_of(half * (rows // 2), 8), rows // 2)]


DMA_CHUNK_BYTES = 1 << 20
BF16_TILE_ROWS = 16


def _n_chunks(ref):
    rows = ref.shape[-2]
    nbytes = math.prod(ref.shape) * jnp.dtype(ref.dtype).itemsize
    n = 1
    while 2 * n * DMA_CHUNK_BYTES <= nbytes and rows % (2 * n * BF16_TILE_ROWS) == 0:
        n *= 2
    return n


def _row_chunk(ref, k, n):
    rows = ref.shape[-2] // n
    return ref.at[pl.ds(k * rows, rows)] if len(ref.shape) == 2 else ref.at[:, pl.ds(k * rows, rows)]


def _push(src, dst, send_sem, recv_sem, device_id):
    n = _n_chunks(src)
    for k in range(n):
        pltpu.make_async_remote_copy(src_ref=_row_chunk(src, k, n), dst_ref=_row_chunk(dst, k, n), send_sem=send_sem,
                                     recv_sem=recv_sem, device_id=device_id, device_id_type=MESH).start()
    return pltpu.make_async_remote_copy(src_ref=src, dst_ref=dst, send_sem=send_sem, recv_sem=recv_sem,
                                        device_id=device_id, device_id_type=MESH)


HBM = pl.BlockSpec(memory_space=pltpu.HBM)
SEM = pl.BlockSpec(memory_space=pltpu.SEMAPHORE)
SPLIT_COPY_EFFECT = pltpu.SideEffectType.DATAFLOW_SIDE_EFFECTING


def _in_hbm(a):
    return pltpu.with_memory_space_constraint(a, pltpu.HBM)


def _start_copies(name, arrays, plan, n_copies, after=None):
    n = len(arrays)

    def body(*refs):
        send_sem, recv_sem = refs[-n - 3], refs[-n - 2]
        for k, (src, dst, peer) in enumerate(plan(refs[:n])):
            _push(src, dst, send_sem.at[k], recv_sem.at[k], peer)
        refs[-1][...] = jnp.zeros_like(refs[-1])

    dma = pltpu.SemaphoreType.DMA((n_copies,))
    outs = pl.pallas_call(
        body, name=name,
        out_shape=(dma, dma, *[pltpu.HBM(a.shape, a.dtype) for a in arrays], jax.ShapeDtypeStruct((8, 128), F32)),
        in_specs=[HBM] * n + ([ANY] if after is not None else []),
        out_specs=(SEM, SEM, *[HBM] * n, pl.BlockSpec(memory_space=pltpu.VMEM)),
        input_output_aliases={i: i + 2 for i in range(n)},
        compiler_params=pltpu.CompilerParams(has_side_effects=SPLIT_COPY_EFFECT),
    )(*[_in_hbm(a) for a in arrays], *((after,) if after is not None else ()))
    return outs[0], outs[1], list(outs[2:2 + n]), outs[-1]


def _wait_copies(name, send_sem, recv_sem, arrays, plan, after):
    n = len(arrays)
    after = list(after) if isinstance(after, (list, tuple)) else [after]

    def body(*refs):
        send_ref, recv_ref = refs[n], refs[n + 1]
        for k, (src, dst, peer) in enumerate(plan(refs[:n])):
            cp = pltpu.make_async_remote_copy(src_ref=src, dst_ref=dst, send_sem=send_ref.at[k], recv_sem=recv_ref.at[k],
                                              device_id=peer, device_id_type=MESH)
            cp.wait_send()
            cp.wait_recv()

    return list(pl.pallas_call(
        body, name=name, out_shape=tuple(pltpu.HBM(a.shape, a.dtype) for a in arrays),
        in_specs=[HBM] * n + [SEM, SEM] + [ANY] * len(after), out_specs=tuple([HBM] * n),
        input_output_aliases={i: i for i in range(n)},
        compiler_params=pltpu.CompilerParams(has_side_effects=SPLIT_COPY_EFFECT),
    )(*arrays, send_sem, recv_sem, *after))


def _gather_region(full, kind, chip, half=None):
    if kind == "whole":
        return full.at[chip]
    if kind == "col" or isinstance(kind, tuple):
        part, parts = (0, 1) if kind == "col" else kind[1:]
        rows, width = full.shape[0], full.shape[1] // N_CHIPS
        piece = full.at[:, pl.ds(pl.multiple_of(chip * width + part * (width // parts), LANE_TILE), width // parts)]
    else:
        rows = full.shape[0] // N_CHIPS
        piece = full.at[pl.ds(pl.multiple_of(chip * rows, BF16_TILE_ROWS), rows)]
    return piece if half is None else _half_rows(piece, half, rows)


def _gather_plan(kinds):
    def plan(fulls):
        x, y, c = _place()
        copies = []
        for fx, fy in OTHER_CHIPS:
            for full, kind in zip(fulls, kinds):
                mine = _gather_region(full, kind, 2 * x + y, c)
                copies.append((mine, mine, (x ^ fx, y ^ fy, c)))
        return copies

    return plan


def _gather_start(name, fulls, kinds, after=None):
    return _start_copies(name, list(fulls), _gather_plan(kinds), 3 * len(kinds), after)


def _pass_on_plan(kinds):
    split = [i for i, kind in enumerate(kinds) if kind != "whole"]

    def plan(fulls):
        x, y, c = _place()
        copies = []
        for fx, fy in OTHER_CHIPS:
            chip = 2 * (x ^ fx) + (y ^ fy)
            for i in split:
                landed = _gather_region(fulls[i], kinds[i], chip, c)
                copies.append((landed, landed, (x, y, 1 - c)))
        return copies

    return plan, 3 * len(split)


def _gather_arrive(name, started, kinds, after, arrays=None):
    send_sem, recv_sem, fulls, _ = started
    fulls = _wait_copies(name + "_wait", send_sem, recv_sem, fulls if arrays is None else arrays, _gather_plan(kinds), after)
    plan, count = _pass_on_plan(kinds)
    return _start_copies(name + "_pass_on", fulls, plan, count)


def _gather_done(name, passed_on, kinds, after):
    send_sem, recv_sem, fulls, _ = passed_on
    return _wait_copies(name + "_pass_on_wait", send_sem, recv_sem, fulls, _pass_on_plan(kinds)[0], after)


def _allreduce_small(v):
    def body(v_ref, out_ref, buf, send_sems, recv_sems):
        x, y, c = _place()
        me = 4 * x + 2 * y + c
        buf[me] = v_ref[...]
        copies = []
        for r in range(1, N_DEV):
            peer = (x ^ (r >> 2), y ^ ((r >> 1) & 1), c ^ (r & 1))
            cp = pltpu.make_async_remote_copy(src_ref=v_ref, dst_ref=buf.at[me], send_sem=send_sems.at[r - 1],
                                              recv_sem=recv_sems.at[r - 1], device_id=peer, device_id_type=MESH)
            cp.start()
            copies.append(cp)
        for cp in copies:
            cp.wait_recv()
        for cp in copies:
            cp.wait_send()
        total = buf[0]
        for d in range(1, N_DEV):
            total = total + buf[d]
        out_ref[...] = total

    vm = pl.BlockSpec(memory_space=pltpu.VMEM)
    return pl.pallas_call(
        body, name="allreduce_small", in_specs=[vm], out_specs=vm, out_shape=jax.ShapeDtypeStruct(v.shape, v.dtype),
        scratch_shapes=[pltpu.VMEM((N_DEV,) + v.shape, v.dtype), pltpu.SemaphoreType.DMA((N_DEV - 1,)),
                        pltpu.SemaphoreType.DMA((N_DEV - 1,))],
        compiler_params=pltpu.CompilerParams(has_side_effects=True),
    )(v)


def _exchange_plan(n):
    def plan(refs):
        x, y, c = _place()
        copies = []
        for g, theirs in zip(refs[:n], refs[n:]):
            half = g.shape[1] // 2
            src = g.at[:, pl.ds(pl.multiple_of((1 - c) * half, BF16_TILE_ROWS), half)]
            copies.append((src, theirs, (x, y, 1 - c)))
        return copies

    return plan


SUM_ROWS = 512


def _pair_sum(g, theirs, core, name):
    pieces, half, cols = theirs.shape
    tr = min(half, SUM_ROWS)
    per = half // tr

    def body(core_ref, g_ref, t_ref, o_ref):
        del core_ref
        o_ref[...] = (g_ref[...].astype(F32) + t_ref[...].astype(F32)).astype(BF16)

    spec = pl.BlockSpec((1, tr, cols), lambda p, i, core_ref: (p, i, 0))
    return pl.pallas_call(
        body, name=name, out_shape=jax.ShapeDtypeStruct(theirs.shape, BF16),
        grid_spec=pltpu.PrefetchScalarGridSpec(
            num_scalar_prefetch=1, grid=(pieces, per),
            in_specs=[pl.BlockSpec((1, tr, cols), lambda p, i, core_ref: (p, core_ref[0] * per + i, 0)), spec],
            out_specs=spec),
        compiler_params=_params("parallel", "parallel"))(core, g, theirs)


def _scatter_plan(n):
    def plan(refs):
        parts, stacks = refs[:n], refs[n:]
        x, y, c = _place()
        copies = []
        for fx, fy in OTHER_CHIPS:
            chip = 2 * (x ^ fx) + (y ^ fy)
            for part, stack in zip(parts, stacks):
                if part.shape[0] == N_CHIPS:
                    piece = part.at[chip]
                else:
                    width = part.shape[2] // N_CHIPS
                    piece = part.at[0].at[:, pl.ds(pl.multiple_of(chip * width, 128), width)]
                copies.append((piece, stack.at[2 * x + y], (x ^ fx, y ^ fy, c)))
        return copies

    return plan


def _scatter_start(name, parts, after=None):
    def landing(a):
        return (N_CHIPS, a.shape[1], a.shape[2] if a.shape[0] == N_CHIPS else a.shape[2] // N_CHIPS)

    stacks = [lax.empty(landing(a), a.dtype) for a in parts]
    return _start_copies(name, list(parts) + stacks, _scatter_plan(len(parts)), 3 * len(parts), after)


def _scatter_finish(name, started, after):
    send_sem, recv_sem, arrays, _ = started
    n = len(arrays) // 2
    arrays = _wait_copies(name + "_wait", send_sem, recv_sem, arrays, _scatter_plan(n), after)
    return arrays[:n], arrays[n:]


def _sum_chips(part, stack, place, name):
    _, r, c = stack.shape
    tr = min(r, SUM_ROWS)
    per = r // tr

    def body(place_ref, own_ref, a_ref, b_ref, c_ref, o_ref):
        del place_ref
        total = own_ref[0].astype(F32)
        for ref in (a_ref, b_ref, c_ref):
            total = total + ref[0].astype(F32)
        o_ref[...] = total

    if part.shape[0] == N_CHIPS:
        own = pl.BlockSpec((1, tr, c), lambda i, pr: (pr[1], i, 0))
    else:
        own = pl.BlockSpec((1, tr, c), lambda i, pr: (0, i, pr[1]))
    other = lambda flip: pl.BlockSpec((1, tr, c), lambda i, pr: (pr[1] ^ flip, i, 0))
    return pl.pallas_call(
        body, name=name, out_shape=jax.ShapeDtypeStruct((2 * r, c), F32),
        grid_spec=pltpu.PrefetchScalarGridSpec(
            num_scalar_prefetch=1, grid=(per,), in_specs=[own, other(2), other(1), other(3)],
            out_specs=pl.BlockSpec((tr, c), lambda i, pr: (pr[0] * per + i, 0))),
        compiler_params=_params("parallel"))(place, part, stack, stack, stack)


def _own_half_plan(fulls):
    x, y, c = _place()
    return [(mine, mine, (x, y, 1 - c)) for mine in (_half_rows(full, c, full.shape[0]) for full in fulls)]


MM = dict(tm=2048, tn=1024, tk=2048)
MM_LONG_K = dict(tm=1024, tn=1024, tk=4096)


def _local_step(x, target, ev_norm_w, q_norm_w, k_norm_w, sinks, own_first, weights_first, late_arrived, weights_late, grads_out):
    s = x.shape[0]
    cos_r, sin_r = _rope_tables(s, RET_DIM)
    cos_a, sin_a = _rope_tables(s, HEAD_DIM)
    cos_a = np.tile(cos_a, (1, 4))
    sins_a = np.tile(np.concatenate([-sin_a, sin_a], axis=1), (1, 2))
    tables = _retention_tables()
    ones_q, ones_kv = _block_diag_mean(min(GROUP_WIDTH, MXU_WIDTH)), _block_diag_mean(min(KV_WIDTH, MXU_WIDTH))
    qw_g = jnp.tile(q_norm_w, (1, GROUP_WIDTH // HEAD_DIM))
    kw_kv = jnp.tile(k_norm_w, (1, KV_WIDTH // HEAD_DIM))
    sinks1 = sinks.reshape(Q_HEADS)

    own_w_in0, own_block, start_token = own_first
    h0 = _rmsnorm(x, ev_norm_w, "norm0", after=start_token)
    shifted = dict(shift=own_block * (own_w_in0.shape[1] // MM["tn"]), total=EVEN_IN // MM["tn"], tm=MM["tm"], tn=MM["tn"],
                   out_dtype=BF16)
    own_blocks = own_w_in0.shape[1] // MM["tn"]
    proj0 = _mm_shifted(h0, own_w_in0, b_shifted=False, first=0, count=own_blocks, name="proj0_own", **shifted)
    for part in range(own_blocks):
        w_in0, small_weights, token = weights_first(part, proj0)
        if small_weights is not None:
            conv_w, od_norm_w = small_weights
        proj0 = _mm_shifted(h0, w_in0, b_shifted=True, first=own_blocks + part, stride=own_blocks,
                            count=shifted["total"] // own_blocks - 1, name=f"proj0_rest{part}", into=proj0, after=token, **shifted)
    o_ret, cat, states = _retention_fwd(proj0, cos_r, sin_r, tables, after=late_arrived(proj0))
    cat = _conv_fwd(proj0, conv_w, cat)
    w_out0, w_in1, w_out1 = weights_late(cat)

    def residual_and_norm(prod, x_ref, w_ref, x1_ref, h1_ref):
        x1v = x_ref[...] + prod
        x1_ref[...] = x1v
        rstd = lax.rsqrt(jnp.mean(x1v * x1v, axis=-1, keepdims=True) + EPS)
        h1_ref[...] = (x1v * rstd * w_ref[...]).astype(BF16)

    def residual_and_loss(prod, x1_ref, t_ref, dyb_ref, sq_ref):
        diff = (x1_ref[...] + prod) - t_ref[...]
        dyb_ref[...] = (diff * (1.0 / D_MODEL)).astype(BF16)

        @pl.when(pl.program_id(0) == 0)
        def _():
            sq_ref[...] = jnp.zeros_like(sq_ref)

        sq_ref[...] += jnp.sum(jnp.sum(diff * diff, axis=1, keepdims=True), axis=0, keepdims=True)

    act = lambda dt: jax.ShapeDtypeStruct((s, D_MODEL), dt)
    x1, h1 = _mm_rows(cat, w_out0, [x], [od_norm_w], [act(F32), act(BF16)], residual_and_norm, tm=min(s, 512), name="out0")
    proj1 = _mm(h1, w_in1, mode="nn", out_dtype=BF16, name="proj1", tm=2048, tn=1536, tk=2048)
    qk = _qk_prep(proj1, qw_g, kw_kv, cos_a, sins_a, ones_q, ones_kv)
    ag, o_att = _swa_fwd(qk, proj1, sinks1)
    dy_b, sq = _mm_rows(ag, w_out1, [x1, target], [], [act(BF16), jax.ShapeDtypeStruct((8, 128), F32)],
                        residual_and_loss, tm=min(s, 512), name="out1")

    g_w_out1 = _mm(ag, dy_b, mode="tn", out_dtype=BF16, name="g_w_out1", **MM_LONG_K)
    dag = _mm(dy_b, w_out1, mode="nt", out_dtype=BF16, name="d_ag", **MM)
    dqr, dkc, dkp, dvc, dvp, dsink = _swa_bwd(qk, proj1, dag, sinks1)
    dproj1, dqw, dkw = _swa_bwd_finish(proj1, dqr, o_att, dag, dkc, dkp, dvc, dvp, qw_g, kw_kv, cos_a, sins_a, ones_q, ones_kv)
    g_w_in1 = _mm(h1, dproj1, mode="tn", out_dtype=BF16, name="g_w_in1", tm=1024, tn=768, tk=4096)
    dh1 = _mm(dproj1, w_in1, mode="nt", out_dtype=BF16, name="d_h1", tm=1024, tn=1024, tk=ODD_IN)
    dx1_b, g_norm1 = _rmsnorm_bwd(x1, od_norm_w, dh1, dy_b, "norm1_bwd", BF16)

    g_w_out0 = _mm(cat, dx1_b, mode="tn", out_dtype=BF16, name="g_w_out0", **MM_LONG_K)
    token = grads_out.begin("upper", (("od_w_in", g_w_in1, "col"), ("od_w_out", g_w_out1, "row"), ("ev_w_out", g_w_out0, "row")))
    dcat = _mm(dx1_b, w_out0, mode="nt", out_dtype=BF16, name="d_cat", after=token, **MM)
    token = grads_out.send("upper", dcat)
    dproj0 = _retention_bwd(proj0, o_ret, states, dcat, cos_r, sin_r, tables)
    dproj0, g_conv = _conv_bwd(proj0, dcat, conv_w, dproj0)
    g_w_in0 = _mm(h0, dproj0, mode="tn", out_dtype=BF16, name="g_w_in0", after=token, **MM_LONG_K)
    token = grads_out.begin("in0", (("ev_w_in", g_w_in0, "col"),))
    half_blocks = s // (2 * MM_LONG_K["tm"])
    dh0 = _mm(dproj0, w_in0, mode="nt", out_dtype=BF16, name="d_h0_top", after=token, m_blocks=(0, half_blocks), **MM_LONG_K)
    dh0 = _mm(dproj0, w_in0, mode="nt", out_dtype=BF16, name="d_h0_bottom", after=grads_out.send("in0", dh0),
              m_blocks=(half_blocks, half_blocks), into=dh0, **MM_LONG_K)
    grad_x, g_norm0 = _rmsnorm_bwd(x, ev_norm_w, dh0, dx1_b, "norm0_bwd", F32)

    g_qw = dqw[:, 0, :].reshape(Q_HEADS, HEAD_DIM).sum(axis=0)
    g_kw = dkw[0].reshape(KV_HEADS, HEAD_DIM).sum(axis=0)
    g_sinks = dsink[:, :, 0].reshape(Q_HEADS)
    small = dict(ev_norm=g_norm0[0], od_norm=g_norm1[0], conv=g_conv[:3], qw=g_qw, kw=g_kw, sinks=g_sinks)
    return sq[0, 0], grad_x, small


LANE_TILE = 128


def _pack_small_grads(small, sq):
    pad = lambda v, n: jnp.pad(v, (0, n - v.shape[0]))
    tail = jnp.concatenate([pad(small["qw"], LANE_TILE), pad(small["kw"], LANE_TILE), small["sinks"]])
    rows = [small["ev_norm"], small["od_norm"]] + [small["conv"][t] for t in range(3)] + [tail, sq.reshape(1)]
    rows += [jnp.zeros((1,), F32)] * (8 - len(rows))
    return jnp.stack([pad(r, D_MODEL) for r in rows])


def _adamw_small(tot, chip, params):
    n = len(params)

    def body(chip_ref, tot_ref, *refs):
        ins, outs = refs[:3 * n], refs[3 * n:]
        c = chip_ref[0]

        def own(rows, width):
            blocks = [tot_ref[rows, k * width:(k + 1) * width] for k in range(N_CHIPS)]
            g = blocks[-1]
            for k in reversed(range(N_CHIPS - 1)):
                g = jnp.where(c == k, blocks[k], g)
            return g

        grads = [tot_ref[0:1, :], own(slice(2, 5), CONV_WIDTH // N_CHIPS), own(slice(1, 2), D_MODEL // N_CHIPS),
                 tot_ref[5:6, 0:HEAD_DIM], tot_ref[5:6, LANE_TILE:LANE_TILE + HEAD_DIM],
                 tot_ref[5:6, 2 * LANE_TILE:2 * LANE_TILE + Q_HEADS]]
        for i, g in enumerate(grads):
            w_ref, m_ref, v_ref = ins[3 * i:3 * i + 3]
            g_out, d_out, nm_out, nv_out = outs[4 * i:4 * i + 4]
            at = (0,) if len(w_ref.shape) == 3 else ()
            delta, nm, nv = _adamw_math(w_ref[at] if at else w_ref[...], g, m_ref[at] if at else m_ref[...],
                                        v_ref[at] if at else v_ref[...])
            for ref, val in ((g_out, g), (d_out, delta), (nm_out, nm), (nv_out, nv)):
                if at:
                    ref[0] = val
                else:
                    ref[...] = val

    vm = pl.BlockSpec(memory_space=pltpu.VMEM)
    flat = [a for p in params for a in p]
    outs = pl.pallas_call(
        body, name="adamw_small", in_specs=[pl.BlockSpec(memory_space=pltpu.SMEM), vm] + [vm] * len(flat),
        out_specs=[vm] * (4 * n), out_shape=[jax.ShapeDtypeStruct(p[0].shape, F32) for p in params for _ in range(4)],
    )(chip, tot, *flat)
    return [tuple(outs[4 * i:4 * i + 4]) for i in range(n)]


class _ReduceScatter:
    def __init__(self, place):
        self.place = place
        self.exchanging = {}
        self.started = []
        self.sharing = {}

    def begin(self, tag, grads):
        pieces = [g[None] if kind == "col" else g.reshape(N_CHIPS, g.shape[0] // N_CHIPS, g.shape[1]) for _, g, kind in grads]
        landing = [lax.empty((p.shape[0], p.shape[1] // 2, p.shape[2]), p.dtype) for p in pieces]
        started = _start_copies("exchange_" + tag, pieces + landing, _exchange_plan(len(pieces)), len(pieces))
        self.exchanging[tag] = ([nm for nm, _, _ in grads], started)
        return started[3]

    def send(self, tag, after):
        group, (send_sem, recv_sem, arrays, _) = self.exchanging[tag]
        n = len(group)
        arrays = _wait_copies("exchange_" + tag + "_wait", send_sem, recv_sem, arrays, _exchange_plan(n), after)
        parts = [_pair_sum(g, t, self.place[:1], "pair_sum_" + nm) for g, t, nm in zip(arrays[:n], arrays[n:], group)]
        started = _scatter_start("scatter_" + tag, parts)
        token = self.sum_up(started[3])
        self.started.append((tag, group, started))
        return started[3] if token is None else token

    def sum_up(self, after):
        token = None
        for tag, group, started in self.started:
            if tag not in self.sharing:
                parts, stacks = _scatter_finish("scatter_" + tag, started, after)
                sums = [_sum_chips(p, s, self.place, "chip_sum_" + nm) for p, s, nm in zip(parts, stacks, group)]
                self.sharing[tag] = (group, _start_copies("share_" + tag, sums, _own_half_plan, len(sums)))
                token = self.sharing[tag][1][3]
        return token

    def result(self, tag, after):
        group, (send_sem, recv_sem, sums, _) = self.sharing[tag]
        return dict(zip(group, _wait_copies("share_" + tag + "_wait", send_sem, recv_sem, sums, _own_half_plan, after)))


def kernel(x, ev_norm_w, ev_w_in, ev_conv_w, ev_w_out, od_norm_w, od_w_in, od_q_norm_w, od_k_norm_w, od_sinks, od_w_out, loss_target, m_ev_norm_w, m_ev_w_in, m_ev_conv_w, m_ev_w_out, m_od_norm_w, m_od_w_in, m_od_q_norm_w, m_od_k_norm_w, m_od_sinks, m_od_w_out, v_ev_norm_w, v_ev_w_in, v_ev_conv_w, v_ev_w_out, v_od_norm_w, v_od_w_in, v_od_q_norm_w, v_od_k_norm_w, v_od_sinks, v_od_w_out):
    my_chip = 2 * lax.axis_index("x") + lax.axis_index("y")
    place = jnp.stack([lax.axis_index("c"), my_chip]).astype(jnp.int32)
    shard_w = D_MODEL // N_CHIPS
    conv_shard = CONV_WIDTH // N_CHIPS

    small_in = jnp.zeros((8, shard_w), F32)
    small_in = small_in.at[0].set(od_norm_w[0]).at[1:4, :conv_shard].set(ev_conv_w[0])
    small_in = lax.dynamic_update_slice(jnp.zeros((N_CHIPS, 8, shard_w), F32), small_in[None], (my_chip, 0, 0))
    chip = place[1:]
    first_kinds, second_kinds, late_kinds = (("col", 0, 2), "whole"), (("col", 1, 2),), ("row", "col", "row")
    w_in0_own_place, w_in0_shard = _cast_into_gathered(ev_w_in[0], "col", chip, "cast_w_in0", keep_shard=True)
    first = _gather_start("gather_first", [w_in0_own_place, small_in], first_kinds)
    second = _gather_start("gather_second", [first[2][0]], second_kinds)
    late_own = [_cast_into_gathered(ev_w_out[0], "row", chip, "cast_w_out0"),
                _cast_into_gathered(od_w_in[0], "col", chip, "cast_w_in1"),
                _cast_into_gathered(od_w_out[0], "row", chip, "cast_w_out1")]
    late, w_in0_so_far = [], []

    def weights_first(part, after):
        if part == 0:
            passed_on = _gather_arrive("gather_first", first, first_kinds, after, arrays=[second[2][0], first[2][1]])
            w_in0, small_all = _gather_done("gather_first", passed_on, first_kinds, [passed_on[3]] + late_own)
            w_in0_so_far.append(w_in0)
            late.append(_gather_start("gather_late", late_own, late_kinds, after=small_all))
            od_norm_full = small_all[:, 0, :].reshape(1, D_MODEL)
            conv_full = jnp.transpose(small_all[:, 1:4, :conv_shard], (1, 0, 2)).reshape(3, CONV_WIDTH)
            return w_in0, (conv_full, od_norm_full), late[0][3]
        passed_on = _gather_arrive("gather_second", second, second_kinds, after, arrays=w_in0_so_far)
        (w_in0,) = _gather_done("gather_second", passed_on, second_kinds, passed_on[3])
        return w_in0, None, None

    def late_arrived(after):
        late.append(_gather_arrive("gather_late", late[0], late_kinds, after))
        return late[1][3]

    def weights_late(after):
        return _gather_done("gather_late", late[1], late_kinds, after)

    reduce_scatter = _ReduceScatter(place)
    sq, grad_x, small = _local_step(x[0], loss_target[0], ev_norm_w, od_q_norm_w, od_k_norm_w, od_sinks,
                                    (w_in0_shard, chip, second[3]), weights_first, late_arrived, weights_late,
                                    reduce_scatter)

    token = reduce_scatter.sum_up(grad_x)
    upd = {}
    upper = reduce_scatter.result("upper", token)
    upd["od_w_in"] = _adamw(od_w_in[0], upper["od_w_in"], m_od_w_in[0], v_od_w_in[0], "adamw_od_w_in")
    upd["od_w_out"] = _adamw(od_w_out[0], upper["od_w_out"], m_od_w_out[0], v_od_w_out[0], "adamw_od_w_out")
    upd["ev_w_out"] = _adamw(ev_w_out[0], upper["ev_w_out"], m_ev_w_out[0], v_ev_w_out[0], "adamw_ev_w_out")
    g_ev_w_in = reduce_scatter.result("in0", [upd[nm][1] for nm in ("od_w_in", "od_w_out", "ev_w_out")])["ev_w_in"]
    upd["ev_w_in"] = _adamw(ev_w_in[0], g_ev_w_in, m_ev_w_in[0], v_ev_w_in[0], "adamw_ev_w_in")
    tot = _allreduce_small(_pack_small_grads(small, sq))
    loss = 0.5 * tot[6, 0] / D_MODEL

    smalls = (("ev_norm_w", ev_norm_w, m_ev_norm_w, v_ev_norm_w), ("ev_conv_w", ev_conv_w, m_ev_conv_w, v_ev_conv_w),
              ("od_norm_w", od_norm_w, m_od_norm_w, v_od_norm_w), ("od_q_norm_w", od_q_norm_w, m_od_q_norm_w, v_od_q_norm_w),
              ("od_k_norm_w", od_k_norm_w, m_od_k_norm_w, v_od_k_norm_w), ("od_sinks", od_sinks, m_od_sinks, v_od_sinks))
    for (nm, _, _, _), result in zip(smalls, _adamw_small(tot, chip, [p[1:] for p in smalls])):
        upd[nm] = result
    for nm in ("ev_w_in", "ev_w_out", "od_w_in", "od_w_out"):
        upd[nm] = tuple(u[None] for u in upd[nm])
    order = ("ev_norm_w", "ev_w_in", "ev_conv_w", "ev_w_out", "od_norm_w", "od_w_in", "od_q_norm_w", "od_k_norm_w", "od_sinks", "od_w_out")
    return (loss, grad_x[None], *[upd[nm][0] for nm in order], *[upd[nm][1] for nm in order],
            *[upd[nm][2] for nm in order], *[upd[nm][3] for nm in order])
```

```python
import math

import jax
import jax.numpy as jnp
import numpy as np
from jax import lax
from jax.experimental import pallas as pl
from jax.experimental.pallas import tpu as pltpu

F32 = jnp.float32
BF16 = jnp.bfloat16

D_MODEL = 2048
RET_HEADS = 4
RET_DIM = 256
RET_WIDTH = 1024
CONV_WIDTH = 1024
EVEN_IN = 8192
Q_HEADS = 32
HEAD_DIM = 64
KV_HEADS = 4
KV_WIDTH = 256
ATTN_WIDTH = 2048
ODD_IN = 4608
BLK = 128
ROPE_THETA = 10000.0
EPS = 1e-6
ADAM_LR = 0.001
ADAM_B1 = 0.9
ADAM_B2 = 0.999
ADAM_EPS = 1e-08
ADAM_WD = 0.01
ADAM_STEP = 10
N_CHIPS = 4
N_DEV = 8
VMEM_LIMIT_BYTES = 56 * 1024 * 1024
MESH = pl.DeviceIdType.MESH
ANY = pl.BlockSpec(memory_space=pl.ANY)


def _params(*sem):
    return pltpu.CompilerParams(dimension_semantics=sem, vmem_limit_bytes=VMEM_LIMIT_BYTES)


def _dot(a, b):
    return jnp.dot(a, b, preferred_element_type=F32)


def _dot_nt(a, b):
    return lax.dot_general(a, b, (((1,), (1,)), ((), ())), preferred_element_type=F32)


def _dot_tn(a, b):
    return lax.dot_general(a, b, (((0,), (0,)), ((), ())), preferred_element_type=F32)


def _sigmoid(x):
    return 1.0 / (1.0 + jnp.exp(-x))


def _mm(a, b, *, mode, tm, tn, tk, out_dtype, name, add=None, after=None, m_blocks=None, into=None):
    if mode == "nn":
        (m, k), n = a.shape, b.shape[1]
    elif mode == "nt":
        (m, k), n = a.shape, b.shape[0]
    else:
        (k, m), n = a.shape, b.shape[1]
    tm, tn, tk = min(tm, m), min(tn, n), min(tk, k)
    assert m % tm == 0 and n % tn == 0 and k % tk == 0, (name, m, n, k)
    nk = k // tk
    first_m, count_m = (0, m // tm) if m_blocks is None else m_blocks
    dot = {"nn": _dot, "nt": _dot_nt, "tn": _dot_tn}[mode]
    a_spec = (pl.BlockSpec((tk, tm), lambda i, j, kk: (kk, i + first_m)) if mode == "tn"
              else pl.BlockSpec((tm, tk), lambda i, j, kk: (i + first_m, kk)))
    b_spec = (pl.BlockSpec((tn, tk), lambda i, j, kk: (j, kk)) if mode == "nt"
              else pl.BlockSpec((tk, tn), lambda i, j, kk: (kk, j)))
    o_spec = pl.BlockSpec((tm, tn), lambda i, j, kk: (i + first_m, j))
    has_add = add is not None

    def body(*refs):
        a_ref, b_ref = refs[0], refs[1]
        add_ref = refs[2] if has_add else None
        o_ref, acc_ref = refs[-2], refs[-1]
        p = dot(a_ref[...], b_ref[...])

        def finish(total):
            if has_add:
                total = total + add_ref[...].astype(F32)
            o_ref[...] = total.astype(out_dtype)

        if nk == 1:
            finish(p)
        else:
            kk = pl.program_id(2)

            @pl.when(kk == 0)
            def _():
                acc_ref[...] = p

            @pl.when(jnp.logical_and(kk > 0, kk < nk - 1))
            def _():
                acc_ref[...] += p

            @pl.when(kk == nk - 1)
            def _():
                finish(acc_ref[...] + p)

    extra = [arr for arr in (after, into) if arr is not None]
    in_specs = [a_spec, b_spec] + ([o_spec] if has_add else []) + [ANY] * len(extra)
    args = (a, b) + ((add,) if has_add else ()) + tuple(extra)
    return pl.pallas_call(
        body, name=name, grid=(count_m, n // tn, nk), in_specs=in_specs, out_specs=o_spec,
        out_shape=jax.ShapeDtypeStruct((m, n), out_dtype),
        input_output_aliases={len(args) - 1: 0} if into is not None else {},
        scratch_shapes=[pltpu.VMEM((tm, tn) if nk > 1 else (8, 128), F32)],
        compiler_params=_params("parallel", "parallel", "arbitrary"),
    )(*args)


def _mm_shifted(a, b, shift, *, b_shifted, first, count, total, tm, tn, out_dtype, name, stride=1, into=None, after=None):
    m, k = a.shape
    tm = min(tm, m)
    assert m % tm == 0
    col = lambda j, shift_ref: (shift_ref[0] + first + stride * j) % total
    extra = [arr for arr in (into, after) if arr is not None]

    def body(shift_ref, a_ref, b_ref, *rest):
        del shift_ref
        rest[-1][...] = _dot(a_ref[...], b_ref[...]).astype(out_dtype)

    return pl.pallas_call(
        body, name=name, out_shape=jax.ShapeDtypeStruct((m, total * tn), out_dtype),
        grid_spec=pltpu.PrefetchScalarGridSpec(
            num_scalar_prefetch=1, grid=(m // tm, count),
            in_specs=[pl.BlockSpec((tm, k), lambda i, j, s: (i, 0)),
                      pl.BlockSpec((k, tn), (lambda i, j, s: (0, col(j, s))) if b_shifted else (lambda i, j, s: (0, j)))]
            + [ANY] * len(extra),
            out_specs=pl.BlockSpec((tm, tn), lambda i, j, s: (i, col(j, s)))),
        input_output_aliases={3: 0} if into is not None else {},
        compiler_params=_params("parallel", "arbitrary"))(shift, a, b, *extra)


def _mm_rows(a, b, rows_in, vecs_in, out_shapes, epilogue, *, tm, name):
    m, k = a.shape
    n = b.shape[1]
    assert m % tm == 0
    row = pl.BlockSpec((tm, n), lambda i: (i, 0))

    def body(a_ref, b_ref, *rest):
        epilogue(_dot(a_ref[...], b_ref[...]), *rest)

    out_specs = [row if tuple(s.shape) == (m, n) else pl.BlockSpec(s.shape, lambda i: (0, 0)) for s in out_shapes]
    return pl.pallas_call(
        body, name=name, grid=(m // tm,),
        in_specs=[pl.BlockSpec((tm, k), lambda i: (i, 0)), pl.BlockSpec((k, n), lambda i: (0, 0))] + [row] * len(rows_in)
        + [pl.BlockSpec((1, n), lambda i: (0, 0))] * len(vecs_in),
        out_specs=out_specs, out_shape=out_shapes, compiler_params=_params("arbitrary"),
    )(a, b, *rows_in, *vecs_in)


def _cast_into_gathered(w, kind, chip, name, keep_shard=False):
    r, c = w.shape
    tr = min(r, 512)
    per = r // tr

    def body(chip_ref, w_ref, *outs):
        del chip_ref
        for o_ref in outs:
            o_ref[...] = w_ref[...].astype(BF16)

    if kind == "col":
        shape, out_map = (r, N_CHIPS * c), (lambda i, chip_ref: (i, chip_ref[0]))
    else:
        shape, out_map = (N_CHIPS * r, c), (lambda i, chip_ref: (chip_ref[0] * per + i, 0))
    plain = pl.BlockSpec((tr, c), lambda i, chip_ref: (i, 0))
    out = pl.pallas_call(
        body, name=name,
        out_shape=[jax.ShapeDtypeStruct(shape, BF16)] + ([jax.ShapeDtypeStruct((r, c), BF16)] if keep_shard else []),
        grid_spec=pltpu.PrefetchScalarGridSpec(
            num_scalar_prefetch=1, grid=(per,), in_specs=[plain],
            out_specs=[pl.BlockSpec((tr, c), out_map)] + ([plain] if keep_shard else [])),
        compiler_params=_params("parallel"))(chip, w)
    return out if keep_shard else out[0]


NORM_ROWS = 512


def _rmsnorm(x, w, name, after=None):
    s, d = x.shape
    tr = NORM_ROWS

    def body(x_ref, w_ref, *rest):
        xv = x_ref[...]
        rstd = lax.rsqrt(jnp.mean(xv * xv, axis=-1, keepdims=True) + EPS)
        rest[-1][...] = (xv * rstd * w_ref[...]).astype(BF16)

    return pl.pallas_call(
        body, name=name, grid=(s // tr,),
        in_specs=[pl.BlockSpec((tr, d), lambda i: (i, 0)), pl.BlockSpec((1, d), lambda i: (0, 0))]
        + ([ANY] if after is not None else []),
        out_specs=pl.BlockSpec((tr, d), lambda i: (i, 0)),
        out_shape=jax.ShapeDtypeStruct((s, d), BF16), compiler_params=_params("parallel"),
    )(x, w, *((after,) if after is not None else ()))


def _rmsnorm_bwd(x, w, dh, dres, name, out_dtype):
    s, d = x.shape
    tr = NORM_ROWS

    def body(x_ref, w_ref, dh_ref, dres_ref, dx_ref, dw_ref):
        xv = x_ref[...]
        rstd = lax.rsqrt(jnp.mean(xv * xv, axis=-1, keepdims=True) + EPS)
        nrm = xv * rstd
        dhv = dh_ref[...].astype(F32)
        dn = dhv * w_ref[...]
        dx = dres_ref[...].astype(F32) + rstd * (dn - nrm * jnp.mean(dn * nrm, axis=-1, keepdims=True))
        dx_ref[...] = dx.astype(out_dtype)

        @pl.when(pl.program_id(0) == 0)
        def _():
            dw_ref[...] = jnp.zeros_like(dw_ref)

        dw_ref[...] += jnp.sum(dhv * nrm, axis=0, keepdims=True)

    row = pl.BlockSpec((tr, d), lambda i: (i, 0))
    return pl.pallas_call(
        body, name=name, grid=(s // tr,),
        in_specs=[row, pl.BlockSpec((1, d), lambda i: (0, 0)), row, row],
        out_specs=[row, pl.BlockSpec((8, d), lambda i: (0, 0))],
        out_shape=[jax.ShapeDtypeStruct((s, d), out_dtype), jax.ShapeDtypeStruct((8, d), F32)],
        compiler_params=_params("arbitrary"))(x, w, dh, dres)


def _adamw_math(w, g, m, v):
    nm = ADAM_B1 * m + (1.0 - ADAM_B1) * g
    nv = ADAM_B2 * v + (1.0 - ADAM_B2) * (g * g)
    m_hat = nm / (1.0 - ADAM_B1 ** ADAM_STEP)
    v_hat = nv / (1.0 - ADAM_B2 ** ADAM_STEP)
    return -ADAM_LR * (m_hat / (jnp.sqrt(v_hat) + ADAM_EPS) + ADAM_WD * w), nm, nv


def _adamw(w, g, m, v, name):
    return _adamw_same_shape([(w, g, m, v)], name)[0]


def _adamw_same_shape(params, name):
    n = len(params)
    r, c = params[0][0].shape
    tr = min(r, 256 // n)
    assert r % tr == 0 and all(a.shape == (r, c) for p in params for a in p)

    def body(*refs):
        for k in range(n):
            w_ref, g_ref, m_ref, v_ref = refs[4 * k:4 * k + 4]
            g_out, d_ref, nm_ref, nv_ref = refs[4 * (n + k):4 * (n + k) + 4]
            gv = g_ref[...]
            g_out[...] = gv
            d_ref[...], nm_ref[...], nv_ref[...] = _adamw_math(w_ref[...], gv, m_ref[...], v_ref[...])

    spec = pl.BlockSpec((tr, c), lambda i: (i, 0))
    shp = jax.ShapeDtypeStruct((r, c), F32)
    outs = pl.pallas_call(body, name=name, grid=(r // tr,), in_specs=[spec] * (4 * n), out_specs=[spec] * (4 * n),
                          out_shape=[shp] * (4 * n), compiler_params=_params("parallel"))(*[a for p in params for a in p])
    return [tuple(outs[4 * k:4 * k + 4]) for k in range(n)]


def _rope_tables(s, dim):
    inv = (1.0 / (ROPE_THETA ** (np.arange(0, dim, 2, dtype=np.float64) / dim))).astype(np.float32)
    ang = (np.arange(s, dtype=np.float32)[:, None] * inv[None, :]).astype(np.float64)
    return np.cos(ang).astype(np.float32), np.sin(ang).astype(np.float32)


def _rope_half(x, cos, sin):
    h = x.shape[1] // 2
    x1, x2 = x[:, :h], x[:, h:]
    return jnp.concatenate([x1 * cos - x2 * sin, x2 * cos + x1 * sin], axis=1)


def _unrope_half(dy, cos, sin):
    h = dy.shape[1] // 2
    d1, d2 = dy[:, :h], dy[:, h:]
    return jnp.concatenate([d1 * cos + d2 * sin, d2 * cos - d1 * sin], axis=1)


def _lane(shape):
    return lax.broadcasted_iota(jnp.int32, shape, 1)


def _partner64(x):
    w = x.shape[1]
    first = (_lane(x.shape) % HEAD_DIM) < (HEAD_DIM // 2)
    return jnp.where(first, pltpu.roll(x, w - HEAD_DIM // 2, 1), pltpu.roll(x, HEAD_DIM // 2, 1))


def _tile_lanes(t, reps):
    return t if reps == 1 else jnp.concatenate([t] * reps, axis=1)


MXU_WIDTH = 256


def _group_mean(x, ones_bd, passes=2):
    width, tile = x.shape[1], ones_bd.shape[0]
    if width > tile:
        return jnp.concatenate([_group_mean(x[:, c:c + tile], ones_bd, passes) for c in range(0, width, tile)], axis=1)
    hi = x.astype(BF16)
    if passes == 1:
        return _dot(hi, ones_bd)
    lo = (x - hi.astype(F32)).astype(BF16)
    return _dot(hi, ones_bd) + _dot(lo, ones_bd)


def _block_diag_mean(width):
    idx = jnp.arange(width) // HEAD_DIM
    return jnp.where(idx[:, None] == idx[None, :], 1.0 / HEAD_DIM, 0.0).astype(BF16)


RET_STEP = 4


def _retention_tables():
    h = RET_HEADS
    log_g = jnp.log(1.0 - 2.0 ** (-5.0 - jnp.arange(h, dtype=F32)))
    idx = jnp.arange(BLK, dtype=F32)
    diff = idx[:, None] - idx[None, :]
    intra = jnp.where(diff >= 0, jnp.exp(log_g[:, None, None] * jnp.maximum(diff, 0.0)), 0.0).astype(F32)
    q_dec = jnp.exp(log_g[:, None] * (idx[None, :] + 1.0)).astype(F32)[:, :, None]
    k_dec = jnp.exp(log_g[:, None] * (BLK - 1.0 - idx[None, :])).astype(F32)[:, :, None]
    chunk_dec = jnp.exp(log_g * BLK).astype(F32)[:, None, None]
    return intra, q_dec, k_dec, chunk_dec


def _retention_fwd(proj, cos, sin, tables, after=None):
    s = proj.shape[0]
    nc = s // BLK
    intra, q_dec, k_dec, chunk_dec = tables

    def body(p_ref, cos_ref, sin_ref, in_ref, qd_ref, kd_ref, cd_ref, *rest):
        o_ref, cat_ref, st_ref, state = rest[-4:]

        @pl.when(pl.program_id(0) == 0)
        def _():
            state[...] = jnp.zeros_like(state)

        for c in range(RET_STEP):
            rows = slice(c * BLK, (c + 1) * BLK)
            cosv, sinv = cos_ref[rows, :], sin_ref[rows, :]
            for h in range(RET_HEADS):
                c0 = h * RET_DIM
                q = p_ref[rows, c0:c0 + RET_DIM].astype(F32)
                k = p_ref[rows, RET_WIDTH + c0:RET_WIDTH + c0 + RET_DIM].astype(F32)
                v = p_ref[rows, 2 * RET_WIDTH + c0:2 * RET_WIDTH + c0 + RET_DIM]
                g = p_ref[rows, 3 * RET_WIDTH + c0:3 * RET_WIDTH + c0 + RET_DIM].astype(F32)
                qb = _rope_half(q, cosv, sinv).astype(BF16)
                kr = _rope_half(k, cosv, sinv) * (RET_DIM ** -0.5)
                kb = kr.astype(BF16)
                scores = _dot_nt(qb, kb) * in_ref[h]
                inner = _dot(scores.astype(BF16), v)
                prev = state[h]
                prev_b = prev.astype(BF16)
                st_ref[h, c] = prev_b
                o = inner + _dot(qb, prev_b) * qd_ref[h]
                o_ref[rows, c0:c0 + RET_DIM] = o
                rstd = lax.rsqrt(jnp.mean(o * o, axis=-1, keepdims=True) + EPS)
                cat_ref[rows, c0:c0 + RET_DIM] = (o * rstd * (g * _sigmoid(g))).astype(BF16)
                state[h] = cd_ref[h] * prev + _dot_tn((kr * kd_ref[h]).astype(BF16), v)

    full = lambda shape: pl.BlockSpec(shape, lambda n: (0,) * len(shape))
    step = RET_STEP * BLK
    return pl.pallas_call(
        body, name="retention_fwd", grid=(nc // RET_STEP,),
        in_specs=[pl.BlockSpec((step, 4 * RET_WIDTH), lambda n: (n, 0)),
                  pl.BlockSpec((step, RET_DIM // 2), lambda n: (n, 0)), pl.BlockSpec((step, RET_DIM // 2), lambda n: (n, 0)),
                  full((RET_HEADS, BLK, BLK)), full((RET_HEADS, BLK, 1)), full((RET_HEADS, BLK, 1)), full((RET_HEADS, 1, 1))]
        + ([ANY] if after is not None else []),
        out_specs=[pl.BlockSpec((step, RET_WIDTH), lambda n: (n, 0)), pl.BlockSpec((step, RET_WIDTH), lambda n: (n, 0)),
                   pl.BlockSpec((RET_HEADS, RET_STEP, RET_DIM, RET_DIM), lambda n: (0, n, 0, 0))],
        out_shape=[jax.ShapeDtypeStruct((s, RET_WIDTH), F32), jax.ShapeDtypeStruct((s, D_MODEL), BF16),
                   jax.ShapeDtypeStruct((RET_HEADS, nc, RET_DIM, RET_DIM), BF16)],
        scratch_shapes=[pltpu.VMEM((RET_HEADS, RET_DIM, RET_DIM), F32)],
        compiler_params=_params("arbitrary"),
    )(proj, cos, sin, intra, q_dec, k_dec, chunk_dec, *((after,) if after is not None else ()))


def _retention_bwd(proj, o, states, dcat, cos, sin, tables):
    s = proj.shape[0]
    nc = s // BLK
    intra, q_dec, k_dec, chunk_dec = tables

    def body(p_ref, o_ref, st_ref, dc_ref, cos_ref, sin_ref, in_ref, qd_ref, kd_ref, cd_ref, dp_ref, dstate):
        @pl.when(pl.program_id(0) == 0)
        def _():
            dstate[...] = jnp.zeros_like(dstate)

        for c in reversed(range(RET_STEP)):
            rows = slice(c * BLK, (c + 1) * BLK)
            cosv, sinv = cos_ref[rows, :], sin_ref[rows, :]
            for h in range(RET_HEADS):
                c0 = h * RET_DIM
                q = p_ref[rows, c0:c0 + RET_DIM].astype(F32)
                k = p_ref[rows, RET_WIDTH + c0:RET_WIDTH + c0 + RET_DIM].astype(F32)
                v = p_ref[rows, 2 * RET_WIDTH + c0:2 * RET_WIDTH + c0 + RET_DIM]
                g = p_ref[rows, 3 * RET_WIDTH + c0:3 * RET_WIDTH + c0 + RET_DIM].astype(F32)
                o = o_ref[rows, c0:c0 + RET_DIM]
                dc = dc_ref[rows, c0:c0 + RET_DIM].astype(F32)
                rstd = lax.rsqrt(jnp.mean(o * o, axis=-1, keepdims=True) + EPS)
                nrm = o * rstd
                sg = _sigmoid(g)
                dg = dc * nrm * (sg * (1.0 + g * (1.0 - sg)))
                dn = dc * (g * sg)
                do = rstd * (dn - nrm * jnp.mean(dn * nrm, axis=-1, keepdims=True))
                qb = _rope_half(q, cosv, sinv).astype(BF16)
                kr = _rope_half(k, cosv, sinv) * (RET_DIM ** -0.5)
                kb = kr.astype(BF16)
                mask = in_ref[h]
                qd, kd = qd_ref[h], kd_ref[h]
                prev_b = st_ref[h, c]
                dnext = dstate[h]
                dnext_b = dnext.astype(BF16)
                att = (_dot_nt(qb, kb) * mask).astype(BF16)
                do_b = do.astype(BF16)
                doq = (do * qd).astype(BF16)
                dv = _dot_tn(att, do_b) + _dot((kr * kd).astype(BF16), dnext_b)
                ds = (_dot_nt(do_b, v) * mask).astype(BF16)
                dqr = _dot(ds, kb) + _dot_nt(doq, prev_b)
                dkr = _dot_tn(ds, qb) + _dot_nt(v, dnext_b) * kd
                dstate[h] = cd_ref[h] * dnext + _dot_tn(qb, doq)
                dq = _unrope_half(dqr, cosv, sinv)
                dk = _unrope_half(dkr * (RET_DIM ** -0.5), cosv, sinv)
                dp_ref[rows, c0:c0 + RET_DIM] = dq.astype(BF16)
                dp_ref[rows, RET_WIDTH + c0:RET_WIDTH + c0 + RET_DIM] = dk.astype(BF16)
                dp_ref[rows, 2 * RET_WIDTH + c0:2 * RET_WIDTH + c0 + RET_DIM] = dv.astype(BF16)
                dp_ref[rows, 3 * RET_WIDTH + c0:3 * RET_WIDTH + c0 + RET_DIM] = dg.astype(BF16)

    steps = nc // RET_STEP
    rev = lambda n: steps - 1 - n
    full = lambda shape: pl.BlockSpec(shape, lambda n: (0,) * len(shape))
    step = RET_STEP * BLK
    return pl.pallas_call(
        body, name="retention_bwd", grid=(steps,),
        in_specs=[pl.BlockSpec((step, 4 * RET_WIDTH), lambda n: (rev(n), 0)),
                  pl.BlockSpec((step, RET_WIDTH), lambda n: (rev(n), 0)),
                  pl.BlockSpec((RET_HEADS, RET_STEP, RET_DIM, RET_DIM), lambda n: (0, rev(n), 0, 0)),
                  pl.BlockSpec((step, RET_WIDTH), lambda n: (rev(n), 0)),
                  pl.BlockSpec((step, RET_DIM // 2), lambda n: (rev(n), 0)), pl.BlockSpec((step, RET_DIM // 2), lambda n: (rev(n), 0)),
                  full((RET_HEADS, BLK, BLK)), full((RET_HEADS, BLK, 1)), full((RET_HEADS, BLK, 1)), full((RET_HEADS, 1, 1))],
        out_specs=pl.BlockSpec((step, 4 * RET_WIDTH), lambda n: (rev(n), 0)),
        out_shape=jax.ShapeDtypeStruct((s, EVEN_IN), BF16),
        scratch_shapes=[pltpu.VMEM((RET_HEADS, RET_DIM, RET_DIM), F32)],
        compiler_params=_params("arbitrary"))(proj, o, states, dcat, cos, sin, intra, q_dec, k_dec, chunk_dec)


CONV_ROWS = 256
HALO = 16


def _conv_pieces(p, halo, first):
    gb, gc, u, gv = (p[:, i * CONV_WIDTH:(i + 1) * CONV_WIDTH] for i in range(4))
    cu = gc * u
    hcu = halo[:, CONV_WIDTH:2 * CONV_WIDTH] * halo[:, 2 * CONV_WIDTH:3 * CONV_WIDTH]
    hcu = jnp.where(first, 0.0, hcu)
    r1, r2 = hcu[HALO - 1:HALO], hcu[HALO - 2:HALO - 1]
    row = lax.broadcasted_iota(jnp.int32, cu.shape, 0)
    m1 = jnp.where(row == 0, r1, pltpu.roll(cu, 1, 0))
    m2 = jnp.where(row == 0, r2, jnp.where(row == 1, r1, pltpu.roll(cu, 2, 0)))
    return gb, gc, u, gv, cu, m1, m2


def _conv_fwd(proj, conv_w, cat):
    s = proj.shape[0]
    per = CONV_ROWS // HALO

    def body(p_ref, halo_ref, w_ref, cat_in, cat_ref):
        del cat_in
        first = pl.program_id(0) == 0
        gb, _, _, gv, cu, m1, m2 = _conv_pieces(p_ref[...].astype(F32), halo_ref[...].astype(F32), first)
        conv = w_ref[0:1, :] * m2 + w_ref[1:2, :] * m1 + w_ref[2:3, :] * cu
        cat_ref[...] = (gb * conv * (gv * _sigmoid(gv))).astype(BF16)

    return pl.pallas_call(
        body, name="conv_fwd", grid=(s // CONV_ROWS,),
        in_specs=[pl.BlockSpec((CONV_ROWS, 4 * CONV_WIDTH), lambda i: (i, 1)),
                  pl.BlockSpec((HALO, 4 * CONV_WIDTH), lambda i: (jnp.maximum(i * per - 1, 0), 1)),
                  pl.BlockSpec((3, CONV_WIDTH), lambda i: (0, 0)), ANY],
        out_specs=pl.BlockSpec((CONV_ROWS, CONV_WIDTH), lambda i: (i, 1)),
        out_shape=jax.ShapeDtypeStruct(cat.shape, cat.dtype), input_output_aliases={3: 0},
        compiler_params=_params("parallel"))(proj, proj, conv_w, cat)


def _conv_bwd(proj, dcat, conv_w, dproj):
    s = proj.shape[0]
    per = CONV_ROWS // HALO
    last_halo = s // HALO - 1
    nsteps = s // CONV_ROWS

    def body(p_ref, halo_ref, nxt_ref, dc_ref, dnxt_ref, w_ref, dp_in, dp_ref, dw_ref):
        del dp_in
        i = pl.program_id(0)
        gb, gc, u, gv, cu, m1, m2 = _conv_pieces(p_ref[...].astype(F32), halo_ref[...].astype(F32), i == 0)
        w0, w1, w2 = w_ref[0:1, :], w_ref[1:2, :], w_ref[2:3, :]
        conv = w0 * m2 + w1 * m1 + w2 * cu
        dco = dc_ref[...].astype(F32)
        sg = _sigmoid(gv)
        silu = gv * sg
        dgb = dco * conv * silu
        dco_gb = dco * gb
        dgv = dco_gb * conv * (sg * (1.0 + gv * (1.0 - sg)))
        dconv = dco_gb * silu
        nxt = nxt_ref[...].astype(F32)
        ngv = nxt[:, 3 * CONV_WIDTH:]
        dnext = dnxt_ref[...].astype(F32) * nxt[:, :CONV_WIDTH] * (ngv * _sigmoid(ngv))
        dnext = jnp.where(i == nsteps - 1, 0.0, dnext)
        n1, n2 = dnext[0:1], dnext[1:2]
        row = lax.broadcasted_iota(jnp.int32, dconv.shape, 0)
        p1 = jnp.where(row == CONV_ROWS - 1, n1, pltpu.roll(dconv, CONV_ROWS - 1, 0))
        p2 = jnp.where(row == CONV_ROWS - 1, n2, jnp.where(row == CONV_ROWS - 2, n1, pltpu.roll(dconv, CONV_ROWS - 2, 0)))
        dcu = w2 * dconv + w1 * p1 + w0 * p2
        dp_ref[...] = jnp.concatenate([dgb, dcu * u, dcu * gc, dgv], axis=1).astype(BF16)

        @pl.when(i == 0)
        def _():
            dw_ref[...] = jnp.zeros_like(dw_ref)

        taps = [jnp.sum(dconv * m, axis=0, keepdims=True) for m in (m2, m1, cu)]
        r8 = lax.broadcasted_iota(jnp.int32, dw_ref.shape, 0)
        dw_ref[...] += jnp.where(r8 == 0, taps[0], jnp.where(r8 == 1, taps[1], jnp.where(r8 == 2, taps[2], 0.0)))

    return pl.pallas_call(
        body, name="conv_bwd", grid=(nsteps,),
        in_specs=[pl.BlockSpec((CONV_ROWS, 4 * CONV_WIDTH), lambda i: (i, 1)),
                  pl.BlockSpec((HALO, 4 * CONV_WIDTH), lambda i: (jnp.maximum(i * per - 1, 0), 1)),
                  pl.BlockSpec((HALO, 4 * CONV_WIDTH), lambda i: (jnp.minimum((i + 1) * per, last_halo), 1)),
                  pl.BlockSpec((CONV_ROWS, CONV_WIDTH), lambda i: (i, 1)),
                  pl.BlockSpec((HALO, CONV_WIDTH), lambda i: (jnp.minimum((i + 1) * per, last_halo), 1)),
                  pl.BlockSpec((3, CONV_WIDTH), lambda i: (0, 0)), ANY],
        out_specs=[pl.BlockSpec((CONV_ROWS, 4 * CONV_WIDTH), lambda i: (i, 1)), pl.BlockSpec((8, CONV_WIDTH), lambda i: (0, 0))],
        out_shape=[jax.ShapeDtypeStruct(dproj.shape, dproj.dtype), jax.ShapeDtypeStruct((8, CONV_WIDTH), F32)],
        input_output_aliases={6: 0},
        compiler_params=_params("arbitrary"))(proj, proj, proj, dcat, dcat, conv_w, dproj)


GROUP_HEADS = Q_HEADS // KV_HEADS
GROUP_WIDTH = GROUP_HEADS * HEAD_DIM
SLAB = 2 * HEAD_DIM
KV_COL = ATTN_WIDTH // GROUP_WIDTH
GATE_COL = KV_COL + 1
ATTN_SCALE = HEAD_DIM ** -0.5
KV_ROWS = 1024


def _half_mask(shape, which):
    return (_lane(shape) // HEAD_DIM) == which


def _dup_head(slab, which):
    kept = jnp.where(_half_mask(slab.shape, which), slab, 0.0)
    return kept + pltpu.roll(kept, HEAD_DIM, 1)


def _stack_heads(x):
    parts = []
    for sl in range(GROUP_WIDTH // SLAB):
        slab = x[:, sl * SLAB:(sl + 1) * SLAB]
        parts += [jnp.where(_half_mask(slab.shape, e), slab, 0.0) for e in range(2)]
    return jnp.concatenate(parts, axis=0)


def _unstack_heads(y):
    slabs = []
    for sl in range(GROUP_WIDTH // SLAB):
        a, b = y[(2 * sl) * BLK:(2 * sl + 1) * BLK], y[(2 * sl + 1) * BLK:(2 * sl + 2) * BLK]
        slabs.append(jnp.where(_half_mask(a.shape, 0), a, b))
    return jnp.concatenate(slabs, axis=1)


def _q_prep(q, qw, cosf, sins, ones_bd):
    rstd = lax.rsqrt(_group_mean(q * q, ones_bd) + EPS)
    nrm = q * rstd
    y = nrm * qw
    return nrm, rstd, y * cosf + _partner64(y) * sins


def _band(tri_ref, n):
    own = tri_ref[...] > 0.5
    return own, jnp.where(jnp.logical_and(n == 0, jnp.logical_not(own)), -1e30, 0.0)


def _fold(pair, own):
    return jnp.where(own, pair[:, BLK:], pair[:, :BLK])


def _unfold(folded, own):
    return jnp.concatenate([jnp.where(own, 0.0, folded), jnp.where(own, folded, 0.0)], axis=1)


def _head_probs(raw_scores, sink, own, bias):
    sc = _fold(raw_scores, own) + bias
    m = jnp.maximum(jnp.max(sc, axis=-1, keepdims=True), sink)
    p = jnp.exp(sc - m)
    psink = jnp.exp(sink - m)
    inv = 1.0 / (jnp.sum(p, axis=-1, keepdims=True) + psink)
    return p * inv, psink * inv


def _k_prep(k, kw, cosf, sins, ones_bd):
    rstd = lax.rsqrt(_group_mean(k * k, ones_bd) + EPS)
    nrm = k * rstd
    y = nrm * kw
    return nrm, rstd, y * cosf + _partner64(y) * sins


def _qk_prep(proj, qw, kw, cos, sins, ones_q, ones_kv):
    s = proj.shape[0]
    rows = min(KV_ROWS, s)

    def body(p_ref, qw_ref, kw_ref, cos_ref, sin_ref, oq_ref, ok_ref, o_ref):
        j = pl.program_id(1)

        @pl.when(j < KV_COL)
        def _():
            cosf, sinf = _tile_lanes(cos_ref[...], 4), _tile_lanes(sin_ref[...], 4)
            roped = _q_prep(p_ref[...].astype(F32), qw_ref[...], cosf, sinf, oq_ref[...])[2]
            o_ref[...] = (roped * ATTN_SCALE).astype(BF16)

        @pl.when(j == KV_COL)
        def _():
            cosf, sinf = _tile_lanes(cos_ref[...], 2), _tile_lanes(sin_ref[...], 2)
            kr = _k_prep(p_ref[:, :KV_WIDTH].astype(F32), kw_ref[...], cosf, sinf, ok_ref[...])[2]
            o_ref[...] = jnp.concatenate([kr.astype(BF16), p_ref[:, KV_WIDTH:]], axis=1)

    full = lambda shape: pl.BlockSpec(shape, lambda i, j: (0,) * len(shape))
    tab = pl.BlockSpec((rows, SLAB), lambda i, j: (i, 0))
    blk = pl.BlockSpec((rows, GROUP_WIDTH), lambda i, j: (i, j))
    return pl.pallas_call(
        body, name="qk_prep", grid=(s // rows, KV_COL + 1),
        in_specs=[blk, full((1, GROUP_WIDTH)), full((1, KV_WIDTH)), tab, tab, full(ones_q.shape), full(ones_kv.shape)],
        out_specs=blk, out_shape=jax.ShapeDtypeStruct((s, ATTN_WIDTH + 2 * KV_WIDTH), BF16),
        compiler_params=_params("parallel", "arbitrary"))(proj, qw, kw, cos, sins, ones_q, ones_kv)


def _keys_values(kc_ref, kp_ref, vc_ref, vp_ref, head):
    lanes = slice((head // 2) * SLAB, (head // 2 + 1) * SLAB)
    dup = lambda ref: _dup_head(ref[:, lanes].astype(F32), head % 2)
    return (jnp.concatenate([dup(kp_ref), dup(kc_ref)], axis=0).astype(BF16),
            jnp.concatenate([dup(vp_ref), dup(vc_ref)], axis=0).astype(BF16))


FWD_STEP_HEADS = 4
BWD_STEP_HEADS = 2


def _swa_specs(heads):
    kv_width = heads * HEAD_DIM
    prev = lambda n: jnp.maximum(n - 1, 0)
    kv = lambda col0, row: pl.BlockSpec((BLK, kv_width), lambda gs, n: (row(n), col0 // kv_width + gs))
    cur = lambda n: n
    full = lambda shape: pl.BlockSpec(shape, lambda gs, n: (0,) * len(shape))
    gate = lambda t: pl.BlockSpec((BLK, GROUP_WIDTH), lambda gs, n: (n, GATE_COL + heads * gs + t))
    return dict(
        sinks=pl.BlockSpec(memory_space=pltpu.SMEM), gates=[gate(t) for t in range(heads)],
        kc=kv(ATTN_WIDTH, cur), kp=kv(ATTN_WIDTH, prev), vc=kv(ATTN_WIDTH + KV_WIDTH, cur), vp=kv(ATTN_WIDTH + KV_WIDTH, prev),
        tri=full((BLK, BLK)), step=pl.BlockSpec((BLK, heads * GROUP_WIDTH), lambda gs, n: (n, gs)))


def _lower_triangle():
    return jnp.tril(jnp.ones((BLK, BLK), F32))


def _swa_fwd(qk, proj, sinks):
    s = proj.shape[0]
    nb = s // BLK
    heads = FWD_STEP_HEADS
    sp = _swa_specs(heads)

    def body(sink_ref, q_ref, kc_ref, kp_ref, vc_ref, vp_ref, *rest):
        gate_refs, (tri_ref, ag_ref, o_ref) = rest[:heads], rest[heads:]
        gs, n = pl.program_id(0), pl.program_id(1)
        own, bias = _band(tri_ref, n)
        for t, gate_ref in enumerate(gate_refs):
            cols = slice(t * GROUP_WIDTH, (t + 1) * GROUP_WIDTH)
            first_head = (heads * gs + t) * GROUP_HEADS
            kcat, vcat = _keys_values(kc_ref, kp_ref, vc_ref, vp_ref, t)
            scores = _dot_nt(_stack_heads(q_ref[:, cols]), kcat)
            probs = []
            for j in range(GROUP_HEADS):
                p, _ = _head_probs(scores[j * BLK:(j + 1) * BLK], sink_ref[first_head + j], own, bias)
                probs.append(_unfold(p, own).astype(BF16))
            o = _unstack_heads(_dot(jnp.concatenate(probs, axis=0), vcat))
            gate = gate_ref[...].astype(F32)
            o_ref[:, cols] = o.astype(BF16)
            ag_ref[:, cols] = (o * (gate * _sigmoid(gate))).astype(BF16)

    shp = jax.ShapeDtypeStruct((s, ATTN_WIDTH), BF16)
    return pl.pallas_call(
        body, name="swa_fwd", grid=(KV_HEADS // heads, nb),
        in_specs=[sp["sinks"], sp["step"], sp["kc"], sp["kp"], sp["vc"], sp["vp"], *sp["gates"], sp["tri"]],
        out_specs=[sp["step"], sp["step"]], out_shape=[shp, shp],
        compiler_params=_params("parallel", "arbitrary"),
    )(sinks, qk, qk, qk, qk, qk, *[proj] * heads, _lower_triangle())


def _swa_bwd(qk, proj, dag, sinks):
    s = proj.shape[0]
    nb = s // BLK
    heads = BWD_STEP_HEADS
    sp = _swa_specs(heads)

    def body(sink_ref, q_ref, kc_ref, kp_ref, vc_ref, vp_ref, *rest):
        gate_refs = rest[:heads]
        dag_ref, tri_ref, dq_ref, dkc_ref, dkp_ref, dvc_ref, dvp_ref, dsink_ref = rest[heads:]
        gs, n = pl.program_id(0), pl.program_id(1)
        own, bias = _band(tri_ref, n)

        @pl.when(n == 0)
        def _():
            dsink_ref[...] = jnp.zeros_like(dsink_ref)

        for t, gate_ref in enumerate(gate_refs):
            cols = slice(t * GROUP_WIDTH, (t + 1) * GROUP_WIDTH)
            first_head = (heads * gs + t) * GROUP_HEADS
            kcat, vcat = _keys_values(kc_ref, kp_ref, vc_ref, vp_ref, t)
            gate = gate_ref[...].astype(F32)
            do = dag_ref[:, cols].astype(F32) * (gate * _sigmoid(gate))
            q_stack = _stack_heads(q_ref[:, cols])
            do_stack = _stack_heads(do).astype(BF16)
            scores = _dot_nt(q_stack, kcat)
            dprobs = _dot_nt(do_stack, vcat)
            probs, dscores, dsinks = [], [], []
            for j in range(GROUP_HEADS):
                rows = slice(j * BLK, (j + 1) * BLK)
                p, psink = _head_probs(scores[rows], sink_ref[first_head + j], own, bias)
                dp = _fold(dprobs[rows], own)
                delta = jnp.sum(p * dp, axis=-1, keepdims=True)
                probs.append(_unfold(p, own).astype(BF16))
                dscores.append(_unfold(p * (dp - delta), own).astype(BF16))
                dsinks.append(-jnp.sum(psink * delta, axis=0, keepdims=True))
            ds = jnp.concatenate(dscores, axis=0)
            dk = _dot_tn(ds, q_stack)
            dv = _dot_tn(jnp.concatenate(probs, axis=0), do_stack)
            dk = dk + pltpu.roll(dk, HEAD_DIM, 1)
            dv = dv + pltpu.roll(dv, HEAD_DIM, 1)
            dkp_ref[t], dkc_ref[t] = dk[:BLK], dk[BLK:]
            dvp_ref[t], dvc_ref[t] = dv[:BLK], dv[BLK:]
            dq_ref[:, cols] = _unstack_heads(_dot(ds, kcat)).astype(BF16)
            r8 = lax.broadcasted_iota(jnp.int32, (8, SLAB), 0)
            upd = jnp.zeros((8, SLAB), F32)
            for j in range(GROUP_HEADS):
                upd = jnp.where(r8 == j, dsinks[j], upd)
            dsink_ref[t] += upd

    cur_out = pl.BlockSpec((heads, BLK, SLAB), lambda gs, n: (gs, n, 0))
    prev_out = pl.BlockSpec((heads, BLK, SLAB), lambda gs, n: (gs, (n + nb - 1) % nb, 0))
    kv_shape = jax.ShapeDtypeStruct((KV_HEADS, s, SLAB), F32)
    return pl.pallas_call(
        body, name="swa_bwd", grid=(KV_HEADS // heads, nb),
        in_specs=[sp["sinks"], sp["step"], sp["kc"], sp["kp"], sp["vc"], sp["vp"], *sp["gates"], sp["step"], sp["tri"]],
        out_specs=[sp["step"], cur_out, prev_out, cur_out, prev_out,
                   pl.BlockSpec((heads, 8, SLAB), lambda gs, n: (gs, 0, 0))],
        out_shape=[jax.ShapeDtypeStruct((s, ATTN_WIDTH), BF16), kv_shape, kv_shape, kv_shape, kv_shape,
                   jax.ShapeDtypeStruct((KV_HEADS, 8, SLAB), F32)],
        compiler_params=_params("parallel", "arbitrary"),
    )(sinks, qk, qk, qk, qk, qk, *[proj] * heads, dag, _lower_triangle())


def _swa_bwd_finish(proj, dqr, o, dag, dkc, dkp, dvc, dvp, qw, kw, cos, sins, ones_q, ones_kv):
    s = proj.shape[0]
    rows = min(KV_ROWS, s)
    n_q = KV_COL
    q_of = lambda j: jnp.clip(j - 1, 0, n_q - 1)
    gate_of = lambda j: jnp.clip(j - 1 - n_q, 0, n_q - 1)

    def body(kv_ref, p_ref, dqr_ref, o_ref, dag_ref, dkc_ref, dkp_ref, dvc_ref, dvp_ref, qw_ref, kw_ref, cos_ref, sin_ref,
             oq_ref, ok_ref, dp_ref, dqw_ref, dkw_ref):
        j, i = pl.program_id(0), pl.program_id(1)

        @pl.when(j == 0)
        def _():
            @pl.when(i == 0)
            def _():
                dkw_ref[...] = jnp.zeros_like(dkw_ref)

            def assemble(cur_ref, prv_ref):
                tot = [cur_ref[h] + prv_ref[h] for h in range(KV_HEADS)]
                first = _half_mask(tot[0].shape, 0)
                return jnp.concatenate([jnp.where(first, tot[0], tot[1]), jnp.where(first, tot[2], tot[3])], axis=1)

            dkr = assemble(dkc_ref, dkp_ref)
            dv = assemble(dvc_ref, dvp_ref)
            cosf, sinf = _tile_lanes(cos_ref[...], 2), _tile_lanes(sin_ref[...], 2)
            nrm, rstd, _ = _k_prep(kv_ref[:, :KV_WIDTH].astype(F32), kw_ref[...], cosf, sinf, ok_ref[...])
            dy = dkr * cosf + _partner64(dkr * sinf)
            dn = dy * kw_ref[...]
            dk = rstd * (dn - nrm * _group_mean(dn * nrm, ok_ref[...], passes=1))
            dp_ref[...] = jnp.concatenate([dk, dv], axis=1).astype(BF16)
            dkw_ref[...] += jnp.sum(dy * nrm, axis=0, keepdims=True)

        @pl.when(jnp.logical_and(j >= 1, j <= n_q))
        def _():
            @pl.when(i == 0)
            def _():
                dqw_ref[...] = jnp.zeros_like(dqw_ref)

            cosf, sinf = _tile_lanes(cos_ref[...], 4), _tile_lanes(sin_ref[...], 4)
            nrm, rstd, _ = _q_prep(p_ref[...].astype(F32), qw_ref[...], cosf, sinf, oq_ref[...])
            dq = dqr_ref[...].astype(F32) * ATTN_SCALE
            dy = dq * cosf + _partner64(dq * sinf)
            dn = dy * qw_ref[...]
            dp_ref[...] = (rstd * (dn - nrm * _group_mean(dn * nrm, oq_ref[...], passes=1))).astype(BF16)
            dqw_ref[0] += jnp.sum(dy * nrm, axis=0, keepdims=True)

        @pl.when(j > n_q)
        def _():
            gate = p_ref[...].astype(F32)
            sg = _sigmoid(gate)
            dp_ref[...] = (dag_ref[...].astype(F32) * o_ref[...].astype(F32) * (sg * (1.0 + gate * (1.0 - sg)))).astype(BF16)

    first_pass = lambda j, i: jnp.where(j == 0, i, 0)
    acc = pl.BlockSpec((KV_HEADS, rows, SLAB), lambda j, i: (0, first_pass(j, i), 0))
    full = lambda shape: pl.BlockSpec(shape, lambda j, i: (0,) * len(shape))
    tab = pl.BlockSpec((rows, SLAB), lambda j, i: (i, 0))
    out_col = lambda j: jnp.where(j == 0, KV_COL, jnp.where(j <= n_q, j - 1, j))
    return pl.pallas_call(
        body, name="swa_bwd_finish", grid=(2 * n_q + 1, s // rows),
        in_specs=[pl.BlockSpec((rows, GROUP_WIDTH), lambda j, i: (first_pass(j, i), KV_COL)),
                  pl.BlockSpec((rows, GROUP_WIDTH), lambda j, i: (jnp.where(j == 0, 0, i), jnp.where(j <= n_q, q_of(j), j))),
                  pl.BlockSpec((rows, GROUP_WIDTH), lambda j, i: (jnp.where(jnp.logical_and(j >= 1, j <= n_q), i, 0), q_of(j))),
                  pl.BlockSpec((rows, GROUP_WIDTH), lambda j, i: (jnp.where(j > n_q, i, 0), gate_of(j))),
                  pl.BlockSpec((rows, GROUP_WIDTH), lambda j, i: (jnp.where(j > n_q, i, 0), gate_of(j))),
                  acc, acc, acc, acc, full((1, GROUP_WIDTH)), full((1, KV_WIDTH)), tab, tab,
                  full(ones_q.shape), full(ones_kv.shape)],
        out_specs=[pl.BlockSpec((rows, GROUP_WIDTH), lambda j, i: (i, out_col(j))),
                   pl.BlockSpec((1, 8, GROUP_WIDTH), lambda j, i: (q_of(j), 0, 0)), pl.BlockSpec((8, KV_WIDTH), lambda j, i: (0, 0))],
        out_shape=[jax.ShapeDtypeStruct((s, ODD_IN), BF16), jax.ShapeDtypeStruct((n_q, 8, GROUP_WIDTH), F32),
                   jax.ShapeDtypeStruct((8, KV_WIDTH), F32)],
        compiler_params=_params("arbitrary", "arbitrary"),
    )(proj, proj, dqr, o, dag, dkc, dkp, dvc, dvp, qw, kw, cos, sins, ones_q, ones_kv)


def _place():
    x, y, c = lax.axis_index("x"), lax.axis_index("y"), lax.axis_index("c")
    return x, y, c


OTHER_CHIPS = ((1, 0), (0, 1), (1, 1))


def _half_rows(ref, half, rows):
    return ref.at[pl.ds(pl.multiple_of(half * (rows // 2), 8), rows // 2)]


DMA_CHUNK_BYTES = 1 << 20
BF16_TILE_ROWS = 16


def _n_chunks(ref):
    rows = ref.shape[-2]
    nbytes = math.prod(ref.shape) * jnp.dtype(ref.dtype).itemsize
    n = 1
    while 2 * n * DMA_CHUNK_BYTES <= nbytes and rows % (2 * n * BF16_TILE_ROWS) == 0:
        n *= 2
    return n


def _row_chunk(ref, k, n):
    rows = ref.shape[-2] // n
    return ref.at[pl.ds(k * rows, rows)] if len(ref.shape) == 2 else ref.at[:, pl.ds(k * rows, rows)]


def _push(src, dst, send_sem, recv_sem, device_id):
    n = _n_chunks(src)
    for k in range(n):
        pltpu.make_async_remote_copy(src_ref=_row_chunk(src, k, n), dst_ref=_row_chunk(dst, k, n), send_sem=send_sem,
                                     recv_sem=recv_sem, device_id=device_id, device_id_type=MESH).start()
    return pltpu.make_async_remote_copy(src_ref=src, dst_ref=dst, send_sem=send_sem, recv_sem=recv_sem,
                                        device_id=device_id, device_id_type=MESH)


HBM = pl.BlockSpec(memory_space=pltpu.HBM)
SEM = pl.BlockSpec(memory_space=pltpu.SEMAPHORE)
SPLIT_COPY_EFFECT = pltpu.SideEffectType.DATAFLOW_SIDE_EFFECTING


def _in_hbm(a):
    return pltpu.with_memory_space_constraint(a, pltpu.HBM)


def _start_copies(name, arrays, plan, n_copies, after=None):
    n = len(arrays)

    def body(*refs):
        send_sem, recv_sem = refs[-n - 3], refs[-n - 2]
        for k, (src, dst, peer) in enumerate(plan(refs[:n])):
            _push(src, dst, send_sem.at[k], recv_sem.at[k], peer)
        refs[-1][...] = jnp.zeros_like(refs[-1])

    dma = pltpu.SemaphoreType.DMA((n_copies,))
    outs = pl.pallas_call(
        body, name=name,
        out_shape=(dma, dma, *[pltpu.HBM(a.shape, a.dtype) for a in arrays], jax.ShapeDtypeStruct((8, 128), F32)),
        in_specs=[HBM] * n + ([ANY] if after is not None else []),
        out_specs=(SEM, SEM, *[HBM] * n, pl.BlockSpec(memory_space=pltpu.VMEM)),
        input_output_aliases={i: i + 2 for i in range(n)},
        compiler_params=pltpu.CompilerParams(has_side_effects=SPLIT_COPY_EFFECT),
    )(*[_in_hbm(a) for a in arrays], *((after,) if after is not None else ()))
    return outs[0], outs[1], list(outs[2:2 + n]), outs[-1]


def _wait_copies(name, send_sem, recv_sem, arrays, plan, after):
    n = len(arrays)
    after = list(after) if isinstance(after, (list, tuple)) else [after]

    def body(*refs):
        send_ref, recv_ref = refs[n], refs[n + 1]
        for k, (src, dst, peer) in enumerate(plan(refs[:n])):
            cp = pltpu.make_async_remote_copy(src_ref=src, dst_ref=dst, send_sem=send_ref.at[k], recv_sem=recv_ref.at[k],
                                              device_id=peer, device_id_type=MESH)
            cp.wait_send()
            cp.wait_recv()

    return list(pl.pallas_call(
        body, name=name, out_shape=tuple(pltpu.HBM(a.shape, a.dtype) for a in arrays),
        in_specs=[HBM] * n + [SEM, SEM] + [ANY] * len(after), out_specs=tuple([HBM] * n),
        input_output_aliases={i: i for i in range(n)},
        compiler_params=pltpu.CompilerParams(has_side_effects=SPLIT_COPY_EFFECT),
    )(*arrays, send_sem, recv_sem, *after))


def _gather_region(full, kind, chip, half=None):
    if kind == "whole":
        return full.at[chip]
    if kind == "col" or isinstance(kind, tuple):
        part, parts = (0, 1) if kind == "col" else kind[1:]
        rows, width = full.shape[0], full.shape[1] // N_CHIPS
        piece = full.at[:, pl.ds(pl.multiple_of(chip * width + part * (width // parts), LANE_TILE), width // parts)]
    else:
        rows = full.shape[0] // N_CHIPS
        piece = full.at[pl.ds(pl.multiple_of(chip * rows, BF16_TILE_ROWS), rows)]
    return piece if half is None else _half_rows(piece, half, rows)


def _gather_plan(kinds):
    def plan(fulls):
        x, y, c = _place()
        copies = []
        for fx, fy in OTHER_CHIPS:
            for full, kind in zip(fulls, kinds):
                mine = _gather_region(full, kind, 2 * x + y, c)
                copies.append((mine, mine, (x ^ fx, y ^ fy, c)))
        return copies

    return plan


def _gather_start(name, fulls, kinds, after=None):
    return _start_copies(name, list(fulls), _gather_plan(kinds), 3 * len(kinds), after)


def _pass_on_plan(kinds):
    split = [i for i, kind in enumerate(kinds) if kind != "whole"]

    def plan(fulls):
        x, y, c = _place()
        copies = []
        for fx, fy in OTHER_CHIPS:
            chip = 2 * (x ^ fx) + (y ^ fy)
            for i in split:
                landed = _gather_region(fulls[i], kinds[i], chip, c)
                copies.append((landed, landed, (x, y, 1 - c)))
        return copies

    return plan, 3 * len(split)


def _gather_arrive(name, started, kinds, after, arrays=None):
    send_sem, recv_sem, fulls, _ = started
    fulls = _wait_copies(name + "_wait", send_sem, recv_sem, fulls if arrays is None else arrays, _gather_plan(kinds), after)
    plan, count = _pass_on_plan(kinds)
    return _start_copies(name + "_pass_on", fulls, plan, count)


def _gather_done(name, passed_on, kinds, after):
    send_sem, recv_sem, fulls, _ = passed_on
    return _wait_copies(name + "_pass_on_wait", send_sem, recv_sem, fulls, _pass_on_plan(kinds)[0], after)


def _allreduce_small(v):
    def body(v_ref, out_ref, buf, send_sems, recv_sems):
        x, y, c = _place()
        me = 4 * x + 2 * y + c
        buf[me] = v_ref[...]
        copies = []
        for r in range(1, N_DEV):
            peer = (x ^ (r >> 2), y ^ ((r >> 1) & 1), c ^ (r & 1))
            cp = pltpu.make_async_remote_copy(src_ref=v_ref, dst_ref=buf.at[me], send_sem=send_sems.at[r - 1],
                                              recv_sem=recv_sems.at[r - 1], device_id=peer, device_id_type=MESH)
            cp.start()
            copies.append(cp)
        for cp in copies:
            cp.wait_recv()
        for cp in copies:
            cp.wait_send()
        total = buf[0]
        for d in range(1, N_DEV):
            total = total + buf[d]
        out_ref[...] = total

    vm = pl.BlockSpec(memory_space=pltpu.VMEM)
    return pl.pallas_call(
        body, name="allreduce_small", in_specs=[vm], out_specs=vm, out_shape=jax.ShapeDtypeStruct(v.shape, v.dtype),
        scratch_shapes=[pltpu.VMEM((N_DEV,) + v.shape, v.dtype), pltpu.SemaphoreType.DMA((N_DEV - 1,)),
                        pltpu.SemaphoreType.DMA((N_DEV - 1,))],
        compiler_params=pltpu.CompilerParams(has_side_effects=True),
    )(v)


def _exchange_plan(n):
    def plan(refs):
        x, y, c = _place()
        copies = []
        for g, theirs in zip(refs[:n], refs[n:]):
            half = g.shape[1] // 2
            src = g.at[:, pl.ds(pl.multiple_of((1 - c) * half, BF16_TILE_ROWS), half)]
            copies.append((src, theirs, (x, y, 1 - c)))
        return copies

    return plan


def _pair_sum(g, theirs, core, name):
    pieces, half, cols = theirs.shape
    tr = min(half, 256)
    per = half // tr

    def body(core_ref, g_ref, t_ref, o_ref):
        del core_ref
        o_ref[...] = (g_ref[...].astype(F32) + t_ref[...].astype(F32)).astype(BF16)

    spec = pl.BlockSpec((1, tr, cols), lambda p, i, core_ref: (p, i, 0))
    return pl.pallas_call(
        body, name=name, out_shape=jax.ShapeDtypeStruct(theirs.shape, BF16),
        grid_spec=pltpu.PrefetchScalarGridSpec(
            num_scalar_prefetch=1, grid=(pieces, per),
            in_specs=[pl.BlockSpec((1, tr, cols), lambda p, i, core_ref: (p, core_ref[0] * per + i, 0)), spec],
            out_specs=spec),
        compiler_params=_params("parallel", "parallel"))(core, g, theirs)


def _scatter_plan(n):
    def plan(refs):
        parts, stacks = refs[:n], refs[n:]
        x, y, c = _place()
        copies = []
        for fx, fy in OTHER_CHIPS:
            chip = 2 * (x ^ fx) + (y ^ fy)
            for part, stack in zip(parts, stacks):
                if part.shape[0] == N_CHIPS:
                    piece = part.at[chip]
                else:
                    width = part.shape[2] // N_CHIPS
                    piece = part.at[0].at[:, pl.ds(pl.multiple_of(chip * width, 128), width)]
                copies.append((piece, stack.at[2 * x + y], (x ^ fx, y ^ fy, c)))
        return copies

    return plan


def _scatter_start(name, parts, after=None):
    def landing(a):
        return (N_CHIPS, a.shape[1], a.shape[2] if a.shape[0] == N_CHIPS else a.shape[2] // N_CHIPS)

    stacks = [lax.empty(landing(a), a.dtype) for a in parts]
    return _start_copies(name, list(parts) + stacks, _scatter_plan(len(parts)), 3 * len(parts), after)


def _scatter_finish(name, started, after):
    send_sem, recv_sem, arrays, _ = started
    n = len(arrays) // 2
    arrays = _wait_copies(name + "_wait", send_sem, recv_sem, arrays, _scatter_plan(n), after)
    return arrays[:n], arrays[n:]


def _sum_chips(part, stack, place, name):
    _, r, c = stack.shape
    tr = 256
    per = r // tr

    def body(place_ref, own_ref, a_ref, b_ref, c_ref, o_ref):
        del place_ref
        total = own_ref[0].astype(F32)
        for ref in (a_ref, b_ref, c_ref):
            total = total + ref[0].astype(F32)
        o_ref[...] = total

    if part.shape[0] == N_CHIPS:
        own = pl.BlockSpec((1, tr, c), lambda i, pr: (pr[1], i, 0))
    else:
        own = pl.BlockSpec((1, tr, c), lambda i, pr: (0, i, pr[1]))
    other = lambda flip: pl.BlockSpec((1, tr, c), lambda i, pr: (pr[1] ^ flip, i, 0))
    return pl.pallas_call(
        body, name=name, out_shape=jax.ShapeDtypeStruct((2 * r, c), F32),
        grid_spec=pltpu.PrefetchScalarGridSpec(
            num_scalar_prefetch=1, grid=(per,), in_specs=[own, other(2), other(1), other(3)],
            out_specs=pl.BlockSpec((tr, c), lambda i, pr: (pr[0] * per + i, 0))),
        compiler_params=_params("parallel"))(place, part, stack, stack, stack)


def _own_half_plan(fulls):
    x, y, c = _place()
    return [(mine, mine, (x, y, 1 - c)) for mine in (_half_rows(full, c, full.shape[0]) for full in fulls)]


MM = dict(tm=2048, tn=1024, tk=2048)
MM_LONG_K = dict(tm=1024, tn=1024, tk=4096)


def _local_step(x, target, ev_norm_w, q_norm_w, k_norm_w, sinks, own_first, weights_first, late_arrived, weights_late, grads_out):
    s = x.shape[0]
    cos_r, sin_r = _rope_tables(s, RET_DIM)
    cos_a, sin_a = _rope_tables(s, HEAD_DIM)
    cos_a = np.tile(cos_a, (1, 4))
    sins_a = np.tile(np.concatenate([-sin_a, sin_a], axis=1), (1, 2))
    tables = _retention_tables()
    ones_q, ones_kv = _block_diag_mean(min(GROUP_WIDTH, MXU_WIDTH)), _block_diag_mean(min(KV_WIDTH, MXU_WIDTH))
    qw_g = jnp.tile(q_norm_w, (1, GROUP_WIDTH // HEAD_DIM))
    kw_kv = jnp.tile(k_norm_w, (1, KV_WIDTH // HEAD_DIM))
    sinks1 = sinks.reshape(Q_HEADS)

    own_w_in0, own_block, start_token = own_first
    h0 = _rmsnorm(x, ev_norm_w, "norm0", after=start_token)
    shifted = dict(shift=own_block * (own_w_in0.shape[1] // MM["tn"]), total=EVEN_IN // MM["tn"], tm=MM["tm"], tn=MM["tn"],
                   out_dtype=BF16)
    own_blocks = own_w_in0.shape[1] // MM["tn"]
    proj0 = _mm_shifted(h0, own_w_in0, b_shifted=False, first=0, count=own_blocks, name="proj0_own", **shifted)
    for part in range(own_blocks):
        w_in0, small_weights, token = weights_first(part, proj0)
        if small_weights is not None:
            conv_w, od_norm_w = small_weights
        proj0 = _mm_shifted(h0, w_in0, b_shifted=True, first=own_blocks + part, stride=own_blocks,
                            count=shifted["total"] // own_blocks - 1, name=f"proj0_rest{part}", into=proj0, after=token, **shifted)
    o_ret, cat, states = _retention_fwd(proj0, cos_r, sin_r, tables, after=late_arrived(proj0))
    cat = _conv_fwd(proj0, conv_w, cat)
    w_out0, w_in1, w_out1 = weights_late(cat)

    def residual_and_norm(prod, x_ref, w_ref, x1_ref, h1_ref):
        x1v = x_ref[...] + prod
        x1_ref[...] = x1v
        rstd = lax.rsqrt(jnp.mean(x1v * x1v, axis=-1, keepdims=True) + EPS)
        h1_ref[...] = (x1v * rstd * w_ref[...]).astype(BF16)

    def residual_and_loss(prod, x1_ref, t_ref, dyb_ref, sq_ref):
        diff = (x1_ref[...] + prod) - t_ref[...]
        dyb_ref[...] = (diff * (1.0 / D_MODEL)).astype(BF16)

        @pl.when(pl.program_id(0) == 0)
        def _():
            sq_ref[...] = jnp.zeros_like(sq_ref)

        sq_ref[...] += jnp.sum(jnp.sum(diff * diff, axis=1, keepdims=True), axis=0, keepdims=True)

    act = lambda dt: jax.ShapeDtypeStruct((s, D_MODEL), dt)
    x1, h1 = _mm_rows(cat, w_out0, [x], [od_norm_w], [act(F32), act(BF16)], residual_and_norm, tm=min(s, 512), name="out0")
    proj1 = _mm(h1, w_in1, mode="nn", out_dtype=BF16, name="proj1", tm=2048, tn=1536, tk=2048)
    qk = _qk_prep(proj1, qw_g, kw_kv, cos_a, sins_a, ones_q, ones_kv)
    ag, o_att = _swa_fwd(qk, proj1, sinks1)
    dy_b, sq = _mm_rows(ag, w_out1, [x1, target], [], [act(BF16), jax.ShapeDtypeStruct((8, 128), F32)],
                        residual_and_loss, tm=min(s, 512), name="out1")

    g_w_out1 = _mm(ag, dy_b, mode="tn", out_dtype=BF16, name="g_w_out1", **MM_LONG_K)
    dag = _mm(dy_b, w_out1, mode="nt", out_dtype=BF16, name="d_ag", **MM)
    dqr, dkc, dkp, dvc, dvp, dsink = _swa_bwd(qk, proj1, dag, sinks1)
    dproj1, dqw, dkw = _swa_bwd_finish(proj1, dqr, o_att, dag, dkc, dkp, dvc, dvp, qw_g, kw_kv, cos_a, sins_a, ones_q, ones_kv)
    g_w_in1 = _mm(h1, dproj1, mode="tn", out_dtype=BF16, name="g_w_in1", tm=1024, tn=768, tk=4096)
    dh1 = _mm(dproj1, w_in1, mode="nt", out_dtype=BF16, name="d_h1", tm=1024, tn=1024, tk=ODD_IN)
    dx1_b, g_norm1 = _rmsnorm_bwd(x1, od_norm_w, dh1, dy_b, "norm1_bwd", BF16)

    g_w_out0 = _mm(cat, dx1_b, mode="tn", out_dtype=BF16, name="g_w_out0", **MM_LONG_K)
    token = grads_out.begin("upper", (("od_w_in", g_w_in1, "col"), ("od_w_out", g_w_out1, "row"), ("ev_w_out", g_w_out0, "row")))
    dcat = _mm(dx1_b, w_out0, mode="nt", out_dtype=BF16, name="d_cat", after=token, **MM)
    token = grads_out.send("upper", dcat)
    dproj0 = _retention_bwd(proj0, o_ret, states, dcat, cos_r, sin_r, tables)
    dproj0, g_conv = _conv_bwd(proj0, dcat, conv_w, dproj0)
    g_w_in0 = _mm(h0, dproj0, mode="tn", out_dtype=BF16, name="g_w_in0", after=token, **MM_LONG_K)
    token = grads_out.begin("in0", (("ev_w_in", g_w_in0, "col"),))
    half_blocks = s // (2 * MM_LONG_K["tm"])
    dh0 = _mm(dproj0, w_in0, mode="nt", out_dtype=BF16, name="d_h0_top", after=token, m_blocks=(0, half_blocks), **MM_LONG_K)
    dh0 = _mm(dproj0, w_in0, mode="nt", out_dtype=BF16, name="d_h0_bottom", after=grads_out.send("in0", dh0),
              m_blocks=(half_blocks, half_blocks), into=dh0, **MM_LONG_K)
    grad_x, g_norm0 = _rmsnorm_bwd(x, ev_norm_w, dh0, dx1_b, "norm0_bwd", F32)

    g_qw = dqw[:, 0, :].reshape(Q_HEADS, HEAD_DIM).sum(axis=0)
    g_kw = dkw[0].reshape(KV_HEADS, HEAD_DIM).sum(axis=0)
    g_sinks = dsink[:, :, 0].reshape(Q_HEADS)
    small = dict(ev_norm=g_norm0[0], od_norm=g_norm1[0], conv=g_conv[:3], qw=g_qw, kw=g_kw, sinks=g_sinks)
    return sq[0, 0], grad_x, small


LANE_TILE = 128


def _pack_small_grads(small, sq):
    pad = lambda v, n: jnp.pad(v, (0, n - v.shape[0]))
    tail = jnp.concatenate([pad(small["qw"], LANE_TILE), pad(small["kw"], LANE_TILE), small["sinks"]])
    rows = [small["ev_norm"], small["od_norm"]] + [small["conv"][t] for t in range(3)] + [tail, sq.reshape(1)]
    rows += [jnp.zeros((1,), F32)] * (8 - len(rows))
    return jnp.stack([pad(r, D_MODEL) for r in rows])


def _adamw_small(tot, chip, params):
    n = len(params)

    def body(chip_ref, tot_ref, *refs):
        ins, outs = refs[:3 * n], refs[3 * n:]
        c = chip_ref[0]

        def own(rows, width):
            blocks = [tot_ref[rows, k * width:(k + 1) * width] for k in range(N_CHIPS)]
            g = blocks[-1]
            for k in reversed(range(N_CHIPS - 1)):
                g = jnp.where(c == k, blocks[k], g)
            return g

        grads = [tot_ref[0:1, :], own(slice(2, 5), CONV_WIDTH // N_CHIPS), own(slice(1, 2), D_MODEL // N_CHIPS),
                 tot_ref[5:6, 0:HEAD_DIM], tot_ref[5:6, LANE_TILE:LANE_TILE + HEAD_DIM],
                 tot_ref[5:6, 2 * LANE_TILE:2 * LANE_TILE + Q_HEADS]]
        for i, g in enumerate(grads):
            w_ref, m_ref, v_ref = ins[3 * i:3 * i + 3]
            g_out, d_out, nm_out, nv_out = outs[4 * i:4 * i + 4]
            at = (0,) if len(w_ref.shape) == 3 else ()
            delta, nm, nv = _adamw_math(w_ref[at] if at else w_ref[...], g, m_ref[at] if at else m_ref[...],
                                        v_ref[at] if at else v_ref[...])
            for ref, val in ((g_out, g), (d_out, delta), (nm_out, nm), (nv_out, nv)):
                if at:
                    ref[0] = val
                else:
                    ref[...] = val

    vm = pl.BlockSpec(memory_space=pltpu.VMEM)
    flat = [a for p in params for a in p]
    outs = pl.pallas_call(
        body, name="adamw_small", in_specs=[pl.BlockSpec(memory_space=pltpu.SMEM), vm] + [vm] * len(flat),
        out_specs=[vm] * (4 * n), out_shape=[jax.ShapeDtypeStruct(p[0].shape, F32) for p in params for _ in range(4)],
    )(chip, tot, *flat)
    return [tuple(outs[4 * i:4 * i + 4]) for i in range(n)]


class _ReduceScatter:
    def __init__(self, place):
        self.place = place
        self.exchanging = {}
        self.started = []
        self.sharing = {}

    def begin(self, tag, grads):
        pieces = [g[None] if kind == "col" else g.reshape(N_CHIPS, g.shape[0] // N_CHIPS, g.shape[1]) for _, g, kind in grads]
        landing = [lax.empty((p.shape[0], p.shape[1] // 2, p.shape[2]), p.dtype) for p in pieces]
        started = _start_copies("exchange_" + tag, pieces + landing, _exchange_plan(len(pieces)), len(pieces))
        self.exchanging[tag] = ([nm for nm, _, _ in grads], started)
        return started[3]

    def send(self, tag, after):
        group, (send_sem, recv_sem, arrays, _) = self.exchanging[tag]
        n = len(group)
        arrays = _wait_copies("exchange_" + tag + "_wait", send_sem, recv_sem, arrays, _exchange_plan(n), after)
        parts = [_pair_sum(g, t, self.place[:1], "pair_sum_" + nm) for g, t, nm in zip(arrays[:n], arrays[n:], group)]
        started = _scatter_start("scatter_" + tag, parts)
        token = self.sum_up(started[3])
        self.started.append((tag, group, started))
        return started[3] if token is None else token

    def sum_up(self, after):
        token = None
        for tag, group, started in self.started:
            if tag not in self.sharing:
                parts, stacks = _scatter_finish("scatter_" + tag, started, after)
                sums = [_sum_chips(p, s, self.place, "chip_sum_" + nm) for p, s, nm in zip(parts, stacks, group)]
                self.sharing[tag] = (group, _start_copies("share_" + tag, sums, _own_half_plan, len(sums)))
                token = self.sharing[tag][1][3]
        return token

    def result(self, tag, after):
        group, (send_sem, recv_sem, sums, _) = self.sharing[tag]
        return dict(zip(group, _wait_copies("share_" + tag + "_wait", send_sem, recv_sem, sums, _own_half_plan, after)))


def kernel(x, ev_norm_w, ev_w_in, ev_conv_w, ev_w_out, od_norm_w, od_w_in, od_q_norm_w, od_k_norm_w, od_sinks, od_w_out, loss_target, m_ev_norm_w, m_ev_w_in, m_ev_conv_w, m_ev_w_out, m_od_norm_w, m_od_w_in, m_od_q_norm_w, m_od_k_norm_w, m_od_sinks, m_od_w_out, v_ev_norm_w, v_ev_w_in, v_ev_conv_w, v_ev_w_out, v_od_norm_w, v_od_w_in, v_od_q_norm_w, v_od_k_norm_w, v_od_sinks, v_od_w_out):
    my_chip = 2 * lax.axis_index("x") + lax.axis_index("y")
    place = jnp.stack([lax.axis_index("c"), my_chip]).astype(jnp.int32)
    shard_w = D_MODEL // N_CHIPS
    conv_shard = CONV_WIDTH // N_CHIPS

    small_in = jnp.zeros((8, shard_w), F32)
    small_in = small_in.at[0].set(od_norm_w[0]).at[1:4, :conv_shard].set(ev_conv_w[0])
    small_in = lax.dynamic_update_slice(jnp.zeros((N_CHIPS, 8, shard_w), F32), small_in[None], (my_chip, 0, 0))
    chip = place[1:]
    first_kinds, second_kinds, late_kinds = (("col", 0, 2), "whole"), (("col", 1, 2),), ("row", "col", "row")
    w_in0_own_place, w_in0_shard = _cast_into_gathered(ev_w_in[0], "col", chip, "cast_w_in0", keep_shard=True)
    first = _gather_start("gather_first", [w_in0_own_place, small_in], first_kinds)
    second = _gather_start("gather_second", [first[2][0]], second_kinds)
    late_own = [_cast_into_gathered(ev_w_out[0], "row", chip, "cast_w_out0"),
                _cast_into_gathered(od_w_in[0], "col", chip, "cast_w_in1"),
                _cast_into_gathered(od_w_out[0], "row", chip, "cast_w_out1")]
    late, w_in0_so_far = [], []

    def weights_first(part, after):
        if part == 0:
            passed_on = _gather_arrive("gather_first", first, first_kinds, after, arrays=[second[2][0], first[2][1]])
            w_in0, small_all = _gather_done("gather_first", passed_on, first_kinds, [passed_on[3]] + late_own)
            w_in0_so_far.append(w_in0)
            late.append(_gather_start("gather_late", late_own, late_kinds, after=small_all))
            od_norm_full = small_all[:, 0, :].reshape(1, D_MODEL)
            conv_full = jnp.transpose(small_all[:, 1:4, :conv_shard], (1, 0, 2)).reshape(3, CONV_WIDTH)
            return w_in0, (conv_full, od_norm_full), late[0][3]
        passed_on = _gather_arrive("gather_second", second, second_kinds, after, arrays=w_in0_so_far)
        (w_in0,) = _gather_done("gather_second", passed_on, second_kinds, passed_on[3])
        return w_in0, None, None

    def late_arrived(after):
        late.append(_gather_arrive("gather_late", late[0], late_kinds, after))
        return late[1][3]

    def weights_late(after):
        return _gather_done("gather_late", late[1], late_kinds, after)

    reduce_scatter = _ReduceScatter(place)
    sq, grad_x, small = _local_step(x[0], loss_target[0], ev_norm_w, od_q_norm_w, od_k_norm_w, od_sinks,
                                    (w_in0_shard, chip, second[3]), weights_first, late_arrived, weights_late,
                                    reduce_scatter)

    token = reduce_scatter.sum_up(grad_x)
    upd = {}
    upper = reduce_scatter.result("upper", token)
    upd["od_w_in"] = _adamw(od_w_in[0], upper["od_w_in"], m_od_w_in[0], v_od_w_in[0], "adamw_od_w_in")
    upd["od_w_out"], upd["ev_w_out"] = _adamw_same_shape(
        [(od_w_out[0], upper["od_w_out"], m_od_w_out[0], v_od_w_out[0]),
         (ev_w_out[0], upper["ev_w_out"], m_ev_w_out[0], v_ev_w_out[0])], "adamw_w_out")
    g_ev_w_in = reduce_scatter.result("in0", [upd[nm][1] for nm in ("od_w_in", "od_w_out", "ev_w_out")])["ev_w_in"]
    upd["ev_w_in"] = _adamw(ev_w_in[0], g_ev_w_in, m_ev_w_in[0], v_ev_w_in[0], "adamw_ev_w_in")
    tot = _allreduce_small(_pack_small_grads(small, sq))
    loss = 0.5 * tot[6, 0] / D_MODEL

    smalls = (("ev_norm_w", ev_norm_w, m_ev_norm_w, v_ev_norm_w), ("ev_conv_w", ev_conv_w, m_ev_conv_w, v_ev_conv_w),
              ("od_norm_w", od_norm_w, m_od_norm_w, v_od_norm_w), ("od_q_norm_w", od_q_norm_w, m_od_q_norm_w, v_od_q_norm_w),
              ("od_k_norm_w", od_k_norm_w, m_od_k_norm_w, v_od_k_norm_w), ("od_sinks", od_sinks, m_od_sinks, v_od_sinks))
    for (nm, _, _, _), result in zip(smalls, _adamw_small(tot, chip, [p[1:] for p in smalls])):
        upd[nm] = result
    for nm in ("ev_w_in", "ev_w_out", "od_w_in", "od_w_out"):
        upd[nm] = tuple(u[None] for u in upd[nm])
    order = ("ev_norm_w", "ev_w_in", "ev_conv_w", "ev_w_out", "od_norm_w", "od_w_in", "od_q_norm_w", "od_k_norm_w", "od_sinks", "od_w_out")
    return (loss, grad_x[None], *[upd[nm][0] for nm in order], *[upd[nm][1] for nm in order],
            *[upd[nm][2] for nm in order], *[upd[nm][3] for nm in order])
```
